```python
import math
import jax, jax.numpy as jnp
from jax import lax
import numpy as np

D_MODEL = 1024
BATCH = 8
SEQ = 4096
DEPTH = 1

HEAD_DIM = 64
D_MIX = D_MODEL
N_GMLP_HEADS = 8
D_GMLP = N_GMLP_HEADS * HEAD_DIM
N_Q_HEADS = 8
N_KV_HEADS = 2
GQA_GROUP = N_Q_HEADS // N_KV_HEADS
D_ATTN = N_Q_HEADS * HEAD_DIM
D_KV = N_KV_HEADS * HEAD_DIM
D_IN = 2 * D_GMLP + D_ATTN + 2 * D_KV
CHUNK = 128
WINDOW = 128
ATTN_BLOCK = 128
ROPE_THETA = 10000.0
D_FF = 4 * D_MODEL
LN_EPS = 1e-5
DEEPNORM_ALPHA = (2.0 * DEPTH) ** 0.25
DEEPNORM_BETA = (8.0 * DEPTH) ** -0.25
NEG_INF = -1e30

kernel_name = "hymba_gmlp_swa_sink_deepnorm"


def layer_norm(x, g, b):
    xf = x.astype(jnp.float32)
    mu = jnp.mean(xf, axis=-1, keepdims=True)
    var = jnp.mean(jnp.square(xf - mu), axis=-1, keepdims=True)
    y = (xf - mu) * lax.rsqrt(var + LN_EPS)
    return (y * g.astype(jnp.float32) + b.astype(jnp.float32)).astype(x.dtype)


def rope(t, positions):
    half = HEAD_DIM // 2
    inv_freq = ROPE_THETA ** (-jnp.arange(0, HEAD_DIM, 2, dtype=jnp.float32) / HEAD_DIM)
    ang = positions.astype(jnp.float32)[..., None] * inv_freq
    cos = jnp.cos(ang)[:, :, None, :]
    sin = jnp.sin(ang)[:, :, None, :]
    tf = t.astype(jnp.float32)
    t1, t2 = tf[..., :half], tf[..., half:]
    out = jnp.concatenate([t1 * cos - t2 * sin, t2 * cos + t1 * sin], axis=-1)
    return out.astype(t.dtype)


def gmlp_mixer(u, v, v_ln_g, v_ln_b, w_spatial, b_spatial):
    B, S, _ = u.shape
    nc = S // CHUNK
    u = jax.nn.gelu(u)
    v = layer_norm(jax.nn.gelu(v), v_ln_g, v_ln_b)
    vc = v.reshape(B, nc, CHUNK, N_GMLP_HEADS, HEAD_DIM)
    causal = jnp.tril(jnp.ones((CHUNK, CHUNK), dtype=w_spatial.dtype))
    w = w_spatial * causal
    mixed = jnp.einsum('hts,bcshd->bcthd', w, vc) + b_spatial.T[None, None, :, :, None]
    out = u.reshape(B, nc, CHUNK, N_GMLP_HEADS, HEAD_DIM) * mixed
    return out.reshape(B, S, D_GMLP)


def swa_sink_attention(q, k, v, positions, sinks):
    B, S, _ = q.shape
    nb = S // ATTN_BLOCK
    q = rope(q.reshape(B, S, N_Q_HEADS, HEAD_DIM), positions)
    k = rope(k.reshape(B, S, N_KV_HEADS, HEAD_DIM), positions)
    v = v.reshape(B, S, N_KV_HEADS, HEAD_DIM)
    qb = q.reshape(B, nb, ATTN_BLOCK, N_KV_HEADS, GQA_GROUP, HEAD_DIM)

    def banded(t):
        tb = t.reshape(B, nb, ATTN_BLOCK, N_KV_HEADS, HEAD_DIM)
        prev = jnp.pad(tb[:, :-1], ((0, 0), (1, 0), (0, 0), (0, 0), (0, 0)))
        return jnp.concatenate([prev, tb], axis=2)

    kb, vb = banded(k), banded(v)
    scores = jnp.einsum('bnqkgd,bnskd->bnkgqs', qb, kb).astype(jnp.float32)
    scores = scores * (1.0 / math.sqrt(HEAD_DIM))

    qi = jnp.arange(ATTN_BLOCK)[:, None]
    si = jnp.arange(2 * ATTN_BLOCK)[None, :]
    dist = qi + ATTN_BLOCK - si
    band = (dist >= 0) & (dist < WINDOW)
    key_abs = jnp.arange(nb)[:, None, None] * ATTN_BLOCK + si[None] - ATTN_BLOCK
    mask = band[None] & (key_abs >= 0)
    scores = jnp.where(mask[None, :, None, None], scores, NEG_INF)

    sink = sinks.astype(jnp.float32).reshape(N_KV_HEADS, GQA_GROUP)
    sink_col = jnp.broadcast_to(sink[None, None, :, :, None, None], scores.shape[:-1] + (1,))
    probs = jax.nn.softmax(jnp.concatenate([scores, sink_col], axis=-1), axis=-1)[..., :-1]
    out = jnp.einsum('bnkgqs,bnskd->bnqkgd', probs.astype(vb.dtype), vb)
    return out.reshape(B, S, D_ATTN)


def _fwd_setup_inputs(seed: int = 0) -> dict:
    key = jax.random.key(seed)
    ks = jax.random.split(key, 16)
    f32 = jnp.float32
    x = jax.random.normal(ks[0], (BATCH, SEQ, D_MODEL), f32)
    offset = jax.random.randint(ks[1], (BATCH, 1), 0, 1024, dtype=jnp.int32)
    positions = (offset + jnp.arange(SEQ, dtype=jnp.int32)[None, :]).astype(jnp.int32)
    w_in = jax.random.normal(ks[2], (DEPTH, D_MODEL, D_IN), f32) * D_MODEL ** -0.5
    v_ln_g = 1.0 + 0.05 * jax.random.normal(ks[3], (DEPTH, D_GMLP), f32)
    v_ln_b = 0.02 * jax.random.normal(ks[4], (DEPTH, D_GMLP), f32)
    w_spatial = jax.random.normal(ks[5], (DEPTH, N_GMLP_HEADS, CHUNK, CHUNK), f32) * CHUNK ** -0.5
    b_spatial = 1.0 + 0.1 * jax.random.normal(ks[6], (DEPTH, N_GMLP_HEADS, CHUNK), f32)
    sinks = 0.5 * jax.random.normal(ks[7], (DEPTH, N_Q_HEADS), f32)
    w_out = jax.random.normal(ks[8], (DEPTH, D_MIX, D_MODEL), f32) * (D_MIX ** -0.5) * DEEPNORM_BETA
    ln1_g = 1.0 + 0.05 * jax.random.normal(ks[9], (DEPTH, D_MODEL), f32)
    ln1_b = 0.02 * jax.random.normal(ks[10], (DEPTH, D_MODEL), f32)
    w_ff1 = jax.random.normal(ks[11], (DEPTH, D_MODEL, D_FF), f32) * D_MODEL ** -0.5
    w_ff2 = jax.random.normal(ks[12], (DEPTH, D_FF, D_MODEL), f32) * (D_FF ** -0.5) * DEEPNORM_BETA
    ln2_g = 1.0 + 0.05 * jax.random.normal(ks[13], (DEPTH, D_MODEL), f32)
    ln2_b = 0.02 * jax.random.normal(ks[14], (DEPTH, D_MODEL), f32)
    return {"x": x, "positions": positions, "w_in": w_in, "v_ln_g": v_ln_g, "v_ln_b": v_ln_b,
            "w_spatial": w_spatial, "b_spatial": b_spatial, "sinks": sinks, "w_out": w_out,
            "ln1_g": ln1_g, "ln1_b": ln1_b, "w_ff1": w_ff1, "w_ff2": w_ff2,
            "ln2_g": ln2_g, "ln2_b": ln2_b}


def _fwd_reference(x, positions, w_in, v_ln_g, v_ln_b, w_spatial, b_spatial, sinks, w_out,
              ln1_g, ln1_b, w_ff1, w_ff2, ln2_g, ln2_b):
    split_at = [D_GMLP, 2 * D_GMLP, 2 * D_GMLP + D_ATTN, 2 * D_GMLP + D_ATTN + D_KV]
    for l in range(DEPTH):
        h = x @ w_in[l]
        u, v_g, q, k, v_a = jnp.split(h, split_at, axis=-1)
        a_out = gmlp_mixer(u, v_g, v_ln_g[l], v_ln_b[l], w_spatial[l], b_spatial[l])
        b_out = swa_sink_attention(q, k, v_a, positions, sinks[l])
        mix = jnp.concatenate([a_out, b_out], axis=-1) @ w_out[l]
        x = layer_norm(DEEPNORM_ALPHA * x + mix, ln1_g[l], ln1_b[l])
        ff = jnp.square(jax.nn.relu(x @ w_ff1[l])) @ w_ff2[l]
        x = layer_norm(DEEPNORM_ALPHA * x + ff, ln2_g[l], ln2_b[l])
    return x


import jax as _jax
import jax.numpy as _jnp

TWIN_FORMAT = 'train_step'
FWD_PARAMS = ['x', 'positions', 'w_in', 'v_ln_g', 'v_ln_b', 'w_spatial', 'b_spatial', 'sinks', 'w_out', 'ln1_g', 'ln1_b', 'w_ff1', 'w_ff2', 'ln2_g', 'ln2_b']
TWIN_WEIGHTS = ['w_in', 'v_ln_g', 'v_ln_b', 'w_spatial', 'b_spatial', 'sinks', 'w_out', 'ln1_g', 'ln1_b', 'w_ff1', 'w_ff2', 'ln2_g', 'ln2_b']
TWIN_DIFF_INPUT = 'x'
TWIN_INPUTS = ['x', 'positions', 'w_in', 'v_ln_g', 'v_ln_b', 'w_spatial', 'b_spatial', 'sinks', 'w_out', 'ln1_g', 'ln1_b', 'w_ff1', 'w_ff2', 'ln2_g', 'ln2_b', 'loss_target', 'm_w_in', 'm_v_ln_g', 'm_v_ln_b', 'm_w_spatial', 'm_b_spatial', 'm_sinks', 'm_w_out', 'm_ln1_g', 'm_ln1_b', 'm_w_ff1', 'm_w_ff2', 'm_ln2_g', 'm_ln2_b', 'v_w_in', 'v_v_ln_g', 'v_v_ln_b', 'v_w_spatial', 'v_b_spatial', 'v_sinks', 'v_w_out', 'v_ln1_g', 'v_ln1_b', 'v_w_ff1', 'v_w_ff2', 'v_ln2_g', 'v_ln2_b']
TWIN_OUTPUTS = ['loss', 'grad_x', 'grad_w_in', 'grad_v_ln_g', 'grad_v_ln_b', 'grad_w_spatial', 'grad_b_spatial', 'grad_sinks', 'grad_w_out', 'grad_ln1_g', 'grad_ln1_b', 'grad_w_ff1', 'grad_w_ff2', 'grad_ln2_g', 'grad_ln2_b', 'delta_w_in', 'delta_v_ln_g', 'delta_v_ln_b', 'delta_w_spatial', 'delta_b_spatial', 'delta_sinks', 'delta_w_out', 'delta_ln1_g', 'delta_ln1_b', 'delta_w_ff1', 'delta_w_ff2', 'delta_ln2_g', 'delta_ln2_b', 'new_m_w_in', 'new_m_v_ln_g', 'new_m_v_ln_b', 'new_m_w_spatial', 'new_m_b_spatial', 'new_m_sinks', 'new_m_w_out', 'new_m_ln1_g', 'new_m_ln1_b', 'new_m_w_ff1', 'new_m_w_ff2', 'new_m_ln2_g', 'new_m_ln2_b', 'new_v_w_in', 'new_v_v_ln_g', 'new_v_v_ln_b', 'new_v_w_spatial', 'new_v_b_spatial', 'new_v_sinks', 'new_v_w_out', 'new_v_ln1_g', 'new_v_ln1_b', 'new_v_w_ff1', 'new_v_w_ff2', 'new_v_ln2_g', 'new_v_ln2_b']
TWIN_LEAF_KINDS = {'loss': 'loss', 'grad_x': 'grad_x', 'grad_w_in': 'grad_w', 'grad_v_ln_g': 'grad_w', 'grad_v_ln_b': 'grad_w', 'grad_w_spatial': 'grad_w', 'grad_b_spatial': 'grad_w', 'grad_sinks': 'grad_w', 'grad_w_out': 'grad_w', 'grad_ln1_g': 'grad_w', 'grad_ln1_b': 'grad_w', 'grad_w_ff1': 'grad_w', 'grad_w_ff2': 'grad_w', 'grad_ln2_g': 'grad_w', 'grad_ln2_b': 'grad_w', 'delta_w_in': 'delta_w', 'delta_v_ln_g': 'delta_w', 'delta_v_ln_b': 'delta_w', 'delta_w_spatial': 'delta_w', 'delta_b_spatial': 'delta_w', 'delta_sinks': 'delta_w', 'delta_w_out': 'delta_w', 'delta_ln1_g': 'delta_w', 'delta_ln1_b': 'delta_w', 'delta_w_ff1': 'delta_w', 'delta_w_ff2': 'delta_w', 'delta_ln2_g': 'delta_w', 'delta_ln2_b': 'delta_w', 'new_m_w_in': 'new_m', 'new_m_v_ln_g': 'new_m', 'new_m_v_ln_b': 'new_m', 'new_m_w_spatial': 'new_m', 'new_m_b_spatial': 'new_m', 'new_m_sinks': 'new_m', 'new_m_w_out': 'new_m', 'new_m_ln1_g': 'new_m', 'new_m_ln1_b': 'new_m', 'new_m_w_ff1': 'new_m', 'new_m_w_ff2': 'new_m', 'new_m_ln2_g': 'new_m', 'new_m_ln2_b': 'new_m', 'new_v_w_in': 'new_v', 'new_v_v_ln_g': 'new_v', 'new_v_v_ln_b': 'new_v', 'new_v_w_spatial': 'new_v', 'new_v_b_spatial': 'new_v', 'new_v_sinks': 'new_v', 'new_v_w_out': 'new_v', 'new_v_ln1_g': 'new_v', 'new_v_ln1_b': 'new_v', 'new_v_w_ff1': 'new_v', 'new_v_w_ff2': 'new_v', 'new_v_ln2_g': 'new_v', 'new_v_ln2_b': 'new_v'}


def _forward(args):
    return _fwd_reference(*[args[k] for k in FWD_PARAMS])


def _output_shape():
    def fwd():
        inp = _fwd_setup_inputs(0)
        return _fwd_reference(*[inp[k] for k in FWD_PARAMS])
    out = _jax.eval_shape(fwd)
    return out.shape, out.dtype

N_MICROBATCH = 1
ADAM_LR = 0.001
ADAM_B1 = 0.9
ADAM_B2 = 0.999
ADAM_EPS = 1e-08
ADAM_WD = 0.01
ADAM_STEP = 10
PER_EXAMPLE_BATCH_AXIS = {'x': 0, 'positions': 0, 'loss_target': 0}
SHARED_INPUTS = []
_WEIGHT_DTYPES = {'w_in': _jnp.float32, 'v_ln_g': _jnp.float32, 'v_ln_b': _jnp.float32, 'w_spatial': _jnp.float32, 'b_spatial': _jnp.float32, 'sinks': _jnp.float32, 'w_out': _jnp.float32, 'ln1_g': _jnp.float32, 'ln1_b': _jnp.float32, 'w_ff1': _jnp.float32, 'w_ff2': _jnp.float32, 'ln2_g': _jnp.float32, 'ln2_b': _jnp.float32}
MOMENT_SCALE = {'w_in': 4.959542e-02, 'v_ln_g': 4.391341e-02, 'v_ln_b': 4.444058e-02, 'w_spatial': 2.920087e-02, 'b_spatial': 4.151230e-02, 'sinks': 9.611560e-03, 'w_out': 1.384784e-01, 'ln1_g': 2.304865e+00, 'ln1_b': 8.398047e-01, 'w_ff1': 5.615435e-02, 'w_ff2': 3.255135e-01, 'ln2_g': 3.220406e+01, 'ln2_b': 7.507431e+00}


def _to_microbatches(a, axis):
    t = _jnp.moveaxis(a, axis, 0)
    t = t.reshape((N_MICROBATCH, t.shape[0] // N_MICROBATCH) + t.shape[1:])
    return _jnp.moveaxis(t, 1, axis + 1)


def setup_inputs(seed: int = 0) -> dict:
    inp = _fwd_setup_inputs(seed)
    key = _jax.random.fold_in(_jax.random.key(seed), 7919)
    shape, _ = _output_shape()
    out = dict(inp)
    out["loss_target"] = _jax.random.normal(_jax.random.fold_in(key, 0), shape, _jnp.float32)
    for i, name in enumerate(TWIN_WEIGHTS):
        w = inp[name].astype(_jnp.float32)
        if MOMENT_SCALE is None:
            s = _jnp.sqrt(_jnp.mean(_jnp.square(w)) + 1e-30)
        else:
            s = MOMENT_SCALE[name]
        km, kv = _jax.random.split(_jax.random.fold_in(key, i + 1))
        out[name] = w
        out["m_" + name] = s * _jax.random.normal(km, w.shape, _jnp.float32)
        out["v_" + name] = (s * s) * _jax.random.uniform(kv, w.shape, _jnp.float32, 0.5, 1.5)
    if N_MICROBATCH > 1:
        for name, axis in PER_EXAMPLE_BATCH_AXIS.items():
            out[name] = _to_microbatches(out[name], axis)
    return {'x': out['x'], 'positions': out['positions'], 'w_in': out['w_in'], 'v_ln_g': out['v_ln_g'], 'v_ln_b': out['v_ln_b'], 'w_spatial': out['w_spatial'], 'b_spatial': out['b_spatial'], 'sinks': out['sinks'], 'w_out': out['w_out'], 'ln1_g': out['ln1_g'], 'ln1_b': out['ln1_b'], 'w_ff1': out['w_ff1'], 'w_ff2': out['w_ff2'], 'ln2_g': out['ln2_g'], 'ln2_b': out['ln2_b'], 'loss_target': out['loss_target'], 'm_w_in': out['m_w_in'], 'm_v_ln_g': out['m_v_ln_g'], 'm_v_ln_b': out['m_v_ln_b'], 'm_w_spatial': out['m_w_spatial'], 'm_b_spatial': out['m_b_spatial'], 'm_sinks': out['m_sinks'], 'm_w_out': out['m_w_out'], 'm_ln1_g': out['m_ln1_g'], 'm_ln1_b': out['m_ln1_b'], 'm_w_ff1': out['m_w_ff1'], 'm_w_ff2': out['m_w_ff2'], 'm_ln2_g': out['m_ln2_g'], 'm_ln2_b': out['m_ln2_b'], 'v_w_in': out['v_w_in'], 'v_v_ln_g': out['v_v_ln_g'], 'v_v_ln_b': out['v_v_ln_b'], 'v_w_spatial': out['v_w_spatial'], 'v_b_spatial': out['v_b_spatial'], 'v_sinks': out['v_sinks'], 'v_w_out': out['v_w_out'], 'v_ln1_g': out['v_ln1_g'], 'v_ln1_b': out['v_ln1_b'], 'v_w_ff1': out['v_w_ff1'], 'v_w_ff2': out['v_w_ff2'], 'v_ln2_g': out['v_ln2_g'], 'v_ln2_b': out['v_ln2_b']}


def _loss(weights, diff, rest, loss_target):
    with _jax.named_scope("forward"):
        args = {**rest, TWIN_DIFF_INPUT: diff, **{k: w.astype(_WEIGHT_DTYPES[k]) for k, w in weights.items()}}
        y = _forward(args)
    with _jax.named_scope("loss_head"):
        err = _jnp.square(y.astype(_jnp.float32) - loss_target)
        return 0.5 * _jnp.sum(_jnp.mean(err, axis=-1)) if err.ndim else 0.5 * err


def _adamw(w, g, m, v):
    m = ADAM_B1 * m + (1.0 - ADAM_B1) * g
    v = ADAM_B2 * v + (1.0 - ADAM_B2) * _jnp.square(g)
    m_hat = m / (1.0 - ADAM_B1 ** ADAM_STEP)
    v_hat = v / (1.0 - ADAM_B2 ** ADAM_STEP)
    delta = -ADAM_LR * (m_hat / (_jnp.sqrt(v_hat) + ADAM_EPS) + ADAM_WD * w)
    return delta, m, v


def reference(x, positions, w_in, v_ln_g, v_ln_b, w_spatial, b_spatial, sinks, w_out, ln1_g, ln1_b, w_ff1, w_ff2, ln2_g, ln2_b, loss_target, m_w_in, m_v_ln_g, m_v_ln_b, m_w_spatial, m_b_spatial, m_sinks, m_w_out, m_ln1_g, m_ln1_b, m_w_ff1, m_w_ff2, m_ln2_g, m_ln2_b, v_w_in, v_v_ln_g, v_v_ln_b, v_w_spatial, v_b_spatial, v_sinks, v_w_out, v_ln1_g, v_ln1_b, v_w_ff1, v_w_ff2, v_ln2_g, v_ln2_b):
    given = dict(x=x, positions=positions, w_in=w_in, v_ln_g=v_ln_g, v_ln_b=v_ln_b, w_spatial=w_spatial, b_spatial=b_spatial, sinks=sinks, w_out=w_out, ln1_g=ln1_g, ln1_b=ln1_b, w_ff1=w_ff1, w_ff2=w_ff2, ln2_g=ln2_g, ln2_b=ln2_b, loss_target=loss_target, m_w_in=m_w_in, m_v_ln_g=m_v_ln_g, m_v_ln_b=m_v_ln_b, m_w_spatial=m_w_spatial, m_b_spatial=m_b_spatial, m_sinks=m_sinks, m_w_out=m_w_out, m_ln1_g=m_ln1_g, m_ln1_b=m_ln1_b, m_w_ff1=m_w_ff1, m_w_ff2=m_w_ff2, m_ln2_g=m_ln2_g, m_ln2_b=m_ln2_b, v_w_in=v_w_in, v_v_ln_g=v_v_ln_g, v_v_ln_b=v_v_ln_b, v_w_spatial=v_w_spatial, v_b_spatial=v_b_spatial, v_sinks=v_sinks, v_w_out=v_w_out, v_ln1_g=v_ln1_g, v_ln1_b=v_ln1_b, v_w_ff1=v_w_ff1, v_w_ff2=v_w_ff2, v_ln2_g=v_ln2_g, v_ln2_b=v_ln2_b)
    weights = {n: given[n] for n in TWIN_WEIGHTS}
    shared = {n: given[n] for n in SHARED_INPUTS}
    per_example = {n: given[n] for n in ['x', 'positions']}
    grad_fn = _jax.value_and_grad(_loss, argnums=(0, 1))

    def one_microbatch(ex, loss_target):
        ex = dict(ex)
        diff = ex.pop(TWIN_DIFF_INPUT)
        return grad_fn(weights, diff, {**shared, **ex}, loss_target)

    if N_MICROBATCH == 1:
        loss, (grad_w, grad_x) = one_microbatch(per_example, given["loss_target"])
    else:
        def body(carry, xs):
            loss_sum, grad_sum = carry
            l_k, (gw_k, gx_k) = one_microbatch(xs[0], xs[1])
            with _jax.named_scope("update"):
                return (loss_sum + l_k, _jax.tree.map(_jnp.add, grad_sum, gw_k)), gx_k

        init = (_jnp.zeros((), _jnp.float32), _jax.tree.map(_jnp.zeros_like, weights))
        (loss, grad_w), grad_x = _jax.lax.scan(body, init, (per_example, given["loss_target"]))
    with _jax.named_scope("update"):
        delta_w, new_m, new_v = {}, {}, {}
        for n in TWIN_WEIGHTS:
            delta_w[n], new_m[n], new_v[n] = _adamw(weights[n], grad_w[n], given["m_" + n], given["v_" + n])
    return (loss, grad_x, *[grad_w[n] for n in TWIN_WEIGHTS], *[delta_w[n] for n in TWIN_WEIGHTS],
            *[new_m[n] for n in TWIN_WEIGHTS], *[new_v[n] for n in TWIN_WEIGHTS])
```

```python
import functools
import math

import jax
import jax.numpy as jnp
from jax import lax
from jax.experimental import pallas as pl
from jax.experimental.pallas import tpu as pltpu

F32 = jnp.float32
BF16 = jnp.bfloat16

D_MODEL = 1024
HEAD_DIM = 64
D_GMLP = 512
D_ATTN = 512
D_KV = 128
D_IN = 2 * D_GMLP + D_ATTN + 2 * D_KV
D_MAIN = 2 * D_GMLP + D_ATTN
N_HEADS = 8
CHUNK = 128
ROPE_THETA = 10000.0
D_FF = 4 * D_MODEL
N_FF_BLOCKS = 4
LN_EPS = 1e-5
ALPHA = (2.0 * 1) ** 0.25
NEG_INF = -1e30
SCALE = 1.0 / math.sqrt(HEAD_DIM)

ADAM_LR = 0.001
ADAM_B1 = 0.9
ADAM_B2 = 0.999
ADAM_EPS = 1e-08
ADAM_WD = 0.01
ADAM_STEP = 10

N_CHIPS = 4
LANES = 128
V7X_VMEM_BYTES = 64 * 1024 * 1024
VMEM_LIMIT = V7X_VMEM_BYTES - 8 * 1024 * 1024
TM = 512
TM_FFN = 256
TK = 512
SMALL_ROWS = 1088
MESH = pl.DeviceIdType.MESH

NT = (((1,), (1,)), ((), ()))
TN = (((0,), (0,)), ((), ()))


def _dot(a, b, dims=None):
    if dims is None:
        return jnp.dot(a, b, preferred_element_type=F32)
    return lax.dot_general(a, b, dims, preferred_element_type=F32)


def _params(semantics=None):
    return pltpu.CompilerParams(dimension_semantics=semantics, vmem_limit_bytes=VMEM_LIMIT)


def _const_spec(shape, single_buffer=False):
    zeros = (0,) * len(shape)
    if single_buffer:
        return pl.BlockSpec(shape, lambda *_: zeros, pipeline_mode=pl.Buffered(1))
    return pl.BlockSpec(shape, lambda *_: zeros)


def _row_spec(rows, cols):
    return pl.BlockSpec((rows, cols), lambda i: (i, 0))


def _gelu(x):
    k = math.sqrt(2.0 / math.pi)
    return 0.5 * x * (1.0 + jnp.tanh(k * (x + 0.044715 * (x * x * x))))


def _gelu_and_grad(x):
    k = math.sqrt(2.0 / math.pi)
    x2 = x * x
    t = jnp.tanh(k * (x + 0.044715 * (x2 * x)))
    g = 0.5 * x * (1.0 + t)
    dg = 0.5 * (1.0 + t) + 0.5 * x * (1.0 - t * t) * (k * (1.0 + 3.0 * 0.044715 * x2))
    return g, dg


def _layer_norm_stats(z):
    mu = jnp.mean(z, axis=-1, keepdims=True)
    zc = z - mu
    var = jnp.mean(zc * zc, axis=-1, keepdims=True)
    rstd = lax.rsqrt(var + LN_EPS)
    return zc * rstd, rstd


def _layer_norm_bwd(dxhat, xhat, rstd):
    m1 = jnp.mean(dxhat, axis=-1, keepdims=True)
    m2 = jnp.mean(dxhat * xhat, axis=-1, keepdims=True)
    return rstd * (dxhat - m1 - xhat * m2)


def _rotate_half(t):
    n = t.shape[1]
    lane = lax.broadcasted_iota(jnp.int32, t.shape, 1)
    first = (lane & (HEAD_DIM // 2)) == 0
    return jnp.where(first, -pltpu.roll(t, n - HEAD_DIM // 2, 1), pltpu.roll(t, HEAD_DIM // 2, 1))


def _rope(t, cos, sin):
    return t * cos + _rotate_half(t) * sin


def _rope_transposed(g, cos, sin):
    return g * cos - _rotate_half(g * sin)


def _lane_tile(a, reps):
    return jnp.tile(a, (1, reps)) if reps > 1 else a


def _rope_tables(pos_col, inv_freq_row):
    t = pos_col.shape[0]

    def body(pos_ref, f_ref, cos_ref, sin_ref):
        ang = pos_ref[...].astype(F32) * f_ref[...]
        cos_ref[...] = jnp.cos(ang)
        sin_ref[...] = jnp.sin(ang)

    return pl.pallas_call(
        body,
        name="rope_tables",
        grid=(t // TM,),
        in_specs=[_row_spec(TM, 1), _const_spec((1, LANES))],
        out_specs=[_row_spec(TM, LANES), _row_spec(TM, LANES)],
        out_shape=[jax.ShapeDtypeStruct((t, LANES), F32)] * 2,
        compiler_params=_params(("parallel",)),
    )(pos_col, inv_freq_row)


def _in_proj(x, w_in_t, cos, sin):
    t = x.shape[0]

    def body(x_ref, w_ref, cos_ref, sin_ref, u_ref, vg_ref, q_ref, k_ref, va_ref):
        xb = x_ref[...].astype(BF16)
        u_ref[...] = _dot(xb, w_ref[0:D_GMLP, :], NT)
        vg_ref[...] = _dot(xb, w_ref[D_GMLP : 2 * D_GMLP, :], NT)
        q = _dot(xb, w_ref[2 * D_GMLP : D_MAIN, :], NT)
        k = _dot(xb, w_ref[D_MAIN : D_MAIN + D_KV, :], NT)
        va_ref[...] = _dot(xb, w_ref[D_MAIN + D_KV : D_IN, :], NT).astype(BF16)
        c, s = cos_ref[...], sin_ref[...]
        q_ref[...] = _rope(q, _lane_tile(c, D_ATTN // LANES), _lane_tile(s, D_ATTN // LANES)).astype(BF16)
        k_ref[...] = _rope(k, c, s).astype(BF16)

    return pl.pallas_call(
        body,
        name="in_proj",
        grid=(t // TM,),
        in_specs=[_row_spec(TM, D_MODEL), _const_spec((D_IN, D_MODEL)), _row_spec(TM, LANES), _row_spec(TM, LANES)],
        out_specs=[_row_spec(TM, D_GMLP), _row_spec(TM, D_GMLP), _row_spec(TM, D_ATTN), _row_spec(TM, D_KV), _row_spec(TM, D_KV)],
        out_shape=[
            jax.ShapeDtypeStruct((t, D_GMLP), F32),
            jax.ShapeDtypeStruct((t, D_GMLP), F32),
            jax.ShapeDtypeStruct((t, D_ATTN), BF16),
            jax.ShapeDtypeStruct((t, D_KV), BF16),
            jax.ShapeDtypeStruct((t, D_KV), BF16),
        ],
        compiler_params=_params(("parallel",)),
    )(x, w_in_t, cos, sin)


def _chunk_specs():
    cur = lambda i: (i, 0)
    prev = lambda i: (jnp.maximum(i - 1, 0), 0)
    return [
        pl.BlockSpec((CHUNK, D_GMLP), cur),
        pl.BlockSpec((CHUNK, D_GMLP), cur),
        pl.BlockSpec((CHUNK, D_ATTN), cur),
        pl.BlockSpec((CHUNK, D_KV), cur),
        pl.BlockSpec((CHUNK, D_KV), prev),
        pl.BlockSpec((CHUNK, D_KV), cur),
        pl.BlockSpec((CHUNK, D_KV), prev),
    ]


def _half_lane_masks(rows):
    lane = lax.broadcasted_iota(jnp.int32, (rows, LANES), 1)
    return lane < HEAD_DIM


def _kv_variants(kv2):
    left = _half_lane_masks(kv2.shape[0])
    f = kv2.astype(F32)
    swapped = pltpu.roll(f, HEAD_DIM, 1)
    zero = jnp.zeros_like(f)
    g0 = (jnp.where(left, f, zero).astype(BF16), jnp.where(left, zero, swapped).astype(BF16))
    g1 = (jnp.where(left, swapped, zero).astype(BF16), jnp.where(left, zero, f).astype(BF16))
    return (g0, g1)


def _band_mask(i):
    row = lax.broadcasted_iota(jnp.int32, (CHUNK, 2 * CHUNK), 0)
    col = lax.broadcasted_iota(jnp.int32, (CHUNK, 2 * CHUNK), 1)
    no_prev = jnp.where(i > 0, 0, 4 * CHUNK)
    in_prev = jnp.logical_and(col < CHUNK, (col - row) > no_prev)
    in_cur = jnp.logical_and(col >= CHUNK, (col - CHUNK) <= row)
    return jnp.logical_or(in_prev, in_cur)


def _softmax_with_sink(s, sink):
    m = jnp.maximum(jnp.max(s, axis=1, keepdims=True), sink)
    e = jnp.exp(s - m)
    e_sink = jnp.exp(sink - m)
    inv = 1.0 / (jnp.sum(e, axis=1, keepdims=True) + e_sink)
    return e * inv, e_sink * inv


def _spatial_weights(w_ref, pair):
    row = lax.broadcasted_iota(jnp.int32, (CHUNK, CHUNK), 0)
    col = lax.broadcasted_iota(jnp.int32, (CHUNK, CHUNK), 1)
    causal = col <= row
    wl = jnp.where(causal, w_ref[2 * pair], 0.0).astype(BF16)
    wr = jnp.where(causal, w_ref[2 * pair + 1], 0.0).astype(BF16)
    return wl, wr, causal


def _mixer_fwd(u, vg, q, k, va, v_ln_g, v_ln_b, w_spatial, bias_full, sinks):
    t = u.shape[0]

    def body(u_ref, vg_ref, q_ref, kc_ref, kp_ref, vc_ref, vp_ref, g_ref, b_ref, w_ref, bias_ref, sink_ref, cat_ref):
        i = pl.program_id(0)
        left = _half_lane_masks(CHUNK)
        ug = _gelu(u_ref[...])
        xhat, _ = _layer_norm_stats(_gelu(vg_ref[...]))
        vgl = xhat * g_ref[...] + b_ref[...]
        for p in range(D_GMLP // LANES):
            cols = slice(p * LANES, (p + 1) * LANES)
            xp = vgl[:, cols]
            wl, wr, _ = _spatial_weights(w_ref, p)
            mixed = _dot(wl, jnp.where(left, xp, 0.0).astype(BF16)) + _dot(wr, jnp.where(left, 0.0, xp).astype(BF16))
            cat_ref[:, cols] = (ug[:, cols] * (mixed + bias_ref[:, cols])).astype(BF16)

        k_var = _kv_variants(jnp.concatenate([kp_ref[...], kc_ref[...]], axis=0))
        v_var = _kv_variants(jnp.concatenate([vp_ref[...], vc_ref[...]], axis=0))
        valid = _band_mask(i)
        for p in range(D_ATTN // LANES):
            qp = q_ref[:, p * LANES : (p + 1) * LANES]
            group = p // 2
            out = jnp.zeros((CHUNK, LANES), F32)
            for side in range(2):
                s = jnp.where(valid, _dot(qp, k_var[group][side], NT) * SCALE, NEG_INF)
                probs, _ = _softmax_with_sink(s, sink_ref[2 * p + side])
                out = out + _dot(probs.astype(BF16), v_var[group][side])
            cat_ref[:, D_GMLP + p * LANES : D_GMLP + (p + 1) * LANES] = out.astype(BF16)

    return pl.pallas_call(
        body,
        name="mixer_fwd",
        grid=(t // CHUNK,),
        in_specs=_chunk_specs()
        + [
            _const_spec((1, D_GMLP)),
            _const_spec((1, D_GMLP)),
            _const_spec((N_HEADS, CHUNK, CHUNK)),
            _const_spec((CHUNK, D_GMLP)),
            pl.BlockSpec(memory_space=pltpu.SMEM),
        ],
        out_specs=pl.BlockSpec((CHUNK, D_MODEL), lambda i: (i, 0)),
        out_shape=jax.ShapeDtypeStruct((t, D_MODEL), BF16),
        compiler_params=_params(("parallel",)),
    )(u, vg, q, k, k, va, va, v_ln_g, v_ln_b, w_spatial, bias_full, sinks)


def _out_proj_ln1(cat, x, w_out):
    t = x.shape[0]

    def body(cat_ref, x_ref, w_ref, xhat_ref, rstd_ref):
        z = ALPHA * x_ref[...] + _dot(cat_ref[...], w_ref[...])
        xhat, rstd = _layer_norm_stats(z)
        xhat_ref[...] = xhat
        rstd_ref[...] = rstd

    return pl.pallas_call(
        body,
        name="out_proj_ln1",
        grid=(t // TM,),
        in_specs=[_row_spec(TM, D_MODEL), _row_spec(TM, D_MODEL), _const_spec((D_MODEL, D_MODEL))],
        out_specs=[_row_spec(TM, D_MODEL), _row_spec(TM, 1)],
        out_shape=[jax.ShapeDtypeStruct((t, D_MODEL), F32), jax.ShapeDtypeStruct((t, 1), F32)],
        compiler_params=_params(("parallel",)),
    )(cat, x, w_out)


def _ffn_fwd_loss(xhat1, ln1_g, ln1_b, w1, w2, ln2_g, ln2_b, target):
    t = xhat1.shape[0]

    def body(xh_ref, g1_ref, b1_ref, w1_ref, w2_ref, g2_ref, b2_ref, tgt_ref, r_ref, dz2_ref, dg2_ref, db2_ref, sq_ref):
        @pl.when(pl.program_id(0) == 0)
        def _():
            dg2_ref[...] = jnp.zeros_like(dg2_ref)
            db2_ref[...] = jnp.zeros_like(db2_ref)
            sq_ref[...] = jnp.zeros_like(sq_ref)

        x1 = xh_ref[...] * g1_ref[...] + b1_ref[...]
        x1b = x1.astype(BF16)
        ff = jnp.zeros((TM_FFN, D_MODEL), F32)
        for j in range(N_FF_BLOCKS):
            r = jnp.maximum(_dot(x1b, w1_ref[j]), 0.0)
            r_ref[:, j * D_MODEL : (j + 1) * D_MODEL] = r.astype(BF16)
            ff = ff + _dot((r * r).astype(BF16), w2_ref[j])
        xhat2, rstd2 = _layer_norm_stats(ALPHA * x1 + ff)
        err = xhat2 * g2_ref[...] + b2_ref[...] - tgt_ref[...]
        sq_ref[...] += jnp.sum(err * err, axis=0, keepdims=True)
        dy = err * (1.0 / D_MODEL)
        dg2_ref[...] += jnp.sum(dy * xhat2, axis=0, keepdims=True)
        db2_ref[...] += jnp.sum(dy, axis=0, keepdims=True)
        dz2_ref[...] = _layer_norm_bwd(dy * g2_ref[...], xhat2, rstd2)

    vec = _const_spec((1, D_MODEL))
    wspec = _const_spec((N_FF_BLOCKS, D_MODEL, D_MODEL), single_buffer=True)
    return pl.pallas_call(
        body,
        name="ffn_fwd_loss",
        grid=(t // TM_FFN,),
        in_specs=[_row_spec(TM_FFN, D_MODEL), vec, vec, wspec, wspec, vec, vec, _row_spec(TM_FFN, D_MODEL)],
        out_specs=[_row_spec(TM_FFN, D_FF), _row_spec(TM_FFN, D_MODEL), vec, vec, vec],
        out_shape=[
            jax.ShapeDtypeStruct((t, D_FF), BF16),
            jax.ShapeDtypeStruct((t, D_MODEL), F32),
            jax.ShapeDtypeStruct((1, D_MODEL), F32),
            jax.ShapeDtypeStruct((1, D_MODEL), F32),
            jax.ShapeDtypeStruct((1, D_MODEL), F32),
        ],
        compiler_params=_params(("arbitrary",)),
    )(xhat1, ln1_g, ln1_b, w1, w2, ln2_g, ln2_b, target)


def _ffn_bwd_ln1(dz2, r, xhat1, rstd1, ln1_g, w1, w2):
    t = dz2.shape[0]

    def body(dz2_ref, r_ref, xh_ref, rstd_ref, g1_ref, w1_ref, w2_ref, dpre_ref, dz1_ref, dg1_ref, db1_ref):
        @pl.when(pl.program_id(0) == 0)
        def _():
            dg1_ref[...] = jnp.zeros_like(dg1_ref)
            db1_ref[...] = jnp.zeros_like(db1_ref)

        dz2 = dz2_ref[...]
        dz2b = dz2.astype(BF16)
        dx1 = ALPHA * dz2
        for j in range(N_FF_BLOCKS):
            cols = slice(j * D_MODEL, (j + 1) * D_MODEL)
            dpre = (_dot(dz2b, w2_ref[j], NT) * (2.0 * r_ref[:, cols].astype(F32))).astype(BF16)
            dpre_ref[:, cols] = dpre
            dx1 = dx1 + _dot(dpre, w1_ref[j], NT)
        xhat1 = xh_ref[...]
        dg1_ref[...] += jnp.sum(dx1 * xhat1, axis=0, keepdims=True)
        db1_ref[...] += jnp.sum(dx1, axis=0, keepdims=True)
        dz1_ref[...] = _layer_norm_bwd(dx1 * g1_ref[...], xhat1, rstd_ref[...])

    vec = _const_spec((1, D_MODEL))
    wspec = _const_spec((N_FF_BLOCKS, D_MODEL, D_MODEL), single_buffer=True)
    return pl.pallas_call(
        body,
        name="ffn_bwd_ln1",
        grid=(t // TM_FFN,),
        in_specs=[_row_spec(TM_FFN, D_MODEL), _row_spec(TM_FFN, D_FF), _row_spec(TM_FFN, D_MODEL), _row_spec(TM_FFN, 1), vec, wspec, wspec],
        out_specs=[_row_spec(TM_FFN, D_FF), _row_spec(TM_FFN, D_MODEL), vec, vec],
        out_shape=[
            jax.ShapeDtypeStruct((t, D_FF), BF16),
            jax.ShapeDtypeStruct((t, D_MODEL), F32),
            jax.ShapeDtypeStruct((1, D_MODEL), F32),
            jax.ShapeDtypeStruct((1, D_MODEL), F32),
        ],
        compiler_params=_params(("arbitrary",)),
    )(dz2, r, xhat1, rstd1, ln1_g, w1, w2)


def _dcat(dz1, w_out):
    t = dz1.shape[0]

    def body(dz1_ref, w_ref, dcat_ref):
        dcat_ref[...] = _dot(dz1_ref[...].astype(BF16), w_ref[...], NT)

    return pl.pallas_call(
        body,
        name="dcat",
        grid=(t // TM,),
        in_specs=[_row_spec(TM, D_MODEL), _const_spec((D_MODEL, D_MODEL))],
        out_specs=_row_spec(TM, D_MODEL),
        out_shape=jax.ShapeDtypeStruct((t, D_MODEL), F32),
        compiler_params=_params(("parallel",)),
    )(dz1, w_out)


def _mixer_bwd(u, vg, q, k, va, dcat, cos, sin, v_ln_g, v_ln_b, w_spatial, bias_full, sinks):
    t = u.shape[0]
    n_chunks = t // CHUNK

    def body(u_ref, vg_ref, q_ref, kc_ref, kp_ref, vc_ref, vp_ref, dcat_ref, cosc_ref, sinc_ref, cosp_ref, sinp_ref,
             g_ref, b_ref, w_ref, bias_ref, sink_ref,
             dmain_ref, dkv_ref, dg_ref, db_ref, dw_ref, dbs_ref, dsink_ref, dmix_acc):
        i = pl.program_id(0)
        left = _half_lane_masks(CHUNK)
        lane = lax.broadcasted_iota(jnp.int32, (CHUNK, LANES), 1)

        @pl.when(i == 0)
        def _():
            dg_ref[...] = jnp.zeros_like(dg_ref)
            db_ref[...] = jnp.zeros_like(db_ref)
            dw_ref[...] = jnp.zeros_like(dw_ref)
            dsink_ref[...] = jnp.zeros_like(dsink_ref)
            dmix_acc[...] = jnp.zeros_like(dmix_acc)

        ug, dug_du = _gelu_and_grad(u_ref[...])
        gv, dgv_dv = _gelu_and_grad(vg_ref[...])
        xhat, rstd = _layer_norm_stats(gv)
        gain = g_ref[...]
        vgl = xhat * gain + b_ref[...]
        dvgl_parts = []
        for p in range(D_GMLP // LANES):
            cols = slice(p * LANES, (p + 1) * LANES)
            xp = vgl[:, cols]
            xl = jnp.where(left, xp, 0.0).astype(BF16)
            xr = jnp.where(left, 0.0, xp).astype(BF16)
            wl, wr, causal = _spatial_weights(w_ref, p)
            mixed = _dot(wl, xl) + _dot(wr, xr) + bias_ref[:, cols]
            da = dcat_ref[:, cols]
            dmain_ref[:, cols] = (da * mixed * dug_du[:, cols]).astype(BF16)
            dmixed = da * ug[:, cols]
            dmix_acc[:, cols] += dmixed
            dml = jnp.where(left, dmixed, 0.0).astype(BF16)
            dmr = jnp.where(left, 0.0, dmixed).astype(BF16)
            xpb = xp.astype(BF16)
            dw_ref[2 * p] += jnp.where(causal, _dot(dml, xpb, NT), 0.0)
            dw_ref[2 * p + 1] += jnp.where(causal, _dot(dmr, xpb, NT), 0.0)
            dvgl_parts.append(_dot(wl, dml, TN) + _dot(wr, dmr, TN))
        dvgl = jnp.concatenate(dvgl_parts, axis=1)
        dg_ref[...] += jnp.sum(dvgl * xhat, axis=0, keepdims=True)
        db_ref[...] += jnp.sum(dvgl, axis=0, keepdims=True)
        dgv = _layer_norm_bwd(dvgl * gain, xhat, rstd)
        dmain_ref[:, D_GMLP : 2 * D_GMLP] = (dgv * dgv_dv).astype(BF16)

        @pl.when(i == n_chunks - 1)
        def _():
            tile = jnp.zeros((CHUNK, LANES), F32)
            for p in range(D_GMLP // LANES):
                dm = dmix_acc[:, p * LANES : (p + 1) * LANES]
                sl = jnp.sum(jnp.where(left, dm, 0.0), axis=1, keepdims=True)
                sr = jnp.sum(jnp.where(left, 0.0, dm), axis=1, keepdims=True)
                tile = jnp.where(lane == 2 * p, sl, tile)
                tile = jnp.where(lane == 2 * p + 1, sr, tile)
            dbs_ref[...] = tile

        k_var = _kv_variants(jnp.concatenate([kp_ref[...], kc_ref[...]], axis=0))
        v_var = _kv_variants(jnp.concatenate([vp_ref[...], vc_ref[...]], axis=0))
        valid = _band_mask(i)
        dk_acc = [[None, None], [None, None]]
        dv_acc = [[None, None], [None, None]]
        dsink_row = jnp.zeros((1, LANES), F32)
        lane_row = lax.broadcasted_iota(jnp.int32, (1, LANES), 1)
        dq_parts = []
        for p in range(D_ATTN // LANES):
            qp = q_ref[:, p * LANES : (p + 1) * LANES]
            do = dcat_ref[:, D_GMLP + p * LANES : D_GMLP + (p + 1) * LANES]
            dob = do.astype(BF16)
            group = p // 2
            dq = jnp.zeros((CHUNK, LANES), F32)
            for side in range(2):
                head = 2 * p + side
                kx, vx = k_var[group][side], v_var[group][side]
                s = jnp.where(valid, _dot(qp, kx, NT) * SCALE, NEG_INF)
                probs, p_sink = _softmax_with_sink(s, sink_ref[head])
                dp = _dot(dob, vx, NT)
                dsum = jnp.sum(probs * dp, axis=1, keepdims=True)
                ds = (probs * (dp - dsum) * SCALE).astype(BF16)
                dsink_row = dsink_row + jnp.where(lane_row == head, -jnp.sum(p_sink * dsum, axis=0, keepdims=True), 0.0)
                dq = dq + _dot(ds, kx)
                in_side = left if side == 0 else jnp.logical_not(left)
                q_side = jnp.where(in_side, qp.astype(F32), 0.0).astype(BF16)
                do_side = jnp.where(in_side, do, 0.0).astype(BF16)
                dk_h = _dot(ds, q_side, TN)
                dv_h = _dot(probs.astype(BF16), do_side, TN)
                dk_acc[group][side] = dk_h if dk_acc[group][side] is None else dk_acc[group][side] + dk_h
                dv_acc[group][side] = dv_h if dv_acc[group][side] is None else dv_acc[group][side] + dv_h
            dq_parts.append(dq)
        dsink_ref[0:1, :] += dsink_row
        cos_c, sin_c = cosc_ref[...], sinc_ref[...]
        dq_all = jnp.concatenate(dq_parts, axis=1)
        reps = D_ATTN // LANES
        dmain_ref[:, 2 * D_GMLP : D_MAIN] = _rope_transposed(dq_all, _lane_tile(cos_c, reps), _lane_tile(sin_c, reps)).astype(BF16)

        def group_lanes(acc):
            return acc[0][0] + pltpu.roll(acc[0][1], HEAD_DIM, 1) + pltpu.roll(acc[1][0], HEAD_DIM, 1) + acc[1][1]

        dk2 = group_lanes(dk_acc)
        dv2 = group_lanes(dv_acc)
        cur = pl.ds(pl.multiple_of(i * CHUNK, CHUNK), CHUNK)
        dkv_ref[cur, 0:D_KV] = _rope_transposed(dk2[CHUNK:], cos_c, sin_c)
        dkv_ref[cur, D_KV : 2 * D_KV] = dv2[CHUNK:]

        @pl.when(i > 0)
        def _():
            prev = pl.ds(pl.multiple_of((i - 1) * CHUNK, CHUNK), CHUNK)
            dkv_ref[prev, 0:D_KV] += _rope_transposed(dk2[:CHUNK], cosp_ref[...], sinp_ref[...])
            dkv_ref[prev, D_KV : 2 * D_KV] += dv2[:CHUNK]

    cur = lambda i: (i, 0)
    prev = lambda i: (jnp.maximum(i - 1, 0), 0)
    return pl.pallas_call(
        body,
        name="mixer_bwd",
        grid=(n_chunks,),
        in_specs=_chunk_specs()
        + [
            pl.BlockSpec((CHUNK, D_MODEL), cur),
            pl.BlockSpec((CHUNK, LANES), cur),
            pl.BlockSpec((CHUNK, LANES), cur),
            pl.BlockSpec((CHUNK, LANES), prev),
            pl.BlockSpec((CHUNK, LANES), prev),
            _const_spec((1, D_GMLP)),
            _const_spec((1, D_GMLP)),
            _const_spec((N_HEADS, CHUNK, CHUNK)),
            _const_spec((CHUNK, D_GMLP)),
            pl.BlockSpec(memory_space=pltpu.SMEM),
        ],
        out_specs=[
            pl.BlockSpec((CHUNK, D_MAIN), cur),
            _const_spec((t, 2 * D_KV)),
            _const_spec((1, D_GMLP)),
            _const_spec((1, D_GMLP)),
            _const_spec((N_HEADS, CHUNK, CHUNK)),
            _const_spec((CHUNK, LANES)),
            _const_spec((8, LANES)),
        ],
        out_shape=[
            jax.ShapeDtypeStruct((t, D_MAIN), BF16),
            jax.ShapeDtypeStruct((t, 2 * D_KV), F32),
            jax.ShapeDtypeStruct((1, D_GMLP), F32),
            jax.ShapeDtypeStruct((1, D_GMLP), F32),
            jax.ShapeDtypeStruct((N_HEADS, CHUNK, CHUNK), F32),
            jax.ShapeDtypeStruct((CHUNK, LANES), F32),
            jax.ShapeDtypeStruct((8, LANES), F32),
        ],
        scratch_shapes=[pltpu.VMEM((CHUNK, D_GMLP), F32)],
        compiler_params=_params(("arbitrary",)),
    )(u, vg, q, k, k, va, va, dcat, cos, sin, cos, sin, v_ln_g, v_ln_b, w_spatial, bias_full, sinks)


def _grad_x(dh_main, dkv, dz1, w_in_t):
    t = dz1.shape[0]

    def body(dm_ref, dkv_ref, dz1_ref, w_ref, gx_ref):
        acc = ALPHA * dz1_ref[...] + _dot(dm_ref[...], w_ref[0:D_MAIN, :])
        gx_ref[...] = acc + _dot(dkv_ref[...].astype(BF16), w_ref[D_MAIN:D_IN, :])

    return pl.pallas_call(
        body,
        name="grad_x",
        grid=(t // TM,),
        in_specs=[_row_spec(TM, D_MAIN), _row_spec(TM, 2 * D_KV), _row_spec(TM, D_MODEL), _const_spec((D_IN, D_MODEL))],
        out_specs=_row_spec(TM, D_MODEL),
        out_shape=jax.ShapeDtypeStruct((t, D_MODEL), F32),
        compiler_params=_params(("parallel",)),
    )(dh_main, dkv, dz1, w_in_t)


def _token_contraction(name, n_blocks, out_rows, in_arrays, in_specs, contributions):
    t = in_arrays[0].shape[0]
    block_rows = out_rows // n_blocks

    def body(*refs):
        out_ref = refs[-1]

        @pl.when(pl.program_id(1) == 0)
        def _():
            out_ref[...] = jnp.zeros_like(out_ref)

        for row0, a, b in contributions(*refs[:-1]):
            out_ref[row0 : row0 + a.shape[1], :] += _dot(a, b, TN)

    return pl.pallas_call(
        body,
        name=name,
        grid=(n_blocks, t // TK),
        in_specs=in_specs,
        out_specs=pl.BlockSpec((block_rows, D_MODEL), lambda j, k: (j, 0)),
        out_shape=jax.ShapeDtypeStruct((out_rows, D_MODEL), F32),
        compiler_params=_params(("parallel", "arbitrary")),
    )(*in_arrays)


def _tile_spec(cols):
    return pl.BlockSpec((TK, cols), lambda j, k: (k, 0))


def _tile_block_spec():
    return pl.BlockSpec((TK, D_MODEL), lambda j, k: (k, j))


def _grad_w_in_t(dh_main, dkv, x):
    def contributions(dm_ref, dkv_ref, x_ref):
        xb = x_ref[...].astype(BF16)
        return [(0, dm_ref[...], xb), (D_MAIN, dkv_ref[...].astype(BF16), xb)]

    return _token_contraction("grad_w_in", 1, D_IN, [dh_main, dkv, x], [_tile_spec(D_MAIN), _tile_spec(2 * D_KV), _tile_spec(D_MODEL)], contributions)


def _grad_w_out(cat, dz1):
    def contributions(cat_ref, dz1_ref):
        return [(0, cat_ref[...], dz1_ref[...].astype(BF16))]

    return _token_contraction("grad_w_out", 1, D_MODEL, [cat, dz1], [_tile_spec(D_MODEL), _tile_spec(D_MODEL)], contributions)


def _grad_w_ff1(xhat1, ln1_g, ln1_b, dpre):
    def contributions(xh_ref, g_ref, b_ref, dpre_ref):
        return [(0, (xh_ref[...] * g_ref[...] + b_ref[...]).astype(BF16), dpre_ref[...])]

    vec = pl.BlockSpec((1, D_MODEL), lambda j, k: (0, 0))
    return _token_contraction("grad_w_ff1", N_FF_BLOCKS, D_FF, [xhat1, ln1_g, ln1_b, dpre], [_tile_spec(D_MODEL), vec, vec, _tile_block_spec()], contributions)


def _grad_w_ff2(r, dz2):
    def contributions(r_ref, dz2_ref):
        rf = r_ref[...].astype(F32)
        return [(0, (rf * rf).astype(BF16), dz2_ref[...].astype(BF16))]

    return _token_contraction("grad_w_ff2", N_FF_BLOCKS, D_FF, [r, dz2], [_tile_block_spec(), _tile_spec(D_MODEL)], contributions)


ANY = pl.BlockSpec(memory_space=pl.ANY)


def _mesh_position():
    return lax.axis_index("x"), lax.axis_index("y"), lax.axis_index("c")


def _other_chips(x, y):
    return [(1 - x, y), (x, 1 - y), (1 - x, 1 - y)]


def _remote(src, dst, send_sem, recv_sem, device):
    return pltpu.make_async_remote_copy(src_ref=src, dst_ref=dst, send_sem=send_sem, recv_sem=recv_sem, device_id=device, device_id_type=MESH)


def _rows(ref, start, size):
    return ref.at[pl.ds(start, size), :]


def _all_gather_weights(shards):
    n = len(shards)

    def body(*refs):
        ins, outs = refs[:n], refs[n : 2 * n]
        send_sems, recv_sems, local_sems = refs[2 * n :]
        x, y, c = _mesh_position()
        me = 2 * x + y
        chips = _other_chips(x, y)
        sibling = (x, y, 1 - c)
        started = []
        for w in range(n):
            rows = shards[w].shape[0]
            half = rows // 2
            local = pltpu.make_async_copy(ins[w], _rows(outs[w], me * rows, rows), local_sems.at[w])
            local.start()
            started.append(local)
            for kk, (px, py) in enumerate(chips):
                cp = _remote(_rows(ins[w], c * half, half), _rows(outs[w], me * rows + c * half, half),
                             send_sems.at[6 * w + kk], recv_sems.at[6 * w + kk], (px, py, c))
                cp.start()
                started.append(cp)
        passed = []
        for w in range(n):
            rows = shards[w].shape[0]
            half = rows // 2
            for kk, (px, py) in enumerate(chips):
                blk = _rows(outs[w], (2 * px + py) * rows + c * half, half)
                _remote(blk, blk, send_sems.at[6 * w + kk], recv_sems.at[6 * w + kk], (px, py, c)).wait_recv()
                fwd = _remote(blk, blk, send_sems.at[6 * w + 3 + kk], recv_sems.at[6 * w + 3 + kk], sibling)
                fwd.start()
                passed.append(fwd)
        for w in range(n):
            rows = shards[w].shape[0]
            half = rows // 2
            for kk, (px, py) in enumerate(chips):
                blk = _rows(outs[w], (2 * px + py) * rows + (1 - c) * half, half)
                _remote(blk, blk, send_sems.at[6 * w + 3 + kk], recv_sems.at[6 * w + 3 + kk], sibling).wait_recv()
        for cp in started[1::4] + started[2::4] + started[3::4] + passed:
            cp.wait_send()
        for local in started[0::4]:
            local.wait()

    return pl.pallas_call(
        body,
        name="all_gather_weights",
        in_specs=[ANY] * n,
        out_specs=[ANY] * n,
        out_shape=[jax.ShapeDtypeStruct((N_CHIPS * s.shape[0], s.shape[1]), s.dtype) for s in shards],
        scratch_shapes=[pltpu.SemaphoreType.DMA((6 * n,)), pltpu.SemaphoreType.DMA((6 * n,)), pltpu.SemaphoreType.DMA((n,))],
    )(*shards)


def _pair_swap(grads):
    n = len(grads)

    def body(*refs):
        ins, owns, theirs = refs[:n], refs[n : 2 * n], refs[2 * n : 3 * n]
        send_sems, recv_sems, local_sems = refs[3 * n :]
        x, y, c = _mesh_position()
        sibling = (x, y, 1 - c)
        sends, locals_ = [], []
        for w in range(n):
            rows = grads[w].shape[0] // N_CHIPS
            half = rows // 2
            for j in range(N_CHIPS):
                local = pltpu.make_async_copy(_rows(ins[w], j * rows + c * half, half), _rows(owns[w], j * half, half), local_sems.at[4 * w + j])
                local.start()
                locals_.append(local)
                cp = _remote(_rows(ins[w], j * rows + (1 - c) * half, half), _rows(theirs[w], j * half, half),
                             send_sems.at[4 * w + j], recv_sems.at[4 * w + j], sibling)
                cp.start()
                sends.append(cp)
        for cp in sends:
            cp.wait_recv()
        for cp in sends:
            cp.wait_send()
        for local in locals_:
            local.wait()

    halves = [jax.ShapeDtypeStruct((g.shape[0] // 2, g.shape[1]), g.dtype) for g in grads]
    outs = pl.pallas_call(
        body,
        name="grad_pair_swap",
        in_specs=[ANY] * n,
        out_specs=[ANY] * (2 * n),
        out_shape=halves + halves,
        scratch_shapes=[pltpu.SemaphoreType.DMA((4 * n,)), pltpu.SemaphoreType.DMA((4 * n,)), pltpu.SemaphoreType.DMA((4 * n,))],
    )(*grads)
    return outs[:n], outs[n:]


def _chip_exchange(partials):
    n = len(partials)

    def body(*refs):
        ins, outs = refs[:n], refs[n : 2 * n]
        send_sems, recv_sems, local_sems = refs[2 * n :]
        x, y, c = _mesh_position()
        me = 2 * x + y
        chips = _other_chips(x, y)
        sends, locals_ = [], []
        for w in range(n):
            half = partials[w].shape[0] // N_CHIPS
            local = pltpu.make_async_copy(_rows(ins[w], me * half, half), _rows(outs[w], me * half, half), local_sems.at[w])
            local.start()
            locals_.append(local)
            for kk, (px, py) in enumerate(chips):
                cp = _remote(_rows(ins[w], (2 * px + py) * half, half), _rows(outs[w], me * half, half),
                             send_sems.at[3 * w + kk], recv_sems.at[3 * w + kk], (px, py, c))
                cp.start()
                sends.append(cp)
        for w in range(n):
            half = partials[w].shape[0] // N_CHIPS
            for kk, (px, py) in enumerate(chips):
                blk = _rows(outs[w], (2 * px + py) * half, half)
                _remote(blk, blk, send_sems.at[3 * w + kk], recv_sems.at[3 * w + kk], (px, py, c)).wait_recv()
        for cp in sends:
            cp.wait_send()
        for local in locals_:
            local.wait()

    return pl.pallas_call(
        body,
        name="grad_chip_exchange",
        in_specs=[ANY] * n,
        out_specs=[ANY] * n,
        out_shape=[jax.ShapeDtypeStruct(p.shape, p.dtype) for p in partials],
        scratch_shapes=[pltpu.SemaphoreType.DMA((3 * n,)), pltpu.SemaphoreType.DMA((3 * n,)), pltpu.SemaphoreType.DMA((n,))],
    )(*partials)


def _pair_gather(halves):
    n = len(halves)

    def body(*refs):
        ins, outs = refs[:n], refs[n : 2 * n]
        send_sems, recv_sems, local_sems = refs[2 * n :]
        x, y, c = _mesh_position()
        sibling = (x, y, 1 - c)
        sends, locals_ = [], []
        for w in range(n):
            half = halves[w].shape[0]
            local = pltpu.make_async_copy(ins[w], _rows(outs[w], c * half, half), local_sems.at[w])
            local.start()
            locals_.append(local)
            cp = _remote(ins[w], _rows(outs[w], c * half, half), send_sems.at[w], recv_sems.at[w], sibling)
            cp.start()
            sends.append(cp)
        for w in range(n):
            half = halves[w].shape[0]
            blk = _rows(outs[w], (1 - c) * half, half)
            _remote(blk, blk, send_sems.at[w], recv_sems.at[w], sibling).wait_recv()
        for cp in sends:
            cp.wait_send()
        for local in locals_:
            local.wait()

    return pl.pallas_call(
        body,
        name="grad_pair_gather",
        in_specs=[ANY] * n,
        out_specs=[ANY] * n,
        out_shape=[jax.ShapeDtypeStruct((2 * h.shape[0], h.shape[1]), h.dtype) for h in halves],
        scratch_shapes=[pltpu.SemaphoreType.DMA((n,)), pltpu.SemaphoreType.DMA((n,)), pltpu.SemaphoreType.DMA((n,))],
    )(*halves)


def _all_reduce_small(slab):
    rows = slab.shape[0]
    part = rows // 8

    def body(slab_ref, out_ref, landing, reduced, send_sems, recv_sems):
        x, y, c = _mesh_position()
        me = 4 * x + 2 * y + c
        flips = [(k >> 2, (k >> 1) & 1, k & 1) for k in range(1, 8)]

        def peer(flip):
            fx, fy, fc = flip
            return (1 - x if fx else x, 1 - y if fy else y, 1 - c if fc else c)

        def my_rows(ref):
            return ref.at[pl.ds(pl.multiple_of(me * part, 8), part), :]

        sends = []
        for kk, flip in enumerate(flips):
            px, py, pc = peer(flip)
            them = 4 * px + 2 * py + pc
            cp = _remote(slab_ref.at[pl.ds(pl.multiple_of(them * part, 8), part), :], landing.at[me], send_sems.at[kk], recv_sems.at[kk], (px, py, pc))
            cp.start()
            sends.append(cp)
        landing[me] = my_rows(slab_ref)[...]
        for kk, flip in enumerate(flips):
            px, py, pc = peer(flip)
            them = 4 * px + 2 * py + pc
            _remote(landing.at[them], landing.at[them], send_sems.at[kk], recv_sems.at[kk], (px, py, pc)).wait_recv()
        total = landing[0]
        for s in range(1, 8):
            total = total + landing[s]
        reduced[...] = total
        my_rows(out_ref)[...] = total
        for kk, flip in enumerate(flips):
            cp = _remote(reduced, my_rows(out_ref), send_sems.at[7 + kk], recv_sems.at[7 + kk], peer(flip))
            cp.start()
            sends.append(cp)
        for kk, flip in enumerate(flips):
            px, py, pc = peer(flip)
            them = 4 * px + 2 * py + pc
            blk = out_ref.at[pl.ds(pl.multiple_of(them * part, 8), part), :]
            _remote(blk, blk, send_sems.at[7 + kk], recv_sems.at[7 + kk], (px, py, pc)).wait_recv()
        for cp in sends:
            cp.wait_send()

    vmem = pl.BlockSpec(memory_space=pltpu.VMEM)
    return pl.pallas_call(
        body,
        name="all_reduce_small",
        in_specs=[vmem],
        out_specs=vmem,
        out_shape=jax.ShapeDtypeStruct(slab.shape, slab.dtype),
        scratch_shapes=[pltpu.VMEM((8, part, LANES), F32), pltpu.VMEM((part, LANES), F32), pltpu.SemaphoreType.DMA((14,)), pltpu.SemaphoreType.DMA((14,))],
    )(slab)


def _row_tile(rows, cap=256):
    best = 8
    for cand in range(8, cap + 1, 8):
        if rows % cand == 0:
            best = cand
    return best


def _pair_sum(own, theirs):
    rows, cols = own.shape
    tile = _row_tile(rows)

    def body(a_ref, b_ref, o_ref):
        o_ref[...] = (a_ref[...] + b_ref[...]).astype(BF16)

    return pl.pallas_call(
        body,
        name="grad_pair_sum",
        grid=(rows // tile,),
        in_specs=[_row_spec(tile, cols), _row_spec(tile, cols)],
        out_specs=_row_spec(tile, cols),
        out_shape=jax.ShapeDtypeStruct((rows, cols), BF16),
        compiler_params=_params(("parallel",)),
    )(own, theirs)


def _chip_sum(parts):
    half = parts.shape[0] // N_CHIPS
    cols = parts.shape[1]
    tile = _row_tile(half)
    steps = half // tile

    def body(p0, p1, p2, p3, o_ref):
        o_ref[...] = ((p0[...].astype(F32) + p1[...].astype(F32)) + p2[...].astype(F32)) + p3[...].astype(F32)

    specs = [pl.BlockSpec((tile, cols), lambda i, _k=k: (_k * steps + i, 0)) for k in range(N_CHIPS)]
    return pl.pallas_call(
        body,
        name="grad_chip_sum",
        grid=(steps,),
        in_specs=specs,
        out_specs=_row_spec(tile, cols),
        out_shape=jax.ShapeDtypeStruct((half, cols), F32),
        compiler_params=_params(("parallel",)),
    )(parts, parts, parts, parts)


def _adamw(name, w, g, m, v):
    rows, cols = w.shape
    tile = rows if rows * cols <= 256 * 1024 else _row_tile(rows)

    def body(w_ref, g_ref, m_ref, v_ref, d_ref, nm_ref, nv_ref):
        g = g_ref[...]
        nm = ADAM_B1 * m_ref[...] + (1.0 - ADAM_B1) * g
        nv = ADAM_B2 * v_ref[...] + (1.0 - ADAM_B2) * (g * g)
        m_hat = nm / (1.0 - ADAM_B1**ADAM_STEP)
        v_hat = nv / (1.0 - ADAM_B2**ADAM_STEP)
        d_ref[...] = -ADAM_LR * (m_hat / (jnp.sqrt(v_hat) + ADAM_EPS) + ADAM_WD * w_ref[...])
        nm_ref[...] = nm
        nv_ref[...] = nv

    spec = _row_spec(tile, cols)
    return pl.pallas_call(
        body,
        name=name,
        grid=(rows // tile,),
        in_specs=[spec] * 4,
        out_specs=[spec] * 3,
        out_shape=[jax.ShapeDtypeStruct((rows, cols), F32)] * 3,
        compiler_params=_params(("parallel",)),
    )(w, g, m, v)


_SMALL = (
    ("v_ln_g", (D_GMLP,), 8),
    ("v_ln_b", (D_GMLP,), 8),
    ("w_spatial", (N_HEADS, CHUNK, CHUNK), 1024),
    ("b_spatial", (N_HEADS, CHUNK), 8),
    ("sinks", (N_HEADS,), 8),
    ("ln1_g", (D_MODEL,), 8),
    ("ln1_b", (D_MODEL,), 8),
    ("ln2_g", (D_MODEL,), 8),
    ("ln2_b", (D_MODEL,), 8),
)


def _pack_small(values):
    parts = []
    for (name, shape, rows), val in zip(_SMALL, values, strict=True):
        flat = val.reshape(-1).astype(F32)
        parts.append(jnp.pad(flat, (0, rows * LANES - flat.shape[0])).reshape(rows, LANES))
    return jnp.concatenate(parts, axis=0)


def _unpack_small(slab):
    out, row = [], 0
    for name, shape, rows in _SMALL:
        size = math.prod(shape)
        out.append(slab[row : row + rows].reshape(-1)[:size].reshape((1,) + shape))
        row += rows
    return out


def kernel(x, positions, w_in, v_ln_g, v_ln_b, w_spatial, b_spatial, sinks, w_out, ln1_g, ln1_b, w_ff1, w_ff2, ln2_g, ln2_b, loss_target, m_w_in, m_v_ln_g, m_v_ln_b, m_w_spatial, m_b_spatial, m_sinks, m_w_out, m_ln1_g, m_ln1_b, m_w_ff1, m_w_ff2, m_ln2_g, m_ln2_b, v_w_in, v_v_ln_g, v_v_ln_b, v_w_spatial, v_b_spatial, v_sinks, v_w_out, v_ln1_g, v_ln1_b, v_w_ff1, v_w_ff2, v_ln2_g, v_ln2_b):
    t = x.shape[1]
    x2 = x.reshape(t, D_MODEL)
    target = loss_target.reshape(t, D_MODEL)

    shards = [w_in[0].T.astype(BF16), w_out[0].astype(BF16), w_ff1[0].astype(BF16), w_ff2[0].astype(BF16)]
    w_in_t, w_out_all, w1_all, w2_all = _all_gather_weights(shards)
    w1_all = w1_all.reshape(N_FF_BLOCKS, D_MODEL, D_MODEL)
    w2_all = w2_all.reshape(N_FF_BLOCKS, D_MODEL, D_MODEL)

    inv_freq = ROPE_THETA ** (-jnp.arange(0, HEAD_DIM, 2, dtype=F32) / HEAD_DIM)
    cos, sin = _rope_tables(positions.reshape(t, 1), jnp.tile(inv_freq, LANES // (HEAD_DIM // 2)).reshape(1, LANES))
    u, vg, q, k, va = _in_proj(x2, w_in_t, cos, sin)
    bias_full = jnp.repeat(b_spatial[0].T, HEAD_DIM, axis=1)
    sink_vec = sinks.reshape(N_HEADS)
    cat = _mixer_fwd(u, vg, q, k, va, v_ln_g, v_ln_b, w_spatial[0], bias_full, sink_vec)
    xhat1, rstd1 = _out_proj_ln1(cat, x2, w_out_all)
    r, dz2, d_ln2_g, d_ln2_b, sq_err = _ffn_fwd_loss(xhat1, ln1_g, ln1_b, w1_all, w2_all, ln2_g, ln2_b, target)
    loss = lax.psum(0.5 * jnp.sum(sq_err) / D_MODEL, ("x", "y", "c"))

    dpre, dz1, d_ln1_g, d_ln1_b = _ffn_bwd_ln1(dz2, r, xhat1, rstd1, ln1_g, w1_all, w2_all)
    dcat = _dcat(dz1, w_out_all)
    dh_main, dkv, d_v_ln_g, d_v_ln_b, d_w_spatial, d_b_spatial_t, d_sinks = _mixer_bwd(
        u, vg, q, k, va, dcat, cos, sin, v_ln_g, v_ln_b, w_spatial[0], bias_full, sink_vec)
    grad_x = _grad_x(dh_main, dkv, dz1, w_in_t).reshape(1, t, D_MODEL)
    grads = [_grad_w_in_t(dh_main, dkv, x2), _grad_w_out(cat, dz1), _grad_w_ff1(xhat1, ln1_g, ln1_b, dpre), _grad_w_ff2(r, dz2)]

    own, theirs = _pair_swap(grads)
    partials = [_pair_sum(a, b) for a, b in zip(own, theirs)]
    exchanged = _chip_exchange(partials)
    g_w_in_t, g_w_out, g_w_ff1, g_w_ff2 = _pair_gather([_chip_sum(p) for p in exchanged])
    g_w_in = g_w_in_t.T

    small_g = _all_reduce_small(_pack_small(
        [d_v_ln_g, d_v_ln_b, d_w_spatial, d_b_spatial_t[:, :N_HEADS].T, d_sinks[0, :N_HEADS], d_ln1_g, d_ln1_b, d_ln2_g, d_ln2_b]))
    small_grads = _unpack_small(small_g)

    d_w_in, nm_w_in, nv_w_in = _adamw("adamw_w_in", w_in[0], g_w_in, m_w_in[0], v_w_in[0])
    d_w_out, nm_w_out, nv_w_out = _adamw("adamw_w_out", w_out[0], g_w_out, m_w_out[0], v_w_out[0])
    d_w_ff1, nm_w_ff1, nv_w_ff1 = _adamw("adamw_w_ff1", w_ff1[0], g_w_ff1, m_w_ff1[0], v_w_ff1[0])
    d_w_ff2, nm_w_ff2, nv_w_ff2 = _adamw("adamw_w_ff2", w_ff2[0], g_w_ff2, m_w_ff2[0], v_w_ff2[0])
    small_w = _pack_small([v_ln_g, v_ln_b, w_spatial, b_spatial, sinks, ln1_g, ln1_b, ln2_g, ln2_b])
    small_m = _pack_small([m_v_ln_g, m_v_ln_b, m_w_spatial, m_b_spatial, m_sinks, m_ln1_g, m_ln1_b, m_ln2_g, m_ln2_b])
    small_v = _pack_small([v_v_ln_g, v_v_ln_b, v_w_spatial, v_b_spatial, v_sinks, v_ln1_g, v_ln1_b, v_ln2_g, v_ln2_b])
    small_d, small_nm, small_nv = (_unpack_small(s) for s in _adamw("adamw_small", small_w, small_g, small_m, small_v))

    def with_big(small, w_in_v, w_out_v, w_ff1_v, w_ff2_v):
        g_vg, g_vb, g_ws, g_bs, g_sk, g_1g, g_1b, g_2g, g_2b = small
        return [w_in_v[None], g_vg, g_vb, g_ws, g_bs, g_sk, w_out_v[None], g_1g, g_1b, w_ff1_v[None], w_ff2_v[None], g_2g, g_2b]

    return (
        loss,
        grad_x,
        *with_big(small_grads, g_w_in, g_w_out, g_w_ff1, g_w_ff2),
        *with_big(small_d, d_w_in, d_w_out, d_w_ff1, d_w_ff2),
        *with_big(small_nm, nm_w_in, nm_w_out, nm_w_ff1, nm_w_ff2),
        *with_big(small_nv, nv_w_in, nv_w_out, nv_w_ff1, nv_w_ff2),
    )
```

```python
import functools
import math

import jax
import jax.numpy as jnp
from jax import lax
from jax.experimental import pallas as pl
from jax.experimental.pallas import tpu as pltpu

F32 = jnp.float32
BF16 = jnp.bfloat16

D_MODEL = 1024
HEAD_DIM = 64
D_GMLP = 512
D_ATTN = 512
D_KV = 128
D_IN = 2 * D_GMLP + D_ATTN + 2 * D_KV
D_MAIN = 2 * D_GMLP + D_ATTN
N_HEADS = 8
CHUNK = 128
ROPE_THETA = 10000.0
D_FF = 4 * D_MODEL
N_FF_BLOCKS = 4
LN_EPS = 1e-5
ALPHA = (2.0 * 1) ** 0.25
NEG_INF = -1e30
SCALE = 1.0 / math.sqrt(HEAD_DIM)

ADAM_LR = 0.001
ADAM_B1 = 0.9
ADAM_B2 = 0.999
ADAM_EPS = 1e-08
ADAM_WD = 0.01
ADAM_STEP = 10

N_CHIPS = 4
LANES = 128
V7X_VMEM_BYTES = 64 * 1024 * 1024
VMEM_LIMIT = V7X_VMEM_BYTES - 8 * 1024 * 1024
TM = 512
TM_FFN = 256
TK = 512
SMALL_ROWS = 1088
MESH = pl.DeviceIdType.MESH

NT = (((1,), (1,)), ((), ()))
TN = (((0,), (0,)), ((), ()))


def _dot(a, b, dims=None):
    if dims is None:
        return jnp.dot(a, b, preferred_element_type=F32)
    return lax.dot_general(a, b, dims, preferred_element_type=F32)


def _params(semantics=None):
    return pltpu.CompilerParams(dimension_semantics=semantics, vmem_limit_bytes=VMEM_LIMIT)


def _const_spec(shape, single_buffer=False):
    zeros = (0,) * len(shape)
    if single_buffer:
        return pl.BlockSpec(shape, lambda *_: zeros, pipeline_mode=pl.Buffered(1))
    return pl.BlockSpec(shape, lambda *_: zeros)


def _row_spec(rows, cols):
    return pl.BlockSpec((rows, cols), lambda i: (i, 0))


def _gelu(x):
    k = math.sqrt(2.0 / math.pi)
    return 0.5 * x * (1.0 + jnp.tanh(k * (x + 0.044715 * (x * x * x))))


def _gelu_and_grad(x):
    k = math.sqrt(2.0 / math.pi)
    x2 = x * x
    t = jnp.tanh(k * (x + 0.044715 * (x2 * x)))
    g = 0.5 * x * (1.0 + t)
    dg = 0.5 * (1.0 + t) + 0.5 * x * (1.0 - t * t) * (k * (1.0 + 3.0 * 0.044715 * x2))
    return g, dg


def _layer_norm_stats(z):
    mu = jnp.mean(z, axis=-1, keepdims=True)
    zc = z - mu
    var = jnp.mean(zc * zc, axis=-1, keepdims=True)
    rstd = lax.rsqrt(var + LN_EPS)
    return zc * rstd, rstd


def _layer_norm_bwd(dxhat, xhat, rstd):
    m1 = jnp.mean(dxhat, axis=-1, keepdims=True)
    m2 = jnp.mean(dxhat * xhat, axis=-1, keepdims=True)
    return rstd * (dxhat - m1 - xhat * m2)


def _rotate_half(t):
    n = t.shape[1]
    lane = lax.broadcasted_iota(jnp.int32, t.shape, 1)
    first = (lane & (HEAD_DIM // 2)) == 0
    return jnp.where(first, -pltpu.roll(t, n - HEAD_DIM // 2, 1), pltpu.roll(t, HEAD_DIM // 2, 1))


def _rope(t, cos, sin):
    return t * cos + _rotate_half(t) * sin


def _rope_transposed(g, cos, sin):
    return g * cos - _rotate_half(g * sin)


def _lane_tile(a, reps):
    return jnp.tile(a, (1, reps)) if reps > 1 else a


def _rope_tables(pos_col, inv_freq_row):
    t = pos_col.shape[0]

    def body(pos_ref, f_ref, cos_ref, sin_ref):
        ang = pos_ref[...].astype(F32) * f_ref[...]
        cos_ref[...] = jnp.cos(ang)
        sin_ref[...] = jnp.sin(ang)

    return pl.pallas_call(
        body,
        name="rope_tables",
        grid=(t // TM,),
        in_specs=[_row_spec(TM, 1), _const_spec((1, LANES))],
        out_specs=[_row_spec(TM, LANES), _row_spec(TM, LANES)],
        out_shape=[jax.ShapeDtypeStruct((t, LANES), F32)] * 2,
        compiler_params=_params(("parallel",)),
    )(pos_col, inv_freq_row)


def _in_proj(x, w_in_t, cos, sin):
    t = x.shape[0]

    def body(x_ref, w_ref, cos_ref, sin_ref, u_ref, vg_ref, q_ref, k_ref, va_ref):
        xb = x_ref[...].astype(BF16)
        u_ref[...] = _dot(xb, w_ref[0:D_GMLP, :], NT)
        vg_ref[...] = _dot(xb, w_ref[D_GMLP : 2 * D_GMLP, :], NT)
        q = _dot(xb, w_ref[2 * D_GMLP : D_MAIN, :], NT)
        k = _dot(xb, w_ref[D_MAIN : D_MAIN + D_KV, :], NT)
        va_ref[...] = _dot(xb, w_ref[D_MAIN + D_KV : D_IN, :], NT).astype(BF16)
        c, s = cos_ref[...], sin_ref[...]
        q_ref[...] = _rope(q, _lane_tile(c, D_ATTN // LANES), _lane_tile(s, D_ATTN // LANES)).astype(BF16)
        k_ref[...] = _rope(k, c, s).astype(BF16)

    return pl.pallas_call(
        body,
        name="in_proj",
        grid=(t // TM,),
        in_specs=[_row_spec(TM, D_MODEL), _const_spec((D_IN, D_MODEL)), _row_spec(TM, LANES), _row_spec(TM, LANES)],
        out_specs=[_row_spec(TM, D_GMLP), _row_spec(TM, D_GMLP), _row_spec(TM, D_ATTN), _row_spec(TM, D_KV), _row_spec(TM, D_KV)],
        out_shape=[
            jax.ShapeDtypeStruct((t, D_GMLP), F32),
            jax.ShapeDtypeStruct((t, D_GMLP), F32),
            jax.ShapeDtypeStruct((t, D_ATTN), BF16),
            jax.ShapeDtypeStruct((t, D_KV), BF16),
            jax.ShapeDtypeStruct((t, D_KV), BF16),
        ],
        compiler_params=_params(("parallel",)),
    )(x, w_in_t, cos, sin)


def _chunk_specs():
    cur = lambda i: (i, 0)
    prev = lambda i: (jnp.maximum(i - 1, 0), 0)
    return [
        pl.BlockSpec((CHUNK, D_GMLP), cur),
        pl.BlockSpec((CHUNK, D_GMLP), cur),
        pl.BlockSpec((CHUNK, D_ATTN), cur),
        pl.BlockSpec((CHUNK, D_KV), cur),
        pl.BlockSpec((CHUNK, D_KV), prev),
        pl.BlockSpec((CHUNK, D_KV), cur),
        pl.BlockSpec((CHUNK, D_KV), prev),
    ]


def _half_lane_masks(rows):
    lane = lax.broadcasted_iota(jnp.int32, (rows, LANES), 1)
    return lane < HEAD_DIM


def _kv_variants(kv2):
    left = _half_lane_masks(kv2.shape[0])
    f = kv2.astype(F32)
    swapped = pltpu.roll(f, HEAD_DIM, 1)
    zero = jnp.zeros_like(f)
    g0 = (jnp.where(left, f, zero).astype(BF16), jnp.where(left, zero, swapped).astype(BF16))
    g1 = (jnp.where(left, swapped, zero).astype(BF16), jnp.where(left, zero, f).astype(BF16))
    return (g0, g1)


def _band_mask(i):
    row = lax.broadcasted_iota(jnp.int32, (CHUNK, 2 * CHUNK), 0)
    col = lax.broadcasted_iota(jnp.int32, (CHUNK, 2 * CHUNK), 1)
    no_prev = jnp.where(i > 0, 0, 4 * CHUNK)
    in_prev = jnp.logical_and(col < CHUNK, (col - row) > no_prev)
    in_cur = jnp.logical_and(col >= CHUNK, (col - CHUNK) <= row)
    return jnp.logical_or(in_prev, in_cur)


def _softmax_with_sink(s, sink):
    m = jnp.maximum(jnp.max(s, axis=1, keepdims=True), sink)
    e = jnp.exp(s - m)
    e_sink = jnp.exp(sink - m)
    inv = 1.0 / (jnp.sum(e, axis=1, keepdims=True) + e_sink)
    return e * inv, e_sink * inv


def _spatial_weights(w_ref, pair):
    row = lax.broadcasted_iota(jnp.int32, (CHUNK, CHUNK), 0)
    col = lax.broadcasted_iota(jnp.int32, (CHUNK, CHUNK), 1)
    causal = col <= row
    wl = jnp.where(causal, w_ref[2 * pair], 0.0).astype(BF16)
    wr = jnp.where(causal, w_ref[2 * pair + 1], 0.0).astype(BF16)
    return wl, wr, causal


def _mixer_fwd(u, vg, q, k, va, v_ln_g, v_ln_b, w_spatial, bias_full, sinks):
    t = u.shape[0]

    def body(u_ref, vg_ref, q_ref, kc_ref, kp_ref, vc_ref, vp_ref, g_ref, b_ref, w_ref, bias_ref, sink_ref, cat_ref):
        i = pl.program_id(0)
        left = _half_lane_masks(CHUNK)
        ug = _gelu(u_ref[...])
        xhat, _ = _layer_norm_stats(_gelu(vg_ref[...]))
        vgl = xhat * g_ref[...] + b_ref[...]
        for p in range(D_GMLP // LANES):
            cols = slice(p * LANES, (p + 1) * LANES)
            xp = vgl[:, cols]
            wl, wr, _ = _spatial_weights(w_ref, p)
            mixed = _dot(wl, jnp.where(left, xp, 0.0).astype(BF16)) + _dot(wr, jnp.where(left, 0.0, xp).astype(BF16))
            cat_ref[:, cols] = (ug[:, cols] * (mixed + bias_ref[:, cols])).astype(BF16)

        k_var = _kv_variants(jnp.concatenate([kp_ref[...], kc_ref[...]], axis=0))
        v_var = _kv_variants(jnp.concatenate([vp_ref[...], vc_ref[...]], axis=0))
        valid = _band_mask(i)
        for p in range(D_ATTN // LANES):
            qp = q_ref[:, p * LANES : (p + 1) * LANES]
            group = p // 2
            out = jnp.zeros((CHUNK, LANES), F32)
            for side in range(2):
                s = jnp.where(valid, _dot(qp, k_var[group][side], NT) * SCALE, NEG_INF)
                probs, _ = _softmax_with_sink(s, sink_ref[2 * p + side])
                out = out + _dot(probs.astype(BF16), v_var[group][side])
            cat_ref[:, D_GMLP + p * LANES : D_GMLP + (p + 1) * LANES] = out.astype(BF16)

    return pl.pallas_call(
        body,
        name="mixer_fwd",
        grid=(t // CHUNK,),
        in_specs=_chunk_specs()
        + [
            _const_spec((1, D_GMLP)),
            _const_spec((1, D_GMLP)),
            _const_spec((N_HEADS, CHUNK, CHUNK)),
            _const_spec((CHUNK, D_GMLP)),
            pl.BlockSpec(memory_space=pltpu.SMEM),
        ],
        out_specs=pl.BlockSpec((CHUNK, D_MODEL), lambda i: (i, 0)),
        out_shape=jax.ShapeDtypeStruct((t, D_MODEL), BF16),
        compiler_params=_params(("parallel",)),
    )(u, vg, q, k, k, va, va, v_ln_g, v_ln_b, w_spatial, bias_full, sinks)


def _out_proj_ln1(cat, x, w_out):
    t = x.shape[0]

    def body(cat_ref, x_ref, w_ref, xhat_ref, rstd_ref):
        z = ALPHA * x_ref[...] + _dot(cat_ref[...], w_ref[...])
        xhat, rstd = _layer_norm_stats(z)
        xhat_ref[...] = xhat
        rstd_ref[...] = rstd

    return pl.pallas_call(
        body,
        name="out_proj_ln1",
        grid=(t // TM,),
        in_specs=[_row_spec(TM, D_MODEL), _row_spec(TM, D_MODEL), _const_spec((D_MODEL, D_MODEL))],
        out_specs=[_row_spec(TM, D_MODEL), _row_spec(TM, 1)],
        out_shape=[jax.ShapeDtypeStruct((t, D_MODEL), F32), jax.ShapeDtypeStruct((t, 1), F32)],
        compiler_params=_params(("parallel",)),
    )(cat, x, w_out)


def _ffn_fwd_loss(xhat1, ln1_g, ln1_b, w1, w2, ln2_g, ln2_b, target):
    t = xhat1.shape[0]

    def body(xh_ref, g1_ref, b1_ref, w1_ref, w2_ref, g2_ref, b2_ref, tgt_ref, r_ref, dz2_ref, dg2_ref, db2_ref, sq_ref):
        @pl.when(pl.program_id(0) == 0)
        def _():
            dg2_ref[...] = jnp.zeros_like(dg2_ref)
            db2_ref[...] = jnp.zeros_like(db2_ref)
            sq_ref[...] = jnp.zeros_like(sq_ref)

        x1 = xh_ref[...] * g1_ref[...] + b1_ref[...]
        x1b = x1.astype(BF16)
        ff = jnp.zeros((TM_FFN, D_MODEL), F32)
        for j in range(N_FF_BLOCKS):
            r = jnp.maximum(_dot(x1b, w1_ref[j]), 0.0)
            r_ref[:, j * D_MODEL : (j + 1) * D_MODEL] = r.astype(BF16)
            ff = ff + _dot((r * r).astype(BF16), w2_ref[j])
        xhat2, rstd2 = _layer_norm_stats(ALPHA * x1 + ff)
        err = xhat2 * g2_ref[...] + b2_ref[...] - tgt_ref[...]
        sq_ref[...] += jnp.sum(err * err, axis=0, keepdims=True)
        dy = err * (1.0 / D_MODEL)
        dg2_ref[...] += jnp.sum(dy * xhat2, axis=0, keepdims=True)
        db2_ref[...] += jnp.sum(dy, axis=0, keepdims=True)
        dz2_ref[...] = _layer_norm_bwd(dy * g2_ref[...], xhat2, rstd2)

    vec = _const_spec((1, D_MODEL))
    wspec = _const_spec((N_FF_BLOCKS, D_MODEL, D_MODEL), single_buffer=True)
    return pl.pallas_call(
        body,
        name="ffn_fwd_loss",
        grid=(t // TM_FFN,),
        in_specs=[_row_spec(TM_FFN, D_MODEL), vec, vec, wspec, wspec, vec, vec, _row_spec(TM_FFN, D_MODEL)],
        out_specs=[_row_spec(TM_FFN, D_FF), _row_spec(TM_FFN, D_MODEL), vec, vec, vec],
        out_shape=[
            jax.ShapeDtypeStruct((t, D_FF), BF16),
            jax.ShapeDtypeStruct((t, D_MODEL), F32),
            jax.ShapeDtypeStruct((1, D_MODEL), F32),
            jax.ShapeDtypeStruct((1, D_MODEL), F32),
            jax.ShapeDtypeStruct((1, D_MODEL), F32),
        ],
        compiler_params=_params(("arbitrary",)),
    )(xhat1, ln1_g, ln1_b, w1, w2, ln2_g, ln2_b, target)


def _ffn_bwd_ln1(dz2, r, xhat1, rstd1, ln1_g, w1, w2):
    t = dz2.shape[0]

    def body(dz2_ref, r_ref, xh_ref, rstd_ref, g1_ref, w1_ref, w2_ref, dpre_ref, dz1_ref, dg1_ref, db1_ref):
        @pl.when(pl.program_id(0) == 0)
        def _():
            dg1_ref[...] = jnp.zeros_like(dg1_ref)
            db1_ref[...] = jnp.zeros_like(db1_ref)

        dz2 = dz2_ref[...]
        dz2b = dz2.astype(BF16)
        dx1 = ALPHA * dz2
        for j in range(N_FF_BLOCKS):
            cols = slice(j * D_MODEL, (j + 1) * D_MODEL)
            dpre = (_dot(dz2b, w2_ref[j], NT) * (2.0 * r_ref[:, cols].astype(F32))).astype(BF16)
            dpre_ref[:, cols] = dpre
            dx1 = dx1 + _dot(dpre, w1_ref[j], NT)
        xhat1 = xh_ref[...]
        dg1_ref[...] += jnp.sum(dx1 * xhat1, axis=0, keepdims=True)
        db1_ref[...] += jnp.sum(dx1, axis=0, keepdims=True)
        dz1_ref[...] = _layer_norm_bwd(dx1 * g1_ref[...], xhat1, rstd_ref[...])

    vec = _const_spec((1, D_MODEL))
    wspec = _const_spec((N_FF_BLOCKS, D_MODEL, D_MODEL), single_buffer=True)
    return pl.pallas_call(
        body,
        name="ffn_bwd_ln1",
        grid=(t // TM_FFN,),
        in_specs=[_row_spec(TM_FFN, D_MODEL), _row_spec(TM_FFN, D_FF), _row_spec(TM_FFN, D_MODEL), _row_spec(TM_FFN, 1), vec, wspec, wspec],
        out_specs=[_row_spec(TM_FFN, D_FF), _row_spec(TM_FFN, D_MODEL), vec, vec],
        out_shape=[
            jax.ShapeDtypeStruct((t, D_FF), BF16),
            jax.ShapeDtypeStruct((t, D_MODEL), F32),
            jax.ShapeDtypeStruct((1, D_MODEL), F32),
            jax.ShapeDtypeStruct((1, D_MODEL), F32),
        ],
        compiler_params=_params(("arbitrary",)),
    )(dz2, r, xhat1, rstd1, ln1_g, w1, w2)


def _dcat(dz1, w_out):
    t = dz1.shape[0]

    def body(dz1_ref, w_ref, dcat_ref):
        dcat_ref[...] = _dot(dz1_ref[...].astype(BF16), w_ref[...], NT)

    return pl.pallas_call(
        body,
        name="dcat",
        grid=(t // TM,),
        in_specs=[_row_spec(TM, D_MODEL), _const_spec((D_MODEL, D_MODEL))],
        out_specs=_row_spec(TM, D_MODEL),
        out_shape=jax.ShapeDtypeStruct((t, D_MODEL), F32),
        compiler_params=_params(("parallel",)),
    )(dz1, w_out)


def _mixer_bwd(u, vg, q, k, va, dcat, cos, sin, v_ln_g, v_ln_b, w_spatial, bias_full, sinks):
    t = u.shape[0]
    n_chunks = t // CHUNK

    def body(u_ref, vg_ref, q_ref, kc_ref, kp_ref, vc_ref, vp_ref, dcat_ref, cosc_ref, sinc_ref, cosp_ref, sinp_ref,
             g_ref, b_ref, w_ref, bias_ref, sink_ref,
             dmain_ref, dkv_ref, dg_ref, db_ref, dw_ref, dbs_ref, dsink_ref, dmix_acc):
        i = pl.program_id(0)
        left = _half_lane_masks(CHUNK)
        lane = lax.broadcasted_iota(jnp.int32, (CHUNK, LANES), 1)

        @pl.when(i == 0)
        def _():
            dg_ref[...] = jnp.zeros_like(dg_ref)
            db_ref[...] = jnp.zeros_like(db_ref)
            dw_ref[...] = jnp.zeros_like(dw_ref)
            dsink_ref[...] = jnp.zeros_like(dsink_ref)
            dmix_acc[...] = jnp.zeros_like(dmix_acc)

        ug, dug_du = _gelu_and_grad(u_ref[...])
        gv, dgv_dv = _gelu_and_grad(vg_ref[...])
        xhat, rstd = _layer_norm_stats(gv)
        gain = g_ref[...]
        vgl = xhat * gain + b_ref[...]
        dvgl_parts = []
        for p in range(D_GMLP // LANES):
            cols = slice(p * LANES, (p + 1) * LANES)
            xp = vgl[:, cols]
            xl = jnp.where(left, xp, 0.0).astype(BF16)
            xr = jnp.where(left, 0.0, xp).astype(BF16)
            wl, wr, causal = _spatial_weights(w_ref, p)
            mixed = _dot(wl, xl) + _dot(wr, xr) + bias_ref[:, cols]
            da = dcat_ref[:, cols]
            dmain_ref[:, cols] = (da * mixed * dug_du[:, cols]).astype(BF16)
            dmixed = da * ug[:, cols]
            dmix_acc[:, cols] += dmixed
            dml = jnp.where(left, dmixed, 0.0).astype(BF16)
            dmr = jnp.where(left, 0.0, dmixed).astype(BF16)
            xpb = xp.astype(BF16)
            dw_ref[2 * p] += jnp.where(causal, _dot(dml, xpb, NT), 0.0)
            dw_ref[2 * p + 1] += jnp.where(causal, _dot(dmr, xpb, NT), 0.0)
            dvgl_parts.append(_dot(wl, dml, TN) + _dot(wr, dmr, TN))
        dvgl = jnp.concatenate(dvgl_parts, axis=1)
        dg_ref[...] += jnp.sum(dvgl * xhat, axis=0, keepdims=True)
        db_ref[...] += jnp.sum(dvgl, axis=0, keepdims=True)
        dgv = _layer_norm_bwd(dvgl * gain, xhat, rstd)
        dmain_ref[:, D_GMLP : 2 * D_GMLP] = (dgv * dgv_dv).astype(BF16)

        @pl.when(i == n_chunks - 1)
        def _():
            tile = jnp.zeros((CHUNK, LANES), F32)
            for p in range(D_GMLP // LANES):
                dm = dmix_acc[:, p * LANES : (p + 1) * LANES]
                sl = jnp.sum(jnp.where(left, dm, 0.0), axis=1, keepdims=True)
                sr = jnp.sum(jnp.where(left, 0.0, dm), axis=1, keepdims=True)
                tile = jnp.where(lane == 2 * p, sl, tile)
                tile = jnp.where(lane == 2 * p + 1, sr, tile)
            dbs_ref[...] = tile

        k_var = _kv_variants(jnp.concatenate([kp_ref[...], kc_ref[...]], axis=0))
        v_var = _kv_variants(jnp.concatenate([vp_ref[...], vc_ref[...]], axis=0))
        valid = _band_mask(i)
        dk_acc = [[None, None], [None, None]]
        dv_acc = [[None, None], [None, None]]
        dsink_row = jnp.zeros((1, LANES), F32)
        lane_row = lax.broadcasted_iota(jnp.int32, (1, LANES), 1)
        dq_parts = []
        for p in range(D_ATTN // LANES):
            qp = q_ref[:, p * LANES : (p + 1) * LANES]
            do = dcat_ref[:, D_GMLP + p * LANES : D_GMLP + (p + 1) * LANES]
            dob = do.astype(BF16)
            group = p // 2
            dq = jnp.zeros((CHUNK, LANES), F32)
            for side in range(2):
                head = 2 * p + side
                kx, vx = k_var[group][side], v_var[group][side]
                s = jnp.where(valid, _dot(qp, kx, NT) * SCALE, NEG_INF)
                probs, p_sink = _softmax_with_sink(s, sink_ref[head])
                dp = _dot(dob, vx, NT)
                dsum = jnp.sum(probs * dp, axis=1, keepdims=True)
                ds = (probs * (dp - dsum) * SCALE).astype(BF16)
                dsink_row = dsink_row + jnp.where(lane_row == head, -jnp.sum(p_sink * dsum, axis=0, keepdims=True), 0.0)
                dq = dq + _dot(ds, kx)
                in_side = left if side == 0 else jnp.logical_not(left)
                q_side = jnp.where(in_side, qp.astype(F32), 0.0).astype(BF16)
                do_side = jnp.where(in_side, do, 0.0).astype(BF16)
                dk_h = _dot(ds, q_side, TN)
                dv_h = _dot(probs.astype(BF16), do_side, TN)
                dk_acc[group][side] = dk_h if dk_acc[group][side] is None else dk_acc[group][side] + dk_h
                dv_acc[group][side] = dv_h if dv_acc[group][side] is None else dv_acc[group][side] + dv_h
            dq_parts.append(dq)
        dsink_ref[0:1, :] += dsink_row
        cos_c, sin_c = cosc_ref[...], sinc_ref[...]
        dq_all = jnp.concatenate(dq_parts, axis=1)
        reps = D_ATTN // LANES
        dmain_ref[:, 2 * D_GMLP : D_MAIN] = _rope_transposed(dq_all, _lane_tile(cos_c, reps), _lane_tile(sin_c, reps)).astype(BF16)

        def group_lanes(acc):
            return acc[0][0] + pltpu.roll(acc[0][1], HEAD_DIM, 1) + pltpu.roll(acc[1][0], HEAD_DIM, 1) + acc[1][1]

        dk2 = group_lanes(dk_acc)
        dv2 = group_lanes(dv_acc)
        cur = pl.ds(pl.multiple_of(i * CHUNK, CHUNK), CHUNK)
        dkv_ref[cur, 0:D_KV] = _rope_transposed(dk2[CHUNK:], cos_c, sin_c)
        dkv_ref[cur, D_KV : 2 * D_KV] = dv2[CHUNK:]

        @pl.when(i > 0)
        def _():
            prev = pl.ds(pl.multiple_of((i - 1) * CHUNK, CHUNK), CHUNK)
            dkv_ref[prev, 0:D_KV] += _rope_transposed(dk2[:CHUNK], cosp_ref[...], sinp_ref[...])
            dkv_ref[prev, D_KV : 2 * D_KV] += dv2[:CHUNK]

    cur = lambda i: (i, 0)
    prev = lambda i: (jnp.maximum(i - 1, 0), 0)
    return pl.pallas_call(
        body,
        name="mixer_bwd",
        grid=(n_chunks,),
        in_specs=_chunk_specs()
        + [
            pl.BlockSpec((CHUNK, D_MODEL), cur),
            pl.BlockSpec((CHUNK, LANES), cur),
            pl.BlockSpec((CHUNK, LANES), cur),
            pl.BlockSpec((CHUNK, LANES), prev),
            pl.BlockSpec((CHUNK, LANES), prev),
            _const_spec((1, D_GMLP)),
            _const_spec((1, D_GMLP)),
            _const_spec((N_HEADS, CHUNK, CHUNK)),
            _const_spec((CHUNK, D_GMLP)),
            pl.BlockSpec(memory_space=pltpu.SMEM),
        ],
        out_specs=[
            pl.BlockSpec((CHUNK, D_MAIN), cur),
            _const_spec((t, 2 * D_KV)),
            _const_spec((1, D_GMLP)),
            _const_spec((1, D_GMLP)),
            _const_spec((N_HEADS, CHUNK, CHUNK)),
            _const_spec((CHUNK, LANES)),
            _const_spec((8, LANES)),
        ],
        out_shape=[
            jax.ShapeDtypeStruct((t, D_MAIN), BF16),
            jax.ShapeDtypeStruct((t, 2 * D_KV), F32),
            jax.ShapeDtypeStruct((1, D_GMLP), F32),
            jax.ShapeDtypeStruct((1, D_GMLP), F32),
            jax.ShapeDtypeStruct((N_HEADS, CHUNK, CHUNK), F32),
            jax.ShapeDtypeStruct((CHUNK, LANES), F32),
            jax.ShapeDtypeStruct((8, LANES), F32),
        ],
        scratch_shapes=[pltpu.VMEM((CHUNK, D_GMLP), F32)],
        compiler_params=_params(("arbitrary",)),
    )(u, vg, q, k, k, va, va, dcat, cos, sin, cos, sin, v_ln_g, v_ln_b, w_spatial, bias_full, sinks)


def _grad_x(dh_main, dkv, dz1, w_in_t):
    t = dz1.shape[0]

    def body(dm_ref, dkv_ref, dz1_ref, w_ref, gx_ref):
        acc = ALPHA * dz1_ref[...] + _dot(dm_ref[...], w_ref[0:D_MAIN, :])
        gx_ref[...] = acc + _dot(dkv_ref[...].astype(BF16), w_ref[D_MAIN:D_IN, :])

    return pl.pallas_call(
        body,
        name="grad_x",
        grid=(t // TM,),
        in_specs=[_row_spec(TM, D_MAIN), _row_spec(TM, 2 * D_KV), _row_spec(TM, D_MODEL), _const_spec((D_IN, D_MODEL))],
        out_specs=_row_spec(TM, D_MODEL),
        out_shape=jax.ShapeDtypeStruct((t, D_MODEL), F32),
        compiler_params=_params(("parallel",)),
    )(dh_main, dkv, dz1, w_in_t)


def _token_contraction(name, n_blocks, out_rows, in_arrays, in_specs, contributions):
    t = in_arrays[0].shape[0]
    block_rows = out_rows // n_blocks

    def body(*refs):
        out_ref = refs[-1]

        @pl.when(pl.program_id(1) == 0)
        def _():
            out_ref[...] = jnp.zeros_like(out_ref)

        for row0, a, b in contributions(*refs[:-1]):
            out_ref[row0 : row0 + a.shape[1], :] += _dot(a, b, TN)

    return pl.pallas_call(
        body,
        name=name,
        grid=(n_blocks, t // TK),
        in_specs=in_specs,
        out_specs=pl.BlockSpec((block_rows, D_MODEL), lambda j, k: (j, 0)),
        out_shape=jax.ShapeDtypeStruct((out_rows, D_MODEL), F32),
        compiler_params=_params(("parallel", "arbitrary")),
    )(*in_arrays)


def _tile_spec(cols):
    return pl.BlockSpec((TK, cols), lambda j, k: (k, 0))


def _tile_block_spec():
    return pl.BlockSpec((TK, D_MODEL), lambda j, k: (k, j))


def _grad_w_in_t(dh_main, dkv, x):
    def contributions(dm_ref, dkv_ref, x_ref):
        xb = x_ref[...].astype(BF16)
        return [(0, dm_ref[...], xb), (D_MAIN, dkv_ref[...].astype(BF16), xb)]

    return _token_contraction("grad_w_in", 1, D_IN, [dh_main, dkv, x], [_tile_spec(D_MAIN), _tile_spec(2 * D_KV), _tile_spec(D_MODEL)], contributions)


def _grad_w_out(cat, dz1):
    def contributions(cat_ref, dz1_ref):
        return [(0, cat_ref[...], dz1_ref[...].astype(BF16))]

    return _token_contraction("grad_w_out", 1, D_MODEL, [cat, dz1], [_tile_spec(D_MODEL), _tile_spec(D_MODEL)], contributions)


def _grad_w_ff1(xhat1, ln1_g, ln1_b, dpre):
    def contributions(xh_ref, g_ref, b_ref, dpre_ref):
        return [(0, (xh_ref[...] * g_ref[...] + b_ref[...]).astype(BF16), dpre_ref[...])]

    vec = pl.BlockSpec((1, D_MODEL), lambda j, k: (0, 0))
    return _token_contraction("grad_w_ff1", N_FF_BLOCKS, D_FF, [xhat1, ln1_g, ln1_b, dpre], [_tile_spec(D_MODEL), vec, vec, _tile_block_spec()], contributions)


def _grad_w_ff2(r, dz2):
    def contributions(r_ref, dz2_ref):
        rf = r_ref[...].astype(F32)
        return [(0, (rf * rf).astype(BF16), dz2_ref[...].astype(BF16))]

    return _token_contraction("grad_w_ff2", N_FF_BLOCKS, D_FF, [r, dz2], [_tile_block_spec(), _tile_spec(D_MODEL)], contributions)


ANY = pl.BlockSpec(memory_space=pl.ANY)


def _mesh_position():
    return lax.axis_index("x"), lax.axis_index("y"), lax.axis_index("c")


def _other_chips(x, y):
    return [(1 - x, y), (x, 1 - y), (1 - x, 1 - y)]


def _remote(src, dst, send_sem, recv_sem, device):
    return pltpu.make_async_remote_copy(src_ref=src, dst_ref=dst, send_sem=send_sem, recv_sem=recv_sem, device_id=device, device_id_type=MESH)


def _rows(ref, start, size):
    return ref.at[pl.ds(start, size), :]


def _all_gather_weights(shards):
    n = len(shards)
    per = 7

    def body(*refs):
        ins, outs = refs[:n], refs[n : 2 * n]
        send_sems, recv_sems = refs[2 * n :]
        x, y, c = _mesh_position()
        me = 2 * x + y
        chips = _other_chips(x, y)
        sibling = (x, y, 1 - c)
        started = []
        for w in range(n):
            rows = shards[w].shape[0]
            half = rows // 2
            for kk, (px, py) in enumerate(chips):
                cp = _remote(_rows(ins[w], c * half, half), _rows(outs[w], me * rows + c * half, half),
                             send_sems.at[per * w + kk], recv_sems.at[per * w + kk], (px, py, c))
                cp.start()
                started.append(cp)
            cp = _remote(ins[w], _rows(outs[w], me * rows, rows), send_sems.at[per * w + 6], recv_sems.at[per * w + 6], sibling)
            cp.start()
            started.append(cp)
        for w in range(n):
            rows = shards[w].shape[0]
            half = rows // 2
            for kk, (px, py) in enumerate(chips):
                blk = _rows(outs[w], (2 * px + py) * rows + c * half, half)
                _remote(blk, blk, send_sems.at[per * w + kk], recv_sems.at[per * w + kk], (px, py, c)).wait_recv()
                fwd = _remote(blk, blk, send_sems.at[per * w + 3 + kk], recv_sems.at[per * w + 3 + kk], sibling)
                fwd.start()
                started.append(fwd)
        for w in range(n):
            rows = shards[w].shape[0]
            half = rows // 2
            for kk, (px, py) in enumerate(chips):
                blk = _rows(outs[w], (2 * px + py) * rows + (1 - c) * half, half)
                _remote(blk, blk, send_sems.at[per * w + 3 + kk], recv_sems.at[per * w + 3 + kk], sibling).wait_recv()
            own = _rows(outs[w], me * rows, rows)
            _remote(own, own, send_sems.at[per * w + 6], recv_sems.at[per * w + 6], sibling).wait_recv()
        for cp in started:
            cp.wait_send()

    return pl.pallas_call(
        body,
        name="all_gather_weights",
        in_specs=[ANY] * n,
        out_specs=[ANY] * n,
        out_shape=[jax.ShapeDtypeStruct((N_CHIPS * s.shape[0], s.shape[1]), s.dtype) for s in shards],
        scratch_shapes=[pltpu.SemaphoreType.DMA((per * n,)), pltpu.SemaphoreType.DMA((per * n,))],
    )(*shards)


def _pair_swap(grads):
    n = len(grads)

    def body(*refs):
        ins, theirs = refs[:n], refs[n : 2 * n]
        send_sems, recv_sems = refs[2 * n :]
        x, y, c = _mesh_position()
        sibling = (x, y, 1 - c)
        sends = []
        for w in range(n):
            rows = grads[w].shape[0] // N_CHIPS
            half = rows // 2
            for j in range(N_CHIPS):
                cp = _remote(_rows(ins[w], j * rows + (1 - c) * half, half), _rows(theirs[w], j * half, half),
                             send_sems.at[4 * w + j], recv_sems.at[4 * w + j], sibling)
                cp.start()
                sends.append(cp)
        for cp in sends:
            cp.wait_recv()
        for cp in sends:
            cp.wait_send()

    return pl.pallas_call(
        body,
        name="grad_pair_swap",
        in_specs=[ANY] * n,
        out_specs=[ANY] * n,
        out_shape=[jax.ShapeDtypeStruct((g.shape[0] // 2, g.shape[1]), g.dtype) for g in grads],
        scratch_shapes=[pltpu.SemaphoreType.DMA((4 * n,)), pltpu.SemaphoreType.DMA((4 * n,))],
    )(*grads)


def _chip_exchange(partials):
    n = len(partials)

    def body(*refs):
        ins, outs = refs[:n], refs[n : 2 * n]
        send_sems, recv_sems = refs[2 * n :]
        x, y, c = _mesh_position()
        chips = _other_chips(x, y)
        sends = []
        for w in range(n):
            half = partials[w].shape[0] // N_CHIPS
            for kk, (px, py) in enumerate(chips):
                cp = _remote(_rows(ins[w], (2 * px + py) * half, half), _rows(outs[w], kk * half, half),
                             send_sems.at[3 * w + kk], recv_sems.at[3 * w + kk], (px, py, c))
                cp.start()
                sends.append(cp)
        for cp in sends:
            cp.wait_recv()
        for cp in sends:
            cp.wait_send()

    return pl.pallas_call(
        body,
        name="grad_chip_exchange",
        in_specs=[ANY] * n,
        out_specs=[ANY] * n,
        out_shape=[jax.ShapeDtypeStruct((3 * p.shape[0] // N_CHIPS, p.shape[1]), p.dtype) for p in partials],
        scratch_shapes=[pltpu.SemaphoreType.DMA((3 * n,)), pltpu.SemaphoreType.DMA((3 * n,))],
    )(*partials)


def _pair_gather(shards):
    n = len(shards)

    def body(*refs):
        outs = refs[n : 2 * n]
        send_sems, recv_sems = refs[2 * n :]
        x, y, c = _mesh_position()
        sibling = (x, y, 1 - c)
        sends = []
        for w in range(n):
            half = shards[w].shape[0] // 2
            mine = _rows(outs[w], c * half, half)
            cp = _remote(mine, mine, send_sems.at[w], recv_sems.at[w], sibling)
            cp.start()
            sends.append(cp)
        for w in range(n):
            half = shards[w].shape[0] // 2
            blk = _rows(outs[w], (1 - c) * half, half)
            _remote(blk, blk, send_sems.at[w], recv_sems.at[w], sibling).wait_recv()
        for cp in sends:
            cp.wait_send()

    return pl.pallas_call(
        body,
        name="grad_pair_gather",
        in_specs=[ANY] * n,
        out_specs=[ANY] * n,
        out_shape=[jax.ShapeDtypeStruct(s.shape, s.dtype) for s in shards],
        input_output_aliases={w: w for w in range(n)},
        scratch_shapes=[pltpu.SemaphoreType.DMA((n,)), pltpu.SemaphoreType.DMA((n,))],
    )(*shards)


def _all_reduce_small(slab):
    rows = slab.shape[0]
    part = rows // 8

    def body(slab_ref, out_ref, landing, reduced, send_sems, recv_sems):
        x, y, c = _mesh_position()
        me = 4 * x + 2 * y + c
        flips = [(k >> 2, (k >> 1) & 1, k & 1) for k in range(1, 8)]

        def peer(flip):
            fx, fy, fc = flip
            return (1 - x if fx else x, 1 - y if fy else y, 1 - c if fc else c)

        def my_rows(ref):
            return ref.at[pl.ds(pl.multiple_of(me * part, 8), part), :]

        sends = []
        for kk, flip in enumerate(flips):
            px, py, pc = peer(flip)
            them = 4 * px + 2 * py + pc
            cp = _remote(slab_ref.at[pl.ds(pl.multiple_of(them * part, 8), part), :], landing.at[me], send_sems.at[kk], recv_sems.at[kk], (px, py, pc))
            cp.start()
            sends.append(cp)
        landing[me] = my_rows(slab_ref)[...]
        for kk, flip in enumerate(flips):
            px, py, pc = peer(flip)
            them = 4 * px + 2 * py + pc
            _remote(landing.at[them], landing.at[them], send_sems.at[kk], recv_sems.at[kk], (px, py, pc)).wait_recv()
        total = landing[0]
        for s in range(1, 8):
            total = total + landing[s]
        reduced[...] = total
        my_rows(out_ref)[...] = total
        for kk, flip in enumerate(flips):
            cp = _remote(reduced, my_rows(out_ref), send_sems.at[7 + kk], recv_sems.at[7 + kk], peer(flip))
            cp.start()
            sends.append(cp)
        for kk, flip in enumerate(flips):
            px, py, pc = peer(flip)
            them = 4 * px + 2 * py + pc
            blk = out_ref.at[pl.ds(pl.multiple_of(them * part, 8), part), :]
            _remote(blk, blk, send_sems.at[7 + kk], recv_sems.at[7 + kk], (px, py, pc)).wait_recv()
        for cp in sends:
            cp.wait_send()

    vmem = pl.BlockSpec(memory_space=pltpu.VMEM)
    return pl.pallas_call(
        body,
        name="all_reduce_small",
        in_specs=[vmem],
        out_specs=vmem,
        out_shape=jax.ShapeDtypeStruct(slab.shape, slab.dtype),
        scratch_shapes=[pltpu.VMEM((8, part, LANES), F32), pltpu.VMEM((part, LANES), F32), pltpu.SemaphoreType.DMA((14,)), pltpu.SemaphoreType.DMA((14,))],
    )(slab)


def _row_tile(rows, cap=256):
    best = 8
    for cand in range(8, cap + 1, 8):
        if rows % cand == 0:
            best = cand
    return best


def _pair_sum(name, grad, theirs, pos):
    half = theirs.shape[0] // N_CHIPS
    cols = theirs.shape[1]
    tile = _row_tile(half)
    steps = half // tile

    def body(pos_ref, g_ref, t_ref, p_ref, own_ref):
        total = g_ref[...] + t_ref[...]
        p_ref[...] = total.astype(BF16)

        @pl.when(pl.program_id(1) == pos_ref[1])
        def _():
            own_ref[...] = total

    return pl.pallas_call(
        body,
        name=name,
        grid_spec=pltpu.PrefetchScalarGridSpec(
            num_scalar_prefetch=1,
            grid=(steps, N_CHIPS),
            in_specs=[
                pl.BlockSpec((tile, cols), lambda i, j, pos: ((2 * j + pos[0]) * steps + i, 0)),
                pl.BlockSpec((tile, cols), lambda i, j, pos: (j * steps + i, 0)),
            ],
            out_specs=[
                pl.BlockSpec((tile, cols), lambda i, j, pos: (j * steps + i, 0)),
                pl.BlockSpec((tile, cols), lambda i, j, pos: (i, 0)),
            ],
        ),
        out_shape=[jax.ShapeDtypeStruct((N_CHIPS * half, cols), BF16), jax.ShapeDtypeStruct((half, cols), F32)],
        compiler_params=_params(("parallel", "arbitrary")),
    )(pos, grad, theirs)


def _chip_sum(name, own, landed, pos):
    half, cols = own.shape
    tile = _row_tile(half)
    steps = half // tile

    def body(pos_ref, own_ref, l0, l1, l2, o_ref):
        o_ref[...] = ((own_ref[...] + l0[...].astype(F32)) + l1[...].astype(F32)) + l2[...].astype(F32)

    landed_specs = [pl.BlockSpec((tile, cols), lambda i, pos, _k=k: (_k * steps + i, 0)) for k in range(N_CHIPS - 1)]
    return pl.pallas_call(
        body,
        name=name,
        grid_spec=pltpu.PrefetchScalarGridSpec(
            num_scalar_prefetch=1,
            grid=(steps,),
            in_specs=[pl.BlockSpec((tile, cols), lambda i, pos: (i, 0))] + landed_specs,
            out_specs=pl.BlockSpec((tile, cols), lambda i, pos: (pos[0] * steps + i, 0)),
        ),
        out_shape=jax.ShapeDtypeStruct((2 * half, cols), F32),
        compiler_params=_params(("parallel",)),
    )(pos, own, landed, landed, landed)


def _adamw(name, w, g, m, v):
    rows, cols = w.shape
    tile = rows if rows * cols <= 256 * 1024 else _row_tile(rows)

    def body(w_ref, g_ref, m_ref, v_ref, d_ref, nm_ref, nv_ref):
        g = g_ref[...]
        nm = ADAM_B1 * m_ref[...] + (1.0 - ADAM_B1) * g
        nv = ADAM_B2 * v_ref[...] + (1.0 - ADAM_B2) * (g * g)
        m_hat = nm / (1.0 - ADAM_B1**ADAM_STEP)
        v_hat = nv / (1.0 - ADAM_B2**ADAM_STEP)
        d_ref[...] = -ADAM_LR * (m_hat / (jnp.sqrt(v_hat) + ADAM_EPS) + ADAM_WD * w_ref[...])
        nm_ref[...] = nm
        nv_ref[...] = nv

    spec = _row_spec(tile, cols)
    return pl.pallas_call(
        body,
        name=name,
        grid=(rows // tile,),
        in_specs=[spec] * 4,
        out_specs=[spec] * 3,
        out_shape=[jax.ShapeDtypeStruct((rows, cols), F32)] * 3,
        compiler_params=_params(("parallel",)),
    )(w, g, m, v)


_SMALL = (
    ("v_ln_g", (D_GMLP,), 8),
    ("v_ln_b", (D_GMLP,), 8),
    ("w_spatial", (N_HEADS, CHUNK, CHUNK), 1024),
    ("b_spatial", (N_HEADS, CHUNK), 8),
    ("sinks", (N_HEADS,), 8),
    ("ln1_g", (D_MODEL,), 8),
    ("ln1_b", (D_MODEL,), 8),
    ("ln2_g", (D_MODEL,), 8),
    ("ln2_b", (D_MODEL,), 8),
)


def _pack_small(values):
    parts = []
    for (name, shape, rows), val in zip(_SMALL, values, strict=True):
        flat = val.reshape(-1).astype(F32)
        parts.append(jnp.pad(flat, (0, rows * LANES - flat.shape[0])).reshape(rows, LANES))
    return jnp.concatenate(parts, axis=0)


def _unpack_small(slab):
    out, row = [], 0
    for name, shape, rows in _SMALL:
        size = math.prod(shape)
        out.append(slab[row : row + rows].reshape(-1)[:size].reshape((1,) + shape))
        row += rows
    return out


def kernel(x, positions, w_in, v_ln_g, v_ln_b, w_spatial, b_spatial, sinks, w_out, ln1_g, ln1_b, w_ff1, w_ff2, ln2_g, ln2_b, loss_target, m_w_in, m_v_ln_g, m_v_ln_b, m_w_spatial, m_b_spatial, m_sinks, m_w_out, m_ln1_g, m_ln1_b, m_w_ff1, m_w_ff2, m_ln2_g, m_ln2_b, v_w_in, v_v_ln_g, v_v_ln_b, v_w_spatial, v_b_spatial, v_sinks, v_w_out, v_ln1_g, v_ln1_b, v_w_ff1, v_w_ff2, v_ln2_g, v_ln2_b):
    t = x.shape[1]
    x2 = x.reshape(t, D_MODEL)
    target = loss_target.reshape(t, D_MODEL)

    shards = [w_in[0].T.astype(BF16), w_out[0].astype(BF16), w_ff1[0].astype(BF16), w_ff2[0].astype(BF16)]
    w_in_t, w_out_all, w1_all, w2_all = _all_gather_weights(shards)
    w1_all = w1_all.reshape(N_FF_BLOCKS, D_MODEL, D_MODEL)
    w2_all = w2_all.reshape(N_FF_BLOCKS, D_MODEL, D_MODEL)

    inv_freq = ROPE_THETA ** (-jnp.arange(0, HEAD_DIM, 2, dtype=F32) / HEAD_DIM)
    cos, sin = _rope_tables(positions.reshape(t, 1), jnp.tile(inv_freq, LANES // (HEAD_DIM // 2)).reshape(1, LANES))
    u, vg, q, k, va = _in_proj(x2, w_in_t, cos, sin)
    bias_full = jnp.repeat(b_spatial[0].T, HEAD_DIM, axis=1)
    sink_vec = sinks.reshape(N_HEADS)
    cat = _mixer_fwd(u, vg, q, k, va, v_ln_g, v_ln_b, w_spatial[0], bias_full, sink_vec)
    xhat1, rstd1 = _out_proj_ln1(cat, x2, w_out_all)
    r, dz2, d_ln2_g, d_ln2_b, sq_err = _ffn_fwd_loss(xhat1, ln1_g, ln1_b, w1_all, w2_all, ln2_g, ln2_b, target)
    loss = lax.psum(0.5 * jnp.sum(sq_err) / D_MODEL, ("x", "y", "c"))

    dpre, dz1, d_ln1_g, d_ln1_b = _ffn_bwd_ln1(dz2, r, xhat1, rstd1, ln1_g, w1_all, w2_all)
    dcat = _dcat(dz1, w_out_all)
    dh_main, dkv, d_v_ln_g, d_v_ln_b, d_w_spatial, d_b_spatial_t, d_sinks = _mixer_bwd(
        u, vg, q, k, va, dcat, cos, sin, v_ln_g, v_ln_b, w_spatial[0], bias_full, sink_vec)
    grad_x = _grad_x(dh_main, dkv, dz1, w_in_t).reshape(1, t, D_MODEL)
    grads = [_grad_w_in_t(dh_main, dkv, x2), _grad_w_out(cat, dz1), _grad_w_ff1(xhat1, ln1_g, ln1_b, dpre), _grad_w_ff2(r, dz2)]

    names = ["w_in", "w_out", "w_ff1", "w_ff2"]
    pos = jnp.stack([lax.axis_index("c"), 2 * lax.axis_index("x") + lax.axis_index("y")]).astype(jnp.int32)
    theirs = _pair_swap(grads)
    pair_sums = [_pair_sum("grad_pair_sum_" + nm, g, th, pos) for nm, g, th in zip(names, grads, theirs)]
    landed = _chip_exchange([p for p, _ in pair_sums])
    g_w_in_t, g_w_out, g_w_ff1, g_w_ff2 = _pair_gather(
        [_chip_sum("grad_chip_sum_" + nm, own, ld, pos) for nm, (_, own), ld in zip(names, pair_sums, landed)])
    g_w_in = g_w_in_t.T

    small_g = _all_reduce_small(_pack_small(
        [d_v_ln_g, d_v_ln_b, d_w_spatial, d_b_spatial_t[:, :N_HEADS].T, d_sinks[0, :N_HEADS], d_ln1_g, d_ln1_b, d_ln2_g, d_ln2_b]))
    small_grads = _unpack_small(small_g)

    d_w_in, nm_w_in, nv_w_in = _adamw("adamw_w_in", w_in[0], g_w_in, m_w_in[0], v_w_in[0])
    d_w_out, nm_w_out, nv_w_out = _adamw("adamw_w_out", w_out[0], g_w_out, m_w_out[0], v_w_out[0])
    d_w_ff1, nm_w_ff1, nv_w_ff1 = _adamw("adamw_w_ff1", w_ff1[0], g_w_ff1, m_w_ff1[0], v_w_ff1[0])
    d_w_ff2, nm_w_ff2, nv_w_ff2 = _adamw("adamw_w_ff2", w_ff2[0], g_w_ff2, m_w_ff2[0], v_w_ff2[0])
    small_w = _pack_small([v_ln_g, v_ln_b, w_spatial, b_spatial, sinks, ln1_g, ln1_b, ln2_g, ln2_b])
    small_m = _pack_small([m_v_ln_g, m_v_ln_b, m_w_spatial, m_b_spatial, m_sinks, m_ln1_g, m_ln1_b, m_ln2_g, m_ln2_b])
    small_v = _pack_small([v_v_ln_g, v_v_ln_b, v_w_spatial, v_b_spatial, v_sinks, v_ln1_g, v_ln1_b, v_ln2_g, v_ln2_b])
    small_d, small_nm, small_nv = (_unpack_small(s) for s in _adamw("adamw_small", small_w, small_g, small_m, small_v))

    def with_big(small, w_in_v, w_out_v, w_ff1_v, w_ff2_v):
        g_vg, g_vb, g_ws, g_bs, g_sk, g_1g, g_1b, g_2g, g_2b = small
        return [w_in_v[None], g_vg, g_vb, g_ws, g_bs, g_sk, w_out_v[None], g_1g, g_1b, w_ff1_v[None], w_ff2_v[None], g_2g, g_2b]

    return (
        loss,
        grad_x,
        *with_big(small_grads, g_w_in, g_w_out, g_w_ff1, g_w_ff2),
        *with_big(small_d, d_w_in, d_w_out, d_w_ff1, d_w_ff2),
        *with_big(small_nm, nm_w_in, nm_w_out, nm_w_ff1, nm_w_ff2),
        *with_big(small_nv, nv_w_in, nv_w_out, nv_w_ff1, nv_w_ff2),
    )
```

```python
import functools
import math

import jax
import jax.numpy as jnp
from jax import lax
from jax.experimental import pallas as pl
from jax.experimental.pallas import tpu as pltpu

F32 = jnp.float32
BF16 = jnp.bfloat16

D_MODEL = 1024
HEAD_DIM = 64
D_GMLP = 512
D_ATTN = 512
D_KV = 128
D_IN = 2 * D_GMLP + D_ATTN + 2 * D_KV
D_MAIN = 2 * D_GMLP + D_ATTN
N_HEADS = 8
CHUNK = 128
ROPE_THETA = 10000.0
D_FF = 4 * D_MODEL
N_FF_BLOCKS = 4
LN_EPS = 1e-5
ALPHA = (2.0 * 1) ** 0.25
NEG_INF = -1e30
SCALE = 1.0 / math.sqrt(HEAD_DIM)

ADAM_LR = 0.001
ADAM_B1 = 0.9
ADAM_B2 = 0.999
ADAM_EPS = 1e-08
ADAM_WD = 0.01
ADAM_STEP = 10

N_CHIPS = 4
LANES = 128
V7X_VMEM_BYTES = 64 * 1024 * 1024
VMEM_LIMIT = V7X_VMEM_BYTES - 8 * 1024 * 1024
TM = 512
TM_FFN = 256
TK = 512
SMALL_ROWS = 1088
MESH = pl.DeviceIdType.MESH

NT = (((1,), (1,)), ((), ()))
TN = (((0,), (0,)), ((), ()))


def _dot(a, b, dims=None):
    if dims is None:
        return jnp.dot(a, b, preferred_element_type=F32)
    return lax.dot_general(a, b, dims, preferred_element_type=F32)


def _params(semantics=None):
    return pltpu.CompilerParams(dimension_semantics=semantics, vmem_limit_bytes=VMEM_LIMIT)


def _const_spec(shape, single_buffer=False):
    zeros = (0,) * len(shape)
    if single_buffer:
        return pl.BlockSpec(shape, lambda *_: zeros, pipeline_mode=pl.Buffered(1))
    return pl.BlockSpec(shape, lambda *_: zeros)


def _row_spec(rows, cols):
    return pl.BlockSpec((rows, cols), lambda i: (i, 0))


def _after(dep, body, in_specs, operands):
    if dep is None:
        return body, list(in_specs), list(operands)
    return (lambda dep_ref, *refs: body(*refs)), [pl.BlockSpec(memory_space=pl.ANY)] + list(in_specs), [dep] + list(operands)


def _gelu(x):
    k = math.sqrt(2.0 / math.pi)
    return 0.5 * x * (1.0 + jnp.tanh(k * (x + 0.044715 * (x * x * x))))


def _gelu_and_grad(x):
    k = math.sqrt(2.0 / math.pi)
    x2 = x * x
    t = jnp.tanh(k * (x + 0.044715 * (x2 * x)))
    g = 0.5 * x * (1.0 + t)
    dg = 0.5 * (1.0 + t) + 0.5 * x * (1.0 - t * t) * (k * (1.0 + 3.0 * 0.044715 * x2))
    return g, dg


def _layer_norm_stats(z):
    mu = jnp.mean(z, axis=-1, keepdims=True)
    zc = z - mu
    var = jnp.mean(zc * zc, axis=-1, keepdims=True)
    rstd = lax.rsqrt(var + LN_EPS)
    return zc * rstd, rstd


def _layer_norm_bwd(dxhat, xhat, rstd):
    m1 = jnp.mean(dxhat, axis=-1, keepdims=True)
    m2 = jnp.mean(dxhat * xhat, axis=-1, keepdims=True)
    return rstd * (dxhat - m1 - xhat * m2)


def _rotate_half(t):
    n = t.shape[1]
    lane = lax.broadcasted_iota(jnp.int32, t.shape, 1)
    first = (lane & (HEAD_DIM // 2)) == 0
    return jnp.where(first, -pltpu.roll(t, n - HEAD_DIM // 2, 1), pltpu.roll(t, HEAD_DIM // 2, 1))


def _rope(t, cos, sin):
    return t * cos + _rotate_half(t) * sin


def _rope_transposed(g, cos, sin):
    return g * cos - _rotate_half(g * sin)


def _lane_tile(a, reps):
    return jnp.tile(a, (1, reps)) if reps > 1 else a


def _rope_tables(pos_col, inv_freq_row):
    t = pos_col.shape[0]

    def body(pos_ref, f_ref, cos_ref, sin_ref):
        ang = pos_ref[...].astype(F32) * f_ref[...]
        cos_ref[...] = jnp.cos(ang)
        sin_ref[...] = jnp.sin(ang)

    return pl.pallas_call(
        body,
        name="rope_tables",
        grid=(t // TM,),
        in_specs=[_row_spec(TM, 1), _const_spec((1, LANES))],
        out_specs=[_row_spec(TM, LANES), _row_spec(TM, LANES)],
        out_shape=[jax.ShapeDtypeStruct((t, LANES), F32)] * 2,
        compiler_params=_params(("parallel",)),
    )(pos_col, inv_freq_row)


def _in_proj(x, w_in_t, cos, sin, dep=None):
    t = x.shape[0]

    def body(x_ref, w_ref, cos_ref, sin_ref, u_ref, vg_ref, q_ref, k_ref, va_ref):
        xb = x_ref[...].astype(BF16)
        u_ref[...] = _dot(xb, w_ref[0:D_GMLP, :], NT)
        vg_ref[...] = _dot(xb, w_ref[D_GMLP : 2 * D_GMLP, :], NT)
        q = _dot(xb, w_ref[2 * D_GMLP : D_MAIN, :], NT)
        k = _dot(xb, w_ref[D_MAIN : D_MAIN + D_KV, :], NT)
        va_ref[...] = _dot(xb, w_ref[D_MAIN + D_KV : D_IN, :], NT).astype(BF16)
        c, s = cos_ref[...], sin_ref[...]
        q_ref[...] = _rope(q, _lane_tile(c, D_ATTN // LANES), _lane_tile(s, D_ATTN // LANES)).astype(BF16)
        k_ref[...] = _rope(k, c, s).astype(BF16)

    body, in_specs, operands = _after(
        dep, body, [_row_spec(TM, D_MODEL), _const_spec((D_IN, D_MODEL)), _row_spec(TM, LANES), _row_spec(TM, LANES)], [x, w_in_t, cos, sin])
    return pl.pallas_call(
        body,
        name="in_proj",
        grid=(t // TM,),
        in_specs=in_specs,
        out_specs=[_row_spec(TM, D_GMLP), _row_spec(TM, D_GMLP), _row_spec(TM, D_ATTN), _row_spec(TM, D_KV), _row_spec(TM, D_KV)],
        out_shape=[
            jax.ShapeDtypeStruct((t, D_GMLP), F32),
            jax.ShapeDtypeStruct((t, D_GMLP), F32),
            jax.ShapeDtypeStruct((t, D_ATTN), BF16),
            jax.ShapeDtypeStruct((t, D_KV), BF16),
            jax.ShapeDtypeStruct((t, D_KV), BF16),
        ],
        compiler_params=_params(("parallel",)),
    )(*operands)


def _chunk_specs():
    cur = lambda i: (i, 0)
    prev = lambda i: (jnp.maximum(i - 1, 0), 0)
    return [
        pl.BlockSpec((CHUNK, D_GMLP), cur),
        pl.BlockSpec((CHUNK, D_GMLP), cur),
        pl.BlockSpec((CHUNK, D_ATTN), cur),
        pl.BlockSpec((CHUNK, D_KV), cur),
        pl.BlockSpec((CHUNK, D_KV), prev),
        pl.BlockSpec((CHUNK, D_KV), cur),
        pl.BlockSpec((CHUNK, D_KV), prev),
    ]


def _half_lane_masks(rows):
    lane = lax.broadcasted_iota(jnp.int32, (rows, LANES), 1)
    return lane < HEAD_DIM


def _kv_variants(kv2):
    left = _half_lane_masks(kv2.shape[0])
    f = kv2.astype(F32)
    swapped = pltpu.roll(f, HEAD_DIM, 1)
    zero = jnp.zeros_like(f)
    g0 = (jnp.where(left, f, zero).astype(BF16), jnp.where(left, zero, swapped).astype(BF16))
    g1 = (jnp.where(left, swapped, zero).astype(BF16), jnp.where(left, zero, f).astype(BF16))
    return (g0, g1)


def _band_mask(i):
    row = lax.broadcasted_iota(jnp.int32, (CHUNK, 2 * CHUNK), 0)
    col = lax.broadcasted_iota(jnp.int32, (CHUNK, 2 * CHUNK), 1)
    no_prev = jnp.where(i > 0, 0, 4 * CHUNK)
    in_prev = jnp.logical_and(col < CHUNK, (col - row) > no_prev)
    in_cur = jnp.logical_and(col >= CHUNK, (col - CHUNK) <= row)
    return jnp.logical_or(in_prev, in_cur)


def _softmax_with_sink(s, sink):
    m = jnp.maximum(jnp.max(s, axis=1, keepdims=True), sink)
    e = jnp.exp(s - m)
    e_sink = jnp.exp(sink - m)
    inv = 1.0 / (jnp.sum(e, axis=1, keepdims=True) + e_sink)
    return e * inv, e_sink * inv


def _spatial_weights(w_ref, pair):
    row = lax.broadcasted_iota(jnp.int32, (CHUNK, CHUNK), 0)
    col = lax.broadcasted_iota(jnp.int32, (CHUNK, CHUNK), 1)
    causal = col <= row
    wl = jnp.where(causal, w_ref[2 * pair], 0.0).astype(BF16)
    wr = jnp.where(causal, w_ref[2 * pair + 1], 0.0).astype(BF16)
    return wl, wr, causal


def _mixer_fwd(u, vg, q, k, va, v_ln_g, v_ln_b, w_spatial, bias_full, sinks):
    t = u.shape[0]

    def body(u_ref, vg_ref, q_ref, kc_ref, kp_ref, vc_ref, vp_ref, g_ref, b_ref, w_ref, bias_ref, sink_ref, cat_ref):
        i = pl.program_id(0)
        left = _half_lane_masks(CHUNK)
        ug = _gelu(u_ref[...])
        xhat, _ = _layer_norm_stats(_gelu(vg_ref[...]))
        vgl = xhat * g_ref[...] + b_ref[...]
        for p in range(D_GMLP // LANES):
            cols = slice(p * LANES, (p + 1) * LANES)
            xp = vgl[:, cols]
            wl, wr, _ = _spatial_weights(w_ref, p)
            mixed = _dot(wl, jnp.where(left, xp, 0.0).astype(BF16)) + _dot(wr, jnp.where(left, 0.0, xp).astype(BF16))
            cat_ref[:, cols] = (ug[:, cols] * (mixed + bias_ref[:, cols])).astype(BF16)

        k_var = _kv_variants(jnp.concatenate([kp_ref[...], kc_ref[...]], axis=0))
        v_var = _kv_variants(jnp.concatenate([vp_ref[...], vc_ref[...]], axis=0))
        valid = _band_mask(i)
        for p in range(D_ATTN // LANES):
            qp = q_ref[:, p * LANES : (p + 1) * LANES]
            group = p // 2
            out = jnp.zeros((CHUNK, LANES), F32)
            for side in range(2):
                s = jnp.where(valid, _dot(qp, k_var[group][side], NT) * SCALE, NEG_INF)
                probs, _ = _softmax_with_sink(s, sink_ref[2 * p + side])
                out = out + _dot(probs.astype(BF16), v_var[group][side])
            cat_ref[:, D_GMLP + p * LANES : D_GMLP + (p + 1) * LANES] = out.astype(BF16)

    return pl.pallas_call(
        body,
        name="mixer_fwd",
        grid=(t // CHUNK,),
        in_specs=_chunk_specs()
        + [
            _const_spec((1, D_GMLP)),
            _const_spec((1, D_GMLP)),
            _const_spec((N_HEADS, CHUNK, CHUNK)),
            _const_spec((CHUNK, D_GMLP)),
            pl.BlockSpec(memory_space=pltpu.SMEM),
        ],
        out_specs=pl.BlockSpec((CHUNK, D_MODEL), lambda i: (i, 0)),
        out_shape=jax.ShapeDtypeStruct((t, D_MODEL), BF16),
        compiler_params=_params(("parallel",)),
    )(u, vg, q, k, k, va, va, v_ln_g, v_ln_b, w_spatial, bias_full, sinks)


def _out_proj_ln1(cat, x, w_out):
    t = x.shape[0]

    def body(cat_ref, x_ref, w_ref, xhat_ref, rstd_ref):
        z = ALPHA * x_ref[...] + _dot(cat_ref[...], w_ref[...])
        xhat, rstd = _layer_norm_stats(z)
        xhat_ref[...] = xhat
        rstd_ref[...] = rstd

    return pl.pallas_call(
        body,
        name="out_proj_ln1",
        grid=(t // TM,),
        in_specs=[_row_spec(TM, D_MODEL), _row_spec(TM, D_MODEL), _const_spec((D_MODEL, D_MODEL))],
        out_specs=[_row_spec(TM, D_MODEL), _row_spec(TM, 1)],
        out_shape=[jax.ShapeDtypeStruct((t, D_MODEL), F32), jax.ShapeDtypeStruct((t, 1), F32)],
        compiler_params=_params(("parallel",)),
    )(cat, x, w_out)


def _ffn_fwd_loss(xhat1, ln1_g, ln1_b, w1, w2, ln2_g, ln2_b, target):
    t = xhat1.shape[0]

    def body(xh_ref, g1_ref, b1_ref, w1_ref, w2_ref, g2_ref, b2_ref, tgt_ref, r_ref, dz2_ref, dg2_ref, db2_ref, sq_ref):
        @pl.when(pl.program_id(0) == 0)
        def _():
            dg2_ref[...] = jnp.zeros_like(dg2_ref)
            db2_ref[...] = jnp.zeros_like(db2_ref)
            sq_ref[...] = jnp.zeros_like(sq_ref)

        x1 = xh_ref[...] * g1_ref[...] + b1_ref[...]
        x1b = x1.astype(BF16)
        ff = jnp.zeros((TM_FFN, D_MODEL), F32)
        for j in range(N_FF_BLOCKS):
            r = jnp.maximum(_dot(x1b, w1_ref[j]), 0.0)
            r_ref[:, j * D_MODEL : (j + 1) * D_MODEL] = r.astype(BF16)
            ff = ff + _dot((r * r).astype(BF16), w2_ref[j])
        xhat2, rstd2 = _layer_norm_stats(ALPHA * x1 + ff)
        err = xhat2 * g2_ref[...] + b2_ref[...] - tgt_ref[...]
        sq_ref[...] += jnp.sum(err * err, axis=0, keepdims=True)
        dy = err * (1.0 / D_MODEL)
        dg2_ref[...] += jnp.sum(dy * xhat2, axis=0, keepdims=True)
        db2_ref[...] += jnp.sum(dy, axis=0, keepdims=True)
        dz2_ref[...] = _layer_norm_bwd(dy * g2_ref[...], xhat2, rstd2)

    vec = _const_spec((1, D_MODEL))
    wspec = _const_spec((N_FF_BLOCKS, D_MODEL, D_MODEL), single_buffer=True)
    return pl.pallas_call(
        body,
        name="ffn_fwd_loss",
        grid=(t // TM_FFN,),
        in_specs=[_row_spec(TM_FFN, D_MODEL), vec, vec, wspec, wspec, vec, vec, _row_spec(TM_FFN, D_MODEL)],
        out_specs=[_row_spec(TM_FFN, D_FF), _row_spec(TM_FFN, D_MODEL), vec, vec, vec],
        out_shape=[
            jax.ShapeDtypeStruct((t, D_FF), BF16),
            jax.ShapeDtypeStruct((t, D_MODEL), F32),
            jax.ShapeDtypeStruct((1, D_MODEL), F32),
            jax.ShapeDtypeStruct((1, D_MODEL), F32),
            jax.ShapeDtypeStruct((1, D_MODEL), F32),
        ],
        compiler_params=_params(("arbitrary",)),
    )(xhat1, ln1_g, ln1_b, w1, w2, ln2_g, ln2_b, target)


def _ffn_bwd_ln1(dz2, r, xhat1, rstd1, ln1_g, w1, w2):
    t = dz2.shape[0]

    def body(dz2_ref, r_ref, xh_ref, rstd_ref, g1_ref, w1_ref, w2_ref, dpre_ref, dz1_ref, dg1_ref, db1_ref):
        @pl.when(pl.program_id(0) == 0)
        def _():
            dg1_ref[...] = jnp.zeros_like(dg1_ref)
            db1_ref[...] = jnp.zeros_like(db1_ref)

        dz2 = dz2_ref[...]
        dz2b = dz2.astype(BF16)
        dx1 = ALPHA * dz2
        for j in range(N_FF_BLOCKS):
            cols = slice(j * D_MODEL, (j + 1) * D_MODEL)
            dpre = (_dot(dz2b, w2_ref[j], NT) * (2.0 * r_ref[:, cols].astype(F32))).astype(BF16)
            dpre_ref[:, cols] = dpre
            dx1 = dx1 + _dot(dpre, w1_ref[j], NT)
        xhat1 = xh_ref[...]
        dg1_ref[...] += jnp.sum(dx1 * xhat1, axis=0, keepdims=True)
        db1_ref[...] += jnp.sum(dx1, axis=0, keepdims=True)
        dz1_ref[...] = _layer_norm_bwd(dx1 * g1_ref[...], xhat1, rstd_ref[...])

    vec = _const_spec((1, D_MODEL))
    wspec = _const_spec((N_FF_BLOCKS, D_MODEL, D_MODEL), single_buffer=True)
    return pl.pallas_call(
        body,
        name="ffn_bwd_ln1",
        grid=(t // TM_FFN,),
        in_specs=[_row_spec(TM_FFN, D_MODEL), _row_spec(TM_FFN, D_FF), _row_spec(TM_FFN, D_MODEL), _row_spec(TM_FFN, 1), vec, wspec, wspec],
        out_specs=[_row_spec(TM_FFN, D_FF), _row_spec(TM_FFN, D_MODEL), vec, vec],
        out_shape=[
            jax.ShapeDtypeStruct((t, D_FF), BF16),
            jax.ShapeDtypeStruct((t, D_MODEL), F32),
            jax.ShapeDtypeStruct((1, D_MODEL), F32),
            jax.ShapeDtypeStruct((1, D_MODEL), F32),
        ],
        compiler_params=_params(("arbitrary",)),
    )(dz2, r, xhat1, rstd1, ln1_g, w1, w2)


def _dcat(dz1, w_out, dep=None):
    t = dz1.shape[0]

    def body(dz1_ref, w_ref, dcat_ref):
        dcat_ref[...] = _dot(dz1_ref[...].astype(BF16), w_ref[...], NT)

    body, in_specs, operands = _after(dep, body, [_row_spec(TM, D_MODEL), _const_spec((D_MODEL, D_MODEL))], [dz1, w_out])
    return pl.pallas_call(
        body,
        name="dcat",
        grid=(t // TM,),
        in_specs=in_specs,
        out_specs=_row_spec(TM, D_MODEL),
        out_shape=jax.ShapeDtypeStruct((t, D_MODEL), F32),
        compiler_params=_params(("parallel",)),
    )(*operands)


def _mixer_bwd(u, vg, q, k, va, dcat, cos, sin, v_ln_g, v_ln_b, w_spatial, bias_full, sinks, dep=None):
    t = u.shape[0]
    n_chunks = t // CHUNK

    def body(u_ref, vg_ref, q_ref, kc_ref, kp_ref, vc_ref, vp_ref, dcat_ref, cosc_ref, sinc_ref, cosp_ref, sinp_ref,
             g_ref, b_ref, w_ref, bias_ref, sink_ref,
             dmain_ref, dkv_ref, dg_ref, db_ref, dw_ref, dbs_ref, dsink_ref, dmix_acc):
        i = pl.program_id(0)
        left = _half_lane_masks(CHUNK)
        lane = lax.broadcasted_iota(jnp.int32, (CHUNK, LANES), 1)

        @pl.when(i == 0)
        def _():
            dg_ref[...] = jnp.zeros_like(dg_ref)
            db_ref[...] = jnp.zeros_like(db_ref)
            dw_ref[...] = jnp.zeros_like(dw_ref)
            dsink_ref[...] = jnp.zeros_like(dsink_ref)
            dmix_acc[...] = jnp.zeros_like(dmix_acc)

        ug, dug_du = _gelu_and_grad(u_ref[...])
        gv, dgv_dv = _gelu_and_grad(vg_ref[...])
        xhat, rstd = _layer_norm_stats(gv)
        gain = g_ref[...]
        vgl = xhat * gain + b_ref[...]
        dvgl_parts = []
        for p in range(D_GMLP // LANES):
            cols = slice(p * LANES, (p + 1) * LANES)
            xp = vgl[:, cols]
            xl = jnp.where(left, xp, 0.0).astype(BF16)
            xr = jnp.where(left, 0.0, xp).astype(BF16)
            wl, wr, causal = _spatial_weights(w_ref, p)
            mixed = _dot(wl, xl) + _dot(wr, xr) + bias_ref[:, cols]
            da = dcat_ref[:, cols]
            dmain_ref[:, cols] = (da * mixed * dug_du[:, cols]).astype(BF16)
            dmixed = da * ug[:, cols]
            dmix_acc[:, cols] += dmixed
            dml = jnp.where(left, dmixed, 0.0).astype(BF16)
            dmr = jnp.where(left, 0.0, dmixed).astype(BF16)
            xpb = xp.astype(BF16)
            dw_ref[2 * p] += jnp.where(causal, _dot(dml, xpb, NT), 0.0)
            dw_ref[2 * p + 1] += jnp.where(causal, _dot(dmr, xpb, NT), 0.0)
            dvgl_parts.append(_dot(wl, dml, TN) + _dot(wr, dmr, TN))
        dvgl = jnp.concatenate(dvgl_parts, axis=1)
        dg_ref[...] += jnp.sum(dvgl * xhat, axis=0, keepdims=True)
        db_ref[...] += jnp.sum(dvgl, axis=0, keepdims=True)
        dgv = _layer_norm_bwd(dvgl * gain, xhat, rstd)
        dmain_ref[:, D_GMLP : 2 * D_GMLP] = (dgv * dgv_dv).astype(BF16)

        @pl.when(i == n_chunks - 1)
        def _():
            tile = jnp.zeros((CHUNK, LANES), F32)
            for p in range(D_GMLP // LANES):
                dm = dmix_acc[:, p * LANES : (p + 1) * LANES]
                sl = jnp.sum(jnp.where(left, dm, 0.0), axis=1, keepdims=True)
                sr = jnp.sum(jnp.where(left, 0.0, dm), axis=1, keepdims=True)
                tile = jnp.where(lane == 2 * p, sl, tile)
                tile = jnp.where(lane == 2 * p + 1, sr, tile)
            dbs_ref[...] = tile

        k_var = _kv_variants(jnp.concatenate([kp_ref[...], kc_ref[...]], axis=0))
        v_var = _kv_variants(jnp.concatenate([vp_ref[...], vc_ref[...]], axis=0))
        valid = _band_mask(i)
        dk_acc = [[None, None], [None, None]]
        dv_acc = [[None, None], [None, None]]
        dsink_row = jnp.zeros((1, LANES), F32)
        lane_row = lax.broadcasted_iota(jnp.int32, (1, LANES), 1)
        dq_parts = []
        for p in range(D_ATTN // LANES):
            qp = q_ref[:, p * LANES : (p + 1) * LANES]
            do = dcat_ref[:, D_GMLP + p * LANES : D_GMLP + (p + 1) * LANES]
            dob = do.astype(BF16)
            group = p // 2
            dq = jnp.zeros((CHUNK, LANES), F32)
            for side in range(2):
                head = 2 * p + side
                kx, vx = k_var[group][side], v_var[group][side]
                s = jnp.where(valid, _dot(qp, kx, NT) * SCALE, NEG_INF)
                probs, p_sink = _softmax_with_sink(s, sink_ref[head])
                dp = _dot(dob, vx, NT)
                dsum = jnp.sum(probs * dp, axis=1, keepdims=True)
                ds = (probs * (dp - dsum) * SCALE).astype(BF16)
                dsink_row = dsink_row + jnp.where(lane_row == head, -jnp.sum(p_sink * dsum, axis=0, keepdims=True), 0.0)
                dq = dq + _dot(ds, kx)
                in_side = left if side == 0 else jnp.logical_not(left)
                q_side = jnp.where(in_side, qp.astype(F32), 0.0).astype(BF16)
                do_side = jnp.where(in_side, do, 0.0).astype(BF16)
                dk_h = _dot(ds, q_side, TN)
                dv_h = _dot(probs.astype(BF16), do_side, TN)
                dk_acc[group][side] = dk_h if dk_acc[group][side] is None else dk_acc[group][side] + dk_h
                dv_acc[group][side] = dv_h if dv_acc[group][side] is None else dv_acc[group][side] + dv_h
            dq_parts.append(dq)
        dsink_ref[0:1, :] += dsink_row
        cos_c, sin_c = cosc_ref[...], sinc_ref[...]
        dq_all = jnp.concatenate(dq_parts, axis=1)
        reps = D_ATTN // LANES
        dmain_ref[:, 2 * D_GMLP : D_MAIN] = _rope_transposed(dq_all, _lane_tile(cos_c, reps), _lane_tile(sin_c, reps)).astype(BF16)

        def group_lanes(acc):
            return acc[0][0] + pltpu.roll(acc[0][1], HEAD_DIM, 1) + pltpu.roll(acc[1][0], HEAD_DIM, 1) + acc[1][1]

        dk2 = group_lanes(dk_acc)
        dv2 = group_lanes(dv_acc)
        cur = pl.ds(pl.multiple_of(i * CHUNK, CHUNK), CHUNK)
        dkv_ref[cur, 0:D_KV] = _rope_transposed(dk2[CHUNK:], cos_c, sin_c)
        dkv_ref[cur, D_KV : 2 * D_KV] = dv2[CHUNK:]

        @pl.when(i > 0)
        def _():
            prev = pl.ds(pl.multiple_of((i - 1) * CHUNK, CHUNK), CHUNK)
            dkv_ref[prev, 0:D_KV] += _rope_transposed(dk2[:CHUNK], cosp_ref[...], sinp_ref[...])
            dkv_ref[prev, D_KV : 2 * D_KV] += dv2[:CHUNK]

    cur = lambda i: (i, 0)
    prev = lambda i: (jnp.maximum(i - 1, 0), 0)
    in_specs = _chunk_specs() + [
        pl.BlockSpec((CHUNK, D_MODEL), cur),
        pl.BlockSpec((CHUNK, LANES), cur),
        pl.BlockSpec((CHUNK, LANES), cur),
        pl.BlockSpec((CHUNK, LANES), prev),
        pl.BlockSpec((CHUNK, LANES), prev),
        _const_spec((1, D_GMLP)),
        _const_spec((1, D_GMLP)),
        _const_spec((N_HEADS, CHUNK, CHUNK)),
        _const_spec((CHUNK, D_GMLP)),
        pl.BlockSpec(memory_space=pltpu.SMEM),
    ]
    body, in_specs, operands = _after(
        dep, body, in_specs, [u, vg, q, k, k, va, va, dcat, cos, sin, cos, sin, v_ln_g, v_ln_b, w_spatial, bias_full, sinks])
    return pl.pallas_call(
        body,
        name="mixer_bwd",
        grid=(n_chunks,),
        in_specs=in_specs,
        out_specs=[
            pl.BlockSpec((CHUNK, D_MAIN), cur),
            _const_spec((t, 2 * D_KV)),
            _const_spec((1, D_GMLP)),
            _const_spec((1, D_GMLP)),
            _const_spec((N_HEADS, CHUNK, CHUNK)),
            _const_spec((CHUNK, LANES)),
            _const_spec((8, LANES)),
        ],
        out_shape=[
            jax.ShapeDtypeStruct((t, D_MAIN), BF16),
            jax.ShapeDtypeStruct((t, 2 * D_KV), F32),
            jax.ShapeDtypeStruct((1, D_GMLP), F32),
            jax.ShapeDtypeStruct((1, D_GMLP), F32),
            jax.ShapeDtypeStruct((N_HEADS, CHUNK, CHUNK), F32),
            jax.ShapeDtypeStruct((CHUNK, LANES), F32),
            jax.ShapeDtypeStruct((8, LANES), F32),
        ],
        scratch_shapes=[pltpu.VMEM((CHUNK, D_GMLP), F32)],
        compiler_params=_params(("arbitrary",)),
    )(*operands)


def _grad_x(dh_main, dkv, dz1, w_in_t):
    t = dz1.shape[0]

    def body(dm_ref, dkv_ref, dz1_ref, w_ref, gx_ref):
        acc = ALPHA * dz1_ref[...] + _dot(dm_ref[...], w_ref[0:D_MAIN, :])
        gx_ref[...] = acc + _dot(dkv_ref[...].astype(BF16), w_ref[D_MAIN:D_IN, :])

    return pl.pallas_call(
        body,
        name="grad_x",
        grid=(t // TM,),
        in_specs=[_row_spec(TM, D_MAIN), _row_spec(TM, 2 * D_KV), _row_spec(TM, D_MODEL), _const_spec((D_IN, D_MODEL))],
        out_specs=_row_spec(TM, D_MODEL),
        out_shape=jax.ShapeDtypeStruct((t, D_MODEL), F32),
        compiler_params=_params(("parallel",)),
    )(dh_main, dkv, dz1, w_in_t)


def _token_contraction(name, n_blocks, out_rows, in_arrays, in_specs, contributions, dep=None):
    t = in_arrays[0].shape[0]
    block_rows = out_rows // n_blocks

    def body(*refs):
        out_ref = refs[-1]

        @pl.when(pl.program_id(1) == 0)
        def _():
            out_ref[...] = jnp.zeros_like(out_ref)

        for row0, a, b in contributions(*refs[:-1]):
            out_ref[row0 : row0 + a.shape[1], :] += _dot(a, b, TN)

    body, in_specs, operands = _after(dep, body, in_specs, in_arrays)
    return pl.pallas_call(
        body,
        name=name,
        grid=(n_blocks, t // TK),
        in_specs=in_specs,
        out_specs=pl.BlockSpec((block_rows, D_MODEL), lambda j, k: (j, 0)),
        out_shape=jax.ShapeDtypeStruct((out_rows, D_MODEL), F32),
        compiler_params=_params(("parallel", "arbitrary")),
    )(*operands)


def _tile_spec(cols):
    return pl.BlockSpec((TK, cols), lambda j, k: (k, 0))


def _tile_block_spec():
    return pl.BlockSpec((TK, D_MODEL), lambda j, k: (k, j))


def _grad_w_in_t(dh_main, dkv, x):
    def contributions(dm_ref, dkv_ref, x_ref):
        xb = x_ref[...].astype(BF16)
        return [(0, dm_ref[...], xb), (D_MAIN, dkv_ref[...].astype(BF16), xb)]

    return _token_contraction("grad_w_in", 1, D_IN, [dh_main, dkv, x], [_tile_spec(D_MAIN), _tile_spec(2 * D_KV), _tile_spec(D_MODEL)], contributions)


def _grad_w_out(cat, dz1, dep=None):
    def contributions(cat_ref, dz1_ref):
        return [(0, cat_ref[...], dz1_ref[...].astype(BF16))]

    return _token_contraction("grad_w_out", 1, D_MODEL, [cat, dz1], [_tile_spec(D_MODEL), _tile_spec(D_MODEL)], contributions, dep)


def _grad_w_ff1(xhat1, ln1_g, ln1_b, dpre):
    def contributions(xh_ref, g_ref, b_ref, dpre_ref):
        return [(0, (xh_ref[...] * g_ref[...] + b_ref[...]).astype(BF16), dpre_ref[...])]

    vec = pl.BlockSpec((1, D_MODEL), lambda j, k: (0, 0))
    return _token_contraction("grad_w_ff1", N_FF_BLOCKS, D_FF, [xhat1, ln1_g, ln1_b, dpre], [_tile_spec(D_MODEL), vec, vec, _tile_block_spec()], contributions)


def _grad_w_ff2(r, dz2):
    def contributions(r_ref, dz2_ref):
        rf = r_ref[...].astype(F32)
        return [(0, (rf * rf).astype(BF16), dz2_ref[...].astype(BF16))]

    return _token_contraction("grad_w_ff2", N_FF_BLOCKS, D_FF, [r, dz2], [_tile_block_spec(), _tile_spec(D_MODEL)], contributions)


ANY = pl.BlockSpec(memory_space=pl.ANY)


def _mesh_position():
    return lax.axis_index("x"), lax.axis_index("y"), lax.axis_index("c")


def _other_chips(x, y):
    return [(1 - x, y), (x, 1 - y), (1 - x, 1 - y)]


def _remote(src, dst, send_sem, recv_sem, device):
    return pltpu.make_async_remote_copy(src_ref=src, dst_ref=dst, send_sem=send_sem, recv_sem=recv_sem, device_id=device, device_id_type=MESH)


def _rows(ref, start, size):
    return ref.at[pl.ds(start, size), :]


def _all_gather_weights(shards):
    n = len(shards)
    per = 7

    def body(*refs):
        ins, outs = refs[:n], refs[n : 2 * n]
        send_sems, recv_sems = refs[2 * n :]
        x, y, c = _mesh_position()
        me = 2 * x + y
        chips = _other_chips(x, y)
        sibling = (x, y, 1 - c)
        started = []
        for w in range(n):
            rows = shards[w].shape[0]
            half = rows // 2
            for kk, (px, py) in enumerate(chips):
                cp = _remote(_rows(ins[w], c * half, half), _rows(outs[w], me * rows + c * half, half),
                             send_sems.at[per * w + kk], recv_sems.at[per * w + kk], (px, py, c))
                cp.start()
                started.append(cp)
            cp = _remote(ins[w], _rows(outs[w], me * rows, rows), send_sems.at[per * w + 6], recv_sems.at[per * w + 6], sibling)
            cp.start()
            started.append(cp)
        for w in range(n):
            rows = shards[w].shape[0]
            half = rows // 2
            for kk, (px, py) in enumerate(chips):
                blk = _rows(outs[w], (2 * px + py) * rows + c * half, half)
                _remote(blk, blk, send_sems.at[per * w + kk], recv_sems.at[per * w + kk], (px, py, c)).wait_recv()
                fwd = _remote(blk, blk, send_sems.at[per * w + 3 + kk], recv_sems.at[per * w + 3 + kk], sibling)
                fwd.start()
                started.append(fwd)
        for w in range(n):
            rows = shards[w].shape[0]
            half = rows // 2
            for kk, (px, py) in enumerate(chips):
                blk = _rows(outs[w], (2 * px + py) * rows + (1 - c) * half, half)
                _remote(blk, blk, send_sems.at[per * w + 3 + kk], recv_sems.at[per * w + 3 + kk], sibling).wait_recv()
            own = _rows(outs[w], me * rows, rows)
            _remote(own, own, send_sems.at[per * w + 6], recv_sems.at[per * w + 6], sibling).wait_recv()
        for cp in started:
            cp.wait_send()

    return pl.pallas_call(
        body,
        name="all_gather_weights",
        in_specs=[ANY] * n,
        out_specs=[ANY] * n,
        out_shape=[jax.ShapeDtypeStruct((N_CHIPS * s.shape[0], s.shape[1]), s.dtype) for s in shards],
        scratch_shapes=[pltpu.SemaphoreType.DMA((per * n,)), pltpu.SemaphoreType.DMA((per * n,))],
    )(*shards)


def _pair_swap(grads):
    n = len(grads)

    def body(*refs):
        ins, theirs = refs[:n], refs[n : 2 * n]
        send_sems, recv_sems = refs[2 * n :]
        x, y, c = _mesh_position()
        sibling = (x, y, 1 - c)
        sends = []
        for w in range(n):
            rows = grads[w].shape[0] // N_CHIPS
            half = rows // 2
            for j in range(N_CHIPS):
                cp = _remote(_rows(ins[w], j * rows + (1 - c) * half, half), _rows(theirs[w], j * half, half),
                             send_sems.at[4 * w + j], recv_sems.at[4 * w + j], sibling)
                cp.start()
                sends.append(cp)
        for cp in sends:
            cp.wait_recv()
        for cp in sends:
            cp.wait_send()

    return pl.pallas_call(
        body,
        name="grad_pair_swap",
        in_specs=[ANY] * n,
        out_specs=[ANY] * n,
        out_shape=[jax.ShapeDtypeStruct((g.shape[0] // 2, g.shape[1]), g.dtype) for g in grads],
        scratch_shapes=[pltpu.SemaphoreType.DMA((4 * n,)), pltpu.SemaphoreType.DMA((4 * n,))],
    )(*grads)


def _chip_exchange(partials):
    n = len(partials)

    def body(*refs):
        ins, outs = refs[:n], refs[n : 2 * n]
        send_sems, recv_sems = refs[2 * n :]
        x, y, c = _mesh_position()
        chips = _other_chips(x, y)
        sends = []
        for w in range(n):
            half = partials[w].shape[0] // N_CHIPS
            for kk, (px, py) in enumerate(chips):
                cp = _remote(_rows(ins[w], (2 * px + py) * half, half), _rows(outs[w], kk * half, half),
                             send_sems.at[3 * w + kk], recv_sems.at[3 * w + kk], (px, py, c))
                cp.start()
                sends.append(cp)
        for cp in sends:
            cp.wait_recv()
        for cp in sends:
            cp.wait_send()

    return pl.pallas_call(
        body,
        name="grad_chip_exchange",
        in_specs=[ANY] * n,
        out_specs=[ANY] * n,
        out_shape=[jax.ShapeDtypeStruct((3 * p.shape[0] // N_CHIPS, p.shape[1]), p.dtype) for p in partials],
        scratch_shapes=[pltpu.SemaphoreType.DMA((3 * n,)), pltpu.SemaphoreType.DMA((3 * n,))],
    )(*partials)


def _pair_gather(shards):
    n = len(shards)

    def body(*refs):
        outs = refs[n : 2 * n]
        send_sems, recv_sems = refs[2 * n :]
        x, y, c = _mesh_position()
        sibling = (x, y, 1 - c)
        sends = []
        for w in range(n):
            half = shards[w].shape[0] // 2
            mine = _rows(outs[w], c * half, half)
            cp = _remote(mine, mine, send_sems.at[w], recv_sems.at[w], sibling)
            cp.start()
            sends.append(cp)
        for w in range(n):
            half = shards[w].shape[0] // 2
            blk = _rows(outs[w], (1 - c) * half, half)
            _remote(blk, blk, send_sems.at[w], recv_sems.at[w], sibling).wait_recv()
        for cp in sends:
            cp.wait_send()

    return pl.pallas_call(
        body,
        name="grad_pair_gather",
        in_specs=[ANY] * n,
        out_specs=[ANY] * n,
        out_shape=[jax.ShapeDtypeStruct(s.shape, s.dtype) for s in shards],
        input_output_aliases={w: w for w in range(n)},
        scratch_shapes=[pltpu.SemaphoreType.DMA((n,)), pltpu.SemaphoreType.DMA((n,))],
    )(*shards)


def _all_reduce_small(slab):
    rows = slab.shape[0]
    part = rows // 8

    def body(slab_ref, out_ref, landing, reduced, send_sems, recv_sems):
        x, y, c = _mesh_position()
        me = 4 * x + 2 * y + c
        flips = [(k >> 2, (k >> 1) & 1, k & 1) for k in range(1, 8)]

        def peer(flip):
            fx, fy, fc = flip
            return (1 - x if fx else x, 1 - y if fy else y, 1 - c if fc else c)

        def my_rows(ref):
            return ref.at[pl.ds(pl.multiple_of(me * part, 8), part), :]

        sends = []
        for kk, flip in enumerate(flips):
            px, py, pc = peer(flip)
            them = 4 * px + 2 * py + pc
            cp = _remote(slab_ref.at[pl.ds(pl.multiple_of(them * part, 8), part), :], landing.at[me], send_sems.at[kk], recv_sems.at[kk], (px, py, pc))
            cp.start()
            sends.append(cp)
        landing[me] = my_rows(slab_ref)[...]
        for kk, flip in enumerate(flips):
            px, py, pc = peer(flip)
            them = 4 * px + 2 * py + pc
            _remote(landing.at[them], landing.at[them], send_sems.at[kk], recv_sems.at[kk], (px, py, pc)).wait_recv()
        total = landing[0]
        for s in range(1, 8):
            total = total + landing[s]
        reduced[...] = total
        my_rows(out_ref)[...] = total
        for kk, flip in enumerate(flips):
            cp = _remote(reduced, my_rows(out_ref), send_sems.at[7 + kk], recv_sems.at[7 + kk], peer(flip))
            cp.start()
            sends.append(cp)
        for kk, flip in enumerate(flips):
            px, py, pc = peer(flip)
            them = 4 * px + 2 * py + pc
            blk = out_ref.at[pl.ds(pl.multiple_of(them * part, 8), part), :]
            _remote(blk, blk, send_sems.at[7 + kk], recv_sems.at[7 + kk], (px, py, pc)).wait_recv()
        for cp in sends:
            cp.wait_send()

    vmem = pl.BlockSpec(memory_space=pltpu.VMEM)
    return pl.pallas_call(
        body,
        name="all_reduce_small",
        in_specs=[vmem],
        out_specs=vmem,
        out_shape=jax.ShapeDtypeStruct(slab.shape, slab.dtype),
        scratch_shapes=[pltpu.VMEM((8, part, LANES), F32), pltpu.VMEM((part, LANES), F32), pltpu.SemaphoreType.DMA((14,)), pltpu.SemaphoreType.DMA((14,))],
    )(slab)


HBM = pl.BlockSpec(memory_space=pltpu.HBM)
SEM = pl.BlockSpec(memory_space=pltpu.SEMAPHORE)
DATAFLOW = pltpu.SideEffectType.DATAFLOW_SIDE_EFFECTING
TOKEN = jax.ShapeDtypeStruct((8, LANES), F32)


def _plan_copies(bufs, plan, send_sems, recv_sems):
    out = []
    for i, (src, src_row, dst, dst_row, recv_row, rows, device) in enumerate(plan):
        send = _remote(_rows(bufs[src], src_row, rows), _rows(bufs[dst], dst_row, rows), send_sems.at[i], recv_sems.at[i], device)
        landed = _rows(bufs[dst], recv_row, rows)
        recv = _remote(landed, landed, send_sems.at[i], recv_sems.at[i], device)
        out.append((send, recv))
    return out


def _split_call(name, bufs, wait=None, start=None, after=None):
    n = len(bufs)
    n_in = n + (2 if wait else 0) + (1 if after is not None else 0)
    n_start = len(start(0, 0, 0)) if start else 0

    def body(*refs):
        ins = refs[:n]
        x, y, c = _mesh_position()
        if wait:
            for send, recv in _plan_copies(ins, wait[0](x, y, c), refs[n], refs[n + 1]):
                recv.wait_recv()
                send.wait_send()
        if start:
            for send, _ in _plan_copies(ins, start(x, y, c), refs[n_in + n + 1], refs[n_in + n + 2]):
                send.start()
        token = refs[n_in + n]
        token[...] = jnp.zeros_like(token)

    operands = [pltpu.with_memory_space_constraint(b, pltpu.HBM) for b in bufs]
    in_specs = [HBM] * n
    if wait:
        operands += [wait[1], wait[2]]
        in_specs += [SEM, SEM]
    if after is not None:
        operands.append(after)
        in_specs.append(ANY)
    out_shape = [pltpu.HBM(b.shape, b.dtype) for b in bufs] + [TOKEN]
    out_specs = [HBM] * n + [pl.BlockSpec(memory_space=pltpu.VMEM)]
    if start:
        out_shape += [pltpu.SemaphoreType.DMA((n_start,)), pltpu.SemaphoreType.DMA((n_start,))]
        out_specs += [SEM, SEM]
    outs = pl.pallas_call(
        body,
        name=name,
        in_specs=in_specs,
        out_specs=out_specs,
        out_shape=out_shape,
        input_output_aliases={i: i for i in range(n)},
        compiler_params=pltpu.CompilerParams(has_side_effects=DATAFLOW),
    )(*operands)
    return (list(outs[:n]), outs[n]) + tuple(outs[n + 1 :])


def _gather_plans(shard_rows):
    n = len(shard_rows)

    def ici(x, y, c):
        me = 2 * x + y
        plan = []
        for w, rows in enumerate(shard_rows):
            half = rows // 2
            for px, py in _other_chips(x, y):
                plan.append((w, c * half, n + w, me * rows + c * half, (2 * px + py) * rows + c * half, half, (px, py, c)))
            plan.append((w, 0, n + w, me * rows, me * rows, rows, (x, y, 1 - c)))
        return plan

    def passed_on(x, y, c):
        plan = []
        for w, rows in enumerate(shard_rows):
            half = rows // 2
            for px, py in _other_chips(x, y):
                row = (2 * px + py) * rows
                plan.append((n + w, row + c * half, n + w, row + c * half, row + (1 - c) * half, half, (x, y, 1 - c)))
        return plan

    return ici, passed_on


def _swap_plan(block_rows):
    n = len(block_rows)

    def plan_fn(x, y, c):
        plan = []
        for w, rows in enumerate(block_rows):
            half = rows // 2
            for j in range(N_CHIPS):
                plan.append((w, j * rows + (1 - c) * half, n + w, j * half, j * half, half, (x, y, 1 - c)))
        return plan

    return plan_fn


def _exchange_plan(halves):
    n = len(halves)

    def plan_fn(x, y, c):
        plan = []
        for w, half in enumerate(halves):
            for kk, (px, py) in enumerate(_other_chips(x, y)):
                plan.append((w, (2 * px + py) * half, n + w, kk * half, kk * half, half, (px, py, c)))
        return plan

    return plan_fn


def _landing(rows, cols, dtype):
    return lax.empty((rows, cols), dtype)


def _row_tile(rows, cap=256):
    best = 8
    for cand in range(8, cap + 1, 8):
        if rows % cand == 0:
            best = cand
    return best


def _pair_sum(name, grad, theirs, pos):
    half = theirs.shape[0] // N_CHIPS
    cols = theirs.shape[1]
    tile = _row_tile(half)
    steps = half // tile

    def body(pos_ref, g_ref, t_ref, p_ref, own_ref):
        total = g_ref[...] + t_ref[...]
        p_ref[...] = total.astype(BF16)

        @pl.when(pl.program_id(1) == pos_ref[1])
        def _():
            own_ref[...] = total

    return pl.pallas_call(
        body,
        name=name,
        grid_spec=pltpu.PrefetchScalarGridSpec(
            num_scalar_prefetch=1,
            grid=(steps, N_CHIPS),
            in_specs=[
                pl.BlockSpec((tile, cols), lambda i, j, pos: ((2 * j + pos[0]) * steps + i, 0)),
                pl.BlockSpec((tile, cols), lambda i, j, pos: (j * steps + i, 0)),
            ],
            out_specs=[
                pl.BlockSpec((tile, cols), lambda i, j, pos: (j * steps + i, 0)),
                pl.BlockSpec((tile, cols), lambda i, j, pos: (i, 0)),
            ],
        ),
        out_shape=[jax.ShapeDtypeStruct((N_CHIPS * half, cols), BF16), jax.ShapeDtypeStruct((half, cols), F32)],
        compiler_params=_params(("parallel", "arbitrary")),
    )(pos, grad, theirs)


def _chip_sum(name, own, landed, pos):
    half, cols = own.shape
    tile = _row_tile(half)
    steps = half // tile

    def body(pos_ref, own_ref, l0, l1, l2, o_ref):
        o_ref[...] = ((own_ref[...] + l0[...].astype(F32)) + l1[...].astype(F32)) + l2[...].astype(F32)

    landed_specs = [pl.BlockSpec((tile, cols), lambda i, pos, _k=k: (_k * steps + i, 0)) for k in range(N_CHIPS - 1)]
    return pl.pallas_call(
        body,
        name=name,
        grid_spec=pltpu.PrefetchScalarGridSpec(
            num_scalar_prefetch=1,
            grid=(steps,),
            in_specs=[pl.BlockSpec((tile, cols), lambda i, pos: (i, 0))] + landed_specs,
            out_specs=pl.BlockSpec((tile, cols), lambda i, pos: (pos[0] * steps + i, 0)),
        ),
        out_shape=jax.ShapeDtypeStruct((2 * half, cols), F32),
        compiler_params=_params(("parallel",)),
    )(pos, own, landed, landed, landed)


def _adamw(name, w, g, m, v):
    rows, cols = w.shape
    tile = rows if rows * cols <= 256 * 1024 else _row_tile(rows)

    def body(w_ref, g_ref, m_ref, v_ref, d_ref, nm_ref, nv_ref):
        g = g_ref[...]
        nm = ADAM_B1 * m_ref[...] + (1.0 - ADAM_B1) * g
        nv = ADAM_B2 * v_ref[...] + (1.0 - ADAM_B2) * (g * g)
        m_hat = nm / (1.0 - ADAM_B1**ADAM_STEP)
        v_hat = nv / (1.0 - ADAM_B2**ADAM_STEP)
        d_ref[...] = -ADAM_LR * (m_hat / (jnp.sqrt(v_hat) + ADAM_EPS) + ADAM_WD * w_ref[...])
        nm_ref[...] = nm
        nv_ref[...] = nv

    spec = _row_spec(tile, cols)
    return pl.pallas_call(
        body,
        name=name,
        grid=(rows // tile,),
        in_specs=[spec] * 4,
        out_specs=[spec] * 3,
        out_shape=[jax.ShapeDtypeStruct((rows, cols), F32)] * 3,
        compiler_params=_params(("parallel",)),
    )(w, g, m, v)


_SMALL = (
    ("v_ln_g", (D_GMLP,), 8),
    ("v_ln_b", (D_GMLP,), 8),
    ("w_spatial", (N_HEADS, CHUNK, CHUNK), 1024),
    ("b_spatial", (N_HEADS, CHUNK), 8),
    ("sinks", (N_HEADS,), 8),
    ("ln1_g", (D_MODEL,), 8),
    ("ln1_b", (D_MODEL,), 8),
    ("ln2_g", (D_MODEL,), 8),
    ("ln2_b", (D_MODEL,), 8),
)


def _pack_small(values):
    parts = []
    for (name, shape, rows), val in zip(_SMALL, values, strict=True):
        flat = val.reshape(-1).astype(F32)
        parts.append(jnp.pad(flat, (0, rows * LANES - flat.shape[0])).reshape(rows, LANES))
    return jnp.concatenate(parts, axis=0)


def _unpack_small(slab):
    out, row = [], 0
    for name, shape, rows in _SMALL:
        size = math.prod(shape)
        out.append(slab[row : row + rows].reshape(-1)[:size].reshape((1,) + shape))
        row += rows
    return out


def kernel(x, positions, w_in, v_ln_g, v_ln_b, w_spatial, b_spatial, sinks, w_out, ln1_g, ln1_b, w_ff1, w_ff2, ln2_g, ln2_b, loss_target, m_w_in, m_v_ln_g, m_v_ln_b, m_w_spatial, m_b_spatial, m_sinks, m_w_out, m_ln1_g, m_ln1_b, m_w_ff1, m_w_ff2, m_ln2_g, m_ln2_b, v_w_in, v_v_ln_g, v_v_ln_b, v_w_spatial, v_b_spatial, v_sinks, v_w_out, v_ln1_g, v_ln1_b, v_w_ff1, v_w_ff2, v_ln2_g, v_ln2_b):
    t = x.shape[1]
    x2 = x.reshape(t, D_MODEL)
    target = loss_target.reshape(t, D_MODEL)

    (w_in_t,) = _all_gather_weights([w_in[0].T.astype(BF16)])
    later = [w_out[0].astype(BF16), w_ff1[0].astype(BF16), w_ff2[0].astype(BF16)]
    later_rows = [s.shape[0] for s in later]
    ici_plan, pass_plan = _gather_plans(later_rows)
    bufs, started, ici_send, ici_recv = _split_call(
        "gather_start", later + [_landing(N_CHIPS * r, D_MODEL, BF16) for r in later_rows], start=ici_plan)

    inv_freq = ROPE_THETA ** (-jnp.arange(0, HEAD_DIM, 2, dtype=F32) / HEAD_DIM)
    cos, sin = _rope_tables(positions.reshape(t, 1), jnp.tile(inv_freq, LANES // (HEAD_DIM // 2)).reshape(1, LANES))
    u, vg, q, k, va = _in_proj(x2, w_in_t, cos, sin, dep=started)
    bias_full = jnp.repeat(b_spatial[0].T, HEAD_DIM, axis=1)
    sink_vec = sinks.reshape(N_HEADS)
    cat = _mixer_fwd(u, vg, q, k, va, v_ln_g, v_ln_b, w_spatial[0], bias_full, sink_vec)
    bufs, passed, pass_send, pass_recv = _split_call("gather_pass", bufs, wait=(ici_plan, ici_send, ici_recv), start=pass_plan, after=cat)
    bufs, _ = _split_call("gather_end", bufs, wait=(pass_plan, pass_send, pass_recv), after=passed)
    w_out_all = bufs[3]
    w1_all = bufs[4].reshape(N_FF_BLOCKS, D_MODEL, D_MODEL)
    w2_all = bufs[5].reshape(N_FF_BLOCKS, D_MODEL, D_MODEL)
    xhat1, rstd1 = _out_proj_ln1(cat, x2, w_out_all)
    r, dz2, d_ln2_g, d_ln2_b, sq_err = _ffn_fwd_loss(xhat1, ln1_g, ln1_b, w1_all, w2_all, ln2_g, ln2_b, target)
    loss = lax.psum(0.5 * jnp.sum(sq_err) / D_MODEL, ("x", "y", "c"))

    pos = jnp.stack([lax.axis_index("c"), 2 * lax.axis_index("x") + lax.axis_index("y")]).astype(jnp.int32)
    g_ff2_local = _grad_w_ff2(r, dz2)
    dpre, dz1, d_ln1_g, d_ln1_b = _ffn_bwd_ln1(dz2, r, xhat1, rstd1, ln1_g, w1_all, w2_all)
    g_ff1_local = _grad_w_ff1(xhat1, ln1_g, ln1_b, dpre)
    ff_grads = [g_ff1_local, g_ff2_local]
    ff_rows = [g.shape[0] // N_CHIPS for g in ff_grads]
    swap_plan = _swap_plan(ff_rows)
    bufs, swapping, swap_send, swap_recv = _split_call(
        "ff_swap_start", ff_grads + [_landing(g.shape[0] // 2, D_MODEL, F32) for g in ff_grads], start=swap_plan)
    dcat = _dcat(dz1, w_out_all, dep=swapping)
    g_out_local = _grad_w_out(cat, dz1, dep=dcat)
    bufs, _ = _split_call("ff_swap_wait", bufs, wait=(swap_plan, swap_send, swap_recv), after=g_out_local)
    ff_sums = [_pair_sum("grad_pair_sum_" + nm, g, th, pos) for nm, g, th in zip(["w_ff1", "w_ff2"], bufs[:2], bufs[2:])]
    ff_halves = [p.shape[0] // N_CHIPS for p, _ in ff_sums]
    exchange_plan = _exchange_plan(ff_halves)
    bufs, exchanging, ex_send, ex_recv = _split_call(
        "ff_exchange_start", [p for p, _ in ff_sums] + [_landing(3 * h, D_MODEL, BF16) for h in ff_halves], start=exchange_plan)
    dh_main, dkv, d_v_ln_g, d_v_ln_b, d_w_spatial, d_b_spatial_t, d_sinks = _mixer_bwd(
        u, vg, q, k, va, dcat, cos, sin, v_ln_g, v_ln_b, w_spatial[0], bias_full, sink_vec, dep=exchanging)
    grad_x = _grad_x(dh_main, dkv, dz1, w_in_t).reshape(1, t, D_MODEL)
    g_in_local = _grad_w_in_t(dh_main, dkv, x2)
    bufs, _ = _split_call("ff_exchange_wait", bufs, wait=(exchange_plan, ex_send, ex_recv), after=g_in_local)
    ff_shards = [_chip_sum("grad_chip_sum_" + nm, own, ld, pos) for nm, (_, own), ld in zip(["w_ff1", "w_ff2"], ff_sums, bufs[2:])]

    small = [g_in_local, g_out_local]
    theirs = _pair_swap(small)
    pair_sums = [_pair_sum("grad_pair_sum_" + nm, g, th, pos) for nm, g, th in zip(["w_in", "w_out"], small, theirs)]
    landed = _chip_exchange([p for p, _ in pair_sums])
    shards = [_chip_sum("grad_chip_sum_" + nm, own, ld, pos) for nm, (_, own), ld in zip(["w_in", "w_out"], pair_sums, landed)]
    g_w_in_t, g_w_out, g_w_ff1, g_w_ff2 = _pair_gather(shards + ff_shards)
    g_w_in = g_w_in_t.T

    small_g = _all_reduce_small(_pack_small(
        [d_v_ln_g, d_v_ln_b, d_w_spatial, d_b_spatial_t[:, :N_HEADS].T, d_sinks[0, :N_HEADS], d_ln1_g, d_ln1_b, d_ln2_g, d_ln2_b]))
    small_grads = _unpack_small(small_g)

    d_w_in, nm_w_in, nv_w_in = _adamw("adamw_w_in", w_in[0], g_w_in, m_w_in[0], v_w_in[0])
    d_w_out, nm_w_out, nv_w_out = _adamw("adamw_w_out", w_out[0], g_w_out, m_w_out[0], v_w_out[0])
    d_w_ff1, nm_w_ff1, nv_w_ff1 = _adamw("adamw_w_ff1", w_ff1[0], g_w_ff1, m_w_ff1[0], v_w_ff1[0])
    d_w_ff2, nm_w_ff2, nv_w_ff2 = _adamw("adamw_w_ff2", w_ff2[0], g_w_ff2, m_w_ff2[0], v_w_ff2[0])
    small_w = _pack_small([v_ln_g, v_ln_b, w_spatial, b_spatial, sinks, ln1_g, ln1_b, ln2_g, ln2_b])
    small_m = _pack_small([m_v_ln_g, m_v_ln_b, m_w_spatial, m_b_spatial, m_sinks, m_ln1_g, m_ln1_b, m_ln2_g, m_ln2_b])
    small_v = _pack_small([v_v_ln_g, v_v_ln_b, v_w_spatial, v_b_spatial, v_sinks, v_ln1_g, v_ln1_b, v_ln2_g, v_ln2_b])
    small_d, small_nm, small_nv = (_unpack_small(s) for s in _adamw("adamw_small", small_w, small_g, small_m, small_v))

    def with_big(small, w_in_v, w_out_v, w_ff1_v, w_ff2_v):
        g_vg, g_vb, g_ws, g_bs, g_sk, g_1g, g_1b, g_2g, g_2b = small
        return [w_in_v[None], g_vg, g_vb, g_ws, g_bs, g_sk, w_out_v[None], g_1g, g_1b, w_ff1_v[None], w_ff2_v[None], g_2g, g_2b]

    return (
        loss,
        grad_x,
        *with_big(small_grads, g_w_in, g_w_out, g_w_ff1, g_w_ff2),
        *with_big(small_d, d_w_in, d_w_out, d_w_ff1, d_w_ff2),
        *with_big(small_nm, nm_w_in, nm_w_out, nm_w_ff1, nm_w_ff2),
        *with_big(small_nv, nv_w_in, nv_w_out, nv_w_ff1, nv_w_ff2),
    )
```

```python
import functools
import math

import jax
import jax.numpy as jnp
from jax import lax
from jax.experimental import pallas as pl
from jax.experimental.pallas import tpu as pltpu

F32 = jnp.float32
BF16 = jnp.bfloat16

D_MODEL = 1024
HEAD_DIM = 64
D_GMLP = 512
D_ATTN = 512
D_KV = 128
D_IN = 2 * D_GMLP + D_ATTN + 2 * D_KV
D_MAIN = 2 * D_GMLP + D_ATTN
N_HEADS = 8
CHUNK = 128
ROPE_THETA = 10000.0
D_FF = 4 * D_MODEL
N_FF_BLOCKS = 4
LN_EPS = 1e-5
ALPHA = (2.0 * 1) ** 0.25
NEG_INF = -1e30
SCALE = 1.0 / math.sqrt(HEAD_DIM)

ADAM_LR = 0.001
ADAM_B1 = 0.9
ADAM_B2 = 0.999
ADAM_EPS = 1e-08
ADAM_WD = 0.01
ADAM_STEP = 10

N_CHIPS = 4
LANES = 128
V7X_VMEM_BYTES = 64 * 1024 * 1024
VMEM_LIMIT = V7X_VMEM_BYTES - 8 * 1024 * 1024
TM = 512
TM_FFN = 256
TK = 512
SMALL_ROWS = 1152
MESH = pl.DeviceIdType.MESH

NT = (((1,), (1,)), ((), ()))
TN = (((0,), (0,)), ((), ()))


def _dot(a, b, dims=None):
    if dims is None:
        return jnp.dot(a, b, preferred_element_type=F32)
    return lax.dot_general(a, b, dims, preferred_element_type=F32)


def _params(semantics=None):
    return pltpu.CompilerParams(dimension_semantics=semantics, vmem_limit_bytes=VMEM_LIMIT)


def _const_spec(shape, single_buffer=False):
    zeros = (0,) * len(shape)
    if single_buffer:
        return pl.BlockSpec(shape, lambda *_: zeros, pipeline_mode=pl.Buffered(1))
    return pl.BlockSpec(shape, lambda *_: zeros)


def _row_spec(rows, cols):
    return pl.BlockSpec((rows, cols), lambda i: (i, 0))


def _after(dep, body, in_specs, operands):
    if dep is None:
        return body, list(in_specs), list(operands)
    return (lambda dep_ref, *refs: body(*refs)), [pl.BlockSpec(memory_space=pl.ANY)] + list(in_specs), [dep] + list(operands)


def _gelu(x):
    k = math.sqrt(2.0 / math.pi)
    return 0.5 * x * (1.0 + jnp.tanh(k * (x + 0.044715 * (x * x * x))))


def _gelu_and_grad(x):
    k = math.sqrt(2.0 / math.pi)
    x2 = x * x
    t = jnp.tanh(k * (x + 0.044715 * (x2 * x)))
    g = 0.5 * x * (1.0 + t)
    dg = 0.5 * (1.0 + t) + 0.5 * x * (1.0 - t * t) * (k * (1.0 + 3.0 * 0.044715 * x2))
    return g, dg


def _layer_norm_stats(z):
    mu = jnp.mean(z, axis=-1, keepdims=True)
    zc = z - mu
    var = jnp.mean(zc * zc, axis=-1, keepdims=True)
    rstd = lax.rsqrt(var + LN_EPS)
    return zc * rstd, rstd


def _layer_norm_bwd(dxhat, xhat, rstd):
    m1 = jnp.mean(dxhat, axis=-1, keepdims=True)
    m2 = jnp.mean(dxhat * xhat, axis=-1, keepdims=True)
    return rstd * (dxhat - m1 - xhat * m2)


def _rotate_half(t):
    n = t.shape[1]
    lane = lax.broadcasted_iota(jnp.int32, t.shape, 1)
    first = (lane & (HEAD_DIM // 2)) == 0
    return jnp.where(first, -pltpu.roll(t, n - HEAD_DIM // 2, 1), pltpu.roll(t, HEAD_DIM // 2, 1))


def _rope(t, cos, sin):
    return t * cos + _rotate_half(t) * sin


def _rope_transposed(g, cos, sin):
    return g * cos - _rotate_half(g * sin)


def _lane_tile(a, reps):
    return jnp.tile(a, (1, reps)) if reps > 1 else a


def _rope_tables(pos_col, inv_freq_row):
    t = pos_col.shape[0]

    def body(pos_ref, f_ref, cos_ref, sin_ref):
        ang = pos_ref[...].astype(F32) * f_ref[...]
        cos_ref[...] = jnp.cos(ang)
        sin_ref[...] = jnp.sin(ang)

    return pl.pallas_call(
        body,
        name="rope_tables",
        grid=(t // TM,),
        in_specs=[_row_spec(TM, 1), _const_spec((1, LANES))],
        out_specs=[_row_spec(TM, LANES), _row_spec(TM, LANES)],
        out_shape=[jax.ShapeDtypeStruct((t, LANES), F32)] * 2,
        compiler_params=_params(("parallel",)),
    )(pos_col, inv_freq_row)


def _in_proj(x, w_in_t, cos, sin, dep=None):
    t = x.shape[0]

    def body(x_ref, w_ref, cos_ref, sin_ref, u_ref, vg_ref, q_ref, k_ref, va_ref):
        xb = x_ref[...].astype(BF16)
        u_ref[...] = _dot(xb, w_ref[0:D_GMLP, :], NT)
        vg_ref[...] = _dot(xb, w_ref[D_GMLP : 2 * D_GMLP, :], NT)
        q = _dot(xb, w_ref[2 * D_GMLP : D_MAIN, :], NT)
        k = _dot(xb, w_ref[D_MAIN : D_MAIN + D_KV, :], NT)
        va_ref[...] = _dot(xb, w_ref[D_MAIN + D_KV : D_IN, :], NT).astype(BF16)
        c, s = cos_ref[...], sin_ref[...]
        q_ref[...] = _rope(q, _lane_tile(c, D_ATTN // LANES), _lane_tile(s, D_ATTN // LANES)).astype(BF16)
        k_ref[...] = _rope(k, c, s).astype(BF16)

    body, in_specs, operands = _after(
        dep, body, [_row_spec(TM, D_MODEL), _const_spec((D_IN, D_MODEL)), _row_spec(TM, LANES), _row_spec(TM, LANES)], [x, w_in_t, cos, sin])
    return pl.pallas_call(
        body,
        name="in_proj",
        grid=(t // TM,),
        in_specs=in_specs,
        out_specs=[_row_spec(TM, D_GMLP), _row_spec(TM, D_GMLP), _row_spec(TM, D_ATTN), _row_spec(TM, D_KV), _row_spec(TM, D_KV)],
        out_shape=[
            jax.ShapeDtypeStruct((t, D_GMLP), F32),
            jax.ShapeDtypeStruct((t, D_GMLP), F32),
            jax.ShapeDtypeStruct((t, D_ATTN), BF16),
            jax.ShapeDtypeStruct((t, D_KV), BF16),
            jax.ShapeDtypeStruct((t, D_KV), BF16),
        ],
        compiler_params=_params(("parallel",)),
    )(*operands)


def _chunk_specs():
    cur = lambda i: (i, 0)
    prev = lambda i: (jnp.maximum(i - 1, 0), 0)
    return [
        pl.BlockSpec((CHUNK, D_GMLP), cur),
        pl.BlockSpec((CHUNK, D_GMLP), cur),
        pl.BlockSpec((CHUNK, D_ATTN), cur),
        pl.BlockSpec((CHUNK, D_KV), cur),
        pl.BlockSpec((CHUNK, D_KV), prev),
        pl.BlockSpec((CHUNK, D_KV), cur),
        pl.BlockSpec((CHUNK, D_KV), prev),
    ]


def _half_lane_masks(rows):
    lane = lax.broadcasted_iota(jnp.int32, (rows, LANES), 1)
    return lane < HEAD_DIM


def _kv_variants(kv2):
    left = _half_lane_masks(kv2.shape[0])
    f = kv2.astype(F32)
    swapped = pltpu.roll(f, HEAD_DIM, 1)
    zero = jnp.zeros_like(f)
    g0 = (jnp.where(left, f, zero).astype(BF16), jnp.where(left, zero, swapped).astype(BF16))
    g1 = (jnp.where(left, swapped, zero).astype(BF16), jnp.where(left, zero, f).astype(BF16))
    return (g0, g1)


def _band_mask(i):
    row = lax.broadcasted_iota(jnp.int32, (CHUNK, 2 * CHUNK), 0)
    col = lax.broadcasted_iota(jnp.int32, (CHUNK, 2 * CHUNK), 1)
    no_prev = jnp.where(i > 0, 0, 4 * CHUNK)
    in_prev = jnp.logical_and(col < CHUNK, (col - row) > no_prev)
    in_cur = jnp.logical_and(col >= CHUNK, (col - CHUNK) <= row)
    return jnp.logical_or(in_prev, in_cur)


def _softmax_with_sink(s, sink):
    m = jnp.maximum(jnp.max(s, axis=1, keepdims=True), sink)
    e = jnp.exp(s - m)
    e_sink = jnp.exp(sink - m)
    inv = 1.0 / (jnp.sum(e, axis=1, keepdims=True) + e_sink)
    return e * inv, e_sink * inv


def _spatial_weights(w_ref, pair):
    row = lax.broadcasted_iota(jnp.int32, (CHUNK, CHUNK), 0)
    col = lax.broadcasted_iota(jnp.int32, (CHUNK, CHUNK), 1)
    causal = col <= row
    wl = jnp.where(causal, w_ref[2 * pair], 0.0).astype(BF16)
    wr = jnp.where(causal, w_ref[2 * pair + 1], 0.0).astype(BF16)
    return wl, wr, causal


def _mixer_fwd(u, vg, q, k, va, v_ln_g, v_ln_b, w_spatial, bias_full, sinks):
    t = u.shape[0]

    def body(u_ref, vg_ref, q_ref, kc_ref, kp_ref, vc_ref, vp_ref, g_ref, b_ref, w_ref, bias_ref, sink_ref, cat_ref):
        i = pl.program_id(0)
        left = _half_lane_masks(CHUNK)
        ug = _gelu(u_ref[...])
        xhat, _ = _layer_norm_stats(_gelu(vg_ref[...]))
        vgl = xhat * g_ref[...] + b_ref[...]
        for p in range(D_GMLP // LANES):
            cols = slice(p * LANES, (p + 1) * LANES)
            xp = vgl[:, cols]
            wl, wr, _ = _spatial_weights(w_ref, p)
            mixed = _dot(wl, jnp.where(left, xp, 0.0).astype(BF16)) + _dot(wr, jnp.where(left, 0.0, xp).astype(BF16))
            cat_ref[:, cols] = (ug[:, cols] * (mixed + bias_ref[:, cols])).astype(BF16)

        k_var = _kv_variants(jnp.concatenate([kp_ref[...], kc_ref[...]], axis=0))
        v_var = _kv_variants(jnp.concatenate([vp_ref[...], vc_ref[...]], axis=0))
        valid = _band_mask(i)
        for p in range(D_ATTN // LANES):
            qp = q_ref[:, p * LANES : (p + 1) * LANES]
            group = p // 2
            out = jnp.zeros((CHUNK, LANES), F32)
            for side in range(2):
                s = jnp.where(valid, _dot(qp, k_var[group][side], NT) * SCALE, NEG_INF)
                probs, _ = _softmax_with_sink(s, sink_ref[2 * p + side])
                out = out + _dot(probs.astype(BF16), v_var[group][side])
            cat_ref[:, D_GMLP + p * LANES : D_GMLP + (p + 1) * LANES] = out.astype(BF16)

    return pl.pallas_call(
        body,
        name="mixer_fwd",
        grid=(t // CHUNK,),
        in_specs=_chunk_specs()
        + [
            _const_spec((1, D_GMLP)),
            _const_spec((1, D_GMLP)),
            _const_spec((N_HEADS, CHUNK, CHUNK)),
            _const_spec((CHUNK, D_GMLP)),
            pl.BlockSpec(memory_space=pltpu.SMEM),
        ],
        out_specs=pl.BlockSpec((CHUNK, D_MODEL), lambda i: (i, 0)),
        out_shape=jax.ShapeDtypeStruct((t, D_MODEL), BF16),
        compiler_params=_params(("parallel",)),
    )(u, vg, q, k, k, va, va, v_ln_g, v_ln_b, w_spatial, bias_full, sinks)


def _out_proj_ln1(cat, x, w_out):
    t = x.shape[0]

    def body(cat_ref, x_ref, w_ref, xhat_ref, rstd_ref):
        z = ALPHA * x_ref[...] + _dot(cat_ref[...], w_ref[...])
        xhat, rstd = _layer_norm_stats(z)
        xhat_ref[...] = xhat
        rstd_ref[...] = rstd

    return pl.pallas_call(
        body,
        name="out_proj_ln1",
        grid=(t // TM,),
        in_specs=[_row_spec(TM, D_MODEL), _row_spec(TM, D_MODEL), _const_spec((D_MODEL, D_MODEL))],
        out_specs=[_row_spec(TM, D_MODEL), _row_spec(TM, 1)],
        out_shape=[jax.ShapeDtypeStruct((t, D_MODEL), F32), jax.ShapeDtypeStruct((t, 1), F32)],
        compiler_params=_params(("parallel",)),
    )(cat, x, w_out)


def _ffn_fwd_loss(xhat1, ln1_g, ln1_b, w1, w2, ln2_g, ln2_b, target):
    t = xhat1.shape[0]

    def body(xh_ref, g1_ref, b1_ref, w1_ref, w2_ref, g2_ref, b2_ref, tgt_ref, r_ref, dz2_ref, dg2_ref, db2_ref, sq_ref):
        @pl.when(pl.program_id(0) == 0)
        def _():
            dg2_ref[...] = jnp.zeros_like(dg2_ref)
            db2_ref[...] = jnp.zeros_like(db2_ref)
            sq_ref[...] = jnp.zeros_like(sq_ref)

        x1 = xh_ref[...] * g1_ref[...] + b1_ref[...]
        x1b = x1.astype(BF16)
        ff = jnp.zeros((TM_FFN, D_MODEL), F32)
        for j in range(N_FF_BLOCKS):
            r = jnp.maximum(_dot(x1b, w1_ref[j]), 0.0)
            r_ref[:, j * D_MODEL : (j + 1) * D_MODEL] = r.astype(BF16)
            ff = ff + _dot((r * r).astype(BF16), w2_ref[j])
        xhat2, rstd2 = _layer_norm_stats(ALPHA * x1 + ff)
        err = xhat2 * g2_ref[...] + b2_ref[...] - tgt_ref[...]
        sq_ref[...] += jnp.sum(err * err, axis=0, keepdims=True)
        dy = err * (1.0 / D_MODEL)
        dg2_ref[...] += jnp.sum(dy * xhat2, axis=0, keepdims=True)
        db2_ref[...] += jnp.sum(dy, axis=0, keepdims=True)
        dz2_ref[...] = _layer_norm_bwd(dy * g2_ref[...], xhat2, rstd2)

    vec = _const_spec((1, D_MODEL))
    wspec = _const_spec((N_FF_BLOCKS, D_MODEL, D_MODEL), single_buffer=True)
    return pl.pallas_call(
        body,
        name="ffn_fwd_loss",
        grid=(t // TM_FFN,),
        in_specs=[_row_spec(TM_FFN, D_MODEL), vec, vec, wspec, wspec, vec, vec, _row_spec(TM_FFN, D_MODEL)],
        out_specs=[_row_spec(TM_FFN, D_FF), _row_spec(TM_FFN, D_MODEL), vec, vec, vec],
        out_shape=[
            jax.ShapeDtypeStruct((t, D_FF), BF16),
            jax.ShapeDtypeStruct((t, D_MODEL), F32),
            jax.ShapeDtypeStruct((1, D_MODEL), F32),
            jax.ShapeDtypeStruct((1, D_MODEL), F32),
            jax.ShapeDtypeStruct((1, D_MODEL), F32),
        ],
        compiler_params=_params(("arbitrary",)),
    )(xhat1, ln1_g, ln1_b, w1, w2, ln2_g, ln2_b, target)


def _ffn_bwd_ln1(dz2, r, xhat1, rstd1, ln1_g, w1, w2):
    t = dz2.shape[0]

    def body(dz2_ref, r_ref, xh_ref, rstd_ref, g1_ref, w1_ref, w2_ref, dpre_ref, dz1_ref, dg1_ref, db1_ref):
        @pl.when(pl.program_id(0) == 0)
        def _():
            dg1_ref[...] = jnp.zeros_like(dg1_ref)
            db1_ref[...] = jnp.zeros_like(db1_ref)

        dz2 = dz2_ref[...]
        dz2b = dz2.astype(BF16)
        dx1 = ALPHA * dz2
        for j in range(N_FF_BLOCKS):
            cols = slice(j * D_MODEL, (j + 1) * D_MODEL)
            dpre = (_dot(dz2b, w2_ref[j], NT) * (2.0 * r_ref[:, cols].astype(F32))).astype(BF16)
            dpre_ref[:, cols] = dpre
            dx1 = dx1 + _dot(dpre, w1_ref[j], NT)
        xhat1 = xh_ref[...]
        dg1_ref[...] += jnp.sum(dx1 * xhat1, axis=0, keepdims=True)
        db1_ref[...] += jnp.sum(dx1, axis=0, keepdims=True)
        dz1_ref[...] = _layer_norm_bwd(dx1 * g1_ref[...], xhat1, rstd_ref[...])

    vec = _const_spec((1, D_MODEL))
    wspec = _const_spec((N_FF_BLOCKS, D_MODEL, D_MODEL), single_buffer=True)
    return pl.pallas_call(
        body,
        name="ffn_bwd_ln1",
        grid=(t // TM_FFN,),
        in_specs=[_row_spec(TM_FFN, D_MODEL), _row_spec(TM_FFN, D_FF), _row_spec(TM_FFN, D_MODEL), _row_spec(TM_FFN, 1), vec, wspec, wspec],
        out_specs=[_row_spec(TM_FFN, D_FF), _row_spec(TM_FFN, D_MODEL), vec, vec],
        out_shape=[
            jax.ShapeDtypeStruct((t, D_FF), BF16),
            jax.ShapeDtypeStruct((t, D_MODEL), F32),
            jax.ShapeDtypeStruct((1, D_MODEL), F32),
            jax.ShapeDtypeStruct((1, D_MODEL), F32),
        ],
        compiler_params=_params(("arbitrary",)),
    )(dz2, r, xhat1, rstd1, ln1_g, w1, w2)


def _dcat(dz1, w_out, dep=None):
    t = dz1.shape[0]

    def body(dz1_ref, w_ref, dcat_ref):
        dcat_ref[...] = _dot(dz1_ref[...].astype(BF16), w_ref[...], NT)

    body, in_specs, operands = _after(dep, body, [_row_spec(TM, D_MODEL), _const_spec((D_MODEL, D_MODEL))], [dz1, w_out])
    return pl.pallas_call(
        body,
        name="dcat",
        grid=(t // TM,),
        in_specs=in_specs,
        out_specs=_row_spec(TM, D_MODEL),
        out_shape=jax.ShapeDtypeStruct((t, D_MODEL), F32),
        compiler_params=_params(("parallel",)),
    )(*operands)


def _mixer_bwd(u, vg, q, k, va, dcat, cos, sin, v_ln_g, v_ln_b, w_spatial, bias_full, sinks, dep=None):
    t = u.shape[0]
    n_chunks = t // CHUNK

    def body(u_ref, vg_ref, q_ref, kc_ref, kp_ref, vc_ref, vp_ref, dcat_ref, cosc_ref, sinc_ref, cosp_ref, sinp_ref,
             g_ref, b_ref, w_ref, bias_ref, sink_ref,
             dmain_ref, dkv_ref, dg_ref, db_ref, dw_ref, dbs_ref, dsink_ref, dmix_acc):
        i = pl.program_id(0)
        left = _half_lane_masks(CHUNK)
        lane = lax.broadcasted_iota(jnp.int32, (CHUNK, LANES), 1)

        @pl.when(i == 0)
        def _():
            dg_ref[...] = jnp.zeros_like(dg_ref)
            db_ref[...] = jnp.zeros_like(db_ref)
            dw_ref[...] = jnp.zeros_like(dw_ref)
            dsink_ref[...] = jnp.zeros_like(dsink_ref)
            dmix_acc[...] = jnp.zeros_like(dmix_acc)

        ug, dug_du = _gelu_and_grad(u_ref[...])
        gv, dgv_dv = _gelu_and_grad(vg_ref[...])
        xhat, rstd = _layer_norm_stats(gv)
        gain = g_ref[...]
        vgl = xhat * gain + b_ref[...]
        dvgl_parts = []
        for p in range(D_GMLP // LANES):
            cols = slice(p * LANES, (p + 1) * LANES)
            xp = vgl[:, cols]
            xl = jnp.where(left, xp, 0.0).astype(BF16)
            xr = jnp.where(left, 0.0, xp).astype(BF16)
            wl, wr, causal = _spatial_weights(w_ref, p)
            mixed = _dot(wl, xl) + _dot(wr, xr) + bias_ref[:, cols]
            da = dcat_ref[:, cols]
            dmain_ref[:, cols] = (da * mixed * dug_du[:, cols]).astype(BF16)
            dmixed = da * ug[:, cols]
            dmix_acc[:, cols] += dmixed
            dml = jnp.where(left, dmixed, 0.0).astype(BF16)
            dmr = jnp.where(left, 0.0, dmixed).astype(BF16)
            xpb = xp.astype(BF16)
            dw_ref[2 * p] += jnp.where(causal, _dot(dml, xpb, NT), 0.0)
            dw_ref[2 * p + 1] += jnp.where(causal, _dot(dmr, xpb, NT), 0.0)
            dvgl_parts.append(_dot(wl, dml, TN) + _dot(wr, dmr, TN))
        dvgl = jnp.concatenate(dvgl_parts, axis=1)
        dg_ref[...] += jnp.sum(dvgl * xhat, axis=0, keepdims=True)
        db_ref[...] += jnp.sum(dvgl, axis=0, keepdims=True)
        dgv = _layer_norm_bwd(dvgl * gain, xhat, rstd)
        dmain_ref[:, D_GMLP : 2 * D_GMLP] = (dgv * dgv_dv).astype(BF16)

        @pl.when(i == n_chunks - 1)
        def _():
            tile = jnp.zeros((CHUNK, LANES), F32)
            for p in range(D_GMLP // LANES):
                dm = dmix_acc[:, p * LANES : (p + 1) * LANES]
                sl = jnp.sum(jnp.where(left, dm, 0.0), axis=1, keepdims=True)
                sr = jnp.sum(jnp.where(left, 0.0, dm), axis=1, keepdims=True)
                tile = jnp.where(lane == 2 * p, sl, tile)
                tile = jnp.where(lane == 2 * p + 1, sr, tile)
            dbs_ref[...] = tile

        k_var = _kv_variants(jnp.concatenate([kp_ref[...], kc_ref[...]], axis=0))
        v_var = _kv_variants(jnp.concatenate([vp_ref[...], vc_ref[...]], axis=0))
        valid = _band_mask(i)
        dk_acc = [[None, None], [None, None]]
        dv_acc = [[None, None], [None, None]]
        dsink_row = jnp.zeros((1, LANES), F32)
        lane_row = lax.broadcasted_iota(jnp.int32, (1, LANES), 1)
        dq_parts = []
        for p in range(D_ATTN // LANES):
            qp = q_ref[:, p * LANES : (p + 1) * LANES]
            do = dcat_ref[:, D_GMLP + p * LANES : D_GMLP + (p + 1) * LANES]
            dob = do.astype(BF16)
            group = p // 2
            dq = jnp.zeros((CHUNK, LANES), F32)
            for side in range(2):
                head = 2 * p + side
                kx, vx = k_var[group][side], v_var[group][side]
                s = jnp.where(valid, _dot(qp, kx, NT) * SCALE, NEG_INF)
                probs, p_sink = _softmax_with_sink(s, sink_ref[head])
                dp = _dot(dob, vx, NT)
                dsum = jnp.sum(probs * dp, axis=1, keepdims=True)
                ds = (probs * (dp - dsum) * SCALE).astype(BF16)
                dsink_row = dsink_row + jnp.where(lane_row == head, -jnp.sum(p_sink * dsum, axis=0, keepdims=True), 0.0)
                dq = dq + _dot(ds, kx)
                in_side = left if side == 0 else jnp.logical_not(left)
                q_side = jnp.where(in_side, qp.astype(F32), 0.0).astype(BF16)
                do_side = jnp.where(in_side, do, 0.0).astype(BF16)
                dk_h = _dot(ds, q_side, TN)
                dv_h = _dot(probs.astype(BF16), do_side, TN)
                dk_acc[group][side] = dk_h if dk_acc[group][side] is None else dk_acc[group][side] + dk_h
                dv_acc[group][side] = dv_h if dv_acc[group][side] is None else dv_acc[group][side] + dv_h
            dq_parts.append(dq)
        dsink_ref[0:1, :] += dsink_row
        cos_c, sin_c = cosc_ref[...], sinc_ref[...]
        dq_all = jnp.concatenate(dq_parts, axis=1)
        reps = D_ATTN // LANES
        dmain_ref[:, 2 * D_GMLP : D_MAIN] = _rope_transposed(dq_all, _lane_tile(cos_c, reps), _lane_tile(sin_c, reps)).astype(BF16)

        def group_lanes(acc):
            return acc[0][0] + pltpu.roll(acc[0][1], HEAD_DIM, 1) + pltpu.roll(acc[1][0], HEAD_DIM, 1) + acc[1][1]

        dk2 = group_lanes(dk_acc)
        dv2 = group_lanes(dv_acc)
        cur = pl.ds(pl.multiple_of(i * CHUNK, CHUNK), CHUNK)
        dkv_ref[cur, 0:D_KV] = _rope_transposed(dk2[CHUNK:], cos_c, sin_c)
        dkv_ref[cur, D_KV : 2 * D_KV] = dv2[CHUNK:]

        @pl.when(i > 0)
        def _():
            prev = pl.ds(pl.multiple_of((i - 1) * CHUNK, CHUNK), CHUNK)
            dkv_ref[prev, 0:D_KV] += _rope_transposed(dk2[:CHUNK], cosp_ref[...], sinp_ref[...])
            dkv_ref[prev, D_KV : 2 * D_KV] += dv2[:CHUNK]

    cur = lambda i: (i, 0)
    prev = lambda i: (jnp.maximum(i - 1, 0), 0)
    in_specs = _chunk_specs() + [
        pl.BlockSpec((CHUNK, D_MODEL), cur),
        pl.BlockSpec((CHUNK, LANES), cur),
        pl.BlockSpec((CHUNK, LANES), cur),
        pl.BlockSpec((CHUNK, LANES), prev),
        pl.BlockSpec((CHUNK, LANES), prev),
        _const_spec((1, D_GMLP)),
        _const_spec((1, D_GMLP)),
        _const_spec((N_HEADS, CHUNK, CHUNK)),
        _const_spec((CHUNK, D_GMLP)),
        pl.BlockSpec(memory_space=pltpu.SMEM),
    ]
    body, in_specs, operands = _after(
        dep, body, in_specs, [u, vg, q, k, k, va, va, dcat, cos, sin, cos, sin, v_ln_g, v_ln_b, w_spatial, bias_full, sinks])
    return pl.pallas_call(
        body,
        name="mixer_bwd",
        grid=(n_chunks,),
        in_specs=in_specs,
        out_specs=[
            pl.BlockSpec((CHUNK, D_MAIN), cur),
            _const_spec((t, 2 * D_KV)),
            _const_spec((1, D_GMLP)),
            _const_spec((1, D_GMLP)),
            _const_spec((N_HEADS, CHUNK, CHUNK)),
            _const_spec((CHUNK, LANES)),
            _const_spec((8, LANES)),
        ],
        out_shape=[
            jax.ShapeDtypeStruct((t, D_MAIN), BF16),
            jax.ShapeDtypeStruct((t, 2 * D_KV), F32),
            jax.ShapeDtypeStruct((1, D_GMLP), F32),
            jax.ShapeDtypeStruct((1, D_GMLP), F32),
            jax.ShapeDtypeStruct((N_HEADS, CHUNK, CHUNK), F32),
            jax.ShapeDtypeStruct((CHUNK, LANES), F32),
            jax.ShapeDtypeStruct((8, LANES), F32),
        ],
        scratch_shapes=[pltpu.VMEM((CHUNK, D_GMLP), F32)],
        compiler_params=_params(("arbitrary",)),
    )(*operands)


def _grad_x(dh_main, dkv, dz1, w_in_t, dep=None):
    t = dz1.shape[0]

    def body(dm_ref, dkv_ref, dz1_ref, w_ref, gx_ref):
        acc = ALPHA * dz1_ref[...] + _dot(dm_ref[...], w_ref[0:D_MAIN, :])
        gx_ref[...] = acc + _dot(dkv_ref[...].astype(BF16), w_ref[D_MAIN:D_IN, :])

    body, in_specs, operands = _after(
        dep, body, [_row_spec(TM, D_MAIN), _row_spec(TM, 2 * D_KV), _row_spec(TM, D_MODEL), _const_spec((D_IN, D_MODEL))], [dh_main, dkv, dz1, w_in_t])
    return pl.pallas_call(
        body,
        name="grad_x",
        grid=(t // TM,),
        in_specs=in_specs,
        out_specs=_row_spec(TM, D_MODEL),
        out_shape=jax.ShapeDtypeStruct((t, D_MODEL), F32),
        compiler_params=_params(("parallel",)),
    )(*operands)


def _token_contraction(name, n_blocks, out_rows, in_arrays, in_specs, contributions, dep=None):
    t = in_arrays[0].shape[0]
    block_rows = out_rows // n_blocks

    def body(*refs):
        out_ref = refs[-1]

        @pl.when(pl.program_id(1) == 0)
        def _():
            out_ref[...] = jnp.zeros_like(out_ref)

        for row0, a, b in contributions(*refs[:-1]):
            out_ref[row0 : row0 + a.shape[1], :] += _dot(a, b, TN)

    body, in_specs, operands = _after(dep, body, in_specs, in_arrays)
    return pl.pallas_call(
        body,
        name=name,
        grid=(n_blocks, t // TK),
        in_specs=in_specs,
        out_specs=pl.BlockSpec((block_rows, D_MODEL), lambda j, k: (j, 0)),
        out_shape=jax.ShapeDtypeStruct((out_rows, D_MODEL), F32),
        compiler_params=_params(("parallel", "arbitrary")),
    )(*operands)


def _tile_spec(cols):
    return pl.BlockSpec((TK, cols), lambda j, k: (k, 0))


def _tile_block_spec():
    return pl.BlockSpec((TK, D_MODEL), lambda j, k: (k, j))


def _grad_w_in_t(dh_main, dkv, x):
    def contributions(dm_ref, dkv_ref, x_ref):
        xb = x_ref[...].astype(BF16)
        return [(0, dm_ref[...], xb), (D_MAIN, dkv_ref[...].astype(BF16), xb)]

    return _token_contraction("grad_w_in", 1, D_IN, [dh_main, dkv, x], [_tile_spec(D_MAIN), _tile_spec(2 * D_KV), _tile_spec(D_MODEL)], contributions)


def _grad_w_out(cat, dz1, dep=None):
    def contributions(cat_ref, dz1_ref):
        return [(0, cat_ref[...], dz1_ref[...].astype(BF16))]

    return _token_contraction("grad_w_out", 1, D_MODEL, [cat, dz1], [_tile_spec(D_MODEL), _tile_spec(D_MODEL)], contributions, dep)


def _grad_w_ff1(xhat1, ln1_g, ln1_b, dpre):
    def contributions(xh_ref, g_ref, b_ref, dpre_ref):
        return [(0, (xh_ref[...] * g_ref[...] + b_ref[...]).astype(BF16), dpre_ref[...])]

    vec = pl.BlockSpec((1, D_MODEL), lambda j, k: (0, 0))
    return _token_contraction("grad_w_ff1", N_FF_BLOCKS, D_FF, [xhat1, ln1_g, ln1_b, dpre], [_tile_spec(D_MODEL), vec, vec, _tile_block_spec()], contributions)


def _grad_w_ff2(r, dz2):
    def contributions(r_ref, dz2_ref):
        rf = r_ref[...].astype(F32)
        return [(0, (rf * rf).astype(BF16), dz2_ref[...].astype(BF16))]

    return _token_contraction("grad_w_ff2", N_FF_BLOCKS, D_FF, [r, dz2], [_tile_block_spec(), _tile_spec(D_MODEL)], contributions)


ANY = pl.BlockSpec(memory_space=pl.ANY)


def _mesh_position():
    return lax.axis_index("x"), lax.axis_index("y"), lax.axis_index("c")


def _other_chips(x, y):
    return [(1 - x, y), (x, 1 - y), (1 - x, 1 - y)]


def _remote(src, dst, send_sem, recv_sem, device):
    return pltpu.make_async_remote_copy(src_ref=src, dst_ref=dst, send_sem=send_sem, recv_sem=recv_sem, device_id=device, device_id_type=MESH)


def _rows(ref, start, size):
    return ref.at[pl.ds(start, size), :]


def _all_gather_weights(shards):
    n = len(shards)
    per = 7

    def body(*refs):
        ins, outs = refs[:n], refs[n : 2 * n]
        send_sems, recv_sems = refs[2 * n :]
        x, y, c = _mesh_position()
        me = 2 * x + y
        chips = _other_chips(x, y)
        sibling = (x, y, 1 - c)
        started = []
        for w in range(n):
            rows = shards[w].shape[0]
            half = rows // 2
            for kk, (px, py) in enumerate(chips):
                cp = _remote(_rows(ins[w], c * half, half), _rows(outs[w], me * rows + c * half, half),
                             send_sems.at[per * w + kk], recv_sems.at[per * w + kk], (px, py, c))
                cp.start()
                started.append(cp)
            cp = _remote(ins[w], _rows(outs[w], me * rows, rows), send_sems.at[per * w + 6], recv_sems.at[per * w + 6], sibling)
            cp.start()
            started.append(cp)
        for w in range(n):
            rows = shards[w].shape[0]
            half = rows // 2
            for kk, (px, py) in enumerate(chips):
                blk = _rows(outs[w], (2 * px + py) * rows + c * half, half)
                _remote(blk, blk, send_sems.at[per * w + kk], recv_sems.at[per * w + kk], (px, py, c)).wait_recv()
                fwd = _remote(blk, blk, send_sems.at[per * w + 3 + kk], recv_sems.at[per * w + 3 + kk], sibling)
                fwd.start()
                started.append(fwd)
        for w in range(n):
            rows = shards[w].shape[0]
            half = rows // 2
            for kk, (px, py) in enumerate(chips):
                blk = _rows(outs[w], (2 * px + py) * rows + (1 - c) * half, half)
                _remote(blk, blk, send_sems.at[per * w + 3 + kk], recv_sems.at[per * w + 3 + kk], sibling).wait_recv()
            own = _rows(outs[w], me * rows, rows)
            _remote(own, own, send_sems.at[per * w + 6], recv_sems.at[per * w + 6], sibling).wait_recv()
        for cp in started:
            cp.wait_send()

    return pl.pallas_call(
        body,
        name="all_gather_weights",
        in_specs=[ANY] * n,
        out_specs=[ANY] * n,
        out_shape=[jax.ShapeDtypeStruct((N_CHIPS * s.shape[0], s.shape[1]), s.dtype) for s in shards],
        scratch_shapes=[pltpu.SemaphoreType.DMA((per * n,)), pltpu.SemaphoreType.DMA((per * n,))],
    )(*shards)


def _pair_swap(grads):
    n = len(grads)

    def body(*refs):
        ins, theirs = refs[:n], refs[n : 2 * n]
        send_sems, recv_sems = refs[2 * n :]
        x, y, c = _mesh_position()
        sibling = (x, y, 1 - c)
        sends = []
        for w in range(n):
            rows = grads[w].shape[0] // N_CHIPS
            half = rows // 2
            for j in range(N_CHIPS):
                cp = _remote(_rows(ins[w], j * rows + (1 - c) * half, half), _rows(theirs[w], j * half, half),
                             send_sems.at[4 * w + j], recv_sems.at[4 * w + j], sibling)
                cp.start()
                sends.append(cp)
        for cp in sends:
            cp.wait_recv()
        for cp in sends:
            cp.wait_send()

    return pl.pallas_call(
        body,
        name="grad_pair_swap",
        in_specs=[ANY] * n,
        out_specs=[ANY] * n,
        out_shape=[jax.ShapeDtypeStruct((g.shape[0] // 2, g.shape[1]), g.dtype) for g in grads],
        scratch_shapes=[pltpu.SemaphoreType.DMA((4 * n,)), pltpu.SemaphoreType.DMA((4 * n,))],
    )(*grads)


def _chip_exchange(partials):
    n = len(partials)

    def body(*refs):
        ins, outs = refs[:n], refs[n : 2 * n]
        send_sems, recv_sems = refs[2 * n :]
        x, y, c = _mesh_position()
        chips = _other_chips(x, y)
        sends = []
        for w in range(n):
            half = partials[w].shape[0] // N_CHIPS
            for kk, (px, py) in enumerate(chips):
                cp = _remote(_rows(ins[w], (2 * px + py) * half, half), _rows(outs[w], kk * half, half),
                             send_sems.at[3 * w + kk], recv_sems.at[3 * w + kk], (px, py, c))
                cp.start()
                sends.append(cp)
        for cp in sends:
            cp.wait_recv()
        for cp in sends:
            cp.wait_send()

    return pl.pallas_call(
        body,
        name="grad_chip_exchange",
        in_specs=[ANY] * n,
        out_specs=[ANY] * n,
        out_shape=[jax.ShapeDtypeStruct((3 * p.shape[0] // N_CHIPS, p.shape[1]), p.dtype) for p in partials],
        scratch_shapes=[pltpu.SemaphoreType.DMA((3 * n,)), pltpu.SemaphoreType.DMA((3 * n,))],
    )(*partials)


def _pair_gather(shards):
    n = len(shards)

    def body(*refs):
        outs = refs[n : 2 * n]
        send_sems, recv_sems = refs[2 * n :]
        x, y, c = _mesh_position()
        sibling = (x, y, 1 - c)
        sends = []
        for w in range(n):
            half = shards[w].shape[0] // 2
            mine = _rows(outs[w], c * half, half)
            cp = _remote(mine, mine, send_sems.at[w], recv_sems.at[w], sibling)
            cp.start()
            sends.append(cp)
        for w in range(n):
            half = shards[w].shape[0] // 2
            blk = _rows(outs[w], (1 - c) * half, half)
            _remote(blk, blk, send_sems.at[w], recv_sems.at[w], sibling).wait_recv()
        for cp in sends:
            cp.wait_send()

    return pl.pallas_call(
        body,
        name="grad_pair_gather",
        in_specs=[ANY] * n,
        out_specs=[ANY] * n,
        out_shape=[jax.ShapeDtypeStruct(s.shape, s.dtype) for s in shards],
        input_output_aliases={w: w for w in range(n)},
        scratch_shapes=[pltpu.SemaphoreType.DMA((n,)), pltpu.SemaphoreType.DMA((n,))],
    )(*shards)


def _all_reduce_small(slab):
    rows = slab.shape[0]
    part = rows // 8

    def body(slab_ref, out_ref, landing, reduced, send_sems, recv_sems):
        x, y, c = _mesh_position()
        me = 4 * x + 2 * y + c
        flips = [(k >> 2, (k >> 1) & 1, k & 1) for k in range(1, 8)]

        def peer(flip):
            fx, fy, fc = flip
            return (1 - x if fx else x, 1 - y if fy else y, 1 - c if fc else c)

        def my_rows(ref):
            return ref.at[pl.ds(pl.multiple_of(me * part, 8), part), :]

        sends = []
        for kk, flip in enumerate(flips):
            px, py, pc = peer(flip)
            them = 4 * px + 2 * py + pc
            cp = _remote(slab_ref.at[pl.ds(pl.multiple_of(them * part, 8), part), :], landing.at[me], send_sems.at[kk], recv_sems.at[kk], (px, py, pc))
            cp.start()
            sends.append(cp)
        landing[me] = my_rows(slab_ref)[...]
        for kk, flip in enumerate(flips):
            px, py, pc = peer(flip)
            them = 4 * px + 2 * py + pc
            _remote(landing.at[them], landing.at[them], send_sems.at[kk], recv_sems.at[kk], (px, py, pc)).wait_recv()
        total = landing[0]
        for s in range(1, 8):
            total = total + landing[s]
        reduced[...] = total
        my_rows(out_ref)[...] = total
        for kk, flip in enumerate(flips):
            cp = _remote(reduced, my_rows(out_ref), send_sems.at[7 + kk], recv_sems.at[7 + kk], peer(flip))
            cp.start()
            sends.append(cp)
        for kk, flip in enumerate(flips):
            px, py, pc = peer(flip)
            them = 4 * px + 2 * py + pc
            blk = out_ref.at[pl.ds(pl.multiple_of(them * part, 8), part), :]
            _remote(blk, blk, send_sems.at[7 + kk], recv_sems.at[7 + kk], (px, py, pc)).wait_recv()
        for cp in sends:
            cp.wait_send()

    vmem = pl.BlockSpec(memory_space=pltpu.VMEM)
    return pl.pallas_call(
        body,
        name="all_reduce_small",
        in_specs=[vmem],
        out_specs=vmem,
        out_shape=jax.ShapeDtypeStruct(slab.shape, slab.dtype),
        scratch_shapes=[pltpu.VMEM((8, part, LANES), F32), pltpu.VMEM((part, LANES), F32), pltpu.SemaphoreType.DMA((14,)), pltpu.SemaphoreType.DMA((14,))],
    )(slab)


HBM = pl.BlockSpec(memory_space=pltpu.HBM)
SEM = pl.BlockSpec(memory_space=pltpu.SEMAPHORE)
DATAFLOW = pltpu.SideEffectType.DATAFLOW_SIDE_EFFECTING
TOKEN = jax.ShapeDtypeStruct((8, LANES), F32)


def _plan_copies(bufs, plan, send_sems, recv_sems):
    out = []
    for i, (src, src_row, dst, dst_row, recv_row, rows, device) in enumerate(plan):
        send = _remote(_rows(bufs[src], src_row, rows), _rows(bufs[dst], dst_row, rows), send_sems.at[i], recv_sems.at[i], device)
        landed = _rows(bufs[dst], recv_row, rows)
        recv = _remote(landed, landed, send_sems.at[i], recv_sems.at[i], device)
        out.append((send, recv))
    return out


def _split_call(name, bufs, wait=None, start=None, after=None):
    n = len(bufs)
    n_in = n + (2 if wait else 0) + (1 if after is not None else 0)
    n_start = len(start(0, 0, 0)) if start else 0

    def body(*refs):
        ins = refs[:n]
        x, y, c = _mesh_position()
        if wait:
            for send, recv in _plan_copies(ins, wait[0](x, y, c), refs[n], refs[n + 1]):
                recv.wait_recv()
                send.wait_send()
        if start:
            for send, _ in _plan_copies(ins, start(x, y, c), refs[n_in + n + 1], refs[n_in + n + 2]):
                send.start()
        token = refs[n_in + n]
        token[...] = jnp.zeros_like(token)

    operands = [pltpu.with_memory_space_constraint(b, pltpu.HBM) for b in bufs]
    in_specs = [HBM] * n
    if wait:
        operands += [wait[1], wait[2]]
        in_specs += [SEM, SEM]
    if after is not None:
        operands.append(after)
        in_specs.append(ANY)
    out_shape = [pltpu.HBM(b.shape, b.dtype) for b in bufs] + [TOKEN]
    out_specs = [HBM] * n + [pl.BlockSpec(memory_space=pltpu.VMEM)]
    if start:
        out_shape += [pltpu.SemaphoreType.DMA((n_start,)), pltpu.SemaphoreType.DMA((n_start,))]
        out_specs += [SEM, SEM]
    outs = pl.pallas_call(
        body,
        name=name,
        in_specs=in_specs,
        out_specs=out_specs,
        out_shape=out_shape,
        input_output_aliases={i: i for i in range(n)},
        compiler_params=pltpu.CompilerParams(has_side_effects=DATAFLOW),
    )(*operands)
    return (list(outs[:n]), outs[n]) + tuple(outs[n + 1 :])


def _gather_plans(shard_rows):
    n = len(shard_rows)

    def ici(x, y, c):
        me = 2 * x + y
        plan = []
        for w, rows in enumerate(shard_rows):
            half = rows // 2
            for px, py in _other_chips(x, y):
                plan.append((w, c * half, n + w, me * rows + c * half, (2 * px + py) * rows + c * half, half, (px, py, c)))
            plan.append((w, 0, n + w, me * rows, me * rows, rows, (x, y, 1 - c)))
        return plan

    def passed_on(x, y, c):
        plan = []
        for w, rows in enumerate(shard_rows):
            half = rows // 2
            for px, py in _other_chips(x, y):
                row = (2 * px + py) * rows
                plan.append((n + w, row + c * half, n + w, row + c * half, row + (1 - c) * half, half, (x, y, 1 - c)))
        return plan

    return ici, passed_on


def _swap_plan(block_rows):
    n = len(block_rows)

    def plan_fn(x, y, c):
        plan = []
        for w, rows in enumerate(block_rows):
            half = rows // 2
            for j in range(N_CHIPS):
                plan.append((w, j * rows + (1 - c) * half, n + w, j * half, j * half, half, (x, y, 1 - c)))
        return plan

    return plan_fn


def _exchange_plan(halves):
    n = len(halves)

    def plan_fn(x, y, c):
        plan = []
        for w, half in enumerate(halves):
            for kk, (px, py) in enumerate(_other_chips(x, y)):
                plan.append((w, (2 * px + py) * half, n + w, kk * half, kk * half, half, (px, py, c)))
        return plan

    return plan_fn


def _landing(rows, cols, dtype):
    return lax.empty((rows, cols), dtype)


def _row_tile(rows, cap=256):
    best = 8
    for cand in range(8, cap + 1, 8):
        if rows % cand == 0:
            best = cand
    return best


def _pair_sum(name, grad, theirs, pos):
    half = theirs.shape[0] // N_CHIPS
    cols = theirs.shape[1]
    tile = _row_tile(half)
    steps = half // tile

    def body(pos_ref, g_ref, t_ref, p_ref, own_ref):
        total = g_ref[...] + t_ref[...]
        p_ref[...] = total.astype(BF16)

        @pl.when(pl.program_id(1) == pos_ref[1])
        def _():
            own_ref[...] = total

    return pl.pallas_call(
        body,
        name=name,
        grid_spec=pltpu.PrefetchScalarGridSpec(
            num_scalar_prefetch=1,
            grid=(steps, N_CHIPS),
            in_specs=[
                pl.BlockSpec((tile, cols), lambda i, j, pos: ((2 * j + pos[0]) * steps + i, 0)),
                pl.BlockSpec((tile, cols), lambda i, j, pos: (j * steps + i, 0)),
            ],
            out_specs=[
                pl.BlockSpec((tile, cols), lambda i, j, pos: (j * steps + i, 0)),
                pl.BlockSpec((tile, cols), lambda i, j, pos: (i, 0)),
            ],
        ),
        out_shape=[jax.ShapeDtypeStruct((N_CHIPS * half, cols), BF16), jax.ShapeDtypeStruct((half, cols), F32)],
        compiler_params=_params(("parallel", "arbitrary")),
    )(pos, grad, theirs)


def _chip_sum(name, own, landed, pos):
    half, cols = own.shape
    tile = _row_tile(half)
    steps = half // tile

    def body(pos_ref, own_ref, l0, l1, l2, o_ref):
        o_ref[...] = ((own_ref[...] + l0[...].astype(F32)) + l1[...].astype(F32)) + l2[...].astype(F32)

    landed_specs = [pl.BlockSpec((tile, cols), lambda i, pos, _k=k: (_k * steps + i, 0)) for k in range(N_CHIPS - 1)]
    return pl.pallas_call(
        body,
        name=name,
        grid_spec=pltpu.PrefetchScalarGridSpec(
            num_scalar_prefetch=1,
            grid=(steps,),
            in_specs=[pl.BlockSpec((tile, cols), lambda i, pos: (i, 0))] + landed_specs,
            out_specs=pl.BlockSpec((tile, cols), lambda i, pos: (pos[0] * steps + i, 0)),
        ),
        out_shape=jax.ShapeDtypeStruct((2 * half, cols), F32),
        compiler_params=_params(("parallel",)),
    )(pos, own, landed, landed, landed)


def _adamw(name, w, g, m, v):
    rows, cols = w.shape
    tile = rows if rows * cols <= 256 * 1024 else _row_tile(rows)

    def body(w_ref, g_ref, m_ref, v_ref, d_ref, nm_ref, nv_ref):
        g = g_ref[...]
        nm = ADAM_B1 * m_ref[...] + (1.0 - ADAM_B1) * g
        nv = ADAM_B2 * v_ref[...] + (1.0 - ADAM_B2) * (g * g)
        m_hat = nm / (1.0 - ADAM_B1**ADAM_STEP)
        v_hat = nv / (1.0 - ADAM_B2**ADAM_STEP)
        d_ref[...] = -ADAM_LR * (m_hat / (jnp.sqrt(v_hat) + ADAM_EPS) + ADAM_WD * w_ref[...])
        nm_ref[...] = nm
        nv_ref[...] = nv

    spec = _row_spec(tile, cols)
    return pl.pallas_call(
        body,
        name=name,
        grid=(rows // tile,),
        in_specs=[spec] * 4,
        out_specs=[spec] * 3,
        out_shape=[jax.ShapeDtypeStruct((rows, cols), F32)] * 3,
        compiler_params=_params(("parallel",)),
    )(w, g, m, v)


_SMALL = (
    ("v_ln_g", (D_GMLP,), 8),
    ("v_ln_b", (D_GMLP,), 8),
    ("w_spatial", (N_HEADS, CHUNK, CHUNK), 1024),
    ("b_spatial", (N_HEADS, CHUNK), 8),
    ("sinks", (N_HEADS,), 8),
    ("ln1_g", (D_MODEL,), 8),
    ("ln1_b", (D_MODEL,), 8),
    ("ln2_g", (D_MODEL,), 8),
    ("ln2_b", (D_MODEL,), 8),
    ("squared_error", (D_MODEL,), 8),
)
N_SMALL_PARAMS = len(_SMALL) - 1


def _pack_small(values):
    parts = []
    for (name, shape, rows), val in zip(_SMALL, values, strict=True):
        flat = val.reshape(-1).astype(F32)
        parts.append(jnp.pad(flat, (0, rows * LANES - flat.shape[0])).reshape(rows, LANES))
    parts.append(jnp.zeros((SMALL_ROWS - sum(rows for _, _, rows in _SMALL), LANES), F32))
    return jnp.concatenate(parts, axis=0)


def _unpack_small(slab):
    out, row = [], 0
    for name, shape, rows in _SMALL:
        size = math.prod(shape)
        out.append(slab[row : row + rows].reshape(-1)[:size].reshape((1,) + shape))
        row += rows
    return out


def kernel(x, positions, w_in, v_ln_g, v_ln_b, w_spatial, b_spatial, sinks, w_out, ln1_g, ln1_b, w_ff1, w_ff2, ln2_g, ln2_b, loss_target, m_w_in, m_v_ln_g, m_v_ln_b, m_w_spatial, m_b_spatial, m_sinks, m_w_out, m_ln1_g, m_ln1_b, m_w_ff1, m_w_ff2, m_ln2_g, m_ln2_b, v_w_in, v_v_ln_g, v_v_ln_b, v_w_spatial, v_b_spatial, v_sinks, v_w_out, v_ln1_g, v_ln1_b, v_w_ff1, v_w_ff2, v_ln2_g, v_ln2_b):
    t = x.shape[1]
    x2 = x.reshape(t, D_MODEL)
    target = loss_target.reshape(t, D_MODEL)

    (w_in_t,) = _all_gather_weights([w_in[0].T.astype(BF16)])
    later = [w_out[0].astype(BF16), w_ff1[0].astype(BF16), w_ff2[0].astype(BF16)]
    later_rows = [s.shape[0] for s in later]
    ici_plan, pass_plan = _gather_plans(later_rows)
    bufs, started, ici_send, ici_recv = _split_call(
        "gather_start", later + [_landing(N_CHIPS * r, D_MODEL, BF16) for r in later_rows], start=ici_plan, after=w_in_t)

    inv_freq = ROPE_THETA ** (-jnp.arange(0, HEAD_DIM, 2, dtype=F32) / HEAD_DIM)
    cos, sin = _rope_tables(positions.reshape(t, 1), jnp.tile(inv_freq, LANES // (HEAD_DIM // 2)).reshape(1, LANES))
    u, vg, q, k, va = _in_proj(x2, w_in_t, cos, sin, dep=started)
    bias_full = jnp.repeat(b_spatial[0].T, HEAD_DIM, axis=1)
    sink_vec = sinks.reshape(N_HEADS)
    cat = _mixer_fwd(u, vg, q, k, va, v_ln_g, v_ln_b, w_spatial[0], bias_full, sink_vec)
    bufs, passed, pass_send, pass_recv = _split_call("gather_pass", bufs, wait=(ici_plan, ici_send, ici_recv), start=pass_plan, after=cat)
    bufs, _ = _split_call("gather_end", bufs, wait=(pass_plan, pass_send, pass_recv), after=passed)
    w_out_all = bufs[3]
    w1_all = bufs[4].reshape(N_FF_BLOCKS, D_MODEL, D_MODEL)
    w2_all = bufs[5].reshape(N_FF_BLOCKS, D_MODEL, D_MODEL)
    xhat1, rstd1 = _out_proj_ln1(cat, x2, w_out_all)
    r, dz2, d_ln2_g, d_ln2_b, sq_err = _ffn_fwd_loss(xhat1, ln1_g, ln1_b, w1_all, w2_all, ln2_g, ln2_b, target)

    pos = jnp.stack([lax.axis_index("c"), 2 * lax.axis_index("x") + lax.axis_index("y")]).astype(jnp.int32)
    g_ff2_local = _grad_w_ff2(r, dz2)
    dpre, dz1, d_ln1_g, d_ln1_b = _ffn_bwd_ln1(dz2, r, xhat1, rstd1, ln1_g, w1_all, w2_all)
    g_ff1_local = _grad_w_ff1(xhat1, ln1_g, ln1_b, dpre)
    ff_grads = [g_ff1_local, g_ff2_local]
    ff_rows = [g.shape[0] // N_CHIPS for g in ff_grads]
    swap_plan = _swap_plan(ff_rows)
    bufs, swapping, swap_send, swap_recv = _split_call(
        "ff_swap_start", ff_grads + [_landing(g.shape[0] // 2, D_MODEL, F32) for g in ff_grads], start=swap_plan)
    dcat = _dcat(dz1, w_out_all, dep=swapping)
    g_out_local = _grad_w_out(cat, dz1, dep=dcat)
    bufs, _ = _split_call("ff_swap_wait", bufs, wait=(swap_plan, swap_send, swap_recv), after=g_out_local)
    ff_sums = [_pair_sum("grad_pair_sum_" + nm, g, th, pos) for nm, g, th in zip(["w_ff1", "w_ff2"], bufs[:2], bufs[2:])]
    ff_halves = [p.shape[0] // N_CHIPS for p, _ in ff_sums]
    exchange_plan = _exchange_plan(ff_halves)
    bufs, exchanging, ex_send, ex_recv = _split_call(
        "ff_exchange_start", [p for p, _ in ff_sums] + [_landing(3 * h, D_MODEL, BF16) for h in ff_halves], start=exchange_plan)
    dh_main, dkv, d_v_ln_g, d_v_ln_b, d_w_spatial, d_b_spatial_t, d_sinks = _mixer_bwd(
        u, vg, q, k, va, dcat, cos, sin, v_ln_g, v_ln_b, w_spatial[0], bias_full, sink_vec, dep=exchanging)
    g_in_local = _grad_w_in_t(dh_main, dkv, x2)

    small = [g_in_local, g_out_local]
    theirs = _pair_swap(small)
    pair_sums = [_pair_sum("grad_pair_sum_" + nm, g, th, pos) for nm, g, th in zip(["w_in", "w_out"], small, theirs)]
    small_halves = [p.shape[0] // N_CHIPS for p, _ in pair_sums]
    small_plan = _exchange_plan(small_halves)
    small_bufs, small_exchanging, sm_send, sm_recv = _split_call(
        "small_exchange_start", [p for p, _ in pair_sums] + [_landing(3 * h, D_MODEL, BF16) for h in small_halves], start=small_plan)
    grad_x_flat = _grad_x(dh_main, dkv, dz1, w_in_t, dep=small_exchanging)
    grad_x = grad_x_flat.reshape(1, t, D_MODEL)
    bufs, ff_done = _split_call("ff_exchange_wait", bufs, wait=(exchange_plan, ex_send, ex_recv), after=grad_x_flat)
    ff_shards = [_chip_sum("grad_chip_sum_" + nm, own, ld, pos) for nm, (_, own), ld in zip(["w_ff1", "w_ff2"], ff_sums, bufs[2:])]
    small_bufs, _ = _split_call("small_exchange_wait", small_bufs, wait=(small_plan, sm_send, sm_recv), after=ff_done)
    shards = [_chip_sum("grad_chip_sum_" + nm, own, ld, pos) for nm, (_, own), ld in zip(["w_in", "w_out"], pair_sums, small_bufs[2:])]
    g_w_in_t, g_w_out, g_w_ff1, g_w_ff2 = _pair_gather(shards + ff_shards)
    g_w_in = g_w_in_t.T

    small_g = _all_reduce_small(_pack_small(
        [d_v_ln_g, d_v_ln_b, d_w_spatial, d_b_spatial_t[:, :N_HEADS].T, d_sinks[0, :N_HEADS], d_ln1_g, d_ln1_b, d_ln2_g, d_ln2_b, sq_err]))
    small_grads = _unpack_small(small_g)
    loss = 0.5 * jnp.sum(small_grads[N_SMALL_PARAMS]) / D_MODEL
    small_grads = small_grads[:N_SMALL_PARAMS]

    d_w_in, nm_w_in, nv_w_in = _adamw("adamw_w_in", w_in[0], g_w_in, m_w_in[0], v_w_in[0])
    d_w_out, nm_w_out, nv_w_out = _adamw("adamw_w_out", w_out[0], g_w_out, m_w_out[0], v_w_out[0])
    d_w_ff1, nm_w_ff1, nv_w_ff1 = _adamw("adamw_w_ff1", w_ff1[0], g_w_ff1, m_w_ff1[0], v_w_ff1[0])
    d_w_ff2, nm_w_ff2, nv_w_ff2 = _adamw("adamw_w_ff2", w_ff2[0], g_w_ff2, m_w_ff2[0], v_w_ff2[0])
    no_param = jnp.zeros((D_MODEL,), F32)
    small_w = _pack_small([v_ln_g, v_ln_b, w_spatial, b_spatial, sinks, ln1_g, ln1_b, ln2_g, ln2_b, no_param])
    small_m = _pack_small([m_v_ln_g, m_v_ln_b, m_w_spatial, m_b_spatial, m_sinks, m_ln1_g, m_ln1_b, m_ln2_g, m_ln2_b, no_param])
    small_v = _pack_small([v_v_ln_g, v_v_ln_b, v_w_spatial, v_b_spatial, v_sinks, v_ln1_g, v_ln1_b, v_ln2_g, v_ln2_b, no_param])
    small_d, small_nm, small_nv = (
        _unpack_small(s)[:N_SMALL_PARAMS] for s in _adamw("adamw_small", small_w, small_g, small_m, small_v))

    def with_big(small, w_in_v, w_out_v, w_ff1_v, w_ff2_v):
        g_vg, g_vb, g_ws, g_bs, g_sk, g_1g, g_1b, g_2g, g_2b = small
        return [w_in_v[None], g_vg, g_vb, g_ws, g_bs, g_sk, w_out_v[None], g_1g, g_1b, w_ff1_v[None], w_ff2_v[None], g_2g, g_2b]

    return (
        loss,
        grad_x,
        *with_big(small_grads, g_w_in, g_w_out, g_w_ff1, g_w_ff2),
        *with_big(small_d, d_w_in, d_w_out, d_w_ff1, d_w_ff2),
        *with_big(small_nm, nm_w_in, nm_w_out, nm_w_ff1, nm_w_ff2),
        *with_big(small_nv, nv_w_in, nv_w_out, nv_w_ff1, nv_w_ff2),
    )
```

```python
import functools
import math

import jax
import jax.numpy as jnp
from jax import lax
from jax.experimental import pallas as pl
from jax.experimental.pallas import tpu as pltpu

F32 = jnp.float32
BF16 = jnp.bfloat16

D_MODEL = 1024
HEAD_DIM = 64
D_GMLP = 512
D_ATTN = 512
D_KV = 128
D_IN = 2 * D_GMLP + D_ATTN + 2 * D_KV
D_MAIN = 2 * D_GMLP + D_ATTN
N_HEADS = 8
CHUNK = 128
ROPE_THETA = 10000.0
D_FF = 4 * D_MODEL
N_FF_BLOCKS = 4
LN_EPS = 1e-5
ALPHA = (2.0 * 1) ** 0.25
NEG_INF = -1e30
SCALE = 1.0 / math.sqrt(HEAD_DIM)

ADAM_LR = 0.001
ADAM_B1 = 0.9
ADAM_B2 = 0.999
ADAM_EPS = 1e-08
ADAM_WD = 0.01
ADAM_STEP = 10

N_CHIPS = 4
LANES = 128
V7X_VMEM_BYTES = 64 * 1024 * 1024
VMEM_LIMIT = V7X_VMEM_BYTES - 8 * 1024 * 1024
TM = 512
TM_FFN = 256
TK = 512
SMALL_ROWS = 1152
MESH = pl.DeviceIdType.MESH

NT = (((1,), (1,)), ((), ()))
TN = (((0,), (0,)), ((), ()))


def _dot(a, b, dims=None):
    if dims is None:
        return jnp.dot(a, b, preferred_element_type=F32)
    return lax.dot_general(a, b, dims, preferred_element_type=F32)


def _params(semantics=None):
    return pltpu.CompilerParams(dimension_semantics=semantics, vmem_limit_bytes=VMEM_LIMIT)


def _const_spec(shape, single_buffer=False):
    zeros = (0,) * len(shape)
    if single_buffer:
        return pl.BlockSpec(shape, lambda *_: zeros, pipeline_mode=pl.Buffered(1))
    return pl.BlockSpec(shape, lambda *_: zeros)


def _row_spec(rows, cols):
    return pl.BlockSpec((rows, cols), lambda i: (i, 0))


def _after(dep, body, in_specs, operands):
    if dep is None:
        return body, list(in_specs), list(operands)
    return (lambda dep_ref, *refs: body(*refs)), [pl.BlockSpec(memory_space=pl.ANY)] + list(in_specs), [dep] + list(operands)


def _gelu(x):
    k = math.sqrt(2.0 / math.pi)
    return 0.5 * x * (1.0 + jnp.tanh(k * (x + 0.044715 * (x * x * x))))


def _gelu_and_grad(x):
    k = math.sqrt(2.0 / math.pi)
    x2 = x * x
    t = jnp.tanh(k * (x + 0.044715 * (x2 * x)))
    g = 0.5 * x * (1.0 + t)
    dg = 0.5 * (1.0 + t) + 0.5 * x * (1.0 - t * t) * (k * (1.0 + 3.0 * 0.044715 * x2))
    return g, dg


def _layer_norm_stats(z):
    mu = jnp.mean(z, axis=-1, keepdims=True)
    zc = z - mu
    var = jnp.mean(zc * zc, axis=-1, keepdims=True)
    rstd = lax.rsqrt(var + LN_EPS)
    return zc * rstd, rstd


def _layer_norm_bwd(dxhat, xhat, rstd):
    m1 = jnp.mean(dxhat, axis=-1, keepdims=True)
    m2 = jnp.mean(dxhat * xhat, axis=-1, keepdims=True)
    return rstd * (dxhat - m1 - xhat * m2)


def _rotate_half(t):
    n = t.shape[1]
    lane = lax.broadcasted_iota(jnp.int32, t.shape, 1)
    first = (lane & (HEAD_DIM // 2)) == 0
    return jnp.where(first, -pltpu.roll(t, n - HEAD_DIM // 2, 1), pltpu.roll(t, HEAD_DIM // 2, 1))


def _rope(t, cos, sin):
    return t * cos + _rotate_half(t) * sin


def _rope_transposed(g, cos, sin):
    return g * cos - _rotate_half(g * sin)


def _lane_tile(a, reps):
    return jnp.tile(a, (1, reps)) if reps > 1 else a


def _rope_tables(pos_col, inv_freq_row):
    t = pos_col.shape[0]

    def body(pos_ref, f_ref, cos_ref, sin_ref):
        ang = pos_ref[...].astype(F32) * f_ref[...]
        cos_ref[...] = jnp.cos(ang)
        sin_ref[...] = jnp.sin(ang)

    return pl.pallas_call(
        body,
        name="rope_tables",
        grid=(t // TM,),
        in_specs=[_row_spec(TM, 1), _const_spec((1, LANES))],
        out_specs=[_row_spec(TM, LANES), _row_spec(TM, LANES)],
        out_shape=[jax.ShapeDtypeStruct((t, LANES), F32)] * 2,
        compiler_params=_params(("parallel",)),
    )(pos_col, inv_freq_row)


def _in_proj(x, w_in_t, cos, sin, dep=None):
    t = x.shape[0]

    def body(x_ref, w_ref, cos_ref, sin_ref, u_ref, vg_ref, q_ref, k_ref, va_ref):
        xb = x_ref[...].astype(BF16)
        u_ref[...] = _dot(xb, w_ref[0:D_GMLP, :], NT)
        vg_ref[...] = _dot(xb, w_ref[D_GMLP : 2 * D_GMLP, :], NT)
        q = _dot(xb, w_ref[2 * D_GMLP : D_MAIN, :], NT)
        k = _dot(xb, w_ref[D_MAIN : D_MAIN + D_KV, :], NT)
        va_ref[...] = _dot(xb, w_ref[D_MAIN + D_KV : D_IN, :], NT).astype(BF16)
        c, s = cos_ref[...], sin_ref[...]
        q_ref[...] = _rope(q, _lane_tile(c, D_ATTN // LANES), _lane_tile(s, D_ATTN // LANES)).astype(BF16)
        k_ref[...] = _rope(k, c, s).astype(BF16)

    body, in_specs, operands = _after(
        dep, body, [_row_spec(TM, D_MODEL), _const_spec((D_IN, D_MODEL)), _row_spec(TM, LANES), _row_spec(TM, LANES)], [x, w_in_t, cos, sin])
    return pl.pallas_call(
        body,
        name="in_proj",
        grid=(t // TM,),
        in_specs=in_specs,
        out_specs=[_row_spec(TM, D_GMLP), _row_spec(TM, D_GMLP), _row_spec(TM, D_ATTN), _row_spec(TM, D_KV), _row_spec(TM, D_KV)],
        out_shape=[
            jax.ShapeDtypeStruct((t, D_GMLP), F32),
            jax.ShapeDtypeStruct((t, D_GMLP), F32),
            jax.ShapeDtypeStruct((t, D_ATTN), BF16),
            jax.ShapeDtypeStruct((t, D_KV), BF16),
            jax.ShapeDtypeStruct((t, D_KV), BF16),
        ],
        compiler_params=_params(("parallel",)),
    )(*operands)


def _chunk_specs():
    cur = lambda i: (i, 0)
    prev = lambda i: (jnp.maximum(i - 1, 0), 0)
    return [
        pl.BlockSpec((CHUNK, D_GMLP), cur),
        pl.BlockSpec((CHUNK, D_GMLP), cur),
        pl.BlockSpec((CHUNK, D_ATTN), cur),
        pl.BlockSpec((CHUNK, D_KV), cur),
        pl.BlockSpec((CHUNK, D_KV), prev),
        pl.BlockSpec((CHUNK, D_KV), cur),
        pl.BlockSpec((CHUNK, D_KV), prev),
    ]


def _half_lane_masks(rows):
    lane = lax.broadcasted_iota(jnp.int32, (rows, LANES), 1)
    return lane < HEAD_DIM


def _kv_variants(kv2):
    left = _half_lane_masks(kv2.shape[0])
    f = kv2.astype(F32)
    swapped = pltpu.roll(f, HEAD_DIM, 1)
    zero = jnp.zeros_like(f)
    g0 = (jnp.where(left, f, zero).astype(BF16), jnp.where(left, zero, swapped).astype(BF16))
    g1 = (jnp.where(left, swapped, zero).astype(BF16), jnp.where(left, zero, f).astype(BF16))
    return (g0, g1)


def _band_mask(i, heads=1):
    row = lax.broadcasted_iota(jnp.int32, (heads * CHUNK, 2 * CHUNK), 0) & (CHUNK - 1)
    col = lax.broadcasted_iota(jnp.int32, (heads * CHUNK, 2 * CHUNK), 1)
    no_prev = jnp.where(i > 0, 0, 4 * CHUNK)
    in_prev = jnp.logical_and(col < CHUNK, (col - row) > no_prev)
    in_cur = jnp.logical_and(col >= CHUNK, (col - CHUNK) <= row)
    return jnp.logical_or(in_prev, in_cur)


def _softmax_with_sink(s, sink):
    m = jnp.maximum(jnp.max(s, axis=1, keepdims=True), sink)
    e = jnp.exp(s - m)
    e_sink = jnp.exp(sink - m)
    inv = 1.0 / (jnp.sum(e, axis=1, keepdims=True) + e_sink)
    return e * inv, e_sink * inv


def _causal_mask():
    row = lax.broadcasted_iota(jnp.int32, (CHUNK, CHUNK), 0)
    col = lax.broadcasted_iota(jnp.int32, (CHUNK, CHUNK), 1)
    return col <= row


def _store_spatial_weights(w_ref, wcat_ref, wcat_t_ref=None):
    causal = _causal_mask()
    for p in range(D_GMLP // LANES):
        wl = jnp.where(causal, w_ref[2 * p], 0.0)
        wr = jnp.where(causal, w_ref[2 * p + 1], 0.0)
        wcat_ref[p] = jnp.concatenate([wl, wr], axis=1).astype(BF16)
        if wcat_t_ref is not None:
            wcat_t_ref[p] = jnp.concatenate([wl.T, wr.T], axis=1).astype(BF16)


def _pair_stack(xp, left):
    return jnp.concatenate([jnp.where(left, xp, 0.0), jnp.where(left, 0.0, xp)], axis=0).astype(BF16)


def _mixer_fwd(u, vg, q, k, va, v_ln_g, v_ln_b, w_spatial, bias_full, sinks):
    t = u.shape[0]

    def body(u_ref, vg_ref, q_ref, kc_ref, kp_ref, vc_ref, vp_ref, g_ref, b_ref, w_ref, bias_ref, sink_ref, cat_ref, wcat):
        i = pl.program_id(0)
        left = _half_lane_masks(CHUNK)

        @pl.when(i == 0)
        def _():
            _store_spatial_weights(w_ref, wcat)

        ug = _gelu(u_ref[...])
        xhat, _ = _layer_norm_stats(_gelu(vg_ref[...]))
        vgl = xhat * g_ref[...] + b_ref[...]
        for p in range(D_GMLP // LANES):
            cols = slice(p * LANES, (p + 1) * LANES)
            mixed = _dot(wcat[p], _pair_stack(vgl[:, cols], left))
            cat_ref[:, cols] = (ug[:, cols] * (mixed + bias_ref[:, cols])).astype(BF16)

        k_var = _kv_variants(jnp.concatenate([kp_ref[...], kc_ref[...]], axis=0))
        v_var = _kv_variants(jnp.concatenate([vp_ref[...], vc_ref[...]], axis=0))
        valid = _band_mask(i)
        scores = [_dot(q_ref[:, (h // 2) * LANES : (h // 2 + 1) * LANES], k_var[h // 4][h % 2], NT) for h in range(N_HEADS)]
        heads = range(N_HEADS)
        sinks_h = [sink_ref[h] for h in heads]
        masked = [jnp.where(valid, scores[h] * SCALE, NEG_INF) for h in heads]
        maxes = [jnp.maximum(jnp.max(masked[h], axis=1, keepdims=True), sinks_h[h]) for h in heads]
        exps = [jnp.exp(masked[h] - maxes[h]) for h in heads]
        invs = [1.0 / (jnp.sum(exps[h], axis=1, keepdims=True) + jnp.exp(sinks_h[h] - maxes[h])) for h in heads]
        probs = [(exps[h] * invs[h]).astype(BF16) for h in heads]
        for p in range(D_ATTN // LANES):
            out = _dot(probs[2 * p], v_var[p // 2][0]) + _dot(probs[2 * p + 1], v_var[p // 2][1])
            cat_ref[:, D_GMLP + p * LANES : D_GMLP + (p + 1) * LANES] = out.astype(BF16)

    return pl.pallas_call(
        body,
        name="mixer_fwd",
        grid=(t // CHUNK,),
        in_specs=_chunk_specs()
        + [
            _const_spec((1, D_GMLP)),
            _const_spec((1, D_GMLP)),
            _const_spec((N_HEADS, CHUNK, CHUNK)),
            _const_spec((CHUNK, D_GMLP)),
            pl.BlockSpec(memory_space=pltpu.SMEM),
        ],
        out_specs=pl.BlockSpec((CHUNK, D_MODEL), lambda i: (i, 0)),
        out_shape=jax.ShapeDtypeStruct((t, D_MODEL), BF16),
        scratch_shapes=[pltpu.VMEM((D_GMLP // LANES, CHUNK, 2 * CHUNK), BF16)],
        compiler_params=_params(("arbitrary",)),
    )(u, vg, q, k, k, va, va, v_ln_g, v_ln_b, w_spatial, bias_full, sinks)


def _out_proj_ln1(cat, x, w_out):
    t = x.shape[0]

    def body(cat_ref, x_ref, w_ref, xhat_ref, rstd_ref):
        z = ALPHA * x_ref[...] + _dot(cat_ref[...], w_ref[...])
        xhat, rstd = _layer_norm_stats(z)
        xhat_ref[...] = xhat
        rstd_ref[...] = rstd

    return pl.pallas_call(
        body,
        name="out_proj_ln1",
        grid=(t // TM,),
        in_specs=[_row_spec(TM, D_MODEL), _row_spec(TM, D_MODEL), _const_spec((D_MODEL, D_MODEL))],
        out_specs=[_row_spec(TM, D_MODEL), _row_spec(TM, 1)],
        out_shape=[jax.ShapeDtypeStruct((t, D_MODEL), F32), jax.ShapeDtypeStruct((t, 1), F32)],
        compiler_params=_params(("parallel",)),
    )(cat, x, w_out)


def _ffn_fwd_loss(xhat1, ln1_g, ln1_b, w1, w2, ln2_g, ln2_b, target):
    t = xhat1.shape[0]

    def body(xh_ref, g1_ref, b1_ref, w1_ref, w2_ref, g2_ref, b2_ref, tgt_ref, r_ref, dz2_ref, dg2_ref, db2_ref, sq_ref):
        @pl.when(pl.program_id(0) == 0)
        def _():
            dg2_ref[...] = jnp.zeros_like(dg2_ref)
            db2_ref[...] = jnp.zeros_like(db2_ref)
            sq_ref[...] = jnp.zeros_like(sq_ref)

        x1 = xh_ref[...] * g1_ref[...] + b1_ref[...]
        x1b = x1.astype(BF16)
        ff = jnp.zeros((TM_FFN, D_MODEL), F32)
        for j in range(N_FF_BLOCKS):
            r = jnp.maximum(_dot(x1b, w1_ref[j]), 0.0)
            r_ref[:, j * D_MODEL : (j + 1) * D_MODEL] = r.astype(BF16)
            ff = ff + _dot((r * r).astype(BF16), w2_ref[j])
        xhat2, rstd2 = _layer_norm_stats(ALPHA * x1 + ff)
        err = xhat2 * g2_ref[...] + b2_ref[...] - tgt_ref[...]
        sq_ref[...] += jnp.sum(err * err, axis=0, keepdims=True)
        dy = err * (1.0 / D_MODEL)
        dg2_ref[...] += jnp.sum(dy * xhat2, axis=0, keepdims=True)
        db2_ref[...] += jnp.sum(dy, axis=0, keepdims=True)
        dz2_ref[...] = _layer_norm_bwd(dy * g2_ref[...], xhat2, rstd2)

    vec = _const_spec((1, D_MODEL))
    wspec = _const_spec((N_FF_BLOCKS, D_MODEL, D_MODEL), single_buffer=True)
    return pl.pallas_call(
        body,
        name="ffn_fwd_loss",
        grid=(t // TM_FFN,),
        in_specs=[_row_spec(TM_FFN, D_MODEL), vec, vec, wspec, wspec, vec, vec, _row_spec(TM_FFN, D_MODEL)],
        out_specs=[_row_spec(TM_FFN, D_FF), _row_spec(TM_FFN, D_MODEL), vec, vec, vec],
        out_shape=[
            jax.ShapeDtypeStruct((t, D_FF), BF16),
            jax.ShapeDtypeStruct((t, D_MODEL), F32),
            jax.ShapeDtypeStruct((1, D_MODEL), F32),
            jax.ShapeDtypeStruct((1, D_MODEL), F32),
            jax.ShapeDtypeStruct((1, D_MODEL), F32),
        ],
        compiler_params=_params(("arbitrary",)),
    )(xhat1, ln1_g, ln1_b, w1, w2, ln2_g, ln2_b, target)


def _ffn_bwd_ln1(dz2, r, xhat1, rstd1, ln1_g, w1, w2):
    t = dz2.shape[0]

    def body(dz2_ref, r_ref, xh_ref, rstd_ref, g1_ref, w1_ref, w2_ref, dpre_ref, dz1_ref, dg1_ref, db1_ref):
        @pl.when(pl.program_id(0) == 0)
        def _():
            dg1_ref[...] = jnp.zeros_like(dg1_ref)
            db1_ref[...] = jnp.zeros_like(db1_ref)

        dz2 = dz2_ref[...]
        dz2b = dz2.astype(BF16)
        dx1 = ALPHA * dz2
        for j in range(N_FF_BLOCKS):
            cols = slice(j * D_MODEL, (j + 1) * D_MODEL)
            dpre = (_dot(dz2b, w2_ref[j], NT) * (2.0 * r_ref[:, cols].astype(F32))).astype(BF16)
            dpre_ref[:, cols] = dpre
            dx1 = dx1 + _dot(dpre, w1_ref[j], NT)
        xhat1 = xh_ref[...]
        dg1_ref[...] += jnp.sum(dx1 * xhat1, axis=0, keepdims=True)
        db1_ref[...] += jnp.sum(dx1, axis=0, keepdims=True)
        dz1_ref[...] = _layer_norm_bwd(dx1 * g1_ref[...], xhat1, rstd_ref[...])

    vec = _const_spec((1, D_MODEL))
    wspec = _const_spec((N_FF_BLOCKS, D_MODEL, D_MODEL), single_buffer=True)
    return pl.pallas_call(
        body,
        name="ffn_bwd_ln1",
        grid=(t // TM_FFN,),
        in_specs=[_row_spec(TM_FFN, D_MODEL), _row_spec(TM_FFN, D_FF), _row_spec(TM_FFN, D_MODEL), _row_spec(TM_FFN, 1), vec, wspec, wspec],
        out_specs=[_row_spec(TM_FFN, D_FF), _row_spec(TM_FFN, D_MODEL), vec, vec],
        out_shape=[
            jax.ShapeDtypeStruct((t, D_FF), BF16),
            jax.ShapeDtypeStruct((t, D_MODEL), F32),
            jax.ShapeDtypeStruct((1, D_MODEL), F32),
            jax.ShapeDtypeStruct((1, D_MODEL), F32),
        ],
        compiler_params=_params(("arbitrary",)),
    )(dz2, r, xhat1, rstd1, ln1_g, w1, w2)


def _dcat(dz1, w_out, dep=None):
    t = dz1.shape[0]

    def body(dz1_ref, w_ref, dcat_ref):
        dcat_ref[...] = _dot(dz1_ref[...].astype(BF16), w_ref[...], NT)

    body, in_specs, operands = _after(dep, body, [_row_spec(TM, D_MODEL), _const_spec((D_MODEL, D_MODEL))], [dz1, w_out])
    return pl.pallas_call(
        body,
        name="dcat",
        grid=(t // TM,),
        in_specs=in_specs,
        out_specs=_row_spec(TM, D_MODEL),
        out_shape=jax.ShapeDtypeStruct((t, D_MODEL), F32),
        compiler_params=_params(("parallel",)),
    )(*operands)


def _mixer_bwd(u, vg, q, k, va, dcat, cos, sin, v_ln_g, v_ln_b, w_spatial, bias_full, sinks, dep=None):
    t = u.shape[0]
    n_chunks = t // CHUNK

    def body(u_ref, vg_ref, q_ref, kc_ref, kp_ref, vc_ref, vp_ref, dcat_ref, cosc_ref, sinc_ref, cosp_ref, sinp_ref,
             g_ref, b_ref, w_ref, bias_ref, sink_ref,
             dmain_ref, dkv_ref, dg_ref, db_ref, dw_ref, dbs_ref, dsink_ref, dmix_acc, wcat, wcat_t):
        i = pl.program_id(0)
        left = _half_lane_masks(CHUNK)
        lane = lax.broadcasted_iota(jnp.int32, (CHUNK, LANES), 1)
        n_pairs = D_GMLP // LANES

        @pl.when(i == 0)
        def _():
            dg_ref[...] = jnp.zeros_like(dg_ref)
            db_ref[...] = jnp.zeros_like(db_ref)
            dw_ref[...] = jnp.zeros_like(dw_ref)
            dsink_ref[...] = jnp.zeros_like(dsink_ref)
            dmix_acc[...] = jnp.zeros_like(dmix_acc)
            _store_spatial_weights(w_ref, wcat, wcat_t)

        ug, dug_du = _gelu_and_grad(u_ref[...])
        gv, dgv_dv = _gelu_and_grad(vg_ref[...])
        xhat, rstd = _layer_norm_stats(gv)
        gain = g_ref[...]
        vgl = xhat * gain + b_ref[...]
        pair_cols = [slice(p * LANES, (p + 1) * LANES) for p in range(n_pairs)]
        mixed = [_dot(wcat[p], _pair_stack(vgl[:, cols], left)) for p, cols in enumerate(pair_cols)]
        dm_stacks = []
        for p, cols in enumerate(pair_cols):
            da = dcat_ref[:, cols]
            dmain_ref[:, cols] = (da * (mixed[p] + bias_ref[:, cols]) * dug_du[:, cols]).astype(BF16)
            dmixed = da * ug[:, cols]
            dmix_acc[:, cols] += dmixed
            dm_stacks.append(_pair_stack(dmixed, left))
        causal = _causal_mask()
        for p, cols in enumerate(pair_cols):
            dw_pair = _dot(dm_stacks[p], vgl[:, cols].astype(BF16), NT)
            dw_ref[2 * p] += jnp.where(causal, dw_pair[:CHUNK], 0.0)
            dw_ref[2 * p + 1] += jnp.where(causal, dw_pair[CHUNK:], 0.0)
        dvgl = jnp.concatenate([_dot(wcat_t[p], dm_stacks[p]) for p in range(n_pairs)], axis=1)
        dg_ref[...] += jnp.sum(dvgl * xhat, axis=0, keepdims=True)
        db_ref[...] += jnp.sum(dvgl, axis=0, keepdims=True)
        dgv = _layer_norm_bwd(dvgl * gain, xhat, rstd)
        dmain_ref[:, D_GMLP : 2 * D_GMLP] = (dgv * dgv_dv).astype(BF16)

        @pl.when(i == n_chunks - 1)
        def _():
            tile = jnp.zeros((CHUNK, LANES), F32)
            for p in range(D_GMLP // LANES):
                dm = dmix_acc[:, p * LANES : (p + 1) * LANES]
                sl = jnp.sum(jnp.where(left, dm, 0.0), axis=1, keepdims=True)
                sr = jnp.sum(jnp.where(left, 0.0, dm), axis=1, keepdims=True)
                tile = jnp.where(lane == 2 * p, sl, tile)
                tile = jnp.where(lane == 2 * p + 1, sr, tile)
            dbs_ref[...] = tile

        k_var = _kv_variants(jnp.concatenate([kp_ref[...], kc_ref[...]], axis=0))
        v_var = _kv_variants(jnp.concatenate([vp_ref[...], vc_ref[...]], axis=0))
        valid = _band_mask(i)
        n_qpairs = D_ATTN // LANES
        q_pairs = [q_ref[:, p * LANES : (p + 1) * LANES] for p in range(n_qpairs)]
        do_all = dcat_ref[:, D_GMLP:D_MODEL]
        do_pairs = [do_all[:, p * LANES : (p + 1) * LANES].astype(BF16) for p in range(n_qpairs)]
        scores = [_dot(q_pairs[h // 2], k_var[h // 4][h % 2], NT) for h in range(N_HEADS)]
        dprobs = [_dot(do_pairs[h // 2], v_var[h // 4][h % 2], NT) for h in range(N_HEADS)]
        heads = range(N_HEADS)
        sinks_h = [sink_ref[h] for h in heads]
        masked = [jnp.where(valid, scores[h] * SCALE, NEG_INF) for h in heads]
        maxes = [jnp.maximum(jnp.max(masked[h], axis=1, keepdims=True), sinks_h[h]) for h in heads]
        exps = [jnp.exp(masked[h] - maxes[h]) for h in heads]
        exp_sinks = [jnp.exp(sinks_h[h] - maxes[h]) for h in heads]
        invs = [1.0 / (jnp.sum(exps[h], axis=1, keepdims=True) + exp_sinks[h]) for h in heads]
        probs = [exps[h] * invs[h] for h in heads]
        dsums = [jnp.sum(probs[h] * dprobs[h], axis=1, keepdims=True) for h in heads]
        ds_b = [(probs[h] * (dprobs[h] - dsums[h]) * SCALE).astype(BF16) for h in heads]
        probs_b = [probs[h].astype(BF16) for h in heads]
        dsink_row = jnp.zeros((1, LANES), F32)
        lane_row = lax.broadcasted_iota(jnp.int32, (1, LANES), 1)
        for h in heads:
            d_sink = -jnp.sum(exp_sinks[h] * invs[h] * dsums[h], axis=0, keepdims=True)
            dsink_row = dsink_row + jnp.where(lane_row == h, d_sink, 0.0)
        dsink_ref[0:1, :] += dsink_row
        dq_all = jnp.concatenate(
            [_dot(ds_b[2 * p], k_var[p // 2][0]) + _dot(ds_b[2 * p + 1], k_var[p // 2][1]) for p in range(n_qpairs)], axis=1)
        cos_c, sin_c = cosc_ref[...], sinc_ref[...]
        dmain_ref[:, 2 * D_GMLP : D_MAIN] = _rope_transposed(dq_all, _lane_tile(cos_c, n_qpairs), _lane_tile(sin_c, n_qpairs)).astype(BF16)

        q_t = q_ref[...].astype(F32).T.astype(BF16)
        do_t = do_all.T.astype(BF16)
        heads_per_group = N_HEADS // 2

        def group_grad_t(lhs_t, rhs_heads):
            parts = []
            for g in range(2):
                heads = range(g * heads_per_group, (g + 1) * heads_per_group)
                lhs = jnp.concatenate([lhs_t[h * HEAD_DIM : (h + 1) * HEAD_DIM] for h in heads], axis=1)
                parts.append(_dot(lhs, jnp.concatenate([rhs_heads[h] for h in heads], axis=0)))
            return jnp.concatenate(parts, axis=0).T

        dk2 = group_grad_t(q_t, ds_b)
        dv2 = group_grad_t(do_t, probs_b)
        cur = pl.ds(pl.multiple_of(i * CHUNK, CHUNK), CHUNK)
        dkv_ref[cur, 0:D_KV] = _rope_transposed(dk2[CHUNK:], cos_c, sin_c)
        dkv_ref[cur, D_KV : 2 * D_KV] = dv2[CHUNK:]

        @pl.when(i > 0)
        def _():
            prev = pl.ds(pl.multiple_of((i - 1) * CHUNK, CHUNK), CHUNK)
            dkv_ref[prev, 0:D_KV] += _rope_transposed(dk2[:CHUNK], cosp_ref[...], sinp_ref[...])
            dkv_ref[prev, D_KV : 2 * D_KV] += dv2[:CHUNK]

    cur = lambda i: (i, 0)
    prev = lambda i: (jnp.maximum(i - 1, 0), 0)
    in_specs = _chunk_specs() + [
        pl.BlockSpec((CHUNK, D_MODEL), cur),
        pl.BlockSpec((CHUNK, LANES), cur),
        pl.BlockSpec((CHUNK, LANES), cur),
        pl.BlockSpec((CHUNK, LANES), prev),
        pl.BlockSpec((CHUNK, LANES), prev),
        _const_spec((1, D_GMLP)),
        _const_spec((1, D_GMLP)),
        _const_spec((N_HEADS, CHUNK, CHUNK)),
        _const_spec((CHUNK, D_GMLP)),
        pl.BlockSpec(memory_space=pltpu.SMEM),
    ]
    body, in_specs, operands = _after(
        dep, body, in_specs, [u, vg, q, k, k, va, va, dcat, cos, sin, cos, sin, v_ln_g, v_ln_b, w_spatial, bias_full, sinks])
    return pl.pallas_call(
        body,
        name="mixer_bwd",
        grid=(n_chunks,),
        in_specs=in_specs,
        out_specs=[
            pl.BlockSpec((CHUNK, D_MAIN), cur),
            _const_spec((t, 2 * D_KV)),
            _const_spec((1, D_GMLP)),
            _const_spec((1, D_GMLP)),
            _const_spec((N_HEADS, CHUNK, CHUNK)),
            _const_spec((CHUNK, LANES)),
            _const_spec((8, LANES)),
        ],
        out_shape=[
            jax.ShapeDtypeStruct((t, D_MAIN), BF16),
            jax.ShapeDtypeStruct((t, 2 * D_KV), F32),
            jax.ShapeDtypeStruct((1, D_GMLP), F32),
            jax.ShapeDtypeStruct((1, D_GMLP), F32),
            jax.ShapeDtypeStruct((N_HEADS, CHUNK, CHUNK), F32),
            jax.ShapeDtypeStruct((CHUNK, LANES), F32),
            jax.ShapeDtypeStruct((8, LANES), F32),
        ],
        scratch_shapes=[
            pltpu.VMEM((CHUNK, D_GMLP), F32),
            pltpu.VMEM((D_GMLP // LANES, CHUNK, 2 * CHUNK), BF16),
            pltpu.VMEM((D_GMLP // LANES, CHUNK, 2 * CHUNK), BF16),
        ],
        compiler_params=_params(("arbitrary",)),
    )(*operands)


def _grad_x(dh_main, dkv, dz1, w_in_t, dep=None):
    t = dz1.shape[0]

    def body(dm_ref, dkv_ref, dz1_ref, w_ref, gx_ref):
        acc = ALPHA * dz1_ref[...] + _dot(dm_ref[...], w_ref[0:D_MAIN, :])
        gx_ref[...] = acc + _dot(dkv_ref[...].astype(BF16), w_ref[D_MAIN:D_IN, :])

    body, in_specs, operands = _after(
        dep, body, [_row_spec(TM, D_MAIN), _row_spec(TM, 2 * D_KV), _row_spec(TM, D_MODEL), _const_spec((D_IN, D_MODEL))], [dh_main, dkv, dz1, w_in_t])
    return pl.pallas_call(
        body,
        name="grad_x",
        grid=(t // TM,),
        in_specs=in_specs,
        out_specs=_row_spec(TM, D_MODEL),
        out_shape=jax.ShapeDtypeStruct((t, D_MODEL), F32),
        compiler_params=_params(("parallel",)),
    )(*operands)


def _token_contraction(name, n_blocks, out_rows, in_arrays, in_specs, contributions, dep=None):
    t = in_arrays[0].shape[0]
    block_rows = out_rows // n_blocks

    def body(*refs):
        out_ref = refs[-1]

        @pl.when(pl.program_id(1) == 0)
        def _():
            out_ref[...] = jnp.zeros_like(out_ref)

        for row0, a, b in contributions(*refs[:-1]):
            out_ref[row0 : row0 + a.shape[1], :] += _dot(a, b, TN)

    body, in_specs, operands = _after(dep, body, in_specs, in_arrays)
    return pl.pallas_call(
        body,
        name=name,
        grid=(n_blocks, t // TK),
        in_specs=in_specs,
        out_specs=pl.BlockSpec((block_rows, D_MODEL), lambda j, k: (j, 0)),
        out_shape=jax.ShapeDtypeStruct((out_rows, D_MODEL), F32),
        compiler_params=_params(("parallel", "arbitrary")),
    )(*operands)


def _tile_spec(cols):
    return pl.BlockSpec((TK, cols), lambda j, k: (k, 0))


def _tile_block_spec():
    return pl.BlockSpec((TK, D_MODEL), lambda j, k: (k, j))


def _grad_w_in_t(dh_main, dkv, x):
    def contributions(dm_ref, dkv_ref, x_ref):
        xb = x_ref[...].astype(BF16)
        return [(0, dm_ref[...], xb), (D_MAIN, dkv_ref[...].astype(BF16), xb)]

    return _token_contraction("grad_w_in", 1, D_IN, [dh_main, dkv, x], [_tile_spec(D_MAIN), _tile_spec(2 * D_KV), _tile_spec(D_MODEL)], contributions)


def _grad_w_out(cat, dz1, dep=None):
    def contributions(cat_ref, dz1_ref):
        return [(0, cat_ref[...], dz1_ref[...].astype(BF16))]

    return _token_contraction("grad_w_out", 1, D_MODEL, [cat, dz1], [_tile_spec(D_MODEL), _tile_spec(D_MODEL)], contributions, dep)


def _grad_w_ff1(xhat1, ln1_g, ln1_b, dpre):
    def contributions(xh_ref, g_ref, b_ref, dpre_ref):
        return [(0, (xh_ref[...] * g_ref[...] + b_ref[...]).astype(BF16), dpre_ref[...])]

    vec = pl.BlockSpec((1, D_MODEL), lambda j, k: (0, 0))
    return _token_contraction("grad_w_ff1", N_FF_BLOCKS, D_FF, [xhat1, ln1_g, ln1_b, dpre], [_tile_spec(D_MODEL), vec, vec, _tile_block_spec()], contributions)


def _grad_w_ff2(r, dz2):
    def contributions(r_ref, dz2_ref):
        rf = r_ref[...].astype(F32)
        return [(0, (rf * rf).astype(BF16), dz2_ref[...].astype(BF16))]

    return _token_contraction("grad_w_ff2", N_FF_BLOCKS, D_FF, [r, dz2], [_tile_block_spec(), _tile_spec(D_MODEL)], contributions)


ANY = pl.BlockSpec(memory_space=pl.ANY)


def _mesh_position():
    return lax.axis_index("x"), lax.axis_index("y"), lax.axis_index("c")


def _other_chips(x, y):
    return [(1 - x, y), (x, 1 - y), (1 - x, 1 - y)]


def _remote(src, dst, send_sem, recv_sem, device):
    return pltpu.make_async_remote_copy(src_ref=src, dst_ref=dst, send_sem=send_sem, recv_sem=recv_sem, device_id=device, device_id_type=MESH)


def _rows(ref, start, size):
    return ref.at[pl.ds(start, size), :]


def _all_gather_weights(shards):
    n = len(shards)
    per = 7

    def body(*refs):
        ins, outs = refs[:n], refs[n : 2 * n]
        send_sems, recv_sems = refs[2 * n :]
        x, y, c = _mesh_position()
        me = 2 * x + y
        chips = _other_chips(x, y)
        sibling = (x, y, 1 - c)
        started = []
        for w in range(n):
            rows = shards[w].shape[0]
            half = rows // 2
            for kk, (px, py) in enumerate(chips):
                cp = _remote(_rows(ins[w], c * half, half), _rows(outs[w], me * rows + c * half, half),
                             send_sems.at[per * w + kk], recv_sems.at[per * w + kk], (px, py, c))
                cp.start()
                started.append(cp)
            cp = _remote(ins[w], _rows(outs[w], me * rows, rows), send_sems.at[per * w + 6], recv_sems.at[per * w + 6], sibling)
            cp.start()
            started.append(cp)
        for w in range(n):
            rows = shards[w].shape[0]
            half = rows // 2
            for kk, (px, py) in enumerate(chips):
                blk = _rows(outs[w], (2 * px + py) * rows + c * half, half)
                _remote(blk, blk, send_sems.at[per * w + kk], recv_sems.at[per * w + kk], (px, py, c)).wait_recv()
                fwd = _remote(blk, blk, send_sems.at[per * w + 3 + kk], recv_sems.at[per * w + 3 + kk], sibling)
                fwd.start()
                started.append(fwd)
        for w in range(n):
            rows = shards[w].shape[0]
            half = rows // 2
            for kk, (px, py) in enumerate(chips):
                blk = _rows(outs[w], (2 * px + py) * rows + (1 - c) * half, half)
                _remote(blk, blk, send_sems.at[per * w + 3 + kk], recv_sems.at[per * w + 3 + kk], sibling).wait_recv()
            own = _rows(outs[w], me * rows, rows)
            _remote(own, own, send_sems.at[per * w + 6], recv_sems.at[per * w + 6], sibling).wait_recv()
        for cp in started:
            cp.wait_send()

    return pl.pallas_call(
        body,
        name="all_gather_weights",
        in_specs=[ANY] * n,
        out_specs=[ANY] * n,
        out_shape=[jax.ShapeDtypeStruct((N_CHIPS * s.shape[0], s.shape[1]), s.dtype) for s in shards],
        scratch_shapes=[pltpu.SemaphoreType.DMA((per * n,)), pltpu.SemaphoreType.DMA((per * n,))],
    )(*shards)


def _pair_swap(grads):
    n = len(grads)

    def body(*refs):
        ins, theirs = refs[:n], refs[n : 2 * n]
        send_sems, recv_sems = refs[2 * n :]
        x, y, c = _mesh_position()
        sibling = (x, y, 1 - c)
        sends = []
        for w in range(n):
            rows = grads[w].shape[0] // N_CHIPS
            half = rows // 2
            for j in range(N_CHIPS):
                cp = _remote(_rows(ins[w], j * rows + (1 - c) * half, half), _rows(theirs[w], j * half, half),
                             send_sems.at[4 * w + j], recv_sems.at[4 * w + j], sibling)
                cp.start()
                sends.append(cp)
        for cp in sends:
            cp.wait_recv()
        for cp in sends:
            cp.wait_send()

    return pl.pallas_call(
        body,
        name="grad_pair_swap",
        in_specs=[ANY] * n,
        out_specs=[ANY] * n,
        out_shape=[jax.ShapeDtypeStruct((g.shape[0] // 2, g.shape[1]), g.dtype) for g in grads],
        scratch_shapes=[pltpu.SemaphoreType.DMA((4 * n,)), pltpu.SemaphoreType.DMA((4 * n,))],
    )(*grads)


def _chip_exchange(partials):
    n = len(partials)

    def body(*refs):
        ins, outs = refs[:n], refs[n : 2 * n]
        send_sems, recv_sems = refs[2 * n :]
        x, y, c = _mesh_position()
        chips = _other_chips(x, y)
        sends = []
        for w in range(n):
            half = partials[w].shape[0] // N_CHIPS
            for kk, (px, py) in enumerate(chips):
                cp = _remote(_rows(ins[w], (2 * px + py) * half, half), _rows(outs[w], kk * half, half),
                             send_sems.at[3 * w + kk], recv_sems.at[3 * w + kk], (px, py, c))
                cp.start()
                sends.append(cp)
        for cp in sends:
            cp.wait_recv()
        for cp in sends:
            cp.wait_send()

    return pl.pallas_call(
        body,
        name="grad_chip_exchange",
        in_specs=[ANY] * n,
        out_specs=[ANY] * n,
        out_shape=[jax.ShapeDtypeStruct((3 * p.shape[0] // N_CHIPS, p.shape[1]), p.dtype) for p in partials],
        scratch_shapes=[pltpu.SemaphoreType.DMA((3 * n,)), pltpu.SemaphoreType.DMA((3 * n,))],
    )(*partials)


def _pair_gather(shards):
    n = len(shards)

    def body(*refs):
        outs = refs[n : 2 * n]
        send_sems, recv_sems = refs[2 * n :]
        x, y, c = _mesh_position()
        sibling = (x, y, 1 - c)
        sends = []
        for w in range(n):
            half = shards[w].shape[0] // 2
            mine = _rows(outs[w], c * half, half)
            cp = _remote(mine, mine, send_sems.at[w], recv_sems.at[w], sibling)
            cp.start()
            sends.append(cp)
        for w in range(n):
            half = shards[w].shape[0] // 2
            blk = _rows(outs[w], (1 - c) * half, half)
            _remote(blk, blk, send_sems.at[w], recv_sems.at[w], sibling).wait_recv()
        for cp in sends:
            cp.wait_send()

    return pl.pallas_call(
        body,
        name="grad_pair_gather",
        in_specs=[ANY] * n,
        out_specs=[ANY] * n,
        out_shape=[jax.ShapeDtypeStruct(s.shape, s.dtype) for s in shards],
        input_output_aliases={w: w for w in range(n)},
        scratch_shapes=[pltpu.SemaphoreType.DMA((n,)), pltpu.SemaphoreType.DMA((n,))],
    )(*shards)


def _all_reduce_small(slab):
    rows = slab.shape[0]
    part = rows // 8

    def body(slab_ref, out_ref, landing, reduced, send_sems, recv_sems):
        x, y, c = _mesh_position()
        me = 4 * x + 2 * y + c
        flips = [(k >> 2, (k >> 1) & 1, k & 1) for k in range(1, 8)]

        def peer(flip):
            fx, fy, fc = flip
            return (1 - x if fx else x, 1 - y if fy else y, 1 - c if fc else c)

        def my_rows(ref):
            return ref.at[pl.ds(pl.multiple_of(me * part, 8), part), :]

        sends = []
        for kk, flip in enumerate(flips):
            px, py, pc = peer(flip)
            them = 4 * px + 2 * py + pc
            cp = _remote(slab_ref.at[pl.ds(pl.multiple_of(them * part, 8), part), :], landing.at[me], send_sems.at[kk], recv_sems.at[kk], (px, py, pc))
            cp.start()
            sends.append(cp)
        landing[me] = my_rows(slab_ref)[...]
        for kk, flip in enumerate(flips):
            px, py, pc = peer(flip)
            them = 4 * px + 2 * py + pc
            _remote(landing.at[them], landing.at[them], send_sems.at[kk], recv_sems.at[kk], (px, py, pc)).wait_recv()
        total = landing[0]
        for s in range(1, 8):
            total = total + landing[s]
        reduced[...] = total
        my_rows(out_ref)[...] = total
        for kk, flip in enumerate(flips):
            cp = _remote(reduced, my_rows(out_ref), send_sems.at[7 + kk], recv_sems.at[7 + kk], peer(flip))
            cp.start()
            sends.append(cp)
        for kk, flip in enumerate(flips):
            px, py, pc = peer(flip)
            them = 4 * px + 2 * py + pc
            blk = out_ref.at[pl.ds(pl.multiple_of(them * part, 8), part), :]
            _remote(blk, blk, send_sems.at[7 + kk], recv_sems.at[7 + kk], (px, py, pc)).wait_recv()
        for cp in sends:
            cp.wait_send()

    vmem = pl.BlockSpec(memory_space=pltpu.VMEM)
    return pl.pallas_call(
        body,
        name="all_reduce_small",
        in_specs=[vmem],
        out_specs=vmem,
        out_shape=jax.ShapeDtypeStruct(slab.shape, slab.dtype),
        scratch_shapes=[pltpu.VMEM((8, part, LANES), F32), pltpu.VMEM((part, LANES), F32), pltpu.SemaphoreType.DMA((14,)), pltpu.SemaphoreType.DMA((14,))],
    )(slab)


HBM = pl.BlockSpec(memory_space=pltpu.HBM)
SEM = pl.BlockSpec(memory_space=pltpu.SEMAPHORE)
DATAFLOW = pltpu.SideEffectType.DATAFLOW_SIDE_EFFECTING
TOKEN = jax.ShapeDtypeStruct((8, LANES), F32)


def _plan_copies(bufs, plan, send_sems, recv_sems):
    out = []
    for i, (src, src_row, dst, dst_row, recv_row, rows, device) in enumerate(plan):
        send = _remote(_rows(bufs[src], src_row, rows), _rows(bufs[dst], dst_row, rows), send_sems.at[i], recv_sems.at[i], device)
        landed = _rows(bufs[dst], recv_row, rows)
        recv = _remote(landed, landed, send_sems.at[i], recv_sems.at[i], device)
        out.append((send, recv))
    return out


def _split_call(name, bufs, wait=None, start=None, after=None):
    n = len(bufs)
    n_in = n + (2 if wait else 0) + (1 if after is not None else 0)
    n_start = len(start(0, 0, 0)) if start else 0

    def body(*refs):
        ins = refs[:n]
        x, y, c = _mesh_position()
        if wait:
            for send, recv in _plan_copies(ins, wait[0](x, y, c), refs[n], refs[n + 1]):
                recv.wait_recv()
                send.wait_send()
        if start:
            for send, _ in _plan_copies(ins, start(x, y, c), refs[n_in + n + 1], refs[n_in + n + 2]):
                send.start()
        token = refs[n_in + n]
        token[...] = jnp.zeros_like(token)

    operands = [pltpu.with_memory_space_constraint(b, pltpu.HBM) for b in bufs]
    in_specs = [HBM] * n
    if wait:
        operands += [wait[1], wait[2]]
        in_specs += [SEM, SEM]
    if after is not None:
        operands.append(after)
        in_specs.append(ANY)
    out_shape = [pltpu.HBM(b.shape, b.dtype) for b in bufs] + [TOKEN]
    out_specs = [HBM] * n + [pl.BlockSpec(memory_space=pltpu.VMEM)]
    if start:
        out_shape += [pltpu.SemaphoreType.DMA((n_start,)), pltpu.SemaphoreType.DMA((n_start,))]
        out_specs += [SEM, SEM]
    outs = pl.pallas_call(
        body,
        name=name,
        in_specs=in_specs,
        out_specs=out_specs,
        out_shape=out_shape,
        input_output_aliases={i: i for i in range(n)},
        compiler_params=pltpu.CompilerParams(has_side_effects=DATAFLOW),
    )(*operands)
    return (list(outs[:n]), outs[n]) + tuple(outs[n + 1 :])


def _gather_plans(shard_rows):
    n = len(shard_rows)

    def ici(x, y, c):
        me = 2 * x + y
        plan = []
        for w, rows in enumerate(shard_rows):
            half = rows // 2
            for px, py in _other_chips(x, y):
                plan.append((w, c * half, n + w, me * rows + c * half, (2 * px + py) * rows + c * half, half, (px, py, c)))
            plan.append((w, 0, n + w, me * rows, me * rows, rows, (x, y, 1 - c)))
        return plan

    def passed_on(x, y, c):
        plan = []
        for w, rows in enumerate(shard_rows):
            half = rows // 2
            for px, py in _other_chips(x, y):
                row = (2 * px + py) * rows
                plan.append((n + w, row + c * half, n + w, row + c * half, row + (1 - c) * half, half, (x, y, 1 - c)))
        return plan

    return ici, passed_on


def _swap_plan(block_rows):
    n = len(block_rows)

    def plan_fn(x, y, c):
        plan = []
        for w, rows in enumerate(block_rows):
            half = rows // 2
            for j in range(N_CHIPS):
                plan.append((w, j * rows + (1 - c) * half, n + w, j * half, j * half, half, (x, y, 1 - c)))
        return plan

    return plan_fn


def _exchange_plan(halves):
    n = len(halves)

    def plan_fn(x, y, c):
        plan = []
        for w, half in enumerate(halves):
            for kk, (px, py) in enumerate(_other_chips(x, y)):
                plan.append((w, (2 * px + py) * half, n + w, kk * half, kk * half, half, (px, py, c)))
        return plan

    return plan_fn


def _landing(rows, cols, dtype):
    return lax.empty((rows, cols), dtype)


def _row_tile(rows, cap=256):
    best = 8
    for cand in range(8, cap + 1, 8):
        if rows % cand == 0:
            best = cand
    return best


def _pair_sum(name, grad, theirs, pos):
    half = theirs.shape[0] // N_CHIPS
    cols = theirs.shape[1]
    tile = _row_tile(half)
    steps = half // tile

    def body(pos_ref, g_ref, t_ref, p_ref, own_ref):
        total = g_ref[...] + t_ref[...]
        p_ref[...] = total.astype(BF16)

        @pl.when(pl.program_id(1) == pos_ref[1])
        def _():
            own_ref[...] = total

    return pl.pallas_call(
        body,
        name=name,
        grid_spec=pltpu.PrefetchScalarGridSpec(
            num_scalar_prefetch=1,
            grid=(steps, N_CHIPS),
            in_specs=[
                pl.BlockSpec((tile, cols), lambda i, j, pos: ((2 * j + pos[0]) * steps + i, 0)),
                pl.BlockSpec((tile, cols), lambda i, j, pos: (j * steps + i, 0)),
            ],
            out_specs=[
                pl.BlockSpec((tile, cols), lambda i, j, pos: (j * steps + i, 0)),
                pl.BlockSpec((tile, cols), lambda i, j, pos: (i, 0)),
            ],
        ),
        out_shape=[jax.ShapeDtypeStruct((N_CHIPS * half, cols), BF16), jax.ShapeDtypeStruct((half, cols), F32)],
        compiler_params=_params(("parallel", "arbitrary")),
    )(pos, grad, theirs)


def _chip_sum(name, own, landed, pos):
    half, cols = own.shape
    tile = _row_tile(half)
    steps = half // tile

    def body(pos_ref, own_ref, l0, l1, l2, o_ref):
        o_ref[...] = ((own_ref[...] + l0[...].astype(F32)) + l1[...].astype(F32)) + l2[...].astype(F32)

    landed_specs = [pl.BlockSpec((tile, cols), lambda i, pos, _k=k: (_k * steps + i, 0)) for k in range(N_CHIPS - 1)]
    return pl.pallas_call(
        body,
        name=name,
        grid_spec=pltpu.PrefetchScalarGridSpec(
            num_scalar_prefetch=1,
            grid=(steps,),
            in_specs=[pl.BlockSpec((tile, cols), lambda i, pos: (i, 0))] + landed_specs,
            out_specs=pl.BlockSpec((tile, cols), lambda i, pos: (pos[0] * steps + i, 0)),
        ),
        out_shape=jax.ShapeDtypeStruct((2 * half, cols), F32),
        compiler_params=_params(("parallel",)),
    )(pos, own, landed, landed, landed)


def _adamw(name, w, g, m, v):
    rows, cols = w.shape
    tile = rows if rows * cols <= 256 * 1024 else _row_tile(rows)

    def body(w_ref, g_ref, m_ref, v_ref, d_ref, nm_ref, nv_ref):
        g = g_ref[...]
        nm = ADAM_B1 * m_ref[...] + (1.0 - ADAM_B1) * g
        nv = ADAM_B2 * v_ref[...] + (1.0 - ADAM_B2) * (g * g)
        m_hat = nm / (1.0 - ADAM_B1**ADAM_STEP)
        v_hat = nv / (1.0 - ADAM_B2**ADAM_STEP)
        d_ref[...] = -ADAM_LR * (m_hat / (jnp.sqrt(v_hat) + ADAM_EPS) + ADAM_WD * w_ref[...])
        nm_ref[...] = nm
        nv_ref[...] = nv

    spec = _row_spec(tile, cols)
    return pl.pallas_call(
        body,
        name=name,
        grid=(rows // tile,),
        in_specs=[spec] * 4,
        out_specs=[spec] * 3,
        out_shape=[jax.ShapeDtypeStruct((rows, cols), F32)] * 3,
        compiler_params=_params(("parallel",)),
    )(w, g, m, v)


_SMALL = (
    ("v_ln_g", (D_GMLP,), 8),
    ("v_ln_b", (D_GMLP,), 8),
    ("w_spatial", (N_HEADS, CHUNK, CHUNK), 1024),
    ("b_spatial", (N_HEADS, CHUNK), 8),
    ("sinks", (N_HEADS,), 8),
    ("ln1_g", (D_MODEL,), 8),
    ("ln1_b", (D_MODEL,), 8),
    ("ln2_g", (D_MODEL,), 8),
    ("ln2_b", (D_MODEL,), 8),
    ("squared_error", (D_MODEL,), 8),
)
N_SMALL_PARAMS = len(_SMALL) - 1


def _pack_small(values):
    parts = []
    for (name, shape, rows), val in zip(_SMALL, values, strict=True):
        flat = val.reshape(-1).astype(F32)
        parts.append(jnp.pad(flat, (0, rows * LANES - flat.shape[0])).reshape(rows, LANES))
    parts.append(jnp.zeros((SMALL_ROWS - sum(rows for _, _, rows in _SMALL), LANES), F32))
    return jnp.concatenate(parts, axis=0)


def _unpack_small(slab):
    out, row = [], 0
    for name, shape, rows in _SMALL:
        size = math.prod(shape)
        out.append(slab[row : row + rows].reshape(-1)[:size].reshape((1,) + shape))
        row += rows
    return out


def kernel(x, positions, w_in, v_ln_g, v_ln_b, w_spatial, b_spatial, sinks, w_out, ln1_g, ln1_b, w_ff1, w_ff2, ln2_g, ln2_b, loss_target, m_w_in, m_v_ln_g, m_v_ln_b, m_w_spatial, m_b_spatial, m_sinks, m_w_out, m_ln1_g, m_ln1_b, m_w_ff1, m_w_ff2, m_ln2_g, m_ln2_b, v_w_in, v_v_ln_g, v_v_ln_b, v_w_spatial, v_b_spatial, v_sinks, v_w_out, v_ln1_g, v_ln1_b, v_w_ff1, v_w_ff2, v_ln2_g, v_ln2_b):
    t = x.shape[1]
    x2 = x.reshape(t, D_MODEL)
    target = loss_target.reshape(t, D_MODEL)

    (w_in_t,) = _all_gather_weights([w_in[0].T.astype(BF16)])
    later = [w_out[0].astype(BF16), w_ff1[0].astype(BF16), w_ff2[0].astype(BF16)]
    later_rows = [s.shape[0] for s in later]
    ici_plan, pass_plan = _gather_plans(later_rows)
    bufs, started, ici_send, ici_recv = _split_call(
        "gather_start", later + [_landing(N_CHIPS * r, D_MODEL, BF16) for r in later_rows], start=ici_plan, after=w_in_t)

    inv_freq = ROPE_THETA ** (-jnp.arange(0, HEAD_DIM, 2, dtype=F32) / HEAD_DIM)
    cos, sin = _rope_tables(positions.reshape(t, 1), jnp.tile(inv_freq, LANES // (HEAD_DIM // 2)).reshape(1, LANES))
    u, vg, q, k, va = _in_proj(x2, w_in_t, cos, sin, dep=started)
    bias_full = jnp.repeat(b_spatial[0].T, HEAD_DIM, axis=1)
    sink_vec = sinks.reshape(N_HEADS)
    cat = _mixer_fwd(u, vg, q, k, va, v_ln_g, v_ln_b, w_spatial[0], bias_full, sink_vec)
    bufs, passed, pass_send, pass_recv = _split_call("gather_pass", bufs, wait=(ici_plan, ici_send, ici_recv), start=pass_plan, after=cat)
    bufs, _ = _split_call("gather_end", bufs, wait=(pass_plan, pass_send, pass_recv), after=passed)
    w_out_all = bufs[3]
    w1_all = bufs[4].reshape(N_FF_BLOCKS, D_MODEL, D_MODEL)
    w2_all = bufs[5].reshape(N_FF_BLOCKS, D_MODEL, D_MODEL)
    xhat1, rstd1 = _out_proj_ln1(cat, x2, w_out_all)
    r, dz2, d_ln2_g, d_ln2_b, sq_err = _ffn_fwd_loss(xhat1, ln1_g, ln1_b, w1_all, w2_all, ln2_g, ln2_b, target)

    pos = jnp.stack([lax.axis_index("c"), 2 * lax.axis_index("x") + lax.axis_index("y")]).astype(jnp.int32)
    g_ff2_local = _grad_w_ff2(r, dz2)
    dpre, dz1, d_ln1_g, d_ln1_b = _ffn_bwd_ln1(dz2, r, xhat1, rstd1, ln1_g, w1_all, w2_all)
    g_ff1_local = _grad_w_ff1(xhat1, ln1_g, ln1_b, dpre)
    ff_grads = [g_ff1_local, g_ff2_local]
    ff_rows = [g.shape[0] // N_CHIPS for g in ff_grads]
    swap_plan = _swap_plan(ff_rows)
    bufs, swapping, swap_send, swap_recv = _split_call(
        "ff_swap_start", ff_grads + [_landing(g.shape[0] // 2, D_MODEL, F32) for g in ff_grads], start=swap_plan)
    dcat = _dcat(dz1, w_out_all, dep=swapping)
    g_out_local = _grad_w_out(cat, dz1, dep=dcat)
    bufs, _ = _split_call("ff_swap_wait", bufs, wait=(swap_plan, swap_send, swap_recv), after=g_out_local)
    ff_sums = [_pair_sum("grad_pair_sum_" + nm, g, th, pos) for nm, g, th in zip(["w_ff1", "w_ff2"], bufs[:2], bufs[2:])]
    ff_halves = [p.shape[0] // N_CHIPS for p, _ in ff_sums]
    exchange_plan = _exchange_plan(ff_halves)
    bufs, exchanging, ex_send, ex_recv = _split_call(
        "ff_exchange_start", [p for p, _ in ff_sums] + [_landing(3 * h, D_MODEL, BF16) for h in ff_halves], start=exchange_plan)
    dh_main, dkv, d_v_ln_g, d_v_ln_b, d_w_spatial, d_b_spatial_t, d_sinks = _mixer_bwd(
        u, vg, q, k, va, dcat, cos, sin, v_ln_g, v_ln_b, w_spatial[0], bias_full, sink_vec, dep=exchanging)
    g_in_local = _grad_w_in_t(dh_main, dkv, x2)

    small = [g_in_local, g_out_local]
    theirs = _pair_swap(small)
    pair_sums = [_pair_sum("grad_pair_sum_" + nm, g, th, pos) for nm, g, th in zip(["w_in", "w_out"], small, theirs)]
    small_halves = [p.shape[0] // N_CHIPS for p, _ in pair_sums]
    small_plan = _exchange_plan(small_halves)
    small_bufs, small_exchanging, sm_send, sm_recv = _split_call(
        "small_exchange_start", [p for p, _ in pair_sums] + [_landing(3 * h, D_MODEL, BF16) for h in small_halves], start=small_plan)
    grad_x_flat = _grad_x(dh_main, dkv, dz1, w_in_t, dep=small_exchanging)
    grad_x = grad_x_flat.reshape(1, t, D_MODEL)
    bufs, ff_done = _split_call("ff_exchange_wait", bufs, wait=(exchange_plan, ex_send, ex_recv), after=grad_x_flat)
    ff_shards = [_chip_sum("grad_chip_sum_" + nm, own, ld, pos) for nm, (_, own), ld in zip(["w_ff1", "w_ff2"], ff_sums, bufs[2:])]
    small_bufs, _ = _split_call("small_exchange_wait", small_bufs, wait=(small_plan, sm_send, sm_recv), after=ff_done)
    shards = [_chip_sum("grad_chip_sum_" + nm, own, ld, pos) for nm, (_, own), ld in zip(["w_in", "w_out"], pair_sums, small_bufs[2:])]
    g_w_in_t, g_w_out, g_w_ff1, g_w_ff2 = _pair_gather(shards + ff_shards)
    g_w_in = g_w_in_t.T

    small_g = _all_reduce_small(_pack_small(
        [d_v_ln_g, d_v_ln_b, d_w_spatial, d_b_spatial_t[:, :N_HEADS].T, d_sinks[0, :N_HEADS], d_ln1_g, d_ln1_b, d_ln2_g, d_ln2_b, sq_err]))
    small_grads = _unpack_small(small_g)
    loss = 0.5 * jnp.sum(small_grads[N_SMALL_PARAMS]) / D_MODEL
    small_grads = small_grads[:N_SMALL_PARAMS]

    d_w_in, nm_w_in, nv_w_in = _adamw("adamw_w_in", w_in[0], g_w_in, m_w_in[0], v_w_in[0])
    d_w_out, nm_w_out, nv_w_out = _adamw("adamw_w_out", w_out[0], g_w_out, m_w_out[0], v_w_out[0])
    d_w_ff1, nm_w_ff1, nv_w_ff1 = _adamw("adamw_w_ff1", w_ff1[0], g_w_ff1, m_w_ff1[0], v_w_ff1[0])
    d_w_ff2, nm_w_ff2, nv_w_ff2 = _adamw("adamw_w_ff2", w_ff2[0], g_w_ff2, m_w_ff2[0], v_w_ff2[0])
    no_param = jnp.zeros((D_MODEL,), F32)
    small_w = _pack_small([v_ln_g, v_ln_b, w_spatial, b_spatial, sinks, ln1_g, ln1_b, ln2_g, ln2_b, no_param])
    small_m = _pack_small([m_v_ln_g, m_v_ln_b, m_w_spatial, m_b_spatial, m_sinks, m_ln1_g, m_ln1_b, m_ln2_g, m_ln2_b, no_param])
    small_v = _pack_small([v_v_ln_g, v_v_ln_b, v_w_spatial, v_b_spatial, v_sinks, v_ln1_g, v_ln1_b, v_ln2_g, v_ln2_b, no_param])
    small_d, small_nm, small_nv = (
        _unpack_small(s)[:N_SMALL_PARAMS] for s in _adamw("adamw_small", small_w, small_g, small_m, small_v))

    def with_big(small, w_in_v, w_out_v, w_ff1_v, w_ff2_v):
        g_vg, g_vb, g_ws, g_bs, g_sk, g_1g, g_1b, g_2g, g_2b = small
        return [w_in_v[None], g_vg, g_vb, g_ws, g_bs, g_sk, w_out_v[None], g_1g, g_1b, w_ff1_v[None], w_ff2_v[None], g_2g, g_2b]

    return (
        loss,
        grad_x,
        *with_big(small_grads, g_w_in, g_w_out, g_w_ff1, g_w_ff2),
        *with_big(small_d, d_w_in, d_w_out, d_w_ff1, d_w_ff2),
        *with_big(small_nm, nm_w_in, nm_w_out, nm_w_ff1, nm_w_ff2),
        *with_big(small_nv, nv_w_in, nv_w_out, nv_w_ff1, nv_w_ff2),
    )
```

```python
import functools
import math

import jax
import jax.numpy as jnp
from jax import lax
from jax.experimental import pallas as pl
from jax.experimental.pallas import tpu as pltpu

F32 = jnp.float32
BF16 = jnp.bfloat16

D_MODEL = 1024
HEAD_DIM = 64
D_GMLP = 512
D_ATTN = 512
D_KV = 128
D_IN = 2 * D_GMLP + D_ATTN + 2 * D_KV
D_MAIN = 2 * D_GMLP + D_ATTN
N_HEADS = 8
CHUNK = 128
ROPE_THETA = 10000.0
D_FF = 4 * D_MODEL
N_FF_BLOCKS = 4
LN_EPS = 1e-5
ALPHA = (2.0 * 1) ** 0.25
NEG_INF = -1e30
SCALE = 1.0 / math.sqrt(HEAD_DIM)

ADAM_LR = 0.001
ADAM_B1 = 0.9
ADAM_B2 = 0.999
ADAM_EPS = 1e-08
ADAM_WD = 0.01
ADAM_STEP = 10

N_CHIPS = 4
LANES = 128
V7X_VMEM_BYTES = 64 * 1024 * 1024
VMEM_LIMIT = V7X_VMEM_BYTES - 8 * 1024 * 1024
TM = 512
TM_FFN = 256
TK = 1024
SMALL_ROWS = 1152
MESH = pl.DeviceIdType.MESH

NT = (((1,), (1,)), ((), ()))
TN = (((0,), (0,)), ((), ()))


def _dot(a, b, dims=None):
    if dims is None:
        return jnp.dot(a, b, preferred_element_type=F32)
    return lax.dot_general(a, b, dims, preferred_element_type=F32)


def _params(semantics=None):
    return pltpu.CompilerParams(dimension_semantics=semantics, vmem_limit_bytes=VMEM_LIMIT)


def _const_spec(shape, single_buffer=False):
    zeros = (0,) * len(shape)
    if single_buffer:
        return pl.BlockSpec(shape, lambda *_: zeros, pipeline_mode=pl.Buffered(1))
    return pl.BlockSpec(shape, lambda *_: zeros)


def _row_spec(rows, cols):
    return pl.BlockSpec((rows, cols), lambda i: (i, 0))


def _after(dep, body, in_specs, operands):
    if dep is None:
        return body, list(in_specs), list(operands)
    return (lambda dep_ref, *refs: body(*refs)), [pl.BlockSpec(memory_space=pl.ANY)] + list(in_specs), [dep] + list(operands)


def _gelu(x):
    k = math.sqrt(2.0 / math.pi)
    return 0.5 * x * (1.0 + jnp.tanh(k * (x + 0.044715 * (x * x * x))))


def _gelu_and_grad(x):
    k = math.sqrt(2.0 / math.pi)
    x2 = x * x
    t = jnp.tanh(k * (x + 0.044715 * (x2 * x)))
    g = 0.5 * x * (1.0 + t)
    dg = 0.5 * (1.0 + t) + 0.5 * x * (1.0 - t * t) * (k * (1.0 + 3.0 * 0.044715 * x2))
    return g, dg


def _layer_norm_stats(z):
    mu = jnp.mean(z, axis=-1, keepdims=True)
    zc = z - mu
    var = jnp.mean(zc * zc, axis=-1, keepdims=True)
    rstd = lax.rsqrt(var + LN_EPS)
    return zc * rstd, rstd


def _layer_norm_bwd(dxhat, xhat, rstd):
    m1 = jnp.mean(dxhat, axis=-1, keepdims=True)
    m2 = jnp.mean(dxhat * xhat, axis=-1, keepdims=True)
    return rstd * (dxhat - m1 - xhat * m2)


def _rotate_half(t):
    n = t.shape[1]
    lane = lax.broadcasted_iota(jnp.int32, t.shape, 1)
    first = (lane & (HEAD_DIM // 2)) == 0
    return jnp.where(first, -pltpu.roll(t, n - HEAD_DIM // 2, 1), pltpu.roll(t, HEAD_DIM // 2, 1))


def _rope(t, cos, sin):
    return t * cos + _rotate_half(t) * sin


def _rope_transposed(g, cos, sin):
    return g * cos - _rotate_half(g * sin)


def _lane_tile(a, reps):
    return jnp.tile(a, (1, reps)) if reps > 1 else a


def _rope_tables(pos_col, inv_freq_row):
    t = pos_col.shape[0]

    def body(pos_ref, f_ref, cos_ref, sin_ref):
        ang = pos_ref[...].astype(F32) * f_ref[...]
        cos_ref[...] = jnp.cos(ang)
        sin_ref[...] = jnp.sin(ang)

    return pl.pallas_call(
        body,
        name="rope_tables",
        grid=(t // TM,),
        in_specs=[_row_spec(TM, 1), _const_spec((1, LANES))],
        out_specs=[_row_spec(TM, LANES), _row_spec(TM, LANES)],
        out_shape=[jax.ShapeDtypeStruct((t, LANES), F32)] * 2,
        compiler_params=_params(("parallel",)),
    )(pos_col, inv_freq_row)


def _in_proj(x, w_in_t, cos, sin, dep=None):
    t = x.shape[0]

    def body(x_ref, w_ref, cos_ref, sin_ref, u_ref, vg_ref, q_ref, k_ref, va_ref):
        xb = x_ref[...].astype(BF16)
        u_ref[...] = _dot(xb, w_ref[0:D_GMLP, :], NT)
        vg_ref[...] = _dot(xb, w_ref[D_GMLP : 2 * D_GMLP, :], NT)
        q = _dot(xb, w_ref[2 * D_GMLP : D_MAIN, :], NT)
        k = _dot(xb, w_ref[D_MAIN : D_MAIN + D_KV, :], NT)
        va_ref[...] = _dot(xb, w_ref[D_MAIN + D_KV : D_IN, :], NT).astype(BF16)
        c, s = cos_ref[...], sin_ref[...]
        q_ref[...] = _rope(q, _lane_tile(c, D_ATTN // LANES), _lane_tile(s, D_ATTN // LANES)).astype(BF16)
        k_ref[...] = _rope(k, c, s).astype(BF16)

    body, in_specs, operands = _after(
        dep, body, [_row_spec(TM, D_MODEL), _const_spec((D_IN, D_MODEL)), _row_spec(TM, LANES), _row_spec(TM, LANES)], [x, w_in_t, cos, sin])
    return pl.pallas_call(
        body,
        name="in_proj",
        grid=(t // TM,),
        in_specs=in_specs,
        out_specs=[_row_spec(TM, D_GMLP), _row_spec(TM, D_GMLP), _row_spec(TM, D_ATTN), _row_spec(TM, D_KV), _row_spec(TM, D_KV)],
        out_shape=[
            jax.ShapeDtypeStruct((t, D_GMLP), F32),
            jax.ShapeDtypeStruct((t, D_GMLP), F32),
            jax.ShapeDtypeStruct((t, D_ATTN), BF16),
            jax.ShapeDtypeStruct((t, D_KV), BF16),
            jax.ShapeDtypeStruct((t, D_KV), BF16),
        ],
        compiler_params=_params(("parallel",)),
    )(*operands)


def _chunk_specs():
    cur = lambda i: (i, 0)
    prev = lambda i: (jnp.maximum(i - 1, 0), 0)
    return [
        pl.BlockSpec((CHUNK, D_GMLP), cur),
        pl.BlockSpec((CHUNK, D_GMLP), cur),
        pl.BlockSpec((CHUNK, D_ATTN), cur),
        pl.BlockSpec((CHUNK, D_KV), cur),
        pl.BlockSpec((CHUNK, D_KV), prev),
        pl.BlockSpec((CHUNK, D_KV), cur),
        pl.BlockSpec((CHUNK, D_KV), prev),
    ]


def _half_lane_masks(rows):
    lane = lax.broadcasted_iota(jnp.int32, (rows, LANES), 1)
    return lane < HEAD_DIM


def _kv_variants(kv2):
    left = _half_lane_masks(kv2.shape[0])
    f = kv2.astype(F32)
    swapped = pltpu.roll(f, HEAD_DIM, 1)
    zero = jnp.zeros_like(f)
    g0 = (jnp.where(left, f, zero).astype(BF16), jnp.where(left, zero, swapped).astype(BF16))
    g1 = (jnp.where(left, swapped, zero).astype(BF16), jnp.where(left, zero, f).astype(BF16))
    return (g0, g1)


def _band_mask(i, heads=1):
    row = lax.broadcasted_iota(jnp.int32, (heads * CHUNK, 2 * CHUNK), 0) & (CHUNK - 1)
    col = lax.broadcasted_iota(jnp.int32, (heads * CHUNK, 2 * CHUNK), 1)
    no_prev = jnp.where(i > 0, 0, 4 * CHUNK)
    in_prev = jnp.logical_and(col < CHUNK, (col - row) > no_prev)
    in_cur = jnp.logical_and(col >= CHUNK, (col - CHUNK) <= row)
    return jnp.logical_or(in_prev, in_cur)


def _softmax_with_sink(s, sink):
    m = jnp.maximum(jnp.max(s, axis=1, keepdims=True), sink)
    e = jnp.exp(s - m)
    e_sink = jnp.exp(sink - m)
    inv = 1.0 / (jnp.sum(e, axis=1, keepdims=True) + e_sink)
    return e * inv, e_sink * inv


def _causal_mask():
    row = lax.broadcasted_iota(jnp.int32, (CHUNK, CHUNK), 0)
    col = lax.broadcasted_iota(jnp.int32, (CHUNK, CHUNK), 1)
    return col <= row


def _store_spatial_weights(w_ref, wcat_ref, wcat_t_ref=None):
    causal = _causal_mask()
    for p in range(D_GMLP // LANES):
        wl = jnp.where(causal, w_ref[2 * p], 0.0)
        wr = jnp.where(causal, w_ref[2 * p + 1], 0.0)
        wcat_ref[p] = jnp.concatenate([wl, wr], axis=1).astype(BF16)
        if wcat_t_ref is not None:
            wcat_t_ref[p] = jnp.concatenate([wl.T, wr.T], axis=1).astype(BF16)


def _pair_stack(xp, left):
    return jnp.concatenate([jnp.where(left, xp, 0.0), jnp.where(left, 0.0, xp)], axis=0).astype(BF16)


def _mixer_fwd(u, vg, q, k, va, v_ln_g, v_ln_b, w_spatial, bias_full, sinks):
    t = u.shape[0]

    def body(u_ref, vg_ref, q_ref, kc_ref, kp_ref, vc_ref, vp_ref, g_ref, b_ref, w_ref, bias_ref, sink_ref, cat_ref, wcat):
        i = pl.program_id(0)
        left = _half_lane_masks(CHUNK)

        @pl.when(i == 0)
        def _():
            _store_spatial_weights(w_ref, wcat)

        ug = _gelu(u_ref[...])
        xhat, _ = _layer_norm_stats(_gelu(vg_ref[...]))
        vgl = xhat * g_ref[...] + b_ref[...]
        for p in range(D_GMLP // LANES):
            cols = slice(p * LANES, (p + 1) * LANES)
            mixed = _dot(wcat[p], _pair_stack(vgl[:, cols], left))
            cat_ref[:, cols] = (ug[:, cols] * (mixed + bias_ref[:, cols])).astype(BF16)

        k_var = _kv_variants(jnp.concatenate([kp_ref[...], kc_ref[...]], axis=0))
        v_var = _kv_variants(jnp.concatenate([vp_ref[...], vc_ref[...]], axis=0))
        valid = _band_mask(i)
        scores = [_dot(q_ref[:, (h // 2) * LANES : (h // 2 + 1) * LANES], k_var[h // 4][h % 2], NT) for h in range(N_HEADS)]
        heads = range(N_HEADS)
        sinks_h = [sink_ref[h] for h in heads]
        masked = [jnp.where(valid, scores[h] * SCALE, NEG_INF) for h in heads]
        maxes = [jnp.maximum(jnp.max(masked[h], axis=1, keepdims=True), sinks_h[h]) for h in heads]
        exps = [jnp.exp(masked[h] - maxes[h]) for h in heads]
        invs = [1.0 / (jnp.sum(exps[h], axis=1, keepdims=True) + jnp.exp(sinks_h[h] - maxes[h])) for h in heads]
        probs = [(exps[h] * invs[h]).astype(BF16) for h in heads]
        for p in range(D_ATTN // LANES):
            out = _dot(probs[2 * p], v_var[p // 2][0]) + _dot(probs[2 * p + 1], v_var[p // 2][1])
            cat_ref[:, D_GMLP + p * LANES : D_GMLP + (p + 1) * LANES] = out.astype(BF16)

    return pl.pallas_call(
        body,
        name="mixer_fwd",
        grid=(t // CHUNK,),
        in_specs=_chunk_specs()
        + [
            _const_spec((1, D_GMLP)),
            _const_spec((1, D_GMLP)),
            _const_spec((N_HEADS, CHUNK, CHUNK)),
            _const_spec((CHUNK, D_GMLP)),
            pl.BlockSpec(memory_space=pltpu.SMEM),
        ],
        out_specs=pl.BlockSpec((CHUNK, D_MODEL), lambda i: (i, 0)),
        out_shape=jax.ShapeDtypeStruct((t, D_MODEL), BF16),
        scratch_shapes=[pltpu.VMEM((D_GMLP // LANES, CHUNK, 2 * CHUNK), BF16)],
        compiler_params=_params(("arbitrary",)),
    )(u, vg, q, k, k, va, va, v_ln_g, v_ln_b, w_spatial, bias_full, sinks)


def _out_proj_ln1(cat, x, w_out):
    t = x.shape[0]

    def body(cat_ref, x_ref, w_ref, xhat_ref, rstd_ref):
        z = ALPHA * x_ref[...] + _dot(cat_ref[...], w_ref[...])
        xhat, rstd = _layer_norm_stats(z)
        xhat_ref[...] = xhat
        rstd_ref[...] = rstd

    return pl.pallas_call(
        body,
        name="out_proj_ln1",
        grid=(t // TM,),
        in_specs=[_row_spec(TM, D_MODEL), _row_spec(TM, D_MODEL), _const_spec((D_MODEL, D_MODEL))],
        out_specs=[_row_spec(TM, D_MODEL), _row_spec(TM, 1)],
        out_shape=[jax.ShapeDtypeStruct((t, D_MODEL), F32), jax.ShapeDtypeStruct((t, 1), F32)],
        compiler_params=_params(("parallel",)),
    )(cat, x, w_out)


def _ffn_fwd_loss(xhat1, ln1_g, ln1_b, w1, w2, ln2_g, ln2_b, target):
    t = xhat1.shape[0]

    def body(xh_ref, g1_ref, b1_ref, w1_ref, w2_ref, g2_ref, b2_ref, tgt_ref, r_ref, dz2_ref, dg2_ref, db2_ref, sq_ref):
        @pl.when(pl.program_id(0) == 0)
        def _():
            dg2_ref[...] = jnp.zeros_like(dg2_ref)
            db2_ref[...] = jnp.zeros_like(db2_ref)
            sq_ref[...] = jnp.zeros_like(sq_ref)

        x1 = xh_ref[...] * g1_ref[...] + b1_ref[...]
        x1b = x1.astype(BF16)
        ff = jnp.zeros((TM_FFN, D_MODEL), F32)
        for j in range(N_FF_BLOCKS):
            r = jnp.maximum(_dot(x1b, w1_ref[j]), 0.0)
            r_ref[:, j * D_MODEL : (j + 1) * D_MODEL] = r.astype(BF16)
            ff = ff + _dot((r * r).astype(BF16), w2_ref[j])
        xhat2, rstd2 = _layer_norm_stats(ALPHA * x1 + ff)
        err = xhat2 * g2_ref[...] + b2_ref[...] - tgt_ref[...]
        sq_ref[...] += jnp.sum(err * err, axis=0, keepdims=True)
        dy = err * (1.0 / D_MODEL)
        dg2_ref[...] += jnp.sum(dy * xhat2, axis=0, keepdims=True)
        db2_ref[...] += jnp.sum(dy, axis=0, keepdims=True)
        dz2_ref[...] = _layer_norm_bwd(dy * g2_ref[...], xhat2, rstd2)

    vec = _const_spec((1, D_MODEL))
    wspec = _const_spec((N_FF_BLOCKS, D_MODEL, D_MODEL), single_buffer=True)
    return pl.pallas_call(
        body,
        name="ffn_fwd_loss",
        grid=(t // TM_FFN,),
        in_specs=[_row_spec(TM_FFN, D_MODEL), vec, vec, wspec, wspec, vec, vec, _row_spec(TM_FFN, D_MODEL)],
        out_specs=[_row_spec(TM_FFN, D_FF), _row_spec(TM_FFN, D_MODEL), vec, vec, vec],
        out_shape=[
            jax.ShapeDtypeStruct((t, D_FF), BF16),
            jax.ShapeDtypeStruct((t, D_MODEL), F32),
            jax.ShapeDtypeStruct((1, D_MODEL), F32),
            jax.ShapeDtypeStruct((1, D_MODEL), F32),
            jax.ShapeDtypeStruct((1, D_MODEL), F32),
        ],
        compiler_params=_params(("arbitrary",)),
    )(xhat1, ln1_g, ln1_b, w1, w2, ln2_g, ln2_b, target)


def _ffn_bwd_ln1(dz2, r, xhat1, rstd1, ln1_g, w1, w2):
    t = dz2.shape[0]

    def body(dz2_ref, r_ref, xh_ref, rstd_ref, g1_ref, w1_ref, w2_ref, dpre_ref, dz1_ref, dg1_ref, db1_ref):
        @pl.when(pl.program_id(0) == 0)
        def _():
            dg1_ref[...] = jnp.zeros_like(dg1_ref)
            db1_ref[...] = jnp.zeros_like(db1_ref)

        dz2 = dz2_ref[...]
        dz2b = dz2.astype(BF16)
        dx1 = ALPHA * dz2
        for j in range(N_FF_BLOCKS):
            cols = slice(j * D_MODEL, (j + 1) * D_MODEL)
            dpre = (_dot(dz2b, w2_ref[j], NT) * (2.0 * r_ref[:, cols].astype(F32))).astype(BF16)
            dpre_ref[:, cols] = dpre
            dx1 = dx1 + _dot(dpre, w1_ref[j], NT)
        xhat1 = xh_ref[...]
        dg1_ref[...] += jnp.sum(dx1 * xhat1, axis=0, keepdims=True)
        db1_ref[...] += jnp.sum(dx1, axis=0, keepdims=True)
        dz1_ref[...] = _layer_norm_bwd(dx1 * g1_ref[...], xhat1, rstd_ref[...])

    vec = _const_spec((1, D_MODEL))
    wspec = _const_spec((N_FF_BLOCKS, D_MODEL, D_MODEL), single_buffer=True)
    return pl.pallas_call(
        body,
        name="ffn_bwd_ln1",
        grid=(t // TM_FFN,),
        in_specs=[_row_spec(TM_FFN, D_MODEL), _row_spec(TM_FFN, D_FF), _row_spec(TM_FFN, D_MODEL), _row_spec(TM_FFN, 1), vec, wspec, wspec],
        out_specs=[_row_spec(TM_FFN, D_FF), _row_spec(TM_FFN, D_MODEL), vec, vec],
        out_shape=[
            jax.ShapeDtypeStruct((t, D_FF), BF16),
            jax.ShapeDtypeStruct((t, D_MODEL), F32),
            jax.ShapeDtypeStruct((1, D_MODEL), F32),
            jax.ShapeDtypeStruct((1, D_MODEL), F32),
        ],
        compiler_params=_params(("arbitrary",)),
    )(dz2, r, xhat1, rstd1, ln1_g, w1, w2)


def _dcat(dz1, w_out, dep=None):
    t = dz1.shape[0]

    def body(dz1_ref, w_ref, dcat_ref):
        dcat_ref[...] = _dot(dz1_ref[...].astype(BF16), w_ref[...], NT)

    body, in_specs, operands = _after(dep, body, [_row_spec(TM, D_MODEL), _const_spec((D_MODEL, D_MODEL))], [dz1, w_out])
    return pl.pallas_call(
        body,
        name="dcat",
        grid=(t // TM,),
        in_specs=in_specs,
        out_specs=_row_spec(TM, D_MODEL),
        out_shape=jax.ShapeDtypeStruct((t, D_MODEL), F32),
        compiler_params=_params(("parallel",)),
    )(*operands)


def _mixer_bwd(u, vg, q, k, va, dcat, cos, sin, v_ln_g, v_ln_b, w_spatial, bias_full, sinks, dep=None):
    t = u.shape[0]
    n_chunks = t // CHUNK

    def body(u_ref, vg_ref, q_ref, kc_ref, kp_ref, vc_ref, vp_ref, dcat_ref, cosc_ref, sinc_ref, cosp_ref, sinp_ref,
             g_ref, b_ref, w_ref, bias_ref, sink_ref,
             dmain_ref, dkv_ref, dg_ref, db_ref, dw_ref, dbs_ref, dsink_ref, dmix_acc, wcat, wcat_t):
        i = pl.program_id(0)
        left = _half_lane_masks(CHUNK)
        lane = lax.broadcasted_iota(jnp.int32, (CHUNK, LANES), 1)
        n_pairs = D_GMLP // LANES

        @pl.when(i == 0)
        def _():
            dg_ref[...] = jnp.zeros_like(dg_ref)
            db_ref[...] = jnp.zeros_like(db_ref)
            dw_ref[...] = jnp.zeros_like(dw_ref)
            dsink_ref[...] = jnp.zeros_like(dsink_ref)
            dmix_acc[...] = jnp.zeros_like(dmix_acc)
            _store_spatial_weights(w_ref, wcat, wcat_t)

        ug, dug_du = _gelu_and_grad(u_ref[...])
        gv, dgv_dv = _gelu_and_grad(vg_ref[...])
        xhat, rstd = _layer_norm_stats(gv)
        gain = g_ref[...]
        vgl = xhat * gain + b_ref[...]
        pair_cols = [slice(p * LANES, (p + 1) * LANES) for p in range(n_pairs)]
        mixed = [_dot(wcat[p], _pair_stack(vgl[:, cols], left)) for p, cols in enumerate(pair_cols)]
        dm_stacks = []
        for p, cols in enumerate(pair_cols):
            da = dcat_ref[:, cols]
            dmain_ref[:, cols] = (da * (mixed[p] + bias_ref[:, cols]) * dug_du[:, cols]).astype(BF16)
            dmixed = da * ug[:, cols]
            dmix_acc[:, cols] += dmixed
            dm_stacks.append(_pair_stack(dmixed, left))
        causal = _causal_mask()
        for p, cols in enumerate(pair_cols):
            dw_pair = _dot(dm_stacks[p], vgl[:, cols].astype(BF16), NT)
            dw_ref[2 * p] += jnp.where(causal, dw_pair[:CHUNK], 0.0)
            dw_ref[2 * p + 1] += jnp.where(causal, dw_pair[CHUNK:], 0.0)
        dvgl = jnp.concatenate([_dot(wcat_t[p], dm_stacks[p]) for p in range(n_pairs)], axis=1)
        dg_ref[...] += jnp.sum(dvgl * xhat, axis=0, keepdims=True)
        db_ref[...] += jnp.sum(dvgl, axis=0, keepdims=True)
        dgv = _layer_norm_bwd(dvgl * gain, xhat, rstd)
        dmain_ref[:, D_GMLP : 2 * D_GMLP] = (dgv * dgv_dv).astype(BF16)

        @pl.when(i == n_chunks - 1)
        def _():
            tile = jnp.zeros((CHUNK, LANES), F32)
            for p in range(D_GMLP // LANES):
                dm = dmix_acc[:, p * LANES : (p + 1) * LANES]
                sl = jnp.sum(jnp.where(left, dm, 0.0), axis=1, keepdims=True)
                sr = jnp.sum(jnp.where(left, 0.0, dm), axis=1, keepdims=True)
                tile = jnp.where(lane == 2 * p, sl, tile)
                tile = jnp.where(lane == 2 * p + 1, sr, tile)
            dbs_ref[...] = tile

        k_var = _kv_variants(jnp.concatenate([kp_ref[...], kc_ref[...]], axis=0))
        v_var = _kv_variants(jnp.concatenate([vp_ref[...], vc_ref[...]], axis=0))
        valid = _band_mask(i)
        n_qpairs = D_ATTN // LANES
        q_pairs = [q_ref[:, p * LANES : (p + 1) * LANES] for p in range(n_qpairs)]
        do_all = dcat_ref[:, D_GMLP:D_MODEL]
        do_pairs = [do_all[:, p * LANES : (p + 1) * LANES].astype(BF16) for p in range(n_qpairs)]
        scores = [_dot(q_pairs[h // 2], k_var[h // 4][h % 2], NT) for h in range(N_HEADS)]
        dprobs = [_dot(do_pairs[h // 2], v_var[h // 4][h % 2], NT) for h in range(N_HEADS)]
        heads = range(N_HEADS)
        sinks_h = [sink_ref[h] for h in heads]
        masked = [jnp.where(valid, scores[h] * SCALE, NEG_INF) for h in heads]
        maxes = [jnp.maximum(jnp.max(masked[h], axis=1, keepdims=True), sinks_h[h]) for h in heads]
        exps = [jnp.exp(masked[h] - maxes[h]) for h in heads]
        exp_sinks = [jnp.exp(sinks_h[h] - maxes[h]) for h in heads]
        invs = [1.0 / (jnp.sum(exps[h], axis=1, keepdims=True) + exp_sinks[h]) for h in heads]
        probs = [exps[h] * invs[h] for h in heads]
        dsums = [jnp.sum(probs[h] * dprobs[h], axis=1, keepdims=True) for h in heads]
        ds_b = [(probs[h] * (dprobs[h] - dsums[h]) * SCALE).astype(BF16) for h in heads]
        probs_b = [probs[h].astype(BF16) for h in heads]
        dsink_row = jnp.zeros((1, LANES), F32)
        lane_row = lax.broadcasted_iota(jnp.int32, (1, LANES), 1)
        for h in heads:
            d_sink = -jnp.sum(exp_sinks[h] * invs[h] * dsums[h], axis=0, keepdims=True)
            dsink_row = dsink_row + jnp.where(lane_row == h, d_sink, 0.0)
        dsink_ref[0:1, :] += dsink_row
        dq_all = jnp.concatenate(
            [_dot(ds_b[2 * p], k_var[p // 2][0]) + _dot(ds_b[2 * p + 1], k_var[p // 2][1]) for p in range(n_qpairs)], axis=1)
        cos_c, sin_c = cosc_ref[...], sinc_ref[...]
        dmain_ref[:, 2 * D_GMLP : D_MAIN] = _rope_transposed(dq_all, _lane_tile(cos_c, n_qpairs), _lane_tile(sin_c, n_qpairs)).astype(BF16)

        q_t = q_ref[...].astype(F32).T.astype(BF16)
        do_t = do_all.T.astype(BF16)
        heads_per_group = N_HEADS // 2

        def group_grad_t(lhs_t, rhs_heads):
            parts = []
            for g in range(2):
                heads = range(g * heads_per_group, (g + 1) * heads_per_group)
                lhs = jnp.concatenate([lhs_t[h * HEAD_DIM : (h + 1) * HEAD_DIM] for h in heads], axis=1)
                parts.append(_dot(lhs, jnp.concatenate([rhs_heads[h] for h in heads], axis=0)))
            return jnp.concatenate(parts, axis=0).T

        dk2 = group_grad_t(q_t, ds_b)
        dv2 = group_grad_t(do_t, probs_b)
        cur = pl.ds(pl.multiple_of(i * CHUNK, CHUNK), CHUNK)
        dkv_ref[cur, 0:D_KV] = _rope_transposed(dk2[CHUNK:], cos_c, sin_c)
        dkv_ref[cur, D_KV : 2 * D_KV] = dv2[CHUNK:]

        @pl.when(i > 0)
        def _():
            prev = pl.ds(pl.multiple_of((i - 1) * CHUNK, CHUNK), CHUNK)
            dkv_ref[prev, 0:D_KV] += _rope_transposed(dk2[:CHUNK], cosp_ref[...], sinp_ref[...])
            dkv_ref[prev, D_KV : 2 * D_KV] += dv2[:CHUNK]

    cur = lambda i: (i, 0)
    prev = lambda i: (jnp.maximum(i - 1, 0), 0)
    in_specs = _chunk_specs() + [
        pl.BlockSpec((CHUNK, D_MODEL), cur),
        pl.BlockSpec((CHUNK, LANES), cur),
        pl.BlockSpec((CHUNK, LANES), cur),
        pl.BlockSpec((CHUNK, LANES), prev),
        pl.BlockSpec((CHUNK, LANES), prev),
        _const_spec((1, D_GMLP)),
        _const_spec((1, D_GMLP)),
        _const_spec((N_HEADS, CHUNK, CHUNK)),
        _const_spec((CHUNK, D_GMLP)),
        pl.BlockSpec(memory_space=pltpu.SMEM),
    ]
    body, in_specs, operands = _after(
        dep, body, in_specs, [u, vg, q, k, k, va, va, dcat, cos, sin, cos, sin, v_ln_g, v_ln_b, w_spatial, bias_full, sinks])
    return pl.pallas_call(
        body,
        name="mixer_bwd",
        grid=(n_chunks,),
        in_specs=in_specs,
        out_specs=[
            pl.BlockSpec((CHUNK, D_MAIN), cur),
            _const_spec((t, 2 * D_KV)),
            _const_spec((1, D_GMLP)),
            _const_spec((1, D_GMLP)),
            _const_spec((N_HEADS, CHUNK, CHUNK)),
            _const_spec((CHUNK, LANES)),
            _const_spec((8, LANES)),
        ],
        out_shape=[
            jax.ShapeDtypeStruct((t, D_MAIN), BF16),
            jax.ShapeDtypeStruct((t, 2 * D_KV), F32),
            jax.ShapeDtypeStruct((1, D_GMLP), F32),
            jax.ShapeDtypeStruct((1, D_GMLP), F32),
            jax.ShapeDtypeStruct((N_HEADS, CHUNK, CHUNK), F32),
            jax.ShapeDtypeStruct((CHUNK, LANES), F32),
            jax.ShapeDtypeStruct((8, LANES), F32),
        ],
        scratch_shapes=[
            pltpu.VMEM((CHUNK, D_GMLP), F32),
            pltpu.VMEM((D_GMLP // LANES, CHUNK, 2 * CHUNK), BF16),
            pltpu.VMEM((D_GMLP // LANES, CHUNK, 2 * CHUNK), BF16),
        ],
        compiler_params=_params(("arbitrary",)),
    )(*operands)


def _grad_x(dh_main, dkv, dz1, w_in_t, dep=None):
    t = dz1.shape[0]

    def body(dm_ref, dkv_ref, dz1_ref, w_ref, gx_ref):
        acc = ALPHA * dz1_ref[...] + _dot(dm_ref[...], w_ref[0:D_MAIN, :])
        gx_ref[...] = acc + _dot(dkv_ref[...].astype(BF16), w_ref[D_MAIN:D_IN, :])

    body, in_specs, operands = _after(
        dep, body, [_row_spec(TM, D_MAIN), _row_spec(TM, 2 * D_KV), _row_spec(TM, D_MODEL), _const_spec((D_IN, D_MODEL))], [dh_main, dkv, dz1, w_in_t])
    return pl.pallas_call(
        body,
        name="grad_x",
        grid=(t // TM,),
        in_specs=in_specs,
        out_specs=_row_spec(TM, D_MODEL),
        out_shape=jax.ShapeDtypeStruct((t, D_MODEL), F32),
        compiler_params=_params(("parallel",)),
    )(*operands)


def _token_contraction(name, n_blocks, out_rows, in_arrays, in_specs, contributions, dep=None):
    t = in_arrays[0].shape[0]
    block_rows = out_rows // n_blocks

    def body(*refs):
        out_ref = refs[-1]

        @pl.when(pl.program_id(1) == 0)
        def _():
            out_ref[...] = jnp.zeros_like(out_ref)

        for row0, a, b in contributions(*refs[:-1]):
            out_ref[row0 : row0 + a.shape[1], :] += _dot(a, b, TN)

    body, in_specs, operands = _after(dep, body, in_specs, in_arrays)
    return pl.pallas_call(
        body,
        name=name,
        grid=(n_blocks, t // TK),
        in_specs=in_specs,
        out_specs=pl.BlockSpec((block_rows, D_MODEL), lambda j, k: (j, 0)),
        out_shape=jax.ShapeDtypeStruct((out_rows, D_MODEL), F32),
        compiler_params=_params(("parallel", "arbitrary")),
    )(*operands)


def _tile_spec(cols):
    return pl.BlockSpec((TK, cols), lambda j, k: (k, 0))


def _tile_block_spec():
    return pl.BlockSpec((TK, D_MODEL), lambda j, k: (k, j))


def _grad_w_in_t(dh_main, dkv, x):
    def contributions(dm_ref, dkv_ref, x_ref):
        xb = x_ref[...].astype(BF16)
        return [(0, dm_ref[...], xb), (D_MAIN, dkv_ref[...].astype(BF16), xb)]

    return _token_contraction("grad_w_in", 1, D_IN, [dh_main, dkv, x], [_tile_spec(D_MAIN), _tile_spec(2 * D_KV), _tile_spec(D_MODEL)], contributions)


def _grad_w_out(cat, dz1, dep=None):
    def contributions(cat_ref, dz1_ref):
        return [(0, cat_ref[...], dz1_ref[...].astype(BF16))]

    return _token_contraction("grad_w_out", 1, D_MODEL, [cat, dz1], [_tile_spec(D_MODEL), _tile_spec(D_MODEL)], contributions, dep)


def _grad_w_ff1(xhat1, ln1_g, ln1_b, dpre):
    def contributions(xh_ref, g_ref, b_ref, dpre_ref):
        return [(0, (xh_ref[...] * g_ref[...] + b_ref[...]).astype(BF16), dpre_ref[...])]

    vec = pl.BlockSpec((1, D_MODEL), lambda j, k: (0, 0))
    return _token_contraction("grad_w_ff1", N_FF_BLOCKS, D_FF, [xhat1, ln1_g, ln1_b, dpre], [_tile_spec(D_MODEL), vec, vec, _tile_block_spec()], contributions)


def _grad_w_ff2(r, dz2):
    def contributions(r_ref, dz2_ref):
        rf = r_ref[...].astype(F32)
        return [(0, (rf * rf).astype(BF16), dz2_ref[...].astype(BF16))]

    return _token_contraction("grad_w_ff2", N_FF_BLOCKS, D_FF, [r, dz2], [_tile_block_spec(), _tile_spec(D_MODEL)], contributions)


ANY = pl.BlockSpec(memory_space=pl.ANY)


def _mesh_position():
    return lax.axis_index("x"), lax.axis_index("y"), lax.axis_index("c")


def _other_chips(x, y):
    return [(1 - x, y), (x, 1 - y), (1 - x, 1 - y)]


def _remote(src, dst, send_sem, recv_sem, device):
    return pltpu.make_async_remote_copy(src_ref=src, dst_ref=dst, send_sem=send_sem, recv_sem=recv_sem, device_id=device, device_id_type=MESH)


def _rows(ref, start, size):
    return ref.at[pl.ds(start, size), :]


def _all_gather_weights(shards):
    n = len(shards)
    per = 7

    def body(*refs):
        ins, outs = refs[:n], refs[n : 2 * n]
        send_sems, recv_sems = refs[2 * n :]
        x, y, c = _mesh_position()
        me = 2 * x + y
        chips = _other_chips(x, y)
        sibling = (x, y, 1 - c)
        started = []
        for w in range(n):
            rows = shards[w].shape[0]
            half = rows // 2
            for kk, (px, py) in enumerate(chips):
                cp = _remote(_rows(ins[w], c * half, half), _rows(outs[w], me * rows + c * half, half),
                             send_sems.at[per * w + kk], recv_sems.at[per * w + kk], (px, py, c))
                cp.start()
                started.append(cp)
            cp = _remote(ins[w], _rows(outs[w], me * rows, rows), send_sems.at[per * w + 6], recv_sems.at[per * w + 6], sibling)
            cp.start()
            started.append(cp)
        for w in range(n):
            rows = shards[w].shape[0]
            half = rows // 2
            for kk, (px, py) in enumerate(chips):
                blk = _rows(outs[w], (2 * px + py) * rows + c * half, half)
                _remote(blk, blk, send_sems.at[per * w + kk], recv_sems.at[per * w + kk], (px, py, c)).wait_recv()
                fwd = _remote(blk, blk, send_sems.at[per * w + 3 + kk], recv_sems.at[per * w + 3 + kk], sibling)
                fwd.start()
                started.append(fwd)
        for w in range(n):
            rows = shards[w].shape[0]
            half = rows // 2
            for kk, (px, py) in enumerate(chips):
                blk = _rows(outs[w], (2 * px + py) * rows + (1 - c) * half, half)
                _remote(blk, blk, send_sems.at[per * w + 3 + kk], recv_sems.at[per * w + 3 + kk], sibling).wait_recv()
            own = _rows(outs[w], me * rows, rows)
            _remote(own, own, send_sems.at[per * w + 6], recv_sems.at[per * w + 6], sibling).wait_recv()
        for cp in started:
            cp.wait_send()

    return pl.pallas_call(
        body,
        name="all_gather_weights",
        in_specs=[ANY] * n,
        out_specs=[ANY] * n,
        out_shape=[jax.ShapeDtypeStruct((N_CHIPS * s.shape[0], s.shape[1]), s.dtype) for s in shards],
        scratch_shapes=[pltpu.SemaphoreType.DMA((per * n,)), pltpu.SemaphoreType.DMA((per * n,))],
    )(*shards)


def _pair_swap(grads):
    n = len(grads)

    def body(*refs):
        ins, theirs = refs[:n], refs[n : 2 * n]
        send_sems, recv_sems = refs[2 * n :]
        x, y, c = _mesh_position()
        sibling = (x, y, 1 - c)
        sends = []
        for w in range(n):
            rows = grads[w].shape[0] // N_CHIPS
            half = rows // 2
            for j in range(N_CHIPS):
                cp = _remote(_rows(ins[w], j * rows + (1 - c) * half, half), _rows(theirs[w], j * half, half),
                             send_sems.at[4 * w + j], recv_sems.at[4 * w + j], sibling)
                cp.start()
                sends.append(cp)
        for cp in sends:
            cp.wait_recv()
        for cp in sends:
            cp.wait_send()

    return pl.pallas_call(
        body,
        name="grad_pair_swap",
        in_specs=[ANY] * n,
        out_specs=[ANY] * n,
        out_shape=[jax.ShapeDtypeStruct((g.shape[0] // 2, g.shape[1]), g.dtype) for g in grads],
        scratch_shapes=[pltpu.SemaphoreType.DMA((4 * n,)), pltpu.SemaphoreType.DMA((4 * n,))],
    )(*grads)


def _chip_exchange(partials):
    n = len(partials)

    def body(*refs):
        ins, outs = refs[:n], refs[n : 2 * n]
        send_sems, recv_sems = refs[2 * n :]
        x, y, c = _mesh_position()
        chips = _other_chips(x, y)
        sends = []
        for w in range(n):
            half = partials[w].shape[0] // N_CHIPS
            for kk, (px, py) in enumerate(chips):
                cp = _remote(_rows(ins[w], (2 * px + py) * half, half), _rows(outs[w], kk * half, half),
                             send_sems.at[3 * w + kk], recv_sems.at[3 * w + kk], (px, py, c))
                cp.start()
                sends.append(cp)
        for cp in sends:
            cp.wait_recv()
        for cp in sends:
            cp.wait_send()

    return pl.pallas_call(
        body,
        name="grad_chip_exchange",
        in_specs=[ANY] * n,
        out_specs=[ANY] * n,
        out_shape=[jax.ShapeDtypeStruct((3 * p.shape[0] // N_CHIPS, p.shape[1]), p.dtype) for p in partials],
        scratch_shapes=[pltpu.SemaphoreType.DMA((3 * n,)), pltpu.SemaphoreType.DMA((3 * n,))],
    )(*partials)


def _pair_gather(shards):
    n = len(shards)

    def body(*refs):
        outs = refs[n : 2 * n]
        send_sems, recv_sems = refs[2 * n :]
        x, y, c = _mesh_position()
        sibling = (x, y, 1 - c)
        sends = []
        for w in range(n):
            half = shards[w].shape[0] // 2
            mine = _rows(outs[w], c * half, half)
            cp = _remote(mine, mine, send_sems.at[w], recv_sems.at[w], sibling)
            cp.start()
            sends.append(cp)
        for w in range(n):
            half = shards[w].shape[0] // 2
            blk = _rows(outs[w], (1 - c) * half, half)
            _remote(blk, blk, send_sems.at[w], recv_sems.at[w], sibling).wait_recv()
        for cp in sends:
            cp.wait_send()

    return pl.pallas_call(
        body,
        name="grad_pair_gather",
        in_specs=[ANY] * n,
        out_specs=[ANY] * n,
        out_shape=[jax.ShapeDtypeStruct(s.shape, s.dtype) for s in shards],
        input_output_aliases={w: w for w in range(n)},
        scratch_shapes=[pltpu.SemaphoreType.DMA((n,)), pltpu.SemaphoreType.DMA((n,))],
    )(*shards)


def _all_reduce_small(slab):
    rows = slab.shape[0]
    part = rows // 8

    def body(slab_ref, out_ref, landing, reduced, send_sems, recv_sems):
        x, y, c = _mesh_position()
        me = 4 * x + 2 * y + c
        flips = [(k >> 2, (k >> 1) & 1, k & 1) for k in range(1, 8)]

        def peer(flip):
            fx, fy, fc = flip
            return (1 - x if fx else x, 1 - y if fy else y, 1 - c if fc else c)

        def my_rows(ref):
            return ref.at[pl.ds(pl.multiple_of(me * part, 8), part), :]

        sends = []
        for kk, flip in enumerate(flips):
            px, py, pc = peer(flip)
            them = 4 * px + 2 * py + pc
            cp = _remote(slab_ref.at[pl.ds(pl.multiple_of(them * part, 8), part), :], landing.at[me], send_sems.at[kk], recv_sems.at[kk], (px, py, pc))
            cp.start()
            sends.append(cp)
        landing[me] = my_rows(slab_ref)[...]
        for kk, flip in enumerate(flips):
            px, py, pc = peer(flip)
            them = 4 * px + 2 * py + pc
            _remote(landing.at[them], landing.at[them], send_sems.at[kk], recv_sems.at[kk], (px, py, pc)).wait_recv()
        total = landing[0]
        for s in range(1, 8):
            total = total + landing[s]
        reduced[...] = total
        my_rows(out_ref)[...] = total
        for kk, flip in enumerate(flips):
            cp = _remote(reduced, my_rows(out_ref), send_sems.at[7 + kk], recv_sems.at[7 + kk], peer(flip))
            cp.start()
            sends.append(cp)
        for kk, flip in enumerate(flips):
            px, py, pc = peer(flip)
            them = 4 * px + 2 * py + pc
            blk = out_ref.at[pl.ds(pl.multiple_of(them * part, 8), part), :]
            _remote(blk, blk, send_sems.at[7 + kk], recv_sems.at[7 + kk], (px, py, pc)).wait_recv()
        for cp in sends:
            cp.wait_send()

    vmem = pl.BlockSpec(memory_space=pltpu.VMEM)
    return pl.pallas_call(
        body,
        name="all_reduce_small",
        in_specs=[vmem],
        out_specs=vmem,
        out_shape=jax.ShapeDtypeStruct(slab.shape, slab.dtype),
        scratch_shapes=[pltpu.VMEM((8, part, LANES), F32), pltpu.VMEM((part, LANES), F32), pltpu.SemaphoreType.DMA((14,)), pltpu.SemaphoreType.DMA((14,))],
    )(slab)


HBM = pl.BlockSpec(memory_space=pltpu.HBM)
SEM = pl.BlockSpec(memory_space=pltpu.SEMAPHORE)
DATAFLOW = pltpu.SideEffectType.DATAFLOW_SIDE_EFFECTING
TOKEN = jax.ShapeDtypeStruct((8, LANES), F32)


def _plan_copies(bufs, plan, send_sems, recv_sems):
    out = []
    for i, (src, src_row, dst, dst_row, recv_row, rows, device) in enumerate(plan):
        send = _remote(_rows(bufs[src], src_row, rows), _rows(bufs[dst], dst_row, rows), send_sems.at[i], recv_sems.at[i], device)
        landed = _rows(bufs[dst], recv_row, rows)
        recv = _remote(landed, landed, send_sems.at[i], recv_sems.at[i], device)
        out.append((send, recv))
    return out


def _split_call(name, bufs, wait=None, start=None, after=None):
    n = len(bufs)
    n_in = n + (2 if wait else 0) + (1 if after is not None else 0)
    n_start = len(start(0, 0, 0)) if start else 0

    def body(*refs):
        ins = refs[:n]
        x, y, c = _mesh_position()
        if wait:
            for send, recv in _plan_copies(ins, wait[0](x, y, c), refs[n], refs[n + 1]):
                recv.wait_recv()
                send.wait_send()
        if start:
            for send, _ in _plan_copies(ins, start(x, y, c), refs[n_in + n + 1], refs[n_in + n + 2]):
                send.start()
        token = refs[n_in + n]
        token[...] = jnp.zeros_like(token)

    operands = [pltpu.with_memory_space_constraint(b, pltpu.HBM) for b in bufs]
    in_specs = [HBM] * n
    if wait:
        operands += [wait[1], wait[2]]
        in_specs += [SEM, SEM]
    if after is not None:
        operands.append(after)
        in_specs.append(ANY)
    out_shape = [pltpu.HBM(b.shape, b.dtype) for b in bufs] + [TOKEN]
    out_specs = [HBM] * n + [pl.BlockSpec(memory_space=pltpu.VMEM)]
    if start:
        out_shape += [pltpu.SemaphoreType.DMA((n_start,)), pltpu.SemaphoreType.DMA((n_start,))]
        out_specs += [SEM, SEM]
    outs = pl.pallas_call(
        body,
        name=name,
        in_specs=in_specs,
        out_specs=out_specs,
        out_shape=out_shape,
        input_output_aliases={i: i for i in range(n)},
        compiler_params=pltpu.CompilerParams(has_side_effects=DATAFLOW),
    )(*operands)
    return (list(outs[:n]), outs[n]) + tuple(outs[n + 1 :])


def _gather_plans(shard_rows):
    n = len(shard_rows)

    def ici(x, y, c):
        me = 2 * x + y
        plan = []
        for w, rows in enumerate(shard_rows):
            half = rows // 2
            for px, py in _other_chips(x, y):
                plan.append((w, c * half, n + w, me * rows + c * half, (2 * px + py) * rows + c * half, half, (px, py, c)))
            plan.append((w, 0, n + w, me * rows, me * rows, rows, (x, y, 1 - c)))
        return plan

    def passed_on(x, y, c):
        plan = []
        for w, rows in enumerate(shard_rows):
            half = rows // 2
            for px, py in _other_chips(x, y):
                row = (2 * px + py) * rows
                plan.append((n + w, row + c * half, n + w, row + c * half, row + (1 - c) * half, half, (x, y, 1 - c)))
        return plan

    return ici, passed_on


def _swap_plan(block_rows):
    n = len(block_rows)

    def plan_fn(x, y, c):
        plan = []
        for w, rows in enumerate(block_rows):
            half = rows // 2
            for j in range(N_CHIPS):
                plan.append((w, j * rows + (1 - c) * half, n + w, j * half, j * half, half, (x, y, 1 - c)))
        return plan

    return plan_fn


def _exchange_plan(halves):
    n = len(halves)

    def plan_fn(x, y, c):
        plan = []
        for w, half in enumerate(halves):
            for kk, (px, py) in enumerate(_other_chips(x, y)):
                plan.append((w, (2 * px + py) * half, n + w, kk * half, kk * half, half, (px, py, c)))
        return plan

    return plan_fn


def _landing(rows, cols, dtype):
    return lax.empty((rows, cols), dtype)


def _row_tile(rows, cap=512):
    best = 8
    for cand in range(8, cap + 1, 8):
        if rows % cand == 0:
            best = cand
    return best


def _pair_sum(name, grad, theirs, pos):
    half = theirs.shape[0] // N_CHIPS
    cols = theirs.shape[1]
    tile = _row_tile(half)
    steps = half // tile

    def body(pos_ref, g_ref, t_ref, p_ref, own_ref):
        total = g_ref[...] + t_ref[...]
        p_ref[...] = total.astype(BF16)

        @pl.when(pl.program_id(1) == pos_ref[1])
        def _():
            own_ref[...] = total

    return pl.pallas_call(
        body,
        name=name,
        grid_spec=pltpu.PrefetchScalarGridSpec(
            num_scalar_prefetch=1,
            grid=(steps, N_CHIPS),
            in_specs=[
                pl.BlockSpec((tile, cols), lambda i, j, pos: ((2 * j + pos[0]) * steps + i, 0)),
                pl.BlockSpec((tile, cols), lambda i, j, pos: (j * steps + i, 0)),
            ],
            out_specs=[
                pl.BlockSpec((tile, cols), lambda i, j, pos: (j * steps + i, 0)),
                pl.BlockSpec((tile, cols), lambda i, j, pos: (i, 0)),
            ],
        ),
        out_shape=[jax.ShapeDtypeStruct((N_CHIPS * half, cols), BF16), jax.ShapeDtypeStruct((half, cols), F32)],
        compiler_params=_params(("parallel", "arbitrary")),
    )(pos, grad, theirs)


def _chip_sum(name, own, landed, pos):
    half, cols = own.shape
    tile = _row_tile(half)
    steps = half // tile

    def body(pos_ref, own_ref, l0, l1, l2, o_ref):
        o_ref[...] = ((own_ref[...] + l0[...].astype(F32)) + l1[...].astype(F32)) + l2[...].astype(F32)

    landed_specs = [pl.BlockSpec((tile, cols), lambda i, pos, _k=k: (_k * steps + i, 0)) for k in range(N_CHIPS - 1)]
    return pl.pallas_call(
        body,
        name=name,
        grid_spec=pltpu.PrefetchScalarGridSpec(
            num_scalar_prefetch=1,
            grid=(steps,),
            in_specs=[pl.BlockSpec((tile, cols), lambda i, pos: (i, 0))] + landed_specs,
            out_specs=pl.BlockSpec((tile, cols), lambda i, pos: (pos[0] * steps + i, 0)),
        ),
        out_shape=jax.ShapeDtypeStruct((2 * half, cols), F32),
        compiler_params=_params(("parallel",)),
    )(pos, own, landed, landed, landed)


def _adamw(name, w, g, m, v):
    rows, cols = w.shape
    tile = rows if rows * cols <= 256 * 1024 else _row_tile(rows)

    def body(w_ref, g_ref, m_ref, v_ref, g_out_ref, d_ref, nm_ref, nv_ref):
        g = g_ref[...]
        g_out_ref[...] = g
        nm = ADAM_B1 * m_ref[...] + (1.0 - ADAM_B1) * g
        nv = ADAM_B2 * v_ref[...] + (1.0 - ADAM_B2) * (g * g)
        m_hat = nm / (1.0 - ADAM_B1**ADAM_STEP)
        v_hat = nv / (1.0 - ADAM_B2**ADAM_STEP)
        d_ref[...] = -ADAM_LR * (m_hat / (jnp.sqrt(v_hat) + ADAM_EPS) + ADAM_WD * w_ref[...])
        nm_ref[...] = nm
        nv_ref[...] = nv

    spec = _row_spec(tile, cols)
    return pl.pallas_call(
        body,
        name=name,
        grid=(rows // tile,),
        in_specs=[spec] * 4,
        out_specs=[spec] * 4,
        out_shape=[jax.ShapeDtypeStruct((rows, cols), F32)] * 4,
        compiler_params=_params(("parallel",)),
    )(w, g, m, v)


_SMALL = (
    ("v_ln_g", (D_GMLP,), 8),
    ("v_ln_b", (D_GMLP,), 8),
    ("w_spatial", (N_HEADS, CHUNK, CHUNK), 1024),
    ("b_spatial", (N_HEADS, CHUNK), 8),
    ("sinks", (N_HEADS,), 8),
    ("ln1_g", (D_MODEL,), 8),
    ("ln1_b", (D_MODEL,), 8),
    ("ln2_g", (D_MODEL,), 8),
    ("ln2_b", (D_MODEL,), 8),
    ("squared_error", (D_MODEL,), 8),
)
N_SMALL_PARAMS = len(_SMALL) - 1


def _pack_small(values):
    parts = []
    for (name, shape, rows), val in zip(_SMALL, values, strict=True):
        flat = val.reshape(-1).astype(F32)
        parts.append(jnp.pad(flat, (0, rows * LANES - flat.shape[0])).reshape(rows, LANES))
    parts.append(jnp.zeros((SMALL_ROWS - sum(rows for _, _, rows in _SMALL), LANES), F32))
    return jnp.concatenate(parts, axis=0)


def _adamw_update(w, g, m, v):
    nm = ADAM_B1 * m + (1.0 - ADAM_B1) * g
    nv = ADAM_B2 * v + (1.0 - ADAM_B2) * (g * g)
    m_hat = nm / (1.0 - ADAM_B1**ADAM_STEP)
    v_hat = nv / (1.0 - ADAM_B2**ADAM_STEP)
    return -ADAM_LR * (m_hat / (jnp.sqrt(v_hat) + ADAM_EPS) + ADAM_WD * w), nm, nv


def _adamw_small(g_slab, params, first, second):
    n = N_SMALL_PARAMS

    def pieces(shape):
        if len(shape) == 3:
            return [((0, h), h * shape[1], shape[1], shape[2]) for h in range(shape[0])]
        if len(shape) == 2:
            return [((0,), 0, shape[0], shape[1])]
        if shape[0] >= LANES:
            return [((slice(None), slice(r * LANES, (r + 1) * LANES)), r, 1, LANES) for r in range(shape[0] // LANES)]
        return [((slice(None), slice(0, shape[0])), 0, 1, shape[0])]

    def body(*refs):
        g_ref = refs[0]
        w_refs, m_refs, v_refs = refs[1 : 1 + n], refs[1 + n : 1 + 2 * n], refs[1 + 2 * n : 1 + 3 * n]
        outs = refs[1 + 3 * n :]
        row0 = 0
        for idx, (_, shape, rows) in enumerate(_SMALL[:n]):
            for where, first_row, n_rows, lanes in pieces(shape):
                g = g_ref[row0 + first_row : row0 + first_row + n_rows, 0:lanes]
                delta, nm, nv = _adamw_update(w_refs[idx][where], g, m_refs[idx][where], v_refs[idx][where])
                for group, val in enumerate((g, delta, nm, nv)):
                    outs[group * n + idx][where] = val
            row0 += rows

    vmem = pl.BlockSpec(memory_space=pltpu.VMEM)
    shapes = [jax.ShapeDtypeStruct(p.shape, F32) for p in params]
    outs = pl.pallas_call(
        body,
        name="adamw_small",
        in_specs=[vmem] * (1 + 3 * n),
        out_specs=[vmem] * (4 * n),
        out_shape=shapes * 4,
        compiler_params=_params(),
    )(g_slab, *params, *first, *second)
    return [list(outs[group * n : (group + 1) * n]) for group in range(4)]


def kernel(x, positions, w_in, v_ln_g, v_ln_b, w_spatial, b_spatial, sinks, w_out, ln1_g, ln1_b, w_ff1, w_ff2, ln2_g, ln2_b, loss_target, m_w_in, m_v_ln_g, m_v_ln_b, m_w_spatial, m_b_spatial, m_sinks, m_w_out, m_ln1_g, m_ln1_b, m_w_ff1, m_w_ff2, m_ln2_g, m_ln2_b, v_w_in, v_v_ln_g, v_v_ln_b, v_w_spatial, v_b_spatial, v_sinks, v_w_out, v_ln1_g, v_ln1_b, v_w_ff1, v_w_ff2, v_ln2_g, v_ln2_b):
    t = x.shape[1]
    x2 = x.reshape(t, D_MODEL)
    target = loss_target.reshape(t, D_MODEL)

    (w_in_t,) = _all_gather_weights([w_in[0].T.astype(BF16)])
    later = [w_out[0].astype(BF16), w_ff1[0].astype(BF16), w_ff2[0].astype(BF16)]
    later_rows = [s.shape[0] for s in later]
    ici_plan, pass_plan = _gather_plans(later_rows)
    bufs, started, ici_send, ici_recv = _split_call(
        "gather_start", later + [_landing(N_CHIPS * r, D_MODEL, BF16) for r in later_rows], start=ici_plan, after=w_in_t)

    inv_freq = ROPE_THETA ** (-jnp.arange(0, HEAD_DIM, 2, dtype=F32) / HEAD_DIM)
    cos, sin = _rope_tables(positions.reshape(t, 1), jnp.tile(inv_freq, LANES // (HEAD_DIM // 2)).reshape(1, LANES))
    u, vg, q, k, va = _in_proj(x2, w_in_t, cos, sin, dep=started)
    bias_full = jnp.repeat(b_spatial[0].T, HEAD_DIM, axis=1)
    sink_vec = sinks.reshape(N_HEADS)
    cat = _mixer_fwd(u, vg, q, k, va, v_ln_g, v_ln_b, w_spatial[0], bias_full, sink_vec)
    bufs, passed, pass_send, pass_recv = _split_call("gather_pass", bufs, wait=(ici_plan, ici_send, ici_recv), start=pass_plan, after=cat)
    bufs, _ = _split_call("gather_end", bufs, wait=(pass_plan, pass_send, pass_recv), after=passed)
    w_out_all = bufs[3]
    w1_all = bufs[4].reshape(N_FF_BLOCKS, D_MODEL, D_MODEL)
    w2_all = bufs[5].reshape(N_FF_BLOCKS, D_MODEL, D_MODEL)
    xhat1, rstd1 = _out_proj_ln1(cat, x2, w_out_all)
    r, dz2, d_ln2_g, d_ln2_b, sq_err = _ffn_fwd_loss(xhat1, ln1_g, ln1_b, w1_all, w2_all, ln2_g, ln2_b, target)

    pos = jnp.stack([lax.axis_index("c"), 2 * lax.axis_index("x") + lax.axis_index("y")]).astype(jnp.int32)
    g_ff2_local = _grad_w_ff2(r, dz2)
    dpre, dz1, d_ln1_g, d_ln1_b = _ffn_bwd_ln1(dz2, r, xhat1, rstd1, ln1_g, w1_all, w2_all)
    g_ff1_local = _grad_w_ff1(xhat1, ln1_g, ln1_b, dpre)
    ff_grads = [g_ff1_local, g_ff2_local]
    ff_rows = [g.shape[0] // N_CHIPS for g in ff_grads]
    swap_plan = _swap_plan(ff_rows)
    bufs, swapping, swap_send, swap_recv = _split_call(
        "ff_swap_start", ff_grads + [_landing(g.shape[0] // 2, D_MODEL, F32) for g in ff_grads], start=swap_plan)
    dcat = _dcat(dz1, w_out_all, dep=swapping)
    g_out_local = _grad_w_out(cat, dz1, dep=dcat)
    bufs, _ = _split_call("ff_swap_wait", bufs, wait=(swap_plan, swap_send, swap_recv), after=g_out_local)
    ff_sums = [_pair_sum("grad_pair_sum_" + nm, g, th, pos) for nm, g, th in zip(["w_ff1", "w_ff2"], bufs[:2], bufs[2:])]
    ff_halves = [p.shape[0] // N_CHIPS for p, _ in ff_sums]
    exchange_plan = _exchange_plan(ff_halves)
    bufs, exchanging, ex_send, ex_recv = _split_call(
        "ff_exchange_start", [p for p, _ in ff_sums] + [_landing(3 * h, D_MODEL, BF16) for h in ff_halves], start=exchange_plan)
    dh_main, dkv, d_v_ln_g, d_v_ln_b, d_w_spatial, d_b_spatial_t, d_sinks = _mixer_bwd(
        u, vg, q, k, va, dcat, cos, sin, v_ln_g, v_ln_b, w_spatial[0], bias_full, sink_vec, dep=exchanging)
    g_in_local = _grad_w_in_t(dh_main, dkv, x2)

    small = [g_in_local, g_out_local]
    theirs = _pair_swap(small)
    pair_sums = [_pair_sum("grad_pair_sum_" + nm, g, th, pos) for nm, g, th in zip(["w_in", "w_out"], small, theirs)]
    small_halves = [p.shape[0] // N_CHIPS for p, _ in pair_sums]
    small_plan = _exchange_plan(small_halves)
    small_bufs, small_exchanging, sm_send, sm_recv = _split_call(
        "small_exchange_start", [p for p, _ in pair_sums] + [_landing(3 * h, D_MODEL, BF16) for h in small_halves], start=small_plan)
    grad_x_flat = _grad_x(dh_main, dkv, dz1, w_in_t, dep=small_exchanging)
    grad_x = grad_x_flat.reshape(1, t, D_MODEL)
    bufs, ff_done = _split_call("ff_exchange_wait", bufs, wait=(exchange_plan, ex_send, ex_recv), after=grad_x_flat)
    ff_shards = [_chip_sum("grad_chip_sum_" + nm, own, ld, pos) for nm, (_, own), ld in zip(["w_ff1", "w_ff2"], ff_sums, bufs[2:])]
    small_bufs, _ = _split_call("small_exchange_wait", small_bufs, wait=(small_plan, sm_send, sm_recv), after=ff_done)
    shards = [_chip_sum("grad_chip_sum_" + nm, own, ld, pos) for nm, (_, own), ld in zip(["w_in", "w_out"], pair_sums, small_bufs[2:])]
    g_w_in_t, g_w_out, g_w_ff1, g_w_ff2 = _pair_gather(shards + ff_shards)
    g_w_in = g_w_in_t.T

    small_g = _all_reduce_small(_pack_small(
        [d_v_ln_g, d_v_ln_b, d_w_spatial, d_b_spatial_t[:, :N_HEADS].T, d_sinks[0, :N_HEADS], d_ln1_g, d_ln1_b, d_ln2_g, d_ln2_b, sq_err]))
    sq_row = sum(rows for _, _, rows in _SMALL[:N_SMALL_PARAMS])
    loss = 0.5 * jnp.sum(small_g[sq_row : sq_row + _SMALL[N_SMALL_PARAMS][2]]) / D_MODEL

    g_w_in, d_w_in, nm_w_in, nv_w_in = _adamw("adamw_w_in", w_in[0], g_w_in, m_w_in[0], v_w_in[0])
    g_w_out, d_w_out, nm_w_out, nv_w_out = _adamw("adamw_w_out", w_out[0], g_w_out, m_w_out[0], v_w_out[0])
    g_w_ff1, d_w_ff1, nm_w_ff1, nv_w_ff1 = _adamw("adamw_w_ff1", w_ff1[0], g_w_ff1, m_w_ff1[0], v_w_ff1[0])
    g_w_ff2, d_w_ff2, nm_w_ff2, nv_w_ff2 = _adamw("adamw_w_ff2", w_ff2[0], g_w_ff2, m_w_ff2[0], v_w_ff2[0])
    small_grads, small_d, small_nm, small_nv = _adamw_small(
        small_g,
        [v_ln_g, v_ln_b, w_spatial, b_spatial, sinks, ln1_g, ln1_b, ln2_g, ln2_b],
        [m_v_ln_g, m_v_ln_b, m_w_spatial, m_b_spatial, m_sinks, m_ln1_g, m_ln1_b, m_ln2_g, m_ln2_b],
        [v_v_ln_g, v_v_ln_b, v_w_spatial, v_b_spatial, v_sinks, v_ln1_g, v_ln1_b, v_ln2_g, v_ln2_b])

    def with_big(small, w_in_v, w_out_v, w_ff1_v, w_ff2_v):
        g_vg, g_vb, g_ws, g_bs, g_sk, g_1g, g_1b, g_2g, g_2b = small
        return [w_in_v[None], g_vg, g_vb, g_ws, g_bs, g_sk, w_out_v[None], g_1g, g_1b, w_ff1_v[None], w_ff2_v[None], g_2g, g_2b]

    return (
        loss,
        grad_x,
        *with_big(small_grads, g_w_in, g_w_out, g_w_ff1, g_w_ff2),
        *with_big(small_d, d_w_in, d_w_out, d_w_ff1, d_w_ff2),
        *with_big(small_nm, nm_w_in, nm_w_out, nm_w_ff1, nm_w_ff2),
        *with_big(small_nv, nv_w_in, nv_w_out, nv_w_ff1, nv_w_ff2),
    )
```

```python
import functools
import math

import jax
import jax.numpy as jnp
from jax import lax
from jax.experimental import pallas as pl
from jax.experimental.pallas import tpu as pltpu

F32 = jnp.float32
BF16 = jnp.bfloat16

D_MODEL = 1024
HEAD_DIM = 64
D_GMLP = 512
D_ATTN = 512
D_KV = 128
D_IN = 2 * D_GMLP + D_ATTN + 2 * D_KV
D_MAIN = 2 * D_GMLP + D_ATTN
N_HEADS = 8
CHUNK = 128
ROPE_THETA = 10000.0
D_FF = 4 * D_MODEL
N_FF_BLOCKS = 4
LN_EPS = 1e-5
ALPHA = (2.0 * 1) ** 0.25
NEG_INF = -1e30
SCALE = 1.0 / math.sqrt(HEAD_DIM)

ADAM_LR = 0.001
ADAM_B1 = 0.9
ADAM_B2 = 0.999
ADAM_EPS = 1e-08
ADAM_WD = 0.01
ADAM_STEP = 10

N_CHIPS = 4
LANES = 128
V7X_VMEM_BYTES = 64 * 1024 * 1024
VMEM_LIMIT = V7X_VMEM_BYTES - 8 * 1024 * 1024
TM = 512
TM_FFN = 256
TK = 1024
TK_FF = 512
SMALL_ROWS = 1152
MESH = pl.DeviceIdType.MESH

NT = (((1,), (1,)), ((), ()))
TN = (((0,), (0,)), ((), ()))


def _dot(a, b, dims=None):
    if dims is None:
        return jnp.dot(a, b, preferred_element_type=F32)
    return lax.dot_general(a, b, dims, preferred_element_type=F32)


def _params(semantics=None):
    return pltpu.CompilerParams(dimension_semantics=semantics, vmem_limit_bytes=VMEM_LIMIT)


def _const_spec(shape, single_buffer=False):
    zeros = (0,) * len(shape)
    if single_buffer:
        return pl.BlockSpec(shape, lambda *_: zeros, pipeline_mode=pl.Buffered(1))
    return pl.BlockSpec(shape, lambda *_: zeros)


def _row_spec(rows, cols):
    return pl.BlockSpec((rows, cols), lambda i: (i, 0))


def _after(dep, body, in_specs, operands):
    if dep is None:
        return body, list(in_specs), list(operands)
    return (lambda dep_ref, *refs: body(*refs)), [pl.BlockSpec(memory_space=pl.ANY)] + list(in_specs), [dep] + list(operands)


def _gelu(x):
    k = math.sqrt(2.0 / math.pi)
    return 0.5 * x * (1.0 + jnp.tanh(k * (x + 0.044715 * (x * x * x))))


def _gelu_and_grad(x):
    k = math.sqrt(2.0 / math.pi)
    x2 = x * x
    t = jnp.tanh(k * (x + 0.044715 * (x2 * x)))
    g = 0.5 * x * (1.0 + t)
    dg = 0.5 * (1.0 + t) + 0.5 * x * (1.0 - t * t) * (k * (1.0 + 3.0 * 0.044715 * x2))
    return g, dg


def _layer_norm_stats(z):
    mu = jnp.mean(z, axis=-1, keepdims=True)
    zc = z - mu
    var = jnp.mean(zc * zc, axis=-1, keepdims=True)
    rstd = lax.rsqrt(var + LN_EPS)
    return zc * rstd, rstd


def _layer_norm_bwd(dxhat, xhat, rstd):
    m1 = jnp.mean(dxhat, axis=-1, keepdims=True)
    m2 = jnp.mean(dxhat * xhat, axis=-1, keepdims=True)
    return rstd * (dxhat - m1 - xhat * m2)


def _rotate_half(t):
    n = t.shape[1]
    lane = lax.broadcasted_iota(jnp.int32, t.shape, 1)
    first = (lane & (HEAD_DIM // 2)) == 0
    return jnp.where(first, -pltpu.roll(t, n - HEAD_DIM // 2, 1), pltpu.roll(t, HEAD_DIM // 2, 1))


def _rope(t, cos, sin):
    return t * cos + _rotate_half(t) * sin


def _rope_transposed(g, cos, sin):
    return g * cos - _rotate_half(g * sin)


def _lane_tile(a, reps):
    return jnp.tile(a, (1, reps)) if reps > 1 else a


def _rope_tables(pos_row, inv_freq_row):
    t = pos_row.shape[1]

    def body(pos_ref, f_ref, cos_ref, sin_ref):
        pos_rows = jnp.broadcast_to(pos_ref[...].astype(F32), (LANES, TM)).T
        ang = pos_rows * f_ref[...]
        cos_ref[...] = jnp.cos(ang)
        sin_ref[...] = jnp.sin(ang)

    return pl.pallas_call(
        body,
        name="rope_tables",
        grid=(t // TM,),
        in_specs=[pl.BlockSpec((1, TM), lambda i: (0, i)), _const_spec((1, LANES))],
        out_specs=[_row_spec(TM, LANES), _row_spec(TM, LANES)],
        out_shape=[jax.ShapeDtypeStruct((t, LANES), F32)] * 2,
        compiler_params=_params(("parallel",)),
    )(pos_row, inv_freq_row)


def _in_proj(x, w_in_t, cos, sin, dep=None):
    t = x.shape[0]

    def body(x_ref, w_ref, cos_ref, sin_ref, u_ref, vg_ref, q_ref, k_ref, va_ref):
        xb = x_ref[...].astype(BF16)
        u_ref[...] = _dot(xb, w_ref[0:D_GMLP, :], NT)
        vg_ref[...] = _dot(xb, w_ref[D_GMLP : 2 * D_GMLP, :], NT)
        q = _dot(xb, w_ref[2 * D_GMLP : D_MAIN, :], NT)
        k = _dot(xb, w_ref[D_MAIN : D_MAIN + D_KV, :], NT)
        va_ref[...] = _dot(xb, w_ref[D_MAIN + D_KV : D_IN, :], NT).astype(BF16)
        c, s = cos_ref[...], sin_ref[...]
        q_ref[...] = _rope(q, _lane_tile(c, D_ATTN // LANES), _lane_tile(s, D_ATTN // LANES)).astype(BF16)
        k_ref[...] = _rope(k, c, s).astype(BF16)

    body, in_specs, operands = _after(
        dep, body, [_row_spec(TM, D_MODEL), _const_spec((D_IN, D_MODEL)), _row_spec(TM, LANES), _row_spec(TM, LANES)], [x, w_in_t, cos, sin])
    return pl.pallas_call(
        body,
        name="in_proj",
        grid=(t // TM,),
        in_specs=in_specs,
        out_specs=[_row_spec(TM, D_GMLP), _row_spec(TM, D_GMLP), _row_spec(TM, D_ATTN), _row_spec(TM, D_KV), _row_spec(TM, D_KV)],
        out_shape=[
            jax.ShapeDtypeStruct((t, D_GMLP), F32),
            jax.ShapeDtypeStruct((t, D_GMLP), F32),
            jax.ShapeDtypeStruct((t, D_ATTN), BF16),
            jax.ShapeDtypeStruct((t, D_KV), BF16),
            jax.ShapeDtypeStruct((t, D_KV), BF16),
        ],
        compiler_params=_params(("parallel",)),
    )(*operands)


def _chunk_specs():
    cur = lambda i: (i, 0)
    prev = lambda i: (jnp.maximum(i - 1, 0), 0)
    return [
        pl.BlockSpec((CHUNK, D_GMLP), cur),
        pl.BlockSpec((CHUNK, D_GMLP), cur),
        pl.BlockSpec((CHUNK, D_ATTN), cur),
        pl.BlockSpec((CHUNK, D_KV), cur),
        pl.BlockSpec((CHUNK, D_KV), prev),
        pl.BlockSpec((CHUNK, D_KV), cur),
        pl.BlockSpec((CHUNK, D_KV), prev),
    ]


def _half_lane_masks(rows):
    lane = lax.broadcasted_iota(jnp.int32, (rows, LANES), 1)
    return lane < HEAD_DIM


def _kv_variants(kv2):
    left = _half_lane_masks(kv2.shape[0])
    f = kv2.astype(F32)
    swapped = pltpu.roll(f, HEAD_DIM, 1)
    zero = jnp.zeros_like(f)
    g0 = (jnp.where(left, f, zero).astype(BF16), jnp.where(left, zero, swapped).astype(BF16))
    g1 = (jnp.where(left, swapped, zero).astype(BF16), jnp.where(left, zero, f).astype(BF16))
    return (g0, g1)


def _band_mask(i, heads=1):
    row = lax.broadcasted_iota(jnp.int32, (heads * CHUNK, 2 * CHUNK), 0) & (CHUNK - 1)
    col = lax.broadcasted_iota(jnp.int32, (heads * CHUNK, 2 * CHUNK), 1)
    no_prev = jnp.where(i > 0, 0, 4 * CHUNK)
    in_prev = jnp.logical_and(col < CHUNK, (col - row) > no_prev)
    in_cur = jnp.logical_and(col >= CHUNK, (col - CHUNK) <= row)
    return jnp.logical_or(in_prev, in_cur)


def _softmax_with_sink(s, sink):
    m = jnp.maximum(jnp.max(s, axis=1, keepdims=True), sink)
    e = jnp.exp(s - m)
    e_sink = jnp.exp(sink - m)
    inv = 1.0 / (jnp.sum(e, axis=1, keepdims=True) + e_sink)
    return e * inv, e_sink * inv


def _causal_mask():
    row = lax.broadcasted_iota(jnp.int32, (CHUNK, CHUNK), 0)
    col = lax.broadcasted_iota(jnp.int32, (CHUNK, CHUNK), 1)
    return col <= row


def _store_spatial_weights(w_ref, wcat_ref, wcat_t_ref=None):
    causal = _causal_mask()
    for p in range(D_GMLP // LANES):
        wl = jnp.where(causal, w_ref[2 * p], 0.0)
        wr = jnp.where(causal, w_ref[2 * p + 1], 0.0)
        wcat_ref[p] = jnp.concatenate([wl, wr], axis=1).astype(BF16)
        if wcat_t_ref is not None:
            wcat_t_ref[p] = jnp.concatenate([wl.T, wr.T], axis=1).astype(BF16)


def _pair_stack(xp, left):
    return jnp.concatenate([jnp.where(left, xp, 0.0), jnp.where(left, 0.0, xp)], axis=0).astype(BF16)


def _mixer_fwd(u, vg, q, k, va, v_ln_g, v_ln_b, w_spatial, bias_full, sinks):
    t = u.shape[0]

    def body(u_ref, vg_ref, q_ref, kc_ref, kp_ref, vc_ref, vp_ref, g_ref, b_ref, w_ref, bias_ref, sink_ref, cat_ref, wcat):
        i = pl.program_id(0)
        left = _half_lane_masks(CHUNK)

        @pl.when(i == 0)
        def _():
            _store_spatial_weights(w_ref, wcat)

        ug = _gelu(u_ref[...])
        xhat, _ = _layer_norm_stats(_gelu(vg_ref[...]))
        vgl = xhat * g_ref[...] + b_ref[...]
        for p in range(D_GMLP // LANES):
            cols = slice(p * LANES, (p + 1) * LANES)
            mixed = _dot(wcat[p], _pair_stack(vgl[:, cols], left))
            cat_ref[:, cols] = (ug[:, cols] * (mixed + bias_ref[:, cols])).astype(BF16)

        k_var = _kv_variants(jnp.concatenate([kp_ref[...], kc_ref[...]], axis=0))
        v_var = _kv_variants(jnp.concatenate([vp_ref[...], vc_ref[...]], axis=0))
        valid = _band_mask(i)
        scores = [_dot(q_ref[:, (h // 2) * LANES : (h // 2 + 1) * LANES], k_var[h // 4][h % 2], NT) for h in range(N_HEADS)]
        heads = range(N_HEADS)
        sinks_h = [sink_ref[h] for h in heads]
        masked = [jnp.where(valid, scores[h] * SCALE, NEG_INF) for h in heads]
        maxes = [jnp.maximum(jnp.max(masked[h], axis=1, keepdims=True), sinks_h[h]) for h in heads]
        exps = [jnp.exp(masked[h] - maxes[h]) for h in heads]
        invs = [1.0 / (jnp.sum(exps[h], axis=1, keepdims=True) + jnp.exp(sinks_h[h] - maxes[h])) for h in heads]
        probs = [(exps[h] * invs[h]).astype(BF16) for h in heads]
        for p in range(D_ATTN // LANES):
            out = _dot(probs[2 * p], v_var[p // 2][0]) + _dot(probs[2 * p + 1], v_var[p // 2][1])
            cat_ref[:, D_GMLP + p * LANES : D_GMLP + (p + 1) * LANES] = out.astype(BF16)

    return pl.pallas_call(
        body,
        name="mixer_fwd",
        grid=(t // CHUNK,),
        in_specs=_chunk_specs()
        + [
            _const_spec((1, D_GMLP)),
            _const_spec((1, D_GMLP)),
            _const_spec((N_HEADS, CHUNK, CHUNK)),
            _const_spec((CHUNK, D_GMLP)),
            pl.BlockSpec(memory_space=pltpu.SMEM),
        ],
        out_specs=pl.BlockSpec((CHUNK, D_MODEL), lambda i: (i, 0)),
        out_shape=jax.ShapeDtypeStruct((t, D_MODEL), BF16),
        scratch_shapes=[pltpu.VMEM((D_GMLP // LANES, CHUNK, 2 * CHUNK), BF16)],
        compiler_params=_params(("arbitrary",)),
    )(u, vg, q, k, k, va, va, v_ln_g, v_ln_b, w_spatial, bias_full, sinks)


def _ffn_fwd_loss(cat, x, w_out, ln1_g, ln1_b, w1, w2, ln2_g, ln2_b, target):
    t = x.shape[0]

    def body(cat_ref, x_ref, wo_ref, g1_ref, b1_ref, w1_ref, w2_ref, g2_ref, b2_ref, tgt_ref,
             xh_ref, rstd_ref, x1b_ref, r_ref, dz2_ref, dz2b_ref, dg2_ref, db2_ref, sq_ref):
        @pl.when(pl.program_id(0) == 0)
        def _():
            dg2_ref[...] = jnp.zeros_like(dg2_ref)
            db2_ref[...] = jnp.zeros_like(db2_ref)
            sq_ref[...] = jnp.zeros_like(sq_ref)

        xhat1, rstd1 = _layer_norm_stats(ALPHA * x_ref[...] + _dot(cat_ref[...], wo_ref[...]))
        xh_ref[...] = xhat1
        rstd_ref[...] = rstd1
        x1 = xhat1 * g1_ref[...] + b1_ref[...]
        x1b = x1.astype(BF16)
        x1b_ref[...] = x1b
        ff = jnp.zeros((TM_FFN, D_MODEL), F32)
        for j in range(N_FF_BLOCKS):
            r = jnp.maximum(_dot(x1b, w1_ref[j]), 0.0)
            r_ref[:, j * D_MODEL : (j + 1) * D_MODEL] = r.astype(BF16)
            ff = ff + _dot((r * r).astype(BF16), w2_ref[j])
        xhat2, rstd2 = _layer_norm_stats(ALPHA * x1 + ff)
        err = xhat2 * g2_ref[...] + b2_ref[...] - tgt_ref[...]
        sq_ref[...] += jnp.sum(err * err, axis=0, keepdims=True)
        dy = err * (1.0 / D_MODEL)
        dg2_ref[...] += jnp.sum(dy * xhat2, axis=0, keepdims=True)
        db2_ref[...] += jnp.sum(dy, axis=0, keepdims=True)
        dz2 = _layer_norm_bwd(dy * g2_ref[...], xhat2, rstd2)
        dz2_ref[...] = dz2
        dz2b_ref[...] = dz2.astype(BF16)

    vec = _const_spec((1, D_MODEL))
    tile = _row_spec(TM_FFN, D_MODEL)
    wspec = _const_spec((N_FF_BLOCKS, D_MODEL, D_MODEL), single_buffer=True)
    return pl.pallas_call(
        body,
        name="ffn_fwd_loss",
        grid=(t // TM_FFN,),
        in_specs=[tile, tile, _const_spec((D_MODEL, D_MODEL), single_buffer=True), vec, vec, wspec, wspec, vec, vec, tile],
        out_specs=[tile, _row_spec(TM_FFN, 1), tile, _row_spec(TM_FFN, D_FF), tile, tile, vec, vec, vec],
        out_shape=[
            jax.ShapeDtypeStruct((t, D_MODEL), F32),
            jax.ShapeDtypeStruct((t, 1), F32),
            jax.ShapeDtypeStruct((t, D_MODEL), BF16),
            jax.ShapeDtypeStruct((t, D_FF), BF16),
            jax.ShapeDtypeStruct((t, D_MODEL), F32),
            jax.ShapeDtypeStruct((t, D_MODEL), BF16),
            jax.ShapeDtypeStruct((1, D_MODEL), F32),
            jax.ShapeDtypeStruct((1, D_MODEL), F32),
            jax.ShapeDtypeStruct((1, D_MODEL), F32),
        ],
        compiler_params=_params(("arbitrary",)),
    )(cat, x, w_out, ln1_g, ln1_b, w1, w2, ln2_g, ln2_b, target)


def _ffn_bwd_ln1(dz2, r, xhat1, rstd1, ln1_g, w1, w2, w_out, dep=None):
    t = dz2.shape[0]

    def body(dz2_ref, r_ref, xh_ref, rstd_ref, g1_ref, w1_ref, w2_ref, wo_ref, dpre_ref, dz1_ref, dz1b_ref, dcat_ref, dg1_ref, db1_ref):
        @pl.when(pl.program_id(0) == 0)
        def _():
            dg1_ref[...] = jnp.zeros_like(dg1_ref)
            db1_ref[...] = jnp.zeros_like(db1_ref)

        dz2 = dz2_ref[...]
        dz2b = dz2.astype(BF16)
        dx1 = ALPHA * dz2
        for j in range(N_FF_BLOCKS):
            cols = slice(j * D_MODEL, (j + 1) * D_MODEL)
            dpre = (_dot(dz2b, w2_ref[j], NT) * (2.0 * r_ref[:, cols].astype(F32))).astype(BF16)
            dpre_ref[:, cols] = dpre
            dx1 = dx1 + _dot(dpre, w1_ref[j], NT)
        xhat1 = xh_ref[...]
        dg1_ref[...] += jnp.sum(dx1 * xhat1, axis=0, keepdims=True)
        db1_ref[...] += jnp.sum(dx1, axis=0, keepdims=True)
        dz1 = _layer_norm_bwd(dx1 * g1_ref[...], xhat1, rstd_ref[...])
        dz1_ref[...] = dz1
        dz1b = dz1.astype(BF16)
        dz1b_ref[...] = dz1b
        dcat_ref[...] = _dot(dz1b, wo_ref[...], NT).astype(BF16)

    vec = _const_spec((1, D_MODEL))
    tile = _row_spec(TM_FFN, D_MODEL)
    wspec = _const_spec((N_FF_BLOCKS, D_MODEL, D_MODEL), single_buffer=True)
    body, in_specs, operands = _after(
        dep, body,
        [tile, _row_spec(TM_FFN, D_FF), tile, _row_spec(TM_FFN, 1), vec, wspec, wspec, _const_spec((D_MODEL, D_MODEL), single_buffer=True)],
        [dz2, r, xhat1, rstd1, ln1_g, w1, w2, w_out])
    return pl.pallas_call(
        body,
        name="ffn_bwd_ln1",
        grid=(t // TM_FFN,),
        in_specs=in_specs,
        out_specs=[_row_spec(TM_FFN, D_FF), tile, tile, tile, vec, vec],
        out_shape=[
            jax.ShapeDtypeStruct((t, D_FF), BF16),
            jax.ShapeDtypeStruct((t, D_MODEL), F32),
            jax.ShapeDtypeStruct((t, D_MODEL), BF16),
            jax.ShapeDtypeStruct((t, D_MODEL), BF16),
            jax.ShapeDtypeStruct((1, D_MODEL), F32),
            jax.ShapeDtypeStruct((1, D_MODEL), F32),
        ],
        compiler_params=_params(("arbitrary",)),
    )(*operands)


def _mixer_bwd(u, vg, q, k, va, dcat, cos, sin, v_ln_g, v_ln_b, w_spatial, bias_full, sinks, dep=None):
    t = u.shape[0]
    n_chunks = t // CHUNK

    def body(u_ref, vg_ref, q_ref, kc_ref, kp_ref, vc_ref, vp_ref, dcat_ref, cosc_ref, sinc_ref, cosp_ref, sinp_ref,
             g_ref, b_ref, w_ref, bias_ref, sink_ref,
             dmain_ref, dkv_ref, dg_ref, db_ref, dw_ref, dbs_ref, dsink_ref, dmix_acc, wcat, wcat_t):
        i = pl.program_id(0)
        left = _half_lane_masks(CHUNK)
        lane = lax.broadcasted_iota(jnp.int32, (CHUNK, LANES), 1)
        n_pairs = D_GMLP // LANES

        @pl.when(i == 0)
        def _():
            dg_ref[...] = jnp.zeros_like(dg_ref)
            db_ref[...] = jnp.zeros_like(db_ref)
            dw_ref[...] = jnp.zeros_like(dw_ref)
            dsink_ref[...] = jnp.zeros_like(dsink_ref)
            dmix_acc[...] = jnp.zeros_like(dmix_acc)
            _store_spatial_weights(w_ref, wcat, wcat_t)

        ug, dug_du = _gelu_and_grad(u_ref[...])
        gv, dgv_dv = _gelu_and_grad(vg_ref[...])
        xhat, rstd = _layer_norm_stats(gv)
        gain = g_ref[...]
        vgl = xhat * gain + b_ref[...]
        pair_cols = [slice(p * LANES, (p + 1) * LANES) for p in range(n_pairs)]
        mixed = [_dot(wcat[p], _pair_stack(vgl[:, cols], left)) for p, cols in enumerate(pair_cols)]
        dm_stacks = []
        for p, cols in enumerate(pair_cols):
            da = dcat_ref[:, cols].astype(F32)
            dmain_ref[:, cols] = (da * (mixed[p] + bias_ref[:, cols]) * dug_du[:, cols]).astype(BF16)
            dmixed = da * ug[:, cols]
            dmix_acc[:, cols] += dmixed
            dm_stacks.append(_pair_stack(dmixed, left))
        causal = _causal_mask()
        for p, cols in enumerate(pair_cols):
            dw_pair = _dot(dm_stacks[p], vgl[:, cols].astype(BF16), NT)
            dw_ref[2 * p] += jnp.where(causal, dw_pair[:CHUNK], 0.0)
            dw_ref[2 * p + 1] += jnp.where(causal, dw_pair[CHUNK:], 0.0)
        dvgl = jnp.concatenate([_dot(wcat_t[p], dm_stacks[p]) for p in range(n_pairs)], axis=1)
        dg_ref[...] += jnp.sum(dvgl * xhat, axis=0, keepdims=True)
        db_ref[...] += jnp.sum(dvgl, axis=0, keepdims=True)
        dgv = _layer_norm_bwd(dvgl * gain, xhat, rstd)
        dmain_ref[:, D_GMLP : 2 * D_GMLP] = (dgv * dgv_dv).astype(BF16)

        @pl.when(i == n_chunks - 1)
        def _():
            tile = jnp.zeros((CHUNK, LANES), F32)
            for p in range(D_GMLP // LANES):
                dm = dmix_acc[:, p * LANES : (p + 1) * LANES]
                sl = jnp.sum(jnp.where(left, dm, 0.0), axis=1, keepdims=True)
                sr = jnp.sum(jnp.where(left, 0.0, dm), axis=1, keepdims=True)
                tile = jnp.where(lane == 2 * p, sl, tile)
                tile = jnp.where(lane == 2 * p + 1, sr, tile)
            dbs_ref[...] = tile

        k_var = _kv_variants(jnp.concatenate([kp_ref[...], kc_ref[...]], axis=0))
        v_var = _kv_variants(jnp.concatenate([vp_ref[...], vc_ref[...]], axis=0))
        valid = _band_mask(i)
        n_qpairs = D_ATTN // LANES
        q_pairs = [q_ref[:, p * LANES : (p + 1) * LANES] for p in range(n_qpairs)]
        do_all = dcat_ref[:, D_GMLP:D_MODEL]
        do_pairs = [do_all[:, p * LANES : (p + 1) * LANES] for p in range(n_qpairs)]
        scores = [_dot(q_pairs[h // 2], k_var[h // 4][h % 2], NT) for h in range(N_HEADS)]
        dprobs = [_dot(do_pairs[h // 2], v_var[h // 4][h % 2], NT) for h in range(N_HEADS)]
        heads = range(N_HEADS)
        sinks_h = [sink_ref[h] for h in heads]
        masked = [jnp.where(valid, scores[h] * SCALE, NEG_INF) for h in heads]
        maxes = [jnp.maximum(jnp.max(masked[h], axis=1, keepdims=True), sinks_h[h]) for h in heads]
        exps = [jnp.exp(masked[h] - maxes[h]) for h in heads]
        exp_sinks = [jnp.exp(sinks_h[h] - maxes[h]) for h in heads]
        invs = [1.0 / (jnp.sum(exps[h], axis=1, keepdims=True) + exp_sinks[h]) for h in heads]
        probs = [exps[h] * invs[h] for h in heads]
        dsums = [jnp.sum(probs[h] * dprobs[h], axis=1, keepdims=True) for h in heads]
        ds_b = [(probs[h] * (dprobs[h] - dsums[h]) * SCALE).astype(BF16) for h in heads]
        probs_b = [probs[h].astype(BF16) for h in heads]
        dsink_row = jnp.zeros((1, LANES), F32)
        lane_row = lax.broadcasted_iota(jnp.int32, (1, LANES), 1)
        for h in heads:
            d_sink = -jnp.sum(exp_sinks[h] * invs[h] * dsums[h], axis=0, keepdims=True)
            dsink_row = dsink_row + jnp.where(lane_row == h, d_sink, 0.0)
        dsink_ref[0:1, :] += dsink_row
        dq_all = jnp.concatenate(
            [_dot(ds_b[2 * p], k_var[p // 2][0]) + _dot(ds_b[2 * p + 1], k_var[p // 2][1]) for p in range(n_qpairs)], axis=1)
        cos_c, sin_c = cosc_ref[...], sinc_ref[...]
        dmain_ref[:, 2 * D_GMLP : D_MAIN] = _rope_transposed(dq_all, _lane_tile(cos_c, n_qpairs), _lane_tile(sin_c, n_qpairs)).astype(BF16)

        q_t = q_ref[...].astype(F32).T.astype(BF16)
        do_t = do_all.astype(F32).T.astype(BF16)
        heads_per_group = N_HEADS // 2

        def group_grad_t(lhs_t, rhs_heads):
            parts = []
            for g in range(2):
                heads = range(g * heads_per_group, (g + 1) * heads_per_group)
                lhs = jnp.concatenate([lhs_t[h * HEAD_DIM : (h + 1) * HEAD_DIM] for h in heads], axis=1)
                parts.append(_dot(lhs, jnp.concatenate([rhs_heads[h] for h in heads], axis=0)))
            return jnp.concatenate(parts, axis=0).T

        dk2 = group_grad_t(q_t, ds_b)
        dv2 = group_grad_t(do_t, probs_b)
        cur = pl.ds(pl.multiple_of(i * CHUNK, CHUNK), CHUNK)
        dkv_ref[cur, 0:D_KV] = _rope_transposed(dk2[CHUNK:], cos_c, sin_c)
        dkv_ref[cur, D_KV : 2 * D_KV] = dv2[CHUNK:]

        @pl.when(i > 0)
        def _():
            prev = pl.ds(pl.multiple_of((i - 1) * CHUNK, CHUNK), CHUNK)
            dkv_ref[prev, 0:D_KV] += _rope_transposed(dk2[:CHUNK], cosp_ref[...], sinp_ref[...])
            dkv_ref[prev, D_KV : 2 * D_KV] += dv2[:CHUNK]

    cur = lambda i: (i, 0)
    prev = lambda i: (jnp.maximum(i - 1, 0), 0)
    in_specs = _chunk_specs() + [
        pl.BlockSpec((CHUNK, D_MODEL), cur),
        pl.BlockSpec((CHUNK, LANES), cur),
        pl.BlockSpec((CHUNK, LANES), cur),
        pl.BlockSpec((CHUNK, LANES), prev),
        pl.BlockSpec((CHUNK, LANES), prev),
        _const_spec((1, D_GMLP)),
        _const_spec((1, D_GMLP)),
        _const_spec((N_HEADS, CHUNK, CHUNK)),
        _const_spec((CHUNK, D_GMLP)),
        pl.BlockSpec(memory_space=pltpu.SMEM),
    ]
    body, in_specs, operands = _after(
        dep, body, in_specs, [u, vg, q, k, k, va, va, dcat, cos, sin, cos, sin, v_ln_g, v_ln_b, w_spatial, bias_full, sinks])
    return pl.pallas_call(
        body,
        name="mixer_bwd",
        grid=(n_chunks,),
        in_specs=in_specs,
        out_specs=[
            pl.BlockSpec((CHUNK, D_MAIN), cur),
            _const_spec((t, 2 * D_KV)),
            _const_spec((1, D_GMLP)),
            _const_spec((1, D_GMLP)),
            _const_spec((N_HEADS, CHUNK, CHUNK)),
            _const_spec((CHUNK, LANES)),
            _const_spec((8, LANES)),
        ],
        out_shape=[
            jax.ShapeDtypeStruct((t, D_MAIN), BF16),
            jax.ShapeDtypeStruct((t, 2 * D_KV), F32),
            jax.ShapeDtypeStruct((1, D_GMLP), F32),
            jax.ShapeDtypeStruct((1, D_GMLP), F32),
            jax.ShapeDtypeStruct((N_HEADS, CHUNK, CHUNK), F32),
            jax.ShapeDtypeStruct((CHUNK, LANES), F32),
            jax.ShapeDtypeStruct((8, LANES), F32),
        ],
        scratch_shapes=[
            pltpu.VMEM((CHUNK, D_GMLP), F32),
            pltpu.VMEM((D_GMLP // LANES, CHUNK, 2 * CHUNK), BF16),
            pltpu.VMEM((D_GMLP // LANES, CHUNK, 2 * CHUNK), BF16),
        ],
        compiler_params=_params(("arbitrary",)),
    )(*operands)


def _grad_x(dh_main, dkv, dz1, w_in_t, dep=None):
    t = dz1.shape[0]

    def body(dm_ref, dkv_ref, dz1_ref, w_ref, gx_ref):
        acc = ALPHA * dz1_ref[...] + _dot(dm_ref[...], w_ref[0:D_MAIN, :])
        gx_ref[...] = acc + _dot(dkv_ref[...].astype(BF16), w_ref[D_MAIN:D_IN, :])

    body, in_specs, operands = _after(
        dep, body, [_row_spec(TM, D_MAIN), _row_spec(TM, 2 * D_KV), _row_spec(TM, D_MODEL), _const_spec((D_IN, D_MODEL))], [dh_main, dkv, dz1, w_in_t])
    return pl.pallas_call(
        body,
        name="grad_x",
        grid=(t // TM,),
        in_specs=in_specs,
        out_specs=_row_spec(TM, D_MODEL),
        out_shape=jax.ShapeDtypeStruct((t, D_MODEL), F32),
        compiler_params=_params(("parallel",)),
    )(*operands)


def _token_contraction(name, out_rows, tk, in_arrays, contributions, dep=None):
    t = in_arrays[0].shape[0]

    def body(*refs):
        out_ref = refs[-1]

        @pl.when(pl.program_id(0) == 0)
        def _():
            out_ref[...] = jnp.zeros_like(out_ref)

        for row0, a, b in contributions(*refs[:-1]):
            out_ref[row0 : row0 + a.shape[1], :] += _dot(a, b, TN)

    in_specs = [_row_spec(tk, a.shape[1]) for a in in_arrays]
    body, in_specs, operands = _after(dep, body, in_specs, in_arrays)
    return pl.pallas_call(
        body,
        name=name,
        grid=(t // tk,),
        in_specs=in_specs,
        out_specs=_const_spec((out_rows, D_MODEL)),
        out_shape=jax.ShapeDtypeStruct((out_rows, D_MODEL), F32),
        compiler_params=_params(("arbitrary",)),
    )(*operands)


def _grad_w_in_t(dh_main, dkv, x):
    def contributions(dm_ref, dkv_ref, x_ref):
        xb = x_ref[...].astype(BF16)
        return [(0, dm_ref[...], xb), (D_MAIN, dkv_ref[...].astype(BF16), xb)]

    return _token_contraction("grad_w_in", D_IN, TK, [dh_main, dkv, x], contributions)


def _grad_w_out(cat, dz1b, dep=None):
    def contributions(cat_ref, dz1_ref):
        return [(0, cat_ref[...], dz1_ref[...])]

    return _token_contraction("grad_w_out", D_MODEL, TK, [cat, dz1b], contributions, dep)


def _grad_w_ff1(x1b, dpre):
    def contributions(x1_ref, dpre_ref):
        x1 = x1_ref[...]
        return [(j * D_MODEL, x1, dpre_ref[:, j * D_MODEL : (j + 1) * D_MODEL]) for j in range(N_FF_BLOCKS)]

    return _token_contraction("grad_w_ff1", D_FF, TK_FF, [x1b, dpre], contributions)


def _grad_w_ff2(r, dz2b):
    def contributions(r_ref, dz2_ref):
        dz2 = dz2_ref[...]
        out = []
        for j in range(N_FF_BLOCKS):
            rf = r_ref[:, j * D_MODEL : (j + 1) * D_MODEL].astype(F32)
            out.append((j * D_MODEL, (rf * rf).astype(BF16), dz2))
        return out

    return _token_contraction("grad_w_ff2", D_FF, TK_FF, [r, dz2b], contributions)


ANY = pl.BlockSpec(memory_space=pl.ANY)


def _mesh_position():
    return lax.axis_index("x"), lax.axis_index("y"), lax.axis_index("c")


def _other_chips(x, y):
    return [(1 - x, y), (x, 1 - y), (1 - x, 1 - y)]


def _remote(src, dst, send_sem, recv_sem, device):
    return pltpu.make_async_remote_copy(src_ref=src, dst_ref=dst, send_sem=send_sem, recv_sem=recv_sem, device_id=device, device_id_type=MESH)


def _rows(ref, start, size):
    return ref.at[pl.ds(start, size), :]


def _all_gather_weights(shards):
    n = len(shards)
    per = 7

    def body(*refs):
        ins, outs = refs[:n], refs[n : 2 * n]
        send_sems, recv_sems = refs[2 * n :]
        x, y, c = _mesh_position()
        me = 2 * x + y
        chips = _other_chips(x, y)
        sibling = (x, y, 1 - c)
        started = []
        for w in range(n):
            rows = shards[w].shape[0]
            half = rows // 2
            for kk, (px, py) in enumerate(chips):
                cp = _remote(_rows(ins[w], c * half, half), _rows(outs[w], me * rows + c * half, half),
                             send_sems.at[per * w + kk], recv_sems.at[per * w + kk], (px, py, c))
                cp.start()
                started.append(cp)
            cp = _remote(ins[w], _rows(outs[w], me * rows, rows), send_sems.at[per * w + 6], recv_sems.at[per * w + 6], sibling)
            cp.start()
            started.append(cp)
        for w in range(n):
            rows = shards[w].shape[0]
            half = rows // 2
            for kk, (px, py) in enumerate(chips):
                blk = _rows(outs[w], (2 * px + py) * rows + c * half, half)
                _remote(blk, blk, send_sems.at[per * w + kk], recv_sems.at[per * w + kk], (px, py, c)).wait_recv()
                fwd = _remote(blk, blk, send_sems.at[per * w + 3 + kk], recv_sems.at[per * w + 3 + kk], sibling)
                fwd.start()
                started.append(fwd)
        for w in range(n):
            rows = shards[w].shape[0]
            half = rows // 2
            for kk, (px, py) in enumerate(chips):
                blk = _rows(outs[w], (2 * px + py) * rows + (1 - c) * half, half)
                _remote(blk, blk, send_sems.at[per * w + 3 + kk], recv_sems.at[per * w + 3 + kk], sibling).wait_recv()
            own = _rows(outs[w], me * rows, rows)
            _remote(own, own, send_sems.at[per * w + 6], recv_sems.at[per * w + 6], sibling).wait_recv()
        for cp in started:
            cp.wait_send()

    return pl.pallas_call(
        body,
        name="all_gather_weights",
        in_specs=[ANY] * n,
        out_specs=[ANY] * n,
        out_shape=[jax.ShapeDtypeStruct((N_CHIPS * s.shape[0], s.shape[1]), s.dtype) for s in shards],
        scratch_shapes=[pltpu.SemaphoreType.DMA((per * n,)), pltpu.SemaphoreType.DMA((per * n,))],
    )(*shards)


def _pair_swap(grads):
    n = len(grads)

    def body(*refs):
        ins, theirs = refs[:n], refs[n : 2 * n]
        send_sems, recv_sems = refs[2 * n :]
        x, y, c = _mesh_position()
        sibling = (x, y, 1 - c)
        sends = []
        for w in range(n):
            rows = grads[w].shape[0] // N_CHIPS
            half = rows // 2
            for j in range(N_CHIPS):
                cp = _remote(_rows(ins[w], j * rows + (1 - c) * half, half), _rows(theirs[w], j * half, half),
                             send_sems.at[4 * w + j], recv_sems.at[4 * w + j], sibling)
                cp.start()
                sends.append(cp)
        for cp in sends:
            cp.wait_recv()
        for cp in sends:
            cp.wait_send()

    return pl.pallas_call(
        body,
        name="grad_pair_swap",
        in_specs=[ANY] * n,
        out_specs=[ANY] * n,
        out_shape=[jax.ShapeDtypeStruct((g.shape[0] // 2, g.shape[1]), g.dtype) for g in grads],
        scratch_shapes=[pltpu.SemaphoreType.DMA((4 * n,)), pltpu.SemaphoreType.DMA((4 * n,))],
    )(*grads)


def _chip_exchange(partials):
    n = len(partials)

    def body(*refs):
        ins, outs = refs[:n], refs[n : 2 * n]
        send_sems, recv_sems = refs[2 * n :]
        x, y, c = _mesh_position()
        chips = _other_chips(x, y)
        sends = []
        for w in range(n):
            half = partials[w].shape[0] // N_CHIPS
            for kk, (px, py) in enumerate(chips):
                cp = _remote(_rows(ins[w], (2 * px + py) * half, half), _rows(outs[w], kk * half, half),
                             send_sems.at[3 * w + kk], recv_sems.at[3 * w + kk], (px, py, c))
                cp.start()
                sends.append(cp)
        for cp in sends:
            cp.wait_recv()
        for cp in sends:
            cp.wait_send()

    return pl.pallas_call(
        body,
        name="grad_chip_exchange",
        in_specs=[ANY] * n,
        out_specs=[ANY] * n,
        out_shape=[jax.ShapeDtypeStruct((3 * p.shape[0] // N_CHIPS, p.shape[1]), p.dtype) for p in partials],
        scratch_shapes=[pltpu.SemaphoreType.DMA((3 * n,)), pltpu.SemaphoreType.DMA((3 * n,))],
    )(*partials)


def _pair_gather(shards):
    n = len(shards)

    def body(*refs):
        outs = refs[n : 2 * n]
        send_sems, recv_sems = refs[2 * n :]
        x, y, c = _mesh_position()
        sibling = (x, y, 1 - c)
        sends = []
        for w in range(n):
            half = shards[w].shape[0] // 2
            mine = _rows(outs[w], c * half, half)
            cp = _remote(mine, mine, send_sems.at[w], recv_sems.at[w], sibling)
            cp.start()
            sends.append(cp)
        for w in range(n):
            half = shards[w].shape[0] // 2
            blk = _rows(outs[w], (1 - c) * half, half)
            _remote(blk, blk, send_sems.at[w], recv_sems.at[w], sibling).wait_recv()
        for cp in sends:
            cp.wait_send()

    return pl.pallas_call(
        body,
        name="grad_pair_gather",
        in_specs=[ANY] * n,
        out_specs=[ANY] * n,
        out_shape=[jax.ShapeDtypeStruct(s.shape, s.dtype) for s in shards],
        input_output_aliases={w: w for w in range(n)},
        scratch_shapes=[pltpu.SemaphoreType.DMA((n,)), pltpu.SemaphoreType.DMA((n,))],
    )(*shards)


def _all_reduce_small(slab):
    rows = slab.shape[0]
    part = rows // 8

    def body(slab_ref, out_ref, landing, reduced, send_sems, recv_sems):
        x, y, c = _mesh_position()
        me = 4 * x + 2 * y + c
        flips = [(k >> 2, (k >> 1) & 1, k & 1) for k in range(1, 8)]

        def peer(flip):
            fx, fy, fc = flip
            return (1 - x if fx else x, 1 - y if fy else y, 1 - c if fc else c)

        def my_rows(ref):
            return ref.at[pl.ds(pl.multiple_of(me * part, 8), part), :]

        sends = []
        for kk, flip in enumerate(flips):
            px, py, pc = peer(flip)
            them = 4 * px + 2 * py + pc
            cp = _remote(slab_ref.at[pl.ds(pl.multiple_of(them * part, 8), part), :], landing.at[me], send_sems.at[kk], recv_sems.at[kk], (px, py, pc))
            cp.start()
            sends.append(cp)
        landing[me] = my_rows(slab_ref)[...]
        for kk, flip in enumerate(flips):
            px, py, pc = peer(flip)
            them = 4 * px + 2 * py + pc
            _remote(landing.at[them], landing.at[them], send_sems.at[kk], recv_sems.at[kk], (px, py, pc)).wait_recv()
        total = landing[0]
        for s in range(1, 8):
            total = total + landing[s]
        reduced[...] = total
        my_rows(out_ref)[...] = total
        for kk, flip in enumerate(flips):
            cp = _remote(reduced, my_rows(out_ref), send_sems.at[7 + kk], recv_sems.at[7 + kk], peer(flip))
            cp.start()
            sends.append(cp)
        for kk, flip in enumerate(flips):
            px, py, pc = peer(flip)
            them = 4 * px + 2 * py + pc
            blk = out_ref.at[pl.ds(pl.multiple_of(them * part, 8), part), :]
            _remote(blk, blk, send_sems.at[7 + kk], recv_sems.at[7 + kk], (px, py, pc)).wait_recv()
        for cp in sends:
            cp.wait_send()

    vmem = pl.BlockSpec(memory_space=pltpu.VMEM)
    return pl.pallas_call(
        body,
        name="all_reduce_small",
        in_specs=[vmem],
        out_specs=vmem,
        out_shape=jax.ShapeDtypeStruct(slab.shape, slab.dtype),
        scratch_shapes=[pltpu.VMEM((8, part, LANES), F32), pltpu.VMEM((part, LANES), F32), pltpu.SemaphoreType.DMA((14,)), pltpu.SemaphoreType.DMA((14,))],
    )(slab)


HBM = pl.BlockSpec(memory_space=pltpu.HBM)
SEM = pl.BlockSpec(memory_space=pltpu.SEMAPHORE)
DATAFLOW = pltpu.SideEffectType.DATAFLOW_SIDE_EFFECTING
TOKEN = jax.ShapeDtypeStruct((8, LANES), F32)


def _plan_copies(bufs, plan, send_sems, recv_sems):
    out = []
    for i, (src, src_row, dst, dst_row, recv_row, rows, device) in enumerate(plan):
        send = _remote(_rows(bufs[src], src_row, rows), _rows(bufs[dst], dst_row, rows), send_sems.at[i], recv_sems.at[i], device)
        landed = _rows(bufs[dst], recv_row, rows)
        recv = _remote(landed, landed, send_sems.at[i], recv_sems.at[i], device)
        out.append((send, recv))
    return out


def _split_call(name, bufs, wait=None, start=None, after=None):
    n = len(bufs)
    n_in = n + (2 if wait else 0) + (1 if after is not None else 0)
    n_start = len(start(0, 0, 0)) if start else 0

    def body(*refs):
        ins = refs[:n]
        x, y, c = _mesh_position()
        if wait:
            for send, recv in _plan_copies(ins, wait[0](x, y, c), refs[n], refs[n + 1]):
                recv.wait_recv()
                send.wait_send()
        if start:
            for send, _ in _plan_copies(ins, start(x, y, c), refs[n_in + n + 1], refs[n_in + n + 2]):
                send.start()
        token = refs[n_in + n]
        token[...] = jnp.zeros_like(token)

    operands = [pltpu.with_memory_space_constraint(b, pltpu.HBM) for b in bufs]
    in_specs = [HBM] * n
    if wait:
        operands += [wait[1], wait[2]]
        in_specs += [SEM, SEM]
    if after is not None:
        operands.append(after)
        in_specs.append(ANY)
    out_shape = [pltpu.HBM(b.shape, b.dtype) for b in bufs] + [TOKEN]
    out_specs = [HBM] * n + [pl.BlockSpec(memory_space=pltpu.VMEM)]
    if start:
        out_shape += [pltpu.SemaphoreType.DMA((n_start,)), pltpu.SemaphoreType.DMA((n_start,))]
        out_specs += [SEM, SEM]
    outs = pl.pallas_call(
        body,
        name=name,
        in_specs=in_specs,
        out_specs=out_specs,
        out_shape=out_shape,
        input_output_aliases={i: i for i in range(n)},
        compiler_params=pltpu.CompilerParams(has_side_effects=DATAFLOW),
    )(*operands)
    return (list(outs[:n]), outs[n]) + tuple(outs[n + 1 :])


def _gather_plans(shard_rows):
    n = len(shard_rows)

    def ici(x, y, c):
        me = 2 * x + y
        plan = []
        for w, rows in enumerate(shard_rows):
            half = rows // 2
            for px, py in _other_chips(x, y):
                plan.append((w, c * half, n + w, me * rows + c * half, (2 * px + py) * rows + c * half, half, (px, py, c)))
            plan.append((w, 0, n + w, me * rows, me * rows, rows, (x, y, 1 - c)))
        return plan

    def passed_on(x, y, c):
        plan = []
        for w, rows in enumerate(shard_rows):
            half = rows // 2
            for px, py in _other_chips(x, y):
                row = (2 * px + py) * rows
                plan.append((n + w, row + c * half, n + w, row + c * half, row + (1 - c) * half, half, (x, y, 1 - c)))
        return plan

    return ici, passed_on


def _swap_plan(block_rows):
    n = len(block_rows)

    def plan_fn(x, y, c):
        plan = []
        for w, rows in enumerate(block_rows):
            half = rows // 2
            for j in range(N_CHIPS):
                plan.append((w, j * rows + (1 - c) * half, n + w, j * half, j * half, half, (x, y, 1 - c)))
        return plan

    return plan_fn


def _exchange_plan(halves):
    n = len(halves)

    def plan_fn(x, y, c):
        plan = []
        for w, half in enumerate(halves):
            for kk, (px, py) in enumerate(_other_chips(x, y)):
                plan.append((w, (2 * px + py) * half, n + w, kk * half, kk * half, half, (px, py, c)))
        return plan

    return plan_fn


def _landing(rows, cols, dtype):
    return lax.empty((rows, cols), dtype)


def _row_tile(rows, cap=512):
    best = 8
    for cand in range(8, cap + 1, 8):
        if rows % cand == 0:
            best = cand
    return best


def _pair_sum(name, grad, theirs, pos):
    half = theirs.shape[0] // N_CHIPS
    cols = theirs.shape[1]
    tile = _row_tile(half)
    steps = half // tile

    def body(pos_ref, g_ref, t_ref, p_ref, own_ref):
        total = g_ref[...] + t_ref[...]
        p_ref[...] = total.astype(BF16)

        @pl.when(pl.program_id(1) == pos_ref[1])
        def _():
            own_ref[...] = total

    return pl.pallas_call(
        body,
        name=name,
        grid_spec=pltpu.PrefetchScalarGridSpec(
            num_scalar_prefetch=1,
            grid=(steps, N_CHIPS),
            in_specs=[
                pl.BlockSpec((tile, cols), lambda i, j, pos: ((2 * j + pos[0]) * steps + i, 0)),
                pl.BlockSpec((tile, cols), lambda i, j, pos: (j * steps + i, 0)),
            ],
            out_specs=[
                pl.BlockSpec((tile, cols), lambda i, j, pos: (j * steps + i, 0)),
                pl.BlockSpec((tile, cols), lambda i, j, pos: (i, 0)),
            ],
        ),
        out_shape=[jax.ShapeDtypeStruct((N_CHIPS * half, cols), BF16), jax.ShapeDtypeStruct((half, cols), F32)],
        compiler_params=_params(("parallel", "arbitrary")),
    )(pos, grad, theirs)


def _chip_sum(name, own, landed, pos):
    half, cols = own.shape
    tile = _row_tile(half)
    steps = half // tile

    def body(pos_ref, own_ref, l0, l1, l2, o_ref):
        o_ref[...] = ((own_ref[...] + l0[...].astype(F32)) + l1[...].astype(F32)) + l2[...].astype(F32)

    landed_specs = [pl.BlockSpec((tile, cols), lambda i, pos, _k=k: (_k * steps + i, 0)) for k in range(N_CHIPS - 1)]
    return pl.pallas_call(
        body,
        name=name,
        grid_spec=pltpu.PrefetchScalarGridSpec(
            num_scalar_prefetch=1,
            grid=(steps,),
            in_specs=[pl.BlockSpec((tile, cols), lambda i, pos: (i, 0))] + landed_specs,
            out_specs=pl.BlockSpec((tile, cols), lambda i, pos: (pos[0] * steps + i, 0)),
        ),
        out_shape=jax.ShapeDtypeStruct((2 * half, cols), F32),
        compiler_params=_params(("parallel",)),
    )(pos, own, landed, landed, landed)


def _adamw(name, w, g, m, v):
    rows, cols = w.shape
    tile = rows if rows * cols <= 256 * 1024 else _row_tile(rows)

    def body(w_ref, g_ref, m_ref, v_ref, g_out_ref, d_ref, nm_ref, nv_ref):
        g = g_ref[...]
        g_out_ref[...] = g
        nm = ADAM_B1 * m_ref[...] + (1.0 - ADAM_B1) * g
        nv = ADAM_B2 * v_ref[...] + (1.0 - ADAM_B2) * (g * g)
        m_hat = nm / (1.0 - ADAM_B1**ADAM_STEP)
        v_hat = nv / (1.0 - ADAM_B2**ADAM_STEP)
        d_ref[...] = -ADAM_LR * (m_hat / (jnp.sqrt(v_hat) + ADAM_EPS) + ADAM_WD * w_ref[...])
        nm_ref[...] = nm
        nv_ref[...] = nv

    spec = _row_spec(tile, cols)
    return pl.pallas_call(
        body,
        name=name,
        grid=(rows // tile,),
        in_specs=[spec] * 4,
        out_specs=[spec] * 4,
        out_shape=[jax.ShapeDtypeStruct((rows, cols), F32)] * 4,
        compiler_params=_params(("parallel",)),
    )(w, g, m, v)


_SMALL = (
    ("v_ln_g", (D_GMLP,), 8),
    ("v_ln_b", (D_GMLP,), 8),
    ("w_spatial", (N_HEADS, CHUNK, CHUNK), 1024),
    ("b_spatial", (N_HEADS, CHUNK), 8),
    ("sinks", (N_HEADS,), 8),
    ("ln1_g", (D_MODEL,), 8),
    ("ln1_b", (D_MODEL,), 8),
    ("ln2_g", (D_MODEL,), 8),
    ("ln2_b", (D_MODEL,), 8),
    ("squared_error", (D_MODEL,), 8),
)
N_SMALL_PARAMS = len(_SMALL) - 1


def _pack_small(values):
    parts = []
    for (name, shape, rows), val in zip(_SMALL, values, strict=True):
        flat = val.reshape(-1).astype(F32)
        parts.append(jnp.pad(flat, (0, rows * LANES - flat.shape[0])).reshape(rows, LANES))
    parts.append(jnp.zeros((SMALL_ROWS - sum(rows for _, _, rows in _SMALL), LANES), F32))
    return jnp.concatenate(parts, axis=0)


def _adamw_update(w, g, m, v):
    nm = ADAM_B1 * m + (1.0 - ADAM_B1) * g
    nv = ADAM_B2 * v + (1.0 - ADAM_B2) * (g * g)
    m_hat = nm / (1.0 - ADAM_B1**ADAM_STEP)
    v_hat = nv / (1.0 - ADAM_B2**ADAM_STEP)
    return -ADAM_LR * (m_hat / (jnp.sqrt(v_hat) + ADAM_EPS) + ADAM_WD * w), nm, nv


def _adamw_small(g_slab, params, first, second):
    n = N_SMALL_PARAMS

    def pieces(shape):
        if len(shape) == 3:
            return [((0, h), h * shape[1], shape[1], shape[2]) for h in range(shape[0])]
        if len(shape) == 2:
            return [((0,), 0, shape[0], shape[1])]
        if shape[0] >= LANES:
            return [((slice(None), slice(r * LANES, (r + 1) * LANES)), r, 1, LANES) for r in range(shape[0] // LANES)]
        return [((slice(None), slice(0, shape[0])), 0, 1, shape[0])]

    def body(*refs):
        g_ref = refs[0]
        w_refs, m_refs, v_refs = refs[1 : 1 + n], refs[1 + n : 1 + 2 * n], refs[1 + 2 * n : 1 + 3 * n]
        outs = refs[1 + 3 * n :]
        row0 = 0
        for idx, (_, shape, rows) in enumerate(_SMALL[:n]):
            for where, first_row, n_rows, lanes in pieces(shape):
                g = g_ref[row0 + first_row : row0 + first_row + n_rows, 0:lanes]
                delta, nm, nv = _adamw_update(w_refs[idx][where], g, m_refs[idx][where], v_refs[idx][where])
                for group, val in enumerate((g, delta, nm, nv)):
                    outs[group * n + idx][where] = val
            row0 += rows

    vmem = pl.BlockSpec(memory_space=pltpu.VMEM)
    shapes = [jax.ShapeDtypeStruct(p.shape, F32) for p in params]
    outs = pl.pallas_call(
        body,
        name="adamw_small",
        in_specs=[vmem] * (1 + 3 * n),
        out_specs=[vmem] * (4 * n),
        out_shape=shapes * 4,
        compiler_params=_params(),
    )(g_slab, *params, *first, *second)
    return [list(outs[group * n : (group + 1) * n]) for group in range(4)]


def kernel(x, positions, w_in, v_ln_g, v_ln_b, w_spatial, b_spatial, sinks, w_out, ln1_g, ln1_b, w_ff1, w_ff2, ln2_g, ln2_b, loss_target, m_w_in, m_v_ln_g, m_v_ln_b, m_w_spatial, m_b_spatial, m_sinks, m_w_out, m_ln1_g, m_ln1_b, m_w_ff1, m_w_ff2, m_ln2_g, m_ln2_b, v_w_in, v_v_ln_g, v_v_ln_b, v_w_spatial, v_b_spatial, v_sinks, v_w_out, v_ln1_g, v_ln1_b, v_w_ff1, v_w_ff2, v_ln2_g, v_ln2_b):
    t = x.shape[1]
    x2 = x.reshape(t, D_MODEL)
    target = loss_target.reshape(t, D_MODEL)

    (w_in_t,) = _all_gather_weights([w_in[0].T.astype(BF16)])
    later = [w_out[0].astype(BF16), w_ff1[0].astype(BF16), w_ff2[0].astype(BF16)]
    later_rows = [s.shape[0] for s in later]
    ici_plan, pass_plan = _gather_plans(later_rows)
    bufs, started, ici_send, ici_recv = _split_call(
        "gather_start", later + [_landing(N_CHIPS * r, D_MODEL, BF16) for r in later_rows], start=ici_plan, after=w_in_t)

    inv_freq = ROPE_THETA ** (-jnp.arange(0, HEAD_DIM, 2, dtype=F32) / HEAD_DIM)
    cos, sin = _rope_tables(positions, jnp.tile(inv_freq, LANES // (HEAD_DIM // 2)).reshape(1, LANES))
    u, vg, q, k, va = _in_proj(x2, w_in_t, cos, sin, dep=started)
    bias_full = jnp.repeat(b_spatial[0].T, HEAD_DIM, axis=1)
    sink_vec = sinks.reshape(N_HEADS)
    cat = _mixer_fwd(u, vg, q, k, va, v_ln_g, v_ln_b, w_spatial[0], bias_full, sink_vec)
    bufs, passed, pass_send, pass_recv = _split_call("gather_pass", bufs, wait=(ici_plan, ici_send, ici_recv), start=pass_plan, after=cat)
    bufs, _ = _split_call("gather_end", bufs, wait=(pass_plan, pass_send, pass_recv), after=passed)
    w_out_all = bufs[3]
    w1_all = bufs[4].reshape(N_FF_BLOCKS, D_MODEL, D_MODEL)
    w2_all = bufs[5].reshape(N_FF_BLOCKS, D_MODEL, D_MODEL)
    xhat1, rstd1, x1b, r, dz2, dz2b, d_ln2_g, d_ln2_b, sq_err = _ffn_fwd_loss(
        cat, x2, w_out_all, ln1_g, ln1_b, w1_all, w2_all, ln2_g, ln2_b, target)

    pos = jnp.stack([lax.axis_index("c"), 2 * lax.axis_index("x") + lax.axis_index("y")]).astype(jnp.int32)
    half_landing = lambda g: _landing(g.shape[0] // 2, D_MODEL, F32)
    g_ff2_local = _grad_w_ff2(r, dz2b)
    swap_plan = _swap_plan([D_FF // N_CHIPS])
    ff2_bufs, swapping2, swap2_send, swap2_recv = _split_call("ff2_swap_start", [g_ff2_local, half_landing(g_ff2_local)], start=swap_plan)
    dpre, dz1, dz1b, dcat, d_ln1_g, d_ln1_b = _ffn_bwd_ln1(dz2, r, xhat1, rstd1, ln1_g, w1_all, w2_all, w_out_all, dep=swapping2)
    g_ff1_local = _grad_w_ff1(x1b, dpre)
    ff1_bufs, swapping1, swap1_send, swap1_recv = _split_call("ff1_swap_start", [g_ff1_local, half_landing(g_ff1_local)], start=swap_plan)
    g_out_local = _grad_w_out(cat, dz1b, dep=swapping1)
    ff2_bufs, swapped2 = _split_call("ff2_swap_wait", ff2_bufs, wait=(swap_plan, swap2_send, swap2_recv), after=g_out_local)
    ff1_bufs, _ = _split_call("ff1_swap_wait", ff1_bufs, wait=(swap_plan, swap1_send, swap1_recv), after=swapped2)
    ff_sums = [_pair_sum("grad_pair_sum_w_ff1", ff1_bufs[0], ff1_bufs[1], pos), _pair_sum("grad_pair_sum_w_ff2", ff2_bufs[0], ff2_bufs[1], pos)]
    ff_halves = [p.shape[0] // N_CHIPS for p, _ in ff_sums]
    exchange_plan = _exchange_plan(ff_halves)
    bufs, exchanging, ex_send, ex_recv = _split_call(
        "ff_exchange_start", [p for p, _ in ff_sums] + [_landing(3 * h, D_MODEL, BF16) for h in ff_halves], start=exchange_plan)
    dh_main, dkv, d_v_ln_g, d_v_ln_b, d_w_spatial, d_b_spatial_t, d_sinks = _mixer_bwd(
        u, vg, q, k, va, dcat, cos, sin, v_ln_g, v_ln_b, w_spatial[0], bias_full, sink_vec, dep=exchanging)
    g_in_local = _grad_w_in_t(dh_main, dkv, x2)

    small = [g_in_local, g_out_local]
    theirs = _pair_swap(small)
    pair_sums = [_pair_sum("grad_pair_sum_" + nm, g, th, pos) for nm, g, th in zip(["w_in", "w_out"], small, theirs)]
    small_halves = [p.shape[0] // N_CHIPS for p, _ in pair_sums]
    small_plan = _exchange_plan(small_halves)
    small_bufs, small_exchanging, sm_send, sm_recv = _split_call(
        "small_exchange_start", [p for p, _ in pair_sums] + [_landing(3 * h, D_MODEL, BF16) for h in small_halves], start=small_plan)
    grad_x_flat = _grad_x(dh_main, dkv, dz1, w_in_t, dep=small_exchanging)
    grad_x = grad_x_flat.reshape(1, t, D_MODEL)
    bufs, ff_done = _split_call("ff_exchange_wait", bufs, wait=(exchange_plan, ex_send, ex_recv), after=grad_x_flat)
    ff_shards = [_chip_sum("grad_chip_sum_" + nm, own, ld, pos) for nm, (_, own), ld in zip(["w_ff1", "w_ff2"], ff_sums, bufs[2:])]
    small_bufs, _ = _split_call("small_exchange_wait", small_bufs, wait=(small_plan, sm_send, sm_recv), after=ff_done)
    shards = [_chip_sum("grad_chip_sum_" + nm, own, ld, pos) for nm, (_, own), ld in zip(["w_in", "w_out"], pair_sums, small_bufs[2:])]
    g_w_in_t, g_w_out, g_w_ff1, g_w_ff2 = _pair_gather(shards + ff_shards)

    small_g = _all_reduce_small(_pack_small(
        [d_v_ln_g, d_v_ln_b, d_w_spatial, d_b_spatial_t[:, :N_HEADS].T, d_sinks[0, :N_HEADS], d_ln1_g, d_ln1_b, d_ln2_g, d_ln2_b, sq_err]))
    sq_row = sum(rows for _, _, rows in _SMALL[:N_SMALL_PARAMS])
    loss = 0.5 * jnp.sum(small_g[sq_row : sq_row + _SMALL[N_SMALL_PARAMS][2]]) / D_MODEL

    g_w_in, d_w_in, nm_w_in, nv_w_in = (a.T for a in _adamw("adamw_w_in", w_in[0].T, g_w_in_t, m_w_in[0].T, v_w_in[0].T))
    g_w_out, d_w_out, nm_w_out, nv_w_out = _adamw("adamw_w_out", w_out[0], g_w_out, m_w_out[0], v_w_out[0])
    g_w_ff1, d_w_ff1, nm_w_ff1, nv_w_ff1 = _adamw("adamw_w_ff1", w_ff1[0], g_w_ff1, m_w_ff1[0], v_w_ff1[0])
    g_w_ff2, d_w_ff2, nm_w_ff2, nv_w_ff2 = _adamw("adamw_w_ff2", w_ff2[0], g_w_ff2, m_w_ff2[0], v_w_ff2[0])
    small_grads, small_d, small_nm, small_nv = _adamw_small(
        small_g,
        [v_ln_g, v_ln_b, w_spatial, b_spatial, sinks, ln1_g, ln1_b, ln2_g, ln2_b],
        [m_v_ln_g, m_v_ln_b, m_w_spatial, m_b_spatial, m_sinks, m_ln1_g, m_ln1_b, m_ln2_g, m_ln2_b],
        [v_v_ln_g, v_v_ln_b, v_w_spatial, v_b_spatial, v_sinks, v_ln1_g, v_ln1_b, v_ln2_g, v_ln2_b])

    def with_big(small, w_in_v, w_out_v, w_ff1_v, w_ff2_v):
        g_vg, g_vb, g_ws, g_bs, g_sk, g_1g, g_1b, g_2g, g_2b = small
        return [w_in_v[None], g_vg, g_vb, g_ws, g_bs, g_sk, w_out_v[None], g_1g, g_1b, w_ff1_v[None], w_ff2_v[None], g_2g, g_2b]

    return (
        loss,
        grad_x,
        *with_big(small_grads, g_w_in, g_w_out, g_w_ff1, g_w_ff2),
        *with_big(small_d, d_w_in, d_w_out, d_w_ff1, d_w_ff2),
        *with_big(small_nm, nm_w_in, nm_w_out, nm_w_ff1, nm_w_ff2),
        *with_big(small_nv, nv_w_in, nv_w_out, nv_w_ff1, nv_w_ff2),
    )
```

```python
import functools
import math

import jax
import jax.numpy as jnp
from jax import lax
from jax.experimental import pallas as pl
from jax.experimental.pallas import tpu as pltpu

F32 = jnp.float32
BF16 = jnp.bfloat16

D_MODEL = 1024
HEAD_DIM = 64
D_GMLP = 512
D_ATTN = 512
D_KV = 128
D_IN = 2 * D_GMLP + D_ATTN + 2 * D_KV
D_MAIN = 2 * D_GMLP + D_ATTN
N_HEADS = 8
CHUNK = 128
ROPE_THETA = 10000.0
D_FF = 4 * D_MODEL
N_FF_BLOCKS = 4
LN_EPS = 1e-5
ALPHA = (2.0 * 1) ** 0.25
NEG_INF = -1e30
SCALE = 1.0 / math.sqrt(HEAD_DIM)

ADAM_LR = 0.001
ADAM_B1 = 0.9
ADAM_B2 = 0.999
ADAM_EPS = 1e-08
ADAM_WD = 0.01
ADAM_STEP = 10

N_CHIPS = 4
LANES = 128
V7X_VMEM_BYTES = 64 * 1024 * 1024
VMEM_LIMIT = V7X_VMEM_BYTES - 8 * 1024 * 1024
TM = 512
TM_FFN = 256
TK = 1024
TK_FF = 1024
SMALL_ROWS = 1152
MESH = pl.DeviceIdType.MESH

NT = (((1,), (1,)), ((), ()))
TN = (((0,), (0,)), ((), ()))


def _dot(a, b, dims=None):
    if dims is None:
        return jnp.dot(a, b, preferred_element_type=F32)
    return lax.dot_general(a, b, dims, preferred_element_type=F32)


def _params(semantics=None):
    return pltpu.CompilerParams(dimension_semantics=semantics, vmem_limit_bytes=VMEM_LIMIT)


def _const_spec(shape, single_buffer=False):
    zeros = (0,) * len(shape)
    if single_buffer:
        return pl.BlockSpec(shape, lambda *_: zeros, pipeline_mode=pl.Buffered(1))
    return pl.BlockSpec(shape, lambda *_: zeros)


def _row_spec(rows, cols):
    return pl.BlockSpec((rows, cols), lambda i: (i, 0))


def _after(dep, body, in_specs, operands):
    if dep is None:
        return body, list(in_specs), list(operands)
    return (lambda dep_ref, *refs: body(*refs)), [pl.BlockSpec(memory_space=pl.ANY)] + list(in_specs), [dep] + list(operands)


def _gelu(x):
    k = math.sqrt(2.0 / math.pi)
    return 0.5 * x * (1.0 + jnp.tanh(k * (x + 0.044715 * (x * x * x))))


def _gelu_and_grad(x):
    k = math.sqrt(2.0 / math.pi)
    x2 = x * x
    t = jnp.tanh(k * (x + 0.044715 * (x2 * x)))
    g = 0.5 * x * (1.0 + t)
    dg = 0.5 * (1.0 + t) + 0.5 * x * (1.0 - t * t) * (k * (1.0 + 3.0 * 0.044715 * x2))
    return g, dg


def _layer_norm_stats(z):
    mu = jnp.mean(z, axis=-1, keepdims=True)
    zc = z - mu
    var = jnp.mean(zc * zc, axis=-1, keepdims=True)
    rstd = lax.rsqrt(var + LN_EPS)
    return zc * rstd, rstd


def _layer_norm_bwd(dxhat, xhat, rstd):
    m1 = jnp.mean(dxhat, axis=-1, keepdims=True)
    m2 = jnp.mean(dxhat * xhat, axis=-1, keepdims=True)
    return rstd * (dxhat - m1 - xhat * m2)


def _rotate_half(t):
    n = t.shape[1]
    lane = lax.broadcasted_iota(jnp.int32, t.shape, 1)
    first = (lane & (HEAD_DIM // 2)) == 0
    return jnp.where(first, -pltpu.roll(t, n - HEAD_DIM // 2, 1), pltpu.roll(t, HEAD_DIM // 2, 1))


def _rope(t, cos, sin):
    return t * cos + _rotate_half(t) * sin


def _rope_transposed(g, cos, sin):
    return g * cos - _rotate_half(g * sin)


def _lane_tile(a, reps):
    return jnp.tile(a, (1, reps)) if reps > 1 else a


def _rope_tables(pos_row, inv_freq_row):
    t = pos_row.shape[1]

    def body(pos_ref, f_ref, cos_ref, sin_ref):
        pos_rows = jnp.broadcast_to(pos_ref[...].astype(F32), (LANES, TM)).T
        ang = pos_rows * f_ref[...]
        cos_ref[...] = jnp.cos(ang)
        sin_ref[...] = jnp.sin(ang)

    return pl.pallas_call(
        body,
        name="rope_tables",
        grid=(t // TM,),
        in_specs=[pl.BlockSpec((1, TM), lambda i: (0, i)), _const_spec((1, LANES))],
        out_specs=[_row_spec(TM, LANES), _row_spec(TM, LANES)],
        out_shape=[jax.ShapeDtypeStruct((t, LANES), F32)] * 2,
        compiler_params=_params(("parallel",)),
    )(pos_row, inv_freq_row)


def _in_proj(x, w_in_t, cos, sin, dep=None):
    t = x.shape[0]

    def body(x_ref, w_ref, cos_ref, sin_ref, u_ref, vg_ref, q_ref, k_ref, va_ref):
        xb = x_ref[...].astype(BF16)
        u_ref[...] = _dot(xb, w_ref[0:D_GMLP, :], NT)
        vg_ref[...] = _dot(xb, w_ref[D_GMLP : 2 * D_GMLP, :], NT)
        q = _dot(xb, w_ref[2 * D_GMLP : D_MAIN, :], NT)
        k = _dot(xb, w_ref[D_MAIN : D_MAIN + D_KV, :], NT)
        va_ref[...] = _dot(xb, w_ref[D_MAIN + D_KV : D_IN, :], NT).astype(BF16)
        c, s = cos_ref[...], sin_ref[...]
        q_ref[...] = _rope(q, _lane_tile(c, D_ATTN // LANES), _lane_tile(s, D_ATTN // LANES)).astype(BF16)
        k_ref[...] = _rope(k, c, s).astype(BF16)

    body, in_specs, operands = _after(
        dep, body, [_row_spec(TM, D_MODEL), _const_spec((D_IN, D_MODEL)), _row_spec(TM, LANES), _row_spec(TM, LANES)], [x, w_in_t, cos, sin])
    return pl.pallas_call(
        body,
        name="in_proj",
        grid=(t // TM,),
        in_specs=in_specs,
        out_specs=[_row_spec(TM, D_GMLP), _row_spec(TM, D_GMLP), _row_spec(TM, D_ATTN), _row_spec(TM, D_KV), _row_spec(TM, D_KV)],
        out_shape=[
            jax.ShapeDtypeStruct((t, D_GMLP), F32),
            jax.ShapeDtypeStruct((t, D_GMLP), F32),
            jax.ShapeDtypeStruct((t, D_ATTN), BF16),
            jax.ShapeDtypeStruct((t, D_KV), BF16),
            jax.ShapeDtypeStruct((t, D_KV), BF16),
        ],
        compiler_params=_params(("parallel",)),
    )(*operands)


def _chunk_specs():
    cur = lambda i: (i, 0)
    prev = lambda i: (jnp.maximum(i - 1, 0), 0)
    return [
        pl.BlockSpec((CHUNK, D_GMLP), cur),
        pl.BlockSpec((CHUNK, D_GMLP), cur),
        pl.BlockSpec((CHUNK, D_ATTN), cur),
        pl.BlockSpec((CHUNK, D_KV), cur),
        pl.BlockSpec((CHUNK, D_KV), prev),
        pl.BlockSpec((CHUNK, D_KV), cur),
        pl.BlockSpec((CHUNK, D_KV), prev),
    ]


def _half_lane_masks(rows):
    lane = lax.broadcasted_iota(jnp.int32, (rows, LANES), 1)
    return lane < HEAD_DIM


def _kv_variants(kv2):
    left = _half_lane_masks(kv2.shape[0])
    f = kv2.astype(F32)
    swapped = pltpu.roll(f, HEAD_DIM, 1)
    zero = jnp.zeros_like(f)
    g0 = (jnp.where(left, f, zero).astype(BF16), jnp.where(left, zero, swapped).astype(BF16))
    g1 = (jnp.where(left, swapped, zero).astype(BF16), jnp.where(left, zero, f).astype(BF16))
    return (g0, g1)


def _band_mask(i, heads=1):
    row = lax.broadcasted_iota(jnp.int32, (heads * CHUNK, 2 * CHUNK), 0) & (CHUNK - 1)
    col = lax.broadcasted_iota(jnp.int32, (heads * CHUNK, 2 * CHUNK), 1)
    no_prev = jnp.where(i > 0, 0, 4 * CHUNK)
    in_prev = jnp.logical_and(col < CHUNK, (col - row) > no_prev)
    in_cur = jnp.logical_and(col >= CHUNK, (col - CHUNK) <= row)
    return jnp.logical_or(in_prev, in_cur)


def _softmax_with_sink(s, sink):
    m = jnp.maximum(jnp.max(s, axis=1, keepdims=True), sink)
    e = jnp.exp(s - m)
    e_sink = jnp.exp(sink - m)
    inv = 1.0 / (jnp.sum(e, axis=1, keepdims=True) + e_sink)
    return e * inv, e_sink * inv


def _causal_mask():
    row = lax.broadcasted_iota(jnp.int32, (CHUNK, CHUNK), 0)
    col = lax.broadcasted_iota(jnp.int32, (CHUNK, CHUNK), 1)
    return col <= row


def _store_spatial_weights(w_ref, wcat_ref, wcat_t_ref=None):
    causal = _causal_mask()
    for p in range(D_GMLP // LANES):
        wl = jnp.where(causal, w_ref[2 * p], 0.0)
        wr = jnp.where(causal, w_ref[2 * p + 1], 0.0)
        wcat_ref[p] = jnp.concatenate([wl, wr], axis=1).astype(BF16)
        if wcat_t_ref is not None:
            wcat_t_ref[p] = jnp.concatenate([wl.T, wr.T], axis=1).astype(BF16)


def _pair_stack(xp, left):
    return jnp.concatenate([jnp.where(left, xp, 0.0), jnp.where(left, 0.0, xp)], axis=0).astype(BF16)


def _mixer_fwd(u, vg, q, k, va, v_ln_g, v_ln_b, w_spatial, bias_full, sinks):
    t = u.shape[0]

    def body(u_ref, vg_ref, q_ref, kc_ref, kp_ref, vc_ref, vp_ref, g_ref, b_ref, w_ref, bias_ref, sink_ref, cat_ref, wcat):
        i = pl.program_id(0)
        left = _half_lane_masks(CHUNK)

        @pl.when(i == 0)
        def _():
            _store_spatial_weights(w_ref, wcat)

        ug = _gelu(u_ref[...])
        xhat, _ = _layer_norm_stats(_gelu(vg_ref[...]))
        vgl = xhat * g_ref[...] + b_ref[...]
        for p in range(D_GMLP // LANES):
            cols = slice(p * LANES, (p + 1) * LANES)
            mixed = _dot(wcat[p], _pair_stack(vgl[:, cols], left))
            cat_ref[:, cols] = (ug[:, cols] * (mixed + bias_ref[:, cols])).astype(BF16)

        k_var = _kv_variants(jnp.concatenate([kp_ref[...], kc_ref[...]], axis=0))
        v_var = _kv_variants(jnp.concatenate([vp_ref[...], vc_ref[...]], axis=0))
        valid = _band_mask(i)
        scores = [_dot(q_ref[:, (h // 2) * LANES : (h // 2 + 1) * LANES], k_var[h // 4][h % 2], NT) for h in range(N_HEADS)]
        heads = range(N_HEADS)
        sinks_h = [sink_ref[h] for h in heads]
        masked = [jnp.where(valid, scores[h] * SCALE, NEG_INF) for h in heads]
        maxes = [jnp.maximum(jnp.max(masked[h], axis=1, keepdims=True), sinks_h[h]) for h in heads]
        exps = [jnp.exp(masked[h] - maxes[h]) for h in heads]
        invs = [1.0 / (jnp.sum(exps[h], axis=1, keepdims=True) + jnp.exp(sinks_h[h] - maxes[h])) for h in heads]
        probs = [(exps[h] * invs[h]).astype(BF16) for h in heads]
        for p in range(D_ATTN // LANES):
            out = _dot(probs[2 * p], v_var[p // 2][0]) + _dot(probs[2 * p + 1], v_var[p // 2][1])
            cat_ref[:, D_GMLP + p * LANES : D_GMLP + (p + 1) * LANES] = out.astype(BF16)

    return pl.pallas_call(
        body,
        name="mixer_fwd",
        grid=(t // CHUNK,),
        in_specs=_chunk_specs()
        + [
            _const_spec((1, D_GMLP)),
            _const_spec((1, D_GMLP)),
            _const_spec((N_HEADS, CHUNK, CHUNK)),
            _const_spec((CHUNK, D_GMLP)),
            pl.BlockSpec(memory_space=pltpu.SMEM),
        ],
        out_specs=pl.BlockSpec((CHUNK, D_MODEL), lambda i: (i, 0)),
        out_shape=jax.ShapeDtypeStruct((t, D_MODEL), BF16),
        scratch_shapes=[pltpu.VMEM((D_GMLP // LANES, CHUNK, 2 * CHUNK), BF16)],
        compiler_params=_params(("arbitrary",)),
    )(u, vg, q, k, k, va, va, v_ln_g, v_ln_b, w_spatial, bias_full, sinks)


def _ffn_fwd_loss(cat, x, w_out, ln1_g, ln1_b, w1, w2, ln2_g, ln2_b, target):
    t = x.shape[0]

    def body(cat_ref, x_ref, wo_ref, g1_ref, b1_ref, w1_ref, w2_ref, g2_ref, b2_ref, tgt_ref,
             xh_ref, rstd_ref, x1b_ref, r_ref, dz2_ref, dz2b_ref, dg2_ref, db2_ref, sq_ref):
        @pl.when(pl.program_id(0) == 0)
        def _():
            dg2_ref[...] = jnp.zeros_like(dg2_ref)
            db2_ref[...] = jnp.zeros_like(db2_ref)
            sq_ref[...] = jnp.zeros_like(sq_ref)

        xhat1, rstd1 = _layer_norm_stats(ALPHA * x_ref[...] + _dot(cat_ref[...], wo_ref[...]))
        xh_ref[...] = xhat1
        rstd_ref[...] = rstd1
        x1 = xhat1 * g1_ref[...] + b1_ref[...]
        x1b = x1.astype(BF16)
        x1b_ref[...] = x1b
        ff = jnp.zeros((TM_FFN, D_MODEL), F32)
        for j in range(N_FF_BLOCKS):
            r = jnp.maximum(_dot(x1b, w1_ref[j]), 0.0)
            r_ref[:, j * D_MODEL : (j + 1) * D_MODEL] = r.astype(BF16)
            ff = ff + _dot((r * r).astype(BF16), w2_ref[j])
        xhat2, rstd2 = _layer_norm_stats(ALPHA * x1 + ff)
        err = xhat2 * g2_ref[...] + b2_ref[...] - tgt_ref[...]
        sq_ref[...] += jnp.sum(err * err, axis=0, keepdims=True)
        dy = err * (1.0 / D_MODEL)
        dg2_ref[...] += jnp.sum(dy * xhat2, axis=0, keepdims=True)
        db2_ref[...] += jnp.sum(dy, axis=0, keepdims=True)
        dz2 = _layer_norm_bwd(dy * g2_ref[...], xhat2, rstd2)
        dz2_ref[...] = dz2
        dz2b_ref[...] = dz2.astype(BF16)

    vec = _const_spec((1, D_MODEL))
    tile = _row_spec(TM_FFN, D_MODEL)
    wspec = _const_spec((N_FF_BLOCKS, D_MODEL, D_MODEL), single_buffer=True)
    return pl.pallas_call(
        body,
        name="ffn_fwd_loss",
        grid=(t // TM_FFN,),
        in_specs=[tile, tile, _const_spec((D_MODEL, D_MODEL), single_buffer=True), vec, vec, wspec, wspec, vec, vec, tile],
        out_specs=[tile, _row_spec(TM_FFN, 1), tile, _row_spec(TM_FFN, D_FF), tile, tile, vec, vec, vec],
        out_shape=[
            jax.ShapeDtypeStruct((t, D_MODEL), F32),
            jax.ShapeDtypeStruct((t, 1), F32),
            jax.ShapeDtypeStruct((t, D_MODEL), BF16),
            jax.ShapeDtypeStruct((t, D_FF), BF16),
            jax.ShapeDtypeStruct((t, D_MODEL), F32),
            jax.ShapeDtypeStruct((t, D_MODEL), BF16),
            jax.ShapeDtypeStruct((1, D_MODEL), F32),
            jax.ShapeDtypeStruct((1, D_MODEL), F32),
            jax.ShapeDtypeStruct((1, D_MODEL), F32),
        ],
        compiler_params=_params(("arbitrary",)),
    )(cat, x, w_out, ln1_g, ln1_b, w1, w2, ln2_g, ln2_b, target)


def _ffn_bwd_ln1(dz2, r, xhat1, rstd1, ln1_g, w1, w2, w_out, dep=None):
    t = dz2.shape[0]

    def body(dz2_ref, r_ref, xh_ref, rstd_ref, g1_ref, w1_ref, w2_ref, wo_ref, dpre_ref, dz1_ref, dz1b_ref, dcat_ref, dg1_ref, db1_ref):
        @pl.when(pl.program_id(0) == 0)
        def _():
            dg1_ref[...] = jnp.zeros_like(dg1_ref)
            db1_ref[...] = jnp.zeros_like(db1_ref)

        dz2 = dz2_ref[...]
        dz2b = dz2.astype(BF16)
        dx1 = ALPHA * dz2
        for j in range(N_FF_BLOCKS):
            cols = slice(j * D_MODEL, (j + 1) * D_MODEL)
            dpre = (_dot(dz2b, w2_ref[j], NT) * (2.0 * r_ref[:, cols].astype(F32))).astype(BF16)
            dpre_ref[:, cols] = dpre
            dx1 = dx1 + _dot(dpre, w1_ref[j], NT)
        xhat1 = xh_ref[...]
        dg1_ref[...] += jnp.sum(dx1 * xhat1, axis=0, keepdims=True)
        db1_ref[...] += jnp.sum(dx1, axis=0, keepdims=True)
        dz1 = _layer_norm_bwd(dx1 * g1_ref[...], xhat1, rstd_ref[...])
        dz1_ref[...] = dz1
        dz1b = dz1.astype(BF16)
        dz1b_ref[...] = dz1b
        dcat_ref[...] = _dot(dz1b, wo_ref[...], NT).astype(BF16)

    vec = _const_spec((1, D_MODEL))
    tile = _row_spec(TM_FFN, D_MODEL)
    wspec = _const_spec((N_FF_BLOCKS, D_MODEL, D_MODEL), single_buffer=True)
    body, in_specs, operands = _after(
        dep, body,
        [tile, _row_spec(TM_FFN, D_FF), tile, _row_spec(TM_FFN, 1), vec, wspec, wspec, _const_spec((D_MODEL, D_MODEL), single_buffer=True)],
        [dz2, r, xhat1, rstd1, ln1_g, w1, w2, w_out])
    return pl.pallas_call(
        body,
        name="ffn_bwd_ln1",
        grid=(t // TM_FFN,),
        in_specs=in_specs,
        out_specs=[_row_spec(TM_FFN, D_FF), tile, tile, tile, vec, vec],
        out_shape=[
            jax.ShapeDtypeStruct((t, D_FF), BF16),
            jax.ShapeDtypeStruct((t, D_MODEL), F32),
            jax.ShapeDtypeStruct((t, D_MODEL), BF16),
            jax.ShapeDtypeStruct((t, D_MODEL), BF16),
            jax.ShapeDtypeStruct((1, D_MODEL), F32),
            jax.ShapeDtypeStruct((1, D_MODEL), F32),
        ],
        compiler_params=_params(("arbitrary",)),
    )(*operands)


def _mixer_bwd(u, vg, q, k, va, dcat, cos, sin, v_ln_g, v_ln_b, w_spatial, bias_full, sinks, dep=None):
    t = u.shape[0]
    n_chunks = t // CHUNK

    def body(u_ref, vg_ref, q_ref, kc_ref, kp_ref, vc_ref, vp_ref, dcat_ref, cosc_ref, sinc_ref, cosp_ref, sinp_ref,
             g_ref, b_ref, w_ref, bias_ref, sink_ref,
             dmain_ref, dkv_ref, dg_ref, db_ref, dw_ref, dbs_ref, dsink_ref, dmix_acc, wcat, wcat_t):
        i = pl.program_id(0)
        left = _half_lane_masks(CHUNK)
        lane = lax.broadcasted_iota(jnp.int32, (CHUNK, LANES), 1)
        n_pairs = D_GMLP // LANES

        @pl.when(i == 0)
        def _():
            dg_ref[...] = jnp.zeros_like(dg_ref)
            db_ref[...] = jnp.zeros_like(db_ref)
            dw_ref[...] = jnp.zeros_like(dw_ref)
            dsink_ref[...] = jnp.zeros_like(dsink_ref)
            dmix_acc[...] = jnp.zeros_like(dmix_acc)
            _store_spatial_weights(w_ref, wcat, wcat_t)

        ug, dug_du = _gelu_and_grad(u_ref[...])
        gv, dgv_dv = _gelu_and_grad(vg_ref[...])
        xhat, rstd = _layer_norm_stats(gv)
        gain = g_ref[...]
        vgl = xhat * gain + b_ref[...]
        pair_cols = [slice(p * LANES, (p + 1) * LANES) for p in range(n_pairs)]
        mixed = [_dot(wcat[p], _pair_stack(vgl[:, cols], left)) for p, cols in enumerate(pair_cols)]
        dm_stacks = []
        for p, cols in enumerate(pair_cols):
            da = dcat_ref[:, cols].astype(F32)
            dmain_ref[:, cols] = (da * (mixed[p] + bias_ref[:, cols]) * dug_du[:, cols]).astype(BF16)
            dmixed = da * ug[:, cols]
            dmix_acc[:, cols] += dmixed
            dm_stacks.append(_pair_stack(dmixed, left))
        causal = _causal_mask()
        for p, cols in enumerate(pair_cols):
            dw_pair = _dot(dm_stacks[p], vgl[:, cols].astype(BF16), NT)
            dw_ref[2 * p] += jnp.where(causal, dw_pair[:CHUNK], 0.0)
            dw_ref[2 * p + 1] += jnp.where(causal, dw_pair[CHUNK:], 0.0)
        dvgl = jnp.concatenate([_dot(wcat_t[p], dm_stacks[p]) for p in range(n_pairs)], axis=1)
        dg_ref[...] += jnp.sum(dvgl * xhat, axis=0, keepdims=True)
        db_ref[...] += jnp.sum(dvgl, axis=0, keepdims=True)
        dgv = _layer_norm_bwd(dvgl * gain, xhat, rstd)
        dmain_ref[:, D_GMLP : 2 * D_GMLP] = (dgv * dgv_dv).astype(BF16)

        @pl.when(i == n_chunks - 1)
        def _():
            tile = jnp.zeros((CHUNK, LANES), F32)
            for p in range(D_GMLP // LANES):
                dm = dmix_acc[:, p * LANES : (p + 1) * LANES]
                sl = jnp.sum(jnp.where(left, dm, 0.0), axis=1, keepdims=True)
                sr = jnp.sum(jnp.where(left, 0.0, dm), axis=1, keepdims=True)
                tile = jnp.where(lane == 2 * p, sl, tile)
                tile = jnp.where(lane == 2 * p + 1, sr, tile)
            dbs_ref[...] = tile

        k_var = _kv_variants(jnp.concatenate([kp_ref[...], kc_ref[...]], axis=0))
        v_var = _kv_variants(jnp.concatenate([vp_ref[...], vc_ref[...]], axis=0))
        valid = _band_mask(i)
        n_qpairs = D_ATTN // LANES
        q_pairs = [q_ref[:, p * LANES : (p + 1) * LANES] for p in range(n_qpairs)]
        do_all = dcat_ref[:, D_GMLP:D_MODEL]
        do_pairs = [do_all[:, p * LANES : (p + 1) * LANES] for p in range(n_qpairs)]
        scores = [_dot(q_pairs[h // 2], k_var[h // 4][h % 2], NT) for h in range(N_HEADS)]
        dprobs = [_dot(do_pairs[h // 2], v_var[h // 4][h % 2], NT) for h in range(N_HEADS)]
        heads = range(N_HEADS)
        sinks_h = [sink_ref[h] for h in heads]
        masked = [jnp.where(valid, scores[h] * SCALE, NEG_INF) for h in heads]
        maxes = [jnp.maximum(jnp.max(masked[h], axis=1, keepdims=True), sinks_h[h]) for h in heads]
        exps = [jnp.exp(masked[h] - maxes[h]) for h in heads]
        exp_sinks = [jnp.exp(sinks_h[h] - maxes[h]) for h in heads]
        invs = [1.0 / (jnp.sum(exps[h], axis=1, keepdims=True) + exp_sinks[h]) for h in heads]
        probs = [exps[h] * invs[h] for h in heads]
        dsums = [jnp.sum(probs[h] * dprobs[h], axis=1, keepdims=True) for h in heads]
        ds_b = [(probs[h] * (dprobs[h] - dsums[h]) * SCALE).astype(BF16) for h in heads]
        probs_b = [probs[h].astype(BF16) for h in heads]
        dsink_row = jnp.zeros((1, LANES), F32)
        lane_row = lax.broadcasted_iota(jnp.int32, (1, LANES), 1)
        for h in heads:
            d_sink = -jnp.sum(exp_sinks[h] * invs[h] * dsums[h], axis=0, keepdims=True)
            dsink_row = dsink_row + jnp.where(lane_row == h, d_sink, 0.0)
        dsink_ref[0:1, :] += dsink_row
        dq_all = jnp.concatenate(
            [_dot(ds_b[2 * p], k_var[p // 2][0]) + _dot(ds_b[2 * p + 1], k_var[p // 2][1]) for p in range(n_qpairs)], axis=1)
        cos_c, sin_c = cosc_ref[...], sinc_ref[...]
        dmain_ref[:, 2 * D_GMLP : D_MAIN] = _rope_transposed(dq_all, _lane_tile(cos_c, n_qpairs), _lane_tile(sin_c, n_qpairs)).astype(BF16)

        q_t = q_ref[...].astype(F32).T.astype(BF16)
        do_t = do_all.astype(F32).T.astype(BF16)
        heads_per_group = N_HEADS // 2

        def group_grad_t(lhs_t, rhs_heads):
            parts = []
            for g in range(2):
                heads = range(g * heads_per_group, (g + 1) * heads_per_group)
                lhs = jnp.concatenate([lhs_t[h * HEAD_DIM : (h + 1) * HEAD_DIM] for h in heads], axis=1)
                parts.append(_dot(lhs, jnp.concatenate([rhs_heads[h] for h in heads], axis=0)))
            return jnp.concatenate(parts, axis=0).T

        dk2 = group_grad_t(q_t, ds_b)
        dv2 = group_grad_t(do_t, probs_b)
        cur = pl.ds(pl.multiple_of(i * CHUNK, CHUNK), CHUNK)
        dkv_ref[cur, 0:D_KV] = _rope_transposed(dk2[CHUNK:], cos_c, sin_c)
        dkv_ref[cur, D_KV : 2 * D_KV] = dv2[CHUNK:]

        @pl.when(i > 0)
        def _():
            prev = pl.ds(pl.multiple_of((i - 1) * CHUNK, CHUNK), CHUNK)
            dkv_ref[prev, 0:D_KV] += _rope_transposed(dk2[:CHUNK], cosp_ref[...], sinp_ref[...])
            dkv_ref[prev, D_KV : 2 * D_KV] += dv2[:CHUNK]

    cur = lambda i: (i, 0)
    prev = lambda i: (jnp.maximum(i - 1, 0), 0)
    in_specs = _chunk_specs() + [
        pl.BlockSpec((CHUNK, D_MODEL), cur),
        pl.BlockSpec((CHUNK, LANES), cur),
        pl.BlockSpec((CHUNK, LANES), cur),
        pl.BlockSpec((CHUNK, LANES), prev),
        pl.BlockSpec((CHUNK, LANES), prev),
        _const_spec((1, D_GMLP)),
        _const_spec((1, D_GMLP)),
        _const_spec((N_HEADS, CHUNK, CHUNK)),
        _const_spec((CHUNK, D_GMLP)),
        pl.BlockSpec(memory_space=pltpu.SMEM),
    ]
    body, in_specs, operands = _after(
        dep, body, in_specs, [u, vg, q, k, k, va, va, dcat, cos, sin, cos, sin, v_ln_g, v_ln_b, w_spatial, bias_full, sinks])
    return pl.pallas_call(
        body,
        name="mixer_bwd",
        grid=(n_chunks,),
        in_specs=in_specs,
        out_specs=[
            pl.BlockSpec((CHUNK, D_MAIN), cur),
            _const_spec((t, 2 * D_KV)),
            _const_spec((1, D_GMLP)),
            _const_spec((1, D_GMLP)),
            _const_spec((N_HEADS, CHUNK, CHUNK)),
            _const_spec((CHUNK, LANES)),
            _const_spec((8, LANES)),
        ],
        out_shape=[
            jax.ShapeDtypeStruct((t, D_MAIN), BF16),
            jax.ShapeDtypeStruct((t, 2 * D_KV), F32),
            jax.ShapeDtypeStruct((1, D_GMLP), F32),
            jax.ShapeDtypeStruct((1, D_GMLP), F32),
            jax.ShapeDtypeStruct((N_HEADS, CHUNK, CHUNK), F32),
            jax.ShapeDtypeStruct((CHUNK, LANES), F32),
            jax.ShapeDtypeStruct((8, LANES), F32),
        ],
        scratch_shapes=[
            pltpu.VMEM((CHUNK, D_GMLP), F32),
            pltpu.VMEM((D_GMLP // LANES, CHUNK, 2 * CHUNK), BF16),
            pltpu.VMEM((D_GMLP // LANES, CHUNK, 2 * CHUNK), BF16),
        ],
        compiler_params=_params(("arbitrary",)),
    )(*operands)


def _grad_x(dh_main, dkv, dz1, w_in_t, dep=None):
    t = dz1.shape[0]

    def body(dm_ref, dkv_ref, dz1_ref, w_ref, gx_ref):
        acc = ALPHA * dz1_ref[...] + _dot(dm_ref[...], w_ref[0:D_MAIN, :])
        gx_ref[...] = acc + _dot(dkv_ref[...].astype(BF16), w_ref[D_MAIN:D_IN, :])

    body, in_specs, operands = _after(
        dep, body, [_row_spec(TM, D_MAIN), _row_spec(TM, 2 * D_KV), _row_spec(TM, D_MODEL), _const_spec((D_IN, D_MODEL))], [dh_main, dkv, dz1, w_in_t])
    return pl.pallas_call(
        body,
        name="grad_x",
        grid=(t // TM,),
        in_specs=in_specs,
        out_specs=_row_spec(TM, D_MODEL),
        out_shape=jax.ShapeDtypeStruct((t, D_MODEL), F32),
        compiler_params=_params(("parallel",)),
    )(*operands)


def _token_contraction(name, out_rows, tk, in_arrays, contributions, dep=None):
    t = in_arrays[0].shape[0]

    def body(*refs):
        out_ref = refs[-1]

        @pl.when(pl.program_id(0) == 0)
        def _():
            out_ref[...] = jnp.zeros_like(out_ref)

        for row0, a, b in contributions(*refs[:-1]):
            out_ref[row0 : row0 + a.shape[1], :] += _dot(a, b, TN)

    in_specs = [_row_spec(tk, a.shape[1]) for a in in_arrays]
    body, in_specs, operands = _after(dep, body, in_specs, in_arrays)
    return pl.pallas_call(
        body,
        name=name,
        grid=(t // tk,),
        in_specs=in_specs,
        out_specs=_const_spec((out_rows, D_MODEL), single_buffer=True),
        out_shape=jax.ShapeDtypeStruct((out_rows, D_MODEL), F32),
        compiler_params=_params(("arbitrary",)),
    )(*operands)


def _grad_w_in_t(dh_main, dkv, x):
    def contributions(dm_ref, dkv_ref, x_ref):
        xb = x_ref[...].astype(BF16)
        return [(0, dm_ref[...], xb), (D_MAIN, dkv_ref[...].astype(BF16), xb)]

    return _token_contraction("grad_w_in", D_IN, TK, [dh_main, dkv, x], contributions)


def _grad_w_out(cat, dz1b, dep=None):
    def contributions(cat_ref, dz1_ref):
        return [(0, cat_ref[...], dz1_ref[...])]

    return _token_contraction("grad_w_out", D_MODEL, TK, [cat, dz1b], contributions, dep)


def _grad_w_ff1(x1b, dpre):
    def contributions(x1_ref, dpre_ref):
        x1 = x1_ref[...]
        return [(j * D_MODEL, x1, dpre_ref[:, j * D_MODEL : (j + 1) * D_MODEL]) for j in range(N_FF_BLOCKS)]

    return _token_contraction("grad_w_ff1", D_FF, TK_FF, [x1b, dpre], contributions)


def _grad_w_ff2(r, dz2b):
    def contributions(r_ref, dz2_ref):
        dz2 = dz2_ref[...]
        out = []
        for j in range(N_FF_BLOCKS):
            rf = r_ref[:, j * D_MODEL : (j + 1) * D_MODEL].astype(F32)
            out.append((j * D_MODEL, (rf * rf).astype(BF16), dz2))
        return out

    return _token_contraction("grad_w_ff2", D_FF, TK_FF, [r, dz2b], contributions)


ANY = pl.BlockSpec(memory_space=pl.ANY)


def _mesh_position():
    return lax.axis_index("x"), lax.axis_index("y"), lax.axis_index("c")


def _other_chips(x, y):
    return [(1 - x, y), (x, 1 - y), (1 - x, 1 - y)]


def _remote(src, dst, send_sem, recv_sem, device):
    return pltpu.make_async_remote_copy(src_ref=src, dst_ref=dst, send_sem=send_sem, recv_sem=recv_sem, device_id=device, device_id_type=MESH)


def _rows(ref, start, size):
    return ref.at[pl.ds(start, size), :]


def _all_gather_weights(shards):
    n = len(shards)
    per = 7

    def body(*refs):
        ins, outs = refs[:n], refs[n : 2 * n]
        send_sems, recv_sems = refs[2 * n :]
        x, y, c = _mesh_position()
        me = 2 * x + y
        chips = _other_chips(x, y)
        sibling = (x, y, 1 - c)
        started = []
        for w in range(n):
            rows = shards[w].shape[0]
            half = rows // 2
            for kk, (px, py) in enumerate(chips):
                cp = _remote(_rows(ins[w], c * half, half), _rows(outs[w], me * rows + c * half, half),
                             send_sems.at[per * w + kk], recv_sems.at[per * w + kk], (px, py, c))
                cp.start()
                started.append(cp)
            cp = _remote(ins[w], _rows(outs[w], me * rows, rows), send_sems.at[per * w + 6], recv_sems.at[per * w + 6], sibling)
            cp.start()
            started.append(cp)
        for w in range(n):
            rows = shards[w].shape[0]
            half = rows // 2
            for kk, (px, py) in enumerate(chips):
                blk = _rows(outs[w], (2 * px + py) * rows + c * half, half)
                _remote(blk, blk, send_sems.at[per * w + kk], recv_sems.at[per * w + kk], (px, py, c)).wait_recv()
                fwd = _remote(blk, blk, send_sems.at[per * w + 3 + kk], recv_sems.at[per * w + 3 + kk], sibling)
                fwd.start()
                started.append(fwd)
        for w in range(n):
            rows = shards[w].shape[0]
            half = rows // 2
            for kk, (px, py) in enumerate(chips):
                blk = _rows(outs[w], (2 * px + py) * rows + (1 - c) * half, half)
                _remote(blk, blk, send_sems.at[per * w + 3 + kk], recv_sems.at[per * w + 3 + kk], sibling).wait_recv()
            own = _rows(outs[w], me * rows, rows)
            _remote(own, own, send_sems.at[per * w + 6], recv_sems.at[per * w + 6], sibling).wait_recv()
        for cp in started:
            cp.wait_send()

    return pl.pallas_call(
        body,
        name="all_gather_weights",
        in_specs=[ANY] * n,
        out_specs=[ANY] * n,
        out_shape=[jax.ShapeDtypeStruct((N_CHIPS * s.shape[0], s.shape[1]), s.dtype) for s in shards],
        scratch_shapes=[pltpu.SemaphoreType.DMA((per * n,)), pltpu.SemaphoreType.DMA((per * n,))],
    )(*shards)


def _pair_swap(grads):
    n = len(grads)

    def body(*refs):
        ins, theirs = refs[:n], refs[n : 2 * n]
        send_sems, recv_sems = refs[2 * n :]
        x, y, c = _mesh_position()
        sibling = (x, y, 1 - c)
        sends = []
        for w in range(n):
            rows = grads[w].shape[0] // N_CHIPS
            half = rows // 2
            for j in range(N_CHIPS):
                cp = _remote(_rows(ins[w], j * rows + (1 - c) * half, half), _rows(theirs[w], j * half, half),
                             send_sems.at[4 * w + j], recv_sems.at[4 * w + j], sibling)
                cp.start()
                sends.append(cp)
        for cp in sends:
            cp.wait_recv()
        for cp in sends:
            cp.wait_send()

    return pl.pallas_call(
        body,
        name="grad_pair_swap",
        in_specs=[ANY] * n,
        out_specs=[ANY] * n,
        out_shape=[jax.ShapeDtypeStruct((g.shape[0] // 2, g.shape[1]), g.dtype) for g in grads],
        scratch_shapes=[pltpu.SemaphoreType.DMA((4 * n,)), pltpu.SemaphoreType.DMA((4 * n,))],
    )(*grads)


def _chip_exchange(partials):
    n = len(partials)

    def body(*refs):
        ins, outs = refs[:n], refs[n : 2 * n]
        send_sems, recv_sems = refs[2 * n :]
        x, y, c = _mesh_position()
        chips = _other_chips(x, y)
        sends = []
        for w in range(n):
            half = partials[w].shape[0] // N_CHIPS
            for kk, (px, py) in enumerate(chips):
                cp = _remote(_rows(ins[w], (2 * px + py) * half, half), _rows(outs[w], kk * half, half),
                             send_sems.at[3 * w + kk], recv_sems.at[3 * w + kk], (px, py, c))
                cp.start()
                sends.append(cp)
        for cp in sends:
            cp.wait_recv()
        for cp in sends:
            cp.wait_send()

    return pl.pallas_call(
        body,
        name="grad_chip_exchange",
        in_specs=[ANY] * n,
        out_specs=[ANY] * n,
        out_shape=[jax.ShapeDtypeStruct((3 * p.shape[0] // N_CHIPS, p.shape[1]), p.dtype) for p in partials],
        scratch_shapes=[pltpu.SemaphoreType.DMA((3 * n,)), pltpu.SemaphoreType.DMA((3 * n,))],
    )(*partials)


def _pair_gather(shards):
    n = len(shards)

    def body(*refs):
        outs = refs[n : 2 * n]
        send_sems, recv_sems = refs[2 * n :]
        x, y, c = _mesh_position()
        sibling = (x, y, 1 - c)
        sends = []
        for w in range(n):
            half = shards[w].shape[0] // 2
            mine = _rows(outs[w], c * half, half)
            cp = _remote(mine, mine, send_sems.at[w], recv_sems.at[w], sibling)
            cp.start()
            sends.append(cp)
        for w in range(n):
            half = shards[w].shape[0] // 2
            blk = _rows(outs[w], (1 - c) * half, half)
            _remote(blk, blk, send_sems.at[w], recv_sems.at[w], sibling).wait_recv()
        for cp in sends:
            cp.wait_send()

    return pl.pallas_call(
        body,
        name="grad_pair_gather",
        in_specs=[ANY] * n,
        out_specs=[ANY] * n,
        out_shape=[jax.ShapeDtypeStruct(s.shape, s.dtype) for s in shards],
        input_output_aliases={w: w for w in range(n)},
        scratch_shapes=[pltpu.SemaphoreType.DMA((n,)), pltpu.SemaphoreType.DMA((n,))],
    )(*shards)


def _all_reduce_small(slab, dep=None):
    rows = slab.shape[0]
    part = rows // 8

    def body(slab_ref, out_ref, landing, reduced, send_sems, recv_sems):
        x, y, c = _mesh_position()
        me = 4 * x + 2 * y + c
        flips = [(k >> 2, (k >> 1) & 1, k & 1) for k in range(1, 8)]

        def peer(flip):
            fx, fy, fc = flip
            return (1 - x if fx else x, 1 - y if fy else y, 1 - c if fc else c)

        def my_rows(ref):
            return ref.at[pl.ds(pl.multiple_of(me * part, 8), part), :]

        sends = []
        for kk, flip in enumerate(flips):
            px, py, pc = peer(flip)
            them = 4 * px + 2 * py + pc
            cp = _remote(slab_ref.at[pl.ds(pl.multiple_of(them * part, 8), part), :], landing.at[me], send_sems.at[kk], recv_sems.at[kk], (px, py, pc))
            cp.start()
            sends.append(cp)
        landing[me] = my_rows(slab_ref)[...]
        for kk, flip in enumerate(flips):
            px, py, pc = peer(flip)
            them = 4 * px + 2 * py + pc
            _remote(landing.at[them], landing.at[them], send_sems.at[kk], recv_sems.at[kk], (px, py, pc)).wait_recv()
        total = landing[0]
        for s in range(1, 8):
            total = total + landing[s]
        reduced[...] = total
        my_rows(out_ref)[...] = total
        for kk, flip in enumerate(flips):
            cp = _remote(reduced, my_rows(out_ref), send_sems.at[7 + kk], recv_sems.at[7 + kk], peer(flip))
            cp.start()
            sends.append(cp)
        for kk, flip in enumerate(flips):
            px, py, pc = peer(flip)
            them = 4 * px + 2 * py + pc
            blk = out_ref.at[pl.ds(pl.multiple_of(them * part, 8), part), :]
            _remote(blk, blk, send_sems.at[7 + kk], recv_sems.at[7 + kk], (px, py, pc)).wait_recv()
        for cp in sends:
            cp.wait_send()

    vmem = pl.BlockSpec(memory_space=pltpu.VMEM)
    body, in_specs, operands = _after(dep, body, [vmem], [slab])
    return pl.pallas_call(
        body,
        name="all_reduce_small",
        in_specs=in_specs,
        out_specs=vmem,
        out_shape=jax.ShapeDtypeStruct(slab.shape, slab.dtype),
        scratch_shapes=[pltpu.VMEM((8, part, LANES), F32), pltpu.VMEM((part, LANES), F32), pltpu.SemaphoreType.DMA((14,)), pltpu.SemaphoreType.DMA((14,))],
    )(*operands)


HBM = pl.BlockSpec(memory_space=pltpu.HBM)
SEM = pl.BlockSpec(memory_space=pltpu.SEMAPHORE)
DATAFLOW = pltpu.SideEffectType.DATAFLOW_SIDE_EFFECTING
TOKEN = jax.ShapeDtypeStruct((8, LANES), F32)


def _plan_copies(bufs, plan, send_sems, recv_sems):
    out = []
    for i, (src, src_row, dst, dst_row, recv_row, rows, device) in enumerate(plan):
        send = _remote(_rows(bufs[src], src_row, rows), _rows(bufs[dst], dst_row, rows), send_sems.at[i], recv_sems.at[i], device)
        landed = _rows(bufs[dst], recv_row, rows)
        recv = _remote(landed, landed, send_sems.at[i], recv_sems.at[i], device)
        out.append((send, recv))
    return out


def _split_call(name, bufs, wait=None, start=None, after=None):
    n = len(bufs)
    n_in = n + (2 if wait else 0) + (1 if after is not None else 0)
    n_start = len(start(0, 0, 0)) if start else 0

    def body(*refs):
        ins = refs[:n]
        x, y, c = _mesh_position()
        if wait:
            for send, recv in _plan_copies(ins, wait[0](x, y, c), refs[n], refs[n + 1]):
                recv.wait_recv()
                send.wait_send()
        if start:
            for send, _ in _plan_copies(ins, start(x, y, c), refs[n_in + n + 1], refs[n_in + n + 2]):
                send.start()
        token = refs[n_in + n]
        token[...] = jnp.zeros_like(token)

    operands = [pltpu.with_memory_space_constraint(b, pltpu.HBM) for b in bufs]
    in_specs = [HBM] * n
    if wait:
        operands += [wait[1], wait[2]]
        in_specs += [SEM, SEM]
    if after is not None:
        operands.append(after)
        in_specs.append(ANY)
    out_shape = [pltpu.HBM(b.shape, b.dtype) for b in bufs] + [TOKEN]
    out_specs = [HBM] * n + [pl.BlockSpec(memory_space=pltpu.VMEM)]
    if start:
        out_shape += [pltpu.SemaphoreType.DMA((n_start,)), pltpu.SemaphoreType.DMA((n_start,))]
        out_specs += [SEM, SEM]
    outs = pl.pallas_call(
        body,
        name=name,
        in_specs=in_specs,
        out_specs=out_specs,
        out_shape=out_shape,
        input_output_aliases={i: i for i in range(n)},
        compiler_params=pltpu.CompilerParams(has_side_effects=DATAFLOW),
    )(*operands)
    return (list(outs[:n]), outs[n]) + tuple(outs[n + 1 :])


def _gather_plans(shard_rows):
    n = len(shard_rows)

    def ici(x, y, c):
        me = 2 * x + y
        plan = []
        for w, rows in enumerate(shard_rows):
            half = rows // 2
            for px, py in _other_chips(x, y):
                plan.append((w, c * half, n + w, me * rows + c * half, (2 * px + py) * rows + c * half, half, (px, py, c)))
            plan.append((w, 0, n + w, me * rows, me * rows, rows, (x, y, 1 - c)))
        return plan

    def passed_on(x, y, c):
        plan = []
        for w, rows in enumerate(shard_rows):
            half = rows // 2
            for px, py in _other_chips(x, y):
                row = (2 * px + py) * rows
                plan.append((n + w, row + c * half, n + w, row + c * half, row + (1 - c) * half, half, (x, y, 1 - c)))
        return plan

    return ici, passed_on


def _swap_plan(block_rows):
    n = len(block_rows)

    def plan_fn(x, y, c):
        plan = []
        for w, rows in enumerate(block_rows):
            half = rows // 2
            for j in range(N_CHIPS):
                plan.append((w, j * rows + (1 - c) * half, n + w, j * half, j * half, half, (x, y, 1 - c)))
        return plan

    return plan_fn


def _exchange_plan(halves):
    n = len(halves)

    def plan_fn(x, y, c):
        plan = []
        for w, half in enumerate(halves):
            for kk, (px, py) in enumerate(_other_chips(x, y)):
                plan.append((w, (2 * px + py) * half, n + w, kk * half, kk * half, half, (px, py, c)))
        return plan

    return plan_fn


def _pair_gather_plan(shard_rows):
    def plan_fn(x, y, c):
        return [(w, c * (rows // 2), w, c * (rows // 2), (1 - c) * (rows // 2), rows // 2, (x, y, 1 - c)) for w, rows in enumerate(shard_rows)]

    return plan_fn


def _landing(rows, cols, dtype):
    return lax.empty((rows, cols), dtype)


def _row_tile(rows, cap=512):
    best = 8
    for cand in range(8, cap + 1, 8):
        if rows % cand == 0:
            best = cand
    return best


def _pair_sum(name, grad, theirs, pos):
    half = theirs.shape[0] // N_CHIPS
    cols = theirs.shape[1]
    tile = _row_tile(half)
    steps = half // tile

    def body(pos_ref, g_ref, t_ref, p_ref, own_ref):
        total = g_ref[...] + t_ref[...]
        p_ref[...] = total.astype(BF16)

        @pl.when(pl.program_id(1) == pos_ref[1])
        def _():
            own_ref[...] = total

    return pl.pallas_call(
        body,
        name=name,
        grid_spec=pltpu.PrefetchScalarGridSpec(
            num_scalar_prefetch=1,
            grid=(steps, N_CHIPS),
            in_specs=[
                pl.BlockSpec((tile, cols), lambda i, j, pos: ((2 * j + pos[0]) * steps + i, 0)),
                pl.BlockSpec((tile, cols), lambda i, j, pos: (j * steps + i, 0)),
            ],
            out_specs=[
                pl.BlockSpec((tile, cols), lambda i, j, pos: (j * steps + i, 0)),
                pl.BlockSpec((tile, cols), lambda i, j, pos: (i, 0)),
            ],
        ),
        out_shape=[jax.ShapeDtypeStruct((N_CHIPS * half, cols), BF16), jax.ShapeDtypeStruct((half, cols), F32)],
        compiler_params=_params(("parallel", "arbitrary")),
    )(pos, grad, theirs)


def _chip_sum(name, own, landed, pos):
    half, cols = own.shape
    tile = _row_tile(half)
    steps = half // tile

    def body(pos_ref, own_ref, l0, l1, l2, o_ref):
        o_ref[...] = ((own_ref[...] + l0[...].astype(F32)) + l1[...].astype(F32)) + l2[...].astype(F32)

    landed_specs = [pl.BlockSpec((tile, cols), lambda i, pos, _k=k: (_k * steps + i, 0)) for k in range(N_CHIPS - 1)]
    return pl.pallas_call(
        body,
        name=name,
        grid_spec=pltpu.PrefetchScalarGridSpec(
            num_scalar_prefetch=1,
            grid=(steps,),
            in_specs=[pl.BlockSpec((tile, cols), lambda i, pos: (i, 0))] + landed_specs,
            out_specs=pl.BlockSpec((tile, cols), lambda i, pos: (pos[0] * steps + i, 0)),
        ),
        out_shape=jax.ShapeDtypeStruct((2 * half, cols), F32),
        compiler_params=_params(("parallel",)),
    )(pos, own, landed, landed, landed)


def _adamw(name, w, g, m, v):
    rows, cols = w.shape
    tile = rows if rows * cols <= 256 * 1024 else _row_tile(rows)

    def body(w_ref, g_ref, m_ref, v_ref, g_out_ref, d_ref, nm_ref, nv_ref):
        g = g_ref[...]
        g_out_ref[...] = g
        nm = ADAM_B1 * m_ref[...] + (1.0 - ADAM_B1) * g
        nv = ADAM_B2 * v_ref[...] + (1.0 - ADAM_B2) * (g * g)
        m_hat = nm / (1.0 - ADAM_B1**ADAM_STEP)
        v_hat = nv / (1.0 - ADAM_B2**ADAM_STEP)
        d_ref[...] = -ADAM_LR * (m_hat / (jnp.sqrt(v_hat) + ADAM_EPS) + ADAM_WD * w_ref[...])
        nm_ref[...] = nm
        nv_ref[...] = nv

    spec = _row_spec(tile, cols)
    return pl.pallas_call(
        body,
        name=name,
        grid=(rows // tile,),
        in_specs=[spec] * 4,
        out_specs=[spec] * 4,
        out_shape=[jax.ShapeDtypeStruct((rows, cols), F32)] * 4,
        compiler_params=_params(("parallel",)),
    )(w, g, m, v)


_SMALL = (
    ("v_ln_g", (D_GMLP,), 8),
    ("v_ln_b", (D_GMLP,), 8),
    ("w_spatial", (N_HEADS, CHUNK, CHUNK), 1024),
    ("b_spatial", (N_HEADS, CHUNK), 8),
    ("sinks", (N_HEADS,), 8),
    ("ln1_g", (D_MODEL,), 8),
    ("ln1_b", (D_MODEL,), 8),
    ("ln2_g", (D_MODEL,), 8),
    ("ln2_b", (D_MODEL,), 8),
    ("squared_error", (D_MODEL,), 8),
)
N_SMALL_PARAMS = len(_SMALL) - 1


def _pack_small(values):
    parts = []
    for (name, shape, rows), val in zip(_SMALL, values, strict=True):
        flat = val.reshape(-1).astype(F32)
        parts.append(jnp.pad(flat, (0, rows * LANES - flat.shape[0])).reshape(rows, LANES))
    parts.append(jnp.zeros((SMALL_ROWS - sum(rows for _, _, rows in _SMALL), LANES), F32))
    return jnp.concatenate(parts, axis=0)


def _adamw_update(w, g, m, v):
    nm = ADAM_B1 * m + (1.0 - ADAM_B1) * g
    nv = ADAM_B2 * v + (1.0 - ADAM_B2) * (g * g)
    m_hat = nm / (1.0 - ADAM_B1**ADAM_STEP)
    v_hat = nv / (1.0 - ADAM_B2**ADAM_STEP)
    return -ADAM_LR * (m_hat / (jnp.sqrt(v_hat) + ADAM_EPS) + ADAM_WD * w), nm, nv


def _adamw_small(g_slab, params, first, second):
    n = N_SMALL_PARAMS

    def pieces(shape):
        if len(shape) == 3:
            return [((0, h), h * shape[1], shape[1], shape[2]) for h in range(shape[0])]
        if len(shape) == 2:
            return [((0,), 0, shape[0], shape[1])]
        if shape[0] >= LANES:
            return [((slice(None), slice(r * LANES, (r + 1) * LANES)), r, 1, LANES) for r in range(shape[0] // LANES)]
        return [((slice(None), slice(0, shape[0])), 0, 1, shape[0])]

    def body(*refs):
        g_ref = refs[0]
        w_refs, m_refs, v_refs = refs[1 : 1 + n], refs[1 + n : 1 + 2 * n], refs[1 + 2 * n : 1 + 3 * n]
        outs = refs[1 + 3 * n :]
        row0 = 0
        for idx, (_, shape, rows) in enumerate(_SMALL[:n]):
            for where, first_row, n_rows, lanes in pieces(shape):
                g = g_ref[row0 + first_row : row0 + first_row + n_rows, 0:lanes]
                delta, nm, nv = _adamw_update(w_refs[idx][where], g, m_refs[idx][where], v_refs[idx][where])
                for group, val in enumerate((g, delta, nm, nv)):
                    outs[group * n + idx][where] = val
            row0 += rows

    vmem = pl.BlockSpec(memory_space=pltpu.VMEM)
    shapes = [jax.ShapeDtypeStruct(p.shape, F32) for p in params]
    outs = pl.pallas_call(
        body,
        name="adamw_small",
        in_specs=[vmem] * (1 + 3 * n),
        out_specs=[vmem] * (4 * n),
        out_shape=shapes * 4,
        compiler_params=_params(),
    )(g_slab, *params, *first, *second)
    return [list(outs[group * n : (group + 1) * n]) for group in range(4)]


def kernel(x, positions, w_in, v_ln_g, v_ln_b, w_spatial, b_spatial, sinks, w_out, ln1_g, ln1_b, w_ff1, w_ff2, ln2_g, ln2_b, loss_target, m_w_in, m_v_ln_g, m_v_ln_b, m_w_spatial, m_b_spatial, m_sinks, m_w_out, m_ln1_g, m_ln1_b, m_w_ff1, m_w_ff2, m_ln2_g, m_ln2_b, v_w_in, v_v_ln_g, v_v_ln_b, v_w_spatial, v_b_spatial, v_sinks, v_w_out, v_ln1_g, v_ln1_b, v_w_ff1, v_w_ff2, v_ln2_g, v_ln2_b):
    t = x.shape[1]
    x2 = x.reshape(t, D_MODEL)
    target = loss_target.reshape(t, D_MODEL)

    (w_in_t,) = _all_gather_weights([w_in[0].T.astype(BF16)])
    later = [w_out[0].astype(BF16), w_ff1[0].astype(BF16), w_ff2[0].astype(BF16)]
    later_rows = [s.shape[0] for s in later]
    ici_plan, pass_plan = _gather_plans(later_rows)
    bufs, started, ici_send, ici_recv = _split_call(
        "gather_start", later + [_landing(N_CHIPS * r, D_MODEL, BF16) for r in later_rows], start=ici_plan, after=w_in_t)

    inv_freq = ROPE_THETA ** (-jnp.arange(0, HEAD_DIM, 2, dtype=F32) / HEAD_DIM)
    cos, sin = _rope_tables(positions, jnp.tile(inv_freq, LANES // (HEAD_DIM // 2)).reshape(1, LANES))
    u, vg, q, k, va = _in_proj(x2, w_in_t, cos, sin, dep=started)
    bias_full = jnp.repeat(b_spatial[0].T, HEAD_DIM, axis=1)
    sink_vec = sinks.reshape(N_HEADS)
    cat = _mixer_fwd(u, vg, q, k, va, v_ln_g, v_ln_b, w_spatial[0], bias_full, sink_vec)
    bufs, passed, pass_send, pass_recv = _split_call("gather_pass", bufs, wait=(ici_plan, ici_send, ici_recv), start=pass_plan, after=cat)
    bufs, _ = _split_call("gather_end", bufs, wait=(pass_plan, pass_send, pass_recv), after=passed)
    w_out_all = bufs[3]
    w1_all = bufs[4].reshape(N_FF_BLOCKS, D_MODEL, D_MODEL)
    w2_all = bufs[5].reshape(N_FF_BLOCKS, D_MODEL, D_MODEL)
    xhat1, rstd1, x1b, r, dz2, dz2b, d_ln2_g, d_ln2_b, sq_err = _ffn_fwd_loss(
        cat, x2, w_out_all, ln1_g, ln1_b, w1_all, w2_all, ln2_g, ln2_b, target)

    pos = jnp.stack([lax.axis_index("c"), 2 * lax.axis_index("x") + lax.axis_index("y")]).astype(jnp.int32)
    half_landing = lambda g: _landing(g.shape[0] // 2, D_MODEL, F32)
    g_ff2_local = _grad_w_ff2(r, dz2b)
    swap_plan = _swap_plan([D_FF // N_CHIPS])
    ff2_bufs, swapping2, swap2_send, swap2_recv = _split_call("ff2_swap_start", [g_ff2_local, half_landing(g_ff2_local)], start=swap_plan)
    dpre, dz1, dz1b, dcat, d_ln1_g, d_ln1_b = _ffn_bwd_ln1(dz2, r, xhat1, rstd1, ln1_g, w1_all, w2_all, w_out_all, dep=swapping2)
    g_ff1_local = _grad_w_ff1(x1b, dpre)
    ff1_bufs, swapping1, swap1_send, swap1_recv = _split_call("ff1_swap_start", [g_ff1_local, half_landing(g_ff1_local)], start=swap_plan)
    g_out_local = _grad_w_out(cat, dz1b, dep=swapping1)
    ff2_bufs, swapped2 = _split_call("ff2_swap_wait", ff2_bufs, wait=(swap_plan, swap2_send, swap2_recv), after=g_out_local)
    ff1_bufs, _ = _split_call("ff1_swap_wait", ff1_bufs, wait=(swap_plan, swap1_send, swap1_recv), after=swapped2)
    ff_sums = [_pair_sum("grad_pair_sum_w_ff1", ff1_bufs[0], ff1_bufs[1], pos), _pair_sum("grad_pair_sum_w_ff2", ff2_bufs[0], ff2_bufs[1], pos)]
    ff_halves = [p.shape[0] // N_CHIPS for p, _ in ff_sums]
    exchange_plan = _exchange_plan(ff_halves)
    bufs, exchanging, ex_send, ex_recv = _split_call(
        "ff_exchange_start", [p for p, _ in ff_sums] + [_landing(3 * h, D_MODEL, BF16) for h in ff_halves], start=exchange_plan)
    dh_main, dkv, d_v_ln_g, d_v_ln_b, d_w_spatial, d_b_spatial_t, d_sinks = _mixer_bwd(
        u, vg, q, k, va, dcat, cos, sin, v_ln_g, v_ln_b, w_spatial[0], bias_full, sink_vec, dep=exchanging)
    g_in_local = _grad_w_in_t(dh_main, dkv, x2)

    small = [g_in_local, g_out_local]
    small_swap_plan = _swap_plan([g.shape[0] // N_CHIPS for g in small])
    swap_bufs, small_swapping, ss_send, ss_recv = _split_call(
        "small_swap_start", small + [half_landing(g) for g in small], start=small_swap_plan)
    grad_x_flat = _grad_x(dh_main, dkv, dz1, w_in_t, dep=small_swapping)
    grad_x = grad_x_flat.reshape(1, t, D_MODEL)
    swap_bufs, _ = _split_call("small_swap_wait", swap_bufs, wait=(small_swap_plan, ss_send, ss_recv), after=grad_x_flat)
    pair_sums = [_pair_sum("grad_pair_sum_" + nm, g, th, pos) for nm, g, th in zip(["w_in", "w_out"], swap_bufs[:2], swap_bufs[2:])]
    small_halves = [p.shape[0] // N_CHIPS for p, _ in pair_sums]
    small_plan = _exchange_plan(small_halves)
    small_bufs, small_exchanging, sm_send, sm_recv = _split_call(
        "small_exchange_start", [p for p, _ in pair_sums] + [_landing(3 * h, D_MODEL, BF16) for h in small_halves], start=small_plan)

    bufs, _ = _split_call("ff_exchange_wait", bufs, wait=(exchange_plan, ex_send, ex_recv), after=small_exchanging)
    ff_shards = [_chip_sum("grad_chip_sum_" + nm, own, ld, pos) for nm, (_, own), ld in zip(["w_ff1", "w_ff2"], ff_sums, bufs[2:])]
    ff_gather_plan = _pair_gather_plan([s.shape[0] for s in ff_shards])
    ff_shards, ff_gathering, fg_send, fg_recv = _split_call("ff_gather_start", ff_shards, start=ff_gather_plan)
    small_g = _all_reduce_small(_pack_small(
        [d_v_ln_g, d_v_ln_b, d_w_spatial, d_b_spatial_t[:, :N_HEADS].T, d_sinks[0, :N_HEADS], d_ln1_g, d_ln1_b, d_ln2_g, d_ln2_b, sq_err]),
        dep=ff_gathering)
    sq_row = sum(rows for _, _, rows in _SMALL[:N_SMALL_PARAMS])
    loss = 0.5 * jnp.sum(small_g[sq_row : sq_row + _SMALL[N_SMALL_PARAMS][2]]) / D_MODEL
    (g_w_ff1, g_w_ff2), _ = _split_call("ff_gather_wait", ff_shards, wait=(ff_gather_plan, fg_send, fg_recv), after=small_g)

    g_w_ff1, d_w_ff1, nm_w_ff1, nv_w_ff1 = _adamw("adamw_w_ff1", w_ff1[0], g_w_ff1, m_w_ff1[0], v_w_ff1[0])
    g_w_ff2, d_w_ff2, nm_w_ff2, nv_w_ff2 = _adamw("adamw_w_ff2", w_ff2[0], g_w_ff2, m_w_ff2[0], v_w_ff2[0])
    small_bufs, _ = _split_call("small_exchange_wait", small_bufs, wait=(small_plan, sm_send, sm_recv), after=nv_w_ff2)
    shards = [_chip_sum("grad_chip_sum_" + nm, own, ld, pos) for nm, (_, own), ld in zip(["w_in", "w_out"], pair_sums, small_bufs[2:])]
    g_w_in_t, g_w_out = _pair_gather(shards)
    g_w_in, d_w_in, nm_w_in, nv_w_in = (a.T for a in _adamw("adamw_w_in", w_in[0].T, g_w_in_t, m_w_in[0].T, v_w_in[0].T))
    g_w_out, d_w_out, nm_w_out, nv_w_out = _adamw("adamw_w_out", w_out[0], g_w_out, m_w_out[0], v_w_out[0])
    small_grads, small_d, small_nm, small_nv = _adamw_small(
        small_g,
        [v_ln_g, v_ln_b, w_spatial, b_spatial, sinks, ln1_g, ln1_b, ln2_g, ln2_b],
        [m_v_ln_g, m_v_ln_b, m_w_spatial, m_b_spatial, m_sinks, m_ln1_g, m_ln1_b, m_ln2_g, m_ln2_b],
        [v_v_ln_g, v_v_ln_b, v_w_spatial, v_b_spatial, v_sinks, v_ln1_g, v_ln1_b, v_ln2_g, v_ln2_b])

    def with_big(small, w_in_v, w_out_v, w_ff1_v, w_ff2_v):
        g_vg, g_vb, g_ws, g_bs, g_sk, g_1g, g_1b, g_2g, g_2b = small
        return [w_in_v[None], g_vg, g_vb, g_ws, g_bs, g_sk, w_out_v[None], g_1g, g_1b, w_ff1_v[None], w_ff2_v[None], g_2g, g_2b]

    return (
        loss,
        grad_x,
        *with_big(small_grads, g_w_in, g_w_out, g_w_ff1, g_w_ff2),
        *with_big(small_d, d_w_in, d_w_out, d_w_ff1, d_w_ff2),
        *with_big(small_nm, nm_w_in, nm_w_out, nm_w_ff1, nm_w_ff2),
        *with_big(small_nv, nv_w_in, nv_w_out, nv_w_ff1, nv_w_ff2),
    )
```

```python
import math

import jax
import jax.numpy as jnp
from jax import lax
from jax.experimental import pallas as pl
from jax.experimental.pallas import tpu as pltpu

F32 = jnp.float32
BF16 = jnp.bfloat16

D_MODEL = 1024
HEAD_DIM = 64
D_GMLP = 512
D_ATTN = 512
D_KV = 128
D_IN = 2 * D_GMLP + D_ATTN + 2 * D_KV
D_MAIN = 2 * D_GMLP + D_ATTN
N_HEADS = 8
CHUNK = 128
ROPE_THETA = 10000.0
D_FF = 4 * D_MODEL
N_FF_BLOCKS = 4
LN_EPS = 1e-5
ALPHA = (2.0 * 1) ** 0.25
NEG_INF = -1e30
SCALE = 1.0 / math.sqrt(HEAD_DIM)

ADAM_LR = 0.001
ADAM_B1 = 0.9
ADAM_B2 = 0.999
ADAM_EPS = 1e-08
ADAM_WD = 0.01
ADAM_STEP = 10

N_CHIPS = 4
LANES = 128
V7X_VMEM_BYTES = 64 * 1024 * 1024
VMEM_LIMIT = V7X_VMEM_BYTES - 8 * 1024 * 1024
TM = 512
TM_FFN = 256
TK = 1024
TK_FF = 1024
SMALL_ROWS = 1152
MESH = pl.DeviceIdType.MESH

NT = (((1,), (1,)), ((), ()))
TN = (((0,), (0,)), ((), ()))


def _dot(a, b, dims=None):
    if dims is None:
        return jnp.dot(a, b, preferred_element_type=F32)
    return lax.dot_general(a, b, dims, preferred_element_type=F32)


def _params(semantics=None):
    return pltpu.CompilerParams(dimension_semantics=semantics, vmem_limit_bytes=VMEM_LIMIT)


def _const_spec(shape, single_buffer=False):
    zeros = (0,) * len(shape)
    if single_buffer:
        return pl.BlockSpec(shape, lambda *_: zeros, pipeline_mode=pl.Buffered(1))
    return pl.BlockSpec(shape, lambda *_: zeros)


def _row_spec(rows, cols):
    return pl.BlockSpec((rows, cols), lambda i: (i, 0))


def _after(dep, body, in_specs, operands):
    if dep is None:
        return body, list(in_specs), list(operands)
    return (lambda dep_ref, *refs: body(*refs)), [pl.BlockSpec(memory_space=pl.ANY)] + list(in_specs), [dep] + list(operands)


def _gelu(x):
    k = math.sqrt(2.0 / math.pi)
    return 0.5 * x * (1.0 + jnp.tanh(k * (x + 0.044715 * (x * x * x))))


def _gelu_and_grad(x):
    k = math.sqrt(2.0 / math.pi)
    x2 = x * x
    t = jnp.tanh(k * (x + 0.044715 * (x2 * x)))
    g = 0.5 * x * (1.0 + t)
    dg = 0.5 * (1.0 + t) + 0.5 * x * (1.0 - t * t) * (k * (1.0 + 3.0 * 0.044715 * x2))
    return g, dg


def _layer_norm_stats(z):
    mu = jnp.mean(z, axis=-1, keepdims=True)
    zc = z - mu
    var = jnp.mean(zc * zc, axis=-1, keepdims=True)
    rstd = lax.rsqrt(var + LN_EPS)
    return zc * rstd, rstd


def _layer_norm_bwd(dxhat, xhat, rstd):
    m1 = jnp.mean(dxhat, axis=-1, keepdims=True)
    m2 = jnp.mean(dxhat * xhat, axis=-1, keepdims=True)
    return rstd * (dxhat - m1 - xhat * m2)


def _rotate_half(t):
    n = t.shape[1]
    lane = lax.broadcasted_iota(jnp.int32, t.shape, 1)
    first = (lane & (HEAD_DIM // 2)) == 0
    return jnp.where(first, -pltpu.roll(t, n - HEAD_DIM // 2, 1), pltpu.roll(t, HEAD_DIM // 2, 1))


def _rope(t, cos, sin):
    return t * cos + _rotate_half(t) * sin


def _rope_transposed(g, cos, sin):
    return g * cos - _rotate_half(g * sin)


def _lane_tile(a, reps):
    return jnp.tile(a, (1, reps)) if reps > 1 else a


def _rope_tables(pos_row, inv_freq_row):
    t = pos_row.shape[1]

    def body(pos_ref, f_ref, cos_ref, sin_ref):
        pos_rows = jnp.broadcast_to(pos_ref[...].astype(F32), (LANES, TM)).T
        ang = pos_rows * f_ref[...]
        cos_ref[...] = jnp.cos(ang)
        sin_ref[...] = jnp.sin(ang)

    return pl.pallas_call(
        body,
        name="rope_tables",
        grid=(t // TM,),
        in_specs=[pl.BlockSpec((1, TM), lambda i: (0, i)), _const_spec((1, LANES))],
        out_specs=[_row_spec(TM, LANES), _row_spec(TM, LANES)],
        out_shape=[jax.ShapeDtypeStruct((t, LANES), F32)] * 2,
        compiler_params=_params(("parallel",)),
    )(pos_row, inv_freq_row)


def _in_proj(x, w_in_t, cos, sin, dep=None):
    t = x.shape[0]

    def body(x_ref, w_ref, cos_ref, sin_ref, u_ref, vg_ref, q_ref, k_ref, va_ref):
        xb = x_ref[...].astype(BF16)
        u_ref[...] = _dot(xb, w_ref[0:D_GMLP, :], NT)
        vg_ref[...] = _dot(xb, w_ref[D_GMLP : 2 * D_GMLP, :], NT)
        q = _dot(xb, w_ref[2 * D_GMLP : D_MAIN, :], NT)
        k = _dot(xb, w_ref[D_MAIN : D_MAIN + D_KV, :], NT)
        va_ref[...] = _dot(xb, w_ref[D_MAIN + D_KV : D_IN, :], NT).astype(BF16)
        c, s = cos_ref[...], sin_ref[...]
        q_ref[...] = _rope(q, _lane_tile(c, D_ATTN // LANES), _lane_tile(s, D_ATTN // LANES)).astype(BF16)
        k_ref[...] = _rope(k, c, s).astype(BF16)

    body, in_specs, operands = _after(
        dep, body, [_row_spec(TM, D_MODEL), _const_spec((D_IN, D_MODEL)), _row_spec(TM, LANES), _row_spec(TM, LANES)], [x, w_in_t, cos, sin])
    return pl.pallas_call(
        body,
        name="in_proj",
        grid=(t // TM,),
        in_specs=in_specs,
        out_specs=[_row_spec(TM, D_GMLP), _row_spec(TM, D_GMLP), _row_spec(TM, D_ATTN), _row_spec(TM, D_KV), _row_spec(TM, D_KV)],
        out_shape=[
            jax.ShapeDtypeStruct((t, D_GMLP), F32),
            jax.ShapeDtypeStruct((t, D_GMLP), F32),
            jax.ShapeDtypeStruct((t, D_ATTN), BF16),
            jax.ShapeDtypeStruct((t, D_KV), BF16),
            jax.ShapeDtypeStruct((t, D_KV), BF16),
        ],
        compiler_params=_params(("parallel",)),
    )(*operands)


def _chunk_specs():
    cur = lambda i: (i, 0)
    prev = lambda i: (jnp.maximum(i - 1, 0), 0)
    return [
        pl.BlockSpec((CHUNK, D_GMLP), cur),
        pl.BlockSpec((CHUNK, D_GMLP), cur),
        pl.BlockSpec((CHUNK, D_ATTN), cur),
        pl.BlockSpec((CHUNK, D_KV), cur),
        pl.BlockSpec((CHUNK, D_KV), prev),
        pl.BlockSpec((CHUNK, D_KV), cur),
        pl.BlockSpec((CHUNK, D_KV), prev),
    ]


def _half_lane_masks(rows):
    lane = lax.broadcasted_iota(jnp.int32, (rows, LANES), 1)
    return lane < HEAD_DIM


def _kv_variants(kv2):
    left = _half_lane_masks(kv2.shape[0])
    f = kv2.astype(F32)
    swapped = pltpu.roll(f, HEAD_DIM, 1)
    zero = jnp.zeros_like(f)
    g0 = (jnp.where(left, f, zero).astype(BF16), jnp.where(left, zero, swapped).astype(BF16))
    g1 = (jnp.where(left, swapped, zero).astype(BF16), jnp.where(left, zero, f).astype(BF16))
    return (g0, g1)


def _band_mask(i, heads=1):
    row = lax.broadcasted_iota(jnp.int32, (heads * CHUNK, 2 * CHUNK), 0) & (CHUNK - 1)
    col = lax.broadcasted_iota(jnp.int32, (heads * CHUNK, 2 * CHUNK), 1)
    no_prev = jnp.where(i > 0, 0, 4 * CHUNK)
    in_prev = jnp.logical_and(col < CHUNK, (col - row) > no_prev)
    in_cur = jnp.logical_and(col >= CHUNK, (col - CHUNK) <= row)
    return jnp.logical_or(in_prev, in_cur)


def _causal_mask():
    row = lax.broadcasted_iota(jnp.int32, (CHUNK, CHUNK), 0)
    col = lax.broadcasted_iota(jnp.int32, (CHUNK, CHUNK), 1)
    return col <= row


def _store_spatial_weights(w_ref, wcat_ref, wcat_t_ref=None):
    causal = _causal_mask()
    for p in range(D_GMLP // LANES):
        wl = jnp.where(causal, w_ref[2 * p], 0.0)
        wr = jnp.where(causal, w_ref[2 * p + 1], 0.0)
        wcat_ref[p] = jnp.concatenate([wl, wr], axis=1).astype(BF16)
        if wcat_t_ref is not None:
            wcat_t_ref[p] = jnp.concatenate([wl.T, wr.T], axis=1).astype(BF16)


def _pair_stack(xp, left):
    return jnp.concatenate([jnp.where(left, xp, 0.0), jnp.where(left, 0.0, xp)], axis=0).astype(BF16)


def _mixer_fwd(u, vg, q, k, va, v_ln_g, v_ln_b, w_spatial, bias_full, sinks):
    t = u.shape[0]

    def body(u_ref, vg_ref, q_ref, kc_ref, kp_ref, vc_ref, vp_ref, g_ref, b_ref, w_ref, bias_ref, sink_ref, cat_ref, wcat):
        i = pl.program_id(0)
        left = _half_lane_masks(CHUNK)

        @pl.when(i == 0)
        def _():
            _store_spatial_weights(w_ref, wcat)

        ug = _gelu(u_ref[...])
        xhat, _ = _layer_norm_stats(_gelu(vg_ref[...]))
        vgl = xhat * g_ref[...] + b_ref[...]
        for p in range(D_GMLP // LANES):
            cols = slice(p * LANES, (p + 1) * LANES)
            mixed = _dot(wcat[p], _pair_stack(vgl[:, cols], left))
            cat_ref[:, cols] = (ug[:, cols] * (mixed + bias_ref[:, cols])).astype(BF16)

        k_var = _kv_variants(jnp.concatenate([kp_ref[...], kc_ref[...]], axis=0))
        v_var = _kv_variants(jnp.concatenate([vp_ref[...], vc_ref[...]], axis=0))
        valid = _band_mask(i)
        scores = [_dot(q_ref[:, (h // 2) * LANES : (h // 2 + 1) * LANES], k_var[h // 4][h % 2], NT) for h in range(N_HEADS)]
        heads = range(N_HEADS)
        sinks_h = [sink_ref[h] for h in heads]
        masked = [jnp.where(valid, scores[h] * SCALE, NEG_INF) for h in heads]
        maxes = [jnp.maximum(jnp.max(masked[h], axis=1, keepdims=True), sinks_h[h]) for h in heads]
        exps = [jnp.exp(masked[h] - maxes[h]) for h in heads]
        invs = [1.0 / (jnp.sum(exps[h], axis=1, keepdims=True) + jnp.exp(sinks_h[h] - maxes[h])) for h in heads]
        probs = [(exps[h] * invs[h]).astype(BF16) for h in heads]
        for p in range(D_ATTN // LANES):
            out = _dot(probs[2 * p], v_var[p // 2][0]) + _dot(probs[2 * p + 1], v_var[p // 2][1])
            cat_ref[:, D_GMLP + p * LANES : D_GMLP + (p + 1) * LANES] = out.astype(BF16)

    return pl.pallas_call(
        body,
        name="mixer_fwd",
        grid=(t // CHUNK,),
        in_specs=_chunk_specs()
        + [
            _const_spec((1, D_GMLP)),
            _const_spec((1, D_GMLP)),
            _const_spec((N_HEADS, CHUNK, CHUNK)),
            _const_spec((CHUNK, D_GMLP)),
            pl.BlockSpec(memory_space=pltpu.SMEM),
        ],
        out_specs=pl.BlockSpec((CHUNK, D_MODEL), lambda i: (i, 0)),
        out_shape=jax.ShapeDtypeStruct((t, D_MODEL), BF16),
        scratch_shapes=[pltpu.VMEM((D_GMLP // LANES, CHUNK, 2 * CHUNK), BF16)],
        compiler_params=_params(("arbitrary",)),
    )(u, vg, q, k, k, va, va, v_ln_g, v_ln_b, w_spatial, bias_full, sinks)


def _ffn_fwd_loss(cat, x, w_out, ln1_g, ln1_b, w1, w2, ln2_g, ln2_b, target):
    t = x.shape[0]

    def body(cat_ref, x_ref, wo_ref, g1_ref, b1_ref, w1_ref, w2_ref, g2_ref, b2_ref, tgt_ref,
             xh_ref, rstd_ref, x1b_ref, r_ref, dz2_ref, dz2b_ref, dg2_ref, db2_ref, sq_ref):
        @pl.when(pl.program_id(0) == 0)
        def _():
            dg2_ref[...] = jnp.zeros_like(dg2_ref)
            db2_ref[...] = jnp.zeros_like(db2_ref)
            sq_ref[...] = jnp.zeros_like(sq_ref)

        xhat1, rstd1 = _layer_norm_stats(ALPHA * x_ref[...] + _dot(cat_ref[...], wo_ref[...]))
        xh_ref[...] = xhat1
        rstd_ref[...] = rstd1
        x1 = xhat1 * g1_ref[...] + b1_ref[...]
        x1b = x1.astype(BF16)
        x1b_ref[...] = x1b
        ff = jnp.zeros((TM_FFN, D_MODEL), F32)
        for j in range(N_FF_BLOCKS):
            r = jnp.maximum(_dot(x1b, w1_ref[j]), 0.0)
            r_ref[:, j * D_MODEL : (j + 1) * D_MODEL] = r.astype(BF16)
            ff = ff + _dot((r * r).astype(BF16), w2_ref[j])
        xhat2, rstd2 = _layer_norm_stats(ALPHA * x1 + ff)
        err = xhat2 * g2_ref[...] + b2_ref[...] - tgt_ref[...]
        sq_ref[...] += jnp.sum(err * err, axis=0, keepdims=True)
        dy = err * (1.0 / D_MODEL)
        dg2_ref[...] += jnp.sum(dy * xhat2, axis=0, keepdims=True)
        db2_ref[...] += jnp.sum(dy, axis=0, keepdims=True)
        dz2 = _layer_norm_bwd(dy * g2_ref[...], xhat2, rstd2)
        dz2_ref[...] = dz2
        dz2b_ref[...] = dz2.astype(BF16)

    vec = _const_spec((1, D_MODEL))
    tile = _row_spec(TM_FFN, D_MODEL)
    wspec = _const_spec((N_FF_BLOCKS, D_MODEL, D_MODEL), single_buffer=True)
    return pl.pallas_call(
        body,
        name="ffn_fwd_loss",
        grid=(t // TM_FFN,),
        in_specs=[tile, tile, _const_spec((D_MODEL, D_MODEL), single_buffer=True), vec, vec, wspec, wspec, vec, vec, tile],
        out_specs=[tile, _row_spec(TM_FFN, 1), tile, _row_spec(TM_FFN, D_FF), tile, tile, vec, vec, vec],
        out_shape=[
            jax.ShapeDtypeStruct((t, D_MODEL), F32),
            jax.ShapeDtypeStruct((t, 1), F32),
            jax.ShapeDtypeStruct((t, D_MODEL), BF16),
            jax.ShapeDtypeStruct((t, D_FF), BF16),
            jax.ShapeDtypeStruct((t, D_MODEL), F32),
            jax.ShapeDtypeStruct((t, D_MODEL), BF16),
            jax.ShapeDtypeStruct((1, D_MODEL), F32),
            jax.ShapeDtypeStruct((1, D_MODEL), F32),
            jax.ShapeDtypeStruct((1, D_MODEL), F32),
        ],
        compiler_params=_params(("arbitrary",)),
    )(cat, x, w_out, ln1_g, ln1_b, w1, w2, ln2_g, ln2_b, target)


def _ffn_bwd_ln1(dz2, r, xhat1, rstd1, ln1_g, w1, w2, w_out, dep=None):
    t = dz2.shape[0]

    def body(dz2_ref, r_ref, xh_ref, rstd_ref, g1_ref, w1_ref, w2_ref, wo_ref, dpre_ref, dz1_ref, dz1b_ref, dcat_ref, dg1_ref, db1_ref):
        @pl.when(pl.program_id(0) == 0)
        def _():
            dg1_ref[...] = jnp.zeros_like(dg1_ref)
            db1_ref[...] = jnp.zeros_like(db1_ref)

        dz2 = dz2_ref[...]
        dz2b = dz2.astype(BF16)
        dx1 = ALPHA * dz2
        for j in range(N_FF_BLOCKS):
            cols = slice(j * D_MODEL, (j + 1) * D_MODEL)
            dpre = (_dot(dz2b, w2_ref[j], NT) * (2.0 * r_ref[:, cols].astype(F32))).astype(BF16)
            dpre_ref[:, cols] = dpre
            dx1 = dx1 + _dot(dpre, w1_ref[j], NT)
        xhat1 = xh_ref[...]
        dg1_ref[...] += jnp.sum(dx1 * xhat1, axis=0, keepdims=True)
        db1_ref[...] += jnp.sum(dx1, axis=0, keepdims=True)
        dz1 = _layer_norm_bwd(dx1 * g1_ref[...], xhat1, rstd_ref[...])
        dz1_ref[...] = dz1
        dz1b = dz1.astype(BF16)
        dz1b_ref[...] = dz1b
        dcat_ref[...] = _dot(dz1b, wo_ref[...], NT).astype(BF16)

    vec = _const_spec((1, D_MODEL))
    tile = _row_spec(TM_FFN, D_MODEL)
    wspec = _const_spec((N_FF_BLOCKS, D_MODEL, D_MODEL), single_buffer=True)
    body, in_specs, operands = _after(
        dep, body,
        [tile, _row_spec(TM_FFN, D_FF), tile, _row_spec(TM_FFN, 1), vec, wspec, wspec, _const_spec((D_MODEL, D_MODEL), single_buffer=True)],
        [dz2, r, xhat1, rstd1, ln1_g, w1, w2, w_out])
    return pl.pallas_call(
        body,
        name="ffn_bwd_ln1",
        grid=(t // TM_FFN,),
        in_specs=in_specs,
        out_specs=[_row_spec(TM_FFN, D_FF), tile, tile, tile, vec, vec],
        out_shape=[
            jax.ShapeDtypeStruct((t, D_FF), BF16),
            jax.ShapeDtypeStruct((t, D_MODEL), F32),
            jax.ShapeDtypeStruct((t, D_MODEL), BF16),
            jax.ShapeDtypeStruct((t, D_MODEL), BF16),
            jax.ShapeDtypeStruct((1, D_MODEL), F32),
            jax.ShapeDtypeStruct((1, D_MODEL), F32),
        ],
        compiler_params=_params(("arbitrary",)),
    )(*operands)


def _mixer_bwd(u, vg, q, k, va, dcat, cos, sin, v_ln_g, v_ln_b, w_spatial, bias_full, sinks, dep=None):
    t = u.shape[0]
    n_chunks = t // CHUNK

    def body(u_ref, vg_ref, q_ref, kc_ref, kp_ref, vc_ref, vp_ref, dcat_ref, cosc_ref, sinc_ref, cosp_ref, sinp_ref,
             g_ref, b_ref, w_ref, bias_ref, sink_ref,
             dmain_ref, dkv_ref, dg_ref, db_ref, dw_ref, dbs_ref, dsink_ref, dmix_acc, wcat, wcat_t):
        i = pl.program_id(0)
        left = _half_lane_masks(CHUNK)
        lane = lax.broadcasted_iota(jnp.int32, (CHUNK, LANES), 1)
        n_pairs = D_GMLP // LANES

        @pl.when(i == 0)
        def _():
            dg_ref[...] = jnp.zeros_like(dg_ref)
            db_ref[...] = jnp.zeros_like(db_ref)
            dw_ref[...] = jnp.zeros_like(dw_ref)
            dsink_ref[...] = jnp.zeros_like(dsink_ref)
            dmix_acc[...] = jnp.zeros_like(dmix_acc)
            _store_spatial_weights(w_ref, wcat, wcat_t)

        ug, dug_du = _gelu_and_grad(u_ref[...])
        gv, dgv_dv = _gelu_and_grad(vg_ref[...])
        xhat, rstd = _layer_norm_stats(gv)
        gain = g_ref[...]
        vgl = xhat * gain + b_ref[...]
        pair_cols = [slice(p * LANES, (p + 1) * LANES) for p in range(n_pairs)]
        mixed = [_dot(wcat[p], _pair_stack(vgl[:, cols], left)) for p, cols in enumerate(pair_cols)]
        dm_stacks = []
        for p, cols in enumerate(pair_cols):
            da = dcat_ref[:, cols].astype(F32)
            dmain_ref[:, cols] = (da * (mixed[p] + bias_ref[:, cols]) * dug_du[:, cols]).astype(BF16)
            dmixed = da * ug[:, cols]
            dmix_acc[:, cols] += dmixed
            dm_stacks.append(_pair_stack(dmixed, left))
        causal = _causal_mask()
        for p, cols in enumerate(pair_cols):
            dw_pair = _dot(dm_stacks[p], vgl[:, cols].astype(BF16), NT)
            dw_ref[2 * p] += jnp.where(causal, dw_pair[:CHUNK], 0.0)
            dw_ref[2 * p + 1] += jnp.where(causal, dw_pair[CHUNK:], 0.0)
        dvgl = jnp.concatenate([_dot(wcat_t[p], dm_stacks[p]) for p in range(n_pairs)], axis=1)
        dg_ref[...] += jnp.sum(dvgl * xhat, axis=0, keepdims=True)
        db_ref[...] += jnp.sum(dvgl, axis=0, keepdims=True)
        dgv = _layer_norm_bwd(dvgl * gain, xhat, rstd)
        dmain_ref[:, D_GMLP : 2 * D_GMLP] = (dgv * dgv_dv).astype(BF16)

        @pl.when(i == n_chunks - 1)
        def _():
            tile = jnp.zeros((CHUNK, LANES), F32)
            for p in range(D_GMLP // LANES):
                dm = dmix_acc[:, p * LANES : (p + 1) * LANES]
                sl = jnp.sum(jnp.where(left, dm, 0.0), axis=1, keepdims=True)
                sr = jnp.sum(jnp.where(left, 0.0, dm), axis=1, keepdims=True)
                tile = jnp.where(lane == 2 * p, sl, tile)
                tile = jnp.where(lane == 2 * p + 1, sr, tile)
            dbs_ref[...] = tile

        k_var = _kv_variants(jnp.concatenate([kp_ref[...], kc_ref[...]], axis=0))
        v_var = _kv_variants(jnp.concatenate([vp_ref[...], vc_ref[...]], axis=0))
        valid = _band_mask(i)
        n_qpairs = D_ATTN // LANES
        q_pairs = [q_ref[:, p * LANES : (p + 1) * LANES] for p in range(n_qpairs)]
        do_all = dcat_ref[:, D_GMLP:D_MODEL]
        do_pairs = [do_all[:, p * LANES : (p + 1) * LANES] for p in range(n_qpairs)]
        scores = [_dot(q_pairs[h // 2], k_var[h // 4][h % 2], NT) for h in range(N_HEADS)]
        dprobs = [_dot(do_pairs[h // 2], v_var[h // 4][h % 2], NT) for h in range(N_HEADS)]
        heads = range(N_HEADS)
        sinks_h = [sink_ref[h] for h in heads]
        masked = [jnp.where(valid, scores[h] * SCALE, NEG_INF) for h in heads]
        maxes = [jnp.maximum(jnp.max(masked[h], axis=1, keepdims=True), sinks_h[h]) for h in heads]
        exps = [jnp.exp(masked[h] - maxes[h]) for h in heads]
        exp_sinks = [jnp.exp(sinks_h[h] - maxes[h]) for h in heads]
        invs = [1.0 / (jnp.sum(exps[h], axis=1, keepdims=True) + exp_sinks[h]) for h in heads]
        probs = [exps[h] * invs[h] for h in heads]
        dsums = [jnp.sum(probs[h] * dprobs[h], axis=1, keepdims=True) for h in heads]
        ds_b = [(probs[h] * (dprobs[h] - dsums[h]) * SCALE).astype(BF16) for h in heads]
        probs_b = [probs[h].astype(BF16) for h in heads]
        dsink_row = jnp.zeros((1, LANES), F32)
        lane_row = lax.broadcasted_iota(jnp.int32, (1, LANES), 1)
        for h in heads:
            d_sink = -jnp.sum(exp_sinks[h] * invs[h] * dsums[h], axis=0, keepdims=True)
            dsink_row = dsink_row + jnp.where(lane_row == h, d_sink, 0.0)
        dsink_ref[0:1, :] += dsink_row
        dq_all = jnp.concatenate(
            [_dot(ds_b[2 * p], k_var[p // 2][0]) + _dot(ds_b[2 * p + 1], k_var[p // 2][1]) for p in range(n_qpairs)], axis=1)
        cos_c, sin_c = cosc_ref[...], sinc_ref[...]
        dmain_ref[:, 2 * D_GMLP : D_MAIN] = _rope_transposed(dq_all, _lane_tile(cos_c, n_qpairs), _lane_tile(sin_c, n_qpairs)).astype(BF16)

        q_t = q_ref[...].astype(F32).T.astype(BF16)
        do_t = do_all.astype(F32).T.astype(BF16)
        heads_per_group = N_HEADS // 2

        def group_grad_t(lhs_t, rhs_heads):
            parts = []
            for g in range(2):
                heads = range(g * heads_per_group, (g + 1) * heads_per_group)
                lhs = jnp.concatenate([lhs_t[h * HEAD_DIM : (h + 1) * HEAD_DIM] for h in heads], axis=1)
                parts.append(_dot(lhs, jnp.concatenate([rhs_heads[h] for h in heads], axis=0)))
            return jnp.concatenate(parts, axis=0).T

        dk2 = group_grad_t(q_t, ds_b)
        dv2 = group_grad_t(do_t, probs_b)
        cur = pl.ds(pl.multiple_of(i * CHUNK, CHUNK), CHUNK)
        dkv_ref[cur, 0:D_KV] = _rope_transposed(dk2[CHUNK:], cos_c, sin_c)
        dkv_ref[cur, D_KV : 2 * D_KV] = dv2[CHUNK:]

        @pl.when(i > 0)
        def _():
            prev = pl.ds(pl.multiple_of((i - 1) * CHUNK, CHUNK), CHUNK)
            dkv_ref[prev, 0:D_KV] += _rope_transposed(dk2[:CHUNK], cosp_ref[...], sinp_ref[...])
            dkv_ref[prev, D_KV : 2 * D_KV] += dv2[:CHUNK]

    cur = lambda i: (i, 0)
    prev = lambda i: (jnp.maximum(i - 1, 0), 0)
    in_specs = _chunk_specs() + [
        pl.BlockSpec((CHUNK, D_MODEL), cur),
        pl.BlockSpec((CHUNK, LANES), cur),
        pl.BlockSpec((CHUNK, LANES), cur),
        pl.BlockSpec((CHUNK, LANES), prev),
        pl.BlockSpec((CHUNK, LANES), prev),
        _const_spec((1, D_GMLP)),
        _const_spec((1, D_GMLP)),
        _const_spec((N_HEADS, CHUNK, CHUNK)),
        _const_spec((CHUNK, D_GMLP)),
        pl.BlockSpec(memory_space=pltpu.SMEM),
    ]
    body, in_specs, operands = _after(
        dep, body, in_specs, [u, vg, q, k, k, va, va, dcat, cos, sin, cos, sin, v_ln_g, v_ln_b, w_spatial, bias_full, sinks])
    return pl.pallas_call(
        body,
        name="mixer_bwd",
        grid=(n_chunks,),
        in_specs=in_specs,
        out_specs=[
            pl.BlockSpec((CHUNK, D_MAIN), cur),
            _const_spec((t, 2 * D_KV)),
            _const_spec((1, D_GMLP)),
            _const_spec((1, D_GMLP)),
            _const_spec((N_HEADS, CHUNK, CHUNK)),
            _const_spec((CHUNK, LANES)),
            _const_spec((8, LANES)),
        ],
        out_shape=[
            jax.ShapeDtypeStruct((t, D_MAIN), BF16),
            jax.ShapeDtypeStruct((t, 2 * D_KV), F32),
            jax.ShapeDtypeStruct((1, D_GMLP), F32),
            jax.ShapeDtypeStruct((1, D_GMLP), F32),
            jax.ShapeDtypeStruct((N_HEADS, CHUNK, CHUNK), F32),
            jax.ShapeDtypeStruct((CHUNK, LANES), F32),
            jax.ShapeDtypeStruct((8, LANES), F32),
        ],
        scratch_shapes=[
            pltpu.VMEM((CHUNK, D_GMLP), F32),
            pltpu.VMEM((D_GMLP // LANES, CHUNK, 2 * CHUNK), BF16),
            pltpu.VMEM((D_GMLP // LANES, CHUNK, 2 * CHUNK), BF16),
        ],
        compiler_params=_params(("arbitrary",)),
    )(*operands)


def _grad_x(dh_main, dkv, dz1, w_in_t, dep=None):
    t = dz1.shape[0]

    def body(dm_ref, dkv_ref, dz1_ref, w_ref, gx_ref):
        acc = ALPHA * dz1_ref[...] + _dot(dm_ref[...], w_ref[0:D_MAIN, :])
        gx_ref[...] = acc + _dot(dkv_ref[...].astype(BF16), w_ref[D_MAIN:D_IN, :])

    body, in_specs, operands = _after(
        dep, body, [_row_spec(TM, D_MAIN), _row_spec(TM, 2 * D_KV), _row_spec(TM, D_MODEL), _const_spec((D_IN, D_MODEL))], [dh_main, dkv, dz1, w_in_t])
    return pl.pallas_call(
        body,
        name="grad_x",
        grid=(t // TM,),
        in_specs=in_specs,
        out_specs=_row_spec(TM, D_MODEL),
        out_shape=jax.ShapeDtypeStruct((t, D_MODEL), F32),
        compiler_params=_params(("parallel",)),
    )(*operands)


def _token_contraction(name, out_rows, tk, in_arrays, contributions, dep=None):
    t = in_arrays[0].shape[0]

    def body(*refs):
        out_ref = refs[-1]

        @pl.when(pl.program_id(0) == 0)
        def _():
            out_ref[...] = jnp.zeros_like(out_ref)

        for row0, a, b in contributions(*refs[:-1]):
            out_ref[row0 : row0 + a.shape[1], :] += _dot(a, b, TN)

    in_specs = [_row_spec(tk, a.shape[1]) for a in in_arrays]
    body, in_specs, operands = _after(dep, body, in_specs, in_arrays)
    return pl.pallas_call(
        body,
        name=name,
        grid=(t // tk,),
        in_specs=in_specs,
        out_specs=_const_spec((out_rows, D_MODEL), single_buffer=True),
        out_shape=jax.ShapeDtypeStruct((out_rows, D_MODEL), F32),
        compiler_params=_params(("arbitrary",)),
    )(*operands)


def _grad_w_in_t(dh_main, dkv, x, dep=None):
    def contributions(dm_ref, dkv_ref, x_ref):
        xb = x_ref[...].astype(BF16)
        return [(0, dm_ref[...], xb), (D_MAIN, dkv_ref[...].astype(BF16), xb)]

    return _token_contraction("grad_w_in", D_IN, TK, [dh_main, dkv, x], contributions, dep)


def _grad_w_out(cat, dz1b, dep=None):
    def contributions(cat_ref, dz1_ref):
        return [(0, cat_ref[...], dz1_ref[...])]

    return _token_contraction("grad_w_out", D_MODEL, TK, [cat, dz1b], contributions, dep)


def _grad_w_ff1(x1b, dpre):
    def contributions(x1_ref, dpre_ref):
        x1 = x1_ref[...]
        return [(j * D_MODEL, x1, dpre_ref[:, j * D_MODEL : (j + 1) * D_MODEL]) for j in range(N_FF_BLOCKS)]

    return _token_contraction("grad_w_ff1", D_FF, TK_FF, [x1b, dpre], contributions)


def _grad_w_ff2(r, dz2b):
    def contributions(r_ref, dz2_ref):
        dz2 = dz2_ref[...]
        out = []
        for j in range(N_FF_BLOCKS):
            rf = r_ref[:, j * D_MODEL : (j + 1) * D_MODEL].astype(F32)
            out.append((j * D_MODEL, (rf * rf).astype(BF16), dz2))
        return out

    return _token_contraction("grad_w_ff2", D_FF, TK_FF, [r, dz2b], contributions)


ANY = pl.BlockSpec(memory_space=pl.ANY)


def _mesh_position():
    return lax.axis_index("x"), lax.axis_index("y"), lax.axis_index("c")


def _other_chips(x, y):
    return [(1 - x, y), (x, 1 - y), (1 - x, 1 - y)]


def _remote(src, dst, send_sem, recv_sem, device):
    return pltpu.make_async_remote_copy(src_ref=src, dst_ref=dst, send_sem=send_sem, recv_sem=recv_sem, device_id=device, device_id_type=MESH)


def _rows(ref, start, size):
    return ref.at[pl.ds(start, size), :]


def _all_gather_weights(shards):
    n = len(shards)
    per = 7

    def body(*refs):
        ins, outs = refs[:n], refs[n : 2 * n]
        send_sems, recv_sems = refs[2 * n :]
        x, y, c = _mesh_position()
        me = 2 * x + y
        chips = _other_chips(x, y)
        sibling = (x, y, 1 - c)
        started = []
        for w in range(n):
            rows = shards[w].shape[0]
            half = rows // 2
            for kk, (px, py) in enumerate(chips):
                cp = _remote(_rows(ins[w], c * half, half), _rows(outs[w], me * rows + c * half, half),
                             send_sems.at[per * w + kk], recv_sems.at[per * w + kk], (px, py, c))
                cp.start()
                started.append(cp)
            cp = _remote(ins[w], _rows(outs[w], me * rows, rows), send_sems.at[per * w + 6], recv_sems.at[per * w + 6], sibling)
            cp.start()
            started.append(cp)
        for w in range(n):
            rows = shards[w].shape[0]
            half = rows // 2
            for kk, (px, py) in enumerate(chips):
                blk = _rows(outs[w], (2 * px + py) * rows + c * half, half)
                _remote(blk, blk, send_sems.at[per * w + kk], recv_sems.at[per * w + kk], (px, py, c)).wait_recv()
                fwd = _remote(blk, blk, send_sems.at[per * w + 3 + kk], recv_sems.at[per * w + 3 + kk], sibling)
                fwd.start()
                started.append(fwd)
        for w in range(n):
            rows = shards[w].shape[0]
            half = rows // 2
            for kk, (px, py) in enumerate(chips):
                blk = _rows(outs[w], (2 * px + py) * rows + (1 - c) * half, half)
                _remote(blk, blk, send_sems.at[per * w + 3 + kk], recv_sems.at[per * w + 3 + kk], sibling).wait_recv()
            own = _rows(outs[w], me * rows, rows)
            _remote(own, own, send_sems.at[per * w + 6], recv_sems.at[per * w + 6], sibling).wait_recv()
        for cp in started:
            cp.wait_send()

    return pl.pallas_call(
        body,
        name="all_gather_weights",
        in_specs=[ANY] * n,
        out_specs=[ANY] * n,
        out_shape=[jax.ShapeDtypeStruct((N_CHIPS * s.shape[0], s.shape[1]), s.dtype) for s in shards],
        scratch_shapes=[pltpu.SemaphoreType.DMA((per * n,)), pltpu.SemaphoreType.DMA((per * n,))],
    )(*shards)


def _pair_gather(name, shards):
    n = len(shards)

    def body(*refs):
        outs = refs[n : 2 * n]
        send_sems, recv_sems = refs[2 * n :]
        x, y, c = _mesh_position()
        sibling = (x, y, 1 - c)
        sends = []
        for w in range(n):
            half = shards[w].shape[0] // 2
            mine = _rows(outs[w], c * half, half)
            cp = _remote(mine, mine, send_sems.at[w], recv_sems.at[w], sibling)
            cp.start()
            sends.append(cp)
        for w in range(n):
            half = shards[w].shape[0] // 2
            blk = _rows(outs[w], (1 - c) * half, half)
            _remote(blk, blk, send_sems.at[w], recv_sems.at[w], sibling).wait_recv()
        for cp in sends:
            cp.wait_send()

    return pl.pallas_call(
        body,
        name=name,
        in_specs=[ANY] * n,
        out_specs=[ANY] * n,
        out_shape=[jax.ShapeDtypeStruct(s.shape, s.dtype) for s in shards],
        input_output_aliases={w: w for w in range(n)},
        scratch_shapes=[pltpu.SemaphoreType.DMA((n,)), pltpu.SemaphoreType.DMA((n,))],
    )(*shards)


def _all_reduce_small(slab, dep=None):
    rows = slab.shape[0]
    part = rows // 8

    def body(slab_ref, out_ref, landing, reduced, send_sems, recv_sems):
        x, y, c = _mesh_position()
        me = 4 * x + 2 * y + c
        flips = [(k >> 2, (k >> 1) & 1, k & 1) for k in range(1, 8)]

        def peer(flip):
            fx, fy, fc = flip
            return (1 - x if fx else x, 1 - y if fy else y, 1 - c if fc else c)

        def my_rows(ref):
            return ref.at[pl.ds(pl.multiple_of(me * part, 8), part), :]

        sends = []
        for kk, flip in enumerate(flips):
            px, py, pc = peer(flip)
            them = 4 * px + 2 * py + pc
            cp = _remote(slab_ref.at[pl.ds(pl.multiple_of(them * part, 8), part), :], landing.at[me], send_sems.at[kk], recv_sems.at[kk], (px, py, pc))
            cp.start()
            sends.append(cp)
        landing[me] = my_rows(slab_ref)[...]
        for kk, flip in enumerate(flips):
            px, py, pc = peer(flip)
            them = 4 * px + 2 * py + pc
            _remote(landing.at[them], landing.at[them], send_sems.at[kk], recv_sems.at[kk], (px, py, pc)).wait_recv()
        total = landing[0]
        for s in range(1, 8):
            total = total + landing[s]
        reduced[...] = total
        my_rows(out_ref)[...] = total
        for kk, flip in enumerate(flips):
            cp = _remote(reduced, my_rows(out_ref), send_sems.at[7 + kk], recv_sems.at[7 + kk], peer(flip))
            cp.start()
            sends.append(cp)
        for kk, flip in enumerate(flips):
            px, py, pc = peer(flip)
            them = 4 * px + 2 * py + pc
            blk = out_ref.at[pl.ds(pl.multiple_of(them * part, 8), part), :]
            _remote(blk, blk, send_sems.at[7 + kk], recv_sems.at[7 + kk], (px, py, pc)).wait_recv()
        for cp in sends:
            cp.wait_send()

    vmem = pl.BlockSpec(memory_space=pltpu.VMEM)
    body, in_specs, operands = _after(dep, body, [vmem], [slab])
    return pl.pallas_call(
        body,
        name="all_reduce_small",
        in_specs=in_specs,
        out_specs=vmem,
        out_shape=jax.ShapeDtypeStruct(slab.shape, slab.dtype),
        scratch_shapes=[pltpu.VMEM((8, part, LANES), F32), pltpu.VMEM((part, LANES), F32), pltpu.SemaphoreType.DMA((14,)), pltpu.SemaphoreType.DMA((14,))],
    )(*operands)


HBM = pl.BlockSpec(memory_space=pltpu.HBM)
SEM = pl.BlockSpec(memory_space=pltpu.SEMAPHORE)
DATAFLOW = pltpu.SideEffectType.DATAFLOW_SIDE_EFFECTING
TOKEN = jax.ShapeDtypeStruct((8, LANES), F32)


def _plan_copies(bufs, plan, send_sems, recv_sems):
    out = []
    for i, (src, src_row, dst, dst_row, recv_row, rows, device) in enumerate(plan):
        send = _remote(_rows(bufs[src], src_row, rows), _rows(bufs[dst], dst_row, rows), send_sems.at[i], recv_sems.at[i], device)
        landed = _rows(bufs[dst], recv_row, rows)
        recv = _remote(landed, landed, send_sems.at[i], recv_sems.at[i], device)
        out.append((send, recv))
    return out


def _split_call(name, bufs, wait=None, start=None, after=None):
    n = len(bufs)
    n_in = n + (2 if wait else 0) + (1 if after is not None else 0)
    n_start = len(start(0, 0, 0)) if start else 0

    def body(*refs):
        ins = refs[:n]
        x, y, c = _mesh_position()
        if wait:
            for send, recv in _plan_copies(ins, wait[0](x, y, c), refs[n], refs[n + 1]):
                recv.wait_recv()
                send.wait_send()
        if start:
            for send, _ in _plan_copies(ins, start(x, y, c), refs[n_in + n + 1], refs[n_in + n + 2]):
                send.start()
        token = refs[n_in + n]
        token[...] = jnp.zeros_like(token)

    operands = [pltpu.with_memory_space_constraint(b, pltpu.HBM) for b in bufs]
    in_specs = [HBM] * n
    if wait:
        operands += [wait[1], wait[2]]
        in_specs += [SEM, SEM]
    if after is not None:
        operands.append(after)
        in_specs.append(ANY)
    out_shape = [pltpu.HBM(b.shape, b.dtype) for b in bufs] + [TOKEN]
    out_specs = [HBM] * n + [pl.BlockSpec(memory_space=pltpu.VMEM)]
    if start:
        out_shape += [pltpu.SemaphoreType.DMA((n_start,)), pltpu.SemaphoreType.DMA((n_start,))]
        out_specs += [SEM, SEM]
    outs = pl.pallas_call(
        body,
        name=name,
        in_specs=in_specs,
        out_specs=out_specs,
        out_shape=out_shape,
        input_output_aliases={i: i for i in range(n)},
        compiler_params=pltpu.CompilerParams(has_side_effects=DATAFLOW),
    )(*operands)
    return (list(outs[:n]), outs[n]) + tuple(outs[n + 1 :])


def _gather_plans(shard_rows):
    n = len(shard_rows)

    def ici(x, y, c):
        me = 2 * x + y
        plan = []
        for w, rows in enumerate(shard_rows):
            half = rows // 2
            for px, py in _other_chips(x, y):
                plan.append((w, c * half, n + w, me * rows + c * half, (2 * px + py) * rows + c * half, half, (px, py, c)))
            plan.append((w, 0, n + w, me * rows, me * rows, rows, (x, y, 1 - c)))
        return plan

    def passed_on(x, y, c):
        plan = []
        for w, rows in enumerate(shard_rows):
            half = rows // 2
            for px, py in _other_chips(x, y):
                row = (2 * px + py) * rows
                plan.append((n + w, row + c * half, n + w, row + c * half, row + (1 - c) * half, half, (x, y, 1 - c)))
        return plan

    return ici, passed_on


def _swap_plan(block_rows):
    n = len(block_rows)

    def plan_fn(x, y, c):
        plan = []
        for w, rows in enumerate(block_rows):
            half = rows // 2
            for j in range(N_CHIPS):
                plan.append((w, j * rows + (1 - c) * half, n + w, j * half, j * half, half, (x, y, 1 - c)))
        return plan

    return plan_fn


def _exchange_plan(halves):
    n = len(halves)

    def plan_fn(x, y, c):
        plan = []
        for w, half in enumerate(halves):
            for kk, (px, py) in enumerate(_other_chips(x, y)):
                plan.append((w, (2 * px + py) * half, n + w, kk * half, kk * half, half, (px, py, c)))
        return plan

    return plan_fn


def _landing(rows, cols, dtype):
    return lax.empty((rows, cols), dtype)


def _row_tile(rows, cap=512):
    best = 8
    for cand in range(8, cap + 1, 8):
        if rows % cand == 0:
            best = cand
    return best


def _pair_sum(name, grad, theirs, pos):
    half = theirs.shape[0] // N_CHIPS
    cols = theirs.shape[1]
    tile = _row_tile(half)
    steps = half // tile

    def body(pos_ref, g_ref, t_ref, p_ref, own_ref):
        total = g_ref[...] + t_ref[...]
        p_ref[...] = total.astype(BF16)

        @pl.when(pl.program_id(1) == pos_ref[1])
        def _():
            own_ref[...] = total

    return pl.pallas_call(
        body,
        name=name,
        grid_spec=pltpu.PrefetchScalarGridSpec(
            num_scalar_prefetch=1,
            grid=(steps, N_CHIPS),
            in_specs=[
                pl.BlockSpec((tile, cols), lambda i, j, pos: ((2 * j + pos[0]) * steps + i, 0)),
                pl.BlockSpec((tile, cols), lambda i, j, pos: (j * steps + i, 0)),
            ],
            out_specs=[
                pl.BlockSpec((tile, cols), lambda i, j, pos: (j * steps + i, 0)),
                pl.BlockSpec((tile, cols), lambda i, j, pos: (i, 0)),
            ],
        ),
        out_shape=[jax.ShapeDtypeStruct((N_CHIPS * half, cols), BF16), jax.ShapeDtypeStruct((half, cols), F32)],
        compiler_params=_params(("parallel", "arbitrary")),
    )(pos, grad, theirs)


def _chip_sum(name, own, landed, pos):
    half, cols = own.shape
    tile = _row_tile(half)
    steps = half // tile

    def body(pos_ref, own_ref, l0, l1, l2, o_ref):
        o_ref[...] = ((own_ref[...] + l0[...].astype(F32)) + l1[...].astype(F32)) + l2[...].astype(F32)

    landed_specs = [pl.BlockSpec((tile, cols), lambda i, pos, _k=k: (_k * steps + i, 0)) for k in range(N_CHIPS - 1)]
    return pl.pallas_call(
        body,
        name=name,
        grid_spec=pltpu.PrefetchScalarGridSpec(
            num_scalar_prefetch=1,
            grid=(steps,),
            in_specs=[pl.BlockSpec((tile, cols), lambda i, pos: (i, 0))] + landed_specs,
            out_specs=pl.BlockSpec((tile, cols), lambda i, pos: (pos[0] * steps + i, 0)),
        ),
        out_shape=jax.ShapeDtypeStruct((2 * half, cols), F32),
        compiler_params=_params(("parallel",)),
    )(pos, own, landed, landed, landed)


def _adamw(name, w, g, m, v):
    rows, cols = w.shape
    tile = rows if rows * cols <= 256 * 1024 else _row_tile(rows)

    def body(w_ref, g_ref, m_ref, v_ref, g_out_ref, d_ref, nm_ref, nv_ref):
        g = g_ref[...]
        g_out_ref[...] = g
        nm = ADAM_B1 * m_ref[...] + (1.0 - ADAM_B1) * g
        nv = ADAM_B2 * v_ref[...] + (1.0 - ADAM_B2) * (g * g)
        m_hat = nm / (1.0 - ADAM_B1**ADAM_STEP)
        v_hat = nv / (1.0 - ADAM_B2**ADAM_STEP)
        d_ref[...] = -ADAM_LR * (m_hat / (jnp.sqrt(v_hat) + ADAM_EPS) + ADAM_WD * w_ref[...])
        nm_ref[...] = nm
        nv_ref[...] = nv

    spec = _row_spec(tile, cols)
    return pl.pallas_call(
        body,
        name=name,
        grid=(rows // tile,),
        in_specs=[spec] * 4,
        out_specs=[spec] * 4,
        out_shape=[jax.ShapeDtypeStruct((rows, cols), F32)] * 4,
        compiler_params=_params(("parallel",)),
    )(w, g, m, v)


_SMALL = (
    ("v_ln_g", (D_GMLP,), 8),
    ("v_ln_b", (D_GMLP,), 8),
    ("w_spatial", (N_HEADS, CHUNK, CHUNK), 1024),
    ("b_spatial", (N_HEADS, CHUNK), 8),
    ("sinks", (N_HEADS,), 8),
    ("ln1_g", (D_MODEL,), 8),
    ("ln1_b", (D_MODEL,), 8),
    ("ln2_g", (D_MODEL,), 8),
    ("ln2_b", (D_MODEL,), 8),
    ("squared_error", (D_MODEL,), 8),
)
N_SMALL_PARAMS = len(_SMALL) - 1


def _pack_small(values):
    parts = []
    for (name, shape, rows), val in zip(_SMALL, values, strict=True):
        flat = val.reshape(-1).astype(F32)
        parts.append(jnp.pad(flat, (0, rows * LANES - flat.shape[0])).reshape(rows, LANES))
    parts.append(jnp.zeros((SMALL_ROWS - sum(rows for _, _, rows in _SMALL), LANES), F32))
    return jnp.concatenate(parts, axis=0)


def _adamw_update(w, g, m, v):
    nm = ADAM_B1 * m + (1.0 - ADAM_B1) * g
    nv = ADAM_B2 * v + (1.0 - ADAM_B2) * (g * g)
    m_hat = nm / (1.0 - ADAM_B1**ADAM_STEP)
    v_hat = nv / (1.0 - ADAM_B2**ADAM_STEP)
    return -ADAM_LR * (m_hat / (jnp.sqrt(v_hat) + ADAM_EPS) + ADAM_WD * w), nm, nv


def _adamw_small(g_slab, params, first, second):
    n = N_SMALL_PARAMS

    def pieces(shape):
        if len(shape) == 3:
            return [((0, h), h * shape[1], shape[1], shape[2]) for h in range(shape[0])]
        if len(shape) == 2:
            return [((0,), 0, shape[0], shape[1])]
        if shape[0] >= LANES:
            return [((slice(None), slice(r * LANES, (r + 1) * LANES)), r, 1, LANES) for r in range(shape[0] // LANES)]
        return [((slice(None), slice(0, shape[0])), 0, 1, shape[0])]

    def body(*refs):
        g_ref = refs[0]
        w_refs, m_refs, v_refs = refs[1 : 1 + n], refs[1 + n : 1 + 2 * n], refs[1 + 2 * n : 1 + 3 * n]
        outs = refs[1 + 3 * n :]
        row0 = 0
        for idx, (_, shape, rows) in enumerate(_SMALL[:n]):
            for where, first_row, n_rows, lanes in pieces(shape):
                g = g_ref[row0 + first_row : row0 + first_row + n_rows, 0:lanes]
                delta, nm, nv = _adamw_update(w_refs[idx][where], g, m_refs[idx][where], v_refs[idx][where])
                for group, val in enumerate((g, delta, nm, nv)):
                    outs[group * n + idx][where] = val
            row0 += rows

    vmem = pl.BlockSpec(memory_space=pltpu.VMEM)
    shapes = [jax.ShapeDtypeStruct(p.shape, F32) for p in params]
    outs = pl.pallas_call(
        body,
        name="adamw_small",
        in_specs=[vmem] * (1 + 3 * n),
        out_specs=[vmem] * (4 * n),
        out_shape=shapes * 4,
        compiler_params=_params(),
    )(g_slab, *params, *first, *second)
    return [list(outs[group * n : (group + 1) * n]) for group in range(4)]


def kernel(x, positions, w_in, v_ln_g, v_ln_b, w_spatial, b_spatial, sinks, w_out, ln1_g, ln1_b, w_ff1, w_ff2, ln2_g, ln2_b, loss_target, m_w_in, m_v_ln_g, m_v_ln_b, m_w_spatial, m_b_spatial, m_sinks, m_w_out, m_ln1_g, m_ln1_b, m_w_ff1, m_w_ff2, m_ln2_g, m_ln2_b, v_w_in, v_v_ln_g, v_v_ln_b, v_w_spatial, v_b_spatial, v_sinks, v_w_out, v_ln1_g, v_ln1_b, v_w_ff1, v_w_ff2, v_ln2_g, v_ln2_b):
    t = x.shape[1]
    x2 = x.reshape(t, D_MODEL)
    target = loss_target.reshape(t, D_MODEL)

    (w_in_t,) = _all_gather_weights([w_in[0].T.astype(BF16)])
    later = [w_out[0].astype(BF16), w_ff1[0].astype(BF16), w_ff2[0].astype(BF16)]
    later_rows = [s.shape[0] for s in later]
    ici_plan, pass_plan = _gather_plans(later_rows)
    bufs, started, ici_send, ici_recv = _split_call(
        "gather_start", later + [_landing(N_CHIPS * r, D_MODEL, BF16) for r in later_rows], start=ici_plan, after=w_in_t)

    inv_freq = ROPE_THETA ** (-jnp.arange(0, HEAD_DIM, 2, dtype=F32) / HEAD_DIM)
    cos, sin = _rope_tables(positions, jnp.tile(inv_freq, LANES // (HEAD_DIM // 2)).reshape(1, LANES))
    u, vg, q, k, va = _in_proj(x2, w_in_t, cos, sin, dep=started)
    bias_full = jnp.repeat(b_spatial[0].T, HEAD_DIM, axis=1)
    sink_vec = sinks.reshape(N_HEADS)
    cat = _mixer_fwd(u, vg, q, k, va, v_ln_g, v_ln_b, w_spatial[0], bias_full, sink_vec)
    bufs, passed, pass_send, pass_recv = _split_call("gather_pass", bufs, wait=(ici_plan, ici_send, ici_recv), start=pass_plan, after=cat)
    bufs, _ = _split_call("gather_end", bufs, wait=(pass_plan, pass_send, pass_recv), after=passed)
    w_out_all = bufs[3]
    w1_all = bufs[4].reshape(N_FF_BLOCKS, D_MODEL, D_MODEL)
    w2_all = bufs[5].reshape(N_FF_BLOCKS, D_MODEL, D_MODEL)
    xhat1, rstd1, x1b, r, dz2, dz2b, d_ln2_g, d_ln2_b, sq_err = _ffn_fwd_loss(
        cat, x2, w_out_all, ln1_g, ln1_b, w1_all, w2_all, ln2_g, ln2_b, target)

    pos = jnp.stack([lax.axis_index("c"), 2 * lax.axis_index("x") + lax.axis_index("y")]).astype(jnp.int32)
    half_landing = lambda g: _landing(g.shape[0] // 2, D_MODEL, F32)
    g_ff2_local = _grad_w_ff2(r, dz2b)
    swap_plan = _swap_plan([D_FF // N_CHIPS])
    ff2_bufs, swapping2, swap2_send, swap2_recv = _split_call("ff2_swap_start", [g_ff2_local, half_landing(g_ff2_local)], start=swap_plan)
    dpre, dz1, dz1b, dcat, d_ln1_g, d_ln1_b = _ffn_bwd_ln1(dz2, r, xhat1, rstd1, ln1_g, w1_all, w2_all, w_out_all, dep=swapping2)
    g_ff1_local = _grad_w_ff1(x1b, dpre)
    ff1_bufs, swapping1, swap1_send, swap1_recv = _split_call("ff1_swap_start", [g_ff1_local, half_landing(g_ff1_local)], start=swap_plan)
    g_out_local = _grad_w_out(cat, dz1b, dep=swapping1)
    ff2_bufs, swapped2 = _split_call("ff2_swap_wait", ff2_bufs, wait=(swap_plan, swap2_send, swap2_recv), after=g_out_local)
    ff1_bufs, _ = _split_call("ff1_swap_wait", ff1_bufs, wait=(swap_plan, swap1_send, swap1_recv), after=swapped2)
    ff_sums = [_pair_sum("grad_pair_sum_w_ff1", ff1_bufs[0], ff1_bufs[1], pos), _pair_sum("grad_pair_sum_w_ff2", ff2_bufs[0], ff2_bufs[1], pos)]
    ff_halves = [p.shape[0] // N_CHIPS for p, _ in ff_sums]
    exchange_plan = _exchange_plan(ff_halves)
    bufs, exchanging, ex_send, ex_recv = _split_call(
        "ff_exchange_start", [p for p, _ in ff_sums] + [_landing(3 * h, D_MODEL, BF16) for h in ff_halves], start=exchange_plan)
    dh_main, dkv, d_v_ln_g, d_v_ln_b, d_w_spatial, d_b_spatial_t, d_sinks = _mixer_bwd(
        u, vg, q, k, va, dcat, cos, sin, v_ln_g, v_ln_b, w_spatial[0], bias_full, sink_vec, dep=exchanging)
    small_g = _all_reduce_small(_pack_small(
        [d_v_ln_g, d_v_ln_b, d_w_spatial, d_b_spatial_t[:, :N_HEADS].T, d_sinks[0, :N_HEADS], d_ln1_g, d_ln1_b, d_ln2_g, d_ln2_b, sq_err]))
    sq_row = sum(rows for _, _, rows in _SMALL[:N_SMALL_PARAMS])
    loss = 0.5 * jnp.sum(small_g[sq_row : sq_row + _SMALL[N_SMALL_PARAMS][2]]) / D_MODEL
    g_in_local = _grad_w_in_t(dh_main, dkv, x2, dep=small_g)

    small = [g_in_local, g_out_local]
    small_swap_plan = _swap_plan([g.shape[0] // N_CHIPS for g in small])
    swap_bufs, small_swapping, ss_send, ss_recv = _split_call(
        "small_swap_start", small + [half_landing(g) for g in small], start=small_swap_plan)
    grad_x_flat = _grad_x(dh_main, dkv, dz1, w_in_t, dep=small_swapping)
    grad_x = grad_x_flat.reshape(1, t, D_MODEL)
    swap_bufs, _ = _split_call("small_swap_wait", swap_bufs, wait=(small_swap_plan, ss_send, ss_recv), after=grad_x_flat)
    pair_sums = [_pair_sum("grad_pair_sum_" + nm, g, th, pos) for nm, g, th in zip(["w_in", "w_out"], swap_bufs[:2], swap_bufs[2:])]
    small_halves = [p.shape[0] // N_CHIPS for p, _ in pair_sums]
    small_plan = _exchange_plan(small_halves)
    small_bufs, small_exchanging, sm_send, sm_recv = _split_call(
        "small_exchange_start", [p for p, _ in pair_sums] + [_landing(3 * h, D_MODEL, BF16) for h in small_halves], start=small_plan)

    bufs, _ = _split_call("ff_exchange_wait", bufs, wait=(exchange_plan, ex_send, ex_recv), after=small_exchanging)
    ff_shards = [_chip_sum("grad_chip_sum_" + nm, own, ld, pos) for nm, (_, own), ld in zip(["w_ff1", "w_ff2"], ff_sums, bufs[2:])]
    g_w_ff1, g_w_ff2 = _pair_gather("grad_pair_gather_ff", ff_shards)

    g_w_ff1, d_w_ff1, nm_w_ff1, nv_w_ff1 = _adamw("adamw_w_ff1", w_ff1[0], g_w_ff1, m_w_ff1[0], v_w_ff1[0])
    g_w_ff2, d_w_ff2, nm_w_ff2, nv_w_ff2 = _adamw("adamw_w_ff2", w_ff2[0], g_w_ff2, m_w_ff2[0], v_w_ff2[0])
    small_bufs, _ = _split_call("small_exchange_wait", small_bufs, wait=(small_plan, sm_send, sm_recv), after=nv_w_ff2)
    shards = [_chip_sum("grad_chip_sum_" + nm, own, ld, pos) for nm, (_, own), ld in zip(["w_in", "w_out"], pair_sums, small_bufs[2:])]
    g_w_in_t, g_w_out = _pair_gather("grad_pair_gather_small", shards)
    g_w_in, d_w_in, nm_w_in, nv_w_in = (a.T for a in _adamw("adamw_w_in", w_in[0].T, g_w_in_t, m_w_in[0].T, v_w_in[0].T))
    g_w_out, d_w_out, nm_w_out, nv_w_out = _adamw("adamw_w_out", w_out[0], g_w_out, m_w_out[0], v_w_out[0])
    small_grads, small_d, small_nm, small_nv = _adamw_small(
        small_g,
        [v_ln_g, v_ln_b, w_spatial, b_spatial, sinks, ln1_g, ln1_b, ln2_g, ln2_b],
        [m_v_ln_g, m_v_ln_b, m_w_spatial, m_b_spatial, m_sinks, m_ln1_g, m_ln1_b, m_ln2_g, m_ln2_b],
        [v_v_ln_g, v_v_ln_b, v_w_spatial, v_b_spatial, v_sinks, v_ln1_g, v_ln1_b, v_ln2_g, v_ln2_b])

    def with_big(small, w_in_v, w_out_v, w_ff1_v, w_ff2_v):
        g_vg, g_vb, g_ws, g_bs, g_sk, g_1g, g_1b, g_2g, g_2b = small
        return [w_in_v[None], g_vg, g_vb, g_ws, g_bs, g_sk, w_out_v[None], g_1g, g_1b, w_ff1_v[None], w_ff2_v[None], g_2g, g_2b]

    return (
        loss,
        grad_x,
        *with_big(small_grads, g_w_in, g_w_out, g_w_ff1, g_w_ff2),
        *with_big(small_d, d_w_in, d_w_out, d_w_ff1, d_w_ff2),
        *with_big(small_nm, nm_w_in, nm_w_out, nm_w_ff1, nm_w_ff2),
        *with_big(small_nv, nv_w_in, nv_w_out, nv_w_ff1, nv_w_ff2),
    )
```

```python
import math

import jax
import jax.numpy as jnp
from jax import lax
from jax.experimental import pallas as pl
from jax.experimental.pallas import tpu as pltpu

F32 = jnp.float32
BF16 = jnp.bfloat16

D_MODEL = 1024
HEAD_DIM = 64
D_GMLP = 512
D_ATTN = 512
D_KV = 128
D_IN = 2 * D_GMLP + D_ATTN + 2 * D_KV
D_MAIN = 2 * D_GMLP + D_ATTN
N_HEADS = 8
CHUNK = 128
ROPE_THETA = 10000.0
D_FF = 4 * D_MODEL
N_FF_BLOCKS = 4
LN_EPS = 1e-5
ALPHA = (2.0 * 1) ** 0.25
NEG_INF = -1e30
SCALE = 1.0 / math.sqrt(HEAD_DIM)

ADAM_LR = 0.001
ADAM_B1 = 0.9
ADAM_B2 = 0.999
ADAM_EPS = 1e-08
ADAM_WD = 0.01
ADAM_STEP = 10

N_CHIPS = 4
LANES = 128
V7X_VMEM_BYTES = 64 * 1024 * 1024
VMEM_LIMIT = V7X_VMEM_BYTES - 8 * 1024 * 1024
TM = 512
TM_FFN = 256
TK = 1024
TK_FF = 1024
SMALL_ROWS = 1152
MESH = pl.DeviceIdType.MESH

NT = (((1,), (1,)), ((), ()))
TN = (((0,), (0,)), ((), ()))


def _dot(a, b, dims=None):
    if dims is None:
        return jnp.dot(a, b, preferred_element_type=F32)
    return lax.dot_general(a, b, dims, preferred_element_type=F32)


def _params(semantics=None):
    return pltpu.CompilerParams(dimension_semantics=semantics, vmem_limit_bytes=VMEM_LIMIT)


def _const_spec(shape, single_buffer=False):
    zeros = (0,) * len(shape)
    if single_buffer:
        return pl.BlockSpec(shape, lambda *_: zeros, pipeline_mode=pl.Buffered(1))
    return pl.BlockSpec(shape, lambda *_: zeros)


def _row_spec(rows, cols):
    return pl.BlockSpec((rows, cols), lambda i: (i, 0))


def _after(dep, body, in_specs, operands):
    if dep is None:
        return body, list(in_specs), list(operands)
    return (lambda dep_ref, *refs: body(*refs)), [pl.BlockSpec(memory_space=pl.ANY)] + list(in_specs), [dep] + list(operands)


def _gelu(x):
    k = math.sqrt(2.0 / math.pi)
    return 0.5 * x * (1.0 + jnp.tanh(k * (x + 0.044715 * (x * x * x))))


def _gelu_and_grad(x):
    k = math.sqrt(2.0 / math.pi)
    x2 = x * x
    t = jnp.tanh(k * (x + 0.044715 * (x2 * x)))
    g = 0.5 * x * (1.0 + t)
    dg = 0.5 * (1.0 + t) + 0.5 * x * (1.0 - t * t) * (k * (1.0 + 3.0 * 0.044715 * x2))
    return g, dg


def _layer_norm_stats(z):
    mu = jnp.mean(z, axis=-1, keepdims=True)
    zc = z - mu
    var = jnp.mean(zc * zc, axis=-1, keepdims=True)
    rstd = lax.rsqrt(var + LN_EPS)
    return zc * rstd, rstd


def _layer_norm_bwd(dxhat, xhat, rstd):
    m1 = jnp.mean(dxhat, axis=-1, keepdims=True)
    m2 = jnp.mean(dxhat * xhat, axis=-1, keepdims=True)
    return rstd * (dxhat - m1 - xhat * m2)


def _rotate_half(t):
    n = t.shape[1]
    lane = lax.broadcasted_iota(jnp.int32, t.shape, 1)
    first = (lane & (HEAD_DIM // 2)) == 0
    return jnp.where(first, -pltpu.roll(t, n - HEAD_DIM // 2, 1), pltpu.roll(t, HEAD_DIM // 2, 1))


def _rope(t, cos, sin):
    return t * cos + _rotate_half(t) * sin


def _rope_transposed(g, cos, sin):
    return g * cos - _rotate_half(g * sin)


def _lane_tile(a, reps):
    return jnp.tile(a, (1, reps)) if reps > 1 else a


def _rope_tables(pos_row, inv_freq_row):
    t = pos_row.shape[1]

    def body(pos_ref, f_ref, cos_ref, sin_ref):
        pos_rows = jnp.broadcast_to(pos_ref[...].astype(F32), (LANES, TM)).T
        ang = pos_rows * f_ref[...]
        cos_ref[...] = jnp.cos(ang)
        sin_ref[...] = jnp.sin(ang)

    return pl.pallas_call(
        body,
        name="rope_tables",
        grid=(t // TM,),
        in_specs=[pl.BlockSpec((1, TM), lambda i: (0, i)), _const_spec((1, LANES))],
        out_specs=[_row_spec(TM, LANES), _row_spec(TM, LANES)],
        out_shape=[jax.ShapeDtypeStruct((t, LANES), F32)] * 2,
        compiler_params=_params(("parallel",)),
    )(pos_row, inv_freq_row)


def _in_proj(x, w_in_t, cos, sin, dep=None):
    t = x.shape[0]

    def body(x_ref, w_ref, cos_ref, sin_ref, u_ref, vg_ref, q_ref, k_ref, va_ref):
        xb = x_ref[...].astype(BF16)
        u_ref[...] = _dot(xb, w_ref[0:D_GMLP, :], NT)
        vg_ref[...] = _dot(xb, w_ref[D_GMLP : 2 * D_GMLP, :], NT)
        q = _dot(xb, w_ref[2 * D_GMLP : D_MAIN, :], NT)
        k = _dot(xb, w_ref[D_MAIN : D_MAIN + D_KV, :], NT)
        va_ref[...] = _dot(xb, w_ref[D_MAIN + D_KV : D_IN, :], NT).astype(BF16)
        c, s = cos_ref[...], sin_ref[...]
        q_ref[...] = _rope(q, _lane_tile(c, D_ATTN // LANES), _lane_tile(s, D_ATTN // LANES)).astype(BF16)
        k_ref[...] = _rope(k, c, s).astype(BF16)

    body, in_specs, operands = _after(
        dep, body, [_row_spec(TM, D_MODEL), _const_spec((D_IN, D_MODEL)), _row_spec(TM, LANES), _row_spec(TM, LANES)], [x, w_in_t, cos, sin])
    return pl.pallas_call(
        body,
        name="in_proj",
        grid=(t // TM,),
        in_specs=in_specs,
        out_specs=[_row_spec(TM, D_GMLP), _row_spec(TM, D_GMLP), _row_spec(TM, D_ATTN), _row_spec(TM, D_KV), _row_spec(TM, D_KV)],
        out_shape=[
            jax.ShapeDtypeStruct((t, D_GMLP), F32),
            jax.ShapeDtypeStruct((t, D_GMLP), F32),
            jax.ShapeDtypeStruct((t, D_ATTN), BF16),
            jax.ShapeDtypeStruct((t, D_KV), BF16),
            jax.ShapeDtypeStruct((t, D_KV), BF16),
        ],
        compiler_params=_params(("parallel",)),
    )(*operands)


def _chunk_specs():
    cur = lambda i: (i, 0)
    prev = lambda i: (jnp.maximum(i - 1, 0), 0)
    return [
        pl.BlockSpec((CHUNK, D_GMLP), cur),
        pl.BlockSpec((CHUNK, D_GMLP), cur),
        pl.BlockSpec((CHUNK, D_ATTN), cur),
        pl.BlockSpec((CHUNK, D_KV), cur),
        pl.BlockSpec((CHUNK, D_KV), prev),
        pl.BlockSpec((CHUNK, D_KV), cur),
        pl.BlockSpec((CHUNK, D_KV), prev),
    ]


def _half_lane_masks(rows):
    lane = lax.broadcasted_iota(jnp.int32, (rows, LANES), 1)
    return lane < HEAD_DIM


def _kv_variants(kv2):
    left = _half_lane_masks(kv2.shape[0])
    f = kv2.astype(F32)
    swapped = pltpu.roll(f, HEAD_DIM, 1)
    zero = jnp.zeros_like(f)
    g0 = (jnp.where(left, f, zero).astype(BF16), jnp.where(left, zero, swapped).astype(BF16))
    g1 = (jnp.where(left, swapped, zero).astype(BF16), jnp.where(left, zero, f).astype(BF16))
    return (g0, g1)


def _band_mask(i, heads=1):
    row = lax.broadcasted_iota(jnp.int32, (heads * CHUNK, 2 * CHUNK), 0) & (CHUNK - 1)
    col = lax.broadcasted_iota(jnp.int32, (heads * CHUNK, 2 * CHUNK), 1)
    no_prev = jnp.where(i > 0, 0, 4 * CHUNK)
    in_prev = jnp.logical_and(col < CHUNK, (col - row) > no_prev)
    in_cur = jnp.logical_and(col >= CHUNK, (col - CHUNK) <= row)
    return jnp.logical_or(in_prev, in_cur)


def _causal_mask():
    row = lax.broadcasted_iota(jnp.int32, (CHUNK, CHUNK), 0)
    col = lax.broadcasted_iota(jnp.int32, (CHUNK, CHUNK), 1)
    return col <= row


def _store_spatial_weights(w_ref, wcat_ref, wcat_t_ref=None):
    causal = _causal_mask()
    for p in range(D_GMLP // LANES):
        wl = jnp.where(causal, w_ref[2 * p], 0.0)
        wr = jnp.where(causal, w_ref[2 * p + 1], 0.0)
        wcat_ref[p] = jnp.concatenate([wl, wr], axis=1).astype(BF16)
        if wcat_t_ref is not None:
            wcat_t_ref[p] = jnp.concatenate([wl.T, wr.T], axis=1).astype(BF16)


def _pair_stack(xp, left):
    return jnp.concatenate([jnp.where(left, xp, 0.0), jnp.where(left, 0.0, xp)], axis=0).astype(BF16)


def _mixer_fwd(u, vg, q, k, va, v_ln_g, v_ln_b, w_spatial, bias_full, sinks):
    t = u.shape[0]

    def body(u_ref, vg_ref, q_ref, kc_ref, kp_ref, vc_ref, vp_ref, g_ref, b_ref, w_ref, bias_ref, sink_ref, cat_ref, wcat):
        i = pl.program_id(0)
        left = _half_lane_masks(CHUNK)

        @pl.when(i == 0)
        def _():
            _store_spatial_weights(w_ref, wcat)

        heads = range(N_HEADS)
        pair_cols = [slice(p * LANES, (p + 1) * LANES) for p in range(D_GMLP // LANES)]
        k_var = _kv_variants(jnp.concatenate([kp_ref[...], kc_ref[...]], axis=0))
        v_var = _kv_variants(jnp.concatenate([vp_ref[...], vc_ref[...]], axis=0))
        scores = [_dot(q_ref[:, pair_cols[h // 2]], k_var[h // 4][h % 2], NT) for h in heads]

        ug = _gelu(u_ref[...])
        xhat, _ = _layer_norm_stats(_gelu(vg_ref[...]))
        vgl = xhat * g_ref[...] + b_ref[...]
        mixed = [_dot(wcat[p], _pair_stack(vgl[:, cols], left)) for p, cols in enumerate(pair_cols)]

        valid = _band_mask(i)
        sinks_h = [sink_ref[h] for h in heads]
        masked = [jnp.where(valid, scores[h] * SCALE, NEG_INF) for h in heads]
        maxes = [jnp.maximum(jnp.max(masked[h], axis=1, keepdims=True), sinks_h[h]) for h in heads]
        exps = [jnp.exp(masked[h] - maxes[h]) for h in heads]
        invs = [1.0 / (jnp.sum(exps[h], axis=1, keepdims=True) + jnp.exp(sinks_h[h] - maxes[h])) for h in heads]
        probs = [(exps[h] * invs[h]).astype(BF16) for h in heads]
        for p, cols in enumerate(pair_cols):
            cat_ref[:, cols] = (ug[:, cols] * (mixed[p] + bias_ref[:, cols])).astype(BF16)
        for p in range(D_ATTN // LANES):
            out = _dot(probs[2 * p], v_var[p // 2][0]) + _dot(probs[2 * p + 1], v_var[p // 2][1])
            cat_ref[:, D_GMLP + p * LANES : D_GMLP + (p + 1) * LANES] = out.astype(BF16)

    return pl.pallas_call(
        body,
        name="mixer_fwd",
        grid=(t // CHUNK,),
        in_specs=_chunk_specs()
        + [
            _const_spec((1, D_GMLP)),
            _const_spec((1, D_GMLP)),
            _const_spec((N_HEADS, CHUNK, CHUNK)),
            _const_spec((CHUNK, D_GMLP)),
            pl.BlockSpec(memory_space=pltpu.SMEM),
        ],
        out_specs=pl.BlockSpec((CHUNK, D_MODEL), lambda i: (i, 0)),
        out_shape=jax.ShapeDtypeStruct((t, D_MODEL), BF16),
        scratch_shapes=[pltpu.VMEM((D_GMLP // LANES, CHUNK, 2 * CHUNK), BF16)],
        compiler_params=_params(("arbitrary",)),
    )(u, vg, q, k, k, va, va, v_ln_g, v_ln_b, w_spatial, bias_full, sinks)


def _ffn_fwd_loss(cat, x, w_out, ln1_g, ln1_b, w1, w2, ln2_g, ln2_b, target):
    t = x.shape[0]

    def body(cat_ref, x_ref, wo_ref, g1_ref, b1_ref, w1_ref, w2_ref, g2_ref, b2_ref, tgt_ref,
             xh_ref, rstd_ref, x1b_ref, r_ref, dz2_ref, dz2b_ref, dg2_ref, db2_ref, sq_ref):
        @pl.when(pl.program_id(0) == 0)
        def _():
            dg2_ref[...] = jnp.zeros_like(dg2_ref)
            db2_ref[...] = jnp.zeros_like(db2_ref)
            sq_ref[...] = jnp.zeros_like(sq_ref)

        xhat1, rstd1 = _layer_norm_stats(ALPHA * x_ref[...] + _dot(cat_ref[...], wo_ref[...]))
        xh_ref[...] = xhat1
        rstd_ref[...] = rstd1
        x1 = xhat1 * g1_ref[...] + b1_ref[...]
        x1b = x1.astype(BF16)
        x1b_ref[...] = x1b
        ff = jnp.zeros((TM_FFN, D_MODEL), F32)
        for j in range(N_FF_BLOCKS):
            r = jnp.maximum(_dot(x1b, w1_ref[j]), 0.0)
            r_ref[:, j * D_MODEL : (j + 1) * D_MODEL] = r.astype(BF16)
            ff = ff + _dot((r * r).astype(BF16), w2_ref[j])
        xhat2, rstd2 = _layer_norm_stats(ALPHA * x1 + ff)
        err = xhat2 * g2_ref[...] + b2_ref[...] - tgt_ref[...]
        sq_ref[...] += jnp.sum(err * err, axis=0, keepdims=True)
        dy = err * (1.0 / D_MODEL)
        dg2_ref[...] += jnp.sum(dy * xhat2, axis=0, keepdims=True)
        db2_ref[...] += jnp.sum(dy, axis=0, keepdims=True)
        dz2 = _layer_norm_bwd(dy * g2_ref[...], xhat2, rstd2)
        dz2_ref[...] = dz2
        dz2b_ref[...] = dz2.astype(BF16)

    vec = _const_spec((1, D_MODEL))
    tile = _row_spec(TM_FFN, D_MODEL)
    wspec = _const_spec((N_FF_BLOCKS, D_MODEL, D_MODEL), single_buffer=True)
    return pl.pallas_call(
        body,
        name="ffn_fwd_loss",
        grid=(t // TM_FFN,),
        in_specs=[tile, tile, _const_spec((D_MODEL, D_MODEL), single_buffer=True), vec, vec, wspec, wspec, vec, vec, tile],
        out_specs=[tile, _row_spec(TM_FFN, 1), tile, _row_spec(TM_FFN, D_FF), tile, tile, vec, vec, vec],
        out_shape=[
            jax.ShapeDtypeStruct((t, D_MODEL), F32),
            jax.ShapeDtypeStruct((t, 1), F32),
            jax.ShapeDtypeStruct((t, D_MODEL), BF16),
            jax.ShapeDtypeStruct((t, D_FF), BF16),
            jax.ShapeDtypeStruct((t, D_MODEL), F32),
            jax.ShapeDtypeStruct((t, D_MODEL), BF16),
            jax.ShapeDtypeStruct((1, D_MODEL), F32),
            jax.ShapeDtypeStruct((1, D_MODEL), F32),
            jax.ShapeDtypeStruct((1, D_MODEL), F32),
        ],
        compiler_params=_params(("arbitrary",)),
    )(cat, x, w_out, ln1_g, ln1_b, w1, w2, ln2_g, ln2_b, target)


def _ffn_bwd_ln1(dz2, r, xhat1, rstd1, ln1_g, w1, w2, w_out, dep=None):
    t = dz2.shape[0]

    def body(dz2_ref, r_ref, xh_ref, rstd_ref, g1_ref, w1_ref, w2_ref, wo_ref, dpre_ref, dz1_ref, dz1b_ref, dcat_ref, dg1_ref, db1_ref):
        @pl.when(pl.program_id(0) == 0)
        def _():
            dg1_ref[...] = jnp.zeros_like(dg1_ref)
            db1_ref[...] = jnp.zeros_like(db1_ref)

        dz2 = dz2_ref[...]
        dz2b = dz2.astype(BF16)
        dx1 = ALPHA * dz2
        for j in range(N_FF_BLOCKS):
            cols = slice(j * D_MODEL, (j + 1) * D_MODEL)
            dpre = (_dot(dz2b, w2_ref[j], NT) * (2.0 * r_ref[:, cols].astype(F32))).astype(BF16)
            dpre_ref[:, cols] = dpre
            dx1 = dx1 + _dot(dpre, w1_ref[j], NT)
        xhat1 = xh_ref[...]
        dg1_ref[...] += jnp.sum(dx1 * xhat1, axis=0, keepdims=True)
        db1_ref[...] += jnp.sum(dx1, axis=0, keepdims=True)
        dz1 = _layer_norm_bwd(dx1 * g1_ref[...], xhat1, rstd_ref[...])
        dz1_ref[...] = dz1
        dz1b = dz1.astype(BF16)
        dz1b_ref[...] = dz1b
        dcat_ref[...] = _dot(dz1b, wo_ref[...], NT).astype(BF16)

    vec = _const_spec((1, D_MODEL))
    tile = _row_spec(TM_FFN, D_MODEL)
    wspec = _const_spec((N_FF_BLOCKS, D_MODEL, D_MODEL), single_buffer=True)
    body, in_specs, operands = _after(
        dep, body,
        [tile, _row_spec(TM_FFN, D_FF), tile, _row_spec(TM_FFN, 1), vec, wspec, wspec, _const_spec((D_MODEL, D_MODEL), single_buffer=True)],
        [dz2, r, xhat1, rstd1, ln1_g, w1, w2, w_out])
    return pl.pallas_call(
        body,
        name="ffn_bwd_ln1",
        grid=(t // TM_FFN,),
        in_specs=in_specs,
        out_specs=[_row_spec(TM_FFN, D_FF), tile, tile, tile, vec, vec],
        out_shape=[
            jax.ShapeDtypeStruct((t, D_FF), BF16),
            jax.ShapeDtypeStruct((t, D_MODEL), F32),
            jax.ShapeDtypeStruct((t, D_MODEL), BF16),
            jax.ShapeDtypeStruct((t, D_MODEL), BF16),
            jax.ShapeDtypeStruct((1, D_MODEL), F32),
            jax.ShapeDtypeStruct((1, D_MODEL), F32),
        ],
        compiler_params=_params(("arbitrary",)),
    )(*operands)


def _mixer_bwd(u, vg, q, k, va, dcat, cos, sin, v_ln_g, v_ln_b, w_spatial, bias_full, sinks, dep=None):
    t = u.shape[0]
    n_chunks = t // CHUNK

    def body(u_ref, vg_ref, q_ref, kc_ref, kp_ref, vc_ref, vp_ref, dcat_ref, cosc_ref, sinc_ref, cosp_ref, sinp_ref,
             g_ref, b_ref, w_ref, bias_ref, sink_ref,
             dmain_ref, dkv_ref, dg_ref, db_ref, dw_ref, dbs_ref, dsink_ref, dmix_acc, wcat, wcat_t):
        i = pl.program_id(0)
        left = _half_lane_masks(CHUNK)
        lane = lax.broadcasted_iota(jnp.int32, (CHUNK, LANES), 1)
        n_pairs = D_GMLP // LANES

        @pl.when(i == 0)
        def _():
            dg_ref[...] = jnp.zeros_like(dg_ref)
            db_ref[...] = jnp.zeros_like(db_ref)
            dw_ref[...] = jnp.zeros_like(dw_ref)
            dsink_ref[...] = jnp.zeros_like(dsink_ref)
            dmix_acc[...] = jnp.zeros_like(dmix_acc)
            _store_spatial_weights(w_ref, wcat, wcat_t)

        n_qpairs = D_ATTN // LANES
        heads = range(N_HEADS)
        pair_cols = [slice(p * LANES, (p + 1) * LANES) for p in range(n_pairs)]

        k_var = _kv_variants(jnp.concatenate([kp_ref[...], kc_ref[...]], axis=0))
        v_var = _kv_variants(jnp.concatenate([vp_ref[...], vc_ref[...]], axis=0))
        q_pairs = [q_ref[:, cols] for cols in pair_cols]
        do_all = dcat_ref[:, D_GMLP:D_MODEL]
        do_pairs = [do_all[:, cols] for cols in pair_cols]
        scores = [_dot(q_pairs[h // 2], k_var[h // 4][h % 2], NT) for h in heads]
        dprobs = [_dot(do_pairs[h // 2], v_var[h // 4][h % 2], NT) for h in heads]
        q_t = q_ref[...].astype(F32).T.astype(BF16)
        do_t = do_all.astype(F32).T.astype(BF16)

        ug, dug_du = _gelu_and_grad(u_ref[...])
        gv, dgv_dv = _gelu_and_grad(vg_ref[...])
        xhat, rstd = _layer_norm_stats(gv)
        gain = g_ref[...]
        vgl = xhat * gain + b_ref[...]
        mixed = [_dot(wcat[p], _pair_stack(vgl[:, cols], left)) for p, cols in enumerate(pair_cols)]

        valid = _band_mask(i)
        sinks_h = [sink_ref[h] for h in heads]
        masked = [jnp.where(valid, scores[h] * SCALE, NEG_INF) for h in heads]
        maxes = [jnp.maximum(jnp.max(masked[h], axis=1, keepdims=True), sinks_h[h]) for h in heads]
        exps = [jnp.exp(masked[h] - maxes[h]) for h in heads]
        exp_sinks = [jnp.exp(sinks_h[h] - maxes[h]) for h in heads]
        invs = [1.0 / (jnp.sum(exps[h], axis=1, keepdims=True) + exp_sinks[h]) for h in heads]
        probs = [exps[h] * invs[h] for h in heads]
        dsums = [jnp.sum(probs[h] * dprobs[h], axis=1, keepdims=True) for h in heads]
        ds_b = [(probs[h] * (dprobs[h] - dsums[h]) * SCALE).astype(BF16) for h in heads]
        probs_b = [probs[h].astype(BF16) for h in heads]

        dm_stacks = []
        for p, cols in enumerate(pair_cols):
            da = dcat_ref[:, cols].astype(F32)
            dmain_ref[:, cols] = (da * (mixed[p] + bias_ref[:, cols]) * dug_du[:, cols]).astype(BF16)
            dmixed = da * ug[:, cols]
            dmix_acc[:, cols] += dmixed
            dm_stacks.append(_pair_stack(dmixed, left))

        dq_all = jnp.concatenate(
            [_dot(ds_b[2 * p], k_var[p // 2][0]) + _dot(ds_b[2 * p + 1], k_var[p // 2][1]) for p in range(n_qpairs)], axis=1)
        heads_per_group = N_HEADS // 2

        def group_grad_t(lhs_t, rhs_heads):
            parts = []
            for g in range(2):
                group = range(g * heads_per_group, (g + 1) * heads_per_group)
                lhs = jnp.concatenate([lhs_t[h * HEAD_DIM : (h + 1) * HEAD_DIM] for h in group], axis=1)
                parts.append(_dot(lhs, jnp.concatenate([rhs_heads[h] for h in group], axis=0)))
            return jnp.concatenate(parts, axis=0)

        dk2_t = group_grad_t(q_t, ds_b)
        dv2_t = group_grad_t(do_t, probs_b)

        causal = _causal_mask()
        for p, cols in enumerate(pair_cols):
            dw_pair = _dot(dm_stacks[p], vgl[:, cols].astype(BF16), NT)
            dw_ref[2 * p] += jnp.where(causal, dw_pair[:CHUNK], 0.0)
            dw_ref[2 * p + 1] += jnp.where(causal, dw_pair[CHUNK:], 0.0)
        dvgl = jnp.concatenate([_dot(wcat_t[p], dm_stacks[p]) for p in range(n_pairs)], axis=1)

        dsink_row = jnp.zeros((1, LANES), F32)
        lane_row = lax.broadcasted_iota(jnp.int32, (1, LANES), 1)
        for h in heads:
            d_sink = -jnp.sum(exp_sinks[h] * invs[h] * dsums[h], axis=0, keepdims=True)
            dsink_row = dsink_row + jnp.where(lane_row == h, d_sink, 0.0)
        dsink_ref[0:1, :] += dsink_row
        cos_c, sin_c = cosc_ref[...], sinc_ref[...]
        dmain_ref[:, 2 * D_GMLP : D_MAIN] = _rope_transposed(dq_all, _lane_tile(cos_c, n_qpairs), _lane_tile(sin_c, n_qpairs)).astype(BF16)
        dk2 = dk2_t.T
        dv2 = dv2_t.T
        cur = pl.ds(pl.multiple_of(i * CHUNK, CHUNK), CHUNK)
        dkv_ref[cur, 0:D_KV] = _rope_transposed(dk2[CHUNK:], cos_c, sin_c)
        dkv_ref[cur, D_KV : 2 * D_KV] = dv2[CHUNK:]

        dg_ref[...] += jnp.sum(dvgl * xhat, axis=0, keepdims=True)
        db_ref[...] += jnp.sum(dvgl, axis=0, keepdims=True)
        dgv = _layer_norm_bwd(dvgl * gain, xhat, rstd)
        dmain_ref[:, D_GMLP : 2 * D_GMLP] = (dgv * dgv_dv).astype(BF16)

        @pl.when(i > 0)
        def _():
            prev = pl.ds(pl.multiple_of((i - 1) * CHUNK, CHUNK), CHUNK)
            dkv_ref[prev, 0:D_KV] += _rope_transposed(dk2[:CHUNK], cosp_ref[...], sinp_ref[...])
            dkv_ref[prev, D_KV : 2 * D_KV] += dv2[:CHUNK]

        @pl.when(i == n_chunks - 1)
        def _():
            tile = jnp.zeros((CHUNK, LANES), F32)
            for p, cols in enumerate(pair_cols):
                dm = dmix_acc[:, cols]
                sl = jnp.sum(jnp.where(left, dm, 0.0), axis=1, keepdims=True)
                sr = jnp.sum(jnp.where(left, 0.0, dm), axis=1, keepdims=True)
                tile = jnp.where(lane == 2 * p, sl, tile)
                tile = jnp.where(lane == 2 * p + 1, sr, tile)
            dbs_ref[...] = tile

    cur = lambda i: (i, 0)
    prev = lambda i: (jnp.maximum(i - 1, 0), 0)
    in_specs = _chunk_specs() + [
        pl.BlockSpec((CHUNK, D_MODEL), cur),
        pl.BlockSpec((CHUNK, LANES), cur),
        pl.BlockSpec((CHUNK, LANES), cur),
        pl.BlockSpec((CHUNK, LANES), prev),
        pl.BlockSpec((CHUNK, LANES), prev),
        _const_spec((1, D_GMLP)),
        _const_spec((1, D_GMLP)),
        _const_spec((N_HEADS, CHUNK, CHUNK)),
        _const_spec((CHUNK, D_GMLP)),
        pl.BlockSpec(memory_space=pltpu.SMEM),
    ]
    body, in_specs, operands = _after(
        dep, body, in_specs, [u, vg, q, k, k, va, va, dcat, cos, sin, cos, sin, v_ln_g, v_ln_b, w_spatial, bias_full, sinks])
    return pl.pallas_call(
        body,
        name="mixer_bwd",
        grid=(n_chunks,),
        in_specs=in_specs,
        out_specs=[
            pl.BlockSpec((CHUNK, D_MAIN), cur),
            _const_spec((t, 2 * D_KV)),
            _const_spec((1, D_GMLP)),
            _const_spec((1, D_GMLP)),
            _const_spec((N_HEADS, CHUNK, CHUNK)),
            _const_spec((CHUNK, LANES)),
            _const_spec((8, LANES)),
        ],
        out_shape=[
            jax.ShapeDtypeStruct((t, D_MAIN), BF16),
            jax.ShapeDtypeStruct((t, 2 * D_KV), F32),
            jax.ShapeDtypeStruct((1, D_GMLP), F32),
            jax.ShapeDtypeStruct((1, D_GMLP), F32),
            jax.ShapeDtypeStruct((N_HEADS, CHUNK, CHUNK), F32),
            jax.ShapeDtypeStruct((CHUNK, LANES), F32),
            jax.ShapeDtypeStruct((8, LANES), F32),
        ],
        scratch_shapes=[
            pltpu.VMEM((CHUNK, D_GMLP), F32),
            pltpu.VMEM((D_GMLP // LANES, CHUNK, 2 * CHUNK), BF16),
            pltpu.VMEM((D_GMLP // LANES, CHUNK, 2 * CHUNK), BF16),
        ],
        compiler_params=_params(("arbitrary",)),
    )(*operands)


def _grad_x(dh_main, dkv, dz1, w_in_t, dep=None):
    t = dz1.shape[0]

    def body(dm_ref, dkv_ref, dz1_ref, w_ref, gx_ref):
        acc = ALPHA * dz1_ref[...] + _dot(dm_ref[...], w_ref[0:D_MAIN, :])
        gx_ref[...] = acc + _dot(dkv_ref[...].astype(BF16), w_ref[D_MAIN:D_IN, :])

    body, in_specs, operands = _after(
        dep, body, [_row_spec(TM, D_MAIN), _row_spec(TM, 2 * D_KV), _row_spec(TM, D_MODEL), _const_spec((D_IN, D_MODEL))], [dh_main, dkv, dz1, w_in_t])
    return pl.pallas_call(
        body,
        name="grad_x",
        grid=(t // TM,),
        in_specs=in_specs,
        out_specs=_row_spec(TM, D_MODEL),
        out_shape=jax.ShapeDtypeStruct((t, D_MODEL), F32),
        compiler_params=_params(("parallel",)),
    )(*operands)


def _token_contraction(name, out_rows, tk, in_arrays, contributions, dep=None):
    t = in_arrays[0].shape[0]

    def body(*refs):
        out_ref = refs[-1]

        @pl.when(pl.program_id(0) == 0)
        def _():
            out_ref[...] = jnp.zeros_like(out_ref)

        for row0, a, b in contributions(*refs[:-1]):
            out_ref[row0 : row0 + a.shape[1], :] += _dot(a, b, TN)

    in_specs = [_row_spec(tk, a.shape[1]) for a in in_arrays]
    body, in_specs, operands = _after(dep, body, in_specs, in_arrays)
    return pl.pallas_call(
        body,
        name=name,
        grid=(t // tk,),
        in_specs=in_specs,
        out_specs=_const_spec((out_rows, D_MODEL), single_buffer=True),
        out_shape=jax.ShapeDtypeStruct((out_rows, D_MODEL), F32),
        compiler_params=_params(("arbitrary",)),
    )(*operands)


def _grad_w_in_t(dh_main, dkv, x, dep=None):
    def contributions(dm_ref, dkv_ref, x_ref):
        xb = x_ref[...].astype(BF16)
        return [(0, dm_ref[...], xb), (D_MAIN, dkv_ref[...].astype(BF16), xb)]

    return _token_contraction("grad_w_in", D_IN, TK, [dh_main, dkv, x], contributions, dep)


def _grad_w_out(cat, dz1b, dep=None):
    def contributions(cat_ref, dz1_ref):
        return [(0, cat_ref[...], dz1_ref[...])]

    return _token_contraction("grad_w_out", D_MODEL, TK, [cat, dz1b], contributions, dep)


def _grad_w_ff1(x1b, dpre):
    def contributions(x1_ref, dpre_ref):
        x1 = x1_ref[...]
        return [(j * D_MODEL, x1, dpre_ref[:, j * D_MODEL : (j + 1) * D_MODEL]) for j in range(N_FF_BLOCKS)]

    return _token_contraction("grad_w_ff1", D_FF, TK_FF, [x1b, dpre], contributions)


def _grad_w_ff2(r, dz2b):
    def contributions(r_ref, dz2_ref):
        dz2 = dz2_ref[...]
        out = []
        for j in range(N_FF_BLOCKS):
            rf = r_ref[:, j * D_MODEL : (j + 1) * D_MODEL].astype(F32)
            out.append((j * D_MODEL, (rf * rf).astype(BF16), dz2))
        return out

    return _token_contraction("grad_w_ff2", D_FF, TK_FF, [r, dz2b], contributions)


ANY = pl.BlockSpec(memory_space=pl.ANY)


def _mesh_position():
    return lax.axis_index("x"), lax.axis_index("y"), lax.axis_index("c")


def _other_chips(x, y):
    return [(1 - x, y), (x, 1 - y), (1 - x, 1 - y)]


def _remote(src, dst, send_sem, recv_sem, device):
    return pltpu.make_async_remote_copy(src_ref=src, dst_ref=dst, send_sem=send_sem, recv_sem=recv_sem, device_id=device, device_id_type=MESH)


def _rows(ref, start, size):
    return ref.at[pl.ds(start, size), :]


def _all_gather_weights(shards):
    n = len(shards)
    per = 7

    def body(*refs):
        ins, outs = refs[:n], refs[n : 2 * n]
        send_sems, recv_sems = refs[2 * n :]
        x, y, c = _mesh_position()
        me = 2 * x + y
        chips = _other_chips(x, y)
        sibling = (x, y, 1 - c)
        started = []
        for w in range(n):
            rows = shards[w].shape[0]
            half = rows // 2
            for kk, (px, py) in enumerate(chips):
                cp = _remote(_rows(ins[w], c * half, half), _rows(outs[w], me * rows + c * half, half),
                             send_sems.at[per * w + kk], recv_sems.at[per * w + kk], (px, py, c))
                cp.start()
                started.append(cp)
            cp = _remote(ins[w], _rows(outs[w], me * rows, rows), send_sems.at[per * w + 6], recv_sems.at[per * w + 6], sibling)
            cp.start()
            started.append(cp)
        for w in range(n):
            rows = shards[w].shape[0]
            half = rows // 2
            for kk, (px, py) in enumerate(chips):
                blk = _rows(outs[w], (2 * px + py) * rows + c * half, half)
                _remote(blk, blk, send_sems.at[per * w + kk], recv_sems.at[per * w + kk], (px, py, c)).wait_recv()
                fwd = _remote(blk, blk, send_sems.at[per * w + 3 + kk], recv_sems.at[per * w + 3 + kk], sibling)
                fwd.start()
                started.append(fwd)
        for w in range(n):
            rows = shards[w].shape[0]
            half = rows // 2
            for kk, (px, py) in enumerate(chips):
                blk = _rows(outs[w], (2 * px + py) * rows + (1 - c) * half, half)
                _remote(blk, blk, send_sems.at[per * w + 3 + kk], recv_sems.at[per * w + 3 + kk], sibling).wait_recv()
            own = _rows(outs[w], me * rows, rows)
            _remote(own, own, send_sems.at[per * w + 6], recv_sems.at[per * w + 6], sibling).wait_recv()
        for cp in started:
            cp.wait_send()

    return pl.pallas_call(
        body,
        name="all_gather_weights",
        in_specs=[ANY] * n,
        out_specs=[ANY] * n,
        out_shape=[jax.ShapeDtypeStruct((N_CHIPS * s.shape[0], s.shape[1]), s.dtype) for s in shards],
        scratch_shapes=[pltpu.SemaphoreType.DMA((per * n,)), pltpu.SemaphoreType.DMA((per * n,))],
    )(*shards)


def _pair_gather(name, shards):
    n = len(shards)

    def body(*refs):
        outs = refs[n : 2 * n]
        send_sems, recv_sems = refs[2 * n :]
        x, y, c = _mesh_position()
        sibling = (x, y, 1 - c)
        sends = []
        for w in range(n):
            half = shards[w].shape[0] // 2
            mine = _rows(outs[w], c * half, half)
            cp = _remote(mine, mine, send_sems.at[w], recv_sems.at[w], sibling)
            cp.start()
            sends.append(cp)
        for w in range(n):
            half = shards[w].shape[0] // 2
            blk = _rows(outs[w], (1 - c) * half, half)
            _remote(blk, blk, send_sems.at[w], recv_sems.at[w], sibling).wait_recv()
        for cp in sends:
            cp.wait_send()

    return pl.pallas_call(
        body,
        name=name,
        in_specs=[ANY] * n,
        out_specs=[ANY] * n,
        out_shape=[jax.ShapeDtypeStruct(s.shape, s.dtype) for s in shards],
        input_output_aliases={w: w for w in range(n)},
        scratch_shapes=[pltpu.SemaphoreType.DMA((n,)), pltpu.SemaphoreType.DMA((n,))],
    )(*shards)


def _all_reduce_small(slab, dep=None):
    rows = slab.shape[0]
    part = rows // 8

    def body(slab_ref, out_ref, landing, reduced, send_sems, recv_sems):
        x, y, c = _mesh_position()
        me = 4 * x + 2 * y + c
        flips = [(k >> 2, (k >> 1) & 1, k & 1) for k in range(1, 8)]

        def peer(flip):
            fx, fy, fc = flip
            return (1 - x if fx else x, 1 - y if fy else y, 1 - c if fc else c)

        def my_rows(ref):
            return ref.at[pl.ds(pl.multiple_of(me * part, 8), part), :]

        sends = []
        for kk, flip in enumerate(flips):
            px, py, pc = peer(flip)
            them = 4 * px + 2 * py + pc
            cp = _remote(slab_ref.at[pl.ds(pl.multiple_of(them * part, 8), part), :], landing.at[me], send_sems.at[kk], recv_sems.at[kk], (px, py, pc))
            cp.start()
            sends.append(cp)
        landing[me] = my_rows(slab_ref)[...]
        for kk, flip in enumerate(flips):
            px, py, pc = peer(flip)
            them = 4 * px + 2 * py + pc
            _remote(landing.at[them], landing.at[them], send_sems.at[kk], recv_sems.at[kk], (px, py, pc)).wait_recv()
        total = landing[0]
        for s in range(1, 8):
            total = total + landing[s]
        reduced[...] = total
        my_rows(out_ref)[...] = total
        for kk, flip in enumerate(flips):
            cp = _remote(reduced, my_rows(out_ref), send_sems.at[7 + kk], recv_sems.at[7 + kk], peer(flip))
            cp.start()
            sends.append(cp)
        for kk, flip in enumerate(flips):
            px, py, pc = peer(flip)
            them = 4 * px + 2 * py + pc
            blk = out_ref.at[pl.ds(pl.multiple_of(them * part, 8), part), :]
            _remote(blk, blk, send_sems.at[7 + kk], recv_sems.at[7 + kk], (px, py, pc)).wait_recv()
        for cp in sends:
            cp.wait_send()

    vmem = pl.BlockSpec(memory_space=pltpu.VMEM)
    body, in_specs, operands = _after(dep, body, [vmem], [slab])
    return pl.pallas_call(
        body,
        name="all_reduce_small",
        in_specs=in_specs,
        out_specs=vmem,
        out_shape=jax.ShapeDtypeStruct(slab.shape, slab.dtype),
        scratch_shapes=[pltpu.VMEM((8, part, LANES), F32), pltpu.VMEM((part, LANES), F32), pltpu.SemaphoreType.DMA((14,)), pltpu.SemaphoreType.DMA((14,))],
    )(*operands)


HBM = pl.BlockSpec(memory_space=pltpu.HBM)
SEM = pl.BlockSpec(memory_space=pltpu.SEMAPHORE)
DATAFLOW = pltpu.SideEffectType.DATAFLOW_SIDE_EFFECTING
TOKEN = jax.ShapeDtypeStruct((8, LANES), F32)


def _plan_copies(bufs, plan, send_sems, recv_sems):
    out = []
    for i, (src, src_row, dst, dst_row, recv_row, rows, device) in enumerate(plan):
        send = _remote(_rows(bufs[src], src_row, rows), _rows(bufs[dst], dst_row, rows), send_sems.at[i], recv_sems.at[i], device)
        landed = _rows(bufs[dst], recv_row, rows)
        recv = _remote(landed, landed, send_sems.at[i], recv_sems.at[i], device)
        out.append((send, recv))
    return out


def _split_call(name, bufs, wait=None, start=None, after=None):
    n = len(bufs)
    n_in = n + (2 if wait else 0) + (1 if after is not None else 0)
    n_start = len(start(0, 0, 0)) if start else 0

    def body(*refs):
        ins = refs[:n]
        x, y, c = _mesh_position()
        if wait:
            for send, recv in _plan_copies(ins, wait[0](x, y, c), refs[n], refs[n + 1]):
                recv.wait_recv()
                send.wait_send()
        if start:
            for send, _ in _plan_copies(ins, start(x, y, c), refs[n_in + n + 1], refs[n_in + n + 2]):
                send.start()
        token = refs[n_in + n]
        token[...] = jnp.zeros_like(token)

    operands = [pltpu.with_memory_space_constraint(b, pltpu.HBM) for b in bufs]
    in_specs = [HBM] * n
    if wait:
        operands += [wait[1], wait[2]]
        in_specs += [SEM, SEM]
    if after is not None:
        operands.append(after)
        in_specs.append(ANY)
    out_shape = [pltpu.HBM(b.shape, b.dtype) for b in bufs] + [TOKEN]
    out_specs = [HBM] * n + [pl.BlockSpec(memory_space=pltpu.VMEM)]
    if start:
        out_shape += [pltpu.SemaphoreType.DMA((n_start,)), pltpu.SemaphoreType.DMA((n_start,))]
        out_specs += [SEM, SEM]
    outs = pl.pallas_call(
        body,
        name=name,
        in_specs=in_specs,
        out_specs=out_specs,
        out_shape=out_shape,
        input_output_aliases={i: i for i in range(n)},
        compiler_params=pltpu.CompilerParams(has_side_effects=DATAFLOW),
    )(*operands)
    return (list(outs[:n]), outs[n]) + tuple(outs[n + 1 :])


def _gather_plans(shard_rows):
    n = len(shard_rows)

    def ici(x, y, c):
        me = 2 * x + y
        plan = []
        for w, rows in enumerate(shard_rows):
            half = rows // 2
            for px, py in _other_chips(x, y):
                plan.append((w, c * half, n + w, me * rows + c * half, (2 * px + py) * rows + c * half, half, (px, py, c)))
            plan.append((w, 0, n + w, me * rows, me * rows, rows, (x, y, 1 - c)))
        return plan

    def passed_on(x, y, c):
        plan = []
        for w, rows in enumerate(shard_rows):
            half = rows // 2
            for px, py in _other_chips(x, y):
                row = (2 * px + py) * rows
                plan.append((n + w, row + c * half, n + w, row + c * half, row + (1 - c) * half, half, (x, y, 1 - c)))
        return plan

    return ici, passed_on


def _swap_plan(block_rows):
    n = len(block_rows)

    def plan_fn(x, y, c):
        plan = []
        for w, rows in enumerate(block_rows):
            half = rows // 2
            for j in range(N_CHIPS):
                plan.append((w, j * rows + (1 - c) * half, n + w, j * half, j * half, half, (x, y, 1 - c)))
        return plan

    return plan_fn


def _exchange_plan(halves):
    n = len(halves)

    def plan_fn(x, y, c):
        plan = []
        for w, half in enumerate(halves):
            for kk, (px, py) in enumerate(_other_chips(x, y)):
                plan.append((w, (2 * px + py) * half, n + w, kk * half, kk * half, half, (px, py, c)))
        return plan

    return plan_fn


def _landing(rows, cols, dtype):
    return lax.empty((rows, cols), dtype)


def _row_tile(rows, cap=512):
    best = 8
    for cand in range(8, cap + 1, 8):
        if rows % cand == 0:
            best = cand
    return best


def _pair_sum(name, grad, theirs, pos):
    half = theirs.shape[0] // N_CHIPS
    cols = theirs.shape[1]
    tile = _row_tile(half)
    steps = half // tile

    def body(pos_ref, g_ref, t_ref, p_ref, own_ref):
        total = g_ref[...] + t_ref[...]
        p_ref[...] = total.astype(BF16)

        @pl.when(pl.program_id(1) == pos_ref[1])
        def _():
            own_ref[...] = total

    return pl.pallas_call(
        body,
        name=name,
        grid_spec=pltpu.PrefetchScalarGridSpec(
            num_scalar_prefetch=1,
            grid=(steps, N_CHIPS),
            in_specs=[
                pl.BlockSpec((tile, cols), lambda i, j, pos: ((2 * j + pos[0]) * steps + i, 0)),
                pl.BlockSpec((tile, cols), lambda i, j, pos: (j * steps + i, 0)),
            ],
            out_specs=[
                pl.BlockSpec((tile, cols), lambda i, j, pos: (j * steps + i, 0)),
                pl.BlockSpec((tile, cols), lambda i, j, pos: (i, 0)),
            ],
        ),
        out_shape=[jax.ShapeDtypeStruct((N_CHIPS * half, cols), BF16), jax.ShapeDtypeStruct((half, cols), F32)],
        compiler_params=_params(("parallel", "arbitrary")),
    )(pos, grad, theirs)


def _chip_sum(name, own, landed, pos):
    half, cols = own.shape
    tile = _row_tile(half)
    steps = half // tile

    def body(pos_ref, own_ref, l0, l1, l2, o_ref):
        o_ref[...] = ((own_ref[...] + l0[...].astype(F32)) + l1[...].astype(F32)) + l2[...].astype(F32)

    landed_specs = [pl.BlockSpec((tile, cols), lambda i, pos, _k=k: (_k * steps + i, 0)) for k in range(N_CHIPS - 1)]
    return pl.pallas_call(
        body,
        name=name,
        grid_spec=pltpu.PrefetchScalarGridSpec(
            num_scalar_prefetch=1,
            grid=(steps,),
            in_specs=[pl.BlockSpec((tile, cols), lambda i, pos: (i, 0))] + landed_specs,
            out_specs=pl.BlockSpec((tile, cols), lambda i, pos: (pos[0] * steps + i, 0)),
        ),
        out_shape=jax.ShapeDtypeStruct((2 * half, cols), F32),
        compiler_params=_params(("parallel",)),
    )(pos, own, landed, landed, landed)


def _adamw(name, w, g, m, v):
    rows, cols = w.shape
    tile = rows if rows * cols <= 256 * 1024 else _row_tile(rows)

    def body(w_ref, g_ref, m_ref, v_ref, g_out_ref, d_ref, nm_ref, nv_ref):
        g = g_ref[...]
        g_out_ref[...] = g
        nm = ADAM_B1 * m_ref[...] + (1.0 - ADAM_B1) * g
        nv = ADAM_B2 * v_ref[...] + (1.0 - ADAM_B2) * (g * g)
        m_hat = nm / (1.0 - ADAM_B1**ADAM_STEP)
        v_hat = nv / (1.0 - ADAM_B2**ADAM_STEP)
        d_ref[...] = -ADAM_LR * (m_hat / (jnp.sqrt(v_hat) + ADAM_EPS) + ADAM_WD * w_ref[...])
        nm_ref[...] = nm
        nv_ref[...] = nv

    spec = _row_spec(tile, cols)
    return pl.pallas_call(
        body,
        name=name,
        grid=(rows // tile,),
        in_specs=[spec] * 4,
        out_specs=[spec] * 4,
        out_shape=[jax.ShapeDtypeStruct((rows, cols), F32)] * 4,
        compiler_params=_params(("parallel",)),
    )(w, g, m, v)


_SMALL = (
    ("v_ln_g", (D_GMLP,), 8),
    ("v_ln_b", (D_GMLP,), 8),
    ("w_spatial", (N_HEADS, CHUNK, CHUNK), 1024),
    ("b_spatial", (N_HEADS, CHUNK), 8),
    ("sinks", (N_HEADS,), 8),
    ("ln1_g", (D_MODEL,), 8),
    ("ln1_b", (D_MODEL,), 8),
    ("ln2_g", (D_MODEL,), 8),
    ("ln2_b", (D_MODEL,), 8),
    ("squared_error", (D_MODEL,), 8),
)
N_SMALL_PARAMS = len(_SMALL) - 1


def _pack_small(values):
    parts = []
    for (name, shape, rows), val in zip(_SMALL, values, strict=True):
        flat = val.reshape(-1).astype(F32)
        parts.append(jnp.pad(flat, (0, rows * LANES - flat.shape[0])).reshape(rows, LANES))
    parts.append(jnp.zeros((SMALL_ROWS - sum(rows for _, _, rows in _SMALL), LANES), F32))
    return jnp.concatenate(parts, axis=0)


def _adamw_update(w, g, m, v):
    nm = ADAM_B1 * m + (1.0 - ADAM_B1) * g
    nv = ADAM_B2 * v + (1.0 - ADAM_B2) * (g * g)
    m_hat = nm / (1.0 - ADAM_B1**ADAM_STEP)
    v_hat = nv / (1.0 - ADAM_B2**ADAM_STEP)
    return -ADAM_LR * (m_hat / (jnp.sqrt(v_hat) + ADAM_EPS) + ADAM_WD * w), nm, nv


def _adamw_small(g_slab, params, first, second):
    n = N_SMALL_PARAMS

    def pieces(shape):
        if len(shape) == 3:
            return [((0, h), h * shape[1], shape[1], shape[2]) for h in range(shape[0])]
        if len(shape) == 2:
            return [((0,), 0, shape[0], shape[1])]
        if shape[0] >= LANES:
            return [((slice(None), slice(r * LANES, (r + 1) * LANES)), r, 1, LANES) for r in range(shape[0] // LANES)]
        return [((slice(None), slice(0, shape[0])), 0, 1, shape[0])]

    def body(*refs):
        g_ref = refs[0]
        w_refs, m_refs, v_refs = refs[1 : 1 + n], refs[1 + n : 1 + 2 * n], refs[1 + 2 * n : 1 + 3 * n]
        outs = refs[1 + 3 * n :]
        row0 = 0
        for idx, (_, shape, rows) in enumerate(_SMALL[:n]):
            for where, first_row, n_rows, lanes in pieces(shape):
                g = g_ref[row0 + first_row : row0 + first_row + n_rows, 0:lanes]
                delta, nm, nv = _adamw_update(w_refs[idx][where], g, m_refs[idx][where], v_refs[idx][where])
                for group, val in enumerate((g, delta, nm, nv)):
                    outs[group * n + idx][where] = val
            row0 += rows

    vmem = pl.BlockSpec(memory_space=pltpu.VMEM)
    shapes = [jax.ShapeDtypeStruct(p.shape, F32) for p in params]
    outs = pl.pallas_call(
        body,
        name="adamw_small",
        in_specs=[vmem] * (1 + 3 * n),
        out_specs=[vmem] * (4 * n),
        out_shape=shapes * 4,
        compiler_params=_params(),
    )(g_slab, *params, *first, *second)
    return [list(outs[group * n : (group + 1) * n]) for group in range(4)]


def kernel(x, positions, w_in, v_ln_g, v_ln_b, w_spatial, b_spatial, sinks, w_out, ln1_g, ln1_b, w_ff1, w_ff2, ln2_g, ln2_b, loss_target, m_w_in, m_v_ln_g, m_v_ln_b, m_w_spatial, m_b_spatial, m_sinks, m_w_out, m_ln1_g, m_ln1_b, m_w_ff1, m_w_ff2, m_ln2_g, m_ln2_b, v_w_in, v_v_ln_g, v_v_ln_b, v_w_spatial, v_b_spatial, v_sinks, v_w_out, v_ln1_g, v_ln1_b, v_w_ff1, v_w_ff2, v_ln2_g, v_ln2_b):
    t = x.shape[1]
    x2 = x.reshape(t, D_MODEL)
    target = loss_target.reshape(t, D_MODEL)

    (w_in_t,) = _all_gather_weights([w_in[0].T.astype(BF16)])
    later = [w_out[0].astype(BF16), w_ff1[0].astype(BF16), w_ff2[0].astype(BF16)]
    later_rows = [s.shape[0] for s in later]
    ici_plan, pass_plan = _gather_plans(later_rows)
    bufs, started, ici_send, ici_recv = _split_call(
        "gather_start", later + [_landing(N_CHIPS * r, D_MODEL, BF16) for r in later_rows], start=ici_plan, after=w_in_t)

    inv_freq = ROPE_THETA ** (-jnp.arange(0, HEAD_DIM, 2, dtype=F32) / HEAD_DIM)
    cos, sin = _rope_tables(positions, jnp.tile(inv_freq, LANES // (HEAD_DIM // 2)).reshape(1, LANES))
    u, vg, q, k, va = _in_proj(x2, w_in_t, cos, sin, dep=started)
    bias_full = jnp.repeat(b_spatial[0].T, HEAD_DIM, axis=1)
    sink_vec = sinks.reshape(N_HEADS)
    cat = _mixer_fwd(u, vg, q, k, va, v_ln_g, v_ln_b, w_spatial[0], bias_full, sink_vec)
    bufs, passed, pass_send, pass_recv = _split_call("gather_pass", bufs, wait=(ici_plan, ici_send, ici_recv), start=pass_plan, after=cat)
    bufs, _ = _split_call("gather_end", bufs, wait=(pass_plan, pass_send, pass_recv), after=passed)
    w_out_all = bufs[3]
    w1_all = bufs[4].reshape(N_FF_BLOCKS, D_MODEL, D_MODEL)
    w2_all = bufs[5].reshape(N_FF_BLOCKS, D_MODEL, D_MODEL)
    xhat1, rstd1, x1b, r, dz2, dz2b, d_ln2_g, d_ln2_b, sq_err = _ffn_fwd_loss(
        cat, x2, w_out_all, ln1_g, ln1_b, w1_all, w2_all, ln2_g, ln2_b, target)

    pos = jnp.stack([lax.axis_index("c"), 2 * lax.axis_index("x") + lax.axis_index("y")]).astype(jnp.int32)
    half_landing = lambda g: _landing(g.shape[0] // 2, D_MODEL, F32)
    g_ff2_local = _grad_w_ff2(r, dz2b)
    swap_plan = _swap_plan([D_FF // N_CHIPS])
    ff2_bufs, swapping2, swap2_send, swap2_recv = _split_call("ff2_swap_start", [g_ff2_local, half_landing(g_ff2_local)], start=swap_plan)
    dpre, dz1, dz1b, dcat, d_ln1_g, d_ln1_b = _ffn_bwd_ln1(dz2, r, xhat1, rstd1, ln1_g, w1_all, w2_all, w_out_all, dep=swapping2)
    g_ff1_local = _grad_w_ff1(x1b, dpre)
    ff1_bufs, swapping1, swap1_send, swap1_recv = _split_call("ff1_swap_start", [g_ff1_local, half_landing(g_ff1_local)], start=swap_plan)
    g_out_local = _grad_w_out(cat, dz1b, dep=swapping1)
    ff2_bufs, swapped2 = _split_call("ff2_swap_wait", ff2_bufs, wait=(swap_plan, swap2_send, swap2_recv), after=g_out_local)
    ff1_bufs, _ = _split_call("ff1_swap_wait", ff1_bufs, wait=(swap_plan, swap1_send, swap1_recv), after=swapped2)
    ff_sums = [_pair_sum("grad_pair_sum_w_ff1", ff1_bufs[0], ff1_bufs[1], pos), _pair_sum("grad_pair_sum_w_ff2", ff2_bufs[0], ff2_bufs[1], pos)]
    ff_halves = [p.shape[0] // N_CHIPS for p, _ in ff_sums]
    exchange_plan = _exchange_plan(ff_halves)
    bufs, exchanging, ex_send, ex_recv = _split_call(
        "ff_exchange_start", [p for p, _ in ff_sums] + [_landing(3 * h, D_MODEL, BF16) for h in ff_halves], start=exchange_plan)
    dh_main, dkv, d_v_ln_g, d_v_ln_b, d_w_spatial, d_b_spatial_t, d_sinks = _mixer_bwd(
        u, vg, q, k, va, dcat, cos, sin, v_ln_g, v_ln_b, w_spatial[0], bias_full, sink_vec, dep=exchanging)
    small_g = _all_reduce_small(_pack_small(
        [d_v_ln_g, d_v_ln_b, d_w_spatial, d_b_spatial_t[:, :N_HEADS].T, d_sinks[0, :N_HEADS], d_ln1_g, d_ln1_b, d_ln2_g, d_ln2_b, sq_err]))
    sq_row = sum(rows for _, _, rows in _SMALL[:N_SMALL_PARAMS])
    loss = 0.5 * jnp.sum(small_g[sq_row : sq_row + _SMALL[N_SMALL_PARAMS][2]]) / D_MODEL
    g_in_local = _grad_w_in_t(dh_main, dkv, x2, dep=small_g)

    small = [g_in_local, g_out_local]
    small_swap_plan = _swap_plan([g.shape[0] // N_CHIPS for g in small])
    swap_bufs, small_swapping, ss_send, ss_recv = _split_call(
        "small_swap_start", small + [half_landing(g) for g in small], start=small_swap_plan)
    grad_x_flat = _grad_x(dh_main, dkv, dz1, w_in_t, dep=small_swapping)
    grad_x = grad_x_flat.reshape(1, t, D_MODEL)
    swap_bufs, _ = _split_call("small_swap_wait", swap_bufs, wait=(small_swap_plan, ss_send, ss_recv), after=grad_x_flat)
    pair_sums = [_pair_sum("grad_pair_sum_" + nm, g, th, pos) for nm, g, th in zip(["w_in", "w_out"], swap_bufs[:2], swap_bufs[2:])]
    small_halves = [p.shape[0] // N_CHIPS for p, _ in pair_sums]
    small_plan = _exchange_plan(small_halves)
    small_bufs, small_exchanging, sm_send, sm_recv = _split_call(
        "small_exchange_start", [p for p, _ in pair_sums] + [_landing(3 * h, D_MODEL, BF16) for h in small_halves], start=small_plan)

    bufs, _ = _split_call("ff_exchange_wait", bufs, wait=(exchange_plan, ex_send, ex_recv), after=small_exchanging)
    ff_shards = [_chip_sum("grad_chip_sum_" + nm, own, ld, pos) for nm, (_, own), ld in zip(["w_ff1", "w_ff2"], ff_sums, bufs[2:])]
    g_w_ff1, g_w_ff2 = _pair_gather("grad_pair_gather_ff", ff_shards)

    g_w_ff1, d_w_ff1, nm_w_ff1, nv_w_ff1 = _adamw("adamw_w_ff1", w_ff1[0], g_w_ff1, m_w_ff1[0], v_w_ff1[0])
    g_w_ff2, d_w_ff2, nm_w_ff2, nv_w_ff2 = _adamw("adamw_w_ff2", w_ff2[0], g_w_ff2, m_w_ff2[0], v_w_ff2[0])
    small_bufs, _ = _split_call("small_exchange_wait", small_bufs, wait=(small_plan, sm_send, sm_recv), after=nv_w_ff2)
    shards = [_chip_sum("grad_chip_sum_" + nm, own, ld, pos) for nm, (_, own), ld in zip(["w_in", "w_out"], pair_sums, small_bufs[2:])]
    g_w_in_t, g_w_out = _pair_gather("grad_pair_gather_small", shards)
    g_w_in, d_w_in, nm_w_in, nv_w_in = (a.T for a in _adamw("adamw_w_in", w_in[0].T, g_w_in_t, m_w_in[0].T, v_w_in[0].T))
    g_w_out, d_w_out, nm_w_out, nv_w_out = _adamw("adamw_w_out", w_out[0], g_w_out, m_w_out[0], v_w_out[0])
    small_grads, small_d, small_nm, small_nv = _adamw_small(
        small_g,
        [v_ln_g, v_ln_b, w_spatial, b_spatial, sinks, ln1_g, ln1_b, ln2_g, ln2_b],
        [m_v_ln_g, m_v_ln_b, m_w_spatial, m_b_spatial, m_sinks, m_ln1_g, m_ln1_b, m_ln2_g, m_ln2_b],
        [v_v_ln_g, v_v_ln_b, v_w_spatial, v_b_spatial, v_sinks, v_ln1_g, v_ln1_b, v_ln2_g, v_ln2_b])

    def with_big(small, w_in_v, w_out_v, w_ff1_v, w_ff2_v):
        g_vg, g_vb, g_ws, g_bs, g_sk, g_1g, g_1b, g_2g, g_2b = small
        return [w_in_v[None], g_vg, g_vb, g_ws, g_bs, g_sk, w_out_v[None], g_1g, g_1b, w_ff1_v[None], w_ff2_v[None], g_2g, g_2b]

    return (
        loss,
        grad_x,
        *with_big(small_grads, g_w_in, g_w_out, g_w_ff1, g_w_ff2),
        *with_big(small_d, d_w_in, d_w_out, d_w_ff1, d_w_ff2),
        *with_big(small_nm, nm_w_in, nm_w_out, nm_w_ff1, nm_w_ff2),
        *with_big(small_nv, nv_w_in, nv_w_out, nv_w_ff1, nv_w_ff2),
    )
```

```python
import math

import jax
import jax.numpy as jnp
from jax import lax
from jax.experimental import pallas as pl
from jax.experimental.pallas import tpu as pltpu

F32 = jnp.float32
BF16 = jnp.bfloat16

D_MODEL = 1024
HEAD_DIM = 64
D_GMLP = 512
D_ATTN = 512
D_KV = 128
D_IN = 2 * D_GMLP + D_ATTN + 2 * D_KV
D_MAIN = 2 * D_GMLP + D_ATTN
N_HEADS = 8
CHUNK = 128
CHUNKS_PER_STEP = 4
ROPE_THETA = 10000.0
D_FF = 4 * D_MODEL
N_FF_BLOCKS = 4
LN_EPS = 1e-5
ALPHA = (2.0 * 1) ** 0.25
NEG_INF = -1e30
SCALE = 1.0 / math.sqrt(HEAD_DIM)

ADAM_LR = 0.001
ADAM_B1 = 0.9
ADAM_B2 = 0.999
ADAM_EPS = 1e-08
ADAM_WD = 0.01
ADAM_STEP = 10

N_CHIPS = 4
LANES = 128
V7X_VMEM_BYTES = 64 * 1024 * 1024
VMEM_LIMIT = V7X_VMEM_BYTES - 8 * 1024 * 1024
TM = 512
TM_FFN = 256
TK = 1024
TK_FF = 1024
SMALL_ROWS = 1152
MESH = pl.DeviceIdType.MESH

NT = (((1,), (1,)), ((), ()))
TN = (((0,), (0,)), ((), ()))


def _dot(a, b, dims=None):
    if dims is None:
        return jnp.dot(a, b, preferred_element_type=F32)
    return lax.dot_general(a, b, dims, preferred_element_type=F32)


def _params(semantics=None):
    return pltpu.CompilerParams(dimension_semantics=semantics, vmem_limit_bytes=VMEM_LIMIT)


def _const_spec(shape, single_buffer=False):
    zeros = (0,) * len(shape)
    if single_buffer:
        return pl.BlockSpec(shape, lambda *_: zeros, pipeline_mode=pl.Buffered(1))
    return pl.BlockSpec(shape, lambda *_: zeros)


def _row_spec(rows, cols):
    return pl.BlockSpec((rows, cols), lambda i: (i, 0))


def _after(dep, body, in_specs, operands):
    if dep is None:
        return body, list(in_specs), list(operands)
    return (lambda dep_ref, *refs: body(*refs)), [pl.BlockSpec(memory_space=pl.ANY)] + list(in_specs), [dep] + list(operands)


def _gelu(x):
    k = math.sqrt(2.0 / math.pi)
    return 0.5 * x * (1.0 + jnp.tanh(k * (x + 0.044715 * (x * x * x))))


def _gelu_and_grad(x):
    k = math.sqrt(2.0 / math.pi)
    x2 = x * x
    t = jnp.tanh(k * (x + 0.044715 * (x2 * x)))
    g = 0.5 * x * (1.0 + t)
    dg = 0.5 * (1.0 + t) + 0.5 * x * (1.0 - t * t) * (k * (1.0 + 3.0 * 0.044715 * x2))
    return g, dg


def _layer_norm_stats(z):
    mu = jnp.mean(z, axis=-1, keepdims=True)
    zc = z - mu
    var = jnp.mean(zc * zc, axis=-1, keepdims=True)
    rstd = lax.rsqrt(var + LN_EPS)
    return zc * rstd, rstd


def _layer_norm_bwd(dxhat, xhat, rstd):
    m1 = jnp.mean(dxhat, axis=-1, keepdims=True)
    m2 = jnp.mean(dxhat * xhat, axis=-1, keepdims=True)
    return rstd * (dxhat - m1 - xhat * m2)


def _rotate_half(t):
    n = t.shape[1]
    lane = lax.broadcasted_iota(jnp.int32, t.shape, 1)
    first = (lane & (HEAD_DIM // 2)) == 0
    return jnp.where(first, -pltpu.roll(t, n - HEAD_DIM // 2, 1), pltpu.roll(t, HEAD_DIM // 2, 1))


def _rope(t, cos, sin):
    return t * cos + _rotate_half(t) * sin


def _rope_transposed(g, cos, sin):
    return g * cos - _rotate_half(g * sin)


def _lane_tile(a, reps):
    return jnp.tile(a, (1, reps)) if reps > 1 else a


def _rope_tables(pos_row, inv_freq_row):
    t = pos_row.shape[1]

    def body(pos_ref, f_ref, cos_ref, sin_ref):
        pos_rows = jnp.broadcast_to(pos_ref[...].astype(F32), (LANES, TM)).T
        ang = pos_rows * f_ref[...]
        cos_ref[...] = jnp.cos(ang)
        sin_ref[...] = jnp.sin(ang)

    return pl.pallas_call(
        body,
        name="rope_tables",
        grid=(t // TM,),
        in_specs=[pl.BlockSpec((1, TM), lambda i: (0, i)), _const_spec((1, LANES))],
        out_specs=[_row_spec(TM, LANES), _row_spec(TM, LANES)],
        out_shape=[jax.ShapeDtypeStruct((t, LANES), F32)] * 2,
        compiler_params=_params(("parallel",)),
    )(pos_row, inv_freq_row)


def _in_proj(x, w_in_t, cos, sin, dep=None):
    t = x.shape[0]

    def body(x_ref, w_ref, cos_ref, sin_ref, u_ref, vg_ref, q_ref, k_ref, va_ref):
        xb = x_ref[...].astype(BF16)
        u_ref[...] = _dot(xb, w_ref[0:D_GMLP, :], NT)
        vg_ref[...] = _dot(xb, w_ref[D_GMLP : 2 * D_GMLP, :], NT)
        q = _dot(xb, w_ref[2 * D_GMLP : D_MAIN, :], NT)
        k = _dot(xb, w_ref[D_MAIN : D_MAIN + D_KV, :], NT)
        va_ref[...] = _dot(xb, w_ref[D_MAIN + D_KV : D_IN, :], NT).astype(BF16)
        c, s = cos_ref[...], sin_ref[...]
        q_ref[...] = _rope(q, _lane_tile(c, D_ATTN // LANES), _lane_tile(s, D_ATTN // LANES)).astype(BF16)
        k_ref[...] = _rope(k, c, s).astype(BF16)

    body, in_specs, operands = _after(
        dep, body, [_row_spec(TM, D_MODEL), _const_spec((D_IN, D_MODEL)), _row_spec(TM, LANES), _row_spec(TM, LANES)], [x, w_in_t, cos, sin])
    return pl.pallas_call(
        body,
        name="in_proj",
        grid=(t // TM,),
        in_specs=in_specs,
        out_specs=[_row_spec(TM, D_GMLP), _row_spec(TM, D_GMLP), _row_spec(TM, D_ATTN), _row_spec(TM, D_KV), _row_spec(TM, D_KV)],
        out_shape=[
            jax.ShapeDtypeStruct((t, D_GMLP), F32),
            jax.ShapeDtypeStruct((t, D_GMLP), F32),
            jax.ShapeDtypeStruct((t, D_ATTN), BF16),
            jax.ShapeDtypeStruct((t, D_KV), BF16),
            jax.ShapeDtypeStruct((t, D_KV), BF16),
        ],
        compiler_params=_params(("parallel",)),
    )(*operands)


def _step_rows(i):
    return (i, 0)


def _chunk_before_step(i):
    return (jnp.maximum(CHUNKS_PER_STEP * i - 1, 0), 0)


def _chunk_specs():
    step = CHUNKS_PER_STEP * CHUNK
    return [
        pl.BlockSpec((step, D_GMLP), _step_rows),
        pl.BlockSpec((step, D_GMLP), _step_rows),
        pl.BlockSpec((step, D_ATTN), _step_rows),
        pl.BlockSpec((step, D_KV), _step_rows),
        pl.BlockSpec((CHUNK, D_KV), _chunk_before_step),
        pl.BlockSpec((step, D_KV), _step_rows),
        pl.BlockSpec((CHUNK, D_KV), _chunk_before_step),
    ]


def _half_lane_masks(rows):
    lane = lax.broadcasted_iota(jnp.int32, (rows, LANES), 1)
    return lane < HEAD_DIM


def _kv_variants(kv2):
    left = _half_lane_masks(kv2.shape[0])
    f = kv2.astype(F32)
    swapped = pltpu.roll(f, HEAD_DIM, 1)
    zero = jnp.zeros_like(f)
    g0 = (jnp.where(left, f, zero).astype(BF16), jnp.where(left, zero, swapped).astype(BF16))
    g1 = (jnp.where(left, swapped, zero).astype(BF16), jnp.where(left, zero, f).astype(BF16))
    return (g0, g1)


def _band_mask(i, heads=1):
    row = lax.broadcasted_iota(jnp.int32, (heads * CHUNK, 2 * CHUNK), 0) & (CHUNK - 1)
    col = lax.broadcasted_iota(jnp.int32, (heads * CHUNK, 2 * CHUNK), 1)
    no_prev = jnp.where(i > 0, 0, 4 * CHUNK)
    in_prev = jnp.logical_and(col < CHUNK, (col - row) > no_prev)
    in_cur = jnp.logical_and(col >= CHUNK, (col - CHUNK) <= row)
    return jnp.logical_or(in_prev, in_cur)


def _causal_mask():
    row = lax.broadcasted_iota(jnp.int32, (CHUNK, CHUNK), 0)
    col = lax.broadcasted_iota(jnp.int32, (CHUNK, CHUNK), 1)
    return col <= row


def _store_spatial_weights(w_ref, wcat_ref, wcat_t_ref=None):
    causal = _causal_mask()
    for p in range(D_GMLP // LANES):
        wl = jnp.where(causal, w_ref[2 * p], 0.0)
        wr = jnp.where(causal, w_ref[2 * p + 1], 0.0)
        wcat_ref[p] = jnp.concatenate([wl, wr], axis=1).astype(BF16)
        if wcat_t_ref is not None:
            wcat_t_ref[p] = jnp.concatenate([wl.T, wr.T], axis=1).astype(BF16)


def _pair_stack(xp, left):
    return jnp.concatenate([jnp.where(left, xp, 0.0), jnp.where(left, 0.0, xp)], axis=0).astype(BF16)


def _mixer_fwd(u, vg, q, k, va, v_ln_g, v_ln_b, w_spatial, bias_full, sinks):
    t = u.shape[0]

    def body(u_ref, vg_ref, q_ref, kc_ref, kp_ref, vc_ref, vp_ref, g_ref, b_ref, w_ref, bias_ref, sink_ref, cat_ref, wcat):
        i = pl.program_id(0)
        left = _half_lane_masks(CHUNK)

        @pl.when(i == 0)
        def _():
            _store_spatial_weights(w_ref, wcat)

        heads = range(N_HEADS)
        pair_cols = [slice(p * LANES, (p + 1) * LANES) for p in range(D_GMLP // LANES)]
        sinks_h = [sink_ref[h] for h in heads]
        for c in range(CHUNKS_PER_STEP):
            rows = slice(c * CHUNK, (c + 1) * CHUNK)
            before = slice((c - 1) * CHUNK, c * CHUNK)
            k_prev = kp_ref[...] if c == 0 else kc_ref[before, :]
            v_prev = vp_ref[...] if c == 0 else vc_ref[before, :]
            k_var = _kv_variants(jnp.concatenate([k_prev, kc_ref[rows, :]], axis=0))
            v_var = _kv_variants(jnp.concatenate([v_prev, vc_ref[rows, :]], axis=0))
            scores = [_dot(q_ref[rows, pair_cols[h // 2]], k_var[h // 4][h % 2], NT) for h in heads]

            ug = _gelu(u_ref[rows, :])
            xhat, _ = _layer_norm_stats(_gelu(vg_ref[rows, :]))
            vgl = xhat * g_ref[...] + b_ref[...]
            mixed = [_dot(wcat[p], _pair_stack(vgl[:, cols], left)) for p, cols in enumerate(pair_cols)]

            valid = _band_mask(CHUNKS_PER_STEP * i + c)
            masked = [jnp.where(valid, scores[h] * SCALE, NEG_INF) for h in heads]
            maxes = [jnp.maximum(jnp.max(masked[h], axis=1, keepdims=True), sinks_h[h]) for h in heads]
            exps = [jnp.exp(masked[h] - maxes[h]) for h in heads]
            invs = [1.0 / (jnp.sum(exps[h], axis=1, keepdims=True) + jnp.exp(sinks_h[h] - maxes[h])) for h in heads]
            probs = [(exps[h] * invs[h]).astype(BF16) for h in heads]
            for p, cols in enumerate(pair_cols):
                cat_ref[rows, cols] = (ug[:, cols] * (mixed[p] + bias_ref[:, cols])).astype(BF16)
            for p in range(D_ATTN // LANES):
                out = _dot(probs[2 * p], v_var[p // 2][0]) + _dot(probs[2 * p + 1], v_var[p // 2][1])
                cat_ref[rows, D_GMLP + p * LANES : D_GMLP + (p + 1) * LANES] = out.astype(BF16)

    return pl.pallas_call(
        body,
        name="mixer_fwd",
        grid=(t // (CHUNKS_PER_STEP * CHUNK),),
        in_specs=_chunk_specs()
        + [
            _const_spec((1, D_GMLP)),
            _const_spec((1, D_GMLP)),
            _const_spec((N_HEADS, CHUNK, CHUNK)),
            _const_spec((CHUNK, D_GMLP)),
            pl.BlockSpec(memory_space=pltpu.SMEM),
        ],
        out_specs=pl.BlockSpec((CHUNKS_PER_STEP * CHUNK, D_MODEL), lambda i: (i, 0)),
        out_shape=jax.ShapeDtypeStruct((t, D_MODEL), BF16),
        scratch_shapes=[pltpu.VMEM((D_GMLP // LANES, CHUNK, 2 * CHUNK), BF16)],
        compiler_params=_params(("arbitrary",)),
    )(u, vg, q, k, k, va, va, v_ln_g, v_ln_b, w_spatial, bias_full, sinks)


def _ffn_fwd_loss(cat, x, w_out, ln1_g, ln1_b, w1, w2, ln2_g, ln2_b, target):
    t = x.shape[0]

    def body(cat_ref, x_ref, wo_ref, g1_ref, b1_ref, w1_ref, w2_ref, g2_ref, b2_ref, tgt_ref,
             xh_ref, rstd_ref, x1b_ref, r_ref, dz2_ref, dz2b_ref, dg2_ref, db2_ref, sq_ref):
        @pl.when(pl.program_id(0) == 0)
        def _():
            dg2_ref[...] = jnp.zeros_like(dg2_ref)
            db2_ref[...] = jnp.zeros_like(db2_ref)
            sq_ref[...] = jnp.zeros_like(sq_ref)

        xhat1, rstd1 = _layer_norm_stats(ALPHA * x_ref[...] + _dot(cat_ref[...], wo_ref[...]))
        xh_ref[...] = xhat1
        rstd_ref[...] = rstd1
        x1 = xhat1 * g1_ref[...] + b1_ref[...]
        x1b = x1.astype(BF16)
        x1b_ref[...] = x1b
        ff = jnp.zeros((TM_FFN, D_MODEL), F32)
        for j in range(N_FF_BLOCKS):
            r = jnp.maximum(_dot(x1b, w1_ref[j]), 0.0)
            r_ref[:, j * D_MODEL : (j + 1) * D_MODEL] = r.astype(BF16)
            ff = ff + _dot((r * r).astype(BF16), w2_ref[j])
        xhat2, rstd2 = _layer_norm_stats(ALPHA * x1 + ff)
        err = xhat2 * g2_ref[...] + b2_ref[...] - tgt_ref[...]
        sq_ref[...] += jnp.sum(err * err, axis=0, keepdims=True)
        dy = err * (1.0 / D_MODEL)
        dg2_ref[...] += jnp.sum(dy * xhat2, axis=0, keepdims=True)
        db2_ref[...] += jnp.sum(dy, axis=0, keepdims=True)
        dz2 = _layer_norm_bwd(dy * g2_ref[...], xhat2, rstd2)
        dz2_ref[...] = dz2
        dz2b_ref[...] = dz2.astype(BF16)

    vec = _const_spec((1, D_MODEL))
    tile = _row_spec(TM_FFN, D_MODEL)
    wspec = _const_spec((N_FF_BLOCKS, D_MODEL, D_MODEL), single_buffer=True)
    return pl.pallas_call(
        body,
        name="ffn_fwd_loss",
        grid=(t // TM_FFN,),
        in_specs=[tile, tile, _const_spec((D_MODEL, D_MODEL), single_buffer=True), vec, vec, wspec, wspec, vec, vec, tile],
        out_specs=[tile, _row_spec(TM_FFN, 1), tile, _row_spec(TM_FFN, D_FF), tile, tile, vec, vec, vec],
        out_shape=[
            jax.ShapeDtypeStruct((t, D_MODEL), F32),
            jax.ShapeDtypeStruct((t, 1), F32),
            jax.ShapeDtypeStruct((t, D_MODEL), BF16),
            jax.ShapeDtypeStruct((t, D_FF), BF16),
            jax.ShapeDtypeStruct((t, D_MODEL), F32),
            jax.ShapeDtypeStruct((t, D_MODEL), BF16),
            jax.ShapeDtypeStruct((1, D_MODEL), F32),
            jax.ShapeDtypeStruct((1, D_MODEL), F32),
            jax.ShapeDtypeStruct((1, D_MODEL), F32),
        ],
        compiler_params=_params(("arbitrary",)),
    )(cat, x, w_out, ln1_g, ln1_b, w1, w2, ln2_g, ln2_b, target)


def _ffn_bwd_ln1(dz2, r, xhat1, rstd1, ln1_g, w1, w2, w_out, dep=None):
    t = dz2.shape[0]

    def body(dz2_ref, r_ref, xh_ref, rstd_ref, g1_ref, w1_ref, w2_ref, wo_ref, dpre_ref, dz1_ref, dz1b_ref, dcat_ref, dg1_ref, db1_ref):
        @pl.when(pl.program_id(0) == 0)
        def _():
            dg1_ref[...] = jnp.zeros_like(dg1_ref)
            db1_ref[...] = jnp.zeros_like(db1_ref)

        dz2 = dz2_ref[...]
        dz2b = dz2.astype(BF16)
        dx1 = ALPHA * dz2
        for j in range(N_FF_BLOCKS):
            cols = slice(j * D_MODEL, (j + 1) * D_MODEL)
            dpre = (_dot(dz2b, w2_ref[j], NT) * (2.0 * r_ref[:, cols].astype(F32))).astype(BF16)
            dpre_ref[:, cols] = dpre
            dx1 = dx1 + _dot(dpre, w1_ref[j], NT)
        xhat1 = xh_ref[...]
        dg1_ref[...] += jnp.sum(dx1 * xhat1, axis=0, keepdims=True)
        db1_ref[...] += jnp.sum(dx1, axis=0, keepdims=True)
        dz1 = _layer_norm_bwd(dx1 * g1_ref[...], xhat1, rstd_ref[...])
        dz1_ref[...] = dz1
        dz1b = dz1.astype(BF16)
        dz1b_ref[...] = dz1b
        dcat_ref[...] = _dot(dz1b, wo_ref[...], NT).astype(BF16)

    vec = _const_spec((1, D_MODEL))
    tile = _row_spec(TM_FFN, D_MODEL)
    wspec = _const_spec((N_FF_BLOCKS, D_MODEL, D_MODEL), single_buffer=True)
    body, in_specs, operands = _after(
        dep, body,
        [tile, _row_spec(TM_FFN, D_FF), tile, _row_spec(TM_FFN, 1), vec, wspec, wspec, _const_spec((D_MODEL, D_MODEL), single_buffer=True)],
        [dz2, r, xhat1, rstd1, ln1_g, w1, w2, w_out])
    return pl.pallas_call(
        body,
        name="ffn_bwd_ln1",
        grid=(t // TM_FFN,),
        in_specs=in_specs,
        out_specs=[_row_spec(TM_FFN, D_FF), tile, tile, tile, vec, vec],
        out_shape=[
            jax.ShapeDtypeStruct((t, D_FF), BF16),
            jax.ShapeDtypeStruct((t, D_MODEL), F32),
            jax.ShapeDtypeStruct((t, D_MODEL), BF16),
            jax.ShapeDtypeStruct((t, D_MODEL), BF16),
            jax.ShapeDtypeStruct((1, D_MODEL), F32),
            jax.ShapeDtypeStruct((1, D_MODEL), F32),
        ],
        compiler_params=_params(("arbitrary",)),
    )(*operands)


def _mixer_bwd(u, vg, q, k, va, dcat, cos, sin, v_ln_g, v_ln_b, w_spatial, bias_full, sinks, dep=None):
    t = u.shape[0]
    n_chunks = t // CHUNK

    def body(u_ref, vg_ref, q_ref, kc_ref, kp_ref, vc_ref, vp_ref, dcat_ref, cosc_ref, sinc_ref, cosp_ref, sinp_ref,
             g_ref, b_ref, w_ref, bias_ref, sink_ref,
             dmain_ref, dkv_ref, dg_ref, db_ref, dw_ref, dbs_ref, dsink_ref, dmix_acc, wcat, wcat_t):
        i = pl.program_id(0)
        left = _half_lane_masks(CHUNK)
        lane = lax.broadcasted_iota(jnp.int32, (CHUNK, LANES), 1)
        n_pairs = D_GMLP // LANES

        @pl.when(i == 0)
        def _():
            dg_ref[...] = jnp.zeros_like(dg_ref)
            db_ref[...] = jnp.zeros_like(db_ref)
            dw_ref[...] = jnp.zeros_like(dw_ref)
            dsink_ref[...] = jnp.zeros_like(dsink_ref)
            dmix_acc[...] = jnp.zeros_like(dmix_acc)
            _store_spatial_weights(w_ref, wcat, wcat_t)

        n_qpairs = D_ATTN // LANES
        heads = range(N_HEADS)
        pair_cols = [slice(p * LANES, (p + 1) * LANES) for p in range(n_pairs)]
        sinks_h = [sink_ref[h] for h in heads]
        gain = g_ref[...]
        causal = _causal_mask()
        lane_row = lax.broadcasted_iota(jnp.int32, (1, LANES), 1)
        heads_per_group = N_HEADS // 2

        def group_grad_t(lhs_t, rhs_heads):
            parts = []
            for g in range(2):
                group = range(g * heads_per_group, (g + 1) * heads_per_group)
                lhs = jnp.concatenate([lhs_t[h * HEAD_DIM : (h + 1) * HEAD_DIM] for h in group], axis=1)
                parts.append(_dot(lhs, jnp.concatenate([rhs_heads[h] for h in group], axis=0)))
            return jnp.concatenate(parts, axis=0)

        for c in range(CHUNKS_PER_STEP):
            chunk = CHUNKS_PER_STEP * i + c
            rows = slice(c * CHUNK, (c + 1) * CHUNK)
            before = slice((c - 1) * CHUNK, c * CHUNK)

            k_prev = kp_ref[...] if c == 0 else kc_ref[before, :]
            v_prev = vp_ref[...] if c == 0 else vc_ref[before, :]
            k_var = _kv_variants(jnp.concatenate([k_prev, kc_ref[rows, :]], axis=0))
            v_var = _kv_variants(jnp.concatenate([v_prev, vc_ref[rows, :]], axis=0))
            q_pairs = [q_ref[rows, cols] for cols in pair_cols]
            do_all = dcat_ref[rows, D_GMLP:D_MODEL]
            do_pairs = [do_all[:, cols] for cols in pair_cols]
            scores = [_dot(q_pairs[h // 2], k_var[h // 4][h % 2], NT) for h in heads]
            dprobs = [_dot(do_pairs[h // 2], v_var[h // 4][h % 2], NT) for h in heads]
            q_t = q_ref[rows, :].astype(F32).T.astype(BF16)
            do_t = do_all.astype(F32).T.astype(BF16)

            ug, dug_du = _gelu_and_grad(u_ref[rows, :])
            gv, dgv_dv = _gelu_and_grad(vg_ref[rows, :])
            xhat, rstd = _layer_norm_stats(gv)
            vgl = xhat * gain + b_ref[...]
            mixed = [_dot(wcat[p], _pair_stack(vgl[:, cols], left)) for p, cols in enumerate(pair_cols)]

            valid = _band_mask(chunk)
            masked = [jnp.where(valid, scores[h] * SCALE, NEG_INF) for h in heads]
            maxes = [jnp.maximum(jnp.max(masked[h], axis=1, keepdims=True), sinks_h[h]) for h in heads]
            exps = [jnp.exp(masked[h] - maxes[h]) for h in heads]
            exp_sinks = [jnp.exp(sinks_h[h] - maxes[h]) for h in heads]
            invs = [1.0 / (jnp.sum(exps[h], axis=1, keepdims=True) + exp_sinks[h]) for h in heads]
            probs = [exps[h] * invs[h] for h in heads]
            dsums = [jnp.sum(probs[h] * dprobs[h], axis=1, keepdims=True) for h in heads]
            ds_b = [(probs[h] * (dprobs[h] - dsums[h]) * SCALE).astype(BF16) for h in heads]
            probs_b = [probs[h].astype(BF16) for h in heads]

            dm_stacks = []
            for p, cols in enumerate(pair_cols):
                da = dcat_ref[rows, cols].astype(F32)
                dmain_ref[rows, cols] = (da * (mixed[p] + bias_ref[:, cols]) * dug_du[:, cols]).astype(BF16)
                dmixed = da * ug[:, cols]
                dmix_acc[:, cols] += dmixed
                dm_stacks.append(_pair_stack(dmixed, left))

            dq_all = jnp.concatenate(
                [_dot(ds_b[2 * p], k_var[p // 2][0]) + _dot(ds_b[2 * p + 1], k_var[p // 2][1]) for p in range(n_qpairs)], axis=1)
            dk2_t = group_grad_t(q_t, ds_b)
            dv2_t = group_grad_t(do_t, probs_b)

            for p, cols in enumerate(pair_cols):
                dw_pair = _dot(dm_stacks[p], vgl[:, cols].astype(BF16), NT)
                dw_ref[2 * p] += jnp.where(causal, dw_pair[:CHUNK], 0.0)
                dw_ref[2 * p + 1] += jnp.where(causal, dw_pair[CHUNK:], 0.0)
            dvgl = jnp.concatenate([_dot(wcat_t[p], dm_stacks[p]) for p in range(n_pairs)], axis=1)

            dsink_row = jnp.zeros((1, LANES), F32)
            for h in heads:
                d_sink = -jnp.sum(exp_sinks[h] * invs[h] * dsums[h], axis=0, keepdims=True)
                dsink_row = dsink_row + jnp.where(lane_row == h, d_sink, 0.0)
            dsink_ref[0:1, :] += dsink_row
            cos_c, sin_c = cosc_ref[rows, :], sinc_ref[rows, :]
            cos_p = cosp_ref[...] if c == 0 else cosc_ref[before, :]
            sin_p = sinp_ref[...] if c == 0 else sinc_ref[before, :]
            dmain_ref[rows, 2 * D_GMLP : D_MAIN] = _rope_transposed(dq_all, _lane_tile(cos_c, n_qpairs), _lane_tile(sin_c, n_qpairs)).astype(BF16)
            dk2 = dk2_t.T
            dv2 = dv2_t.T
            cur = pl.ds(pl.multiple_of(chunk * CHUNK, CHUNK), CHUNK)
            dkv_ref[cur, 0:D_KV] = _rope_transposed(dk2[CHUNK:], cos_c, sin_c)
            dkv_ref[cur, D_KV : 2 * D_KV] = dv2[CHUNK:]
            prev = pl.ds(pl.multiple_of(jnp.maximum(chunk - 1, 0) * CHUNK, CHUNK), CHUNK)
            dkv_ref[prev, 0:D_KV] += _rope_transposed(dk2[:CHUNK], cos_p, sin_p)
            dkv_ref[prev, D_KV : 2 * D_KV] += dv2[:CHUNK]

            dg_ref[...] += jnp.sum(dvgl * xhat, axis=0, keepdims=True)
            db_ref[...] += jnp.sum(dvgl, axis=0, keepdims=True)
            dgv = _layer_norm_bwd(dvgl * gain, xhat, rstd)
            dmain_ref[rows, D_GMLP : 2 * D_GMLP] = (dgv * dgv_dv).astype(BF16)

        @pl.when(i == n_chunks // CHUNKS_PER_STEP - 1)
        def _():
            tile = jnp.zeros((CHUNK, LANES), F32)
            for p, cols in enumerate(pair_cols):
                dm = dmix_acc[:, cols]
                sl = jnp.sum(jnp.where(left, dm, 0.0), axis=1, keepdims=True)
                sr = jnp.sum(jnp.where(left, 0.0, dm), axis=1, keepdims=True)
                tile = jnp.where(lane == 2 * p, sl, tile)
                tile = jnp.where(lane == 2 * p + 1, sr, tile)
            dbs_ref[...] = tile

    step = CHUNKS_PER_STEP * CHUNK
    in_specs = _chunk_specs() + [
        pl.BlockSpec((step, D_MODEL), _step_rows),
        pl.BlockSpec((step, LANES), _step_rows),
        pl.BlockSpec((step, LANES), _step_rows),
        pl.BlockSpec((CHUNK, LANES), _chunk_before_step),
        pl.BlockSpec((CHUNK, LANES), _chunk_before_step),
        _const_spec((1, D_GMLP)),
        _const_spec((1, D_GMLP)),
        _const_spec((N_HEADS, CHUNK, CHUNK)),
        _const_spec((CHUNK, D_GMLP)),
        pl.BlockSpec(memory_space=pltpu.SMEM),
    ]
    body, in_specs, operands = _after(
        dep, body, in_specs, [u, vg, q, k, k, va, va, dcat, cos, sin, cos, sin, v_ln_g, v_ln_b, w_spatial, bias_full, sinks])
    return pl.pallas_call(
        body,
        name="mixer_bwd",
        grid=(n_chunks // CHUNKS_PER_STEP,),
        in_specs=in_specs,
        out_specs=[
            pl.BlockSpec((step, D_MAIN), _step_rows),
            _const_spec((t, 2 * D_KV)),
            _const_spec((1, D_GMLP)),
            _const_spec((1, D_GMLP)),
            _const_spec((N_HEADS, CHUNK, CHUNK)),
            _const_spec((CHUNK, LANES)),
            _const_spec((8, LANES)),
        ],
        out_shape=[
            jax.ShapeDtypeStruct((t, D_MAIN), BF16),
            jax.ShapeDtypeStruct((t, 2 * D_KV), F32),
            jax.ShapeDtypeStruct((1, D_GMLP), F32),
            jax.ShapeDtypeStruct((1, D_GMLP), F32),
            jax.ShapeDtypeStruct((N_HEADS, CHUNK, CHUNK), F32),
            jax.ShapeDtypeStruct((CHUNK, LANES), F32),
            jax.ShapeDtypeStruct((8, LANES), F32),
        ],
        scratch_shapes=[
            pltpu.VMEM((CHUNK, D_GMLP), F32),
            pltpu.VMEM((D_GMLP // LANES, CHUNK, 2 * CHUNK), BF16),
            pltpu.VMEM((D_GMLP // LANES, CHUNK, 2 * CHUNK), BF16),
        ],
        compiler_params=_params(("arbitrary",)),
    )(*operands)


def _grad_x(dh_main, dkv, dz1, w_in_t, dep=None):
    t = dz1.shape[0]

    def body(dm_ref, dkv_ref, dz1_ref, w_ref, gx_ref):
        acc = ALPHA * dz1_ref[...] + _dot(dm_ref[...], w_ref[0:D_MAIN, :])
        gx_ref[...] = acc + _dot(dkv_ref[...].astype(BF16), w_ref[D_MAIN:D_IN, :])

    body, in_specs, operands = _after(
        dep, body, [_row_spec(TM, D_MAIN), _row_spec(TM, 2 * D_KV), _row_spec(TM, D_MODEL), _const_spec((D_IN, D_MODEL))], [dh_main, dkv, dz1, w_in_t])
    return pl.pallas_call(
        body,
        name="grad_x",
        grid=(t // TM,),
        in_specs=in_specs,
        out_specs=_row_spec(TM, D_MODEL),
        out_shape=jax.ShapeDtypeStruct((t, D_MODEL), F32),
        compiler_params=_params(("parallel",)),
    )(*operands)


def _token_contraction(name, out_rows, tk, in_arrays, contributions, dep=None):
    t = in_arrays[0].shape[0]

    def body(*refs):
        out_ref = refs[-1]

        @pl.when(pl.program_id(0) == 0)
        def _():
            out_ref[...] = jnp.zeros_like(out_ref)

        for row0, a, b in contributions(*refs[:-1]):
            out_ref[row0 : row0 + a.shape[1], :] += _dot(a, b, TN)

    in_specs = [_row_spec(tk, a.shape[1]) for a in in_arrays]
    body, in_specs, operands = _after(dep, body, in_specs, in_arrays)
    return pl.pallas_call(
        body,
        name=name,
        grid=(t // tk,),
        in_specs=in_specs,
        out_specs=_const_spec((out_rows, D_MODEL), single_buffer=True),
        out_shape=jax.ShapeDtypeStruct((out_rows, D_MODEL), F32),
        compiler_params=_params(("arbitrary",)),
    )(*operands)


def _grad_w_in_t(dh_main, dkv, x, dep=None):
    def contributions(dm_ref, dkv_ref, x_ref):
        xb = x_ref[...].astype(BF16)
        return [(0, dm_ref[...], xb), (D_MAIN, dkv_ref[...].astype(BF16), xb)]

    return _token_contraction("grad_w_in", D_IN, TK, [dh_main, dkv, x], contributions, dep)


def _grad_w_out(cat, dz1b, dep=None):
    def contributions(cat_ref, dz1_ref):
        return [(0, cat_ref[...], dz1_ref[...])]

    return _token_contraction("grad_w_out", D_MODEL, TK, [cat, dz1b], contributions, dep)


def _grad_w_ff1(x1b, dpre):
    def contributions(x1_ref, dpre_ref):
        x1 = x1_ref[...]
        return [(j * D_MODEL, x1, dpre_ref[:, j * D_MODEL : (j + 1) * D_MODEL]) for j in range(N_FF_BLOCKS)]

    return _token_contraction("grad_w_ff1", D_FF, TK_FF, [x1b, dpre], contributions)


def _grad_w_ff2(r, dz2b):
    def contributions(r_ref, dz2_ref):
        dz2 = dz2_ref[...]
        out = []
        for j in range(N_FF_BLOCKS):
            rf = r_ref[:, j * D_MODEL : (j + 1) * D_MODEL].astype(F32)
            out.append((j * D_MODEL, (rf * rf).astype(BF16), dz2))
        return out

    return _token_contraction("grad_w_ff2", D_FF, TK_FF, [r, dz2b], contributions)


ANY = pl.BlockSpec(memory_space=pl.ANY)


def _mesh_position():
    return lax.axis_index("x"), lax.axis_index("y"), lax.axis_index("c")


def _other_chips(x, y):
    return [(1 - x, y), (x, 1 - y), (1 - x, 1 - y)]


def _remote(src, dst, send_sem, recv_sem, device):
    return pltpu.make_async_remote_copy(src_ref=src, dst_ref=dst, send_sem=send_sem, recv_sem=recv_sem, device_id=device, device_id_type=MESH)


def _rows(ref, start, size):
    return ref.at[pl.ds(start, size), :]


def _all_gather_weights(shards):
    n = len(shards)
    per = 7

    def body(*refs):
        ins, outs = refs[:n], refs[n : 2 * n]
        send_sems, recv_sems = refs[2 * n :]
        x, y, c = _mesh_position()
        me = 2 * x + y
        chips = _other_chips(x, y)
        sibling = (x, y, 1 - c)
        started = []
        for w in range(n):
            rows = shards[w].shape[0]
            half = rows // 2
            for kk, (px, py) in enumerate(chips):
                cp = _remote(_rows(ins[w], c * half, half), _rows(outs[w], me * rows + c * half, half),
                             send_sems.at[per * w + kk], recv_sems.at[per * w + kk], (px, py, c))
                cp.start()
                started.append(cp)
            cp = _remote(ins[w], _rows(outs[w], me * rows, rows), send_sems.at[per * w + 6], recv_sems.at[per * w + 6], sibling)
            cp.start()
            started.append(cp)
        for w in range(n):
            rows = shards[w].shape[0]
            half = rows // 2
            for kk, (px, py) in enumerate(chips):
                blk = _rows(outs[w], (2 * px + py) * rows + c * half, half)
                _remote(blk, blk, send_sems.at[per * w + kk], recv_sems.at[per * w + kk], (px, py, c)).wait_recv()
                fwd = _remote(blk, blk, send_sems.at[per * w + 3 + kk], recv_sems.at[per * w + 3 + kk], sibling)
                fwd.start()
                started.append(fwd)
        for w in range(n):
            rows = shards[w].shape[0]
            half = rows // 2
            for kk, (px, py) in enumerate(chips):
                blk = _rows(outs[w], (2 * px + py) * rows + (1 - c) * half, half)
                _remote(blk, blk, send_sems.at[per * w + 3 + kk], recv_sems.at[per * w + 3 + kk], sibling).wait_recv()
            own = _rows(outs[w], me * rows, rows)
            _remote(own, own, send_sems.at[per * w + 6], recv_sems.at[per * w + 6], sibling).wait_recv()
        for cp in started:
            cp.wait_send()

    return pl.pallas_call(
        body,
        name="all_gather_weights",
        in_specs=[ANY] * n,
        out_specs=[ANY] * n,
        out_shape=[jax.ShapeDtypeStruct((N_CHIPS * s.shape[0], s.shape[1]), s.dtype) for s in shards],
        scratch_shapes=[pltpu.SemaphoreType.DMA((per * n,)), pltpu.SemaphoreType.DMA((per * n,))],
    )(*shards)


def _pair_gather(name, shards):
    n = len(shards)

    def body(*refs):
        outs = refs[n : 2 * n]
        send_sems, recv_sems = refs[2 * n :]
        x, y, c = _mesh_position()
        sibling = (x, y, 1 - c)
        sends = []
        for w in range(n):
            half = shards[w].shape[0] // 2
            mine = _rows(outs[w], c * half, half)
            cp = _remote(mine, mine, send_sems.at[w], recv_sems.at[w], sibling)
            cp.start()
            sends.append(cp)
        for w in range(n):
            half = shards[w].shape[0] // 2
            blk = _rows(outs[w], (1 - c) * half, half)
            _remote(blk, blk, send_sems.at[w], recv_sems.at[w], sibling).wait_recv()
        for cp in sends:
            cp.wait_send()

    return pl.pallas_call(
        body,
        name=name,
        in_specs=[ANY] * n,
        out_specs=[ANY] * n,
        out_shape=[jax.ShapeDtypeStruct(s.shape, s.dtype) for s in shards],
        input_output_aliases={w: w for w in range(n)},
        scratch_shapes=[pltpu.SemaphoreType.DMA((n,)), pltpu.SemaphoreType.DMA((n,))],
    )(*shards)


def _all_reduce_small(slab, dep=None):
    rows = slab.shape[0]
    part = rows // 8

    def body(slab_ref, out_ref, landing, reduced, send_sems, recv_sems):
        x, y, c = _mesh_position()
        me = 4 * x + 2 * y + c
        flips = [(k >> 2, (k >> 1) & 1, k & 1) for k in range(1, 8)]

        def peer(flip):
            fx, fy, fc = flip
            return (1 - x if fx else x, 1 - y if fy else y, 1 - c if fc else c)

        def my_rows(ref):
            return ref.at[pl.ds(pl.multiple_of(me * part, 8), part), :]

        sends = []
        for kk, flip in enumerate(flips):
            px, py, pc = peer(flip)
            them = 4 * px + 2 * py + pc
            cp = _remote(slab_ref.at[pl.ds(pl.multiple_of(them * part, 8), part), :], landing.at[me], send_sems.at[kk], recv_sems.at[kk], (px, py, pc))
            cp.start()
            sends.append(cp)
        landing[me] = my_rows(slab_ref)[...]
        for kk, flip in enumerate(flips):
            px, py, pc = peer(flip)
            them = 4 * px + 2 * py + pc
            _remote(landing.at[them], landing.at[them], send_sems.at[kk], recv_sems.at[kk], (px, py, pc)).wait_recv()
        total = landing[0]
        for s in range(1, 8):
            total = total + landing[s]
        reduced[...] = total
        my_rows(out_ref)[...] = total
        for kk, flip in enumerate(flips):
            cp = _remote(reduced, my_rows(out_ref), send_sems.at[7 + kk], recv_sems.at[7 + kk], peer(flip))
            cp.start()
            sends.append(cp)
        for kk, flip in enumerate(flips):
            px, py, pc = peer(flip)
            them = 4 * px + 2 * py + pc
            blk = out_ref.at[pl.ds(pl.multiple_of(them * part, 8), part), :]
            _remote(blk, blk, send_sems.at[7 + kk], recv_sems.at[7 + kk], (px, py, pc)).wait_recv()
        for cp in sends:
            cp.wait_send()

    vmem = pl.BlockSpec(memory_space=pltpu.VMEM)
    body, in_specs, operands = _after(dep, body, [vmem], [slab])
    return pl.pallas_call(
        body,
        name="all_reduce_small",
        in_specs=in_specs,
        out_specs=vmem,
        out_shape=jax.ShapeDtypeStruct(slab.shape, slab.dtype),
        scratch_shapes=[pltpu.VMEM((8, part, LANES), F32), pltpu.VMEM((part, LANES), F32), pltpu.SemaphoreType.DMA((14,)), pltpu.SemaphoreType.DMA((14,))],
    )(*operands)


HBM = pl.BlockSpec(memory_space=pltpu.HBM)
SEM = pl.BlockSpec(memory_space=pltpu.SEMAPHORE)
DATAFLOW = pltpu.SideEffectType.DATAFLOW_SIDE_EFFECTING
TOKEN = jax.ShapeDtypeStruct((8, LANES), F32)


def _plan_copies(bufs, plan, send_sems, recv_sems):
    out = []
    for i, (src, src_row, dst, dst_row, recv_row, rows, device) in enumerate(plan):
        send = _remote(_rows(bufs[src], src_row, rows), _rows(bufs[dst], dst_row, rows), send_sems.at[i], recv_sems.at[i], device)
        landed = _rows(bufs[dst], recv_row, rows)
        recv = _remote(landed, landed, send_sems.at[i], recv_sems.at[i], device)
        out.append((send, recv))
    return out


def _split_call(name, bufs, wait=None, start=None, after=None):
    n = len(bufs)
    n_in = n + (2 if wait else 0) + (1 if after is not None else 0)
    n_start = len(start(0, 0, 0)) if start else 0

    def body(*refs):
        ins = refs[:n]
        x, y, c = _mesh_position()
        if wait:
            for send, recv in _plan_copies(ins, wait[0](x, y, c), refs[n], refs[n + 1]):
                recv.wait_recv()
                send.wait_send()
        if start:
            for send, _ in _plan_copies(ins, start(x, y, c), refs[n_in + n + 1], refs[n_in + n + 2]):
                send.start()
        token = refs[n_in + n]
        token[...] = jnp.zeros_like(token)

    operands = [pltpu.with_memory_space_constraint(b, pltpu.HBM) for b in bufs]
    in_specs = [HBM] * n
    if wait:
        operands += [wait[1], wait[2]]
        in_specs += [SEM, SEM]
    if after is not None:
        operands.append(after)
        in_specs.append(ANY)
    out_shape = [pltpu.HBM(b.shape, b.dtype) for b in bufs] + [TOKEN]
    out_specs = [HBM] * n + [pl.BlockSpec(memory_space=pltpu.VMEM)]
    if start:
        out_shape += [pltpu.SemaphoreType.DMA((n_start,)), pltpu.SemaphoreType.DMA((n_start,))]
        out_specs += [SEM, SEM]
    outs = pl.pallas_call(
        body,
        name=name,
        in_specs=in_specs,
        out_specs=out_specs,
        out_shape=out_shape,
        input_output_aliases={i: i for i in range(n)},
        compiler_params=pltpu.CompilerParams(has_side_effects=DATAFLOW),
    )(*operands)
    return (list(outs[:n]), outs[n]) + tuple(outs[n + 1 :])


def _gather_plans(shard_rows):
    n = len(shard_rows)

    def ici(x, y, c):
        me = 2 * x + y
        plan = []
        for w, rows in enumerate(shard_rows):
            half = rows // 2
            for px, py in _other_chips(x, y):
                plan.append((w, c * half, n + w, me * rows + c * half, (2 * px + py) * rows + c * half, half, (px, py, c)))
            plan.append((w, 0, n + w, me * rows, me * rows, rows, (x, y, 1 - c)))
        return plan

    def passed_on(x, y, c):
        plan = []
        for w, rows in enumerate(shard_rows):
            half = rows // 2
            for px, py in _other_chips(x, y):
                row = (2 * px + py) * rows
                plan.append((n + w, row + c * half, n + w, row + c * half, row + (1 - c) * half, half, (x, y, 1 - c)))
        return plan

    return ici, passed_on


def _swap_plan(block_rows):
    n = len(block_rows)

    def plan_fn(x, y, c):
        plan = []
        for w, rows in enumerate(block_rows):
            half = rows // 2
            for j in range(N_CHIPS):
                plan.append((w, j * rows + (1 - c) * half, n + w, j * half, j * half, half, (x, y, 1 - c)))
        return plan

    return plan_fn


def _exchange_plan(halves):
    n = len(halves)

    def plan_fn(x, y, c):
        plan = []
        for w, half in enumerate(halves):
            for kk, (px, py) in enumerate(_other_chips(x, y)):
                plan.append((w, (2 * px + py) * half, n + w, kk * half, kk * half, half, (px, py, c)))
        return plan

    return plan_fn


def _landing(rows, cols, dtype):
    return lax.empty((rows, cols), dtype)


def _row_tile(rows, cap=512):
    best = 8
    for cand in range(8, cap + 1, 8):
        if rows % cand == 0:
            best = cand
    return best


def _pair_sum(name, grad, theirs, pos):
    half = theirs.shape[0] // N_CHIPS
    cols = theirs.shape[1]
    tile = _row_tile(half)
    steps = half // tile

    def body(pos_ref, g_ref, t_ref, p_ref, own_ref):
        total = g_ref[...] + t_ref[...]
        p_ref[...] = total.astype(BF16)

        @pl.when(pl.program_id(1) == pos_ref[1])
        def _():
            own_ref[...] = total

    return pl.pallas_call(
        body,
        name=name,
        grid_spec=pltpu.PrefetchScalarGridSpec(
            num_scalar_prefetch=1,
            grid=(steps, N_CHIPS),
            in_specs=[
                pl.BlockSpec((tile, cols), lambda i, j, pos: ((2 * j + pos[0]) * steps + i, 0)),
                pl.BlockSpec((tile, cols), lambda i, j, pos: (j * steps + i, 0)),
            ],
            out_specs=[
                pl.BlockSpec((tile, cols), lambda i, j, pos: (j * steps + i, 0)),
                pl.BlockSpec((tile, cols), lambda i, j, pos: (i, 0)),
            ],
        ),
        out_shape=[jax.ShapeDtypeStruct((N_CHIPS * half, cols), BF16), jax.ShapeDtypeStruct((half, cols), F32)],
        compiler_params=_params(("parallel", "arbitrary")),
    )(pos, grad, theirs)


def _chip_sum(name, own, landed, pos):
    half, cols = own.shape
    tile = _row_tile(half)
    steps = half // tile

    def body(pos_ref, own_ref, l0, l1, l2, o_ref):
        o_ref[...] = ((own_ref[...] + l0[...].astype(F32)) + l1[...].astype(F32)) + l2[...].astype(F32)

    landed_specs = [pl.BlockSpec((tile, cols), lambda i, pos, _k=k: (_k * steps + i, 0)) for k in range(N_CHIPS - 1)]
    return pl.pallas_call(
        body,
        name=name,
        grid_spec=pltpu.PrefetchScalarGridSpec(
            num_scalar_prefetch=1,
            grid=(steps,),
            in_specs=[pl.BlockSpec((tile, cols), lambda i, pos: (i, 0))] + landed_specs,
            out_specs=pl.BlockSpec((tile, cols), lambda i, pos: (pos[0] * steps + i, 0)),
        ),
        out_shape=jax.ShapeDtypeStruct((2 * half, cols), F32),
        compiler_params=_params(("parallel",)),
    )(pos, own, landed, landed, landed)


def _adamw(name, w, g, m, v):
    rows, cols = w.shape
    tile = rows if rows * cols <= 256 * 1024 else _row_tile(rows)

    def body(w_ref, g_ref, m_ref, v_ref, g_out_ref, d_ref, nm_ref, nv_ref):
        g = g_ref[...]
        g_out_ref[...] = g
        nm = ADAM_B1 * m_ref[...] + (1.0 - ADAM_B1) * g
        nv = ADAM_B2 * v_ref[...] + (1.0 - ADAM_B2) * (g * g)
        m_hat = nm / (1.0 - ADAM_B1**ADAM_STEP)
        v_hat = nv / (1.0 - ADAM_B2**ADAM_STEP)
        d_ref[...] = -ADAM_LR * (m_hat / (jnp.sqrt(v_hat) + ADAM_EPS) + ADAM_WD * w_ref[...])
        nm_ref[...] = nm
        nv_ref[...] = nv

    spec = _row_spec(tile, cols)
    return pl.pallas_call(
        body,
        name=name,
        grid=(rows // tile,),
        in_specs=[spec] * 4,
        out_specs=[spec] * 4,
        out_shape=[jax.ShapeDtypeStruct((rows, cols), F32)] * 4,
        compiler_params=_params(("parallel",)),
    )(w, g, m, v)


_SMALL = (
    ("v_ln_g", (D_GMLP,), 8),
    ("v_ln_b", (D_GMLP,), 8),
    ("w_spatial", (N_HEADS, CHUNK, CHUNK), 1024),
    ("b_spatial", (N_HEADS, CHUNK), 8),
    ("sinks", (N_HEADS,), 8),
    ("ln1_g", (D_MODEL,), 8),
    ("ln1_b", (D_MODEL,), 8),
    ("ln2_g", (D_MODEL,), 8),
    ("ln2_b", (D_MODEL,), 8),
    ("squared_error", (D_MODEL,), 8),
)
N_SMALL_PARAMS = len(_SMALL) - 1


def _pack_small(values):
    parts = []
    for (name, shape, rows), val in zip(_SMALL, values, strict=True):
        flat = val.reshape(-1).astype(F32)
        parts.append(jnp.pad(flat, (0, rows * LANES - flat.shape[0])).reshape(rows, LANES))
    parts.append(jnp.zeros((SMALL_ROWS - sum(rows for _, _, rows in _SMALL), LANES), F32))
    return jnp.concatenate(parts, axis=0)


def _adamw_update(w, g, m, v):
    nm = ADAM_B1 * m + (1.0 - ADAM_B1) * g
    nv = ADAM_B2 * v + (1.0 - ADAM_B2) * (g * g)
    m_hat = nm / (1.0 - ADAM_B1**ADAM_STEP)
    v_hat = nv / (1.0 - ADAM_B2**ADAM_STEP)
    return -ADAM_LR * (m_hat / (jnp.sqrt(v_hat) + ADAM_EPS) + ADAM_WD * w), nm, nv


def _adamw_small(g_slab, params, first, second):
    n = N_SMALL_PARAMS

    def pieces(shape):
        if len(shape) == 3:
            return [((0, h), h * shape[1], shape[1], shape[2]) for h in range(shape[0])]
        if len(shape) == 2:
            return [((0,), 0, shape[0], shape[1])]
        if shape[0] >= LANES:
            return [((slice(None), slice(r * LANES, (r + 1) * LANES)), r, 1, LANES) for r in range(shape[0] // LANES)]
        return [((slice(None), slice(0, shape[0])), 0, 1, shape[0])]

    def body(*refs):
        g_ref = refs[0]
        w_refs, m_refs, v_refs = refs[1 : 1 + n], refs[1 + n : 1 + 2 * n], refs[1 + 2 * n : 1 + 3 * n]
        outs = refs[1 + 3 * n :]
        row0 = 0
        for idx, (_, shape, rows) in enumerate(_SMALL[:n]):
            for where, first_row, n_rows, lanes in pieces(shape):
                g = g_ref[row0 + first_row : row0 + first_row + n_rows, 0:lanes]
                delta, nm, nv = _adamw_update(w_refs[idx][where], g, m_refs[idx][where], v_refs[idx][where])
                for group, val in enumerate((g, delta, nm, nv)):
                    outs[group * n + idx][where] = val
            row0 += rows

    vmem = pl.BlockSpec(memory_space=pltpu.VMEM)
    shapes = [jax.ShapeDtypeStruct(p.shape, F32) for p in params]
    outs = pl.pallas_call(
        body,
        name="adamw_small",
        in_specs=[vmem] * (1 + 3 * n),
        out_specs=[vmem] * (4 * n),
        out_shape=shapes * 4,
        compiler_params=_params(),
    )(g_slab, *params, *first, *second)
    return [list(outs[group * n : (group + 1) * n]) for group in range(4)]


def kernel(x, positions, w_in, v_ln_g, v_ln_b, w_spatial, b_spatial, sinks, w_out, ln1_g, ln1_b, w_ff1, w_ff2, ln2_g, ln2_b, loss_target, m_w_in, m_v_ln_g, m_v_ln_b, m_w_spatial, m_b_spatial, m_sinks, m_w_out, m_ln1_g, m_ln1_b, m_w_ff1, m_w_ff2, m_ln2_g, m_ln2_b, v_w_in, v_v_ln_g, v_v_ln_b, v_w_spatial, v_b_spatial, v_sinks, v_w_out, v_ln1_g, v_ln1_b, v_w_ff1, v_w_ff2, v_ln2_g, v_ln2_b):
    t = x.shape[1]
    x2 = x.reshape(t, D_MODEL)
    target = loss_target.reshape(t, D_MODEL)

    (w_in_t,) = _all_gather_weights([w_in[0].T.astype(BF16)])
    later = [w_out[0].astype(BF16), w_ff1[0].astype(BF16), w_ff2[0].astype(BF16)]
    later_rows = [s.shape[0] for s in later]
    ici_plan, pass_plan = _gather_plans(later_rows)
    bufs, started, ici_send, ici_recv = _split_call(
        "gather_start", later + [_landing(N_CHIPS * r, D_MODEL, BF16) for r in later_rows], start=ici_plan, after=w_in_t)

    inv_freq = ROPE_THETA ** (-jnp.arange(0, HEAD_DIM, 2, dtype=F32) / HEAD_DIM)
    cos, sin = _rope_tables(positions, jnp.tile(inv_freq, LANES // (HEAD_DIM // 2)).reshape(1, LANES))
    u, vg, q, k, va = _in_proj(x2, w_in_t, cos, sin, dep=started)
    bias_full = jnp.repeat(b_spatial[0].T, HEAD_DIM, axis=1)
    sink_vec = sinks.reshape(N_HEADS)
    cat = _mixer_fwd(u, vg, q, k, va, v_ln_g, v_ln_b, w_spatial[0], bias_full, sink_vec)
    bufs, passed, pass_send, pass_recv = _split_call("gather_pass", bufs, wait=(ici_plan, ici_send, ici_recv), start=pass_plan, after=cat)
    bufs, _ = _split_call("gather_end", bufs, wait=(pass_plan, pass_send, pass_recv), after=passed)
    w_out_all = bufs[3]
    w1_all = bufs[4].reshape(N_FF_BLOCKS, D_MODEL, D_MODEL)
    w2_all = bufs[5].reshape(N_FF_BLOCKS, D_MODEL, D_MODEL)
    xhat1, rstd1, x1b, r, dz2, dz2b, d_ln2_g, d_ln2_b, sq_err = _ffn_fwd_loss(
        cat, x2, w_out_all, ln1_g, ln1_b, w1_all, w2_all, ln2_g, ln2_b, target)

    pos = jnp.stack([lax.axis_index("c"), 2 * lax.axis_index("x") + lax.axis_index("y")]).astype(jnp.int32)
    half_landing = lambda g: _landing(g.shape[0] // 2, D_MODEL, F32)
    g_ff2_local = _grad_w_ff2(r, dz2b)
    swap_plan = _swap_plan([D_FF // N_CHIPS])
    ff2_bufs, swapping2, swap2_send, swap2_recv = _split_call("ff2_swap_start", [g_ff2_local, half_landing(g_ff2_local)], start=swap_plan)
    dpre, dz1, dz1b, dcat, d_ln1_g, d_ln1_b = _ffn_bwd_ln1(dz2, r, xhat1, rstd1, ln1_g, w1_all, w2_all, w_out_all, dep=swapping2)
    g_ff1_local = _grad_w_ff1(x1b, dpre)
    ff1_bufs, swapping1, swap1_send, swap1_recv = _split_call("ff1_swap_start", [g_ff1_local, half_landing(g_ff1_local)], start=swap_plan)
    g_out_local = _grad_w_out(cat, dz1b, dep=swapping1)
    ff2_bufs, swapped2 = _split_call("ff2_swap_wait", ff2_bufs, wait=(swap_plan, swap2_send, swap2_recv), after=g_out_local)
    ff1_bufs, _ = _split_call("ff1_swap_wait", ff1_bufs, wait=(swap_plan, swap1_send, swap1_recv), after=swapped2)
    ff_sums = [_pair_sum("grad_pair_sum_w_ff1", ff1_bufs[0], ff1_bufs[1], pos), _pair_sum("grad_pair_sum_w_ff2", ff2_bufs[0], ff2_bufs[1], pos)]
    ff_halves = [p.shape[0] // N_CHIPS for p, _ in ff_sums]
    exchange_plan = _exchange_plan(ff_halves)
    bufs, exchanging, ex_send, ex_recv = _split_call(
        "ff_exchange_start", [p for p, _ in ff_sums] + [_landing(3 * h, D_MODEL, BF16) for h in ff_halves], start=exchange_plan)
    dh_main, dkv, d_v_ln_g, d_v_ln_b, d_w_spatial, d_b_spatial_t, d_sinks = _mixer_bwd(
        u, vg, q, k, va, dcat, cos, sin, v_ln_g, v_ln_b, w_spatial[0], bias_full, sink_vec, dep=exchanging)
    small_g = _all_reduce_small(_pack_small(
        [d_v_ln_g, d_v_ln_b, d_w_spatial, d_b_spatial_t[:, :N_HEADS].T, d_sinks[0, :N_HEADS], d_ln1_g, d_ln1_b, d_ln2_g, d_ln2_b, sq_err]))
    sq_row = sum(rows for _, _, rows in _SMALL[:N_SMALL_PARAMS])
    loss = 0.5 * jnp.sum(small_g[sq_row : sq_row + _SMALL[N_SMALL_PARAMS][2]]) / D_MODEL
    g_in_local = _grad_w_in_t(dh_main, dkv, x2, dep=small_g)

    small = [g_in_local, g_out_local]
    small_swap_plan = _swap_plan([g.shape[0] // N_CHIPS for g in small])
    swap_bufs, small_swapping, ss_send, ss_recv = _split_call(
        "small_swap_start", small + [half_landing(g) for g in small], start=small_swap_plan)
    grad_x_flat = _grad_x(dh_main, dkv, dz1, w_in_t, dep=small_swapping)
    grad_x = grad_x_flat.reshape(1, t, D_MODEL)
    swap_bufs, _ = _split_call("small_swap_wait", swap_bufs, wait=(small_swap_plan, ss_send, ss_recv), after=grad_x_flat)
    pair_sums = [_pair_sum("grad_pair_sum_" + nm, g, th, pos) for nm, g, th in zip(["w_in", "w_out"], swap_bufs[:2], swap_bufs[2:])]
    small_halves = [p.shape[0] // N_CHIPS for p, _ in pair_sums]
    small_plan = _exchange_plan(small_halves)
    small_bufs, small_exchanging, sm_send, sm_recv = _split_call(
        "small_exchange_start", [p for p, _ in pair_sums] + [_landing(3 * h, D_MODEL, BF16) for h in small_halves], start=small_plan)

    bufs, _ = _split_call("ff_exchange_wait", bufs, wait=(exchange_plan, ex_send, ex_recv), after=small_exchanging)
    ff_shards = [_chip_sum("grad_chip_sum_" + nm, own, ld, pos) for nm, (_, own), ld in zip(["w_ff1", "w_ff2"], ff_sums, bufs[2:])]
    g_w_ff1, g_w_ff2 = _pair_gather("grad_pair_gather_ff", ff_shards)

    g_w_ff1, d_w_ff1, nm_w_ff1, nv_w_ff1 = _adamw("adamw_w_ff1", w_ff1[0], g_w_ff1, m_w_ff1[0], v_w_ff1[0])
    g_w_ff2, d_w_ff2, nm_w_ff2, nv_w_ff2 = _adamw("adamw_w_ff2", w_ff2[0], g_w_ff2, m_w_ff2[0], v_w_ff2[0])
    small_bufs, _ = _split_call("small_exchange_wait", small_bufs, wait=(small_plan, sm_send, sm_recv), after=nv_w_ff2)
    shards = [_chip_sum("grad_chip_sum_" + nm, own, ld, pos) for nm, (_, own), ld in zip(["w_in", "w_out"], pair_sums, small_bufs[2:])]
    g_w_in_t, g_w_out = _pair_gather("grad_pair_gather_small", shards)
    g_w_in, d_w_in, nm_w_in, nv_w_in = (a.T for a in _adamw("adamw_w_in", w_in[0].T, g_w_in_t, m_w_in[0].T, v_w_in[0].T))
    g_w_out, d_w_out, nm_w_out, nv_w_out = _adamw("adamw_w_out", w_out[0], g_w_out, m_w_out[0], v_w_out[0])
    small_grads, small_d, small_nm, small_nv = _adamw_small(
        small_g,
        [v_ln_g, v_ln_b, w_spatial, b_spatial, sinks, ln1_g, ln1_b, ln2_g, ln2_b],
        [m_v_ln_g, m_v_ln_b, m_w_spatial, m_b_spatial, m_sinks, m_ln1_g, m_ln1_b, m_ln2_g, m_ln2_b],
        [v_v_ln_g, v_v_ln_b, v_w_spatial, v_b_spatial, v_sinks, v_ln1_g, v_ln1_b, v_ln2_g, v_ln2_b])

    def with_big(small, w_in_v, w_out_v, w_ff1_v, w_ff2_v):
        g_vg, g_vb, g_ws, g_bs, g_sk, g_1g, g_1b, g_2g, g_2b = small
        return [w_in_v[None], g_vg, g_vb, g_ws, g_bs, g_sk, w_out_v[None], g_1g, g_1b, w_ff1_v[None], w_ff2_v[None], g_2g, g_2b]

    return (
        loss,
        grad_x,
        *with_big(small_grads, g_w_in, g_w_out, g_w_ff1, g_w_ff2),
        *with_big(small_d, d_w_in, d_w_out, d_w_ff1, d_w_ff2),
        *with_big(small_nm, nm_w_in, nm_w_out, nm_w_ff1, nm_w_ff2),
        *with_big(small_nv, nv_w_in, nv_w_out, nv_w_ff1, nv_w_ff2),
    )
```

```python
import math

import jax
import jax.numpy as jnp
from jax import lax
from jax.experimental import pallas as pl
from jax.experimental.pallas import tpu as pltpu

F32 = jnp.float32
BF16 = jnp.bfloat16

D_MODEL = 1024
HEAD_DIM = 64
D_GMLP = 512
D_ATTN = 512
D_KV = 128
D_IN = 2 * D_GMLP + D_ATTN + 2 * D_KV
D_MAIN = 2 * D_GMLP + D_ATTN
N_HEADS = 8
CHUNK = 128
CHUNKS_PER_STEP = 4
ROPE_THETA = 10000.0
D_FF = 4 * D_MODEL
N_FF_BLOCKS = 4
LN_EPS = 1e-5
ALPHA = (2.0 * 1) ** 0.25
NEG_INF = -1e30
SCALE = 1.0 / math.sqrt(HEAD_DIM)

ADAM_LR = 0.001
ADAM_B1 = 0.9
ADAM_B2 = 0.999
ADAM_EPS = 1e-08
ADAM_WD = 0.01
ADAM_STEP = 10

N_CHIPS = 4
LANES = 128
V7X_VMEM_BYTES = 64 * 1024 * 1024
VMEM_LIMIT = V7X_VMEM_BYTES - 8 * 1024 * 1024
TM = 512
TM_FFN = 256
TK = 1024
TK_FF = 1024
SMALL_ROWS = 1152
MESH = pl.DeviceIdType.MESH

NT = (((1,), (1,)), ((), ()))
TN = (((0,), (0,)), ((), ()))


def _dot(a, b, dims=None):
    if dims is None:
        return jnp.dot(a, b, preferred_element_type=F32)
    return lax.dot_general(a, b, dims, preferred_element_type=F32)


def _params(semantics=None):
    return pltpu.CompilerParams(dimension_semantics=semantics, vmem_limit_bytes=VMEM_LIMIT)


def _const_spec(shape, single_buffer=False):
    zeros = (0,) * len(shape)
    if single_buffer:
        return pl.BlockSpec(shape, lambda *_: zeros, pipeline_mode=pl.Buffered(1))
    return pl.BlockSpec(shape, lambda *_: zeros)


def _row_spec(rows, cols):
    return pl.BlockSpec((rows, cols), lambda i: (i, 0))


def _after(dep, body, in_specs, operands):
    if dep is None:
        return body, list(in_specs), list(operands)
    return (lambda dep_ref, *refs: body(*refs)), [pl.BlockSpec(memory_space=pl.ANY)] + list(in_specs), [dep] + list(operands)


def _gelu(x):
    k = math.sqrt(2.0 / math.pi)
    return 0.5 * x * (1.0 + jnp.tanh(k * (x + 0.044715 * (x * x * x))))


def _gelu_and_grad(x):
    k = math.sqrt(2.0 / math.pi)
    x2 = x * x
    t = jnp.tanh(k * (x + 0.044715 * (x2 * x)))
    g = 0.5 * x * (1.0 + t)
    dg = 0.5 * (1.0 + t) + 0.5 * x * (1.0 - t * t) * (k * (1.0 + 3.0 * 0.044715 * x2))
    return g, dg


def _layer_norm_stats(z):
    mu = jnp.mean(z, axis=-1, keepdims=True)
    zc = z - mu
    var = jnp.mean(zc * zc, axis=-1, keepdims=True)
    rstd = lax.rsqrt(var + LN_EPS)
    return zc * rstd, rstd


def _layer_norm_bwd(dxhat, xhat, rstd):
    m1 = jnp.mean(dxhat, axis=-1, keepdims=True)
    m2 = jnp.mean(dxhat * xhat, axis=-1, keepdims=True)
    return rstd * (dxhat - m1 - xhat * m2)


def _rotate_half(t):
    n = t.shape[1]
    lane = lax.broadcasted_iota(jnp.int32, t.shape, 1)
    first = (lane & (HEAD_DIM // 2)) == 0
    return jnp.where(first, -pltpu.roll(t, n - HEAD_DIM // 2, 1), pltpu.roll(t, HEAD_DIM // 2, 1))


def _rope(t, cos, sin):
    return t * cos + _rotate_half(t) * sin


def _rope_transposed(g, cos, sin):
    return g * cos - _rotate_half(g * sin)


def _lane_tile(a, reps):
    return jnp.tile(a, (1, reps)) if reps > 1 else a


def _rope_tables(pos_row, inv_freq_row):
    t = pos_row.shape[1]

    def body(pos_ref, f_ref, cos_ref, sin_ref):
        pos_rows = jnp.broadcast_to(pos_ref[...].astype(F32), (LANES, TM)).T
        ang = pos_rows * f_ref[...]
        cos_ref[...] = jnp.cos(ang)
        sin_ref[...] = jnp.sin(ang)

    return pl.pallas_call(
        body,
        name="rope_tables",
        grid=(t // TM,),
        in_specs=[pl.BlockSpec((1, TM), lambda i: (0, i)), _const_spec((1, LANES))],
        out_specs=[_row_spec(TM, LANES), _row_spec(TM, LANES)],
        out_shape=[jax.ShapeDtypeStruct((t, LANES), F32)] * 2,
        compiler_params=_params(("parallel",)),
    )(pos_row, inv_freq_row)


def _in_proj(x, w_in_t, cos, sin, dep=None):
    t = x.shape[0]

    def body(x_ref, w_ref, cos_ref, sin_ref, u_ref, vg_ref, q_ref, k_ref, va_ref):
        xb = x_ref[...].astype(BF16)
        u_ref[...] = _dot(xb, w_ref[0:D_GMLP, :], NT)
        vg_ref[...] = _dot(xb, w_ref[D_GMLP : 2 * D_GMLP, :], NT)
        q = _dot(xb, w_ref[2 * D_GMLP : D_MAIN, :], NT)
        k = _dot(xb, w_ref[D_MAIN : D_MAIN + D_KV, :], NT)
        va_ref[...] = _dot(xb, w_ref[D_MAIN + D_KV : D_IN, :], NT).astype(BF16)
        c, s = cos_ref[...], sin_ref[...]
        q_ref[...] = _rope(q, _lane_tile(c, D_ATTN // LANES), _lane_tile(s, D_ATTN // LANES)).astype(BF16)
        k_ref[...] = _rope(k, c, s).astype(BF16)

    body, in_specs, operands = _after(
        dep, body, [_row_spec(TM, D_MODEL), _const_spec((D_IN, D_MODEL)), _row_spec(TM, LANES), _row_spec(TM, LANES)], [x, w_in_t, cos, sin])
    return pl.pallas_call(
        body,
        name="in_proj",
        grid=(t // TM,),
        in_specs=in_specs,
        out_specs=[_row_spec(TM, D_GMLP), _row_spec(TM, D_GMLP), _row_spec(TM, D_ATTN), _row_spec(TM, D_KV), _row_spec(TM, D_KV)],
        out_shape=[
            jax.ShapeDtypeStruct((t, D_GMLP), F32),
            jax.ShapeDtypeStruct((t, D_GMLP), F32),
            jax.ShapeDtypeStruct((t, D_ATTN), BF16),
            jax.ShapeDtypeStruct((t, D_KV), BF16),
            jax.ShapeDtypeStruct((t, D_KV), BF16),
        ],
        compiler_params=_params(("parallel",)),
    )(*operands)


def _step_rows(i):
    return (i, 0)


def _chunk_before_step(i):
    return (jnp.maximum(CHUNKS_PER_STEP * i - 1, 0), 0)


def _chunk_specs():
    step = CHUNKS_PER_STEP * CHUNK
    return [
        pl.BlockSpec((step, D_GMLP), _step_rows),
        pl.BlockSpec((step, D_GMLP), _step_rows),
        pl.BlockSpec((step, D_ATTN), _step_rows),
        pl.BlockSpec((step, D_KV), _step_rows),
        pl.BlockSpec((CHUNK, D_KV), _chunk_before_step),
        pl.BlockSpec((step, D_KV), _step_rows),
        pl.BlockSpec((CHUNK, D_KV), _chunk_before_step),
    ]


def _half_lane_masks(rows):
    lane = lax.broadcasted_iota(jnp.int32, (rows, LANES), 1)
    return lane < HEAD_DIM


def _kv_variants(kv2):
    left = _half_lane_masks(kv2.shape[0])
    f = kv2.astype(F32)
    swapped = pltpu.roll(f, HEAD_DIM, 1)
    zero = jnp.zeros_like(f)
    g0 = (jnp.where(left, f, zero).astype(BF16), jnp.where(left, zero, swapped).astype(BF16))
    g1 = (jnp.where(left, swapped, zero).astype(BF16), jnp.where(left, zero, f).astype(BF16))
    return (g0, g1)


def _band_mask(i, heads=1):
    row = lax.broadcasted_iota(jnp.int32, (heads * CHUNK, 2 * CHUNK), 0) & (CHUNK - 1)
    col = lax.broadcasted_iota(jnp.int32, (heads * CHUNK, 2 * CHUNK), 1)
    no_prev = jnp.where(i > 0, 0, 4 * CHUNK)
    in_prev = jnp.logical_and(col < CHUNK, (col - row) > no_prev)
    in_cur = jnp.logical_and(col >= CHUNK, (col - CHUNK) <= row)
    return jnp.logical_or(in_prev, in_cur)


def _causal_mask():
    row = lax.broadcasted_iota(jnp.int32, (CHUNK, CHUNK), 0)
    col = lax.broadcasted_iota(jnp.int32, (CHUNK, CHUNK), 1)
    return col <= row


def _store_spatial_weights(w_ref, wcat_ref, wcat_t_ref=None):
    causal = _causal_mask()
    for p in range(D_GMLP // LANES):
        wl = jnp.where(causal, w_ref[2 * p], 0.0)
        wr = jnp.where(causal, w_ref[2 * p + 1], 0.0)
        wcat_ref[p] = jnp.concatenate([wl, wr], axis=1).astype(BF16)
        if wcat_t_ref is not None:
            wcat_t_ref[p] = jnp.concatenate([wl.T, wr.T], axis=1).astype(BF16)


def _pair_stack(xp, left):
    return jnp.concatenate([jnp.where(left, xp, 0.0), jnp.where(left, 0.0, xp)], axis=0).astype(BF16)


def _mixer_fwd(u, vg, q, k, va, v_ln_g, v_ln_b, w_spatial, bias_full, sinks, dep=None):
    t = u.shape[0]

    def body(u_ref, vg_ref, q_ref, kc_ref, kp_ref, vc_ref, vp_ref, g_ref, b_ref, w_ref, bias_ref, sink_ref, cat_ref, wcat):
        i = pl.program_id(0)
        left = _half_lane_masks(CHUNK)

        @pl.when(i == 0)
        def _():
            _store_spatial_weights(w_ref, wcat)

        heads = range(N_HEADS)
        pair_cols = [slice(p * LANES, (p + 1) * LANES) for p in range(D_GMLP // LANES)]
        sinks_h = [sink_ref[h] for h in heads]
        for c in range(CHUNKS_PER_STEP):
            rows = slice(c * CHUNK, (c + 1) * CHUNK)
            before = slice((c - 1) * CHUNK, c * CHUNK)
            k_prev = kp_ref[...] if c == 0 else kc_ref[before, :]
            v_prev = vp_ref[...] if c == 0 else vc_ref[before, :]
            k_var = _kv_variants(jnp.concatenate([k_prev, kc_ref[rows, :]], axis=0))
            v_var = _kv_variants(jnp.concatenate([v_prev, vc_ref[rows, :]], axis=0))
            scores = [_dot(q_ref[rows, pair_cols[h // 2]], k_var[h // 4][h % 2], NT) for h in heads]

            ug = _gelu(u_ref[rows, :])
            xhat, _ = _layer_norm_stats(_gelu(vg_ref[rows, :]))
            vgl = xhat * g_ref[...] + b_ref[...]
            mixed = [_dot(wcat[p], _pair_stack(vgl[:, cols], left)) for p, cols in enumerate(pair_cols)]

            valid = _band_mask(CHUNKS_PER_STEP * i + c)
            masked = [jnp.where(valid, scores[h] * SCALE, NEG_INF) for h in heads]
            maxes = [jnp.maximum(jnp.max(masked[h], axis=1, keepdims=True), sinks_h[h]) for h in heads]
            exps = [jnp.exp(masked[h] - maxes[h]) for h in heads]
            invs = [1.0 / (jnp.sum(exps[h], axis=1, keepdims=True) + jnp.exp(sinks_h[h] - maxes[h])) for h in heads]
            probs = [(exps[h] * invs[h]).astype(BF16) for h in heads]
            for p, cols in enumerate(pair_cols):
                cat_ref[rows, cols] = (ug[:, cols] * (mixed[p] + bias_ref[:, cols])).astype(BF16)
            for p in range(D_ATTN // LANES):
                out = _dot(probs[2 * p], v_var[p // 2][0]) + _dot(probs[2 * p + 1], v_var[p // 2][1])
                cat_ref[rows, D_GMLP + p * LANES : D_GMLP + (p + 1) * LANES] = out.astype(BF16)

    in_specs = _chunk_specs() + [
        _const_spec((1, D_GMLP)),
        _const_spec((1, D_GMLP)),
        _const_spec((N_HEADS, CHUNK, CHUNK)),
        _const_spec((CHUNK, D_GMLP)),
        pl.BlockSpec(memory_space=pltpu.SMEM),
    ]
    body, in_specs, operands = _after(dep, body, in_specs, [u, vg, q, k, k, va, va, v_ln_g, v_ln_b, w_spatial, bias_full, sinks])
    return pl.pallas_call(
        body,
        name="mixer_fwd",
        grid=(t // (CHUNKS_PER_STEP * CHUNK),),
        in_specs=in_specs,
        out_specs=pl.BlockSpec((CHUNKS_PER_STEP * CHUNK, D_MODEL), lambda i: (i, 0)),
        out_shape=jax.ShapeDtypeStruct((t, D_MODEL), BF16),
        scratch_shapes=[pltpu.VMEM((D_GMLP // LANES, CHUNK, 2 * CHUNK), BF16)],
        compiler_params=_params(("arbitrary",)),
    )(*operands)


def _ffn_fwd_loss(cat, x, w_out, ln1_g, ln1_b, w1, w2, ln2_g, ln2_b, target):
    t = x.shape[0]

    def body(cat_ref, x_ref, wo_ref, g1_ref, b1_ref, w1_ref, w2_ref, g2_ref, b2_ref, tgt_ref,
             xh_ref, rstd_ref, x1b_ref, r_ref, dz2_ref, dz2b_ref, dg2_ref, db2_ref, sq_ref):
        @pl.when(pl.program_id(0) == 0)
        def _():
            dg2_ref[...] = jnp.zeros_like(dg2_ref)
            db2_ref[...] = jnp.zeros_like(db2_ref)
            sq_ref[...] = jnp.zeros_like(sq_ref)

        xhat1, rstd1 = _layer_norm_stats(ALPHA * x_ref[...] + _dot(cat_ref[...], wo_ref[...]))
        xh_ref[...] = xhat1
        rstd_ref[...] = rstd1
        x1 = xhat1 * g1_ref[...] + b1_ref[...]
        x1b = x1.astype(BF16)
        x1b_ref[...] = x1b
        ff = jnp.zeros((TM_FFN, D_MODEL), F32)
        for j in range(N_FF_BLOCKS):
            r = jnp.maximum(_dot(x1b, w1_ref[j]), 0.0)
            r_ref[:, j * D_MODEL : (j + 1) * D_MODEL] = r.astype(BF16)
            ff = ff + _dot((r * r).astype(BF16), w2_ref[j])
        xhat2, rstd2 = _layer_norm_stats(ALPHA * x1 + ff)
        err = xhat2 * g2_ref[...] + b2_ref[...] - tgt_ref[...]
        sq_ref[...] += jnp.sum(err * err, axis=0, keepdims=True)
        dy = err * (1.0 / D_MODEL)
        dg2_ref[...] += jnp.sum(dy * xhat2, axis=0, keepdims=True)
        db2_ref[...] += jnp.sum(dy, axis=0, keepdims=True)
        dz2 = _layer_norm_bwd(dy * g2_ref[...], xhat2, rstd2)
        dz2_ref[...] = dz2
        dz2b_ref[...] = dz2.astype(BF16)

    vec = _const_spec((1, D_MODEL))
    tile = _row_spec(TM_FFN, D_MODEL)
    wspec = _const_spec((N_FF_BLOCKS, D_MODEL, D_MODEL), single_buffer=True)
    return pl.pallas_call(
        body,
        name="ffn_fwd_loss",
        grid=(t // TM_FFN,),
        in_specs=[tile, tile, _const_spec((D_MODEL, D_MODEL), single_buffer=True), vec, vec, wspec, wspec, vec, vec, tile],
        out_specs=[tile, _row_spec(TM_FFN, 1), tile, _row_spec(TM_FFN, D_FF), tile, tile, vec, vec, vec],
        out_shape=[
            jax.ShapeDtypeStruct((t, D_MODEL), F32),
            jax.ShapeDtypeStruct((t, 1), F32),
            jax.ShapeDtypeStruct((t, D_MODEL), BF16),
            jax.ShapeDtypeStruct((t, D_FF), BF16),
            jax.ShapeDtypeStruct((t, D_MODEL), F32),
            jax.ShapeDtypeStruct((t, D_MODEL), BF16),
            jax.ShapeDtypeStruct((1, D_MODEL), F32),
            jax.ShapeDtypeStruct((1, D_MODEL), F32),
            jax.ShapeDtypeStruct((1, D_MODEL), F32),
        ],
        compiler_params=_params(("arbitrary",)),
    )(cat, x, w_out, ln1_g, ln1_b, w1, w2, ln2_g, ln2_b, target)


def _ffn_bwd_ln1(dz2, r, xhat1, rstd1, ln1_g, w1, w2, w_out, dep=None):
    t = dz2.shape[0]

    def body(dz2_ref, r_ref, xh_ref, rstd_ref, g1_ref, w1_ref, w2_ref, wo_ref, dpre_ref, dz1_ref, dz1b_ref, dcat_ref, dg1_ref, db1_ref):
        @pl.when(pl.program_id(0) == 0)
        def _():
            dg1_ref[...] = jnp.zeros_like(dg1_ref)
            db1_ref[...] = jnp.zeros_like(db1_ref)

        dz2 = dz2_ref[...]
        dz2b = dz2.astype(BF16)
        dx1 = ALPHA * dz2
        for j in range(N_FF_BLOCKS):
            cols = slice(j * D_MODEL, (j + 1) * D_MODEL)
            dpre = (_dot(dz2b, w2_ref[j], NT) * (2.0 * r_ref[:, cols].astype(F32))).astype(BF16)
            dpre_ref[:, cols] = dpre
            dx1 = dx1 + _dot(dpre, w1_ref[j], NT)
        xhat1 = xh_ref[...]
        dg1_ref[...] += jnp.sum(dx1 * xhat1, axis=0, keepdims=True)
        db1_ref[...] += jnp.sum(dx1, axis=0, keepdims=True)
        dz1 = _layer_norm_bwd(dx1 * g1_ref[...], xhat1, rstd_ref[...])
        dz1_ref[...] = dz1
        dz1b = dz1.astype(BF16)
        dz1b_ref[...] = dz1b
        dcat_ref[...] = _dot(dz1b, wo_ref[...], NT).astype(BF16)

    vec = _const_spec((1, D_MODEL))
    tile = _row_spec(TM_FFN, D_MODEL)
    wspec = _const_spec((N_FF_BLOCKS, D_MODEL, D_MODEL), single_buffer=True)
    body, in_specs, operands = _after(
        dep, body,
        [tile, _row_spec(TM_FFN, D_FF), tile, _row_spec(TM_FFN, 1), vec, wspec, wspec, _const_spec((D_MODEL, D_MODEL), single_buffer=True)],
        [dz2, r, xhat1, rstd1, ln1_g, w1, w2, w_out])
    return pl.pallas_call(
        body,
        name="ffn_bwd_ln1",
        grid=(t // TM_FFN,),
        in_specs=in_specs,
        out_specs=[_row_spec(TM_FFN, D_FF), tile, tile, tile, vec, vec],
        out_shape=[
            jax.ShapeDtypeStruct((t, D_FF), BF16),
            jax.ShapeDtypeStruct((t, D_MODEL), F32),
            jax.ShapeDtypeStruct((t, D_MODEL), BF16),
            jax.ShapeDtypeStruct((t, D_MODEL), BF16),
            jax.ShapeDtypeStruct((1, D_MODEL), F32),
            jax.ShapeDtypeStruct((1, D_MODEL), F32),
        ],
        compiler_params=_params(("arbitrary",)),
    )(*operands)


def _mixer_bwd(u, vg, q, k, va, dcat, cos, sin, v_ln_g, v_ln_b, w_spatial, bias_full, sinks, dep=None):
    t = u.shape[0]
    n_chunks = t // CHUNK

    def body(u_ref, vg_ref, q_ref, kc_ref, kp_ref, vc_ref, vp_ref, dcat_ref, cosc_ref, sinc_ref, cosp_ref, sinp_ref,
             g_ref, b_ref, w_ref, bias_ref, sink_ref,
             dmain_ref, dkv_ref, dg_ref, db_ref, dw_ref, dbs_ref, dsink_ref, dmix_acc, wcat, wcat_t):
        i = pl.program_id(0)
        left = _half_lane_masks(CHUNK)
        lane = lax.broadcasted_iota(jnp.int32, (CHUNK, LANES), 1)
        n_pairs = D_GMLP // LANES

        @pl.when(i == 0)
        def _():
            dg_ref[...] = jnp.zeros_like(dg_ref)
            db_ref[...] = jnp.zeros_like(db_ref)
            dw_ref[...] = jnp.zeros_like(dw_ref)
            dsink_ref[...] = jnp.zeros_like(dsink_ref)
            dmix_acc[...] = jnp.zeros_like(dmix_acc)
            _store_spatial_weights(w_ref, wcat, wcat_t)

        n_qpairs = D_ATTN // LANES
        heads = range(N_HEADS)
        pair_cols = [slice(p * LANES, (p + 1) * LANES) for p in range(n_pairs)]
        sinks_h = [sink_ref[h] for h in heads]
        gain = g_ref[...]
        causal = _causal_mask()
        lane_row = lax.broadcasted_iota(jnp.int32, (1, LANES), 1)
        heads_per_group = N_HEADS // 2

        def group_grad_t(lhs_t, rhs_heads):
            parts = []
            for g in range(2):
                group = range(g * heads_per_group, (g + 1) * heads_per_group)
                lhs = jnp.concatenate([lhs_t[h * HEAD_DIM : (h + 1) * HEAD_DIM] for h in group], axis=1)
                parts.append(_dot(lhs, jnp.concatenate([rhs_heads[h] for h in group], axis=0)))
            return jnp.concatenate(parts, axis=0)

        for c in range(CHUNKS_PER_STEP):
            chunk = CHUNKS_PER_STEP * i + c
            rows = slice(c * CHUNK, (c + 1) * CHUNK)
            before = slice((c - 1) * CHUNK, c * CHUNK)

            k_prev = kp_ref[...] if c == 0 else kc_ref[before, :]
            v_prev = vp_ref[...] if c == 0 else vc_ref[before, :]
            k_var = _kv_variants(jnp.concatenate([k_prev, kc_ref[rows, :]], axis=0))
            v_var = _kv_variants(jnp.concatenate([v_prev, vc_ref[rows, :]], axis=0))
            q_pairs = [q_ref[rows, cols] for cols in pair_cols]
            do_all = dcat_ref[rows, D_GMLP:D_MODEL]
            do_pairs = [do_all[:, cols] for cols in pair_cols]
            scores = [_dot(q_pairs[h // 2], k_var[h // 4][h % 2], NT) for h in heads]
            dprobs = [_dot(do_pairs[h // 2], v_var[h // 4][h % 2], NT) for h in heads]
            q_t = q_ref[rows, :].astype(F32).T.astype(BF16)
            do_t = do_all.astype(F32).T.astype(BF16)

            ug, dug_du = _gelu_and_grad(u_ref[rows, :])
            gv, dgv_dv = _gelu_and_grad(vg_ref[rows, :])
            xhat, rstd = _layer_norm_stats(gv)
            vgl = xhat * gain + b_ref[...]
            mixed = [_dot(wcat[p], _pair_stack(vgl[:, cols], left)) for p, cols in enumerate(pair_cols)]

            valid = _band_mask(chunk)
            masked = [jnp.where(valid, scores[h] * SCALE, NEG_INF) for h in heads]
            maxes = [jnp.maximum(jnp.max(masked[h], axis=1, keepdims=True), sinks_h[h]) for h in heads]
            exps = [jnp.exp(masked[h] - maxes[h]) for h in heads]
            exp_sinks = [jnp.exp(sinks_h[h] - maxes[h]) for h in heads]
            invs = [1.0 / (jnp.sum(exps[h], axis=1, keepdims=True) + exp_sinks[h]) for h in heads]
            probs = [exps[h] * invs[h] for h in heads]
            dsums = [jnp.sum(probs[h] * dprobs[h], axis=1, keepdims=True) for h in heads]
            ds_b = [(probs[h] * (dprobs[h] - dsums[h]) * SCALE).astype(BF16) for h in heads]
            probs_b = [probs[h].astype(BF16) for h in heads]

            dm_stacks = []
            for p, cols in enumerate(pair_cols):
                da = dcat_ref[rows, cols].astype(F32)
                dmain_ref[rows, cols] = (da * (mixed[p] + bias_ref[:, cols]) * dug_du[:, cols]).astype(BF16)
                dmixed = da * ug[:, cols]
                dmix_acc[:, cols] += dmixed
                dm_stacks.append(_pair_stack(dmixed, left))

            dq_all = jnp.concatenate(
                [_dot(ds_b[2 * p], k_var[p // 2][0]) + _dot(ds_b[2 * p + 1], k_var[p // 2][1]) for p in range(n_qpairs)], axis=1)
            dk2_t = group_grad_t(q_t, ds_b)
            dv2_t = group_grad_t(do_t, probs_b)

            for p, cols in enumerate(pair_cols):
                dw_pair = _dot(dm_stacks[p], vgl[:, cols].astype(BF16), NT)
                dw_ref[2 * p] += jnp.where(causal, dw_pair[:CHUNK], 0.0)
                dw_ref[2 * p + 1] += jnp.where(causal, dw_pair[CHUNK:], 0.0)
            dvgl = jnp.concatenate([_dot(wcat_t[p], dm_stacks[p]) for p in range(n_pairs)], axis=1)

            dsink_row = jnp.zeros((1, LANES), F32)
            for h in heads:
                d_sink = -jnp.sum(exp_sinks[h] * invs[h] * dsums[h], axis=0, keepdims=True)
                dsink_row = dsink_row + jnp.where(lane_row == h, d_sink, 0.0)
            dsink_ref[0:1, :] += dsink_row
            cos_c, sin_c = cosc_ref[rows, :], sinc_ref[rows, :]
            cos_p = cosp_ref[...] if c == 0 else cosc_ref[before, :]
            sin_p = sinp_ref[...] if c == 0 else sinc_ref[before, :]
            dmain_ref[rows, 2 * D_GMLP : D_MAIN] = _rope_transposed(dq_all, _lane_tile(cos_c, n_qpairs), _lane_tile(sin_c, n_qpairs)).astype(BF16)
            dk2 = dk2_t.T
            dv2 = dv2_t.T
            cur = pl.ds(pl.multiple_of(chunk * CHUNK, CHUNK), CHUNK)
            dkv_ref[cur, 0:D_KV] = _rope_transposed(dk2[CHUNK:], cos_c, sin_c)
            dkv_ref[cur, D_KV : 2 * D_KV] = dv2[CHUNK:]
            prev = pl.ds(pl.multiple_of(jnp.maximum(chunk - 1, 0) * CHUNK, CHUNK), CHUNK)
            dkv_ref[prev, 0:D_KV] += _rope_transposed(dk2[:CHUNK], cos_p, sin_p)
            dkv_ref[prev, D_KV : 2 * D_KV] += dv2[:CHUNK]

            dg_ref[...] += jnp.sum(dvgl * xhat, axis=0, keepdims=True)
            db_ref[...] += jnp.sum(dvgl, axis=0, keepdims=True)
            dgv = _layer_norm_bwd(dvgl * gain, xhat, rstd)
            dmain_ref[rows, D_GMLP : 2 * D_GMLP] = (dgv * dgv_dv).astype(BF16)

        @pl.when(i == n_chunks // CHUNKS_PER_STEP - 1)
        def _():
            tile = jnp.zeros((CHUNK, LANES), F32)
            for p, cols in enumerate(pair_cols):
                dm = dmix_acc[:, cols]
                sl = jnp.sum(jnp.where(left, dm, 0.0), axis=1, keepdims=True)
                sr = jnp.sum(jnp.where(left, 0.0, dm), axis=1, keepdims=True)
                tile = jnp.where(lane == 2 * p, sl, tile)
                tile = jnp.where(lane == 2 * p + 1, sr, tile)
            dbs_ref[...] = tile

    step = CHUNKS_PER_STEP * CHUNK
    in_specs = _chunk_specs() + [
        pl.BlockSpec((step, D_MODEL), _step_rows),
        pl.BlockSpec((step, LANES), _step_rows),
        pl.BlockSpec((step, LANES), _step_rows),
        pl.BlockSpec((CHUNK, LANES), _chunk_before_step),
        pl.BlockSpec((CHUNK, LANES), _chunk_before_step),
        _const_spec((1, D_GMLP)),
        _const_spec((1, D_GMLP)),
        _const_spec((N_HEADS, CHUNK, CHUNK)),
        _const_spec((CHUNK, D_GMLP)),
        pl.BlockSpec(memory_space=pltpu.SMEM),
    ]
    body, in_specs, operands = _after(
        dep, body, in_specs, [u, vg, q, k, k, va, va, dcat, cos, sin, cos, sin, v_ln_g, v_ln_b, w_spatial, bias_full, sinks])
    return pl.pallas_call(
        body,
        name="mixer_bwd",
        grid=(n_chunks // CHUNKS_PER_STEP,),
        in_specs=in_specs,
        out_specs=[
            pl.BlockSpec((step, D_MAIN), _step_rows),
            _const_spec((t, 2 * D_KV)),
            _const_spec((1, D_GMLP)),
            _const_spec((1, D_GMLP)),
            _const_spec((N_HEADS, CHUNK, CHUNK)),
            _const_spec((CHUNK, LANES)),
            _const_spec((8, LANES)),
        ],
        out_shape=[
            jax.ShapeDtypeStruct((t, D_MAIN), BF16),
            jax.ShapeDtypeStruct((t, 2 * D_KV), F32),
            jax.ShapeDtypeStruct((1, D_GMLP), F32),
            jax.ShapeDtypeStruct((1, D_GMLP), F32),
            jax.ShapeDtypeStruct((N_HEADS, CHUNK, CHUNK), F32),
            jax.ShapeDtypeStruct((CHUNK, LANES), F32),
            jax.ShapeDtypeStruct((8, LANES), F32),
        ],
        scratch_shapes=[
            pltpu.VMEM((CHUNK, D_GMLP), F32),
            pltpu.VMEM((D_GMLP // LANES, CHUNK, 2 * CHUNK), BF16),
            pltpu.VMEM((D_GMLP // LANES, CHUNK, 2 * CHUNK), BF16),
        ],
        compiler_params=_params(("arbitrary",)),
    )(*operands)


def _grad_x(dh_main, dkv, dz1, w_in_t, dep=None):
    t = dz1.shape[0]

    def body(dm_ref, dkv_ref, dz1_ref, w_ref, gx_ref):
        acc = ALPHA * dz1_ref[...] + _dot(dm_ref[...], w_ref[0:D_MAIN, :])
        gx_ref[...] = acc + _dot(dkv_ref[...].astype(BF16), w_ref[D_MAIN:D_IN, :])

    body, in_specs, operands = _after(
        dep, body, [_row_spec(TM, D_MAIN), _row_spec(TM, 2 * D_KV), _row_spec(TM, D_MODEL), _const_spec((D_IN, D_MODEL))], [dh_main, dkv, dz1, w_in_t])
    return pl.pallas_call(
        body,
        name="grad_x",
        grid=(t // TM,),
        in_specs=in_specs,
        out_specs=_row_spec(TM, D_MODEL),
        out_shape=jax.ShapeDtypeStruct((t, D_MODEL), F32),
        compiler_params=_params(("parallel",)),
    )(*operands)


def _token_contraction(name, out_rows, tk, in_arrays, contributions, dep=None):
    t = in_arrays[0].shape[0]

    def body(*refs):
        out_ref = refs[-1]

        @pl.when(pl.program_id(0) == 0)
        def _():
            out_ref[...] = jnp.zeros_like(out_ref)

        for row0, a, b in contributions(*refs[:-1]):
            out_ref[row0 : row0 + a.shape[1], :] += _dot(a, b, TN)

    in_specs = [_row_spec(tk, a.shape[1]) for a in in_arrays]
    body, in_specs, operands = _after(dep, body, in_specs, in_arrays)
    return pl.pallas_call(
        body,
        name=name,
        grid=(t // tk,),
        in_specs=in_specs,
        out_specs=_const_spec((out_rows, D_MODEL), single_buffer=True),
        out_shape=jax.ShapeDtypeStruct((out_rows, D_MODEL), F32),
        compiler_params=_params(("arbitrary",)),
    )(*operands)


def _grad_w_in_t(dh_main, dkv, x, dep=None):
    def contributions(dm_ref, dkv_ref, x_ref):
        xb = x_ref[...].astype(BF16)
        return [(0, dm_ref[...], xb), (D_MAIN, dkv_ref[...].astype(BF16), xb)]

    return _token_contraction("grad_w_in", D_IN, TK, [dh_main, dkv, x], contributions, dep)


def _grad_w_out(cat, dz1b, dep=None):
    def contributions(cat_ref, dz1_ref):
        return [(0, cat_ref[...], dz1_ref[...])]

    return _token_contraction("grad_w_out", D_MODEL, TK, [cat, dz1b], contributions, dep)


def _grad_w_ff1(x1b, dpre):
    def contributions(x1_ref, dpre_ref):
        x1 = x1_ref[...]
        return [(j * D_MODEL, x1, dpre_ref[:, j * D_MODEL : (j + 1) * D_MODEL]) for j in range(N_FF_BLOCKS)]

    return _token_contraction("grad_w_ff1", D_FF, TK_FF, [x1b, dpre], contributions)


def _grad_w_ff2(r, dz2b):
    def contributions(r_ref, dz2_ref):
        dz2 = dz2_ref[...]
        out = []
        for j in range(N_FF_BLOCKS):
            rf = r_ref[:, j * D_MODEL : (j + 1) * D_MODEL].astype(F32)
            out.append((j * D_MODEL, (rf * rf).astype(BF16), dz2))
        return out

    return _token_contraction("grad_w_ff2", D_FF, TK_FF, [r, dz2b], contributions)


ANY = pl.BlockSpec(memory_space=pl.ANY)


def _mesh_position():
    return lax.axis_index("x"), lax.axis_index("y"), lax.axis_index("c")


def _other_chips(x, y):
    return [(1 - x, y), (x, 1 - y), (1 - x, 1 - y)]


def _remote(src, dst, send_sem, recv_sem, device):
    return pltpu.make_async_remote_copy(src_ref=src, dst_ref=dst, send_sem=send_sem, recv_sem=recv_sem, device_id=device, device_id_type=MESH)


def _rows(ref, start, size):
    return ref.at[pl.ds(start, size), :]


def _all_gather_weights(shards):
    n = len(shards)
    per = 7

    def body(*refs):
        ins, outs = refs[:n], refs[n : 2 * n]
        send_sems, recv_sems = refs[2 * n :]
        x, y, c = _mesh_position()
        me = 2 * x + y
        chips = _other_chips(x, y)
        sibling = (x, y, 1 - c)
        started = []
        for w in range(n):
            rows = shards[w].shape[0]
            half = rows // 2
            for kk, (px, py) in enumerate(chips):
                cp = _remote(_rows(ins[w], c * half, half), _rows(outs[w], me * rows + c * half, half),
                             send_sems.at[per * w + kk], recv_sems.at[per * w + kk], (px, py, c))
                cp.start()
                started.append(cp)
            cp = _remote(ins[w], _rows(outs[w], me * rows, rows), send_sems.at[per * w + 6], recv_sems.at[per * w + 6], sibling)
            cp.start()
            started.append(cp)
        for w in range(n):
            rows = shards[w].shape[0]
            half = rows // 2
            for kk, (px, py) in enumerate(chips):
                blk = _rows(outs[w], (2 * px + py) * rows + c * half, half)
                _remote(blk, blk, send_sems.at[per * w + kk], recv_sems.at[per * w + kk], (px, py, c)).wait_recv()
                fwd = _remote(blk, blk, send_sems.at[per * w + 3 + kk], recv_sems.at[per * w + 3 + kk], sibling)
                fwd.start()
                started.append(fwd)
        for w in range(n):
            rows = shards[w].shape[0]
            half = rows // 2
            for kk, (px, py) in enumerate(chips):
                blk = _rows(outs[w], (2 * px + py) * rows + (1 - c) * half, half)
                _remote(blk, blk, send_sems.at[per * w + 3 + kk], recv_sems.at[per * w + 3 + kk], sibling).wait_recv()
            own = _rows(outs[w], me * rows, rows)
            _remote(own, own, send_sems.at[per * w + 6], recv_sems.at[per * w + 6], sibling).wait_recv()
        for cp in started:
            cp.wait_send()

    return pl.pallas_call(
        body,
        name="all_gather_weights",
        in_specs=[ANY] * n,
        out_specs=[ANY] * n,
        out_shape=[jax.ShapeDtypeStruct((N_CHIPS * s.shape[0], s.shape[1]), s.dtype) for s in shards],
        scratch_shapes=[pltpu.SemaphoreType.DMA((per * n,)), pltpu.SemaphoreType.DMA((per * n,))],
    )(*shards)


def _pair_gather(name, shards):
    n = len(shards)

    def body(*refs):
        outs = refs[n : 2 * n]
        send_sems, recv_sems = refs[2 * n :]
        x, y, c = _mesh_position()
        sibling = (x, y, 1 - c)
        sends = []
        for w in range(n):
            half = shards[w].shape[0] // 2
            mine = _rows(outs[w], c * half, half)
            cp = _remote(mine, mine, send_sems.at[w], recv_sems.at[w], sibling)
            cp.start()
            sends.append(cp)
        for w in range(n):
            half = shards[w].shape[0] // 2
            blk = _rows(outs[w], (1 - c) * half, half)
            _remote(blk, blk, send_sems.at[w], recv_sems.at[w], sibling).wait_recv()
        for cp in sends:
            cp.wait_send()

    return pl.pallas_call(
        body,
        name=name,
        in_specs=[ANY] * n,
        out_specs=[ANY] * n,
        out_shape=[jax.ShapeDtypeStruct(s.shape, s.dtype) for s in shards],
        input_output_aliases={w: w for w in range(n)},
        scratch_shapes=[pltpu.SemaphoreType.DMA((n,)), pltpu.SemaphoreType.DMA((n,))],
    )(*shards)


def _all_reduce_small(slab, dep=None):
    rows = slab.shape[0]
    part = rows // 8

    def body(slab_ref, out_ref, landing, reduced, send_sems, recv_sems):
        x, y, c = _mesh_position()
        me = 4 * x + 2 * y + c
        flips = [(k >> 2, (k >> 1) & 1, k & 1) for k in range(1, 8)]

        def peer(flip):
            fx, fy, fc = flip
            return (1 - x if fx else x, 1 - y if fy else y, 1 - c if fc else c)

        def my_rows(ref):
            return ref.at[pl.ds(pl.multiple_of(me * part, 8), part), :]

        sends = []
        for kk, flip in enumerate(flips):
            px, py, pc = peer(flip)
            them = 4 * px + 2 * py + pc
            cp = _remote(slab_ref.at[pl.ds(pl.multiple_of(them * part, 8), part), :], landing.at[me], send_sems.at[kk], recv_sems.at[kk], (px, py, pc))
            cp.start()
            sends.append(cp)
        landing[me] = my_rows(slab_ref)[...]
        for kk, flip in enumerate(flips):
            px, py, pc = peer(flip)
            them = 4 * px + 2 * py + pc
            _remote(landing.at[them], landing.at[them], send_sems.at[kk], recv_sems.at[kk], (px, py, pc)).wait_recv()
        total = landing[0]
        for s in range(1, 8):
            total = total + landing[s]
        reduced[...] = total
        my_rows(out_ref)[...] = total
        for kk, flip in enumerate(flips):
            cp = _remote(reduced, my_rows(out_ref), send_sems.at[7 + kk], recv_sems.at[7 + kk], peer(flip))
            cp.start()
            sends.append(cp)
        for kk, flip in enumerate(flips):
            px, py, pc = peer(flip)
            them = 4 * px + 2 * py + pc
            blk = out_ref.at[pl.ds(pl.multiple_of(them * part, 8), part), :]
            _remote(blk, blk, send_sems.at[7 + kk], recv_sems.at[7 + kk], (px, py, pc)).wait_recv()
        for cp in sends:
            cp.wait_send()

    vmem = pl.BlockSpec(memory_space=pltpu.VMEM)
    body, in_specs, operands = _after(dep, body, [vmem], [slab])
    return pl.pallas_call(
        body,
        name="all_reduce_small",
        in_specs=in_specs,
        out_specs=vmem,
        out_shape=jax.ShapeDtypeStruct(slab.shape, slab.dtype),
        scratch_shapes=[pltpu.VMEM((8, part, LANES), F32), pltpu.VMEM((part, LANES), F32), pltpu.SemaphoreType.DMA((14,)), pltpu.SemaphoreType.DMA((14,))],
    )(*operands)


HBM = pl.BlockSpec(memory_space=pltpu.HBM)
SEM = pl.BlockSpec(memory_space=pltpu.SEMAPHORE)
DATAFLOW = pltpu.SideEffectType.DATAFLOW_SIDE_EFFECTING
TOKEN = jax.ShapeDtypeStruct((8, LANES), F32)


def _plan_copies(bufs, plan, send_sems, recv_sems):
    out = []
    for i, (src, src_row, dst, dst_row, recv_row, rows, device) in enumerate(plan):
        send = _remote(_rows(bufs[src], src_row, rows), _rows(bufs[dst], dst_row, rows), send_sems.at[i], recv_sems.at[i], device)
        landed = _rows(bufs[dst], recv_row, rows)
        recv = _remote(landed, landed, send_sems.at[i], recv_sems.at[i], device)
        out.append((send, recv))
    return out


def _split_call(name, bufs, wait=None, start=None, after=None):
    n = len(bufs)
    n_in = n + (2 if wait else 0) + (1 if after is not None else 0)
    n_start = len(start(0, 0, 0)) if start else 0

    def body(*refs):
        ins = refs[:n]
        x, y, c = _mesh_position()
        if wait:
            for send, recv in _plan_copies(ins, wait[0](x, y, c), refs[n], refs[n + 1]):
                recv.wait_recv()
                send.wait_send()
        if start:
            for send, _ in _plan_copies(ins, start(x, y, c), refs[n_in + n + 1], refs[n_in + n + 2]):
                send.start()
        token = refs[n_in + n]
        token[...] = jnp.zeros_like(token)

    operands = [pltpu.with_memory_space_constraint(b, pltpu.HBM) for b in bufs]
    in_specs = [HBM] * n
    if wait:
        operands += [wait[1], wait[2]]
        in_specs += [SEM, SEM]
    if after is not None:
        operands.append(after)
        in_specs.append(ANY)
    out_shape = [pltpu.HBM(b.shape, b.dtype) for b in bufs] + [TOKEN]
    out_specs = [HBM] * n + [pl.BlockSpec(memory_space=pltpu.VMEM)]
    if start:
        out_shape += [pltpu.SemaphoreType.DMA((n_start,)), pltpu.SemaphoreType.DMA((n_start,))]
        out_specs += [SEM, SEM]
    outs = pl.pallas_call(
        body,
        name=name,
        in_specs=in_specs,
        out_specs=out_specs,
        out_shape=out_shape,
        input_output_aliases={i: i for i in range(n)},
        compiler_params=pltpu.CompilerParams(has_side_effects=DATAFLOW),
    )(*operands)
    return (list(outs[:n]), outs[n]) + tuple(outs[n + 1 :])


def _gather_plans(shard_rows):
    n = len(shard_rows)

    def neighbours(x, y):
        return ((1 - x, y), (x, 1 - y))

    def direct(x, y, c):
        me = 2 * x + y
        plan = []
        for w, rows in enumerate(shard_rows):
            half = rows // 2
            for px, py in neighbours(x, y):
                plan.append((w, c * half, n + w, me * rows + c * half, (2 * px + py) * rows + c * half, half, (px, py, c)))
            plan.append((w, 0, n + w, me * rows, me * rows, rows, (x, y, 1 - c)))
        return plan

    def passed_on(x, y, c):
        (xn, yn), diagonal = neighbours(x, y), 2 * (1 - x) + (1 - y)
        relayed = (1 - c) * (2 * xn[0] + xn[1]) + c * (2 * yn[0] + yn[1])
        target = (x * (1 - c) + (1 - x) * c, (1 - y) * (1 - c) + y * c, c)
        plan = []
        for w, rows in enumerate(shard_rows):
            half = rows // 2
            for px, py in (xn, yn):
                row = (2 * px + py) * rows
                plan.append((n + w, row + c * half, n + w, row + c * half, row + (1 - c) * half, half, (x, y, 1 - c)))
            plan.append((n + w, relayed * rows + c * half, n + w, relayed * rows + c * half, diagonal * rows + c * half, half, target))
        return plan

    def diagonal_passed_on(x, y, c):
        plan = []
        for w, rows in enumerate(shard_rows):
            half = rows // 2
            row = (2 * (1 - x) + (1 - y)) * rows
            plan.append((n + w, row + c * half, n + w, row + c * half, row + (1 - c) * half, half, (x, y, 1 - c)))
        return plan

    return direct, passed_on, diagonal_passed_on


def _swap_plan(block_rows):
    n = len(block_rows)

    def plan_fn(x, y, c):
        plan = []
        for w, rows in enumerate(block_rows):
            half = rows // 2
            for j in range(N_CHIPS):
                plan.append((w, j * rows + (1 - c) * half, n + w, j * half, j * half, half, (x, y, 1 - c)))
        return plan

    return plan_fn


def _exchange_plan(halves):
    n = len(halves)

    def plan_fn(x, y, c):
        plan = []
        for w, half in enumerate(halves):
            for kk, (px, py) in enumerate(_other_chips(x, y)):
                plan.append((w, (2 * px + py) * half, n + w, kk * half, kk * half, half, (px, py, c)))
        return plan

    return plan_fn


def _landing(rows, cols, dtype):
    return lax.empty((rows, cols), dtype)


def _row_tile(rows, cap=512):
    best = 8
    for cand in range(8, cap + 1, 8):
        if rows % cand == 0:
            best = cand
    return best


def _pair_sum(name, grad, theirs, pos):
    half = theirs.shape[0] // N_CHIPS
    cols = theirs.shape[1]
    tile = _row_tile(half)
    steps = half // tile

    def body(pos_ref, g_ref, t_ref, p_ref, own_ref):
        total = g_ref[...] + t_ref[...]
        p_ref[...] = total.astype(BF16)

        @pl.when(pl.program_id(1) == pos_ref[1])
        def _():
            own_ref[...] = total

    return pl.pallas_call(
        body,
        name=name,
        grid_spec=pltpu.PrefetchScalarGridSpec(
            num_scalar_prefetch=1,
            grid=(steps, N_CHIPS),
            in_specs=[
                pl.BlockSpec((tile, cols), lambda i, j, pos: ((2 * j + pos[0]) * steps + i, 0)),
                pl.BlockSpec((tile, cols), lambda i, j, pos: (j * steps + i, 0)),
            ],
            out_specs=[
                pl.BlockSpec((tile, cols), lambda i, j, pos: (j * steps + i, 0)),
                pl.BlockSpec((tile, cols), lambda i, j, pos: (i, 0)),
            ],
        ),
        out_shape=[jax.ShapeDtypeStruct((N_CHIPS * half, cols), BF16), jax.ShapeDtypeStruct((half, cols), F32)],
        compiler_params=_params(("parallel", "arbitrary")),
    )(pos, grad, theirs)


def _chip_sum(name, own, landed, pos):
    half, cols = own.shape
    tile = _row_tile(half)
    steps = half // tile

    def body(pos_ref, own_ref, l0, l1, l2, o_ref):
        o_ref[...] = ((own_ref[...] + l0[...].astype(F32)) + l1[...].astype(F32)) + l2[...].astype(F32)

    landed_specs = [pl.BlockSpec((tile, cols), lambda i, pos, _k=k: (_k * steps + i, 0)) for k in range(N_CHIPS - 1)]
    return pl.pallas_call(
        body,
        name=name,
        grid_spec=pltpu.PrefetchScalarGridSpec(
            num_scalar_prefetch=1,
            grid=(steps,),
            in_specs=[pl.BlockSpec((tile, cols), lambda i, pos: (i, 0))] + landed_specs,
            out_specs=pl.BlockSpec((tile, cols), lambda i, pos: (pos[0] * steps + i, 0)),
        ),
        out_shape=jax.ShapeDtypeStruct((2 * half, cols), F32),
        compiler_params=_params(("parallel",)),
    )(pos, own, landed, landed, landed)


def _adamw(name, w, g, m, v):
    rows, cols = w.shape
    tile = rows if rows * cols <= 256 * 1024 else _row_tile(rows)

    def body(w_ref, g_ref, m_ref, v_ref, g_out_ref, d_ref, nm_ref, nv_ref):
        g = g_ref[...]
        g_out_ref[...] = g
        nm = ADAM_B1 * m_ref[...] + (1.0 - ADAM_B1) * g
        nv = ADAM_B2 * v_ref[...] + (1.0 - ADAM_B2) * (g * g)
        m_hat = nm / (1.0 - ADAM_B1**ADAM_STEP)
        v_hat = nv / (1.0 - ADAM_B2**ADAM_STEP)
        d_ref[...] = -ADAM_LR * (m_hat / (jnp.sqrt(v_hat) + ADAM_EPS) + ADAM_WD * w_ref[...])
        nm_ref[...] = nm
        nv_ref[...] = nv

    spec = _row_spec(tile, cols)
    return pl.pallas_call(
        body,
        name=name,
        grid=(rows // tile,),
        in_specs=[spec] * 4,
        out_specs=[spec] * 4,
        out_shape=[jax.ShapeDtypeStruct((rows, cols), F32)] * 4,
        compiler_params=_params(("parallel",)),
    )(w, g, m, v)


_SMALL = (
    ("v_ln_g", (D_GMLP,), 8),
    ("v_ln_b", (D_GMLP,), 8),
    ("w_spatial", (N_HEADS, CHUNK, CHUNK), 1024),
    ("b_spatial", (N_HEADS, CHUNK), 8),
    ("sinks", (N_HEADS,), 8),
    ("ln1_g", (D_MODEL,), 8),
    ("ln1_b", (D_MODEL,), 8),
    ("ln2_g", (D_MODEL,), 8),
    ("ln2_b", (D_MODEL,), 8),
    ("squared_error", (D_MODEL,), 8),
)
N_SMALL_PARAMS = len(_SMALL) - 1


def _pack_small(values):
    parts = []
    for (name, shape, rows), val in zip(_SMALL, values, strict=True):
        flat = val.reshape(-1).astype(F32)
        parts.append(jnp.pad(flat, (0, rows * LANES - flat.shape[0])).reshape(rows, LANES))
    parts.append(jnp.zeros((SMALL_ROWS - sum(rows for _, _, rows in _SMALL), LANES), F32))
    return jnp.concatenate(parts, axis=0)


def _adamw_update(w, g, m, v):
    nm = ADAM_B1 * m + (1.0 - ADAM_B1) * g
    nv = ADAM_B2 * v + (1.0 - ADAM_B2) * (g * g)
    m_hat = nm / (1.0 - ADAM_B1**ADAM_STEP)
    v_hat = nv / (1.0 - ADAM_B2**ADAM_STEP)
    return -ADAM_LR * (m_hat / (jnp.sqrt(v_hat) + ADAM_EPS) + ADAM_WD * w), nm, nv


def _adamw_small(g_slab, params, first, second):
    n = N_SMALL_PARAMS

    def pieces(shape):
        if len(shape) == 3:
            return [((0, h), h * shape[1], shape[1], shape[2]) for h in range(shape[0])]
        if len(shape) == 2:
            return [((0,), 0, shape[0], shape[1])]
        if shape[0] >= LANES:
            return [((slice(None), slice(r * LANES, (r + 1) * LANES)), r, 1, LANES) for r in range(shape[0] // LANES)]
        return [((slice(None), slice(0, shape[0])), 0, 1, shape[0])]

    def body(*refs):
        g_ref = refs[0]
        w_refs, m_refs, v_refs = refs[1 : 1 + n], refs[1 + n : 1 + 2 * n], refs[1 + 2 * n : 1 + 3 * n]
        outs = refs[1 + 3 * n :]
        row0 = 0
        for idx, (_, shape, rows) in enumerate(_SMALL[:n]):
            for where, first_row, n_rows, lanes in pieces(shape):
                g = g_ref[row0 + first_row : row0 + first_row + n_rows, 0:lanes]
                delta, nm, nv = _adamw_update(w_refs[idx][where], g, m_refs[idx][where], v_refs[idx][where])
                for group, val in enumerate((g, delta, nm, nv)):
                    outs[group * n + idx][where] = val
            row0 += rows

    vmem = pl.BlockSpec(memory_space=pltpu.VMEM)
    shapes = [jax.ShapeDtypeStruct(p.shape, F32) for p in params]
    outs = pl.pallas_call(
        body,
        name="adamw_small",
        in_specs=[vmem] * (1 + 3 * n),
        out_specs=[vmem] * (4 * n),
        out_shape=shapes * 4,
        compiler_params=_params(),
    )(g_slab, *params, *first, *second)
    return [list(outs[group * n : (group + 1) * n]) for group in range(4)]


def kernel(x, positions, w_in, v_ln_g, v_ln_b, w_spatial, b_spatial, sinks, w_out, ln1_g, ln1_b, w_ff1, w_ff2, ln2_g, ln2_b, loss_target, m_w_in, m_v_ln_g, m_v_ln_b, m_w_spatial, m_b_spatial, m_sinks, m_w_out, m_ln1_g, m_ln1_b, m_w_ff1, m_w_ff2, m_ln2_g, m_ln2_b, v_w_in, v_v_ln_g, v_v_ln_b, v_w_spatial, v_b_spatial, v_sinks, v_w_out, v_ln1_g, v_ln1_b, v_w_ff1, v_w_ff2, v_ln2_g, v_ln2_b):
    t = x.shape[1]
    x2 = x.reshape(t, D_MODEL)
    target = loss_target.reshape(t, D_MODEL)

    (w_in_t,) = _all_gather_weights([w_in[0].T.astype(BF16)])
    later = [w_out[0].astype(BF16), w_ff1[0].astype(BF16), w_ff2[0].astype(BF16)]
    later_rows = [s.shape[0] for s in later]
    direct_plan, pass_plan, diagonal_plan = _gather_plans(later_rows)
    bufs, started, direct_send, direct_recv = _split_call(
        "gather_start", later + [_landing(N_CHIPS * r, D_MODEL, BF16) for r in later_rows], start=direct_plan, after=w_in_t)

    inv_freq = ROPE_THETA ** (-jnp.arange(0, HEAD_DIM, 2, dtype=F32) / HEAD_DIM)
    cos, sin = _rope_tables(positions, jnp.tile(inv_freq, LANES // (HEAD_DIM // 2)).reshape(1, LANES))
    u, vg, q, k, va = _in_proj(x2, w_in_t, cos, sin, dep=started)
    bias_full = jnp.repeat(b_spatial[0].T, HEAD_DIM, axis=1)
    sink_vec = sinks.reshape(N_HEADS)
    bufs, passing, pass_send, pass_recv = _split_call(
        "gather_pass", bufs, wait=(direct_plan, direct_send, direct_recv), start=pass_plan, after=u)
    cat = _mixer_fwd(u, vg, q, k, va, v_ln_g, v_ln_b, w_spatial[0], bias_full, sink_vec, dep=passing)
    bufs, passing, diag_send, diag_recv = _split_call(
        "gather_pass_diagonal", bufs, wait=(pass_plan, pass_send, pass_recv), start=diagonal_plan, after=cat)
    bufs, _ = _split_call("gather_end", bufs, wait=(diagonal_plan, diag_send, diag_recv), after=passing)
    w_out_all = bufs[3]
    w1_all = bufs[4].reshape(N_FF_BLOCKS, D_MODEL, D_MODEL)
    w2_all = bufs[5].reshape(N_FF_BLOCKS, D_MODEL, D_MODEL)
    xhat1, rstd1, x1b, r, dz2, dz2b, d_ln2_g, d_ln2_b, sq_err = _ffn_fwd_loss(
        cat, x2, w_out_all, ln1_g, ln1_b, w1_all, w2_all, ln2_g, ln2_b, target)

    pos = jnp.stack([lax.axis_index("c"), 2 * lax.axis_index("x") + lax.axis_index("y")]).astype(jnp.int32)
    half_landing = lambda g: _landing(g.shape[0] // 2, D_MODEL, F32)
    g_ff2_local = _grad_w_ff2(r, dz2b)
    swap_plan = _swap_plan([D_FF // N_CHIPS])
    ff2_bufs, swapping2, swap2_send, swap2_recv = _split_call("ff2_swap_start", [g_ff2_local, half_landing(g_ff2_local)], start=swap_plan)
    dpre, dz1, dz1b, dcat, d_ln1_g, d_ln1_b = _ffn_bwd_ln1(dz2, r, xhat1, rstd1, ln1_g, w1_all, w2_all, w_out_all, dep=swapping2)
    g_ff1_local = _grad_w_ff1(x1b, dpre)
    ff1_bufs, swapping1, swap1_send, swap1_recv = _split_call("ff1_swap_start", [g_ff1_local, half_landing(g_ff1_local)], start=swap_plan)
    g_out_local = _grad_w_out(cat, dz1b, dep=swapping1)
    ff2_bufs, swapped2 = _split_call("ff2_swap_wait", ff2_bufs, wait=(swap_plan, swap2_send, swap2_recv), after=g_out_local)
    ff1_bufs, _ = _split_call("ff1_swap_wait", ff1_bufs, wait=(swap_plan, swap1_send, swap1_recv), after=swapped2)
    ff_sums = [_pair_sum("grad_pair_sum_w_ff1", ff1_bufs[0], ff1_bufs[1], pos), _pair_sum("grad_pair_sum_w_ff2", ff2_bufs[0], ff2_bufs[1], pos)]
    ff_halves = [p.shape[0] // N_CHIPS for p, _ in ff_sums]
    exchange_plan = _exchange_plan(ff_halves)
    bufs, exchanging, ex_send, ex_recv = _split_call(
        "ff_exchange_start", [p for p, _ in ff_sums] + [_landing(3 * h, D_MODEL, BF16) for h in ff_halves], start=exchange_plan)
    dh_main, dkv, d_v_ln_g, d_v_ln_b, d_w_spatial, d_b_spatial_t, d_sinks = _mixer_bwd(
        u, vg, q, k, va, dcat, cos, sin, v_ln_g, v_ln_b, w_spatial[0], bias_full, sink_vec, dep=exchanging)
    small_g = _all_reduce_small(_pack_small(
        [d_v_ln_g, d_v_ln_b, d_w_spatial, d_b_spatial_t[:, :N_HEADS].T, d_sinks[0, :N_HEADS], d_ln1_g, d_ln1_b, d_ln2_g, d_ln2_b, sq_err]))
    sq_row = sum(rows for _, _, rows in _SMALL[:N_SMALL_PARAMS])
    loss = 0.5 * jnp.sum(small_g[sq_row : sq_row + _SMALL[N_SMALL_PARAMS][2]]) / D_MODEL
    g_in_local = _grad_w_in_t(dh_main, dkv, x2, dep=small_g)

    small = [g_in_local, g_out_local]
    small_swap_plan = _swap_plan([g.shape[0] // N_CHIPS for g in small])
    swap_bufs, small_swapping, ss_send, ss_recv = _split_call(
        "small_swap_start", small + [half_landing(g) for g in small], start=small_swap_plan)
    grad_x_flat = _grad_x(dh_main, dkv, dz1, w_in_t, dep=small_swapping)
    grad_x = grad_x_flat.reshape(1, t, D_MODEL)
    swap_bufs, _ = _split_call("small_swap_wait", swap_bufs, wait=(small_swap_plan, ss_send, ss_recv), after=grad_x_flat)
    pair_sums = [_pair_sum("grad_pair_sum_" + nm, g, th, pos) for nm, g, th in zip(["w_in", "w_out"], swap_bufs[:2], swap_bufs[2:])]
    small_halves = [p.shape[0] // N_CHIPS for p, _ in pair_sums]
    small_plan = _exchange_plan(small_halves)
    small_bufs, small_exchanging, sm_send, sm_recv = _split_call(
        "small_exchange_start", [p for p, _ in pair_sums] + [_landing(3 * h, D_MODEL, BF16) for h in small_halves], start=small_plan)

    bufs, _ = _split_call("ff_exchange_wait", bufs, wait=(exchange_plan, ex_send, ex_recv), after=small_exchanging)
    ff_shards = [_chip_sum("grad_chip_sum_" + nm, own, ld, pos) for nm, (_, own), ld in zip(["w_ff1", "w_ff2"], ff_sums, bufs[2:])]
    g_w_ff1, g_w_ff2 = _pair_gather("grad_pair_gather_ff", ff_shards)

    g_w_ff1, d_w_ff1, nm_w_ff1, nv_w_ff1 = _adamw("adamw_w_ff1", w_ff1[0], g_w_ff1, m_w_ff1[0], v_w_ff1[0])
    g_w_ff2, d_w_ff2, nm_w_ff2, nv_w_ff2 = _adamw("adamw_w_ff2", w_ff2[0], g_w_ff2, m_w_ff2[0], v_w_ff2[0])
    small_bufs, _ = _split_call("small_exchange_wait", small_bufs, wait=(small_plan, sm_send, sm_recv), after=nv_w_ff2)
    shards = [_chip_sum("grad_chip_sum_" + nm, own, ld, pos) for nm, (_, own), ld in zip(["w_in", "w_out"], pair_sums, small_bufs[2:])]
    g_w_in_t, g_w_out = _pair_gather("grad_pair_gather_small", shards)
    g_w_in, d_w_in, nm_w_in, nv_w_in = (a.T for a in _adamw("adamw_w_in", w_in[0].T, g_w_in_t, m_w_in[0].T, v_w_in[0].T))
    g_w_out, d_w_out, nm_w_out, nv_w_out = _adamw("adamw_w_out", w_out[0], g_w_out, m_w_out[0], v_w_out[0])
    small_grads, small_d, small_nm, small_nv = _adamw_small(
        small_g,
        [v_ln_g, v_ln_b, w_spatial, b_spatial, sinks, ln1_g, ln1_b, ln2_g, ln2_b],
        [m_v_ln_g, m_v_ln_b, m_w_spatial, m_b_spatial, m_sinks, m_ln1_g, m_ln1_b, m_ln2_g, m_ln2_b],
        [v_v_ln_g, v_v_ln_b, v_w_spatial, v_b_spatial, v_sinks, v_ln1_g, v_ln1_b, v_ln2_g, v_ln2_b])

    def with_big(small, w_in_v, w_out_v, w_ff1_v, w_ff2_v):
        g_vg, g_vb, g_ws, g_bs, g_sk, g_1g, g_1b, g_2g, g_2b = small
        return [w_in_v[None], g_vg, g_vb, g_ws, g_bs, g_sk, w_out_v[None], g_1g, g_1b, w_ff1_v[None], w_ff2_v[None], g_2g, g_2b]

    return (
        loss,
        grad_x,
        *with_big(small_grads, g_w_in, g_w_out, g_w_ff1, g_w_ff2),
        *with_big(small_d, d_w_in, d_w_out, d_w_ff1, d_w_ff2),
        *with_big(small_nm, nm_w_in, nm_w_out, nm_w_ff1, nm_w_ff2),
        *with_big(small_nv, nv_w_in, nv_w_out, nv_w_ff1, nv_w_ff2),
    )
```

```python
import math

import jax
import jax.numpy as jnp
from jax import lax
from jax.experimental import pallas as pl
from jax.experimental.pallas import tpu as pltpu

F32 = jnp.float32
BF16 = jnp.bfloat16

D_MODEL = 1024
HEAD_DIM = 64
D_GMLP = 512
D_ATTN = 512
D_KV = 128
D_IN = 2 * D_GMLP + D_ATTN + 2 * D_KV
D_MAIN = 2 * D_GMLP + D_ATTN
N_HEADS = 8
CHUNK = 128
CHUNKS_PER_STEP = 4
ROPE_THETA = 10000.0
D_FF = 4 * D_MODEL
N_FF_BLOCKS = 4
LN_EPS = 1e-5
ALPHA = (2.0 * 1) ** 0.25
NEG_INF = -1e30
SCALE = 1.0 / math.sqrt(HEAD_DIM)

ADAM_LR = 0.001
ADAM_B1 = 0.9
ADAM_B2 = 0.999
ADAM_EPS = 1e-08
ADAM_WD = 0.01
ADAM_STEP = 10

N_CHIPS = 4
LANES = 128
V7X_VMEM_BYTES = 64 * 1024 * 1024
VMEM_LIMIT = V7X_VMEM_BYTES - 8 * 1024 * 1024
TM = 512
TM_FFN = 256
TK = 1024
TK_FF = 1024
SMALL_ROWS = 1152
MESH = pl.DeviceIdType.MESH

NT = (((1,), (1,)), ((), ()))
TN = (((0,), (0,)), ((), ()))


def _dot(a, b, dims=None):
    if dims is None:
        return jnp.dot(a, b, preferred_element_type=F32)
    return lax.dot_general(a, b, dims, preferred_element_type=F32)


def _params(semantics=None):
    return pltpu.CompilerParams(dimension_semantics=semantics, vmem_limit_bytes=VMEM_LIMIT)


def _const_spec(shape, single_buffer=False):
    zeros = (0,) * len(shape)
    if single_buffer:
        return pl.BlockSpec(shape, lambda *_: zeros, pipeline_mode=pl.Buffered(1))
    return pl.BlockSpec(shape, lambda *_: zeros)


def _row_spec(rows, cols):
    return pl.BlockSpec((rows, cols), lambda i: (i, 0))


def _after(dep, body, in_specs, operands):
    if dep is None:
        return body, list(in_specs), list(operands)
    return (lambda dep_ref, *refs: body(*refs)), [pl.BlockSpec(memory_space=pl.ANY)] + list(in_specs), [dep] + list(operands)


def _gelu(x):
    k = math.sqrt(2.0 / math.pi)
    return 0.5 * x * (1.0 + jnp.tanh(k * (x + 0.044715 * (x * x * x))))


def _gelu_and_grad(x):
    k = math.sqrt(2.0 / math.pi)
    x2 = x * x
    t = jnp.tanh(k * (x + 0.044715 * (x2 * x)))
    g = 0.5 * x * (1.0 + t)
    dg = 0.5 * (1.0 + t) + 0.5 * x * (1.0 - t * t) * (k * (1.0 + 3.0 * 0.044715 * x2))
    return g, dg


def _layer_norm_stats(z):
    mu = jnp.mean(z, axis=-1, keepdims=True)
    zc = z - mu
    var = jnp.mean(zc * zc, axis=-1, keepdims=True)
    rstd = lax.rsqrt(var + LN_EPS)
    return zc * rstd, rstd


def _layer_norm_bwd(dxhat, xhat, rstd):
    m1 = jnp.mean(dxhat, axis=-1, keepdims=True)
    m2 = jnp.mean(dxhat * xhat, axis=-1, keepdims=True)
    return rstd * (dxhat - m1 - xhat * m2)


def _rotate_half(t):
    n = t.shape[1]
    lane = lax.broadcasted_iota(jnp.int32, t.shape, 1)
    first = (lane & (HEAD_DIM // 2)) == 0
    return jnp.where(first, -pltpu.roll(t, n - HEAD_DIM // 2, 1), pltpu.roll(t, HEAD_DIM // 2, 1))


def _rope(t, cos, sin):
    return t * cos + _rotate_half(t) * sin


def _rope_transposed(g, cos, sin):
    return g * cos - _rotate_half(g * sin)


def _lane_tile(a, reps):
    return jnp.tile(a, (1, reps)) if reps > 1 else a


def _rope_tables(pos_row, inv_freq_row):
    t = pos_row.shape[1]

    def body(pos_ref, f_ref, cos_ref, sin_ref):
        pos_rows = jnp.broadcast_to(pos_ref[...].astype(F32), (LANES, TM)).T
        ang = pos_rows * f_ref[...]
        cos_ref[...] = jnp.cos(ang)
        sin_ref[...] = jnp.sin(ang)

    return pl.pallas_call(
        body,
        name="rope_tables",
        grid=(t // TM,),
        in_specs=[pl.BlockSpec((1, TM), lambda i: (0, i)), _const_spec((1, LANES))],
        out_specs=[_row_spec(TM, LANES), _row_spec(TM, LANES)],
        out_shape=[jax.ShapeDtypeStruct((t, LANES), F32)] * 2,
        compiler_params=_params(("parallel",)),
    )(pos_row, inv_freq_row)


def _in_proj(x, w_in_t, cos, sin, dep=None):
    t = x.shape[0]

    def body(x_ref, w_ref, cos_ref, sin_ref, u_ref, vg_ref, q_ref, k_ref, va_ref):
        xb = x_ref[...].astype(BF16)
        u_ref[...] = _dot(xb, w_ref[0:D_GMLP, :], NT)
        vg_ref[...] = _dot(xb, w_ref[D_GMLP : 2 * D_GMLP, :], NT)
        q = _dot(xb, w_ref[2 * D_GMLP : D_MAIN, :], NT)
        k = _dot(xb, w_ref[D_MAIN : D_MAIN + D_KV, :], NT)
        va_ref[...] = _dot(xb, w_ref[D_MAIN + D_KV : D_IN, :], NT).astype(BF16)
        c, s = cos_ref[...], sin_ref[...]
        q_ref[...] = _rope(q, _lane_tile(c, D_ATTN // LANES), _lane_tile(s, D_ATTN // LANES)).astype(BF16)
        k_ref[...] = _rope(k, c, s).astype(BF16)

    body, in_specs, operands = _after(
        dep, body, [_row_spec(TM, D_MODEL), _const_spec((D_IN, D_MODEL)), _row_spec(TM, LANES), _row_spec(TM, LANES)], [x, w_in_t, cos, sin])
    return pl.pallas_call(
        body,
        name="in_proj",
        grid=(t // TM,),
        in_specs=in_specs,
        out_specs=[_row_spec(TM, D_GMLP), _row_spec(TM, D_GMLP), _row_spec(TM, D_ATTN), _row_spec(TM, D_KV), _row_spec(TM, D_KV)],
        out_shape=[
            jax.ShapeDtypeStruct((t, D_GMLP), F32),
            jax.ShapeDtypeStruct((t, D_GMLP), F32),
            jax.ShapeDtypeStruct((t, D_ATTN), BF16),
            jax.ShapeDtypeStruct((t, D_KV), BF16),
            jax.ShapeDtypeStruct((t, D_KV), BF16),
        ],
        compiler_params=_params(("parallel",)),
    )(*operands)


def _step_rows(i):
    return (i, 0)


def _chunk_before_step(i):
    return (jnp.maximum(CHUNKS_PER_STEP * i - 1, 0), 0)


def _chunk_specs():
    step = CHUNKS_PER_STEP * CHUNK
    return [
        pl.BlockSpec((step, D_GMLP), _step_rows),
        pl.BlockSpec((step, D_GMLP), _step_rows),
        pl.BlockSpec((step, D_ATTN), _step_rows),
        pl.BlockSpec((step, D_KV), _step_rows),
        pl.BlockSpec((CHUNK, D_KV), _chunk_before_step),
        pl.BlockSpec((step, D_KV), _step_rows),
        pl.BlockSpec((CHUNK, D_KV), _chunk_before_step),
    ]


def _half_lane_masks(rows):
    lane = lax.broadcasted_iota(jnp.int32, (rows, LANES), 1)
    return lane < HEAD_DIM


def _kv_variants(kv2):
    left = _half_lane_masks(kv2.shape[0])
    f = kv2.astype(F32)
    swapped = pltpu.roll(f, HEAD_DIM, 1)
    zero = jnp.zeros_like(f)
    g0 = (jnp.where(left, f, zero).astype(BF16), jnp.where(left, zero, swapped).astype(BF16))
    g1 = (jnp.where(left, swapped, zero).astype(BF16), jnp.where(left, zero, f).astype(BF16))
    return (g0, g1)


def _band_mask(i, heads=1):
    row = lax.broadcasted_iota(jnp.int32, (heads * CHUNK, 2 * CHUNK), 0) & (CHUNK - 1)
    col = lax.broadcasted_iota(jnp.int32, (heads * CHUNK, 2 * CHUNK), 1)
    no_prev = jnp.where(i > 0, 0, 4 * CHUNK)
    in_prev = jnp.logical_and(col < CHUNK, (col - row) > no_prev)
    in_cur = jnp.logical_and(col >= CHUNK, (col - CHUNK) <= row)
    return jnp.logical_or(in_prev, in_cur)


def _causal_mask():
    row = lax.broadcasted_iota(jnp.int32, (CHUNK, CHUNK), 0)
    col = lax.broadcasted_iota(jnp.int32, (CHUNK, CHUNK), 1)
    return col <= row


def _store_spatial_weights(w_ref, wcat_ref, wcat_t_ref=None):
    causal = _causal_mask()
    for p in range(D_GMLP // LANES):
        wl = jnp.where(causal, w_ref[2 * p], 0.0)
        wr = jnp.where(causal, w_ref[2 * p + 1], 0.0)
        wcat_ref[p] = jnp.concatenate([wl, wr], axis=1).astype(BF16)
        if wcat_t_ref is not None:
            wcat_t_ref[p] = jnp.concatenate([wl.T, wr.T], axis=1).astype(BF16)


def _pair_stack(xp, left):
    return jnp.concatenate([jnp.where(left, xp, 0.0), jnp.where(left, 0.0, xp)], axis=0).astype(BF16)


def _mixer_fwd(u, vg, q, k, va, v_ln_g, v_ln_b, w_spatial, bias_full, sinks, dep=None):
    t = u.shape[0]

    def body(u_ref, vg_ref, q_ref, kc_ref, kp_ref, vc_ref, vp_ref, g_ref, b_ref, w_ref, bias_ref, sink_ref, cat_ref, wcat):
        i = pl.program_id(0)
        left = _half_lane_masks(CHUNK)

        @pl.when(i == 0)
        def _():
            _store_spatial_weights(w_ref, wcat)

        heads = range(N_HEADS)
        pair_cols = [slice(p * LANES, (p + 1) * LANES) for p in range(D_GMLP // LANES)]
        sinks_h = [sink_ref[h] for h in heads]
        for c in range(CHUNKS_PER_STEP):
            rows = slice(c * CHUNK, (c + 1) * CHUNK)
            before = slice((c - 1) * CHUNK, c * CHUNK)
            k_prev = kp_ref[...] if c == 0 else kc_ref[before, :]
            v_prev = vp_ref[...] if c == 0 else vc_ref[before, :]
            k_var = _kv_variants(jnp.concatenate([k_prev, kc_ref[rows, :]], axis=0))
            v_var = _kv_variants(jnp.concatenate([v_prev, vc_ref[rows, :]], axis=0))
            scores = [_dot(q_ref[rows, pair_cols[h // 2]], k_var[h // 4][h % 2], NT) for h in heads]

            ug = _gelu(u_ref[rows, :])
            xhat, _ = _layer_norm_stats(_gelu(vg_ref[rows, :]))
            vgl = xhat * g_ref[...] + b_ref[...]
            mixed = [_dot(wcat[p], _pair_stack(vgl[:, cols], left)) for p, cols in enumerate(pair_cols)]

            valid = _band_mask(CHUNKS_PER_STEP * i + c)
            masked = [jnp.where(valid, scores[h] * SCALE, NEG_INF) for h in heads]
            maxes = [jnp.maximum(jnp.max(masked[h], axis=1, keepdims=True), sinks_h[h]) for h in heads]
            exps = [jnp.exp(masked[h] - maxes[h]) for h in heads]
            invs = [1.0 / (jnp.sum(exps[h], axis=1, keepdims=True) + jnp.exp(sinks_h[h] - maxes[h])) for h in heads]
            probs = [(exps[h] * invs[h]).astype(BF16) for h in heads]
            for p, cols in enumerate(pair_cols):
                cat_ref[rows, cols] = (ug[:, cols] * (mixed[p] + bias_ref[:, cols])).astype(BF16)
            for p in range(D_ATTN // LANES):
                out = _dot(probs[2 * p], v_var[p // 2][0]) + _dot(probs[2 * p + 1], v_var[p // 2][1])
                cat_ref[rows, D_GMLP + p * LANES : D_GMLP + (p + 1) * LANES] = out.astype(BF16)

    in_specs = _chunk_specs() + [
        _const_spec((1, D_GMLP)),
        _const_spec((1, D_GMLP)),
        _const_spec((N_HEADS, CHUNK, CHUNK)),
        _const_spec((CHUNK, D_GMLP)),
        pl.BlockSpec(memory_space=pltpu.SMEM),
    ]
    body, in_specs, operands = _after(dep, body, in_specs, [u, vg, q, k, k, va, va, v_ln_g, v_ln_b, w_spatial, bias_full, sinks])
    return pl.pallas_call(
        body,
        name="mixer_fwd",
        grid=(t // (CHUNKS_PER_STEP * CHUNK),),
        in_specs=in_specs,
        out_specs=pl.BlockSpec((CHUNKS_PER_STEP * CHUNK, D_MODEL), lambda i: (i, 0)),
        out_shape=jax.ShapeDtypeStruct((t, D_MODEL), BF16),
        scratch_shapes=[pltpu.VMEM((D_GMLP // LANES, CHUNK, 2 * CHUNK), BF16)],
        compiler_params=_params(("arbitrary",)),
    )(*operands)


def _ffn_fwd_loss(cat, x, w_out, ln1_g, ln1_b, w1, w2, ln2_g, ln2_b, target):
    t = x.shape[0]

    def body(cat_ref, x_ref, wo_ref, g1_ref, b1_ref, w1_ref, w2_ref, g2_ref, b2_ref, tgt_ref,
             xh_ref, rstd_ref, x1b_ref, r_ref, dz2_ref, dz2b_ref, dg2_ref, db2_ref, sq_ref):
        @pl.when(pl.program_id(0) == 0)
        def _():
            dg2_ref[...] = jnp.zeros_like(dg2_ref)
            db2_ref[...] = jnp.zeros_like(db2_ref)
            sq_ref[...] = jnp.zeros_like(sq_ref)

        xhat1, rstd1 = _layer_norm_stats(ALPHA * x_ref[...] + _dot(cat_ref[...], wo_ref[...]))
        xh_ref[...] = xhat1
        rstd_ref[...] = rstd1
        x1 = xhat1 * g1_ref[...] + b1_ref[...]
        x1b = x1.astype(BF16)
        x1b_ref[...] = x1b
        ff = jnp.zeros((TM_FFN, D_MODEL), F32)
        for j in range(N_FF_BLOCKS):
            r = jnp.maximum(_dot(x1b, w1_ref[j]), 0.0)
            r_ref[:, j * D_MODEL : (j + 1) * D_MODEL] = r.astype(BF16)
            ff = ff + _dot((r * r).astype(BF16), w2_ref[j])
        xhat2, rstd2 = _layer_norm_stats(ALPHA * x1 + ff)
        err = xhat2 * g2_ref[...] + b2_ref[...] - tgt_ref[...]
        sq_ref[...] += jnp.sum(err * err, axis=0, keepdims=True)
        dy = err * (1.0 / D_MODEL)
        dg2_ref[...] += jnp.sum(dy * xhat2, axis=0, keepdims=True)
        db2_ref[...] += jnp.sum(dy, axis=0, keepdims=True)
        dz2 = _layer_norm_bwd(dy * g2_ref[...], xhat2, rstd2)
        dz2_ref[...] = dz2
        dz2b_ref[...] = dz2.astype(BF16)

    vec = _const_spec((1, D_MODEL))
    tile = _row_spec(TM_FFN, D_MODEL)
    wspec = _const_spec((N_FF_BLOCKS, D_MODEL, D_MODEL), single_buffer=True)
    return pl.pallas_call(
        body,
        name="ffn_fwd_loss",
        grid=(t // TM_FFN,),
        in_specs=[tile, tile, _const_spec((D_MODEL, D_MODEL), single_buffer=True), vec, vec, wspec, wspec, vec, vec, tile],
        out_specs=[tile, _row_spec(TM_FFN, 1), tile, _row_spec(TM_FFN, D_FF), tile, tile, vec, vec, vec],
        out_shape=[
            jax.ShapeDtypeStruct((t, D_MODEL), F32),
            jax.ShapeDtypeStruct((t, 1), F32),
            jax.ShapeDtypeStruct((t, D_MODEL), BF16),
            jax.ShapeDtypeStruct((t, D_FF), BF16),
            jax.ShapeDtypeStruct((t, D_MODEL), F32),
            jax.ShapeDtypeStruct((t, D_MODEL), BF16),
            jax.ShapeDtypeStruct((1, D_MODEL), F32),
            jax.ShapeDtypeStruct((1, D_MODEL), F32),
            jax.ShapeDtypeStruct((1, D_MODEL), F32),
        ],
        compiler_params=_params(("arbitrary",)),
    )(cat, x, w_out, ln1_g, ln1_b, w1, w2, ln2_g, ln2_b, target)


def _ffn_bwd_ln1(dz2, r, xhat1, rstd1, ln1_g, w1, w2, w_out, dep=None):
    t = dz2.shape[0]

    def body(dz2_ref, r_ref, xh_ref, rstd_ref, g1_ref, w1_ref, w2_ref, wo_ref, dpre_ref, dz1_ref, dz1b_ref, dcat_ref, dg1_ref, db1_ref):
        @pl.when(pl.program_id(0) == 0)
        def _():
            dg1_ref[...] = jnp.zeros_like(dg1_ref)
            db1_ref[...] = jnp.zeros_like(db1_ref)

        dz2 = dz2_ref[...]
        dz2b = dz2.astype(BF16)
        dx1 = ALPHA * dz2
        for j in range(N_FF_BLOCKS):
            cols = slice(j * D_MODEL, (j + 1) * D_MODEL)
            dpre = (_dot(dz2b, w2_ref[j], NT) * (2.0 * r_ref[:, cols].astype(F32))).astype(BF16)
            dpre_ref[:, cols] = dpre
            dx1 = dx1 + _dot(dpre, w1_ref[j], NT)
        xhat1 = xh_ref[...]
        dg1_ref[...] += jnp.sum(dx1 * xhat1, axis=0, keepdims=True)
        db1_ref[...] += jnp.sum(dx1, axis=0, keepdims=True)
        dz1 = _layer_norm_bwd(dx1 * g1_ref[...], xhat1, rstd_ref[...])
        dz1_ref[...] = dz1
        dz1b = dz1.astype(BF16)
        dz1b_ref[...] = dz1b
        dcat_ref[...] = _dot(dz1b, wo_ref[...], NT).astype(BF16)

    vec = _const_spec((1, D_MODEL))
    tile = _row_spec(TM_FFN, D_MODEL)
    wspec = _const_spec((N_FF_BLOCKS, D_MODEL, D_MODEL), single_buffer=True)
    body, in_specs, operands = _after(
        dep, body,
        [tile, _row_spec(TM_FFN, D_FF), tile, _row_spec(TM_FFN, 1), vec, wspec, wspec, _const_spec((D_MODEL, D_MODEL), single_buffer=True)],
        [dz2, r, xhat1, rstd1, ln1_g, w1, w2, w_out])
    return pl.pallas_call(
        body,
        name="ffn_bwd_ln1",
        grid=(t // TM_FFN,),
        in_specs=in_specs,
        out_specs=[_row_spec(TM_FFN, D_FF), tile, tile, tile, vec, vec],
        out_shape=[
            jax.ShapeDtypeStruct((t, D_FF), BF16),
            jax.ShapeDtypeStruct((t, D_MODEL), F32),
            jax.ShapeDtypeStruct((t, D_MODEL), BF16),
            jax.ShapeDtypeStruct((t, D_MODEL), BF16),
            jax.ShapeDtypeStruct((1, D_MODEL), F32),
            jax.ShapeDtypeStruct((1, D_MODEL), F32),
        ],
        compiler_params=_params(("arbitrary",)),
    )(*operands)


def _mixer_bwd(u, vg, q, k, va, dcat, cos, sin, v_ln_g, v_ln_b, w_spatial, bias_full, sinks, dep=None):
    t = u.shape[0]
    n_chunks = t // CHUNK

    def body(u_ref, vg_ref, q_ref, kc_ref, kp_ref, vc_ref, vp_ref, dcat_ref, cosc_ref, sinc_ref, cosp_ref, sinp_ref,
             g_ref, b_ref, w_ref, bias_ref, sink_ref,
             dmain_ref, dkv_ref, dg_ref, db_ref, dw_ref, dbs_ref, dsink_ref, dmix_acc, wcat, wcat_t):
        i = pl.program_id(0)
        left = _half_lane_masks(CHUNK)
        lane = lax.broadcasted_iota(jnp.int32, (CHUNK, LANES), 1)
        n_pairs = D_GMLP // LANES

        @pl.when(i == 0)
        def _():
            dg_ref[...] = jnp.zeros_like(dg_ref)
            db_ref[...] = jnp.zeros_like(db_ref)
            dw_ref[...] = jnp.zeros_like(dw_ref)
            dsink_ref[...] = jnp.zeros_like(dsink_ref)
            dmix_acc[...] = jnp.zeros_like(dmix_acc)
            _store_spatial_weights(w_ref, wcat, wcat_t)

        n_qpairs = D_ATTN // LANES
        heads = range(N_HEADS)
        pair_cols = [slice(p * LANES, (p + 1) * LANES) for p in range(n_pairs)]
        sinks_h = [sink_ref[h] for h in heads]
        gain = g_ref[...]
        causal = _causal_mask()
        lane_row = lax.broadcasted_iota(jnp.int32, (1, LANES), 1)
        heads_per_group = N_HEADS // 2

        def group_grad_t(lhs_t, rhs_heads):
            parts = []
            for g in range(2):
                group = range(g * heads_per_group, (g + 1) * heads_per_group)
                lhs = jnp.concatenate([lhs_t[h * HEAD_DIM : (h + 1) * HEAD_DIM] for h in group], axis=1)
                parts.append(_dot(lhs, jnp.concatenate([rhs_heads[h] for h in group], axis=0)))
            return jnp.concatenate(parts, axis=0)

        for c in range(CHUNKS_PER_STEP):
            chunk = CHUNKS_PER_STEP * i + c
            rows = slice(c * CHUNK, (c + 1) * CHUNK)
            before = slice((c - 1) * CHUNK, c * CHUNK)

            k_prev = kp_ref[...] if c == 0 else kc_ref[before, :]
            v_prev = vp_ref[...] if c == 0 else vc_ref[before, :]
            k_var = _kv_variants(jnp.concatenate([k_prev, kc_ref[rows, :]], axis=0))
            v_var = _kv_variants(jnp.concatenate([v_prev, vc_ref[rows, :]], axis=0))
            q_pairs = [q_ref[rows, cols] for cols in pair_cols]
            do_all = dcat_ref[rows, D_GMLP:D_MODEL]
            do_pairs = [do_all[:, cols] for cols in pair_cols]
            scores = [_dot(q_pairs[h // 2], k_var[h // 4][h % 2], NT) for h in heads]
            dprobs = [_dot(do_pairs[h // 2], v_var[h // 4][h % 2], NT) for h in heads]
            q_t = q_ref[rows, :].astype(F32).T.astype(BF16)
            do_t = do_all.astype(F32).T.astype(BF16)

            ug, dug_du = _gelu_and_grad(u_ref[rows, :])
            gv, dgv_dv = _gelu_and_grad(vg_ref[rows, :])
            xhat, rstd = _layer_norm_stats(gv)
            vgl = xhat * gain + b_ref[...]
            mixed = [_dot(wcat[p], _pair_stack(vgl[:, cols], left)) for p, cols in enumerate(pair_cols)]

            valid = _band_mask(chunk)
            masked = [jnp.where(valid, scores[h] * SCALE, NEG_INF) for h in heads]
            maxes = [jnp.maximum(jnp.max(masked[h], axis=1, keepdims=True), sinks_h[h]) for h in heads]
            exps = [jnp.exp(masked[h] - maxes[h]) for h in heads]
            exp_sinks = [jnp.exp(sinks_h[h] - maxes[h]) for h in heads]
            invs = [1.0 / (jnp.sum(exps[h], axis=1, keepdims=True) + exp_sinks[h]) for h in heads]
            probs = [exps[h] * invs[h] for h in heads]
            dsums = [jnp.sum(probs[h] * dprobs[h], axis=1, keepdims=True) for h in heads]
            ds_b = [(probs[h] * (dprobs[h] - dsums[h]) * SCALE).astype(BF16) for h in heads]
            probs_b = [probs[h].astype(BF16) for h in heads]

            dm_stacks = []
            for p, cols in enumerate(pair_cols):
                da = dcat_ref[rows, cols].astype(F32)
                dmain_ref[rows, cols] = (da * (mixed[p] + bias_ref[:, cols]) * dug_du[:, cols]).astype(BF16)
                dmixed = da * ug[:, cols]
                dmix_acc[:, cols] += dmixed
                dm_stacks.append(_pair_stack(dmixed, left))

            dq_all = jnp.concatenate(
                [_dot(ds_b[2 * p], k_var[p // 2][0]) + _dot(ds_b[2 * p + 1], k_var[p // 2][1]) for p in range(n_qpairs)], axis=1)
            dk2_t = group_grad_t(q_t, ds_b)
            dv2_t = group_grad_t(do_t, probs_b)

            for p, cols in enumerate(pair_cols):
                dw_pair = _dot(dm_stacks[p], vgl[:, cols].astype(BF16), NT)
                dw_ref[2 * p] += jnp.where(causal, dw_pair[:CHUNK], 0.0)
                dw_ref[2 * p + 1] += jnp.where(causal, dw_pair[CHUNK:], 0.0)
            dvgl = jnp.concatenate([_dot(wcat_t[p], dm_stacks[p]) for p in range(n_pairs)], axis=1)

            dsink_row = jnp.zeros((1, LANES), F32)
            for h in heads:
                d_sink = -jnp.sum(exp_sinks[h] * invs[h] * dsums[h], axis=0, keepdims=True)
                dsink_row = dsink_row + jnp.where(lane_row == h, d_sink, 0.0)
            dsink_ref[0:1, :] += dsink_row
            cos_c, sin_c = cosc_ref[rows, :], sinc_ref[rows, :]
            cos_p = cosp_ref[...] if c == 0 else cosc_ref[before, :]
            sin_p = sinp_ref[...] if c == 0 else sinc_ref[before, :]
            dmain_ref[rows, 2 * D_GMLP : D_MAIN] = _rope_transposed(dq_all, _lane_tile(cos_c, n_qpairs), _lane_tile(sin_c, n_qpairs)).astype(BF16)
            dk2 = dk2_t.T
            dv2 = dv2_t.T
            cur = pl.ds(pl.multiple_of(chunk * CHUNK, CHUNK), CHUNK)
            dkv_ref[cur, 0:D_KV] = _rope_transposed(dk2[CHUNK:], cos_c, sin_c)
            dkv_ref[cur, D_KV : 2 * D_KV] = dv2[CHUNK:]
            prev = pl.ds(pl.multiple_of(jnp.maximum(chunk - 1, 0) * CHUNK, CHUNK), CHUNK)
            dkv_ref[prev, 0:D_KV] += _rope_transposed(dk2[:CHUNK], cos_p, sin_p)
            dkv_ref[prev, D_KV : 2 * D_KV] += dv2[:CHUNK]

            dg_ref[...] += jnp.sum(dvgl * xhat, axis=0, keepdims=True)
            db_ref[...] += jnp.sum(dvgl, axis=0, keepdims=True)
            dgv = _layer_norm_bwd(dvgl * gain, xhat, rstd)
            dmain_ref[rows, D_GMLP : 2 * D_GMLP] = (dgv * dgv_dv).astype(BF16)

        @pl.when(i == n_chunks // CHUNKS_PER_STEP - 1)
        def _():
            tile = jnp.zeros((CHUNK, LANES), F32)
            for p, cols in enumerate(pair_cols):
                dm = dmix_acc[:, cols]
                sl = jnp.sum(jnp.where(left, dm, 0.0), axis=1, keepdims=True)
                sr = jnp.sum(jnp.where(left, 0.0, dm), axis=1, keepdims=True)
                tile = jnp.where(lane == 2 * p, sl, tile)
                tile = jnp.where(lane == 2 * p + 1, sr, tile)
            dbs_ref[...] = tile

    step = CHUNKS_PER_STEP * CHUNK
    in_specs = _chunk_specs() + [
        pl.BlockSpec((step, D_MODEL), _step_rows),
        pl.BlockSpec((step, LANES), _step_rows),
        pl.BlockSpec((step, LANES), _step_rows),
        pl.BlockSpec((CHUNK, LANES), _chunk_before_step),
        pl.BlockSpec((CHUNK, LANES), _chunk_before_step),
        _const_spec((1, D_GMLP)),
        _const_spec((1, D_GMLP)),
        _const_spec((N_HEADS, CHUNK, CHUNK)),
        _const_spec((CHUNK, D_GMLP)),
        pl.BlockSpec(memory_space=pltpu.SMEM),
    ]
    body, in_specs, operands = _after(
        dep, body, in_specs, [u, vg, q, k, k, va, va, dcat, cos, sin, cos, sin, v_ln_g, v_ln_b, w_spatial, bias_full, sinks])
    return pl.pallas_call(
        body,
        name="mixer_bwd",
        grid=(n_chunks // CHUNKS_PER_STEP,),
        in_specs=in_specs,
        out_specs=[
            pl.BlockSpec((step, D_MAIN), _step_rows),
            _const_spec((t, 2 * D_KV)),
            _const_spec((1, D_GMLP)),
            _const_spec((1, D_GMLP)),
            _const_spec((N_HEADS, CHUNK, CHUNK)),
            _const_spec((CHUNK, LANES)),
            _const_spec((8, LANES)),
        ],
        out_shape=[
            jax.ShapeDtypeStruct((t, D_MAIN), BF16),
            jax.ShapeDtypeStruct((t, 2 * D_KV), F32),
            jax.ShapeDtypeStruct((1, D_GMLP), F32),
            jax.ShapeDtypeStruct((1, D_GMLP), F32),
            jax.ShapeDtypeStruct((N_HEADS, CHUNK, CHUNK), F32),
            jax.ShapeDtypeStruct((CHUNK, LANES), F32),
            jax.ShapeDtypeStruct((8, LANES), F32),
        ],
        scratch_shapes=[
            pltpu.VMEM((CHUNK, D_GMLP), F32),
            pltpu.VMEM((D_GMLP // LANES, CHUNK, 2 * CHUNK), BF16),
            pltpu.VMEM((D_GMLP // LANES, CHUNK, 2 * CHUNK), BF16),
        ],
        compiler_params=_params(("arbitrary",)),
    )(*operands)


def _grad_x(dh_main, dkv, dz1, w_in_t, dep=None):
    t = dz1.shape[0]

    def body(dm_ref, dkv_ref, dz1_ref, w_ref, gx_ref):
        acc = ALPHA * dz1_ref[...] + _dot(dm_ref[...], w_ref[0:D_MAIN, :])
        gx_ref[...] = acc + _dot(dkv_ref[...].astype(BF16), w_ref[D_MAIN:D_IN, :])

    body, in_specs, operands = _after(
        dep, body, [_row_spec(TM, D_MAIN), _row_spec(TM, 2 * D_KV), _row_spec(TM, D_MODEL), _const_spec((D_IN, D_MODEL))], [dh_main, dkv, dz1, w_in_t])
    return pl.pallas_call(
        body,
        name="grad_x",
        grid=(t // TM,),
        in_specs=in_specs,
        out_specs=_row_spec(TM, D_MODEL),
        out_shape=jax.ShapeDtypeStruct((t, D_MODEL), F32),
        compiler_params=_params(("parallel",)),
    )(*operands)


def _token_contraction(name, out_rows, tk, in_arrays, contributions, dep=None):
    t = in_arrays[0].shape[0]

    def body(*refs):
        out_ref = refs[-1]

        @pl.when(pl.program_id(0) == 0)
        def _():
            out_ref[...] = jnp.zeros_like(out_ref)

        for row0, a, b in contributions(*refs[:-1]):
            out_ref[row0 : row0 + a.shape[1], :] += _dot(a, b, TN)

    in_specs = [_row_spec(tk, a.shape[1]) for a in in_arrays]
    body, in_specs, operands = _after(dep, body, in_specs, in_arrays)
    return pl.pallas_call(
        body,
        name=name,
        grid=(t // tk,),
        in_specs=in_specs,
        out_specs=_const_spec((out_rows, D_MODEL), single_buffer=True),
        out_shape=jax.ShapeDtypeStruct((out_rows, D_MODEL), F32),
        compiler_params=_params(("arbitrary",)),
    )(*operands)


def _grad_w_out(cat, dz1b, dep=None):
    def contributions(cat_ref, dz1_ref):
        return [(0, cat_ref[...], dz1_ref[...])]

    return _token_contraction("grad_w_out", D_MODEL, TK, [cat, dz1b], contributions, dep)


def _grad_w_ff1(x1b, dpre):
    def contributions(x1_ref, dpre_ref):
        x1 = x1_ref[...]
        return [(j * D_MODEL, x1, dpre_ref[:, j * D_MODEL : (j + 1) * D_MODEL]) for j in range(N_FF_BLOCKS)]

    return _token_contraction("grad_w_ff1", D_FF, TK_FF, [x1b, dpre], contributions)


def _grad_w_ff2(r, dz2b):
    def contributions(r_ref, dz2_ref):
        dz2 = dz2_ref[...]
        out = []
        for j in range(N_FF_BLOCKS):
            rf = r_ref[:, j * D_MODEL : (j + 1) * D_MODEL].astype(F32)
            out.append((j * D_MODEL, (rf * rf).astype(BF16), dz2))
        return out

    return _token_contraction("grad_w_ff2", D_FF, TK_FF, [r, dz2b], contributions)


ANY = pl.BlockSpec(memory_space=pl.ANY)


def _mesh_position():
    return lax.axis_index("x"), lax.axis_index("y"), lax.axis_index("c")


def _other_chips(x, y):
    return [(1 - x, y), (x, 1 - y), (1 - x, 1 - y)]


def _remote(src, dst, send_sem, recv_sem, device):
    return pltpu.make_async_remote_copy(src_ref=src, dst_ref=dst, send_sem=send_sem, recv_sem=recv_sem, device_id=device, device_id_type=MESH)


def _rows(ref, start, size):
    return ref.at[pl.ds(start, size), :]


def _all_gather_weights(shards):
    n = len(shards)
    per = 7

    def body(*refs):
        ins, outs = refs[:n], refs[n : 2 * n]
        send_sems, recv_sems = refs[2 * n :]
        x, y, c = _mesh_position()
        me = 2 * x + y
        chips = _other_chips(x, y)
        sibling = (x, y, 1 - c)
        started = []
        for w in range(n):
            rows = shards[w].shape[0]
            half = rows // 2
            for kk, (px, py) in enumerate(chips):
                cp = _remote(_rows(ins[w], c * half, half), _rows(outs[w], me * rows + c * half, half),
                             send_sems.at[per * w + kk], recv_sems.at[per * w + kk], (px, py, c))
                cp.start()
                started.append(cp)
            cp = _remote(ins[w], _rows(outs[w], me * rows, rows), send_sems.at[per * w + 6], recv_sems.at[per * w + 6], sibling)
            cp.start()
            started.append(cp)
        for w in range(n):
            rows = shards[w].shape[0]
            half = rows // 2
            for kk, (px, py) in enumerate(chips):
                blk = _rows(outs[w], (2 * px + py) * rows + c * half, half)
                _remote(blk, blk, send_sems.at[per * w + kk], recv_sems.at[per * w + kk], (px, py, c)).wait_recv()
                fwd = _remote(blk, blk, send_sems.at[per * w + 3 + kk], recv_sems.at[per * w + 3 + kk], sibling)
                fwd.start()
                started.append(fwd)
        for w in range(n):
            rows = shards[w].shape[0]
            half = rows // 2
            for kk, (px, py) in enumerate(chips):
                blk = _rows(outs[w], (2 * px + py) * rows + (1 - c) * half, half)
                _remote(blk, blk, send_sems.at[per * w + 3 + kk], recv_sems.at[per * w + 3 + kk], sibling).wait_recv()
            own = _rows(outs[w], me * rows, rows)
            _remote(own, own, send_sems.at[per * w + 6], recv_sems.at[per * w + 6], sibling).wait_recv()
        for cp in started:
            cp.wait_send()

    return pl.pallas_call(
        body,
        name="all_gather_weights",
        in_specs=[ANY] * n,
        out_specs=[ANY] * n,
        out_shape=[jax.ShapeDtypeStruct((N_CHIPS * s.shape[0], s.shape[1]), s.dtype) for s in shards],
        scratch_shapes=[pltpu.SemaphoreType.DMA((per * n,)), pltpu.SemaphoreType.DMA((per * n,))],
    )(*shards)


def _pair_gather(name, shards):
    n = len(shards)

    def body(*refs):
        outs = refs[n : 2 * n]
        send_sems, recv_sems = refs[2 * n :]
        x, y, c = _mesh_position()
        sibling = (x, y, 1 - c)
        sends = []
        for w in range(n):
            half = shards[w].shape[0] // 2
            mine = _rows(outs[w], c * half, half)
            cp = _remote(mine, mine, send_sems.at[w], recv_sems.at[w], sibling)
            cp.start()
            sends.append(cp)
        for w in range(n):
            half = shards[w].shape[0] // 2
            blk = _rows(outs[w], (1 - c) * half, half)
            _remote(blk, blk, send_sems.at[w], recv_sems.at[w], sibling).wait_recv()
        for cp in sends:
            cp.wait_send()

    return pl.pallas_call(
        body,
        name=name,
        in_specs=[ANY] * n,
        out_specs=[ANY] * n,
        out_shape=[jax.ShapeDtypeStruct(s.shape, s.dtype) for s in shards],
        input_output_aliases={w: w for w in range(n)},
        scratch_shapes=[pltpu.SemaphoreType.DMA((n,)), pltpu.SemaphoreType.DMA((n,))],
    )(*shards)


def _grad_w_in_t_and_small_all_reduce(dh_main, dkv, x, slab):
    t = x.shape[0]
    steps = t // TK
    rows = slab.shape[0]
    part = rows // 8

    def body(dm_ref, dkv_ref, x_ref, slab_ref, grad_ref, sum_ref, landing, reduced, gathered, send_sems, recv_sems):
        k = pl.program_id(0)
        x_, y_, c_ = _mesh_position()
        me = 4 * x_ + 2 * y_ + c_
        flips = [(f >> 2, (f >> 1) & 1, f & 1) for f in range(1, 8)]

        def peer(flip):
            fx, fy, fc = flip
            return (1 - x_ if fx else x_, 1 - y_ if fy else y_, 1 - c_ if fc else c_)

        def part_of(ref, device):
            return ref.at[pl.ds(pl.multiple_of(device * part, 8), part), :]

        def scatter_copies():
            out = []
            for kk, flip in enumerate(flips):
                px, py, pc = peer(flip)
                them = 4 * px + 2 * py + pc
                send = _remote(part_of(slab_ref, them), landing.at[me], send_sems.at[kk], recv_sems.at[kk], (px, py, pc))
                recv = _remote(landing.at[them], landing.at[them], send_sems.at[kk], recv_sems.at[kk], (px, py, pc))
                out.append((send, recv))
            return out

        def gather_copies():
            out = []
            for kk, flip in enumerate(flips):
                px, py, pc = peer(flip)
                them = 4 * px + 2 * py + pc
                send = _remote(reduced, part_of(gathered, me), send_sems.at[7 + kk], recv_sems.at[7 + kk], (px, py, pc))
                recv = _remote(part_of(gathered, them), part_of(gathered, them), send_sems.at[7 + kk], recv_sems.at[7 + kk], (px, py, pc))
                out.append((send, recv))
            return out

        @pl.when(k == 0)
        def _():
            grad_ref[...] = jnp.zeros_like(grad_ref)
            for send, _ in scatter_copies():
                send.start()
            landing[me] = part_of(slab_ref, me)[...]

        @pl.when(k == min(1, steps - 1))
        def _():
            for _, recv in scatter_copies():
                recv.wait_recv()
            total = landing[0]
            for s in range(1, 8):
                total = total + landing[s]
            reduced[...] = total
            part_of(gathered, me)[...] = total
            for send, _ in gather_copies():
                send.start()

        xb = x_ref[...].astype(BF16)
        grad_ref[0:D_MAIN, :] += _dot(dm_ref[...], xb, TN)
        grad_ref[D_MAIN:D_IN, :] += _dot(dkv_ref[...].astype(BF16), xb, TN)

        @pl.when(k == steps - 1)
        def _():
            for send, recv in gather_copies():
                recv.wait_recv()
                send.wait_send()
            for send, _ in scatter_copies():
                send.wait_send()
            sum_ref[...] = gathered[...]

    return pl.pallas_call(
        body,
        name="grad_w_in_and_small_all_reduce",
        grid=(steps,),
        in_specs=[_row_spec(TK, D_MAIN), _row_spec(TK, 2 * D_KV), _row_spec(TK, D_MODEL), _const_spec(slab.shape)],
        out_specs=[_const_spec((D_IN, D_MODEL), single_buffer=True), _const_spec(slab.shape)],
        out_shape=[jax.ShapeDtypeStruct((D_IN, D_MODEL), F32), jax.ShapeDtypeStruct(slab.shape, slab.dtype)],
        scratch_shapes=[
            pltpu.VMEM((8, part, LANES), F32),
            pltpu.VMEM((part, LANES), F32),
            pltpu.VMEM(slab.shape, F32),
            pltpu.SemaphoreType.DMA((14,)),
            pltpu.SemaphoreType.DMA((14,)),
        ],
        compiler_params=_params(("arbitrary",)),
    )(dh_main, dkv, x, slab)


HBM = pl.BlockSpec(memory_space=pltpu.HBM)
SEM = pl.BlockSpec(memory_space=pltpu.SEMAPHORE)
DATAFLOW = pltpu.SideEffectType.DATAFLOW_SIDE_EFFECTING
TOKEN = jax.ShapeDtypeStruct((8, LANES), F32)


def _plan_copies(bufs, plan, send_sems, recv_sems):
    out = []
    for i, (src, src_row, dst, dst_row, recv_row, rows, device) in enumerate(plan):
        send = _remote(_rows(bufs[src], src_row, rows), _rows(bufs[dst], dst_row, rows), send_sems.at[i], recv_sems.at[i], device)
        landed = _rows(bufs[dst], recv_row, rows)
        recv = _remote(landed, landed, send_sems.at[i], recv_sems.at[i], device)
        out.append((send, recv))
    return out


def _split_call(name, bufs, wait=None, start=None, after=None):
    n = len(bufs)
    n_in = n + (2 if wait else 0) + (1 if after is not None else 0)
    n_start = len(start(0, 0, 0)) if start else 0

    def body(*refs):
        ins = refs[:n]
        x, y, c = _mesh_position()
        if wait:
            for send, recv in _plan_copies(ins, wait[0](x, y, c), refs[n], refs[n + 1]):
                recv.wait_recv()
                send.wait_send()
        if start:
            for send, _ in _plan_copies(ins, start(x, y, c), refs[n_in + n + 1], refs[n_in + n + 2]):
                send.start()
        token = refs[n_in + n]
        token[...] = jnp.zeros_like(token)

    operands = [pltpu.with_memory_space_constraint(b, pltpu.HBM) for b in bufs]
    in_specs = [HBM] * n
    if wait:
        operands += [wait[1], wait[2]]
        in_specs += [SEM, SEM]
    if after is not None:
        operands.append(after)
        in_specs.append(ANY)
    out_shape = [pltpu.HBM(b.shape, b.dtype) for b in bufs] + [TOKEN]
    out_specs = [HBM] * n + [pl.BlockSpec(memory_space=pltpu.VMEM)]
    if start:
        out_shape += [pltpu.SemaphoreType.DMA((n_start,)), pltpu.SemaphoreType.DMA((n_start,))]
        out_specs += [SEM, SEM]
    outs = pl.pallas_call(
        body,
        name=name,
        in_specs=in_specs,
        out_specs=out_specs,
        out_shape=out_shape,
        input_output_aliases={i: i for i in range(n)},
        compiler_params=pltpu.CompilerParams(has_side_effects=DATAFLOW),
    )(*operands)
    return (list(outs[:n]), outs[n]) + tuple(outs[n + 1 :])


def _gather_plans(shard_rows):
    n = len(shard_rows)

    def neighbours(x, y):
        return ((1 - x, y), (x, 1 - y))

    def direct(x, y, c):
        me = 2 * x + y
        plan = []
        for w, rows in enumerate(shard_rows):
            half = rows // 2
            for px, py in neighbours(x, y):
                plan.append((w, c * half, n + w, me * rows + c * half, (2 * px + py) * rows + c * half, half, (px, py, c)))
            plan.append((w, 0, n + w, me * rows, me * rows, rows, (x, y, 1 - c)))
        return plan

    def passed_on(x, y, c):
        (xn, yn), diagonal = neighbours(x, y), 2 * (1 - x) + (1 - y)
        relayed = (1 - c) * (2 * xn[0] + xn[1]) + c * (2 * yn[0] + yn[1])
        target = (x * (1 - c) + (1 - x) * c, (1 - y) * (1 - c) + y * c, c)
        plan = []
        for w, rows in enumerate(shard_rows):
            half = rows // 2
            for px, py in (xn, yn):
                row = (2 * px + py) * rows
                plan.append((n + w, row + c * half, n + w, row + c * half, row + (1 - c) * half, half, (x, y, 1 - c)))
            plan.append((n + w, relayed * rows + c * half, n + w, relayed * rows + c * half, diagonal * rows + c * half, half, target))
        return plan

    def diagonal_passed_on(x, y, c):
        plan = []
        for w, rows in enumerate(shard_rows):
            half = rows // 2
            row = (2 * (1 - x) + (1 - y)) * rows
            plan.append((n + w, row + c * half, n + w, row + c * half, row + (1 - c) * half, half, (x, y, 1 - c)))
        return plan

    return direct, passed_on, diagonal_passed_on


def _swap_plan(block_rows):
    n = len(block_rows)

    def plan_fn(x, y, c):
        plan = []
        for w, rows in enumerate(block_rows):
            half = rows // 2
            for j in range(N_CHIPS):
                plan.append((w, j * rows + (1 - c) * half, n + w, j * half, j * half, half, (x, y, 1 - c)))
        return plan

    return plan_fn


def _exchange_plan(halves):
    n = len(halves)

    def plan_fn(x, y, c):
        plan = []
        for w, half in enumerate(halves):
            for kk, (px, py) in enumerate(_other_chips(x, y)):
                plan.append((w, (2 * px + py) * half, n + w, kk * half, kk * half, half, (px, py, c)))
        return plan

    return plan_fn


def _landing(rows, cols, dtype):
    return lax.empty((rows, cols), dtype)


def _row_tile(rows, cap=512):
    best = 8
    for cand in range(8, cap + 1, 8):
        if rows % cand == 0:
            best = cand
    return best


def _pair_sum(name, grad, theirs, pos):
    half = theirs.shape[0] // N_CHIPS
    cols = theirs.shape[1]
    tile = _row_tile(half)
    steps = half // tile

    def body(pos_ref, g_ref, t_ref, p_ref, own_ref):
        total = g_ref[...] + t_ref[...]
        p_ref[...] = total.astype(BF16)

        @pl.when(pl.program_id(1) == pos_ref[1])
        def _():
            own_ref[...] = total

    return pl.pallas_call(
        body,
        name=name,
        grid_spec=pltpu.PrefetchScalarGridSpec(
            num_scalar_prefetch=1,
            grid=(steps, N_CHIPS),
            in_specs=[
                pl.BlockSpec((tile, cols), lambda i, j, pos: ((2 * j + pos[0]) * steps + i, 0)),
                pl.BlockSpec((tile, cols), lambda i, j, pos: (j * steps + i, 0)),
            ],
            out_specs=[
                pl.BlockSpec((tile, cols), lambda i, j, pos: (j * steps + i, 0)),
                pl.BlockSpec((tile, cols), lambda i, j, pos: (i, 0)),
            ],
        ),
        out_shape=[jax.ShapeDtypeStruct((N_CHIPS * half, cols), BF16), jax.ShapeDtypeStruct((half, cols), F32)],
        compiler_params=_params(("parallel", "arbitrary")),
    )(pos, grad, theirs)


def _chip_sum(name, own, landed, pos):
    half, cols = own.shape
    tile = _row_tile(half)
    steps = half // tile

    def body(pos_ref, own_ref, l0, l1, l2, o_ref):
        o_ref[...] = ((own_ref[...] + l0[...].astype(F32)) + l1[...].astype(F32)) + l2[...].astype(F32)

    landed_specs = [pl.BlockSpec((tile, cols), lambda i, pos, _k=k: (_k * steps + i, 0)) for k in range(N_CHIPS - 1)]
    return pl.pallas_call(
        body,
        name=name,
        grid_spec=pltpu.PrefetchScalarGridSpec(
            num_scalar_prefetch=1,
            grid=(steps,),
            in_specs=[pl.BlockSpec((tile, cols), lambda i, pos: (i, 0))] + landed_specs,
            out_specs=pl.BlockSpec((tile, cols), lambda i, pos: (pos[0] * steps + i, 0)),
        ),
        out_shape=jax.ShapeDtypeStruct((2 * half, cols), F32),
        compiler_params=_params(("parallel",)),
    )(pos, own, landed, landed, landed)


def _adamw(name, w, g, m, v):
    rows, cols = w.shape
    tile = rows if rows * cols <= 256 * 1024 else _row_tile(rows)

    def body(w_ref, g_ref, m_ref, v_ref, g_out_ref, d_ref, nm_ref, nv_ref):
        g = g_ref[...]
        g_out_ref[...] = g
        nm = ADAM_B1 * m_ref[...] + (1.0 - ADAM_B1) * g
        nv = ADAM_B2 * v_ref[...] + (1.0 - ADAM_B2) * (g * g)
        m_hat = nm / (1.0 - ADAM_B1**ADAM_STEP)
        v_hat = nv / (1.0 - ADAM_B2**ADAM_STEP)
        d_ref[...] = -ADAM_LR * (m_hat / (jnp.sqrt(v_hat) + ADAM_EPS) + ADAM_WD * w_ref[...])
        nm_ref[...] = nm
        nv_ref[...] = nv

    spec = _row_spec(tile, cols)
    return pl.pallas_call(
        body,
        name=name,
        grid=(rows // tile,),
        in_specs=[spec] * 4,
        out_specs=[spec] * 4,
        out_shape=[jax.ShapeDtypeStruct((rows, cols), F32)] * 4,
        compiler_params=_params(("parallel",)),
    )(w, g, m, v)


_SMALL = (
    ("v_ln_g", (D_GMLP,), 8),
    ("v_ln_b", (D_GMLP,), 8),
    ("w_spatial", (N_HEADS, CHUNK, CHUNK), 1024),
    ("b_spatial", (N_HEADS, CHUNK), 8),
    ("sinks", (N_HEADS,), 8),
    ("ln1_g", (D_MODEL,), 8),
    ("ln1_b", (D_MODEL,), 8),
    ("ln2_g", (D_MODEL,), 8),
    ("ln2_b", (D_MODEL,), 8),
    ("squared_error", (D_MODEL,), 8),
)
N_SMALL_PARAMS = len(_SMALL) - 1


def _pack_small(values):
    parts = []
    for (name, shape, rows), val in zip(_SMALL, values, strict=True):
        flat = val.reshape(-1).astype(F32)
        parts.append(jnp.pad(flat, (0, rows * LANES - flat.shape[0])).reshape(rows, LANES))
    parts.append(jnp.zeros((SMALL_ROWS - sum(rows for _, _, rows in _SMALL), LANES), F32))
    return jnp.concatenate(parts, axis=0)


def _adamw_update(w, g, m, v):
    nm = ADAM_B1 * m + (1.0 - ADAM_B1) * g
    nv = ADAM_B2 * v + (1.0 - ADAM_B2) * (g * g)
    m_hat = nm / (1.0 - ADAM_B1**ADAM_STEP)
    v_hat = nv / (1.0 - ADAM_B2**ADAM_STEP)
    return -ADAM_LR * (m_hat / (jnp.sqrt(v_hat) + ADAM_EPS) + ADAM_WD * w), nm, nv


def _adamw_small(g_slab, params, first, second):
    n = N_SMALL_PARAMS

    def pieces(shape):
        if len(shape) == 3:
            return [((0, h), h * shape[1], shape[1], shape[2]) for h in range(shape[0])]
        if len(shape) == 2:
            return [((0,), 0, shape[0], shape[1])]
        if shape[0] >= LANES:
            return [((slice(None), slice(r * LANES, (r + 1) * LANES)), r, 1, LANES) for r in range(shape[0] // LANES)]
        return [((slice(None), slice(0, shape[0])), 0, 1, shape[0])]

    def body(*refs):
        g_ref = refs[0]
        w_refs, m_refs, v_refs = refs[1 : 1 + n], refs[1 + n : 1 + 2 * n], refs[1 + 2 * n : 1 + 3 * n]
        outs = refs[1 + 3 * n :]
        row0 = 0
        for idx, (_, shape, rows) in enumerate(_SMALL[:n]):
            for where, first_row, n_rows, lanes in pieces(shape):
                g = g_ref[row0 + first_row : row0 + first_row + n_rows, 0:lanes]
                delta, nm, nv = _adamw_update(w_refs[idx][where], g, m_refs[idx][where], v_refs[idx][where])
                for group, val in enumerate((g, delta, nm, nv)):
                    outs[group * n + idx][where] = val
            row0 += rows

    vmem = pl.BlockSpec(memory_space=pltpu.VMEM)
    shapes = [jax.ShapeDtypeStruct(p.shape, F32) for p in params]
    outs = pl.pallas_call(
        body,
        name="adamw_small",
        in_specs=[vmem] * (1 + 3 * n),
        out_specs=[vmem] * (4 * n),
        out_shape=shapes * 4,
        compiler_params=_params(),
    )(g_slab, *params, *first, *second)
    return [list(outs[group * n : (group + 1) * n]) for group in range(4)]


def kernel(x, positions, w_in, v_ln_g, v_ln_b, w_spatial, b_spatial, sinks, w_out, ln1_g, ln1_b, w_ff1, w_ff2, ln2_g, ln2_b, loss_target, m_w_in, m_v_ln_g, m_v_ln_b, m_w_spatial, m_b_spatial, m_sinks, m_w_out, m_ln1_g, m_ln1_b, m_w_ff1, m_w_ff2, m_ln2_g, m_ln2_b, v_w_in, v_v_ln_g, v_v_ln_b, v_w_spatial, v_b_spatial, v_sinks, v_w_out, v_ln1_g, v_ln1_b, v_w_ff1, v_w_ff2, v_ln2_g, v_ln2_b):
    t = x.shape[1]
    x2 = x.reshape(t, D_MODEL)
    target = loss_target.reshape(t, D_MODEL)

    (w_in_t,) = _all_gather_weights([w_in[0].T.astype(BF16)])
    later = [w_out[0].astype(BF16), w_ff1[0].astype(BF16), w_ff2[0].astype(BF16)]
    later_rows = [s.shape[0] for s in later]
    direct_plan, pass_plan, diagonal_plan = _gather_plans(later_rows)
    bufs, started, direct_send, direct_recv = _split_call(
        "gather_start", later + [_landing(N_CHIPS * r, D_MODEL, BF16) for r in later_rows], start=direct_plan, after=w_in_t)

    inv_freq = ROPE_THETA ** (-jnp.arange(0, HEAD_DIM, 2, dtype=F32) / HEAD_DIM)
    cos, sin = _rope_tables(positions, jnp.tile(inv_freq, LANES // (HEAD_DIM // 2)).reshape(1, LANES))
    u, vg, q, k, va = _in_proj(x2, w_in_t, cos, sin, dep=started)
    bias_full = jnp.repeat(b_spatial[0].T, HEAD_DIM, axis=1)
    sink_vec = sinks.reshape(N_HEADS)
    bufs, passing, pass_send, pass_recv = _split_call(
        "gather_pass", bufs, wait=(direct_plan, direct_send, direct_recv), start=pass_plan, after=u)
    cat = _mixer_fwd(u, vg, q, k, va, v_ln_g, v_ln_b, w_spatial[0], bias_full, sink_vec, dep=passing)
    bufs, passing, diag_send, diag_recv = _split_call(
        "gather_pass_diagonal", bufs, wait=(pass_plan, pass_send, pass_recv), start=diagonal_plan, after=cat)
    bufs, _ = _split_call("gather_end", bufs, wait=(diagonal_plan, diag_send, diag_recv), after=passing)
    w_out_all = bufs[3]
    w1_all = bufs[4].reshape(N_FF_BLOCKS, D_MODEL, D_MODEL)
    w2_all = bufs[5].reshape(N_FF_BLOCKS, D_MODEL, D_MODEL)
    xhat1, rstd1, x1b, r, dz2, dz2b, d_ln2_g, d_ln2_b, sq_err = _ffn_fwd_loss(
        cat, x2, w_out_all, ln1_g, ln1_b, w1_all, w2_all, ln2_g, ln2_b, target)

    pos = jnp.stack([lax.axis_index("c"), 2 * lax.axis_index("x") + lax.axis_index("y")]).astype(jnp.int32)
    half_landing = lambda g: _landing(g.shape[0] // 2, D_MODEL, F32)
    g_ff2_local = _grad_w_ff2(r, dz2b)
    swap_plan = _swap_plan([D_FF // N_CHIPS])
    ff2_bufs, swapping2, swap2_send, swap2_recv = _split_call("ff2_swap_start", [g_ff2_local, half_landing(g_ff2_local)], start=swap_plan)
    dpre, dz1, dz1b, dcat, d_ln1_g, d_ln1_b = _ffn_bwd_ln1(dz2, r, xhat1, rstd1, ln1_g, w1_all, w2_all, w_out_all, dep=swapping2)
    g_ff1_local = _grad_w_ff1(x1b, dpre)
    ff1_bufs, swapping1, swap1_send, swap1_recv = _split_call("ff1_swap_start", [g_ff1_local, half_landing(g_ff1_local)], start=swap_plan)
    g_out_local = _grad_w_out(cat, dz1b, dep=swapping1)
    ff2_bufs, swapped2 = _split_call("ff2_swap_wait", ff2_bufs, wait=(swap_plan, swap2_send, swap2_recv), after=g_out_local)
    ff1_bufs, _ = _split_call("ff1_swap_wait", ff1_bufs, wait=(swap_plan, swap1_send, swap1_recv), after=swapped2)
    ff_sums = [_pair_sum("grad_pair_sum_w_ff1", ff1_bufs[0], ff1_bufs[1], pos), _pair_sum("grad_pair_sum_w_ff2", ff2_bufs[0], ff2_bufs[1], pos)]
    ff_halves = [p.shape[0] // N_CHIPS for p, _ in ff_sums]
    exchange_plan = _exchange_plan(ff_halves)
    bufs, exchanging, ex_send, ex_recv = _split_call(
        "ff_exchange_start", [p for p, _ in ff_sums] + [_landing(3 * h, D_MODEL, BF16) for h in ff_halves], start=exchange_plan)
    dh_main, dkv, d_v_ln_g, d_v_ln_b, d_w_spatial, d_b_spatial_t, d_sinks = _mixer_bwd(
        u, vg, q, k, va, dcat, cos, sin, v_ln_g, v_ln_b, w_spatial[0], bias_full, sink_vec, dep=exchanging)
    g_in_local, small_g = _grad_w_in_t_and_small_all_reduce(dh_main, dkv, x2, _pack_small(
        [d_v_ln_g, d_v_ln_b, d_w_spatial, d_b_spatial_t[:, :N_HEADS].T, d_sinks[0, :N_HEADS], d_ln1_g, d_ln1_b, d_ln2_g, d_ln2_b, sq_err]))
    sq_row = sum(rows for _, _, rows in _SMALL[:N_SMALL_PARAMS])
    loss = 0.5 * jnp.sum(small_g[sq_row : sq_row + _SMALL[N_SMALL_PARAMS][2]]) / D_MODEL

    small = [g_in_local, g_out_local]
    small_swap_plan = _swap_plan([g.shape[0] // N_CHIPS for g in small])
    swap_bufs, small_swapping, ss_send, ss_recv = _split_call(
        "small_swap_start", small + [half_landing(g) for g in small], start=small_swap_plan)
    grad_x_flat = _grad_x(dh_main, dkv, dz1, w_in_t, dep=small_swapping)
    grad_x = grad_x_flat.reshape(1, t, D_MODEL)
    swap_bufs, _ = _split_call("small_swap_wait", swap_bufs, wait=(small_swap_plan, ss_send, ss_recv), after=grad_x_flat)
    pair_sums = [_pair_sum("grad_pair_sum_" + nm, g, th, pos) for nm, g, th in zip(["w_in", "w_out"], swap_bufs[:2], swap_bufs[2:])]
    small_halves = [p.shape[0] // N_CHIPS for p, _ in pair_sums]
    small_plan = _exchange_plan(small_halves)
    small_bufs, small_exchanging, sm_send, sm_recv = _split_call(
        "small_exchange_start", [p for p, _ in pair_sums] + [_landing(3 * h, D_MODEL, BF16) for h in small_halves], start=small_plan)

    bufs, _ = _split_call("ff_exchange_wait", bufs, wait=(exchange_plan, ex_send, ex_recv), after=small_exchanging)
    ff_shards = [_chip_sum("grad_chip_sum_" + nm, own, ld, pos) for nm, (_, own), ld in zip(["w_ff1", "w_ff2"], ff_sums, bufs[2:])]
    g_w_ff1, g_w_ff2 = _pair_gather("grad_pair_gather_ff", ff_shards)

    g_w_ff1, d_w_ff1, nm_w_ff1, nv_w_ff1 = _adamw("adamw_w_ff1", w_ff1[0], g_w_ff1, m_w_ff1[0], v_w_ff1[0])
    g_w_ff2, d_w_ff2, nm_w_ff2, nv_w_ff2 = _adamw("adamw_w_ff2", w_ff2[0], g_w_ff2, m_w_ff2[0], v_w_ff2[0])
    small_bufs, _ = _split_call("small_exchange_wait", small_bufs, wait=(small_plan, sm_send, sm_recv), after=nv_w_ff2)
    shards = [_chip_sum("grad_chip_sum_" + nm, own, ld, pos) for nm, (_, own), ld in zip(["w_in", "w_out"], pair_sums, small_bufs[2:])]
    g_w_in_t, g_w_out = _pair_gather("grad_pair_gather_small", shards)
    g_w_in, d_w_in, nm_w_in, nv_w_in = (a.T for a in _adamw("adamw_w_in", w_in[0].T, g_w_in_t, m_w_in[0].T, v_w_in[0].T))
    g_w_out, d_w_out, nm_w_out, nv_w_out = _adamw("adamw_w_out", w_out[0], g_w_out, m_w_out[0], v_w_out[0])
    small_grads, small_d, small_nm, small_nv = _adamw_small(
        small_g,
        [v_ln_g, v_ln_b, w_spatial, b_spatial, sinks, ln1_g, ln1_b, ln2_g, ln2_b],
        [m_v_ln_g, m_v_ln_b, m_w_spatial, m_b_spatial, m_sinks, m_ln1_g, m_ln1_b, m_ln2_g, m_ln2_b],
        [v_v_ln_g, v_v_ln_b, v_w_spatial, v_b_spatial, v_sinks, v_ln1_g, v_ln1_b, v_ln2_g, v_ln2_b])

    def with_big(small, w_in_v, w_out_v, w_ff1_v, w_ff2_v):
        g_vg, g_vb, g_ws, g_bs, g_sk, g_1g, g_1b, g_2g, g_2b = small
        return [w_in_v[None], g_vg, g_vb, g_ws, g_bs, g_sk, w_out_v[None], g_1g, g_1b, w_ff1_v[None], w_ff2_v[None], g_2g, g_2b]

    return (
        loss,
        grad_x,
        *with_big(small_grads, g_w_in, g_w_out, g_w_ff1, g_w_ff2),
        *with_big(small_d, d_w_in, d_w_out, d_w_ff1, d_w_ff2),
        *with_big(small_nm, nm_w_in, nm_w_out, nm_w_ff1, nm_w_ff2),
        *with_big(small_nv, nv_w_in, nv_w_out, nv_w_ff1, nv_w_ff2),
    )
```

```python
import math

import jax
import jax.numpy as jnp
from jax import lax
from jax.experimental import pallas as pl
from jax.experimental.pallas import tpu as pltpu

F32 = jnp.float32
BF16 = jnp.bfloat16

D_MODEL = 1024
HEAD_DIM = 64
D_GMLP = 512
D_ATTN = 512
D_KV = 128
D_IN = 2 * D_GMLP + D_ATTN + 2 * D_KV
D_MAIN = 2 * D_GMLP + D_ATTN
N_HEADS = 8
CHUNK = 128
CHUNKS_PER_STEP = 4
ROPE_THETA = 10000.0
D_FF = 4 * D_MODEL
N_FF_BLOCKS = 4
LN_EPS = 1e-5
ALPHA = (2.0 * 1) ** 0.25
NEG_INF = -1e30
SCALE = 1.0 / math.sqrt(HEAD_DIM)

ADAM_LR = 0.001
ADAM_B1 = 0.9
ADAM_B2 = 0.999
ADAM_EPS = 1e-08
ADAM_WD = 0.01
ADAM_STEP = 10

N_CHIPS = 4
LANES = 128
V7X_VMEM_BYTES = 64 * 1024 * 1024
VMEM_LIMIT = V7X_VMEM_BYTES - 8 * 1024 * 1024
TM = 512
TM_FFN = 256
TK = 1024
TK_FF = 1024
SMALL_ROWS = 1152
MESH = pl.DeviceIdType.MESH

NT = (((1,), (1,)), ((), ()))
TN = (((0,), (0,)), ((), ()))


def _dot(a, b, dims=None):
    if dims is None:
        return jnp.dot(a, b, preferred_element_type=F32)
    return lax.dot_general(a, b, dims, preferred_element_type=F32)


def _params(semantics=None):
    return pltpu.CompilerParams(dimension_semantics=semantics, vmem_limit_bytes=VMEM_LIMIT)


def _const_spec(shape, single_buffer=False):
    zeros = (0,) * len(shape)
    if single_buffer:
        return pl.BlockSpec(shape, lambda *_: zeros, pipeline_mode=pl.Buffered(1))
    return pl.BlockSpec(shape, lambda *_: zeros)


def _row_spec(rows, cols):
    return pl.BlockSpec((rows, cols), lambda i: (i, 0))


def _after(dep, body, in_specs, operands):
    if dep is None:
        return body, list(in_specs), list(operands)
    return (lambda dep_ref, *refs: body(*refs)), [pl.BlockSpec(memory_space=pl.ANY)] + list(in_specs), [dep] + list(operands)


def _gelu(x):
    k = math.sqrt(2.0 / math.pi)
    return 0.5 * x * (1.0 + jnp.tanh(k * (x + 0.044715 * (x * x * x))))


def _gelu_and_grad(x):
    k = math.sqrt(2.0 / math.pi)
    x2 = x * x
    t = jnp.tanh(k * (x + 0.044715 * (x2 * x)))
    g = 0.5 * x * (1.0 + t)
    dg = 0.5 * (1.0 + t) + 0.5 * x * (1.0 - t * t) * (k * (1.0 + 3.0 * 0.044715 * x2))
    return g, dg


def _layer_norm_stats(z):
    mu = jnp.mean(z, axis=-1, keepdims=True)
    zc = z - mu
    var = jnp.mean(zc * zc, axis=-1, keepdims=True)
    rstd = lax.rsqrt(var + LN_EPS)
    return zc * rstd, rstd


def _layer_norm_bwd(dxhat, xhat, rstd):
    m1 = jnp.mean(dxhat, axis=-1, keepdims=True)
    m2 = jnp.mean(dxhat * xhat, axis=-1, keepdims=True)
    return rstd * (dxhat - m1 - xhat * m2)


def _rotate_half(t):
    n = t.shape[1]
    lane = lax.broadcasted_iota(jnp.int32, t.shape, 1)
    first = (lane & (HEAD_DIM // 2)) == 0
    return jnp.where(first, -pltpu.roll(t, n - HEAD_DIM // 2, 1), pltpu.roll(t, HEAD_DIM // 2, 1))


def _rope(t, cos, sin):
    return t * cos + _rotate_half(t) * sin


def _rope_transposed(g, cos, sin):
    return g * cos - _rotate_half(g * sin)


def _lane_tile(a, reps):
    return jnp.tile(a, (1, reps)) if reps > 1 else a


def _in_proj(x, w_in_t, cos, sin, dep=None):
    t = x.shape[0]

    def body(x_ref, w_ref, cos_ref, sin_ref, u_ref, vg_ref, q_ref, k_ref, va_ref):
        xb = x_ref[...].astype(BF16)
        u_ref[...] = _dot(xb, w_ref[0:D_GMLP, :], NT)
        vg_ref[...] = _dot(xb, w_ref[D_GMLP : 2 * D_GMLP, :], NT)
        q = _dot(xb, w_ref[2 * D_GMLP : D_MAIN, :], NT)
        k = _dot(xb, w_ref[D_MAIN : D_MAIN + D_KV, :], NT)
        va_ref[...] = _dot(xb, w_ref[D_MAIN + D_KV : D_IN, :], NT).astype(BF16)
        c, s = cos_ref[...], sin_ref[...]
        q_ref[...] = _rope(q, _lane_tile(c, D_ATTN // LANES), _lane_tile(s, D_ATTN // LANES)).astype(BF16)
        k_ref[...] = _rope(k, c, s).astype(BF16)

    body, in_specs, operands = _after(
        dep, body, [_row_spec(TM, D_MODEL), _const_spec((D_IN, D_MODEL)), _row_spec(TM, LANES), _row_spec(TM, LANES)], [x, w_in_t, cos, sin])
    return pl.pallas_call(
        body,
        name="in_proj",
        grid=(t // TM,),
        in_specs=in_specs,
        out_specs=[_row_spec(TM, D_GMLP), _row_spec(TM, D_GMLP), _row_spec(TM, D_ATTN), _row_spec(TM, D_KV), _row_spec(TM, D_KV)],
        out_shape=[
            jax.ShapeDtypeStruct((t, D_GMLP), F32),
            jax.ShapeDtypeStruct((t, D_GMLP), F32),
            jax.ShapeDtypeStruct((t, D_ATTN), BF16),
            jax.ShapeDtypeStruct((t, D_KV), BF16),
            jax.ShapeDtypeStruct((t, D_KV), BF16),
        ],
        compiler_params=_params(("parallel",)),
    )(*operands)


def _step_rows(i):
    return (i, 0)


def _chunk_before_step(i):
    return (jnp.maximum(CHUNKS_PER_STEP * i - 1, 0), 0)


def _chunk_specs():
    step = CHUNKS_PER_STEP * CHUNK
    return [
        pl.BlockSpec((step, D_GMLP), _step_rows),
        pl.BlockSpec((step, D_GMLP), _step_rows),
        pl.BlockSpec((step, D_ATTN), _step_rows),
        pl.BlockSpec((step, D_KV), _step_rows),
        pl.BlockSpec((CHUNK, D_KV), _chunk_before_step),
        pl.BlockSpec((step, D_KV), _step_rows),
        pl.BlockSpec((CHUNK, D_KV), _chunk_before_step),
    ]


def _half_lane_masks(rows):
    lane = lax.broadcasted_iota(jnp.int32, (rows, LANES), 1)
    return lane < HEAD_DIM


def _kv_variants(kv2):
    left = _half_lane_masks(kv2.shape[0])
    f = kv2.astype(F32)
    swapped = pltpu.roll(f, HEAD_DIM, 1)
    zero = jnp.zeros_like(f)
    g0 = (jnp.where(left, f, zero).astype(BF16), jnp.where(left, zero, swapped).astype(BF16))
    g1 = (jnp.where(left, swapped, zero).astype(BF16), jnp.where(left, zero, f).astype(BF16))
    return (g0, g1)


def _band_mask(i, heads=1):
    row = lax.broadcasted_iota(jnp.int32, (heads * CHUNK, 2 * CHUNK), 0) & (CHUNK - 1)
    col = lax.broadcasted_iota(jnp.int32, (heads * CHUNK, 2 * CHUNK), 1)
    no_prev = jnp.where(i > 0, 0, 4 * CHUNK)
    in_prev = jnp.logical_and(col < CHUNK, (col - row) > no_prev)
    in_cur = jnp.logical_and(col >= CHUNK, (col - CHUNK) <= row)
    return jnp.logical_or(in_prev, in_cur)


def _causal_mask():
    row = lax.broadcasted_iota(jnp.int32, (CHUNK, CHUNK), 0)
    col = lax.broadcasted_iota(jnp.int32, (CHUNK, CHUNK), 1)
    return col <= row


def _store_spatial_weights(w_ref, wcat_ref, wcat_t_ref=None):
    causal = _causal_mask()
    for p in range(D_GMLP // LANES):
        wl = jnp.where(causal, w_ref[2 * p], 0.0)
        wr = jnp.where(causal, w_ref[2 * p + 1], 0.0)
        wcat_ref[p] = jnp.concatenate([wl, wr], axis=1).astype(BF16)
        if wcat_t_ref is not None:
            wcat_t_ref[p] = jnp.concatenate([wl.T, wr.T], axis=1).astype(BF16)


def _pair_stack(xp, left):
    return jnp.concatenate([jnp.where(left, xp, 0.0), jnp.where(left, 0.0, xp)], axis=0).astype(BF16)


def _mixer_fwd(u, vg, q, k, va, v_ln_g, v_ln_b, w_spatial, bias_full, sinks, dep=None):
    t = u.shape[0]

    def body(u_ref, vg_ref, q_ref, kc_ref, kp_ref, vc_ref, vp_ref, g_ref, b_ref, w_ref, bias_ref, sink_ref, cat_ref, wcat):
        i = pl.program_id(0)
        left = _half_lane_masks(CHUNK)

        @pl.when(i == 0)
        def _():
            _store_spatial_weights(w_ref, wcat)

        heads = range(N_HEADS)
        pair_cols = [slice(p * LANES, (p + 1) * LANES) for p in range(D_GMLP // LANES)]
        sinks_h = [sink_ref[h] for h in heads]
        for c in range(CHUNKS_PER_STEP):
            rows = slice(c * CHUNK, (c + 1) * CHUNK)
            before = slice((c - 1) * CHUNK, c * CHUNK)
            k_prev = kp_ref[...] if c == 0 else kc_ref[before, :]
            v_prev = vp_ref[...] if c == 0 else vc_ref[before, :]
            k_var = _kv_variants(jnp.concatenate([k_prev, kc_ref[rows, :]], axis=0))
            v_var = _kv_variants(jnp.concatenate([v_prev, vc_ref[rows, :]], axis=0))
            scores = [_dot(q_ref[rows, pair_cols[h // 2]], k_var[h // 4][h % 2], NT) for h in heads]

            ug = _gelu(u_ref[rows, :])
            xhat, _ = _layer_norm_stats(_gelu(vg_ref[rows, :]))
            vgl = xhat * g_ref[...] + b_ref[...]
            mixed = [_dot(wcat[p], _pair_stack(vgl[:, cols], left)) for p, cols in enumerate(pair_cols)]

            valid = _band_mask(CHUNKS_PER_STEP * i + c)
            masked = [jnp.where(valid, scores[h] * SCALE, NEG_INF) for h in heads]
            maxes = [jnp.maximum(jnp.max(masked[h], axis=1, keepdims=True), sinks_h[h]) for h in heads]
            exps = [jnp.exp(masked[h] - maxes[h]) for h in heads]
            invs = [1.0 / (jnp.sum(exps[h], axis=1, keepdims=True) + jnp.exp(sinks_h[h] - maxes[h])) for h in heads]
            probs = [(exps[h] * invs[h]).astype(BF16) for h in heads]
            for p, cols in enumerate(pair_cols):
                cat_ref[rows, cols] = (ug[:, cols] * (mixed[p] + bias_ref[:, cols])).astype(BF16)
            for p in range(D_ATTN // LANES):
                out = _dot(probs[2 * p], v_var[p // 2][0]) + _dot(probs[2 * p + 1], v_var[p // 2][1])
                cat_ref[rows, D_GMLP + p * LANES : D_GMLP + (p + 1) * LANES] = out.astype(BF16)

    in_specs = _chunk_specs() + [
        _const_spec((1, D_GMLP)),
        _const_spec((1, D_GMLP)),
        _const_spec((N_HEADS, CHUNK, CHUNK)),
        _const_spec((CHUNK, D_GMLP)),
        pl.BlockSpec(memory_space=pltpu.SMEM),
    ]
    body, in_specs, operands = _after(dep, body, in_specs, [u, vg, q, k, k, va, va, v_ln_g, v_ln_b, w_spatial, bias_full, sinks])
    return pl.pallas_call(
        body,
        name="mixer_fwd",
        grid=(t // (CHUNKS_PER_STEP * CHUNK),),
        in_specs=in_specs,
        out_specs=pl.BlockSpec((CHUNKS_PER_STEP * CHUNK, D_MODEL), lambda i: (i, 0)),
        out_shape=jax.ShapeDtypeStruct((t, D_MODEL), BF16),
        scratch_shapes=[pltpu.VMEM((D_GMLP // LANES, CHUNK, 2 * CHUNK), BF16)],
        compiler_params=_params(("arbitrary",)),
    )(*operands)


def _ffn_fwd_loss(cat, x, w_out, ln1_g, ln1_b, w1, w2, ln2_g, ln2_b, target):
    t = x.shape[0]

    def body(cat_ref, x_ref, wo_ref, g1_ref, b1_ref, w1_ref, w2_ref, g2_ref, b2_ref, tgt_ref,
             xh_ref, rstd_ref, x1b_ref, r_ref, dz2_ref, dz2b_ref, dg2_ref, db2_ref, sq_ref):
        @pl.when(pl.program_id(0) == 0)
        def _():
            dg2_ref[...] = jnp.zeros_like(dg2_ref)
            db2_ref[...] = jnp.zeros_like(db2_ref)
            sq_ref[...] = jnp.zeros_like(sq_ref)

        xhat1, rstd1 = _layer_norm_stats(ALPHA * x_ref[...] + _dot(cat_ref[...], wo_ref[...]))
        xh_ref[...] = xhat1
        rstd_ref[...] = rstd1
        x1 = xhat1 * g1_ref[...] + b1_ref[...]
        x1b = x1.astype(BF16)
        x1b_ref[...] = x1b
        ff = jnp.zeros((TM_FFN, D_MODEL), F32)
        for j in range(N_FF_BLOCKS):
            r = jnp.maximum(_dot(x1b, w1_ref[j]), 0.0)
            r_ref[:, j * D_MODEL : (j + 1) * D_MODEL] = r.astype(BF16)
            ff = ff + _dot((r * r).astype(BF16), w2_ref[j])
        xhat2, rstd2 = _layer_norm_stats(ALPHA * x1 + ff)
        err = xhat2 * g2_ref[...] + b2_ref[...] - tgt_ref[...]
        sq_ref[...] += jnp.sum(err * err, axis=0, keepdims=True)
        dy = err * (1.0 / D_MODEL)
        dg2_ref[...] += jnp.sum(dy * xhat2, axis=0, keepdims=True)
        db2_ref[...] += jnp.sum(dy, axis=0, keepdims=True)
        dz2 = _layer_norm_bwd(dy * g2_ref[...], xhat2, rstd2)
        dz2_ref[...] = dz2
        dz2b_ref[...] = dz2.astype(BF16)

    vec = _const_spec((1, D_MODEL))
    tile = _row_spec(TM_FFN, D_MODEL)
    wspec = _const_spec((N_FF_BLOCKS, D_MODEL, D_MODEL), single_buffer=True)
    return pl.pallas_call(
        body,
        name="ffn_fwd_loss",
        grid=(t // TM_FFN,),
        in_specs=[tile, tile, _const_spec((D_MODEL, D_MODEL), single_buffer=True), vec, vec, wspec, wspec, vec, vec, tile],
        out_specs=[tile, _row_spec(TM_FFN, 1), tile, _row_spec(TM_FFN, D_FF), tile, tile, vec, vec, vec],
        out_shape=[
            jax.ShapeDtypeStruct((t, D_MODEL), F32),
            jax.ShapeDtypeStruct((t, 1), F32),
            jax.ShapeDtypeStruct((t, D_MODEL), BF16),
            jax.ShapeDtypeStruct((t, D_FF), BF16),
            jax.ShapeDtypeStruct((t, D_MODEL), F32),
            jax.ShapeDtypeStruct((t, D_MODEL), BF16),
            jax.ShapeDtypeStruct((1, D_MODEL), F32),
            jax.ShapeDtypeStruct((1, D_MODEL), F32),
            jax.ShapeDtypeStruct((1, D_MODEL), F32),
        ],
        compiler_params=_params(("arbitrary",)),
    )(cat, x, w_out, ln1_g, ln1_b, w1, w2, ln2_g, ln2_b, target)


def _ffn_bwd_ln1(dz2, r, xhat1, rstd1, ln1_g, w1, w2, w_out, dep=None):
    t = dz2.shape[0]

    def body(dz2_ref, r_ref, xh_ref, rstd_ref, g1_ref, w1_ref, w2_ref, wo_ref, dpre_ref, dz1_ref, dz1b_ref, dcat_ref, dg1_ref, db1_ref):
        @pl.when(pl.program_id(0) == 0)
        def _():
            dg1_ref[...] = jnp.zeros_like(dg1_ref)
            db1_ref[...] = jnp.zeros_like(db1_ref)

        dz2 = dz2_ref[...]
        dz2b = dz2.astype(BF16)
        dx1 = ALPHA * dz2
        for j in range(N_FF_BLOCKS):
            cols = slice(j * D_MODEL, (j + 1) * D_MODEL)
            dpre = (_dot(dz2b, w2_ref[j], NT) * (2.0 * r_ref[:, cols].astype(F32))).astype(BF16)
            dpre_ref[:, cols] = dpre
            dx1 = dx1 + _dot(dpre, w1_ref[j], NT)
        xhat1 = xh_ref[...]
        dg1_ref[...] += jnp.sum(dx1 * xhat1, axis=0, keepdims=True)
        db1_ref[...] += jnp.sum(dx1, axis=0, keepdims=True)
        dz1 = _layer_norm_bwd(dx1 * g1_ref[...], xhat1, rstd_ref[...])
        dz1_ref[...] = dz1
        dz1b = dz1.astype(BF16)
        dz1b_ref[...] = dz1b
        dcat_ref[...] = _dot(dz1b, wo_ref[...], NT).astype(BF16)

    vec = _const_spec((1, D_MODEL))
    tile = _row_spec(TM_FFN, D_MODEL)
    wspec = _const_spec((N_FF_BLOCKS, D_MODEL, D_MODEL), single_buffer=True)
    body, in_specs, operands = _after(
        dep, body,
        [tile, _row_spec(TM_FFN, D_FF), tile, _row_spec(TM_FFN, 1), vec, wspec, wspec, _const_spec((D_MODEL, D_MODEL), single_buffer=True)],
        [dz2, r, xhat1, rstd1, ln1_g, w1, w2, w_out])
    return pl.pallas_call(
        body,
        name="ffn_bwd_ln1",
        grid=(t // TM_FFN,),
        in_specs=in_specs,
        out_specs=[_row_spec(TM_FFN, D_FF), tile, tile, tile, vec, vec],
        out_shape=[
            jax.ShapeDtypeStruct((t, D_FF), BF16),
            jax.ShapeDtypeStruct((t, D_MODEL), F32),
            jax.ShapeDtypeStruct((t, D_MODEL), BF16),
            jax.ShapeDtypeStruct((t, D_MODEL), BF16),
            jax.ShapeDtypeStruct((1, D_MODEL), F32),
            jax.ShapeDtypeStruct((1, D_MODEL), F32),
        ],
        compiler_params=_params(("arbitrary",)),
    )(*operands)


def _mixer_bwd(u, vg, q, k, va, dcat, cos, sin, v_ln_g, v_ln_b, w_spatial, bias_full, sinks, dep=None):
    t = u.shape[0]
    n_chunks = t // CHUNK

    def body(u_ref, vg_ref, q_ref, kc_ref, kp_ref, vc_ref, vp_ref, dcat_ref, cosc_ref, sinc_ref, cosp_ref, sinp_ref,
             g_ref, b_ref, w_ref, bias_ref, sink_ref,
             dmain_ref, dkv_ref, dg_ref, db_ref, dw_ref, dbs_ref, dsink_ref, dmix_acc, wcat, wcat_t):
        i = pl.program_id(0)
        left = _half_lane_masks(CHUNK)
        lane = lax.broadcasted_iota(jnp.int32, (CHUNK, LANES), 1)
        n_pairs = D_GMLP // LANES

        @pl.when(i == 0)
        def _():
            dg_ref[...] = jnp.zeros_like(dg_ref)
            db_ref[...] = jnp.zeros_like(db_ref)
            dw_ref[...] = jnp.zeros_like(dw_ref)
            dsink_ref[...] = jnp.zeros_like(dsink_ref)
            dmix_acc[...] = jnp.zeros_like(dmix_acc)
            _store_spatial_weights(w_ref, wcat, wcat_t)

        n_qpairs = D_ATTN // LANES
        heads = range(N_HEADS)
        pair_cols = [slice(p * LANES, (p + 1) * LANES) for p in range(n_pairs)]
        sinks_h = [sink_ref[h] for h in heads]
        gain = g_ref[...]
        causal = _causal_mask()
        lane_row = lax.broadcasted_iota(jnp.int32, (1, LANES), 1)
        heads_per_group = N_HEADS // 2

        def group_grad_t(lhs_t, rhs_heads):
            parts = []
            for g in range(2):
                group = range(g * heads_per_group, (g + 1) * heads_per_group)
                lhs = jnp.concatenate([lhs_t[h * HEAD_DIM : (h + 1) * HEAD_DIM] for h in group], axis=1)
                parts.append(_dot(lhs, jnp.concatenate([rhs_heads[h] for h in group], axis=0)))
            return jnp.concatenate(parts, axis=0)

        for c in range(CHUNKS_PER_STEP):
            chunk = CHUNKS_PER_STEP * i + c
            rows = slice(c * CHUNK, (c + 1) * CHUNK)
            before = slice((c - 1) * CHUNK, c * CHUNK)

            k_prev = kp_ref[...] if c == 0 else kc_ref[before, :]
            v_prev = vp_ref[...] if c == 0 else vc_ref[before, :]
            k_var = _kv_variants(jnp.concatenate([k_prev, kc_ref[rows, :]], axis=0))
            v_var = _kv_variants(jnp.concatenate([v_prev, vc_ref[rows, :]], axis=0))
            q_pairs = [q_ref[rows, cols] for cols in pair_cols]
            do_all = dcat_ref[rows, D_GMLP:D_MODEL]
            do_pairs = [do_all[:, cols] for cols in pair_cols]
            scores = [_dot(q_pairs[h // 2], k_var[h // 4][h % 2], NT) for h in heads]
            dprobs = [_dot(do_pairs[h // 2], v_var[h // 4][h % 2], NT) for h in heads]
            q_t = q_ref[rows, :].astype(F32).T.astype(BF16)
            do_t = do_all.astype(F32).T.astype(BF16)

            ug, dug_du = _gelu_and_grad(u_ref[rows, :])
            gv, dgv_dv = _gelu_and_grad(vg_ref[rows, :])
            xhat, rstd = _layer_norm_stats(gv)
            vgl = xhat * gain + b_ref[...]
            mixed = [_dot(wcat[p], _pair_stack(vgl[:, cols], left)) for p, cols in enumerate(pair_cols)]

            valid = _band_mask(chunk)
            masked = [jnp.where(valid, scores[h] * SCALE, NEG_INF) for h in heads]
            maxes = [jnp.maximum(jnp.max(masked[h], axis=1, keepdims=True), sinks_h[h]) for h in heads]
            exps = [jnp.exp(masked[h] - maxes[h]) for h in heads]
            exp_sinks = [jnp.exp(sinks_h[h] - maxes[h]) for h in heads]
            invs = [1.0 / (jnp.sum(exps[h], axis=1, keepdims=True) + exp_sinks[h]) for h in heads]
            probs = [exps[h] * invs[h] for h in heads]
            dsums = [jnp.sum(probs[h] * dprobs[h], axis=1, keepdims=True) for h in heads]
            ds_b = [(probs[h] * (dprobs[h] - dsums[h]) * SCALE).astype(BF16) for h in heads]
            probs_b = [probs[h].astype(BF16) for h in heads]

            dm_stacks = []
            for p, cols in enumerate(pair_cols):
                da = dcat_ref[rows, cols].astype(F32)
                dmain_ref[rows, cols] = (da * (mixed[p] + bias_ref[:, cols]) * dug_du[:, cols]).astype(BF16)
                dmixed = da * ug[:, cols]
                dmix_acc[:, cols] += dmixed
                dm_stacks.append(_pair_stack(dmixed, left))

            dq_all = jnp.concatenate(
                [_dot(ds_b[2 * p], k_var[p // 2][0]) + _dot(ds_b[2 * p + 1], k_var[p // 2][1]) for p in range(n_qpairs)], axis=1)
            dk2_t = group_grad_t(q_t, ds_b)
            dv2_t = group_grad_t(do_t, probs_b)

            for p, cols in enumerate(pair_cols):
                dw_pair = _dot(dm_stacks[p], vgl[:, cols].astype(BF16), NT)
                dw_ref[2 * p] += jnp.where(causal, dw_pair[:CHUNK], 0.0)
                dw_ref[2 * p + 1] += jnp.where(causal, dw_pair[CHUNK:], 0.0)
            dvgl = jnp.concatenate([_dot(wcat_t[p], dm_stacks[p]) for p in range(n_pairs)], axis=1)

            dsink_row = jnp.zeros((1, LANES), F32)
            for h in heads:
                d_sink = -jnp.sum(exp_sinks[h] * invs[h] * dsums[h], axis=0, keepdims=True)
                dsink_row = dsink_row + jnp.where(lane_row == h, d_sink, 0.0)
            dsink_ref[0:1, :] += dsink_row
            cos_c, sin_c = cosc_ref[rows, :], sinc_ref[rows, :]
            cos_p = cosp_ref[...] if c == 0 else cosc_ref[before, :]
            sin_p = sinp_ref[...] if c == 0 else sinc_ref[before, :]
            dmain_ref[rows, 2 * D_GMLP : D_MAIN] = _rope_transposed(dq_all, _lane_tile(cos_c, n_qpairs), _lane_tile(sin_c, n_qpairs)).astype(BF16)
            dk2 = dk2_t.T
            dv2 = dv2_t.T
            cur = pl.ds(pl.multiple_of(chunk * CHUNK, CHUNK), CHUNK)
            dkv_ref[cur, 0:D_KV] = _rope_transposed(dk2[CHUNK:], cos_c, sin_c)
            dkv_ref[cur, D_KV : 2 * D_KV] = dv2[CHUNK:]
            prev = pl.ds(pl.multiple_of(jnp.maximum(chunk - 1, 0) * CHUNK, CHUNK), CHUNK)
            dkv_ref[prev, 0:D_KV] += _rope_transposed(dk2[:CHUNK], cos_p, sin_p)
            dkv_ref[prev, D_KV : 2 * D_KV] += dv2[:CHUNK]

            dg_ref[...] += jnp.sum(dvgl * xhat, axis=0, keepdims=True)
            db_ref[...] += jnp.sum(dvgl, axis=0, keepdims=True)
            dgv = _layer_norm_bwd(dvgl * gain, xhat, rstd)
            dmain_ref[rows, D_GMLP : 2 * D_GMLP] = (dgv * dgv_dv).astype(BF16)

        @pl.when(i == n_chunks // CHUNKS_PER_STEP - 1)
        def _():
            tile = jnp.zeros((CHUNK, LANES), F32)
            for p, cols in enumerate(pair_cols):
                dm = dmix_acc[:, cols]
                sl = jnp.sum(jnp.where(left, dm, 0.0), axis=1, keepdims=True)
                sr = jnp.sum(jnp.where(left, 0.0, dm), axis=1, keepdims=True)
                tile = jnp.where(lane == 2 * p, sl, tile)
                tile = jnp.where(lane == 2 * p + 1, sr, tile)
            dbs_ref[...] = tile

    step = CHUNKS_PER_STEP * CHUNK
    in_specs = _chunk_specs() + [
        pl.BlockSpec((step, D_MODEL), _step_rows),
        pl.BlockSpec((step, LANES), _step_rows),
        pl.BlockSpec((step, LANES), _step_rows),
        pl.BlockSpec((CHUNK, LANES), _chunk_before_step),
        pl.BlockSpec((CHUNK, LANES), _chunk_before_step),
        _const_spec((1, D_GMLP)),
        _const_spec((1, D_GMLP)),
        _const_spec((N_HEADS, CHUNK, CHUNK)),
        _const_spec((CHUNK, D_GMLP)),
        pl.BlockSpec(memory_space=pltpu.SMEM),
    ]
    body, in_specs, operands = _after(
        dep, body, in_specs, [u, vg, q, k, k, va, va, dcat, cos, sin, cos, sin, v_ln_g, v_ln_b, w_spatial, bias_full, sinks])
    return pl.pallas_call(
        body,
        name="mixer_bwd",
        grid=(n_chunks // CHUNKS_PER_STEP,),
        in_specs=in_specs,
        out_specs=[
            pl.BlockSpec((step, D_MAIN), _step_rows),
            _const_spec((t, 2 * D_KV)),
            _const_spec((1, D_GMLP)),
            _const_spec((1, D_GMLP)),
            _const_spec((N_HEADS, CHUNK, CHUNK)),
            _const_spec((CHUNK, LANES)),
            _const_spec((8, LANES)),
        ],
        out_shape=[
            jax.ShapeDtypeStruct((t, D_MAIN), BF16),
            jax.ShapeDtypeStruct((t, 2 * D_KV), F32),
            jax.ShapeDtypeStruct((1, D_GMLP), F32),
            jax.ShapeDtypeStruct((1, D_GMLP), F32),
            jax.ShapeDtypeStruct((N_HEADS, CHUNK, CHUNK), F32),
            jax.ShapeDtypeStruct((CHUNK, LANES), F32),
            jax.ShapeDtypeStruct((8, LANES), F32),
        ],
        scratch_shapes=[
            pltpu.VMEM((CHUNK, D_GMLP), F32),
            pltpu.VMEM((D_GMLP // LANES, CHUNK, 2 * CHUNK), BF16),
            pltpu.VMEM((D_GMLP // LANES, CHUNK, 2 * CHUNK), BF16),
        ],
        compiler_params=_params(("arbitrary",)),
    )(*operands)


def _grad_x(dh_main, dkv, dz1, w_in_t, dep=None):
    t = dz1.shape[0]

    def body(dm_ref, dkv_ref, dz1_ref, w_ref, gx_ref):
        acc = ALPHA * dz1_ref[...] + _dot(dm_ref[...], w_ref[0:D_MAIN, :])
        gx_ref[...] = acc + _dot(dkv_ref[...].astype(BF16), w_ref[D_MAIN:D_IN, :])

    body, in_specs, operands = _after(
        dep, body, [_row_spec(TM, D_MAIN), _row_spec(TM, 2 * D_KV), _row_spec(TM, D_MODEL), _const_spec((D_IN, D_MODEL))], [dh_main, dkv, dz1, w_in_t])
    return pl.pallas_call(
        body,
        name="grad_x",
        grid=(t // TM,),
        in_specs=in_specs,
        out_specs=_row_spec(TM, D_MODEL),
        out_shape=jax.ShapeDtypeStruct((t, D_MODEL), F32),
        compiler_params=_params(("parallel",)),
    )(*operands)


def _token_contraction(name, out_rows, tk, in_arrays, contributions, dep=None):
    t = in_arrays[0].shape[0]

    def body(*refs):
        out_ref = refs[-1]

        @pl.when(pl.program_id(0) == 0)
        def _():
            out_ref[...] = jnp.zeros_like(out_ref)

        for row0, a, b in contributions(*refs[:-1]):
            out_ref[row0 : row0 + a.shape[1], :] += _dot(a, b, TN)

    in_specs = [_row_spec(tk, a.shape[1]) for a in in_arrays]
    body, in_specs, operands = _after(dep, body, in_specs, in_arrays)
    return pl.pallas_call(
        body,
        name=name,
        grid=(t // tk,),
        in_specs=in_specs,
        out_specs=_const_spec((out_rows, D_MODEL), single_buffer=True),
        out_shape=jax.ShapeDtypeStruct((out_rows, D_MODEL), F32),
        compiler_params=_params(("arbitrary",)),
    )(*operands)


def _grad_w_out(cat, dz1b, dep=None):
    def contributions(cat_ref, dz1_ref):
        return [(0, cat_ref[...], dz1_ref[...])]

    return _token_contraction("grad_w_out", D_MODEL, TK, [cat, dz1b], contributions, dep)


def _grad_w_ff1(x1b, dpre):
    def contributions(x1_ref, dpre_ref):
        x1 = x1_ref[...]
        return [(j * D_MODEL, x1, dpre_ref[:, j * D_MODEL : (j + 1) * D_MODEL]) for j in range(N_FF_BLOCKS)]

    return _token_contraction("grad_w_ff1", D_FF, TK_FF, [x1b, dpre], contributions)


def _grad_w_ff2(r, dz2b):
    def contributions(r_ref, dz2_ref):
        dz2 = dz2_ref[...]
        out = []
        for j in range(N_FF_BLOCKS):
            rf = r_ref[:, j * D_MODEL : (j + 1) * D_MODEL].astype(F32)
            out.append((j * D_MODEL, (rf * rf).astype(BF16), dz2))
        return out

    return _token_contraction("grad_w_ff2", D_FF, TK_FF, [r, dz2b], contributions)


ANY = pl.BlockSpec(memory_space=pl.ANY)


def _mesh_position():
    return lax.axis_index("x"), lax.axis_index("y"), lax.axis_index("c")


def _other_chips(x, y):
    return [(1 - x, y), (x, 1 - y), (1 - x, 1 - y)]


def _remote(src, dst, send_sem, recv_sem, device):
    return pltpu.make_async_remote_copy(src_ref=src, dst_ref=dst, send_sem=send_sem, recv_sem=recv_sem, device_id=device, device_id_type=MESH)


def _rows(ref, start, size):
    return ref.at[pl.ds(start, size), :]


def _gather_w_in_and_rope_tables(shard, pos_row, inv_freq_row):
    t = pos_row.shape[1]
    steps = t // TM
    rows = shard.shape[0]
    half = rows // 2

    def body(pos_ref, f_ref, shard_ref, cos_ref, sin_ref, all_ref, send_sems, recv_sems):
        k = pl.program_id(0)
        x, y, c = _mesh_position()
        me = 2 * x + y
        chips = _other_chips(x, y)
        sibling = (x, y, 1 - c)

        def direct_copies():
            out = []
            for kk, (px, py) in enumerate(chips):
                send = _remote(_rows(shard_ref, c * half, half), _rows(all_ref, me * rows + c * half, half),
                               send_sems.at[kk], recv_sems.at[kk], (px, py, c))
                landed = _rows(all_ref, (2 * px + py) * rows + c * half, half)
                out.append((send, _remote(landed, landed, send_sems.at[kk], recv_sems.at[kk], (px, py, c))))
            own = _rows(all_ref, me * rows, rows)
            out.append((_remote(shard_ref, own, send_sems.at[6], recv_sems.at[6], sibling),
                        _remote(own, own, send_sems.at[6], recv_sems.at[6], sibling)))
            return out

        def passed_on_copies():
            out = []
            for kk, (px, py) in enumerate(chips):
                row = (2 * px + py) * rows
                mine, theirs = _rows(all_ref, row + c * half, half), _rows(all_ref, row + (1 - c) * half, half)
                out.append((_remote(mine, mine, send_sems.at[3 + kk], recv_sems.at[3 + kk], sibling),
                            _remote(theirs, theirs, send_sems.at[3 + kk], recv_sems.at[3 + kk], sibling)))
            return out

        @pl.when(k == 0)
        def _():
            for send, _ in direct_copies():
                send.start()

        pos_rows = jnp.broadcast_to(pos_ref[...].astype(F32), (LANES, TM)).T
        ang = pos_rows * f_ref[...]
        cos_ref[...] = jnp.cos(ang)
        sin_ref[...] = jnp.sin(ang)

        @pl.when(k == steps - 1)
        def _():
            direct, passed = direct_copies(), passed_on_copies()
            for (_, arrived), (forward, _) in zip(direct[:3], passed):
                arrived.wait_recv()
                forward.start()
            for _, arrived in passed + direct[3:]:
                arrived.wait_recv()
            for send, _ in direct + passed:
                send.wait_send()

    return pl.pallas_call(
        body,
        name="gather_w_in_and_rope_tables",
        grid=(steps,),
        in_specs=[pl.BlockSpec((1, TM), lambda i: (0, i)), _const_spec((1, LANES)), ANY],
        out_specs=[_row_spec(TM, LANES), _row_spec(TM, LANES), ANY],
        out_shape=[jax.ShapeDtypeStruct((t, LANES), F32)] * 2 + [jax.ShapeDtypeStruct((N_CHIPS * rows, shard.shape[1]), shard.dtype)],
        scratch_shapes=[pltpu.SemaphoreType.DMA((7,)), pltpu.SemaphoreType.DMA((7,))],
        compiler_params=_params(("arbitrary",)),
    )(pos_row, inv_freq_row, shard)


def _pair_gather(name, shards):
    n = len(shards)

    def body(*refs):
        outs = refs[n : 2 * n]
        send_sems, recv_sems = refs[2 * n :]
        x, y, c = _mesh_position()
        sibling = (x, y, 1 - c)
        sends = []
        for w in range(n):
            half = shards[w].shape[0] // 2
            mine = _rows(outs[w], c * half, half)
            cp = _remote(mine, mine, send_sems.at[w], recv_sems.at[w], sibling)
            cp.start()
            sends.append(cp)
        for w in range(n):
            half = shards[w].shape[0] // 2
            blk = _rows(outs[w], (1 - c) * half, half)
            _remote(blk, blk, send_sems.at[w], recv_sems.at[w], sibling).wait_recv()
        for cp in sends:
            cp.wait_send()

    return pl.pallas_call(
        body,
        name=name,
        in_specs=[ANY] * n,
        out_specs=[ANY] * n,
        out_shape=[jax.ShapeDtypeStruct(s.shape, s.dtype) for s in shards],
        input_output_aliases={w: w for w in range(n)},
        scratch_shapes=[pltpu.SemaphoreType.DMA((n,)), pltpu.SemaphoreType.DMA((n,))],
    )(*shards)


def _grad_w_in_t_and_small_all_reduce(dh_main, dkv, x, slab):
    t = x.shape[0]
    steps = t // TK
    rows = slab.shape[0]
    part = rows // 8

    def body(dm_ref, dkv_ref, x_ref, slab_ref, grad_ref, sum_ref, landing, reduced, gathered, send_sems, recv_sems):
        k = pl.program_id(0)
        x_, y_, c_ = _mesh_position()
        me = 4 * x_ + 2 * y_ + c_
        flips = [(f >> 2, (f >> 1) & 1, f & 1) for f in range(1, 8)]

        def peer(flip):
            fx, fy, fc = flip
            return (1 - x_ if fx else x_, 1 - y_ if fy else y_, 1 - c_ if fc else c_)

        def part_of(ref, device):
            return ref.at[pl.ds(pl.multiple_of(device * part, 8), part), :]

        def scatter_copies():
            out = []
            for kk, flip in enumerate(flips):
                px, py, pc = peer(flip)
                them = 4 * px + 2 * py + pc
                send = _remote(part_of(slab_ref, them), landing.at[me], send_sems.at[kk], recv_sems.at[kk], (px, py, pc))
                recv = _remote(landing.at[them], landing.at[them], send_sems.at[kk], recv_sems.at[kk], (px, py, pc))
                out.append((send, recv))
            return out

        def gather_copies():
            out = []
            for kk, flip in enumerate(flips):
                px, py, pc = peer(flip)
                them = 4 * px + 2 * py + pc
                send = _remote(reduced, part_of(gathered, me), send_sems.at[7 + kk], recv_sems.at[7 + kk], (px, py, pc))
                recv = _remote(part_of(gathered, them), part_of(gathered, them), send_sems.at[7 + kk], recv_sems.at[7 + kk], (px, py, pc))
                out.append((send, recv))
            return out

        @pl.when(k == 0)
        def _():
            grad_ref[...] = jnp.zeros_like(grad_ref)
            for send, _ in scatter_copies():
                send.start()
            landing[me] = part_of(slab_ref, me)[...]

        @pl.when(k == steps // 2)
        def _():
            for _, recv in scatter_copies():
                recv.wait_recv()
            total = landing[0]
            for s in range(1, 8):
                total = total + landing[s]
            reduced[...] = total
            part_of(gathered, me)[...] = total
            for send, _ in gather_copies():
                send.start()

        xb = x_ref[...].astype(BF16)
        grad_ref[0:D_MAIN, :] += _dot(dm_ref[...], xb, TN)
        grad_ref[D_MAIN:D_IN, :] += _dot(dkv_ref[...].astype(BF16), xb, TN)

        @pl.when(k == steps - 1)
        def _():
            for send, recv in gather_copies():
                recv.wait_recv()
                send.wait_send()
            for send, _ in scatter_copies():
                send.wait_send()
            sum_ref[...] = gathered[...]

    return pl.pallas_call(
        body,
        name="grad_w_in_and_small_all_reduce",
        grid=(steps,),
        in_specs=[_row_spec(TK, D_MAIN), _row_spec(TK, 2 * D_KV), _row_spec(TK, D_MODEL), _const_spec(slab.shape)],
        out_specs=[_const_spec((D_IN, D_MODEL), single_buffer=True), _const_spec(slab.shape)],
        out_shape=[jax.ShapeDtypeStruct((D_IN, D_MODEL), F32), jax.ShapeDtypeStruct(slab.shape, slab.dtype)],
        scratch_shapes=[
            pltpu.VMEM((8, part, LANES), F32),
            pltpu.VMEM((part, LANES), F32),
            pltpu.VMEM(slab.shape, F32),
            pltpu.SemaphoreType.DMA((14,)),
            pltpu.SemaphoreType.DMA((14,)),
        ],
        compiler_params=_params(("arbitrary",)),
    )(dh_main, dkv, x, slab)


HBM = pl.BlockSpec(memory_space=pltpu.HBM)
SEM = pl.BlockSpec(memory_space=pltpu.SEMAPHORE)
DATAFLOW = pltpu.SideEffectType.DATAFLOW_SIDE_EFFECTING
TOKEN = jax.ShapeDtypeStruct((8, LANES), F32)


def _plan_copies(bufs, plan, send_sems, recv_sems):
    out = []
    for i, (src, src_row, dst, dst_row, recv_row, rows, device) in enumerate(plan):
        send = _remote(_rows(bufs[src], src_row, rows), _rows(bufs[dst], dst_row, rows), send_sems.at[i], recv_sems.at[i], device)
        landed = _rows(bufs[dst], recv_row, rows)
        recv = _remote(landed, landed, send_sems.at[i], recv_sems.at[i], device)
        out.append((send, recv))
    return out


def _split_call(name, bufs, wait=None, start=None, after=None):
    n = len(bufs)
    n_in = n + (2 if wait else 0) + (1 if after is not None else 0)
    n_start = len(start(0, 0, 0)) if start else 0

    def body(*refs):
        ins = refs[:n]
        x, y, c = _mesh_position()
        if wait:
            for send, recv in _plan_copies(ins, wait[0](x, y, c), refs[n], refs[n + 1]):
                recv.wait_recv()
                send.wait_send()
        if start:
            for send, _ in _plan_copies(ins, start(x, y, c), refs[n_in + n + 1], refs[n_in + n + 2]):
                send.start()
        token = refs[n_in + n]
        token[...] = jnp.zeros_like(token)

    operands = [pltpu.with_memory_space_constraint(b, pltpu.HBM) for b in bufs]
    in_specs = [HBM] * n
    if wait:
        operands += [wait[1], wait[2]]
        in_specs += [SEM, SEM]
    if after is not None:
        operands.append(after)
        in_specs.append(ANY)
    out_shape = [pltpu.HBM(b.shape, b.dtype) for b in bufs] + [TOKEN]
    out_specs = [HBM] * n + [pl.BlockSpec(memory_space=pltpu.VMEM)]
    if start:
        out_shape += [pltpu.SemaphoreType.DMA((n_start,)), pltpu.SemaphoreType.DMA((n_start,))]
        out_specs += [SEM, SEM]
    outs = pl.pallas_call(
        body,
        name=name,
        in_specs=in_specs,
        out_specs=out_specs,
        out_shape=out_shape,
        input_output_aliases={i: i for i in range(n)},
        compiler_params=pltpu.CompilerParams(has_side_effects=DATAFLOW),
    )(*operands)
    return (list(outs[:n]), outs[n]) + tuple(outs[n + 1 :])


def _gather_plans(shard_rows):
    n = len(shard_rows)

    def neighbours(x, y):
        return ((1 - x, y), (x, 1 - y))

    def direct(x, y, c):
        me = 2 * x + y
        plan = []
        for w, rows in enumerate(shard_rows):
            half = rows // 2
            for px, py in neighbours(x, y):
                plan.append((w, c * half, n + w, me * rows + c * half, (2 * px + py) * rows + c * half, half, (px, py, c)))
            plan.append((w, 0, n + w, me * rows, me * rows, rows, (x, y, 1 - c)))
        return plan

    def passed_on(x, y, c):
        (xn, yn), diagonal = neighbours(x, y), 2 * (1 - x) + (1 - y)
        relayed = (1 - c) * (2 * xn[0] + xn[1]) + c * (2 * yn[0] + yn[1])
        target = (x * (1 - c) + (1 - x) * c, (1 - y) * (1 - c) + y * c, c)
        plan = []
        for w, rows in enumerate(shard_rows):
            half = rows // 2
            for px, py in (xn, yn):
                row = (2 * px + py) * rows
                plan.append((n + w, row + c * half, n + w, row + c * half, row + (1 - c) * half, half, (x, y, 1 - c)))
            plan.append((n + w, relayed * rows + c * half, n + w, relayed * rows + c * half, diagonal * rows + c * half, half, target))
        return plan

    def diagonal_passed_on(x, y, c):
        plan = []
        for w, rows in enumerate(shard_rows):
            half = rows // 2
            row = (2 * (1 - x) + (1 - y)) * rows
            plan.append((n + w, row + c * half, n + w, row + c * half, row + (1 - c) * half, half, (x, y, 1 - c)))
        return plan

    return direct, passed_on, diagonal_passed_on


def _swap_plan(block_rows):
    n = len(block_rows)

    def plan_fn(x, y, c):
        plan = []
        for w, rows in enumerate(block_rows):
            half = rows // 2
            for j in range(N_CHIPS):
                plan.append((w, j * rows + (1 - c) * half, n + w, j * half, j * half, half, (x, y, 1 - c)))
        return plan

    return plan_fn


def _exchange_plan(halves):
    n = len(halves)

    def plan_fn(x, y, c):
        plan = []
        for w, half in enumerate(halves):
            for kk, (px, py) in enumerate(_other_chips(x, y)):
                plan.append((w, (2 * px + py) * half, n + w, kk * half, kk * half, half, (px, py, c)))
        return plan

    return plan_fn


def _landing(rows, cols, dtype):
    return lax.empty((rows, cols), dtype)


def _row_tile(rows, cap=512):
    best = 8
    for cand in range(8, cap + 1, 8):
        if rows % cand == 0:
            best = cand
    return best


def _pair_sum(name, grad, theirs, pos):
    half = theirs.shape[0] // N_CHIPS
    cols = theirs.shape[1]
    tile = _row_tile(half)
    steps = half // tile

    def body(pos_ref, g_ref, t_ref, p_ref, own_ref):
        total = g_ref[...] + t_ref[...]
        p_ref[...] = total.astype(BF16)

        @pl.when(pl.program_id(1) == pos_ref[1])
        def _():
            own_ref[...] = total

    return pl.pallas_call(
        body,
        name=name,
        grid_spec=pltpu.PrefetchScalarGridSpec(
            num_scalar_prefetch=1,
            grid=(steps, N_CHIPS),
            in_specs=[
                pl.BlockSpec((tile, cols), lambda i, j, pos: ((2 * j + pos[0]) * steps + i, 0)),
                pl.BlockSpec((tile, cols), lambda i, j, pos: (j * steps + i, 0)),
            ],
            out_specs=[
                pl.BlockSpec((tile, cols), lambda i, j, pos: (j * steps + i, 0)),
                pl.BlockSpec((tile, cols), lambda i, j, pos: (i, 0)),
            ],
        ),
        out_shape=[jax.ShapeDtypeStruct((N_CHIPS * half, cols), BF16), jax.ShapeDtypeStruct((half, cols), F32)],
        compiler_params=_params(("parallel", "arbitrary")),
    )(pos, grad, theirs)


def _chip_sum(name, own, landed, pos):
    half, cols = own.shape
    tile = _row_tile(half)
    steps = half // tile

    def body(pos_ref, own_ref, l0, l1, l2, o_ref):
        o_ref[...] = ((own_ref[...] + l0[...].astype(F32)) + l1[...].astype(F32)) + l2[...].astype(F32)

    landed_specs = [pl.BlockSpec((tile, cols), lambda i, pos, _k=k: (_k * steps + i, 0)) for k in range(N_CHIPS - 1)]
    return pl.pallas_call(
        body,
        name=name,
        grid_spec=pltpu.PrefetchScalarGridSpec(
            num_scalar_prefetch=1,
            grid=(steps,),
            in_specs=[pl.BlockSpec((tile, cols), lambda i, pos: (i, 0))] + landed_specs,
            out_specs=pl.BlockSpec((tile, cols), lambda i, pos: (pos[0] * steps + i, 0)),
        ),
        out_shape=jax.ShapeDtypeStruct((2 * half, cols), F32),
        compiler_params=_params(("parallel",)),
    )(pos, own, landed, landed, landed)


def _adamw(name, w, g, m, v):
    rows, cols = w.shape
    tile = rows if rows * cols <= 256 * 1024 else _row_tile(rows)

    def body(w_ref, g_ref, m_ref, v_ref, g_out_ref, d_ref, nm_ref, nv_ref):
        g = g_ref[...]
        g_out_ref[...] = g
        nm = ADAM_B1 * m_ref[...] + (1.0 - ADAM_B1) * g
        nv = ADAM_B2 * v_ref[...] + (1.0 - ADAM_B2) * (g * g)
        m_hat = nm / (1.0 - ADAM_B1**ADAM_STEP)
        v_hat = nv / (1.0 - ADAM_B2**ADAM_STEP)
        d_ref[...] = -ADAM_LR * (m_hat / (jnp.sqrt(v_hat) + ADAM_EPS) + ADAM_WD * w_ref[...])
        nm_ref[...] = nm
        nv_ref[...] = nv

    spec = _row_spec(tile, cols)
    return pl.pallas_call(
        body,
        name=name,
        grid=(rows // tile,),
        in_specs=[spec] * 4,
        out_specs=[spec] * 4,
        out_shape=[jax.ShapeDtypeStruct((rows, cols), F32)] * 4,
        compiler_params=_params(("parallel",)),
    )(w, g, m, v)


_SMALL = (
    ("v_ln_g", (D_GMLP,), 8),
    ("v_ln_b", (D_GMLP,), 8),
    ("w_spatial", (N_HEADS, CHUNK, CHUNK), 1024),
    ("b_spatial", (N_HEADS, CHUNK), 8),
    ("sinks", (N_HEADS,), 8),
    ("ln1_g", (D_MODEL,), 8),
    ("ln1_b", (D_MODEL,), 8),
    ("ln2_g", (D_MODEL,), 8),
    ("ln2_b", (D_MODEL,), 8),
    ("squared_error", (D_MODEL,), 8),
)
N_SMALL_PARAMS = len(_SMALL) - 1


def _pack_small(values):
    parts = []
    for (name, shape, rows), val in zip(_SMALL, values, strict=True):
        flat = val.reshape(-1).astype(F32)
        parts.append(jnp.pad(flat, (0, rows * LANES - flat.shape[0])).reshape(rows, LANES))
    parts.append(jnp.zeros((SMALL_ROWS - sum(rows for _, _, rows in _SMALL), LANES), F32))
    return jnp.concatenate(parts, axis=0)


def _adamw_update(w, g, m, v):
    nm = ADAM_B1 * m + (1.0 - ADAM_B1) * g
    nv = ADAM_B2 * v + (1.0 - ADAM_B2) * (g * g)
    m_hat = nm / (1.0 - ADAM_B1**ADAM_STEP)
    v_hat = nv / (1.0 - ADAM_B2**ADAM_STEP)
    return -ADAM_LR * (m_hat / (jnp.sqrt(v_hat) + ADAM_EPS) + ADAM_WD * w), nm, nv


def _adamw_small(g_slab, params, first, second):
    n = N_SMALL_PARAMS

    def pieces(shape):
        if len(shape) == 3:
            return [((0, h), h * shape[1], shape[1], shape[2]) for h in range(shape[0])]
        if len(shape) == 2:
            return [((0,), 0, shape[0], shape[1])]
        if shape[0] >= LANES:
            return [((slice(None), slice(r * LANES, (r + 1) * LANES)), r, 1, LANES) for r in range(shape[0] // LANES)]
        return [((slice(None), slice(0, shape[0])), 0, 1, shape[0])]

    def body(*refs):
        g_ref = refs[0]
        w_refs, m_refs, v_refs = refs[1 : 1 + n], refs[1 + n : 1 + 2 * n], refs[1 + 2 * n : 1 + 3 * n]
        outs = refs[1 + 3 * n :]
        row0 = 0
        for idx, (_, shape, rows) in enumerate(_SMALL[:n]):
            for where, first_row, n_rows, lanes in pieces(shape):
                g = g_ref[row0 + first_row : row0 + first_row + n_rows, 0:lanes]
                delta, nm, nv = _adamw_update(w_refs[idx][where], g, m_refs[idx][where], v_refs[idx][where])
                for group, val in enumerate((g, delta, nm, nv)):
                    outs[group * n + idx][where] = val
            row0 += rows

    vmem = pl.BlockSpec(memory_space=pltpu.VMEM)
    shapes = [jax.ShapeDtypeStruct(p.shape, F32) for p in params]
    outs = pl.pallas_call(
        body,
        name="adamw_small",
        in_specs=[vmem] * (1 + 3 * n),
        out_specs=[vmem] * (4 * n),
        out_shape=shapes * 4,
        compiler_params=_params(),
    )(g_slab, *params, *first, *second)
    return [list(outs[group * n : (group + 1) * n]) for group in range(4)]


def kernel(x, positions, w_in, v_ln_g, v_ln_b, w_spatial, b_spatial, sinks, w_out, ln1_g, ln1_b, w_ff1, w_ff2, ln2_g, ln2_b, loss_target, m_w_in, m_v_ln_g, m_v_ln_b, m_w_spatial, m_b_spatial, m_sinks, m_w_out, m_ln1_g, m_ln1_b, m_w_ff1, m_w_ff2, m_ln2_g, m_ln2_b, v_w_in, v_v_ln_g, v_v_ln_b, v_w_spatial, v_b_spatial, v_sinks, v_w_out, v_ln1_g, v_ln1_b, v_w_ff1, v_w_ff2, v_ln2_g, v_ln2_b):
    t = x.shape[1]
    x2 = x.reshape(t, D_MODEL)
    target = loss_target.reshape(t, D_MODEL)

    inv_freq = ROPE_THETA ** (-jnp.arange(0, HEAD_DIM, 2, dtype=F32) / HEAD_DIM)
    cos, sin, w_in_t = _gather_w_in_and_rope_tables(
        w_in[0].T.astype(BF16), positions, jnp.tile(inv_freq, LANES // (HEAD_DIM // 2)).reshape(1, LANES))
    later = [w_out[0].astype(BF16), w_ff1[0].astype(BF16), w_ff2[0].astype(BF16)]
    later_rows = [s.shape[0] for s in later]
    direct_plan, pass_plan, diagonal_plan = _gather_plans(later_rows)
    bufs, started, direct_send, direct_recv = _split_call(
        "gather_start", later + [_landing(N_CHIPS * r, D_MODEL, BF16) for r in later_rows], start=direct_plan, after=w_in_t)

    u, vg, q, k, va = _in_proj(x2, w_in_t, cos, sin, dep=started)
    bias_full = jnp.repeat(b_spatial[0].T, HEAD_DIM, axis=1)
    sink_vec = sinks.reshape(N_HEADS)
    bufs, passing, pass_send, pass_recv = _split_call(
        "gather_pass", bufs, wait=(direct_plan, direct_send, direct_recv), start=pass_plan, after=u)
    cat = _mixer_fwd(u, vg, q, k, va, v_ln_g, v_ln_b, w_spatial[0], bias_full, sink_vec, dep=passing)
    bufs, passing, diag_send, diag_recv = _split_call(
        "gather_pass_diagonal", bufs, wait=(pass_plan, pass_send, pass_recv), start=diagonal_plan, after=cat)
    bufs, _ = _split_call("gather_end", bufs, wait=(diagonal_plan, diag_send, diag_recv), after=passing)
    w_out_all = bufs[3]
    w1_all = bufs[4].reshape(N_FF_BLOCKS, D_MODEL, D_MODEL)
    w2_all = bufs[5].reshape(N_FF_BLOCKS, D_MODEL, D_MODEL)
    xhat1, rstd1, x1b, r, dz2, dz2b, d_ln2_g, d_ln2_b, sq_err = _ffn_fwd_loss(
        cat, x2, w_out_all, ln1_g, ln1_b, w1_all, w2_all, ln2_g, ln2_b, target)

    pos = jnp.stack([lax.axis_index("c"), 2 * lax.axis_index("x") + lax.axis_index("y")]).astype(jnp.int32)
    half_landing = lambda g: _landing(g.shape[0] // 2, D_MODEL, F32)
    g_ff2_local = _grad_w_ff2(r, dz2b)
    swap_plan = _swap_plan([D_FF // N_CHIPS])
    ff2_bufs, swapping2, swap2_send, swap2_recv = _split_call("ff2_swap_start", [g_ff2_local, half_landing(g_ff2_local)], start=swap_plan)
    dpre, dz1, dz1b, dcat, d_ln1_g, d_ln1_b = _ffn_bwd_ln1(dz2, r, xhat1, rstd1, ln1_g, w1_all, w2_all, w_out_all, dep=swapping2)
    g_ff1_local = _grad_w_ff1(x1b, dpre)
    ff1_bufs, swapping1, swap1_send, swap1_recv = _split_call("ff1_swap_start", [g_ff1_local, half_landing(g_ff1_local)], start=swap_plan)
    g_out_local = _grad_w_out(cat, dz1b, dep=swapping1)
    ff2_bufs, swapped2 = _split_call("ff2_swap_wait", ff2_bufs, wait=(swap_plan, swap2_send, swap2_recv), after=g_out_local)
    ff1_bufs, _ = _split_call("ff1_swap_wait", ff1_bufs, wait=(swap_plan, swap1_send, swap1_recv), after=swapped2)
    ff_sums = [_pair_sum("grad_pair_sum_w_ff1", ff1_bufs[0], ff1_bufs[1], pos), _pair_sum("grad_pair_sum_w_ff2", ff2_bufs[0], ff2_bufs[1], pos)]
    ff_halves = [p.shape[0] // N_CHIPS for p, _ in ff_sums]
    exchange_plan = _exchange_plan(ff_halves)
    bufs, exchanging, ex_send, ex_recv = _split_call(
        "ff_exchange_start", [p for p, _ in ff_sums] + [_landing(3 * h, D_MODEL, BF16) for h in ff_halves], start=exchange_plan)
    dh_main, dkv, d_v_ln_g, d_v_ln_b, d_w_spatial, d_b_spatial_t, d_sinks = _mixer_bwd(
        u, vg, q, k, va, dcat, cos, sin, v_ln_g, v_ln_b, w_spatial[0], bias_full, sink_vec, dep=exchanging)
    g_in_local, small_g = _grad_w_in_t_and_small_all_reduce(dh_main, dkv, x2, _pack_small(
        [d_v_ln_g, d_v_ln_b, d_w_spatial, d_b_spatial_t[:, :N_HEADS].T, d_sinks[0, :N_HEADS], d_ln1_g, d_ln1_b, d_ln2_g, d_ln2_b, sq_err]))
    sq_row = sum(rows for _, _, rows in _SMALL[:N_SMALL_PARAMS])
    loss = 0.5 * jnp.sum(small_g[sq_row : sq_row + _SMALL[N_SMALL_PARAMS][2]]) / D_MODEL

    small = [g_in_local, g_out_local]
    small_swap_plan = _swap_plan([g.shape[0] // N_CHIPS for g in small])
    swap_bufs, small_swapping, ss_send, ss_recv = _split_call(
        "small_swap_start", small + [half_landing(g) for g in small], start=small_swap_plan)
    grad_x_flat = _grad_x(dh_main, dkv, dz1, w_in_t, dep=small_swapping)
    grad_x = grad_x_flat.reshape(1, t, D_MODEL)
    swap_bufs, _ = _split_call("small_swap_wait", swap_bufs, wait=(small_swap_plan, ss_send, ss_recv), after=grad_x_flat)
    pair_sums = [_pair_sum("grad_pair_sum_" + nm, g, th, pos) for nm, g, th in zip(["w_in", "w_out"], swap_bufs[:2], swap_bufs[2:])]
    small_halves = [p.shape[0] // N_CHIPS for p, _ in pair_sums]
    small_plan = _exchange_plan(small_halves)
    small_bufs, small_exchanging, sm_send, sm_recv = _split_call(
        "small_exchange_start", [p for p, _ in pair_sums] + [_landing(3 * h, D_MODEL, BF16) for h in small_halves], start=small_plan)

    bufs, _ = _split_call("ff_exchange_wait", bufs, wait=(exchange_plan, ex_send, ex_recv), after=small_exchanging)
    ff_shards = [_chip_sum("grad_chip_sum_" + nm, own, ld, pos) for nm, (_, own), ld in zip(["w_ff1", "w_ff2"], ff_sums, bufs[2:])]
    g_w_ff1, g_w_ff2 = _pair_gather("grad_pair_gather_ff", ff_shards)

    g_w_ff1, d_w_ff1, nm_w_ff1, nv_w_ff1 = _adamw("adamw_w_ff1", w_ff1[0], g_w_ff1, m_w_ff1[0], v_w_ff1[0])
    g_w_ff2, d_w_ff2, nm_w_ff2, nv_w_ff2 = _adamw("adamw_w_ff2", w_ff2[0], g_w_ff2, m_w_ff2[0], v_w_ff2[0])
    small_bufs, _ = _split_call("small_exchange_wait", small_bufs, wait=(small_plan, sm_send, sm_recv), after=nv_w_ff2)
    shards = [_chip_sum("grad_chip_sum_" + nm, own, ld, pos) for nm, (_, own), ld in zip(["w_in", "w_out"], pair_sums, small_bufs[2:])]
    g_w_in_t, g_w_out = _pair_gather("grad_pair_gather_small", shards)
    g_w_in, d_w_in, nm_w_in, nv_w_in = (a.T for a in _adamw("adamw_w_in", w_in[0].T, g_w_in_t, m_w_in[0].T, v_w_in[0].T))
    g_w_out, d_w_out, nm_w_out, nv_w_out = _adamw("adamw_w_out", w_out[0], g_w_out, m_w_out[0], v_w_out[0])
    small_grads, small_d, small_nm, small_nv = _adamw_small(
        small_g,
        [v_ln_g, v_ln_b, w_spatial, b_spatial, sinks, ln1_g, ln1_b, ln2_g, ln2_b],
        [m_v_ln_g, m_v_ln_b, m_w_spatial, m_b_spatial, m_sinks, m_ln1_g, m_ln1_b, m_ln2_g, m_ln2_b],
        [v_v_ln_g, v_v_ln_b, v_w_spatial, v_b_spatial, v_sinks, v_ln1_g, v_ln1_b, v_ln2_g, v_ln2_b])

    def with_big(small, w_in_v, w_out_v, w_ff1_v, w_ff2_v):
        g_vg, g_vb, g_ws, g_bs, g_sk, g_1g, g_1b, g_2g, g_2b = small
        return [w_in_v[None], g_vg, g_vb, g_ws, g_bs, g_sk, w_out_v[None], g_1g, g_1b, w_ff1_v[None], w_ff2_v[None], g_2g, g_2b]

    return (
        loss,
        grad_x,
        *with_big(small_grads, g_w_in, g_w_out, g_w_ff1, g_w_ff2),
        *with_big(small_d, d_w_in, d_w_out, d_w_ff1, d_w_ff2),
        *with_big(small_nm, nm_w_in, nm_w_out, nm_w_ff1, nm_w_ff2),
        *with_big(small_nv, nv_w_in, nv_w_out, nv_w_ff1, nv_w_ff2),
    )
```

```python
import math

import jax
import jax.numpy as jnp
from jax import lax
from jax.experimental import pallas as pl
from jax.experimental.pallas import tpu as pltpu

F32 = jnp.float32
BF16 = jnp.bfloat16

D_MODEL = 1024
HEAD_DIM = 64
D_GMLP = 512
D_ATTN = 512
D_KV = 128
D_IN = 2 * D_GMLP + D_ATTN + 2 * D_KV
D_MAIN = 2 * D_GMLP + D_ATTN
N_HEADS = 8
CHUNK = 128
CHUNKS_PER_STEP = 4
ROPE_THETA = 10000.0
D_FF = 4 * D_MODEL
N_FF_BLOCKS = 4
LN_EPS = 1e-5
ALPHA = (2.0 * 1) ** 0.25
NEG_INF = -1e30
SCALE = 1.0 / math.sqrt(HEAD_DIM)

ADAM_LR = 0.001
ADAM_B1 = 0.9
ADAM_B2 = 0.999
ADAM_EPS = 1e-08
ADAM_WD = 0.01
ADAM_STEP = 10

N_CHIPS = 4
LANES = 128
V7X_VMEM_BYTES = 64 * 1024 * 1024
VMEM_LIMIT = V7X_VMEM_BYTES - 8 * 1024 * 1024
TM = 512
TM_FFN = 256
TK = 1024
TK_FF = 1024
SMALL_ROWS = 1152
MESH = pl.DeviceIdType.MESH

NT = (((1,), (1,)), ((), ()))
TN = (((0,), (0,)), ((), ()))


def _dot(a, b, dims=None):
    if dims is None:
        return jnp.dot(a, b, preferred_element_type=F32)
    return lax.dot_general(a, b, dims, preferred_element_type=F32)


def _params(semantics=None):
    return pltpu.CompilerParams(dimension_semantics=semantics, vmem_limit_bytes=VMEM_LIMIT)


def _const_spec(shape, single_buffer=False):
    zeros = (0,) * len(shape)
    if single_buffer:
        return pl.BlockSpec(shape, lambda *_: zeros, pipeline_mode=pl.Buffered(1))
    return pl.BlockSpec(shape, lambda *_: zeros)


def _row_spec(rows, cols):
    return pl.BlockSpec((rows, cols), lambda i: (i, 0))


def _after(dep, body, in_specs, operands):
    if dep is None:
        return body, list(in_specs), list(operands)
    return (lambda dep_ref, *refs: body(*refs)), [pl.BlockSpec(memory_space=pl.ANY)] + list(in_specs), [dep] + list(operands)


def _gelu(x):
    k = math.sqrt(2.0 / math.pi)
    return 0.5 * x * (1.0 + jnp.tanh(k * (x + 0.044715 * (x * x * x))))


def _gelu_and_grad(x):
    k = math.sqrt(2.0 / math.pi)
    x2 = x * x
    t = jnp.tanh(k * (x + 0.044715 * (x2 * x)))
    g = 0.5 * x * (1.0 + t)
    dg = 0.5 * (1.0 + t) + 0.5 * x * (1.0 - t * t) * (k * (1.0 + 3.0 * 0.044715 * x2))
    return g, dg


def _layer_norm_stats(z):
    mu = jnp.mean(z, axis=-1, keepdims=True)
    zc = z - mu
    var = jnp.mean(zc * zc, axis=-1, keepdims=True)
    rstd = lax.rsqrt(var + LN_EPS)
    return zc * rstd, rstd


def _layer_norm_bwd(dxhat, xhat, rstd):
    m1 = jnp.mean(dxhat, axis=-1, keepdims=True)
    m2 = jnp.mean(dxhat * xhat, axis=-1, keepdims=True)
    return rstd * (dxhat - m1 - xhat * m2)


def _rotate_half(t):
    n = t.shape[1]
    lane = lax.broadcasted_iota(jnp.int32, t.shape, 1)
    first = (lane & (HEAD_DIM // 2)) == 0
    return jnp.where(first, -pltpu.roll(t, n - HEAD_DIM // 2, 1), pltpu.roll(t, HEAD_DIM // 2, 1))


def _rope(t, cos, sin):
    return t * cos + _rotate_half(t) * sin


def _rope_transposed(g, cos, sin):
    return g * cos - _rotate_half(g * sin)


def _lane_tile(a, reps):
    return jnp.tile(a, (1, reps)) if reps > 1 else a


def _in_proj(x, w_in_t, cos, sin, dep=None):
    t = x.shape[0]

    def body(x_ref, w_ref, cos_ref, sin_ref, u_ref, vg_ref, q_ref, k_ref, va_ref):
        xb = x_ref[...].astype(BF16)
        u_ref[...] = _dot(xb, w_ref[0:D_GMLP, :], NT)
        vg_ref[...] = _dot(xb, w_ref[D_GMLP : 2 * D_GMLP, :], NT)
        q = _dot(xb, w_ref[2 * D_GMLP : D_MAIN, :], NT)
        k = _dot(xb, w_ref[D_MAIN : D_MAIN + D_KV, :], NT)
        va_ref[...] = _dot(xb, w_ref[D_MAIN + D_KV : D_IN, :], NT).astype(BF16)
        c, s = cos_ref[...], sin_ref[...]
        q_ref[...] = _rope(q, _lane_tile(c, D_ATTN // LANES), _lane_tile(s, D_ATTN // LANES)).astype(BF16)
        k_ref[...] = _rope(k, c, s).astype(BF16)

    body, in_specs, operands = _after(
        dep, body, [_row_spec(TM, D_MODEL), _const_spec((D_IN, D_MODEL)), _row_spec(TM, LANES), _row_spec(TM, LANES)], [x, w_in_t, cos, sin])
    return pl.pallas_call(
        body,
        name="in_proj",
        grid=(t // TM,),
        in_specs=in_specs,
        out_specs=[_row_spec(TM, D_GMLP), _row_spec(TM, D_GMLP), _row_spec(TM, D_ATTN), _row_spec(TM, D_KV), _row_spec(TM, D_KV)],
        out_shape=[
            jax.ShapeDtypeStruct((t, D_GMLP), F32),
            jax.ShapeDtypeStruct((t, D_GMLP), F32),
            jax.ShapeDtypeStruct((t, D_ATTN), BF16),
            jax.ShapeDtypeStruct((t, D_KV), BF16),
            jax.ShapeDtypeStruct((t, D_KV), BF16),
        ],
        compiler_params=_params(("parallel",)),
    )(*operands)


def _step_rows(i):
    return (i, 0)


def _chunk_before_step(i):
    return (jnp.maximum(CHUNKS_PER_STEP * i - 1, 0), 0)


def _chunk_specs():
    step = CHUNKS_PER_STEP * CHUNK
    return [
        pl.BlockSpec((step, D_GMLP), _step_rows),
        pl.BlockSpec((step, D_GMLP), _step_rows),
        pl.BlockSpec((step, D_ATTN), _step_rows),
        pl.BlockSpec((step, D_KV), _step_rows),
        pl.BlockSpec((CHUNK, D_KV), _chunk_before_step),
        pl.BlockSpec((step, D_KV), _step_rows),
        pl.BlockSpec((CHUNK, D_KV), _chunk_before_step),
    ]


def _half_lane_masks(rows):
    lane = lax.broadcasted_iota(jnp.int32, (rows, LANES), 1)
    return lane < HEAD_DIM


def _kv_variants(kv2):
    left = _half_lane_masks(kv2.shape[0])
    f = kv2.astype(F32)
    swapped = pltpu.roll(f, HEAD_DIM, 1)
    zero = jnp.zeros_like(f)
    g0 = (jnp.where(left, f, zero).astype(BF16), jnp.where(left, zero, swapped).astype(BF16))
    g1 = (jnp.where(left, swapped, zero).astype(BF16), jnp.where(left, zero, f).astype(BF16))
    return (g0, g1)


def _band_mask(i, heads=1):
    row = lax.broadcasted_iota(jnp.int32, (heads * CHUNK, 2 * CHUNK), 0) & (CHUNK - 1)
    col = lax.broadcasted_iota(jnp.int32, (heads * CHUNK, 2 * CHUNK), 1)
    no_prev = jnp.where(i > 0, 0, 4 * CHUNK)
    in_prev = jnp.logical_and(col < CHUNK, (col - row) > no_prev)
    in_cur = jnp.logical_and(col >= CHUNK, (col - CHUNK) <= row)
    return jnp.logical_or(in_prev, in_cur)


def _causal_mask():
    row = lax.broadcasted_iota(jnp.int32, (CHUNK, CHUNK), 0)
    col = lax.broadcasted_iota(jnp.int32, (CHUNK, CHUNK), 1)
    return col <= row


def _store_spatial_weights(w_ref, wcat_ref, wcat_t_ref=None):
    causal = _causal_mask()
    for p in range(D_GMLP // LANES):
        wl = jnp.where(causal, w_ref[2 * p], 0.0)
        wr = jnp.where(causal, w_ref[2 * p + 1], 0.0)
        wcat_ref[p] = jnp.concatenate([wl, wr], axis=1).astype(BF16)
        if wcat_t_ref is not None:
            wcat_t_ref[p] = jnp.concatenate([wl.T, wr.T], axis=1).astype(BF16)


def _pair_stack(xp, left):
    return jnp.concatenate([jnp.where(left, xp, 0.0), jnp.where(left, 0.0, xp)], axis=0).astype(BF16)


def _mixer_fwd(u, vg, q, k, va, v_ln_g, v_ln_b, w_spatial, bias_full, sinks, dep=None):
    t = u.shape[0]

    def body(u_ref, vg_ref, q_ref, kc_ref, kp_ref, vc_ref, vp_ref, g_ref, b_ref, w_ref, bias_ref, sink_ref, cat_ref, wcat):
        i = pl.program_id(0)
        left = _half_lane_masks(CHUNK)

        @pl.when(i == 0)
        def _():
            _store_spatial_weights(w_ref, wcat)

        heads = range(N_HEADS)
        pair_cols = [slice(p * LANES, (p + 1) * LANES) for p in range(D_GMLP // LANES)]
        sinks_h = [sink_ref[h] for h in heads]
        for c in range(CHUNKS_PER_STEP):
            rows = slice(c * CHUNK, (c + 1) * CHUNK)
            before = slice((c - 1) * CHUNK, c * CHUNK)
            k_prev = kp_ref[...] if c == 0 else kc_ref[before, :]
            v_prev = vp_ref[...] if c == 0 else vc_ref[before, :]
            k_var = _kv_variants(jnp.concatenate([k_prev, kc_ref[rows, :]], axis=0))
            v_var = _kv_variants(jnp.concatenate([v_prev, vc_ref[rows, :]], axis=0))
            scores = [_dot(q_ref[rows, pair_cols[h // 2]], k_var[h // 4][h % 2], NT) for h in heads]

            ug = _gelu(u_ref[rows, :])
            xhat, _ = _layer_norm_stats(_gelu(vg_ref[rows, :]))
            vgl = xhat * g_ref[...] + b_ref[...]
            mixed = [_dot(wcat[p], _pair_stack(vgl[:, cols], left)) for p, cols in enumerate(pair_cols)]

            valid = _band_mask(CHUNKS_PER_STEP * i + c)
            masked = [jnp.where(valid, scores[h] * SCALE, NEG_INF) for h in heads]
            maxes = [jnp.maximum(jnp.max(masked[h], axis=1, keepdims=True), sinks_h[h]) for h in heads]
            exps = [jnp.exp(masked[h] - maxes[h]) for h in heads]
            invs = [1.0 / (jnp.sum(exps[h], axis=1, keepdims=True) + jnp.exp(sinks_h[h] - maxes[h])) for h in heads]
            probs = [(exps[h] * invs[h]).astype(BF16) for h in heads]
            for p, cols in enumerate(pair_cols):
                cat_ref[rows, cols] = (ug[:, cols] * (mixed[p] + bias_ref[:, cols])).astype(BF16)
            for p in range(D_ATTN // LANES):
                out = _dot(probs[2 * p], v_var[p // 2][0]) + _dot(probs[2 * p + 1], v_var[p // 2][1])
                cat_ref[rows, D_GMLP + p * LANES : D_GMLP + (p + 1) * LANES] = out.astype(BF16)

    in_specs = _chunk_specs() + [
        _const_spec((1, D_GMLP)),
        _const_spec((1, D_GMLP)),
        _const_spec((N_HEADS, CHUNK, CHUNK)),
        _const_spec((CHUNK, D_GMLP)),
        pl.BlockSpec(memory_space=pltpu.SMEM),
    ]
    body, in_specs, operands = _after(dep, body, in_specs, [u, vg, q, k, k, va, va, v_ln_g, v_ln_b, w_spatial, bias_full, sinks])
    return pl.pallas_call(
        body,
        name="mixer_fwd",
        grid=(t // (CHUNKS_PER_STEP * CHUNK),),
        in_specs=in_specs,
        out_specs=pl.BlockSpec((CHUNKS_PER_STEP * CHUNK, D_MODEL), lambda i: (i, 0)),
        out_shape=jax.ShapeDtypeStruct((t, D_MODEL), BF16),
        scratch_shapes=[pltpu.VMEM((D_GMLP // LANES, CHUNK, 2 * CHUNK), BF16)],
        compiler_params=_params(("arbitrary",)),
    )(*operands)


def _ffn_fwd_loss(cat, x, w_out, ln1_g, ln1_b, w1, w2, ln2_g, ln2_b, target):
    t = x.shape[0]

    def body(cat_ref, x_ref, wo_ref, g1_ref, b1_ref, w1_ref, w2_ref, g2_ref, b2_ref, tgt_ref,
             xh_ref, rstd_ref, x1b_ref, r_ref, dz2_ref, dz2b_ref, dg2_ref, db2_ref, sq_ref):
        @pl.when(pl.program_id(0) == 0)
        def _():
            dg2_ref[...] = jnp.zeros_like(dg2_ref)
            db2_ref[...] = jnp.zeros_like(db2_ref)
            sq_ref[...] = jnp.zeros_like(sq_ref)

        xhat1, rstd1 = _layer_norm_stats(ALPHA * x_ref[...] + _dot(cat_ref[...], wo_ref[...]))
        xh_ref[...] = xhat1
        rstd_ref[...] = rstd1
        x1 = xhat1 * g1_ref[...] + b1_ref[...]
        x1b = x1.astype(BF16)
        x1b_ref[...] = x1b
        ff = jnp.zeros((TM_FFN, D_MODEL), F32)
        for j in range(N_FF_BLOCKS):
            r = jnp.maximum(_dot(x1b, w1_ref[j]), 0.0)
            r_ref[:, j * D_MODEL : (j + 1) * D_MODEL] = r.astype(BF16)
            ff = ff + _dot((r * r).astype(BF16), w2_ref[j])
        xhat2, rstd2 = _layer_norm_stats(ALPHA * x1 + ff)
        err = xhat2 * g2_ref[...] + b2_ref[...] - tgt_ref[...]
        sq_ref[...] += jnp.sum(err * err, axis=0, keepdims=True)
        dy = err * (1.0 / D_MODEL)
        dg2_ref[...] += jnp.sum(dy * xhat2, axis=0, keepdims=True)
        db2_ref[...] += jnp.sum(dy, axis=0, keepdims=True)
        dz2 = _layer_norm_bwd(dy * g2_ref[...], xhat2, rstd2)
        dz2_ref[...] = dz2
        dz2b_ref[...] = dz2.astype(BF16)

    vec = _const_spec((1, D_MODEL))
    tile = _row_spec(TM_FFN, D_MODEL)
    wspec = _const_spec((N_FF_BLOCKS, D_MODEL, D_MODEL), single_buffer=True)
    return pl.pallas_call(
        body,
        name="ffn_fwd_loss",
        grid=(t // TM_FFN,),
        in_specs=[tile, tile, _const_spec((D_MODEL, D_MODEL), single_buffer=True), vec, vec, wspec, wspec, vec, vec, tile],
        out_specs=[tile, _row_spec(TM_FFN, 1), tile, _row_spec(TM_FFN, D_FF), tile, tile, vec, vec, vec],
        out_shape=[
            jax.ShapeDtypeStruct((t, D_MODEL), F32),
            jax.ShapeDtypeStruct((t, 1), F32),
            jax.ShapeDtypeStruct((t, D_MODEL), BF16),
            jax.ShapeDtypeStruct((t, D_FF), BF16),
            jax.ShapeDtypeStruct((t, D_MODEL), F32),
            jax.ShapeDtypeStruct((t, D_MODEL), BF16),
            jax.ShapeDtypeStruct((1, D_MODEL), F32),
            jax.ShapeDtypeStruct((1, D_MODEL), F32),
            jax.ShapeDtypeStruct((1, D_MODEL), F32),
        ],
        compiler_params=_params(("arbitrary",)),
    )(cat, x, w_out, ln1_g, ln1_b, w1, w2, ln2_g, ln2_b, target)


def _ffn_bwd_ln1(dz2, r, xhat1, rstd1, ln1_g, w1, w2, w_out, dep=None):
    t = dz2.shape[0]

    def body(dz2_ref, r_ref, xh_ref, rstd_ref, g1_ref, w1_ref, w2_ref, wo_ref, dpre_ref, dz1_ref, dz1b_ref, dcat_ref, dg1_ref, db1_ref):
        @pl.when(pl.program_id(0) == 0)
        def _():
            dg1_ref[...] = jnp.zeros_like(dg1_ref)
            db1_ref[...] = jnp.zeros_like(db1_ref)

        dz2 = dz2_ref[...]
        dz2b = dz2.astype(BF16)
        dx1 = ALPHA * dz2
        for j in range(N_FF_BLOCKS):
            cols = slice(j * D_MODEL, (j + 1) * D_MODEL)
            dpre = (_dot(dz2b, w2_ref[j], NT) * (2.0 * r_ref[:, cols].astype(F32))).astype(BF16)
            dpre_ref[:, cols] = dpre
            dx1 = dx1 + _dot(dpre, w1_ref[j], NT)
        xhat1 = xh_ref[...]
        dg1_ref[...] += jnp.sum(dx1 * xhat1, axis=0, keepdims=True)
        db1_ref[...] += jnp.sum(dx1, axis=0, keepdims=True)
        dz1 = _layer_norm_bwd(dx1 * g1_ref[...], xhat1, rstd_ref[...])
        dz1_ref[...] = dz1
        dz1b = dz1.astype(BF16)
        dz1b_ref[...] = dz1b
        dcat_ref[...] = _dot(dz1b, wo_ref[...], NT).astype(BF16)

    vec = _const_spec((1, D_MODEL))
    tile = _row_spec(TM_FFN, D_MODEL)
    wspec = _const_spec((N_FF_BLOCKS, D_MODEL, D_MODEL), single_buffer=True)
    body, in_specs, operands = _after(
        dep, body,
        [tile, _row_spec(TM_FFN, D_FF), tile, _row_spec(TM_FFN, 1), vec, wspec, wspec, _const_spec((D_MODEL, D_MODEL), single_buffer=True)],
        [dz2, r, xhat1, rstd1, ln1_g, w1, w2, w_out])
    return pl.pallas_call(
        body,
        name="ffn_bwd_ln1",
        grid=(t // TM_FFN,),
        in_specs=in_specs,
        out_specs=[_row_spec(TM_FFN, D_FF), tile, tile, tile, vec, vec],
        out_shape=[
            jax.ShapeDtypeStruct((t, D_FF), BF16),
            jax.ShapeDtypeStruct((t, D_MODEL), F32),
            jax.ShapeDtypeStruct((t, D_MODEL), BF16),
            jax.ShapeDtypeStruct((t, D_MODEL), BF16),
            jax.ShapeDtypeStruct((1, D_MODEL), F32),
            jax.ShapeDtypeStruct((1, D_MODEL), F32),
        ],
        compiler_params=_params(("arbitrary",)),
    )(*operands)


def _mixer_bwd(u, vg, q, k, va, dcat, cos, sin, v_ln_g, v_ln_b, w_spatial, bias_full, sinks, dep=None):
    t = u.shape[0]
    n_chunks = t // CHUNK

    def body(u_ref, vg_ref, q_ref, kc_ref, kp_ref, vc_ref, vp_ref, dcat_ref, cosc_ref, sinc_ref, cosp_ref, sinp_ref,
             g_ref, b_ref, w_ref, bias_ref, sink_ref,
             dmain_ref, dkv_ref, dg_ref, db_ref, dw_ref, dbs_ref, dsink_ref, dmix_acc, wcat, wcat_t):
        i = pl.program_id(0)
        left = _half_lane_masks(CHUNK)
        lane = lax.broadcasted_iota(jnp.int32, (CHUNK, LANES), 1)
        n_pairs = D_GMLP // LANES

        @pl.when(i == 0)
        def _():
            dg_ref[...] = jnp.zeros_like(dg_ref)
            db_ref[...] = jnp.zeros_like(db_ref)
            dw_ref[...] = jnp.zeros_like(dw_ref)
            dsink_ref[...] = jnp.zeros_like(dsink_ref)
            dmix_acc[...] = jnp.zeros_like(dmix_acc)
            _store_spatial_weights(w_ref, wcat, wcat_t)

        n_qpairs = D_ATTN // LANES
        heads = range(N_HEADS)
        pair_cols = [slice(p * LANES, (p + 1) * LANES) for p in range(n_pairs)]
        sinks_h = [sink_ref[h] for h in heads]
        gain = g_ref[...]
        causal = _causal_mask()
        lane_row = lax.broadcasted_iota(jnp.int32, (1, LANES), 1)
        heads_per_group = N_HEADS // 2

        def group_grad_t(lhs_t, rhs_heads):
            parts = []
            for g in range(2):
                group = range(g * heads_per_group, (g + 1) * heads_per_group)
                lhs = jnp.concatenate([lhs_t[h * HEAD_DIM : (h + 1) * HEAD_DIM] for h in group], axis=1)
                parts.append(_dot(lhs, jnp.concatenate([rhs_heads[h] for h in group], axis=0)))
            return jnp.concatenate(parts, axis=0)

        for c in range(CHUNKS_PER_STEP):
            chunk = CHUNKS_PER_STEP * i + c
            rows = slice(c * CHUNK, (c + 1) * CHUNK)
            before = slice((c - 1) * CHUNK, c * CHUNK)

            k_prev = kp_ref[...] if c == 0 else kc_ref[before, :]
            v_prev = vp_ref[...] if c == 0 else vc_ref[before, :]
            k_var = _kv_variants(jnp.concatenate([k_prev, kc_ref[rows, :]], axis=0))
            v_var = _kv_variants(jnp.concatenate([v_prev, vc_ref[rows, :]], axis=0))
            q_pairs = [q_ref[rows, cols] for cols in pair_cols]
            do_all = dcat_ref[rows, D_GMLP:D_MODEL]
            do_pairs = [do_all[:, cols] for cols in pair_cols]
            scores = [_dot(q_pairs[h // 2], k_var[h // 4][h % 2], NT) for h in heads]
            dprobs = [_dot(do_pairs[h // 2], v_var[h // 4][h % 2], NT) for h in heads]
            q_t = q_ref[rows, :].astype(F32).T.astype(BF16)
            do_t = do_all.astype(F32).T.astype(BF16)

            ug, dug_du = _gelu_and_grad(u_ref[rows, :])
            gv, dgv_dv = _gelu_and_grad(vg_ref[rows, :])
            xhat, rstd = _layer_norm_stats(gv)
            vgl = xhat * gain + b_ref[...]
            mixed = [_dot(wcat[p], _pair_stack(vgl[:, cols], left)) for p, cols in enumerate(pair_cols)]

            valid = _band_mask(chunk)
            masked = [jnp.where(valid, scores[h] * SCALE, NEG_INF) for h in heads]
            maxes = [jnp.maximum(jnp.max(masked[h], axis=1, keepdims=True), sinks_h[h]) for h in heads]
            exps = [jnp.exp(masked[h] - maxes[h]) for h in heads]
            exp_sinks = [jnp.exp(sinks_h[h] - maxes[h]) for h in heads]
            invs = [1.0 / (jnp.sum(exps[h], axis=1, keepdims=True) + exp_sinks[h]) for h in heads]
            probs = [exps[h] * invs[h] for h in heads]
            dsums = [jnp.sum(probs[h] * dprobs[h], axis=1, keepdims=True) for h in heads]
            ds_b = [(probs[h] * (dprobs[h] - dsums[h]) * SCALE).astype(BF16) for h in heads]
            probs_b = [probs[h].astype(BF16) for h in heads]

            dm_stacks = []
            for p, cols in enumerate(pair_cols):
                da = dcat_ref[rows, cols].astype(F32)
                dmain_ref[rows, cols] = (da * (mixed[p] + bias_ref[:, cols]) * dug_du[:, cols]).astype(BF16)
                dmixed = da * ug[:, cols]
                dmix_acc[:, cols] += dmixed
                dm_stacks.append(_pair_stack(dmixed, left))

            dq_all = jnp.concatenate(
                [_dot(ds_b[2 * p], k_var[p // 2][0]) + _dot(ds_b[2 * p + 1], k_var[p // 2][1]) for p in range(n_qpairs)], axis=1)
            dk2_t = group_grad_t(q_t, ds_b)
            dv2_t = group_grad_t(do_t, probs_b)

            for p, cols in enumerate(pair_cols):
                dw_pair = _dot(dm_stacks[p], vgl[:, cols].astype(BF16), NT)
                dw_ref[2 * p] += jnp.where(causal, dw_pair[:CHUNK], 0.0)
                dw_ref[2 * p + 1] += jnp.where(causal, dw_pair[CHUNK:], 0.0)
            dvgl = jnp.concatenate([_dot(wcat_t[p], dm_stacks[p]) for p in range(n_pairs)], axis=1)

            dsink_row = jnp.zeros((1, LANES), F32)
            for h in heads:
                d_sink = -jnp.sum(exp_sinks[h] * invs[h] * dsums[h], axis=0, keepdims=True)
                dsink_row = dsink_row + jnp.where(lane_row == h, d_sink, 0.0)
            dsink_ref[0:1, :] += dsink_row
            cos_c, sin_c = cosc_ref[rows, :], sinc_ref[rows, :]
            cos_p = cosp_ref[...] if c == 0 else cosc_ref[before, :]
            sin_p = sinp_ref[...] if c == 0 else sinc_ref[before, :]
            dmain_ref[rows, 2 * D_GMLP : D_MAIN] = _rope_transposed(dq_all, _lane_tile(cos_c, n_qpairs), _lane_tile(sin_c, n_qpairs)).astype(BF16)
            dk2 = dk2_t.T
            dv2 = dv2_t.T
            cur = pl.ds(pl.multiple_of(chunk * CHUNK, CHUNK), CHUNK)
            dkv_ref[cur, 0:D_KV] = _rope_transposed(dk2[CHUNK:], cos_c, sin_c)
            dkv_ref[cur, D_KV : 2 * D_KV] = dv2[CHUNK:]
            prev = pl.ds(pl.multiple_of(jnp.maximum(chunk - 1, 0) * CHUNK, CHUNK), CHUNK)
            dkv_ref[prev, 0:D_KV] += _rope_transposed(dk2[:CHUNK], cos_p, sin_p)
            dkv_ref[prev, D_KV : 2 * D_KV] += dv2[:CHUNK]

            dg_ref[...] += jnp.sum(dvgl * xhat, axis=0, keepdims=True)
            db_ref[...] += jnp.sum(dvgl, axis=0, keepdims=True)
            dgv = _layer_norm_bwd(dvgl * gain, xhat, rstd)
            dmain_ref[rows, D_GMLP : 2 * D_GMLP] = (dgv * dgv_dv).astype(BF16)

        @pl.when(i == n_chunks // CHUNKS_PER_STEP - 1)
        def _():
            tile = jnp.zeros((CHUNK, LANES), F32)
            for p, cols in enumerate(pair_cols):
                dm = dmix_acc[:, cols]
                sl = jnp.sum(jnp.where(left, dm, 0.0), axis=1, keepdims=True)
                sr = jnp.sum(jnp.where(left, 0.0, dm), axis=1, keepdims=True)
                tile = jnp.where(lane == 2 * p, sl, tile)
                tile = jnp.where(lane == 2 * p + 1, sr, tile)
            dbs_ref[...] = tile

    step = CHUNKS_PER_STEP * CHUNK
    in_specs = _chunk_specs() + [
        pl.BlockSpec((step, D_MODEL), _step_rows),
        pl.BlockSpec((step, LANES), _step_rows),
        pl.BlockSpec((step, LANES), _step_rows),
        pl.BlockSpec((CHUNK, LANES), _chunk_before_step),
        pl.BlockSpec((CHUNK, LANES), _chunk_before_step),
        _const_spec((1, D_GMLP)),
        _const_spec((1, D_GMLP)),
        _const_spec((N_HEADS, CHUNK, CHUNK)),
        _const_spec((CHUNK, D_GMLP)),
        pl.BlockSpec(memory_space=pltpu.SMEM),
    ]
    body, in_specs, operands = _after(
        dep, body, in_specs, [u, vg, q, k, k, va, va, dcat, cos, sin, cos, sin, v_ln_g, v_ln_b, w_spatial, bias_full, sinks])
    return pl.pallas_call(
        body,
        name="mixer_bwd",
        grid=(n_chunks // CHUNKS_PER_STEP,),
        in_specs=in_specs,
        out_specs=[
            pl.BlockSpec((step, D_MAIN), _step_rows),
            _const_spec((t, 2 * D_KV)),
            _const_spec((1, D_GMLP)),
            _const_spec((1, D_GMLP)),
            _const_spec((N_HEADS, CHUNK, CHUNK)),
            _const_spec((CHUNK, LANES)),
            _const_spec((8, LANES)),
        ],
        out_shape=[
            jax.ShapeDtypeStruct((t, D_MAIN), BF16),
            jax.ShapeDtypeStruct((t, 2 * D_KV), F32),
            jax.ShapeDtypeStruct((1, D_GMLP), F32),
            jax.ShapeDtypeStruct((1, D_GMLP), F32),
            jax.ShapeDtypeStruct((N_HEADS, CHUNK, CHUNK), F32),
            jax.ShapeDtypeStruct((CHUNK, LANES), F32),
            jax.ShapeDtypeStruct((8, LANES), F32),
        ],
        scratch_shapes=[
            pltpu.VMEM((CHUNK, D_GMLP), F32),
            pltpu.VMEM((D_GMLP // LANES, CHUNK, 2 * CHUNK), BF16),
            pltpu.VMEM((D_GMLP // LANES, CHUNK, 2 * CHUNK), BF16),
        ],
        compiler_params=_params(("arbitrary",)),
    )(*operands)


def _grad_x(dh_main, dkv, dz1, w_in_t, dep=None):
    t = dz1.shape[0]

    def body(dm_ref, dkv_ref, dz1_ref, w_ref, gx_ref):
        acc = ALPHA * dz1_ref[...] + _dot(dm_ref[...], w_ref[0:D_MAIN, :])
        gx_ref[...] = acc + _dot(dkv_ref[...].astype(BF16), w_ref[D_MAIN:D_IN, :])

    body, in_specs, operands = _after(
        dep, body, [_row_spec(TM, D_MAIN), _row_spec(TM, 2 * D_KV), _row_spec(TM, D_MODEL), _const_spec((D_IN, D_MODEL))], [dh_main, dkv, dz1, w_in_t])
    return pl.pallas_call(
        body,
        name="grad_x",
        grid=(t // TM,),
        in_specs=in_specs,
        out_specs=_row_spec(TM, D_MODEL),
        out_shape=jax.ShapeDtypeStruct((t, D_MODEL), F32),
        compiler_params=_params(("parallel",)),
    )(*operands)


def _token_contraction(name, out_rows, tk, in_arrays, contributions, dep=None):
    t = in_arrays[0].shape[0]

    def body(*refs):
        out_ref = refs[-1]

        @pl.when(pl.program_id(0) == 0)
        def _():
            out_ref[...] = jnp.zeros_like(out_ref)

        for row0, a, b in contributions(*refs[:-1]):
            out_ref[row0 : row0 + a.shape[1], :] += _dot(a, b, TN)

    in_specs = [_row_spec(tk, a.shape[1]) for a in in_arrays]
    body, in_specs, operands = _after(dep, body, in_specs, in_arrays)
    return pl.pallas_call(
        body,
        name=name,
        grid=(t // tk,),
        in_specs=in_specs,
        out_specs=_const_spec((out_rows, D_MODEL), single_buffer=True),
        out_shape=jax.ShapeDtypeStruct((out_rows, D_MODEL), F32),
        compiler_params=_params(("arbitrary",)),
    )(*operands)


def _grad_w_out(cat, dz1b, dep=None):
    def contributions(cat_ref, dz1_ref):
        return [(0, cat_ref[...], dz1_ref[...])]

    return _token_contraction("grad_w_out", D_MODEL, TK, [cat, dz1b], contributions, dep)


def _grad_w_ff1(x1b, dpre):
    def contributions(x1_ref, dpre_ref):
        x1 = x1_ref[...]
        return [(j * D_MODEL, x1, dpre_ref[:, j * D_MODEL : (j + 1) * D_MODEL]) for j in range(N_FF_BLOCKS)]

    return _token_contraction("grad_w_ff1", D_FF, TK_FF, [x1b, dpre], contributions)


def _grad_w_ff2(r, dz2b):
    def contributions(r_ref, dz2_ref):
        dz2 = dz2_ref[...]
        out = []
        for j in range(N_FF_BLOCKS):
            rf = r_ref[:, j * D_MODEL : (j + 1) * D_MODEL].astype(F32)
            out.append((j * D_MODEL, (rf * rf).astype(BF16), dz2))
        return out

    return _token_contraction("grad_w_ff2", D_FF, TK_FF, [r, dz2b], contributions)


ANY = pl.BlockSpec(memory_space=pl.ANY)


def _mesh_position():
    return lax.axis_index("x"), lax.axis_index("y"), lax.axis_index("c")


def _other_chips(x, y):
    return [(1 - x, y), (x, 1 - y), (1 - x, 1 - y)]


def _remote(src, dst, send_sem, recv_sem, device):
    return pltpu.make_async_remote_copy(src_ref=src, dst_ref=dst, send_sem=send_sem, recv_sem=recv_sem, device_id=device, device_id_type=MESH)


def _rows(ref, start, size):
    return ref.at[pl.ds(start, size), :]


def _gather_w_in_and_rope_tables(shard, pos_row, inv_freq_row, other_shards):
    t = pos_row.shape[1]
    steps = t // TM
    rows = shard.shape[0]
    half = rows // 2
    n_other = len(other_shards)

    def body(pos_ref, f_ref, shard_ref, *rest):
        others, (cos_ref, sin_ref, all_ref) = rest[:n_other], rest[n_other : n_other + 3]
        others_bf16, (send_sems, recv_sems) = rest[n_other + 3 : 2 * n_other + 3], rest[2 * n_other + 3 :]
        for src, dst in zip(others, others_bf16):
            dst[...] = src[...].astype(BF16)
        k = pl.program_id(0)
        x, y, c = _mesh_position()
        me = 2 * x + y
        chips = _other_chips(x, y)
        sibling = (x, y, 1 - c)

        def direct_copies():
            out = []
            for kk, (px, py) in enumerate(chips):
                send = _remote(_rows(shard_ref, c * half, half), _rows(all_ref, me * rows + c * half, half),
                               send_sems.at[kk], recv_sems.at[kk], (px, py, c))
                landed = _rows(all_ref, (2 * px + py) * rows + c * half, half)
                out.append((send, _remote(landed, landed, send_sems.at[kk], recv_sems.at[kk], (px, py, c))))
            own = _rows(all_ref, me * rows, rows)
            out.append((_remote(shard_ref, own, send_sems.at[6], recv_sems.at[6], sibling),
                        _remote(own, own, send_sems.at[6], recv_sems.at[6], sibling)))
            return out

        def passed_on_copies():
            out = []
            for kk, (px, py) in enumerate(chips):
                row = (2 * px + py) * rows
                mine, theirs = _rows(all_ref, row + c * half, half), _rows(all_ref, row + (1 - c) * half, half)
                out.append((_remote(mine, mine, send_sems.at[3 + kk], recv_sems.at[3 + kk], sibling),
                            _remote(theirs, theirs, send_sems.at[3 + kk], recv_sems.at[3 + kk], sibling)))
            return out

        @pl.when(k == 0)
        def _():
            for send, _ in direct_copies():
                send.start()

        pos_rows = jnp.broadcast_to(pos_ref[...].astype(F32), (LANES, TM)).T
        ang = pos_rows * f_ref[...]
        cos_ref[...] = jnp.cos(ang)
        sin_ref[...] = jnp.sin(ang)

        @pl.when(k == steps - 1)
        def _():
            direct, passed = direct_copies(), passed_on_copies()
            for (_, arrived), (forward, _) in zip(direct[:3], passed):
                arrived.wait_recv()
                forward.start()
            for _, arrived in passed + direct[3:]:
                arrived.wait_recv()
            for send, _ in direct + passed:
                send.wait_send()

    other_specs = [_row_spec(s.shape[0] // steps, s.shape[1]) for s in other_shards]
    outs = pl.pallas_call(
        body,
        name="gather_w_in_and_rope_tables",
        grid=(steps,),
        in_specs=[pl.BlockSpec((1, TM), lambda i: (0, i)), _const_spec((1, LANES)), ANY] + other_specs,
        out_specs=[_row_spec(TM, LANES), _row_spec(TM, LANES), ANY] + other_specs,
        out_shape=[jax.ShapeDtypeStruct((t, LANES), F32)] * 2
        + [jax.ShapeDtypeStruct((N_CHIPS * rows, shard.shape[1]), shard.dtype)]
        + [jax.ShapeDtypeStruct(s.shape, BF16) for s in other_shards],
        scratch_shapes=[pltpu.SemaphoreType.DMA((7,)), pltpu.SemaphoreType.DMA((7,))],
        compiler_params=_params(("arbitrary",)),
    )(pos_row, inv_freq_row, shard, *other_shards)
    return outs[0], outs[1], outs[2], list(outs[3:])


def _pair_gather(name, shards):
    n = len(shards)

    def body(*refs):
        outs = refs[n : 2 * n]
        send_sems, recv_sems = refs[2 * n :]
        x, y, c = _mesh_position()
        sibling = (x, y, 1 - c)
        sends = []
        for w in range(n):
            half = shards[w].shape[0] // 2
            mine = _rows(outs[w], c * half, half)
            cp = _remote(mine, mine, send_sems.at[w], recv_sems.at[w], sibling)
            cp.start()
            sends.append(cp)
        for w in range(n):
            half = shards[w].shape[0] // 2
            blk = _rows(outs[w], (1 - c) * half, half)
            _remote(blk, blk, send_sems.at[w], recv_sems.at[w], sibling).wait_recv()
        for cp in sends:
            cp.wait_send()

    return pl.pallas_call(
        body,
        name=name,
        in_specs=[ANY] * n,
        out_specs=[ANY] * n,
        out_shape=[jax.ShapeDtypeStruct(s.shape, s.dtype) for s in shards],
        input_output_aliases={w: w for w in range(n)},
        scratch_shapes=[pltpu.SemaphoreType.DMA((n,)), pltpu.SemaphoreType.DMA((n,))],
    )(*shards)


def _grad_w_in_t_and_small_all_reduce(dh_main, dkv, x, slab):
    t = x.shape[0]
    steps = t // TK
    rows = slab.shape[0]
    part = rows // 8

    def body(dm_ref, dkv_ref, x_ref, slab_ref, grad_ref, sum_ref, landing, reduced, gathered, send_sems, recv_sems):
        k = pl.program_id(0)
        x_, y_, c_ = _mesh_position()
        me = 4 * x_ + 2 * y_ + c_
        flips = [(f >> 2, (f >> 1) & 1, f & 1) for f in range(1, 8)]

        def peer(flip):
            fx, fy, fc = flip
            return (1 - x_ if fx else x_, 1 - y_ if fy else y_, 1 - c_ if fc else c_)

        def part_of(ref, device):
            return ref.at[pl.ds(pl.multiple_of(device * part, 8), part), :]

        def scatter_copies():
            out = []
            for kk, flip in enumerate(flips):
                px, py, pc = peer(flip)
                them = 4 * px + 2 * py + pc
                send = _remote(part_of(slab_ref, them), landing.at[me], send_sems.at[kk], recv_sems.at[kk], (px, py, pc))
                recv = _remote(landing.at[them], landing.at[them], send_sems.at[kk], recv_sems.at[kk], (px, py, pc))
                out.append((send, recv))
            return out

        def gather_copies():
            out = []
            for kk, flip in enumerate(flips):
                px, py, pc = peer(flip)
                them = 4 * px + 2 * py + pc
                send = _remote(reduced, part_of(gathered, me), send_sems.at[7 + kk], recv_sems.at[7 + kk], (px, py, pc))
                recv = _remote(part_of(gathered, them), part_of(gathered, them), send_sems.at[7 + kk], recv_sems.at[7 + kk], (px, py, pc))
                out.append((send, recv))
            return out

        @pl.when(k == 0)
        def _():
            grad_ref[...] = jnp.zeros_like(grad_ref)
            for send, _ in scatter_copies():
                send.start()
            landing[me] = part_of(slab_ref, me)[...]

        @pl.when(k == steps // 2)
        def _():
            for _, recv in scatter_copies():
                recv.wait_recv()
            total = landing[0]
            for s in range(1, 8):
                total = total + landing[s]
            reduced[...] = total
            part_of(gathered, me)[...] = total
            for send, _ in gather_copies():
                send.start()

        xb = x_ref[...].astype(BF16)
        grad_ref[0:D_MAIN, :] += _dot(dm_ref[...], xb, TN)
        grad_ref[D_MAIN:D_IN, :] += _dot(dkv_ref[...].astype(BF16), xb, TN)

        @pl.when(k == steps - 1)
        def _():
            for send, recv in gather_copies():
                recv.wait_recv()
                send.wait_send()
            for send, _ in scatter_copies():
                send.wait_send()
            sum_ref[...] = gathered[...]

    return pl.pallas_call(
        body,
        name="grad_w_in_and_small_all_reduce",
        grid=(steps,),
        in_specs=[_row_spec(TK, D_MAIN), _row_spec(TK, 2 * D_KV), _row_spec(TK, D_MODEL), _const_spec(slab.shape)],
        out_specs=[_const_spec((D_IN, D_MODEL), single_buffer=True), _const_spec(slab.shape)],
        out_shape=[jax.ShapeDtypeStruct((D_IN, D_MODEL), F32), jax.ShapeDtypeStruct(slab.shape, slab.dtype)],
        scratch_shapes=[
            pltpu.VMEM((8, part, LANES), F32),
            pltpu.VMEM((part, LANES), F32),
            pltpu.VMEM(slab.shape, F32),
            pltpu.SemaphoreType.DMA((14,)),
            pltpu.SemaphoreType.DMA((14,)),
        ],
        compiler_params=_params(("arbitrary",)),
    )(dh_main, dkv, x, slab)


HBM = pl.BlockSpec(memory_space=pltpu.HBM)
SEM = pl.BlockSpec(memory_space=pltpu.SEMAPHORE)
DATAFLOW = pltpu.SideEffectType.DATAFLOW_SIDE_EFFECTING
TOKEN = jax.ShapeDtypeStruct((8, LANES), F32)


def _plan_copies(bufs, plan, send_sems, recv_sems):
    out = []
    for i, (src, src_row, dst, dst_row, recv_row, rows, device) in enumerate(plan):
        send = _remote(_rows(bufs[src], src_row, rows), _rows(bufs[dst], dst_row, rows), send_sems.at[i], recv_sems.at[i], device)
        landed = _rows(bufs[dst], recv_row, rows)
        recv = _remote(landed, landed, send_sems.at[i], recv_sems.at[i], device)
        out.append((send, recv))
    return out


def _split_call(name, bufs, wait=None, start=None, after=None):
    n = len(bufs)
    n_in = n + (2 if wait else 0) + (1 if after is not None else 0)
    n_start = len(start(0, 0, 0)) if start else 0

    def body(*refs):
        ins = refs[:n]
        x, y, c = _mesh_position()
        if wait:
            for send, recv in _plan_copies(ins, wait[0](x, y, c), refs[n], refs[n + 1]):
                recv.wait_recv()
                send.wait_send()
        if start:
            for send, _ in _plan_copies(ins, start(x, y, c), refs[n_in + n + 1], refs[n_in + n + 2]):
                send.start()
        token = refs[n_in + n]
        token[...] = jnp.zeros_like(token)

    operands = [pltpu.with_memory_space_constraint(b, pltpu.HBM) for b in bufs]
    in_specs = [HBM] * n
    if wait:
        operands += [wait[1], wait[2]]
        in_specs += [SEM, SEM]
    if after is not None:
        operands.append(after)
        in_specs.append(ANY)
    out_shape = [pltpu.HBM(b.shape, b.dtype) for b in bufs] + [TOKEN]
    out_specs = [HBM] * n + [pl.BlockSpec(memory_space=pltpu.VMEM)]
    if start:
        out_shape += [pltpu.SemaphoreType.DMA((n_start,)), pltpu.SemaphoreType.DMA((n_start,))]
        out_specs += [SEM, SEM]
    outs = pl.pallas_call(
        body,
        name=name,
        in_specs=in_specs,
        out_specs=out_specs,
        out_shape=out_shape,
        input_output_aliases={i: i for i in range(n)},
        compiler_params=pltpu.CompilerParams(has_side_effects=DATAFLOW),
    )(*operands)
    return (list(outs[:n]), outs[n]) + tuple(outs[n + 1 :])


def _gather_plans(shard_rows):
    n = len(shard_rows)

    def neighbours(x, y):
        return ((1 - x, y), (x, 1 - y))

    def direct(x, y, c):
        me = 2 * x + y
        plan = []
        for w, rows in enumerate(shard_rows):
            half = rows // 2
            for px, py in neighbours(x, y):
                plan.append((w, c * half, n + w, me * rows + c * half, (2 * px + py) * rows + c * half, half, (px, py, c)))
            plan.append((w, 0, n + w, me * rows, me * rows, rows, (x, y, 1 - c)))
        return plan

    def passed_on(x, y, c):
        (xn, yn), diagonal = neighbours(x, y), 2 * (1 - x) + (1 - y)
        relayed = (1 - c) * (2 * xn[0] + xn[1]) + c * (2 * yn[0] + yn[1])
        target = (x * (1 - c) + (1 - x) * c, (1 - y) * (1 - c) + y * c, c)
        plan = []
        for w, rows in enumerate(shard_rows):
            half = rows // 2
            for px, py in (xn, yn):
                row = (2 * px + py) * rows
                plan.append((n + w, row + c * half, n + w, row + c * half, row + (1 - c) * half, half, (x, y, 1 - c)))
            plan.append((n + w, relayed * rows + c * half, n + w, relayed * rows + c * half, diagonal * rows + c * half, half, target))
        return plan

    def diagonal_passed_on(x, y, c):
        plan = []
        for w, rows in enumerate(shard_rows):
            half = rows // 2
            row = (2 * (1 - x) + (1 - y)) * rows
            plan.append((n + w, row + c * half, n + w, row + c * half, row + (1 - c) * half, half, (x, y, 1 - c)))
        return plan

    return direct, passed_on, diagonal_passed_on


def _swap_plan(block_rows):
    n = len(block_rows)

    def plan_fn(x, y, c):
        plan = []
        for w, rows in enumerate(block_rows):
            half = rows // 2
            for j in range(N_CHIPS):
                plan.append((w, j * rows + (1 - c) * half, n + w, j * half, j * half, half, (x, y, 1 - c)))
        return plan

    return plan_fn


def _exchange_plan(halves):
    n = len(halves)

    def plan_fn(x, y, c):
        plan = []
        for w, half in enumerate(halves):
            for kk, (px, py) in enumerate(_other_chips(x, y)):
                plan.append((w, (2 * px + py) * half, n + w, kk * half, kk * half, half, (px, py, c)))
        return plan

    return plan_fn


def _landing(rows, cols, dtype):
    return lax.empty((rows, cols), dtype)


def _row_tile(rows, cap=512):
    best = 8
    for cand in range(8, cap + 1, 8):
        if rows % cand == 0:
            best = cand
    return best


def _pair_sum(name, grad, theirs, pos):
    half = theirs.shape[0] // N_CHIPS
    cols = theirs.shape[1]
    tile = _row_tile(half)
    steps = half // tile

    def body(pos_ref, g_ref, t_ref, p_ref, own_ref):
        total = g_ref[...] + t_ref[...]
        p_ref[...] = total.astype(BF16)

        @pl.when(pl.program_id(1) == pos_ref[1])
        def _():
            own_ref[...] = total

    return pl.pallas_call(
        body,
        name=name,
        grid_spec=pltpu.PrefetchScalarGridSpec(
            num_scalar_prefetch=1,
            grid=(steps, N_CHIPS),
            in_specs=[
                pl.BlockSpec((tile, cols), lambda i, j, pos: ((2 * j + pos[0]) * steps + i, 0)),
                pl.BlockSpec((tile, cols), lambda i, j, pos: (j * steps + i, 0)),
            ],
            out_specs=[
                pl.BlockSpec((tile, cols), lambda i, j, pos: (j * steps + i, 0)),
                pl.BlockSpec((tile, cols), lambda i, j, pos: (i, 0)),
            ],
        ),
        out_shape=[jax.ShapeDtypeStruct((N_CHIPS * half, cols), BF16), jax.ShapeDtypeStruct((half, cols), F32)],
        compiler_params=_params(("parallel", "arbitrary")),
    )(pos, grad, theirs)


def _chip_sum(name, own, landed, pos):
    half, cols = own.shape
    tile = _row_tile(half)
    steps = half // tile

    def body(pos_ref, own_ref, l0, l1, l2, o_ref):
        o_ref[...] = ((own_ref[...] + l0[...].astype(F32)) + l1[...].astype(F32)) + l2[...].astype(F32)

    landed_specs = [pl.BlockSpec((tile, cols), lambda i, pos, _k=k: (_k * steps + i, 0)) for k in range(N_CHIPS - 1)]
    return pl.pallas_call(
        body,
        name=name,
        grid_spec=pltpu.PrefetchScalarGridSpec(
            num_scalar_prefetch=1,
            grid=(steps,),
            in_specs=[pl.BlockSpec((tile, cols), lambda i, pos: (i, 0))] + landed_specs,
            out_specs=pl.BlockSpec((tile, cols), lambda i, pos: (pos[0] * steps + i, 0)),
        ),
        out_shape=jax.ShapeDtypeStruct((2 * half, cols), F32),
        compiler_params=_params(("parallel",)),
    )(pos, own, landed, landed, landed)


def _adamw(name, w, g, m, v):
    rows, cols = w.shape
    tile = rows if rows * cols <= 256 * 1024 else _row_tile(rows)

    def body(w_ref, g_ref, m_ref, v_ref, g_out_ref, d_ref, nm_ref, nv_ref):
        g = g_ref[...]
        g_out_ref[...] = g
        nm = ADAM_B1 * m_ref[...] + (1.0 - ADAM_B1) * g
        nv = ADAM_B2 * v_ref[...] + (1.0 - ADAM_B2) * (g * g)
        m_hat = nm / (1.0 - ADAM_B1**ADAM_STEP)
        v_hat = nv / (1.0 - ADAM_B2**ADAM_STEP)
        d_ref[...] = -ADAM_LR * (m_hat / (jnp.sqrt(v_hat) + ADAM_EPS) + ADAM_WD * w_ref[...])
        nm_ref[...] = nm
        nv_ref[...] = nv

    spec = _row_spec(tile, cols)
    return pl.pallas_call(
        body,
        name=name,
        grid=(rows // tile,),
        in_specs=[spec] * 4,
        out_specs=[spec] * 4,
        out_shape=[jax.ShapeDtypeStruct((rows, cols), F32)] * 4,
        compiler_params=_params(("parallel",)),
    )(w, g, m, v)


_SMALL = (
    ("v_ln_g", (D_GMLP,), 8),
    ("v_ln_b", (D_GMLP,), 8),
    ("w_spatial", (N_HEADS, CHUNK, CHUNK), 1024),
    ("b_spatial", (N_HEADS, CHUNK), 8),
    ("sinks", (N_HEADS,), 8),
    ("ln1_g", (D_MODEL,), 8),
    ("ln1_b", (D_MODEL,), 8),
    ("ln2_g", (D_MODEL,), 8),
    ("ln2_b", (D_MODEL,), 8),
    ("squared_error", (D_MODEL,), 8),
)
N_SMALL_PARAMS = len(_SMALL) - 1


def _pack_small(values):
    parts = []
    for (name, shape, rows), val in zip(_SMALL, values, strict=True):
        flat = val.reshape(-1).astype(F32)
        parts.append(jnp.pad(flat, (0, rows * LANES - flat.shape[0])).reshape(rows, LANES))
    parts.append(jnp.zeros((SMALL_ROWS - sum(rows for _, _, rows in _SMALL), LANES), F32))
    return jnp.concatenate(parts, axis=0)


def _adamw_update(w, g, m, v):
    nm = ADAM_B1 * m + (1.0 - ADAM_B1) * g
    nv = ADAM_B2 * v + (1.0 - ADAM_B2) * (g * g)
    m_hat = nm / (1.0 - ADAM_B1**ADAM_STEP)
    v_hat = nv / (1.0 - ADAM_B2**ADAM_STEP)
    return -ADAM_LR * (m_hat / (jnp.sqrt(v_hat) + ADAM_EPS) + ADAM_WD * w), nm, nv


def _adamw_small(g_slab, params, first, second):
    n = N_SMALL_PARAMS

    def pieces(shape):
        if len(shape) == 3:
            return [((0, h), h * shape[1], shape[1], shape[2]) for h in range(shape[0])]
        if len(shape) == 2:
            return [((0,), 0, shape[0], shape[1])]
        if shape[0] >= LANES:
            return [((slice(None), slice(r * LANES, (r + 1) * LANES)), r, 1, LANES) for r in range(shape[0] // LANES)]
        return [((slice(None), slice(0, shape[0])), 0, 1, shape[0])]

    def body(*refs):
        g_ref = refs[0]
        w_refs, m_refs, v_refs = refs[1 : 1 + n], refs[1 + n : 1 + 2 * n], refs[1 + 2 * n : 1 + 3 * n]
        outs = refs[1 + 3 * n :]
        row0 = 0
        for idx, (_, shape, rows) in enumerate(_SMALL[:n]):
            for where, first_row, n_rows, lanes in pieces(shape):
                g = g_ref[row0 + first_row : row0 + first_row + n_rows, 0:lanes]
                delta, nm, nv = _adamw_update(w_refs[idx][where], g, m_refs[idx][where], v_refs[idx][where])
                for group, val in enumerate((g, delta, nm, nv)):
                    outs[group * n + idx][where] = val
            row0 += rows

    vmem = pl.BlockSpec(memory_space=pltpu.VMEM)
    shapes = [jax.ShapeDtypeStruct(p.shape, F32) for p in params]
    outs = pl.pallas_call(
        body,
        name="adamw_small",
        in_specs=[vmem] * (1 + 3 * n),
        out_specs=[vmem] * (4 * n),
        out_shape=shapes * 4,
        compiler_params=_params(),
    )(g_slab, *params, *first, *second)
    return [list(outs[group * n : (group + 1) * n]) for group in range(4)]


def kernel(x, positions, w_in, v_ln_g, v_ln_b, w_spatial, b_spatial, sinks, w_out, ln1_g, ln1_b, w_ff1, w_ff2, ln2_g, ln2_b, loss_target, m_w_in, m_v_ln_g, m_v_ln_b, m_w_spatial, m_b_spatial, m_sinks, m_w_out, m_ln1_g, m_ln1_b, m_w_ff1, m_w_ff2, m_ln2_g, m_ln2_b, v_w_in, v_v_ln_g, v_v_ln_b, v_w_spatial, v_b_spatial, v_sinks, v_w_out, v_ln1_g, v_ln1_b, v_w_ff1, v_w_ff2, v_ln2_g, v_ln2_b):
    t = x.shape[1]
    x2 = x.reshape(t, D_MODEL)
    target = loss_target.reshape(t, D_MODEL)

    inv_freq = ROPE_THETA ** (-jnp.arange(0, HEAD_DIM, 2, dtype=F32) / HEAD_DIM)
    cos, sin, w_in_t, later = _gather_w_in_and_rope_tables(
        w_in[0].T.astype(BF16), positions, jnp.tile(inv_freq, LANES // (HEAD_DIM // 2)).reshape(1, LANES),
        [w_out[0], w_ff1[0], w_ff2[0]])
    later_rows = [s.shape[0] for s in later]
    direct_plan, pass_plan, diagonal_plan = _gather_plans(later_rows)
    bufs, started, direct_send, direct_recv = _split_call(
        "gather_start", later + [_landing(N_CHIPS * r, D_MODEL, BF16) for r in later_rows], start=direct_plan, after=w_in_t)

    u, vg, q, k, va = _in_proj(x2, w_in_t, cos, sin, dep=started)
    bias_full = jnp.repeat(b_spatial[0].T, HEAD_DIM, axis=1)
    sink_vec = sinks.reshape(N_HEADS)
    bufs, passing, pass_send, pass_recv = _split_call(
        "gather_pass", bufs, wait=(direct_plan, direct_send, direct_recv), start=pass_plan, after=u)
    cat = _mixer_fwd(u, vg, q, k, va, v_ln_g, v_ln_b, w_spatial[0], bias_full, sink_vec, dep=passing)
    bufs, passing, diag_send, diag_recv = _split_call(
        "gather_pass_diagonal", bufs, wait=(pass_plan, pass_send, pass_recv), start=diagonal_plan, after=cat)
    bufs, _ = _split_call("gather_end", bufs, wait=(diagonal_plan, diag_send, diag_recv), after=passing)
    w_out_all = bufs[3]
    w1_all = bufs[4].reshape(N_FF_BLOCKS, D_MODEL, D_MODEL)
    w2_all = bufs[5].reshape(N_FF_BLOCKS, D_MODEL, D_MODEL)
    xhat1, rstd1, x1b, r, dz2, dz2b, d_ln2_g, d_ln2_b, sq_err = _ffn_fwd_loss(
        cat, x2, w_out_all, ln1_g, ln1_b, w1_all, w2_all, ln2_g, ln2_b, target)

    pos = jnp.stack([lax.axis_index("c"), 2 * lax.axis_index("x") + lax.axis_index("y")]).astype(jnp.int32)
    half_landing = lambda g: _landing(g.shape[0] // 2, D_MODEL, F32)
    g_ff2_local = _grad_w_ff2(r, dz2b)
    swap_plan = _swap_plan([D_FF // N_CHIPS])
    ff2_bufs, swapping2, swap2_send, swap2_recv = _split_call("ff2_swap_start", [g_ff2_local, half_landing(g_ff2_local)], start=swap_plan)
    dpre, dz1, dz1b, dcat, d_ln1_g, d_ln1_b = _ffn_bwd_ln1(dz2, r, xhat1, rstd1, ln1_g, w1_all, w2_all, w_out_all, dep=swapping2)
    g_ff1_local = _grad_w_ff1(x1b, dpre)
    ff1_bufs, swapping1, swap1_send, swap1_recv = _split_call("ff1_swap_start", [g_ff1_local, half_landing(g_ff1_local)], start=swap_plan)
    g_out_local = _grad_w_out(cat, dz1b, dep=swapping1)
    ff2_bufs, swapped2 = _split_call("ff2_swap_wait", ff2_bufs, wait=(swap_plan, swap2_send, swap2_recv), after=g_out_local)
    ff1_bufs, _ = _split_call("ff1_swap_wait", ff1_bufs, wait=(swap_plan, swap1_send, swap1_recv), after=swapped2)
    ff_sums = [_pair_sum("grad_pair_sum_w_ff1", ff1_bufs[0], ff1_bufs[1], pos), _pair_sum("grad_pair_sum_w_ff2", ff2_bufs[0], ff2_bufs[1], pos)]
    ff_halves = [p.shape[0] // N_CHIPS for p, _ in ff_sums]
    exchange_plan = _exchange_plan(ff_halves)
    bufs, exchanging, ex_send, ex_recv = _split_call(
        "ff_exchange_start", [p for p, _ in ff_sums] + [_landing(3 * h, D_MODEL, BF16) for h in ff_halves], start=exchange_plan)
    dh_main, dkv, d_v_ln_g, d_v_ln_b, d_w_spatial, d_b_spatial_t, d_sinks = _mixer_bwd(
        u, vg, q, k, va, dcat, cos, sin, v_ln_g, v_ln_b, w_spatial[0], bias_full, sink_vec, dep=exchanging)
    g_in_local, small_g = _grad_w_in_t_and_small_all_reduce(dh_main, dkv, x2, _pack_small(
        [d_v_ln_g, d_v_ln_b, d_w_spatial, d_b_spatial_t[:, :N_HEADS].T, d_sinks[0, :N_HEADS], d_ln1_g, d_ln1_b, d_ln2_g, d_ln2_b, sq_err]))
    sq_row = sum(rows for _, _, rows in _SMALL[:N_SMALL_PARAMS])
    loss = 0.5 * jnp.sum(small_g[sq_row : sq_row + _SMALL[N_SMALL_PARAMS][2]]) / D_MODEL

    small = [g_in_local, g_out_local]
    small_swap_plan = _swap_plan([g.shape[0] // N_CHIPS for g in small])
    swap_bufs, small_swapping, ss_send, ss_recv = _split_call(
        "small_swap_start", small + [half_landing(g) for g in small], start=small_swap_plan)
    grad_x_flat = _grad_x(dh_main, dkv, dz1, w_in_t, dep=small_swapping)
    grad_x = grad_x_flat.reshape(1, t, D_MODEL)
    swap_bufs, _ = _split_call("small_swap_wait", swap_bufs, wait=(small_swap_plan, ss_send, ss_recv), after=grad_x_flat)
    pair_sums = [_pair_sum("grad_pair_sum_" + nm, g, th, pos) for nm, g, th in zip(["w_in", "w_out"], swap_bufs[:2], swap_bufs[2:])]
    small_halves = [p.shape[0] // N_CHIPS for p, _ in pair_sums]
    small_plan = _exchange_plan(small_halves)
    small_bufs, small_exchanging, sm_send, sm_recv = _split_call(
        "small_exchange_start", [p for p, _ in pair_sums] + [_landing(3 * h, D_MODEL, BF16) for h in small_halves], start=small_plan)

    bufs, _ = _split_call("ff_exchange_wait", bufs, wait=(exchange_plan, ex_send, ex_recv), after=small_exchanging)
    ff_shards = [_chip_sum("grad_chip_sum_" + nm, own, ld, pos) for nm, (_, own), ld in zip(["w_ff1", "w_ff2"], ff_sums, bufs[2:])]
    g_w_ff1, g_w_ff2 = _pair_gather("grad_pair_gather_ff", ff_shards)

    g_w_ff1, d_w_ff1, nm_w_ff1, nv_w_ff1 = _adamw("adamw_w_ff1", w_ff1[0], g_w_ff1, m_w_ff1[0], v_w_ff1[0])
    g_w_ff2, d_w_ff2, nm_w_ff2, nv_w_ff2 = _adamw("adamw_w_ff2", w_ff2[0], g_w_ff2, m_w_ff2[0], v_w_ff2[0])
    small_bufs, _ = _split_call("small_exchange_wait", small_bufs, wait=(small_plan, sm_send, sm_recv), after=nv_w_ff2)
    shards = [_chip_sum("grad_chip_sum_" + nm, own, ld, pos) for nm, (_, own), ld in zip(["w_in", "w_out"], pair_sums, small_bufs[2:])]
    g_w_in_t, g_w_out = _pair_gather("grad_pair_gather_small", shards)
    g_w_in, d_w_in, nm_w_in, nv_w_in = (a.T for a in _adamw("adamw_w_in", w_in[0].T, g_w_in_t, m_w_in[0].T, v_w_in[0].T))
    g_w_out, d_w_out, nm_w_out, nv_w_out = _adamw("adamw_w_out", w_out[0], g_w_out, m_w_out[0], v_w_out[0])
    small_grads, small_d, small_nm, small_nv = _adamw_small(
        small_g,
        [v_ln_g, v_ln_b, w_spatial, b_spatial, sinks, ln1_g, ln1_b, ln2_g, ln2_b],
        [m_v_ln_g, m_v_ln_b, m_w_spatial, m_b_spatial, m_sinks, m_ln1_g, m_ln1_b, m_ln2_g, m_ln2_b],
        [v_v_ln_g, v_v_ln_b, v_w_spatial, v_b_spatial, v_sinks, v_ln1_g, v_ln1_b, v_ln2_g, v_ln2_b])

    def with_big(small, w_in_v, w_out_v, w_ff1_v, w_ff2_v):
        g_vg, g_vb, g_ws, g_bs, g_sk, g_1g, g_1b, g_2g, g_2b = small
        return [w_in_v[None], g_vg, g_vb, g_ws, g_bs, g_sk, w_out_v[None], g_1g, g_1b, w_ff1_v[None], w_ff2_v[None], g_2g, g_2b]

    return (
        loss,
        grad_x,
        *with_big(small_grads, g_w_in, g_w_out, g_w_ff1, g_w_ff2),
        *with_big(small_d, d_w_in, d_w_out, d_w_ff1, d_w_ff2),
        *with_big(small_nm, nm_w_in, nm_w_out, nm_w_ff1, nm_w_ff2),
        *with_big(small_nv, nv_w_in, nv_w_out, nv_w_ff1, nv_w_ff2),
    )
```

```python
import math

import jax
import jax.numpy as jnp
from jax import lax
from jax.experimental import pallas as pl
from jax.experimental.pallas import tpu as pltpu

F32 = jnp.float32
BF16 = jnp.bfloat16

D_MODEL = 1024
HEAD_DIM = 64
D_GMLP = 512
D_ATTN = 512
D_KV = 128
D_IN = 2 * D_GMLP + D_ATTN + 2 * D_KV
D_MAIN = 2 * D_GMLP + D_ATTN
N_HEADS = 8
CHUNK = 128
CHUNKS_PER_STEP = 4
ROPE_THETA = 10000.0
D_FF = 4 * D_MODEL
N_FF_BLOCKS = 4
LN_EPS = 1e-5
ALPHA = (2.0 * 1) ** 0.25
NEG_INF = -1e30
SCALE = 1.0 / math.sqrt(HEAD_DIM)

ADAM_LR = 0.001
ADAM_B1 = 0.9
ADAM_B2 = 0.999
ADAM_EPS = 1e-08
ADAM_WD = 0.01
ADAM_STEP = 10

N_CHIPS = 4
LANES = 128
V7X_VMEM_BYTES = 64 * 1024 * 1024
VMEM_LIMIT = V7X_VMEM_BYTES - 8 * 1024 * 1024
TM = 512
TM_FFN = 256
TK = 1024
TK_FF = 1024
SMALL_ROWS = 1152
MESH = pl.DeviceIdType.MESH

NT = (((1,), (1,)), ((), ()))
TN = (((0,), (0,)), ((), ()))


def _dot(a, b, dims=None):
    if dims is None:
        return jnp.dot(a, b, preferred_element_type=F32)
    return lax.dot_general(a, b, dims, preferred_element_type=F32)


def _params(semantics=None):
    return pltpu.CompilerParams(dimension_semantics=semantics, vmem_limit_bytes=VMEM_LIMIT)


def _const_spec(shape, single_buffer=False):
    zeros = (0,) * len(shape)
    if single_buffer:
        return pl.BlockSpec(shape, lambda *_: zeros, pipeline_mode=pl.Buffered(1))
    return pl.BlockSpec(shape, lambda *_: zeros)


def _row_spec(rows, cols):
    return pl.BlockSpec((rows, cols), lambda i: (i, 0))


def _after(dep, body, in_specs, operands):
    if dep is None:
        return body, list(in_specs), list(operands)
    return (lambda dep_ref, *refs: body(*refs)), [pl.BlockSpec(memory_space=pl.ANY)] + list(in_specs), [dep] + list(operands)


def _gelu(x):
    k = math.sqrt(2.0 / math.pi)
    return 0.5 * x * (1.0 + jnp.tanh(k * (x + 0.044715 * (x * x * x))))


def _gelu_and_grad(x):
    k = math.sqrt(2.0 / math.pi)
    x2 = x * x
    t = jnp.tanh(k * (x + 0.044715 * (x2 * x)))
    g = 0.5 * x * (1.0 + t)
    dg = 0.5 * (1.0 + t) + 0.5 * x * (1.0 - t * t) * (k * (1.0 + 3.0 * 0.044715 * x2))
    return g, dg


def _layer_norm_stats(z):
    mu = jnp.mean(z, axis=-1, keepdims=True)
    zc = z - mu
    var = jnp.mean(zc * zc, axis=-1, keepdims=True)
    rstd = lax.rsqrt(var + LN_EPS)
    return zc * rstd, rstd


def _layer_norm_bwd(dxhat, xhat, rstd):
    m1 = jnp.mean(dxhat, axis=-1, keepdims=True)
    m2 = jnp.mean(dxhat * xhat, axis=-1, keepdims=True)
    return rstd * (dxhat - m1 - xhat * m2)


def _rotate_half(t):
    n = t.shape[1]
    lane = lax.broadcasted_iota(jnp.int32, t.shape, 1)
    first = (lane & (HEAD_DIM // 2)) == 0
    return jnp.where(first, -pltpu.roll(t, n - HEAD_DIM // 2, 1), pltpu.roll(t, HEAD_DIM // 2, 1))


def _rope(t, cos, sin):
    return t * cos + _rotate_half(t) * sin


def _rope_transposed(g, cos, sin):
    return g * cos - _rotate_half(g * sin)


def _lane_tile(a, reps):
    return jnp.tile(a, (1, reps)) if reps > 1 else a


def _in_proj(x, w_in_t, cos, sin, dep=None):
    t = x.shape[0]

    def body(x_ref, w_ref, cos_ref, sin_ref, u_ref, vg_ref, q_ref, k_ref, va_ref):
        xb = x_ref[...].astype(BF16)
        u_ref[...] = _dot(xb, w_ref[0:D_GMLP, :], NT)
        vg_ref[...] = _dot(xb, w_ref[D_GMLP : 2 * D_GMLP, :], NT)
        q = _dot(xb, w_ref[2 * D_GMLP : D_MAIN, :], NT)
        k = _dot(xb, w_ref[D_MAIN : D_MAIN + D_KV, :], NT)
        va_ref[...] = _dot(xb, w_ref[D_MAIN + D_KV : D_IN, :], NT).astype(BF16)
        c, s = cos_ref[...], sin_ref[...]
        q_ref[...] = _rope(q, _lane_tile(c, D_ATTN // LANES), _lane_tile(s, D_ATTN // LANES)).astype(BF16)
        k_ref[...] = _rope(k, c, s).astype(BF16)

    body, in_specs, operands = _after(
        dep, body, [_row_spec(TM, D_MODEL), _const_spec((D_IN, D_MODEL)), _row_spec(TM, LANES), _row_spec(TM, LANES)], [x, w_in_t, cos, sin])
    return pl.pallas_call(
        body,
        name="in_proj",
        grid=(t // TM,),
        in_specs=in_specs,
        out_specs=[_row_spec(TM, D_GMLP), _row_spec(TM, D_GMLP), _row_spec(TM, D_ATTN), _row_spec(TM, D_KV), _row_spec(TM, D_KV)],
        out_shape=[
            jax.ShapeDtypeStruct((t, D_GMLP), F32),
            jax.ShapeDtypeStruct((t, D_GMLP), F32),
            jax.ShapeDtypeStruct((t, D_ATTN), BF16),
            jax.ShapeDtypeStruct((t, D_KV), BF16),
            jax.ShapeDtypeStruct((t, D_KV), BF16),
        ],
        compiler_params=_params(("parallel",)),
    )(*operands)


def _step_rows(i):
    return (i, 0)


def _chunk_before_step(i):
    return (jnp.maximum(CHUNKS_PER_STEP * i - 1, 0), 0)


def _chunk_specs():
    step = CHUNKS_PER_STEP * CHUNK
    return [
        pl.BlockSpec((step, D_GMLP), _step_rows),
        pl.BlockSpec((step, D_GMLP), _step_rows),
        pl.BlockSpec((step, D_ATTN), _step_rows),
        pl.BlockSpec((step, D_KV), _step_rows),
        pl.BlockSpec((CHUNK, D_KV), _chunk_before_step),
        pl.BlockSpec((step, D_KV), _step_rows),
        pl.BlockSpec((CHUNK, D_KV), _chunk_before_step),
    ]


def _half_lane_masks(rows):
    lane = lax.broadcasted_iota(jnp.int32, (rows, LANES), 1)
    return lane < HEAD_DIM


def _kv_variants(kv2):
    left = _half_lane_masks(kv2.shape[0])
    f = kv2.astype(F32)
    swapped = pltpu.roll(f, HEAD_DIM, 1)
    zero = jnp.zeros_like(f)
    g0 = (jnp.where(left, f, zero).astype(BF16), jnp.where(left, zero, swapped).astype(BF16))
    g1 = (jnp.where(left, swapped, zero).astype(BF16), jnp.where(left, zero, f).astype(BF16))
    return (g0, g1)


def _band_mask(i, heads=1):
    row = lax.broadcasted_iota(jnp.int32, (heads * CHUNK, 2 * CHUNK), 0) & (CHUNK - 1)
    col = lax.broadcasted_iota(jnp.int32, (heads * CHUNK, 2 * CHUNK), 1)
    no_prev = jnp.where(i > 0, 0, 4 * CHUNK)
    in_prev = jnp.logical_and(col < CHUNK, (col - row) > no_prev)
    in_cur = jnp.logical_and(col >= CHUNK, (col - CHUNK) <= row)
    return jnp.logical_or(in_prev, in_cur)


def _causal_mask():
    row = lax.broadcasted_iota(jnp.int32, (CHUNK, CHUNK), 0)
    col = lax.broadcasted_iota(jnp.int32, (CHUNK, CHUNK), 1)
    return col <= row


def _store_spatial_weights(w_ref, wcat_ref, wcat_t_ref=None):
    causal = _causal_mask()
    for p in range(D_GMLP // LANES):
        wl = jnp.where(causal, w_ref[2 * p], 0.0)
        wr = jnp.where(causal, w_ref[2 * p + 1], 0.0)
        wcat_ref[p] = jnp.concatenate([wl, wr], axis=1).astype(BF16)
        if wcat_t_ref is not None:
            wcat_t_ref[p] = jnp.concatenate([wl.T, wr.T], axis=1).astype(BF16)


def _pair_stack(xp, left):
    return jnp.concatenate([jnp.where(left, xp, 0.0), jnp.where(left, 0.0, xp)], axis=0).astype(BF16)


def _mixer_fwd(u, vg, q, k, va, v_ln_g, v_ln_b, w_spatial, bias_full, sinks, dep=None):
    t = u.shape[0]

    def body(u_ref, vg_ref, q_ref, kc_ref, kp_ref, vc_ref, vp_ref, g_ref, b_ref, w_ref, bias_ref, sink_ref, cat_ref, wcat):
        i = pl.program_id(0)
        left = _half_lane_masks(CHUNK)

        @pl.when(i == 0)
        def _():
            _store_spatial_weights(w_ref, wcat)

        heads = range(N_HEADS)
        pair_cols = [slice(p * LANES, (p + 1) * LANES) for p in range(D_GMLP // LANES)]
        sinks_h = [sink_ref[h] for h in heads]
        for c in range(CHUNKS_PER_STEP):
            rows = slice(c * CHUNK, (c + 1) * CHUNK)
            before = slice((c - 1) * CHUNK, c * CHUNK)
            k_prev = kp_ref[...] if c == 0 else kc_ref[before, :]
            v_prev = vp_ref[...] if c == 0 else vc_ref[before, :]
            k_var = _kv_variants(jnp.concatenate([k_prev, kc_ref[rows, :]], axis=0))
            v_var = _kv_variants(jnp.concatenate([v_prev, vc_ref[rows, :]], axis=0))
            scores = [_dot(q_ref[rows, pair_cols[h // 2]], k_var[h // 4][h % 2], NT) for h in heads]

            ug = _gelu(u_ref[rows, :])
            xhat, _ = _layer_norm_stats(_gelu(vg_ref[rows, :]))
            vgl = xhat * g_ref[...] + b_ref[...]
            mixed = [_dot(wcat[p], _pair_stack(vgl[:, cols], left)) for p, cols in enumerate(pair_cols)]

            valid = _band_mask(CHUNKS_PER_STEP * i + c)
            masked = [jnp.where(valid, scores[h] * SCALE, NEG_INF) for h in heads]
            maxes = [jnp.maximum(jnp.max(masked[h], axis=1, keepdims=True), sinks_h[h]) for h in heads]
            exps = [jnp.exp(masked[h] - maxes[h]) for h in heads]
            invs = [1.0 / (jnp.sum(exps[h], axis=1, keepdims=True) + jnp.exp(sinks_h[h] - maxes[h])) for h in heads]
            probs = [(exps[h] * invs[h]).astype(BF16) for h in heads]
            for p, cols in enumerate(pair_cols):
                cat_ref[rows, cols] = (ug[:, cols] * (mixed[p] + bias_ref[:, cols])).astype(BF16)
            for p in range(D_ATTN // LANES):
                out = _dot(probs[2 * p], v_var[p // 2][0]) + _dot(probs[2 * p + 1], v_var[p // 2][1])
                cat_ref[rows, D_GMLP + p * LANES : D_GMLP + (p + 1) * LANES] = out.astype(BF16)

    in_specs = _chunk_specs() + [
        _const_spec((1, D_GMLP)),
        _const_spec((1, D_GMLP)),
        _const_spec((N_HEADS, CHUNK, CHUNK)),
        _const_spec((CHUNK, D_GMLP)),
        pl.BlockSpec(memory_space=pltpu.SMEM),
    ]
    body, in_specs, operands = _after(dep, body, in_specs, [u, vg, q, k, k, va, va, v_ln_g, v_ln_b, w_spatial, bias_full, sinks])
    return pl.pallas_call(
        body,
        name="mixer_fwd",
        grid=(t // (CHUNKS_PER_STEP * CHUNK),),
        in_specs=in_specs,
        out_specs=pl.BlockSpec((CHUNKS_PER_STEP * CHUNK, D_MODEL), lambda i: (i, 0)),
        out_shape=jax.ShapeDtypeStruct((t, D_MODEL), BF16),
        scratch_shapes=[pltpu.VMEM((D_GMLP // LANES, CHUNK, 2 * CHUNK), BF16)],
        compiler_params=_params(("arbitrary",)),
    )(*operands)


def _ffn_fwd_loss(cat, x, w_out, ln1_g, ln1_b, w1, w2, ln2_g, ln2_b, target):
    t = x.shape[0]

    def body(cat_ref, x_ref, wo_ref, g1_ref, b1_ref, w1_ref, w2_ref, g2_ref, b2_ref, tgt_ref,
             xh_ref, rstd_ref, x1b_ref, r_ref, dz2_ref, dz2b_ref, dg2_ref, db2_ref, sq_ref):
        @pl.when(pl.program_id(0) == 0)
        def _():
            dg2_ref[...] = jnp.zeros_like(dg2_ref)
            db2_ref[...] = jnp.zeros_like(db2_ref)
            sq_ref[...] = jnp.zeros_like(sq_ref)

        xhat1, rstd1 = _layer_norm_stats(ALPHA * x_ref[...] + _dot(cat_ref[...], wo_ref[...]))
        xh_ref[...] = xhat1
        rstd_ref[...] = rstd1
        x1 = xhat1 * g1_ref[...] + b1_ref[...]
        x1b = x1.astype(BF16)
        x1b_ref[...] = x1b
        ff = jnp.zeros((TM_FFN, D_MODEL), F32)
        for j in range(N_FF_BLOCKS):
            r = jnp.maximum(_dot(x1b, w1_ref[j]), 0.0)
            r_ref[:, j * D_MODEL : (j + 1) * D_MODEL] = r.astype(BF16)
            ff = ff + _dot((r * r).astype(BF16), w2_ref[j])
        xhat2, rstd2 = _layer_norm_stats(ALPHA * x1 + ff)
        err = xhat2 * g2_ref[...] + b2_ref[...] - tgt_ref[...]
        sq_ref[...] += jnp.sum(err * err, axis=0, keepdims=True)
        dy = err * (1.0 / D_MODEL)
        dg2_ref[...] += jnp.sum(dy * xhat2, axis=0, keepdims=True)
        db2_ref[...] += jnp.sum(dy, axis=0, keepdims=True)
        dz2 = _layer_norm_bwd(dy * g2_ref[...], xhat2, rstd2)
        dz2_ref[...] = dz2
        dz2b_ref[...] = dz2.astype(BF16)

    vec = _const_spec((1, D_MODEL))
    tile = _row_spec(TM_FFN, D_MODEL)
    wspec = _const_spec((N_FF_BLOCKS, D_MODEL, D_MODEL), single_buffer=True)
    return pl.pallas_call(
        body,
        name="ffn_fwd_loss",
        grid=(t // TM_FFN,),
        in_specs=[tile, tile, _const_spec((D_MODEL, D_MODEL), single_buffer=True), vec, vec, wspec, wspec, vec, vec, tile],
        out_specs=[tile, _row_spec(TM_FFN, 1), tile, _row_spec(TM_FFN, D_FF), tile, tile, vec, vec, vec],
        out_shape=[
            jax.ShapeDtypeStruct((t, D_MODEL), F32),
            jax.ShapeDtypeStruct((t, 1), F32),
            jax.ShapeDtypeStruct((t, D_MODEL), BF16),
            jax.ShapeDtypeStruct((t, D_FF), BF16),
            jax.ShapeDtypeStruct((t, D_MODEL), F32),
            jax.ShapeDtypeStruct((t, D_MODEL), BF16),
            jax.ShapeDtypeStruct((1, D_MODEL), F32),
            jax.ShapeDtypeStruct((1, D_MODEL), F32),
            jax.ShapeDtypeStruct((1, D_MODEL), F32),
        ],
        compiler_params=_params(("arbitrary",)),
    )(cat, x, w_out, ln1_g, ln1_b, w1, w2, ln2_g, ln2_b, target)


def _ffn_bwd_ln1(dz2, r, x1b, xhat1, rstd1, ln1_g, w1, w2, w_out, dep=None):
    t = dz2.shape[0]

    def body(dz2_ref, r_ref, x1b_ref, xh_ref, rstd_ref, g1_ref, w1_ref, w2_ref, wo_ref, gw1_ref, dz1_ref, dz1b_ref, dcat_ref, dg1_ref, db1_ref):
        @pl.when(pl.program_id(0) == 0)
        def _():
            dg1_ref[...] = jnp.zeros_like(dg1_ref)
            db1_ref[...] = jnp.zeros_like(db1_ref)
            gw1_ref[...] = jnp.zeros_like(gw1_ref)

        dz2 = dz2_ref[...]
        dz2b = dz2.astype(BF16)
        x1_t = x1b_ref[...].astype(F32).T.astype(BF16)
        dx1 = ALPHA * dz2
        for j in range(N_FF_BLOCKS):
            cols = slice(j * D_MODEL, (j + 1) * D_MODEL)
            dpre = (_dot(dz2b, w2_ref[j], NT) * (2.0 * r_ref[:, cols].astype(F32))).astype(BF16)
            gw1_ref[cols, :] += _dot(x1_t, dpre)
            dx1 = dx1 + _dot(dpre, w1_ref[j], NT)
        xhat1 = xh_ref[...]
        dg1_ref[...] += jnp.sum(dx1 * xhat1, axis=0, keepdims=True)
        db1_ref[...] += jnp.sum(dx1, axis=0, keepdims=True)
        dz1 = _layer_norm_bwd(dx1 * g1_ref[...], xhat1, rstd_ref[...])
        dz1_ref[...] = dz1
        dz1b = dz1.astype(BF16)
        dz1b_ref[...] = dz1b
        dcat_ref[...] = _dot(dz1b, wo_ref[...], NT).astype(BF16)

    vec = _const_spec((1, D_MODEL))
    tile = _row_spec(TM_FFN, D_MODEL)
    wspec = _const_spec((N_FF_BLOCKS, D_MODEL, D_MODEL), single_buffer=True)
    body, in_specs, operands = _after(
        dep, body,
        [tile, _row_spec(TM_FFN, D_FF), tile, tile, _row_spec(TM_FFN, 1), vec, wspec, wspec, _const_spec((D_MODEL, D_MODEL), single_buffer=True)],
        [dz2, r, x1b, xhat1, rstd1, ln1_g, w1, w2, w_out])
    return pl.pallas_call(
        body,
        name="ffn_bwd_ln1",
        grid=(t // TM_FFN,),
        in_specs=in_specs,
        out_specs=[_const_spec((D_FF, D_MODEL), single_buffer=True), tile, tile, tile, vec, vec],
        out_shape=[
            jax.ShapeDtypeStruct((D_FF, D_MODEL), F32),
            jax.ShapeDtypeStruct((t, D_MODEL), F32),
            jax.ShapeDtypeStruct((t, D_MODEL), BF16),
            jax.ShapeDtypeStruct((t, D_MODEL), BF16),
            jax.ShapeDtypeStruct((1, D_MODEL), F32),
            jax.ShapeDtypeStruct((1, D_MODEL), F32),
        ],
        compiler_params=_params(("arbitrary",)),
    )(*operands)


def _mixer_bwd(u, vg, q, k, va, dcat, cos, sin, v_ln_g, v_ln_b, w_spatial, bias_full, sinks, dep=None):
    t = u.shape[0]
    n_chunks = t // CHUNK

    def body(u_ref, vg_ref, q_ref, kc_ref, kp_ref, vc_ref, vp_ref, dcat_ref, cosc_ref, sinc_ref, cosp_ref, sinp_ref,
             g_ref, b_ref, w_ref, bias_ref, sink_ref,
             dmain_ref, dkv_ref, dg_ref, db_ref, dw_ref, dbs_ref, dsink_ref, dmix_acc, wcat, wcat_t):
        i = pl.program_id(0)
        left = _half_lane_masks(CHUNK)
        lane = lax.broadcasted_iota(jnp.int32, (CHUNK, LANES), 1)
        n_pairs = D_GMLP // LANES

        @pl.when(i == 0)
        def _():
            dg_ref[...] = jnp.zeros_like(dg_ref)
            db_ref[...] = jnp.zeros_like(db_ref)
            dw_ref[...] = jnp.zeros_like(dw_ref)
            dsink_ref[...] = jnp.zeros_like(dsink_ref)
            dmix_acc[...] = jnp.zeros_like(dmix_acc)
            _store_spatial_weights(w_ref, wcat, wcat_t)

        n_qpairs = D_ATTN // LANES
        heads = range(N_HEADS)
        pair_cols = [slice(p * LANES, (p + 1) * LANES) for p in range(n_pairs)]
        sinks_h = [sink_ref[h] for h in heads]
        gain = g_ref[...]
        causal = _causal_mask()
        lane_row = lax.broadcasted_iota(jnp.int32, (1, LANES), 1)
        heads_per_group = N_HEADS // 2

        def group_grad_t(lhs_t, rhs_heads):
            parts = []
            for g in range(2):
                group = range(g * heads_per_group, (g + 1) * heads_per_group)
                lhs = jnp.concatenate([lhs_t[h * HEAD_DIM : (h + 1) * HEAD_DIM] for h in group], axis=1)
                parts.append(_dot(lhs, jnp.concatenate([rhs_heads[h] for h in group], axis=0)))
            return jnp.concatenate(parts, axis=0)

        for c in range(CHUNKS_PER_STEP):
            chunk = CHUNKS_PER_STEP * i + c
            rows = slice(c * CHUNK, (c + 1) * CHUNK)
            before = slice((c - 1) * CHUNK, c * CHUNK)

            k_prev = kp_ref[...] if c == 0 else kc_ref[before, :]
            v_prev = vp_ref[...] if c == 0 else vc_ref[before, :]
            k_var = _kv_variants(jnp.concatenate([k_prev, kc_ref[rows, :]], axis=0))
            v_var = _kv_variants(jnp.concatenate([v_prev, vc_ref[rows, :]], axis=0))
            q_pairs = [q_ref[rows, cols] for cols in pair_cols]
            do_all = dcat_ref[rows, D_GMLP:D_MODEL]
            do_pairs = [do_all[:, cols] for cols in pair_cols]
            scores = [_dot(q_pairs[h // 2], k_var[h // 4][h % 2], NT) for h in heads]
            dprobs = [_dot(do_pairs[h // 2], v_var[h // 4][h % 2], NT) for h in heads]
            q_t = q_ref[rows, :].astype(F32).T.astype(BF16)
            do_t = do_all.astype(F32).T.astype(BF16)

            ug, dug_du = _gelu_and_grad(u_ref[rows, :])
            gv, dgv_dv = _gelu_and_grad(vg_ref[rows, :])
            xhat, rstd = _layer_norm_stats(gv)
            vgl = xhat * gain + b_ref[...]
            mixed = [_dot(wcat[p], _pair_stack(vgl[:, cols], left)) for p, cols in enumerate(pair_cols)]

            valid = _band_mask(chunk)
            masked = [jnp.where(valid, scores[h] * SCALE, NEG_INF) for h in heads]
            maxes = [jnp.maximum(jnp.max(masked[h], axis=1, keepdims=True), sinks_h[h]) for h in heads]
            exps = [jnp.exp(masked[h] - maxes[h]) for h in heads]
            exp_sinks = [jnp.exp(sinks_h[h] - maxes[h]) for h in heads]
            invs = [1.0 / (jnp.sum(exps[h], axis=1, keepdims=True) + exp_sinks[h]) for h in heads]
            probs = [exps[h] * invs[h] for h in heads]
            dsums = [jnp.sum(probs[h] * dprobs[h], axis=1, keepdims=True) for h in heads]
            ds_b = [(probs[h] * (dprobs[h] - dsums[h]) * SCALE).astype(BF16) for h in heads]
            probs_b = [probs[h].astype(BF16) for h in heads]

            dm_stacks = []
            for p, cols in enumerate(pair_cols):
                da = dcat_ref[rows, cols].astype(F32)
                dmain_ref[rows, cols] = (da * (mixed[p] + bias_ref[:, cols]) * dug_du[:, cols]).astype(BF16)
                dmixed = da * ug[:, cols]
                dmix_acc[:, cols] += dmixed
                dm_stacks.append(_pair_stack(dmixed, left))

            dq_all = jnp.concatenate(
                [_dot(ds_b[2 * p], k_var[p // 2][0]) + _dot(ds_b[2 * p + 1], k_var[p // 2][1]) for p in range(n_qpairs)], axis=1)
            dk2_t = group_grad_t(q_t, ds_b)
            dv2_t = group_grad_t(do_t, probs_b)

            for p, cols in enumerate(pair_cols):
                dw_pair = _dot(dm_stacks[p], vgl[:, cols].astype(BF16), NT)
                dw_ref[2 * p] += jnp.where(causal, dw_pair[:CHUNK], 0.0)
                dw_ref[2 * p + 1] += jnp.where(causal, dw_pair[CHUNK:], 0.0)
            dvgl = jnp.concatenate([_dot(wcat_t[p], dm_stacks[p]) for p in range(n_pairs)], axis=1)

            dsink_row = jnp.zeros((1, LANES), F32)
            for h in heads:
                d_sink = -jnp.sum(exp_sinks[h] * invs[h] * dsums[h], axis=0, keepdims=True)
                dsink_row = dsink_row + jnp.where(lane_row == h, d_sink, 0.0)
            dsink_ref[0:1, :] += dsink_row
            cos_c, sin_c = cosc_ref[rows, :], sinc_ref[rows, :]
            cos_p = cosp_ref[...] if c == 0 else cosc_ref[before, :]
            sin_p = sinp_ref[...] if c == 0 else sinc_ref[before, :]
            dmain_ref[rows, 2 * D_GMLP : D_MAIN] = _rope_transposed(dq_all, _lane_tile(cos_c, n_qpairs), _lane_tile(sin_c, n_qpairs)).astype(BF16)
            dk2 = dk2_t.T
            dv2 = dv2_t.T
            cur = pl.ds(pl.multiple_of(chunk * CHUNK, CHUNK), CHUNK)
            dkv_ref[cur, 0:D_KV] = _rope_transposed(dk2[CHUNK:], cos_c, sin_c)
            dkv_ref[cur, D_KV : 2 * D_KV] = dv2[CHUNK:]
            prev = pl.ds(pl.multiple_of(jnp.maximum(chunk - 1, 0) * CHUNK, CHUNK), CHUNK)
            dkv_ref[prev, 0:D_KV] += _rope_transposed(dk2[:CHUNK], cos_p, sin_p)
            dkv_ref[prev, D_KV : 2 * D_KV] += dv2[:CHUNK]

            dg_ref[...] += jnp.sum(dvgl * xhat, axis=0, keepdims=True)
            db_ref[...] += jnp.sum(dvgl, axis=0, keepdims=True)
            dgv = _layer_norm_bwd(dvgl * gain, xhat, rstd)
            dmain_ref[rows, D_GMLP : 2 * D_GMLP] = (dgv * dgv_dv).astype(BF16)

        @pl.when(i == n_chunks // CHUNKS_PER_STEP - 1)
        def _():
            tile = jnp.zeros((CHUNK, LANES), F32)
            for p, cols in enumerate(pair_cols):
                dm = dmix_acc[:, cols]
                sl = jnp.sum(jnp.where(left, dm, 0.0), axis=1, keepdims=True)
                sr = jnp.sum(jnp.where(left, 0.0, dm), axis=1, keepdims=True)
                tile = jnp.where(lane == 2 * p, sl, tile)
                tile = jnp.where(lane == 2 * p + 1, sr, tile)
            dbs_ref[...] = tile

    step = CHUNKS_PER_STEP * CHUNK
    in_specs = _chunk_specs() + [
        pl.BlockSpec((step, D_MODEL), _step_rows),
        pl.BlockSpec((step, LANES), _step_rows),
        pl.BlockSpec((step, LANES), _step_rows),
        pl.BlockSpec((CHUNK, LANES), _chunk_before_step),
        pl.BlockSpec((CHUNK, LANES), _chunk_before_step),
        _const_spec((1, D_GMLP)),
        _const_spec((1, D_GMLP)),
        _const_spec((N_HEADS, CHUNK, CHUNK)),
        _const_spec((CHUNK, D_GMLP)),
        pl.BlockSpec(memory_space=pltpu.SMEM),
    ]
    body, in_specs, operands = _after(
        dep, body, in_specs, [u, vg, q, k, k, va, va, dcat, cos, sin, cos, sin, v_ln_g, v_ln_b, w_spatial, bias_full, sinks])
    return pl.pallas_call(
        body,
        name="mixer_bwd",
        grid=(n_chunks // CHUNKS_PER_STEP,),
        in_specs=in_specs,
        out_specs=[
            pl.BlockSpec((step, D_MAIN), _step_rows),
            _const_spec((t, 2 * D_KV)),
            _const_spec((1, D_GMLP)),
            _const_spec((1, D_GMLP)),
            _const_spec((N_HEADS, CHUNK, CHUNK)),
            _const_spec((CHUNK, LANES)),
            _const_spec((8, LANES)),
        ],
        out_shape=[
            jax.ShapeDtypeStruct((t, D_MAIN), BF16),
            jax.ShapeDtypeStruct((t, 2 * D_KV), F32),
            jax.ShapeDtypeStruct((1, D_GMLP), F32),
            jax.ShapeDtypeStruct((1, D_GMLP), F32),
            jax.ShapeDtypeStruct((N_HEADS, CHUNK, CHUNK), F32),
            jax.ShapeDtypeStruct((CHUNK, LANES), F32),
            jax.ShapeDtypeStruct((8, LANES), F32),
        ],
        scratch_shapes=[
            pltpu.VMEM((CHUNK, D_GMLP), F32),
            pltpu.VMEM((D_GMLP // LANES, CHUNK, 2 * CHUNK), BF16),
            pltpu.VMEM((D_GMLP // LANES, CHUNK, 2 * CHUNK), BF16),
        ],
        compiler_params=_params(("arbitrary",)),
    )(*operands)


def _grad_x(dh_main, dkv, dz1, w_in_t, dep=None):
    t = dz1.shape[0]

    def body(dm_ref, dkv_ref, dz1_ref, w_ref, gx_ref):
        acc = ALPHA * dz1_ref[...] + _dot(dm_ref[...], w_ref[0:D_MAIN, :])
        gx_ref[...] = acc + _dot(dkv_ref[...].astype(BF16), w_ref[D_MAIN:D_IN, :])

    body, in_specs, operands = _after(
        dep, body, [_row_spec(TM, D_MAIN), _row_spec(TM, 2 * D_KV), _row_spec(TM, D_MODEL), _const_spec((D_IN, D_MODEL))], [dh_main, dkv, dz1, w_in_t])
    return pl.pallas_call(
        body,
        name="grad_x",
        grid=(t // TM,),
        in_specs=in_specs,
        out_specs=_row_spec(TM, D_MODEL),
        out_shape=jax.ShapeDtypeStruct((t, D_MODEL), F32),
        compiler_params=_params(("parallel",)),
    )(*operands)


def _token_contraction(name, out_rows, tk, in_arrays, contributions, dep=None):
    t = in_arrays[0].shape[0]

    def body(*refs):
        out_ref = refs[-1]

        @pl.when(pl.program_id(0) == 0)
        def _():
            out_ref[...] = jnp.zeros_like(out_ref)

        for row0, a, b in contributions(*refs[:-1]):
            out_ref[row0 : row0 + a.shape[1], :] += _dot(a, b, TN)

    in_specs = [_row_spec(tk, a.shape[1]) for a in in_arrays]
    body, in_specs, operands = _after(dep, body, in_specs, in_arrays)
    return pl.pallas_call(
        body,
        name=name,
        grid=(t // tk,),
        in_specs=in_specs,
        out_specs=_const_spec((out_rows, D_MODEL), single_buffer=True),
        out_shape=jax.ShapeDtypeStruct((out_rows, D_MODEL), F32),
        compiler_params=_params(("arbitrary",)),
    )(*operands)


def _grad_w_out(cat, dz1b, dep=None):
    def contributions(cat_ref, dz1_ref):
        return [(0, cat_ref[...], dz1_ref[...])]

    return _token_contraction("grad_w_out", D_MODEL, TK, [cat, dz1b], contributions, dep)


def _grad_w_ff1(x1b, dpre):
    def contributions(x1_ref, dpre_ref):
        x1 = x1_ref[...]
        return [(j * D_MODEL, x1, dpre_ref[:, j * D_MODEL : (j + 1) * D_MODEL]) for j in range(N_FF_BLOCKS)]

    return _token_contraction("grad_w_ff1", D_FF, TK_FF, [x1b, dpre], contributions)


def _grad_w_ff2(r, dz2b):
    def contributions(r_ref, dz2_ref):
        dz2 = dz2_ref[...]
        out = []
        for j in range(N_FF_BLOCKS):
            rf = r_ref[:, j * D_MODEL : (j + 1) * D_MODEL].astype(F32)
            out.append((j * D_MODEL, (rf * rf).astype(BF16), dz2))
        return out

    return _token_contraction("grad_w_ff2", D_FF, TK_FF, [r, dz2b], contributions)


ANY = pl.BlockSpec(memory_space=pl.ANY)


def _mesh_position():
    return lax.axis_index("x"), lax.axis_index("y"), lax.axis_index("c")


def _other_chips(x, y):
    return [(1 - x, y), (x, 1 - y), (1 - x, 1 - y)]


def _remote(src, dst, send_sem, recv_sem, device):
    return pltpu.make_async_remote_copy(src_ref=src, dst_ref=dst, send_sem=send_sem, recv_sem=recv_sem, device_id=device, device_id_type=MESH)


def _rows(ref, start, size):
    return ref.at[pl.ds(start, size), :]


def _gather_w_in_and_rope_tables(shard, pos_row, inv_freq_row, other_shards):
    t = pos_row.shape[1]
    steps = t // TM
    rows = shard.shape[0]
    half = rows // 2
    n_other = len(other_shards)

    def body(pos_ref, f_ref, shard_ref, *rest):
        others, (cos_ref, sin_ref, all_ref) = rest[:n_other], rest[n_other : n_other + 3]
        others_bf16, (send_sems, recv_sems) = rest[n_other + 3 : 2 * n_other + 3], rest[2 * n_other + 3 :]
        for src, dst in zip(others, others_bf16):
            dst[...] = src[...].astype(BF16)
        k = pl.program_id(0)
        x, y, c = _mesh_position()
        me = 2 * x + y
        chips = _other_chips(x, y)
        sibling = (x, y, 1 - c)

        def direct_copies():
            out = []
            for kk, (px, py) in enumerate(chips):
                send = _remote(_rows(shard_ref, c * half, half), _rows(all_ref, me * rows + c * half, half),
                               send_sems.at[kk], recv_sems.at[kk], (px, py, c))
                landed = _rows(all_ref, (2 * px + py) * rows + c * half, half)
                out.append((send, _remote(landed, landed, send_sems.at[kk], recv_sems.at[kk], (px, py, c))))
            own = _rows(all_ref, me * rows, rows)
            out.append((_remote(shard_ref, own, send_sems.at[6], recv_sems.at[6], sibling),
                        _remote(own, own, send_sems.at[6], recv_sems.at[6], sibling)))
            return out

        def passed_on_copies():
            out = []
            for kk, (px, py) in enumerate(chips):
                row = (2 * px + py) * rows
                mine, theirs = _rows(all_ref, row + c * half, half), _rows(all_ref, row + (1 - c) * half, half)
                out.append((_remote(mine, mine, send_sems.at[3 + kk], recv_sems.at[3 + kk], sibling),
                            _remote(theirs, theirs, send_sems.at[3 + kk], recv_sems.at[3 + kk], sibling)))
            return out

        @pl.when(k == 0)
        def _():
            for send, _ in direct_copies():
                send.start()

        pos_rows = jnp.broadcast_to(pos_ref[...].astype(F32), (LANES, TM)).T
        ang = pos_rows * f_ref[...]
        cos_ref[...] = jnp.cos(ang)
        sin_ref[...] = jnp.sin(ang)

        @pl.when(k == steps - 1)
        def _():
            direct, passed = direct_copies(), passed_on_copies()
            for (_, arrived), (forward, _) in zip(direct[:3], passed):
                arrived.wait_recv()
                forward.start()
            for _, arrived in passed + direct[3:]:
                arrived.wait_recv()
            for send, _ in direct + passed:
                send.wait_send()

    other_specs = [_row_spec(s.shape[0] // steps, s.shape[1]) for s in other_shards]
    outs = pl.pallas_call(
        body,
        name="gather_w_in_and_rope_tables",
        grid=(steps,),
        in_specs=[pl.BlockSpec((1, TM), lambda i: (0, i)), _const_spec((1, LANES)), ANY] + other_specs,
        out_specs=[_row_spec(TM, LANES), _row_spec(TM, LANES), ANY] + other_specs,
        out_shape=[jax.ShapeDtypeStruct((t, LANES), F32)] * 2
        + [jax.ShapeDtypeStruct((N_CHIPS * rows, shard.shape[1]), shard.dtype)]
        + [jax.ShapeDtypeStruct(s.shape, BF16) for s in other_shards],
        scratch_shapes=[pltpu.SemaphoreType.DMA((7,)), pltpu.SemaphoreType.DMA((7,))],
        compiler_params=_params(("arbitrary",)),
    )(pos_row, inv_freq_row, shard, *other_shards)
    return outs[0], outs[1], outs[2], list(outs[3:])


def _pair_gather(name, shards):
    n = len(shards)

    def body(*refs):
        outs = refs[n : 2 * n]
        send_sems, recv_sems = refs[2 * n :]
        x, y, c = _mesh_position()
        sibling = (x, y, 1 - c)
        sends = []
        for w in range(n):
            half = shards[w].shape[0] // 2
            mine = _rows(outs[w], c * half, half)
            cp = _remote(mine, mine, send_sems.at[w], recv_sems.at[w], sibling)
            cp.start()
            sends.append(cp)
        for w in range(n):
            half = shards[w].shape[0] // 2
            blk = _rows(outs[w], (1 - c) * half, half)
            _remote(blk, blk, send_sems.at[w], recv_sems.at[w], sibling).wait_recv()
        for cp in sends:
            cp.wait_send()

    return pl.pallas_call(
        body,
        name=name,
        in_specs=[ANY] * n,
        out_specs=[ANY] * n,
        out_shape=[jax.ShapeDtypeStruct(s.shape, s.dtype) for s in shards],
        input_output_aliases={w: w for w in range(n)},
        scratch_shapes=[pltpu.SemaphoreType.DMA((n,)), pltpu.SemaphoreType.DMA((n,))],
    )(*shards)


def _grad_w_in_t_and_small_all_reduce(dh_main, dkv, x, slab):
    t = x.shape[0]
    steps = t // TK
    rows = slab.shape[0]
    part = rows // 8

    def body(dm_ref, dkv_ref, x_ref, slab_ref, grad_ref, sum_ref, landing, reduced, gathered, send_sems, recv_sems):
        k = pl.program_id(0)
        x_, y_, c_ = _mesh_position()
        me = 4 * x_ + 2 * y_ + c_
        flips = [(f >> 2, (f >> 1) & 1, f & 1) for f in range(1, 8)]

        def peer(flip):
            fx, fy, fc = flip
            return (1 - x_ if fx else x_, 1 - y_ if fy else y_, 1 - c_ if fc else c_)

        def part_of(ref, device):
            return ref.at[pl.ds(pl.multiple_of(device * part, 8), part), :]

        def scatter_copies():
            out = []
            for kk, flip in enumerate(flips):
                px, py, pc = peer(flip)
                them = 4 * px + 2 * py + pc
                send = _remote(part_of(slab_ref, them), landing.at[me], send_sems.at[kk], recv_sems.at[kk], (px, py, pc))
                recv = _remote(landing.at[them], landing.at[them], send_sems.at[kk], recv_sems.at[kk], (px, py, pc))
                out.append((send, recv))
            return out

        def gather_copies():
            out = []
            for kk, flip in enumerate(flips):
                px, py, pc = peer(flip)
                them = 4 * px + 2 * py + pc
                send = _remote(reduced, part_of(gathered, me), send_sems.at[7 + kk], recv_sems.at[7 + kk], (px, py, pc))
                recv = _remote(part_of(gathered, them), part_of(gathered, them), send_sems.at[7 + kk], recv_sems.at[7 + kk], (px, py, pc))
                out.append((send, recv))
            return out

        @pl.when(k == 0)
        def _():
            grad_ref[...] = jnp.zeros_like(grad_ref)
            for send, _ in scatter_copies():
                send.start()
            landing[me] = part_of(slab_ref, me)[...]

        @pl.when(k == steps // 2)
        def _():
            for _, recv in scatter_copies():
                recv.wait_recv()
            total = landing[0]
            for s in range(1, 8):
                total = total + landing[s]
            reduced[...] = total
            part_of(gathered, me)[...] = total
            for send, _ in gather_copies():
                send.start()

        xb = x_ref[...].astype(BF16)
        grad_ref[0:D_MAIN, :] += _dot(dm_ref[...], xb, TN)
        grad_ref[D_MAIN:D_IN, :] += _dot(dkv_ref[...].astype(BF16), xb, TN)

        @pl.when(k == steps - 1)
        def _():
            for send, recv in gather_copies():
                recv.wait_recv()
                send.wait_send()
            for send, _ in scatter_copies():
                send.wait_send()
            sum_ref[...] = gathered[...]

    return pl.pallas_call(
        body,
        name="grad_w_in_and_small_all_reduce",
        grid=(steps,),
        in_specs=[_row_spec(TK, D_MAIN), _row_spec(TK, 2 * D_KV), _row_spec(TK, D_MODEL), _const_spec(slab.shape)],
        out_specs=[_const_spec((D_IN, D_MODEL), single_buffer=True), _const_spec(slab.shape)],
        out_shape=[jax.ShapeDtypeStruct((D_IN, D_MODEL), F32), jax.ShapeDtypeStruct(slab.shape, slab.dtype)],
        scratch_shapes=[
            pltpu.VMEM((8, part, LANES), F32),
            pltpu.VMEM((part, LANES), F32),
            pltpu.VMEM(slab.shape, F32),
            pltpu.SemaphoreType.DMA((14,)),
            pltpu.SemaphoreType.DMA((14,)),
        ],
        compiler_params=_params(("arbitrary",)),
    )(dh_main, dkv, x, slab)


HBM = pl.BlockSpec(memory_space=pltpu.HBM)
SEM = pl.BlockSpec(memory_space=pltpu.SEMAPHORE)
DATAFLOW = pltpu.SideEffectType.DATAFLOW_SIDE_EFFECTING
TOKEN = jax.ShapeDtypeStruct((8, LANES), F32)


def _plan_copies(bufs, plan, send_sems, recv_sems):
    out = []
    for i, (src, src_row, dst, dst_row, recv_row, rows, device) in enumerate(plan):
        send = _remote(_rows(bufs[src], src_row, rows), _rows(bufs[dst], dst_row, rows), send_sems.at[i], recv_sems.at[i], device)
        landed = _rows(bufs[dst], recv_row, rows)
        recv = _remote(landed, landed, send_sems.at[i], recv_sems.at[i], device)
        out.append((send, recv))
    return out


def _split_call(name, bufs, wait=None, start=None, after=None):
    n = len(bufs)
    n_in = n + (2 if wait else 0) + (1 if after is not None else 0)
    n_start = len(start(0, 0, 0)) if start else 0

    def body(*refs):
        ins = refs[:n]
        x, y, c = _mesh_position()
        if wait:
            for send, recv in _plan_copies(ins, wait[0](x, y, c), refs[n], refs[n + 1]):
                recv.wait_recv()
                send.wait_send()
        if start:
            for send, _ in _plan_copies(ins, start(x, y, c), refs[n_in + n + 1], refs[n_in + n + 2]):
                send.start()
        token = refs[n_in + n]
        token[...] = jnp.zeros_like(token)

    operands = [pltpu.with_memory_space_constraint(b, pltpu.HBM) for b in bufs]
    in_specs = [HBM] * n
    if wait:
        operands += [wait[1], wait[2]]
        in_specs += [SEM, SEM]
    if after is not None:
        operands.append(after)
        in_specs.append(ANY)
    out_shape = [pltpu.HBM(b.shape, b.dtype) for b in bufs] + [TOKEN]
    out_specs = [HBM] * n + [pl.BlockSpec(memory_space=pltpu.VMEM)]
    if start:
        out_shape += [pltpu.SemaphoreType.DMA((n_start,)), pltpu.SemaphoreType.DMA((n_start,))]
        out_specs += [SEM, SEM]
    outs = pl.pallas_call(
        body,
        name=name,
        in_specs=in_specs,
        out_specs=out_specs,
        out_shape=out_shape,
        input_output_aliases={i: i for i in range(n)},
        compiler_params=pltpu.CompilerParams(has_side_effects=DATAFLOW),
    )(*operands)
    return (list(outs[:n]), outs[n]) + tuple(outs[n + 1 :])


def _gather_plans(shard_rows):
    n = len(shard_rows)

    def neighbours(x, y):
        return ((1 - x, y), (x, 1 - y))

    def direct(x, y, c):
        me = 2 * x + y
        plan = []
        for w, rows in enumerate(shard_rows):
            half = rows // 2
            for px, py in neighbours(x, y):
                plan.append((w, c * half, n + w, me * rows + c * half, (2 * px + py) * rows + c * half, half, (px, py, c)))
            plan.append((w, 0, n + w, me * rows, me * rows, rows, (x, y, 1 - c)))
        return plan

    def passed_on(x, y, c):
        (xn, yn), diagonal = neighbours(x, y), 2 * (1 - x) + (1 - y)
        relayed = (1 - c) * (2 * xn[0] + xn[1]) + c * (2 * yn[0] + yn[1])
        target = (x * (1 - c) + (1 - x) * c, (1 - y) * (1 - c) + y * c, c)
        plan = []
        for w, rows in enumerate(shard_rows):
            half = rows // 2
            for px, py in (xn, yn):
                row = (2 * px + py) * rows
                plan.append((n + w, row + c * half, n + w, row + c * half, row + (1 - c) * half, half, (x, y, 1 - c)))
            plan.append((n + w, relayed * rows + c * half, n + w, relayed * rows + c * half, diagonal * rows + c * half, half, target))
        return plan

    def diagonal_passed_on(x, y, c):
        plan = []
        for w, rows in enumerate(shard_rows):
            half = rows // 2
            row = (2 * (1 - x) + (1 - y)) * rows
            plan.append((n + w, row + c * half, n + w, row + c * half, row + (1 - c) * half, half, (x, y, 1 - c)))
        return plan

    return direct, passed_on, diagonal_passed_on


def _swap_plan(block_rows):
    n = len(block_rows)

    def plan_fn(x, y, c):
        plan = []
        for w, rows in enumerate(block_rows):
            half = rows // 2
            for j in range(N_CHIPS):
                plan.append((w, j * rows + (1 - c) * half, n + w, j * half, j * half, half, (x, y, 1 - c)))
        return plan

    return plan_fn


def _exchange_plan(halves):
    n = len(halves)

    def plan_fn(x, y, c):
        plan = []
        for w, half in enumerate(halves):
            for kk, (px, py) in enumerate(_other_chips(x, y)):
                plan.append((w, (2 * px + py) * half, n + w, kk * half, kk * half, half, (px, py, c)))
        return plan

    return plan_fn


def _landing(rows, cols, dtype):
    return lax.empty((rows, cols), dtype)


def _row_tile(rows, cap=512):
    best = 8
    for cand in range(8, cap + 1, 8):
        if rows % cand == 0:
            best = cand
    return best


def _pair_sum(name, grad, theirs, pos):
    half = theirs.shape[0] // N_CHIPS
    cols = theirs.shape[1]
    tile = _row_tile(half)
    steps = half // tile

    def body(pos_ref, g_ref, t_ref, p_ref, own_ref):
        total = g_ref[...] + t_ref[...]
        p_ref[...] = total.astype(BF16)

        @pl.when(pl.program_id(1) == pos_ref[1])
        def _():
            own_ref[...] = total

    return pl.pallas_call(
        body,
        name=name,
        grid_spec=pltpu.PrefetchScalarGridSpec(
            num_scalar_prefetch=1,
            grid=(steps, N_CHIPS),
            in_specs=[
                pl.BlockSpec((tile, cols), lambda i, j, pos: ((2 * j + pos[0]) * steps + i, 0)),
                pl.BlockSpec((tile, cols), lambda i, j, pos: (j * steps + i, 0)),
            ],
            out_specs=[
                pl.BlockSpec((tile, cols), lambda i, j, pos: (j * steps + i, 0)),
                pl.BlockSpec((tile, cols), lambda i, j, pos: (i, 0)),
            ],
        ),
        out_shape=[jax.ShapeDtypeStruct((N_CHIPS * half, cols), BF16), jax.ShapeDtypeStruct((half, cols), F32)],
        compiler_params=_params(("parallel", "arbitrary")),
    )(pos, grad, theirs)


def _chip_sum(name, own, landed, pos):
    half, cols = own.shape
    tile = _row_tile(half)
    steps = half // tile

    def body(pos_ref, own_ref, l0, l1, l2, o_ref):
        o_ref[...] = ((own_ref[...] + l0[...].astype(F32)) + l1[...].astype(F32)) + l2[...].astype(F32)

    landed_specs = [pl.BlockSpec((tile, cols), lambda i, pos, _k=k: (_k * steps + i, 0)) for k in range(N_CHIPS - 1)]
    return pl.pallas_call(
        body,
        name=name,
        grid_spec=pltpu.PrefetchScalarGridSpec(
            num_scalar_prefetch=1,
            grid=(steps,),
            in_specs=[pl.BlockSpec((tile, cols), lambda i, pos: (i, 0))] + landed_specs,
            out_specs=pl.BlockSpec((tile, cols), lambda i, pos: (pos[0] * steps + i, 0)),
        ),
        out_shape=jax.ShapeDtypeStruct((2 * half, cols), F32),
        compiler_params=_params(("parallel",)),
    )(pos, own, landed, landed, landed)


def _adamw(name, w, g, m, v):
    rows, cols = w.shape
    tile = rows if rows * cols <= 256 * 1024 else _row_tile(rows)

    def body(w_ref, g_ref, m_ref, v_ref, g_out_ref, d_ref, nm_ref, nv_ref):
        g = g_ref[...]
        g_out_ref[...] = g
        nm = ADAM_B1 * m_ref[...] + (1.0 - ADAM_B1) * g
        nv = ADAM_B2 * v_ref[...] + (1.0 - ADAM_B2) * (g * g)
        m_hat = nm / (1.0 - ADAM_B1**ADAM_STEP)
        v_hat = nv / (1.0 - ADAM_B2**ADAM_STEP)
        d_ref[...] = -ADAM_LR * (m_hat / (jnp.sqrt(v_hat) + ADAM_EPS) + ADAM_WD * w_ref[...])
        nm_ref[...] = nm
        nv_ref[...] = nv

    spec = _row_spec(tile, cols)
    return pl.pallas_call(
        body,
        name=name,
        grid=(rows // tile,),
        in_specs=[spec] * 4,
        out_specs=[spec] * 4,
        out_shape=[jax.ShapeDtypeStruct((rows, cols), F32)] * 4,
        compiler_params=_params(("parallel",)),
    )(w, g, m, v)


_SMALL = (
    ("v_ln_g", (D_GMLP,), 8),
    ("v_ln_b", (D_GMLP,), 8),
    ("w_spatial", (N_HEADS, CHUNK, CHUNK), 1024),
    ("b_spatial", (N_HEADS, CHUNK), 8),
    ("sinks", (N_HEADS,), 8),
    ("ln1_g", (D_MODEL,), 8),
    ("ln1_b", (D_MODEL,), 8),
    ("ln2_g", (D_MODEL,), 8),
    ("ln2_b", (D_MODEL,), 8),
    ("squared_error", (D_MODEL,), 8),
)
N_SMALL_PARAMS = len(_SMALL) - 1


def _pack_small(values):
    parts = []
    for (name, shape, rows), val in zip(_SMALL, values, strict=True):
        flat = val.reshape(-1).astype(F32)
        parts.append(jnp.pad(flat, (0, rows * LANES - flat.shape[0])).reshape(rows, LANES))
    parts.append(jnp.zeros((SMALL_ROWS - sum(rows for _, _, rows in _SMALL), LANES), F32))
    return jnp.concatenate(parts, axis=0)


def _adamw_update(w, g, m, v):
    nm = ADAM_B1 * m + (1.0 - ADAM_B1) * g
    nv = ADAM_B2 * v + (1.0 - ADAM_B2) * (g * g)
    m_hat = nm / (1.0 - ADAM_B1**ADAM_STEP)
    v_hat = nv / (1.0 - ADAM_B2**ADAM_STEP)
    return -ADAM_LR * (m_hat / (jnp.sqrt(v_hat) + ADAM_EPS) + ADAM_WD * w), nm, nv


def _adamw_small(g_slab, params, first, second):
    n = N_SMALL_PARAMS

    def pieces(shape):
        if len(shape) == 3:
            return [((0, h), h * shape[1], shape[1], shape[2]) for h in range(shape[0])]
        if len(shape) == 2:
            return [((0,), 0, shape[0], shape[1])]
        if shape[0] >= LANES:
            return [((slice(None), slice(r * LANES, (r + 1) * LANES)), r, 1, LANES) for r in range(shape[0] // LANES)]
        return [((slice(None), slice(0, shape[0])), 0, 1, shape[0])]

    def body(*refs):
        g_ref = refs[0]
        w_refs, m_refs, v_refs = refs[1 : 1 + n], refs[1 + n : 1 + 2 * n], refs[1 + 2 * n : 1 + 3 * n]
        outs = refs[1 + 3 * n :]
        row0 = 0
        for idx, (_, shape, rows) in enumerate(_SMALL[:n]):
            for where, first_row, n_rows, lanes in pieces(shape):
                g = g_ref[row0 + first_row : row0 + first_row + n_rows, 0:lanes]
                delta, nm, nv = _adamw_update(w_refs[idx][where], g, m_refs[idx][where], v_refs[idx][where])
                for group, val in enumerate((g, delta, nm, nv)):
                    outs[group * n + idx][where] = val
            row0 += rows

    vmem = pl.BlockSpec(memory_space=pltpu.VMEM)
    shapes = [jax.ShapeDtypeStruct(p.shape, F32) for p in params]
    outs = pl.pallas_call(
        body,
        name="adamw_small",
        in_specs=[vmem] * (1 + 3 * n),
        out_specs=[vmem] * (4 * n),
        out_shape=shapes * 4,
        compiler_params=_params(),
    )(g_slab, *params, *first, *second)
    return [list(outs[group * n : (group + 1) * n]) for group in range(4)]


def kernel(x, positions, w_in, v_ln_g, v_ln_b, w_spatial, b_spatial, sinks, w_out, ln1_g, ln1_b, w_ff1, w_ff2, ln2_g, ln2_b, loss_target, m_w_in, m_v_ln_g, m_v_ln_b, m_w_spatial, m_b_spatial, m_sinks, m_w_out, m_ln1_g, m_ln1_b, m_w_ff1, m_w_ff2, m_ln2_g, m_ln2_b, v_w_in, v_v_ln_g, v_v_ln_b, v_w_spatial, v_b_spatial, v_sinks, v_w_out, v_ln1_g, v_ln1_b, v_w_ff1, v_w_ff2, v_ln2_g, v_ln2_b):
    t = x.shape[1]
    x2 = x.reshape(t, D_MODEL)
    target = loss_target.reshape(t, D_MODEL)

    inv_freq = ROPE_THETA ** (-jnp.arange(0, HEAD_DIM, 2, dtype=F32) / HEAD_DIM)
    cos, sin, w_in_t, later = _gather_w_in_and_rope_tables(
        w_in[0].T.astype(BF16), positions, jnp.tile(inv_freq, LANES // (HEAD_DIM // 2)).reshape(1, LANES),
        [w_out[0], w_ff1[0], w_ff2[0]])
    later_rows = [s.shape[0] for s in later]
    direct_plan, pass_plan, diagonal_plan = _gather_plans(later_rows)
    bufs, started, direct_send, direct_recv = _split_call(
        "gather_start", later + [_landing(N_CHIPS * r, D_MODEL, BF16) for r in later_rows], start=direct_plan, after=w_in_t)

    u, vg, q, k, va = _in_proj(x2, w_in_t, cos, sin, dep=started)
    bias_full = jnp.repeat(b_spatial[0].T, HEAD_DIM, axis=1)
    sink_vec = sinks.reshape(N_HEADS)
    bufs, passing, pass_send, pass_recv = _split_call(
        "gather_pass", bufs, wait=(direct_plan, direct_send, direct_recv), start=pass_plan, after=u)
    cat = _mixer_fwd(u, vg, q, k, va, v_ln_g, v_ln_b, w_spatial[0], bias_full, sink_vec, dep=passing)
    bufs, passing, diag_send, diag_recv = _split_call(
        "gather_pass_diagonal", bufs, wait=(pass_plan, pass_send, pass_recv), start=diagonal_plan, after=cat)
    bufs, _ = _split_call("gather_end", bufs, wait=(diagonal_plan, diag_send, diag_recv), after=passing)
    w_out_all = bufs[3]
    w1_all = bufs[4].reshape(N_FF_BLOCKS, D_MODEL, D_MODEL)
    w2_all = bufs[5].reshape(N_FF_BLOCKS, D_MODEL, D_MODEL)
    xhat1, rstd1, x1b, r, dz2, dz2b, d_ln2_g, d_ln2_b, sq_err = _ffn_fwd_loss(
        cat, x2, w_out_all, ln1_g, ln1_b, w1_all, w2_all, ln2_g, ln2_b, target)

    pos = jnp.stack([lax.axis_index("c"), 2 * lax.axis_index("x") + lax.axis_index("y")]).astype(jnp.int32)
    half_landing = lambda g: _landing(g.shape[0] // 2, D_MODEL, F32)
    g_ff2_local = _grad_w_ff2(r, dz2b)
    swap_plan = _swap_plan([D_FF // N_CHIPS])
    ff2_bufs, swapping2, swap2_send, swap2_recv = _split_call("ff2_swap_start", [g_ff2_local, half_landing(g_ff2_local)], start=swap_plan)
    g_ff1_local, dz1, dz1b, dcat, d_ln1_g, d_ln1_b = _ffn_bwd_ln1(
        dz2, r, x1b, xhat1, rstd1, ln1_g, w1_all, w2_all, w_out_all, dep=swapping2)
    ff1_bufs, swapping1, swap1_send, swap1_recv = _split_call("ff1_swap_start", [g_ff1_local, half_landing(g_ff1_local)], start=swap_plan)
    g_out_local = _grad_w_out(cat, dz1b, dep=swapping1)
    ff2_bufs, swapped2 = _split_call("ff2_swap_wait", ff2_bufs, wait=(swap_plan, swap2_send, swap2_recv), after=g_out_local)
    ff1_bufs, _ = _split_call("ff1_swap_wait", ff1_bufs, wait=(swap_plan, swap1_send, swap1_recv), after=swapped2)
    ff_sums = [_pair_sum("grad_pair_sum_w_ff1", ff1_bufs[0], ff1_bufs[1], pos), _pair_sum("grad_pair_sum_w_ff2", ff2_bufs[0], ff2_bufs[1], pos)]
    ff_halves = [p.shape[0] // N_CHIPS for p, _ in ff_sums]
    exchange_plan = _exchange_plan(ff_halves)
    bufs, exchanging, ex_send, ex_recv = _split_call(
        "ff_exchange_start", [p for p, _ in ff_sums] + [_landing(3 * h, D_MODEL, BF16) for h in ff_halves], start=exchange_plan)
    dh_main, dkv, d_v_ln_g, d_v_ln_b, d_w_spatial, d_b_spatial_t, d_sinks = _mixer_bwd(
        u, vg, q, k, va, dcat, cos, sin, v_ln_g, v_ln_b, w_spatial[0], bias_full, sink_vec, dep=exchanging)
    g_in_local, small_g = _grad_w_in_t_and_small_all_reduce(dh_main, dkv, x2, _pack_small(
        [d_v_ln_g, d_v_ln_b, d_w_spatial, d_b_spatial_t[:, :N_HEADS].T, d_sinks[0, :N_HEADS], d_ln1_g, d_ln1_b, d_ln2_g, d_ln2_b, sq_err]))
    sq_row = sum(rows for _, _, rows in _SMALL[:N_SMALL_PARAMS])
    loss = 0.5 * jnp.sum(small_g[sq_row : sq_row + _SMALL[N_SMALL_PARAMS][2]]) / D_MODEL

    small = [g_in_local, g_out_local]
    small_swap_plan = _swap_plan([g.shape[0] // N_CHIPS for g in small])
    swap_bufs, small_swapping, ss_send, ss_recv = _split_call(
        "small_swap_start", small + [half_landing(g) for g in small], start=small_swap_plan)
    grad_x_flat = _grad_x(dh_main, dkv, dz1, w_in_t, dep=small_swapping)
    grad_x = grad_x_flat.reshape(1, t, D_MODEL)
    swap_bufs, _ = _split_call("small_swap_wait", swap_bufs, wait=(small_swap_plan, ss_send, ss_recv), after=grad_x_flat)
    pair_sums = [_pair_sum("grad_pair_sum_" + nm, g, th, pos) for nm, g, th in zip(["w_in", "w_out"], swap_bufs[:2], swap_bufs[2:])]
    small_halves = [p.shape[0] // N_CHIPS for p, _ in pair_sums]
    small_plan = _exchange_plan(small_halves)
    small_bufs, small_exchanging, sm_send, sm_recv = _split_call(
        "small_exchange_start", [p for p, _ in pair_sums] + [_landing(3 * h, D_MODEL, BF16) for h in small_halves], start=small_plan)

    bufs, _ = _split_call("ff_exchange_wait", bufs, wait=(exchange_plan, ex_send, ex_recv), after=small_exchanging)
    ff_shards = [_chip_sum("grad_chip_sum_" + nm, own, ld, pos) for nm, (_, own), ld in zip(["w_ff1", "w_ff2"], ff_sums, bufs[2:])]
    g_w_ff1, g_w_ff2 = _pair_gather("grad_pair_gather_ff", ff_shards)

    g_w_ff1, d_w_ff1, nm_w_ff1, nv_w_ff1 = _adamw("adamw_w_ff1", w_ff1[0], g_w_ff1, m_w_ff1[0], v_w_ff1[0])
    g_w_ff2, d_w_ff2, nm_w_ff2, nv_w_ff2 = _adamw("adamw_w_ff2", w_ff2[0], g_w_ff2, m_w_ff2[0], v_w_ff2[0])
    small_bufs, _ = _split_call("small_exchange_wait", small_bufs, wait=(small_plan, sm_send, sm_recv), after=nv_w_ff2)
    shards = [_chip_sum("grad_chip_sum_" + nm, own, ld, pos) for nm, (_, own), ld in zip(["w_in", "w_out"], pair_sums, small_bufs[2:])]
    g_w_in_t, g_w_out = _pair_gather("grad_pair_gather_small", shards)
    g_w_in, d_w_in, nm_w_in, nv_w_in = (a.T for a in _adamw("adamw_w_in", w_in[0].T, g_w_in_t, m_w_in[0].T, v_w_in[0].T))
    g_w_out, d_w_out, nm_w_out, nv_w_out = _adamw("adamw_w_out", w_out[0], g_w_out, m_w_out[0], v_w_out[0])
    small_grads, small_d, small_nm, small_nv = _adamw_small(
        small_g,
        [v_ln_g, v_ln_b, w_spatial, b_spatial, sinks, ln1_g, ln1_b, ln2_g, ln2_b],
        [m_v_ln_g, m_v_ln_b, m_w_spatial, m_b_spatial, m_sinks, m_ln1_g, m_ln1_b, m_ln2_g, m_ln2_b],
        [v_v_ln_g, v_v_ln_b, v_w_spatial, v_b_spatial, v_sinks, v_ln1_g, v_ln1_b, v_ln2_g, v_ln2_b])

    def with_big(small, w_in_v, w_out_v, w_ff1_v, w_ff2_v):
        g_vg, g_vb, g_ws, g_bs, g_sk, g_1g, g_1b, g_2g, g_2b = small
        return [w_in_v[None], g_vg, g_vb, g_ws, g_bs, g_sk, w_out_v[None], g_1g, g_1b, w_ff1_v[None], w_ff2_v[None], g_2g, g_2b]

    return (
        loss,
        grad_x,
        *with_big(small_grads, g_w_in, g_w_out, g_w_ff1, g_w_ff2),
        *with_big(small_d, d_w_in, d_w_out, d_w_ff1, d_w_ff2),
        *with_big(small_nm, nm_w_in, nm_w_out, nm_w_ff1, nm_w_ff2),
        *with_big(small_nv, nv_w_in, nv_w_out, nv_w_ff1, nv_w_ff2),
    )
```

```python
import math

import jax
import jax.numpy as jnp
from jax import lax
from jax.experimental import pallas as pl
from jax.experimental.pallas import tpu as pltpu

F32 = jnp.float32
BF16 = jnp.bfloat16

D_MODEL = 1024
HEAD_DIM = 64
D_GMLP = 512
D_ATTN = 512
D_KV = 128
D_IN = 2 * D_GMLP + D_ATTN + 2 * D_KV
D_MAIN = 2 * D_GMLP + D_ATTN
N_HEADS = 8
CHUNK = 128
CHUNKS_PER_STEP = 4
ROPE_THETA = 10000.0
D_FF = 4 * D_MODEL
N_FF_BLOCKS = 4
LN_EPS = 1e-5
ALPHA = (2.0 * 1) ** 0.25
NEG_INF = -1e30
SCALE = 1.0 / math.sqrt(HEAD_DIM)

ADAM_LR = 0.001
ADAM_B1 = 0.9
ADAM_B2 = 0.999
ADAM_EPS = 1e-08
ADAM_WD = 0.01
ADAM_STEP = 10

N_CHIPS = 4
LANES = 128
V7X_VMEM_BYTES = 64 * 1024 * 1024
VMEM_LIMIT = V7X_VMEM_BYTES - 8 * 1024 * 1024
TM = 512
TM_FFN = 256
TK = 1024
TK_FF = 1024
SMALL_ROWS = 1152
MESH = pl.DeviceIdType.MESH

NT = (((1,), (1,)), ((), ()))
TN = (((0,), (0,)), ((), ()))


def _dot(a, b, dims=None):
    if dims is None:
        return jnp.dot(a, b, preferred_element_type=F32)
    return lax.dot_general(a, b, dims, preferred_element_type=F32)


def _params(semantics=None):
    return pltpu.CompilerParams(dimension_semantics=semantics, vmem_limit_bytes=VMEM_LIMIT)


def _const_spec(shape, single_buffer=False):
    zeros = (0,) * len(shape)
    if single_buffer:
        return pl.BlockSpec(shape, lambda *_: zeros, pipeline_mode=pl.Buffered(1))
    return pl.BlockSpec(shape, lambda *_: zeros)


def _row_spec(rows, cols):
    return pl.BlockSpec((rows, cols), lambda i: (i, 0))


def _after(dep, body, in_specs, operands):
    if dep is None:
        return body, list(in_specs), list(operands)
    return (lambda dep_ref, *refs: body(*refs)), [pl.BlockSpec(memory_space=pl.ANY)] + list(in_specs), [dep] + list(operands)


def _gelu(x):
    k = math.sqrt(2.0 / math.pi)
    return 0.5 * x * (1.0 + jnp.tanh(k * (x + 0.044715 * (x * x * x))))


def _gelu_and_grad(x):
    k = math.sqrt(2.0 / math.pi)
    x2 = x * x
    t = jnp.tanh(k * (x + 0.044715 * (x2 * x)))
    g = 0.5 * x * (1.0 + t)
    dg = 0.5 * (1.0 + t) + 0.5 * x * (1.0 - t * t) * (k * (1.0 + 3.0 * 0.044715 * x2))
    return g, dg


def _layer_norm_stats(z):
    mu = jnp.mean(z, axis=-1, keepdims=True)
    zc = z - mu
    var = jnp.mean(zc * zc, axis=-1, keepdims=True)
    rstd = lax.rsqrt(var + LN_EPS)
    return zc * rstd, rstd


def _layer_norm_bwd(dxhat, xhat, rstd):
    m1 = jnp.mean(dxhat, axis=-1, keepdims=True)
    m2 = jnp.mean(dxhat * xhat, axis=-1, keepdims=True)
    return rstd * (dxhat - m1 - xhat * m2)


def _rotate_half(t):
    n = t.shape[1]
    lane = lax.broadcasted_iota(jnp.int32, t.shape, 1)
    first = (lane & (HEAD_DIM // 2)) == 0
    return jnp.where(first, -pltpu.roll(t, n - HEAD_DIM // 2, 1), pltpu.roll(t, HEAD_DIM // 2, 1))


def _rope(t, cos, sin):
    return t * cos + _rotate_half(t) * sin


def _rope_transposed(g, cos, sin):
    return g * cos - _rotate_half(g * sin)


def _lane_tile(a, reps):
    return jnp.tile(a, (1, reps)) if reps > 1 else a


def _in_proj(x, w_in_t, cos, sin, dep=None):
    t = x.shape[0]

    def body(x_ref, w_ref, cos_ref, sin_ref, u_ref, vg_ref, q_ref, k_ref, va_ref):
        xb = x_ref[...].astype(BF16)
        u_ref[...] = _dot(xb, w_ref[0:D_GMLP, :], NT)
        vg_ref[...] = _dot(xb, w_ref[D_GMLP : 2 * D_GMLP, :], NT)
        q = _dot(xb, w_ref[2 * D_GMLP : D_MAIN, :], NT)
        k = _dot(xb, w_ref[D_MAIN : D_MAIN + D_KV, :], NT)
        va_ref[...] = _dot(xb, w_ref[D_MAIN + D_KV : D_IN, :], NT).astype(BF16)
        c, s = cos_ref[...], sin_ref[...]
        q_ref[...] = _rope(q, _lane_tile(c, D_ATTN // LANES), _lane_tile(s, D_ATTN // LANES)).astype(BF16)
        k_ref[...] = _rope(k, c, s).astype(BF16)

    body, in_specs, operands = _after(
        dep, body, [_row_spec(TM, D_MODEL), _const_spec((D_IN, D_MODEL)), _row_spec(TM, LANES), _row_spec(TM, LANES)], [x, w_in_t, cos, sin])
    return pl.pallas_call(
        body,
        name="in_proj",
        grid=(t // TM,),
        in_specs=in_specs,
        out_specs=[_row_spec(TM, D_GMLP), _row_spec(TM, D_GMLP), _row_spec(TM, D_ATTN), _row_spec(TM, D_KV), _row_spec(TM, D_KV)],
        out_shape=[
            jax.ShapeDtypeStruct((t, D_GMLP), F32),
            jax.ShapeDtypeStruct((t, D_GMLP), F32),
            jax.ShapeDtypeStruct((t, D_ATTN), BF16),
            jax.ShapeDtypeStruct((t, D_KV), BF16),
            jax.ShapeDtypeStruct((t, D_KV), BF16),
        ],
        compiler_params=_params(("parallel",)),
    )(*operands)


def _step_rows(i):
    return (i, 0)


def _chunk_before_step(i):
    return (jnp.maximum(CHUNKS_PER_STEP * i - 1, 0), 0)


def _chunk_specs():
    step = CHUNKS_PER_STEP * CHUNK
    return [
        pl.BlockSpec((step, D_GMLP), _step_rows),
        pl.BlockSpec((step, D_GMLP), _step_rows),
        pl.BlockSpec((step, D_ATTN), _step_rows),
        pl.BlockSpec((step, D_KV), _step_rows),
        pl.BlockSpec((CHUNK, D_KV), _chunk_before_step),
        pl.BlockSpec((step, D_KV), _step_rows),
        pl.BlockSpec((CHUNK, D_KV), _chunk_before_step),
    ]


def _half_lane_masks(rows):
    lane = lax.broadcasted_iota(jnp.int32, (rows, LANES), 1)
    return lane < HEAD_DIM


def _kv_variants(kv2):
    left = _half_lane_masks(kv2.shape[0])
    f = kv2.astype(F32)
    swapped = pltpu.roll(f, HEAD_DIM, 1)
    zero = jnp.zeros_like(f)
    g0 = (jnp.where(left, f, zero).astype(BF16), jnp.where(left, zero, swapped).astype(BF16))
    g1 = (jnp.where(left, swapped, zero).astype(BF16), jnp.where(left, zero, f).astype(BF16))
    return (g0, g1)


def _band_mask(i, heads=1):
    row = lax.broadcasted_iota(jnp.int32, (heads * CHUNK, 2 * CHUNK), 0) & (CHUNK - 1)
    col = lax.broadcasted_iota(jnp.int32, (heads * CHUNK, 2 * CHUNK), 1)
    no_prev = jnp.where(i > 0, 0, 4 * CHUNK)
    in_prev = jnp.logical_and(col < CHUNK, (col - row) > no_prev)
    in_cur = jnp.logical_and(col >= CHUNK, (col - CHUNK) <= row)
    return jnp.logical_or(in_prev, in_cur)


def _causal_mask():
    row = lax.broadcasted_iota(jnp.int32, (CHUNK, CHUNK), 0)
    col = lax.broadcasted_iota(jnp.int32, (CHUNK, CHUNK), 1)
    return col <= row


def _store_spatial_weights(w_ref, wcat_ref, wcat_t_ref=None):
    causal = _causal_mask()
    for p in range(D_GMLP // LANES):
        wl = jnp.where(causal, w_ref[2 * p], 0.0)
        wr = jnp.where(causal, w_ref[2 * p + 1], 0.0)
        wcat_ref[p] = jnp.concatenate([wl, wr], axis=1).astype(BF16)
        if wcat_t_ref is not None:
            wcat_t_ref[p] = jnp.concatenate([wl.T, wr.T], axis=1).astype(BF16)


def _pair_stack(xp, left):
    return jnp.concatenate([jnp.where(left, xp, 0.0), jnp.where(left, 0.0, xp)], axis=0).astype(BF16)


def _mixer_fwd(u, vg, q, k, va, v_ln_g, v_ln_b, w_spatial, bias_full, sinks, dep=None):
    t = u.shape[0]

    def body(u_ref, vg_ref, q_ref, kc_ref, kp_ref, vc_ref, vp_ref, g_ref, b_ref, w_ref, bias_ref, sink_ref, cat_ref, wcat):
        i = pl.program_id(0)
        left = _half_lane_masks(CHUNK)

        @pl.when(i == 0)
        def _():
            _store_spatial_weights(w_ref, wcat)

        heads = range(N_HEADS)
        pair_cols = [slice(p * LANES, (p + 1) * LANES) for p in range(D_GMLP // LANES)]
        sinks_h = [sink_ref[h] for h in heads]
        for c in range(CHUNKS_PER_STEP):
            rows = slice(c * CHUNK, (c + 1) * CHUNK)
            before = slice((c - 1) * CHUNK, c * CHUNK)
            k_prev = kp_ref[...] if c == 0 else kc_ref[before, :]
            v_prev = vp_ref[...] if c == 0 else vc_ref[before, :]
            k_var = _kv_variants(jnp.concatenate([k_prev, kc_ref[rows, :]], axis=0))
            v_var = _kv_variants(jnp.concatenate([v_prev, vc_ref[rows, :]], axis=0))
            scores = [_dot(q_ref[rows, pair_cols[h // 2]], k_var[h // 4][h % 2], NT) for h in heads]

            ug = _gelu(u_ref[rows, :])
            xhat, _ = _layer_norm_stats(_gelu(vg_ref[rows, :]))
            vgl = xhat * g_ref[...] + b_ref[...]
            mixed = [_dot(wcat[p], _pair_stack(vgl[:, cols], left)) for p, cols in enumerate(pair_cols)]

            valid = _band_mask(CHUNKS_PER_STEP * i + c)
            masked = [jnp.where(valid, scores[h] * SCALE, NEG_INF) for h in heads]
            maxes = [jnp.maximum(jnp.max(masked[h], axis=1, keepdims=True), sinks_h[h]) for h in heads]
            exps = [jnp.exp(masked[h] - maxes[h]) for h in heads]
            invs = [1.0 / (jnp.sum(exps[h], axis=1, keepdims=True) + jnp.exp(sinks_h[h] - maxes[h])) for h in heads]
            probs = [(exps[h] * invs[h]).astype(BF16) for h in heads]
            for p, cols in enumerate(pair_cols):
                cat_ref[rows, cols] = (ug[:, cols] * (mixed[p] + bias_ref[:, cols])).astype(BF16)
            for p in range(D_ATTN // LANES):
                out = _dot(probs[2 * p], v_var[p // 2][0]) + _dot(probs[2 * p + 1], v_var[p // 2][1])
                cat_ref[rows, D_GMLP + p * LANES : D_GMLP + (p + 1) * LANES] = out.astype(BF16)

    in_specs = _chunk_specs() + [
        _const_spec((1, D_GMLP)),
        _const_spec((1, D_GMLP)),
        _const_spec((N_HEADS, CHUNK, CHUNK)),
        _const_spec((CHUNK, D_GMLP)),
        pl.BlockSpec(memory_space=pltpu.SMEM),
    ]
    body, in_specs, operands = _after(dep, body, in_specs, [u, vg, q, k, k, va, va, v_ln_g, v_ln_b, w_spatial, bias_full, sinks])
    return pl.pallas_call(
        body,
        name="mixer_fwd",
        grid=(t // (CHUNKS_PER_STEP * CHUNK),),
        in_specs=in_specs,
        out_specs=pl.BlockSpec((CHUNKS_PER_STEP * CHUNK, D_MODEL), lambda i: (i, 0)),
        out_shape=jax.ShapeDtypeStruct((t, D_MODEL), BF16),
        scratch_shapes=[pltpu.VMEM((D_GMLP // LANES, CHUNK, 2 * CHUNK), BF16)],
        compiler_params=_params(("arbitrary",)),
    )(*operands)


def _ffn_fwd_loss(cat, x, w_out, ln1_g, ln1_b, w1, w2, ln2_g, ln2_b, target):
    t = x.shape[0]

    def body(cat_ref, x_ref, wo_ref, g1_ref, b1_ref, w1_ref, w2_ref, g2_ref, b2_ref, tgt_ref,
             xh_ref, rstd_ref, x1b_ref, r_ref, dz2_ref, dz2b_ref, dg2_ref, db2_ref, sq_ref):
        @pl.when(pl.program_id(0) == 0)
        def _():
            dg2_ref[...] = jnp.zeros_like(dg2_ref)
            db2_ref[...] = jnp.zeros_like(db2_ref)
            sq_ref[...] = jnp.zeros_like(sq_ref)

        xhat1, rstd1 = _layer_norm_stats(ALPHA * x_ref[...] + _dot(cat_ref[...], wo_ref[...]))
        xh_ref[...] = xhat1
        rstd_ref[...] = rstd1
        x1 = xhat1 * g1_ref[...] + b1_ref[...]
        x1b = x1.astype(BF16)
        x1b_ref[...] = x1b
        ff = jnp.zeros((TM_FFN, D_MODEL), F32)
        for j in range(N_FF_BLOCKS):
            r = jnp.maximum(_dot(x1b, w1_ref[j]), 0.0)
            r_ref[:, j * D_MODEL : (j + 1) * D_MODEL] = r.astype(BF16)
            ff = ff + _dot((r * r).astype(BF16), w2_ref[j])
        xhat2, rstd2 = _layer_norm_stats(ALPHA * x1 + ff)
        err = xhat2 * g2_ref[...] + b2_ref[...] - tgt_ref[...]
        sq_ref[...] += jnp.sum(err * err, axis=0, keepdims=True)
        dy = err * (1.0 / D_MODEL)
        dg2_ref[...] += jnp.sum(dy * xhat2, axis=0, keepdims=True)
        db2_ref[...] += jnp.sum(dy, axis=0, keepdims=True)
        dz2 = _layer_norm_bwd(dy * g2_ref[...], xhat2, rstd2)
        dz2_ref[...] = dz2
        dz2b_ref[...] = dz2.astype(BF16)

    vec = _const_spec((1, D_MODEL))
    tile = _row_spec(TM_FFN, D_MODEL)
    wspec = _const_spec((N_FF_BLOCKS, D_MODEL, D_MODEL), single_buffer=True)
    return pl.pallas_call(
        body,
        name="ffn_fwd_loss",
        grid=(t // TM_FFN,),
        in_specs=[tile, tile, _const_spec((D_MODEL, D_MODEL), single_buffer=True), vec, vec, wspec, wspec, vec, vec, tile],
        out_specs=[tile, _row_spec(TM_FFN, 1), tile, _row_spec(TM_FFN, D_FF), tile, tile, vec, vec, vec],
        out_shape=[
            jax.ShapeDtypeStruct((t, D_MODEL), F32),
            jax.ShapeDtypeStruct((t, 1), F32),
            jax.ShapeDtypeStruct((t, D_MODEL), BF16),
            jax.ShapeDtypeStruct((t, D_FF), BF16),
            jax.ShapeDtypeStruct((t, D_MODEL), F32),
            jax.ShapeDtypeStruct((t, D_MODEL), BF16),
            jax.ShapeDtypeStruct((1, D_MODEL), F32),
            jax.ShapeDtypeStruct((1, D_MODEL), F32),
            jax.ShapeDtypeStruct((1, D_MODEL), F32),
        ],
        compiler_params=_params(("arbitrary",)),
    )(cat, x, w_out, ln1_g, ln1_b, w1, w2, ln2_g, ln2_b, target)


def _ffn_bwd_ln1(dz2, r, x1b, xhat1, rstd1, ln1_g, w1, w2, w_out, dep=None):
    t = dz2.shape[0]

    def body(dz2_ref, r_ref, x1b_ref, xh_ref, rstd_ref, g1_ref, w1_ref, w2_ref, wo_ref, gw1_ref, dz1_ref, dz1b_ref, dcat_ref, dg1_ref, db1_ref):
        @pl.when(pl.program_id(0) == 0)
        def _():
            dg1_ref[...] = jnp.zeros_like(dg1_ref)
            db1_ref[...] = jnp.zeros_like(db1_ref)
            gw1_ref[...] = jnp.zeros_like(gw1_ref)

        dz2 = dz2_ref[...]
        dz2b = dz2.astype(BF16)
        x1_t = x1b_ref[...].astype(F32).T.astype(BF16)
        dx1 = ALPHA * dz2
        for j in range(N_FF_BLOCKS):
            cols = slice(j * D_MODEL, (j + 1) * D_MODEL)
            dpre = (_dot(dz2b, w2_ref[j], NT) * (2.0 * r_ref[:, cols].astype(F32))).astype(BF16)
            gw1_ref[cols, :] += _dot(x1_t, dpre)
            dx1 = dx1 + _dot(dpre, w1_ref[j], NT)
        xhat1 = xh_ref[...]
        dg1_ref[...] += jnp.sum(dx1 * xhat1, axis=0, keepdims=True)
        db1_ref[...] += jnp.sum(dx1, axis=0, keepdims=True)
        dz1 = _layer_norm_bwd(dx1 * g1_ref[...], xhat1, rstd_ref[...])
        dz1_ref[...] = dz1
        dz1b = dz1.astype(BF16)
        dz1b_ref[...] = dz1b
        dcat_ref[...] = _dot(dz1b, wo_ref[...], NT).astype(BF16)

    vec = _const_spec((1, D_MODEL))
    tile = _row_spec(TM_FFN, D_MODEL)
    wspec = _const_spec((N_FF_BLOCKS, D_MODEL, D_MODEL), single_buffer=True)
    body, in_specs, operands = _after(
        dep, body,
        [tile, _row_spec(TM_FFN, D_FF), tile, tile, _row_spec(TM_FFN, 1), vec, wspec, wspec, _const_spec((D_MODEL, D_MODEL), single_buffer=True)],
        [dz2, r, x1b, xhat1, rstd1, ln1_g, w1, w2, w_out])
    return pl.pallas_call(
        body,
        name="ffn_bwd_ln1",
        grid=(t // TM_FFN,),
        in_specs=in_specs,
        out_specs=[_const_spec((D_FF, D_MODEL), single_buffer=True), tile, tile, tile, vec, vec],
        out_shape=[
            jax.ShapeDtypeStruct((D_FF, D_MODEL), F32),
            jax.ShapeDtypeStruct((t, D_MODEL), F32),
            jax.ShapeDtypeStruct((t, D_MODEL), BF16),
            jax.ShapeDtypeStruct((t, D_MODEL), BF16),
            jax.ShapeDtypeStruct((1, D_MODEL), F32),
            jax.ShapeDtypeStruct((1, D_MODEL), F32),
        ],
        compiler_params=_params(("arbitrary",)),
    )(*operands)


def _mixer_bwd(u, vg, q, k, va, dcat, cos, sin, v_ln_g, v_ln_b, w_spatial, bias_full, sinks, dep=None):
    t = u.shape[0]
    n_chunks = t // CHUNK

    def body(u_ref, vg_ref, q_ref, kc_ref, kp_ref, vc_ref, vp_ref, dcat_ref, cosc_ref, sinc_ref, cosp_ref, sinp_ref,
             g_ref, b_ref, w_ref, bias_ref, sink_ref,
             dmain_ref, dkv_ref, dg_ref, db_ref, dw_ref, dbs_ref, dsink_ref, dmix_acc, wcat, wcat_t):
        i = pl.program_id(0)
        left = _half_lane_masks(CHUNK)
        lane = lax.broadcasted_iota(jnp.int32, (CHUNK, LANES), 1)
        n_pairs = D_GMLP // LANES

        @pl.when(i == 0)
        def _():
            dg_ref[...] = jnp.zeros_like(dg_ref)
            db_ref[...] = jnp.zeros_like(db_ref)
            dw_ref[...] = jnp.zeros_like(dw_ref)
            dsink_ref[...] = jnp.zeros_like(dsink_ref)
            dmix_acc[...] = jnp.zeros_like(dmix_acc)
            _store_spatial_weights(w_ref, wcat, wcat_t)

        n_qpairs = D_ATTN // LANES
        heads = range(N_HEADS)
        pair_cols = [slice(p * LANES, (p + 1) * LANES) for p in range(n_pairs)]
        sinks_h = [sink_ref[h] for h in heads]
        gain = g_ref[...]
        causal = _causal_mask()
        lane_row = lax.broadcasted_iota(jnp.int32, (1, LANES), 1)
        heads_per_group = N_HEADS // 2

        def group_grad_t(lhs_t, rhs_heads):
            parts = []
            for g in range(2):
                group = range(g * heads_per_group, (g + 1) * heads_per_group)
                lhs = jnp.concatenate([lhs_t[h * HEAD_DIM : (h + 1) * HEAD_DIM] for h in group], axis=1)
                parts.append(_dot(lhs, jnp.concatenate([rhs_heads[h] for h in group], axis=0)))
            return jnp.concatenate(parts, axis=0)

        for c in range(CHUNKS_PER_STEP):
            chunk = CHUNKS_PER_STEP * i + c
            rows = slice(c * CHUNK, (c + 1) * CHUNK)
            before = slice((c - 1) * CHUNK, c * CHUNK)

            k_prev = kp_ref[...] if c == 0 else kc_ref[before, :]
            v_prev = vp_ref[...] if c == 0 else vc_ref[before, :]
            k_var = _kv_variants(jnp.concatenate([k_prev, kc_ref[rows, :]], axis=0))
            v_var = _kv_variants(jnp.concatenate([v_prev, vc_ref[rows, :]], axis=0))
            q_pairs = [q_ref[rows, cols] for cols in pair_cols]
            do_all = dcat_ref[rows, D_GMLP:D_MODEL]
            do_pairs = [do_all[:, cols] for cols in pair_cols]
            scores = [_dot(q_pairs[h // 2], k_var[h // 4][h % 2], NT) for h in heads]
            dprobs = [_dot(do_pairs[h // 2], v_var[h // 4][h % 2], NT) for h in heads]
            q_t = q_ref[rows, :].astype(F32).T.astype(BF16)
            do_t = do_all.astype(F32).T.astype(BF16)

            ug, dug_du = _gelu_and_grad(u_ref[rows, :])
            gv, dgv_dv = _gelu_and_grad(vg_ref[rows, :])
            xhat, rstd = _layer_norm_stats(gv)
            vgl = xhat * gain + b_ref[...]
            mixed = [_dot(wcat[p], _pair_stack(vgl[:, cols], left)) for p, cols in enumerate(pair_cols)]

            valid = _band_mask(chunk)
            masked = [jnp.where(valid, scores[h] * SCALE, NEG_INF) for h in heads]
            maxes = [jnp.maximum(jnp.max(masked[h], axis=1, keepdims=True), sinks_h[h]) for h in heads]
            exps = [jnp.exp(masked[h] - maxes[h]) for h in heads]
            exp_sinks = [jnp.exp(sinks_h[h] - maxes[h]) for h in heads]
            invs = [1.0 / (jnp.sum(exps[h], axis=1, keepdims=True) + exp_sinks[h]) for h in heads]
            probs = [exps[h] * invs[h] for h in heads]
            dsums = [jnp.sum(probs[h] * dprobs[h], axis=1, keepdims=True) for h in heads]
            ds_b = [(probs[h] * (dprobs[h] - dsums[h]) * SCALE).astype(BF16) for h in heads]
            probs_b = [probs[h].astype(BF16) for h in heads]

            dm_stacks = []
            for p, cols in enumerate(pair_cols):
                da = dcat_ref[rows, cols].astype(F32)
                dmain_ref[rows, cols] = (da * (mixed[p] + bias_ref[:, cols]) * dug_du[:, cols]).astype(BF16)
                dmixed = da * ug[:, cols]
                dmix_acc[:, cols] += dmixed
                dm_stacks.append(_pair_stack(dmixed, left))

            dq_all = jnp.concatenate(
                [_dot(ds_b[2 * p], k_var[p // 2][0]) + _dot(ds_b[2 * p + 1], k_var[p // 2][1]) for p in range(n_qpairs)], axis=1)
            dk2_t = group_grad_t(q_t, ds_b)
            dv2_t = group_grad_t(do_t, probs_b)

            for p, cols in enumerate(pair_cols):
                dw_pair = _dot(dm_stacks[p], vgl[:, cols].astype(BF16), NT)
                dw_ref[2 * p] += jnp.where(causal, dw_pair[:CHUNK], 0.0)
                dw_ref[2 * p + 1] += jnp.where(causal, dw_pair[CHUNK:], 0.0)
            dvgl = jnp.concatenate([_dot(wcat_t[p], dm_stacks[p]) for p in range(n_pairs)], axis=1)

            dsink_row = jnp.zeros((1, LANES), F32)
            for h in heads:
                d_sink = -jnp.sum(exp_sinks[h] * invs[h] * dsums[h], axis=0, keepdims=True)
                dsink_row = dsink_row + jnp.where(lane_row == h, d_sink, 0.0)
            dsink_ref[0:1, :] += dsink_row
            cos_c, sin_c = cosc_ref[rows, :], sinc_ref[rows, :]
            cos_p = cosp_ref[...] if c == 0 else cosc_ref[before, :]
            sin_p = sinp_ref[...] if c == 0 else sinc_ref[before, :]
            dmain_ref[rows, 2 * D_GMLP : D_MAIN] = _rope_transposed(dq_all, _lane_tile(cos_c, n_qpairs), _lane_tile(sin_c, n_qpairs)).astype(BF16)
            dk2 = dk2_t.T
            dv2 = dv2_t.T
            cur = pl.ds(pl.multiple_of(chunk * CHUNK, CHUNK), CHUNK)
            dkv_ref[cur, 0:D_KV] = _rope_transposed(dk2[CHUNK:], cos_c, sin_c)
            dkv_ref[cur, D_KV : 2 * D_KV] = dv2[CHUNK:]
            prev = pl.ds(pl.multiple_of(jnp.maximum(chunk - 1, 0) * CHUNK, CHUNK), CHUNK)
            dkv_ref[prev, 0:D_KV] += _rope_transposed(dk2[:CHUNK], cos_p, sin_p)
            dkv_ref[prev, D_KV : 2 * D_KV] += dv2[:CHUNK]

            dg_ref[...] += jnp.sum(dvgl * xhat, axis=0, keepdims=True)
            db_ref[...] += jnp.sum(dvgl, axis=0, keepdims=True)
            dgv = _layer_norm_bwd(dvgl * gain, xhat, rstd)
            dmain_ref[rows, D_GMLP : 2 * D_GMLP] = (dgv * dgv_dv).astype(BF16)

        @pl.when(i == n_chunks // CHUNKS_PER_STEP - 1)
        def _():
            tile = jnp.zeros((CHUNK, LANES), F32)
            for p, cols in enumerate(pair_cols):
                dm = dmix_acc[:, cols]
                sl = jnp.sum(jnp.where(left, dm, 0.0), axis=1, keepdims=True)
                sr = jnp.sum(jnp.where(left, 0.0, dm), axis=1, keepdims=True)
                tile = jnp.where(lane == 2 * p, sl, tile)
                tile = jnp.where(lane == 2 * p + 1, sr, tile)
            dbs_ref[...] = tile

    step = CHUNKS_PER_STEP * CHUNK
    in_specs = _chunk_specs() + [
        pl.BlockSpec((step, D_MODEL), _step_rows),
        pl.BlockSpec((step, LANES), _step_rows),
        pl.BlockSpec((step, LANES), _step_rows),
        pl.BlockSpec((CHUNK, LANES), _chunk_before_step),
        pl.BlockSpec((CHUNK, LANES), _chunk_before_step),
        _const_spec((1, D_GMLP)),
        _const_spec((1, D_GMLP)),
        _const_spec((N_HEADS, CHUNK, CHUNK)),
        _const_spec((CHUNK, D_GMLP)),
        pl.BlockSpec(memory_space=pltpu.SMEM),
    ]
    body, in_specs, operands = _after(
        dep, body, in_specs, [u, vg, q, k, k, va, va, dcat, cos, sin, cos, sin, v_ln_g, v_ln_b, w_spatial, bias_full, sinks])
    return pl.pallas_call(
        body,
        name="mixer_bwd",
        grid=(n_chunks // CHUNKS_PER_STEP,),
        in_specs=in_specs,
        out_specs=[
            pl.BlockSpec((step, D_MAIN), _step_rows),
            _const_spec((t, 2 * D_KV)),
            _const_spec((1, D_GMLP)),
            _const_spec((1, D_GMLP)),
            _const_spec((N_HEADS, CHUNK, CHUNK)),
            _const_spec((CHUNK, LANES)),
            _const_spec((8, LANES)),
        ],
        out_shape=[
            jax.ShapeDtypeStruct((t, D_MAIN), BF16),
            jax.ShapeDtypeStruct((t, 2 * D_KV), F32),
            jax.ShapeDtypeStruct((1, D_GMLP), F32),
            jax.ShapeDtypeStruct((1, D_GMLP), F32),
            jax.ShapeDtypeStruct((N_HEADS, CHUNK, CHUNK), F32),
            jax.ShapeDtypeStruct((CHUNK, LANES), F32),
            jax.ShapeDtypeStruct((8, LANES), F32),
        ],
        scratch_shapes=[
            pltpu.VMEM((CHUNK, D_GMLP), F32),
            pltpu.VMEM((D_GMLP // LANES, CHUNK, 2 * CHUNK), BF16),
            pltpu.VMEM((D_GMLP // LANES, CHUNK, 2 * CHUNK), BF16),
        ],
        compiler_params=_params(("arbitrary",)),
    )(*operands)


def _grad_x(dh_main, dkv, dz1, w_in_t, dep=None):
    t = dz1.shape[0]

    def body(dm_ref, dkv_ref, dz1_ref, w_ref, gx_ref):
        acc = ALPHA * dz1_ref[...] + _dot(dm_ref[...], w_ref[0:D_MAIN, :])
        gx_ref[...] = acc + _dot(dkv_ref[...].astype(BF16), w_ref[D_MAIN:D_IN, :])

    body, in_specs, operands = _after(
        dep, body, [_row_spec(TM, D_MAIN), _row_spec(TM, 2 * D_KV), _row_spec(TM, D_MODEL), _const_spec((D_IN, D_MODEL))], [dh_main, dkv, dz1, w_in_t])
    return pl.pallas_call(
        body,
        name="grad_x",
        grid=(t // TM,),
        in_specs=in_specs,
        out_specs=_row_spec(TM, D_MODEL),
        out_shape=jax.ShapeDtypeStruct((t, D_MODEL), F32),
        compiler_params=_params(("parallel",)),
    )(*operands)


def _token_contraction(name, out_rows, tk, in_arrays, contributions, dep=None):
    t = in_arrays[0].shape[0]

    def body(*refs):
        out_ref = refs[-1]

        @pl.when(pl.program_id(0) == 0)
        def _():
            out_ref[...] = jnp.zeros_like(out_ref)

        for row0, a, b in contributions(*refs[:-1]):
            out_ref[row0 : row0 + a.shape[1], :] += _dot(a, b, TN)

    in_specs = [_row_spec(tk, a.shape[1]) for a in in_arrays]
    body, in_specs, operands = _after(dep, body, in_specs, in_arrays)
    return pl.pallas_call(
        body,
        name=name,
        grid=(t // tk,),
        in_specs=in_specs,
        out_specs=_const_spec((out_rows, D_MODEL), single_buffer=True),
        out_shape=jax.ShapeDtypeStruct((out_rows, D_MODEL), F32),
        compiler_params=_params(("arbitrary",)),
    )(*operands)


def _grad_w_out(cat, dz1b, dep=None):
    def contributions(cat_ref, dz1_ref):
        return [(0, cat_ref[...], dz1_ref[...])]

    return _token_contraction("grad_w_out", D_MODEL, TK, [cat, dz1b], contributions, dep)


def _grad_w_ff2(r, dz2b):
    def contributions(r_ref, dz2_ref):
        dz2 = dz2_ref[...]
        out = []
        for j in range(N_FF_BLOCKS):
            rf = r_ref[:, j * D_MODEL : (j + 1) * D_MODEL].astype(F32)
            out.append((j * D_MODEL, (rf * rf).astype(BF16), dz2))
        return out

    return _token_contraction("grad_w_ff2", D_FF, TK_FF, [r, dz2b], contributions)


ANY = pl.BlockSpec(memory_space=pl.ANY)


def _mesh_position():
    return lax.axis_index("x"), lax.axis_index("y"), lax.axis_index("c")


def _other_chips(x, y):
    return [(1 - x, y), (x, 1 - y), (1 - x, 1 - y)]


def _remote(src, dst, send_sem, recv_sem, device):
    return pltpu.make_async_remote_copy(src_ref=src, dst_ref=dst, send_sem=send_sem, recv_sem=recv_sem, device_id=device, device_id_type=MESH)


def _rows(ref, start, size):
    return ref.at[pl.ds(start, size), :]


def _rope_tables_and_casts(pos_row, inv_freq_row, shards, dep=None):
    t = pos_row.shape[1]
    steps = t // TM
    n = len(shards)

    def body(pos_ref, f_ref, *rest):
        f32_refs, (cos_ref, sin_ref), bf16_refs = rest[:n], rest[n : n + 2], rest[n + 2 :]
        for src, dst in zip(f32_refs, bf16_refs):
            dst[...] = src[...].astype(BF16)
        pos_rows = jnp.broadcast_to(pos_ref[...].astype(F32), (LANES, TM)).T
        ang = pos_rows * f_ref[...]
        cos_ref[...] = jnp.cos(ang)
        sin_ref[...] = jnp.sin(ang)

    shard_specs = [_row_spec(s.shape[0] // steps, s.shape[1]) for s in shards]
    body, in_specs, operands = _after(
        dep, body, [pl.BlockSpec((1, TM), lambda i: (0, i)), _const_spec((1, LANES))] + shard_specs, [pos_row, inv_freq_row, *shards])
    outs = pl.pallas_call(
        body,
        name="rope_tables_and_casts",
        grid=(steps,),
        in_specs=in_specs,
        out_specs=[_row_spec(TM, LANES), _row_spec(TM, LANES)] + shard_specs,
        out_shape=[jax.ShapeDtypeStruct((t, LANES), F32)] * 2 + [jax.ShapeDtypeStruct(s.shape, BF16) for s in shards],
        compiler_params=_params(("parallel",)),
    )(*operands)
    return outs[0], outs[1], list(outs[2:])


def _pair_gather(name, shards):
    n = len(shards)

    def body(*refs):
        outs = refs[n : 2 * n]
        send_sems, recv_sems = refs[2 * n :]
        x, y, c = _mesh_position()
        sibling = (x, y, 1 - c)
        sends = []
        for w in range(n):
            half = shards[w].shape[0] // 2
            mine = _rows(outs[w], c * half, half)
            cp = _remote(mine, mine, send_sems.at[w], recv_sems.at[w], sibling)
            cp.start()
            sends.append(cp)
        for w in range(n):
            half = shards[w].shape[0] // 2
            blk = _rows(outs[w], (1 - c) * half, half)
            _remote(blk, blk, send_sems.at[w], recv_sems.at[w], sibling).wait_recv()
        for cp in sends:
            cp.wait_send()

    return pl.pallas_call(
        body,
        name=name,
        in_specs=[ANY] * n,
        out_specs=[ANY] * n,
        out_shape=[jax.ShapeDtypeStruct(s.shape, s.dtype) for s in shards],
        input_output_aliases={w: w for w in range(n)},
        scratch_shapes=[pltpu.SemaphoreType.DMA((n,)), pltpu.SemaphoreType.DMA((n,))],
    )(*shards)


def _grad_w_in_t_and_small_all_reduce(dh_main, dkv, x, slab):
    t = x.shape[0]
    steps = t // TK
    rows = slab.shape[0]
    part = rows // 8

    def body(dm_ref, dkv_ref, x_ref, slab_ref, grad_ref, sum_ref, landing, reduced, gathered, send_sems, recv_sems):
        k = pl.program_id(0)
        x_, y_, c_ = _mesh_position()
        me = 4 * x_ + 2 * y_ + c_
        flips = [(f >> 2, (f >> 1) & 1, f & 1) for f in range(1, 8)]

        def peer(flip):
            fx, fy, fc = flip
            return (1 - x_ if fx else x_, 1 - y_ if fy else y_, 1 - c_ if fc else c_)

        def part_of(ref, device):
            return ref.at[pl.ds(pl.multiple_of(device * part, 8), part), :]

        def scatter_copies():
            out = []
            for kk, flip in enumerate(flips):
                px, py, pc = peer(flip)
                them = 4 * px + 2 * py + pc
                send = _remote(part_of(slab_ref, them), landing.at[me], send_sems.at[kk], recv_sems.at[kk], (px, py, pc))
                recv = _remote(landing.at[them], landing.at[them], send_sems.at[kk], recv_sems.at[kk], (px, py, pc))
                out.append((send, recv))
            return out

        def gather_copies():
            out = []
            for kk, flip in enumerate(flips):
                px, py, pc = peer(flip)
                them = 4 * px + 2 * py + pc
                send = _remote(reduced, part_of(gathered, me), send_sems.at[7 + kk], recv_sems.at[7 + kk], (px, py, pc))
                recv = _remote(part_of(gathered, them), part_of(gathered, them), send_sems.at[7 + kk], recv_sems.at[7 + kk], (px, py, pc))
                out.append((send, recv))
            return out

        @pl.when(k == 0)
        def _():
            grad_ref[...] = jnp.zeros_like(grad_ref)
            for send, _ in scatter_copies():
                send.start()
            landing[me] = part_of(slab_ref, me)[...]

        @pl.when(k == steps // 2)
        def _():
            for _, recv in scatter_copies():
                recv.wait_recv()
            total = landing[0]
            for s in range(1, 8):
                total = total + landing[s]
            reduced[...] = total
            part_of(gathered, me)[...] = total
            for send, _ in gather_copies():
                send.start()

        xb = x_ref[...].astype(BF16)
        grad_ref[0:D_MAIN, :] += _dot(dm_ref[...], xb, TN)
        grad_ref[D_MAIN:D_IN, :] += _dot(dkv_ref[...].astype(BF16), xb, TN)

        @pl.when(k == steps - 1)
        def _():
            for send, recv in gather_copies():
                recv.wait_recv()
                send.wait_send()
            for send, _ in scatter_copies():
                send.wait_send()
            sum_ref[...] = gathered[...]

    return pl.pallas_call(
        body,
        name="grad_w_in_and_small_all_reduce",
        grid=(steps,),
        in_specs=[_row_spec(TK, D_MAIN), _row_spec(TK, 2 * D_KV), _row_spec(TK, D_MODEL), _const_spec(slab.shape)],
        out_specs=[_const_spec((D_IN, D_MODEL), single_buffer=True), _const_spec(slab.shape)],
        out_shape=[jax.ShapeDtypeStruct((D_IN, D_MODEL), F32), jax.ShapeDtypeStruct(slab.shape, slab.dtype)],
        scratch_shapes=[
            pltpu.VMEM((8, part, LANES), F32),
            pltpu.VMEM((part, LANES), F32),
            pltpu.VMEM(slab.shape, F32),
            pltpu.SemaphoreType.DMA((14,)),
            pltpu.SemaphoreType.DMA((14,)),
        ],
        compiler_params=_params(("arbitrary",)),
    )(dh_main, dkv, x, slab)


HBM = pl.BlockSpec(memory_space=pltpu.HBM)
SEM = pl.BlockSpec(memory_space=pltpu.SEMAPHORE)
DATAFLOW = pltpu.SideEffectType.DATAFLOW_SIDE_EFFECTING
TOKEN = jax.ShapeDtypeStruct((8, LANES), F32)


def _plan_copies(bufs, plan, send_sems, recv_sems):
    out = []
    for i, (src, src_row, dst, dst_row, recv_row, rows, device) in enumerate(plan):
        send = _remote(_rows(bufs[src], src_row, rows), _rows(bufs[dst], dst_row, rows), send_sems.at[i], recv_sems.at[i], device)
        landed = _rows(bufs[dst], recv_row, rows)
        recv = _remote(landed, landed, send_sems.at[i], recv_sems.at[i], device)
        out.append((send, recv))
    return out


def _split_call(name, bufs, wait=None, start=None, after=None):
    n = len(bufs)
    n_in = n + (2 if wait else 0) + (1 if after is not None else 0)
    n_start = len(start(0, 0, 0)) if start else 0

    def body(*refs):
        ins = refs[:n]
        x, y, c = _mesh_position()
        if wait:
            for send, recv in _plan_copies(ins, wait[0](x, y, c), refs[n], refs[n + 1]):
                recv.wait_recv()
                send.wait_send()
        if start:
            for send, _ in _plan_copies(ins, start(x, y, c), refs[n_in + n + 1], refs[n_in + n + 2]):
                send.start()
        token = refs[n_in + n]
        token[...] = jnp.zeros_like(token)

    operands = [pltpu.with_memory_space_constraint(b, pltpu.HBM) for b in bufs]
    in_specs = [HBM] * n
    if wait:
        operands += [wait[1], wait[2]]
        in_specs += [SEM, SEM]
    if after is not None:
        operands.append(after)
        in_specs.append(ANY)
    out_shape = [pltpu.HBM(b.shape, b.dtype) for b in bufs] + [TOKEN]
    out_specs = [HBM] * n + [pl.BlockSpec(memory_space=pltpu.VMEM)]
    if start:
        out_shape += [pltpu.SemaphoreType.DMA((n_start,)), pltpu.SemaphoreType.DMA((n_start,))]
        out_specs += [SEM, SEM]
    outs = pl.pallas_call(
        body,
        name=name,
        in_specs=in_specs,
        out_specs=out_specs,
        out_shape=out_shape,
        input_output_aliases={i: i for i in range(n)},
        compiler_params=pltpu.CompilerParams(has_side_effects=DATAFLOW),
    )(*operands)
    return (list(outs[:n]), outs[n]) + tuple(outs[n + 1 :])


def _direct_gather_plans(shard_rows):
    n = len(shard_rows)

    def direct(x, y, c):
        me = 2 * x + y
        plan = []
        for w, rows in enumerate(shard_rows):
            half = rows // 2
            for px, py in _other_chips(x, y):
                plan.append((w, c * half, n + w, me * rows + c * half, (2 * px + py) * rows + c * half, half, (px, py, c)))
            plan.append((w, 0, n + w, me * rows, me * rows, rows, (x, y, 1 - c)))
        return plan

    def passed_on(x, y, c):
        plan = []
        for w, rows in enumerate(shard_rows):
            half = rows // 2
            for px, py in _other_chips(x, y):
                row = (2 * px + py) * rows
                plan.append((n + w, row + c * half, n + w, row + c * half, row + (1 - c) * half, half, (x, y, 1 - c)))
        return plan

    return direct, passed_on


def _gather_plans(shard_rows):
    n = len(shard_rows)

    def neighbours(x, y):
        return ((1 - x, y), (x, 1 - y))

    def direct(x, y, c):
        me = 2 * x + y
        plan = []
        for w, rows in enumerate(shard_rows):
            half = rows // 2
            for px, py in neighbours(x, y):
                plan.append((w, c * half, n + w, me * rows + c * half, (2 * px + py) * rows + c * half, half, (px, py, c)))
            plan.append((w, 0, n + w, me * rows, me * rows, rows, (x, y, 1 - c)))
        return plan

    def passed_on(x, y, c):
        (xn, yn), diagonal = neighbours(x, y), 2 * (1 - x) + (1 - y)
        relayed = (1 - c) * (2 * xn[0] + xn[1]) + c * (2 * yn[0] + yn[1])
        target = (x * (1 - c) + (1 - x) * c, (1 - y) * (1 - c) + y * c, c)
        plan = []
        for w, rows in enumerate(shard_rows):
            half = rows // 2
            for px, py in (xn, yn):
                row = (2 * px + py) * rows
                plan.append((n + w, row + c * half, n + w, row + c * half, row + (1 - c) * half, half, (x, y, 1 - c)))
            plan.append((n + w, relayed * rows + c * half, n + w, relayed * rows + c * half, diagonal * rows + c * half, half, target))
        return plan

    def diagonal_passed_on(x, y, c):
        plan = []
        for w, rows in enumerate(shard_rows):
            half = rows // 2
            row = (2 * (1 - x) + (1 - y)) * rows
            plan.append((n + w, row + c * half, n + w, row + c * half, row + (1 - c) * half, half, (x, y, 1 - c)))
        return plan

    return direct, passed_on, diagonal_passed_on


def _swap_plan(block_rows):
    n = len(block_rows)

    def plan_fn(x, y, c):
        plan = []
        for w, rows in enumerate(block_rows):
            half = rows // 2
            for j in range(N_CHIPS):
                plan.append((w, j * rows + (1 - c) * half, n + w, j * half, j * half, half, (x, y, 1 - c)))
        return plan

    return plan_fn


def _exchange_plan(halves):
    n = len(halves)

    def plan_fn(x, y, c):
        plan = []
        for w, half in enumerate(halves):
            for kk, (px, py) in enumerate(_other_chips(x, y)):
                plan.append((w, (2 * px + py) * half, n + w, kk * half, kk * half, half, (px, py, c)))
        return plan

    return plan_fn


def _landing(rows, cols, dtype):
    return lax.empty((rows, cols), dtype)


def _row_tile(rows, cap=512):
    best = 8
    for cand in range(8, cap + 1, 8):
        if rows % cand == 0:
            best = cand
    return best


def _pair_sum(name, grad, theirs, pos):
    half = theirs.shape[0] // N_CHIPS
    cols = theirs.shape[1]
    tile = _row_tile(half)
    steps = half // tile

    def body(pos_ref, g_ref, t_ref, p_ref, own_ref):
        total = g_ref[...] + t_ref[...]
        p_ref[...] = total.astype(BF16)

        @pl.when(pl.program_id(1) == pos_ref[1])
        def _():
            own_ref[...] = total

    return pl.pallas_call(
        body,
        name=name,
        grid_spec=pltpu.PrefetchScalarGridSpec(
            num_scalar_prefetch=1,
            grid=(steps, N_CHIPS),
            in_specs=[
                pl.BlockSpec((tile, cols), lambda i, j, pos: ((2 * j + pos[0]) * steps + i, 0)),
                pl.BlockSpec((tile, cols), lambda i, j, pos: (j * steps + i, 0)),
            ],
            out_specs=[
                pl.BlockSpec((tile, cols), lambda i, j, pos: (j * steps + i, 0)),
                pl.BlockSpec((tile, cols), lambda i, j, pos: (i, 0)),
            ],
        ),
        out_shape=[jax.ShapeDtypeStruct((N_CHIPS * half, cols), BF16), jax.ShapeDtypeStruct((half, cols), F32)],
        compiler_params=_params(("parallel", "arbitrary")),
    )(pos, grad, theirs)


def _chip_sum(name, own, landed, pos):
    half, cols = own.shape
    tile = _row_tile(half)
    steps = half // tile

    def body(pos_ref, own_ref, l0, l1, l2, o_ref):
        o_ref[...] = ((own_ref[...] + l0[...].astype(F32)) + l1[...].astype(F32)) + l2[...].astype(F32)

    landed_specs = [pl.BlockSpec((tile, cols), lambda i, pos, _k=k: (_k * steps + i, 0)) for k in range(N_CHIPS - 1)]
    return pl.pallas_call(
        body,
        name=name,
        grid_spec=pltpu.PrefetchScalarGridSpec(
            num_scalar_prefetch=1,
            grid=(steps,),
            in_specs=[pl.BlockSpec((tile, cols), lambda i, pos: (i, 0))] + landed_specs,
            out_specs=pl.BlockSpec((tile, cols), lambda i, pos: (pos[0] * steps + i, 0)),
        ),
        out_shape=jax.ShapeDtypeStruct((2 * half, cols), F32),
        compiler_params=_params(("parallel",)),
    )(pos, own, landed, landed, landed)


def _adamw(name, w, g, m, v):
    rows, cols = w.shape
    tile = rows if rows * cols <= 256 * 1024 else _row_tile(rows)

    def body(w_ref, g_ref, m_ref, v_ref, g_out_ref, d_ref, nm_ref, nv_ref):
        g = g_ref[...]
        g_out_ref[...] = g
        nm = ADAM_B1 * m_ref[...] + (1.0 - ADAM_B1) * g
        nv = ADAM_B2 * v_ref[...] + (1.0 - ADAM_B2) * (g * g)
        m_hat = nm / (1.0 - ADAM_B1**ADAM_STEP)
        v_hat = nv / (1.0 - ADAM_B2**ADAM_STEP)
        d_ref[...] = -ADAM_LR * (m_hat / (jnp.sqrt(v_hat) + ADAM_EPS) + ADAM_WD * w_ref[...])
        nm_ref[...] = nm
        nv_ref[...] = nv

    spec = _row_spec(tile, cols)
    return pl.pallas_call(
        body,
        name=name,
        grid=(rows // tile,),
        in_specs=[spec] * 4,
        out_specs=[spec] * 4,
        out_shape=[jax.ShapeDtypeStruct((rows, cols), F32)] * 4,
        compiler_params=_params(("parallel",)),
    )(w, g, m, v)


_SMALL = (
    ("v_ln_g", (D_GMLP,), 8),
    ("v_ln_b", (D_GMLP,), 8),
    ("w_spatial", (N_HEADS, CHUNK, CHUNK), 1024),
    ("b_spatial", (N_HEADS, CHUNK), 8),
    ("sinks", (N_HEADS,), 8),
    ("ln1_g", (D_MODEL,), 8),
    ("ln1_b", (D_MODEL,), 8),
    ("ln2_g", (D_MODEL,), 8),
    ("ln2_b", (D_MODEL,), 8),
    ("squared_error", (D_MODEL,), 8),
)
N_SMALL_PARAMS = len(_SMALL) - 1


def _pack_small(values):
    parts = []
    for (name, shape, rows), val in zip(_SMALL, values, strict=True):
        flat = val.reshape(-1).astype(F32)
        parts.append(jnp.pad(flat, (0, rows * LANES - flat.shape[0])).reshape(rows, LANES))
    parts.append(jnp.zeros((SMALL_ROWS - sum(rows for _, _, rows in _SMALL), LANES), F32))
    return jnp.concatenate(parts, axis=0)


def _adamw_update(w, g, m, v):
    nm = ADAM_B1 * m + (1.0 - ADAM_B1) * g
    nv = ADAM_B2 * v + (1.0 - ADAM_B2) * (g * g)
    m_hat = nm / (1.0 - ADAM_B1**ADAM_STEP)
    v_hat = nv / (1.0 - ADAM_B2**ADAM_STEP)
    return -ADAM_LR * (m_hat / (jnp.sqrt(v_hat) + ADAM_EPS) + ADAM_WD * w), nm, nv


def _adamw_small(g_slab, params, first, second):
    n = N_SMALL_PARAMS

    def pieces(shape):
        if len(shape) == 3:
            return [((0, h), h * shape[1], shape[1], shape[2]) for h in range(shape[0])]
        if len(shape) == 2:
            return [((0,), 0, shape[0], shape[1])]
        if shape[0] >= LANES:
            return [((slice(None), slice(r * LANES, (r + 1) * LANES)), r, 1, LANES) for r in range(shape[0] // LANES)]
        return [((slice(None), slice(0, shape[0])), 0, 1, shape[0])]

    def body(*refs):
        g_ref = refs[0]
        w_refs, m_refs, v_refs = refs[1 : 1 + n], refs[1 + n : 1 + 2 * n], refs[1 + 2 * n : 1 + 3 * n]
        outs = refs[1 + 3 * n :]
        row0 = 0
        for idx, (_, shape, rows) in enumerate(_SMALL[:n]):
            for where, first_row, n_rows, lanes in pieces(shape):
                g = g_ref[row0 + first_row : row0 + first_row + n_rows, 0:lanes]
                delta, nm, nv = _adamw_update(w_refs[idx][where], g, m_refs[idx][where], v_refs[idx][where])
                for group, val in enumerate((g, delta, nm, nv)):
                    outs[group * n + idx][where] = val
            row0 += rows

    vmem = pl.BlockSpec(memory_space=pltpu.VMEM)
    shapes = [jax.ShapeDtypeStruct(p.shape, F32) for p in params]
    outs = pl.pallas_call(
        body,
        name="adamw_small",
        in_specs=[vmem] * (1 + 3 * n),
        out_specs=[vmem] * (4 * n),
        out_shape=shapes * 4,
        compiler_params=_params(),
    )(g_slab, *params, *first, *second)
    return [list(outs[group * n : (group + 1) * n]) for group in range(4)]


def kernel(x, positions, w_in, v_ln_g, v_ln_b, w_spatial, b_spatial, sinks, w_out, ln1_g, ln1_b, w_ff1, w_ff2, ln2_g, ln2_b, loss_target, m_w_in, m_v_ln_g, m_v_ln_b, m_w_spatial, m_b_spatial, m_sinks, m_w_out, m_ln1_g, m_ln1_b, m_w_ff1, m_w_ff2, m_ln2_g, m_ln2_b, v_w_in, v_v_ln_g, v_v_ln_b, v_w_spatial, v_b_spatial, v_sinks, v_w_out, v_ln1_g, v_ln1_b, v_w_ff1, v_w_ff2, v_ln2_g, v_ln2_b):
    t = x.shape[1]
    x2 = x.reshape(t, D_MODEL)
    target = loss_target.reshape(t, D_MODEL)

    w_in_shard = w_in[0].T.astype(BF16)
    in_direct, in_pass = _direct_gather_plans([w_in_shard.shape[0]])
    in_bufs, in_started, in_send, in_recv = _split_call(
        "gather_w_in_start", [w_in_shard, _landing(N_CHIPS * w_in_shard.shape[0], D_MODEL, BF16)], start=in_direct)
    inv_freq = ROPE_THETA ** (-jnp.arange(0, HEAD_DIM, 2, dtype=F32) / HEAD_DIM)
    cos, sin, later = _rope_tables_and_casts(
        positions, jnp.tile(inv_freq, LANES // (HEAD_DIM // 2)).reshape(1, LANES), [w_out[0], w_ff1[0], w_ff2[0]], dep=in_started)
    later_rows = [s.shape[0] for s in later]
    direct_plan, pass_plan, diagonal_plan = _gather_plans(later_rows)
    bufs, started, direct_send, direct_recv = _split_call(
        "gather_start", later + [_landing(N_CHIPS * r, D_MODEL, BF16) for r in later_rows], start=direct_plan, after=cos)
    in_bufs, in_passing, in_pass_send, in_pass_recv = _split_call(
        "gather_w_in_pass", in_bufs, wait=(in_direct, in_send, in_recv), start=in_pass, after=started)
    in_bufs, _ = _split_call("gather_w_in_end", in_bufs, wait=(in_pass, in_pass_send, in_pass_recv), after=in_passing)
    w_in_t = in_bufs[1]

    u, vg, q, k, va = _in_proj(x2, w_in_t, cos, sin)
    bias_full = jnp.repeat(b_spatial[0].T, HEAD_DIM, axis=1)
    sink_vec = sinks.reshape(N_HEADS)
    bufs, passing, pass_send, pass_recv = _split_call(
        "gather_pass", bufs, wait=(direct_plan, direct_send, direct_recv), start=pass_plan, after=u)
    cat = _mixer_fwd(u, vg, q, k, va, v_ln_g, v_ln_b, w_spatial[0], bias_full, sink_vec, dep=passing)
    bufs, passing, diag_send, diag_recv = _split_call(
        "gather_pass_diagonal", bufs, wait=(pass_plan, pass_send, pass_recv), start=diagonal_plan, after=cat)
    bufs, _ = _split_call("gather_end", bufs, wait=(diagonal_plan, diag_send, diag_recv), after=passing)
    w_out_all = bufs[3]
    w1_all = bufs[4].reshape(N_FF_BLOCKS, D_MODEL, D_MODEL)
    w2_all = bufs[5].reshape(N_FF_BLOCKS, D_MODEL, D_MODEL)
    xhat1, rstd1, x1b, r, dz2, dz2b, d_ln2_g, d_ln2_b, sq_err = _ffn_fwd_loss(
        cat, x2, w_out_all, ln1_g, ln1_b, w1_all, w2_all, ln2_g, ln2_b, target)

    pos = jnp.stack([lax.axis_index("c"), 2 * lax.axis_index("x") + lax.axis_index("y")]).astype(jnp.int32)
    half_landing = lambda g: _landing(g.shape[0] // 2, D_MODEL, F32)
    g_ff2_local = _grad_w_ff2(r, dz2b)
    swap_plan = _swap_plan([D_FF // N_CHIPS])
    ff2_bufs, swapping2, swap2_send, swap2_recv = _split_call("ff2_swap_start", [g_ff2_local, half_landing(g_ff2_local)], start=swap_plan)
    g_ff1_local, dz1, dz1b, dcat, d_ln1_g, d_ln1_b = _ffn_bwd_ln1(
        dz2, r, x1b, xhat1, rstd1, ln1_g, w1_all, w2_all, w_out_all, dep=swapping2)
    ff1_bufs, swapping1, swap1_send, swap1_recv = _split_call("ff1_swap_start", [g_ff1_local, half_landing(g_ff1_local)], start=swap_plan)
    g_out_local = _grad_w_out(cat, dz1b, dep=swapping1)
    ff2_bufs, swapped2 = _split_call("ff2_swap_wait", ff2_bufs, wait=(swap_plan, swap2_send, swap2_recv), after=g_out_local)
    ff1_bufs, _ = _split_call("ff1_swap_wait", ff1_bufs, wait=(swap_plan, swap1_send, swap1_recv), after=swapped2)
    ff_sums = [_pair_sum("grad_pair_sum_w_ff1", ff1_bufs[0], ff1_bufs[1], pos), _pair_sum("grad_pair_sum_w_ff2", ff2_bufs[0], ff2_bufs[1], pos)]
    ff_halves = [p.shape[0] // N_CHIPS for p, _ in ff_sums]
    exchange_plan = _exchange_plan(ff_halves)
    bufs, exchanging, ex_send, ex_recv = _split_call(
        "ff_exchange_start", [p for p, _ in ff_sums] + [_landing(3 * h, D_MODEL, BF16) for h in ff_halves], start=exchange_plan)
    dh_main, dkv, d_v_ln_g, d_v_ln_b, d_w_spatial, d_b_spatial_t, d_sinks = _mixer_bwd(
        u, vg, q, k, va, dcat, cos, sin, v_ln_g, v_ln_b, w_spatial[0], bias_full, sink_vec, dep=exchanging)
    g_in_local, small_g = _grad_w_in_t_and_small_all_reduce(dh_main, dkv, x2, _pack_small(
        [d_v_ln_g, d_v_ln_b, d_w_spatial, d_b_spatial_t[:, :N_HEADS].T, d_sinks[0, :N_HEADS], d_ln1_g, d_ln1_b, d_ln2_g, d_ln2_b, sq_err]))
    sq_row = sum(rows for _, _, rows in _SMALL[:N_SMALL_PARAMS])
    loss = 0.5 * jnp.sum(small_g[sq_row : sq_row + _SMALL[N_SMALL_PARAMS][2]]) / D_MODEL

    small = [g_in_local, g_out_local]
    small_swap_plan = _swap_plan([g.shape[0] // N_CHIPS for g in small])
    swap_bufs, small_swapping, ss_send, ss_recv = _split_call(
        "small_swap_start", small + [half_landing(g) for g in small], start=small_swap_plan)
    grad_x_flat = _grad_x(dh_main, dkv, dz1, w_in_t, dep=small_swapping)
    grad_x = grad_x_flat.reshape(1, t, D_MODEL)
    swap_bufs, _ = _split_call("small_swap_wait", swap_bufs, wait=(small_swap_plan, ss_send, ss_recv), after=grad_x_flat)
    pair_sums = [_pair_sum("grad_pair_sum_" + nm, g, th, pos) for nm, g, th in zip(["w_in", "w_out"], swap_bufs[:2], swap_bufs[2:])]
    small_halves = [p.shape[0] // N_CHIPS for p, _ in pair_sums]
    small_plan = _exchange_plan(small_halves)
    small_bufs, small_exchanging, sm_send, sm_recv = _split_call(
        "small_exchange_start", [p for p, _ in pair_sums] + [_landing(3 * h, D_MODEL, BF16) for h in small_halves], start=small_plan)

    bufs, _ = _split_call("ff_exchange_wait", bufs, wait=(exchange_plan, ex_send, ex_recv), after=small_exchanging)
    ff_shards = [_chip_sum("grad_chip_sum_" + nm, own, ld, pos) for nm, (_, own), ld in zip(["w_ff1", "w_ff2"], ff_sums, bufs[2:])]
    g_w_ff1, g_w_ff2 = _pair_gather("grad_pair_gather_ff", ff_shards)

    g_w_ff1, d_w_ff1, nm_w_ff1, nv_w_ff1 = _adamw("adamw_w_ff1", w_ff1[0], g_w_ff1, m_w_ff1[0], v_w_ff1[0])
    g_w_ff2, d_w_ff2, nm_w_ff2, nv_w_ff2 = _adamw("adamw_w_ff2", w_ff2[0], g_w_ff2, m_w_ff2[0], v_w_ff2[0])
    small_bufs, _ = _split_call("small_exchange_wait", small_bufs, wait=(small_plan, sm_send, sm_recv), after=nv_w_ff2)
    shards = [_chip_sum("grad_chip_sum_" + nm, own, ld, pos) for nm, (_, own), ld in zip(["w_in", "w_out"], pair_sums, small_bufs[2:])]
    g_w_in_t, g_w_out = _pair_gather("grad_pair_gather_small", shards)
    g_w_in, d_w_in, nm_w_in, nv_w_in = (a.T for a in _adamw("adamw_w_in", w_in[0].T, g_w_in_t, m_w_in[0].T, v_w_in[0].T))
    g_w_out, d_w_out, nm_w_out, nv_w_out = _adamw("adamw_w_out", w_out[0], g_w_out, m_w_out[0], v_w_out[0])
    small_grads, small_d, small_nm, small_nv = _adamw_small(
        small_g,
        [v_ln_g, v_ln_b, w_spatial, b_spatial, sinks, ln1_g, ln1_b, ln2_g, ln2_b],
        [m_v_ln_g, m_v_ln_b, m_w_spatial, m_b_spatial, m_sinks, m_ln1_g, m_ln1_b, m_ln2_g, m_ln2_b],
        [v_v_ln_g, v_v_ln_b, v_w_spatial, v_b_spatial, v_sinks, v_ln1_g, v_ln1_b, v_ln2_g, v_ln2_b])

    def with_big(small, w_in_v, w_out_v, w_ff1_v, w_ff2_v):
        g_vg, g_vb, g_ws, g_bs, g_sk, g_1g, g_1b, g_2g, g_2b = small
        return [w_in_v[None], g_vg, g_vb, g_ws, g_bs, g_sk, w_out_v[None], g_1g, g_1b, w_ff1_v[None], w_ff2_v[None], g_2g, g_2b]

    return (
        loss,
        grad_x,
        *with_big(small_grads, g_w_in, g_w_out, g_w_ff1, g_w_ff2),
        *with_big(small_d, d_w_in, d_w_out, d_w_ff1, d_w_ff2),
        *with_big(small_nm, nm_w_in, nm_w_out, nm_w_ff1, nm_w_ff2),
        *with_big(small_nv, nv_w_in, nv_w_out, nv_w_ff1, nv_w_ff2),
    )
```

```python
import math

import jax
import jax.numpy as jnp
from jax import lax
from jax.experimental import pallas as pl
from jax.experimental.pallas import tpu as pltpu

F32 = jnp.float32
BF16 = jnp.bfloat16

D_MODEL = 1024
HEAD_DIM = 64
D_GMLP = 512
D_ATTN = 512
D_KV = 128
D_IN = 2 * D_GMLP + D_ATTN + 2 * D_KV
D_MAIN = 2 * D_GMLP + D_ATTN
N_HEADS = 8
CHUNK = 128
CHUNKS_PER_STEP = 4
ROPE_THETA = 10000.0
D_FF = 4 * D_MODEL
N_FF_BLOCKS = 4
LN_EPS = 1e-5
ALPHA = (2.0 * 1) ** 0.25
NEG_INF = -1e30
SCALE = 1.0 / math.sqrt(HEAD_DIM)

ADAM_LR = 0.001
ADAM_B1 = 0.9
ADAM_B2 = 0.999
ADAM_EPS = 1e-08
ADAM_WD = 0.01
ADAM_STEP = 10

N_CHIPS = 4
LANES = 128
V7X_VMEM_BYTES = 64 * 1024 * 1024
VMEM_LIMIT = V7X_VMEM_BYTES - 8 * 1024 * 1024
TM = 512
TM_FFN = 256
TK = 1024
SMALL_ROWS = 1152
MESH = pl.DeviceIdType.MESH

NT = (((1,), (1,)), ((), ()))
TN = (((0,), (0,)), ((), ()))


def _dot(a, b, dims=None):
    if dims is None:
        return jnp.dot(a, b, preferred_element_type=F32)
    return lax.dot_general(a, b, dims, preferred_element_type=F32)


def _params(semantics=None):
    return pltpu.CompilerParams(dimension_semantics=semantics, vmem_limit_bytes=VMEM_LIMIT)


def _const_spec(shape, single_buffer=False):
    zeros = (0,) * len(shape)
    if single_buffer:
        return pl.BlockSpec(shape, lambda *_: zeros, pipeline_mode=pl.Buffered(1))
    return pl.BlockSpec(shape, lambda *_: zeros)


def _row_spec(rows, cols):
    return pl.BlockSpec((rows, cols), lambda i: (i, 0))


def _after(dep, body, in_specs, operands):
    if dep is None:
        return body, list(in_specs), list(operands)
    return (lambda dep_ref, *refs: body(*refs)), [pl.BlockSpec(memory_space=pl.ANY)] + list(in_specs), [dep] + list(operands)


def _gelu(x):
    k = math.sqrt(2.0 / math.pi)
    return 0.5 * x * (1.0 + jnp.tanh(k * (x + 0.044715 * (x * x * x))))


def _gelu_and_grad(x):
    k = math.sqrt(2.0 / math.pi)
    x2 = x * x
    t = jnp.tanh(k * (x + 0.044715 * (x2 * x)))
    g = 0.5 * x * (1.0 + t)
    dg = 0.5 * (1.0 + t) + 0.5 * x * (1.0 - t * t) * (k * (1.0 + 3.0 * 0.044715 * x2))
    return g, dg


def _layer_norm_stats(z):
    mu = jnp.mean(z, axis=-1, keepdims=True)
    zc = z - mu
    var = jnp.mean(zc * zc, axis=-1, keepdims=True)
    rstd = lax.rsqrt(var + LN_EPS)
    return zc * rstd, rstd


def _layer_norm_bwd(dxhat, xhat, rstd):
    m1 = jnp.mean(dxhat, axis=-1, keepdims=True)
    m2 = jnp.mean(dxhat * xhat, axis=-1, keepdims=True)
    return rstd * (dxhat - m1 - xhat * m2)


def _rotate_half(t):
    n = t.shape[1]
    lane = lax.broadcasted_iota(jnp.int32, t.shape, 1)
    first = (lane & (HEAD_DIM // 2)) == 0
    return jnp.where(first, -pltpu.roll(t, n - HEAD_DIM // 2, 1), pltpu.roll(t, HEAD_DIM // 2, 1))


def _rope(t, cos, sin):
    return t * cos + _rotate_half(t) * sin


def _rope_transposed(g, cos, sin):
    return g * cos - _rotate_half(g * sin)


def _lane_tile(a, reps):
    return jnp.tile(a, (1, reps)) if reps > 1 else a


def _in_proj(x, w_in_t, cos, sin, dep=None):
    t = x.shape[0]

    def body(x_ref, w_ref, cos_ref, sin_ref, u_ref, vg_ref, q_ref, k_ref, va_ref):
        xb = x_ref[...].astype(BF16)
        u_ref[...] = _dot(xb, w_ref[0:D_GMLP, :], NT)
        vg_ref[...] = _dot(xb, w_ref[D_GMLP : 2 * D_GMLP, :], NT)
        q = _dot(xb, w_ref[2 * D_GMLP : D_MAIN, :], NT)
        k = _dot(xb, w_ref[D_MAIN : D_MAIN + D_KV, :], NT)
        va_ref[...] = _dot(xb, w_ref[D_MAIN + D_KV : D_IN, :], NT).astype(BF16)
        c, s = cos_ref[...], sin_ref[...]
        q_ref[...] = _rope(q, _lane_tile(c, D_ATTN // LANES), _lane_tile(s, D_ATTN // LANES)).astype(BF16)
        k_ref[...] = _rope(k, c, s).astype(BF16)

    body, in_specs, operands = _after(
        dep, body, [_row_spec(TM, D_MODEL), _const_spec((D_IN, D_MODEL)), _row_spec(TM, LANES), _row_spec(TM, LANES)], [x, w_in_t, cos, sin])
    return pl.pallas_call(
        body,
        name="in_proj",
        grid=(t // TM,),
        in_specs=in_specs,
        out_specs=[_row_spec(TM, D_GMLP), _row_spec(TM, D_GMLP), _row_spec(TM, D_ATTN), _row_spec(TM, D_KV), _row_spec(TM, D_KV)],
        out_shape=[
            jax.ShapeDtypeStruct((t, D_GMLP), F32),
            jax.ShapeDtypeStruct((t, D_GMLP), F32),
            jax.ShapeDtypeStruct((t, D_ATTN), BF16),
            jax.ShapeDtypeStruct((t, D_KV), BF16),
            jax.ShapeDtypeStruct((t, D_KV), BF16),
        ],
        compiler_params=_params(("parallel",)),
    )(*operands)


def _step_rows(i):
    return (i, 0)


def _chunk_before_step(i):
    return (jnp.maximum(CHUNKS_PER_STEP * i - 1, 0), 0)


def _chunk_specs():
    step = CHUNKS_PER_STEP * CHUNK
    return [
        pl.BlockSpec((step, D_GMLP), _step_rows),
        pl.BlockSpec((step, D_GMLP), _step_rows),
        pl.BlockSpec((step, D_ATTN), _step_rows),
        pl.BlockSpec((step, D_KV), _step_rows),
        pl.BlockSpec((CHUNK, D_KV), _chunk_before_step),
        pl.BlockSpec((step, D_KV), _step_rows),
        pl.BlockSpec((CHUNK, D_KV), _chunk_before_step),
    ]


def _half_lane_masks(rows):
    lane = lax.broadcasted_iota(jnp.int32, (rows, LANES), 1)
    return lane < HEAD_DIM


def _kv_variants(kv2):
    left = _half_lane_masks(kv2.shape[0])
    f = kv2.astype(F32)
    swapped = pltpu.roll(f, HEAD_DIM, 1)
    zero = jnp.zeros_like(f)
    g0 = (jnp.where(left, f, zero).astype(BF16), jnp.where(left, zero, swapped).astype(BF16))
    g1 = (jnp.where(left, swapped, zero).astype(BF16), jnp.where(left, zero, f).astype(BF16))
    return (g0, g1)


def _band_mask(i, heads=1):
    row = lax.broadcasted_iota(jnp.int32, (heads * CHUNK, 2 * CHUNK), 0) & (CHUNK - 1)
    col = lax.broadcasted_iota(jnp.int32, (heads * CHUNK, 2 * CHUNK), 1)
    no_prev = jnp.where(i > 0, 0, 4 * CHUNK)
    in_prev = jnp.logical_and(col < CHUNK, (col - row) > no_prev)
    in_cur = jnp.logical_and(col >= CHUNK, (col - CHUNK) <= row)
    return jnp.logical_or(in_prev, in_cur)


def _causal_mask():
    row = lax.broadcasted_iota(jnp.int32, (CHUNK, CHUNK), 0)
    col = lax.broadcasted_iota(jnp.int32, (CHUNK, CHUNK), 1)
    return col <= row


def _store_spatial_weights(w_ref, wcat_ref, wcat_t_ref=None):
    causal = _causal_mask()
    for p in range(D_GMLP // LANES):
        wl = jnp.where(causal, w_ref[2 * p], 0.0)
        wr = jnp.where(causal, w_ref[2 * p + 1], 0.0)
        wcat_ref[p] = jnp.concatenate([wl, wr], axis=1).astype(BF16)
        if wcat_t_ref is not None:
            wcat_t_ref[p] = jnp.concatenate([wl.T, wr.T], axis=1).astype(BF16)


def _pair_stack(xp, left):
    return jnp.concatenate([jnp.where(left, xp, 0.0), jnp.where(left, 0.0, xp)], axis=0).astype(BF16)


def _mixer_fwd(u, vg, q, k, va, v_ln_g, v_ln_b, w_spatial, bias_full, sinks, dep=None):
    t = u.shape[0]

    def body(u_ref, vg_ref, q_ref, kc_ref, kp_ref, vc_ref, vp_ref, g_ref, b_ref, w_ref, bias_ref, sink_ref, cat_ref, wcat):
        i = pl.program_id(0)
        left = _half_lane_masks(CHUNK)

        @pl.when(i == 0)
        def _():
            _store_spatial_weights(w_ref, wcat)

        heads = range(N_HEADS)
        pair_cols = [slice(p * LANES, (p + 1) * LANES) for p in range(D_GMLP // LANES)]
        sinks_h = [sink_ref[h] for h in heads]
        for c in range(CHUNKS_PER_STEP):
            rows = slice(c * CHUNK, (c + 1) * CHUNK)
            before = slice((c - 1) * CHUNK, c * CHUNK)
            k_prev = kp_ref[...] if c == 0 else kc_ref[before, :]
            v_prev = vp_ref[...] if c == 0 else vc_ref[before, :]
            k_var = _kv_variants(jnp.concatenate([k_prev, kc_ref[rows, :]], axis=0))
            v_var = _kv_variants(jnp.concatenate([v_prev, vc_ref[rows, :]], axis=0))
            scores = [_dot(q_ref[rows, pair_cols[h // 2]], k_var[h // 4][h % 2], NT) for h in heads]

            ug = _gelu(u_ref[rows, :])
            xhat, _ = _layer_norm_stats(_gelu(vg_ref[rows, :]))
            vgl = xhat * g_ref[...] + b_ref[...]
            mixed = [_dot(wcat[p], _pair_stack(vgl[:, cols], left)) for p, cols in enumerate(pair_cols)]

            valid = _band_mask(CHUNKS_PER_STEP * i + c)
            masked = [jnp.where(valid, scores[h] * SCALE, NEG_INF) for h in heads]
            maxes = [jnp.maximum(jnp.max(masked[h], axis=1, keepdims=True), sinks_h[h]) for h in heads]
            exps = [jnp.exp(masked[h] - maxes[h]) for h in heads]
            invs = [1.0 / (jnp.sum(exps[h], axis=1, keepdims=True) + jnp.exp(sinks_h[h] - maxes[h])) for h in heads]
            probs = [(exps[h] * invs[h]).astype(BF16) for h in heads]
            for p, cols in enumerate(pair_cols):
                cat_ref[rows, cols] = (ug[:, cols] * (mixed[p] + bias_ref[:, cols])).astype(BF16)
            for p in range(D_ATTN // LANES):
                out = _dot(probs[2 * p], v_var[p // 2][0]) + _dot(probs[2 * p + 1], v_var[p // 2][1])
                cat_ref[rows, D_GMLP + p * LANES : D_GMLP + (p + 1) * LANES] = out.astype(BF16)

    in_specs = _chunk_specs() + [
        _const_spec((1, D_GMLP)),
        _const_spec((1, D_GMLP)),
        _const_spec((N_HEADS, CHUNK, CHUNK)),
        _const_spec((CHUNK, D_GMLP)),
        pl.BlockSpec(memory_space=pltpu.SMEM),
    ]
    body, in_specs, operands = _after(dep, body, in_specs, [u, vg, q, k, k, va, va, v_ln_g, v_ln_b, w_spatial, bias_full, sinks])
    return pl.pallas_call(
        body,
        name="mixer_fwd",
        grid=(t // (CHUNKS_PER_STEP * CHUNK),),
        in_specs=in_specs,
        out_specs=pl.BlockSpec((CHUNKS_PER_STEP * CHUNK, D_MODEL), lambda i: (i, 0)),
        out_shape=jax.ShapeDtypeStruct((t, D_MODEL), BF16),
        scratch_shapes=[pltpu.VMEM((D_GMLP // LANES, CHUNK, 2 * CHUNK), BF16)],
        compiler_params=_params(("arbitrary",)),
    )(*operands)


def _ffn_fwd_loss(cat, x, w_out, ln1_g, ln1_b, w1, w2, ln2_g, ln2_b, target):
    t = x.shape[0]

    def body(cat_ref, x_ref, wo_ref, g1_ref, b1_ref, w1_ref, w2_ref, g2_ref, b2_ref, tgt_ref,
             xh_ref, rstd_ref, x1b_ref, r_ref, dz2_ref, dz2b_ref, dg2_ref, db2_ref, sq_ref):
        @pl.when(pl.program_id(0) == 0)
        def _():
            dg2_ref[...] = jnp.zeros_like(dg2_ref)
            db2_ref[...] = jnp.zeros_like(db2_ref)
            sq_ref[...] = jnp.zeros_like(sq_ref)

        xhat1, rstd1 = _layer_norm_stats(ALPHA * x_ref[...] + _dot(cat_ref[...], wo_ref[...]))
        xh_ref[...] = xhat1
        rstd_ref[...] = rstd1
        x1 = xhat1 * g1_ref[...] + b1_ref[...]
        x1b = x1.astype(BF16)
        x1b_ref[...] = x1b
        ff = jnp.zeros((TM_FFN, D_MODEL), F32)
        for j in range(N_FF_BLOCKS):
            r = jnp.maximum(_dot(x1b, w1_ref[j]), 0.0)
            r_ref[:, j * D_MODEL : (j + 1) * D_MODEL] = r.astype(BF16)
            ff = ff + _dot((r * r).astype(BF16), w2_ref[j])
        xhat2, rstd2 = _layer_norm_stats(ALPHA * x1 + ff)
        err = xhat2 * g2_ref[...] + b2_ref[...] - tgt_ref[...]
        sq_ref[...] += jnp.sum(err * err, axis=0, keepdims=True)
        dy = err * (1.0 / D_MODEL)
        dg2_ref[...] += jnp.sum(dy * xhat2, axis=0, keepdims=True)
        db2_ref[...] += jnp.sum(dy, axis=0, keepdims=True)
        dz2 = _layer_norm_bwd(dy * g2_ref[...], xhat2, rstd2)
        dz2_ref[...] = dz2
        dz2b_ref[...] = dz2.astype(BF16)

    vec = _const_spec((1, D_MODEL))
    tile = _row_spec(TM_FFN, D_MODEL)
    wspec = _const_spec((N_FF_BLOCKS, D_MODEL, D_MODEL), single_buffer=True)
    return pl.pallas_call(
        body,
        name="ffn_fwd_loss",
        grid=(t // TM_FFN,),
        in_specs=[tile, tile, _const_spec((D_MODEL, D_MODEL), single_buffer=True), vec, vec, wspec, wspec, vec, vec, tile],
        out_specs=[tile, _row_spec(TM_FFN, 1), tile, _row_spec(TM_FFN, D_FF), tile, tile, vec, vec, vec],
        out_shape=[
            jax.ShapeDtypeStruct((t, D_MODEL), F32),
            jax.ShapeDtypeStruct((t, 1), F32),
            jax.ShapeDtypeStruct((t, D_MODEL), BF16),
            jax.ShapeDtypeStruct((t, D_FF), BF16),
            jax.ShapeDtypeStruct((t, D_MODEL), F32),
            jax.ShapeDtypeStruct((t, D_MODEL), BF16),
            jax.ShapeDtypeStruct((1, D_MODEL), F32),
            jax.ShapeDtypeStruct((1, D_MODEL), F32),
            jax.ShapeDtypeStruct((1, D_MODEL), F32),
        ],
        compiler_params=_params(("arbitrary",)),
    )(cat, x, w_out, ln1_g, ln1_b, w1, w2, ln2_g, ln2_b, target)


def _ffn_bwd_ln1(dz2, r, x1b, xhat1, rstd1, ln1_g, w1, w2, w_out, dep=None):
    t = dz2.shape[0]

    def body(dz2_ref, r_ref, x1b_ref, xh_ref, rstd_ref, g1_ref, w1_ref, w2_ref, wo_ref, gw1_ref, dz1_ref, dz1b_ref, dcat_ref, dg1_ref, db1_ref):
        @pl.when(pl.program_id(0) == 0)
        def _():
            dg1_ref[...] = jnp.zeros_like(dg1_ref)
            db1_ref[...] = jnp.zeros_like(db1_ref)
            gw1_ref[...] = jnp.zeros_like(gw1_ref)

        dz2 = dz2_ref[...]
        dz2b = dz2.astype(BF16)
        x1_t = x1b_ref[...].astype(F32).T.astype(BF16)
        dx1 = ALPHA * dz2
        for j in range(N_FF_BLOCKS):
            cols = slice(j * D_MODEL, (j + 1) * D_MODEL)
            dpre = (_dot(dz2b, w2_ref[j], NT) * (2.0 * r_ref[:, cols].astype(F32))).astype(BF16)
            gw1_ref[cols, :] += _dot(x1_t, dpre)
            dx1 = dx1 + _dot(dpre, w1_ref[j], NT)
        xhat1 = xh_ref[...]
        dg1_ref[...] += jnp.sum(dx1 * xhat1, axis=0, keepdims=True)
        db1_ref[...] += jnp.sum(dx1, axis=0, keepdims=True)
        dz1 = _layer_norm_bwd(dx1 * g1_ref[...], xhat1, rstd_ref[...])
        dz1_ref[...] = dz1
        dz1b = dz1.astype(BF16)
        dz1b_ref[...] = dz1b
        dcat_ref[...] = _dot(dz1b, wo_ref[...], NT).astype(BF16)

    vec = _const_spec((1, D_MODEL))
    tile = _row_spec(TM_FFN, D_MODEL)
    wspec = _const_spec((N_FF_BLOCKS, D_MODEL, D_MODEL), single_buffer=True)
    body, in_specs, operands = _after(
        dep, body,
        [tile, _row_spec(TM_FFN, D_FF), tile, tile, _row_spec(TM_FFN, 1), vec, wspec, wspec, _const_spec((D_MODEL, D_MODEL), single_buffer=True)],
        [dz2, r, x1b, xhat1, rstd1, ln1_g, w1, w2, w_out])
    return pl.pallas_call(
        body,
        name="ffn_bwd_ln1",
        grid=(t // TM_FFN,),
        in_specs=in_specs,
        out_specs=[_const_spec((D_FF, D_MODEL), single_buffer=True), tile, tile, tile, vec, vec],
        out_shape=[
            jax.ShapeDtypeStruct((D_FF, D_MODEL), F32),
            jax.ShapeDtypeStruct((t, D_MODEL), F32),
            jax.ShapeDtypeStruct((t, D_MODEL), BF16),
            jax.ShapeDtypeStruct((t, D_MODEL), BF16),
            jax.ShapeDtypeStruct((1, D_MODEL), F32),
            jax.ShapeDtypeStruct((1, D_MODEL), F32),
        ],
        compiler_params=_params(("arbitrary",)),
    )(*operands)


def _mixer_bwd(u, vg, q, k, va, dcat, cos, sin, v_ln_g, v_ln_b, w_spatial, bias_full, sinks, r, dz2b, dep=None):
    t = u.shape[0]
    n_chunks = t // CHUNK
    assert CHUNKS_PER_STEP == N_FF_BLOCKS

    def body(u_ref, vg_ref, q_ref, kc_ref, kp_ref, vc_ref, vp_ref, dcat_ref, cosc_ref, sinc_ref, cosp_ref, sinp_ref,
             g_ref, b_ref, w_ref, bias_ref, sink_ref, r_ref, dz2b_ref,
             dmain_ref, dkv_ref, gw2_ref, dg_ref, db_ref, dw_ref, dbs_ref, dsink_ref, dmix_acc, wcat, wcat_t):
        i = pl.program_id(0)
        left = _half_lane_masks(CHUNK)
        lane = lax.broadcasted_iota(jnp.int32, (CHUNK, LANES), 1)
        n_pairs = D_GMLP // LANES

        @pl.when(i == 0)
        def _():
            dg_ref[...] = jnp.zeros_like(dg_ref)
            db_ref[...] = jnp.zeros_like(db_ref)
            dw_ref[...] = jnp.zeros_like(dw_ref)
            dsink_ref[...] = jnp.zeros_like(dsink_ref)
            dmix_acc[...] = jnp.zeros_like(dmix_acc)
            gw2_ref[...] = jnp.zeros_like(gw2_ref)
            _store_spatial_weights(w_ref, wcat, wcat_t)

        n_qpairs = D_ATTN // LANES
        heads = range(N_HEADS)
        pair_cols = [slice(p * LANES, (p + 1) * LANES) for p in range(n_pairs)]
        sinks_h = [sink_ref[h] for h in heads]
        gain = g_ref[...]
        causal = _causal_mask()
        lane_row = lax.broadcasted_iota(jnp.int32, (1, LANES), 1)
        heads_per_group = N_HEADS // 2

        def group_grad_t(lhs_t, rhs_heads):
            parts = []
            for g in range(2):
                group = range(g * heads_per_group, (g + 1) * heads_per_group)
                lhs = jnp.concatenate([lhs_t[h * HEAD_DIM : (h + 1) * HEAD_DIM] for h in group], axis=1)
                parts.append(_dot(lhs, jnp.concatenate([rhs_heads[h] for h in group], axis=0)))
            return jnp.concatenate(parts, axis=0)

        for c in range(CHUNKS_PER_STEP):
            chunk = CHUNKS_PER_STEP * i + c
            rows = slice(c * CHUNK, (c + 1) * CHUNK)
            before = slice((c - 1) * CHUNK, c * CHUNK)

            k_prev = kp_ref[...] if c == 0 else kc_ref[before, :]
            v_prev = vp_ref[...] if c == 0 else vc_ref[before, :]
            k_var = _kv_variants(jnp.concatenate([k_prev, kc_ref[rows, :]], axis=0))
            v_var = _kv_variants(jnp.concatenate([v_prev, vc_ref[rows, :]], axis=0))
            q_pairs = [q_ref[rows, cols] for cols in pair_cols]
            do_all = dcat_ref[rows, D_GMLP:D_MODEL]
            do_pairs = [do_all[:, cols] for cols in pair_cols]
            scores = [_dot(q_pairs[h // 2], k_var[h // 4][h % 2], NT) for h in heads]
            dprobs = [_dot(do_pairs[h // 2], v_var[h // 4][h % 2], NT) for h in heads]
            q_t = q_ref[rows, :].astype(F32).T.astype(BF16)
            do_t = do_all.astype(F32).T.astype(BF16)

            ff_cols = slice(c * D_MODEL, (c + 1) * D_MODEL)
            relu_block = r_ref[:, ff_cols]
            gw2_ref[ff_cols, :] += _dot(relu_block * relu_block, dz2b_ref[...], TN)

            ug, dug_du = _gelu_and_grad(u_ref[rows, :])
            gv, dgv_dv = _gelu_and_grad(vg_ref[rows, :])
            xhat, rstd = _layer_norm_stats(gv)
            vgl = xhat * gain + b_ref[...]
            mixed = [_dot(wcat[p], _pair_stack(vgl[:, cols], left)) for p, cols in enumerate(pair_cols)]

            valid = _band_mask(chunk)
            masked = [jnp.where(valid, scores[h] * SCALE, NEG_INF) for h in heads]
            maxes = [jnp.maximum(jnp.max(masked[h], axis=1, keepdims=True), sinks_h[h]) for h in heads]
            exps = [jnp.exp(masked[h] - maxes[h]) for h in heads]
            exp_sinks = [jnp.exp(sinks_h[h] - maxes[h]) for h in heads]
            invs = [1.0 / (jnp.sum(exps[h], axis=1, keepdims=True) + exp_sinks[h]) for h in heads]
            probs = [exps[h] * invs[h] for h in heads]
            dsums = [jnp.sum(probs[h] * dprobs[h], axis=1, keepdims=True) for h in heads]
            ds_b = [(probs[h] * (dprobs[h] - dsums[h]) * SCALE).astype(BF16) for h in heads]
            probs_b = [probs[h].astype(BF16) for h in heads]

            dm_stacks = []
            for p, cols in enumerate(pair_cols):
                da = dcat_ref[rows, cols].astype(F32)
                dmain_ref[rows, cols] = (da * (mixed[p] + bias_ref[:, cols]) * dug_du[:, cols]).astype(BF16)
                dmixed = da * ug[:, cols]
                dmix_acc[:, cols] += dmixed
                dm_stacks.append(_pair_stack(dmixed, left))

            dq_all = jnp.concatenate(
                [_dot(ds_b[2 * p], k_var[p // 2][0]) + _dot(ds_b[2 * p + 1], k_var[p // 2][1]) for p in range(n_qpairs)], axis=1)
            dk2_t = group_grad_t(q_t, ds_b)
            dv2_t = group_grad_t(do_t, probs_b)

            for p, cols in enumerate(pair_cols):
                dw_pair = _dot(dm_stacks[p], vgl[:, cols].astype(BF16), NT)
                dw_ref[2 * p] += jnp.where(causal, dw_pair[:CHUNK], 0.0)
                dw_ref[2 * p + 1] += jnp.where(causal, dw_pair[CHUNK:], 0.0)
            dvgl = jnp.concatenate([_dot(wcat_t[p], dm_stacks[p]) for p in range(n_pairs)], axis=1)

            dsink_row = jnp.zeros((1, LANES), F32)
            for h in heads:
                d_sink = -jnp.sum(exp_sinks[h] * invs[h] * dsums[h], axis=0, keepdims=True)
                dsink_row = dsink_row + jnp.where(lane_row == h, d_sink, 0.0)
            dsink_ref[0:1, :] += dsink_row
            cos_c, sin_c = cosc_ref[rows, :], sinc_ref[rows, :]
            cos_p = cosp_ref[...] if c == 0 else cosc_ref[before, :]
            sin_p = sinp_ref[...] if c == 0 else sinc_ref[before, :]
            dmain_ref[rows, 2 * D_GMLP : D_MAIN] = _rope_transposed(dq_all, _lane_tile(cos_c, n_qpairs), _lane_tile(sin_c, n_qpairs)).astype(BF16)
            dk2 = dk2_t.T
            dv2 = dv2_t.T
            cur = pl.ds(pl.multiple_of(chunk * CHUNK, CHUNK), CHUNK)
            dkv_ref[cur, 0:D_KV] = _rope_transposed(dk2[CHUNK:], cos_c, sin_c)
            dkv_ref[cur, D_KV : 2 * D_KV] = dv2[CHUNK:]
            prev = pl.ds(pl.multiple_of(jnp.maximum(chunk - 1, 0) * CHUNK, CHUNK), CHUNK)
            dkv_ref[prev, 0:D_KV] += _rope_transposed(dk2[:CHUNK], cos_p, sin_p)
            dkv_ref[prev, D_KV : 2 * D_KV] += dv2[:CHUNK]

            dg_ref[...] += jnp.sum(dvgl * xhat, axis=0, keepdims=True)
            db_ref[...] += jnp.sum(dvgl, axis=0, keepdims=True)
            dgv = _layer_norm_bwd(dvgl * gain, xhat, rstd)
            dmain_ref[rows, D_GMLP : 2 * D_GMLP] = (dgv * dgv_dv).astype(BF16)

        @pl.when(i == n_chunks // CHUNKS_PER_STEP - 1)
        def _():
            tile = jnp.zeros((CHUNK, LANES), F32)
            for p, cols in enumerate(pair_cols):
                dm = dmix_acc[:, cols]
                sl = jnp.sum(jnp.where(left, dm, 0.0), axis=1, keepdims=True)
                sr = jnp.sum(jnp.where(left, 0.0, dm), axis=1, keepdims=True)
                tile = jnp.where(lane == 2 * p, sl, tile)
                tile = jnp.where(lane == 2 * p + 1, sr, tile)
            dbs_ref[...] = tile

    step = CHUNKS_PER_STEP * CHUNK
    in_specs = _chunk_specs() + [
        pl.BlockSpec((step, D_MODEL), _step_rows),
        pl.BlockSpec((step, LANES), _step_rows),
        pl.BlockSpec((step, LANES), _step_rows),
        pl.BlockSpec((CHUNK, LANES), _chunk_before_step),
        pl.BlockSpec((CHUNK, LANES), _chunk_before_step),
        _const_spec((1, D_GMLP)),
        _const_spec((1, D_GMLP)),
        _const_spec((N_HEADS, CHUNK, CHUNK)),
        _const_spec((CHUNK, D_GMLP)),
        pl.BlockSpec(memory_space=pltpu.SMEM),
        pl.BlockSpec((step, D_FF), _step_rows),
        pl.BlockSpec((step, D_MODEL), _step_rows),
    ]
    body, in_specs, operands = _after(
        dep, body, in_specs, [u, vg, q, k, k, va, va, dcat, cos, sin, cos, sin, v_ln_g, v_ln_b, w_spatial, bias_full, sinks, r, dz2b])
    return pl.pallas_call(
        body,
        name="mixer_bwd",
        grid=(n_chunks // CHUNKS_PER_STEP,),
        in_specs=in_specs,
        out_specs=[
            pl.BlockSpec((step, D_MAIN), _step_rows),
            _const_spec((t, 2 * D_KV)),
            _const_spec((D_FF, D_MODEL), single_buffer=True),
            _const_spec((1, D_GMLP)),
            _const_spec((1, D_GMLP)),
            _const_spec((N_HEADS, CHUNK, CHUNK)),
            _const_spec((CHUNK, LANES)),
            _const_spec((8, LANES)),
        ],
        out_shape=[
            jax.ShapeDtypeStruct((t, D_MAIN), BF16),
            jax.ShapeDtypeStruct((t, 2 * D_KV), F32),
            jax.ShapeDtypeStruct((D_FF, D_MODEL), F32),
            jax.ShapeDtypeStruct((1, D_GMLP), F32),
            jax.ShapeDtypeStruct((1, D_GMLP), F32),
            jax.ShapeDtypeStruct((N_HEADS, CHUNK, CHUNK), F32),
            jax.ShapeDtypeStruct((CHUNK, LANES), F32),
            jax.ShapeDtypeStruct((8, LANES), F32),
        ],
        scratch_shapes=[
            pltpu.VMEM((CHUNK, D_GMLP), F32),
            pltpu.VMEM((D_GMLP // LANES, CHUNK, 2 * CHUNK), BF16),
            pltpu.VMEM((D_GMLP // LANES, CHUNK, 2 * CHUNK), BF16),
        ],
        compiler_params=_params(("arbitrary",)),
    )(*operands)


def _grad_x(dh_main, dkv, dz1, w_in_t, dep=None):
    t = dz1.shape[0]

    def body(dm_ref, dkv_ref, dz1_ref, w_ref, gx_ref):
        acc = ALPHA * dz1_ref[...] + _dot(dm_ref[...], w_ref[0:D_MAIN, :])
        gx_ref[...] = acc + _dot(dkv_ref[...].astype(BF16), w_ref[D_MAIN:D_IN, :])

    body, in_specs, operands = _after(
        dep, body, [_row_spec(TM, D_MAIN), _row_spec(TM, 2 * D_KV), _row_spec(TM, D_MODEL), _const_spec((D_IN, D_MODEL))], [dh_main, dkv, dz1, w_in_t])
    return pl.pallas_call(
        body,
        name="grad_x",
        grid=(t // TM,),
        in_specs=in_specs,
        out_specs=_row_spec(TM, D_MODEL),
        out_shape=jax.ShapeDtypeStruct((t, D_MODEL), F32),
        compiler_params=_params(("parallel",)),
    )(*operands)


def _token_contraction(name, out_rows, tk, in_arrays, contributions, dep=None):
    t = in_arrays[0].shape[0]

    def body(*refs):
        out_ref = refs[-1]

        @pl.when(pl.program_id(0) == 0)
        def _():
            out_ref[...] = jnp.zeros_like(out_ref)

        for row0, a, b in contributions(*refs[:-1]):
            out_ref[row0 : row0 + a.shape[1], :] += _dot(a, b, TN)

    in_specs = [_row_spec(tk, a.shape[1]) for a in in_arrays]
    body, in_specs, operands = _after(dep, body, in_specs, in_arrays)
    return pl.pallas_call(
        body,
        name=name,
        grid=(t // tk,),
        in_specs=in_specs,
        out_specs=_const_spec((out_rows, D_MODEL), single_buffer=True),
        out_shape=jax.ShapeDtypeStruct((out_rows, D_MODEL), F32),
        compiler_params=_params(("arbitrary",)),
    )(*operands)


def _grad_w_out(cat, dz1b, dep=None):
    def contributions(cat_ref, dz1_ref):
        return [(0, cat_ref[...], dz1_ref[...])]

    return _token_contraction("grad_w_out", D_MODEL, TK, [cat, dz1b], contributions, dep)


ANY = pl.BlockSpec(memory_space=pl.ANY)


def _mesh_position():
    return lax.axis_index("x"), lax.axis_index("y"), lax.axis_index("c")


def _other_chips(x, y):
    return [(1 - x, y), (x, 1 - y), (1 - x, 1 - y)]


def _remote(src, dst, send_sem, recv_sem, device):
    return pltpu.make_async_remote_copy(src_ref=src, dst_ref=dst, send_sem=send_sem, recv_sem=recv_sem, device_id=device, device_id_type=MESH)


def _rows(ref, start, size):
    return ref.at[pl.ds(start, size), :]


def _rope_tables_and_casts(pos_row, inv_freq_row, shards, dep=None):
    t = pos_row.shape[1]
    steps = t // TM
    n = len(shards)

    def body(pos_ref, f_ref, *rest):
        f32_refs, (cos_ref, sin_ref), bf16_refs = rest[:n], rest[n : n + 2], rest[n + 2 :]
        for src, dst in zip(f32_refs, bf16_refs):
            dst[...] = src[...].astype(BF16)
        pos_rows = jnp.broadcast_to(pos_ref[...].astype(F32), (LANES, TM)).T
        ang = pos_rows * f_ref[...]
        cos_ref[...] = jnp.cos(ang)
        sin_ref[...] = jnp.sin(ang)

    shard_specs = [_row_spec(s.shape[0] // steps, s.shape[1]) for s in shards]
    body, in_specs, operands = _after(
        dep, body, [pl.BlockSpec((1, TM), lambda i: (0, i)), _const_spec((1, LANES))] + shard_specs, [pos_row, inv_freq_row, *shards])
    outs = pl.pallas_call(
        body,
        name="rope_tables_and_casts",
        grid=(steps,),
        in_specs=in_specs,
        out_specs=[_row_spec(TM, LANES), _row_spec(TM, LANES)] + shard_specs,
        out_shape=[jax.ShapeDtypeStruct((t, LANES), F32)] * 2 + [jax.ShapeDtypeStruct(s.shape, BF16) for s in shards],
        compiler_params=_params(("parallel",)),
    )(*operands)
    return outs[0], outs[1], list(outs[2:])


def _pair_gather(name, shards):
    n = len(shards)

    def body(*refs):
        outs = refs[n : 2 * n]
        send_sems, recv_sems = refs[2 * n :]
        x, y, c = _mesh_position()
        sibling = (x, y, 1 - c)
        sends = []
        for w in range(n):
            half = shards[w].shape[0] // 2
            mine = _rows(outs[w], c * half, half)
            cp = _remote(mine, mine, send_sems.at[w], recv_sems.at[w], sibling)
            cp.start()
            sends.append(cp)
        for w in range(n):
            half = shards[w].shape[0] // 2
            blk = _rows(outs[w], (1 - c) * half, half)
            _remote(blk, blk, send_sems.at[w], recv_sems.at[w], sibling).wait_recv()
        for cp in sends:
            cp.wait_send()

    return pl.pallas_call(
        body,
        name=name,
        in_specs=[ANY] * n,
        out_specs=[ANY] * n,
        out_shape=[jax.ShapeDtypeStruct(s.shape, s.dtype) for s in shards],
        input_output_aliases={w: w for w in range(n)},
        scratch_shapes=[pltpu.SemaphoreType.DMA((n,)), pltpu.SemaphoreType.DMA((n,))],
    )(*shards)


def _grad_w_in_t_and_small_all_reduce(dh_main, dkv, x, slab, dep=None):
    t = x.shape[0]
    steps = t // TK
    rows = slab.shape[0]
    part = rows // 8

    def body(dm_ref, dkv_ref, x_ref, slab_ref, grad_ref, sum_ref, landing, reduced, gathered, send_sems, recv_sems):
        k = pl.program_id(0)
        x_, y_, c_ = _mesh_position()
        me = 4 * x_ + 2 * y_ + c_
        flips = [(f >> 2, (f >> 1) & 1, f & 1) for f in range(1, 8)]

        def peer(flip):
            fx, fy, fc = flip
            return (1 - x_ if fx else x_, 1 - y_ if fy else y_, 1 - c_ if fc else c_)

        def part_of(ref, device):
            return ref.at[pl.ds(pl.multiple_of(device * part, 8), part), :]

        def scatter_copies():
            out = []
            for kk, flip in enumerate(flips):
                px, py, pc = peer(flip)
                them = 4 * px + 2 * py + pc
                send = _remote(part_of(slab_ref, them), landing.at[me], send_sems.at[kk], recv_sems.at[kk], (px, py, pc))
                recv = _remote(landing.at[them], landing.at[them], send_sems.at[kk], recv_sems.at[kk], (px, py, pc))
                out.append((send, recv))
            return out

        def gather_copies():
            out = []
            for kk, flip in enumerate(flips):
                px, py, pc = peer(flip)
                them = 4 * px + 2 * py + pc
                send = _remote(reduced, part_of(gathered, me), send_sems.at[7 + kk], recv_sems.at[7 + kk], (px, py, pc))
                recv = _remote(part_of(gathered, them), part_of(gathered, them), send_sems.at[7 + kk], recv_sems.at[7 + kk], (px, py, pc))
                out.append((send, recv))
            return out

        @pl.when(k == 0)
        def _():
            grad_ref[...] = jnp.zeros_like(grad_ref)
            for send, _ in scatter_copies():
                send.start()
            landing[me] = part_of(slab_ref, me)[...]

        @pl.when(k == steps // 2)
        def _():
            for _, recv in scatter_copies():
                recv.wait_recv()
            total = landing[0]
            for s in range(1, 8):
                total = total + landing[s]
            reduced[...] = total
            part_of(gathered, me)[...] = total
            for send, _ in gather_copies():
                send.start()

        xb = x_ref[...].astype(BF16)
        grad_ref[0:D_MAIN, :] += _dot(dm_ref[...], xb, TN)
        grad_ref[D_MAIN:D_IN, :] += _dot(dkv_ref[...].astype(BF16), xb, TN)

        @pl.when(k == steps - 1)
        def _():
            for send, recv in gather_copies():
                recv.wait_recv()
                send.wait_send()
            for send, _ in scatter_copies():
                send.wait_send()
            sum_ref[...] = gathered[...]

    body, in_specs, operands = _after(
        dep, body, [_row_spec(TK, D_MAIN), _row_spec(TK, 2 * D_KV), _row_spec(TK, D_MODEL), _const_spec(slab.shape)], [dh_main, dkv, x, slab])
    return pl.pallas_call(
        body,
        name="grad_w_in_and_small_all_reduce",
        grid=(steps,),
        in_specs=in_specs,
        out_specs=[_const_spec((D_IN, D_MODEL), single_buffer=True), _const_spec(slab.shape)],
        out_shape=[jax.ShapeDtypeStruct((D_IN, D_MODEL), F32), jax.ShapeDtypeStruct(slab.shape, slab.dtype)],
        scratch_shapes=[
            pltpu.VMEM((8, part, LANES), F32),
            pltpu.VMEM((part, LANES), F32),
            pltpu.VMEM(slab.shape, F32),
            pltpu.SemaphoreType.DMA((14,)),
            pltpu.SemaphoreType.DMA((14,)),
        ],
        compiler_params=_params(("arbitrary",)),
    )(*operands)


HBM = pl.BlockSpec(memory_space=pltpu.HBM)
SEM = pl.BlockSpec(memory_space=pltpu.SEMAPHORE)
DATAFLOW = pltpu.SideEffectType.DATAFLOW_SIDE_EFFECTING
TOKEN = jax.ShapeDtypeStruct((8, LANES), F32)


def _plan_copies(bufs, plan, send_sems, recv_sems):
    out = []
    for i, (src, src_row, dst, dst_row, recv_row, rows, device) in enumerate(plan):
        send = _remote(_rows(bufs[src], src_row, rows), _rows(bufs[dst], dst_row, rows), send_sems.at[i], recv_sems.at[i], device)
        landed = _rows(bufs[dst], recv_row, rows)
        recv = _remote(landed, landed, send_sems.at[i], recv_sems.at[i], device)
        out.append((send, recv))
    return out


def _split_call(name, bufs, wait=None, start=None, after=None):
    n = len(bufs)
    n_in = n + (2 if wait else 0) + (1 if after is not None else 0)
    n_start = len(start(0, 0, 0)) if start else 0

    def body(*refs):
        ins = refs[:n]
        x, y, c = _mesh_position()
        if wait:
            for send, recv in _plan_copies(ins, wait[0](x, y, c), refs[n], refs[n + 1]):
                recv.wait_recv()
                send.wait_send()
        if start:
            for send, _ in _plan_copies(ins, start(x, y, c), refs[n_in + n + 1], refs[n_in + n + 2]):
                send.start()
        token = refs[n_in + n]
        token[...] = jnp.zeros_like(token)

    operands = [pltpu.with_memory_space_constraint(b, pltpu.HBM) for b in bufs]
    in_specs = [HBM] * n
    if wait:
        operands += [wait[1], wait[2]]
        in_specs += [SEM, SEM]
    if after is not None:
        operands.append(after)
        in_specs.append(ANY)
    out_shape = [pltpu.HBM(b.shape, b.dtype) for b in bufs] + [TOKEN]
    out_specs = [HBM] * n + [pl.BlockSpec(memory_space=pltpu.VMEM)]
    if start:
        out_shape += [pltpu.SemaphoreType.DMA((n_start,)), pltpu.SemaphoreType.DMA((n_start,))]
        out_specs += [SEM, SEM]
    outs = pl.pallas_call(
        body,
        name=name,
        in_specs=in_specs,
        out_specs=out_specs,
        out_shape=out_shape,
        input_output_aliases={i: i for i in range(n)},
        compiler_params=pltpu.CompilerParams(has_side_effects=DATAFLOW),
    )(*operands)
    return (list(outs[:n]), outs[n]) + tuple(outs[n + 1 :])


def _direct_gather_plans(shard_rows):
    n = len(shard_rows)

    def direct(x, y, c):
        me = 2 * x + y
        plan = []
        for w, rows in enumerate(shard_rows):
            half = rows // 2
            for px, py in _other_chips(x, y):
                plan.append((w, c * half, n + w, me * rows + c * half, (2 * px + py) * rows + c * half, half, (px, py, c)))
            plan.append((w, 0, n + w, me * rows, me * rows, rows, (x, y, 1 - c)))
        return plan

    def passed_on(x, y, c):
        plan = []
        for w, rows in enumerate(shard_rows):
            half = rows // 2
            for px, py in _other_chips(x, y):
                row = (2 * px + py) * rows
                plan.append((n + w, row + c * half, n + w, row + c * half, row + (1 - c) * half, half, (x, y, 1 - c)))
        return plan

    return direct, passed_on


def _gather_plans(shard_rows):
    n = len(shard_rows)

    def neighbours(x, y):
        return ((1 - x, y), (x, 1 - y))

    def direct(x, y, c):
        me = 2 * x + y
        plan = []
        for w, rows in enumerate(shard_rows):
            half = rows // 2
            for px, py in neighbours(x, y):
                plan.append((w, c * half, n + w, me * rows + c * half, (2 * px + py) * rows + c * half, half, (px, py, c)))
            plan.append((w, 0, n + w, me * rows, me * rows, rows, (x, y, 1 - c)))
        return plan

    def passed_on(x, y, c):
        (xn, yn), diagonal = neighbours(x, y), 2 * (1 - x) + (1 - y)
        relayed = (1 - c) * (2 * xn[0] + xn[1]) + c * (2 * yn[0] + yn[1])
        target = (x * (1 - c) + (1 - x) * c, (1 - y) * (1 - c) + y * c, c)
        plan = []
        for w, rows in enumerate(shard_rows):
            half = rows // 2
            for px, py in (xn, yn):
                row = (2 * px + py) * rows
                plan.append((n + w, row + c * half, n + w, row + c * half, row + (1 - c) * half, half, (x, y, 1 - c)))
            plan.append((n + w, relayed * rows + c * half, n + w, relayed * rows + c * half, diagonal * rows + c * half, half, target))
        return plan

    def diagonal_passed_on(x, y, c):
        plan = []
        for w, rows in enumerate(shard_rows):
            half = rows // 2
            row = (2 * (1 - x) + (1 - y)) * rows
            plan.append((n + w, row + c * half, n + w, row + c * half, row + (1 - c) * half, half, (x, y, 1 - c)))
        return plan

    return direct, passed_on, diagonal_passed_on


def _swap_plan(block_rows):
    n = len(block_rows)

    def plan_fn(x, y, c):
        plan = []
        for w, rows in enumerate(block_rows):
            half = rows // 2
            for j in range(N_CHIPS):
                plan.append((w, j * rows + (1 - c) * half, n + w, j * half, j * half, half, (x, y, 1 - c)))
        return plan

    return plan_fn


def _exchange_plan(halves):
    n = len(halves)

    def plan_fn(x, y, c):
        plan = []
        for w, half in enumerate(halves):
            for kk, (px, py) in enumerate(_other_chips(x, y)):
                plan.append((w, (2 * px + py) * half, n + w, kk * half, kk * half, half, (px, py, c)))
        return plan

    return plan_fn


def _landing(rows, cols, dtype):
    return lax.empty((rows, cols), dtype)


def _row_tile(rows, cap=512):
    best = 8
    for cand in range(8, cap + 1, 8):
        if rows % cand == 0:
            best = cand
    return best


def _pair_sum(name, grad, theirs, pos):
    half = theirs.shape[0] // N_CHIPS
    cols = theirs.shape[1]
    tile = _row_tile(half)
    steps = half // tile

    def body(pos_ref, g_ref, t_ref, p_ref, own_ref):
        total = g_ref[...] + t_ref[...]
        p_ref[...] = total.astype(BF16)

        @pl.when(pl.program_id(1) == pos_ref[1])
        def _():
            own_ref[...] = total

    return pl.pallas_call(
        body,
        name=name,
        grid_spec=pltpu.PrefetchScalarGridSpec(
            num_scalar_prefetch=1,
            grid=(steps, N_CHIPS),
            in_specs=[
                pl.BlockSpec((tile, cols), lambda i, j, pos: ((2 * j + pos[0]) * steps + i, 0)),
                pl.BlockSpec((tile, cols), lambda i, j, pos: (j * steps + i, 0)),
            ],
            out_specs=[
                pl.BlockSpec((tile, cols), lambda i, j, pos: (j * steps + i, 0)),
                pl.BlockSpec((tile, cols), lambda i, j, pos: (i, 0)),
            ],
        ),
        out_shape=[jax.ShapeDtypeStruct((N_CHIPS * half, cols), BF16), jax.ShapeDtypeStruct((half, cols), F32)],
        compiler_params=_params(("parallel", "arbitrary")),
    )(pos, grad, theirs)


def _chip_sum(name, own, landed, pos):
    half, cols = own.shape
    tile = _row_tile(half)
    steps = half // tile

    def body(pos_ref, own_ref, l0, l1, l2, o_ref):
        o_ref[...] = ((own_ref[...] + l0[...].astype(F32)) + l1[...].astype(F32)) + l2[...].astype(F32)

    landed_specs = [pl.BlockSpec((tile, cols), lambda i, pos, _k=k: (_k * steps + i, 0)) for k in range(N_CHIPS - 1)]
    return pl.pallas_call(
        body,
        name=name,
        grid_spec=pltpu.PrefetchScalarGridSpec(
            num_scalar_prefetch=1,
            grid=(steps,),
            in_specs=[pl.BlockSpec((tile, cols), lambda i, pos: (i, 0))] + landed_specs,
            out_specs=pl.BlockSpec((tile, cols), lambda i, pos: (pos[0] * steps + i, 0)),
        ),
        out_shape=jax.ShapeDtypeStruct((2 * half, cols), F32),
        compiler_params=_params(("parallel",)),
    )(pos, own, landed, landed, landed)


def _adamw(name, w, g, m, v):
    rows, cols = w.shape
    tile = rows if rows * cols <= 256 * 1024 else _row_tile(rows)

    def body(w_ref, g_ref, m_ref, v_ref, g_out_ref, d_ref, nm_ref, nv_ref):
        g = g_ref[...]
        g_out_ref[...] = g
        nm = ADAM_B1 * m_ref[...] + (1.0 - ADAM_B1) * g
        nv = ADAM_B2 * v_ref[...] + (1.0 - ADAM_B2) * (g * g)
        m_hat = nm / (1.0 - ADAM_B1**ADAM_STEP)
        v_hat = nv / (1.0 - ADAM_B2**ADAM_STEP)
        d_ref[...] = -ADAM_LR * (m_hat / (jnp.sqrt(v_hat) + ADAM_EPS) + ADAM_WD * w_ref[...])
        nm_ref[...] = nm
        nv_ref[...] = nv

    spec = _row_spec(tile, cols)
    return pl.pallas_call(
        body,
        name=name,
        grid=(rows // tile,),
        in_specs=[spec] * 4,
        out_specs=[spec] * 4,
        out_shape=[jax.ShapeDtypeStruct((rows, cols), F32)] * 4,
        compiler_params=_params(("parallel",)),
    )(w, g, m, v)


_SMALL = (
    ("v_ln_g", (D_GMLP,), 8),
    ("v_ln_b", (D_GMLP,), 8),
    ("w_spatial", (N_HEADS, CHUNK, CHUNK), 1024),
    ("b_spatial", (N_HEADS, CHUNK), 8),
    ("sinks", (N_HEADS,), 8),
    ("ln1_g", (D_MODEL,), 8),
    ("ln1_b", (D_MODEL,), 8),
    ("ln2_g", (D_MODEL,), 8),
    ("ln2_b", (D_MODEL,), 8),
    ("squared_error", (D_MODEL,), 8),
)
N_SMALL_PARAMS = len(_SMALL) - 1


def _pack_small(values):
    parts = []
    for (name, shape, rows), val in zip(_SMALL, values, strict=True):
        flat = val.reshape(-1).astype(F32)
        parts.append(jnp.pad(flat, (0, rows * LANES - flat.shape[0])).reshape(rows, LANES))
    parts.append(jnp.zeros((SMALL_ROWS - sum(rows for _, _, rows in _SMALL), LANES), F32))
    return jnp.concatenate(parts, axis=0)


def _adamw_update(w, g, m, v):
    nm = ADAM_B1 * m + (1.0 - ADAM_B1) * g
    nv = ADAM_B2 * v + (1.0 - ADAM_B2) * (g * g)
    m_hat = nm / (1.0 - ADAM_B1**ADAM_STEP)
    v_hat = nv / (1.0 - ADAM_B2**ADAM_STEP)
    return -ADAM_LR * (m_hat / (jnp.sqrt(v_hat) + ADAM_EPS) + ADAM_WD * w), nm, nv


def _adamw_small(g_slab, params, first, second):
    n = N_SMALL_PARAMS

    def pieces(shape):
        if len(shape) == 3:
            return [((0, h), h * shape[1], shape[1], shape[2]) for h in range(shape[0])]
        if len(shape) == 2:
            return [((0,), 0, shape[0], shape[1])]
        if shape[0] >= LANES:
            return [((slice(None), slice(r * LANES, (r + 1) * LANES)), r, 1, LANES) for r in range(shape[0] // LANES)]
        return [((slice(None), slice(0, shape[0])), 0, 1, shape[0])]

    def body(*refs):
        g_ref = refs[0]
        w_refs, m_refs, v_refs = refs[1 : 1 + n], refs[1 + n : 1 + 2 * n], refs[1 + 2 * n : 1 + 3 * n]
        outs = refs[1 + 3 * n :]
        row0 = 0
        for idx, (_, shape, rows) in enumerate(_SMALL[:n]):
            for where, first_row, n_rows, lanes in pieces(shape):
                g = g_ref[row0 + first_row : row0 + first_row + n_rows, 0:lanes]
                delta, nm, nv = _adamw_update(w_refs[idx][where], g, m_refs[idx][where], v_refs[idx][where])
                for group, val in enumerate((g, delta, nm, nv)):
                    outs[group * n + idx][where] = val
            row0 += rows

    vmem = pl.BlockSpec(memory_space=pltpu.VMEM)
    shapes = [jax.ShapeDtypeStruct(p.shape, F32) for p in params]
    outs = pl.pallas_call(
        body,
        name="adamw_small",
        in_specs=[vmem] * (1 + 3 * n),
        out_specs=[vmem] * (4 * n),
        out_shape=shapes * 4,
        compiler_params=_params(),
    )(g_slab, *params, *first, *second)
    return [list(outs[group * n : (group + 1) * n]) for group in range(4)]


def kernel(x, positions, w_in, v_ln_g, v_ln_b, w_spatial, b_spatial, sinks, w_out, ln1_g, ln1_b, w_ff1, w_ff2, ln2_g, ln2_b, loss_target, m_w_in, m_v_ln_g, m_v_ln_b, m_w_spatial, m_b_spatial, m_sinks, m_w_out, m_ln1_g, m_ln1_b, m_w_ff1, m_w_ff2, m_ln2_g, m_ln2_b, v_w_in, v_v_ln_g, v_v_ln_b, v_w_spatial, v_b_spatial, v_sinks, v_w_out, v_ln1_g, v_ln1_b, v_w_ff1, v_w_ff2, v_ln2_g, v_ln2_b):
    t = x.shape[1]
    x2 = x.reshape(t, D_MODEL)
    target = loss_target.reshape(t, D_MODEL)

    w_in_shard = w_in[0].T.astype(BF16)
    in_direct, in_pass = _direct_gather_plans([w_in_shard.shape[0]])
    in_bufs, in_started, in_send, in_recv = _split_call(
        "gather_w_in_start", [w_in_shard, _landing(N_CHIPS * w_in_shard.shape[0], D_MODEL, BF16)], start=in_direct)
    inv_freq = ROPE_THETA ** (-jnp.arange(0, HEAD_DIM, 2, dtype=F32) / HEAD_DIM)
    cos, sin, later = _rope_tables_and_casts(
        positions, jnp.tile(inv_freq, LANES // (HEAD_DIM // 2)).reshape(1, LANES), [w_out[0], w_ff1[0], w_ff2[0]], dep=in_started)
    later_rows = [s.shape[0] for s in later]
    direct_plan, pass_plan, diagonal_plan = _gather_plans(later_rows)
    bufs, started, direct_send, direct_recv = _split_call(
        "gather_start", later + [_landing(N_CHIPS * r, D_MODEL, BF16) for r in later_rows], start=direct_plan, after=cos)
    in_bufs, in_passing, in_pass_send, in_pass_recv = _split_call(
        "gather_w_in_pass", in_bufs, wait=(in_direct, in_send, in_recv), start=in_pass, after=started)
    in_bufs, _ = _split_call("gather_w_in_end", in_bufs, wait=(in_pass, in_pass_send, in_pass_recv), after=in_passing)
    w_in_t = in_bufs[1]

    u, vg, q, k, va = _in_proj(x2, w_in_t, cos, sin)
    bias_full = jnp.repeat(b_spatial[0].T, HEAD_DIM, axis=1)
    sink_vec = sinks.reshape(N_HEADS)
    bufs, passing, pass_send, pass_recv = _split_call(
        "gather_pass", bufs, wait=(direct_plan, direct_send, direct_recv), start=pass_plan, after=u)
    cat = _mixer_fwd(u, vg, q, k, va, v_ln_g, v_ln_b, w_spatial[0], bias_full, sink_vec, dep=passing)
    bufs, passing, diag_send, diag_recv = _split_call(
        "gather_pass_diagonal", bufs, wait=(pass_plan, pass_send, pass_recv), start=diagonal_plan, after=cat)
    bufs, _ = _split_call("gather_end", bufs, wait=(diagonal_plan, diag_send, diag_recv), after=passing)
    w_out_all = bufs[3]
    w1_all = bufs[4].reshape(N_FF_BLOCKS, D_MODEL, D_MODEL)
    w2_all = bufs[5].reshape(N_FF_BLOCKS, D_MODEL, D_MODEL)
    xhat1, rstd1, x1b, r, dz2, dz2b, d_ln2_g, d_ln2_b, sq_err = _ffn_fwd_loss(
        cat, x2, w_out_all, ln1_g, ln1_b, w1_all, w2_all, ln2_g, ln2_b, target)

    pos = jnp.stack([lax.axis_index("c"), 2 * lax.axis_index("x") + lax.axis_index("y")]).astype(jnp.int32)
    half_landing = lambda g: _landing(g.shape[0] // 2, D_MODEL, F32)
    ff_swap_plan = _swap_plan([D_FF // N_CHIPS])
    ff_exchange_plan = _exchange_plan([D_FF // N_CHIPS // 2])
    exchange_landing = lambda p: _landing(3 * p.shape[0] // N_CHIPS, D_MODEL, BF16)
    g_ff1_local, dz1, dz1b, dcat, d_ln1_g, d_ln1_b = _ffn_bwd_ln1(dz2, r, x1b, xhat1, rstd1, ln1_g, w1_all, w2_all, w_out_all)
    ff1_bufs, swapping1, swap1_send, swap1_recv = _split_call("ff1_swap_start", [g_ff1_local, half_landing(g_ff1_local)], start=ff_swap_plan)
    g_out_local = _grad_w_out(cat, dz1b, dep=swapping1)
    ff1_bufs, _ = _split_call("ff1_swap_wait", ff1_bufs, wait=(ff_swap_plan, swap1_send, swap1_recv), after=g_out_local)
    ff1_sum, ff1_own = _pair_sum("grad_pair_sum_w_ff1", ff1_bufs[0], ff1_bufs[1], pos)
    ff1_ex, exchanging1, ex1_send, ex1_recv = _split_call(
        "ff1_exchange_start", [ff1_sum, exchange_landing(ff1_sum)], start=ff_exchange_plan)
    dh_main, dkv, g_ff2_local, d_v_ln_g, d_v_ln_b, d_w_spatial, d_b_spatial_t, d_sinks = _mixer_bwd(
        u, vg, q, k, va, dcat, cos, sin, v_ln_g, v_ln_b, w_spatial[0], bias_full, sink_vec, r, dz2b, dep=exchanging1)
    ff2_bufs, swapping2, swap2_send, swap2_recv = _split_call("ff2_swap_start", [g_ff2_local, half_landing(g_ff2_local)], start=ff_swap_plan)
    g_in_local, small_g = _grad_w_in_t_and_small_all_reduce(dh_main, dkv, x2, _pack_small(
        [d_v_ln_g, d_v_ln_b, d_w_spatial, d_b_spatial_t[:, :N_HEADS].T, d_sinks[0, :N_HEADS], d_ln1_g, d_ln1_b, d_ln2_g, d_ln2_b, sq_err]),
        dep=swapping2)
    sq_row = sum(rows for _, _, rows in _SMALL[:N_SMALL_PARAMS])
    loss = 0.5 * jnp.sum(small_g[sq_row : sq_row + _SMALL[N_SMALL_PARAMS][2]]) / D_MODEL
    ff2_bufs, _ = _split_call("ff2_swap_wait", ff2_bufs, wait=(ff_swap_plan, swap2_send, swap2_recv), after=g_in_local)
    ff2_sum, ff2_own = _pair_sum("grad_pair_sum_w_ff2", ff2_bufs[0], ff2_bufs[1], pos)
    ff2_ex, exchanging2, ex2_send, ex2_recv = _split_call(
        "ff2_exchange_start", [ff2_sum, exchange_landing(ff2_sum)], start=ff_exchange_plan)

    small = [g_in_local, g_out_local]
    small_swap_plan = _swap_plan([g.shape[0] // N_CHIPS for g in small])
    swap_bufs, small_swapping, ss_send, ss_recv = _split_call(
        "small_swap_start", small + [half_landing(g) for g in small], start=small_swap_plan, after=exchanging2)
    grad_x_flat = _grad_x(dh_main, dkv, dz1, w_in_t, dep=small_swapping)
    grad_x = grad_x_flat.reshape(1, t, D_MODEL)
    swap_bufs, _ = _split_call("small_swap_wait", swap_bufs, wait=(small_swap_plan, ss_send, ss_recv), after=grad_x_flat)
    pair_sums = [_pair_sum("grad_pair_sum_" + nm, g, th, pos) for nm, g, th in zip(["w_in", "w_out"], swap_bufs[:2], swap_bufs[2:])]
    small_plan = _exchange_plan([p.shape[0] // N_CHIPS for p, _ in pair_sums])
    small_bufs, small_exchanging, sm_send, sm_recv = _split_call(
        "small_exchange_start", [p for p, _ in pair_sums] + [exchange_landing(p) for p, _ in pair_sums], start=small_plan)

    ff1_ex, _ = _split_call("ff1_exchange_wait", ff1_ex, wait=(ff_exchange_plan, ex1_send, ex1_recv), after=small_exchanging)
    (g_w_ff1,) = _pair_gather("grad_pair_gather_ff1", [_chip_sum("grad_chip_sum_w_ff1", ff1_own, ff1_ex[1], pos)])
    g_w_ff1, d_w_ff1, nm_w_ff1, nv_w_ff1 = _adamw("adamw_w_ff1", w_ff1[0], g_w_ff1, m_w_ff1[0], v_w_ff1[0])
    ff2_ex, _ = _split_call("ff2_exchange_wait", ff2_ex, wait=(ff_exchange_plan, ex2_send, ex2_recv), after=nv_w_ff1)
    (g_w_ff2,) = _pair_gather("grad_pair_gather_ff2", [_chip_sum("grad_chip_sum_w_ff2", ff2_own, ff2_ex[1], pos)])
    g_w_ff2, d_w_ff2, nm_w_ff2, nv_w_ff2 = _adamw("adamw_w_ff2", w_ff2[0], g_w_ff2, m_w_ff2[0], v_w_ff2[0])
    small_bufs, _ = _split_call("small_exchange_wait", small_bufs, wait=(small_plan, sm_send, sm_recv), after=nv_w_ff2)
    shards = [_chip_sum("grad_chip_sum_" + nm, own, ld, pos) for nm, (_, own), ld in zip(["w_in", "w_out"], pair_sums, small_bufs[2:])]
    g_w_in_t, g_w_out = _pair_gather("grad_pair_gather_small", shards)
    g_w_in, d_w_in, nm_w_in, nv_w_in = (a.T for a in _adamw("adamw_w_in", w_in[0].T, g_w_in_t, m_w_in[0].T, v_w_in[0].T))
    g_w_out, d_w_out, nm_w_out, nv_w_out = _adamw("adamw_w_out", w_out[0], g_w_out, m_w_out[0], v_w_out[0])
    small_grads, small_d, small_nm, small_nv = _adamw_small(
        small_g,
        [v_ln_g, v_ln_b, w_spatial, b_spatial, sinks, ln1_g, ln1_b, ln2_g, ln2_b],
        [m_v_ln_g, m_v_ln_b, m_w_spatial, m_b_spatial, m_sinks, m_ln1_g, m_ln1_b, m_ln2_g, m_ln2_b],
        [v_v_ln_g, v_v_ln_b, v_w_spatial, v_b_spatial, v_sinks, v_ln1_g, v_ln1_b, v_ln2_g, v_ln2_b])

    def with_big(small, w_in_v, w_out_v, w_ff1_v, w_ff2_v):
        g_vg, g_vb, g_ws, g_bs, g_sk, g_1g, g_1b, g_2g, g_2b = small
        return [w_in_v[None], g_vg, g_vb, g_ws, g_bs, g_sk, w_out_v[None], g_1g, g_1b, w_ff1_v[None], w_ff2_v[None], g_2g, g_2b]

    return (
        loss,
        grad_x,
        *with_big(small_grads, g_w_in, g_w_out, g_w_ff1, g_w_ff2),
        *with_big(small_d, d_w_in, d_w_out, d_w_ff1, d_w_ff2),
        *with_big(small_nm, nm_w_in, nm_w_out, nm_w_ff1, nm_w_ff2),
        *with_big(small_nv, nv_w_in, nv_w_out, nv_w_ff1, nv_w_ff2),
    )
```

```python
import math

import jax
import jax.numpy as jnp
from jax import lax
from jax.experimental import pallas as pl
from jax.experimental.pallas import tpu as pltpu

F32 = jnp.float32
BF16 = jnp.bfloat16

D_MODEL = 1024
HEAD_DIM = 64
D_GMLP = 512
D_ATTN = 512
D_KV = 128
D_IN = 2 * D_GMLP + D_ATTN + 2 * D_KV
D_MAIN = 2 * D_GMLP + D_ATTN
N_HEADS = 8
CHUNK = 128
CHUNKS_PER_STEP = 4
ROPE_THETA = 10000.0
D_FF = 4 * D_MODEL
N_FF_BLOCKS = 4
LN_EPS = 1e-5
ALPHA = (2.0 * 1) ** 0.25
NEG_INF = -1e30
SCALE = 1.0 / math.sqrt(HEAD_DIM)

ADAM_LR = 0.001
ADAM_B1 = 0.9
ADAM_B2 = 0.999
ADAM_EPS = 1e-08
ADAM_WD = 0.01
ADAM_STEP = 10

N_CHIPS = 4
LANES = 128
V7X_VMEM_BYTES = 64 * 1024 * 1024
VMEM_LIMIT = V7X_VMEM_BYTES - 8 * 1024 * 1024
TM = 512
TM_FFN = 256
TK = 1024
SMALL_ROWS = 1152
MESH = pl.DeviceIdType.MESH

NT = (((1,), (1,)), ((), ()))
TN = (((0,), (0,)), ((), ()))


def _dot(a, b, dims=None):
    if dims is None:
        return jnp.dot(a, b, preferred_element_type=F32)
    return lax.dot_general(a, b, dims, preferred_element_type=F32)


def _params(semantics=None):
    return pltpu.CompilerParams(dimension_semantics=semantics, vmem_limit_bytes=VMEM_LIMIT)


def _const_spec(shape, single_buffer=False):
    zeros = (0,) * len(shape)
    if single_buffer:
        return pl.BlockSpec(shape, lambda *_: zeros, pipeline_mode=pl.Buffered(1))
    return pl.BlockSpec(shape, lambda *_: zeros)


def _row_spec(rows, cols):
    return pl.BlockSpec((rows, cols), lambda i: (i, 0))


def _after(dep, body, in_specs, operands):
    if dep is None:
        return body, list(in_specs), list(operands)
    return (lambda dep_ref, *refs: body(*refs)), [pl.BlockSpec(memory_space=pl.ANY)] + list(in_specs), [dep] + list(operands)


def _gelu(x):
    k = math.sqrt(2.0 / math.pi)
    return 0.5 * x * (1.0 + jnp.tanh(k * (x + 0.044715 * (x * x * x))))


def _gelu_and_grad(x):
    k = math.sqrt(2.0 / math.pi)
    x2 = x * x
    t = jnp.tanh(k * (x + 0.044715 * (x2 * x)))
    g = 0.5 * x * (1.0 + t)
    dg = 0.5 * (1.0 + t) + 0.5 * x * (1.0 - t * t) * (k * (1.0 + 3.0 * 0.044715 * x2))
    return g, dg


def _layer_norm_stats(z):
    mu = jnp.mean(z, axis=-1, keepdims=True)
    zc = z - mu
    var = jnp.mean(zc * zc, axis=-1, keepdims=True)
    rstd = lax.rsqrt(var + LN_EPS)
    return zc * rstd, rstd


def _layer_norm_bwd(dxhat, xhat, rstd):
    m1 = jnp.mean(dxhat, axis=-1, keepdims=True)
    m2 = jnp.mean(dxhat * xhat, axis=-1, keepdims=True)
    return rstd * (dxhat - m1 - xhat * m2)


def _rotate_half(t):
    n = t.shape[1]
    lane = lax.broadcasted_iota(jnp.int32, t.shape, 1)
    first = (lane & (HEAD_DIM // 2)) == 0
    return jnp.where(first, -pltpu.roll(t, n - HEAD_DIM // 2, 1), pltpu.roll(t, HEAD_DIM // 2, 1))


def _rope(t, cos, sin):
    return t * cos + _rotate_half(t) * sin


def _rope_transposed(g, cos, sin):
    return g * cos - _rotate_half(g * sin)


def _lane_tile(a, reps):
    return jnp.tile(a, (1, reps)) if reps > 1 else a


def _in_proj(x, w_in_t, cos, sin, dep=None):
    t = x.shape[0]

    def body(x_ref, w_ref, cos_ref, sin_ref, u_ref, vg_ref, q_ref, k_ref, va_ref):
        xb = x_ref[...].astype(BF16)
        u_ref[...] = _dot(xb, w_ref[0:D_GMLP, :], NT)
        vg_ref[...] = _dot(xb, w_ref[D_GMLP : 2 * D_GMLP, :], NT)
        q = _dot(xb, w_ref[2 * D_GMLP : D_MAIN, :], NT)
        k = _dot(xb, w_ref[D_MAIN : D_MAIN + D_KV, :], NT)
        va_ref[...] = _dot(xb, w_ref[D_MAIN + D_KV : D_IN, :], NT).astype(BF16)
        c, s = cos_ref[...], sin_ref[...]
        q_ref[...] = _rope(q, _lane_tile(c, D_ATTN // LANES), _lane_tile(s, D_ATTN // LANES)).astype(BF16)
        k_ref[...] = _rope(k, c, s).astype(BF16)

    body, in_specs, operands = _after(
        dep, body, [_row_spec(TM, D_MODEL), _const_spec((D_IN, D_MODEL)), _row_spec(TM, LANES), _row_spec(TM, LANES)], [x, w_in_t, cos, sin])
    return pl.pallas_call(
        body,
        name="in_proj",
        grid=(t // TM,),
        in_specs=in_specs,
        out_specs=[_row_spec(TM, D_GMLP), _row_spec(TM, D_GMLP), _row_spec(TM, D_ATTN), _row_spec(TM, D_KV), _row_spec(TM, D_KV)],
        out_shape=[
            jax.ShapeDtypeStruct((t, D_GMLP), F32),
            jax.ShapeDtypeStruct((t, D_GMLP), F32),
            jax.ShapeDtypeStruct((t, D_ATTN), BF16),
            jax.ShapeDtypeStruct((t, D_KV), BF16),
            jax.ShapeDtypeStruct((t, D_KV), BF16),
        ],
        compiler_params=_params(("parallel",)),
    )(*operands)


def _step_rows(i):
    return (i, 0)


def _chunk_before_step(i):
    return (jnp.maximum(CHUNKS_PER_STEP * i - 1, 0), 0)


def _chunk_specs():
    step = CHUNKS_PER_STEP * CHUNK
    return [
        pl.BlockSpec((step, D_GMLP), _step_rows),
        pl.BlockSpec((step, D_GMLP), _step_rows),
        pl.BlockSpec((step, D_ATTN), _step_rows),
        pl.BlockSpec((step, D_KV), _step_rows),
        pl.BlockSpec((CHUNK, D_KV), _chunk_before_step),
        pl.BlockSpec((step, D_KV), _step_rows),
        pl.BlockSpec((CHUNK, D_KV), _chunk_before_step),
    ]


def _half_lane_masks(rows):
    lane = lax.broadcasted_iota(jnp.int32, (rows, LANES), 1)
    return lane < HEAD_DIM


def _kv_variants(kv2):
    left = _half_lane_masks(kv2.shape[0])
    f = kv2.astype(F32)
    swapped = pltpu.roll(f, HEAD_DIM, 1)
    zero = jnp.zeros_like(f)
    g0 = (jnp.where(left, f, zero).astype(BF16), jnp.where(left, zero, swapped).astype(BF16))
    g1 = (jnp.where(left, swapped, zero).astype(BF16), jnp.where(left, zero, f).astype(BF16))
    return (g0, g1)


def _band_mask(i, heads=1):
    row = lax.broadcasted_iota(jnp.int32, (heads * CHUNK, 2 * CHUNK), 0) & (CHUNK - 1)
    col = lax.broadcasted_iota(jnp.int32, (heads * CHUNK, 2 * CHUNK), 1)
    no_prev = jnp.where(i > 0, 0, 4 * CHUNK)
    in_prev = jnp.logical_and(col < CHUNK, (col - row) > no_prev)
    in_cur = jnp.logical_and(col >= CHUNK, (col - CHUNK) <= row)
    return jnp.logical_or(in_prev, in_cur)


def _causal_mask():
    row = lax.broadcasted_iota(jnp.int32, (CHUNK, CHUNK), 0)
    col = lax.broadcasted_iota(jnp.int32, (CHUNK, CHUNK), 1)
    return col <= row


def _store_spatial_weights(w_ref, wcat_ref, wcat_t_ref=None):
    causal = _causal_mask()
    for p in range(D_GMLP // LANES):
        wl = jnp.where(causal, w_ref[2 * p], 0.0)
        wr = jnp.where(causal, w_ref[2 * p + 1], 0.0)
        wcat_ref[p] = jnp.concatenate([wl, wr], axis=1).astype(BF16)
        if wcat_t_ref is not None:
            wcat_t_ref[p] = jnp.concatenate([wl.T, wr.T], axis=1).astype(BF16)


def _pair_stack(xp, left):
    return jnp.concatenate([jnp.where(left, xp, 0.0), jnp.where(left, 0.0, xp)], axis=0).astype(BF16)


def _mixer_fwd(u, vg, q, k, va, v_ln_g, v_ln_b, w_spatial, bias_full, sinks, dep=None):
    t = u.shape[0]

    def body(u_ref, vg_ref, q_ref, kc_ref, kp_ref, vc_ref, vp_ref, g_ref, b_ref, w_ref, bias_ref, sink_ref, cat_ref, wcat):
        i = pl.program_id(0)
        left = _half_lane_masks(CHUNK)

        @pl.when(i == 0)
        def _():
            _store_spatial_weights(w_ref, wcat)

        heads = range(N_HEADS)
        pair_cols = [slice(p * LANES, (p + 1) * LANES) for p in range(D_GMLP // LANES)]
        sinks_h = [sink_ref[h] for h in heads]
        for c in range(CHUNKS_PER_STEP):
            rows = slice(c * CHUNK, (c + 1) * CHUNK)
            before = slice((c - 1) * CHUNK, c * CHUNK)
            k_prev = kp_ref[...] if c == 0 else kc_ref[before, :]
            v_prev = vp_ref[...] if c == 0 else vc_ref[before, :]
            k_var = _kv_variants(jnp.concatenate([k_prev, kc_ref[rows, :]], axis=0))
            v_var = _kv_variants(jnp.concatenate([v_prev, vc_ref[rows, :]], axis=0))
            scores = [_dot(q_ref[rows, pair_cols[h // 2]], k_var[h // 4][h % 2], NT) for h in heads]

            ug = _gelu(u_ref[rows, :])
            xhat, _ = _layer_norm_stats(_gelu(vg_ref[rows, :]))
            vgl = xhat * g_ref[...] + b_ref[...]
            mixed = [_dot(wcat[p], _pair_stack(vgl[:, cols], left)) for p, cols in enumerate(pair_cols)]

            valid = _band_mask(CHUNKS_PER_STEP * i + c)
            masked = [jnp.where(valid, scores[h] * SCALE, NEG_INF) for h in heads]
            maxes = [jnp.maximum(jnp.max(masked[h], axis=1, keepdims=True), sinks_h[h]) for h in heads]
            exps = [jnp.exp(masked[h] - maxes[h]) for h in heads]
            invs = [1.0 / (jnp.sum(exps[h], axis=1, keepdims=True) + jnp.exp(sinks_h[h] - maxes[h])) for h in heads]
            probs = [(exps[h] * invs[h]).astype(BF16) for h in heads]
            for p, cols in enumerate(pair_cols):
                cat_ref[rows, cols] = (ug[:, cols] * (mixed[p] + bias_ref[:, cols])).astype(BF16)
            for p in range(D_ATTN // LANES):
                out = _dot(probs[2 * p], v_var[p // 2][0]) + _dot(probs[2 * p + 1], v_var[p // 2][1])
                cat_ref[rows, D_GMLP + p * LANES : D_GMLP + (p + 1) * LANES] = out.astype(BF16)

    in_specs = _chunk_specs() + [
        _const_spec((1, D_GMLP)),
        _const_spec((1, D_GMLP)),
        _const_spec((N_HEADS, CHUNK, CHUNK)),
        _const_spec((CHUNK, D_GMLP)),
        pl.BlockSpec(memory_space=pltpu.SMEM),
    ]
    body, in_specs, operands = _after(dep, body, in_specs, [u, vg, q, k, k, va, va, v_ln_g, v_ln_b, w_spatial, bias_full, sinks])
    return pl.pallas_call(
        body,
        name="mixer_fwd",
        grid=(t // (CHUNKS_PER_STEP * CHUNK),),
        in_specs=in_specs,
        out_specs=pl.BlockSpec((CHUNKS_PER_STEP * CHUNK, D_MODEL), lambda i: (i, 0)),
        out_shape=jax.ShapeDtypeStruct((t, D_MODEL), BF16),
        scratch_shapes=[pltpu.VMEM((D_GMLP // LANES, CHUNK, 2 * CHUNK), BF16)],
        compiler_params=_params(("arbitrary",)),
    )(*operands)


def _ffn_fwd_loss(cat, x, w_out, ln1_g, ln1_b, w1, w2, ln2_g, ln2_b, target):
    t = x.shape[0]

    def body(cat_ref, x_ref, wo_ref, g1_ref, b1_ref, w1_ref, w2_ref, g2_ref, b2_ref, tgt_ref,
             xh_ref, rstd_ref, x1b_ref, r_ref, dz2_ref, dz2b_ref, dg2_ref, db2_ref, sq_ref):
        @pl.when(pl.program_id(0) == 0)
        def _():
            dg2_ref[...] = jnp.zeros_like(dg2_ref)
            db2_ref[...] = jnp.zeros_like(db2_ref)
            sq_ref[...] = jnp.zeros_like(sq_ref)

        xhat1, rstd1 = _layer_norm_stats(ALPHA * x_ref[...] + _dot(cat_ref[...], wo_ref[...]))
        xh_ref[...] = xhat1
        rstd_ref[...] = rstd1
        x1 = xhat1 * g1_ref[...] + b1_ref[...]
        x1b = x1.astype(BF16)
        x1b_ref[...] = x1b
        ff = jnp.zeros((TM_FFN, D_MODEL), F32)
        for j in range(N_FF_BLOCKS):
            r = jnp.maximum(_dot(x1b, w1_ref[j]), 0.0)
            r_ref[:, j * D_MODEL : (j + 1) * D_MODEL] = r.astype(BF16)
            ff = ff + _dot((r * r).astype(BF16), w2_ref[j])
        xhat2, rstd2 = _layer_norm_stats(ALPHA * x1 + ff)
        err = xhat2 * g2_ref[...] + b2_ref[...] - tgt_ref[...]
        sq_ref[...] += jnp.sum(err * err, axis=0, keepdims=True)
        dy = err * (1.0 / D_MODEL)
        dg2_ref[...] += jnp.sum(dy * xhat2, axis=0, keepdims=True)
        db2_ref[...] += jnp.sum(dy, axis=0, keepdims=True)
        dz2 = _layer_norm_bwd(dy * g2_ref[...], xhat2, rstd2)
        dz2_ref[...] = dz2
        dz2b_ref[...] = dz2.astype(BF16)

    vec = _const_spec((1, D_MODEL))
    tile = _row_spec(TM_FFN, D_MODEL)
    wspec = _const_spec((N_FF_BLOCKS, D_MODEL, D_MODEL), single_buffer=True)
    return pl.pallas_call(
        body,
        name="ffn_fwd_loss",
        grid=(t // TM_FFN,),
        in_specs=[tile, tile, _const_spec((D_MODEL, D_MODEL), single_buffer=True), vec, vec, wspec, wspec, vec, vec, tile],
        out_specs=[tile, _row_spec(TM_FFN, 1), tile, _row_spec(TM_FFN, D_FF), tile, tile, vec, vec, vec],
        out_shape=[
            jax.ShapeDtypeStruct((t, D_MODEL), F32),
            jax.ShapeDtypeStruct((t, 1), F32),
            jax.ShapeDtypeStruct((t, D_MODEL), BF16),
            jax.ShapeDtypeStruct((t, D_FF), BF16),
            jax.ShapeDtypeStruct((t, D_MODEL), F32),
            jax.ShapeDtypeStruct((t, D_MODEL), BF16),
            jax.ShapeDtypeStruct((1, D_MODEL), F32),
            jax.ShapeDtypeStruct((1, D_MODEL), F32),
            jax.ShapeDtypeStruct((1, D_MODEL), F32),
        ],
        compiler_params=_params(("arbitrary",)),
    )(cat, x, w_out, ln1_g, ln1_b, w1, w2, ln2_g, ln2_b, target)


def _ffn_bwd_ln1(dz2, r, x1b, xhat1, rstd1, ln1_g, w1, w2, w_out, dep=None):
    t = dz2.shape[0]

    def body(dz2_ref, r_ref, x1b_ref, xh_ref, rstd_ref, g1_ref, w1_ref, w2_ref, wo_ref, gw1_ref, dz1_ref, dz1b_ref, dcat_ref, dg1_ref, db1_ref):
        @pl.when(pl.program_id(0) == 0)
        def _():
            dg1_ref[...] = jnp.zeros_like(dg1_ref)
            db1_ref[...] = jnp.zeros_like(db1_ref)
            gw1_ref[...] = jnp.zeros_like(gw1_ref)

        dz2 = dz2_ref[...]
        dz2b = dz2.astype(BF16)
        x1_t = x1b_ref[...].astype(F32).T.astype(BF16)
        dx1 = ALPHA * dz2
        for j in range(N_FF_BLOCKS):
            cols = slice(j * D_MODEL, (j + 1) * D_MODEL)
            dpre = (_dot(dz2b, w2_ref[j], NT) * (2.0 * r_ref[:, cols].astype(F32))).astype(BF16)
            gw1_ref[cols, :] += _dot(x1_t, dpre)
            dx1 = dx1 + _dot(dpre, w1_ref[j], NT)
        xhat1 = xh_ref[...]
        dg1_ref[...] += jnp.sum(dx1 * xhat1, axis=0, keepdims=True)
        db1_ref[...] += jnp.sum(dx1, axis=0, keepdims=True)
        dz1 = _layer_norm_bwd(dx1 * g1_ref[...], xhat1, rstd_ref[...])
        dz1_ref[...] = dz1
        dz1b = dz1.astype(BF16)
        dz1b_ref[...] = dz1b
        dcat_ref[...] = _dot(dz1b, wo_ref[...], NT).astype(BF16)

    vec = _const_spec((1, D_MODEL))
    tile = _row_spec(TM_FFN, D_MODEL)
    wspec = _const_spec((N_FF_BLOCKS, D_MODEL, D_MODEL), single_buffer=True)
    body, in_specs, operands = _after(
        dep, body,
        [tile, _row_spec(TM_FFN, D_FF), tile, tile, _row_spec(TM_FFN, 1), vec, wspec, wspec, _const_spec((D_MODEL, D_MODEL), single_buffer=True)],
        [dz2, r, x1b, xhat1, rstd1, ln1_g, w1, w2, w_out])
    return pl.pallas_call(
        body,
        name="ffn_bwd_ln1",
        grid=(t // TM_FFN,),
        in_specs=in_specs,
        out_specs=[_const_spec((D_FF, D_MODEL), single_buffer=True), tile, tile, tile, vec, vec],
        out_shape=[
            jax.ShapeDtypeStruct((D_FF, D_MODEL), F32),
            jax.ShapeDtypeStruct((t, D_MODEL), F32),
            jax.ShapeDtypeStruct((t, D_MODEL), BF16),
            jax.ShapeDtypeStruct((t, D_MODEL), BF16),
            jax.ShapeDtypeStruct((1, D_MODEL), F32),
            jax.ShapeDtypeStruct((1, D_MODEL), F32),
        ],
        compiler_params=_params(("arbitrary",)),
    )(*operands)


def _mixer_bwd(u, vg, q, k, va, dcat, cos, sin, v_ln_g, v_ln_b, w_spatial, bias_full, sinks, r, dz2b, dep=None):
    t = u.shape[0]
    n_chunks = t // CHUNK
    assert CHUNKS_PER_STEP == N_FF_BLOCKS

    def body(u_ref, vg_ref, q_ref, kc_ref, kp_ref, vc_ref, vp_ref, dcat_ref, cosc_ref, sinc_ref, cosp_ref, sinp_ref,
             g_ref, b_ref, w_ref, bias_ref, sink_ref, r_ref, dz2b_ref,
             dmain_ref, dkv_ref, gw2_ref, dg_ref, db_ref, dw_ref, dbs_ref, dsink_ref, dmix_acc, wcat, wcat_t):
        i = pl.program_id(0)
        left = _half_lane_masks(CHUNK)
        lane = lax.broadcasted_iota(jnp.int32, (CHUNK, LANES), 1)
        n_pairs = D_GMLP // LANES

        @pl.when(i == 0)
        def _():
            dg_ref[...] = jnp.zeros_like(dg_ref)
            db_ref[...] = jnp.zeros_like(db_ref)
            dw_ref[...] = jnp.zeros_like(dw_ref)
            dsink_ref[...] = jnp.zeros_like(dsink_ref)
            dmix_acc[...] = jnp.zeros_like(dmix_acc)
            gw2_ref[...] = jnp.zeros_like(gw2_ref)
            _store_spatial_weights(w_ref, wcat, wcat_t)

        n_qpairs = D_ATTN // LANES
        heads = range(N_HEADS)
        pair_cols = [slice(p * LANES, (p + 1) * LANES) for p in range(n_pairs)]
        sinks_h = [sink_ref[h] for h in heads]
        gain = g_ref[...]
        causal = _causal_mask()
        lane_row = lax.broadcasted_iota(jnp.int32, (1, LANES), 1)
        heads_per_group = N_HEADS // 2

        def group_grad_t(lhs_t, rhs_heads):
            parts = []
            for g in range(2):
                group = range(g * heads_per_group, (g + 1) * heads_per_group)
                lhs = jnp.concatenate([lhs_t[h * HEAD_DIM : (h + 1) * HEAD_DIM] for h in group], axis=1)
                parts.append(_dot(lhs, jnp.concatenate([rhs_heads[h] for h in group], axis=0)))
            return jnp.concatenate(parts, axis=0)

        for c in range(CHUNKS_PER_STEP):
            chunk = CHUNKS_PER_STEP * i + c
            rows = slice(c * CHUNK, (c + 1) * CHUNK)
            before = slice((c - 1) * CHUNK, c * CHUNK)

            k_prev = kp_ref[...] if c == 0 else kc_ref[before, :]
            v_prev = vp_ref[...] if c == 0 else vc_ref[before, :]
            k_var = _kv_variants(jnp.concatenate([k_prev, kc_ref[rows, :]], axis=0))
            v_var = _kv_variants(jnp.concatenate([v_prev, vc_ref[rows, :]], axis=0))
            q_pairs = [q_ref[rows, cols] for cols in pair_cols]
            do_all = dcat_ref[rows, D_GMLP:D_MODEL]
            do_pairs = [do_all[:, cols] for cols in pair_cols]
            scores = [_dot(q_pairs[h // 2], k_var[h // 4][h % 2], NT) for h in heads]
            dprobs = [_dot(do_pairs[h // 2], v_var[h // 4][h % 2], NT) for h in heads]
            q_t = q_ref[rows, :].astype(F32).T.astype(BF16)
            do_t = do_all.astype(F32).T.astype(BF16)

            ff_cols = slice(c * D_MODEL, (c + 1) * D_MODEL)
            relu_block = r_ref[:, ff_cols]
            gw2_ref[ff_cols, :] += _dot(relu_block * relu_block, dz2b_ref[...], TN)

            ug, dug_du = _gelu_and_grad(u_ref[rows, :])
            gv, dgv_dv = _gelu_and_grad(vg_ref[rows, :])
            xhat, rstd = _layer_norm_stats(gv)
            vgl = xhat * gain + b_ref[...]
            mixed = [_dot(wcat[p], _pair_stack(vgl[:, cols], left)) for p, cols in enumerate(pair_cols)]

            valid = _band_mask(chunk)
            masked = [jnp.where(valid, scores[h] * SCALE, NEG_INF) for h in heads]
            maxes = [jnp.maximum(jnp.max(masked[h], axis=1, keepdims=True), sinks_h[h]) for h in heads]
            exps = [jnp.exp(masked[h] - maxes[h]) for h in heads]
            exp_sinks = [jnp.exp(sinks_h[h] - maxes[h]) for h in heads]
            invs = [1.0 / (jnp.sum(exps[h], axis=1, keepdims=True) + exp_sinks[h]) for h in heads]
            probs = [exps[h] * invs[h] for h in heads]
            dsums = [jnp.sum(probs[h] * dprobs[h], axis=1, keepdims=True) for h in heads]
            ds_b = [(probs[h] * (dprobs[h] - dsums[h]) * SCALE).astype(BF16) for h in heads]
            probs_b = [probs[h].astype(BF16) for h in heads]

            dm_stacks = []
            for p, cols in enumerate(pair_cols):
                da = dcat_ref[rows, cols].astype(F32)
                dmain_ref[rows, cols] = (da * (mixed[p] + bias_ref[:, cols]) * dug_du[:, cols]).astype(BF16)
                dmixed = da * ug[:, cols]
                dmix_acc[:, cols] += dmixed
                dm_stacks.append(_pair_stack(dmixed, left))

            dq_all = jnp.concatenate(
                [_dot(ds_b[2 * p], k_var[p // 2][0]) + _dot(ds_b[2 * p + 1], k_var[p // 2][1]) for p in range(n_qpairs)], axis=1)
            dk2_t = group_grad_t(q_t, ds_b)
            dv2_t = group_grad_t(do_t, probs_b)

            for p, cols in enumerate(pair_cols):
                dw_pair = _dot(dm_stacks[p], vgl[:, cols].astype(BF16), NT)
                dw_ref[2 * p] += jnp.where(causal, dw_pair[:CHUNK], 0.0)
                dw_ref[2 * p + 1] += jnp.where(causal, dw_pair[CHUNK:], 0.0)
            dvgl = jnp.concatenate([_dot(wcat_t[p], dm_stacks[p]) for p in range(n_pairs)], axis=1)

            dsink_row = jnp.zeros((1, LANES), F32)
            for h in heads:
                d_sink = -jnp.sum(exp_sinks[h] * invs[h] * dsums[h], axis=0, keepdims=True)
                dsink_row = dsink_row + jnp.where(lane_row == h, d_sink, 0.0)
            dsink_ref[0:1, :] += dsink_row
            cos_c, sin_c = cosc_ref[rows, :], sinc_ref[rows, :]
            cos_p = cosp_ref[...] if c == 0 else cosc_ref[before, :]
            sin_p = sinp_ref[...] if c == 0 else sinc_ref[before, :]
            dmain_ref[rows, 2 * D_GMLP : D_MAIN] = _rope_transposed(dq_all, _lane_tile(cos_c, n_qpairs), _lane_tile(sin_c, n_qpairs)).astype(BF16)
            dk2 = dk2_t.T
            dv2 = dv2_t.T
            cur = pl.ds(pl.multiple_of(chunk * CHUNK, CHUNK), CHUNK)
            dkv_ref[cur, 0:D_KV] = _rope_transposed(dk2[CHUNK:], cos_c, sin_c)
            dkv_ref[cur, D_KV : 2 * D_KV] = dv2[CHUNK:]
            prev = pl.ds(pl.multiple_of(jnp.maximum(chunk - 1, 0) * CHUNK, CHUNK), CHUNK)
            dkv_ref[prev, 0:D_KV] += _rope_transposed(dk2[:CHUNK], cos_p, sin_p)
            dkv_ref[prev, D_KV : 2 * D_KV] += dv2[:CHUNK]

            dg_ref[...] += jnp.sum(dvgl * xhat, axis=0, keepdims=True)
            db_ref[...] += jnp.sum(dvgl, axis=0, keepdims=True)
            dgv = _layer_norm_bwd(dvgl * gain, xhat, rstd)
            dmain_ref[rows, D_GMLP : 2 * D_GMLP] = (dgv * dgv_dv).astype(BF16)

        @pl.when(i == n_chunks // CHUNKS_PER_STEP - 1)
        def _():
            tile = jnp.zeros((CHUNK, LANES), F32)
            for p, cols in enumerate(pair_cols):
                dm = dmix_acc[:, cols]
                sl = jnp.sum(jnp.where(left, dm, 0.0), axis=1, keepdims=True)
                sr = jnp.sum(jnp.where(left, 0.0, dm), axis=1, keepdims=True)
                tile = jnp.where(lane == 2 * p, sl, tile)
                tile = jnp.where(lane == 2 * p + 1, sr, tile)
            dbs_ref[...] = tile

    step = CHUNKS_PER_STEP * CHUNK
    in_specs = _chunk_specs() + [
        pl.BlockSpec((step, D_MODEL), _step_rows),
        pl.BlockSpec((step, LANES), _step_rows),
        pl.BlockSpec((step, LANES), _step_rows),
        pl.BlockSpec((CHUNK, LANES), _chunk_before_step),
        pl.BlockSpec((CHUNK, LANES), _chunk_before_step),
        _const_spec((1, D_GMLP)),
        _const_spec((1, D_GMLP)),
        _const_spec((N_HEADS, CHUNK, CHUNK)),
        _const_spec((CHUNK, D_GMLP)),
        pl.BlockSpec(memory_space=pltpu.SMEM),
        pl.BlockSpec((step, D_FF), _step_rows),
        pl.BlockSpec((step, D_MODEL), _step_rows),
    ]
    body, in_specs, operands = _after(
        dep, body, in_specs, [u, vg, q, k, k, va, va, dcat, cos, sin, cos, sin, v_ln_g, v_ln_b, w_spatial, bias_full, sinks, r, dz2b])
    return pl.pallas_call(
        body,
        name="mixer_bwd",
        grid=(n_chunks // CHUNKS_PER_STEP,),
        in_specs=in_specs,
        out_specs=[
            pl.BlockSpec((step, D_MAIN), _step_rows),
            _const_spec((t, 2 * D_KV)),
            _const_spec((D_FF, D_MODEL), single_buffer=True),
            _const_spec((1, D_GMLP)),
            _const_spec((1, D_GMLP)),
            _const_spec((N_HEADS, CHUNK, CHUNK)),
            _const_spec((CHUNK, LANES)),
            _const_spec((8, LANES)),
        ],
        out_shape=[
            jax.ShapeDtypeStruct((t, D_MAIN), BF16),
            jax.ShapeDtypeStruct((t, 2 * D_KV), F32),
            jax.ShapeDtypeStruct((D_FF, D_MODEL), F32),
            jax.ShapeDtypeStruct((1, D_GMLP), F32),
            jax.ShapeDtypeStruct((1, D_GMLP), F32),
            jax.ShapeDtypeStruct((N_HEADS, CHUNK, CHUNK), F32),
            jax.ShapeDtypeStruct((CHUNK, LANES), F32),
            jax.ShapeDtypeStruct((8, LANES), F32),
        ],
        scratch_shapes=[
            pltpu.VMEM((CHUNK, D_GMLP), F32),
            pltpu.VMEM((D_GMLP // LANES, CHUNK, 2 * CHUNK), BF16),
            pltpu.VMEM((D_GMLP // LANES, CHUNK, 2 * CHUNK), BF16),
        ],
        compiler_params=_params(("arbitrary",)),
    )(*operands)


def _grad_x(dh_main, dkv, dz1, w_in_t, dep=None):
    t = dz1.shape[0]

    def body(dm_ref, dkv_ref, dz1_ref, w_ref, gx_ref):
        acc = ALPHA * dz1_ref[...] + _dot(dm_ref[...], w_ref[0:D_MAIN, :])
        gx_ref[...] = acc + _dot(dkv_ref[...].astype(BF16), w_ref[D_MAIN:D_IN, :])

    body, in_specs, operands = _after(
        dep, body, [_row_spec(TM, D_MAIN), _row_spec(TM, 2 * D_KV), _row_spec(TM, D_MODEL), _const_spec((D_IN, D_MODEL))], [dh_main, dkv, dz1, w_in_t])
    return pl.pallas_call(
        body,
        name="grad_x",
        grid=(t // TM,),
        in_specs=in_specs,
        out_specs=_row_spec(TM, D_MODEL),
        out_shape=jax.ShapeDtypeStruct((t, D_MODEL), F32),
        compiler_params=_params(("parallel",)),
    )(*operands)


def _token_contraction(name, out_rows, tk, in_arrays, contributions, dep=None):
    t = in_arrays[0].shape[0]

    def body(*refs):
        out_ref = refs[-1]

        @pl.when(pl.program_id(0) == 0)
        def _():
            out_ref[...] = jnp.zeros_like(out_ref)

        for row0, a, b in contributions(*refs[:-1]):
            out_ref[row0 : row0 + a.shape[1], :] += _dot(a, b, TN)

    in_specs = [_row_spec(tk, a.shape[1]) for a in in_arrays]
    body, in_specs, operands = _after(dep, body, in_specs, in_arrays)
    return pl.pallas_call(
        body,
        name=name,
        grid=(t // tk,),
        in_specs=in_specs,
        out_specs=_const_spec((out_rows, D_MODEL), single_buffer=True),
        out_shape=jax.ShapeDtypeStruct((out_rows, D_MODEL), F32),
        compiler_params=_params(("arbitrary",)),
    )(*operands)


def _grad_w_out(cat, dz1b, dep=None):
    def contributions(cat_ref, dz1_ref):
        return [(0, cat_ref[...], dz1_ref[...])]

    return _token_contraction("grad_w_out", D_MODEL, TK, [cat, dz1b], contributions, dep)


ANY = pl.BlockSpec(memory_space=pl.ANY)


def _mesh_position():
    return lax.axis_index("x"), lax.axis_index("y"), lax.axis_index("c")


def _other_chips(x, y):
    return [(1 - x, y), (x, 1 - y), (1 - x, 1 - y)]


def _remote(src, dst, send_sem, recv_sem, device):
    return pltpu.make_async_remote_copy(src_ref=src, dst_ref=dst, send_sem=send_sem, recv_sem=recv_sem, device_id=device, device_id_type=MESH)


def _rows(ref, start, size):
    return ref.at[pl.ds(start, size), :]


def _rope_tables_and_casts(pos_row, inv_freq_row, shards, dep=None):
    t = pos_row.shape[1]
    steps = t // TM
    n = len(shards)

    def body(pos_ref, f_ref, *rest):
        f32_refs, (cos_ref, sin_ref), bf16_refs = rest[:n], rest[n : n + 2], rest[n + 2 :]
        for src, dst in zip(f32_refs, bf16_refs):
            dst[...] = src[...].astype(BF16)
        pos_rows = jnp.broadcast_to(pos_ref[...].astype(F32), (LANES, TM)).T
        ang = pos_rows * f_ref[...]
        cos_ref[...] = jnp.cos(ang)
        sin_ref[...] = jnp.sin(ang)

    shard_specs = [_row_spec(s.shape[0] // steps, s.shape[1]) for s in shards]
    body, in_specs, operands = _after(
        dep, body, [pl.BlockSpec((1, TM), lambda i: (0, i)), _const_spec((1, LANES))] + shard_specs, [pos_row, inv_freq_row, *shards])
    outs = pl.pallas_call(
        body,
        name="rope_tables_and_casts",
        grid=(steps,),
        in_specs=in_specs,
        out_specs=[_row_spec(TM, LANES), _row_spec(TM, LANES)] + shard_specs,
        out_shape=[jax.ShapeDtypeStruct((t, LANES), F32)] * 2 + [jax.ShapeDtypeStruct(s.shape, BF16) for s in shards],
        compiler_params=_params(("parallel",)),
    )(*operands)
    return outs[0], outs[1], list(outs[2:])


def _grad_w_in_t_and_small_all_reduce(dh_main, dkv, x, slab, dep=None):
    t = x.shape[0]
    steps = t // TK
    rows = slab.shape[0]
    part = rows // 8

    def body(dm_ref, dkv_ref, x_ref, slab_ref, grad_ref, sum_ref, landing, reduced, gathered, send_sems, recv_sems):
        k = pl.program_id(0)
        x_, y_, c_ = _mesh_position()
        me = 4 * x_ + 2 * y_ + c_
        flips = [(f >> 2, (f >> 1) & 1, f & 1) for f in range(1, 8)]

        def peer(flip):
            fx, fy, fc = flip
            return (1 - x_ if fx else x_, 1 - y_ if fy else y_, 1 - c_ if fc else c_)

        def part_of(ref, device):
            return ref.at[pl.ds(pl.multiple_of(device * part, 8), part), :]

        def scatter_copies():
            out = []
            for kk, flip in enumerate(flips):
                px, py, pc = peer(flip)
                them = 4 * px + 2 * py + pc
                send = _remote(part_of(slab_ref, them), landing.at[me], send_sems.at[kk], recv_sems.at[kk], (px, py, pc))
                recv = _remote(landing.at[them], landing.at[them], send_sems.at[kk], recv_sems.at[kk], (px, py, pc))
                out.append((send, recv))
            return out

        def gather_copies():
            out = []
            for kk, flip in enumerate(flips):
                px, py, pc = peer(flip)
                them = 4 * px + 2 * py + pc
                send = _remote(reduced, part_of(gathered, me), send_sems.at[7 + kk], recv_sems.at[7 + kk], (px, py, pc))
                recv = _remote(part_of(gathered, them), part_of(gathered, them), send_sems.at[7 + kk], recv_sems.at[7 + kk], (px, py, pc))
                out.append((send, recv))
            return out

        @pl.when(k == 0)
        def _():
            grad_ref[...] = jnp.zeros_like(grad_ref)
            for send, _ in scatter_copies():
                send.start()
            landing[me] = part_of(slab_ref, me)[...]

        @pl.when(k == steps // 2)
        def _():
            for _, recv in scatter_copies():
                recv.wait_recv()
            total = landing[0]
            for s in range(1, 8):
                total = total + landing[s]
            reduced[...] = total
            part_of(gathered, me)[...] = total
            for send, _ in gather_copies():
                send.start()

        xb = x_ref[...].astype(BF16)
        grad_ref[0:D_MAIN, :] += _dot(dm_ref[...], xb, TN)
        grad_ref[D_MAIN:D_IN, :] += _dot(dkv_ref[...].astype(BF16), xb, TN)

        @pl.when(k == steps - 1)
        def _():
            for send, recv in gather_copies():
                recv.wait_recv()
                send.wait_send()
            for send, _ in scatter_copies():
                send.wait_send()
            sum_ref[...] = gathered[...]

    body, in_specs, operands = _after(
        dep, body, [_row_spec(TK, D_MAIN), _row_spec(TK, 2 * D_KV), _row_spec(TK, D_MODEL), _const_spec(slab.shape)], [dh_main, dkv, x, slab])
    return pl.pallas_call(
        body,
        name="grad_w_in_and_small_all_reduce",
        grid=(steps,),
        in_specs=in_specs,
        out_specs=[_const_spec((D_IN, D_MODEL), single_buffer=True), _const_spec(slab.shape)],
        out_shape=[jax.ShapeDtypeStruct((D_IN, D_MODEL), F32), jax.ShapeDtypeStruct(slab.shape, slab.dtype)],
        scratch_shapes=[
            pltpu.VMEM((8, part, LANES), F32),
            pltpu.VMEM((part, LANES), F32),
            pltpu.VMEM(slab.shape, F32),
            pltpu.SemaphoreType.DMA((14,)),
            pltpu.SemaphoreType.DMA((14,)),
        ],
        compiler_params=_params(("arbitrary",)),
    )(*operands)


HBM = pl.BlockSpec(memory_space=pltpu.HBM)
SEM = pl.BlockSpec(memory_space=pltpu.SEMAPHORE)
DATAFLOW = pltpu.SideEffectType.DATAFLOW_SIDE_EFFECTING
TOKEN = jax.ShapeDtypeStruct((8, LANES), F32)


def _plan_copies(bufs, plan, send_sems, recv_sems):
    out = []
    for i, (src, src_row, dst, dst_row, recv_row, rows, device) in enumerate(plan):
        send = _remote(_rows(bufs[src], src_row, rows), _rows(bufs[dst], dst_row, rows), send_sems.at[i], recv_sems.at[i], device)
        landed = _rows(bufs[dst], recv_row, rows)
        recv = _remote(landed, landed, send_sems.at[i], recv_sems.at[i], device)
        out.append((send, recv))
    return out


def _split_call(name, bufs, wait=None, start=None, after=None):
    n = len(bufs)
    n_in = n + (2 if wait else 0) + (1 if after is not None else 0)
    n_start = len(start(0, 0, 0)) if start else 0

    def body(*refs):
        ins = refs[:n]
        x, y, c = _mesh_position()
        if wait:
            for send, recv in _plan_copies(ins, wait[0](x, y, c), refs[n], refs[n + 1]):
                recv.wait_recv()
                send.wait_send()
        if start:
            for send, _ in _plan_copies(ins, start(x, y, c), refs[n_in + n + 1], refs[n_in + n + 2]):
                send.start()
        token = refs[n_in + n]
        token[...] = jnp.zeros_like(token)

    operands = [pltpu.with_memory_space_constraint(b, pltpu.HBM) for b in bufs]
    in_specs = [HBM] * n
    if wait:
        operands += [wait[1], wait[2]]
        in_specs += [SEM, SEM]
    if after is not None:
        operands.append(after)
        in_specs.append(ANY)
    out_shape = [pltpu.HBM(b.shape, b.dtype) for b in bufs] + [TOKEN]
    out_specs = [HBM] * n + [pl.BlockSpec(memory_space=pltpu.VMEM)]
    if start:
        out_shape += [pltpu.SemaphoreType.DMA((n_start,)), pltpu.SemaphoreType.DMA((n_start,))]
        out_specs += [SEM, SEM]
    outs = pl.pallas_call(
        body,
        name=name,
        in_specs=in_specs,
        out_specs=out_specs,
        out_shape=out_shape,
        input_output_aliases={i: i for i in range(n)},
        compiler_params=pltpu.CompilerParams(has_side_effects=DATAFLOW),
    )(*operands)
    return (list(outs[:n]), outs[n]) + tuple(outs[n + 1 :])


def _direct_gather_plans(shard_rows):
    n = len(shard_rows)

    def direct(x, y, c):
        me = 2 * x + y
        plan = []
        for w, rows in enumerate(shard_rows):
            half = rows // 2
            for px, py in _other_chips(x, y):
                plan.append((w, c * half, n + w, me * rows + c * half, (2 * px + py) * rows + c * half, half, (px, py, c)))
            plan.append((w, 0, n + w, me * rows, me * rows, rows, (x, y, 1 - c)))
        return plan

    def passed_on(x, y, c):
        plan = []
        for w, rows in enumerate(shard_rows):
            half = rows // 2
            for px, py in _other_chips(x, y):
                row = (2 * px + py) * rows
                plan.append((n + w, row + c * half, n + w, row + c * half, row + (1 - c) * half, half, (x, y, 1 - c)))
        return plan

    return direct, passed_on


def _gather_plans(shard_rows):
    n = len(shard_rows)

    def neighbours(x, y):
        return ((1 - x, y), (x, 1 - y))

    def direct(x, y, c):
        me = 2 * x + y
        plan = []
        for w, rows in enumerate(shard_rows):
            half = rows // 2
            for px, py in neighbours(x, y):
                plan.append((w, c * half, n + w, me * rows + c * half, (2 * px + py) * rows + c * half, half, (px, py, c)))
            plan.append((w, 0, n + w, me * rows, me * rows, rows, (x, y, 1 - c)))
        return plan

    def passed_on(x, y, c):
        (xn, yn), diagonal = neighbours(x, y), 2 * (1 - x) + (1 - y)
        relayed = (1 - c) * (2 * xn[0] + xn[1]) + c * (2 * yn[0] + yn[1])
        target = (x * (1 - c) + (1 - x) * c, (1 - y) * (1 - c) + y * c, c)
        plan = []
        for w, rows in enumerate(shard_rows):
            half = rows // 2
            for px, py in (xn, yn):
                row = (2 * px + py) * rows
                plan.append((n + w, row + c * half, n + w, row + c * half, row + (1 - c) * half, half, (x, y, 1 - c)))
            plan.append((n + w, relayed * rows + c * half, n + w, relayed * rows + c * half, diagonal * rows + c * half, half, target))
        return plan

    def diagonal_passed_on(x, y, c):
        plan = []
        for w, rows in enumerate(shard_rows):
            half = rows // 2
            row = (2 * (1 - x) + (1 - y)) * rows
            plan.append((n + w, row + c * half, n + w, row + c * half, row + (1 - c) * half, half, (x, y, 1 - c)))
        return plan

    return direct, passed_on, diagonal_passed_on


def _swap_plan(block_rows):
    n = len(block_rows)

    def plan_fn(x, y, c):
        plan = []
        for w, rows in enumerate(block_rows):
            half = rows // 2
            for j in range(N_CHIPS):
                plan.append((w, j * rows + (1 - c) * half, n + w, j * half, j * half, half, (x, y, 1 - c)))
        return plan

    return plan_fn


def _exchange_plan(halves):
    n = len(halves)

    def plan_fn(x, y, c):
        plan = []
        for w, half in enumerate(halves):
            for kk, (px, py) in enumerate(_other_chips(x, y)):
                plan.append((w, (2 * px + py) * half, n + w, kk * half, kk * half, half, (px, py, c)))
        return plan

    return plan_fn


def _landing(rows, cols, dtype):
    return lax.empty((rows, cols), dtype)


def _row_tile(rows, cap=512):
    best = 8
    for cand in range(8, cap + 1, 8):
        if rows % cand == 0:
            best = cand
    return best


def _pair_sum(name, grad, theirs, pos):
    half = theirs.shape[0] // N_CHIPS
    cols = theirs.shape[1]
    tile = _row_tile(half)
    steps = half // tile

    def body(pos_ref, g_ref, t_ref, p_ref, own_ref):
        total = g_ref[...] + t_ref[...]
        p_ref[...] = total.astype(BF16)

        @pl.when(pl.program_id(1) == pos_ref[1])
        def _():
            own_ref[...] = total

    return pl.pallas_call(
        body,
        name=name,
        grid_spec=pltpu.PrefetchScalarGridSpec(
            num_scalar_prefetch=1,
            grid=(steps, N_CHIPS),
            in_specs=[
                pl.BlockSpec((tile, cols), lambda i, j, pos: ((2 * j + pos[0]) * steps + i, 0)),
                pl.BlockSpec((tile, cols), lambda i, j, pos: (j * steps + i, 0)),
            ],
            out_specs=[
                pl.BlockSpec((tile, cols), lambda i, j, pos: (j * steps + i, 0)),
                pl.BlockSpec((tile, cols), lambda i, j, pos: (i, 0)),
            ],
        ),
        out_shape=[jax.ShapeDtypeStruct((N_CHIPS * half, cols), BF16), jax.ShapeDtypeStruct((half, cols), F32)],
        compiler_params=_params(("parallel", "arbitrary")),
    )(pos, grad, theirs)


def _adamw_update(w, g, m, v):
    nm = ADAM_B1 * m + (1.0 - ADAM_B1) * g
    nv = ADAM_B2 * v + (1.0 - ADAM_B2) * (g * g)
    m_hat = nm / (1.0 - ADAM_B1**ADAM_STEP)
    v_hat = nv / (1.0 - ADAM_B2**ADAM_STEP)
    return -ADAM_LR * (m_hat / (jnp.sqrt(v_hat) + ADAM_EPS) + ADAM_WD * w), nm, nv


def _chip_sum_and_adamw(name, own, landed, w, m, v, pos):
    half, cols = own.shape
    tile = _row_tile(half, cap=128)
    steps = half // tile

    def body(pos_ref, own_ref, l0, l1, l2, w_ref, m_ref, v_ref, g_ref, d_ref, nm_ref, nv_ref, mine, theirs, send_sems, recv_sems):
        i = pl.program_id(0)
        x, y, c = _mesh_position()

        def tile_copy(j):
            return _remote(mine.at[j], theirs.at[j], send_sems.at[j], recv_sems.at[j], (x, y, 1 - c))

        def update(g):
            g_ref[...] = g
            d_ref[...], nm_ref[...], nv_ref[...] = _adamw_update(w_ref[...], g, m_ref[...], v_ref[...])

        @pl.when(i < steps)
        def _():
            mine[i] = ((own_ref[...] + l0[...].astype(F32)) + l1[...].astype(F32)) + l2[...].astype(F32)
            tile_copy(i).start()
            update(mine[i])

        @pl.when(i >= steps)
        def _():
            tile_copy(i - steps).wait_recv()
            update(theirs[i - steps])

        @pl.when(i == 2 * steps - 1)
        def _():
            for j in range(steps):
                tile_copy(j).wait_send()

    reduced_tile = lambda i: jnp.minimum(i, steps - 1)
    shard_tile = lambda i, pos: jnp.where(i < steps, pos[0] * steps + i, (1 - pos[0]) * steps + i - steps)
    landed_specs = [pl.BlockSpec((tile, cols), lambda i, pos, _k=k: (_k * steps + reduced_tile(i), 0)) for k in range(N_CHIPS - 1)]
    shard_spec = pl.BlockSpec((tile, cols), lambda i, pos: (shard_tile(i, pos), 0))
    return pl.pallas_call(
        body,
        name=name,
        grid_spec=pltpu.PrefetchScalarGridSpec(
            num_scalar_prefetch=1,
            grid=(2 * steps,),
            in_specs=[pl.BlockSpec((tile, cols), lambda i, pos: (reduced_tile(i), 0))] + landed_specs + [shard_spec] * 3,
            out_specs=[shard_spec] * 4,
            scratch_shapes=[
                pltpu.VMEM((steps, tile, cols), F32),
                pltpu.VMEM((steps, tile, cols), F32),
                pltpu.SemaphoreType.DMA((steps,)),
                pltpu.SemaphoreType.DMA((steps,)),
            ],
        ),
        out_shape=[jax.ShapeDtypeStruct((2 * half, cols), F32)] * 4,
        compiler_params=_params(("arbitrary",)),
    )(pos, own, landed, landed, landed, w, m, v)


_SMALL = (
    ("v_ln_g", (D_GMLP,), 8),
    ("v_ln_b", (D_GMLP,), 8),
    ("w_spatial", (N_HEADS, CHUNK, CHUNK), 1024),
    ("b_spatial", (N_HEADS, CHUNK), 8),
    ("sinks", (N_HEADS,), 8),
    ("ln1_g", (D_MODEL,), 8),
    ("ln1_b", (D_MODEL,), 8),
    ("ln2_g", (D_MODEL,), 8),
    ("ln2_b", (D_MODEL,), 8),
    ("squared_error", (D_MODEL,), 8),
)
N_SMALL_PARAMS = len(_SMALL) - 1


def _pack_small(values):
    parts = []
    for (name, shape, rows), val in zip(_SMALL, values, strict=True):
        flat = val.reshape(-1).astype(F32)
        parts.append(jnp.pad(flat, (0, rows * LANES - flat.shape[0])).reshape(rows, LANES))
    parts.append(jnp.zeros((SMALL_ROWS - sum(rows for _, _, rows in _SMALL), LANES), F32))
    return jnp.concatenate(parts, axis=0)


def _adamw_small(g_slab, params, first, second):
    n = N_SMALL_PARAMS

    def pieces(shape):
        if len(shape) == 3:
            return [((0, h), h * shape[1], shape[1], shape[2]) for h in range(shape[0])]
        if len(shape) == 2:
            return [((0,), 0, shape[0], shape[1])]
        if shape[0] >= LANES:
            return [((slice(None), slice(r * LANES, (r + 1) * LANES)), r, 1, LANES) for r in range(shape[0] // LANES)]
        return [((slice(None), slice(0, shape[0])), 0, 1, shape[0])]

    def body(*refs):
        g_ref = refs[0]
        w_refs, m_refs, v_refs = refs[1 : 1 + n], refs[1 + n : 1 + 2 * n], refs[1 + 2 * n : 1 + 3 * n]
        outs = refs[1 + 3 * n :]
        row0 = 0
        for idx, (_, shape, rows) in enumerate(_SMALL[:n]):
            for where, first_row, n_rows, lanes in pieces(shape):
                g = g_ref[row0 + first_row : row0 + first_row + n_rows, 0:lanes]
                delta, nm, nv = _adamw_update(w_refs[idx][where], g, m_refs[idx][where], v_refs[idx][where])
                for group, val in enumerate((g, delta, nm, nv)):
                    outs[group * n + idx][where] = val
            row0 += rows

    vmem = pl.BlockSpec(memory_space=pltpu.VMEM)
    shapes = [jax.ShapeDtypeStruct(p.shape, F32) for p in params]
    outs = pl.pallas_call(
        body,
        name="adamw_small",
        in_specs=[vmem] * (1 + 3 * n),
        out_specs=[vmem] * (4 * n),
        out_shape=shapes * 4,
        compiler_params=_params(),
    )(g_slab, *params, *first, *second)
    return [list(outs[group * n : (group + 1) * n]) for group in range(4)]


def kernel(x, positions, w_in, v_ln_g, v_ln_b, w_spatial, b_spatial, sinks, w_out, ln1_g, ln1_b, w_ff1, w_ff2, ln2_g, ln2_b, loss_target, m_w_in, m_v_ln_g, m_v_ln_b, m_w_spatial, m_b_spatial, m_sinks, m_w_out, m_ln1_g, m_ln1_b, m_w_ff1, m_w_ff2, m_ln2_g, m_ln2_b, v_w_in, v_v_ln_g, v_v_ln_b, v_w_spatial, v_b_spatial, v_sinks, v_w_out, v_ln1_g, v_ln1_b, v_w_ff1, v_w_ff2, v_ln2_g, v_ln2_b):
    t = x.shape[1]
    x2 = x.reshape(t, D_MODEL)
    target = loss_target.reshape(t, D_MODEL)

    w_in_shard = w_in[0].T.astype(BF16)
    in_direct, in_pass = _direct_gather_plans([w_in_shard.shape[0]])
    in_bufs, in_started, in_send, in_recv = _split_call(
        "gather_w_in_start", [w_in_shard, _landing(N_CHIPS * w_in_shard.shape[0], D_MODEL, BF16)], start=in_direct)
    inv_freq = ROPE_THETA ** (-jnp.arange(0, HEAD_DIM, 2, dtype=F32) / HEAD_DIM)
    cos, sin, later = _rope_tables_and_casts(
        positions, jnp.tile(inv_freq, LANES // (HEAD_DIM // 2)).reshape(1, LANES), [w_out[0], w_ff1[0], w_ff2[0]], dep=in_started)
    later_rows = [s.shape[0] for s in later]
    direct_plan, pass_plan, diagonal_plan = _gather_plans(later_rows)
    bufs, started, direct_send, direct_recv = _split_call(
        "gather_start", later + [_landing(N_CHIPS * r, D_MODEL, BF16) for r in later_rows], start=direct_plan, after=cos)
    in_bufs, in_passing, in_pass_send, in_pass_recv = _split_call(
        "gather_w_in_pass", in_bufs, wait=(in_direct, in_send, in_recv), start=in_pass, after=started)
    in_bufs, _ = _split_call("gather_w_in_end", in_bufs, wait=(in_pass, in_pass_send, in_pass_recv), after=in_passing)
    w_in_t = in_bufs[1]

    u, vg, q, k, va = _in_proj(x2, w_in_t, cos, sin)
    bias_full = jnp.repeat(b_spatial[0].T, HEAD_DIM, axis=1)
    sink_vec = sinks.reshape(N_HEADS)
    bufs, passing, pass_send, pass_recv = _split_call(
        "gather_pass", bufs, wait=(direct_plan, direct_send, direct_recv), start=pass_plan, after=u)
    cat = _mixer_fwd(u, vg, q, k, va, v_ln_g, v_ln_b, w_spatial[0], bias_full, sink_vec, dep=passing)
    bufs, passing, diag_send, diag_recv = _split_call(
        "gather_pass_diagonal", bufs, wait=(pass_plan, pass_send, pass_recv), start=diagonal_plan, after=cat)
    bufs, _ = _split_call("gather_end", bufs, wait=(diagonal_plan, diag_send, diag_recv), after=passing)
    w_out_all = bufs[3]
    w1_all = bufs[4].reshape(N_FF_BLOCKS, D_MODEL, D_MODEL)
    w2_all = bufs[5].reshape(N_FF_BLOCKS, D_MODEL, D_MODEL)
    xhat1, rstd1, x1b, r, dz2, dz2b, d_ln2_g, d_ln2_b, sq_err = _ffn_fwd_loss(
        cat, x2, w_out_all, ln1_g, ln1_b, w1_all, w2_all, ln2_g, ln2_b, target)

    pos = jnp.stack([lax.axis_index("c"), 2 * lax.axis_index("x") + lax.axis_index("y")]).astype(jnp.int32)
    half_landing = lambda g: _landing(g.shape[0] // 2, D_MODEL, F32)
    ff_swap_plan = _swap_plan([D_FF // N_CHIPS])
    ff_exchange_plan = _exchange_plan([D_FF // N_CHIPS // 2])
    exchange_landing = lambda p: _landing(3 * p.shape[0] // N_CHIPS, D_MODEL, BF16)
    g_ff1_local, dz1, dz1b, dcat, d_ln1_g, d_ln1_b = _ffn_bwd_ln1(dz2, r, x1b, xhat1, rstd1, ln1_g, w1_all, w2_all, w_out_all)
    ff1_bufs, swapping1, swap1_send, swap1_recv = _split_call("ff1_swap_start", [g_ff1_local, half_landing(g_ff1_local)], start=ff_swap_plan)
    g_out_local = _grad_w_out(cat, dz1b, dep=swapping1)
    ff1_bufs, _ = _split_call("ff1_swap_wait", ff1_bufs, wait=(ff_swap_plan, swap1_send, swap1_recv), after=g_out_local)
    ff1_sum, ff1_own = _pair_sum("grad_pair_sum_w_ff1", ff1_bufs[0], ff1_bufs[1], pos)
    ff1_ex, exchanging1, ex1_send, ex1_recv = _split_call(
        "ff1_exchange_start", [ff1_sum, exchange_landing(ff1_sum)], start=ff_exchange_plan)
    dh_main, dkv, g_ff2_local, d_v_ln_g, d_v_ln_b, d_w_spatial, d_b_spatial_t, d_sinks = _mixer_bwd(
        u, vg, q, k, va, dcat, cos, sin, v_ln_g, v_ln_b, w_spatial[0], bias_full, sink_vec, r, dz2b, dep=exchanging1)
    ff2_bufs, swapping2, swap2_send, swap2_recv = _split_call("ff2_swap_start", [g_ff2_local, half_landing(g_ff2_local)], start=ff_swap_plan)
    g_in_local, small_g = _grad_w_in_t_and_small_all_reduce(dh_main, dkv, x2, _pack_small(
        [d_v_ln_g, d_v_ln_b, d_w_spatial, d_b_spatial_t[:, :N_HEADS].T, d_sinks[0, :N_HEADS], d_ln1_g, d_ln1_b, d_ln2_g, d_ln2_b, sq_err]),
        dep=swapping2)
    sq_row = sum(rows for _, _, rows in _SMALL[:N_SMALL_PARAMS])
    loss = 0.5 * jnp.sum(small_g[sq_row : sq_row + _SMALL[N_SMALL_PARAMS][2]]) / D_MODEL
    ff2_bufs, _ = _split_call("ff2_swap_wait", ff2_bufs, wait=(ff_swap_plan, swap2_send, swap2_recv), after=g_in_local)
    ff2_sum, ff2_own = _pair_sum("grad_pair_sum_w_ff2", ff2_bufs[0], ff2_bufs[1], pos)
    ff2_ex, exchanging2, ex2_send, ex2_recv = _split_call(
        "ff2_exchange_start", [ff2_sum, exchange_landing(ff2_sum)], start=ff_exchange_plan)

    small = [g_in_local, g_out_local]
    small_swap_plan = _swap_plan([g.shape[0] // N_CHIPS for g in small])
    swap_bufs, small_swapping, ss_send, ss_recv = _split_call(
        "small_swap_start", small + [half_landing(g) for g in small], start=small_swap_plan, after=exchanging2)
    grad_x_flat = _grad_x(dh_main, dkv, dz1, w_in_t, dep=small_swapping)
    grad_x = grad_x_flat.reshape(1, t, D_MODEL)
    swap_bufs, _ = _split_call("small_swap_wait", swap_bufs, wait=(small_swap_plan, ss_send, ss_recv), after=grad_x_flat)
    pair_sums = [_pair_sum("grad_pair_sum_" + nm, g, th, pos) for nm, g, th in zip(["w_in", "w_out"], swap_bufs[:2], swap_bufs[2:])]
    small_plan = _exchange_plan([p.shape[0] // N_CHIPS for p, _ in pair_sums])
    small_bufs, small_exchanging, sm_send, sm_recv = _split_call(
        "small_exchange_start", [p for p, _ in pair_sums] + [exchange_landing(p) for p, _ in pair_sums], start=small_plan)

    ff1_ex, _ = _split_call("ff1_exchange_wait", ff1_ex, wait=(ff_exchange_plan, ex1_send, ex1_recv), after=small_exchanging)
    g_w_ff1, d_w_ff1, nm_w_ff1, nv_w_ff1 = _chip_sum_and_adamw("adamw_w_ff1", ff1_own, ff1_ex[1], w_ff1[0], m_w_ff1[0], v_w_ff1[0], pos)
    ff2_ex, _ = _split_call("ff2_exchange_wait", ff2_ex, wait=(ff_exchange_plan, ex2_send, ex2_recv), after=nv_w_ff1)
    g_w_ff2, d_w_ff2, nm_w_ff2, nv_w_ff2 = _chip_sum_and_adamw("adamw_w_ff2", ff2_own, ff2_ex[1], w_ff2[0], m_w_ff2[0], v_w_ff2[0], pos)
    small_bufs, _ = _split_call("small_exchange_wait", small_bufs, wait=(small_plan, sm_send, sm_recv), after=nv_w_ff2)
    g_w_in, d_w_in, nm_w_in, nv_w_in = (a.T for a in _chip_sum_and_adamw(
        "adamw_w_in", pair_sums[0][1], small_bufs[2], w_in[0].T, m_w_in[0].T, v_w_in[0].T, pos))
    g_w_out, d_w_out, nm_w_out, nv_w_out = _chip_sum_and_adamw(
        "adamw_w_out", pair_sums[1][1], small_bufs[3], w_out[0], m_w_out[0], v_w_out[0], pos)
    small_grads, small_d, small_nm, small_nv = _adamw_small(
        small_g,
        [v_ln_g, v_ln_b, w_spatial, b_spatial, sinks, ln1_g, ln1_b, ln2_g, ln2_b],
        [m_v_ln_g, m_v_ln_b, m_w_spatial, m_b_spatial, m_sinks, m_ln1_g, m_ln1_b, m_ln2_g, m_ln2_b],
        [v_v_ln_g, v_v_ln_b, v_w_spatial, v_b_spatial, v_sinks, v_ln1_g, v_ln1_b, v_ln2_g, v_ln2_b])

    def with_big(small, w_in_v, w_out_v, w_ff1_v, w_ff2_v):
        g_vg, g_vb, g_ws, g_bs, g_sk, g_1g, g_1b, g_2g, g_2b = small
        return [w_in_v[None], g_vg, g_vb, g_ws, g_bs, g_sk, w_out_v[None], g_1g, g_1b, w_ff1_v[None], w_ff2_v[None], g_2g, g_2b]

    return (
        loss,
        grad_x,
        *with_big(small_grads, g_w_in, g_w_out, g_w_ff1, g_w_ff2),
        *with_big(small_d, d_w_in, d_w_out, d_w_ff1, d_w_ff2),
        *with_big(small_nm, nm_w_in, nm_w_out, nm_w_ff1, nm_w_ff2),
        *with_big(small_nv, nv_w_in, nv_w_out, nv_w_ff1, nv_w_ff2),
    )
```

```python
import math

import jax
import jax.numpy as jnp
from jax import lax
from jax.experimental import pallas as pl
from jax.experimental.pallas import tpu as pltpu

F32 = jnp.float32
BF16 = jnp.bfloat16

D_MODEL = 1024
HEAD_DIM = 64
D_GMLP = 512
D_ATTN = 512
D_KV = 128
D_IN = 2 * D_GMLP + D_ATTN + 2 * D_KV
D_MAIN = 2 * D_GMLP + D_ATTN
N_HEADS = 8
CHUNK = 128
CHUNKS_PER_STEP = 4
ROPE_THETA = 10000.0
D_FF = 4 * D_MODEL
N_FF_BLOCKS = 4
LN_EPS = 1e-5
ALPHA = (2.0 * 1) ** 0.25
NEG_INF = -1e30
SCALE = 1.0 / math.sqrt(HEAD_DIM)

ADAM_LR = 0.001
ADAM_B1 = 0.9
ADAM_B2 = 0.999
ADAM_EPS = 1e-08
ADAM_WD = 0.01
ADAM_STEP = 10

N_CHIPS = 4
LANES = 128
V7X_VMEM_BYTES = 64 * 1024 * 1024
VMEM_LIMIT = V7X_VMEM_BYTES - 8 * 1024 * 1024
TM = 512
TM_FFN = 256
TM_FFN_FWD = 512
FFN_PART = 256
TK = 1024
SMALL_ROWS = 1152
MESH = pl.DeviceIdType.MESH

NT = (((1,), (1,)), ((), ()))
TN = (((0,), (0,)), ((), ()))


def _dot(a, b, dims=None):
    if dims is None:
        return jnp.dot(a, b, preferred_element_type=F32)
    return lax.dot_general(a, b, dims, preferred_element_type=F32)


def _params(semantics=None):
    return pltpu.CompilerParams(dimension_semantics=semantics, vmem_limit_bytes=VMEM_LIMIT)


def _const_spec(shape, single_buffer=False):
    zeros = (0,) * len(shape)
    if single_buffer:
        return pl.BlockSpec(shape, lambda *_: zeros, pipeline_mode=pl.Buffered(1))
    return pl.BlockSpec(shape, lambda *_: zeros)


def _row_spec(rows, cols):
    return pl.BlockSpec((rows, cols), lambda i: (i, 0))


def _after(dep, body, in_specs, operands):
    if dep is None:
        return body, list(in_specs), list(operands)
    return (lambda dep_ref, *refs: body(*refs)), [pl.BlockSpec(memory_space=pl.ANY)] + list(in_specs), [dep] + list(operands)


def _gelu(x):
    k = math.sqrt(2.0 / math.pi)
    return 0.5 * x * (1.0 + jnp.tanh(k * (x + 0.044715 * (x * x * x))))


def _gelu_and_grad(x):
    k = math.sqrt(2.0 / math.pi)
    x2 = x * x
    t = jnp.tanh(k * (x + 0.044715 * (x2 * x)))
    g = 0.5 * x * (1.0 + t)
    dg = 0.5 * (1.0 + t) + 0.5 * x * (1.0 - t * t) * (k * (1.0 + 3.0 * 0.044715 * x2))
    return g, dg


def _layer_norm_stats(z):
    mu = jnp.mean(z, axis=-1, keepdims=True)
    zc = z - mu
    var = jnp.mean(zc * zc, axis=-1, keepdims=True)
    rstd = lax.rsqrt(var + LN_EPS)
    return zc * rstd, rstd


def _layer_norm_bwd(dxhat, xhat, rstd):
    m1 = jnp.mean(dxhat, axis=-1, keepdims=True)
    m2 = jnp.mean(dxhat * xhat, axis=-1, keepdims=True)
    return rstd * (dxhat - m1 - xhat * m2)


def _rotate_half(t):
    n = t.shape[1]
    lane = lax.broadcasted_iota(jnp.int32, t.shape, 1)
    first = (lane & (HEAD_DIM // 2)) == 0
    return jnp.where(first, -pltpu.roll(t, n - HEAD_DIM // 2, 1), pltpu.roll(t, HEAD_DIM // 2, 1))


def _rope(t, cos, sin):
    return t * cos + _rotate_half(t) * sin


def _rope_transposed(g, cos, sin):
    return g * cos - _rotate_half(g * sin)


def _lane_tile(a, reps):
    return jnp.tile(a, (1, reps)) if reps > 1 else a


def _in_proj(x, w_in_t, cos, sin, dep=None):
    t = x.shape[0]

    def body(x_ref, w_ref, cos_ref, sin_ref, u_ref, vg_ref, q_ref, k_ref, va_ref):
        xb = x_ref[...].astype(BF16)
        u_ref[...] = _dot(xb, w_ref[0:D_GMLP, :], NT)
        vg_ref[...] = _dot(xb, w_ref[D_GMLP : 2 * D_GMLP, :], NT)
        q = _dot(xb, w_ref[2 * D_GMLP : D_MAIN, :], NT)
        k = _dot(xb, w_ref[D_MAIN : D_MAIN + D_KV, :], NT)
        va_ref[...] = _dot(xb, w_ref[D_MAIN + D_KV : D_IN, :], NT).astype(BF16)
        c, s = cos_ref[...], sin_ref[...]
        q_ref[...] = _rope(q, _lane_tile(c, D_ATTN // LANES), _lane_tile(s, D_ATTN // LANES)).astype(BF16)
        k_ref[...] = _rope(k, c, s).astype(BF16)

    body, in_specs, operands = _after(
        dep, body, [_row_spec(TM, D_MODEL), _const_spec((D_IN, D_MODEL)), _row_spec(TM, LANES), _row_spec(TM, LANES)], [x, w_in_t, cos, sin])
    return pl.pallas_call(
        body,
        name="in_proj",
        grid=(t // TM,),
        in_specs=in_specs,
        out_specs=[_row_spec(TM, D_GMLP), _row_spec(TM, D_GMLP), _row_spec(TM, D_ATTN), _row_spec(TM, D_KV), _row_spec(TM, D_KV)],
        out_shape=[
            jax.ShapeDtypeStruct((t, D_GMLP), F32),
            jax.ShapeDtypeStruct((t, D_GMLP), F32),
            jax.ShapeDtypeStruct((t, D_ATTN), BF16),
            jax.ShapeDtypeStruct((t, D_KV), BF16),
            jax.ShapeDtypeStruct((t, D_KV), BF16),
        ],
        compiler_params=_params(("parallel",)),
    )(*operands)


def _step_rows(i):
    return (i, 0)


def _chunk_before_step(i):
    return (jnp.maximum(CHUNKS_PER_STEP * i - 1, 0), 0)


def _chunk_specs():
    step = CHUNKS_PER_STEP * CHUNK
    return [
        pl.BlockSpec((step, D_GMLP), _step_rows),
        pl.BlockSpec((step, D_GMLP), _step_rows),
        pl.BlockSpec((step, D_ATTN), _step_rows),
        pl.BlockSpec((step, D_KV), _step_rows),
        pl.BlockSpec((CHUNK, D_KV), _chunk_before_step),
        pl.BlockSpec((step, D_KV), _step_rows),
        pl.BlockSpec((CHUNK, D_KV), _chunk_before_step),
    ]


def _half_lane_masks(rows):
    lane = lax.broadcasted_iota(jnp.int32, (rows, LANES), 1)
    return lane < HEAD_DIM


def _kv_variants(kv2):
    left = _half_lane_masks(kv2.shape[0])
    f = kv2.astype(F32)
    swapped = pltpu.roll(f, HEAD_DIM, 1)
    zero = jnp.zeros_like(f)
    g0 = (jnp.where(left, f, zero).astype(BF16), jnp.where(left, zero, swapped).astype(BF16))
    g1 = (jnp.where(left, swapped, zero).astype(BF16), jnp.where(left, zero, f).astype(BF16))
    return (g0, g1)


def _band_mask(i, heads=1):
    row = lax.broadcasted_iota(jnp.int32, (heads * CHUNK, 2 * CHUNK), 0) & (CHUNK - 1)
    col = lax.broadcasted_iota(jnp.int32, (heads * CHUNK, 2 * CHUNK), 1)
    no_prev = jnp.where(i > 0, 0, 4 * CHUNK)
    in_prev = jnp.logical_and(col < CHUNK, (col - row) > no_prev)
    in_cur = jnp.logical_and(col >= CHUNK, (col - CHUNK) <= row)
    return jnp.logical_or(in_prev, in_cur)


def _causal_mask():
    row = lax.broadcasted_iota(jnp.int32, (CHUNK, CHUNK), 0)
    col = lax.broadcasted_iota(jnp.int32, (CHUNK, CHUNK), 1)
    return col <= row


def _store_spatial_weights(w_ref, wcat_ref, wcat_t_ref=None):
    causal = _causal_mask()
    for p in range(D_GMLP // LANES):
        wl = jnp.where(causal, w_ref[2 * p], 0.0)
        wr = jnp.where(causal, w_ref[2 * p + 1], 0.0)
        wcat_ref[p] = jnp.concatenate([wl, wr], axis=1).astype(BF16)
        if wcat_t_ref is not None:
            wcat_t_ref[p] = jnp.concatenate([wl.T, wr.T], axis=1).astype(BF16)


def _pair_stack(xp, left):
    return jnp.concatenate([jnp.where(left, xp, 0.0), jnp.where(left, 0.0, xp)], axis=0).astype(BF16)


def _mixer_fwd(u, vg, q, k, va, v_ln_g, v_ln_b, w_spatial, bias_full, sinks, dep=None):
    t = u.shape[0]

    def body(u_ref, vg_ref, q_ref, kc_ref, kp_ref, vc_ref, vp_ref, g_ref, b_ref, w_ref, bias_ref, sink_ref, cat_ref, wcat):
        i = pl.program_id(0)
        left = _half_lane_masks(CHUNK)

        @pl.when(i == 0)
        def _():
            _store_spatial_weights(w_ref, wcat)

        heads = range(N_HEADS)
        pair_cols = [slice(p * LANES, (p + 1) * LANES) for p in range(D_GMLP // LANES)]
        sinks_h = [sink_ref[h] for h in heads]
        for c in range(CHUNKS_PER_STEP):
            rows = slice(c * CHUNK, (c + 1) * CHUNK)
            before = slice((c - 1) * CHUNK, c * CHUNK)
            k_prev = kp_ref[...] if c == 0 else kc_ref[before, :]
            v_prev = vp_ref[...] if c == 0 else vc_ref[before, :]
            k_var = _kv_variants(jnp.concatenate([k_prev, kc_ref[rows, :]], axis=0))
            v_var = _kv_variants(jnp.concatenate([v_prev, vc_ref[rows, :]], axis=0))
            scores = [_dot(q_ref[rows, pair_cols[h // 2]], k_var[h // 4][h % 2], NT) for h in heads]

            ug = _gelu(u_ref[rows, :])
            xhat, _ = _layer_norm_stats(_gelu(vg_ref[rows, :]))
            vgl = xhat * g_ref[...] + b_ref[...]
            mixed = [_dot(wcat[p], _pair_stack(vgl[:, cols], left)) for p, cols in enumerate(pair_cols)]

            valid = _band_mask(CHUNKS_PER_STEP * i + c)
            masked = [jnp.where(valid, scores[h] * SCALE, NEG_INF) for h in heads]
            maxes = [jnp.maximum(jnp.max(masked[h], axis=1, keepdims=True), sinks_h[h]) for h in heads]
            exps = [jnp.exp(masked[h] - maxes[h]) for h in heads]
            invs = [1.0 / (jnp.sum(exps[h], axis=1, keepdims=True) + jnp.exp(sinks_h[h] - maxes[h])) for h in heads]
            probs = [(exps[h] * invs[h]).astype(BF16) for h in heads]
            for p, cols in enumerate(pair_cols):
                cat_ref[rows, cols] = (ug[:, cols] * (mixed[p] + bias_ref[:, cols])).astype(BF16)
            for p in range(D_ATTN // LANES):
                out = _dot(probs[2 * p], v_var[p // 2][0]) + _dot(probs[2 * p + 1], v_var[p // 2][1])
                cat_ref[rows, D_GMLP + p * LANES : D_GMLP + (p + 1) * LANES] = out.astype(BF16)

    in_specs = _chunk_specs() + [
        _const_spec((1, D_GMLP)),
        _const_spec((1, D_GMLP)),
        _const_spec((N_HEADS, CHUNK, CHUNK)),
        _const_spec((CHUNK, D_GMLP)),
        pl.BlockSpec(memory_space=pltpu.SMEM),
    ]
    body, in_specs, operands = _after(dep, body, in_specs, [u, vg, q, k, k, va, va, v_ln_g, v_ln_b, w_spatial, bias_full, sinks])
    return pl.pallas_call(
        body,
        name="mixer_fwd",
        grid=(t // (CHUNKS_PER_STEP * CHUNK),),
        in_specs=in_specs,
        out_specs=pl.BlockSpec((CHUNKS_PER_STEP * CHUNK, D_MODEL), lambda i: (i, 0)),
        out_shape=jax.ShapeDtypeStruct((t, D_MODEL), BF16),
        scratch_shapes=[pltpu.VMEM((D_GMLP // LANES, CHUNK, 2 * CHUNK), BF16)],
        compiler_params=_params(("arbitrary",)),
    )(*operands)


def _ffn_fwd_loss(cat, x, w_out, ln1_g, ln1_b, w1, w2, ln2_g, ln2_b, target):
    t = x.shape[0]

    def body(cat_ref, x_ref, wo_ref, g1_ref, b1_ref, w1_ref, w2_ref, g2_ref, b2_ref, tgt_ref,
             xh_ref, rstd_ref, x1b_ref, r_ref, dz2_ref, dz2b_ref, dg2_ref, db2_ref, sq_ref):
        @pl.when(pl.program_id(0) == 0)
        def _():
            dg2_ref[...] = jnp.zeros_like(dg2_ref)
            db2_ref[...] = jnp.zeros_like(db2_ref)
            sq_ref[...] = jnp.zeros_like(sq_ref)

        parts = [slice(p * FFN_PART, (p + 1) * FFN_PART) for p in range(TM_FFN_FWD // FFN_PART)]

        def norm1(rows, z):
            xhat1, rstd1 = _layer_norm_stats(z)
            xh_ref[rows, :] = xhat1
            rstd_ref[rows, :] = rstd1
            x1 = xhat1 * g1_ref[...] + b1_ref[...]
            x1b = x1.astype(BF16)
            x1b_ref[rows, :] = x1b
            return x1, x1b

        def feed_forward(rows, x1b, pre):
            ff = None
            for j in range(N_FF_BLOCKS):
                r = jnp.maximum(pre, 0.0)
                r_ref[rows, j * D_MODEL : (j + 1) * D_MODEL] = r.astype(BF16)
                part = _dot((r * r).astype(BF16), w2_ref[j])
                ff = part if ff is None else ff + part
                if j + 1 < N_FF_BLOCKS:
                    pre = _dot(x1b, w1_ref[j + 1])
            return ff

        def norm2_and_loss(rows, x1, ff):
            xhat2, rstd2 = _layer_norm_stats(ALPHA * x1 + ff)
            err = xhat2 * g2_ref[...] + b2_ref[...] - tgt_ref[rows, :]
            sq_ref[...] += jnp.sum(err * err, axis=0, keepdims=True)
            dy = err * (1.0 / D_MODEL)
            dg2_ref[...] += jnp.sum(dy * xhat2, axis=0, keepdims=True)
            db2_ref[...] += jnp.sum(dy, axis=0, keepdims=True)
            dz2 = _layer_norm_bwd(dy * g2_ref[...], xhat2, rstd2)
            dz2_ref[rows, :] = dz2
            dz2b_ref[rows, :] = dz2.astype(BF16)

        projected = _dot(cat_ref[parts[0], :], wo_ref[...])
        last = None
        for i, rows in enumerate(parts):
            z = ALPHA * x_ref[rows, :] + projected
            if i + 1 < len(parts):
                projected = _dot(cat_ref[parts[i + 1], :], wo_ref[...])
            x1, x1b = norm1(rows, z)
            pre = _dot(x1b, w1_ref[0])
            if last is not None:
                norm2_and_loss(*last)
            last = (rows, x1, feed_forward(rows, x1b, pre))
        norm2_and_loss(*last)

    vec = _const_spec((1, D_MODEL))
    tile = _row_spec(TM_FFN_FWD, D_MODEL)
    wspec = _const_spec((N_FF_BLOCKS, D_MODEL, D_MODEL), single_buffer=True)
    return pl.pallas_call(
        body,
        name="ffn_fwd_loss",
        grid=(t // TM_FFN_FWD,),
        in_specs=[tile, tile, _const_spec((D_MODEL, D_MODEL), single_buffer=True), vec, vec, wspec, wspec, vec, vec, tile],
        out_specs=[tile, _row_spec(TM_FFN_FWD, 1), tile, _row_spec(TM_FFN_FWD, D_FF), tile, tile, vec, vec, vec],
        out_shape=[
            jax.ShapeDtypeStruct((t, D_MODEL), F32),
            jax.ShapeDtypeStruct((t, 1), F32),
            jax.ShapeDtypeStruct((t, D_MODEL), BF16),
            jax.ShapeDtypeStruct((t, D_FF), BF16),
            jax.ShapeDtypeStruct((t, D_MODEL), F32),
            jax.ShapeDtypeStruct((t, D_MODEL), BF16),
            jax.ShapeDtypeStruct((1, D_MODEL), F32),
            jax.ShapeDtypeStruct((1, D_MODEL), F32),
            jax.ShapeDtypeStruct((1, D_MODEL), F32),
        ],
        compiler_params=_params(("arbitrary",)),
    )(cat, x, w_out, ln1_g, ln1_b, w1, w2, ln2_g, ln2_b, target)


def _ffn_bwd_ln1(dz2, r, x1b, xhat1, rstd1, ln1_g, w1, w2, w_out, dep=None):
    t = dz2.shape[0]

    def body(dz2_ref, r_ref, x1b_ref, xh_ref, rstd_ref, g1_ref, w1_ref, w2_ref, wo_ref, gw1_ref, dz1_ref, dz1b_ref, dcat_ref, dg1_ref, db1_ref):
        @pl.when(pl.program_id(0) == 0)
        def _():
            dg1_ref[...] = jnp.zeros_like(dg1_ref)
            db1_ref[...] = jnp.zeros_like(db1_ref)
            gw1_ref[...] = jnp.zeros_like(gw1_ref)

        dz2 = dz2_ref[...]
        dz2b = dz2.astype(BF16)
        x1_t = x1b_ref[...].astype(F32).T.astype(BF16)
        dx1 = ALPHA * dz2
        for j in range(N_FF_BLOCKS):
            cols = slice(j * D_MODEL, (j + 1) * D_MODEL)
            dpre = (_dot(dz2b, w2_ref[j], NT) * (2.0 * r_ref[:, cols].astype(F32))).astype(BF16)
            gw1_ref[cols, :] += _dot(x1_t, dpre)
            dx1 = dx1 + _dot(dpre, w1_ref[j], NT)
        xhat1 = xh_ref[...]
        dg1_ref[...] += jnp.sum(dx1 * xhat1, axis=0, keepdims=True)
        db1_ref[...] += jnp.sum(dx1, axis=0, keepdims=True)
        dz1 = _layer_norm_bwd(dx1 * g1_ref[...], xhat1, rstd_ref[...])
        dz1_ref[...] = dz1
        dz1b = dz1.astype(BF16)
        dz1b_ref[...] = dz1b
        dcat_ref[...] = _dot(dz1b, wo_ref[...], NT).astype(BF16)

    vec = _const_spec((1, D_MODEL))
    tile = _row_spec(TM_FFN, D_MODEL)
    wspec = _const_spec((N_FF_BLOCKS, D_MODEL, D_MODEL), single_buffer=True)
    body, in_specs, operands = _after(
        dep, body,
        [tile, _row_spec(TM_FFN, D_FF), tile, tile, _row_spec(TM_FFN, 1), vec, wspec, wspec, _const_spec((D_MODEL, D_MODEL), single_buffer=True)],
        [dz2, r, x1b, xhat1, rstd1, ln1_g, w1, w2, w_out])
    return pl.pallas_call(
        body,
        name="ffn_bwd_ln1",
        grid=(t // TM_FFN,),
        in_specs=in_specs,
        out_specs=[_const_spec((D_FF, D_MODEL), single_buffer=True), tile, tile, tile, vec, vec],
        out_shape=[
            jax.ShapeDtypeStruct((D_FF, D_MODEL), F32),
            jax.ShapeDtypeStruct((t, D_MODEL), F32),
            jax.ShapeDtypeStruct((t, D_MODEL), BF16),
            jax.ShapeDtypeStruct((t, D_MODEL), BF16),
            jax.ShapeDtypeStruct((1, D_MODEL), F32),
            jax.ShapeDtypeStruct((1, D_MODEL), F32),
        ],
        compiler_params=_params(("arbitrary",)),
    )(*operands)


def _mixer_bwd(u, vg, q, k, va, dcat, cos, sin, v_ln_g, v_ln_b, w_spatial, bias_full, sinks, r, dz2b, dep=None):
    t = u.shape[0]
    n_chunks = t // CHUNK
    assert CHUNKS_PER_STEP == N_FF_BLOCKS

    def body(u_ref, vg_ref, q_ref, kc_ref, kp_ref, vc_ref, vp_ref, dcat_ref, cosc_ref, sinc_ref, cosp_ref, sinp_ref,
             g_ref, b_ref, w_ref, bias_ref, sink_ref, r_ref, dz2b_ref,
             dmain_ref, dkv_ref, gw2_ref, dg_ref, db_ref, dw_ref, dbs_ref, dsink_ref, dmix_acc, wcat, wcat_t):
        i = pl.program_id(0)
        left = _half_lane_masks(CHUNK)
        lane = lax.broadcasted_iota(jnp.int32, (CHUNK, LANES), 1)
        n_pairs = D_GMLP // LANES

        @pl.when(i == 0)
        def _():
            dg_ref[...] = jnp.zeros_like(dg_ref)
            db_ref[...] = jnp.zeros_like(db_ref)
            dw_ref[...] = jnp.zeros_like(dw_ref)
            dsink_ref[...] = jnp.zeros_like(dsink_ref)
            dmix_acc[...] = jnp.zeros_like(dmix_acc)
            gw2_ref[...] = jnp.zeros_like(gw2_ref)
            _store_spatial_weights(w_ref, wcat, wcat_t)

        n_qpairs = D_ATTN // LANES
        heads = range(N_HEADS)
        pair_cols = [slice(p * LANES, (p + 1) * LANES) for p in range(n_pairs)]
        sinks_h = [sink_ref[h] for h in heads]
        gain = g_ref[...]
        causal = _causal_mask()
        lane_row = lax.broadcasted_iota(jnp.int32, (1, LANES), 1)
        heads_per_group = N_HEADS // 2

        def group_grad_t(lhs_t, rhs_heads):
            parts = []
            for g in range(2):
                group = range(g * heads_per_group, (g + 1) * heads_per_group)
                lhs = jnp.concatenate([lhs_t[h * HEAD_DIM : (h + 1) * HEAD_DIM] for h in group], axis=1)
                parts.append(_dot(lhs, jnp.concatenate([rhs_heads[h] for h in group], axis=0)))
            return jnp.concatenate(parts, axis=0)

        for c in range(CHUNKS_PER_STEP):
            chunk = CHUNKS_PER_STEP * i + c
            rows = slice(c * CHUNK, (c + 1) * CHUNK)
            before = slice((c - 1) * CHUNK, c * CHUNK)

            k_prev = kp_ref[...] if c == 0 else kc_ref[before, :]
            v_prev = vp_ref[...] if c == 0 else vc_ref[before, :]
            k_var = _kv_variants(jnp.concatenate([k_prev, kc_ref[rows, :]], axis=0))
            v_var = _kv_variants(jnp.concatenate([v_prev, vc_ref[rows, :]], axis=0))
            q_pairs = [q_ref[rows, cols] for cols in pair_cols]
            do_all = dcat_ref[rows, D_GMLP:D_MODEL]
            do_pairs = [do_all[:, cols] for cols in pair_cols]
            scores = [_dot(q_pairs[h // 2], k_var[h // 4][h % 2], NT) for h in heads]
            dprobs = [_dot(do_pairs[h // 2], v_var[h // 4][h % 2], NT) for h in heads]
            q_t = q_ref[rows, :].astype(F32).T.astype(BF16)
            do_t = do_all.astype(F32).T.astype(BF16)

            ff_cols = slice(c * D_MODEL, (c + 1) * D_MODEL)
            relu_block = r_ref[:, ff_cols]
            gw2_ref[ff_cols, :] += _dot(relu_block * relu_block, dz2b_ref[...], TN)

            ug, dug_du = _gelu_and_grad(u_ref[rows, :])
            gv, dgv_dv = _gelu_and_grad(vg_ref[rows, :])
            xhat, rstd = _layer_norm_stats(gv)
            vgl = xhat * gain + b_ref[...]
            mixed = [_dot(wcat[p], _pair_stack(vgl[:, cols], left)) for p, cols in enumerate(pair_cols)]

            valid = _band_mask(chunk)
            masked = [jnp.where(valid, scores[h] * SCALE, NEG_INF) for h in heads]
            maxes = [jnp.maximum(jnp.max(masked[h], axis=1, keepdims=True), sinks_h[h]) for h in heads]
            exps = [jnp.exp(masked[h] - maxes[h]) for h in heads]
            exp_sinks = [jnp.exp(sinks_h[h] - maxes[h]) for h in heads]
            invs = [1.0 / (jnp.sum(exps[h], axis=1, keepdims=True) + exp_sinks[h]) for h in heads]
            probs = [exps[h] * invs[h] for h in heads]
            dsums = [jnp.sum(probs[h] * dprobs[h], axis=1, keepdims=True) for h in heads]
            ds_b = [(probs[h] * (dprobs[h] - dsums[h]) * SCALE).astype(BF16) for h in heads]
            probs_b = [probs[h].astype(BF16) for h in heads]

            dm_stacks = []
            for p, cols in enumerate(pair_cols):
                da = dcat_ref[rows, cols].astype(F32)
                dmain_ref[rows, cols] = (da * (mixed[p] + bias_ref[:, cols]) * dug_du[:, cols]).astype(BF16)
                dmixed = da * ug[:, cols]
                dmix_acc[:, cols] += dmixed
                dm_stacks.append(_pair_stack(dmixed, left))

            dq_all = jnp.concatenate(
                [_dot(ds_b[2 * p], k_var[p // 2][0]) + _dot(ds_b[2 * p + 1], k_var[p // 2][1]) for p in range(n_qpairs)], axis=1)
            dk2_t = group_grad_t(q_t, ds_b)
            dv2_t = group_grad_t(do_t, probs_b)

            for p, cols in enumerate(pair_cols):
                dw_pair = _dot(dm_stacks[p], vgl[:, cols].astype(BF16), NT)
                dw_ref[2 * p] += jnp.where(causal, dw_pair[:CHUNK], 0.0)
                dw_ref[2 * p + 1] += jnp.where(causal, dw_pair[CHUNK:], 0.0)
            dvgl = jnp.concatenate([_dot(wcat_t[p], dm_stacks[p]) for p in range(n_pairs)], axis=1)

            dsink_row = jnp.zeros((1, LANES), F32)
            for h in heads:
                d_sink = -jnp.sum(exp_sinks[h] * invs[h] * dsums[h], axis=0, keepdims=True)
                dsink_row = dsink_row + jnp.where(lane_row == h, d_sink, 0.0)
            dsink_ref[0:1, :] += dsink_row
            cos_c, sin_c = cosc_ref[rows, :], sinc_ref[rows, :]
            cos_p = cosp_ref[...] if c == 0 else cosc_ref[before, :]
            sin_p = sinp_ref[...] if c == 0 else sinc_ref[before, :]
            dmain_ref[rows, 2 * D_GMLP : D_MAIN] = _rope_transposed(dq_all, _lane_tile(cos_c, n_qpairs), _lane_tile(sin_c, n_qpairs)).astype(BF16)
            dk2 = dk2_t.T
            dv2 = dv2_t.T
            cur = pl.ds(pl.multiple_of(chunk * CHUNK, CHUNK), CHUNK)
            dkv_ref[cur, 0:D_KV] = _rope_transposed(dk2[CHUNK:], cos_c, sin_c)
            dkv_ref[cur, D_KV : 2 * D_KV] = dv2[CHUNK:]
            prev = pl.ds(pl.multiple_of(jnp.maximum(chunk - 1, 0) * CHUNK, CHUNK), CHUNK)
            dkv_ref[prev, 0:D_KV] += _rope_transposed(dk2[:CHUNK], cos_p, sin_p)
            dkv_ref[prev, D_KV : 2 * D_KV] += dv2[:CHUNK]

            dg_ref[...] += jnp.sum(dvgl * xhat, axis=0, keepdims=True)
            db_ref[...] += jnp.sum(dvgl, axis=0, keepdims=True)
            dgv = _layer_norm_bwd(dvgl * gain, xhat, rstd)
            dmain_ref[rows, D_GMLP : 2 * D_GMLP] = (dgv * dgv_dv).astype(BF16)

        @pl.when(i == n_chunks // CHUNKS_PER_STEP - 1)
        def _():
            tile = jnp.zeros((CHUNK, LANES), F32)
            for p, cols in enumerate(pair_cols):
                dm = dmix_acc[:, cols]
                sl = jnp.sum(jnp.where(left, dm, 0.0), axis=1, keepdims=True)
                sr = jnp.sum(jnp.where(left, 0.0, dm), axis=1, keepdims=True)
                tile = jnp.where(lane == 2 * p, sl, tile)
                tile = jnp.where(lane == 2 * p + 1, sr, tile)
            dbs_ref[...] = tile

    step = CHUNKS_PER_STEP * CHUNK
    in_specs = _chunk_specs() + [
        pl.BlockSpec((step, D_MODEL), _step_rows),
        pl.BlockSpec((step, LANES), _step_rows),
        pl.BlockSpec((step, LANES), _step_rows),
        pl.BlockSpec((CHUNK, LANES), _chunk_before_step),
        pl.BlockSpec((CHUNK, LANES), _chunk_before_step),
        _const_spec((1, D_GMLP)),
        _const_spec((1, D_GMLP)),
        _const_spec((N_HEADS, CHUNK, CHUNK)),
        _const_spec((CHUNK, D_GMLP)),
        pl.BlockSpec(memory_space=pltpu.SMEM),
        pl.BlockSpec((step, D_FF), _step_rows),
        pl.BlockSpec((step, D_MODEL), _step_rows),
    ]
    body, in_specs, operands = _after(
        dep, body, in_specs, [u, vg, q, k, k, va, va, dcat, cos, sin, cos, sin, v_ln_g, v_ln_b, w_spatial, bias_full, sinks, r, dz2b])
    return pl.pallas_call(
        body,
        name="mixer_bwd",
        grid=(n_chunks // CHUNKS_PER_STEP,),
        in_specs=in_specs,
        out_specs=[
            pl.BlockSpec((step, D_MAIN), _step_rows),
            _const_spec((t, 2 * D_KV)),
            _const_spec((D_FF, D_MODEL), single_buffer=True),
            _const_spec((1, D_GMLP)),
            _const_spec((1, D_GMLP)),
            _const_spec((N_HEADS, CHUNK, CHUNK)),
            _const_spec((CHUNK, LANES)),
            _const_spec((8, LANES)),
        ],
        out_shape=[
            jax.ShapeDtypeStruct((t, D_MAIN), BF16),
            jax.ShapeDtypeStruct((t, 2 * D_KV), F32),
            jax.ShapeDtypeStruct((D_FF, D_MODEL), F32),
            jax.ShapeDtypeStruct((1, D_GMLP), F32),
            jax.ShapeDtypeStruct((1, D_GMLP), F32),
            jax.ShapeDtypeStruct((N_HEADS, CHUNK, CHUNK), F32),
            jax.ShapeDtypeStruct((CHUNK, LANES), F32),
            jax.ShapeDtypeStruct((8, LANES), F32),
        ],
        scratch_shapes=[
            pltpu.VMEM((CHUNK, D_GMLP), F32),
            pltpu.VMEM((D_GMLP // LANES, CHUNK, 2 * CHUNK), BF16),
            pltpu.VMEM((D_GMLP // LANES, CHUNK, 2 * CHUNK), BF16),
        ],
        compiler_params=_params(("arbitrary",)),
    )(*operands)


def _grad_x(dh_main, dkv, dz1, w_in_t, dep=None):
    t = dz1.shape[0]

    def body(dm_ref, dkv_ref, dz1_ref, w_ref, gx_ref):
        acc = ALPHA * dz1_ref[...] + _dot(dm_ref[...], w_ref[0:D_MAIN, :])
        gx_ref[...] = acc + _dot(dkv_ref[...].astype(BF16), w_ref[D_MAIN:D_IN, :])

    body, in_specs, operands = _after(
        dep, body, [_row_spec(TM, D_MAIN), _row_spec(TM, 2 * D_KV), _row_spec(TM, D_MODEL), _const_spec((D_IN, D_MODEL))], [dh_main, dkv, dz1, w_in_t])
    return pl.pallas_call(
        body,
        name="grad_x",
        grid=(t // TM,),
        in_specs=in_specs,
        out_specs=_row_spec(TM, D_MODEL),
        out_shape=jax.ShapeDtypeStruct((t, D_MODEL), F32),
        compiler_params=_params(("parallel",)),
    )(*operands)


def _token_contraction(name, out_rows, tk, in_arrays, contributions, dep=None):
    t = in_arrays[0].shape[0]

    def body(*refs):
        out_ref = refs[-1]

        @pl.when(pl.program_id(0) == 0)
        def _():
            out_ref[...] = jnp.zeros_like(out_ref)

        for row0, a, b in contributions(*refs[:-1]):
            out_ref[row0 : row0 + a.shape[1], :] += _dot(a, b, TN)

    in_specs = [_row_spec(tk, a.shape[1]) for a in in_arrays]
    body, in_specs, operands = _after(dep, body, in_specs, in_arrays)
    return pl.pallas_call(
        body,
        name=name,
        grid=(t // tk,),
        in_specs=in_specs,
        out_specs=_const_spec((out_rows, D_MODEL), single_buffer=True),
        out_shape=jax.ShapeDtypeStruct((out_rows, D_MODEL), F32),
        compiler_params=_params(("arbitrary",)),
    )(*operands)


def _grad_w_out(cat, dz1b, dep=None):
    def contributions(cat_ref, dz1_ref):
        return [(0, cat_ref[...], dz1_ref[...])]

    return _token_contraction("grad_w_out", D_MODEL, TK, [cat, dz1b], contributions, dep)


ANY = pl.BlockSpec(memory_space=pl.ANY)


def _mesh_position():
    return lax.axis_index("x"), lax.axis_index("y"), lax.axis_index("c")


def _other_chips(x, y):
    return [(1 - x, y), (x, 1 - y), (1 - x, 1 - y)]


def _remote(src, dst, send_sem, recv_sem, device):
    return pltpu.make_async_remote_copy(src_ref=src, dst_ref=dst, send_sem=send_sem, recv_sem=recv_sem, device_id=device, device_id_type=MESH)


def _rows(ref, start, size):
    return ref.at[pl.ds(start, size), :]


def _rope_tables_and_casts(pos_row, inv_freq_row, shards, dep=None):
    t = pos_row.shape[1]
    steps = t // TM
    n = len(shards)

    def body(pos_ref, f_ref, *rest):
        f32_refs, (cos_ref, sin_ref), bf16_refs = rest[:n], rest[n : n + 2], rest[n + 2 :]
        for src, dst in zip(f32_refs, bf16_refs):
            dst[...] = src[...].astype(BF16)
        pos_rows = jnp.broadcast_to(pos_ref[...].astype(F32), (LANES, TM)).T
        ang = pos_rows * f_ref[...]
        cos_ref[...] = jnp.cos(ang)
        sin_ref[...] = jnp.sin(ang)

    shard_specs = [_row_spec(s.shape[0] // steps, s.shape[1]) for s in shards]
    body, in_specs, operands = _after(
        dep, body, [pl.BlockSpec((1, TM), lambda i: (0, i)), _const_spec((1, LANES))] + shard_specs, [pos_row, inv_freq_row, *shards])
    outs = pl.pallas_call(
        body,
        name="rope_tables_and_casts",
        grid=(steps,),
        in_specs=in_specs,
        out_specs=[_row_spec(TM, LANES), _row_spec(TM, LANES)] + shard_specs,
        out_shape=[jax.ShapeDtypeStruct((t, LANES), F32)] * 2 + [jax.ShapeDtypeStruct(s.shape, BF16) for s in shards],
        compiler_params=_params(("parallel",)),
    )(*operands)
    return outs[0], outs[1], list(outs[2:])


def _grad_w_in_t_and_small_all_reduce(dh_main, dkv, x, slab, dep=None):
    t = x.shape[0]
    steps = t // TK
    rows = slab.shape[0]
    part = rows // 8

    def body(dm_ref, dkv_ref, x_ref, slab_ref, grad_ref, sum_ref, landing, reduced, gathered, send_sems, recv_sems):
        k = pl.program_id(0)
        x_, y_, c_ = _mesh_position()
        me = 4 * x_ + 2 * y_ + c_
        flips = [(f >> 2, (f >> 1) & 1, f & 1) for f in range(1, 8)]

        def peer(flip):
            fx, fy, fc = flip
            return (1 - x_ if fx else x_, 1 - y_ if fy else y_, 1 - c_ if fc else c_)

        def part_of(ref, device):
            return ref.at[pl.ds(pl.multiple_of(device * part, 8), part), :]

        def scatter_copies():
            out = []
            for kk, flip in enumerate(flips):
                px, py, pc = peer(flip)
                them = 4 * px + 2 * py + pc
                send = _remote(part_of(slab_ref, them), landing.at[me], send_sems.at[kk], recv_sems.at[kk], (px, py, pc))
                recv = _remote(landing.at[them], landing.at[them], send_sems.at[kk], recv_sems.at[kk], (px, py, pc))
                out.append((send, recv))
            return out

        def gather_copies():
            out = []
            for kk, flip in enumerate(flips):
                px, py, pc = peer(flip)
                them = 4 * px + 2 * py + pc
                send = _remote(reduced, part_of(gathered, me), send_sems.at[7 + kk], recv_sems.at[7 + kk], (px, py, pc))
                recv = _remote(part_of(gathered, them), part_of(gathered, them), send_sems.at[7 + kk], recv_sems.at[7 + kk], (px, py, pc))
                out.append((send, recv))
            return out

        @pl.when(k == 0)
        def _():
            grad_ref[...] = jnp.zeros_like(grad_ref)
            for send, _ in scatter_copies():
                send.start()
            landing[me] = part_of(slab_ref, me)[...]

        @pl.when(k == steps // 2)
        def _():
            for _, recv in scatter_copies():
                recv.wait_recv()
            total = landing[0]
            for s in range(1, 8):
                total = total + landing[s]
            reduced[...] = total
            part_of(gathered, me)[...] = total
            for send, _ in gather_copies():
                send.start()

        xb = x_ref[...].astype(BF16)
        grad_ref[0:D_MAIN, :] += _dot(dm_ref[...], xb, TN)
        grad_ref[D_MAIN:D_IN, :] += _dot(dkv_ref[...].astype(BF16), xb, TN)

        @pl.when(k == steps - 1)
        def _():
            for send, recv in gather_copies():
                recv.wait_recv()
                send.wait_send()
            for send, _ in scatter_copies():
                send.wait_send()
            sum_ref[...] = gathered[...]

    body, in_specs, operands = _after(
        dep, body, [_row_spec(TK, D_MAIN), _row_spec(TK, 2 * D_KV), _row_spec(TK, D_MODEL), _const_spec(slab.shape)], [dh_main, dkv, x, slab])
    return pl.pallas_call(
        body,
        name="grad_w_in_and_small_all_reduce",
        grid=(steps,),
        in_specs=in_specs,
        out_specs=[_const_spec((D_IN, D_MODEL), single_buffer=True), _const_spec(slab.shape)],
        out_shape=[jax.ShapeDtypeStruct((D_IN, D_MODEL), F32), jax.ShapeDtypeStruct(slab.shape, slab.dtype)],
        scratch_shapes=[
            pltpu.VMEM((8, part, LANES), F32),
            pltpu.VMEM((part, LANES), F32),
            pltpu.VMEM(slab.shape, F32),
            pltpu.SemaphoreType.DMA((14,)),
            pltpu.SemaphoreType.DMA((14,)),
        ],
        compiler_params=_params(("arbitrary",)),
    )(*operands)


HBM = pl.BlockSpec(memory_space=pltpu.HBM)
SEM = pl.BlockSpec(memory_space=pltpu.SEMAPHORE)
DATAFLOW = pltpu.SideEffectType.DATAFLOW_SIDE_EFFECTING
TOKEN = jax.ShapeDtypeStruct((8, LANES), F32)


def _plan_copies(bufs, plan, send_sems, recv_sems):
    out = []
    for i, (src, src_row, dst, dst_row, recv_row, rows, device) in enumerate(plan):
        send = _remote(_rows(bufs[src], src_row, rows), _rows(bufs[dst], dst_row, rows), send_sems.at[i], recv_sems.at[i], device)
        landed = _rows(bufs[dst], recv_row, rows)
        recv = _remote(landed, landed, send_sems.at[i], recv_sems.at[i], device)
        out.append((send, recv))
    return out


def _split_call(name, bufs, wait=None, start=None, after=None):
    n = len(bufs)
    n_in = n + (2 if wait else 0) + (1 if after is not None else 0)
    n_start = len(start(0, 0, 0)) if start else 0

    def body(*refs):
        ins = refs[:n]
        x, y, c = _mesh_position()
        if wait:
            for send, recv in _plan_copies(ins, wait[0](x, y, c), refs[n], refs[n + 1]):
                recv.wait_recv()
                send.wait_send()
        if start:
            for send, _ in _plan_copies(ins, start(x, y, c), refs[n_in + n + 1], refs[n_in + n + 2]):
                send.start()
        token = refs[n_in + n]
        token[...] = jnp.zeros_like(token)

    operands = [pltpu.with_memory_space_constraint(b, pltpu.HBM) for b in bufs]
    in_specs = [HBM] * n
    if wait:
        operands += [wait[1], wait[2]]
        in_specs += [SEM, SEM]
    if after is not None:
        operands.append(after)
        in_specs.append(ANY)
    out_shape = [pltpu.HBM(b.shape, b.dtype) for b in bufs] + [TOKEN]
    out_specs = [HBM] * n + [pl.BlockSpec(memory_space=pltpu.VMEM)]
    if start:
        out_shape += [pltpu.SemaphoreType.DMA((n_start,)), pltpu.SemaphoreType.DMA((n_start,))]
        out_specs += [SEM, SEM]
    outs = pl.pallas_call(
        body,
        name=name,
        in_specs=in_specs,
        out_specs=out_specs,
        out_shape=out_shape,
        input_output_aliases={i: i for i in range(n)},
        compiler_params=pltpu.CompilerParams(has_side_effects=DATAFLOW),
    )(*operands)
    return (list(outs[:n]), outs[n]) + tuple(outs[n + 1 :])


def _direct_gather_plans(shard_rows):
    n = len(shard_rows)

    def direct(x, y, c):
        me = 2 * x + y
        plan = []
        for w, rows in enumerate(shard_rows):
            half = rows // 2
            for px, py in _other_chips(x, y):
                plan.append((w, c * half, n + w, me * rows + c * half, (2 * px + py) * rows + c * half, half, (px, py, c)))
            plan.append((w, 0, n + w, me * rows, me * rows, rows, (x, y, 1 - c)))
        return plan

    def passed_on(x, y, c):
        plan = []
        for w, rows in enumerate(shard_rows):
            half = rows // 2
            for px, py in _other_chips(x, y):
                row = (2 * px + py) * rows
                plan.append((n + w, row + c * half, n + w, row + c * half, row + (1 - c) * half, half, (x, y, 1 - c)))
        return plan

    return direct, passed_on


def _gather_plans(shard_rows):
    n = len(shard_rows)

    def neighbours(x, y):
        return ((1 - x, y), (x, 1 - y))

    def direct(x, y, c):
        me = 2 * x + y
        plan = []
        for w, rows in enumerate(shard_rows):
            half = rows // 2
            for px, py in neighbours(x, y):
                plan.append((w, c * half, n + w, me * rows + c * half, (2 * px + py) * rows + c * half, half, (px, py, c)))
            plan.append((w, 0, n + w, me * rows, me * rows, rows, (x, y, 1 - c)))
        return plan

    def passed_on(x, y, c):
        (xn, yn), diagonal = neighbours(x, y), 2 * (1 - x) + (1 - y)
        relayed = (1 - c) * (2 * xn[0] + xn[1]) + c * (2 * yn[0] + yn[1])
        target = (x * (1 - c) + (1 - x) * c, (1 - y) * (1 - c) + y * c, c)
        plan = []
        for w, rows in enumerate(shard_rows):
            half = rows // 2
            for px, py in (xn, yn):
                row = (2 * px + py) * rows
                plan.append((n + w, row + c * half, n + w, row + c * half, row + (1 - c) * half, half, (x, y, 1 - c)))
            plan.append((n + w, relayed * rows + c * half, n + w, relayed * rows + c * half, diagonal * rows + c * half, half, target))
        return plan

    def diagonal_passed_on(x, y, c):
        plan = []
        for w, rows in enumerate(shard_rows):
            half = rows // 2
            row = (2 * (1 - x) + (1 - y)) * rows
            plan.append((n + w, row + c * half, n + w, row + c * half, row + (1 - c) * half, half, (x, y, 1 - c)))
        return plan

    return direct, passed_on, diagonal_passed_on


def _swap_plan(block_rows):
    n = len(block_rows)

    def plan_fn(x, y, c):
        plan = []
        for w, rows in enumerate(block_rows):
            half = rows // 2
            for j in range(N_CHIPS):
                plan.append((w, j * rows + (1 - c) * half, n + w, j * half, j * half, half, (x, y, 1 - c)))
        return plan

    return plan_fn


def _exchange_plan(halves):
    n = len(halves)

    def plan_fn(x, y, c):
        plan = []
        for w, half in enumerate(halves):
            for kk, (px, py) in enumerate(_other_chips(x, y)):
                plan.append((w, (2 * px + py) * half, n + w, kk * half, kk * half, half, (px, py, c)))
        return plan

    return plan_fn


def _landing(rows, cols, dtype):
    return lax.empty((rows, cols), dtype)


def _row_tile(rows, cap=512):
    best = 8
    for cand in range(8, cap + 1, 8):
        if rows % cand == 0:
            best = cand
    return best


def _pair_sum(name, grad, theirs, pos):
    half = theirs.shape[0] // N_CHIPS
    cols = theirs.shape[1]
    tile = _row_tile(half)
    steps = half // tile

    def body(pos_ref, g_ref, t_ref, p_ref, own_ref):
        total = g_ref[...] + t_ref[...]
        p_ref[...] = total.astype(BF16)

        @pl.when(pl.program_id(1) == pos_ref[1])
        def _():
            own_ref[...] = total

    return pl.pallas_call(
        body,
        name=name,
        grid_spec=pltpu.PrefetchScalarGridSpec(
            num_scalar_prefetch=1,
            grid=(steps, N_CHIPS),
            in_specs=[
                pl.BlockSpec((tile, cols), lambda i, j, pos: ((2 * j + pos[0]) * steps + i, 0)),
                pl.BlockSpec((tile, cols), lambda i, j, pos: (j * steps + i, 0)),
            ],
            out_specs=[
                pl.BlockSpec((tile, cols), lambda i, j, pos: (j * steps + i, 0)),
                pl.BlockSpec((tile, cols), lambda i, j, pos: (i, 0)),
            ],
        ),
        out_shape=[jax.ShapeDtypeStruct((N_CHIPS * half, cols), BF16), jax.ShapeDtypeStruct((half, cols), F32)],
        compiler_params=_params(("parallel", "arbitrary")),
    )(pos, grad, theirs)


def _adamw_update(w, g, m, v):
    nm = ADAM_B1 * m + (1.0 - ADAM_B1) * g
    nv = ADAM_B2 * v + (1.0 - ADAM_B2) * (g * g)
    m_hat = nm / (1.0 - ADAM_B1**ADAM_STEP)
    v_hat = nv / (1.0 - ADAM_B2**ADAM_STEP)
    return -ADAM_LR * (m_hat / (jnp.sqrt(v_hat) + ADAM_EPS) + ADAM_WD * w), nm, nv


def _chip_sum(name, own, landed, pos):
    half, cols = own.shape
    tile = _row_tile(half)
    steps = half // tile

    def body(pos_ref, own_ref, l0, l1, l2, o_ref):
        o_ref[...] = ((own_ref[...] + l0[...].astype(F32)) + l1[...].astype(F32)) + l2[...].astype(F32)

    landed_specs = [pl.BlockSpec((tile, cols), lambda i, pos, _k=k: (_k * steps + i, 0)) for k in range(N_CHIPS - 1)]
    return pl.pallas_call(
        body,
        name=name,
        grid_spec=pltpu.PrefetchScalarGridSpec(
            num_scalar_prefetch=1,
            grid=(steps,),
            in_specs=[pl.BlockSpec((tile, cols), lambda i, pos: (i, 0))] + landed_specs,
            out_specs=pl.BlockSpec((tile, cols), lambda i, pos: (pos[0] * steps + i, 0)),
        ),
        out_shape=jax.ShapeDtypeStruct((2 * half, cols), F32),
        compiler_params=_params(("parallel",)),
    )(pos, own, landed, landed, landed)


def _pair_gather(name, shards):
    n = len(shards)

    def body(*refs):
        outs = refs[n : 2 * n]
        send_sems, recv_sems = refs[2 * n :]
        x, y, c = _mesh_position()
        sibling = (x, y, 1 - c)
        sends = []
        for w in range(n):
            half = shards[w].shape[0] // 2
            mine = _rows(outs[w], c * half, half)
            cp = _remote(mine, mine, send_sems.at[w], recv_sems.at[w], sibling)
            cp.start()
            sends.append(cp)
        for w in range(n):
            half = shards[w].shape[0] // 2
            blk = _rows(outs[w], (1 - c) * half, half)
            _remote(blk, blk, send_sems.at[w], recv_sems.at[w], sibling).wait_recv()
        for cp in sends:
            cp.wait_send()

    return pl.pallas_call(
        body,
        name=name,
        in_specs=[ANY] * n,
        out_specs=[ANY] * n,
        out_shape=[jax.ShapeDtypeStruct(s.shape, s.dtype) for s in shards],
        input_output_aliases={w: w for w in range(n)},
        scratch_shapes=[pltpu.SemaphoreType.DMA((n,)), pltpu.SemaphoreType.DMA((n,))],
    )(*shards)


def _adamw(name, w, g, m, v):
    rows, cols = w.shape
    tile = rows if rows * cols <= 256 * 1024 else _row_tile(rows)

    def body(w_ref, g_ref, m_ref, v_ref, g_out_ref, d_ref, nm_ref, nv_ref):
        g = g_ref[...]
        g_out_ref[...] = g
        d_ref[...], nm_ref[...], nv_ref[...] = _adamw_update(w_ref[...], g, m_ref[...], v_ref[...])

    spec = _row_spec(tile, cols)
    return pl.pallas_call(
        body,
        name=name,
        grid=(rows // tile,),
        in_specs=[spec] * 4,
        out_specs=[spec] * 4,
        out_shape=[jax.ShapeDtypeStruct((rows, cols), F32)] * 4,
        compiler_params=_params(("parallel",)),
    )(w, g, m, v)


_SMALL = (
    ("v_ln_g", (D_GMLP,), 8),
    ("v_ln_b", (D_GMLP,), 8),
    ("w_spatial", (N_HEADS, CHUNK, CHUNK), 1024),
    ("b_spatial", (N_HEADS, CHUNK), 8),
    ("sinks", (N_HEADS,), 8),
    ("ln1_g", (D_MODEL,), 8),
    ("ln1_b", (D_MODEL,), 8),
    ("ln2_g", (D_MODEL,), 8),
    ("ln2_b", (D_MODEL,), 8),
    ("squared_error", (D_MODEL,), 8),
)
N_SMALL_PARAMS = len(_SMALL) - 1


def _pack_small(values):
    parts = []
    for (name, shape, rows), val in zip(_SMALL, values, strict=True):
        flat = val.reshape(-1).astype(F32)
        parts.append(jnp.pad(flat, (0, rows * LANES - flat.shape[0])).reshape(rows, LANES))
    parts.append(jnp.zeros((SMALL_ROWS - sum(rows for _, _, rows in _SMALL), LANES), F32))
    return jnp.concatenate(parts, axis=0)


def _adamw_small(g_slab, params, first, second):
    n = N_SMALL_PARAMS

    def pieces(shape):
        if len(shape) == 3:
            return [((0, h), h * shape[1], shape[1], shape[2]) for h in range(shape[0])]
        if len(shape) == 2:
            return [((0,), 0, shape[0], shape[1])]
        if shape[0] >= LANES:
            return [((slice(None), slice(r * LANES, (r + 1) * LANES)), r, 1, LANES) for r in range(shape[0] // LANES)]
        return [((slice(None), slice(0, shape[0])), 0, 1, shape[0])]

    def body(*refs):
        g_ref = refs[0]
        w_refs, m_refs, v_refs = refs[1 : 1 + n], refs[1 + n : 1 + 2 * n], refs[1 + 2 * n : 1 + 3 * n]
        outs = refs[1 + 3 * n :]
        row0 = 0
        for idx, (_, shape, rows) in enumerate(_SMALL[:n]):
            for where, first_row, n_rows, lanes in pieces(shape):
                g = g_ref[row0 + first_row : row0 + first_row + n_rows, 0:lanes]
                delta, nm, nv = _adamw_update(w_refs[idx][where], g, m_refs[idx][where], v_refs[idx][where])
                for group, val in enumerate((g, delta, nm, nv)):
                    outs[group * n + idx][where] = val
            row0 += rows

    vmem = pl.BlockSpec(memory_space=pltpu.VMEM)
    shapes = [jax.ShapeDtypeStruct(p.shape, F32) for p in params]
    outs = pl.pallas_call(
        body,
        name="adamw_small",
        in_specs=[vmem] * (1 + 3 * n),
        out_specs=[vmem] * (4 * n),
        out_shape=shapes * 4,
        compiler_params=_params(),
    )(g_slab, *params, *first, *second)
    return [list(outs[group * n : (group + 1) * n]) for group in range(4)]


def kernel(x, positions, w_in, v_ln_g, v_ln_b, w_spatial, b_spatial, sinks, w_out, ln1_g, ln1_b, w_ff1, w_ff2, ln2_g, ln2_b, loss_target, m_w_in, m_v_ln_g, m_v_ln_b, m_w_spatial, m_b_spatial, m_sinks, m_w_out, m_ln1_g, m_ln1_b, m_w_ff1, m_w_ff2, m_ln2_g, m_ln2_b, v_w_in, v_v_ln_g, v_v_ln_b, v_w_spatial, v_b_spatial, v_sinks, v_w_out, v_ln1_g, v_ln1_b, v_w_ff1, v_w_ff2, v_ln2_g, v_ln2_b):
    t = x.shape[1]
    x2 = x.reshape(t, D_MODEL)
    target = loss_target.reshape(t, D_MODEL)

    w_in_shard = w_in[0].T.astype(BF16)
    in_direct, in_pass = _direct_gather_plans([w_in_shard.shape[0]])
    in_bufs, in_started, in_send, in_recv = _split_call(
        "gather_w_in_start", [w_in_shard, _landing(N_CHIPS * w_in_shard.shape[0], D_MODEL, BF16)], start=in_direct)
    inv_freq = ROPE_THETA ** (-jnp.arange(0, HEAD_DIM, 2, dtype=F32) / HEAD_DIM)
    cos, sin, later = _rope_tables_and_casts(
        positions, jnp.tile(inv_freq, LANES // (HEAD_DIM // 2)).reshape(1, LANES), [w_out[0], w_ff1[0], w_ff2[0]], dep=in_started)
    later_rows = [s.shape[0] for s in later]
    direct_plan, pass_plan, diagonal_plan = _gather_plans(later_rows)
    bufs, started, direct_send, direct_recv = _split_call(
        "gather_start", later + [_landing(N_CHIPS * r, D_MODEL, BF16) for r in later_rows], start=direct_plan, after=cos)
    in_bufs, in_passing, in_pass_send, in_pass_recv = _split_call(
        "gather_w_in_pass", in_bufs, wait=(in_direct, in_send, in_recv), start=in_pass, after=started)
    in_bufs, _ = _split_call("gather_w_in_end", in_bufs, wait=(in_pass, in_pass_send, in_pass_recv), after=in_passing)
    w_in_t = in_bufs[1]

    u, vg, q, k, va = _in_proj(x2, w_in_t, cos, sin)
    bias_full = jnp.repeat(b_spatial[0].T, HEAD_DIM, axis=1)
    sink_vec = sinks.reshape(N_HEADS)
    bufs, passing, pass_send, pass_recv = _split_call(
        "gather_pass", bufs, wait=(direct_plan, direct_send, direct_recv), start=pass_plan, after=u)
    cat = _mixer_fwd(u, vg, q, k, va, v_ln_g, v_ln_b, w_spatial[0], bias_full, sink_vec, dep=passing)
    bufs, passing, diag_send, diag_recv = _split_call(
        "gather_pass_diagonal", bufs, wait=(pass_plan, pass_send, pass_recv), start=diagonal_plan, after=cat)
    bufs, _ = _split_call("gather_end", bufs, wait=(diagonal_plan, diag_send, diag_recv), after=passing)
    w_out_all = bufs[3]
    w1_all = bufs[4].reshape(N_FF_BLOCKS, D_MODEL, D_MODEL)
    w2_all = bufs[5].reshape(N_FF_BLOCKS, D_MODEL, D_MODEL)
    xhat1, rstd1, x1b, r, dz2, dz2b, d_ln2_g, d_ln2_b, sq_err = _ffn_fwd_loss(
        cat, x2, w_out_all, ln1_g, ln1_b, w1_all, w2_all, ln2_g, ln2_b, target)

    pos = jnp.stack([lax.axis_index("c"), 2 * lax.axis_index("x") + lax.axis_index("y")]).astype(jnp.int32)
    half_landing = lambda g: _landing(g.shape[0] // 2, D_MODEL, F32)
    ff_swap_plan = _swap_plan([D_FF // N_CHIPS])
    ff_exchange_plan = _exchange_plan([D_FF // N_CHIPS // 2])
    exchange_landing = lambda p: _landing(3 * p.shape[0] // N_CHIPS, D_MODEL, BF16)
    g_ff1_local, dz1, dz1b, dcat, d_ln1_g, d_ln1_b = _ffn_bwd_ln1(dz2, r, x1b, xhat1, rstd1, ln1_g, w1_all, w2_all, w_out_all)
    ff1_bufs, swapping1, swap1_send, swap1_recv = _split_call("ff1_swap_start", [g_ff1_local, half_landing(g_ff1_local)], start=ff_swap_plan)
    g_out_local = _grad_w_out(cat, dz1b, dep=swapping1)
    ff1_bufs, _ = _split_call("ff1_swap_wait", ff1_bufs, wait=(ff_swap_plan, swap1_send, swap1_recv), after=g_out_local)
    ff1_sum, ff1_own = _pair_sum("grad_pair_sum_w_ff1", ff1_bufs[0], ff1_bufs[1], pos)
    ff1_ex, exchanging1, ex1_send, ex1_recv = _split_call(
        "ff1_exchange_start", [ff1_sum, exchange_landing(ff1_sum)], start=ff_exchange_plan)
    dh_main, dkv, g_ff2_local, d_v_ln_g, d_v_ln_b, d_w_spatial, d_b_spatial_t, d_sinks = _mixer_bwd(
        u, vg, q, k, va, dcat, cos, sin, v_ln_g, v_ln_b, w_spatial[0], bias_full, sink_vec, r, dz2b, dep=exchanging1)
    ff2_bufs, swapping2, swap2_send, swap2_recv = _split_call("ff2_swap_start", [g_ff2_local, half_landing(g_ff2_local)], start=ff_swap_plan)
    g_in_local, small_g = _grad_w_in_t_and_small_all_reduce(dh_main, dkv, x2, _pack_small(
        [d_v_ln_g, d_v_ln_b, d_w_spatial, d_b_spatial_t[:, :N_HEADS].T, d_sinks[0, :N_HEADS], d_ln1_g, d_ln1_b, d_ln2_g, d_ln2_b, sq_err]),
        dep=swapping2)
    sq_row = sum(rows for _, _, rows in _SMALL[:N_SMALL_PARAMS])
    loss = 0.5 * jnp.sum(small_g[sq_row : sq_row + _SMALL[N_SMALL_PARAMS][2]]) / D_MODEL
    ff2_bufs, _ = _split_call("ff2_swap_wait", ff2_bufs, wait=(ff_swap_plan, swap2_send, swap2_recv), after=g_in_local)
    ff2_sum, ff2_own = _pair_sum("grad_pair_sum_w_ff2", ff2_bufs[0], ff2_bufs[1], pos)
    ff2_ex, exchanging2, ex2_send, ex2_recv = _split_call(
        "ff2_exchange_start", [ff2_sum, exchange_landing(ff2_sum)], start=ff_exchange_plan)

    small = [g_in_local, g_out_local]
    small_swap_plan = _swap_plan([g.shape[0] // N_CHIPS for g in small])
    swap_bufs, small_swapping, ss_send, ss_recv = _split_call(
        "small_swap_start", small + [half_landing(g) for g in small], start=small_swap_plan, after=exchanging2)
    grad_x_flat = _grad_x(dh_main, dkv, dz1, w_in_t, dep=small_swapping)
    grad_x = grad_x_flat.reshape(1, t, D_MODEL)
    swap_bufs, _ = _split_call("small_swap_wait", swap_bufs, wait=(small_swap_plan, ss_send, ss_recv), after=grad_x_flat)
    pair_sums = [_pair_sum("grad_pair_sum_" + nm, g, th, pos) for nm, g, th in zip(["w_in", "w_out"], swap_bufs[:2], swap_bufs[2:])]
    small_plan = _exchange_plan([p.shape[0] // N_CHIPS for p, _ in pair_sums])
    small_bufs, small_exchanging, sm_send, sm_recv = _split_call(
        "small_exchange_start", [p for p, _ in pair_sums] + [exchange_landing(p) for p, _ in pair_sums], start=small_plan)

    ff1_ex, _ = _split_call("ff1_exchange_wait", ff1_ex, wait=(ff_exchange_plan, ex1_send, ex1_recv), after=small_exchanging)
    (g_w_ff1,) = _pair_gather("grad_pair_gather_ff1", [_chip_sum("grad_chip_sum_w_ff1", ff1_own, ff1_ex[1], pos)])
    g_w_ff1, d_w_ff1, nm_w_ff1, nv_w_ff1 = _adamw("adamw_w_ff1", w_ff1[0], g_w_ff1, m_w_ff1[0], v_w_ff1[0])
    ff2_ex, _ = _split_call("ff2_exchange_wait", ff2_ex, wait=(ff_exchange_plan, ex2_send, ex2_recv), after=nv_w_ff1)
    (g_w_ff2,) = _pair_gather("grad_pair_gather_ff2", [_chip_sum("grad_chip_sum_w_ff2", ff2_own, ff2_ex[1], pos)])
    g_w_ff2, d_w_ff2, nm_w_ff2, nv_w_ff2 = _adamw("adamw_w_ff2", w_ff2[0], g_w_ff2, m_w_ff2[0], v_w_ff2[0])
    small_bufs, _ = _split_call("small_exchange_wait", small_bufs, wait=(small_plan, sm_send, sm_recv), after=nv_w_ff2)
    shards = [_chip_sum("grad_chip_sum_" + nm, own, ld, pos) for nm, (_, own), ld in zip(["w_in", "w_out"], pair_sums, small_bufs[2:])]
    g_w_in_t, g_w_out = _pair_gather("grad_pair_gather_small", shards)
    g_w_in, d_w_in, nm_w_in, nv_w_in = (a.T for a in _adamw("adamw_w_in", w_in[0].T, g_w_in_t, m_w_in[0].T, v_w_in[0].T))
    g_w_out, d_w_out, nm_w_out, nv_w_out = _adamw("adamw_w_out", w_out[0], g_w_out, m_w_out[0], v_w_out[0])
    small_grads, small_d, small_nm, small_nv = _adamw_small(
        small_g,
        [v_ln_g, v_ln_b, w_spatial, b_spatial, sinks, ln1_g, ln1_b, ln2_g, ln2_b],
        [m_v_ln_g, m_v_ln_b, m_w_spatial, m_b_spatial, m_sinks, m_ln1_g, m_ln1_b, m_ln2_g, m_ln2_b],
        [v_v_ln_g, v_v_ln_b, v_w_spatial, v_b_spatial, v_sinks, v_ln1_g, v_ln1_b, v_ln2_g, v_ln2_b])

    def with_big(small, w_in_v, w_out_v, w_ff1_v, w_ff2_v):
        g_vg, g_vb, g_ws, g_bs, g_sk, g_1g, g_1b, g_2g, g_2b = small
        return [w_in_v[None], g_vg, g_vb, g_ws, g_bs, g_sk, w_out_v[None], g_1g, g_1b, w_ff1_v[None], w_ff2_v[None], g_2g, g_2b]

    return (
        loss,
        grad_x,
        *with_big(small_grads, g_w_in, g_w_out, g_w_ff1, g_w_ff2),
        *with_big(small_d, d_w_in, d_w_out, d_w_ff1, d_w_ff2),
        *with_big(small_nm, nm_w_in, nm_w_out, nm_w_ff1, nm_w_ff2),
        *with_big(small_nv, nv_w_in, nv_w_out, nv_w_ff1, nv_w_ff2),
    )
```

```python
import math

import jax
import jax.numpy as jnp
from jax import lax
from jax.experimental import pallas as pl
from jax.experimental.pallas import tpu as pltpu

F32 = jnp.float32
BF16 = jnp.bfloat16

D_MODEL = 1024
HEAD_DIM = 64
D_GMLP = 512
D_ATTN = 512
D_KV = 128
D_IN = 2 * D_GMLP + D_ATTN + 2 * D_KV
D_MAIN = 2 * D_GMLP + D_ATTN
N_HEADS = 8
CHUNK = 128
CHUNKS_PER_STEP = 4
ROPE_THETA = 10000.0
D_FF = 4 * D_MODEL
N_FF_BLOCKS = 4
LN_EPS = 1e-5
ALPHA = (2.0 * 1) ** 0.25
NEG_INF = -1e30
SCALE = 1.0 / math.sqrt(HEAD_DIM)

ADAM_LR = 0.001
ADAM_B1 = 0.9
ADAM_B2 = 0.999
ADAM_EPS = 1e-08
ADAM_WD = 0.01
ADAM_STEP = 10

N_CHIPS = 4
LANES = 128
V7X_VMEM_BYTES = 64 * 1024 * 1024
VMEM_LIMIT = V7X_VMEM_BYTES - 8 * 1024 * 1024
TM = 512
TM_FFN = 256
TM_FFN_FWD = 512
FFN_PART = 256
TK = 1024
SMALL_ROWS = 1152
MESH = pl.DeviceIdType.MESH

NT = (((1,), (1,)), ((), ()))
TN = (((0,), (0,)), ((), ()))


def _dot(a, b, dims=None):
    if dims is None:
        return jnp.dot(a, b, preferred_element_type=F32)
    return lax.dot_general(a, b, dims, preferred_element_type=F32)


def _params(semantics=None):
    return pltpu.CompilerParams(dimension_semantics=semantics, vmem_limit_bytes=VMEM_LIMIT)


def _const_spec(shape, single_buffer=False):
    zeros = (0,) * len(shape)
    if single_buffer:
        return pl.BlockSpec(shape, lambda *_: zeros, pipeline_mode=pl.Buffered(1))
    return pl.BlockSpec(shape, lambda *_: zeros)


def _row_spec(rows, cols):
    return pl.BlockSpec((rows, cols), lambda i: (i, 0))


def _after(dep, body, in_specs, operands):
    if dep is None:
        return body, list(in_specs), list(operands)
    return (lambda dep_ref, *refs: body(*refs)), [pl.BlockSpec(memory_space=pl.ANY)] + list(in_specs), [dep] + list(operands)


def _gelu(x):
    k = math.sqrt(2.0 / math.pi)
    return 0.5 * x * (1.0 + jnp.tanh(k * (x + 0.044715 * (x * x * x))))


def _gelu_and_grad(x):
    k = math.sqrt(2.0 / math.pi)
    x2 = x * x
    t = jnp.tanh(k * (x + 0.044715 * (x2 * x)))
    g = 0.5 * x * (1.0 + t)
    dg = 0.5 * (1.0 + t) + 0.5 * x * (1.0 - t * t) * (k * (1.0 + 3.0 * 0.044715 * x2))
    return g, dg


def _layer_norm_stats(z):
    mu = jnp.mean(z, axis=-1, keepdims=True)
    zc = z - mu
    var = jnp.mean(zc * zc, axis=-1, keepdims=True)
    rstd = lax.rsqrt(var + LN_EPS)
    return zc * rstd, rstd


def _layer_norm_bwd(dxhat, xhat, rstd):
    m1 = jnp.mean(dxhat, axis=-1, keepdims=True)
    m2 = jnp.mean(dxhat * xhat, axis=-1, keepdims=True)
    return rstd * (dxhat - m1 - xhat * m2)


def _rotate_half(t):
    n = t.shape[1]
    lane = lax.broadcasted_iota(jnp.int32, t.shape, 1)
    first = (lane & (HEAD_DIM // 2)) == 0
    return jnp.where(first, -pltpu.roll(t, n - HEAD_DIM // 2, 1), pltpu.roll(t, HEAD_DIM // 2, 1))


def _rope(t, cos, sin):
    return t * cos + _rotate_half(t) * sin


def _rope_transposed(g, cos, sin):
    return g * cos - _rotate_half(g * sin)


def _lane_tile(a, reps):
    return jnp.tile(a, (1, reps)) if reps > 1 else a


def _in_proj(x, w_in_t, cos, sin, dep=None):
    t = x.shape[0]

    def body(x_ref, w_ref, cos_ref, sin_ref, u_ref, vg_ref, q_ref, k_ref, va_ref):
        xb = x_ref[...].astype(BF16)
        u_ref[...] = _dot(xb, w_ref[0:D_GMLP, :], NT)
        vg_ref[...] = _dot(xb, w_ref[D_GMLP : 2 * D_GMLP, :], NT)
        q = _dot(xb, w_ref[2 * D_GMLP : D_MAIN, :], NT)
        k = _dot(xb, w_ref[D_MAIN : D_MAIN + D_KV, :], NT)
        va_ref[...] = _dot(xb, w_ref[D_MAIN + D_KV : D_IN, :], NT).astype(BF16)
        c, s = cos_ref[...], sin_ref[...]
        q_ref[...] = _rope(q, _lane_tile(c, D_ATTN // LANES), _lane_tile(s, D_ATTN // LANES)).astype(BF16)
        k_ref[...] = _rope(k, c, s).astype(BF16)

    body, in_specs, operands = _after(
        dep, body, [_row_spec(TM, D_MODEL), _const_spec((D_IN, D_MODEL)), _row_spec(TM, LANES), _row_spec(TM, LANES)], [x, w_in_t, cos, sin])
    return pl.pallas_call(
        body,
        name="in_proj",
        grid=(t // TM,),
        in_specs=in_specs,
        out_specs=[_row_spec(TM, D_GMLP), _row_spec(TM, D_GMLP), _row_spec(TM, D_ATTN), _row_spec(TM, D_KV), _row_spec(TM, D_KV)],
        out_shape=[
            jax.ShapeDtypeStruct((t, D_GMLP), F32),
            jax.ShapeDtypeStruct((t, D_GMLP), F32),
            jax.ShapeDtypeStruct((t, D_ATTN), BF16),
            jax.ShapeDtypeStruct((t, D_KV), BF16),
            jax.ShapeDtypeStruct((t, D_KV), BF16),
        ],
        compiler_params=_params(("parallel",)),
    )(*operands)


def _step_rows(i):
    return (i, 0)


def _chunk_before_step(i):
    return (jnp.maximum(CHUNKS_PER_STEP * i - 1, 0), 0)


def _chunk_specs():
    step = CHUNKS_PER_STEP * CHUNK
    return [
        pl.BlockSpec((step, D_GMLP), _step_rows),
        pl.BlockSpec((step, D_GMLP), _step_rows),
        pl.BlockSpec((step, D_ATTN), _step_rows),
        pl.BlockSpec((step, D_KV), _step_rows),
        pl.BlockSpec((CHUNK, D_KV), _chunk_before_step),
        pl.BlockSpec((step, D_KV), _step_rows),
        pl.BlockSpec((CHUNK, D_KV), _chunk_before_step),
    ]


def _half_lane_masks(rows):
    lane = lax.broadcasted_iota(jnp.int32, (rows, LANES), 1)
    return lane < HEAD_DIM


def _kv_variants(kv2):
    left = _half_lane_masks(kv2.shape[0])
    f = kv2.astype(F32)
    swapped = pltpu.roll(f, HEAD_DIM, 1)
    zero = jnp.zeros_like(f)
    g0 = (jnp.where(left, f, zero).astype(BF16), jnp.where(left, zero, swapped).astype(BF16))
    g1 = (jnp.where(left, swapped, zero).astype(BF16), jnp.where(left, zero, f).astype(BF16))
    return (g0, g1)


def _band_mask(i, heads=1):
    row = lax.broadcasted_iota(jnp.int32, (heads * CHUNK, 2 * CHUNK), 0) & (CHUNK - 1)
    col = lax.broadcasted_iota(jnp.int32, (heads * CHUNK, 2 * CHUNK), 1)
    no_prev = jnp.where(i > 0, 0, 4 * CHUNK)
    in_prev = jnp.logical_and(col < CHUNK, (col - row) > no_prev)
    in_cur = jnp.logical_and(col >= CHUNK, (col - CHUNK) <= row)
    return jnp.logical_or(in_prev, in_cur)


def _causal_mask():
    row = lax.broadcasted_iota(jnp.int32, (CHUNK, CHUNK), 0)
    col = lax.broadcasted_iota(jnp.int32, (CHUNK, CHUNK), 1)
    return col <= row


def _store_spatial_weights(w_ref, wcat_ref, wcat_t_ref=None):
    causal = _causal_mask()
    for p in range(D_GMLP // LANES):
        wl = jnp.where(causal, w_ref[2 * p], 0.0)
        wr = jnp.where(causal, w_ref[2 * p + 1], 0.0)
        wcat_ref[p] = jnp.concatenate([wl, wr], axis=1).astype(BF16)
        if wcat_t_ref is not None:
            wcat_t_ref[p] = jnp.concatenate([wl.T, wr.T], axis=1).astype(BF16)


def _pair_stack(xp, left):
    return jnp.concatenate([jnp.where(left, xp, 0.0), jnp.where(left, 0.0, xp)], axis=0).astype(BF16)


def _mixer_fwd(u, vg, q, k, va, v_ln_g, v_ln_b, w_spatial, bias_full, sinks, dep=None):
    t = u.shape[0]

    def body(u_ref, vg_ref, q_ref, kc_ref, kp_ref, vc_ref, vp_ref, g_ref, b_ref, w_ref, bias_ref, sink_ref, cat_ref, wcat):
        i = pl.program_id(0)
        left = _half_lane_masks(CHUNK)

        @pl.when(i == 0)
        def _():
            _store_spatial_weights(w_ref, wcat)

        heads = range(N_HEADS)
        pair_cols = [slice(p * LANES, (p + 1) * LANES) for p in range(D_GMLP // LANES)]
        sinks_h = [sink_ref[h] for h in heads]
        for c in range(CHUNKS_PER_STEP):
            rows = slice(c * CHUNK, (c + 1) * CHUNK)
            before = slice((c - 1) * CHUNK, c * CHUNK)
            k_prev = kp_ref[...] if c == 0 else kc_ref[before, :]
            v_prev = vp_ref[...] if c == 0 else vc_ref[before, :]
            k_var = _kv_variants(jnp.concatenate([k_prev, kc_ref[rows, :]], axis=0))
            v_var = _kv_variants(jnp.concatenate([v_prev, vc_ref[rows, :]], axis=0))
            scores = [_dot(q_ref[rows, pair_cols[h // 2]], k_var[h // 4][h % 2], NT) for h in heads]

            ug = _gelu(u_ref[rows, :])
            xhat, _ = _layer_norm_stats(_gelu(vg_ref[rows, :]))
            vgl = xhat * g_ref[...] + b_ref[...]
            mixed = [_dot(wcat[p], _pair_stack(vgl[:, cols], left)) for p, cols in enumerate(pair_cols)]

            valid = _band_mask(CHUNKS_PER_STEP * i + c)
            masked = [jnp.where(valid, scores[h] * SCALE, NEG_INF) for h in heads]
            maxes = [jnp.maximum(jnp.max(masked[h], axis=1, keepdims=True), sinks_h[h]) for h in heads]
            exps = [jnp.exp(masked[h] - maxes[h]) for h in heads]
            invs = [1.0 / (jnp.sum(exps[h], axis=1, keepdims=True) + jnp.exp(sinks_h[h] - maxes[h])) for h in heads]
            probs = [(exps[h] * invs[h]).astype(BF16) for h in heads]
            for p, cols in enumerate(pair_cols):
                cat_ref[rows, cols] = (ug[:, cols] * (mixed[p] + bias_ref[:, cols])).astype(BF16)
            for p in range(D_ATTN // LANES):
                out = _dot(probs[2 * p], v_var[p // 2][0]) + _dot(probs[2 * p + 1], v_var[p // 2][1])
                cat_ref[rows, D_GMLP + p * LANES : D_GMLP + (p + 1) * LANES] = out.astype(BF16)

    in_specs = _chunk_specs() + [
        _const_spec((1, D_GMLP)),
        _const_spec((1, D_GMLP)),
        _const_spec((N_HEADS, CHUNK, CHUNK)),
        _const_spec((CHUNK, D_GMLP)),
        pl.BlockSpec(memory_space=pltpu.SMEM),
    ]
    body, in_specs, operands = _after(dep, body, in_specs, [u, vg, q, k, k, va, va, v_ln_g, v_ln_b, w_spatial, bias_full, sinks])
    return pl.pallas_call(
        body,
        name="mixer_fwd",
        grid=(t // (CHUNKS_PER_STEP * CHUNK),),
        in_specs=in_specs,
        out_specs=pl.BlockSpec((CHUNKS_PER_STEP * CHUNK, D_MODEL), lambda i: (i, 0)),
        out_shape=jax.ShapeDtypeStruct((t, D_MODEL), BF16),
        scratch_shapes=[pltpu.VMEM((D_GMLP // LANES, CHUNK, 2 * CHUNK), BF16)],
        compiler_params=_params(("arbitrary",)),
    )(*operands)


def _ffn_fwd_loss(cat, x, w_out, ln1_g, ln1_b, w1, w2, ln2_g, ln2_b, target):
    t = x.shape[0]

    def body(cat_ref, x_ref, wo_ref, g1_ref, b1_ref, w1_ref, w2_ref, g2_ref, b2_ref, tgt_ref,
             xh_ref, rstd_ref, x1b_ref, r_ref, dz2_ref, dz2b_ref, dg2_ref, db2_ref, sq_ref):
        @pl.when(pl.program_id(0) == 0)
        def _():
            dg2_ref[...] = jnp.zeros_like(dg2_ref)
            db2_ref[...] = jnp.zeros_like(db2_ref)
            sq_ref[...] = jnp.zeros_like(sq_ref)

        parts = [slice(p * FFN_PART, (p + 1) * FFN_PART) for p in range(TM_FFN_FWD // FFN_PART)]

        def norm1(rows, z):
            xhat1, rstd1 = _layer_norm_stats(z)
            xh_ref[rows, :] = xhat1
            rstd_ref[rows, :] = rstd1
            x1 = xhat1 * g1_ref[...] + b1_ref[...]
            x1b = x1.astype(BF16)
            x1b_ref[rows, :] = x1b
            return x1, x1b

        def feed_forward(rows, x1b, pre):
            ff = None
            for j in range(N_FF_BLOCKS):
                r = jnp.maximum(pre, 0.0)
                r_ref[rows, j * D_MODEL : (j + 1) * D_MODEL] = r.astype(BF16)
                part = _dot((r * r).astype(BF16), w2_ref[j])
                ff = part if ff is None else ff + part
                if j + 1 < N_FF_BLOCKS:
                    pre = _dot(x1b, w1_ref[j + 1])
            return ff

        def norm2_and_loss(rows, x1, ff):
            xhat2, rstd2 = _layer_norm_stats(ALPHA * x1 + ff)
            err = xhat2 * g2_ref[...] + b2_ref[...] - tgt_ref[rows, :]
            sq_ref[...] += jnp.sum(err * err, axis=0, keepdims=True)
            dy = err * (1.0 / D_MODEL)
            dg2_ref[...] += jnp.sum(dy * xhat2, axis=0, keepdims=True)
            db2_ref[...] += jnp.sum(dy, axis=0, keepdims=True)
            dz2 = _layer_norm_bwd(dy * g2_ref[...], xhat2, rstd2)
            dz2_ref[rows, :] = dz2
            dz2b_ref[rows, :] = dz2.astype(BF16)

        projected = _dot(cat_ref[parts[0], :], wo_ref[...])
        last = None
        for i, rows in enumerate(parts):
            z = ALPHA * x_ref[rows, :] + projected
            if i + 1 < len(parts):
                projected = _dot(cat_ref[parts[i + 1], :], wo_ref[...])
            x1, x1b = norm1(rows, z)
            pre = _dot(x1b, w1_ref[0])
            if last is not None:
                norm2_and_loss(*last)
            last = (rows, x1, feed_forward(rows, x1b, pre))
        norm2_and_loss(*last)

    vec = _const_spec((1, D_MODEL))
    tile = _row_spec(TM_FFN_FWD, D_MODEL)
    wspec = _const_spec((N_FF_BLOCKS, D_MODEL, D_MODEL), single_buffer=True)
    return pl.pallas_call(
        body,
        name="ffn_fwd_loss",
        grid=(t // TM_FFN_FWD,),
        in_specs=[tile, tile, _const_spec((D_MODEL, D_MODEL), single_buffer=True), vec, vec, wspec, wspec, vec, vec, tile],
        out_specs=[tile, _row_spec(TM_FFN_FWD, 1), tile, _row_spec(TM_FFN_FWD, D_FF), tile, tile, vec, vec, vec],
        out_shape=[
            jax.ShapeDtypeStruct((t, D_MODEL), F32),
            jax.ShapeDtypeStruct((t, 1), F32),
            jax.ShapeDtypeStruct((t, D_MODEL), BF16),
            jax.ShapeDtypeStruct((t, D_FF), BF16),
            jax.ShapeDtypeStruct((t, D_MODEL), F32),
            jax.ShapeDtypeStruct((t, D_MODEL), BF16),
            jax.ShapeDtypeStruct((1, D_MODEL), F32),
            jax.ShapeDtypeStruct((1, D_MODEL), F32),
            jax.ShapeDtypeStruct((1, D_MODEL), F32),
        ],
        compiler_params=_params(("arbitrary",)),
    )(cat, x, w_out, ln1_g, ln1_b, w1, w2, ln2_g, ln2_b, target)


def _ffn_bwd_ln1(dz2, r, x1b, xhat1, rstd1, ln1_g, w1, w2, w_out, dep=None):
    t = dz2.shape[0]

    def body(dz2_ref, r_ref, x1b_ref, xh_ref, rstd_ref, g1_ref, w1_ref, w2_ref, wo_ref, gw1_ref, dz1_ref, dz1b_ref, dcat_ref, dg1_ref, db1_ref):
        @pl.when(pl.program_id(0) == 0)
        def _():
            dg1_ref[...] = jnp.zeros_like(dg1_ref)
            db1_ref[...] = jnp.zeros_like(db1_ref)
            gw1_ref[...] = jnp.zeros_like(gw1_ref)

        dz2 = dz2_ref[...]
        dz2b = dz2.astype(BF16)
        x1_t = x1b_ref[...].astype(F32).T.astype(BF16)
        dx1 = ALPHA * dz2
        for j in range(N_FF_BLOCKS):
            cols = slice(j * D_MODEL, (j + 1) * D_MODEL)
            dpre = (_dot(dz2b, w2_ref[j], NT) * (2.0 * r_ref[:, cols].astype(F32))).astype(BF16)
            gw1_ref[cols, :] += _dot(x1_t, dpre)
            dx1 = dx1 + _dot(dpre, w1_ref[j], NT)
        xhat1 = xh_ref[...]
        dg1_ref[...] += jnp.sum(dx1 * xhat1, axis=0, keepdims=True)
        db1_ref[...] += jnp.sum(dx1, axis=0, keepdims=True)
        dz1 = _layer_norm_bwd(dx1 * g1_ref[...], xhat1, rstd_ref[...])
        dz1_ref[...] = dz1
        dz1b = dz1.astype(BF16)
        dz1b_ref[...] = dz1b
        dcat_ref[...] = _dot(dz1b, wo_ref[...], NT).astype(BF16)

    vec = _const_spec((1, D_MODEL))
    tile = _row_spec(TM_FFN, D_MODEL)
    wspec = _const_spec((N_FF_BLOCKS, D_MODEL, D_MODEL), single_buffer=True)
    body, in_specs, operands = _after(
        dep, body,
        [tile, _row_spec(TM_FFN, D_FF), tile, tile, _row_spec(TM_FFN, 1), vec, wspec, wspec, _const_spec((D_MODEL, D_MODEL), single_buffer=True)],
        [dz2, r, x1b, xhat1, rstd1, ln1_g, w1, w2, w_out])
    return pl.pallas_call(
        body,
        name="ffn_bwd_ln1",
        grid=(t // TM_FFN,),
        in_specs=in_specs,
        out_specs=[_const_spec((D_FF, D_MODEL), single_buffer=True), tile, tile, tile, vec, vec],
        out_shape=[
            jax.ShapeDtypeStruct((D_FF, D_MODEL), F32),
            jax.ShapeDtypeStruct((t, D_MODEL), F32),
            jax.ShapeDtypeStruct((t, D_MODEL), BF16),
            jax.ShapeDtypeStruct((t, D_MODEL), BF16),
            jax.ShapeDtypeStruct((1, D_MODEL), F32),
            jax.ShapeDtypeStruct((1, D_MODEL), F32),
        ],
        compiler_params=_params(("arbitrary",)),
    )(*operands)


def _mixer_bwd(u, vg, q, k, va, dcat, cos, sin, v_ln_g, v_ln_b, w_spatial, bias_full, sinks, r, dz2b, dep=None):
    t = u.shape[0]
    n_chunks = t // CHUNK
    assert CHUNKS_PER_STEP == N_FF_BLOCKS

    def body(u_ref, vg_ref, q_ref, kc_ref, kp_ref, vc_ref, vp_ref, dcat_ref, cosc_ref, sinc_ref, cosp_ref, sinp_ref,
             g_ref, b_ref, w_ref, bias_ref, sink_ref, r_ref, dz2b_ref,
             dmain_ref, dkv_ref, gw2_ref, dg_ref, db_ref, dw_ref, dbs_ref, dsink_ref, dmix_acc, wcat, wcat_t):
        i = pl.program_id(0)
        left = _half_lane_masks(CHUNK)
        lane = lax.broadcasted_iota(jnp.int32, (CHUNK, LANES), 1)
        n_pairs = D_GMLP // LANES

        @pl.when(i == 0)
        def _():
            dg_ref[...] = jnp.zeros_like(dg_ref)
            db_ref[...] = jnp.zeros_like(db_ref)
            dw_ref[...] = jnp.zeros_like(dw_ref)
            dsink_ref[...] = jnp.zeros_like(dsink_ref)
            dmix_acc[...] = jnp.zeros_like(dmix_acc)
            gw2_ref[...] = jnp.zeros_like(gw2_ref)
            _store_spatial_weights(w_ref, wcat, wcat_t)

        n_qpairs = D_ATTN // LANES
        heads = range(N_HEADS)
        pair_cols = [slice(p * LANES, (p + 1) * LANES) for p in range(n_pairs)]
        sinks_h = [sink_ref[h] for h in heads]
        gain = g_ref[...]
        causal = _causal_mask()
        lane_row = lax.broadcasted_iota(jnp.int32, (1, LANES), 1)
        heads_per_group = N_HEADS // 2

        def group_grad_t(lhs_t, rhs_heads):
            parts = []
            for g in range(2):
                group = range(g * heads_per_group, (g + 1) * heads_per_group)
                lhs = jnp.concatenate([lhs_t[h * HEAD_DIM : (h + 1) * HEAD_DIM] for h in group], axis=1)
                parts.append(_dot(lhs, jnp.concatenate([rhs_heads[h] for h in group], axis=0)))
            return jnp.concatenate(parts, axis=0)

        for c in range(CHUNKS_PER_STEP):
            chunk = CHUNKS_PER_STEP * i + c
            rows = slice(c * CHUNK, (c + 1) * CHUNK)
            before = slice((c - 1) * CHUNK, c * CHUNK)

            k_prev = kp_ref[...] if c == 0 else kc_ref[before, :]
            v_prev = vp_ref[...] if c == 0 else vc_ref[before, :]
            k_var = _kv_variants(jnp.concatenate([k_prev, kc_ref[rows, :]], axis=0))
            v_var = _kv_variants(jnp.concatenate([v_prev, vc_ref[rows, :]], axis=0))
            q_pairs = [q_ref[rows, cols] for cols in pair_cols]
            do_all = dcat_ref[rows, D_GMLP:D_MODEL]
            do_pairs = [do_all[:, cols] for cols in pair_cols]
            scores = [_dot(q_pairs[h // 2], k_var[h // 4][h % 2], NT) for h in heads]
            dprobs = [_dot(do_pairs[h // 2], v_var[h // 4][h % 2], NT) for h in heads]
            q_t = q_ref[rows, :].astype(F32).T.astype(BF16)
            do_t = do_all.astype(F32).T.astype(BF16)

            ff_cols = slice(c * D_MODEL, (c + 1) * D_MODEL)
            relu_block = r_ref[:, ff_cols]
            gw2_ref[ff_cols, :] += _dot(relu_block * relu_block, dz2b_ref[...], TN)

            ug, dug_du = _gelu_and_grad(u_ref[rows, :])
            gv, dgv_dv = _gelu_and_grad(vg_ref[rows, :])
            xhat, rstd = _layer_norm_stats(gv)
            vgl = xhat * gain + b_ref[...]
            mixed = [_dot(wcat[p], _pair_stack(vgl[:, cols], left)) for p, cols in enumerate(pair_cols)]

            valid = _band_mask(chunk)
            masked = [jnp.where(valid, scores[h] * SCALE, NEG_INF) for h in heads]
            maxes = [jnp.maximum(jnp.max(masked[h], axis=1, keepdims=True), sinks_h[h]) for h in heads]
            exps = [jnp.exp(masked[h] - maxes[h]) for h in heads]
            exp_sinks = [jnp.exp(sinks_h[h] - maxes[h]) for h in heads]
            invs = [1.0 / (jnp.sum(exps[h], axis=1, keepdims=True) + exp_sinks[h]) for h in heads]
            probs = [exps[h] * invs[h] for h in heads]
            dsums = [jnp.sum(probs[h] * dprobs[h], axis=1, keepdims=True) for h in heads]
            ds_b = [(probs[h] * (dprobs[h] - dsums[h]) * SCALE).astype(BF16) for h in heads]
            probs_b = [probs[h].astype(BF16) for h in heads]

            dm_stacks = []
            for p, cols in enumerate(pair_cols):
                da = dcat_ref[rows, cols].astype(F32)
                dmain_ref[rows, cols] = (da * (mixed[p] + bias_ref[:, cols]) * dug_du[:, cols]).astype(BF16)
                dmixed = da * ug[:, cols]
                dmix_acc[:, cols] += dmixed
                dm_stacks.append(_pair_stack(dmixed, left))

            dq_all = jnp.concatenate(
                [_dot(ds_b[2 * p], k_var[p // 2][0]) + _dot(ds_b[2 * p + 1], k_var[p // 2][1]) for p in range(n_qpairs)], axis=1)
            dk2_t = group_grad_t(q_t, ds_b)
            dv2_t = group_grad_t(do_t, probs_b)

            for p, cols in enumerate(pair_cols):
                dw_pair = _dot(dm_stacks[p], vgl[:, cols].astype(BF16), NT)
                dw_ref[2 * p] += jnp.where(causal, dw_pair[:CHUNK], 0.0)
                dw_ref[2 * p + 1] += jnp.where(causal, dw_pair[CHUNK:], 0.0)
            dvgl = jnp.concatenate([_dot(wcat_t[p], dm_stacks[p]) for p in range(n_pairs)], axis=1)

            dsink_row = jnp.zeros((1, LANES), F32)
            for h in heads:
                d_sink = -jnp.sum(exp_sinks[h] * invs[h] * dsums[h], axis=0, keepdims=True)
                dsink_row = dsink_row + jnp.where(lane_row == h, d_sink, 0.0)
            dsink_ref[0:1, :] += dsink_row
            cos_c, sin_c = cosc_ref[rows, :], sinc_ref[rows, :]
            cos_p = cosp_ref[...] if c == 0 else cosc_ref[before, :]
            sin_p = sinp_ref[...] if c == 0 else sinc_ref[before, :]
            dmain_ref[rows, 2 * D_GMLP : D_MAIN] = _rope_transposed(dq_all, _lane_tile(cos_c, n_qpairs), _lane_tile(sin_c, n_qpairs)).astype(BF16)
            dk2 = dk2_t.T
            dv2 = dv2_t.T
            cur = pl.ds(pl.multiple_of(chunk * CHUNK, CHUNK), CHUNK)
            dkv_ref[cur, 0:D_KV] = _rope_transposed(dk2[CHUNK:], cos_c, sin_c)
            dkv_ref[cur, D_KV : 2 * D_KV] = dv2[CHUNK:]
            prev = pl.ds(pl.multiple_of(jnp.maximum(chunk - 1, 0) * CHUNK, CHUNK), CHUNK)
            dkv_ref[prev, 0:D_KV] += _rope_transposed(dk2[:CHUNK], cos_p, sin_p)
            dkv_ref[prev, D_KV : 2 * D_KV] += dv2[:CHUNK]

            dg_ref[...] += jnp.sum(dvgl * xhat, axis=0, keepdims=True)
            db_ref[...] += jnp.sum(dvgl, axis=0, keepdims=True)
            dgv = _layer_norm_bwd(dvgl * gain, xhat, rstd)
            dmain_ref[rows, D_GMLP : 2 * D_GMLP] = (dgv * dgv_dv).astype(BF16)

        @pl.when(i == n_chunks // CHUNKS_PER_STEP - 1)
        def _():
            tile = jnp.zeros((CHUNK, LANES), F32)
            for p, cols in enumerate(pair_cols):
                dm = dmix_acc[:, cols]
                sl = jnp.sum(jnp.where(left, dm, 0.0), axis=1, keepdims=True)
                sr = jnp.sum(jnp.where(left, 0.0, dm), axis=1, keepdims=True)
                tile = jnp.where(lane == 2 * p, sl, tile)
                tile = jnp.where(lane == 2 * p + 1, sr, tile)
            dbs_ref[...] = tile

    step = CHUNKS_PER_STEP * CHUNK
    in_specs = _chunk_specs() + [
        pl.BlockSpec((step, D_MODEL), _step_rows),
        pl.BlockSpec((step, LANES), _step_rows),
        pl.BlockSpec((step, LANES), _step_rows),
        pl.BlockSpec((CHUNK, LANES), _chunk_before_step),
        pl.BlockSpec((CHUNK, LANES), _chunk_before_step),
        _const_spec((1, D_GMLP)),
        _const_spec((1, D_GMLP)),
        _const_spec((N_HEADS, CHUNK, CHUNK)),
        _const_spec((CHUNK, D_GMLP)),
        pl.BlockSpec(memory_space=pltpu.SMEM),
        pl.BlockSpec((step, D_FF), _step_rows),
        pl.BlockSpec((step, D_MODEL), _step_rows),
    ]
    body, in_specs, operands = _after(
        dep, body, in_specs, [u, vg, q, k, k, va, va, dcat, cos, sin, cos, sin, v_ln_g, v_ln_b, w_spatial, bias_full, sinks, r, dz2b])
    return pl.pallas_call(
        body,
        name="mixer_bwd",
        grid=(n_chunks // CHUNKS_PER_STEP,),
        in_specs=in_specs,
        out_specs=[
            pl.BlockSpec((step, D_MAIN), _step_rows),
            _const_spec((t, 2 * D_KV)),
            _const_spec((D_FF, D_MODEL), single_buffer=True),
            _const_spec((1, D_GMLP)),
            _const_spec((1, D_GMLP)),
            _const_spec((N_HEADS, CHUNK, CHUNK)),
            _const_spec((CHUNK, LANES)),
            _const_spec((8, LANES)),
        ],
        out_shape=[
            jax.ShapeDtypeStruct((t, D_MAIN), BF16),
            jax.ShapeDtypeStruct((t, 2 * D_KV), F32),
            jax.ShapeDtypeStruct((D_FF, D_MODEL), F32),
            jax.ShapeDtypeStruct((1, D_GMLP), F32),
            jax.ShapeDtypeStruct((1, D_GMLP), F32),
            jax.ShapeDtypeStruct((N_HEADS, CHUNK, CHUNK), F32),
            jax.ShapeDtypeStruct((CHUNK, LANES), F32),
            jax.ShapeDtypeStruct((8, LANES), F32),
        ],
        scratch_shapes=[
            pltpu.VMEM((CHUNK, D_GMLP), F32),
            pltpu.VMEM((D_GMLP // LANES, CHUNK, 2 * CHUNK), BF16),
            pltpu.VMEM((D_GMLP // LANES, CHUNK, 2 * CHUNK), BF16),
        ],
        compiler_params=_params(("arbitrary",)),
    )(*operands)


def _grad_x(dh_main, dkv, dz1, w_in_t, dep=None):
    t = dz1.shape[0]

    def body(dm_ref, dkv_ref, dz1_ref, w_ref, gx_ref):
        acc = ALPHA * dz1_ref[...] + _dot(dm_ref[...], w_ref[0:D_MAIN, :])
        gx_ref[...] = acc + _dot(dkv_ref[...].astype(BF16), w_ref[D_MAIN:D_IN, :])

    body, in_specs, operands = _after(
        dep, body, [_row_spec(TM, D_MAIN), _row_spec(TM, 2 * D_KV), _row_spec(TM, D_MODEL), _const_spec((D_IN, D_MODEL))], [dh_main, dkv, dz1, w_in_t])
    return pl.pallas_call(
        body,
        name="grad_x",
        grid=(t // TM,),
        in_specs=in_specs,
        out_specs=_row_spec(TM, D_MODEL),
        out_shape=jax.ShapeDtypeStruct((t, D_MODEL), F32),
        compiler_params=_params(("parallel",)),
    )(*operands)


def _token_contraction(name, out_rows, tk, in_arrays, contributions, dep=None):
    t = in_arrays[0].shape[0]

    def body(*refs):
        out_ref = refs[-1]

        @pl.when(pl.program_id(0) == 0)
        def _():
            out_ref[...] = jnp.zeros_like(out_ref)

        for row0, a, b in contributions(*refs[:-1]):
            out_ref[row0 : row0 + a.shape[1], :] += _dot(a, b, TN)

    in_specs = [_row_spec(tk, a.shape[1]) for a in in_arrays]
    body, in_specs, operands = _after(dep, body, in_specs, in_arrays)
    return pl.pallas_call(
        body,
        name=name,
        grid=(t // tk,),
        in_specs=in_specs,
        out_specs=_const_spec((out_rows, D_MODEL), single_buffer=True),
        out_shape=jax.ShapeDtypeStruct((out_rows, D_MODEL), F32),
        compiler_params=_params(("arbitrary",)),
    )(*operands)


def _grad_w_out(cat, dz1b, dep=None):
    def contributions(cat_ref, dz1_ref):
        return [(0, cat_ref[...], dz1_ref[...])]

    return _token_contraction("grad_w_out", D_MODEL, TK, [cat, dz1b], contributions, dep)


ANY = pl.BlockSpec(memory_space=pl.ANY)


def _mesh_position():
    return lax.axis_index("x"), lax.axis_index("y"), lax.axis_index("c")


def _other_chips(x, y):
    return [(1 - x, y), (x, 1 - y), (1 - x, 1 - y)]


def _remote(src, dst, send_sem, recv_sem, device):
    return pltpu.make_async_remote_copy(src_ref=src, dst_ref=dst, send_sem=send_sem, recv_sem=recv_sem, device_id=device, device_id_type=MESH)


def _rows(ref, start, size):
    return ref.at[pl.ds(start, size), :]


def _rope_tables_and_casts(pos_row, inv_freq_row, shards, dep=None):
    t = pos_row.shape[1]
    steps = t // TM
    n = len(shards)

    def body(pos_ref, f_ref, *rest):
        f32_refs, (cos_ref, sin_ref), bf16_refs = rest[:n], rest[n : n + 2], rest[n + 2 :]
        for src, dst in zip(f32_refs, bf16_refs):
            dst[...] = src[...].astype(BF16)
        pos_rows = jnp.broadcast_to(pos_ref[...].astype(F32), (LANES, TM)).T
        ang = pos_rows * f_ref[...]
        cos_ref[...] = jnp.cos(ang)
        sin_ref[...] = jnp.sin(ang)

    shard_specs = [_row_spec(s.shape[0] // steps, s.shape[1]) for s in shards]
    body, in_specs, operands = _after(
        dep, body, [pl.BlockSpec((1, TM), lambda i: (0, i)), _const_spec((1, LANES))] + shard_specs, [pos_row, inv_freq_row, *shards])
    outs = pl.pallas_call(
        body,
        name="rope_tables_and_casts",
        grid=(steps,),
        in_specs=in_specs,
        out_specs=[_row_spec(TM, LANES), _row_spec(TM, LANES)] + shard_specs,
        out_shape=[jax.ShapeDtypeStruct((t, LANES), F32)] * 2 + [jax.ShapeDtypeStruct(s.shape, BF16) for s in shards],
        compiler_params=_params(("parallel",)),
    )(*operands)
    return outs[0], outs[1], list(outs[2:])


def _grad_w_in_t_and_small_all_reduce(dh_main, dkv, x, slab, dep=None):
    t = x.shape[0]
    steps = t // TK
    rows = slab.shape[0]
    part = rows // 8

    def body(dm_ref, dkv_ref, x_ref, slab_ref, grad_ref, sum_ref, landing, reduced, gathered, send_sems, recv_sems):
        k = pl.program_id(0)
        x_, y_, c_ = _mesh_position()
        me = 4 * x_ + 2 * y_ + c_
        flips = [(f >> 2, (f >> 1) & 1, f & 1) for f in range(1, 8)]

        def peer(flip):
            fx, fy, fc = flip
            return (1 - x_ if fx else x_, 1 - y_ if fy else y_, 1 - c_ if fc else c_)

        def part_of(ref, device):
            return ref.at[pl.ds(pl.multiple_of(device * part, 8), part), :]

        def scatter_copies():
            out = []
            for kk, flip in enumerate(flips):
                px, py, pc = peer(flip)
                them = 4 * px + 2 * py + pc
                send = _remote(part_of(slab_ref, them), landing.at[me], send_sems.at[kk], recv_sems.at[kk], (px, py, pc))
                recv = _remote(landing.at[them], landing.at[them], send_sems.at[kk], recv_sems.at[kk], (px, py, pc))
                out.append((send, recv))
            return out

        def gather_copies():
            out = []
            for kk, flip in enumerate(flips):
                px, py, pc = peer(flip)
                them = 4 * px + 2 * py + pc
                send = _remote(reduced, part_of(gathered, me), send_sems.at[7 + kk], recv_sems.at[7 + kk], (px, py, pc))
                recv = _remote(part_of(gathered, them), part_of(gathered, them), send_sems.at[7 + kk], recv_sems.at[7 + kk], (px, py, pc))
                out.append((send, recv))
            return out

        @pl.when(k == 0)
        def _():
            grad_ref[...] = jnp.zeros_like(grad_ref)
            for send, _ in scatter_copies():
                send.start()
            landing[me] = part_of(slab_ref, me)[...]

        @pl.when(k == steps // 2)
        def _():
            for _, recv in scatter_copies():
                recv.wait_recv()
            total = landing[0]
            for s in range(1, 8):
                total = total + landing[s]
            reduced[...] = total
            part_of(gathered, me)[...] = total
            for send, _ in gather_copies():
                send.start()

        xb = x_ref[...].astype(BF16)
        grad_ref[0:D_MAIN, :] += _dot(dm_ref[...], xb, TN)
        grad_ref[D_MAIN:D_IN, :] += _dot(dkv_ref[...].astype(BF16), xb, TN)

        @pl.when(k == steps - 1)
        def _():
            for send, recv in gather_copies():
                recv.wait_recv()
                send.wait_send()
            for send, _ in scatter_copies():
                send.wait_send()
            sum_ref[...] = gathered[...]

    body, in_specs, operands = _after(
        dep, body, [_row_spec(TK, D_MAIN), _row_spec(TK, 2 * D_KV), _row_spec(TK, D_MODEL), _const_spec(slab.shape)], [dh_main, dkv, x, slab])
    return pl.pallas_call(
        body,
        name="grad_w_in_and_small_all_reduce",
        grid=(steps,),
        in_specs=in_specs,
        out_specs=[_const_spec((D_IN, D_MODEL), single_buffer=True), _const_spec(slab.shape)],
        out_shape=[jax.ShapeDtypeStruct((D_IN, D_MODEL), F32), jax.ShapeDtypeStruct(slab.shape, slab.dtype)],
        scratch_shapes=[
            pltpu.VMEM((8, part, LANES), F32),
            pltpu.VMEM((part, LANES), F32),
            pltpu.VMEM(slab.shape, F32),
            pltpu.SemaphoreType.DMA((14,)),
            pltpu.SemaphoreType.DMA((14,)),
        ],
        compiler_params=_params(("arbitrary",)),
    )(*operands)


HBM = pl.BlockSpec(memory_space=pltpu.HBM)
SEM = pl.BlockSpec(memory_space=pltpu.SEMAPHORE)
DATAFLOW = pltpu.SideEffectType.DATAFLOW_SIDE_EFFECTING
TOKEN = jax.ShapeDtypeStruct((8, LANES), F32)


def _plan_copies(bufs, plan, send_sems, recv_sems):
    out = []
    for i, (src, src_row, dst, dst_row, recv_row, rows, device) in enumerate(plan):
        send = _remote(_rows(bufs[src], src_row, rows), _rows(bufs[dst], dst_row, rows), send_sems.at[i], recv_sems.at[i], device)
        landed = _rows(bufs[dst], recv_row, rows)
        recv = _remote(landed, landed, send_sems.at[i], recv_sems.at[i], device)
        out.append((send, recv))
    return out


def _split_call(name, bufs, wait=None, start=None, after=None):
    n = len(bufs)
    n_in = n + (2 if wait else 0) + (1 if after is not None else 0)
    n_start = len(start(0, 0, 0)) if start else 0

    def body(*refs):
        ins = refs[:n]
        x, y, c = _mesh_position()
        if wait:
            for send, recv in _plan_copies(ins, wait[0](x, y, c), refs[n], refs[n + 1]):
                recv.wait_recv()
                send.wait_send()
        if start:
            for send, _ in _plan_copies(ins, start(x, y, c), refs[n_in + n + 1], refs[n_in + n + 2]):
                send.start()
        token = refs[n_in + n]
        token[...] = jnp.zeros_like(token)

    operands = [pltpu.with_memory_space_constraint(b, pltpu.HBM) for b in bufs]
    in_specs = [HBM] * n
    if wait:
        operands += [wait[1], wait[2]]
        in_specs += [SEM, SEM]
    if after is not None:
        operands.append(after)
        in_specs.append(ANY)
    out_shape = [pltpu.HBM(b.shape, b.dtype) for b in bufs] + [TOKEN]
    out_specs = [HBM] * n + [pl.BlockSpec(memory_space=pltpu.VMEM)]
    if start:
        out_shape += [pltpu.SemaphoreType.DMA((n_start,)), pltpu.SemaphoreType.DMA((n_start,))]
        out_specs += [SEM, SEM]
    outs = pl.pallas_call(
        body,
        name=name,
        in_specs=in_specs,
        out_specs=out_specs,
        out_shape=out_shape,
        input_output_aliases={i: i for i in range(n)},
        compiler_params=pltpu.CompilerParams(has_side_effects=DATAFLOW),
    )(*operands)
    return (list(outs[:n]), outs[n]) + tuple(outs[n + 1 :])


def _direct_gather_plans(shard_rows):
    n = len(shard_rows)

    def direct(x, y, c):
        me = 2 * x + y
        plan = []
        for w, rows in enumerate(shard_rows):
            half = rows // 2
            for px, py in _other_chips(x, y):
                plan.append((w, c * half, n + w, me * rows + c * half, (2 * px + py) * rows + c * half, half, (px, py, c)))
            plan.append((w, 0, n + w, me * rows, me * rows, rows, (x, y, 1 - c)))
        return plan

    def passed_on(x, y, c):
        plan = []
        for w, rows in enumerate(shard_rows):
            half = rows // 2
            for px, py in _other_chips(x, y):
                row = (2 * px + py) * rows
                plan.append((n + w, row + c * half, n + w, row + c * half, row + (1 - c) * half, half, (x, y, 1 - c)))
        return plan

    return direct, passed_on


def _gather_plans(shard_rows):
    n = len(shard_rows)

    def neighbours(x, y):
        return ((1 - x, y), (x, 1 - y))

    def direct(x, y, c):
        me = 2 * x + y
        plan = []
        for w, rows in enumerate(shard_rows):
            half = rows // 2
            for px, py in neighbours(x, y):
                plan.append((w, c * half, n + w, me * rows + c * half, (2 * px + py) * rows + c * half, half, (px, py, c)))
            plan.append((w, 0, n + w, me * rows, me * rows, rows, (x, y, 1 - c)))
        return plan

    def passed_on(x, y, c):
        (xn, yn), diagonal = neighbours(x, y), 2 * (1 - x) + (1 - y)
        relayed = (1 - c) * (2 * xn[0] + xn[1]) + c * (2 * yn[0] + yn[1])
        target = (x * (1 - c) + (1 - x) * c, (1 - y) * (1 - c) + y * c, c)
        plan = []
        for w, rows in enumerate(shard_rows):
            half = rows // 2
            for px, py in (xn, yn):
                row = (2 * px + py) * rows
                plan.append((n + w, row + c * half, n + w, row + c * half, row + (1 - c) * half, half, (x, y, 1 - c)))
            plan.append((n + w, relayed * rows + c * half, n + w, relayed * rows + c * half, diagonal * rows + c * half, half, target))
        return plan

    def diagonal_passed_on(x, y, c):
        plan = []
        for w, rows in enumerate(shard_rows):
            half = rows // 2
            row = (2 * (1 - x) + (1 - y)) * rows
            plan.append((n + w, row + c * half, n + w, row + c * half, row + (1 - c) * half, half, (x, y, 1 - c)))
        return plan

    return direct, passed_on, diagonal_passed_on


def _swap_plan(block_rows):
    n = len(block_rows)

    def plan_fn(x, y, c):
        plan = []
        for w, rows in enumerate(block_rows):
            half = rows // 2
            for j in range(N_CHIPS):
                plan.append((w, j * rows + (1 - c) * half, n + w, j * half, j * half, half, (x, y, 1 - c)))
        return plan

    return plan_fn


def _exchange_plan(halves):
    n = len(halves)

    def plan_fn(x, y, c):
        plan = []
        for w, half in enumerate(halves):
            for kk, (px, py) in enumerate(_other_chips(x, y)):
                plan.append((w, (2 * px + py) * half, n + w, kk * half, kk * half, half, (px, py, c)))
        return plan

    return plan_fn


def _sibling_plan(shard_rows):
    def plan_fn(x, y, c):
        return [(w, c * (rows // 2), w, c * (rows // 2), (1 - c) * (rows // 2), rows // 2, (x, y, 1 - c)) for w, rows in enumerate(shard_rows)]

    return plan_fn


def _shifted(plan_fn, first):
    return lambda x, y, c: [(src + first, a, dst + first, b, r, n, dev) for src, a, dst, b, r, n, dev in plan_fn(x, y, c)]


def _landing(rows, cols, dtype):
    return lax.empty((rows, cols), dtype)


def _row_tile(rows, cap=512):
    best = 8
    for cand in range(8, cap + 1, 8):
        if rows % cand == 0:
            best = cand
    return best


def _pair_sum(name, grad, theirs, pos):
    half = theirs.shape[0] // N_CHIPS
    cols = theirs.shape[1]
    tile = _row_tile(half)
    steps = half // tile

    def body(pos_ref, g_ref, t_ref, p_ref, own_ref):
        total = g_ref[...] + t_ref[...]
        p_ref[...] = total.astype(BF16)

        @pl.when(pl.program_id(1) == pos_ref[1])
        def _():
            own_ref[...] = total

    return pl.pallas_call(
        body,
        name=name,
        grid_spec=pltpu.PrefetchScalarGridSpec(
            num_scalar_prefetch=1,
            grid=(steps, N_CHIPS),
            in_specs=[
                pl.BlockSpec((tile, cols), lambda i, j, pos: ((2 * j + pos[0]) * steps + i, 0)),
                pl.BlockSpec((tile, cols), lambda i, j, pos: (j * steps + i, 0)),
            ],
            out_specs=[
                pl.BlockSpec((tile, cols), lambda i, j, pos: (j * steps + i, 0)),
                pl.BlockSpec((tile, cols), lambda i, j, pos: (i, 0)),
            ],
        ),
        out_shape=[jax.ShapeDtypeStruct((N_CHIPS * half, cols), BF16), jax.ShapeDtypeStruct((half, cols), F32)],
        compiler_params=_params(("parallel", "arbitrary")),
    )(pos, grad, theirs)


def _adamw_update(w, g, m, v):
    nm = ADAM_B1 * m + (1.0 - ADAM_B1) * g
    nv = ADAM_B2 * v + (1.0 - ADAM_B2) * (g * g)
    m_hat = nm / (1.0 - ADAM_B1**ADAM_STEP)
    v_hat = nv / (1.0 - ADAM_B2**ADAM_STEP)
    return -ADAM_LR * (m_hat / (jnp.sqrt(v_hat) + ADAM_EPS) + ADAM_WD * w), nm, nv


def _chip_sum(name, own, landed, pos):
    half, cols = own.shape
    tile = _row_tile(half)
    steps = half // tile

    def body(pos_ref, own_ref, l0, l1, l2, o_ref):
        o_ref[...] = ((own_ref[...] + l0[...].astype(F32)) + l1[...].astype(F32)) + l2[...].astype(F32)

    landed_specs = [pl.BlockSpec((tile, cols), lambda i, pos, _k=k: (_k * steps + i, 0)) for k in range(N_CHIPS - 1)]
    return pl.pallas_call(
        body,
        name=name,
        grid_spec=pltpu.PrefetchScalarGridSpec(
            num_scalar_prefetch=1,
            grid=(steps,),
            in_specs=[pl.BlockSpec((tile, cols), lambda i, pos: (i, 0))] + landed_specs,
            out_specs=pl.BlockSpec((tile, cols), lambda i, pos: (pos[0] * steps + i, 0)),
        ),
        out_shape=jax.ShapeDtypeStruct((2 * half, cols), F32),
        compiler_params=_params(("parallel",)),
    )(pos, own, landed, landed, landed)


def _adamw(name, w, g, m, v):
    rows, cols = w.shape
    tile = rows if rows * cols <= 256 * 1024 else _row_tile(rows)

    def body(w_ref, g_ref, m_ref, v_ref, g_out_ref, d_ref, nm_ref, nv_ref):
        g = g_ref[...]
        g_out_ref[...] = g
        d_ref[...], nm_ref[...], nv_ref[...] = _adamw_update(w_ref[...], g, m_ref[...], v_ref[...])

    spec = _row_spec(tile, cols)
    return pl.pallas_call(
        body,
        name=name,
        grid=(rows // tile,),
        in_specs=[spec] * 4,
        out_specs=[spec] * 4,
        out_shape=[jax.ShapeDtypeStruct((rows, cols), F32)] * 4,
        compiler_params=_params(("parallel",)),
    )(w, g, m, v)


_SMALL = (
    ("v_ln_g", (D_GMLP,), 8),
    ("v_ln_b", (D_GMLP,), 8),
    ("w_spatial", (N_HEADS, CHUNK, CHUNK), 1024),
    ("b_spatial", (N_HEADS, CHUNK), 8),
    ("sinks", (N_HEADS,), 8),
    ("ln1_g", (D_MODEL,), 8),
    ("ln1_b", (D_MODEL,), 8),
    ("ln2_g", (D_MODEL,), 8),
    ("ln2_b", (D_MODEL,), 8),
    ("squared_error", (D_MODEL,), 8),
)
N_SMALL_PARAMS = len(_SMALL) - 1


def _pack_small(values):
    parts = []
    for (name, shape, rows), val in zip(_SMALL, values, strict=True):
        flat = val.reshape(-1).astype(F32)
        parts.append(jnp.pad(flat, (0, rows * LANES - flat.shape[0])).reshape(rows, LANES))
    parts.append(jnp.zeros((SMALL_ROWS - sum(rows for _, _, rows in _SMALL), LANES), F32))
    return jnp.concatenate(parts, axis=0)


def _adamw_small(g_slab, params, first, second):
    n = N_SMALL_PARAMS

    def pieces(shape):
        if len(shape) == 3:
            return [((0, h), h * shape[1], shape[1], shape[2]) for h in range(shape[0])]
        if len(shape) == 2:
            return [((0,), 0, shape[0], shape[1])]
        if shape[0] >= LANES:
            return [((slice(None), slice(r * LANES, (r + 1) * LANES)), r, 1, LANES) for r in range(shape[0] // LANES)]
        return [((slice(None), slice(0, shape[0])), 0, 1, shape[0])]

    def body(*refs):
        g_ref = refs[0]
        w_refs, m_refs, v_refs = refs[1 : 1 + n], refs[1 + n : 1 + 2 * n], refs[1 + 2 * n : 1 + 3 * n]
        outs = refs[1 + 3 * n :]
        row0 = 0
        for idx, (_, shape, rows) in enumerate(_SMALL[:n]):
            for where, first_row, n_rows, lanes in pieces(shape):
                g = g_ref[row0 + first_row : row0 + first_row + n_rows, 0:lanes]
                delta, nm, nv = _adamw_update(w_refs[idx][where], g, m_refs[idx][where], v_refs[idx][where])
                for group, val in enumerate((g, delta, nm, nv)):
                    outs[group * n + idx][where] = val
            row0 += rows

    vmem = pl.BlockSpec(memory_space=pltpu.VMEM)
    shapes = [jax.ShapeDtypeStruct(p.shape, F32) for p in params]
    outs = pl.pallas_call(
        body,
        name="adamw_small",
        in_specs=[vmem] * (1 + 3 * n),
        out_specs=[vmem] * (4 * n),
        out_shape=shapes * 4,
        compiler_params=_params(),
    )(g_slab, *params, *first, *second)
    return [list(outs[group * n : (group + 1) * n]) for group in range(4)]


def kernel(x, positions, w_in, v_ln_g, v_ln_b, w_spatial, b_spatial, sinks, w_out, ln1_g, ln1_b, w_ff1, w_ff2, ln2_g, ln2_b, loss_target, m_w_in, m_v_ln_g, m_v_ln_b, m_w_spatial, m_b_spatial, m_sinks, m_w_out, m_ln1_g, m_ln1_b, m_w_ff1, m_w_ff2, m_ln2_g, m_ln2_b, v_w_in, v_v_ln_g, v_v_ln_b, v_w_spatial, v_b_spatial, v_sinks, v_w_out, v_ln1_g, v_ln1_b, v_w_ff1, v_w_ff2, v_ln2_g, v_ln2_b):
    t = x.shape[1]
    x2 = x.reshape(t, D_MODEL)
    target = loss_target.reshape(t, D_MODEL)

    w_in_shard = w_in[0].T.astype(BF16)
    in_direct, in_pass = _direct_gather_plans([w_in_shard.shape[0]])
    in_bufs, in_started, in_send, in_recv = _split_call(
        "gather_w_in_start", [w_in_shard, _landing(N_CHIPS * w_in_shard.shape[0], D_MODEL, BF16)], start=in_direct)
    inv_freq = ROPE_THETA ** (-jnp.arange(0, HEAD_DIM, 2, dtype=F32) / HEAD_DIM)
    cos, sin, later = _rope_tables_and_casts(
        positions, jnp.tile(inv_freq, LANES // (HEAD_DIM // 2)).reshape(1, LANES), [w_out[0], w_ff1[0], w_ff2[0]], dep=in_started)
    later_rows = [s.shape[0] for s in later]
    direct_plan, pass_plan, diagonal_plan = _gather_plans(later_rows)
    bufs, started, direct_send, direct_recv = _split_call(
        "gather_start", later + [_landing(N_CHIPS * r, D_MODEL, BF16) for r in later_rows], start=direct_plan, after=cos)
    in_bufs, in_passing, in_pass_send, in_pass_recv = _split_call(
        "gather_w_in_pass", in_bufs, wait=(in_direct, in_send, in_recv), start=in_pass, after=started)
    in_bufs, _ = _split_call("gather_w_in_end", in_bufs, wait=(in_pass, in_pass_send, in_pass_recv), after=in_passing)
    w_in_t = in_bufs[1]

    u, vg, q, k, va = _in_proj(x2, w_in_t, cos, sin)
    bias_full = jnp.repeat(b_spatial[0].T, HEAD_DIM, axis=1)
    sink_vec = sinks.reshape(N_HEADS)
    bufs, passing, pass_send, pass_recv = _split_call(
        "gather_pass", bufs, wait=(direct_plan, direct_send, direct_recv), start=pass_plan, after=u)
    cat = _mixer_fwd(u, vg, q, k, va, v_ln_g, v_ln_b, w_spatial[0], bias_full, sink_vec, dep=passing)
    bufs, passing, diag_send, diag_recv = _split_call(
        "gather_pass_diagonal", bufs, wait=(pass_plan, pass_send, pass_recv), start=diagonal_plan, after=cat)
    bufs, _ = _split_call("gather_end", bufs, wait=(diagonal_plan, diag_send, diag_recv), after=passing)
    w_out_all = bufs[3]
    w1_all = bufs[4].reshape(N_FF_BLOCKS, D_MODEL, D_MODEL)
    w2_all = bufs[5].reshape(N_FF_BLOCKS, D_MODEL, D_MODEL)
    xhat1, rstd1, x1b, r, dz2, dz2b, d_ln2_g, d_ln2_b, sq_err = _ffn_fwd_loss(
        cat, x2, w_out_all, ln1_g, ln1_b, w1_all, w2_all, ln2_g, ln2_b, target)

    pos = jnp.stack([lax.axis_index("c"), 2 * lax.axis_index("x") + lax.axis_index("y")]).astype(jnp.int32)
    half_landing = lambda g: _landing(g.shape[0] // 2, D_MODEL, F32)
    ff_swap_plan = _swap_plan([D_FF // N_CHIPS])
    ff_exchange_plan = _exchange_plan([D_FF // N_CHIPS // 2])
    exchange_landing = lambda p: _landing(3 * p.shape[0] // N_CHIPS, D_MODEL, BF16)
    g_ff1_local, dz1, dz1b, dcat, d_ln1_g, d_ln1_b = _ffn_bwd_ln1(dz2, r, x1b, xhat1, rstd1, ln1_g, w1_all, w2_all, w_out_all)
    ff1_bufs, swapping1, swap1_send, swap1_recv = _split_call("ff1_swap_start", [g_ff1_local, half_landing(g_ff1_local)], start=ff_swap_plan)
    g_out_local = _grad_w_out(cat, dz1b, dep=swapping1)
    ff1_bufs, _ = _split_call("ff1_swap_wait", ff1_bufs, wait=(ff_swap_plan, swap1_send, swap1_recv), after=g_out_local)
    ff1_sum, ff1_own = _pair_sum("grad_pair_sum_w_ff1", ff1_bufs[0], ff1_bufs[1], pos)
    ff1_ex, exchanging1, ex1_send, ex1_recv = _split_call(
        "ff1_exchange_start", [ff1_sum, exchange_landing(ff1_sum)], start=ff_exchange_plan)
    dh_main, dkv, g_ff2_local, d_v_ln_g, d_v_ln_b, d_w_spatial, d_b_spatial_t, d_sinks = _mixer_bwd(
        u, vg, q, k, va, dcat, cos, sin, v_ln_g, v_ln_b, w_spatial[0], bias_full, sink_vec, r, dz2b, dep=exchanging1)
    ff2_bufs, swapping2, swap2_send, swap2_recv = _split_call("ff2_swap_start", [g_ff2_local, half_landing(g_ff2_local)], start=ff_swap_plan)
    g_in_local, small_g = _grad_w_in_t_and_small_all_reduce(dh_main, dkv, x2, _pack_small(
        [d_v_ln_g, d_v_ln_b, d_w_spatial, d_b_spatial_t[:, :N_HEADS].T, d_sinks[0, :N_HEADS], d_ln1_g, d_ln1_b, d_ln2_g, d_ln2_b, sq_err]),
        dep=swapping2)
    sq_row = sum(rows for _, _, rows in _SMALL[:N_SMALL_PARAMS])
    loss = 0.5 * jnp.sum(small_g[sq_row : sq_row + _SMALL[N_SMALL_PARAMS][2]]) / D_MODEL
    ff2_bufs, _ = _split_call("ff2_swap_wait", ff2_bufs, wait=(ff_swap_plan, swap2_send, swap2_recv), after=g_in_local)
    ff2_sum, ff2_own = _pair_sum("grad_pair_sum_w_ff2", ff2_bufs[0], ff2_bufs[1], pos)
    ff2_ex, exchanging2, ex2_send, ex2_recv = _split_call(
        "ff2_exchange_start", [ff2_sum, exchange_landing(ff2_sum)], start=ff_exchange_plan)

    small = [g_in_local, g_out_local]
    small_swap_plan = _swap_plan([g.shape[0] // N_CHIPS for g in small])
    swap_bufs, small_swapping, ss_send, ss_recv = _split_call(
        "small_swap_start", small + [half_landing(g) for g in small], start=small_swap_plan, after=exchanging2)
    grad_x_flat = _grad_x(dh_main, dkv, dz1, w_in_t, dep=small_swapping)
    grad_x = grad_x_flat.reshape(1, t, D_MODEL)
    swap_bufs, _ = _split_call("small_swap_wait", swap_bufs, wait=(small_swap_plan, ss_send, ss_recv), after=grad_x_flat)
    pair_sums = [_pair_sum("grad_pair_sum_" + nm, g, th, pos) for nm, g, th in zip(["w_in", "w_out"], swap_bufs[:2], swap_bufs[2:])]
    small_plan = _exchange_plan([p.shape[0] // N_CHIPS for p, _ in pair_sums])
    small_bufs, small_exchanging, sm_send, sm_recv = _split_call(
        "small_exchange_start", [p for p, _ in pair_sums] + [exchange_landing(p) for p, _ in pair_sums], start=small_plan)

    ff1_ex, _ = _split_call("ff1_exchange_wait", ff1_ex, wait=(ff_exchange_plan, ex1_send, ex1_recv), after=small_exchanging)
    ff_pair_plan = _sibling_plan([D_FF // N_CHIPS])
    half_ff1 = _chip_sum("grad_chip_sum_w_ff1", ff1_own, ff1_ex[1], pos)
    (half_ff1, *ff2_ex), _, g1_send, g1_recv = _split_call(
        "ff2_exchange_wait_ff1_pair_start", [half_ff1] + ff2_ex, wait=(_shifted(ff_exchange_plan, 1), ex2_send, ex2_recv), start=ff_pair_plan)
    half_ff2 = _chip_sum("grad_chip_sum_w_ff2", ff2_own, ff2_ex[1], pos)
    (half_ff2, g_w_ff1), _, g2_send, g2_recv = _split_call(
        "ff1_pair_wait_ff2_pair_start", [half_ff2, half_ff1], wait=(_shifted(ff_pair_plan, 1), g1_send, g1_recv), start=ff_pair_plan)
    g_w_ff1, d_w_ff1, nm_w_ff1, nv_w_ff1 = _adamw("adamw_w_ff1", w_ff1[0], g_w_ff1, m_w_ff1[0], v_w_ff1[0])
    small_bufs, _ = _split_call("small_exchange_wait", small_bufs, wait=(small_plan, sm_send, sm_recv), after=nv_w_ff1)
    shards = [_chip_sum("grad_chip_sum_" + nm, own, ld, pos) for nm, (_, own), ld in zip(["w_in", "w_out"], pair_sums, small_bufs[2:])]
    small_pair_plan = _sibling_plan([s.shape[0] for s in shards])
    (*shards, g_w_ff2), _, g3_send, g3_recv = _split_call(
        "ff2_pair_wait_small_pair_start", shards + [half_ff2], wait=(_shifted(ff_pair_plan, 2), g2_send, g2_recv), start=small_pair_plan)
    g_w_ff2, d_w_ff2, nm_w_ff2, nv_w_ff2 = _adamw("adamw_w_ff2", w_ff2[0], g_w_ff2, m_w_ff2[0], v_w_ff2[0])
    (g_w_in_t, g_w_out), _ = _split_call("small_pair_wait", shards, wait=(small_pair_plan, g3_send, g3_recv), after=nv_w_ff2)
    g_w_in, d_w_in, nm_w_in, nv_w_in = (a.T for a in _adamw("adamw_w_in", w_in[0].T, g_w_in_t, m_w_in[0].T, v_w_in[0].T))
    g_w_out, d_w_out, nm_w_out, nv_w_out = _adamw("adamw_w_out", w_out[0], g_w_out, m_w_out[0], v_w_out[0])
    small_grads, small_d, small_nm, small_nv = _adamw_small(
        small_g,
        [v_ln_g, v_ln_b, w_spatial, b_spatial, sinks, ln1_g, ln1_b, ln2_g, ln2_b],
        [m_v_ln_g, m_v_ln_b, m_w_spatial, m_b_spatial, m_sinks, m_ln1_g, m_ln1_b, m_ln2_g, m_ln2_b],
        [v_v_ln_g, v_v_ln_b, v_w_spatial, v_b_spatial, v_sinks, v_ln1_g, v_ln1_b, v_ln2_g, v_ln2_b])

    def with_big(small, w_in_v, w_out_v, w_ff1_v, w_ff2_v):
        g_vg, g_vb, g_ws, g_bs, g_sk, g_1g, g_1b, g_2g, g_2b = small
        return [w_in_v[None], g_vg, g_vb, g_ws, g_bs, g_sk, w_out_v[None], g_1g, g_1b, w_ff1_v[None], w_ff2_v[None], g_2g, g_2b]

    return (
        loss,
        grad_x,
        *with_big(small_grads, g_w_in, g_w_out, g_w_ff1, g_w_ff2),
        *with_big(small_d, d_w_in, d_w_out, d_w_ff1, d_w_ff2),
        *with_big(small_nm, nm_w_in, nm_w_out, nm_w_ff1, nm_w_ff2),
        *with_big(small_nv, nv_w_in, nv_w_out, nv_w_ff1, nv_w_ff2),
    )
```

```python
import math

import jax
import jax.numpy as jnp
from jax import lax
from jax.experimental import pallas as pl
from jax.experimental.pallas import tpu as pltpu

F32 = jnp.float32
BF16 = jnp.bfloat16

D_MODEL = 1024
HEAD_DIM = 64
D_GMLP = 512
D_ATTN = 512
D_KV = 128
D_IN = 2 * D_GMLP + D_ATTN + 2 * D_KV
D_MAIN = 2 * D_GMLP + D_ATTN
N_HEADS = 8
CHUNK = 128
CHUNKS_PER_STEP = 4
ROPE_THETA = 10000.0
D_FF = 4 * D_MODEL
N_FF_BLOCKS = 4
LN_EPS = 1e-5
ALPHA = (2.0 * 1) ** 0.25
NEG_INF = -1e30
SCALE = 1.0 / math.sqrt(HEAD_DIM)

ADAM_LR = 0.001
ADAM_B1 = 0.9
ADAM_B2 = 0.999
ADAM_EPS = 1e-08
ADAM_WD = 0.01
ADAM_STEP = 10

N_CHIPS = 4
LANES = 128
V7X_VMEM_BYTES = 64 * 1024 * 1024
VMEM_LIMIT = V7X_VMEM_BYTES - 8 * 1024 * 1024
TM = 512
TM_FFN = 256
TM_FFN_FWD = 512
FFN_PART = 256
TK = 1024
SMALL_ROWS = 1152
MESH = pl.DeviceIdType.MESH

NT = (((1,), (1,)), ((), ()))
TN = (((0,), (0,)), ((), ()))


def _dot(a, b, dims=None):
    if dims is None:
        return jnp.dot(a, b, preferred_element_type=F32)
    return lax.dot_general(a, b, dims, preferred_element_type=F32)


def _params(semantics=None):
    return pltpu.CompilerParams(dimension_semantics=semantics, vmem_limit_bytes=VMEM_LIMIT)


def _const_spec(shape, single_buffer=False):
    zeros = (0,) * len(shape)
    if single_buffer:
        return pl.BlockSpec(shape, lambda *_: zeros, pipeline_mode=pl.Buffered(1))
    return pl.BlockSpec(shape, lambda *_: zeros)


def _row_spec(rows, cols):
    return pl.BlockSpec((rows, cols), lambda i: (i, 0))


def _after(dep, body, in_specs, operands):
    if dep is None:
        return body, list(in_specs), list(operands)
    return (lambda dep_ref, *refs: body(*refs)), [pl.BlockSpec(memory_space=pl.ANY)] + list(in_specs), [dep] + list(operands)


def _gelu(x):
    k = math.sqrt(2.0 / math.pi)
    return 0.5 * x * (1.0 + jnp.tanh(k * (x + 0.044715 * (x * x * x))))


def _gelu_and_grad(x):
    k = math.sqrt(2.0 / math.pi)
    x2 = x * x
    t = jnp.tanh(k * (x + 0.044715 * (x2 * x)))
    g = 0.5 * x * (1.0 + t)
    dg = 0.5 * (1.0 + t) + 0.5 * x * (1.0 - t * t) * (k * (1.0 + 3.0 * 0.044715 * x2))
    return g, dg


def _layer_norm_stats(z):
    mu = jnp.mean(z, axis=-1, keepdims=True)
    zc = z - mu
    var = jnp.mean(zc * zc, axis=-1, keepdims=True)
    rstd = lax.rsqrt(var + LN_EPS)
    return zc * rstd, rstd


def _layer_norm_bwd(dxhat, xhat, rstd):
    m1 = jnp.mean(dxhat, axis=-1, keepdims=True)
    m2 = jnp.mean(dxhat * xhat, axis=-1, keepdims=True)
    return rstd * (dxhat - m1 - xhat * m2)


def _rotate_half(t):
    n = t.shape[1]
    lane = lax.broadcasted_iota(jnp.int32, t.shape, 1)
    first = (lane & (HEAD_DIM // 2)) == 0
    return jnp.where(first, -pltpu.roll(t, n - HEAD_DIM // 2, 1), pltpu.roll(t, HEAD_DIM // 2, 1))


def _rope(t, cos, sin):
    return t * cos + _rotate_half(t) * sin


def _rope_transposed(g, cos, sin):
    return g * cos - _rotate_half(g * sin)


def _lane_tile(a, reps):
    return jnp.tile(a, (1, reps)) if reps > 1 else a


def _in_proj(x, w_in_t, cos, sin, dep=None):
    t = x.shape[0]

    def body(x_ref, w_ref, cos_ref, sin_ref, u_ref, vg_ref, q_ref, k_ref, va_ref):
        xb = x_ref[...].astype(BF16)
        u_ref[...] = _dot(xb, w_ref[0:D_GMLP, :], NT)
        vg_ref[...] = _dot(xb, w_ref[D_GMLP : 2 * D_GMLP, :], NT)
        q = _dot(xb, w_ref[2 * D_GMLP : D_MAIN, :], NT)
        k = _dot(xb, w_ref[D_MAIN : D_MAIN + D_KV, :], NT)
        va_ref[...] = _dot(xb, w_ref[D_MAIN + D_KV : D_IN, :], NT).astype(BF16)
        c, s = cos_ref[...], sin_ref[...]
        q_ref[...] = _rope(q, _lane_tile(c, D_ATTN // LANES), _lane_tile(s, D_ATTN // LANES)).astype(BF16)
        k_ref[...] = _rope(k, c, s).astype(BF16)

    body, in_specs, operands = _after(
        dep, body, [_row_spec(TM, D_MODEL), _const_spec((D_IN, D_MODEL)), _row_spec(TM, LANES), _row_spec(TM, LANES)], [x, w_in_t, cos, sin])
    return pl.pallas_call(
        body,
        name="in_proj",
        grid=(t // TM,),
        in_specs=in_specs,
        out_specs=[_row_spec(TM, D_GMLP), _row_spec(TM, D_GMLP), _row_spec(TM, D_ATTN), _row_spec(TM, D_KV), _row_spec(TM, D_KV)],
        out_shape=[
            jax.ShapeDtypeStruct((t, D_GMLP), F32),
            jax.ShapeDtypeStruct((t, D_GMLP), F32),
            jax.ShapeDtypeStruct((t, D_ATTN), BF16),
            jax.ShapeDtypeStruct((t, D_KV), BF16),
            jax.ShapeDtypeStruct((t, D_KV), BF16),
        ],
        compiler_params=_params(("parallel",)),
    )(*operands)


def _step_rows(i):
    return (i, 0)


def _chunk_before_step(i):
    return (jnp.maximum(CHUNKS_PER_STEP * i - 1, 0), 0)


def _chunk_specs():
    step = CHUNKS_PER_STEP * CHUNK
    return [
        pl.BlockSpec((step, D_GMLP), _step_rows),
        pl.BlockSpec((step, D_GMLP), _step_rows),
        pl.BlockSpec((step, D_ATTN), _step_rows),
        pl.BlockSpec((step, D_KV), _step_rows),
        pl.BlockSpec((CHUNK, D_KV), _chunk_before_step),
        pl.BlockSpec((step, D_KV), _step_rows),
        pl.BlockSpec((CHUNK, D_KV), _chunk_before_step),
    ]


def _half_lane_masks(rows):
    lane = lax.broadcasted_iota(jnp.int32, (rows, LANES), 1)
    return lane < HEAD_DIM


def _kv_variants(kv2):
    left = _half_lane_masks(kv2.shape[0])
    f = kv2.astype(F32)
    swapped = pltpu.roll(f, HEAD_DIM, 1)
    zero = jnp.zeros_like(f)
    g0 = (jnp.where(left, f, zero).astype(BF16), jnp.where(left, zero, swapped).astype(BF16))
    g1 = (jnp.where(left, swapped, zero).astype(BF16), jnp.where(left, zero, f).astype(BF16))
    return (g0, g1)


def _band_mask(i, heads=1):
    row = lax.broadcasted_iota(jnp.int32, (heads * CHUNK, 2 * CHUNK), 0) & (CHUNK - 1)
    col = lax.broadcasted_iota(jnp.int32, (heads * CHUNK, 2 * CHUNK), 1)
    no_prev = jnp.where(i > 0, 0, 4 * CHUNK)
    in_prev = jnp.logical_and(col < CHUNK, (col - row) > no_prev)
    in_cur = jnp.logical_and(col >= CHUNK, (col - CHUNK) <= row)
    return jnp.logical_or(in_prev, in_cur)


def _causal_mask():
    row = lax.broadcasted_iota(jnp.int32, (CHUNK, CHUNK), 0)
    col = lax.broadcasted_iota(jnp.int32, (CHUNK, CHUNK), 1)
    return col <= row


def _store_spatial_weights(w_ref, wcat_ref, wcat_t_ref=None):
    causal = _causal_mask()
    for p in range(D_GMLP // LANES):
        wl = jnp.where(causal, w_ref[2 * p], 0.0)
        wr = jnp.where(causal, w_ref[2 * p + 1], 0.0)
        wcat_ref[p] = jnp.concatenate([wl, wr], axis=1).astype(BF16)
        if wcat_t_ref is not None:
            wcat_t_ref[p] = jnp.concatenate([wl.T, wr.T], axis=1).astype(BF16)


def _pair_stack(xp, left):
    return jnp.concatenate([jnp.where(left, xp, 0.0), jnp.where(left, 0.0, xp)], axis=0).astype(BF16)


def _mixer_fwd(u, vg, q, k, va, v_ln_g, v_ln_b, w_spatial, bias_full, sinks, dep=None):
    t = u.shape[0]

    def body(u_ref, vg_ref, q_ref, kc_ref, kp_ref, vc_ref, vp_ref, g_ref, b_ref, w_ref, bias_ref, sink_ref, cat_ref, wcat):
        i = pl.program_id(0)
        left = _half_lane_masks(CHUNK)

        @pl.when(i == 0)
        def _():
            _store_spatial_weights(w_ref, wcat)

        heads = range(N_HEADS)
        pair_cols = [slice(p * LANES, (p + 1) * LANES) for p in range(D_GMLP // LANES)]
        sinks_h = [sink_ref[h] for h in heads]
        for c in range(CHUNKS_PER_STEP):
            rows = slice(c * CHUNK, (c + 1) * CHUNK)
            before = slice((c - 1) * CHUNK, c * CHUNK)
            k_prev = kp_ref[...] if c == 0 else kc_ref[before, :]
            v_prev = vp_ref[...] if c == 0 else vc_ref[before, :]
            k_var = _kv_variants(jnp.concatenate([k_prev, kc_ref[rows, :]], axis=0))
            v_var = _kv_variants(jnp.concatenate([v_prev, vc_ref[rows, :]], axis=0))
            scores = [_dot(q_ref[rows, pair_cols[h // 2]], k_var[h // 4][h % 2], NT) for h in heads]

            ug = _gelu(u_ref[rows, :])
            xhat, _ = _layer_norm_stats(_gelu(vg_ref[rows, :]))
            vgl = xhat * g_ref[...] + b_ref[...]
            mixed = [_dot(wcat[p], _pair_stack(vgl[:, cols], left)) for p, cols in enumerate(pair_cols)]

            valid = _band_mask(CHUNKS_PER_STEP * i + c)
            masked = [jnp.where(valid, scores[h] * SCALE, NEG_INF) for h in heads]
            maxes = [jnp.maximum(jnp.max(masked[h], axis=1, keepdims=True), sinks_h[h]) for h in heads]
            exps = [jnp.exp(masked[h] - maxes[h]) for h in heads]
            invs = [1.0 / (jnp.sum(exps[h], axis=1, keepdims=True) + jnp.exp(sinks_h[h] - maxes[h])) for h in heads]
            probs = [(exps[h] * invs[h]).astype(BF16) for h in heads]
            for p, cols in enumerate(pair_cols):
                cat_ref[rows, cols] = (ug[:, cols] * (mixed[p] + bias_ref[:, cols])).astype(BF16)
            for p in range(D_ATTN // LANES):
                out = _dot(probs[2 * p], v_var[p // 2][0]) + _dot(probs[2 * p + 1], v_var[p // 2][1])
                cat_ref[rows, D_GMLP + p * LANES : D_GMLP + (p + 1) * LANES] = out.astype(BF16)

    in_specs = _chunk_specs() + [
        _const_spec((1, D_GMLP)),
        _const_spec((1, D_GMLP)),
        _const_spec((N_HEADS, CHUNK, CHUNK)),
        _const_spec((CHUNK, D_GMLP)),
        pl.BlockSpec(memory_space=pltpu.SMEM),
    ]
    body, in_specs, operands = _after(dep, body, in_specs, [u, vg, q, k, k, va, va, v_ln_g, v_ln_b, w_spatial, bias_full, sinks])
    return pl.pallas_call(
        body,
        name="mixer_fwd",
        grid=(t // (CHUNKS_PER_STEP * CHUNK),),
        in_specs=in_specs,
        out_specs=pl.BlockSpec((CHUNKS_PER_STEP * CHUNK, D_MODEL), lambda i: (i, 0)),
        out_shape=jax.ShapeDtypeStruct((t, D_MODEL), BF16),
        scratch_shapes=[pltpu.VMEM((D_GMLP // LANES, CHUNK, 2 * CHUNK), BF16)],
        compiler_params=_params(("arbitrary",)),
    )(*operands)


def _ffn_fwd_loss(cat, x, w_out, ln1_g, ln1_b, w1, w2, ln2_g, ln2_b, target):
    t = x.shape[0]

    def body(cat_ref, x_ref, wo_ref, g1_ref, b1_ref, w1_ref, w2_ref, g2_ref, b2_ref, tgt_ref,
             xh_ref, rstd_ref, x1b_ref, r_ref, dz2_ref, dz2b_ref, dg2_ref, db2_ref, sq_ref):
        @pl.when(pl.program_id(0) == 0)
        def _():
            dg2_ref[...] = jnp.zeros_like(dg2_ref)
            db2_ref[...] = jnp.zeros_like(db2_ref)
            sq_ref[...] = jnp.zeros_like(sq_ref)

        parts = [slice(p * FFN_PART, (p + 1) * FFN_PART) for p in range(TM_FFN_FWD // FFN_PART)]

        def norm1(rows, z):
            xhat1, rstd1 = _layer_norm_stats(z)
            xh_ref[rows, :] = xhat1
            rstd_ref[rows, :] = rstd1
            x1 = xhat1 * g1_ref[...] + b1_ref[...]
            x1b = x1.astype(BF16)
            x1b_ref[rows, :] = x1b
            return x1, x1b

        def feed_forward(rows, x1b, pre):
            ff = None
            for j in range(N_FF_BLOCKS):
                r = jnp.maximum(pre, 0.0)
                r_ref[rows, j * D_MODEL : (j + 1) * D_MODEL] = r.astype(BF16)
                part = _dot((r * r).astype(BF16), w2_ref[j])
                ff = part if ff is None else ff + part
                if j + 1 < N_FF_BLOCKS:
                    pre = _dot(x1b, w1_ref[j + 1])
            return ff

        def norm2_and_loss(rows, x1, ff):
            xhat2, rstd2 = _layer_norm_stats(ALPHA * x1 + ff)
            err = xhat2 * g2_ref[...] + b2_ref[...] - tgt_ref[rows, :]
            sq_ref[...] += jnp.sum(err * err, axis=0, keepdims=True)
            dy = err * (1.0 / D_MODEL)
            dg2_ref[...] += jnp.sum(dy * xhat2, axis=0, keepdims=True)
            db2_ref[...] += jnp.sum(dy, axis=0, keepdims=True)
            dz2 = _layer_norm_bwd(dy * g2_ref[...], xhat2, rstd2)
            dz2_ref[rows, :] = dz2
            dz2b_ref[rows, :] = dz2.astype(BF16)

        projected = _dot(cat_ref[parts[0], :], wo_ref[...])
        last = None
        for i, rows in enumerate(parts):
            z = ALPHA * x_ref[rows, :] + projected
            if i + 1 < len(parts):
                projected = _dot(cat_ref[parts[i + 1], :], wo_ref[...])
            x1, x1b = norm1(rows, z)
            pre = _dot(x1b, w1_ref[0])
            if last is not None:
                norm2_and_loss(*last)
            last = (rows, x1, feed_forward(rows, x1b, pre))
        norm2_and_loss(*last)

    vec = _const_spec((1, D_MODEL))
    tile = _row_spec(TM_FFN_FWD, D_MODEL)
    wspec = _const_spec((N_FF_BLOCKS, D_MODEL, D_MODEL), single_buffer=True)
    return pl.pallas_call(
        body,
        name="ffn_fwd_loss",
        grid=(t // TM_FFN_FWD,),
        in_specs=[tile, tile, _const_spec((D_MODEL, D_MODEL), single_buffer=True), vec, vec, wspec, wspec, vec, vec, tile],
        out_specs=[tile, _row_spec(TM_FFN_FWD, 1), tile, _row_spec(TM_FFN_FWD, D_FF), tile, tile, vec, vec, vec],
        out_shape=[
            jax.ShapeDtypeStruct((t, D_MODEL), F32),
            jax.ShapeDtypeStruct((t, 1), F32),
            jax.ShapeDtypeStruct((t, D_MODEL), BF16),
            jax.ShapeDtypeStruct((t, D_FF), BF16),
            jax.ShapeDtypeStruct((t, D_MODEL), F32),
            jax.ShapeDtypeStruct((t, D_MODEL), BF16),
            jax.ShapeDtypeStruct((1, D_MODEL), F32),
            jax.ShapeDtypeStruct((1, D_MODEL), F32),
            jax.ShapeDtypeStruct((1, D_MODEL), F32),
        ],
        compiler_params=_params(("arbitrary",)),
    )(cat, x, w_out, ln1_g, ln1_b, w1, w2, ln2_g, ln2_b, target)


def _ffn_bwd_ln1(dz2, r, x1b, xhat1, rstd1, ln1_g, w1, w2, w_out, dep=None):
    t = dz2.shape[0]

    def body(dz2_ref, r_ref, x1b_ref, xh_ref, rstd_ref, g1_ref, w1_ref, w2_ref, wo_ref, gw1_ref, dz1_ref, dz1b_ref, dcat_ref, dg1_ref, db1_ref):
        @pl.when(pl.program_id(0) == 0)
        def _():
            dg1_ref[...] = jnp.zeros_like(dg1_ref)
            db1_ref[...] = jnp.zeros_like(db1_ref)
            gw1_ref[...] = jnp.zeros_like(gw1_ref)

        dz2 = dz2_ref[...]
        dz2b = dz2.astype(BF16)
        x1_t = x1b_ref[...].astype(F32).T.astype(BF16)
        dx1 = ALPHA * dz2
        for j in range(N_FF_BLOCKS):
            cols = slice(j * D_MODEL, (j + 1) * D_MODEL)
            dpre = (_dot(dz2b, w2_ref[j], NT) * (2.0 * r_ref[:, cols].astype(F32))).astype(BF16)
            gw1_ref[cols, :] += _dot(x1_t, dpre)
            dx1 = dx1 + _dot(dpre, w1_ref[j], NT)
        xhat1 = xh_ref[...]
        dg1_ref[...] += jnp.sum(dx1 * xhat1, axis=0, keepdims=True)
        db1_ref[...] += jnp.sum(dx1, axis=0, keepdims=True)
        dz1 = _layer_norm_bwd(dx1 * g1_ref[...], xhat1, rstd_ref[...])
        dz1_ref[...] = dz1
        dz1b = dz1.astype(BF16)
        dz1b_ref[...] = dz1b
        dcat_ref[...] = _dot(dz1b, wo_ref[...], NT).astype(BF16)

    vec = _const_spec((1, D_MODEL))
    tile = _row_spec(TM_FFN, D_MODEL)
    wspec = _const_spec((N_FF_BLOCKS, D_MODEL, D_MODEL), single_buffer=True)
    body, in_specs, operands = _after(
        dep, body,
        [tile, _row_spec(TM_FFN, D_FF), tile, tile, _row_spec(TM_FFN, 1), vec, wspec, wspec, _const_spec((D_MODEL, D_MODEL), single_buffer=True)],
        [dz2, r, x1b, xhat1, rstd1, ln1_g, w1, w2, w_out])
    return pl.pallas_call(
        body,
        name="ffn_bwd_ln1",
        grid=(t // TM_FFN,),
        in_specs=in_specs,
        out_specs=[_const_spec((D_FF, D_MODEL), single_buffer=True), tile, tile, tile, vec, vec],
        out_shape=[
            jax.ShapeDtypeStruct((D_FF, D_MODEL), F32),
            jax.ShapeDtypeStruct((t, D_MODEL), F32),
            jax.ShapeDtypeStruct((t, D_MODEL), BF16),
            jax.ShapeDtypeStruct((t, D_MODEL), BF16),
            jax.ShapeDtypeStruct((1, D_MODEL), F32),
            jax.ShapeDtypeStruct((1, D_MODEL), F32),
        ],
        compiler_params=_params(("arbitrary",)),
    )(*operands)


def _mixer_bwd(u, vg, q, k, va, dcat, cos, sin, v_ln_g, v_ln_b, w_spatial, bias_full, sinks, r, dz2b, dep=None):
    t = u.shape[0]
    n_chunks = t // CHUNK
    assert CHUNKS_PER_STEP == N_FF_BLOCKS

    def body(u_ref, vg_ref, q_ref, kc_ref, kp_ref, vc_ref, vp_ref, dcat_ref, cosc_ref, sinc_ref, cosp_ref, sinp_ref,
             g_ref, b_ref, w_ref, bias_ref, sink_ref, r_ref, dz2b_ref,
             dmain_ref, dkv_ref, gw2_ref, dg_ref, db_ref, dw_ref, dbs_ref, dsink_ref, dmix_acc, wcat, wcat_t):
        i = pl.program_id(0)
        left = _half_lane_masks(CHUNK)
        lane = lax.broadcasted_iota(jnp.int32, (CHUNK, LANES), 1)
        n_pairs = D_GMLP // LANES

        @pl.when(i == 0)
        def _():
            dg_ref[...] = jnp.zeros_like(dg_ref)
            db_ref[...] = jnp.zeros_like(db_ref)
            dw_ref[...] = jnp.zeros_like(dw_ref)
            dsink_ref[...] = jnp.zeros_like(dsink_ref)
            dmix_acc[...] = jnp.zeros_like(dmix_acc)
            gw2_ref[...] = jnp.zeros_like(gw2_ref)
            _store_spatial_weights(w_ref, wcat, wcat_t)

        n_qpairs = D_ATTN // LANES
        heads = range(N_HEADS)
        pair_cols = [slice(p * LANES, (p + 1) * LANES) for p in range(n_pairs)]
        sinks_h = [sink_ref[h] for h in heads]
        gain = g_ref[...]
        causal = _causal_mask()
        lane_row = lax.broadcasted_iota(jnp.int32, (1, LANES), 1)
        heads_per_group = N_HEADS // 2

        def group_grad_t(lhs_t, rhs_heads):
            parts = []
            for g in range(2):
                group = range(g * heads_per_group, (g + 1) * heads_per_group)
                lhs = jnp.concatenate([lhs_t[h * HEAD_DIM : (h + 1) * HEAD_DIM] for h in group], axis=1)
                parts.append(_dot(lhs, jnp.concatenate([rhs_heads[h] for h in group], axis=0)))
            return jnp.concatenate(parts, axis=0)

        for c in range(CHUNKS_PER_STEP):
            chunk = CHUNKS_PER_STEP * i + c
            rows = slice(c * CHUNK, (c + 1) * CHUNK)
            before = slice((c - 1) * CHUNK, c * CHUNK)

            k_prev = kp_ref[...] if c == 0 else kc_ref[before, :]
            v_prev = vp_ref[...] if c == 0 else vc_ref[before, :]
            k_var = _kv_variants(jnp.concatenate([k_prev, kc_ref[rows, :]], axis=0))
            v_var = _kv_variants(jnp.concatenate([v_prev, vc_ref[rows, :]], axis=0))
            q_pairs = [q_ref[rows, cols] for cols in pair_cols]
            do_all = dcat_ref[rows, D_GMLP:D_MODEL]
            do_pairs = [do_all[:, cols] for cols in pair_cols]
            scores = [_dot(q_pairs[h // 2], k_var[h // 4][h % 2], NT) for h in heads]
            dprobs = [_dot(do_pairs[h // 2], v_var[h // 4][h % 2], NT) for h in heads]
            q_t = q_ref[rows, :].astype(F32).T.astype(BF16)
            do_t = do_all.astype(F32).T.astype(BF16)

            ff_cols = slice(c * D_MODEL, (c + 1) * D_MODEL)
            relu_block = r_ref[:, ff_cols]
            gw2_ref[ff_cols, :] += _dot(relu_block * relu_block, dz2b_ref[...], TN)

            ug, dug_du = _gelu_and_grad(u_ref[rows, :])
            gv, dgv_dv = _gelu_and_grad(vg_ref[rows, :])
            xhat, rstd = _layer_norm_stats(gv)
            vgl = xhat * gain + b_ref[...]
            mixed = [_dot(wcat[p], _pair_stack(vgl[:, cols], left)) for p, cols in enumerate(pair_cols)]

            valid = _band_mask(chunk)
            masked = [jnp.where(valid, scores[h] * SCALE, NEG_INF) for h in heads]
            maxes = [jnp.maximum(jnp.max(masked[h], axis=1, keepdims=True), sinks_h[h]) for h in heads]
            exps = [jnp.exp(masked[h] - maxes[h]) for h in heads]
            exp_sinks = [jnp.exp(sinks_h[h] - maxes[h]) for h in heads]
            invs = [1.0 / (jnp.sum(exps[h], axis=1, keepdims=True) + exp_sinks[h]) for h in heads]
            probs = [exps[h] * invs[h] for h in heads]
            dsums = [jnp.sum(probs[h] * dprobs[h], axis=1, keepdims=True) for h in heads]
            ds_b = [(probs[h] * (dprobs[h] - dsums[h]) * SCALE).astype(BF16) for h in heads]
            probs_b = [probs[h].astype(BF16) for h in heads]

            dm_stacks = []
            for p, cols in enumerate(pair_cols):
                da = dcat_ref[rows, cols].astype(F32)
                dmain_ref[rows, cols] = (da * (mixed[p] + bias_ref[:, cols]) * dug_du[:, cols]).astype(BF16)
                dmixed = da * ug[:, cols]
                dmix_acc[:, cols] += dmixed
                dm_stacks.append(_pair_stack(dmixed, left))

            dq_all = jnp.concatenate(
                [_dot(ds_b[2 * p], k_var[p // 2][0]) + _dot(ds_b[2 * p + 1], k_var[p // 2][1]) for p in range(n_qpairs)], axis=1)
            dk2_t = group_grad_t(q_t, ds_b)
            dv2_t = group_grad_t(do_t, probs_b)

            for p, cols in enumerate(pair_cols):
                dw_pair = _dot(dm_stacks[p], vgl[:, cols].astype(BF16), NT)
                dw_ref[2 * p] += jnp.where(causal, dw_pair[:CHUNK], 0.0)
                dw_ref[2 * p + 1] += jnp.where(causal, dw_pair[CHUNK:], 0.0)
            dvgl = jnp.concatenate([_dot(wcat_t[p], dm_stacks[p]) for p in range(n_pairs)], axis=1)

            dsink_row = jnp.zeros((1, LANES), F32)
            for h in heads:
                d_sink = -jnp.sum(exp_sinks[h] * invs[h] * dsums[h], axis=0, keepdims=True)
                dsink_row = dsink_row + jnp.where(lane_row == h, d_sink, 0.0)
            dsink_ref[0:1, :] += dsink_row
            cos_c, sin_c = cosc_ref[rows, :], sinc_ref[rows, :]
            cos_p = cosp_ref[...] if c == 0 else cosc_ref[before, :]
            sin_p = sinp_ref[...] if c == 0 else sinc_ref[before, :]
            dmain_ref[rows, 2 * D_GMLP : D_MAIN] = _rope_transposed(dq_all, _lane_tile(cos_c, n_qpairs), _lane_tile(sin_c, n_qpairs)).astype(BF16)
            dk2 = dk2_t.T
            dv2 = dv2_t.T
            cur = pl.ds(pl.multiple_of(chunk * CHUNK, CHUNK), CHUNK)
            dkv_ref[cur, 0:D_KV] = _rope_transposed(dk2[CHUNK:], cos_c, sin_c)
            dkv_ref[cur, D_KV : 2 * D_KV] = dv2[CHUNK:]
            prev = pl.ds(pl.multiple_of(jnp.maximum(chunk - 1, 0) * CHUNK, CHUNK), CHUNK)
            dkv_ref[prev, 0:D_KV] += _rope_transposed(dk2[:CHUNK], cos_p, sin_p)
            dkv_ref[prev, D_KV : 2 * D_KV] += dv2[:CHUNK]

            dg_ref[...] += jnp.sum(dvgl * xhat, axis=0, keepdims=True)
            db_ref[...] += jnp.sum(dvgl, axis=0, keepdims=True)
            dgv = _layer_norm_bwd(dvgl * gain, xhat, rstd)
            dmain_ref[rows, D_GMLP : 2 * D_GMLP] = (dgv * dgv_dv).astype(BF16)

        @pl.when(i == n_chunks // CHUNKS_PER_STEP - 1)
        def _():
            tile = jnp.zeros((CHUNK, LANES), F32)
            for p, cols in enumerate(pair_cols):
                dm = dmix_acc[:, cols]
                sl = jnp.sum(jnp.where(left, dm, 0.0), axis=1, keepdims=True)
                sr = jnp.sum(jnp.where(left, 0.0, dm), axis=1, keepdims=True)
                tile = jnp.where(lane == 2 * p, sl, tile)
                tile = jnp.where(lane == 2 * p + 1, sr, tile)
            dbs_ref[...] = tile

    step = CHUNKS_PER_STEP * CHUNK
    in_specs = _chunk_specs() + [
        pl.BlockSpec((step, D_MODEL), _step_rows),
        pl.BlockSpec((step, LANES), _step_rows),
        pl.BlockSpec((step, LANES), _step_rows),
        pl.BlockSpec((CHUNK, LANES), _chunk_before_step),
        pl.BlockSpec((CHUNK, LANES), _chunk_before_step),
        _const_spec((1, D_GMLP)),
        _const_spec((1, D_GMLP)),
        _const_spec((N_HEADS, CHUNK, CHUNK)),
        _const_spec((CHUNK, D_GMLP)),
        pl.BlockSpec(memory_space=pltpu.SMEM),
        pl.BlockSpec((step, D_FF), _step_rows),
        pl.BlockSpec((step, D_MODEL), _step_rows),
    ]
    body, in_specs, operands = _after(
        dep, body, in_specs, [u, vg, q, k, k, va, va, dcat, cos, sin, cos, sin, v_ln_g, v_ln_b, w_spatial, bias_full, sinks, r, dz2b])
    return pl.pallas_call(
        body,
        name="mixer_bwd",
        grid=(n_chunks // CHUNKS_PER_STEP,),
        in_specs=in_specs,
        out_specs=[
            pl.BlockSpec((step, D_MAIN), _step_rows),
            _const_spec((t, 2 * D_KV)),
            _const_spec((D_FF, D_MODEL), single_buffer=True),
            _const_spec((1, D_GMLP)),
            _const_spec((1, D_GMLP)),
            _const_spec((N_HEADS, CHUNK, CHUNK)),
            _const_spec((CHUNK, LANES)),
            _const_spec((8, LANES)),
        ],
        out_shape=[
            jax.ShapeDtypeStruct((t, D_MAIN), BF16),
            jax.ShapeDtypeStruct((t, 2 * D_KV), F32),
            jax.ShapeDtypeStruct((D_FF, D_MODEL), F32),
            jax.ShapeDtypeStruct((1, D_GMLP), F32),
            jax.ShapeDtypeStruct((1, D_GMLP), F32),
            jax.ShapeDtypeStruct((N_HEADS, CHUNK, CHUNK), F32),
            jax.ShapeDtypeStruct((CHUNK, LANES), F32),
            jax.ShapeDtypeStruct((8, LANES), F32),
        ],
        scratch_shapes=[
            pltpu.VMEM((CHUNK, D_GMLP), F32),
            pltpu.VMEM((D_GMLP // LANES, CHUNK, 2 * CHUNK), BF16),
            pltpu.VMEM((D_GMLP // LANES, CHUNK, 2 * CHUNK), BF16),
        ],
        compiler_params=_params(("arbitrary",)),
    )(*operands)


def _grad_x_and_w_in_t(dh_main, dkv, dz1, x, w_in_t, dep=None):
    t = dz1.shape[0]

    def body(dm_ref, dkv_ref, dz1_ref, x_ref, w_ref, gx_ref, grad_ref):
        @pl.when(pl.program_id(0) == 0)
        def _():
            grad_ref[...] = jnp.zeros_like(grad_ref)

        dh_main_tile = dm_ref[...]
        dh_kv_tile = dkv_ref[...].astype(BF16)
        acc = ALPHA * dz1_ref[...] + _dot(dh_main_tile, w_ref[0:D_MAIN, :])
        gx_ref[...] = acc + _dot(dh_kv_tile, w_ref[D_MAIN:D_IN, :])
        xb = x_ref[...].astype(BF16)
        grad_ref[0:D_MAIN, :] += _dot(dh_main_tile, xb, TN)
        grad_ref[D_MAIN:D_IN, :] += _dot(dh_kv_tile, xb, TN)

    tile = _row_spec(TM, D_MODEL)
    body, in_specs, operands = _after(
        dep, body, [_row_spec(TM, D_MAIN), _row_spec(TM, 2 * D_KV), tile, tile, _const_spec((D_IN, D_MODEL))], [dh_main, dkv, dz1, x, w_in_t])
    return pl.pallas_call(
        body,
        name="grad_x_and_w_in",
        grid=(t // TM,),
        in_specs=in_specs,
        out_specs=[tile, _const_spec((D_IN, D_MODEL), single_buffer=True)],
        out_shape=[jax.ShapeDtypeStruct((t, D_MODEL), F32), jax.ShapeDtypeStruct((D_IN, D_MODEL), F32)],
        compiler_params=_params(("arbitrary",)),
    )(*operands)


def _token_contraction(name, out_rows, tk, in_arrays, contributions, dep=None):
    t = in_arrays[0].shape[0]

    def body(*refs):
        out_ref = refs[-1]

        @pl.when(pl.program_id(0) == 0)
        def _():
            out_ref[...] = jnp.zeros_like(out_ref)

        for row0, a, b in contributions(*refs[:-1]):
            out_ref[row0 : row0 + a.shape[1], :] += _dot(a, b, TN)

    in_specs = [_row_spec(tk, a.shape[1]) for a in in_arrays]
    body, in_specs, operands = _after(dep, body, in_specs, in_arrays)
    return pl.pallas_call(
        body,
        name=name,
        grid=(t // tk,),
        in_specs=in_specs,
        out_specs=_const_spec((out_rows, D_MODEL), single_buffer=True),
        out_shape=jax.ShapeDtypeStruct((out_rows, D_MODEL), F32),
        compiler_params=_params(("arbitrary",)),
    )(*operands)


def _grad_w_out(cat, dz1b, dep=None):
    def contributions(cat_ref, dz1_ref):
        return [(0, cat_ref[...], dz1_ref[...])]

    return _token_contraction("grad_w_out", D_MODEL, TK, [cat, dz1b], contributions, dep)


ANY = pl.BlockSpec(memory_space=pl.ANY)


def _mesh_position():
    return lax.axis_index("x"), lax.axis_index("y"), lax.axis_index("c")


def _other_chips(x, y):
    return [(1 - x, y), (x, 1 - y), (1 - x, 1 - y)]


def _remote(src, dst, send_sem, recv_sem, device):
    return pltpu.make_async_remote_copy(src_ref=src, dst_ref=dst, send_sem=send_sem, recv_sem=recv_sem, device_id=device, device_id_type=MESH)


def _rows(ref, start, size):
    return ref.at[pl.ds(start, size), :]


def _rope_tables_and_casts(pos_row, inv_freq_row, shards, dep=None):
    t = pos_row.shape[1]
    steps = t // TM
    n = len(shards)

    def body(pos_ref, f_ref, *rest):
        f32_refs, (cos_ref, sin_ref), bf16_refs = rest[:n], rest[n : n + 2], rest[n + 2 :]
        for src, dst in zip(f32_refs, bf16_refs):
            dst[...] = src[...].astype(BF16)
        pos_rows = jnp.broadcast_to(pos_ref[...].astype(F32), (LANES, TM)).T
        ang = pos_rows * f_ref[...]
        cos_ref[...] = jnp.cos(ang)
        sin_ref[...] = jnp.sin(ang)

    shard_specs = [_row_spec(s.shape[0] // steps, s.shape[1]) for s in shards]
    body, in_specs, operands = _after(
        dep, body, [pl.BlockSpec((1, TM), lambda i: (0, i)), _const_spec((1, LANES))] + shard_specs, [pos_row, inv_freq_row, *shards])
    outs = pl.pallas_call(
        body,
        name="rope_tables_and_casts",
        grid=(steps,),
        in_specs=in_specs,
        out_specs=[_row_spec(TM, LANES), _row_spec(TM, LANES)] + shard_specs,
        out_shape=[jax.ShapeDtypeStruct((t, LANES), F32)] * 2 + [jax.ShapeDtypeStruct(s.shape, BF16) for s in shards],
        compiler_params=_params(("parallel",)),
    )(*operands)
    return outs[0], outs[1], list(outs[2:])


def _small_all_reduce(slab_ref, sum_ref, landing, reduced, gathered, send_sems, recv_sems):
    part = slab_ref.shape[0] // 8
    x_, y_, c_ = _mesh_position()
    me = 4 * x_ + 2 * y_ + c_
    flips = [(f >> 2, (f >> 1) & 1, f & 1) for f in range(1, 8)]

    def peer(flip):
        fx, fy, fc = flip
        return (1 - x_ if fx else x_, 1 - y_ if fy else y_, 1 - c_ if fc else c_)

    def part_of(ref, device):
        return ref.at[pl.ds(pl.multiple_of(device * part, 8), part), :]

    def scatter_copies():
        out = []
        for kk, flip in enumerate(flips):
            px, py, pc = peer(flip)
            them = 4 * px + 2 * py + pc
            send = _remote(part_of(slab_ref, them), landing.at[me], send_sems.at[kk], recv_sems.at[kk], (px, py, pc))
            recv = _remote(landing.at[them], landing.at[them], send_sems.at[kk], recv_sems.at[kk], (px, py, pc))
            out.append((send, recv))
        return out

    def gather_copies():
        out = []
        for kk, flip in enumerate(flips):
            px, py, pc = peer(flip)
            them = 4 * px + 2 * py + pc
            send = _remote(reduced, part_of(gathered, me), send_sems.at[7 + kk], recv_sems.at[7 + kk], (px, py, pc))
            recv = _remote(part_of(gathered, them), part_of(gathered, them), send_sems.at[7 + kk], recv_sems.at[7 + kk], (px, py, pc))
            out.append((send, recv))
        return out

    def first():
        for send, _ in scatter_copies():
            send.start()
        landing[me] = part_of(slab_ref, me)[...]

    def middle():
        for _, recv in scatter_copies():
            recv.wait_recv()
        total = landing[0]
        for s in range(1, 8):
            total = total + landing[s]
        reduced[...] = total
        part_of(gathered, me)[...] = total
        for send, _ in gather_copies():
            send.start()

    def last():
        for send, recv in gather_copies():
            recv.wait_recv()
            send.wait_send()
        for send, _ in scatter_copies():
            send.wait_send()
        sum_ref[...] = gathered[...]

    return first, middle, last


def _small_all_reduce_scratch(slab):
    part = slab.shape[0] // 8
    return [
        pltpu.VMEM((8, part, LANES), F32),
        pltpu.VMEM((part, LANES), F32),
        pltpu.VMEM(slab.shape, F32),
        pltpu.SemaphoreType.DMA((14,)),
        pltpu.SemaphoreType.DMA((14,)),
    ]


HBM = pl.BlockSpec(memory_space=pltpu.HBM)
SEM = pl.BlockSpec(memory_space=pltpu.SEMAPHORE)
DATAFLOW = pltpu.SideEffectType.DATAFLOW_SIDE_EFFECTING
TOKEN = jax.ShapeDtypeStruct((8, LANES), F32)


def _plan_copies(bufs, plan, send_sems, recv_sems):
    out = []
    for i, (src, src_row, dst, dst_row, recv_row, rows, device) in enumerate(plan):
        send = _remote(_rows(bufs[src], src_row, rows), _rows(bufs[dst], dst_row, rows), send_sems.at[i], recv_sems.at[i], device)
        landed = _rows(bufs[dst], recv_row, rows)
        recv = _remote(landed, landed, send_sems.at[i], recv_sems.at[i], device)
        out.append((send, recv))
    return out


def _split_call(name, bufs, wait=None, start=None, after=None):
    n = len(bufs)
    n_in = n + (2 if wait else 0) + (1 if after is not None else 0)
    n_start = len(start(0, 0, 0)) if start else 0

    def body(*refs):
        ins = refs[:n]
        x, y, c = _mesh_position()
        if wait:
            for send, recv in _plan_copies(ins, wait[0](x, y, c), refs[n], refs[n + 1]):
                recv.wait_recv()
                send.wait_send()
        if start:
            for send, _ in _plan_copies(ins, start(x, y, c), refs[n_in + n + 1], refs[n_in + n + 2]):
                send.start()
        token = refs[n_in + n]
        token[...] = jnp.zeros_like(token)

    operands = [pltpu.with_memory_space_constraint(b, pltpu.HBM) for b in bufs]
    in_specs = [HBM] * n
    if wait:
        operands += [wait[1], wait[2]]
        in_specs += [SEM, SEM]
    if after is not None:
        operands.append(after)
        in_specs.append(ANY)
    out_shape = [pltpu.HBM(b.shape, b.dtype) for b in bufs] + [TOKEN]
    out_specs = [HBM] * n + [pl.BlockSpec(memory_space=pltpu.VMEM)]
    if start:
        out_shape += [pltpu.SemaphoreType.DMA((n_start,)), pltpu.SemaphoreType.DMA((n_start,))]
        out_specs += [SEM, SEM]
    outs = pl.pallas_call(
        body,
        name=name,
        in_specs=in_specs,
        out_specs=out_specs,
        out_shape=out_shape,
        input_output_aliases={i: i for i in range(n)},
        compiler_params=pltpu.CompilerParams(has_side_effects=DATAFLOW),
    )(*operands)
    return (list(outs[:n]), outs[n]) + tuple(outs[n + 1 :])


def _direct_gather_plans(shard_rows):
    n = len(shard_rows)

    def direct(x, y, c):
        me = 2 * x + y
        plan = []
        for w, rows in enumerate(shard_rows):
            half = rows // 2
            for px, py in _other_chips(x, y):
                plan.append((w, c * half, n + w, me * rows + c * half, (2 * px + py) * rows + c * half, half, (px, py, c)))
            plan.append((w, 0, n + w, me * rows, me * rows, rows, (x, y, 1 - c)))
        return plan

    def passed_on(x, y, c):
        plan = []
        for w, rows in enumerate(shard_rows):
            half = rows // 2
            for px, py in _other_chips(x, y):
                row = (2 * px + py) * rows
                plan.append((n + w, row + c * half, n + w, row + c * half, row + (1 - c) * half, half, (x, y, 1 - c)))
        return plan

    return direct, passed_on


def _gather_plans(shard_rows):
    n = len(shard_rows)

    def neighbours(x, y):
        return ((1 - x, y), (x, 1 - y))

    def direct(x, y, c):
        me = 2 * x + y
        plan = []
        for w, rows in enumerate(shard_rows):
            half = rows // 2
            for px, py in neighbours(x, y):
                plan.append((w, c * half, n + w, me * rows + c * half, (2 * px + py) * rows + c * half, half, (px, py, c)))
            plan.append((w, 0, n + w, me * rows, me * rows, rows, (x, y, 1 - c)))
        return plan

    def passed_on(x, y, c):
        (xn, yn), diagonal = neighbours(x, y), 2 * (1 - x) + (1 - y)
        relayed = (1 - c) * (2 * xn[0] + xn[1]) + c * (2 * yn[0] + yn[1])
        target = (x * (1 - c) + (1 - x) * c, (1 - y) * (1 - c) + y * c, c)
        plan = []
        for w, rows in enumerate(shard_rows):
            half = rows // 2
            for px, py in (xn, yn):
                row = (2 * px + py) * rows
                plan.append((n + w, row + c * half, n + w, row + c * half, row + (1 - c) * half, half, (x, y, 1 - c)))
            plan.append((n + w, relayed * rows + c * half, n + w, relayed * rows + c * half, diagonal * rows + c * half, half, target))
        return plan

    def diagonal_passed_on(x, y, c):
        plan = []
        for w, rows in enumerate(shard_rows):
            half = rows // 2
            row = (2 * (1 - x) + (1 - y)) * rows
            plan.append((n + w, row + c * half, n + w, row + c * half, row + (1 - c) * half, half, (x, y, 1 - c)))
        return plan

    return direct, passed_on, diagonal_passed_on


def _swap_plan(block_rows):
    n = len(block_rows)

    def plan_fn(x, y, c):
        plan = []
        for w, rows in enumerate(block_rows):
            half = rows // 2
            for j in range(N_CHIPS):
                plan.append((w, j * rows + (1 - c) * half, n + w, j * half, j * half, half, (x, y, 1 - c)))
        return plan

    return plan_fn


def _exchange_plan(halves):
    n = len(halves)

    def plan_fn(x, y, c):
        plan = []
        for w, half in enumerate(halves):
            for kk, (px, py) in enumerate(_other_chips(x, y)):
                plan.append((w, (2 * px + py) * half, n + w, kk * half, kk * half, half, (px, py, c)))
        return plan

    return plan_fn


def _sibling_plan(shard_rows):
    def plan_fn(x, y, c):
        return [(w, c * (rows // 2), w, c * (rows // 2), (1 - c) * (rows // 2), rows // 2, (x, y, 1 - c)) for w, rows in enumerate(shard_rows)]

    return plan_fn


def _shifted(plan_fn, first):
    return lambda x, y, c: [(src + first, a, dst + first, b, r, n, dev) for src, a, dst, b, r, n, dev in plan_fn(x, y, c)]


def _landing(rows, cols, dtype):
    return lax.empty((rows, cols), dtype)


def _row_tile(rows, cap=512):
    best = 8
    for cand in range(8, cap + 1, 8):
        if rows % cand == 0:
            best = cand
    return best


def _pair_sum(name, grad, theirs, pos, slab=None):
    half = theirs.shape[0] // N_CHIPS
    cols = theirs.shape[1]
    tile = _row_tile(half)
    steps = half // tile
    n_steps = steps * N_CHIPS

    def body(pos_ref, g_ref, t_ref, *rest):
        if slab is not None:
            slab_ref, p_ref, own_ref, sum_ref = rest[:4]
            k = pl.program_id(0) * N_CHIPS + pl.program_id(1)
            first, middle, last = _small_all_reduce(slab_ref, sum_ref, *rest[4:])
            pl.when(k == 0)(first)
            pl.when(k == n_steps // 2)(middle)
        else:
            p_ref, own_ref = rest
        total = g_ref[...] + t_ref[...]
        p_ref[...] = total.astype(BF16)

        @pl.when(pl.program_id(1) == pos_ref[1])
        def _():
            own_ref[...] = total

        if slab is not None:
            pl.when(k == n_steps - 1)(last)

    carried = slab is not None
    whole_slab = [pl.BlockSpec(slab.shape, lambda i, j, pos: (0, 0))] if carried else []
    outs = pl.pallas_call(
        body,
        name=name,
        grid_spec=pltpu.PrefetchScalarGridSpec(
            num_scalar_prefetch=1,
            grid=(steps, N_CHIPS),
            in_specs=[
                pl.BlockSpec((tile, cols), lambda i, j, pos: ((2 * j + pos[0]) * steps + i, 0)),
                pl.BlockSpec((tile, cols), lambda i, j, pos: (j * steps + i, 0)),
            ] + whole_slab,
            out_specs=[
                pl.BlockSpec((tile, cols), lambda i, j, pos: (j * steps + i, 0)),
                pl.BlockSpec((tile, cols), lambda i, j, pos: (i, 0)),
            ] + whole_slab,
            scratch_shapes=_small_all_reduce_scratch(slab) if carried else [],
        ),
        out_shape=[jax.ShapeDtypeStruct((N_CHIPS * half, cols), BF16), jax.ShapeDtypeStruct((half, cols), F32)]
        + ([jax.ShapeDtypeStruct(slab.shape, slab.dtype)] if carried else []),
        compiler_params=_params(("arbitrary", "arbitrary") if carried else ("parallel", "arbitrary")),
    )(pos, grad, theirs, *([slab] if carried else []))
    return outs


def _adamw_update(w, g, m, v):
    nm = ADAM_B1 * m + (1.0 - ADAM_B1) * g
    nv = ADAM_B2 * v + (1.0 - ADAM_B2) * (g * g)
    m_hat = nm / (1.0 - ADAM_B1**ADAM_STEP)
    v_hat = nv / (1.0 - ADAM_B2**ADAM_STEP)
    return -ADAM_LR * (m_hat / (jnp.sqrt(v_hat) + ADAM_EPS) + ADAM_WD * w), nm, nv


def _chip_sum(name, own, landed, pos):
    half, cols = own.shape
    tile = _row_tile(half)
    steps = half // tile

    def body(pos_ref, own_ref, l0, l1, l2, o_ref):
        o_ref[...] = ((own_ref[...] + l0[...].astype(F32)) + l1[...].astype(F32)) + l2[...].astype(F32)

    landed_specs = [pl.BlockSpec((tile, cols), lambda i, pos, _k=k: (_k * steps + i, 0)) for k in range(N_CHIPS - 1)]
    return pl.pallas_call(
        body,
        name=name,
        grid_spec=pltpu.PrefetchScalarGridSpec(
            num_scalar_prefetch=1,
            grid=(steps,),
            in_specs=[pl.BlockSpec((tile, cols), lambda i, pos: (i, 0))] + landed_specs,
            out_specs=pl.BlockSpec((tile, cols), lambda i, pos: (pos[0] * steps + i, 0)),
        ),
        out_shape=jax.ShapeDtypeStruct((2 * half, cols), F32),
        compiler_params=_params(("parallel",)),
    )(pos, own, landed, landed, landed)


def _adamw(name, w, g, m, v):
    rows, cols = w.shape
    tile = rows if rows * cols <= 256 * 1024 else _row_tile(rows)

    def body(w_ref, g_ref, m_ref, v_ref, g_out_ref, d_ref, nm_ref, nv_ref):
        g = g_ref[...]
        g_out_ref[...] = g
        d_ref[...], nm_ref[...], nv_ref[...] = _adamw_update(w_ref[...], g, m_ref[...], v_ref[...])

    spec = _row_spec(tile, cols)
    return pl.pallas_call(
        body,
        name=name,
        grid=(rows // tile,),
        in_specs=[spec] * 4,
        out_specs=[spec] * 4,
        out_shape=[jax.ShapeDtypeStruct((rows, cols), F32)] * 4,
        compiler_params=_params(("parallel",)),
    )(w, g, m, v)


_SMALL = (
    ("v_ln_g", (D_GMLP,), 8),
    ("v_ln_b", (D_GMLP,), 8),
    ("w_spatial", (N_HEADS, CHUNK, CHUNK), 1024),
    ("b_spatial", (N_HEADS, CHUNK), 8),
    ("sinks", (N_HEADS,), 8),
    ("ln1_g", (D_MODEL,), 8),
    ("ln1_b", (D_MODEL,), 8),
    ("ln2_g", (D_MODEL,), 8),
    ("ln2_b", (D_MODEL,), 8),
    ("squared_error", (D_MODEL,), 8),
)
N_SMALL_PARAMS = len(_SMALL) - 1


def _pack_small(values):
    parts = []
    for (name, shape, rows), val in zip(_SMALL, values, strict=True):
        flat = val.reshape(-1).astype(F32)
        parts.append(jnp.pad(flat, (0, rows * LANES - flat.shape[0])).reshape(rows, LANES))
    parts.append(jnp.zeros((SMALL_ROWS - sum(rows for _, _, rows in _SMALL), LANES), F32))
    return jnp.concatenate(parts, axis=0)


def _adamw_small(g_slab, params, first, second):
    n = N_SMALL_PARAMS

    def pieces(shape):
        if len(shape) == 3:
            return [((0, h), h * shape[1], shape[1], shape[2]) for h in range(shape[0])]
        if len(shape) == 2:
            return [((0,), 0, shape[0], shape[1])]
        if shape[0] >= LANES:
            return [((slice(None), slice(r * LANES, (r + 1) * LANES)), r, 1, LANES) for r in range(shape[0] // LANES)]
        return [((slice(None), slice(0, shape[0])), 0, 1, shape[0])]

    def body(*refs):
        g_ref = refs[0]
        w_refs, m_refs, v_refs = refs[1 : 1 + n], refs[1 + n : 1 + 2 * n], refs[1 + 2 * n : 1 + 3 * n]
        outs = refs[1 + 3 * n :]
        row0 = 0
        for idx, (_, shape, rows) in enumerate(_SMALL[:n]):
            for where, first_row, n_rows, lanes in pieces(shape):
                g = g_ref[row0 + first_row : row0 + first_row + n_rows, 0:lanes]
                delta, nm, nv = _adamw_update(w_refs[idx][where], g, m_refs[idx][where], v_refs[idx][where])
                for group, val in enumerate((g, delta, nm, nv)):
                    outs[group * n + idx][where] = val
            row0 += rows

    vmem = pl.BlockSpec(memory_space=pltpu.VMEM)
    shapes = [jax.ShapeDtypeStruct(p.shape, F32) for p in params]
    outs = pl.pallas_call(
        body,
        name="adamw_small",
        in_specs=[vmem] * (1 + 3 * n),
        out_specs=[vmem] * (4 * n),
        out_shape=shapes * 4,
        compiler_params=_params(),
    )(g_slab, *params, *first, *second)
    return [list(outs[group * n : (group + 1) * n]) for group in range(4)]


def kernel(x, positions, w_in, v_ln_g, v_ln_b, w_spatial, b_spatial, sinks, w_out, ln1_g, ln1_b, w_ff1, w_ff2, ln2_g, ln2_b, loss_target, m_w_in, m_v_ln_g, m_v_ln_b, m_w_spatial, m_b_spatial, m_sinks, m_w_out, m_ln1_g, m_ln1_b, m_w_ff1, m_w_ff2, m_ln2_g, m_ln2_b, v_w_in, v_v_ln_g, v_v_ln_b, v_w_spatial, v_b_spatial, v_sinks, v_w_out, v_ln1_g, v_ln1_b, v_w_ff1, v_w_ff2, v_ln2_g, v_ln2_b):
    t = x.shape[1]
    x2 = x.reshape(t, D_MODEL)
    target = loss_target.reshape(t, D_MODEL)

    w_in_shard = w_in[0].T.astype(BF16)
    in_direct, in_pass = _direct_gather_plans([w_in_shard.shape[0]])
    in_bufs, in_started, in_send, in_recv = _split_call(
        "gather_w_in_start", [w_in_shard, _landing(N_CHIPS * w_in_shard.shape[0], D_MODEL, BF16)], start=in_direct)
    inv_freq = ROPE_THETA ** (-jnp.arange(0, HEAD_DIM, 2, dtype=F32) / HEAD_DIM)
    cos, sin, later = _rope_tables_and_casts(
        positions, jnp.tile(inv_freq, LANES // (HEAD_DIM // 2)).reshape(1, LANES), [w_out[0], w_ff1[0], w_ff2[0]], dep=in_started)
    later_rows = [s.shape[0] for s in later]
    direct_plan, pass_plan, diagonal_plan = _gather_plans(later_rows)
    bufs, started, direct_send, direct_recv = _split_call(
        "gather_start", later + [_landing(N_CHIPS * r, D_MODEL, BF16) for r in later_rows], start=direct_plan, after=cos)
    in_bufs, in_passing, in_pass_send, in_pass_recv = _split_call(
        "gather_w_in_pass", in_bufs, wait=(in_direct, in_send, in_recv), start=in_pass, after=started)
    in_bufs, _ = _split_call("gather_w_in_end", in_bufs, wait=(in_pass, in_pass_send, in_pass_recv), after=in_passing)
    w_in_t = in_bufs[1]

    u, vg, q, k, va = _in_proj(x2, w_in_t, cos, sin)
    bias_full = jnp.repeat(b_spatial[0].T, HEAD_DIM, axis=1)
    sink_vec = sinks.reshape(N_HEADS)
    bufs, passing, pass_send, pass_recv = _split_call(
        "gather_pass", bufs, wait=(direct_plan, direct_send, direct_recv), start=pass_plan, after=u)
    cat = _mixer_fwd(u, vg, q, k, va, v_ln_g, v_ln_b, w_spatial[0], bias_full, sink_vec, dep=passing)
    bufs, passing, diag_send, diag_recv = _split_call(
        "gather_pass_diagonal", bufs, wait=(pass_plan, pass_send, pass_recv), start=diagonal_plan, after=cat)
    bufs, _ = _split_call("gather_end", bufs, wait=(diagonal_plan, diag_send, diag_recv), after=passing)
    w_out_all = bufs[3]
    w1_all = bufs[4].reshape(N_FF_BLOCKS, D_MODEL, D_MODEL)
    w2_all = bufs[5].reshape(N_FF_BLOCKS, D_MODEL, D_MODEL)
    xhat1, rstd1, x1b, r, dz2, dz2b, d_ln2_g, d_ln2_b, sq_err = _ffn_fwd_loss(
        cat, x2, w_out_all, ln1_g, ln1_b, w1_all, w2_all, ln2_g, ln2_b, target)

    pos = jnp.stack([lax.axis_index("c"), 2 * lax.axis_index("x") + lax.axis_index("y")]).astype(jnp.int32)
    half_landing = lambda g: _landing(g.shape[0] // 2, D_MODEL, F32)
    ff_swap_plan = _swap_plan([D_FF // N_CHIPS])
    ff_exchange_plan = _exchange_plan([D_FF // N_CHIPS // 2])
    exchange_landing = lambda p: _landing(3 * p.shape[0] // N_CHIPS, D_MODEL, BF16)
    g_ff1_local, dz1, dz1b, dcat, d_ln1_g, d_ln1_b = _ffn_bwd_ln1(dz2, r, x1b, xhat1, rstd1, ln1_g, w1_all, w2_all, w_out_all)
    ff1_bufs, swapping1, swap1_send, swap1_recv = _split_call("ff1_swap_start", [g_ff1_local, half_landing(g_ff1_local)], start=ff_swap_plan)
    g_out_local = _grad_w_out(cat, dz1b, dep=swapping1)
    ff1_bufs, _ = _split_call("ff1_swap_wait", ff1_bufs, wait=(ff_swap_plan, swap1_send, swap1_recv), after=g_out_local)
    ff1_sum, ff1_own = _pair_sum("grad_pair_sum_w_ff1", ff1_bufs[0], ff1_bufs[1], pos)
    ff1_ex, exchanging1, ex1_send, ex1_recv = _split_call(
        "ff1_exchange_start", [ff1_sum, exchange_landing(ff1_sum)], start=ff_exchange_plan)
    dh_main, dkv, g_ff2_local, d_v_ln_g, d_v_ln_b, d_w_spatial, d_b_spatial_t, d_sinks = _mixer_bwd(
        u, vg, q, k, va, dcat, cos, sin, v_ln_g, v_ln_b, w_spatial[0], bias_full, sink_vec, r, dz2b, dep=exchanging1)
    ff2_bufs, swapping2, swap2_send, swap2_recv = _split_call("ff2_swap_start", [g_ff2_local, half_landing(g_ff2_local)], start=ff_swap_plan)
    grad_x_flat, g_in_local = _grad_x_and_w_in_t(dh_main, dkv, dz1, x2, w_in_t, dep=swapping2)
    grad_x = grad_x_flat.reshape(1, t, D_MODEL)

    small = [g_in_local, g_out_local]
    small_swap_plan = _swap_plan([g.shape[0] // N_CHIPS for g in small])
    swap_bufs, small_swapping, ss_send, ss_recv = _split_call(
        "small_swap_start", small + [half_landing(g) for g in small], start=small_swap_plan)
    ff2_bufs, _ = _split_call("ff2_swap_wait", ff2_bufs, wait=(ff_swap_plan, swap2_send, swap2_recv), after=small_swapping)
    ff2_sum, ff2_own, small_g = _pair_sum("grad_pair_sum_w_ff2", ff2_bufs[0], ff2_bufs[1], pos, slab=_pack_small(
        [d_v_ln_g, d_v_ln_b, d_w_spatial, d_b_spatial_t[:, :N_HEADS].T, d_sinks[0, :N_HEADS], d_ln1_g, d_ln1_b, d_ln2_g, d_ln2_b, sq_err]))
    sq_row = sum(rows for _, _, rows in _SMALL[:N_SMALL_PARAMS])
    loss = 0.5 * jnp.sum(small_g[sq_row : sq_row + _SMALL[N_SMALL_PARAMS][2]]) / D_MODEL
    ff2_ex, exchanging2, ex2_send, ex2_recv = _split_call(
        "ff2_exchange_start", [ff2_sum, exchange_landing(ff2_sum)], start=ff_exchange_plan)
    swap_bufs, _ = _split_call("small_swap_wait", swap_bufs, wait=(small_swap_plan, ss_send, ss_recv), after=exchanging2)
    pair_sums = [_pair_sum("grad_pair_sum_" + nm, g, th, pos) for nm, g, th in zip(["w_in", "w_out"], swap_bufs[:2], swap_bufs[2:])]
    small_plan = _exchange_plan([p.shape[0] // N_CHIPS for p, _ in pair_sums])
    small_bufs, small_exchanging, sm_send, sm_recv = _split_call(
        "small_exchange_start", [p for p, _ in pair_sums] + [exchange_landing(p) for p, _ in pair_sums], start=small_plan)

    ff1_ex, _ = _split_call("ff1_exchange_wait", ff1_ex, wait=(ff_exchange_plan, ex1_send, ex1_recv), after=small_exchanging)
    ff_pair_plan = _sibling_plan([D_FF // N_CHIPS])
    half_ff1 = _chip_sum("grad_chip_sum_w_ff1", ff1_own, ff1_ex[1], pos)
    (half_ff1, *ff2_ex), _, g1_send, g1_recv = _split_call(
        "ff2_exchange_wait_ff1_pair_start", [half_ff1] + ff2_ex, wait=(_shifted(ff_exchange_plan, 1), ex2_send, ex2_recv), start=ff_pair_plan)
    half_ff2 = _chip_sum("grad_chip_sum_w_ff2", ff2_own, ff2_ex[1], pos)
    (half_ff2, g_w_ff1), _, g2_send, g2_recv = _split_call(
        "ff1_pair_wait_ff2_pair_start", [half_ff2, half_ff1], wait=(_shifted(ff_pair_plan, 1), g1_send, g1_recv), start=ff_pair_plan)
    g_w_ff1, d_w_ff1, nm_w_ff1, nv_w_ff1 = _adamw("adamw_w_ff1", w_ff1[0], g_w_ff1, m_w_ff1[0], v_w_ff1[0])
    small_bufs, _ = _split_call("small_exchange_wait", small_bufs, wait=(small_plan, sm_send, sm_recv), after=nv_w_ff1)
    shards = [_chip_sum("grad_chip_sum_" + nm, own, ld, pos) for nm, (_, own), ld in zip(["w_in", "w_out"], pair_sums, small_bufs[2:])]
    small_pair_plan = _sibling_plan([s.shape[0] for s in shards])
    (*shards, g_w_ff2), _, g3_send, g3_recv = _split_call(
        "ff2_pair_wait_small_pair_start", shards + [half_ff2], wait=(_shifted(ff_pair_plan, 2), g2_send, g2_recv), start=small_pair_plan)
    g_w_ff2, d_w_ff2, nm_w_ff2, nv_w_ff2 = _adamw("adamw_w_ff2", w_ff2[0], g_w_ff2, m_w_ff2[0], v_w_ff2[0])
    (g_w_in_t, g_w_out), _ = _split_call("small_pair_wait", shards, wait=(small_pair_plan, g3_send, g3_recv), after=nv_w_ff2)
    g_w_in, d_w_in, nm_w_in, nv_w_in = (a.T for a in _adamw("adamw_w_in", w_in[0].T, g_w_in_t, m_w_in[0].T, v_w_in[0].T))
    g_w_out, d_w_out, nm_w_out, nv_w_out = _adamw("adamw_w_out", w_out[0], g_w_out, m_w_out[0], v_w_out[0])
    small_grads, small_d, small_nm, small_nv = _adamw_small(
        small_g,
        [v_ln_g, v_ln_b, w_spatial, b_spatial, sinks, ln1_g, ln1_b, ln2_g, ln2_b],
        [m_v_ln_g, m_v_ln_b, m_w_spatial, m_b_spatial, m_sinks, m_ln1_g, m_ln1_b, m_ln2_g, m_ln2_b],
        [v_v_ln_g, v_v_ln_b, v_w_spatial, v_b_spatial, v_sinks, v_ln1_g, v_ln1_b, v_ln2_g, v_ln2_b])

    def with_big(small, w_in_v, w_out_v, w_ff1_v, w_ff2_v):
        g_vg, g_vb, g_ws, g_bs, g_sk, g_1g, g_1b, g_2g, g_2b = small
        return [w_in_v[None], g_vg, g_vb, g_ws, g_bs, g_sk, w_out_v[None], g_1g, g_1b, w_ff1_v[None], w_ff2_v[None], g_2g, g_2b]

    return (
        loss,
        grad_x,
        *with_big(small_grads, g_w_in, g_w_out, g_w_ff1, g_w_ff2),
        *with_big(small_d, d_w_in, d_w_out, d_w_ff1, d_w_ff2),
        *with_big(small_nm, nm_w_in, nm_w_out, nm_w_ff1, nm_w_ff2),
        *with_big(small_nv, nv_w_in, nv_w_out, nv_w_ff1, nv_w_ff2),
    )
```

```python
import math

import jax
import jax.numpy as jnp
from jax import lax
from jax.experimental import pallas as pl
from jax.experimental.pallas import tpu as pltpu

F32 = jnp.float32
BF16 = jnp.bfloat16

D_MODEL = 1024
HEAD_DIM = 64
D_GMLP = 512
D_ATTN = 512
D_KV = 128
D_IN = 2 * D_GMLP + D_ATTN + 2 * D_KV
D_MAIN = 2 * D_GMLP + D_ATTN
N_HEADS = 8
CHUNK = 128
CHUNKS_PER_STEP = 4
ROPE_THETA = 10000.0
D_FF = 4 * D_MODEL
N_FF_BLOCKS = 4
LN_EPS = 1e-5
ALPHA = (2.0 * 1) ** 0.25
NEG_INF = -1e30
SCALE = 1.0 / math.sqrt(HEAD_DIM)

ADAM_LR = 0.001
ADAM_B1 = 0.9
ADAM_B2 = 0.999
ADAM_EPS = 1e-08
ADAM_WD = 0.01
ADAM_STEP = 10

N_CHIPS = 4
LANES = 128
V7X_VMEM_BYTES = 64 * 1024 * 1024
VMEM_LIMIT = V7X_VMEM_BYTES - 8 * 1024 * 1024
TM = 512
TM_FFN = 256
TM_FFN_FWD = 512
FFN_PART = 256
TK = 1024
SMALL_ROWS = 1152
MESH = pl.DeviceIdType.MESH

NT = (((1,), (1,)), ((), ()))
TN = (((0,), (0,)), ((), ()))


def _dot(a, b, dims=None):
    if dims is None:
        return jnp.dot(a, b, preferred_element_type=F32)
    return lax.dot_general(a, b, dims, preferred_element_type=F32)


def _params(semantics=None):
    return pltpu.CompilerParams(dimension_semantics=semantics, vmem_limit_bytes=VMEM_LIMIT)


def _const_spec(shape, single_buffer=False):
    zeros = (0,) * len(shape)
    if single_buffer:
        return pl.BlockSpec(shape, lambda *_: zeros, pipeline_mode=pl.Buffered(1))
    return pl.BlockSpec(shape, lambda *_: zeros)


def _row_spec(rows, cols):
    return pl.BlockSpec((rows, cols), lambda i: (i, 0))


def _after(dep, body, in_specs, operands):
    if dep is None:
        return body, list(in_specs), list(operands)
    return (lambda dep_ref, *refs: body(*refs)), [pl.BlockSpec(memory_space=pl.ANY)] + list(in_specs), [dep] + list(operands)


def _gelu(x):
    k = math.sqrt(2.0 / math.pi)
    return 0.5 * x * (1.0 + jnp.tanh(k * (x + 0.044715 * (x * x * x))))


def _gelu_and_grad(x):
    k = math.sqrt(2.0 / math.pi)
    x2 = x * x
    t = jnp.tanh(k * (x + 0.044715 * (x2 * x)))
    g = 0.5 * x * (1.0 + t)
    dg = 0.5 * (1.0 + t) + 0.5 * x * (1.0 - t * t) * (k * (1.0 + 3.0 * 0.044715 * x2))
    return g, dg


def _layer_norm_stats(z):
    mu = jnp.mean(z, axis=-1, keepdims=True)
    zc = z - mu
    var = jnp.mean(zc * zc, axis=-1, keepdims=True)
    rstd = lax.rsqrt(var + LN_EPS)
    return zc * rstd, rstd


def _layer_norm_bwd(dxhat, xhat, rstd):
    m1 = jnp.mean(dxhat, axis=-1, keepdims=True)
    m2 = jnp.mean(dxhat * xhat, axis=-1, keepdims=True)
    return rstd * (dxhat - m1 - xhat * m2)


def _rotate_half(t):
    n = t.shape[1]
    lane = lax.broadcasted_iota(jnp.int32, t.shape, 1)
    first = (lane & (HEAD_DIM // 2)) == 0
    return jnp.where(first, -pltpu.roll(t, n - HEAD_DIM // 2, 1), pltpu.roll(t, HEAD_DIM // 2, 1))


def _rope(t, cos, sin):
    return t * cos + _rotate_half(t) * sin


def _rope_transposed(g, cos, sin):
    return g * cos - _rotate_half(g * sin)


def _lane_tile(a, reps):
    return jnp.tile(a, (1, reps)) if reps > 1 else a


def _in_proj(x, w_in_t, cos, sin, dep=None):
    t = x.shape[0]

    def body(x_ref, w_ref, cos_ref, sin_ref, u_ref, vg_ref, q_ref, k_ref, va_ref):
        xb = x_ref[...].astype(BF16)
        u_ref[...] = _dot(xb, w_ref[0:D_GMLP, :], NT)
        vg_ref[...] = _dot(xb, w_ref[D_GMLP : 2 * D_GMLP, :], NT)
        q = _dot(xb, w_ref[2 * D_GMLP : D_MAIN, :], NT)
        k = _dot(xb, w_ref[D_MAIN : D_MAIN + D_KV, :], NT)
        va_ref[...] = _dot(xb, w_ref[D_MAIN + D_KV : D_IN, :], NT).astype(BF16)
        c, s = cos_ref[...], sin_ref[...]
        q_ref[...] = _rope(q, _lane_tile(c, D_ATTN // LANES), _lane_tile(s, D_ATTN // LANES)).astype(BF16)
        k_ref[...] = _rope(k, c, s).astype(BF16)

    body, in_specs, operands = _after(
        dep, body, [_row_spec(TM, D_MODEL), _const_spec((D_IN, D_MODEL)), _row_spec(TM, LANES), _row_spec(TM, LANES)], [x, w_in_t, cos, sin])
    return pl.pallas_call(
        body,
        name="in_proj",
        grid=(t // TM,),
        in_specs=in_specs,
        out_specs=[_row_spec(TM, D_GMLP), _row_spec(TM, D_GMLP), _row_spec(TM, D_ATTN), _row_spec(TM, D_KV), _row_spec(TM, D_KV)],
        out_shape=[
            jax.ShapeDtypeStruct((t, D_GMLP), F32),
            jax.ShapeDtypeStruct((t, D_GMLP), F32),
            jax.ShapeDtypeStruct((t, D_ATTN), BF16),
            jax.ShapeDtypeStruct((t, D_KV), BF16),
            jax.ShapeDtypeStruct((t, D_KV), BF16),
        ],
        compiler_params=_params(("parallel",)),
    )(*operands)


def _step_rows(i):
    return (i, 0)


def _chunk_before_step(i):
    return (jnp.maximum(CHUNKS_PER_STEP * i - 1, 0), 0)


def _chunk_specs():
    step = CHUNKS_PER_STEP * CHUNK
    return [
        pl.BlockSpec((step, D_GMLP), _step_rows),
        pl.BlockSpec((step, D_GMLP), _step_rows),
        pl.BlockSpec((step, D_ATTN), _step_rows),
        pl.BlockSpec((step, D_KV), _step_rows),
        pl.BlockSpec((CHUNK, D_KV), _chunk_before_step),
        pl.BlockSpec((step, D_KV), _step_rows),
        pl.BlockSpec((CHUNK, D_KV), _chunk_before_step),
    ]


def _half_lane_masks(rows):
    lane = lax.broadcasted_iota(jnp.int32, (rows, LANES), 1)
    return lane < HEAD_DIM


def _kv_variants(kv2):
    left = _half_lane_masks(kv2.shape[0])
    f = kv2.astype(F32)
    swapped = pltpu.roll(f, HEAD_DIM, 1)
    zero = jnp.zeros_like(f)
    g0 = (jnp.where(left, f, zero).astype(BF16), jnp.where(left, zero, swapped).astype(BF16))
    g1 = (jnp.where(left, swapped, zero).astype(BF16), jnp.where(left, zero, f).astype(BF16))
    return (g0, g1)


def _band_mask(i, heads=1):
    row = lax.broadcasted_iota(jnp.int32, (heads * CHUNK, 2 * CHUNK), 0) & (CHUNK - 1)
    col = lax.broadcasted_iota(jnp.int32, (heads * CHUNK, 2 * CHUNK), 1)
    no_prev = jnp.where(i > 0, 0, 4 * CHUNK)
    in_prev = jnp.logical_and(col < CHUNK, (col - row) > no_prev)
    in_cur = jnp.logical_and(col >= CHUNK, (col - CHUNK) <= row)
    return jnp.logical_or(in_prev, in_cur)


def _causal_mask():
    row = lax.broadcasted_iota(jnp.int32, (CHUNK, CHUNK), 0)
    col = lax.broadcasted_iota(jnp.int32, (CHUNK, CHUNK), 1)
    return col <= row


def _store_spatial_weights(w_ref, wcat_ref, wcat_t_ref=None):
    causal = _causal_mask()
    for p in range(D_GMLP // LANES):
        wl = jnp.where(causal, w_ref[2 * p], 0.0)
        wr = jnp.where(causal, w_ref[2 * p + 1], 0.0)
        wcat_ref[p] = jnp.concatenate([wl, wr], axis=1).astype(BF16)
        if wcat_t_ref is not None:
            wcat_t_ref[p] = jnp.concatenate([wl.T, wr.T], axis=1).astype(BF16)


def _pair_stack(xp, left):
    return jnp.concatenate([jnp.where(left, xp, 0.0), jnp.where(left, 0.0, xp)], axis=0).astype(BF16)


def _mixer_fwd(u, vg, q, k, va, v_ln_g, v_ln_b, w_spatial, bias_full, sinks, dep=None):
    t = u.shape[0]

    def body(u_ref, vg_ref, q_ref, kc_ref, kp_ref, vc_ref, vp_ref, g_ref, b_ref, w_ref, bias_ref, sink_ref, cat_ref, wcat):
        i = pl.program_id(0)
        left = _half_lane_masks(CHUNK)

        @pl.when(i == 0)
        def _():
            _store_spatial_weights(w_ref, wcat)

        heads = range(N_HEADS)
        pair_cols = [slice(p * LANES, (p + 1) * LANES) for p in range(D_GMLP // LANES)]
        sinks_h = [sink_ref[h] for h in heads]
        for c in range(CHUNKS_PER_STEP):
            rows = slice(c * CHUNK, (c + 1) * CHUNK)
            before = slice((c - 1) * CHUNK, c * CHUNK)
            k_prev = kp_ref[...] if c == 0 else kc_ref[before, :]
            v_prev = vp_ref[...] if c == 0 else vc_ref[before, :]
            k_var = _kv_variants(jnp.concatenate([k_prev, kc_ref[rows, :]], axis=0))
            v_var = _kv_variants(jnp.concatenate([v_prev, vc_ref[rows, :]], axis=0))
            scores = [_dot(q_ref[rows, pair_cols[h // 2]], k_var[h // 4][h % 2], NT) for h in heads]

            ug = _gelu(u_ref[rows, :])
            xhat, _ = _layer_norm_stats(_gelu(vg_ref[rows, :]))
            vgl = xhat * g_ref[...] + b_ref[...]
            mixed = [_dot(wcat[p], _pair_stack(vgl[:, cols], left)) for p, cols in enumerate(pair_cols)]

            valid = _band_mask(CHUNKS_PER_STEP * i + c)
            masked = [jnp.where(valid, scores[h] * SCALE, NEG_INF) for h in heads]
            maxes = [jnp.maximum(jnp.max(masked[h], axis=1, keepdims=True), sinks_h[h]) for h in heads]
            exps = [jnp.exp(masked[h] - maxes[h]) for h in heads]
            invs = [1.0 / (jnp.sum(exps[h], axis=1, keepdims=True) + jnp.exp(sinks_h[h] - maxes[h])) for h in heads]
            probs = [(exps[h] * invs[h]).astype(BF16) for h in heads]
            for p, cols in enumerate(pair_cols):
                cat_ref[rows, cols] = (ug[:, cols] * (mixed[p] + bias_ref[:, cols])).astype(BF16)
            for p in range(D_ATTN // LANES):
                out = _dot(probs[2 * p], v_var[p // 2][0]) + _dot(probs[2 * p + 1], v_var[p // 2][1])
                cat_ref[rows, D_GMLP + p * LANES : D_GMLP + (p + 1) * LANES] = out.astype(BF16)

    in_specs = _chunk_specs() + [
        _const_spec((1, D_GMLP)),
        _const_spec((1, D_GMLP)),
        _const_spec((N_HEADS, CHUNK, CHUNK)),
        _const_spec((CHUNK, D_GMLP)),
        pl.BlockSpec(memory_space=pltpu.SMEM),
    ]
    body, in_specs, operands = _after(dep, body, in_specs, [u, vg, q, k, k, va, va, v_ln_g, v_ln_b, w_spatial, bias_full, sinks])
    return pl.pallas_call(
        body,
        name="mixer_fwd",
        grid=(t // (CHUNKS_PER_STEP * CHUNK),),
        in_specs=in_specs,
        out_specs=pl.BlockSpec((CHUNKS_PER_STEP * CHUNK, D_MODEL), lambda i: (i, 0)),
        out_shape=jax.ShapeDtypeStruct((t, D_MODEL), BF16),
        scratch_shapes=[pltpu.VMEM((D_GMLP // LANES, CHUNK, 2 * CHUNK), BF16)],
        compiler_params=_params(("arbitrary",)),
    )(*operands)


def _ffn_fwd_loss(cat, x, w_out, ln1_g, ln1_b, w1, w2, ln2_g, ln2_b, target):
    t = x.shape[0]

    def body(cat_ref, x_ref, wo_ref, g1_ref, b1_ref, w1_ref, w2_ref, g2_ref, b2_ref, tgt_ref,
             xh_ref, rstd_ref, x1b_ref, r_ref, dz2_ref, dz2b_ref, dg2_ref, db2_ref, sq_ref):
        @pl.when(pl.program_id(0) == 0)
        def _():
            dg2_ref[...] = jnp.zeros_like(dg2_ref)
            db2_ref[...] = jnp.zeros_like(db2_ref)
            sq_ref[...] = jnp.zeros_like(sq_ref)

        parts = [slice(p * FFN_PART, (p + 1) * FFN_PART) for p in range(TM_FFN_FWD // FFN_PART)]

        def norm1(rows, z):
            xhat1, rstd1 = _layer_norm_stats(z)
            xh_ref[rows, :] = xhat1
            rstd_ref[rows, :] = rstd1
            x1 = xhat1 * g1_ref[...] + b1_ref[...]
            x1b = x1.astype(BF16)
            x1b_ref[rows, :] = x1b
            return x1, x1b

        def feed_forward(rows, x1b, pre):
            ff = None
            for j in range(N_FF_BLOCKS):
                r = jnp.maximum(pre, 0.0)
                r_ref[rows, j * D_MODEL : (j + 1) * D_MODEL] = r.astype(BF16)
                part = _dot((r * r).astype(BF16), w2_ref[j])
                ff = part if ff is None else ff + part
                if j + 1 < N_FF_BLOCKS:
                    pre = _dot(x1b, w1_ref[j + 1])
            return ff

        def norm2_and_loss(rows, x1, ff):
            xhat2, rstd2 = _layer_norm_stats(ALPHA * x1 + ff)
            err = xhat2 * g2_ref[...] + b2_ref[...] - tgt_ref[rows, :]
            sq_ref[...] += jnp.sum(err * err, axis=0, keepdims=True)
            dy = err * (1.0 / D_MODEL)
            dg2_ref[...] += jnp.sum(dy * xhat2, axis=0, keepdims=True)
            db2_ref[...] += jnp.sum(dy, axis=0, keepdims=True)
            dz2 = _layer_norm_bwd(dy * g2_ref[...], xhat2, rstd2)
            dz2_ref[rows, :] = dz2
            dz2b_ref[rows, :] = dz2.astype(BF16)

        projected = _dot(cat_ref[parts[0], :], wo_ref[...])
        last = None
        for i, rows in enumerate(parts):
            z = ALPHA * x_ref[rows, :] + projected
            if i + 1 < len(parts):
                projected = _dot(cat_ref[parts[i + 1], :], wo_ref[...])
            x1, x1b = norm1(rows, z)
            pre = _dot(x1b, w1_ref[0])
            if last is not None:
                norm2_and_loss(*last)
            last = (rows, x1, feed_forward(rows, x1b, pre))
        norm2_and_loss(*last)

    vec = _const_spec((1, D_MODEL))
    tile = _row_spec(TM_FFN_FWD, D_MODEL)
    wspec = _const_spec((N_FF_BLOCKS, D_MODEL, D_MODEL), single_buffer=True)
    return pl.pallas_call(
        body,
        name="ffn_fwd_loss",
        grid=(t // TM_FFN_FWD,),
        in_specs=[tile, tile, _const_spec((D_MODEL, D_MODEL), single_buffer=True), vec, vec, wspec, wspec, vec, vec, tile],
        out_specs=[tile, _row_spec(TM_FFN_FWD, 1), tile, _row_spec(TM_FFN_FWD, D_FF), tile, tile, vec, vec, vec],
        out_shape=[
            jax.ShapeDtypeStruct((t, D_MODEL), F32),
            jax.ShapeDtypeStruct((t, 1), F32),
            jax.ShapeDtypeStruct((t, D_MODEL), BF16),
            jax.ShapeDtypeStruct((t, D_FF), BF16),
            jax.ShapeDtypeStruct((t, D_MODEL), F32),
            jax.ShapeDtypeStruct((t, D_MODEL), BF16),
            jax.ShapeDtypeStruct((1, D_MODEL), F32),
            jax.ShapeDtypeStruct((1, D_MODEL), F32),
            jax.ShapeDtypeStruct((1, D_MODEL), F32),
        ],
        compiler_params=_params(("arbitrary",)),
    )(cat, x, w_out, ln1_g, ln1_b, w1, w2, ln2_g, ln2_b, target)


def _ffn_bwd_ln1(dz2, r, x1b, xhat1, rstd1, ln1_g, w1, w2, w_out, dep=None):
    t = dz2.shape[0]

    def body(dz2_ref, r_ref, x1b_ref, xh_ref, rstd_ref, g1_ref, w1_ref, w2_ref, wo_ref, gw1_ref, dz1_ref, dz1b_ref, dcat_ref, dg1_ref, db1_ref):
        @pl.when(pl.program_id(0) == 0)
        def _():
            dg1_ref[...] = jnp.zeros_like(dg1_ref)
            db1_ref[...] = jnp.zeros_like(db1_ref)
            gw1_ref[...] = jnp.zeros_like(gw1_ref)

        dz2 = dz2_ref[...]
        dz2b = dz2.astype(BF16)
        x1_t = x1b_ref[...].astype(F32).T.astype(BF16)
        dx1 = ALPHA * dz2
        for j in range(N_FF_BLOCKS):
            cols = slice(j * D_MODEL, (j + 1) * D_MODEL)
            dpre = (_dot(dz2b, w2_ref[j], NT) * (2.0 * r_ref[:, cols].astype(F32))).astype(BF16)
            gw1_ref[cols, :] += _dot(x1_t, dpre)
            dx1 = dx1 + _dot(dpre, w1_ref[j], NT)
        xhat1 = xh_ref[...]
        dg1_ref[...] += jnp.sum(dx1 * xhat1, axis=0, keepdims=True)
        db1_ref[...] += jnp.sum(dx1, axis=0, keepdims=True)
        dz1 = _layer_norm_bwd(dx1 * g1_ref[...], xhat1, rstd_ref[...])
        dz1_ref[...] = dz1
        dz1b = dz1.astype(BF16)
        dz1b_ref[...] = dz1b
        dcat_ref[...] = _dot(dz1b, wo_ref[...], NT).astype(BF16)

    vec = _const_spec((1, D_MODEL))
    tile = _row_spec(TM_FFN, D_MODEL)
    wspec = _const_spec((N_FF_BLOCKS, D_MODEL, D_MODEL), single_buffer=True)
    body, in_specs, operands = _after(
        dep, body,
        [tile, _row_spec(TM_FFN, D_FF), tile, tile, _row_spec(TM_FFN, 1), vec, wspec, wspec, _const_spec((D_MODEL, D_MODEL), single_buffer=True)],
        [dz2, r, x1b, xhat1, rstd1, ln1_g, w1, w2, w_out])
    return pl.pallas_call(
        body,
        name="ffn_bwd_ln1",
        grid=(t // TM_FFN,),
        in_specs=in_specs,
        out_specs=[_const_spec((D_FF, D_MODEL), single_buffer=True), tile, tile, tile, vec, vec],
        out_shape=[
            jax.ShapeDtypeStruct((D_FF, D_MODEL), F32),
            jax.ShapeDtypeStruct((t, D_MODEL), F32),
            jax.ShapeDtypeStruct((t, D_MODEL), BF16),
            jax.ShapeDtypeStruct((t, D_MODEL), BF16),
            jax.ShapeDtypeStruct((1, D_MODEL), F32),
            jax.ShapeDtypeStruct((1, D_MODEL), F32),
        ],
        compiler_params=_params(("arbitrary",)),
    )(*operands)


def _mixer_bwd(u, vg, q, k, va, dcat, cos, sin, v_ln_g, v_ln_b, w_spatial, bias_full, sinks, r, dz2b, dep=None):
    t = u.shape[0]
    n_chunks = t // CHUNK
    assert CHUNKS_PER_STEP == N_FF_BLOCKS

    def body(u_ref, vg_ref, q_ref, kc_ref, kp_ref, vc_ref, vp_ref, dcat_ref, cosc_ref, sinc_ref, cosp_ref, sinp_ref,
             g_ref, b_ref, w_ref, bias_ref, sink_ref, r_ref, dz2b_ref,
             dmain_ref, dkv_ref, gw2_ref, dg_ref, db_ref, dw_ref, dbs_ref, dsink_ref, dmix_acc, wcat, wcat_t):
        i = pl.program_id(0)
        left = _half_lane_masks(CHUNK)
        lane = lax.broadcasted_iota(jnp.int32, (CHUNK, LANES), 1)
        n_pairs = D_GMLP // LANES

        @pl.when(i == 0)
        def _():
            dg_ref[...] = jnp.zeros_like(dg_ref)
            db_ref[...] = jnp.zeros_like(db_ref)
            dw_ref[...] = jnp.zeros_like(dw_ref)
            dsink_ref[...] = jnp.zeros_like(dsink_ref)
            dmix_acc[...] = jnp.zeros_like(dmix_acc)
            gw2_ref[...] = jnp.zeros_like(gw2_ref)
            _store_spatial_weights(w_ref, wcat, wcat_t)

        n_qpairs = D_ATTN // LANES
        heads = range(N_HEADS)
        pair_cols = [slice(p * LANES, (p + 1) * LANES) for p in range(n_pairs)]
        sinks_h = [sink_ref[h] for h in heads]
        gain = g_ref[...]
        causal = _causal_mask()
        lane_row = lax.broadcasted_iota(jnp.int32, (1, LANES), 1)
        heads_per_group = N_HEADS // 2

        def group_grad_t(lhs_t, rhs_heads):
            parts = []
            for g in range(2):
                group = range(g * heads_per_group, (g + 1) * heads_per_group)
                lhs = jnp.concatenate([lhs_t[h * HEAD_DIM : (h + 1) * HEAD_DIM] for h in group], axis=1)
                parts.append(_dot(lhs, jnp.concatenate([rhs_heads[h] for h in group], axis=0)))
            return jnp.concatenate(parts, axis=0)

        for c in range(CHUNKS_PER_STEP):
            chunk = CHUNKS_PER_STEP * i + c
            rows = slice(c * CHUNK, (c + 1) * CHUNK)
            before = slice((c - 1) * CHUNK, c * CHUNK)

            k_prev = kp_ref[...] if c == 0 else kc_ref[before, :]
            v_prev = vp_ref[...] if c == 0 else vc_ref[before, :]
            k_var = _kv_variants(jnp.concatenate([k_prev, kc_ref[rows, :]], axis=0))
            v_var = _kv_variants(jnp.concatenate([v_prev, vc_ref[rows, :]], axis=0))
            q_pairs = [q_ref[rows, cols] for cols in pair_cols]
            do_all = dcat_ref[rows, D_GMLP:D_MODEL]
            do_pairs = [do_all[:, cols] for cols in pair_cols]
            scores = [_dot(q_pairs[h // 2], k_var[h // 4][h % 2], NT) for h in heads]
            dprobs = [_dot(do_pairs[h // 2], v_var[h // 4][h % 2], NT) for h in heads]
            q_t = q_ref[rows, :].astype(F32).T.astype(BF16)
            do_t = do_all.astype(F32).T.astype(BF16)

            ff_cols = slice(c * D_MODEL, (c + 1) * D_MODEL)
            relu_block = r_ref[:, ff_cols]
            gw2_ref[ff_cols, :] += _dot(relu_block * relu_block, dz2b_ref[...], TN)

            ug, dug_du = _gelu_and_grad(u_ref[rows, :])
            gv, dgv_dv = _gelu_and_grad(vg_ref[rows, :])
            xhat, rstd = _layer_norm_stats(gv)
            vgl = xhat * gain + b_ref[...]
            mixed = [_dot(wcat[p], _pair_stack(vgl[:, cols], left)) for p, cols in enumerate(pair_cols)]

            valid = _band_mask(chunk)
            masked = [jnp.where(valid, scores[h] * SCALE, NEG_INF) for h in heads]
            maxes = [jnp.maximum(jnp.max(masked[h], axis=1, keepdims=True), sinks_h[h]) for h in heads]
            exps = [jnp.exp(masked[h] - maxes[h]) for h in heads]
            exp_sinks = [jnp.exp(sinks_h[h] - maxes[h]) for h in heads]
            invs = [1.0 / (jnp.sum(exps[h], axis=1, keepdims=True) + exp_sinks[h]) for h in heads]
            probs = [exps[h] * invs[h] for h in heads]
            dsums = [jnp.sum(probs[h] * dprobs[h], axis=1, keepdims=True) for h in heads]
            ds_b = [(probs[h] * (dprobs[h] - dsums[h]) * SCALE).astype(BF16) for h in heads]
            probs_b = [probs[h].astype(BF16) for h in heads]

            dm_stacks = []
            for p, cols in enumerate(pair_cols):
                da = dcat_ref[rows, cols].astype(F32)
                dmain_ref[rows, cols] = (da * (mixed[p] + bias_ref[:, cols]) * dug_du[:, cols]).astype(BF16)
                dmixed = da * ug[:, cols]
                dmix_acc[:, cols] += dmixed
                dm_stacks.append(_pair_stack(dmixed, left))

            dq_all = jnp.concatenate(
                [_dot(ds_b[2 * p], k_var[p // 2][0]) + _dot(ds_b[2 * p + 1], k_var[p // 2][1]) for p in range(n_qpairs)], axis=1)
            dk2_t = group_grad_t(q_t, ds_b)
            dv2_t = group_grad_t(do_t, probs_b)

            for p, cols in enumerate(pair_cols):
                dw_pair = _dot(dm_stacks[p], vgl[:, cols].astype(BF16), NT)
                dw_ref[2 * p] += jnp.where(causal, dw_pair[:CHUNK], 0.0)
                dw_ref[2 * p + 1] += jnp.where(causal, dw_pair[CHUNK:], 0.0)
            dvgl = jnp.concatenate([_dot(wcat_t[p], dm_stacks[p]) for p in range(n_pairs)], axis=1)

            dsink_row = jnp.zeros((1, LANES), F32)
            for h in heads:
                d_sink = -jnp.sum(exp_sinks[h] * invs[h] * dsums[h], axis=0, keepdims=True)
                dsink_row = dsink_row + jnp.where(lane_row == h, d_sink, 0.0)
            dsink_ref[0:1, :] += dsink_row
            cos_c, sin_c = cosc_ref[rows, :], sinc_ref[rows, :]
            cos_p = cosp_ref[...] if c == 0 else cosc_ref[before, :]
            sin_p = sinp_ref[...] if c == 0 else sinc_ref[before, :]
            dmain_ref[rows, 2 * D_GMLP : D_MAIN] = _rope_transposed(dq_all, _lane_tile(cos_c, n_qpairs), _lane_tile(sin_c, n_qpairs)).astype(BF16)
            dk2 = dk2_t.T
            dv2 = dv2_t.T
            cur = pl.ds(pl.multiple_of(chunk * CHUNK, CHUNK), CHUNK)
            dkv_ref[cur, 0:D_KV] = _rope_transposed(dk2[CHUNK:], cos_c, sin_c)
            dkv_ref[cur, D_KV : 2 * D_KV] = dv2[CHUNK:]
            prev = pl.ds(pl.multiple_of(jnp.maximum(chunk - 1, 0) * CHUNK, CHUNK), CHUNK)
            dkv_ref[prev, 0:D_KV] += _rope_transposed(dk2[:CHUNK], cos_p, sin_p)
            dkv_ref[prev, D_KV : 2 * D_KV] += dv2[:CHUNK]

            dg_ref[...] += jnp.sum(dvgl * xhat, axis=0, keepdims=True)
            db_ref[...] += jnp.sum(dvgl, axis=0, keepdims=True)
            dgv = _layer_norm_bwd(dvgl * gain, xhat, rstd)
            dmain_ref[rows, D_GMLP : 2 * D_GMLP] = (dgv * dgv_dv).astype(BF16)

        @pl.when(i == n_chunks // CHUNKS_PER_STEP - 1)
        def _():
            tile = jnp.zeros((CHUNK, LANES), F32)
            for p, cols in enumerate(pair_cols):
                dm = dmix_acc[:, cols]
                sl = jnp.sum(jnp.where(left, dm, 0.0), axis=1, keepdims=True)
                sr = jnp.sum(jnp.where(left, 0.0, dm), axis=1, keepdims=True)
                tile = jnp.where(lane == 2 * p, sl, tile)
                tile = jnp.where(lane == 2 * p + 1, sr, tile)
            dbs_ref[...] = tile

    step = CHUNKS_PER_STEP * CHUNK
    in_specs = _chunk_specs() + [
        pl.BlockSpec((step, D_MODEL), _step_rows),
        pl.BlockSpec((step, LANES), _step_rows),
        pl.BlockSpec((step, LANES), _step_rows),
        pl.BlockSpec((CHUNK, LANES), _chunk_before_step),
        pl.BlockSpec((CHUNK, LANES), _chunk_before_step),
        _const_spec((1, D_GMLP)),
        _const_spec((1, D_GMLP)),
        _const_spec((N_HEADS, CHUNK, CHUNK)),
        _const_spec((CHUNK, D_GMLP)),
        pl.BlockSpec(memory_space=pltpu.SMEM),
        pl.BlockSpec((step, D_FF), _step_rows),
        pl.BlockSpec((step, D_MODEL), _step_rows),
    ]
    body, in_specs, operands = _after(
        dep, body, in_specs, [u, vg, q, k, k, va, va, dcat, cos, sin, cos, sin, v_ln_g, v_ln_b, w_spatial, bias_full, sinks, r, dz2b])
    return pl.pallas_call(
        body,
        name="mixer_bwd",
        grid=(n_chunks // CHUNKS_PER_STEP,),
        in_specs=in_specs,
        out_specs=[
            pl.BlockSpec((step, D_MAIN), _step_rows),
            _const_spec((t, 2 * D_KV)),
            _const_spec((D_FF, D_MODEL), single_buffer=True),
            _const_spec((1, D_GMLP)),
            _const_spec((1, D_GMLP)),
            _const_spec((N_HEADS, CHUNK, CHUNK)),
            _const_spec((CHUNK, LANES)),
            _const_spec((8, LANES)),
        ],
        out_shape=[
            jax.ShapeDtypeStruct((t, D_MAIN), BF16),
            jax.ShapeDtypeStruct((t, 2 * D_KV), F32),
            jax.ShapeDtypeStruct((D_FF, D_MODEL), F32),
            jax.ShapeDtypeStruct((1, D_GMLP), F32),
            jax.ShapeDtypeStruct((1, D_GMLP), F32),
            jax.ShapeDtypeStruct((N_HEADS, CHUNK, CHUNK), F32),
            jax.ShapeDtypeStruct((CHUNK, LANES), F32),
            jax.ShapeDtypeStruct((8, LANES), F32),
        ],
        scratch_shapes=[
            pltpu.VMEM((CHUNK, D_GMLP), F32),
            pltpu.VMEM((D_GMLP // LANES, CHUNK, 2 * CHUNK), BF16),
            pltpu.VMEM((D_GMLP // LANES, CHUNK, 2 * CHUNK), BF16),
        ],
        compiler_params=_params(("arbitrary",)),
    )(*operands)


def _grad_x_and_w_in_t(dh_main, dkv, dz1, x, w_in_t, dep=None):
    t = dz1.shape[0]

    def body(dm_ref, dkv_ref, dz1_ref, x_ref, w_ref, gx_ref, grad_ref):
        @pl.when(pl.program_id(0) == 0)
        def _():
            grad_ref[...] = jnp.zeros_like(grad_ref)

        dh_main_tile = dm_ref[...]
        dh_kv_tile = dkv_ref[...].astype(BF16)
        acc = ALPHA * dz1_ref[...] + _dot(dh_main_tile, w_ref[0:D_MAIN, :])
        gx_ref[...] = acc + _dot(dh_kv_tile, w_ref[D_MAIN:D_IN, :])
        xb = x_ref[...].astype(BF16)
        grad_ref[0:D_MAIN, :] += _dot(dh_main_tile, xb, TN)
        grad_ref[D_MAIN:D_IN, :] += _dot(dh_kv_tile, xb, TN)

    tile = _row_spec(TM, D_MODEL)
    body, in_specs, operands = _after(
        dep, body, [_row_spec(TM, D_MAIN), _row_spec(TM, 2 * D_KV), tile, tile, _const_spec((D_IN, D_MODEL))], [dh_main, dkv, dz1, x, w_in_t])
    return pl.pallas_call(
        body,
        name="grad_x_and_w_in",
        grid=(t // TM,),
        in_specs=in_specs,
        out_specs=[tile, _const_spec((D_IN, D_MODEL), single_buffer=True)],
        out_shape=[jax.ShapeDtypeStruct((t, D_MODEL), F32), jax.ShapeDtypeStruct((D_IN, D_MODEL), F32)],
        compiler_params=_params(("arbitrary",)),
    )(*operands)


ANY = pl.BlockSpec(memory_space=pl.ANY)


def _mesh_position():
    return lax.axis_index("x"), lax.axis_index("y"), lax.axis_index("c")


def _other_chips(x, y):
    return [(1 - x, y), (x, 1 - y), (1 - x, 1 - y)]


def _remote(src, dst, send_sem, recv_sem, device):
    return pltpu.make_async_remote_copy(src_ref=src, dst_ref=dst, send_sem=send_sem, recv_sem=recv_sem, device_id=device, device_id_type=MESH)


def _rows(ref, start, size):
    return ref.at[pl.ds(start, size), :]


def _rope_tables_and_casts(pos_row, inv_freq_row, shards, dep=None):
    t = pos_row.shape[1]
    steps = t // TM
    n = len(shards)

    def body(pos_ref, f_ref, *rest):
        f32_refs, (cos_ref, sin_ref), bf16_refs = rest[:n], rest[n : n + 2], rest[n + 2 :]
        for src, dst in zip(f32_refs, bf16_refs):
            dst[...] = src[...].astype(BF16)
        pos_rows = jnp.broadcast_to(pos_ref[...].astype(F32), (LANES, TM)).T
        ang = pos_rows * f_ref[...]
        cos_ref[...] = jnp.cos(ang)
        sin_ref[...] = jnp.sin(ang)

    shard_specs = [_row_spec(s.shape[0] // steps, s.shape[1]) for s in shards]
    body, in_specs, operands = _after(
        dep, body, [pl.BlockSpec((1, TM), lambda i: (0, i)), _const_spec((1, LANES))] + shard_specs, [pos_row, inv_freq_row, *shards])
    outs = pl.pallas_call(
        body,
        name="rope_tables_and_casts",
        grid=(steps,),
        in_specs=in_specs,
        out_specs=[_row_spec(TM, LANES), _row_spec(TM, LANES)] + shard_specs,
        out_shape=[jax.ShapeDtypeStruct((t, LANES), F32)] * 2 + [jax.ShapeDtypeStruct(s.shape, BF16) for s in shards],
        compiler_params=_params(("parallel",)),
    )(*operands)
    return outs[0], outs[1], list(outs[2:])


def _small_all_reduce(slab_ref, sum_ref, landing, reduced, gathered, send_sems, recv_sems):
    part = slab_ref.shape[0] // 8
    x_, y_, c_ = _mesh_position()
    me = 4 * x_ + 2 * y_ + c_
    flips = [(f >> 2, (f >> 1) & 1, f & 1) for f in range(1, 8)]

    def peer(flip):
        fx, fy, fc = flip
        return (1 - x_ if fx else x_, 1 - y_ if fy else y_, 1 - c_ if fc else c_)

    def part_of(ref, device):
        return ref.at[pl.ds(pl.multiple_of(device * part, 8), part), :]

    def scatter_copies():
        out = []
        for kk, flip in enumerate(flips):
            px, py, pc = peer(flip)
            them = 4 * px + 2 * py + pc
            send = _remote(part_of(slab_ref, them), landing.at[me], send_sems.at[kk], recv_sems.at[kk], (px, py, pc))
            recv = _remote(landing.at[them], landing.at[them], send_sems.at[kk], recv_sems.at[kk], (px, py, pc))
            out.append((send, recv))
        return out

    def gather_copies():
        out = []
        for kk, flip in enumerate(flips):
            px, py, pc = peer(flip)
            them = 4 * px + 2 * py + pc
            send = _remote(reduced, part_of(gathered, me), send_sems.at[7 + kk], recv_sems.at[7 + kk], (px, py, pc))
            recv = _remote(part_of(gathered, them), part_of(gathered, them), send_sems.at[7 + kk], recv_sems.at[7 + kk], (px, py, pc))
            out.append((send, recv))
        return out

    def first():
        for send, _ in scatter_copies():
            send.start()
        landing[me] = part_of(slab_ref, me)[...]

    def middle():
        for _, recv in scatter_copies():
            recv.wait_recv()
        total = landing[0]
        for s in range(1, 8):
            total = total + landing[s]
        reduced[...] = total
        part_of(gathered, me)[...] = total
        for send, _ in gather_copies():
            send.start()

    def last():
        for send, recv in gather_copies():
            recv.wait_recv()
            send.wait_send()
        for send, _ in scatter_copies():
            send.wait_send()
        sum_ref[...] = gathered[...]

    return first, middle, last


def _small_all_reduce_scratch(slab):
    part = slab.shape[0] // 8
    return [
        pltpu.VMEM((8, part, LANES), F32),
        pltpu.VMEM((part, LANES), F32),
        pltpu.VMEM(slab.shape, F32),
        pltpu.SemaphoreType.DMA((14,)),
        pltpu.SemaphoreType.DMA((14,)),
    ]


def _grad_w_out_and_small_all_reduce(cat, dz1b, slab, dep=None):
    t = cat.shape[0]
    steps = t // TM

    def body(cat_ref, dz1_ref, slab_ref, grad_ref, sum_ref, *scratch):
        k = pl.program_id(0)
        first, middle, last = _small_all_reduce(slab_ref, sum_ref, *scratch)

        @pl.when(k == 0)
        def _():
            grad_ref[...] = jnp.zeros_like(grad_ref)
            first()

        pl.when(k == steps // 2)(middle)
        grad_ref[...] += _dot(cat_ref[...], dz1_ref[...], TN)
        pl.when(k == steps - 1)(last)

    tile = _row_spec(TM, D_MODEL)
    body, in_specs, operands = _after(dep, body, [tile, tile, _const_spec(slab.shape)], [cat, dz1b, slab])
    return pl.pallas_call(
        body,
        name="grad_w_out_and_small_all_reduce",
        grid=(steps,),
        in_specs=in_specs,
        out_specs=[_const_spec((D_MODEL, D_MODEL), single_buffer=True), _const_spec(slab.shape)],
        out_shape=[jax.ShapeDtypeStruct((D_MODEL, D_MODEL), F32), jax.ShapeDtypeStruct(slab.shape, slab.dtype)],
        scratch_shapes=_small_all_reduce_scratch(slab),
        compiler_params=_params(("arbitrary",)),
    )(*operands)


HBM = pl.BlockSpec(memory_space=pltpu.HBM)
SEM = pl.BlockSpec(memory_space=pltpu.SEMAPHORE)
DATAFLOW = pltpu.SideEffectType.DATAFLOW_SIDE_EFFECTING
TOKEN = jax.ShapeDtypeStruct((8, LANES), F32)


def _plan_copies(bufs, plan, send_sems, recv_sems):
    out = []
    for i, (src, src_row, dst, dst_row, recv_row, rows, device) in enumerate(plan):
        send = _remote(_rows(bufs[src], src_row, rows), _rows(bufs[dst], dst_row, rows), send_sems.at[i], recv_sems.at[i], device)
        landed = _rows(bufs[dst], recv_row, rows)
        recv = _remote(landed, landed, send_sems.at[i], recv_sems.at[i], device)
        out.append((send, recv))
    return out


def _split_call(name, bufs, wait=None, start=None, after=None):
    n = len(bufs)
    n_in = n + (2 if wait else 0) + (1 if after is not None else 0)
    n_start = len(start(0, 0, 0)) if start else 0

    def body(*refs):
        ins = refs[:n]
        x, y, c = _mesh_position()
        if wait:
            for send, recv in _plan_copies(ins, wait[0](x, y, c), refs[n], refs[n + 1]):
                recv.wait_recv()
                send.wait_send()
        if start:
            for send, _ in _plan_copies(ins, start(x, y, c), refs[n_in + n + 1], refs[n_in + n + 2]):
                send.start()
        token = refs[n_in + n]
        token[...] = jnp.zeros_like(token)

    operands = [pltpu.with_memory_space_constraint(b, pltpu.HBM) for b in bufs]
    in_specs = [HBM] * n
    if wait:
        operands += [wait[1], wait[2]]
        in_specs += [SEM, SEM]
    if after is not None:
        operands.append(after)
        in_specs.append(ANY)
    out_shape = [pltpu.HBM(b.shape, b.dtype) for b in bufs] + [TOKEN]
    out_specs = [HBM] * n + [pl.BlockSpec(memory_space=pltpu.VMEM)]
    if start:
        out_shape += [pltpu.SemaphoreType.DMA((n_start,)), pltpu.SemaphoreType.DMA((n_start,))]
        out_specs += [SEM, SEM]
    outs = pl.pallas_call(
        body,
        name=name,
        in_specs=in_specs,
        out_specs=out_specs,
        out_shape=out_shape,
        input_output_aliases={i: i for i in range(n)},
        compiler_params=pltpu.CompilerParams(has_side_effects=DATAFLOW),
    )(*operands)
    return (list(outs[:n]), outs[n]) + tuple(outs[n + 1 :])


def _direct_gather_plans(shard_rows):
    n = len(shard_rows)

    def direct(x, y, c):
        me = 2 * x + y
        plan = []
        for w, rows in enumerate(shard_rows):
            half = rows // 2
            for px, py in _other_chips(x, y):
                plan.append((w, c * half, n + w, me * rows + c * half, (2 * px + py) * rows + c * half, half, (px, py, c)))
            plan.append((w, 0, n + w, me * rows, me * rows, rows, (x, y, 1 - c)))
        return plan

    def passed_on(x, y, c):
        plan = []
        for w, rows in enumerate(shard_rows):
            half = rows // 2
            for px, py in _other_chips(x, y):
                row = (2 * px + py) * rows
                plan.append((n + w, row + c * half, n + w, row + c * half, row + (1 - c) * half, half, (x, y, 1 - c)))
        return plan

    return direct, passed_on


def _gather_plans(shard_rows):
    n = len(shard_rows)

    def neighbours(x, y):
        return ((1 - x, y), (x, 1 - y))

    def direct(x, y, c):
        me = 2 * x + y
        plan = []
        for w, rows in enumerate(shard_rows):
            half = rows // 2
            for px, py in neighbours(x, y):
                plan.append((w, c * half, n + w, me * rows + c * half, (2 * px + py) * rows + c * half, half, (px, py, c)))
            plan.append((w, 0, n + w, me * rows, me * rows, rows, (x, y, 1 - c)))
        return plan

    def passed_on(x, y, c):
        (xn, yn), diagonal = neighbours(x, y), 2 * (1 - x) + (1 - y)
        relayed = (1 - c) * (2 * xn[0] + xn[1]) + c * (2 * yn[0] + yn[1])
        target = (x * (1 - c) + (1 - x) * c, (1 - y) * (1 - c) + y * c, c)
        plan = []
        for w, rows in enumerate(shard_rows):
            half = rows // 2
            for px, py in (xn, yn):
                row = (2 * px + py) * rows
                plan.append((n + w, row + c * half, n + w, row + c * half, row + (1 - c) * half, half, (x, y, 1 - c)))
            plan.append((n + w, relayed * rows + c * half, n + w, relayed * rows + c * half, diagonal * rows + c * half, half, target))
        return plan

    def diagonal_passed_on(x, y, c):
        plan = []
        for w, rows in enumerate(shard_rows):
            half = rows // 2
            row = (2 * (1 - x) + (1 - y)) * rows
            plan.append((n + w, row + c * half, n + w, row + c * half, row + (1 - c) * half, half, (x, y, 1 - c)))
        return plan

    return direct, passed_on, diagonal_passed_on


def _swap_plan(block_rows):
    n = len(block_rows)

    def plan_fn(x, y, c):
        plan = []
        for w, rows in enumerate(block_rows):
            half = rows // 2
            for j in range(N_CHIPS):
                plan.append((w, j * rows + (1 - c) * half, n + w, j * half, j * half, half, (x, y, 1 - c)))
        return plan

    return plan_fn


def _exchange_plan(halves):
    n = len(halves)

    def plan_fn(x, y, c):
        plan = []
        for w, half in enumerate(halves):
            for kk, (px, py) in enumerate(_other_chips(x, y)):
                plan.append((w, (2 * px + py) * half, n + w, kk * half, kk * half, half, (px, py, c)))
        return plan

    return plan_fn


def _sibling_plan(shard_rows):
    def plan_fn(x, y, c):
        return [(w, c * (rows // 2), w, c * (rows // 2), (1 - c) * (rows // 2), rows // 2, (x, y, 1 - c)) for w, rows in enumerate(shard_rows)]

    return plan_fn


def _shifted(plan_fn, first):
    return lambda x, y, c: [(src + first, a, dst + first, b, r, n, dev) for src, a, dst, b, r, n, dev in plan_fn(x, y, c)]


def _landing(rows, cols, dtype):
    return lax.empty((rows, cols), dtype)


def _row_tile(rows, cap=512):
    best = 8
    for cand in range(8, cap + 1, 8):
        if rows % cand == 0:
            best = cand
    return best


def _pair_sum(name, grad, theirs, pos):
    half = theirs.shape[0] // N_CHIPS
    cols = theirs.shape[1]
    tile = _row_tile(half)
    steps = half // tile

    def body(pos_ref, g_ref, t_ref, p_ref, own_ref):
        total = g_ref[...] + t_ref[...]
        p_ref[...] = total.astype(BF16)

        @pl.when(pl.program_id(1) == pos_ref[1])
        def _():
            own_ref[...] = total

    return pl.pallas_call(
        body,
        name=name,
        grid_spec=pltpu.PrefetchScalarGridSpec(
            num_scalar_prefetch=1,
            grid=(steps, N_CHIPS),
            in_specs=[
                pl.BlockSpec((tile, cols), lambda i, j, pos: ((2 * j + pos[0]) * steps + i, 0)),
                pl.BlockSpec((tile, cols), lambda i, j, pos: (j * steps + i, 0)),
            ],
            out_specs=[
                pl.BlockSpec((tile, cols), lambda i, j, pos: (j * steps + i, 0)),
                pl.BlockSpec((tile, cols), lambda i, j, pos: (i, 0)),
            ],
        ),
        out_shape=[jax.ShapeDtypeStruct((N_CHIPS * half, cols), BF16), jax.ShapeDtypeStruct((half, cols), F32)],
        compiler_params=_params(("parallel", "arbitrary")),
    )(pos, grad, theirs)


def _adamw_update(w, g, m, v):
    nm = ADAM_B1 * m + (1.0 - ADAM_B1) * g
    nv = ADAM_B2 * v + (1.0 - ADAM_B2) * (g * g)
    m_hat = nm / (1.0 - ADAM_B1**ADAM_STEP)
    v_hat = nv / (1.0 - ADAM_B2**ADAM_STEP)
    return -ADAM_LR * (m_hat / (jnp.sqrt(v_hat) + ADAM_EPS) + ADAM_WD * w), nm, nv


def _chip_sum(name, own, landed, pos):
    half, cols = own.shape
    tile = _row_tile(half)
    steps = half // tile

    def body(pos_ref, own_ref, l0, l1, l2, o_ref):
        o_ref[...] = ((own_ref[...] + l0[...].astype(F32)) + l1[...].astype(F32)) + l2[...].astype(F32)

    landed_specs = [pl.BlockSpec((tile, cols), lambda i, pos, _k=k: (_k * steps + i, 0)) for k in range(N_CHIPS - 1)]
    return pl.pallas_call(
        body,
        name=name,
        grid_spec=pltpu.PrefetchScalarGridSpec(
            num_scalar_prefetch=1,
            grid=(steps,),
            in_specs=[pl.BlockSpec((tile, cols), lambda i, pos: (i, 0))] + landed_specs,
            out_specs=pl.BlockSpec((tile, cols), lambda i, pos: (pos[0] * steps + i, 0)),
        ),
        out_shape=jax.ShapeDtypeStruct((2 * half, cols), F32),
        compiler_params=_params(("parallel",)),
    )(pos, own, landed, landed, landed)


def _adamw(name, w, g, m, v):
    rows, cols = w.shape
    tile = rows if rows * cols <= 256 * 1024 else _row_tile(rows)

    def body(w_ref, g_ref, m_ref, v_ref, g_out_ref, d_ref, nm_ref, nv_ref):
        g = g_ref[...]
        g_out_ref[...] = g
        d_ref[...], nm_ref[...], nv_ref[...] = _adamw_update(w_ref[...], g, m_ref[...], v_ref[...])

    spec = _row_spec(tile, cols)
    return pl.pallas_call(
        body,
        name=name,
        grid=(rows // tile,),
        in_specs=[spec] * 4,
        out_specs=[spec] * 4,
        out_shape=[jax.ShapeDtypeStruct((rows, cols), F32)] * 4,
        compiler_params=_params(("parallel",)),
    )(w, g, m, v)


_SMALL = (
    ("v_ln_g", (D_GMLP,), 8),
    ("v_ln_b", (D_GMLP,), 8),
    ("w_spatial", (N_HEADS, CHUNK, CHUNK), 1024),
    ("b_spatial", (N_HEADS, CHUNK), 8),
    ("sinks", (N_HEADS,), 8),
    ("ln1_g", (D_MODEL,), 8),
    ("ln1_b", (D_MODEL,), 8),
    ("ln2_g", (D_MODEL,), 8),
    ("ln2_b", (D_MODEL,), 8),
    ("squared_error", (D_MODEL,), 8),
)
N_SMALL_PARAMS = len(_SMALL) - 1


def _pack_small(values):
    parts = []
    for (name, shape, rows), val in zip(_SMALL, values, strict=True):
        flat = val.reshape(-1).astype(F32)
        parts.append(jnp.pad(flat, (0, rows * LANES - flat.shape[0])).reshape(rows, LANES))
    parts.append(jnp.zeros((SMALL_ROWS - sum(rows for _, _, rows in _SMALL), LANES), F32))
    return jnp.concatenate(parts, axis=0)


def _adamw_small(g_slab, params, first, second):
    n = N_SMALL_PARAMS

    def pieces(shape):
        if len(shape) == 3:
            return [((0, h), h * shape[1], shape[1], shape[2]) for h in range(shape[0])]
        if len(shape) == 2:
            return [((0,), 0, shape[0], shape[1])]
        if shape[0] >= LANES:
            return [((slice(None), slice(r * LANES, (r + 1) * LANES)), r, 1, LANES) for r in range(shape[0] // LANES)]
        return [((slice(None), slice(0, shape[0])), 0, 1, shape[0])]

    def body(*refs):
        g_ref = refs[0]
        w_refs, m_refs, v_refs = refs[1 : 1 + n], refs[1 + n : 1 + 2 * n], refs[1 + 2 * n : 1 + 3 * n]
        outs = refs[1 + 3 * n :]
        row0 = 0
        for idx, (_, shape, rows) in enumerate(_SMALL[:n]):
            for where, first_row, n_rows, lanes in pieces(shape):
                g = g_ref[row0 + first_row : row0 + first_row + n_rows, 0:lanes]
                delta, nm, nv = _adamw_update(w_refs[idx][where], g, m_refs[idx][where], v_refs[idx][where])
                for group, val in enumerate((g, delta, nm, nv)):
                    outs[group * n + idx][where] = val
            row0 += rows

    vmem = pl.BlockSpec(memory_space=pltpu.VMEM)
    shapes = [jax.ShapeDtypeStruct(p.shape, F32) for p in params]
    outs = pl.pallas_call(
        body,
        name="adamw_small",
        in_specs=[vmem] * (1 + 3 * n),
        out_specs=[vmem] * (4 * n),
        out_shape=shapes * 4,
        compiler_params=_params(),
    )(g_slab, *params, *first, *second)
    return [list(outs[group * n : (group + 1) * n]) for group in range(4)]


def kernel(x, positions, w_in, v_ln_g, v_ln_b, w_spatial, b_spatial, sinks, w_out, ln1_g, ln1_b, w_ff1, w_ff2, ln2_g, ln2_b, loss_target, m_w_in, m_v_ln_g, m_v_ln_b, m_w_spatial, m_b_spatial, m_sinks, m_w_out, m_ln1_g, m_ln1_b, m_w_ff1, m_w_ff2, m_ln2_g, m_ln2_b, v_w_in, v_v_ln_g, v_v_ln_b, v_w_spatial, v_b_spatial, v_sinks, v_w_out, v_ln1_g, v_ln1_b, v_w_ff1, v_w_ff2, v_ln2_g, v_ln2_b):
    t = x.shape[1]
    x2 = x.reshape(t, D_MODEL)
    target = loss_target.reshape(t, D_MODEL)

    w_in_shard = w_in[0].T.astype(BF16)
    in_direct, in_pass = _direct_gather_plans([w_in_shard.shape[0]])
    in_bufs, in_started, in_send, in_recv = _split_call(
        "gather_w_in_start", [w_in_shard, _landing(N_CHIPS * w_in_shard.shape[0], D_MODEL, BF16)], start=in_direct)
    inv_freq = ROPE_THETA ** (-jnp.arange(0, HEAD_DIM, 2, dtype=F32) / HEAD_DIM)
    cos, sin, later = _rope_tables_and_casts(
        positions, jnp.tile(inv_freq, LANES // (HEAD_DIM // 2)).reshape(1, LANES), [w_out[0], w_ff1[0], w_ff2[0]], dep=in_started)
    later_rows = [s.shape[0] for s in later]
    direct_plan, pass_plan, diagonal_plan = _gather_plans(later_rows)
    bufs, started, direct_send, direct_recv = _split_call(
        "gather_start", later + [_landing(N_CHIPS * r, D_MODEL, BF16) for r in later_rows], start=direct_plan, after=cos)
    in_bufs, in_passing, in_pass_send, in_pass_recv = _split_call(
        "gather_w_in_pass", in_bufs, wait=(in_direct, in_send, in_recv), start=in_pass, after=started)
    in_bufs, _ = _split_call("gather_w_in_end", in_bufs, wait=(in_pass, in_pass_send, in_pass_recv), after=in_passing)
    w_in_t = in_bufs[1]

    u, vg, q, k, va = _in_proj(x2, w_in_t, cos, sin)
    bias_full = jnp.repeat(b_spatial[0].T, HEAD_DIM, axis=1)
    sink_vec = sinks.reshape(N_HEADS)
    bufs, passing, pass_send, pass_recv = _split_call(
        "gather_pass", bufs, wait=(direct_plan, direct_send, direct_recv), start=pass_plan, after=u)
    cat = _mixer_fwd(u, vg, q, k, va, v_ln_g, v_ln_b, w_spatial[0], bias_full, sink_vec, dep=passing)
    bufs, passing, diag_send, diag_recv = _split_call(
        "gather_pass_diagonal", bufs, wait=(pass_plan, pass_send, pass_recv), start=diagonal_plan, after=cat)
    bufs, _ = _split_call("gather_end", bufs, wait=(diagonal_plan, diag_send, diag_recv), after=passing)
    w_out_all = bufs[3]
    w1_all = bufs[4].reshape(N_FF_BLOCKS, D_MODEL, D_MODEL)
    w2_all = bufs[5].reshape(N_FF_BLOCKS, D_MODEL, D_MODEL)
    xhat1, rstd1, x1b, r, dz2, dz2b, d_ln2_g, d_ln2_b, sq_err = _ffn_fwd_loss(
        cat, x2, w_out_all, ln1_g, ln1_b, w1_all, w2_all, ln2_g, ln2_b, target)

    pos = jnp.stack([lax.axis_index("c"), 2 * lax.axis_index("x") + lax.axis_index("y")]).astype(jnp.int32)
    half_landing = lambda g: _landing(g.shape[0] // 2, D_MODEL, F32)
    ff_swap_plan = _swap_plan([D_FF // N_CHIPS])
    ff_exchange_plan = _exchange_plan([D_FF // N_CHIPS // 2])
    exchange_landing = lambda p: _landing(3 * p.shape[0] // N_CHIPS, D_MODEL, BF16)
    g_ff1_local, dz1, dz1b, dcat, d_ln1_g, d_ln1_b = _ffn_bwd_ln1(dz2, r, x1b, xhat1, rstd1, ln1_g, w1_all, w2_all, w_out_all)
    ff1_bufs, swapping1, swap1_send, swap1_recv = _split_call("ff1_swap_start", [g_ff1_local, half_landing(g_ff1_local)], start=ff_swap_plan)
    dh_main, dkv, g_ff2_local, d_v_ln_g, d_v_ln_b, d_w_spatial, d_b_spatial_t, d_sinks = _mixer_bwd(
        u, vg, q, k, va, dcat, cos, sin, v_ln_g, v_ln_b, w_spatial[0], bias_full, sink_vec, r, dz2b, dep=swapping1)
    ff1_bufs, _ = _split_call("ff1_swap_wait", ff1_bufs, wait=(ff_swap_plan, swap1_send, swap1_recv), after=dh_main)
    ff1_sum, ff1_own = _pair_sum("grad_pair_sum_w_ff1", ff1_bufs[0], ff1_bufs[1], pos)
    ff1_ex, exchanging1, ex1_send, ex1_recv = _split_call(
        "ff1_exchange_start", [ff1_sum, exchange_landing(ff1_sum)], start=ff_exchange_plan)
    ff2_bufs, swapping2, swap2_send, swap2_recv = _split_call(
        "ff2_swap_start", [g_ff2_local, half_landing(g_ff2_local)], start=ff_swap_plan, after=exchanging1)
    g_out_local, small_g = _grad_w_out_and_small_all_reduce(cat, dz1b, _pack_small(
        [d_v_ln_g, d_v_ln_b, d_w_spatial, d_b_spatial_t[:, :N_HEADS].T, d_sinks[0, :N_HEADS], d_ln1_g, d_ln1_b, d_ln2_g, d_ln2_b, sq_err]),
        dep=swapping2)
    sq_row = sum(rows for _, _, rows in _SMALL[:N_SMALL_PARAMS])
    loss = 0.5 * jnp.sum(small_g[sq_row : sq_row + _SMALL[N_SMALL_PARAMS][2]]) / D_MODEL
    ff2_bufs, _ = _split_call("ff2_swap_wait", ff2_bufs, wait=(ff_swap_plan, swap2_send, swap2_recv), after=g_out_local)
    ff2_sum, ff2_own = _pair_sum("grad_pair_sum_w_ff2", ff2_bufs[0], ff2_bufs[1], pos)
    ff2_ex, exchanging2, ex2_send, ex2_recv = _split_call(
        "ff2_exchange_start", [ff2_sum, exchange_landing(ff2_sum)], start=ff_exchange_plan)
    grad_x_flat, g_in_local = _grad_x_and_w_in_t(dh_main, dkv, dz1, x2, w_in_t, dep=exchanging2)
    grad_x = grad_x_flat.reshape(1, t, D_MODEL)

    small = [g_in_local, g_out_local]
    small_swap_plan = _swap_plan([g.shape[0] // N_CHIPS for g in small])
    swap_bufs, small_swapping, ss_send, ss_recv = _split_call(
        "small_swap_start", small + [half_landing(g) for g in small], start=small_swap_plan)
    ff1_ex, _ = _split_call("ff1_exchange_wait", ff1_ex, wait=(ff_exchange_plan, ex1_send, ex1_recv), after=small_swapping)
    half_ff1 = _chip_sum("grad_chip_sum_w_ff1", ff1_own, ff1_ex[1], pos)
    swap_bufs, _ = _split_call("small_swap_wait", swap_bufs, wait=(small_swap_plan, ss_send, ss_recv), after=half_ff1)
    pair_sums = [_pair_sum("grad_pair_sum_" + nm, g, th, pos) for nm, g, th in zip(["w_in", "w_out"], swap_bufs[:2], swap_bufs[2:])]
    small_plan = _exchange_plan([p.shape[0] // N_CHIPS for p, _ in pair_sums])
    small_bufs, small_exchanging, sm_send, sm_recv = _split_call(
        "small_exchange_start", [p for p, _ in pair_sums] + [exchange_landing(p) for p, _ in pair_sums], start=small_plan)

    ff_pair_plan = _sibling_plan([D_FF // N_CHIPS])
    (half_ff1, *ff2_ex), _, g1_send, g1_recv = _split_call(
        "ff2_exchange_wait_ff1_pair_start", [half_ff1] + ff2_ex, wait=(_shifted(ff_exchange_plan, 1), ex2_send, ex2_recv), start=ff_pair_plan,
        after=small_exchanging)
    half_ff2 = _chip_sum("grad_chip_sum_w_ff2", ff2_own, ff2_ex[1], pos)
    (half_ff2, g_w_ff1), _, g2_send, g2_recv = _split_call(
        "ff1_pair_wait_ff2_pair_start", [half_ff2, half_ff1], wait=(_shifted(ff_pair_plan, 1), g1_send, g1_recv), start=ff_pair_plan)
    g_w_ff1, d_w_ff1, nm_w_ff1, nv_w_ff1 = _adamw("adamw_w_ff1", w_ff1[0], g_w_ff1, m_w_ff1[0], v_w_ff1[0])
    small_bufs, _ = _split_call("small_exchange_wait", small_bufs, wait=(small_plan, sm_send, sm_recv), after=nv_w_ff1)
    shards = [_chip_sum("grad_chip_sum_" + nm, own, ld, pos) for nm, (_, own), ld in zip(["w_in", "w_out"], pair_sums, small_bufs[2:])]
    small_pair_plan = _sibling_plan([s.shape[0] for s in shards])
    (*shards, g_w_ff2), _, g3_send, g3_recv = _split_call(
        "ff2_pair_wait_small_pair_start", shards + [half_ff2], wait=(_shifted(ff_pair_plan, 2), g2_send, g2_recv), start=small_pair_plan)
    g_w_ff2, d_w_ff2, nm_w_ff2, nv_w_ff2 = _adamw("adamw_w_ff2", w_ff2[0], g_w_ff2, m_w_ff2[0], v_w_ff2[0])
    (g_w_in_t, g_w_out), _ = _split_call("small_pair_wait", shards, wait=(small_pair_plan, g3_send, g3_recv), after=nv_w_ff2)
    g_w_in, d_w_in, nm_w_in, nv_w_in = (a.T for a in _adamw("adamw_w_in", w_in[0].T, g_w_in_t, m_w_in[0].T, v_w_in[0].T))
    g_w_out, d_w_out, nm_w_out, nv_w_out = _adamw("adamw_w_out", w_out[0], g_w_out, m_w_out[0], v_w_out[0])
    small_grads, small_d, small_nm, small_nv = _adamw_small(
        small_g,
        [v_ln_g, v_ln_b, w_spatial, b_spatial, sinks, ln1_g, ln1_b, ln2_g, ln2_b],
        [m_v_ln_g, m_v_ln_b, m_w_spatial, m_b_spatial, m_sinks, m_ln1_g, m_ln1_b, m_ln2_g, m_ln2_b],
        [v_v_ln_g, v_v_ln_b, v_w_spatial, v_b_spatial, v_sinks, v_ln1_g, v_ln1_b, v_ln2_g, v_ln2_b])

    def with_big(small, w_in_v, w_out_v, w_ff1_v, w_ff2_v):
        g_vg, g_vb, g_ws, g_bs, g_sk, g_1g, g_1b, g_2g, g_2b = small
        return [w_in_v[None], g_vg, g_vb, g_ws, g_bs, g_sk, w_out_v[None], g_1g, g_1b, w_ff1_v[None], w_ff2_v[None], g_2g, g_2b]

    return (
        loss,
        grad_x,
        *with_big(small_grads, g_w_in, g_w_out, g_w_ff1, g_w_ff2),
        *with_big(small_d, d_w_in, d_w_out, d_w_ff1, d_w_ff2),
        *with_big(small_nm, nm_w_in, nm_w_out, nm_w_ff1, nm_w_ff2),
        *with_big(small_nv, nv_w_in, nv_w_out, nv_w_ff1, nv_w_ff2),
    )
```

```python
import math

import jax
import jax.numpy as jnp
from jax import lax
from jax.experimental import pallas as pl
from jax.experimental.pallas import tpu as pltpu

F32 = jnp.float32
BF16 = jnp.bfloat16

D_MODEL = 1024
HEAD_DIM = 64
D_GMLP = 512
D_ATTN = 512
D_KV = 128
D_IN = 2 * D_GMLP + D_ATTN + 2 * D_KV
D_MAIN = 2 * D_GMLP + D_ATTN
N_HEADS = 8
CHUNK = 128
CHUNKS_PER_STEP = 4
ROPE_THETA = 10000.0
D_FF = 4 * D_MODEL
N_FF_BLOCKS = 4
LN_EPS = 1e-5
ALPHA = (2.0 * 1) ** 0.25
NEG_INF = -1e30
SCALE = 1.0 / math.sqrt(HEAD_DIM)

ADAM_LR = 0.001
ADAM_B1 = 0.9
ADAM_B2 = 0.999
ADAM_EPS = 1e-08
ADAM_WD = 0.01
ADAM_STEP = 10

N_CHIPS = 4
LANES = 128
V7X_VMEM_BYTES = 64 * 1024 * 1024
VMEM_LIMIT = V7X_VMEM_BYTES - 8 * 1024 * 1024
TM = 512
TM_FFN = 256
TM_FFN_FWD = 512
FFN_PART = 256
TK = 1024
SMALL_ROWS = 1152
MESH = pl.DeviceIdType.MESH

NT = (((1,), (1,)), ((), ()))
TN = (((0,), (0,)), ((), ()))


def _dot(a, b, dims=None):
    if dims is None:
        return jnp.dot(a, b, preferred_element_type=F32)
    return lax.dot_general(a, b, dims, preferred_element_type=F32)


def _params(semantics=None):
    return pltpu.CompilerParams(dimension_semantics=semantics, vmem_limit_bytes=VMEM_LIMIT)


def _const_spec(shape, single_buffer=False):
    zeros = (0,) * len(shape)
    if single_buffer:
        return pl.BlockSpec(shape, lambda *_: zeros, pipeline_mode=pl.Buffered(1))
    return pl.BlockSpec(shape, lambda *_: zeros)


def _row_spec(rows, cols):
    return pl.BlockSpec((rows, cols), lambda i: (i, 0))


def _after(dep, body, in_specs, operands):
    if dep is None:
        return body, list(in_specs), list(operands)
    return (lambda dep_ref, *refs: body(*refs)), [pl.BlockSpec(memory_space=pl.ANY)] + list(in_specs), [dep] + list(operands)


def _gelu(x):
    k = math.sqrt(2.0 / math.pi)
    return 0.5 * x * (1.0 + jnp.tanh(k * (x + 0.044715 * (x * x * x))))


def _gelu_and_grad(x):
    k = math.sqrt(2.0 / math.pi)
    x2 = x * x
    t = jnp.tanh(k * (x + 0.044715 * (x2 * x)))
    g = 0.5 * x * (1.0 + t)
    dg = 0.5 * (1.0 + t) + 0.5 * x * (1.0 - t * t) * (k * (1.0 + 3.0 * 0.044715 * x2))
    return g, dg


def _layer_norm_stats(z):
    mu = jnp.mean(z, axis=-1, keepdims=True)
    zc = z - mu
    var = jnp.mean(zc * zc, axis=-1, keepdims=True)
    rstd = lax.rsqrt(var + LN_EPS)
    return zc * rstd, rstd


def _layer_norm_bwd(dxhat, xhat, rstd):
    m1 = jnp.mean(dxhat, axis=-1, keepdims=True)
    m2 = jnp.mean(dxhat * xhat, axis=-1, keepdims=True)
    return rstd * (dxhat - m1 - xhat * m2)


def _rotate_half(t):
    n = t.shape[1]
    lane = lax.broadcasted_iota(jnp.int32, t.shape, 1)
    first = (lane & (HEAD_DIM // 2)) == 0
    return jnp.where(first, -pltpu.roll(t, n - HEAD_DIM // 2, 1), pltpu.roll(t, HEAD_DIM // 2, 1))


def _rope(t, cos, sin):
    return t * cos + _rotate_half(t) * sin


def _rope_transposed(g, cos, sin):
    return g * cos - _rotate_half(g * sin)


def _lane_tile(a, reps):
    return jnp.tile(a, (1, reps)) if reps > 1 else a


def _in_proj(x, w_in_t, cos, sin, dep=None):
    t = x.shape[0]

    def body(x_ref, w_ref, cos_ref, sin_ref, u_ref, vg_ref, q_ref, k_ref, va_ref):
        xb = x_ref[...].astype(BF16)
        u_ref[...] = _dot(xb, w_ref[0:D_GMLP, :], NT)
        vg_ref[...] = _dot(xb, w_ref[D_GMLP : 2 * D_GMLP, :], NT)
        q = _dot(xb, w_ref[2 * D_GMLP : D_MAIN, :], NT)
        k = _dot(xb, w_ref[D_MAIN : D_MAIN + D_KV, :], NT)
        va_ref[...] = _dot(xb, w_ref[D_MAIN + D_KV : D_IN, :], NT).astype(BF16)
        c, s = cos_ref[...], sin_ref[...]
        q_ref[...] = _rope(q, _lane_tile(c, D_ATTN // LANES), _lane_tile(s, D_ATTN // LANES)).astype(BF16)
        k_ref[...] = _rope(k, c, s).astype(BF16)

    body, in_specs, operands = _after(
        dep, body, [_row_spec(TM, D_MODEL), _const_spec((D_IN, D_MODEL)), _row_spec(TM, LANES), _row_spec(TM, LANES)], [x, w_in_t, cos, sin])
    return pl.pallas_call(
        body,
        name="in_proj",
        grid=(t // TM,),
        in_specs=in_specs,
        out_specs=[_row_spec(TM, D_GMLP), _row_spec(TM, D_GMLP), _row_spec(TM, D_ATTN), _row_spec(TM, D_KV), _row_spec(TM, D_KV)],
        out_shape=[
            jax.ShapeDtypeStruct((t, D_GMLP), F32),
            jax.ShapeDtypeStruct((t, D_GMLP), F32),
            jax.ShapeDtypeStruct((t, D_ATTN), BF16),
            jax.ShapeDtypeStruct((t, D_KV), BF16),
            jax.ShapeDtypeStruct((t, D_KV), BF16),
        ],
        compiler_params=_params(("parallel",)),
    )(*operands)


def _step_rows(i):
    return (i, 0)


def _chunk_before_step(i):
    return (jnp.maximum(CHUNKS_PER_STEP * i - 1, 0), 0)


def _chunk_specs():
    step = CHUNKS_PER_STEP * CHUNK
    return [
        pl.BlockSpec((step, D_GMLP), _step_rows),
        pl.BlockSpec((step, D_GMLP), _step_rows),
        pl.BlockSpec((step, D_ATTN), _step_rows),
        pl.BlockSpec((step, D_KV), _step_rows),
        pl.BlockSpec((CHUNK, D_KV), _chunk_before_step),
        pl.BlockSpec((step, D_KV), _step_rows),
        pl.BlockSpec((CHUNK, D_KV), _chunk_before_step),
    ]


def _half_lane_masks(rows):
    lane = lax.broadcasted_iota(jnp.int32, (rows, LANES), 1)
    return lane < HEAD_DIM


def _kv_variants(kv2):
    left = _half_lane_masks(kv2.shape[0])
    f = kv2.astype(F32)
    swapped = pltpu.roll(f, HEAD_DIM, 1)
    zero = jnp.zeros_like(f)
    g0 = (jnp.where(left, f, zero).astype(BF16), jnp.where(left, zero, swapped).astype(BF16))
    g1 = (jnp.where(left, swapped, zero).astype(BF16), jnp.where(left, zero, f).astype(BF16))
    return (g0, g1)


def _band_mask(i, heads=1):
    row = lax.broadcasted_iota(jnp.int32, (heads * CHUNK, 2 * CHUNK), 0) & (CHUNK - 1)
    col = lax.broadcasted_iota(jnp.int32, (heads * CHUNK, 2 * CHUNK), 1)
    no_prev = jnp.where(i > 0, 0, 4 * CHUNK)
    in_prev = jnp.logical_and(col < CHUNK, (col - row) > no_prev)
    in_cur = jnp.logical_and(col >= CHUNK, (col - CHUNK) <= row)
    return jnp.logical_or(in_prev, in_cur)


def _causal_mask():
    row = lax.broadcasted_iota(jnp.int32, (CHUNK, CHUNK), 0)
    col = lax.broadcasted_iota(jnp.int32, (CHUNK, CHUNK), 1)
    return col <= row


def _store_spatial_weights(w_ref, wcat_ref, wcat_t_ref=None):
    causal = _causal_mask()
    for p in range(D_GMLP // LANES):
        wl = jnp.where(causal, w_ref[2 * p], 0.0)
        wr = jnp.where(causal, w_ref[2 * p + 1], 0.0)
        wcat_ref[p] = jnp.concatenate([wl, wr], axis=1).astype(BF16)
        if wcat_t_ref is not None:
            wcat_t_ref[p] = jnp.concatenate([wl.T, wr.T], axis=1).astype(BF16)


def _pair_stack(xp, left):
    return jnp.concatenate([jnp.where(left, xp, 0.0), jnp.where(left, 0.0, xp)], axis=0).astype(BF16)


def _mixer_fwd(u, vg, q, k, va, v_ln_g, v_ln_b, w_spatial, bias_full, sinks, dep=None):
    t = u.shape[0]

    def body(u_ref, vg_ref, q_ref, kc_ref, kp_ref, vc_ref, vp_ref, g_ref, b_ref, w_ref, bias_ref, sink_ref, cat_ref, wcat):
        i = pl.program_id(0)
        left = _half_lane_masks(CHUNK)

        @pl.when(i == 0)
        def _():
            _store_spatial_weights(w_ref, wcat)

        heads = range(N_HEADS)
        pair_cols = [slice(p * LANES, (p + 1) * LANES) for p in range(D_GMLP // LANES)]
        sinks_h = [sink_ref[h] for h in heads]
        for c in range(CHUNKS_PER_STEP):
            rows = slice(c * CHUNK, (c + 1) * CHUNK)
            before = slice((c - 1) * CHUNK, c * CHUNK)
            k_prev = kp_ref[...] if c == 0 else kc_ref[before, :]
            v_prev = vp_ref[...] if c == 0 else vc_ref[before, :]
            k_var = _kv_variants(jnp.concatenate([k_prev, kc_ref[rows, :]], axis=0))
            v_var = _kv_variants(jnp.concatenate([v_prev, vc_ref[rows, :]], axis=0))
            scores = [_dot(q_ref[rows, pair_cols[h // 2]], k_var[h // 4][h % 2], NT) for h in heads]

            ug = _gelu(u_ref[rows, :])
            xhat, _ = _layer_norm_stats(_gelu(vg_ref[rows, :]))
            vgl = xhat * g_ref[...] + b_ref[...]
            mixed = [_dot(wcat[p], _pair_stack(vgl[:, cols], left)) for p, cols in enumerate(pair_cols)]

            valid = _band_mask(CHUNKS_PER_STEP * i + c)
            masked = [jnp.where(valid, scores[h] * SCALE, NEG_INF) for h in heads]
            maxes = [jnp.maximum(jnp.max(masked[h], axis=1, keepdims=True), sinks_h[h]) for h in heads]
            exps = [jnp.exp(masked[h] - maxes[h]) for h in heads]
            invs = [1.0 / (jnp.sum(exps[h], axis=1, keepdims=True) + jnp.exp(sinks_h[h] - maxes[h])) for h in heads]
            probs = [(exps[h] * invs[h]).astype(BF16) for h in heads]
            for p, cols in enumerate(pair_cols):
                cat_ref[rows, cols] = (ug[:, cols] * (mixed[p] + bias_ref[:, cols])).astype(BF16)
            for p in range(D_ATTN // LANES):
                out = _dot(probs[2 * p], v_var[p // 2][0]) + _dot(probs[2 * p + 1], v_var[p // 2][1])
                cat_ref[rows, D_GMLP + p * LANES : D_GMLP + (p + 1) * LANES] = out.astype(BF16)

    in_specs = _chunk_specs() + [
        _const_spec((1, D_GMLP)),
        _const_spec((1, D_GMLP)),
        _const_spec((N_HEADS, CHUNK, CHUNK)),
        _const_spec((CHUNK, D_GMLP)),
        pl.BlockSpec(memory_space=pltpu.SMEM),
    ]
    body, in_specs, operands = _after(dep, body, in_specs, [u, vg, q, k, k, va, va, v_ln_g, v_ln_b, w_spatial, bias_full, sinks])
    return pl.pallas_call(
        body,
        name="mixer_fwd",
        grid=(t // (CHUNKS_PER_STEP * CHUNK),),
        in_specs=in_specs,
        out_specs=pl.BlockSpec((CHUNKS_PER_STEP * CHUNK, D_MODEL), lambda i: (i, 0)),
        out_shape=jax.ShapeDtypeStruct((t, D_MODEL), BF16),
        scratch_shapes=[pltpu.VMEM((D_GMLP // LANES, CHUNK, 2 * CHUNK), BF16)],
        compiler_params=_params(("arbitrary",)),
    )(*operands)


def _ffn_fwd_loss(cat, x, w_out, ln1_g, ln1_b, w1, w2, ln2_g, ln2_b, target):
    t = x.shape[0]

    def body(cat_ref, x_ref, wo_ref, g1_ref, b1_ref, w1_ref, w2_ref, g2_ref, b2_ref, tgt_ref,
             xh_ref, rstd_ref, x1b_ref, r_ref, dz2_ref, dz2b_ref, dg2_ref, db2_ref, sq_ref):
        @pl.when(pl.program_id(0) == 0)
        def _():
            dg2_ref[...] = jnp.zeros_like(dg2_ref)
            db2_ref[...] = jnp.zeros_like(db2_ref)
            sq_ref[...] = jnp.zeros_like(sq_ref)

        parts = [slice(p * FFN_PART, (p + 1) * FFN_PART) for p in range(TM_FFN_FWD // FFN_PART)]

        def norm1(rows, z):
            xhat1, rstd1 = _layer_norm_stats(z)
            xh_ref[rows, :] = xhat1
            rstd_ref[rows, :] = rstd1
            x1 = xhat1 * g1_ref[...] + b1_ref[...]
            x1b = x1.astype(BF16)
            x1b_ref[rows, :] = x1b
            return x1, x1b

        def feed_forward(rows, x1b, pre):
            ff = None
            for j in range(N_FF_BLOCKS):
                r = jnp.maximum(pre, 0.0)
                r_ref[rows, j * D_MODEL : (j + 1) * D_MODEL] = r.astype(BF16)
                part = _dot((r * r).astype(BF16), w2_ref[j])
                ff = part if ff is None else ff + part
                if j + 1 < N_FF_BLOCKS:
                    pre = _dot(x1b, w1_ref[j + 1])
            return ff

        def norm2_and_loss(rows, x1, ff):
            xhat2, rstd2 = _layer_norm_stats(ALPHA * x1 + ff)
            err = xhat2 * g2_ref[...] + b2_ref[...] - tgt_ref[rows, :]
            sq_ref[...] += jnp.sum(err * err, axis=0, keepdims=True)
            dy = err * (1.0 / D_MODEL)
            dg2_ref[...] += jnp.sum(dy * xhat2, axis=0, keepdims=True)
            db2_ref[...] += jnp.sum(dy, axis=0, keepdims=True)
            dz2 = _layer_norm_bwd(dy * g2_ref[...], xhat2, rstd2)
            dz2_ref[rows, :] = dz2
            dz2b_ref[rows, :] = dz2.astype(BF16)

        projected = _dot(cat_ref[parts[0], :], wo_ref[...])
        last = None
        for i, rows in enumerate(parts):
            z = ALPHA * x_ref[rows, :] + projected
            if i + 1 < len(parts):
                projected = _dot(cat_ref[parts[i + 1], :], wo_ref[...])
            x1, x1b = norm1(rows, z)
            pre = _dot(x1b, w1_ref[0])
            if last is not None:
                norm2_and_loss(*last)
            last = (rows, x1, feed_forward(rows, x1b, pre))
        norm2_and_loss(*last)

    vec = _const_spec((1, D_MODEL))
    tile = _row_spec(TM_FFN_FWD, D_MODEL)
    wspec = _const_spec((N_FF_BLOCKS, D_MODEL, D_MODEL), single_buffer=True)
    return pl.pallas_call(
        body,
        name="ffn_fwd_loss",
        grid=(t // TM_FFN_FWD,),
        in_specs=[tile, tile, _const_spec((D_MODEL, D_MODEL), single_buffer=True), vec, vec, wspec, wspec, vec, vec, tile],
        out_specs=[tile, _row_spec(TM_FFN_FWD, 1), tile, _row_spec(TM_FFN_FWD, D_FF), tile, tile, vec, vec, vec],
        out_shape=[
            jax.ShapeDtypeStruct((t, D_MODEL), F32),
            jax.ShapeDtypeStruct((t, 1), F32),
            jax.ShapeDtypeStruct((t, D_MODEL), BF16),
            jax.ShapeDtypeStruct((t, D_FF), BF16),
            jax.ShapeDtypeStruct((t, D_MODEL), F32),
            jax.ShapeDtypeStruct((t, D_MODEL), BF16),
            jax.ShapeDtypeStruct((1, D_MODEL), F32),
            jax.ShapeDtypeStruct((1, D_MODEL), F32),
            jax.ShapeDtypeStruct((1, D_MODEL), F32),
        ],
        compiler_params=_params(("arbitrary",)),
    )(cat, x, w_out, ln1_g, ln1_b, w1, w2, ln2_g, ln2_b, target)


def _ffn_bwd_ln1(dz2, r, x1b, xhat1, rstd1, ln1_g, w1, w2, w_out, dep=None):
    t = dz2.shape[0]

    def body(dz2_ref, r_ref, x1b_ref, xh_ref, rstd_ref, g1_ref, w1_ref, w2_ref, wo_ref, gw1_ref, dz1_ref, dz1b_ref, dcat_ref, dg1_ref, db1_ref):
        @pl.when(pl.program_id(0) == 0)
        def _():
            dg1_ref[...] = jnp.zeros_like(dg1_ref)
            db1_ref[...] = jnp.zeros_like(db1_ref)
            gw1_ref[...] = jnp.zeros_like(gw1_ref)

        dz2 = dz2_ref[...]
        dz2b = dz2.astype(BF16)
        x1_t = x1b_ref[...].astype(F32).T.astype(BF16)
        dx1 = ALPHA * dz2
        for j in range(N_FF_BLOCKS):
            cols = slice(j * D_MODEL, (j + 1) * D_MODEL)
            dpre = (_dot(dz2b, w2_ref[j], NT) * (2.0 * r_ref[:, cols].astype(F32))).astype(BF16)
            gw1_ref[cols, :] += _dot(x1_t, dpre)
            dx1 = dx1 + _dot(dpre, w1_ref[j], NT)
        xhat1 = xh_ref[...]
        dg1_ref[...] += jnp.sum(dx1 * xhat1, axis=0, keepdims=True)
        db1_ref[...] += jnp.sum(dx1, axis=0, keepdims=True)
        dz1 = _layer_norm_bwd(dx1 * g1_ref[...], xhat1, rstd_ref[...])
        dz1_ref[...] = dz1
        dz1b = dz1.astype(BF16)
        dz1b_ref[...] = dz1b
        dcat_ref[...] = _dot(dz1b, wo_ref[...], NT).astype(BF16)

    vec = _const_spec((1, D_MODEL))
    tile = _row_spec(TM_FFN, D_MODEL)
    wspec = _const_spec((N_FF_BLOCKS, D_MODEL, D_MODEL), single_buffer=True)
    body, in_specs, operands = _after(
        dep, body,
        [tile, _row_spec(TM_FFN, D_FF), tile, tile, _row_spec(TM_FFN, 1), vec, wspec, wspec, _const_spec((D_MODEL, D_MODEL), single_buffer=True)],
        [dz2, r, x1b, xhat1, rstd1, ln1_g, w1, w2, w_out])
    return pl.pallas_call(
        body,
        name="ffn_bwd_ln1",
        grid=(t // TM_FFN,),
        in_specs=in_specs,
        out_specs=[_const_spec((D_FF, D_MODEL), single_buffer=True), tile, tile, tile, vec, vec],
        out_shape=[
            jax.ShapeDtypeStruct((D_FF, D_MODEL), F32),
            jax.ShapeDtypeStruct((t, D_MODEL), F32),
            jax.ShapeDtypeStruct((t, D_MODEL), BF16),
            jax.ShapeDtypeStruct((t, D_MODEL), BF16),
            jax.ShapeDtypeStruct((1, D_MODEL), F32),
            jax.ShapeDtypeStruct((1, D_MODEL), F32),
        ],
        compiler_params=_params(("arbitrary",)),
    )(*operands)


def _mixer_bwd(u, vg, q, k, va, dcat, cos, sin, v_ln_g, v_ln_b, w_spatial, bias_full, sinks, r, dz2b, dep=None):
    t = u.shape[0]
    n_chunks = t // CHUNK
    assert CHUNKS_PER_STEP == N_FF_BLOCKS

    def body(u_ref, vg_ref, q_ref, kc_ref, kp_ref, vc_ref, vp_ref, dcat_ref, cosc_ref, sinc_ref, cosp_ref, sinp_ref,
             g_ref, b_ref, w_ref, bias_ref, sink_ref, r_ref, dz2b_ref,
             dmain_ref, dkv_ref, gw2_ref, dg_ref, db_ref, dw_ref, dbs_ref, dsink_ref, dmix_acc, wcat, wcat_t):
        i = pl.program_id(0)
        left = _half_lane_masks(CHUNK)
        lane = lax.broadcasted_iota(jnp.int32, (CHUNK, LANES), 1)
        n_pairs = D_GMLP // LANES

        @pl.when(i == 0)
        def _():
            dg_ref[...] = jnp.zeros_like(dg_ref)
            db_ref[...] = jnp.zeros_like(db_ref)
            dw_ref[...] = jnp.zeros_like(dw_ref)
            dsink_ref[...] = jnp.zeros_like(dsink_ref)
            dmix_acc[...] = jnp.zeros_like(dmix_acc)
            gw2_ref[...] = jnp.zeros_like(gw2_ref)
            _store_spatial_weights(w_ref, wcat, wcat_t)

        n_qpairs = D_ATTN // LANES
        heads = range(N_HEADS)
        pair_cols = [slice(p * LANES, (p + 1) * LANES) for p in range(n_pairs)]
        sinks_h = [sink_ref[h] for h in heads]
        gain = g_ref[...]
        causal = _causal_mask()
        lane_row = lax.broadcasted_iota(jnp.int32, (1, LANES), 1)
        heads_per_group = N_HEADS // 2

        def group_grad_t(lhs_t, rhs_heads):
            parts = []
            for g in range(2):
                group = range(g * heads_per_group, (g + 1) * heads_per_group)
                lhs = jnp.concatenate([lhs_t[h * HEAD_DIM : (h + 1) * HEAD_DIM] for h in group], axis=1)
                parts.append(_dot(lhs, jnp.concatenate([rhs_heads[h] for h in group], axis=0)))
            return jnp.concatenate(parts, axis=0)

        for c in range(CHUNKS_PER_STEP):
            chunk = CHUNKS_PER_STEP * i + c
            rows = slice(c * CHUNK, (c + 1) * CHUNK)
            before = slice((c - 1) * CHUNK, c * CHUNK)

            k_prev = kp_ref[...] if c == 0 else kc_ref[before, :]
            v_prev = vp_ref[...] if c == 0 else vc_ref[before, :]
            k_var = _kv_variants(jnp.concatenate([k_prev, kc_ref[rows, :]], axis=0))
            v_var = _kv_variants(jnp.concatenate([v_prev, vc_ref[rows, :]], axis=0))
            q_pairs = [q_ref[rows, cols] for cols in pair_cols]
            do_all = dcat_ref[rows, D_GMLP:D_MODEL]
            do_pairs = [do_all[:, cols] for cols in pair_cols]
            scores = [_dot(q_pairs[h // 2], k_var[h // 4][h % 2], NT) for h in heads]
            dprobs = [_dot(do_pairs[h // 2], v_var[h // 4][h % 2], NT) for h in heads]
            q_t = q_ref[rows, :].astype(F32).T.astype(BF16)
            do_t = do_all.astype(F32).T.astype(BF16)

            ff_cols = slice(c * D_MODEL, (c + 1) * D_MODEL)
            relu_block = r_ref[:, ff_cols]
            gw2_ref[ff_cols, :] += _dot(relu_block * relu_block, dz2b_ref[...], TN)

            ug, dug_du = _gelu_and_grad(u_ref[rows, :])
            gv, dgv_dv = _gelu_and_grad(vg_ref[rows, :])
            xhat, rstd = _layer_norm_stats(gv)
            vgl = xhat * gain + b_ref[...]
            mixed = [_dot(wcat[p], _pair_stack(vgl[:, cols], left)) for p, cols in enumerate(pair_cols)]

            valid = _band_mask(chunk)
            masked = [jnp.where(valid, scores[h] * SCALE, NEG_INF) for h in heads]
            maxes = [jnp.maximum(jnp.max(masked[h], axis=1, keepdims=True), sinks_h[h]) for h in heads]
            exps = [jnp.exp(masked[h] - maxes[h]) for h in heads]
            exp_sinks = [jnp.exp(sinks_h[h] - maxes[h]) for h in heads]
            invs = [1.0 / (jnp.sum(exps[h], axis=1, keepdims=True) + exp_sinks[h]) for h in heads]
            probs = [exps[h] * invs[h] for h in heads]
            dsums = [jnp.sum(probs[h] * dprobs[h], axis=1, keepdims=True) for h in heads]
            ds_b = [(probs[h] * (dprobs[h] - dsums[h]) * SCALE).astype(BF16) for h in heads]
            probs_b = [probs[h].astype(BF16) for h in heads]

            dm_stacks = []
            for p, cols in enumerate(pair_cols):
                da = dcat_ref[rows, cols].astype(F32)
                dmain_ref[rows, cols] = (da * (mixed[p] + bias_ref[:, cols]) * dug_du[:, cols]).astype(BF16)
                dmixed = da * ug[:, cols]
                dmix_acc[:, cols] += dmixed
                dm_stacks.append(_pair_stack(dmixed, left))

            dq_all = jnp.concatenate(
                [_dot(ds_b[2 * p], k_var[p // 2][0]) + _dot(ds_b[2 * p + 1], k_var[p // 2][1]) for p in range(n_qpairs)], axis=1)
            dk2_t = group_grad_t(q_t, ds_b)
            dv2_t = group_grad_t(do_t, probs_b)

            for p, cols in enumerate(pair_cols):
                dw_pair = _dot(dm_stacks[p], vgl[:, cols].astype(BF16), NT)
                dw_ref[2 * p] += jnp.where(causal, dw_pair[:CHUNK], 0.0)
                dw_ref[2 * p + 1] += jnp.where(causal, dw_pair[CHUNK:], 0.0)
            dvgl = jnp.concatenate([_dot(wcat_t[p], dm_stacks[p]) for p in range(n_pairs)], axis=1)

            dsink_row = jnp.zeros((1, LANES), F32)
            for h in heads:
                d_sink = -jnp.sum(exp_sinks[h] * invs[h] * dsums[h], axis=0, keepdims=True)
                dsink_row = dsink_row + jnp.where(lane_row == h, d_sink, 0.0)
            dsink_ref[0:1, :] += dsink_row
            cos_c, sin_c = cosc_ref[rows, :], sinc_ref[rows, :]
            cos_p = cosp_ref[...] if c == 0 else cosc_ref[before, :]
            sin_p = sinp_ref[...] if c == 0 else sinc_ref[before, :]
            dmain_ref[rows, 2 * D_GMLP : D_MAIN] = _rope_transposed(dq_all, _lane_tile(cos_c, n_qpairs), _lane_tile(sin_c, n_qpairs)).astype(BF16)
            dk2 = dk2_t.T
            dv2 = dv2_t.T
            cur = pl.ds(pl.multiple_of(chunk * CHUNK, CHUNK), CHUNK)
            dkv_ref[cur, 0:D_KV] = _rope_transposed(dk2[CHUNK:], cos_c, sin_c)
            dkv_ref[cur, D_KV : 2 * D_KV] = dv2[CHUNK:]
            prev = pl.ds(pl.multiple_of(jnp.maximum(chunk - 1, 0) * CHUNK, CHUNK), CHUNK)
            dkv_ref[prev, 0:D_KV] += _rope_transposed(dk2[:CHUNK], cos_p, sin_p)
            dkv_ref[prev, D_KV : 2 * D_KV] += dv2[:CHUNK]

            dg_ref[...] += jnp.sum(dvgl * xhat, axis=0, keepdims=True)
            db_ref[...] += jnp.sum(dvgl, axis=0, keepdims=True)
            dgv = _layer_norm_bwd(dvgl * gain, xhat, rstd)
            dmain_ref[rows, D_GMLP : 2 * D_GMLP] = (dgv * dgv_dv).astype(BF16)

        @pl.when(i == n_chunks // CHUNKS_PER_STEP - 1)
        def _():
            tile = jnp.zeros((CHUNK, LANES), F32)
            for p, cols in enumerate(pair_cols):
                dm = dmix_acc[:, cols]
                sl = jnp.sum(jnp.where(left, dm, 0.0), axis=1, keepdims=True)
                sr = jnp.sum(jnp.where(left, 0.0, dm), axis=1, keepdims=True)
                tile = jnp.where(lane == 2 * p, sl, tile)
                tile = jnp.where(lane == 2 * p + 1, sr, tile)
            dbs_ref[...] = tile

    step = CHUNKS_PER_STEP * CHUNK
    in_specs = _chunk_specs() + [
        pl.BlockSpec((step, D_MODEL), _step_rows),
        pl.BlockSpec((step, LANES), _step_rows),
        pl.BlockSpec((step, LANES), _step_rows),
        pl.BlockSpec((CHUNK, LANES), _chunk_before_step),
        pl.BlockSpec((CHUNK, LANES), _chunk_before_step),
        _const_spec((1, D_GMLP)),
        _const_spec((1, D_GMLP)),
        _const_spec((N_HEADS, CHUNK, CHUNK)),
        _const_spec((CHUNK, D_GMLP)),
        pl.BlockSpec(memory_space=pltpu.SMEM),
        pl.BlockSpec((step, D_FF), _step_rows),
        pl.BlockSpec((step, D_MODEL), _step_rows),
    ]
    body, in_specs, operands = _after(
        dep, body, in_specs, [u, vg, q, k, k, va, va, dcat, cos, sin, cos, sin, v_ln_g, v_ln_b, w_spatial, bias_full, sinks, r, dz2b])
    return pl.pallas_call(
        body,
        name="mixer_bwd",
        grid=(n_chunks // CHUNKS_PER_STEP,),
        in_specs=in_specs,
        out_specs=[
            pl.BlockSpec((step, D_MAIN), _step_rows),
            _const_spec((t, 2 * D_KV)),
            _const_spec((D_FF, D_MODEL), single_buffer=True),
            _const_spec((1, D_GMLP)),
            _const_spec((1, D_GMLP)),
            _const_spec((N_HEADS, CHUNK, CHUNK)),
            _const_spec((CHUNK, LANES)),
            _const_spec((8, LANES)),
        ],
        out_shape=[
            jax.ShapeDtypeStruct((t, D_MAIN), BF16),
            jax.ShapeDtypeStruct((t, 2 * D_KV), F32),
            jax.ShapeDtypeStruct((D_FF, D_MODEL), F32),
            jax.ShapeDtypeStruct((1, D_GMLP), F32),
            jax.ShapeDtypeStruct((1, D_GMLP), F32),
            jax.ShapeDtypeStruct((N_HEADS, CHUNK, CHUNK), F32),
            jax.ShapeDtypeStruct((CHUNK, LANES), F32),
            jax.ShapeDtypeStruct((8, LANES), F32),
        ],
        scratch_shapes=[
            pltpu.VMEM((CHUNK, D_GMLP), F32),
            pltpu.VMEM((D_GMLP // LANES, CHUNK, 2 * CHUNK), BF16),
            pltpu.VMEM((D_GMLP // LANES, CHUNK, 2 * CHUNK), BF16),
        ],
        compiler_params=_params(("arbitrary",)),
    )(*operands)


def _grad_x_and_w_in_t(dh_main, dkv, dz1, x, w_in_t, dep=None):
    t = dz1.shape[0]

    def body(dm_ref, dkv_ref, dz1_ref, x_ref, w_ref, gx_ref, grad_ref):
        @pl.when(pl.program_id(0) == 0)
        def _():
            grad_ref[...] = jnp.zeros_like(grad_ref)

        dh_main_tile = dm_ref[...]
        dh_kv_tile = dkv_ref[...].astype(BF16)
        acc = ALPHA * dz1_ref[...] + _dot(dh_main_tile, w_ref[0:D_MAIN, :])
        gx_ref[...] = acc + _dot(dh_kv_tile, w_ref[D_MAIN:D_IN, :])
        xb = x_ref[...].astype(BF16)
        grad_ref[0:D_MAIN, :] += _dot(dh_main_tile, xb, TN)
        grad_ref[D_MAIN:D_IN, :] += _dot(dh_kv_tile, xb, TN)

    tile = _row_spec(TM, D_MODEL)
    body, in_specs, operands = _after(
        dep, body, [_row_spec(TM, D_MAIN), _row_spec(TM, 2 * D_KV), tile, tile, _const_spec((D_IN, D_MODEL))], [dh_main, dkv, dz1, x, w_in_t])
    return pl.pallas_call(
        body,
        name="grad_x_and_w_in",
        grid=(t // TM,),
        in_specs=in_specs,
        out_specs=[tile, _const_spec((D_IN, D_MODEL), single_buffer=True)],
        out_shape=[jax.ShapeDtypeStruct((t, D_MODEL), F32), jax.ShapeDtypeStruct((D_IN, D_MODEL), F32)],
        compiler_params=_params(("arbitrary",)),
    )(*operands)


ANY = pl.BlockSpec(memory_space=pl.ANY)


def _mesh_position():
    return lax.axis_index("x"), lax.axis_index("y"), lax.axis_index("c")


def _other_chips(x, y):
    return [(1 - x, y), (x, 1 - y), (1 - x, 1 - y)]


def _remote(src, dst, send_sem, recv_sem, device):
    return pltpu.make_async_remote_copy(src_ref=src, dst_ref=dst, send_sem=send_sem, recv_sem=recv_sem, device_id=device, device_id_type=MESH)


def _rows(ref, start, size):
    return ref.at[pl.ds(start, size), :]


def _rope_tables_and_casts(pos_row, inv_freq_row, shards, dep=None):
    t = pos_row.shape[1]
    steps = t // TM
    n = len(shards)

    def body(pos_ref, f_ref, *rest):
        f32_refs, (cos_ref, sin_ref), bf16_refs = rest[:n], rest[n : n + 2], rest[n + 2 :]
        for src, dst in zip(f32_refs, bf16_refs):
            dst[...] = src[...].astype(BF16)
        pos_rows = jnp.broadcast_to(pos_ref[...].astype(F32), (LANES, TM)).T
        ang = pos_rows * f_ref[...]
        cos_ref[...] = jnp.cos(ang)
        sin_ref[...] = jnp.sin(ang)

    shard_specs = [_row_spec(s.shape[0] // steps, s.shape[1]) for s in shards]
    body, in_specs, operands = _after(
        dep, body, [pl.BlockSpec((1, TM), lambda i: (0, i)), _const_spec((1, LANES))] + shard_specs, [pos_row, inv_freq_row, *shards])
    outs = pl.pallas_call(
        body,
        name="rope_tables_and_casts",
        grid=(steps,),
        in_specs=in_specs,
        out_specs=[_row_spec(TM, LANES), _row_spec(TM, LANES)] + shard_specs,
        out_shape=[jax.ShapeDtypeStruct((t, LANES), F32)] * 2 + [jax.ShapeDtypeStruct(s.shape, BF16) for s in shards],
        compiler_params=_params(("parallel",)),
    )(*operands)
    return outs[0], outs[1], list(outs[2:])


def _small_all_reduce(slab_ref, sum_ref, landing, reduced, gathered, send_sems, recv_sems):
    part = slab_ref.shape[0] // 8
    x_, y_, c_ = _mesh_position()
    me = 4 * x_ + 2 * y_ + c_
    flips = [(f >> 2, (f >> 1) & 1, f & 1) for f in range(1, 8)]

    def peer(flip):
        fx, fy, fc = flip
        return (1 - x_ if fx else x_, 1 - y_ if fy else y_, 1 - c_ if fc else c_)

    def part_of(ref, device):
        return ref.at[pl.ds(pl.multiple_of(device * part, 8), part), :]

    def scatter_copies():
        out = []
        for kk, flip in enumerate(flips):
            px, py, pc = peer(flip)
            them = 4 * px + 2 * py + pc
            send = _remote(part_of(slab_ref, them), landing.at[me], send_sems.at[kk], recv_sems.at[kk], (px, py, pc))
            recv = _remote(landing.at[them], landing.at[them], send_sems.at[kk], recv_sems.at[kk], (px, py, pc))
            out.append((send, recv))
        return out

    def gather_copies():
        out = []
        for kk, flip in enumerate(flips):
            px, py, pc = peer(flip)
            them = 4 * px + 2 * py + pc
            send = _remote(reduced, part_of(gathered, me), send_sems.at[7 + kk], recv_sems.at[7 + kk], (px, py, pc))
            recv = _remote(part_of(gathered, them), part_of(gathered, them), send_sems.at[7 + kk], recv_sems.at[7 + kk], (px, py, pc))
            out.append((send, recv))
        return out

    def first():
        for send, _ in scatter_copies():
            send.start()
        landing[me] = part_of(slab_ref, me)[...]

    def middle():
        for _, recv in scatter_copies():
            recv.wait_recv()
        total = landing[0]
        for s in range(1, 8):
            total = total + landing[s]
        reduced[...] = total
        part_of(gathered, me)[...] = total
        for send, _ in gather_copies():
            send.start()

    def last():
        for send, recv in gather_copies():
            recv.wait_recv()
            send.wait_send()
        for send, _ in scatter_copies():
            send.wait_send()
        sum_ref[...] = gathered[...]

    return first, middle, last


def _small_all_reduce_scratch(slab):
    part = slab.shape[0] // 8
    return [
        pltpu.VMEM((8, part, LANES), F32),
        pltpu.VMEM((part, LANES), F32),
        pltpu.VMEM(slab.shape, F32),
        pltpu.SemaphoreType.DMA((14,)),
        pltpu.SemaphoreType.DMA((14,)),
    ]


def _grad_w_out_and_small_all_reduce(cat, dz1b, slab, dep=None):
    t = cat.shape[0]
    steps = t // TM

    def body(cat_ref, dz1_ref, slab_ref, grad_ref, sum_ref, *scratch):
        k = pl.program_id(0)
        first, middle, last = _small_all_reduce(slab_ref, sum_ref, *scratch)

        @pl.when(k == 0)
        def _():
            grad_ref[...] = jnp.zeros_like(grad_ref)
            first()

        pl.when(k == steps // 2)(middle)
        grad_ref[...] += _dot(cat_ref[...], dz1_ref[...], TN)
        pl.when(k == steps - 1)(last)

    tile = _row_spec(TM, D_MODEL)
    body, in_specs, operands = _after(dep, body, [tile, tile, _const_spec(slab.shape)], [cat, dz1b, slab])
    return pl.pallas_call(
        body,
        name="grad_w_out_and_small_all_reduce",
        grid=(steps,),
        in_specs=in_specs,
        out_specs=[_const_spec((D_MODEL, D_MODEL), single_buffer=True), _const_spec(slab.shape)],
        out_shape=[jax.ShapeDtypeStruct((D_MODEL, D_MODEL), F32), jax.ShapeDtypeStruct(slab.shape, slab.dtype)],
        scratch_shapes=_small_all_reduce_scratch(slab),
        compiler_params=_params(("arbitrary",)),
    )(*operands)


HBM = pl.BlockSpec(memory_space=pltpu.HBM)
SEM = pl.BlockSpec(memory_space=pltpu.SEMAPHORE)
DATAFLOW = pltpu.SideEffectType.DATAFLOW_SIDE_EFFECTING
TOKEN = jax.ShapeDtypeStruct((8, LANES), F32)


def _plan_copies(bufs, plan, send_sems, recv_sems):
    out = []
    for i, (src, src_row, dst, dst_row, recv_row, rows, device) in enumerate(plan):
        send = _remote(_rows(bufs[src], src_row, rows), _rows(bufs[dst], dst_row, rows), send_sems.at[i], recv_sems.at[i], device)
        landed = _rows(bufs[dst], recv_row, rows)
        recv = _remote(landed, landed, send_sems.at[i], recv_sems.at[i], device)
        out.append((send, recv))
    return out


def _split_call(name, bufs, wait=None, start=None, after=None):
    n = len(bufs)
    n_in = n + (2 if wait else 0) + (1 if after is not None else 0)
    n_start = len(start(0, 0, 0)) if start else 0

    def body(*refs):
        ins = refs[:n]
        x, y, c = _mesh_position()
        if wait:
            for send, recv in _plan_copies(ins, wait[0](x, y, c), refs[n], refs[n + 1]):
                recv.wait_recv()
                send.wait_send()
        if start:
            for send, _ in _plan_copies(ins, start(x, y, c), refs[n_in + n + 1], refs[n_in + n + 2]):
                send.start()
        token = refs[n_in + n]
        token[...] = jnp.zeros_like(token)

    operands = [pltpu.with_memory_space_constraint(b, pltpu.HBM) for b in bufs]
    in_specs = [HBM] * n
    if wait:
        operands += [wait[1], wait[2]]
        in_specs += [SEM, SEM]
    if after is not None:
        operands.append(after)
        in_specs.append(ANY)
    out_shape = [pltpu.HBM(b.shape, b.dtype) for b in bufs] + [TOKEN]
    out_specs = [HBM] * n + [pl.BlockSpec(memory_space=pltpu.VMEM)]
    if start:
        out_shape += [pltpu.SemaphoreType.DMA((n_start,)), pltpu.SemaphoreType.DMA((n_start,))]
        out_specs += [SEM, SEM]
    outs = pl.pallas_call(
        body,
        name=name,
        in_specs=in_specs,
        out_specs=out_specs,
        out_shape=out_shape,
        input_output_aliases={i: i for i in range(n)},
        compiler_params=pltpu.CompilerParams(has_side_effects=DATAFLOW),
    )(*operands)
    return (list(outs[:n]), outs[n]) + tuple(outs[n + 1 :])


def _direct_gather_plans(shard_rows):
    n = len(shard_rows)

    def direct(x, y, c):
        me = 2 * x + y
        plan = []
        for w, rows in enumerate(shard_rows):
            half = rows // 2
            for px, py in _other_chips(x, y):
                plan.append((w, c * half, n + w, me * rows + c * half, (2 * px + py) * rows + c * half, half, (px, py, c)))
            plan.append((w, 0, n + w, me * rows, me * rows, rows, (x, y, 1 - c)))
        return plan

    def passed_on(x, y, c):
        plan = []
        for w, rows in enumerate(shard_rows):
            half = rows // 2
            for px, py in _other_chips(x, y):
                row = (2 * px + py) * rows
                plan.append((n + w, row + c * half, n + w, row + c * half, row + (1 - c) * half, half, (x, y, 1 - c)))
        return plan

    return direct, passed_on


def _gather_plans(shard_rows):
    n = len(shard_rows)

    def neighbours(x, y):
        return ((1 - x, y), (x, 1 - y))

    def direct(x, y, c):
        me = 2 * x + y
        plan = []
        for w, rows in enumerate(shard_rows):
            half = rows // 2
            for px, py in neighbours(x, y):
                plan.append((w, c * half, n + w, me * rows + c * half, (2 * px + py) * rows + c * half, half, (px, py, c)))
            plan.append((w, 0, n + w, me * rows, me * rows, rows, (x, y, 1 - c)))
        return plan

    def passed_on(x, y, c):
        (xn, yn), diagonal = neighbours(x, y), 2 * (1 - x) + (1 - y)
        relayed = (1 - c) * (2 * xn[0] + xn[1]) + c * (2 * yn[0] + yn[1])
        target = (x * (1 - c) + (1 - x) * c, (1 - y) * (1 - c) + y * c, c)
        plan = []
        for w, rows in enumerate(shard_rows):
            half = rows // 2
            for px, py in (xn, yn):
                row = (2 * px + py) * rows
                plan.append((n + w, row + c * half, n + w, row + c * half, row + (1 - c) * half, half, (x, y, 1 - c)))
            plan.append((n + w, relayed * rows + c * half, n + w, relayed * rows + c * half, diagonal * rows + c * half, half, target))
        return plan

    def diagonal_passed_on(x, y, c):
        plan = []
        for w, rows in enumerate(shard_rows):
            half = rows // 2
            row = (2 * (1 - x) + (1 - y)) * rows
            plan.append((n + w, row + c * half, n + w, row + c * half, row + (1 - c) * half, half, (x, y, 1 - c)))
        return plan

    return direct, passed_on, diagonal_passed_on


def _swap_plan(block_rows):
    n = len(block_rows)

    def plan_fn(x, y, c):
        plan = []
        for w, rows in enumerate(block_rows):
            half = rows // 2
            for j in range(N_CHIPS):
                plan.append((w, j * rows + (1 - c) * half, n + w, j * half, j * half, half, (x, y, 1 - c)))
        return plan

    return plan_fn


def _exchange_plan(halves):
    n = len(halves)

    def plan_fn(x, y, c):
        plan = []
        for w, half in enumerate(halves):
            for kk, (px, py) in enumerate(_other_chips(x, y)):
                plan.append((w, (2 * px + py) * half, n + w, kk * half, kk * half, half, (px, py, c)))
        return plan

    return plan_fn


def _sibling_plan(shard_rows):
    def plan_fn(x, y, c):
        return [(w, c * (rows // 2), w, c * (rows // 2), (1 - c) * (rows // 2), rows // 2, (x, y, 1 - c)) for w, rows in enumerate(shard_rows)]

    return plan_fn


def _shifted(plan_fn, first):
    return lambda x, y, c: [(src + first, a, dst + first, b, r, n, dev) for src, a, dst, b, r, n, dev in plan_fn(x, y, c)]


def _landing(rows, cols, dtype):
    return lax.empty((rows, cols), dtype)


def _row_tile(rows, cap=512):
    best = 8
    for cand in range(8, cap + 1, 8):
        if rows % cand == 0:
            best = cand
    return best


def _pair_sum(name, grad, theirs, pos):
    half = theirs.shape[0] // N_CHIPS
    cols = theirs.shape[1]
    tile = _row_tile(half)
    steps = half // tile

    def body(pos_ref, g_ref, t_ref, p_ref, own_ref):
        total = g_ref[...] + t_ref[...]
        p_ref[...] = total.astype(BF16)

        @pl.when(pl.program_id(1) == pos_ref[1])
        def _():
            own_ref[...] = total

    return pl.pallas_call(
        body,
        name=name,
        grid_spec=pltpu.PrefetchScalarGridSpec(
            num_scalar_prefetch=1,
            grid=(steps, N_CHIPS),
            in_specs=[
                pl.BlockSpec((tile, cols), lambda i, j, pos: ((2 * j + pos[0]) * steps + i, 0)),
                pl.BlockSpec((tile, cols), lambda i, j, pos: (j * steps + i, 0)),
            ],
            out_specs=[
                pl.BlockSpec((tile, cols), lambda i, j, pos: (j * steps + i, 0)),
                pl.BlockSpec((tile, cols), lambda i, j, pos: (i, 0)),
            ],
        ),
        out_shape=[jax.ShapeDtypeStruct((N_CHIPS * half, cols), BF16), jax.ShapeDtypeStruct((half, cols), F32)],
        compiler_params=_params(("parallel", "arbitrary")),
    )(pos, grad, theirs)


def _adamw_update(w, g, m, v):
    nm = ADAM_B1 * m + (1.0 - ADAM_B1) * g
    nv = ADAM_B2 * v + (1.0 - ADAM_B2) * (g * g)
    m_hat = nm / (1.0 - ADAM_B1**ADAM_STEP)
    v_hat = nv / (1.0 - ADAM_B2**ADAM_STEP)
    return -ADAM_LR * (m_hat / (jnp.sqrt(v_hat) + ADAM_EPS) + ADAM_WD * w), nm, nv


def _chip_sum(name, own, landed, pos):
    half, cols = own.shape
    tile = _row_tile(half)
    steps = half // tile

    def body(pos_ref, own_ref, l0, l1, l2, o_ref):
        o_ref[...] = ((own_ref[...] + l0[...].astype(F32)) + l1[...].astype(F32)) + l2[...].astype(F32)

    landed_specs = [pl.BlockSpec((tile, cols), lambda i, pos, _k=k: (_k * steps + i, 0)) for k in range(N_CHIPS - 1)]
    return pl.pallas_call(
        body,
        name=name,
        grid_spec=pltpu.PrefetchScalarGridSpec(
            num_scalar_prefetch=1,
            grid=(steps,),
            in_specs=[pl.BlockSpec((tile, cols), lambda i, pos: (i, 0))] + landed_specs,
            out_specs=pl.BlockSpec((tile, cols), lambda i, pos: (pos[0] * steps + i, 0)),
        ),
        out_shape=jax.ShapeDtypeStruct((2 * half, cols), F32),
        compiler_params=_params(("parallel",)),
    )(pos, own, landed, landed, landed)


def _adamw(name, w, g, m, v):
    rows, cols = w.shape
    tile = rows if rows * cols <= 256 * 1024 else _row_tile(rows)

    def body(w_ref, g_ref, m_ref, v_ref, g_out_ref, d_ref, nm_ref, nv_ref):
        g = g_ref[...]
        g_out_ref[...] = g
        d_ref[...], nm_ref[...], nv_ref[...] = _adamw_update(w_ref[...], g, m_ref[...], v_ref[...])

    spec = _row_spec(tile, cols)
    return pl.pallas_call(
        body,
        name=name,
        grid=(rows // tile,),
        in_specs=[spec] * 4,
        out_specs=[spec] * 4,
        out_shape=[jax.ShapeDtypeStruct((rows, cols), F32)] * 4,
        compiler_params=_params(("parallel",)),
    )(w, g, m, v)


_SMALL = (
    ("v_ln_g", (D_GMLP,), 8),
    ("v_ln_b", (D_GMLP,), 8),
    ("w_spatial", (N_HEADS, CHUNK, CHUNK), 1024),
    ("b_spatial", (N_HEADS, CHUNK), 8),
    ("sinks", (N_HEADS,), 8),
    ("ln1_g", (D_MODEL,), 8),
    ("ln1_b", (D_MODEL,), 8),
    ("ln2_g", (D_MODEL,), 8),
    ("ln2_b", (D_MODEL,), 8),
    ("squared_error", (D_MODEL,), 8),
)
N_SMALL_PARAMS = len(_SMALL) - 1


def _pack_small(values):
    parts = []
    for (name, shape, rows), val in zip(_SMALL, values, strict=True):
        flat = val.reshape(-1).astype(F32)
        parts.append(jnp.pad(flat, (0, rows * LANES - flat.shape[0])).reshape(rows, LANES))
    parts.append(jnp.zeros((SMALL_ROWS - sum(rows for _, _, rows in _SMALL), LANES), F32))
    return jnp.concatenate(parts, axis=0)


def _adamw_small(g_slab, params, first, second, dep=None):
    n = N_SMALL_PARAMS

    def pieces(shape):
        if len(shape) == 3:
            return [((0, h), h * shape[1], shape[1], shape[2]) for h in range(shape[0])]
        if len(shape) == 2:
            return [((0,), 0, shape[0], shape[1])]
        if shape[0] >= LANES:
            return [((slice(None), slice(r * LANES, (r + 1) * LANES)), r, 1, LANES) for r in range(shape[0] // LANES)]
        return [((slice(None), slice(0, shape[0])), 0, 1, shape[0])]

    def body(*refs):
        g_ref = refs[0]
        w_refs, m_refs, v_refs = refs[1 : 1 + n], refs[1 + n : 1 + 2 * n], refs[1 + 2 * n : 1 + 3 * n]
        outs = refs[1 + 3 * n :]
        row0 = 0
        for idx, (_, shape, rows) in enumerate(_SMALL[:n]):
            for where, first_row, n_rows, lanes in pieces(shape):
                g = g_ref[row0 + first_row : row0 + first_row + n_rows, 0:lanes]
                delta, nm, nv = _adamw_update(w_refs[idx][where], g, m_refs[idx][where], v_refs[idx][where])
                for group, val in enumerate((g, delta, nm, nv)):
                    outs[group * n + idx][where] = val
            row0 += rows

    vmem = pl.BlockSpec(memory_space=pltpu.VMEM)
    shapes = [jax.ShapeDtypeStruct(p.shape, F32) for p in params]
    body, in_specs, operands = _after(dep, body, [vmem] * (1 + 3 * n), [g_slab, *params, *first, *second])
    outs = pl.pallas_call(
        body,
        name="adamw_small",
        in_specs=in_specs,
        out_specs=[vmem] * (4 * n),
        out_shape=shapes * 4,
        compiler_params=_params(),
    )(*operands)
    return [list(outs[group * n : (group + 1) * n]) for group in range(4)]


def kernel(x, positions, w_in, v_ln_g, v_ln_b, w_spatial, b_spatial, sinks, w_out, ln1_g, ln1_b, w_ff1, w_ff2, ln2_g, ln2_b, loss_target, m_w_in, m_v_ln_g, m_v_ln_b, m_w_spatial, m_b_spatial, m_sinks, m_w_out, m_ln1_g, m_ln1_b, m_w_ff1, m_w_ff2, m_ln2_g, m_ln2_b, v_w_in, v_v_ln_g, v_v_ln_b, v_w_spatial, v_b_spatial, v_sinks, v_w_out, v_ln1_g, v_ln1_b, v_w_ff1, v_w_ff2, v_ln2_g, v_ln2_b):
    t = x.shape[1]
    x2 = x.reshape(t, D_MODEL)
    target = loss_target.reshape(t, D_MODEL)

    w_in_shard = w_in[0].T.astype(BF16)
    in_direct, in_pass = _direct_gather_plans([w_in_shard.shape[0]])
    in_bufs, in_started, in_send, in_recv = _split_call(
        "gather_w_in_start", [w_in_shard, _landing(N_CHIPS * w_in_shard.shape[0], D_MODEL, BF16)], start=in_direct)
    inv_freq = ROPE_THETA ** (-jnp.arange(0, HEAD_DIM, 2, dtype=F32) / HEAD_DIM)
    cos, sin, later = _rope_tables_and_casts(
        positions, jnp.tile(inv_freq, LANES // (HEAD_DIM // 2)).reshape(1, LANES), [w_out[0], w_ff1[0], w_ff2[0]], dep=in_started)
    later_rows = [s.shape[0] for s in later]
    direct_plan, pass_plan, diagonal_plan = _gather_plans(later_rows)
    bufs, started, direct_send, direct_recv = _split_call(
        "gather_start", later + [_landing(N_CHIPS * r, D_MODEL, BF16) for r in later_rows], start=direct_plan, after=cos)
    in_bufs, in_passing, in_pass_send, in_pass_recv = _split_call(
        "gather_w_in_pass", in_bufs, wait=(in_direct, in_send, in_recv), start=in_pass, after=started)
    in_bufs, _ = _split_call("gather_w_in_end", in_bufs, wait=(in_pass, in_pass_send, in_pass_recv), after=in_passing)
    w_in_t = in_bufs[1]

    u, vg, q, k, va = _in_proj(x2, w_in_t, cos, sin)
    bias_full = jnp.repeat(b_spatial[0].T, HEAD_DIM, axis=1)
    sink_vec = sinks.reshape(N_HEADS)
    bufs, passing, pass_send, pass_recv = _split_call(
        "gather_pass", bufs, wait=(direct_plan, direct_send, direct_recv), start=pass_plan, after=u)
    cat = _mixer_fwd(u, vg, q, k, va, v_ln_g, v_ln_b, w_spatial[0], bias_full, sink_vec, dep=passing)
    bufs, passing, diag_send, diag_recv = _split_call(
        "gather_pass_diagonal", bufs, wait=(pass_plan, pass_send, pass_recv), start=diagonal_plan, after=cat)
    bufs, _ = _split_call("gather_end", bufs, wait=(diagonal_plan, diag_send, diag_recv), after=passing)
    w_out_all = bufs[3]
    w1_all = bufs[4].reshape(N_FF_BLOCKS, D_MODEL, D_MODEL)
    w2_all = bufs[5].reshape(N_FF_BLOCKS, D_MODEL, D_MODEL)
    xhat1, rstd1, x1b, r, dz2, dz2b, d_ln2_g, d_ln2_b, sq_err = _ffn_fwd_loss(
        cat, x2, w_out_all, ln1_g, ln1_b, w1_all, w2_all, ln2_g, ln2_b, target)

    pos = jnp.stack([lax.axis_index("c"), 2 * lax.axis_index("x") + lax.axis_index("y")]).astype(jnp.int32)
    half_landing = lambda g: _landing(g.shape[0] // 2, D_MODEL, F32)
    ff_swap_plan = _swap_plan([D_FF // N_CHIPS])
    ff_exchange_plan = _exchange_plan([D_FF // N_CHIPS // 2])
    exchange_landing = lambda p: _landing(3 * p.shape[0] // N_CHIPS, D_MODEL, BF16)
    g_ff1_local, dz1, dz1b, dcat, d_ln1_g, d_ln1_b = _ffn_bwd_ln1(dz2, r, x1b, xhat1, rstd1, ln1_g, w1_all, w2_all, w_out_all)
    ff1_bufs, swapping1, swap1_send, swap1_recv = _split_call("ff1_swap_start", [g_ff1_local, half_landing(g_ff1_local)], start=ff_swap_plan)
    dh_main, dkv, g_ff2_local, d_v_ln_g, d_v_ln_b, d_w_spatial, d_b_spatial_t, d_sinks = _mixer_bwd(
        u, vg, q, k, va, dcat, cos, sin, v_ln_g, v_ln_b, w_spatial[0], bias_full, sink_vec, r, dz2b, dep=swapping1)
    ff1_bufs, _ = _split_call("ff1_swap_wait", ff1_bufs, wait=(ff_swap_plan, swap1_send, swap1_recv), after=dh_main)
    ff1_sum, ff1_own = _pair_sum("grad_pair_sum_w_ff1", ff1_bufs[0], ff1_bufs[1], pos)
    ff2_bufs, swapping2, swap2_send, swap2_recv = _split_call(
        "ff2_swap_start", [g_ff2_local, half_landing(g_ff2_local)], start=ff_swap_plan, after=ff1_sum)
    g_out_local, small_g = _grad_w_out_and_small_all_reduce(cat, dz1b, _pack_small(
        [d_v_ln_g, d_v_ln_b, d_w_spatial, d_b_spatial_t[:, :N_HEADS].T, d_sinks[0, :N_HEADS], d_ln1_g, d_ln1_b, d_ln2_g, d_ln2_b, sq_err]),
        dep=swapping2)
    sq_row = sum(rows for _, _, rows in _SMALL[:N_SMALL_PARAMS])
    loss = 0.5 * jnp.sum(small_g[sq_row : sq_row + _SMALL[N_SMALL_PARAMS][2]]) / D_MODEL
    ff1_ex, exchanging1, ex1_send, ex1_recv = _split_call(
        "ff1_exchange_start", [ff1_sum, exchange_landing(ff1_sum)], start=ff_exchange_plan, after=g_out_local)
    ff2_bufs, _ = _split_call("ff2_swap_wait", ff2_bufs, wait=(ff_swap_plan, swap2_send, swap2_recv), after=exchanging1)
    ff2_sum, ff2_own = _pair_sum("grad_pair_sum_w_ff2", ff2_bufs[0], ff2_bufs[1], pos)
    ff2_ex, exchanging2, ex2_send, ex2_recv = _split_call(
        "ff2_exchange_start", [ff2_sum, exchange_landing(ff2_sum)], start=ff_exchange_plan)
    grad_x_flat, g_in_local = _grad_x_and_w_in_t(dh_main, dkv, dz1, x2, w_in_t, dep=exchanging2)
    grad_x = grad_x_flat.reshape(1, t, D_MODEL)

    small = [g_in_local, g_out_local]
    small_swap_plan = _swap_plan([g.shape[0] // N_CHIPS for g in small])
    swap_bufs, small_swapping, ss_send, ss_recv = _split_call(
        "small_swap_start", small + [half_landing(g) for g in small], start=small_swap_plan)
    ff1_ex, _ = _split_call("ff1_exchange_wait", ff1_ex, wait=(ff_exchange_plan, ex1_send, ex1_recv), after=small_swapping)
    half_ff1 = _chip_sum("grad_chip_sum_w_ff1", ff1_own, ff1_ex[1], pos)
    swap_bufs, _ = _split_call("small_swap_wait", swap_bufs, wait=(small_swap_plan, ss_send, ss_recv), after=half_ff1)
    pair_sums = [_pair_sum("grad_pair_sum_" + nm, g, th, pos) for nm, g, th in zip(["w_in", "w_out"], swap_bufs[:2], swap_bufs[2:])]
    small_plan = _exchange_plan([p.shape[0] // N_CHIPS for p, _ in pair_sums])
    small_bufs, small_exchanging, sm_send, sm_recv = _split_call(
        "small_exchange_start", [p for p, _ in pair_sums] + [exchange_landing(p) for p, _ in pair_sums], start=small_plan)

    ff_pair_plan = _sibling_plan([D_FF // N_CHIPS])
    (half_ff1, *ff2_ex), _, g1_send, g1_recv = _split_call(
        "ff2_exchange_wait_ff1_pair_start", [half_ff1] + ff2_ex, wait=(_shifted(ff_exchange_plan, 1), ex2_send, ex2_recv), start=ff_pair_plan,
        after=small_exchanging)
    half_ff2 = _chip_sum("grad_chip_sum_w_ff2", ff2_own, ff2_ex[1], pos)
    (half_ff2, g_w_ff1), _, g2_send, g2_recv = _split_call(
        "ff1_pair_wait_ff2_pair_start", [half_ff2, half_ff1], wait=(_shifted(ff_pair_plan, 1), g1_send, g1_recv), start=ff_pair_plan)
    g_w_ff1, d_w_ff1, nm_w_ff1, nv_w_ff1 = _adamw("adamw_w_ff1", w_ff1[0], g_w_ff1, m_w_ff1[0], v_w_ff1[0])
    (g_w_ff2,), _ = _split_call("ff2_pair_wait", [half_ff2], wait=(ff_pair_plan, g2_send, g2_recv), after=nv_w_ff1)
    g_w_ff2, d_w_ff2, nm_w_ff2, nv_w_ff2 = _adamw("adamw_w_ff2", w_ff2[0], g_w_ff2, m_w_ff2[0], v_w_ff2[0])
    small_bufs, _ = _split_call("small_exchange_wait", small_bufs, wait=(small_plan, sm_send, sm_recv), after=nv_w_ff2)
    shards = [_chip_sum("grad_chip_sum_" + nm, own, ld, pos) for nm, (_, own), ld in zip(["w_in", "w_out"], pair_sums, small_bufs[2:])]
    small_pair_plan = _sibling_plan([s.shape[0] for s in shards])
    shards, small_pairing, g3_send, g3_recv = _split_call("small_pair_start", shards, start=small_pair_plan)
    small_grads, small_d, small_nm, small_nv = _adamw_small(
        small_g,
        [v_ln_g, v_ln_b, w_spatial, b_spatial, sinks, ln1_g, ln1_b, ln2_g, ln2_b],
        [m_v_ln_g, m_v_ln_b, m_w_spatial, m_b_spatial, m_sinks, m_ln1_g, m_ln1_b, m_ln2_g, m_ln2_b],
        [v_v_ln_g, v_v_ln_b, v_w_spatial, v_b_spatial, v_sinks, v_ln1_g, v_ln1_b, v_ln2_g, v_ln2_b],
        dep=small_pairing)
    (g_w_in_t, g_w_out), _ = _split_call("small_pair_wait", shards, wait=(small_pair_plan, g3_send, g3_recv), after=small_nv[0])
    g_w_in, d_w_in, nm_w_in, nv_w_in = (a.T for a in _adamw("adamw_w_in", w_in[0].T, g_w_in_t, m_w_in[0].T, v_w_in[0].T))
    g_w_out, d_w_out, nm_w_out, nv_w_out = _adamw("adamw_w_out", w_out[0], g_w_out, m_w_out[0], v_w_out[0])

    def with_big(small, w_in_v, w_out_v, w_ff1_v, w_ff2_v):
        g_vg, g_vb, g_ws, g_bs, g_sk, g_1g, g_1b, g_2g, g_2b = small
        return [w_in_v[None], g_vg, g_vb, g_ws, g_bs, g_sk, w_out_v[None], g_1g, g_1b, w_ff1_v[None], w_ff2_v[None], g_2g, g_2b]

    return (
        loss,
        grad_x,
        *with_big(small_grads, g_w_in, g_w_out, g_w_ff1, g_w_ff2),
        *with_big(small_d, d_w_in, d_w_out, d_w_ff1, d_w_ff2),
        *with_big(small_nm, nm_w_in, nm_w_out, nm_w_ff1, nm_w_ff2),
        *with_big(small_nv, nv_w_in, nv_w_out, nv_w_ff1, nv_w_ff2),
    )
```

```python
import math

import jax
import jax.numpy as jnp
from jax import lax
from jax.experimental import pallas as pl
from jax.experimental.pallas import tpu as pltpu

F32 = jnp.float32
BF16 = jnp.bfloat16

D_MODEL = 1024
HEAD_DIM = 64
D_GMLP = 512
D_ATTN = 512
D_KV = 128
D_IN = 2 * D_GMLP + D_ATTN + 2 * D_KV
D_MAIN = 2 * D_GMLP + D_ATTN
N_HEADS = 8
CHUNK = 128
CHUNKS_PER_STEP = 4
ROPE_THETA = 10000.0
D_FF = 4 * D_MODEL
N_FF_BLOCKS = 4
LN_EPS = 1e-5
ALPHA = (2.0 * 1) ** 0.25
NEG_INF = -1e30
SCALE = 1.0 / math.sqrt(HEAD_DIM)

ADAM_LR = 0.001
ADAM_B1 = 0.9
ADAM_B2 = 0.999
ADAM_EPS = 1e-08
ADAM_WD = 0.01
ADAM_STEP = 10

N_CHIPS = 4
LANES = 128
V7X_VMEM_BYTES = 64 * 1024 * 1024
VMEM_LIMIT = V7X_VMEM_BYTES - 8 * 1024 * 1024
TM = 512
TM_FFN = 256
TM_FFN_FWD = 512
FFN_PART = 256
TK = 1024
SMALL_ROWS = 1152
MESH = pl.DeviceIdType.MESH

NT = (((1,), (1,)), ((), ()))
TN = (((0,), (0,)), ((), ()))


def _dot(a, b, dims=None):
    if dims is None:
        return jnp.dot(a, b, preferred_element_type=F32)
    return lax.dot_general(a, b, dims, preferred_element_type=F32)


def _params(semantics=None):
    return pltpu.CompilerParams(dimension_semantics=semantics, vmem_limit_bytes=VMEM_LIMIT)


def _const_spec(shape, single_buffer=False):
    zeros = (0,) * len(shape)
    if single_buffer:
        return pl.BlockSpec(shape, lambda *_: zeros, pipeline_mode=pl.Buffered(1))
    return pl.BlockSpec(shape, lambda *_: zeros)


def _row_spec(rows, cols):
    return pl.BlockSpec((rows, cols), lambda i: (i, 0))


def _after(dep, body, in_specs, operands):
    if dep is None:
        return body, list(in_specs), list(operands)
    return (lambda dep_ref, *refs: body(*refs)), [pl.BlockSpec(memory_space=pl.ANY)] + list(in_specs), [dep] + list(operands)


def _gelu(x):
    k = math.sqrt(2.0 / math.pi)
    return 0.5 * x * (1.0 + jnp.tanh(k * (x + 0.044715 * (x * x * x))))


def _gelu_and_grad(x):
    k = math.sqrt(2.0 / math.pi)
    x2 = x * x
    t = jnp.tanh(k * (x + 0.044715 * (x2 * x)))
    g = 0.5 * x * (1.0 + t)
    dg = 0.5 * (1.0 + t) + 0.5 * x * (1.0 - t * t) * (k * (1.0 + 3.0 * 0.044715 * x2))
    return g, dg


def _layer_norm_stats(z):
    mu = jnp.mean(z, axis=-1, keepdims=True)
    zc = z - mu
    var = jnp.mean(zc * zc, axis=-1, keepdims=True)
    rstd = lax.rsqrt(var + LN_EPS)
    return zc * rstd, rstd


def _layer_norm_bwd(dxhat, xhat, rstd):
    m1 = jnp.mean(dxhat, axis=-1, keepdims=True)
    m2 = jnp.mean(dxhat * xhat, axis=-1, keepdims=True)
    return rstd * (dxhat - m1 - xhat * m2)


def _rotate_half(t):
    n = t.shape[1]
    lane = lax.broadcasted_iota(jnp.int32, t.shape, 1)
    first = (lane & (HEAD_DIM // 2)) == 0
    return jnp.where(first, -pltpu.roll(t, n - HEAD_DIM // 2, 1), pltpu.roll(t, HEAD_DIM // 2, 1))


def _rope(t, cos, sin):
    return t * cos + _rotate_half(t) * sin


def _rope_transposed(g, cos, sin):
    return g * cos - _rotate_half(g * sin)


def _lane_tile(a, reps):
    return jnp.tile(a, (1, reps)) if reps > 1 else a


def _in_proj(x, w_in_t, cos, sin, dep=None):
    t = x.shape[0]

    def body(x_ref, w_ref, cos_ref, sin_ref, u_ref, vg_ref, q_ref, k_ref, va_ref):
        xb = x_ref[...].astype(BF16)
        u_ref[...] = _dot(xb, w_ref[0:D_GMLP, :], NT)
        vg_ref[...] = _dot(xb, w_ref[D_GMLP : 2 * D_GMLP, :], NT)
        q = _dot(xb, w_ref[2 * D_GMLP : D_MAIN, :], NT)
        k = _dot(xb, w_ref[D_MAIN : D_MAIN + D_KV, :], NT)
        va_ref[...] = _dot(xb, w_ref[D_MAIN + D_KV : D_IN, :], NT).astype(BF16)
        c, s = cos_ref[...], sin_ref[...]
        q_ref[...] = _rope(q, _lane_tile(c, D_ATTN // LANES), _lane_tile(s, D_ATTN // LANES)).astype(BF16)
        k_ref[...] = _rope(k, c, s).astype(BF16)

    body, in_specs, operands = _after(
        dep, body, [_row_spec(TM, D_MODEL), _const_spec((D_IN, D_MODEL)), _row_spec(TM, LANES), _row_spec(TM, LANES)], [x, w_in_t, cos, sin])
    return pl.pallas_call(
        body,
        name="in_proj",
        grid=(t // TM,),
        in_specs=in_specs,
        out_specs=[_row_spec(TM, D_GMLP), _row_spec(TM, D_GMLP), _row_spec(TM, D_ATTN), _row_spec(TM, D_KV), _row_spec(TM, D_KV)],
        out_shape=[
            jax.ShapeDtypeStruct((t, D_GMLP), F32),
            jax.ShapeDtypeStruct((t, D_GMLP), F32),
            jax.ShapeDtypeStruct((t, D_ATTN), BF16),
            jax.ShapeDtypeStruct((t, D_KV), BF16),
            jax.ShapeDtypeStruct((t, D_KV), BF16),
        ],
        compiler_params=_params(("parallel",)),
    )(*operands)


def _step_rows(i):
    return (i, 0)


def _chunk_before_step(i):
    return (jnp.maximum(CHUNKS_PER_STEP * i - 1, 0), 0)


def _chunk_specs():
    step = CHUNKS_PER_STEP * CHUNK
    return [
        pl.BlockSpec((step, D_GMLP), _step_rows),
        pl.BlockSpec((step, D_GMLP), _step_rows),
        pl.BlockSpec((step, D_ATTN), _step_rows),
        pl.BlockSpec((step, D_KV), _step_rows),
        pl.BlockSpec((CHUNK, D_KV), _chunk_before_step),
        pl.BlockSpec((step, D_KV), _step_rows),
        pl.BlockSpec((CHUNK, D_KV), _chunk_before_step),
    ]


def _half_lane_masks(rows):
    lane = lax.broadcasted_iota(jnp.int32, (rows, LANES), 1)
    return lane < HEAD_DIM


def _kv_variants(kv2):
    left = _half_lane_masks(kv2.shape[0])
    f = kv2.astype(F32)
    swapped = pltpu.roll(f, HEAD_DIM, 1)
    zero = jnp.zeros_like(f)
    g0 = (jnp.where(left, f, zero).astype(BF16), jnp.where(left, zero, swapped).astype(BF16))
    g1 = (jnp.where(left, swapped, zero).astype(BF16), jnp.where(left, zero, f).astype(BF16))
    return (g0, g1)


def _band_mask(i, heads=1):
    row = lax.broadcasted_iota(jnp.int32, (heads * CHUNK, 2 * CHUNK), 0) & (CHUNK - 1)
    col = lax.broadcasted_iota(jnp.int32, (heads * CHUNK, 2 * CHUNK), 1)
    no_prev = jnp.where(i > 0, 0, 4 * CHUNK)
    in_prev = jnp.logical_and(col < CHUNK, (col - row) > no_prev)
    in_cur = jnp.logical_and(col >= CHUNK, (col - CHUNK) <= row)
    return jnp.logical_or(in_prev, in_cur)


def _causal_mask():
    row = lax.broadcasted_iota(jnp.int32, (CHUNK, CHUNK), 0)
    col = lax.broadcasted_iota(jnp.int32, (CHUNK, CHUNK), 1)
    return col <= row


def _store_spatial_weights(w_ref, wcat_ref, wcat_t_ref=None):
    causal = _causal_mask()
    for p in range(D_GMLP // LANES):
        wl = jnp.where(causal, w_ref[2 * p], 0.0)
        wr = jnp.where(causal, w_ref[2 * p + 1], 0.0)
        wcat_ref[p] = jnp.concatenate([wl, wr], axis=1).astype(BF16)
        if wcat_t_ref is not None:
            wcat_t_ref[p] = jnp.concatenate([wl.T, wr.T], axis=1).astype(BF16)


def _pair_stack(xp, left):
    return jnp.concatenate([jnp.where(left, xp, 0.0), jnp.where(left, 0.0, xp)], axis=0).astype(BF16)


def _mixer_fwd(u, vg, q, k, va, v_ln_g, v_ln_b, w_spatial, bias_full, sinks, dep=None):
    t = u.shape[0]

    def body(u_ref, vg_ref, q_ref, kc_ref, kp_ref, vc_ref, vp_ref, g_ref, b_ref, w_ref, bias_ref, sink_ref, cat_ref, wcat):
        i = pl.program_id(0)
        left = _half_lane_masks(CHUNK)

        @pl.when(i == 0)
        def _():
            _store_spatial_weights(w_ref, wcat)

        heads = range(N_HEADS)
        pair_cols = [slice(p * LANES, (p + 1) * LANES) for p in range(D_GMLP // LANES)]
        sinks_h = [sink_ref[h] for h in heads]
        for c in range(CHUNKS_PER_STEP):
            rows = slice(c * CHUNK, (c + 1) * CHUNK)
            before = slice((c - 1) * CHUNK, c * CHUNK)
            k_prev = kp_ref[...] if c == 0 else kc_ref[before, :]
            v_prev = vp_ref[...] if c == 0 else vc_ref[before, :]
            k_var = _kv_variants(jnp.concatenate([k_prev, kc_ref[rows, :]], axis=0))
            v_var = _kv_variants(jnp.concatenate([v_prev, vc_ref[rows, :]], axis=0))
            scores = [_dot(q_ref[rows, pair_cols[h // 2]], k_var[h // 4][h % 2], NT) for h in heads]

            ug = _gelu(u_ref[rows, :])
            xhat, _ = _layer_norm_stats(_gelu(vg_ref[rows, :]))
            vgl = xhat * g_ref[...] + b_ref[...]
            mixed = [_dot(wcat[p], _pair_stack(vgl[:, cols], left)) for p, cols in enumerate(pair_cols)]

            valid = _band_mask(CHUNKS_PER_STEP * i + c)
            masked = [jnp.where(valid, scores[h] * SCALE, NEG_INF) for h in heads]
            maxes = [jnp.maximum(jnp.max(masked[h], axis=1, keepdims=True), sinks_h[h]) for h in heads]
            exps = [jnp.exp(masked[h] - maxes[h]) for h in heads]
            invs = [1.0 / (jnp.sum(exps[h], axis=1, keepdims=True) + jnp.exp(sinks_h[h] - maxes[h])) for h in heads]
            probs = [(exps[h] * invs[h]).astype(BF16) for h in heads]
            for p, cols in enumerate(pair_cols):
                cat_ref[rows, cols] = (ug[:, cols] * (mixed[p] + bias_ref[:, cols])).astype(BF16)
            for p in range(D_ATTN // LANES):
                out = _dot(probs[2 * p], v_var[p // 2][0]) + _dot(probs[2 * p + 1], v_var[p // 2][1])
                cat_ref[rows, D_GMLP + p * LANES : D_GMLP + (p + 1) * LANES] = out.astype(BF16)

    in_specs = _chunk_specs() + [
        _const_spec((1, D_GMLP)),
        _const_spec((1, D_GMLP)),
        _const_spec((N_HEADS, CHUNK, CHUNK)),
        _const_spec((CHUNK, D_GMLP)),
        pl.BlockSpec(memory_space=pltpu.SMEM),
    ]
    body, in_specs, operands = _after(dep, body, in_specs, [u, vg, q, k, k, va, va, v_ln_g, v_ln_b, w_spatial, bias_full, sinks])
    return pl.pallas_call(
        body,
        name="mixer_fwd",
        grid=(t // (CHUNKS_PER_STEP * CHUNK),),
        in_specs=in_specs,
        out_specs=pl.BlockSpec((CHUNKS_PER_STEP * CHUNK, D_MODEL), lambda i: (i, 0)),
        out_shape=jax.ShapeDtypeStruct((t, D_MODEL), BF16),
        scratch_shapes=[pltpu.VMEM((D_GMLP // LANES, CHUNK, 2 * CHUNK), BF16)],
        compiler_params=_params(("arbitrary",)),
    )(*operands)


def _ffn_fwd_loss(cat, x, w_out, ln1_g, ln1_b, w1, w2, ln2_g, ln2_b, target):
    t = x.shape[0]

    def body(cat_ref, x_ref, wo_ref, g1_ref, b1_ref, w1_ref, w2_ref, g2_ref, b2_ref, tgt_ref,
             xh_ref, rstd_ref, x1b_ref, r_ref, dz2_ref, dz2b_ref, dg2_ref, db2_ref, sq_ref):
        @pl.when(pl.program_id(0) == 0)
        def _():
            dg2_ref[...] = jnp.zeros_like(dg2_ref)
            db2_ref[...] = jnp.zeros_like(db2_ref)
            sq_ref[...] = jnp.zeros_like(sq_ref)

        parts = [slice(p * FFN_PART, (p + 1) * FFN_PART) for p in range(TM_FFN_FWD // FFN_PART)]

        def norm1(rows, z):
            xhat1, rstd1 = _layer_norm_stats(z)
            xh_ref[rows, :] = xhat1
            rstd_ref[rows, :] = rstd1
            x1 = xhat1 * g1_ref[...] + b1_ref[...]
            x1b = x1.astype(BF16)
            x1b_ref[rows, :] = x1b
            return x1, x1b

        def feed_forward(rows, x1b, pre):
            ff = None
            for j in range(N_FF_BLOCKS):
                r = jnp.maximum(pre, 0.0)
                r_ref[rows, j * D_MODEL : (j + 1) * D_MODEL] = r.astype(BF16)
                part = _dot((r * r).astype(BF16), w2_ref[j])
                ff = part if ff is None else ff + part
                if j + 1 < N_FF_BLOCKS:
                    pre = _dot(x1b, w1_ref[j + 1])
            return ff

        def norm2_and_loss(rows, x1, ff):
            xhat2, rstd2 = _layer_norm_stats(ALPHA * x1 + ff)
            err = xhat2 * g2_ref[...] + b2_ref[...] - tgt_ref[rows, :]
            sq_ref[...] += jnp.sum(err * err, axis=0, keepdims=True)
            dy = err * (1.0 / D_MODEL)
            dg2_ref[...] += jnp.sum(dy * xhat2, axis=0, keepdims=True)
            db2_ref[...] += jnp.sum(dy, axis=0, keepdims=True)
            dz2 = _layer_norm_bwd(dy * g2_ref[...], xhat2, rstd2)
            dz2_ref[rows, :] = dz2
            dz2b_ref[rows, :] = dz2.astype(BF16)

        projected = _dot(cat_ref[parts[0], :], wo_ref[...])
        last = None
        for i, rows in enumerate(parts):
            z = ALPHA * x_ref[rows, :] + projected
            if i + 1 < len(parts):
                projected = _dot(cat_ref[parts[i + 1], :], wo_ref[...])
            x1, x1b = norm1(rows, z)
            pre = _dot(x1b, w1_ref[0])
            if last is not None:
                norm2_and_loss(*last)
            last = (rows, x1, feed_forward(rows, x1b, pre))
        norm2_and_loss(*last)

    vec = _const_spec((1, D_MODEL))
    tile = _row_spec(TM_FFN_FWD, D_MODEL)
    wspec = _const_spec((N_FF_BLOCKS, D_MODEL, D_MODEL), single_buffer=True)
    return pl.pallas_call(
        body,
        name="ffn_fwd_loss",
        grid=(t // TM_FFN_FWD,),
        in_specs=[tile, tile, _const_spec((D_MODEL, D_MODEL), single_buffer=True), vec, vec, wspec, wspec, vec, vec, tile],
        out_specs=[tile, _row_spec(TM_FFN_FWD, 1), tile, _row_spec(TM_FFN_FWD, D_FF), tile, tile, vec, vec, vec],
        out_shape=[
            jax.ShapeDtypeStruct((t, D_MODEL), F32),
            jax.ShapeDtypeStruct((t, 1), F32),
            jax.ShapeDtypeStruct((t, D_MODEL), BF16),
            jax.ShapeDtypeStruct((t, D_FF), BF16),
            jax.ShapeDtypeStruct((t, D_MODEL), F32),
            jax.ShapeDtypeStruct((t, D_MODEL), BF16),
            jax.ShapeDtypeStruct((1, D_MODEL), F32),
            jax.ShapeDtypeStruct((1, D_MODEL), F32),
            jax.ShapeDtypeStruct((1, D_MODEL), F32),
        ],
        compiler_params=_params(("arbitrary",)),
    )(cat, x, w_out, ln1_g, ln1_b, w1, w2, ln2_g, ln2_b, target)


def _ffn_bwd_ln1(dz2, r, x1b, xhat1, rstd1, ln1_g, w1, w2, w_out, dep=None):
    t = dz2.shape[0]

    def body(dz2_ref, r_ref, x1b_ref, xh_ref, rstd_ref, g1_ref, w1_ref, w2_ref, wo_ref, gw1_ref, dz1_ref, dz1b_ref, dcat_ref, dg1_ref, db1_ref):
        @pl.when(pl.program_id(0) == 0)
        def _():
            dg1_ref[...] = jnp.zeros_like(dg1_ref)
            db1_ref[...] = jnp.zeros_like(db1_ref)
            gw1_ref[...] = jnp.zeros_like(gw1_ref)

        dz2 = dz2_ref[...]
        dz2b = dz2.astype(BF16)
        x1_t = x1b_ref[...].astype(F32).T.astype(BF16)
        dx1 = ALPHA * dz2
        for j in range(N_FF_BLOCKS):
            cols = slice(j * D_MODEL, (j + 1) * D_MODEL)
            dpre = (_dot(dz2b, w2_ref[j], NT) * (2.0 * r_ref[:, cols].astype(F32))).astype(BF16)
            gw1_ref[cols, :] += _dot(x1_t, dpre)
            dx1 = dx1 + _dot(dpre, w1_ref[j], NT)
        xhat1 = xh_ref[...]
        dg1_ref[...] += jnp.sum(dx1 * xhat1, axis=0, keepdims=True)
        db1_ref[...] += jnp.sum(dx1, axis=0, keepdims=True)
        dz1 = _layer_norm_bwd(dx1 * g1_ref[...], xhat1, rstd_ref[...])
        dz1_ref[...] = dz1
        dz1b = dz1.astype(BF16)
        dz1b_ref[...] = dz1b
        dcat_ref[...] = _dot(dz1b, wo_ref[...], NT).astype(BF16)

    vec = _const_spec((1, D_MODEL))
    tile = _row_spec(TM_FFN, D_MODEL)
    wspec = _const_spec((N_FF_BLOCKS, D_MODEL, D_MODEL), single_buffer=True)
    body, in_specs, operands = _after(
        dep, body,
        [tile, _row_spec(TM_FFN, D_FF), tile, tile, _row_spec(TM_FFN, 1), vec, wspec, wspec, _const_spec((D_MODEL, D_MODEL), single_buffer=True)],
        [dz2, r, x1b, xhat1, rstd1, ln1_g, w1, w2, w_out])
    return pl.pallas_call(
        body,
        name="ffn_bwd_ln1",
        grid=(t // TM_FFN,),
        in_specs=in_specs,
        out_specs=[_const_spec((D_FF, D_MODEL), single_buffer=True), tile, tile, tile, vec, vec],
        out_shape=[
            jax.ShapeDtypeStruct((D_FF, D_MODEL), F32),
            jax.ShapeDtypeStruct((t, D_MODEL), F32),
            jax.ShapeDtypeStruct((t, D_MODEL), BF16),
            jax.ShapeDtypeStruct((t, D_MODEL), BF16),
            jax.ShapeDtypeStruct((1, D_MODEL), F32),
            jax.ShapeDtypeStruct((1, D_MODEL), F32),
        ],
        compiler_params=_params(("arbitrary",)),
    )(*operands)


def _mixer_bwd(u, vg, q, k, va, dcat, cos, sin, v_ln_g, v_ln_b, w_spatial, bias_full, sinks, r, dz2b, dep=None):
    t = u.shape[0]
    n_chunks = t // CHUNK
    assert CHUNKS_PER_STEP == N_FF_BLOCKS

    def body(u_ref, vg_ref, q_ref, kc_ref, kp_ref, vc_ref, vp_ref, dcat_ref, cosc_ref, sinc_ref, cosp_ref, sinp_ref,
             g_ref, b_ref, w_ref, bias_ref, sink_ref, r_ref, dz2b_ref,
             dmain_ref, dkv_ref, gw2_ref, dg_ref, db_ref, dw_ref, dbs_ref, dsink_ref, dmix_acc, wcat, wcat_t):
        i = pl.program_id(0)
        left = _half_lane_masks(CHUNK)
        lane = lax.broadcasted_iota(jnp.int32, (CHUNK, LANES), 1)
        n_pairs = D_GMLP // LANES

        @pl.when(i == 0)
        def _():
            dg_ref[...] = jnp.zeros_like(dg_ref)
            db_ref[...] = jnp.zeros_like(db_ref)
            dw_ref[...] = jnp.zeros_like(dw_ref)
            dsink_ref[...] = jnp.zeros_like(dsink_ref)
            dmix_acc[...] = jnp.zeros_like(dmix_acc)
            gw2_ref[...] = jnp.zeros_like(gw2_ref)
            _store_spatial_weights(w_ref, wcat, wcat_t)

        n_qpairs = D_ATTN // LANES
        heads = range(N_HEADS)
        pair_cols = [slice(p * LANES, (p + 1) * LANES) for p in range(n_pairs)]
        sinks_h = [sink_ref[h] for h in heads]
        gain = g_ref[...]
        causal = _causal_mask()
        lane_row = lax.broadcasted_iota(jnp.int32, (1, LANES), 1)
        heads_per_group = N_HEADS // 2

        def group_grad_t(lhs_t, rhs_heads):
            parts = []
            for g in range(2):
                group = range(g * heads_per_group, (g + 1) * heads_per_group)
                lhs = jnp.concatenate([lhs_t[h * HEAD_DIM : (h + 1) * HEAD_DIM] for h in group], axis=1)
                parts.append(_dot(lhs, jnp.concatenate([rhs_heads[h] for h in group], axis=0)))
            return jnp.concatenate(parts, axis=0)

        for c in range(CHUNKS_PER_STEP):
            chunk = CHUNKS_PER_STEP * i + c
            rows = slice(c * CHUNK, (c + 1) * CHUNK)
            before = slice((c - 1) * CHUNK, c * CHUNK)

            k_prev = kp_ref[...] if c == 0 else kc_ref[before, :]
            v_prev = vp_ref[...] if c == 0 else vc_ref[before, :]
            k_var = _kv_variants(jnp.concatenate([k_prev, kc_ref[rows, :]], axis=0))
            v_var = _kv_variants(jnp.concatenate([v_prev, vc_ref[rows, :]], axis=0))
            q_pairs = [q_ref[rows, cols] for cols in pair_cols]
            do_all = dcat_ref[rows, D_GMLP:D_MODEL]
            do_pairs = [do_all[:, cols] for cols in pair_cols]
            scores = [_dot(q_pairs[h // 2], k_var[h // 4][h % 2], NT) for h in heads]
            dprobs = [_dot(do_pairs[h // 2], v_var[h // 4][h % 2], NT) for h in heads]
            q_t = q_ref[rows, :].astype(F32).T.astype(BF16)
            do_t = do_all.astype(F32).T.astype(BF16)

            ff_cols = slice(c * D_MODEL, (c + 1) * D_MODEL)
            relu_block = r_ref[:, ff_cols]
            gw2_ref[ff_cols, :] += _dot(relu_block * relu_block, dz2b_ref[...], TN)

            ug, dug_du = _gelu_and_grad(u_ref[rows, :])
            gv, dgv_dv = _gelu_and_grad(vg_ref[rows, :])
            xhat, rstd = _layer_norm_stats(gv)
            vgl = xhat * gain + b_ref[...]
            mixed = [_dot(wcat[p], _pair_stack(vgl[:, cols], left)) for p, cols in enumerate(pair_cols)]

            valid = _band_mask(chunk)
            masked = [jnp.where(valid, scores[h] * SCALE, NEG_INF) for h in heads]
            maxes = [jnp.maximum(jnp.max(masked[h], axis=1, keepdims=True), sinks_h[h]) for h in heads]
            exps = [jnp.exp(masked[h] - maxes[h]) for h in heads]
            exp_sinks = [jnp.exp(sinks_h[h] - maxes[h]) for h in heads]
            invs = [1.0 / (jnp.sum(exps[h], axis=1, keepdims=True) + exp_sinks[h]) for h in heads]
            probs = [exps[h] * invs[h] for h in heads]
            dsums = [jnp.sum(probs[h] * dprobs[h], axis=1, keepdims=True) for h in heads]
            ds_b = [(probs[h] * (dprobs[h] - dsums[h]) * SCALE).astype(BF16) for h in heads]
            probs_b = [probs[h].astype(BF16) for h in heads]

            dm_stacks = []
            for p, cols in enumerate(pair_cols):
                da = dcat_ref[rows, cols].astype(F32)
                dmain_ref[rows, cols] = (da * (mixed[p] + bias_ref[:, cols]) * dug_du[:, cols]).astype(BF16)
                dmixed = da * ug[:, cols]
                dmix_acc[:, cols] += dmixed
                dm_stacks.append(_pair_stack(dmixed, left))

            dq_all = jnp.concatenate(
                [_dot(ds_b[2 * p], k_var[p // 2][0]) + _dot(ds_b[2 * p + 1], k_var[p // 2][1]) for p in range(n_qpairs)], axis=1)
            dk2_t = group_grad_t(q_t, ds_b)
            dv2_t = group_grad_t(do_t, probs_b)

            for p, cols in enumerate(pair_cols):
                dw_pair = _dot(dm_stacks[p], vgl[:, cols].astype(BF16), NT)
                dw_ref[2 * p] += jnp.where(causal, dw_pair[:CHUNK], 0.0)
                dw_ref[2 * p + 1] += jnp.where(causal, dw_pair[CHUNK:], 0.0)
            dvgl = jnp.concatenate([_dot(wcat_t[p], dm_stacks[p]) for p in range(n_pairs)], axis=1)

            dsink_row = jnp.zeros((1, LANES), F32)
            for h in heads:
                d_sink = -jnp.sum(exp_sinks[h] * invs[h] * dsums[h], axis=0, keepdims=True)
                dsink_row = dsink_row + jnp.where(lane_row == h, d_sink, 0.0)
            dsink_ref[0:1, :] += dsink_row
            cos_c, sin_c = cosc_ref[rows, :], sinc_ref[rows, :]
            cos_p = cosp_ref[...] if c == 0 else cosc_ref[before, :]
            sin_p = sinp_ref[...] if c == 0 else sinc_ref[before, :]
            dmain_ref[rows, 2 * D_GMLP : D_MAIN] = _rope_transposed(dq_all, _lane_tile(cos_c, n_qpairs), _lane_tile(sin_c, n_qpairs)).astype(BF16)
            dk2 = dk2_t.T
            dv2 = dv2_t.T
            cur = pl.ds(pl.multiple_of(chunk * CHUNK, CHUNK), CHUNK)
            dkv_ref[cur, 0:D_KV] = _rope_transposed(dk2[CHUNK:], cos_c, sin_c)
            dkv_ref[cur, D_KV : 2 * D_KV] = dv2[CHUNK:]
            prev = pl.ds(pl.multiple_of(jnp.maximum(chunk - 1, 0) * CHUNK, CHUNK), CHUNK)
            dkv_ref[prev, 0:D_KV] += _rope_transposed(dk2[:CHUNK], cos_p, sin_p)
            dkv_ref[prev, D_KV : 2 * D_KV] += dv2[:CHUNK]

            dg_ref[...] += jnp.sum(dvgl * xhat, axis=0, keepdims=True)
            db_ref[...] += jnp.sum(dvgl, axis=0, keepdims=True)
            dgv = _layer_norm_bwd(dvgl * gain, xhat, rstd)
            dmain_ref[rows, D_GMLP : 2 * D_GMLP] = (dgv * dgv_dv).astype(BF16)

        @pl.when(i == n_chunks // CHUNKS_PER_STEP - 1)
        def _():
            tile = jnp.zeros((CHUNK, LANES), F32)
            for p, cols in enumerate(pair_cols):
                dm = dmix_acc[:, cols]
                sl = jnp.sum(jnp.where(left, dm, 0.0), axis=1, keepdims=True)
                sr = jnp.sum(jnp.where(left, 0.0, dm), axis=1, keepdims=True)
                tile = jnp.where(lane == 2 * p, sl, tile)
                tile = jnp.where(lane == 2 * p + 1, sr, tile)
            dbs_ref[...] = tile

    step = CHUNKS_PER_STEP * CHUNK
    in_specs = _chunk_specs() + [
        pl.BlockSpec((step, D_MODEL), _step_rows),
        pl.BlockSpec((step, LANES), _step_rows),
        pl.BlockSpec((step, LANES), _step_rows),
        pl.BlockSpec((CHUNK, LANES), _chunk_before_step),
        pl.BlockSpec((CHUNK, LANES), _chunk_before_step),
        _const_spec((1, D_GMLP)),
        _const_spec((1, D_GMLP)),
        _const_spec((N_HEADS, CHUNK, CHUNK)),
        _const_spec((CHUNK, D_GMLP)),
        pl.BlockSpec(memory_space=pltpu.SMEM),
        pl.BlockSpec((step, D_FF), _step_rows),
        pl.BlockSpec((step, D_MODEL), _step_rows),
    ]
    body, in_specs, operands = _after(
        dep, body, in_specs, [u, vg, q, k, k, va, va, dcat, cos, sin, cos, sin, v_ln_g, v_ln_b, w_spatial, bias_full, sinks, r, dz2b])
    return pl.pallas_call(
        body,
        name="mixer_bwd",
        grid=(n_chunks // CHUNKS_PER_STEP,),
        in_specs=in_specs,
        out_specs=[
            pl.BlockSpec((step, D_MAIN), _step_rows),
            _const_spec((t, 2 * D_KV)),
            _const_spec((D_FF, D_MODEL), single_buffer=True),
            _const_spec((1, D_GMLP)),
            _const_spec((1, D_GMLP)),
            _const_spec((N_HEADS, CHUNK, CHUNK)),
            _const_spec((CHUNK, LANES)),
            _const_spec((8, LANES)),
        ],
        out_shape=[
            jax.ShapeDtypeStruct((t, D_MAIN), BF16),
            jax.ShapeDtypeStruct((t, 2 * D_KV), F32),
            jax.ShapeDtypeStruct((D_FF, D_MODEL), F32),
            jax.ShapeDtypeStruct((1, D_GMLP), F32),
            jax.ShapeDtypeStruct((1, D_GMLP), F32),
            jax.ShapeDtypeStruct((N_HEADS, CHUNK, CHUNK), F32),
            jax.ShapeDtypeStruct((CHUNK, LANES), F32),
            jax.ShapeDtypeStruct((8, LANES), F32),
        ],
        scratch_shapes=[
            pltpu.VMEM((CHUNK, D_GMLP), F32),
            pltpu.VMEM((D_GMLP // LANES, CHUNK, 2 * CHUNK), BF16),
            pltpu.VMEM((D_GMLP // LANES, CHUNK, 2 * CHUNK), BF16),
        ],
        compiler_params=_params(("arbitrary",)),
    )(*operands)


def _grad_x(dh_main, dkv, dz1, w_in_t, dep=None):
    t = dz1.shape[0]

    def body(dm_ref, dkv_ref, dz1_ref, w_ref, gx_ref):
        acc = ALPHA * dz1_ref[...] + _dot(dm_ref[...], w_ref[0:D_MAIN, :])
        gx_ref[...] = acc + _dot(dkv_ref[...].astype(BF16), w_ref[D_MAIN:D_IN, :])

    body, in_specs, operands = _after(
        dep, body, [_row_spec(TM, D_MAIN), _row_spec(TM, 2 * D_KV), _row_spec(TM, D_MODEL), _const_spec((D_IN, D_MODEL))], [dh_main, dkv, dz1, w_in_t])
    return pl.pallas_call(
        body,
        name="grad_x",
        grid=(t // TM,),
        in_specs=in_specs,
        out_specs=_row_spec(TM, D_MODEL),
        out_shape=jax.ShapeDtypeStruct((t, D_MODEL), F32),
        compiler_params=_params(("parallel",)),
    )(*operands)


def _token_contraction(name, out_rows, tk, in_arrays, contributions, dep=None):
    t = in_arrays[0].shape[0]

    def body(*refs):
        out_ref = refs[-1]

        @pl.when(pl.program_id(0) == 0)
        def _():
            out_ref[...] = jnp.zeros_like(out_ref)

        for row0, a, b in contributions(*refs[:-1]):
            out_ref[row0 : row0 + a.shape[1], :] += _dot(a, b, TN)

    in_specs = [_row_spec(tk, a.shape[1]) for a in in_arrays]
    body, in_specs, operands = _after(dep, body, in_specs, in_arrays)
    return pl.pallas_call(
        body,
        name=name,
        grid=(t // tk,),
        in_specs=in_specs,
        out_specs=_const_spec((out_rows, D_MODEL), single_buffer=True),
        out_shape=jax.ShapeDtypeStruct((out_rows, D_MODEL), F32),
        compiler_params=_params(("arbitrary",)),
    )(*operands)


def _grad_w_out(cat, dz1b, dep=None):
    def contributions(cat_ref, dz1_ref):
        return [(0, cat_ref[...], dz1_ref[...])]

    return _token_contraction("grad_w_out", D_MODEL, TK, [cat, dz1b], contributions, dep)


ANY = pl.BlockSpec(memory_space=pl.ANY)


def _mesh_position():
    return lax.axis_index("x"), lax.axis_index("y"), lax.axis_index("c")


def _other_chips(x, y):
    return [(1 - x, y), (x, 1 - y), (1 - x, 1 - y)]


def _remote(src, dst, send_sem, recv_sem, device):
    return pltpu.make_async_remote_copy(src_ref=src, dst_ref=dst, send_sem=send_sem, recv_sem=recv_sem, device_id=device, device_id_type=MESH)


def _rows(ref, start, size):
    return ref.at[pl.ds(start, size), :]


def _rope_tables_and_casts(pos_row, inv_freq_row, shards, dep=None):
    t = pos_row.shape[1]
    steps = t // TM
    n = len(shards)

    def body(pos_ref, f_ref, *rest):
        f32_refs, (cos_ref, sin_ref), bf16_refs = rest[:n], rest[n : n + 2], rest[n + 2 :]
        for src, dst in zip(f32_refs, bf16_refs):
            dst[...] = src[...].astype(BF16)
        pos_rows = jnp.broadcast_to(pos_ref[...].astype(F32), (LANES, TM)).T
        ang = pos_rows * f_ref[...]
        cos_ref[...] = jnp.cos(ang)
        sin_ref[...] = jnp.sin(ang)

    shard_specs = [_row_spec(s.shape[0] // steps, s.shape[1]) for s in shards]
    body, in_specs, operands = _after(
        dep, body, [pl.BlockSpec((1, TM), lambda i: (0, i)), _const_spec((1, LANES))] + shard_specs, [pos_row, inv_freq_row, *shards])
    outs = pl.pallas_call(
        body,
        name="rope_tables_and_casts",
        grid=(steps,),
        in_specs=in_specs,
        out_specs=[_row_spec(TM, LANES), _row_spec(TM, LANES)] + shard_specs,
        out_shape=[jax.ShapeDtypeStruct((t, LANES), F32)] * 2 + [jax.ShapeDtypeStruct(s.shape, BF16) for s in shards],
        compiler_params=_params(("parallel",)),
    )(*operands)
    return outs[0], outs[1], list(outs[2:])


def _grad_w_in_t_and_small_all_reduce(dh_main, dkv, x, slab, dep=None):
    t = x.shape[0]
    steps = t // TK
    rows = slab.shape[0]
    part = rows // 8

    def body(dm_ref, dkv_ref, x_ref, slab_ref, grad_ref, sum_ref, landing, reduced, gathered, send_sems, recv_sems):
        k = pl.program_id(0)
        x_, y_, c_ = _mesh_position()
        me = 4 * x_ + 2 * y_ + c_
        flips = [(f >> 2, (f >> 1) & 1, f & 1) for f in range(1, 8)]

        def peer(flip):
            fx, fy, fc = flip
            return (1 - x_ if fx else x_, 1 - y_ if fy else y_, 1 - c_ if fc else c_)

        def part_of(ref, device):
            return ref.at[pl.ds(pl.multiple_of(device * part, 8), part), :]

        def scatter_copies():
            out = []
            for kk, flip in enumerate(flips):
                px, py, pc = peer(flip)
                them = 4 * px + 2 * py + pc
                send = _remote(part_of(slab_ref, them), landing.at[me], send_sems.at[kk], recv_sems.at[kk], (px, py, pc))
                recv = _remote(landing.at[them], landing.at[them], send_sems.at[kk], recv_sems.at[kk], (px, py, pc))
                out.append((send, recv))
            return out

        def gather_copies():
            out = []
            for kk, flip in enumerate(flips):
                px, py, pc = peer(flip)
                them = 4 * px + 2 * py + pc
                send = _remote(reduced, part_of(gathered, me), send_sems.at[7 + kk], recv_sems.at[7 + kk], (px, py, pc))
                recv = _remote(part_of(gathered, them), part_of(gathered, them), send_sems.at[7 + kk], recv_sems.at[7 + kk], (px, py, pc))
                out.append((send, recv))
            return out

        @pl.when(k == 0)
        def _():
            grad_ref[...] = jnp.zeros_like(grad_ref)
            for send, _ in scatter_copies():
                send.start()
            landing[me] = part_of(slab_ref, me)[...]

        @pl.when(k == steps // 2)
        def _():
            for _, recv in scatter_copies():
                recv.wait_recv()
            total = landing[0]
            for s in range(1, 8):
                total = total + landing[s]
            reduced[...] = total
            part_of(gathered, me)[...] = total
            for send, _ in gather_copies():
                send.start()

        xb = x_ref[...].astype(BF16)
        grad_ref[0:D_MAIN, :] += _dot(dm_ref[...], xb, TN)
        grad_ref[D_MAIN:D_IN, :] += _dot(dkv_ref[...].astype(BF16), xb, TN)

        @pl.when(k == steps - 1)
        def _():
            for send, recv in gather_copies():
                recv.wait_recv()
                send.wait_send()
            for send, _ in scatter_copies():
                send.wait_send()
            sum_ref[...] = gathered[...]

    body, in_specs, operands = _after(
        dep, body, [_row_spec(TK, D_MAIN), _row_spec(TK, 2 * D_KV), _row_spec(TK, D_MODEL), _const_spec(slab.shape)], [dh_main, dkv, x, slab])
    return pl.pallas_call(
        body,
        name="grad_w_in_and_small_all_reduce",
        grid=(steps,),
        in_specs=in_specs,
        out_specs=[_const_spec((D_IN, D_MODEL), single_buffer=True), _const_spec(slab.shape)],
        out_shape=[jax.ShapeDtypeStruct((D_IN, D_MODEL), F32), jax.ShapeDtypeStruct(slab.shape, slab.dtype)],
        scratch_shapes=[
            pltpu.VMEM((8, part, LANES), F32),
            pltpu.VMEM((part, LANES), F32),
            pltpu.VMEM(slab.shape, F32),
            pltpu.SemaphoreType.DMA((14,)),
            pltpu.SemaphoreType.DMA((14,)),
        ],
        compiler_params=_params(("arbitrary",)),
    )(*operands)


HBM = pl.BlockSpec(memory_space=pltpu.HBM)
SEM = pl.BlockSpec(memory_space=pltpu.SEMAPHORE)
DATAFLOW = pltpu.SideEffectType.DATAFLOW_SIDE_EFFECTING
TOKEN = jax.ShapeDtypeStruct((8, LANES), F32)


def _plan_copies(bufs, plan, send_sems, recv_sems):
    out = []
    for i, (src, src_row, dst, dst_row, recv_row, rows, device) in enumerate(plan):
        send = _remote(_rows(bufs[src], src_row, rows), _rows(bufs[dst], dst_row, rows), send_sems.at[i], recv_sems.at[i], device)
        landed = _rows(bufs[dst], recv_row, rows)
        recv = _remote(landed, landed, send_sems.at[i], recv_sems.at[i], device)
        out.append((send, recv))
    return out


def _split_call(name, bufs, wait=None, start=None, after=None):
    n = len(bufs)
    n_in = n + (2 if wait else 0) + (1 if after is not None else 0)
    n_start = len(start(0, 0, 0)) if start else 0

    def body(*refs):
        ins = refs[:n]
        x, y, c = _mesh_position()
        if wait:
            for send, recv in _plan_copies(ins, wait[0](x, y, c), refs[n], refs[n + 1]):
                recv.wait_recv()
                send.wait_send()
        if start:
            for send, _ in _plan_copies(ins, start(x, y, c), refs[n_in + n + 1], refs[n_in + n + 2]):
                send.start()
        token = refs[n_in + n]
        token[...] = jnp.zeros_like(token)

    operands = [pltpu.with_memory_space_constraint(b, pltpu.HBM) for b in bufs]
    in_specs = [HBM] * n
    if wait:
        operands += [wait[1], wait[2]]
        in_specs += [SEM, SEM]
    if after is not None:
        operands.append(after)
        in_specs.append(ANY)
    out_shape = [pltpu.HBM(b.shape, b.dtype) for b in bufs] + [TOKEN]
    out_specs = [HBM] * n + [pl.BlockSpec(memory_space=pltpu.VMEM)]
    if start:
        out_shape += [pltpu.SemaphoreType.DMA((n_start,)), pltpu.SemaphoreType.DMA((n_start,))]
        out_specs += [SEM, SEM]
    outs = pl.pallas_call(
        body,
        name=name,
        in_specs=in_specs,
        out_specs=out_specs,
        out_shape=out_shape,
        input_output_aliases={i: i for i in range(n)},
        compiler_params=pltpu.CompilerParams(has_side_effects=DATAFLOW),
    )(*operands)
    return (list(outs[:n]), outs[n]) + tuple(outs[n + 1 :])


def _direct_gather_plans(shard_rows):
    n = len(shard_rows)

    def direct(x, y, c):
        me = 2 * x + y
        plan = []
        for w, rows in enumerate(shard_rows):
            half = rows // 2
            for px, py in _other_chips(x, y):
                plan.append((w, c * half, n + w, me * rows + c * half, (2 * px + py) * rows + c * half, half, (px, py, c)))
            plan.append((w, 0, n + w, me * rows, me * rows, rows, (x, y, 1 - c)))
        return plan

    def passed_on(x, y, c):
        plan = []
        for w, rows in enumerate(shard_rows):
            half = rows // 2
            for px, py in _other_chips(x, y):
                row = (2 * px + py) * rows
                plan.append((n + w, row + c * half, n + w, row + c * half, row + (1 - c) * half, half, (x, y, 1 - c)))
        return plan

    return direct, passed_on


def _gather_plans(shard_rows):
    n = len(shard_rows)

    def neighbours(x, y):
        return ((1 - x, y), (x, 1 - y))

    def direct(x, y, c):
        me = 2 * x + y
        plan = []
        for w, rows in enumerate(shard_rows):
            half = rows // 2
            for px, py in neighbours(x, y):
                plan.append((w, c * half, n + w, me * rows + c * half, (2 * px + py) * rows + c * half, half, (px, py, c)))
            plan.append((w, 0, n + w, me * rows, me * rows, rows, (x, y, 1 - c)))
        return plan

    def passed_on(x, y, c):
        (xn, yn), diagonal = neighbours(x, y), 2 * (1 - x) + (1 - y)
        relayed = (1 - c) * (2 * xn[0] + xn[1]) + c * (2 * yn[0] + yn[1])
        target = (x * (1 - c) + (1 - x) * c, (1 - y) * (1 - c) + y * c, c)
        plan = []
        for w, rows in enumerate(shard_rows):
            half = rows // 2
            for px, py in (xn, yn):
                row = (2 * px + py) * rows
                plan.append((n + w, row + c * half, n + w, row + c * half, row + (1 - c) * half, half, (x, y, 1 - c)))
            plan.append((n + w, relayed * rows + c * half, n + w, relayed * rows + c * half, diagonal * rows + c * half, half, target))
        return plan

    def diagonal_passed_on(x, y, c):
        plan = []
        for w, rows in enumerate(shard_rows):
            half = rows // 2
            row = (2 * (1 - x) + (1 - y)) * rows
            plan.append((n + w, row + c * half, n + w, row + c * half, row + (1 - c) * half, half, (x, y, 1 - c)))
        return plan

    return direct, passed_on, diagonal_passed_on


def _swap_plan(block_rows):
    n = len(block_rows)

    def plan_fn(x, y, c):
        plan = []
        for w, rows in enumerate(block_rows):
            half = rows // 2
            for j in range(N_CHIPS):
                plan.append((w, j * rows + (1 - c) * half, n + w, j * half, j * half, half, (x, y, 1 - c)))
        return plan

    return plan_fn


def _exchange_plan(halves):
    n = len(halves)

    def plan_fn(x, y, c):
        plan = []
        for w, half in enumerate(halves):
            for kk, (px, py) in enumerate(_other_chips(x, y)):
                plan.append((w, (2 * px + py) * half, n + w, kk * half, kk * half, half, (px, py, c)))
        return plan

    return plan_fn


def _sibling_plan(shard_rows):
    def plan_fn(x, y, c):
        return [(w, c * (rows // 2), w, c * (rows // 2), (1 - c) * (rows // 2), rows // 2, (x, y, 1 - c)) for w, rows in enumerate(shard_rows)]

    return plan_fn


def _shifted(plan_fn, first):
    return lambda x, y, c: [(src + first, a, dst + first, b, r, n, dev) for src, a, dst, b, r, n, dev in plan_fn(x, y, c)]


def _landing(rows, cols, dtype):
    return lax.empty((rows, cols), dtype)


PIPELINE_STEPS = 4


def _row_tile(rows, cap=512):
    best = 8
    for cand in range(8, cap + 1, 8):
        if rows % cand == 0:
            best = cand
    return best


def _pair_sum(name, grad, theirs, pos):
    half = theirs.shape[0] // N_CHIPS
    cols = theirs.shape[1]
    tile = _row_tile(half)
    steps = half // tile

    def body(pos_ref, g_ref, t_ref, p_ref, own_ref):
        total = g_ref[...] + t_ref[...]
        p_ref[...] = total.astype(BF16)

        @pl.when(pl.program_id(1) == pos_ref[1])
        def _():
            own_ref[...] = total

    return pl.pallas_call(
        body,
        name=name,
        grid_spec=pltpu.PrefetchScalarGridSpec(
            num_scalar_prefetch=1,
            grid=(steps, N_CHIPS),
            in_specs=[
                pl.BlockSpec((tile, cols), lambda i, j, pos: ((2 * j + pos[0]) * steps + i, 0)),
                pl.BlockSpec((tile, cols), lambda i, j, pos: (j * steps + i, 0)),
            ],
            out_specs=[
                pl.BlockSpec((tile, cols), lambda i, j, pos: (j * steps + i, 0)),
                pl.BlockSpec((tile, cols), lambda i, j, pos: (i, 0)),
            ],
        ),
        out_shape=[jax.ShapeDtypeStruct((N_CHIPS * half, cols), BF16), jax.ShapeDtypeStruct((half, cols), F32)],
        compiler_params=_params(("parallel", "arbitrary")),
    )(pos, grad, theirs)


def _adamw_update(w, g, m, v):
    nm = ADAM_B1 * m + (1.0 - ADAM_B1) * g
    nv = ADAM_B2 * v + (1.0 - ADAM_B2) * (g * g)
    m_hat = nm / (1.0 - ADAM_B1**ADAM_STEP)
    v_hat = nv / (1.0 - ADAM_B2**ADAM_STEP)
    return -ADAM_LR * (m_hat / (jnp.sqrt(v_hat) + ADAM_EPS) + ADAM_WD * w), nm, nv


def _chip_sum(name, own, landed, pos):
    half, cols = own.shape
    tile = _row_tile(half, cap=half // PIPELINE_STEPS)
    steps = half // tile

    def body(pos_ref, own_ref, l0, l1, l2, o_ref):
        o_ref[...] = ((own_ref[...] + l0[...].astype(F32)) + l1[...].astype(F32)) + l2[...].astype(F32)

    landed_specs = [pl.BlockSpec((tile, cols), lambda i, pos, _k=k: (_k * steps + i, 0)) for k in range(N_CHIPS - 1)]
    return pl.pallas_call(
        body,
        name=name,
        grid_spec=pltpu.PrefetchScalarGridSpec(
            num_scalar_prefetch=1,
            grid=(steps,),
            in_specs=[pl.BlockSpec((tile, cols), lambda i, pos: (i, 0))] + landed_specs,
            out_specs=pl.BlockSpec((tile, cols), lambda i, pos: (pos[0] * steps + i, 0)),
        ),
        out_shape=jax.ShapeDtypeStruct((2 * half, cols), F32),
        compiler_params=_params(("parallel",)),
    )(pos, own, landed, landed, landed)


def _adamw(name, w, g, m, v):
    rows, cols = w.shape
    tile = _row_tile(rows, cap=rows // PIPELINE_STEPS)

    def body(w_ref, g_ref, m_ref, v_ref, g_out_ref, d_ref, nm_ref, nv_ref):
        g = g_ref[...]
        g_out_ref[...] = g
        d_ref[...], nm_ref[...], nv_ref[...] = _adamw_update(w_ref[...], g, m_ref[...], v_ref[...])

    spec = _row_spec(tile, cols)
    return pl.pallas_call(
        body,
        name=name,
        grid=(rows // tile,),
        in_specs=[spec] * 4,
        out_specs=[spec] * 4,
        out_shape=[jax.ShapeDtypeStruct((rows, cols), F32)] * 4,
        compiler_params=_params(("parallel",)),
    )(w, g, m, v)


_SMALL = (
    ("v_ln_g", (D_GMLP,), 8),
    ("v_ln_b", (D_GMLP,), 8),
    ("w_spatial", (N_HEADS, CHUNK, CHUNK), 1024),
    ("b_spatial", (N_HEADS, CHUNK), 8),
    ("sinks", (N_HEADS,), 8),
    ("ln1_g", (D_MODEL,), 8),
    ("ln1_b", (D_MODEL,), 8),
    ("ln2_g", (D_MODEL,), 8),
    ("ln2_b", (D_MODEL,), 8),
    ("squared_error", (D_MODEL,), 8),
)
N_SMALL_PARAMS = len(_SMALL) - 1


def _pack_small(values):
    parts = []
    for (name, shape, rows), val in zip(_SMALL, values, strict=True):
        flat = val.reshape(-1).astype(F32)
        parts.append(jnp.pad(flat, (0, rows * LANES - flat.shape[0])).reshape(rows, LANES))
    parts.append(jnp.zeros((SMALL_ROWS - sum(rows for _, _, rows in _SMALL), LANES), F32))
    return jnp.concatenate(parts, axis=0)


def _adamw_small(g_slab, params, first, second):
    n = N_SMALL_PARAMS

    def pieces(shape):
        if len(shape) == 3:
            return [((0, h), h * shape[1], shape[1], shape[2]) for h in range(shape[0])]
        if len(shape) == 2:
            return [((0,), 0, shape[0], shape[1])]
        if shape[0] >= LANES:
            return [((slice(None), slice(r * LANES, (r + 1) * LANES)), r, 1, LANES) for r in range(shape[0] // LANES)]
        return [((slice(None), slice(0, shape[0])), 0, 1, shape[0])]

    def body(*refs):
        g_ref = refs[0]
        w_refs, m_refs, v_refs = refs[1 : 1 + n], refs[1 + n : 1 + 2 * n], refs[1 + 2 * n : 1 + 3 * n]
        outs = refs[1 + 3 * n :]
        row0 = 0
        for idx, (_, shape, rows) in enumerate(_SMALL[:n]):
            for where, first_row, n_rows, lanes in pieces(shape):
                g = g_ref[row0 + first_row : row0 + first_row + n_rows, 0:lanes]
                delta, nm, nv = _adamw_update(w_refs[idx][where], g, m_refs[idx][where], v_refs[idx][where])
                for group, val in enumerate((g, delta, nm, nv)):
                    outs[group * n + idx][where] = val
            row0 += rows

    vmem = pl.BlockSpec(memory_space=pltpu.VMEM)
    shapes = [jax.ShapeDtypeStruct(p.shape, F32) for p in params]
    outs = pl.pallas_call(
        body,
        name="adamw_small",
        in_specs=[vmem] * (1 + 3 * n),
        out_specs=[vmem] * (4 * n),
        out_shape=shapes * 4,
        compiler_params=_params(),
    )(g_slab, *params, *first, *second)
    return [list(outs[group * n : (group + 1) * n]) for group in range(4)]


def kernel(x, positions, w_in, v_ln_g, v_ln_b, w_spatial, b_spatial, sinks, w_out, ln1_g, ln1_b, w_ff1, w_ff2, ln2_g, ln2_b, loss_target, m_w_in, m_v_ln_g, m_v_ln_b, m_w_spatial, m_b_spatial, m_sinks, m_w_out, m_ln1_g, m_ln1_b, m_w_ff1, m_w_ff2, m_ln2_g, m_ln2_b, v_w_in, v_v_ln_g, v_v_ln_b, v_w_spatial, v_b_spatial, v_sinks, v_w_out, v_ln1_g, v_ln1_b, v_w_ff1, v_w_ff2, v_ln2_g, v_ln2_b):
    t = x.shape[1]
    x2 = x.reshape(t, D_MODEL)
    target = loss_target.reshape(t, D_MODEL)

    w_in_shard = w_in[0].T.astype(BF16)
    in_direct, in_pass = _direct_gather_plans([w_in_shard.shape[0]])
    in_bufs, in_started, in_send, in_recv = _split_call(
        "gather_w_in_start", [w_in_shard, _landing(N_CHIPS * w_in_shard.shape[0], D_MODEL, BF16)], start=in_direct)
    inv_freq = ROPE_THETA ** (-jnp.arange(0, HEAD_DIM, 2, dtype=F32) / HEAD_DIM)
    cos, sin, later = _rope_tables_and_casts(
        positions, jnp.tile(inv_freq, LANES // (HEAD_DIM // 2)).reshape(1, LANES), [w_out[0], w_ff1[0], w_ff2[0]], dep=in_started)
    later_rows = [s.shape[0] for s in later]
    direct_plan, pass_plan, diagonal_plan = _gather_plans(later_rows)
    bufs, started, direct_send, direct_recv = _split_call(
        "gather_start", later + [_landing(N_CHIPS * r, D_MODEL, BF16) for r in later_rows], start=direct_plan, after=cos)
    in_bufs, in_passing, in_pass_send, in_pass_recv = _split_call(
        "gather_w_in_pass", in_bufs, wait=(in_direct, in_send, in_recv), start=in_pass, after=started)
    in_bufs, _ = _split_call("gather_w_in_end", in_bufs, wait=(in_pass, in_pass_send, in_pass_recv), after=in_passing)
    w_in_t = in_bufs[1]

    u, vg, q, k, va = _in_proj(x2, w_in_t, cos, sin)
    bias_full = jnp.repeat(b_spatial[0].T, HEAD_DIM, axis=1)
    sink_vec = sinks.reshape(N_HEADS)
    bufs, passing, pass_send, pass_recv = _split_call(
        "gather_pass", bufs, wait=(direct_plan, direct_send, direct_recv), start=pass_plan, after=u)
    cat = _mixer_fwd(u, vg, q, k, va, v_ln_g, v_ln_b, w_spatial[0], bias_full, sink_vec, dep=passing)
    bufs, passing, diag_send, diag_recv = _split_call(
        "gather_pass_diagonal", bufs, wait=(pass_plan, pass_send, pass_recv), start=diagonal_plan, after=cat)
    bufs, _ = _split_call("gather_end", bufs, wait=(diagonal_plan, diag_send, diag_recv), after=passing)
    w_out_all = bufs[3]
    w1_all = bufs[4].reshape(N_FF_BLOCKS, D_MODEL, D_MODEL)
    w2_all = bufs[5].reshape(N_FF_BLOCKS, D_MODEL, D_MODEL)
    xhat1, rstd1, x1b, r, dz2, dz2b, d_ln2_g, d_ln2_b, sq_err = _ffn_fwd_loss(
        cat, x2, w_out_all, ln1_g, ln1_b, w1_all, w2_all, ln2_g, ln2_b, target)

    pos = jnp.stack([lax.axis_index("c"), 2 * lax.axis_index("x") + lax.axis_index("y")]).astype(jnp.int32)
    half_landing = lambda g: _landing(g.shape[0] // 2, D_MODEL, F32)
    ff_swap_plan = _swap_plan([D_FF // N_CHIPS])
    ff_exchange_plan = _exchange_plan([D_FF // N_CHIPS // 2])
    exchange_landing = lambda p: _landing(3 * p.shape[0] // N_CHIPS, D_MODEL, BF16)
    g_ff1_local, dz1, dz1b, dcat, d_ln1_g, d_ln1_b = _ffn_bwd_ln1(dz2, r, x1b, xhat1, rstd1, ln1_g, w1_all, w2_all, w_out_all)
    ff1_bufs, swapping1, swap1_send, swap1_recv = _split_call("ff1_swap_start", [g_ff1_local, half_landing(g_ff1_local)], start=ff_swap_plan)
    g_out_local = _grad_w_out(cat, dz1b, dep=swapping1)
    ff1_bufs, _ = _split_call("ff1_swap_wait", ff1_bufs, wait=(ff_swap_plan, swap1_send, swap1_recv), after=g_out_local)
    ff1_sum, ff1_own = _pair_sum("grad_pair_sum_w_ff1", ff1_bufs[0], ff1_bufs[1], pos)
    ff1_ex, exchanging1, ex1_send, ex1_recv = _split_call(
        "ff1_exchange_start", [ff1_sum, exchange_landing(ff1_sum)], start=ff_exchange_plan)
    dh_main, dkv, g_ff2_local, d_v_ln_g, d_v_ln_b, d_w_spatial, d_b_spatial_t, d_sinks = _mixer_bwd(
        u, vg, q, k, va, dcat, cos, sin, v_ln_g, v_ln_b, w_spatial[0], bias_full, sink_vec, r, dz2b, dep=exchanging1)
    ff2_bufs, swapping2, swap2_send, swap2_recv = _split_call("ff2_swap_start", [g_ff2_local, half_landing(g_ff2_local)], start=ff_swap_plan)
    g_in_local, small_g = _grad_w_in_t_and_small_all_reduce(dh_main, dkv, x2, _pack_small(
        [d_v_ln_g, d_v_ln_b, d_w_spatial, d_b_spatial_t[:, :N_HEADS].T, d_sinks[0, :N_HEADS], d_ln1_g, d_ln1_b, d_ln2_g, d_ln2_b, sq_err]),
        dep=swapping2)
    sq_row = sum(rows for _, _, rows in _SMALL[:N_SMALL_PARAMS])
    loss = 0.5 * jnp.sum(small_g[sq_row : sq_row + _SMALL[N_SMALL_PARAMS][2]]) / D_MODEL
    ff2_bufs, _ = _split_call("ff2_swap_wait", ff2_bufs, wait=(ff_swap_plan, swap2_send, swap2_recv), after=g_in_local)
    ff2_sum, ff2_own = _pair_sum("grad_pair_sum_w_ff2", ff2_bufs[0], ff2_bufs[1], pos)
    ff2_ex, exchanging2, ex2_send, ex2_recv = _split_call(
        "ff2_exchange_start", [ff2_sum, exchange_landing(ff2_sum)], start=ff_exchange_plan)

    small = [g_in_local, g_out_local]
    small_swap_plan = _swap_plan([g.shape[0] // N_CHIPS for g in small])
    swap_bufs, small_swapping, ss_send, ss_recv = _split_call(
        "small_swap_start", small + [half_landing(g) for g in small], start=small_swap_plan, after=exchanging2)
    grad_x_flat = _grad_x(dh_main, dkv, dz1, w_in_t, dep=small_swapping)
    grad_x = grad_x_flat.reshape(1, t, D_MODEL)
    swap_bufs, _ = _split_call("small_swap_wait", swap_bufs, wait=(small_swap_plan, ss_send, ss_recv), after=grad_x_flat)
    pair_sums = [_pair_sum("grad_pair_sum_" + nm, g, th, pos) for nm, g, th in zip(["w_in", "w_out"], swap_bufs[:2], swap_bufs[2:])]
    small_plan = _exchange_plan([p.shape[0] // N_CHIPS for p, _ in pair_sums])
    small_bufs, small_exchanging, sm_send, sm_recv = _split_call(
        "small_exchange_start", [p for p, _ in pair_sums] + [exchange_landing(p) for p, _ in pair_sums], start=small_plan)

    ff1_ex, _ = _split_call("ff1_exchange_wait", ff1_ex, wait=(ff_exchange_plan, ex1_send, ex1_recv), after=small_exchanging)
    ff_pair_plan = _sibling_plan([D_FF // N_CHIPS])
    half_ff1 = _chip_sum("grad_chip_sum_w_ff1", ff1_own, ff1_ex[1], pos)
    (half_ff1, *ff2_ex), _, g1_send, g1_recv = _split_call(
        "ff2_exchange_wait_ff1_pair_start", [half_ff1] + ff2_ex, wait=(_shifted(ff_exchange_plan, 1), ex2_send, ex2_recv), start=ff_pair_plan)
    half_ff2 = _chip_sum("grad_chip_sum_w_ff2", ff2_own, ff2_ex[1], pos)
    (half_ff2, g_w_ff1), _, g2_send, g2_recv = _split_call(
        "ff1_pair_wait_ff2_pair_start", [half_ff2, half_ff1], wait=(_shifted(ff_pair_plan, 1), g1_send, g1_recv), start=ff_pair_plan)
    g_w_ff1, d_w_ff1, nm_w_ff1, nv_w_ff1 = _adamw("adamw_w_ff1", w_ff1[0], g_w_ff1, m_w_ff1[0], v_w_ff1[0])
    small_bufs, _ = _split_call("small_exchange_wait", small_bufs, wait=(small_plan, sm_send, sm_recv), after=nv_w_ff1)
    shards = [_chip_sum("grad_chip_sum_" + nm, own, ld, pos) for nm, (_, own), ld in zip(["w_in", "w_out"], pair_sums, small_bufs[2:])]
    small_pair_plan = _sibling_plan([s.shape[0] for s in shards])
    (*shards, g_w_ff2), _, g3_send, g3_recv = _split_call(
        "ff2_pair_wait_small_pair_start", shards + [half_ff2], wait=(_shifted(ff_pair_plan, 2), g2_send, g2_recv), start=small_pair_plan)
    g_w_ff2, d_w_ff2, nm_w_ff2, nv_w_ff2 = _adamw("adamw_w_ff2", w_ff2[0], g_w_ff2, m_w_ff2[0], v_w_ff2[0])
    (g_w_in_t, g_w_out), _ = _split_call("small_pair_wait", shards, wait=(small_pair_plan, g3_send, g3_recv), after=nv_w_ff2)
    g_w_in, d_w_in, nm_w_in, nv_w_in = (a.T for a in _adamw("adamw_w_in", w_in[0].T, g_w_in_t, m_w_in[0].T, v_w_in[0].T))
    g_w_out, d_w_out, nm_w_out, nv_w_out = _adamw("adamw_w_out", w_out[0], g_w_out, m_w_out[0], v_w_out[0])
    small_grads, small_d, small_nm, small_nv = _adamw_small(
        small_g,
        [v_ln_g, v_ln_b, w_spatial, b_spatial, sinks, ln1_g, ln1_b, ln2_g, ln2_b],
        [m_v_ln_g, m_v_ln_b, m_w_spatial, m_b_spatial, m_sinks, m_ln1_g, m_ln1_b, m_ln2_g, m_ln2_b],
        [v_v_ln_g, v_v_ln_b, v_w_spatial, v_b_spatial, v_sinks, v_ln1_g, v_ln1_b, v_ln2_g, v_ln2_b])

    def with_big(small, w_in_v, w_out_v, w_ff1_v, w_ff2_v):
        g_vg, g_vb, g_ws, g_bs, g_sk, g_1g, g_1b, g_2g, g_2b = small
        return [w_in_v[None], g_vg, g_vb, g_ws, g_bs, g_sk, w_out_v[None], g_1g, g_1b, w_ff1_v[None], w_ff2_v[None], g_2g, g_2b]

    return (
        loss,
        grad_x,
        *with_big(small_grads, g_w_in, g_w_out, g_w_ff1, g_w_ff2),
        *with_big(small_d, d_w_in, d_w_out, d_w_ff1, d_w_ff2),
        *with_big(small_nm, nm_w_in, nm_w_out, nm_w_ff1, nm_w_ff2),
        *with_big(small_nv, nv_w_in, nv_w_out, nv_w_ff1, nv_w_ff2),
    )
```

```python
import math

import jax
import jax.numpy as jnp
from jax import lax
from jax.experimental import pallas as pl
from jax.experimental.pallas import tpu as pltpu

F32 = jnp.float32
BF16 = jnp.bfloat16

D_MODEL = 1024
HEAD_DIM = 64
D_GMLP = 512
D_ATTN = 512
D_KV = 128
D_IN = 2 * D_GMLP + D_ATTN + 2 * D_KV
D_MAIN = 2 * D_GMLP + D_ATTN
N_HEADS = 8
CHUNK = 128
CHUNKS_PER_STEP = 4
ROPE_THETA = 10000.0
D_FF = 4 * D_MODEL
N_FF_BLOCKS = 4
LN_EPS = 1e-5
ALPHA = (2.0 * 1) ** 0.25
NEG_INF = -1e30
SCALE = 1.0 / math.sqrt(HEAD_DIM)

ADAM_LR = 0.001
ADAM_B1 = 0.9
ADAM_B2 = 0.999
ADAM_EPS = 1e-08
ADAM_WD = 0.01
ADAM_STEP = 10

N_CHIPS = 4
LANES = 128
V7X_VMEM_BYTES = 64 * 1024 * 1024
VMEM_LIMIT = V7X_VMEM_BYTES - 8 * 1024 * 1024
TM = 1024
TM_FFN = 256
TM_FFN_FWD = 512
FFN_PART = 256
TK = 1024
SMALL_ROWS = 1152
MESH = pl.DeviceIdType.MESH

NT = (((1,), (1,)), ((), ()))
TN = (((0,), (0,)), ((), ()))


def _dot(a, b, dims=None):
    if dims is None:
        return jnp.dot(a, b, preferred_element_type=F32)
    return lax.dot_general(a, b, dims, preferred_element_type=F32)


def _params(semantics=None):
    return pltpu.CompilerParams(dimension_semantics=semantics, vmem_limit_bytes=VMEM_LIMIT)


def _const_spec(shape, single_buffer=False):
    zeros = (0,) * len(shape)
    if single_buffer:
        return pl.BlockSpec(shape, lambda *_: zeros, pipeline_mode=pl.Buffered(1))
    return pl.BlockSpec(shape, lambda *_: zeros)


def _row_spec(rows, cols):
    return pl.BlockSpec((rows, cols), lambda i: (i, 0))


def _after(dep, body, in_specs, operands):
    if dep is None:
        return body, list(in_specs), list(operands)
    return (lambda dep_ref, *refs: body(*refs)), [pl.BlockSpec(memory_space=pl.ANY)] + list(in_specs), [dep] + list(operands)


def _gelu(x):
    k = math.sqrt(2.0 / math.pi)
    return 0.5 * x * (1.0 + jnp.tanh(k * (x + 0.044715 * (x * x * x))))


def _gelu_and_grad(x):
    k = math.sqrt(2.0 / math.pi)
    x2 = x * x
    t = jnp.tanh(k * (x + 0.044715 * (x2 * x)))
    g = 0.5 * x * (1.0 + t)
    dg = 0.5 * (1.0 + t) + 0.5 * x * (1.0 - t * t) * (k * (1.0 + 3.0 * 0.044715 * x2))
    return g, dg


def _layer_norm_stats(z):
    mu = jnp.mean(z, axis=-1, keepdims=True)
    zc = z - mu
    var = jnp.mean(zc * zc, axis=-1, keepdims=True)
    rstd = lax.rsqrt(var + LN_EPS)
    return zc * rstd, rstd


def _layer_norm_bwd(dxhat, xhat, rstd):
    m1 = jnp.mean(dxhat, axis=-1, keepdims=True)
    m2 = jnp.mean(dxhat * xhat, axis=-1, keepdims=True)
    return rstd * (dxhat - m1 - xhat * m2)


def _rotate_half(t):
    n = t.shape[1]
    lane = lax.broadcasted_iota(jnp.int32, t.shape, 1)
    first = (lane & (HEAD_DIM // 2)) == 0
    return jnp.where(first, -pltpu.roll(t, n - HEAD_DIM // 2, 1), pltpu.roll(t, HEAD_DIM // 2, 1))


def _rope(t, cos, sin):
    return t * cos + _rotate_half(t) * sin


def _rope_transposed(g, cos, sin):
    return g * cos - _rotate_half(g * sin)


def _lane_tile(a, reps):
    return jnp.tile(a, (1, reps)) if reps > 1 else a


def _in_proj(x, w_in_t, cos, sin, dep=None):
    t = x.shape[0]

    def body(x_ref, w_ref, cos_ref, sin_ref, u_ref, vg_ref, q_ref, k_ref, va_ref):
        xb = x_ref[...].astype(BF16)
        u_ref[...] = _dot(xb, w_ref[0:D_GMLP, :], NT)
        vg_ref[...] = _dot(xb, w_ref[D_GMLP : 2 * D_GMLP, :], NT)
        q = _dot(xb, w_ref[2 * D_GMLP : D_MAIN, :], NT)
        k = _dot(xb, w_ref[D_MAIN : D_MAIN + D_KV, :], NT)
        va_ref[...] = _dot(xb, w_ref[D_MAIN + D_KV : D_IN, :], NT).astype(BF16)
        c, s = cos_ref[...], sin_ref[...]
        q_ref[...] = _rope(q, _lane_tile(c, D_ATTN // LANES), _lane_tile(s, D_ATTN // LANES)).astype(BF16)
        k_ref[...] = _rope(k, c, s).astype(BF16)

    body, in_specs, operands = _after(
        dep, body, [_row_spec(TM, D_MODEL), _const_spec((D_IN, D_MODEL)), _row_spec(TM, LANES), _row_spec(TM, LANES)], [x, w_in_t, cos, sin])
    return pl.pallas_call(
        body,
        name="in_proj",
        grid=(t // TM,),
        in_specs=in_specs,
        out_specs=[_row_spec(TM, D_GMLP), _row_spec(TM, D_GMLP), _row_spec(TM, D_ATTN), _row_spec(TM, D_KV), _row_spec(TM, D_KV)],
        out_shape=[
            jax.ShapeDtypeStruct((t, D_GMLP), F32),
            jax.ShapeDtypeStruct((t, D_GMLP), F32),
            jax.ShapeDtypeStruct((t, D_ATTN), BF16),
            jax.ShapeDtypeStruct((t, D_KV), BF16),
            jax.ShapeDtypeStruct((t, D_KV), BF16),
        ],
        compiler_params=_params(("parallel",)),
    )(*operands)


def _step_rows(i):
    return (i, 0)


def _chunk_before_step(i):
    return (jnp.maximum(CHUNKS_PER_STEP * i - 1, 0), 0)


def _chunk_specs():
    step = CHUNKS_PER_STEP * CHUNK
    return [
        pl.BlockSpec((step, D_GMLP), _step_rows),
        pl.BlockSpec((step, D_GMLP), _step_rows),
        pl.BlockSpec((step, D_ATTN), _step_rows),
        pl.BlockSpec((step, D_KV), _step_rows),
        pl.BlockSpec((CHUNK, D_KV), _chunk_before_step),
        pl.BlockSpec((step, D_KV), _step_rows),
        pl.BlockSpec((CHUNK, D_KV), _chunk_before_step),
    ]


def _half_lane_masks(rows):
    lane = lax.broadcasted_iota(jnp.int32, (rows, LANES), 1)
    return lane < HEAD_DIM


def _kv_variants(kv2):
    left = _half_lane_masks(kv2.shape[0])
    f = kv2.astype(F32)
    swapped = pltpu.roll(f, HEAD_DIM, 1)
    zero = jnp.zeros_like(f)
    g0 = (jnp.where(left, f, zero).astype(BF16), jnp.where(left, zero, swapped).astype(BF16))
    g1 = (jnp.where(left, swapped, zero).astype(BF16), jnp.where(left, zero, f).astype(BF16))
    return (g0, g1)


def _band_mask(i, heads=1):
    row = lax.broadcasted_iota(jnp.int32, (heads * CHUNK, 2 * CHUNK), 0) & (CHUNK - 1)
    col = lax.broadcasted_iota(jnp.int32, (heads * CHUNK, 2 * CHUNK), 1)
    no_prev = jnp.where(i > 0, 0, 4 * CHUNK)
    in_prev = jnp.logical_and(col < CHUNK, (col - row) > no_prev)
    in_cur = jnp.logical_and(col >= CHUNK, (col - CHUNK) <= row)
    return jnp.logical_or(in_prev, in_cur)


def _causal_mask():
    row = lax.broadcasted_iota(jnp.int32, (CHUNK, CHUNK), 0)
    col = lax.broadcasted_iota(jnp.int32, (CHUNK, CHUNK), 1)
    return col <= row


def _store_spatial_weights(w_ref, wcat_ref, wcat_t_ref=None):
    causal = _causal_mask()
    for p in range(D_GMLP // LANES):
        wl = jnp.where(causal, w_ref[2 * p], 0.0)
        wr = jnp.where(causal, w_ref[2 * p + 1], 0.0)
        wcat_ref[p] = jnp.concatenate([wl, wr], axis=1).astype(BF16)
        if wcat_t_ref is not None:
            wcat_t_ref[p] = jnp.concatenate([wl.T, wr.T], axis=1).astype(BF16)


def _pair_stack(xp, left):
    return jnp.concatenate([jnp.where(left, xp, 0.0), jnp.where(left, 0.0, xp)], axis=0).astype(BF16)


def _mixer_fwd(u, vg, q, k, va, v_ln_g, v_ln_b, w_spatial, bias_full, sinks, dep=None):
    t = u.shape[0]

    def body(u_ref, vg_ref, q_ref, kc_ref, kp_ref, vc_ref, vp_ref, g_ref, b_ref, w_ref, bias_ref, sink_ref, cat_ref, wcat):
        i = pl.program_id(0)
        left = _half_lane_masks(CHUNK)

        @pl.when(i == 0)
        def _():
            _store_spatial_weights(w_ref, wcat)

        heads = range(N_HEADS)
        pair_cols = [slice(p * LANES, (p + 1) * LANES) for p in range(D_GMLP // LANES)]
        sinks_h = [sink_ref[h] for h in heads]
        for c in range(CHUNKS_PER_STEP):
            rows = slice(c * CHUNK, (c + 1) * CHUNK)
            before = slice((c - 1) * CHUNK, c * CHUNK)
            k_prev = kp_ref[...] if c == 0 else kc_ref[before, :]
            v_prev = vp_ref[...] if c == 0 else vc_ref[before, :]
            k_var = _kv_variants(jnp.concatenate([k_prev, kc_ref[rows, :]], axis=0))
            v_var = _kv_variants(jnp.concatenate([v_prev, vc_ref[rows, :]], axis=0))
            scores = [_dot(q_ref[rows, pair_cols[h // 2]], k_var[h // 4][h % 2], NT) for h in heads]

            ug = _gelu(u_ref[rows, :])
            xhat, _ = _layer_norm_stats(_gelu(vg_ref[rows, :]))
            vgl = xhat * g_ref[...] + b_ref[...]
            mixed = [_dot(wcat[p], _pair_stack(vgl[:, cols], left)) for p, cols in enumerate(pair_cols)]

            valid = _band_mask(CHUNKS_PER_STEP * i + c)
            masked = [jnp.where(valid, scores[h] * SCALE, NEG_INF) for h in heads]
            maxes = [jnp.maximum(jnp.max(masked[h], axis=1, keepdims=True), sinks_h[h]) for h in heads]
            exps = [jnp.exp(masked[h] - maxes[h]) for h in heads]
            invs = [1.0 / (jnp.sum(exps[h], axis=1, keepdims=True) + jnp.exp(sinks_h[h] - maxes[h])) for h in heads]
            probs = [(exps[h] * invs[h]).astype(BF16) for h in heads]
            for p, cols in enumerate(pair_cols):
                cat_ref[rows, cols] = (ug[:, cols] * (mixed[p] + bias_ref[:, cols])).astype(BF16)
            for p in range(D_ATTN // LANES):
                out = _dot(probs[2 * p], v_var[p // 2][0]) + _dot(probs[2 * p + 1], v_var[p // 2][1])
                cat_ref[rows, D_GMLP + p * LANES : D_GMLP + (p + 1) * LANES] = out.astype(BF16)

    in_specs = _chunk_specs() + [
        _const_spec((1, D_GMLP)),
        _const_spec((1, D_GMLP)),
        _const_spec((N_HEADS, CHUNK, CHUNK)),
        _const_spec((CHUNK, D_GMLP)),
        pl.BlockSpec(memory_space=pltpu.SMEM),
    ]
    body, in_specs, operands = _after(dep, body, in_specs, [u, vg, q, k, k, va, va, v_ln_g, v_ln_b, w_spatial, bias_full, sinks])
    return pl.pallas_call(
        body,
        name="mixer_fwd",
        grid=(t // (CHUNKS_PER_STEP * CHUNK),),
        in_specs=in_specs,
        out_specs=pl.BlockSpec((CHUNKS_PER_STEP * CHUNK, D_MODEL), lambda i: (i, 0)),
        out_shape=jax.ShapeDtypeStruct((t, D_MODEL), BF16),
        scratch_shapes=[pltpu.VMEM((D_GMLP // LANES, CHUNK, 2 * CHUNK), BF16)],
        compiler_params=_params(("arbitrary",)),
    )(*operands)


def _ffn_fwd_loss(cat, x, w_out, ln1_g, ln1_b, w1, w2, ln2_g, ln2_b, target):
    t = x.shape[0]

    def body(cat_ref, x_ref, wo_ref, g1_ref, b1_ref, w1_ref, w2_ref, g2_ref, b2_ref, tgt_ref,
             xh_ref, rstd_ref, x1b_ref, r_ref, dz2_ref, dz2b_ref, dg2_ref, db2_ref, sq_ref):
        @pl.when(pl.program_id(0) == 0)
        def _():
            dg2_ref[...] = jnp.zeros_like(dg2_ref)
            db2_ref[...] = jnp.zeros_like(db2_ref)
            sq_ref[...] = jnp.zeros_like(sq_ref)

        parts = [slice(p * FFN_PART, (p + 1) * FFN_PART) for p in range(TM_FFN_FWD // FFN_PART)]

        def norm1(rows, z):
            xhat1, rstd1 = _layer_norm_stats(z)
            xh_ref[rows, :] = xhat1
            rstd_ref[rows, :] = rstd1
            x1 = xhat1 * g1_ref[...] + b1_ref[...]
            x1b = x1.astype(BF16)
            x1b_ref[rows, :] = x1b
            return x1, x1b

        def feed_forward(rows, x1b, pre):
            ff = None
            for j in range(N_FF_BLOCKS):
                r = jnp.maximum(pre, 0.0)
                r_ref[rows, j * D_MODEL : (j + 1) * D_MODEL] = r.astype(BF16)
                part = _dot((r * r).astype(BF16), w2_ref[j])
                ff = part if ff is None else ff + part
                if j + 1 < N_FF_BLOCKS:
                    pre = _dot(x1b, w1_ref[j + 1])
            return ff

        def norm2_and_loss(rows, x1, ff):
            xhat2, rstd2 = _layer_norm_stats(ALPHA * x1 + ff)
            err = xhat2 * g2_ref[...] + b2_ref[...] - tgt_ref[rows, :]
            sq_ref[...] += jnp.sum(err * err, axis=0, keepdims=True)
            dy = err * (1.0 / D_MODEL)
            dg2_ref[...] += jnp.sum(dy * xhat2, axis=0, keepdims=True)
            db2_ref[...] += jnp.sum(dy, axis=0, keepdims=True)
            dz2 = _layer_norm_bwd(dy * g2_ref[...], xhat2, rstd2)
            dz2_ref[rows, :] = dz2
            dz2b_ref[rows, :] = dz2.astype(BF16)

        projected = _dot(cat_ref[parts[0], :], wo_ref[...])
        last = None
        for i, rows in enumerate(parts):
            z = ALPHA * x_ref[rows, :] + projected
            if i + 1 < len(parts):
                projected = _dot(cat_ref[parts[i + 1], :], wo_ref[...])
            x1, x1b = norm1(rows, z)
            pre = _dot(x1b, w1_ref[0])
            if last is not None:
                norm2_and_loss(*last)
            last = (rows, x1, feed_forward(rows, x1b, pre))
        norm2_and_loss(*last)

    vec = _const_spec((1, D_MODEL))
    tile = _row_spec(TM_FFN_FWD, D_MODEL)
    wspec = _const_spec((N_FF_BLOCKS, D_MODEL, D_MODEL), single_buffer=True)
    return pl.pallas_call(
        body,
        name="ffn_fwd_loss",
        grid=(t // TM_FFN_FWD,),
        in_specs=[tile, tile, _const_spec((D_MODEL, D_MODEL), single_buffer=True), vec, vec, wspec, wspec, vec, vec, tile],
        out_specs=[tile, _row_spec(TM_FFN_FWD, 1), tile, _row_spec(TM_FFN_FWD, D_FF), tile, tile, vec, vec, vec],
        out_shape=[
            jax.ShapeDtypeStruct((t, D_MODEL), F32),
            jax.ShapeDtypeStruct((t, 1), F32),
            jax.ShapeDtypeStruct((t, D_MODEL), BF16),
            jax.ShapeDtypeStruct((t, D_FF), BF16),
            jax.ShapeDtypeStruct((t, D_MODEL), F32),
            jax.ShapeDtypeStruct((t, D_MODEL), BF16),
            jax.ShapeDtypeStruct((1, D_MODEL), F32),
            jax.ShapeDtypeStruct((1, D_MODEL), F32),
            jax.ShapeDtypeStruct((1, D_MODEL), F32),
        ],
        compiler_params=_params(("arbitrary",)),
    )(cat, x, w_out, ln1_g, ln1_b, w1, w2, ln2_g, ln2_b, target)


def _ffn_bwd_ln1(dz2, r, x1b, xhat1, rstd1, ln1_g, w1, w2, w_out, dep=None):
    t = dz2.shape[0]

    def body(dz2_ref, r_ref, x1b_ref, xh_ref, rstd_ref, g1_ref, w1_ref, w2_ref, wo_ref, gw1_ref, dz1_ref, dz1b_ref, dcat_ref, dg1_ref, db1_ref):
        @pl.when(pl.program_id(0) == 0)
        def _():
            dg1_ref[...] = jnp.zeros_like(dg1_ref)
            db1_ref[...] = jnp.zeros_like(db1_ref)
            gw1_ref[...] = jnp.zeros_like(gw1_ref)

        dz2 = dz2_ref[...]
        dz2b = dz2.astype(BF16)
        x1_t = x1b_ref[...].astype(F32).T.astype(BF16)
        dx1 = ALPHA * dz2
        for j in range(N_FF_BLOCKS):
            cols = slice(j * D_MODEL, (j + 1) * D_MODEL)
            dpre = (_dot(dz2b, w2_ref[j], NT) * (2.0 * r_ref[:, cols].astype(F32))).astype(BF16)
            gw1_ref[cols, :] += _dot(x1_t, dpre)
            dx1 = dx1 + _dot(dpre, w1_ref[j], NT)
        xhat1 = xh_ref[...]
        dg1_ref[...] += jnp.sum(dx1 * xhat1, axis=0, keepdims=True)
        db1_ref[...] += jnp.sum(dx1, axis=0, keepdims=True)
        dz1 = _layer_norm_bwd(dx1 * g1_ref[...], xhat1, rstd_ref[...])
        dz1_ref[...] = dz1
        dz1b = dz1.astype(BF16)
        dz1b_ref[...] = dz1b
        dcat_ref[...] = _dot(dz1b, wo_ref[...], NT).astype(BF16)

    vec = _const_spec((1, D_MODEL))
    tile = _row_spec(TM_FFN, D_MODEL)
    wspec = _const_spec((N_FF_BLOCKS, D_MODEL, D_MODEL), single_buffer=True)
    body, in_specs, operands = _after(
        dep, body,
        [tile, _row_spec(TM_FFN, D_FF), tile, tile, _row_spec(TM_FFN, 1), vec, wspec, wspec, _const_spec((D_MODEL, D_MODEL), single_buffer=True)],
        [dz2, r, x1b, xhat1, rstd1, ln1_g, w1, w2, w_out])
    return pl.pallas_call(
        body,
        name="ffn_bwd_ln1",
        grid=(t // TM_FFN,),
        in_specs=in_specs,
        out_specs=[_const_spec((D_FF, D_MODEL), single_buffer=True), tile, tile, tile, vec, vec],
        out_shape=[
            jax.ShapeDtypeStruct((D_FF, D_MODEL), F32),
            jax.ShapeDtypeStruct((t, D_MODEL), F32),
            jax.ShapeDtypeStruct((t, D_MODEL), BF16),
            jax.ShapeDtypeStruct((t, D_MODEL), BF16),
            jax.ShapeDtypeStruct((1, D_MODEL), F32),
            jax.ShapeDtypeStruct((1, D_MODEL), F32),
        ],
        compiler_params=_params(("arbitrary",)),
    )(*operands)


def _mixer_bwd(u, vg, q, k, va, dcat, cos, sin, v_ln_g, v_ln_b, w_spatial, bias_full, sinks, r, dz2b, dep=None):
    t = u.shape[0]
    n_chunks = t // CHUNK
    assert CHUNKS_PER_STEP == N_FF_BLOCKS

    def body(u_ref, vg_ref, q_ref, kc_ref, kp_ref, vc_ref, vp_ref, dcat_ref, cosc_ref, sinc_ref, cosp_ref, sinp_ref,
             g_ref, b_ref, w_ref, bias_ref, sink_ref, r_ref, dz2b_ref,
             dmain_ref, dkv_ref, gw2_ref, dg_ref, db_ref, dw_ref, dbs_ref, dsink_ref, dmix_acc, wcat, wcat_t):
        i = pl.program_id(0)
        left = _half_lane_masks(CHUNK)
        lane = lax.broadcasted_iota(jnp.int32, (CHUNK, LANES), 1)
        n_pairs = D_GMLP // LANES

        @pl.when(i == 0)
        def _():
            dg_ref[...] = jnp.zeros_like(dg_ref)
            db_ref[...] = jnp.zeros_like(db_ref)
            dw_ref[...] = jnp.zeros_like(dw_ref)
            dsink_ref[...] = jnp.zeros_like(dsink_ref)
            dmix_acc[...] = jnp.zeros_like(dmix_acc)
            gw2_ref[...] = jnp.zeros_like(gw2_ref)
            _store_spatial_weights(w_ref, wcat, wcat_t)

        n_qpairs = D_ATTN // LANES
        heads = range(N_HEADS)
        pair_cols = [slice(p * LANES, (p + 1) * LANES) for p in range(n_pairs)]
        sinks_h = [sink_ref[h] for h in heads]
        gain = g_ref[...]
        causal = _causal_mask()
        lane_row = lax.broadcasted_iota(jnp.int32, (1, LANES), 1)
        heads_per_group = N_HEADS // 2

        def group_grad_t(lhs_t, rhs_heads):
            parts = []
            for g in range(2):
                group = range(g * heads_per_group, (g + 1) * heads_per_group)
                lhs = jnp.concatenate([lhs_t[h * HEAD_DIM : (h + 1) * HEAD_DIM] for h in group], axis=1)
                parts.append(_dot(lhs, jnp.concatenate([rhs_heads[h] for h in group], axis=0)))
            return jnp.concatenate(parts, axis=0)

        for c in range(CHUNKS_PER_STEP):
            chunk = CHUNKS_PER_STEP * i + c
            rows = slice(c * CHUNK, (c + 1) * CHUNK)
            before = slice((c - 1) * CHUNK, c * CHUNK)

            k_prev = kp_ref[...] if c == 0 else kc_ref[before, :]
            v_prev = vp_ref[...] if c == 0 else vc_ref[before, :]
            k_var = _kv_variants(jnp.concatenate([k_prev, kc_ref[rows, :]], axis=0))
            v_var = _kv_variants(jnp.concatenate([v_prev, vc_ref[rows, :]], axis=0))
            q_pairs = [q_ref[rows, cols] for cols in pair_cols]
            do_all = dcat_ref[rows, D_GMLP:D_MODEL]
            do_pairs = [do_all[:, cols] for cols in pair_cols]
            scores = [_dot(q_pairs[h // 2], k_var[h // 4][h % 2], NT) for h in heads]
            dprobs = [_dot(do_pairs[h // 2], v_var[h // 4][h % 2], NT) for h in heads]
            q_t = q_ref[rows, :].astype(F32).T.astype(BF16)
            do_t = do_all.astype(F32).T.astype(BF16)

            ff_cols = slice(c * D_MODEL, (c + 1) * D_MODEL)
            relu_block = r_ref[:, ff_cols]
            gw2_ref[ff_cols, :] += _dot(relu_block * relu_block, dz2b_ref[...], TN)

            ug, dug_du = _gelu_and_grad(u_ref[rows, :])
            gv, dgv_dv = _gelu_and_grad(vg_ref[rows, :])
            xhat, rstd = _layer_norm_stats(gv)
            vgl = xhat * gain + b_ref[...]
            mixed = [_dot(wcat[p], _pair_stack(vgl[:, cols], left)) for p, cols in enumerate(pair_cols)]

            valid = _band_mask(chunk)
            masked = [jnp.where(valid, scores[h] * SCALE, NEG_INF) for h in heads]
            maxes = [jnp.maximum(jnp.max(masked[h], axis=1, keepdims=True), sinks_h[h]) for h in heads]
            exps = [jnp.exp(masked[h] - maxes[h]) for h in heads]
            exp_sinks = [jnp.exp(sinks_h[h] - maxes[h]) for h in heads]
            invs = [1.0 / (jnp.sum(exps[h], axis=1, keepdims=True) + exp_sinks[h]) for h in heads]
            probs = [exps[h] * invs[h] for h in heads]
            dsums = [jnp.sum(probs[h] * dprobs[h], axis=1, keepdims=True) for h in heads]
            ds_b = [(probs[h] * (dprobs[h] - dsums[h]) * SCALE).astype(BF16) for h in heads]
            probs_b = [probs[h].astype(BF16) for h in heads]

            dm_stacks = []
            for p, cols in enumerate(pair_cols):
                da = dcat_ref[rows, cols].astype(F32)
                dmain_ref[rows, cols] = (da * (mixed[p] + bias_ref[:, cols]) * dug_du[:, cols]).astype(BF16)
                dmixed = da * ug[:, cols]
                dmix_acc[:, cols] += dmixed
                dm_stacks.append(_pair_stack(dmixed, left))

            dq_all = jnp.concatenate(
                [_dot(ds_b[2 * p], k_var[p // 2][0]) + _dot(ds_b[2 * p + 1], k_var[p // 2][1]) for p in range(n_qpairs)], axis=1)
            dk2_t = group_grad_t(q_t, ds_b)
            dv2_t = group_grad_t(do_t, probs_b)

            for p, cols in enumerate(pair_cols):
                dw_pair = _dot(dm_stacks[p], vgl[:, cols].astype(BF16), NT)
                dw_ref[2 * p] += jnp.where(causal, dw_pair[:CHUNK], 0.0)
                dw_ref[2 * p + 1] += jnp.where(causal, dw_pair[CHUNK:], 0.0)
            dvgl = jnp.concatenate([_dot(wcat_t[p], dm_stacks[p]) for p in range(n_pairs)], axis=1)

            dsink_row = jnp.zeros((1, LANES), F32)
            for h in heads:
                d_sink = -jnp.sum(exp_sinks[h] * invs[h] * dsums[h], axis=0, keepdims=True)
                dsink_row = dsink_row + jnp.where(lane_row == h, d_sink, 0.0)
            dsink_ref[0:1, :] += dsink_row
            cos_c, sin_c = cosc_ref[rows, :], sinc_ref[rows, :]
            cos_p = cosp_ref[...] if c == 0 else cosc_ref[before, :]
            sin_p = sinp_ref[...] if c == 0 else sinc_ref[before, :]
            dmain_ref[rows, 2 * D_GMLP : D_MAIN] = _rope_transposed(dq_all, _lane_tile(cos_c, n_qpairs), _lane_tile(sin_c, n_qpairs)).astype(BF16)
            dk2 = dk2_t.T
            dv2 = dv2_t.T
            cur = pl.ds(pl.multiple_of(chunk * CHUNK, CHUNK), CHUNK)
            dkv_ref[cur, 0:D_KV] = _rope_transposed(dk2[CHUNK:], cos_c, sin_c)
            dkv_ref[cur, D_KV : 2 * D_KV] = dv2[CHUNK:]
            prev = pl.ds(pl.multiple_of(jnp.maximum(chunk - 1, 0) * CHUNK, CHUNK), CHUNK)
            dkv_ref[prev, 0:D_KV] += _rope_transposed(dk2[:CHUNK], cos_p, sin_p)
            dkv_ref[prev, D_KV : 2 * D_KV] += dv2[:CHUNK]

            dg_ref[...] += jnp.sum(dvgl * xhat, axis=0, keepdims=True)
            db_ref[...] += jnp.sum(dvgl, axis=0, keepdims=True)
            dgv = _layer_norm_bwd(dvgl * gain, xhat, rstd)
            dmain_ref[rows, D_GMLP : 2 * D_GMLP] = (dgv * dgv_dv).astype(BF16)

        @pl.when(i == n_chunks // CHUNKS_PER_STEP - 1)
        def _():
            tile = jnp.zeros((CHUNK, LANES), F32)
            for p, cols in enumerate(pair_cols):
                dm = dmix_acc[:, cols]
                sl = jnp.sum(jnp.where(left, dm, 0.0), axis=1, keepdims=True)
                sr = jnp.sum(jnp.where(left, 0.0, dm), axis=1, keepdims=True)
                tile = jnp.where(lane == 2 * p, sl, tile)
                tile = jnp.where(lane == 2 * p + 1, sr, tile)
            dbs_ref[...] = tile

    step = CHUNKS_PER_STEP * CHUNK
    in_specs = _chunk_specs() + [
        pl.BlockSpec((step, D_MODEL), _step_rows),
        pl.BlockSpec((step, LANES), _step_rows),
        pl.BlockSpec((step, LANES), _step_rows),
        pl.BlockSpec((CHUNK, LANES), _chunk_before_step),
        pl.BlockSpec((CHUNK, LANES), _chunk_before_step),
        _const_spec((1, D_GMLP)),
        _const_spec((1, D_GMLP)),
        _const_spec((N_HEADS, CHUNK, CHUNK)),
        _const_spec((CHUNK, D_GMLP)),
        pl.BlockSpec(memory_space=pltpu.SMEM),
        pl.BlockSpec((step, D_FF), _step_rows),
        pl.BlockSpec((step, D_MODEL), _step_rows),
    ]
    body, in_specs, operands = _after(
        dep, body, in_specs, [u, vg, q, k, k, va, va, dcat, cos, sin, cos, sin, v_ln_g, v_ln_b, w_spatial, bias_full, sinks, r, dz2b])
    return pl.pallas_call(
        body,
        name="mixer_bwd",
        grid=(n_chunks // CHUNKS_PER_STEP,),
        in_specs=in_specs,
        out_specs=[
            pl.BlockSpec((step, D_MAIN), _step_rows),
            _const_spec((t, 2 * D_KV)),
            _const_spec((D_FF, D_MODEL), single_buffer=True),
            _const_spec((1, D_GMLP)),
            _const_spec((1, D_GMLP)),
            _const_spec((N_HEADS, CHUNK, CHUNK)),
            _const_spec((CHUNK, LANES)),
            _const_spec((8, LANES)),
        ],
        out_shape=[
            jax.ShapeDtypeStruct((t, D_MAIN), BF16),
            jax.ShapeDtypeStruct((t, 2 * D_KV), F32),
            jax.ShapeDtypeStruct((D_FF, D_MODEL), F32),
            jax.ShapeDtypeStruct((1, D_GMLP), F32),
            jax.ShapeDtypeStruct((1, D_GMLP), F32),
            jax.ShapeDtypeStruct((N_HEADS, CHUNK, CHUNK), F32),
            jax.ShapeDtypeStruct((CHUNK, LANES), F32),
            jax.ShapeDtypeStruct((8, LANES), F32),
        ],
        scratch_shapes=[
            pltpu.VMEM((CHUNK, D_GMLP), F32),
            pltpu.VMEM((D_GMLP // LANES, CHUNK, 2 * CHUNK), BF16),
            pltpu.VMEM((D_GMLP // LANES, CHUNK, 2 * CHUNK), BF16),
        ],
        compiler_params=_params(("arbitrary",)),
    )(*operands)


def _grad_x(dh_main, dkv, dz1, w_in_t, dep=None):
    t = dz1.shape[0]

    def body(dm_ref, dkv_ref, dz1_ref, w_ref, gx_ref):
        acc = ALPHA * dz1_ref[...] + _dot(dm_ref[...], w_ref[0:D_MAIN, :])
        gx_ref[...] = acc + _dot(dkv_ref[...].astype(BF16), w_ref[D_MAIN:D_IN, :])

    body, in_specs, operands = _after(
        dep, body, [_row_spec(TM, D_MAIN), _row_spec(TM, 2 * D_KV), _row_spec(TM, D_MODEL), _const_spec((D_IN, D_MODEL))], [dh_main, dkv, dz1, w_in_t])
    return pl.pallas_call(
        body,
        name="grad_x",
        grid=(t // TM,),
        in_specs=in_specs,
        out_specs=_row_spec(TM, D_MODEL),
        out_shape=jax.ShapeDtypeStruct((t, D_MODEL), F32),
        compiler_params=_params(("parallel",)),
    )(*operands)


def _token_contraction(name, out_rows, tk, in_arrays, contributions, dep=None):
    t = in_arrays[0].shape[0]

    def body(*refs):
        out_ref = refs[-1]

        @pl.when(pl.program_id(0) == 0)
        def _():
            out_ref[...] = jnp.zeros_like(out_ref)

        for row0, a, b in contributions(*refs[:-1]):
            out_ref[row0 : row0 + a.shape[1], :] += _dot(a, b, TN)

    in_specs = [_row_spec(tk, a.shape[1]) for a in in_arrays]
    body, in_specs, operands = _after(dep, body, in_specs, in_arrays)
    return pl.pallas_call(
        body,
        name=name,
        grid=(t // tk,),
        in_specs=in_specs,
        out_specs=_const_spec((out_rows, D_MODEL), single_buffer=True),
        out_shape=jax.ShapeDtypeStruct((out_rows, D_MODEL), F32),
        compiler_params=_params(("arbitrary",)),
    )(*operands)


def _grad_w_out(cat, dz1b, dep=None):
    def contributions(cat_ref, dz1_ref):
        return [(0, cat_ref[...], dz1_ref[...])]

    return _token_contraction("grad_w_out", D_MODEL, TK, [cat, dz1b], contributions, dep)


ANY = pl.BlockSpec(memory_space=pl.ANY)


def _mesh_position():
    return lax.axis_index("x"), lax.axis_index("y"), lax.axis_index("c")


def _other_chips(x, y):
    return [(1 - x, y), (x, 1 - y), (1 - x, 1 - y)]


def _remote(src, dst, send_sem, recv_sem, device):
    return pltpu.make_async_remote_copy(src_ref=src, dst_ref=dst, send_sem=send_sem, recv_sem=recv_sem, device_id=device, device_id_type=MESH)


def _rows(ref, start, size):
    return ref.at[pl.ds(start, size), :]


def _rope_tables_and_casts(pos_row, inv_freq_row, shards, dep=None):
    t = pos_row.shape[1]
    steps = t // TM
    n = len(shards)

    def body(pos_ref, f_ref, *rest):
        f32_refs, (cos_ref, sin_ref), bf16_refs = rest[:n], rest[n : n + 2], rest[n + 2 :]
        for src, dst in zip(f32_refs, bf16_refs):
            dst[...] = src[...].astype(BF16)
        pos_rows = jnp.broadcast_to(pos_ref[...].astype(F32), (LANES, TM)).T
        ang = pos_rows * f_ref[...]
        cos_ref[...] = jnp.cos(ang)
        sin_ref[...] = jnp.sin(ang)

    shard_specs = [_row_spec(s.shape[0] // steps, s.shape[1]) for s in shards]
    body, in_specs, operands = _after(
        dep, body, [pl.BlockSpec((1, TM), lambda i: (0, i)), _const_spec((1, LANES))] + shard_specs, [pos_row, inv_freq_row, *shards])
    outs = pl.pallas_call(
        body,
        name="rope_tables_and_casts",
        grid=(steps,),
        in_specs=in_specs,
        out_specs=[_row_spec(TM, LANES), _row_spec(TM, LANES)] + shard_specs,
        out_shape=[jax.ShapeDtypeStruct((t, LANES), F32)] * 2 + [jax.ShapeDtypeStruct(s.shape, BF16) for s in shards],
        compiler_params=_params(("parallel",)),
    )(*operands)
    return outs[0], outs[1], list(outs[2:])


def _grad_w_in_t_and_small_all_reduce(dh_main, dkv, x, slab, dep=None):
    t = x.shape[0]
    steps = t // TK
    rows = slab.shape[0]
    part = rows // 8

    def body(dm_ref, dkv_ref, x_ref, slab_ref, grad_ref, sum_ref, landing, reduced, gathered, send_sems, recv_sems):
        k = pl.program_id(0)
        x_, y_, c_ = _mesh_position()
        me = 4 * x_ + 2 * y_ + c_
        flips = [(f >> 2, (f >> 1) & 1, f & 1) for f in range(1, 8)]

        def peer(flip):
            fx, fy, fc = flip
            return (1 - x_ if fx else x_, 1 - y_ if fy else y_, 1 - c_ if fc else c_)

        def part_of(ref, device):
            return ref.at[pl.ds(pl.multiple_of(device * part, 8), part), :]

        def scatter_copies():
            out = []
            for kk, flip in enumerate(flips):
                px, py, pc = peer(flip)
                them = 4 * px + 2 * py + pc
                send = _remote(part_of(slab_ref, them), landing.at[me], send_sems.at[kk], recv_sems.at[kk], (px, py, pc))
                recv = _remote(landing.at[them], landing.at[them], send_sems.at[kk], recv_sems.at[kk], (px, py, pc))
                out.append((send, recv))
            return out

        def gather_copies():
            out = []
            for kk, flip in enumerate(flips):
                px, py, pc = peer(flip)
                them = 4 * px + 2 * py + pc
                send = _remote(reduced, part_of(gathered, me), send_sems.at[7 + kk], recv_sems.at[7 + kk], (px, py, pc))
                recv = _remote(part_of(gathered, them), part_of(gathered, them), send_sems.at[7 + kk], recv_sems.at[7 + kk], (px, py, pc))
                out.append((send, recv))
            return out

        @pl.when(k == 0)
        def _():
            grad_ref[...] = jnp.zeros_like(grad_ref)
            for send, _ in scatter_copies():
                send.start()
            landing[me] = part_of(slab_ref, me)[...]

        @pl.when(k == steps // 2)
        def _():
            for _, recv in scatter_copies():
                recv.wait_recv()
            total = landing[0]
            for s in range(1, 8):
                total = total + landing[s]
            reduced[...] = total
            part_of(gathered, me)[...] = total
            for send, _ in gather_copies():
                send.start()

        xb = x_ref[...].astype(BF16)
        grad_ref[0:D_MAIN, :] += _dot(dm_ref[...], xb, TN)
        grad_ref[D_MAIN:D_IN, :] += _dot(dkv_ref[...].astype(BF16), xb, TN)

        @pl.when(k == steps - 1)
        def _():
            for send, recv in gather_copies():
                recv.wait_recv()
                send.wait_send()
            for send, _ in scatter_copies():
                send.wait_send()
            sum_ref[...] = gathered[...]

    body, in_specs, operands = _after(
        dep, body, [_row_spec(TK, D_MAIN), _row_spec(TK, 2 * D_KV), _row_spec(TK, D_MODEL), _const_spec(slab.shape)], [dh_main, dkv, x, slab])
    return pl.pallas_call(
        body,
        name="grad_w_in_and_small_all_reduce",
        grid=(steps,),
        in_specs=in_specs,
        out_specs=[_const_spec((D_IN, D_MODEL), single_buffer=True), _const_spec(slab.shape)],
        out_shape=[jax.ShapeDtypeStruct((D_IN, D_MODEL), F32), jax.ShapeDtypeStruct(slab.shape, slab.dtype)],
        scratch_shapes=[
            pltpu.VMEM((8, part, LANES), F32),
            pltpu.VMEM((part, LANES), F32),
            pltpu.VMEM(slab.shape, F32),
            pltpu.SemaphoreType.DMA((14,)),
            pltpu.SemaphoreType.DMA((14,)),
        ],
        compiler_params=_params(("arbitrary",)),
    )(*operands)


HBM = pl.BlockSpec(memory_space=pltpu.HBM)
SEM = pl.BlockSpec(memory_space=pltpu.SEMAPHORE)
DATAFLOW = pltpu.SideEffectType.DATAFLOW_SIDE_EFFECTING
TOKEN = jax.ShapeDtypeStruct((8, LANES), F32)


def _plan_copies(bufs, plan, send_sems, recv_sems):
    out = []
    for i, (src, src_row, dst, dst_row, recv_row, rows, device) in enumerate(plan):
        send = _remote(_rows(bufs[src], src_row, rows), _rows(bufs[dst], dst_row, rows), send_sems.at[i], recv_sems.at[i], device)
        landed = _rows(bufs[dst], recv_row, rows)
        recv = _remote(landed, landed, send_sems.at[i], recv_sems.at[i], device)
        out.append((send, recv))
    return out


def _split_call(name, bufs, wait=None, start=None, after=None):
    n = len(bufs)
    n_in = n + (2 if wait else 0) + (1 if after is not None else 0)
    n_start = len(start(0, 0, 0)) if start else 0

    def body(*refs):
        ins = refs[:n]
        x, y, c = _mesh_position()
        if wait:
            for send, recv in _plan_copies(ins, wait[0](x, y, c), refs[n], refs[n + 1]):
                recv.wait_recv()
                send.wait_send()
        if start:
            for send, _ in _plan_copies(ins, start(x, y, c), refs[n_in + n + 1], refs[n_in + n + 2]):
                send.start()
        token = refs[n_in + n]
        token[...] = jnp.zeros_like(token)

    operands = [pltpu.with_memory_space_constraint(b, pltpu.HBM) for b in bufs]
    in_specs = [HBM] * n
    if wait:
        operands += [wait[1], wait[2]]
        in_specs += [SEM, SEM]
    if after is not None:
        operands.append(after)
        in_specs.append(ANY)
    out_shape = [pltpu.HBM(b.shape, b.dtype) for b in bufs] + [TOKEN]
    out_specs = [HBM] * n + [pl.BlockSpec(memory_space=pltpu.VMEM)]
    if start:
        out_shape += [pltpu.SemaphoreType.DMA((n_start,)), pltpu.SemaphoreType.DMA((n_start,))]
        out_specs += [SEM, SEM]
    outs = pl.pallas_call(
        body,
        name=name,
        in_specs=in_specs,
        out_specs=out_specs,
        out_shape=out_shape,
        input_output_aliases={i: i for i in range(n)},
        compiler_params=pltpu.CompilerParams(has_side_effects=DATAFLOW),
    )(*operands)
    return (list(outs[:n]), outs[n]) + tuple(outs[n + 1 :])


def _direct_gather_plans(shard_rows):
    n = len(shard_rows)

    def direct(x, y, c):
        me = 2 * x + y
        plan = []
        for w, rows in enumerate(shard_rows):
            half = rows // 2
            for px, py in _other_chips(x, y):
                plan.append((w, c * half, n + w, me * rows + c * half, (2 * px + py) * rows + c * half, half, (px, py, c)))
            plan.append((w, 0, n + w, me * rows, me * rows, rows, (x, y, 1 - c)))
        return plan

    def passed_on(x, y, c):
        plan = []
        for w, rows in enumerate(shard_rows):
            half = rows // 2
            for px, py in _other_chips(x, y):
                row = (2 * px + py) * rows
                plan.append((n + w, row + c * half, n + w, row + c * half, row + (1 - c) * half, half, (x, y, 1 - c)))
        return plan

    return direct, passed_on


def _gather_plans(shard_rows):
    n = len(shard_rows)

    def neighbours(x, y):
        return ((1 - x, y), (x, 1 - y))

    def direct(x, y, c):
        me = 2 * x + y
        plan = []
        for w, rows in enumerate(shard_rows):
            half = rows // 2
            for px, py in neighbours(x, y):
                plan.append((w, c * half, n + w, me * rows + c * half, (2 * px + py) * rows + c * half, half, (px, py, c)))
            plan.append((w, 0, n + w, me * rows, me * rows, rows, (x, y, 1 - c)))
        return plan

    def passed_on(x, y, c):
        (xn, yn), diagonal = neighbours(x, y), 2 * (1 - x) + (1 - y)
        relayed = (1 - c) * (2 * xn[0] + xn[1]) + c * (2 * yn[0] + yn[1])
        target = (x * (1 - c) + (1 - x) * c, (1 - y) * (1 - c) + y * c, c)
        plan = []
        for w, rows in enumerate(shard_rows):
            half = rows // 2
            for px, py in (xn, yn):
                row = (2 * px + py) * rows
                plan.append((n + w, row + c * half, n + w, row + c * half, row + (1 - c) * half, half, (x, y, 1 - c)))
            plan.append((n + w, relayed * rows + c * half, n + w, relayed * rows + c * half, diagonal * rows + c * half, half, target))
        return plan

    def diagonal_passed_on(x, y, c):
        plan = []
        for w, rows in enumerate(shard_rows):
            half = rows // 2
            row = (2 * (1 - x) + (1 - y)) * rows
            plan.append((n + w, row + c * half, n + w, row + c * half, row + (1 - c) * half, half, (x, y, 1 - c)))
        return plan

    return direct, passed_on, diagonal_passed_on


def _swap_plan(block_rows):
    n = len(block_rows)

    def plan_fn(x, y, c):
        plan = []
        for w, rows in enumerate(block_rows):
            half = rows // 2
            for j in range(N_CHIPS):
                plan.append((w, j * rows + (1 - c) * half, n + w, j * half, j * half, half, (x, y, 1 - c)))
        return plan

    return plan_fn


def _exchange_plan(halves):
    n = len(halves)

    def plan_fn(x, y, c):
        plan = []
        for w, half in enumerate(halves):
            for kk, (px, py) in enumerate(_other_chips(x, y)):
                plan.append((w, (2 * px + py) * half, n + w, kk * half, kk * half, half, (px, py, c)))
        return plan

    return plan_fn


def _sibling_plan(shard_rows):
    def plan_fn(x, y, c):
        return [(w, c * (rows // 2), w, c * (rows // 2), (1 - c) * (rows // 2), rows // 2, (x, y, 1 - c)) for w, rows in enumerate(shard_rows)]

    return plan_fn


def _shifted(plan_fn, first):
    return lambda x, y, c: [(src + first, a, dst + first, b, r, n, dev) for src, a, dst, b, r, n, dev in plan_fn(x, y, c)]


def _landing(rows, cols, dtype):
    return lax.empty((rows, cols), dtype)


def _row_tile(rows, cap=512):
    best = 8
    for cand in range(8, cap + 1, 8):
        if rows % cand == 0:
            best = cand
    return best


def _pair_sum(name, grad, theirs, pos):
    half = theirs.shape[0] // N_CHIPS
    cols = theirs.shape[1]
    tile = _row_tile(half)
    steps = half // tile

    def body(pos_ref, g_ref, t_ref, p_ref, own_ref):
        total = g_ref[...] + t_ref[...]
        p_ref[...] = total.astype(BF16)

        @pl.when(pl.program_id(1) == pos_ref[1])
        def _():
            own_ref[...] = total

    return pl.pallas_call(
        body,
        name=name,
        grid_spec=pltpu.PrefetchScalarGridSpec(
            num_scalar_prefetch=1,
            grid=(steps, N_CHIPS),
            in_specs=[
                pl.BlockSpec((tile, cols), lambda i, j, pos: ((2 * j + pos[0]) * steps + i, 0)),
                pl.BlockSpec((tile, cols), lambda i, j, pos: (j * steps + i, 0)),
            ],
            out_specs=[
                pl.BlockSpec((tile, cols), lambda i, j, pos: (j * steps + i, 0)),
                pl.BlockSpec((tile, cols), lambda i, j, pos: (i, 0)),
            ],
        ),
        out_shape=[jax.ShapeDtypeStruct((N_CHIPS * half, cols), BF16), jax.ShapeDtypeStruct((half, cols), F32)],
        compiler_params=_params(("parallel", "arbitrary")),
    )(pos, grad, theirs)


def _adamw_update(w, g, m, v):
    nm = ADAM_B1 * m + (1.0 - ADAM_B1) * g
    nv = ADAM_B2 * v + (1.0 - ADAM_B2) * (g * g)
    m_hat = nm / (1.0 - ADAM_B1**ADAM_STEP)
    v_hat = nv / (1.0 - ADAM_B2**ADAM_STEP)
    return -ADAM_LR * (m_hat / (jnp.sqrt(v_hat) + ADAM_EPS) + ADAM_WD * w), nm, nv


def _chip_sum(name, own, landed, pos):
    half, cols = own.shape
    tile = _row_tile(half)
    steps = half // tile

    def body(pos_ref, own_ref, l0, l1, l2, o_ref):
        o_ref[...] = ((own_ref[...] + l0[...].astype(F32)) + l1[...].astype(F32)) + l2[...].astype(F32)

    landed_specs = [pl.BlockSpec((tile, cols), lambda i, pos, _k=k: (_k * steps + i, 0)) for k in range(N_CHIPS - 1)]
    return pl.pallas_call(
        body,
        name=name,
        grid_spec=pltpu.PrefetchScalarGridSpec(
            num_scalar_prefetch=1,
            grid=(steps,),
            in_specs=[pl.BlockSpec((tile, cols), lambda i, pos: (i, 0))] + landed_specs,
            out_specs=pl.BlockSpec((tile, cols), lambda i, pos: (pos[0] * steps + i, 0)),
        ),
        out_shape=jax.ShapeDtypeStruct((2 * half, cols), F32),
        compiler_params=_params(("parallel",)),
    )(pos, own, landed, landed, landed)


def _adamw(name, w, g, m, v):
    rows, cols = w.shape
    tile = rows if rows * cols <= 256 * 1024 else _row_tile(rows)

    def body(w_ref, g_ref, m_ref, v_ref, g_out_ref, d_ref, nm_ref, nv_ref):
        g = g_ref[...]
        g_out_ref[...] = g
        d_ref[...], nm_ref[...], nv_ref[...] = _adamw_update(w_ref[...], g, m_ref[...], v_ref[...])

    spec = _row_spec(tile, cols)
    return pl.pallas_call(
        body,
        name=name,
        grid=(rows // tile,),
        in_specs=[spec] * 4,
        out_specs=[spec] * 4,
        out_shape=[jax.ShapeDtypeStruct((rows, cols), F32)] * 4,
        compiler_params=_params(("parallel",)),
    )(w, g, m, v)


_SMALL = (
    ("v_ln_g", (D_GMLP,), 8),
    ("v_ln_b", (D_GMLP,), 8),
    ("w_spatial", (N_HEADS, CHUNK, CHUNK), 1024),
    ("b_spatial", (N_HEADS, CHUNK), 8),
    ("sinks", (N_HEADS,), 8),
    ("ln1_g", (D_MODEL,), 8),
    ("ln1_b", (D_MODEL,), 8),
    ("ln2_g", (D_MODEL,), 8),
    ("ln2_b", (D_MODEL,), 8),
    ("squared_error", (D_MODEL,), 8),
)
N_SMALL_PARAMS = len(_SMALL) - 1


def _pack_small(values):
    parts = []
    for (name, shape, rows), val in zip(_SMALL, values, strict=True):
        flat = val.reshape(-1).astype(F32)
        parts.append(jnp.pad(flat, (0, rows * LANES - flat.shape[0])).reshape(rows, LANES))
    parts.append(jnp.zeros((SMALL_ROWS - sum(rows for _, _, rows in _SMALL), LANES), F32))
    return jnp.concatenate(parts, axis=0)


def _adamw_small(g_slab, params, first, second):
    n = N_SMALL_PARAMS

    def pieces(shape):
        if len(shape) == 3:
            return [((0, h), h * shape[1], shape[1], shape[2]) for h in range(shape[0])]
        if len(shape) == 2:
            return [((0,), 0, shape[0], shape[1])]
        if shape[0] >= LANES:
            return [((slice(None), slice(r * LANES, (r + 1) * LANES)), r, 1, LANES) for r in range(shape[0] // LANES)]
        return [((slice(None), slice(0, shape[0])), 0, 1, shape[0])]

    def body(*refs):
        g_ref = refs[0]
        w_refs, m_refs, v_refs = refs[1 : 1 + n], refs[1 + n : 1 + 2 * n], refs[1 + 2 * n : 1 + 3 * n]
        outs = refs[1 + 3 * n :]
        row0 = 0
        for idx, (_, shape, rows) in enumerate(_SMALL[:n]):
            for where, first_row, n_rows, lanes in pieces(shape):
                g = g_ref[row0 + first_row : row0 + first_row + n_rows, 0:lanes]
                delta, nm, nv = _adamw_update(w_refs[idx][where], g, m_refs[idx][where], v_refs[idx][where])
                for group, val in enumerate((g, delta, nm, nv)):
                    outs[group * n + idx][where] = val
            row0 += rows

    vmem = pl.BlockSpec(memory_space=pltpu.VMEM)
    shapes = [jax.ShapeDtypeStruct(p.shape, F32) for p in params]
    outs = pl.pallas_call(
        body,
        name="adamw_small",
        in_specs=[vmem] * (1 + 3 * n),
        out_specs=[vmem] * (4 * n),
        out_shape=shapes * 4,
        compiler_params=_params(),
    )(g_slab, *params, *first, *second)
    return [list(outs[group * n : (group + 1) * n]) for group in range(4)]


def kernel(x, positions, w_in, v_ln_g, v_ln_b, w_spatial, b_spatial, sinks, w_out, ln1_g, ln1_b, w_ff1, w_ff2, ln2_g, ln2_b, loss_target, m_w_in, m_v_ln_g, m_v_ln_b, m_w_spatial, m_b_spatial, m_sinks, m_w_out, m_ln1_g, m_ln1_b, m_w_ff1, m_w_ff2, m_ln2_g, m_ln2_b, v_w_in, v_v_ln_g, v_v_ln_b, v_w_spatial, v_b_spatial, v_sinks, v_w_out, v_ln1_g, v_ln1_b, v_w_ff1, v_w_ff2, v_ln2_g, v_ln2_b):
    t = x.shape[1]
    x2 = x.reshape(t, D_MODEL)
    target = loss_target.reshape(t, D_MODEL)

    w_in_shard = w_in[0].T.astype(BF16)
    in_direct, in_pass = _direct_gather_plans([w_in_shard.shape[0]])
    in_bufs, in_started, in_send, in_recv = _split_call(
        "gather_w_in_start", [w_in_shard, _landing(N_CHIPS * w_in_shard.shape[0], D_MODEL, BF16)], start=in_direct)
    inv_freq = ROPE_THETA ** (-jnp.arange(0, HEAD_DIM, 2, dtype=F32) / HEAD_DIM)
    cos, sin, later = _rope_tables_and_casts(
        positions, jnp.tile(inv_freq, LANES // (HEAD_DIM // 2)).reshape(1, LANES), [w_out[0], w_ff1[0], w_ff2[0]], dep=in_started)
    later_rows = [s.shape[0] for s in later]
    direct_plan, pass_plan, diagonal_plan = _gather_plans(later_rows)
    bufs, started, direct_send, direct_recv = _split_call(
        "gather_start", later + [_landing(N_CHIPS * r, D_MODEL, BF16) for r in later_rows], start=direct_plan, after=cos)
    in_bufs, in_passing, in_pass_send, in_pass_recv = _split_call(
        "gather_w_in_pass", in_bufs, wait=(in_direct, in_send, in_recv), start=in_pass, after=started)
    in_bufs, _ = _split_call("gather_w_in_end", in_bufs, wait=(in_pass, in_pass_send, in_pass_recv), after=in_passing)
    w_in_t = in_bufs[1]

    u, vg, q, k, va = _in_proj(x2, w_in_t, cos, sin)
    bias_full = jnp.repeat(b_spatial[0].T, HEAD_DIM, axis=1)
    sink_vec = sinks.reshape(N_HEADS)
    bufs, passing, pass_send, pass_recv = _split_call(
        "gather_pass", bufs, wait=(direct_plan, direct_send, direct_recv), start=pass_plan, after=u)
    cat = _mixer_fwd(u, vg, q, k, va, v_ln_g, v_ln_b, w_spatial[0], bias_full, sink_vec, dep=passing)
    bufs, passing, diag_send, diag_recv = _split_call(
        "gather_pass_diagonal", bufs, wait=(pass_plan, pass_send, pass_recv), start=diagonal_plan, after=cat)
    bufs, _ = _split_call("gather_end", bufs, wait=(diagonal_plan, diag_send, diag_recv), after=passing)
    w_out_all = bufs[3]
    w1_all = bufs[4].reshape(N_FF_BLOCKS, D_MODEL, D_MODEL)
    w2_all = bufs[5].reshape(N_FF_BLOCKS, D_MODEL, D_MODEL)
    xhat1, rstd1, x1b, r, dz2, dz2b, d_ln2_g, d_ln2_b, sq_err = _ffn_fwd_loss(
        cat, x2, w_out_all, ln1_g, ln1_b, w1_all, w2_all, ln2_g, ln2_b, target)

    pos = jnp.stack([lax.axis_index("c"), 2 * lax.axis_index("x") + lax.axis_index("y")]).astype(jnp.int32)
    half_landing = lambda g: _landing(g.shape[0] // 2, D_MODEL, F32)
    ff_swap_plan = _swap_plan([D_FF // N_CHIPS])
    ff_exchange_plan = _exchange_plan([D_FF // N_CHIPS // 2])
    exchange_landing = lambda p: _landing(3 * p.shape[0] // N_CHIPS, D_MODEL, BF16)
    g_ff1_local, dz1, dz1b, dcat, d_ln1_g, d_ln1_b = _ffn_bwd_ln1(dz2, r, x1b, xhat1, rstd1, ln1_g, w1_all, w2_all, w_out_all)
    ff1_bufs, swapping1, swap1_send, swap1_recv = _split_call("ff1_swap_start", [g_ff1_local, half_landing(g_ff1_local)], start=ff_swap_plan)
    g_out_local = _grad_w_out(cat, dz1b, dep=swapping1)
    ff1_bufs, _ = _split_call("ff1_swap_wait", ff1_bufs, wait=(ff_swap_plan, swap1_send, swap1_recv), after=g_out_local)
    ff1_sum, ff1_own = _pair_sum("grad_pair_sum_w_ff1", ff1_bufs[0], ff1_bufs[1], pos)
    ff1_ex, exchanging1, ex1_send, ex1_recv = _split_call(
        "ff1_exchange_start", [ff1_sum, exchange_landing(ff1_sum)], start=ff_exchange_plan)
    dh_main, dkv, g_ff2_local, d_v_ln_g, d_v_ln_b, d_w_spatial, d_b_spatial_t, d_sinks = _mixer_bwd(
        u, vg, q, k, va, dcat, cos, sin, v_ln_g, v_ln_b, w_spatial[0], bias_full, sink_vec, r, dz2b, dep=exchanging1)
    ff2_bufs, swapping2, swap2_send, swap2_recv = _split_call("ff2_swap_start", [g_ff2_local, half_landing(g_ff2_local)], start=ff_swap_plan)
    g_in_local, small_g = _grad_w_in_t_and_small_all_reduce(dh_main, dkv, x2, _pack_small(
        [d_v_ln_g, d_v_ln_b, d_w_spatial, d_b_spatial_t[:, :N_HEADS].T, d_sinks[0, :N_HEADS], d_ln1_g, d_ln1_b, d_ln2_g, d_ln2_b, sq_err]),
        dep=swapping2)
    sq_row = sum(rows for _, _, rows in _SMALL[:N_SMALL_PARAMS])
    loss = 0.5 * jnp.sum(small_g[sq_row : sq_row + _SMALL[N_SMALL_PARAMS][2]]) / D_MODEL
    ff2_bufs, _ = _split_call("ff2_swap_wait", ff2_bufs, wait=(ff_swap_plan, swap2_send, swap2_recv), after=g_in_local)
    ff2_sum, ff2_own = _pair_sum("grad_pair_sum_w_ff2", ff2_bufs[0], ff2_bufs[1], pos)
    ff2_ex, exchanging2, ex2_send, ex2_recv = _split_call(
        "ff2_exchange_start", [ff2_sum, exchange_landing(ff2_sum)], start=ff_exchange_plan)

    small = [g_in_local, g_out_local]
    small_swap_plan = _swap_plan([g.shape[0] // N_CHIPS for g in small])
    swap_bufs, small_swapping, ss_send, ss_recv = _split_call(
        "small_swap_start", small + [half_landing(g) for g in small], start=small_swap_plan, after=exchanging2)
    grad_x_flat = _grad_x(dh_main, dkv, dz1, w_in_t, dep=small_swapping)
    grad_x = grad_x_flat.reshape(1, t, D_MODEL)
    swap_bufs, _ = _split_call("small_swap_wait", swap_bufs, wait=(small_swap_plan, ss_send, ss_recv), after=grad_x_flat)
    pair_sums = [_pair_sum("grad_pair_sum_" + nm, g, th, pos) for nm, g, th in zip(["w_in", "w_out"], swap_bufs[:2], swap_bufs[2:])]
    small_plan = _exchange_plan([p.shape[0] // N_CHIPS for p, _ in pair_sums])
    small_bufs, small_exchanging, sm_send, sm_recv = _split_call(
        "small_exchange_start", [p for p, _ in pair_sums] + [exchange_landing(p) for p, _ in pair_sums], start=small_plan)

    ff1_ex, _ = _split_call("ff1_exchange_wait", ff1_ex, wait=(ff_exchange_plan, ex1_send, ex1_recv), after=small_exchanging)
    ff_pair_plan = _sibling_plan([D_FF // N_CHIPS])
    half_ff1 = _chip_sum("grad_chip_sum_w_ff1", ff1_own, ff1_ex[1], pos)
    (half_ff1, *ff2_ex), _, g1_send, g1_recv = _split_call(
        "ff2_exchange_wait_ff1_pair_start", [half_ff1] + ff2_ex, wait=(_shifted(ff_exchange_plan, 1), ex2_send, ex2_recv), start=ff_pair_plan)
    half_ff2 = _chip_sum("grad_chip_sum_w_ff2", ff2_own, ff2_ex[1], pos)
    (half_ff2, g_w_ff1), _, g2_send, g2_recv = _split_call(
        "ff1_pair_wait_ff2_pair_start", [half_ff2, half_ff1], wait=(_shifted(ff_pair_plan, 1), g1_send, g1_recv), start=ff_pair_plan)
    g_w_ff1, d_w_ff1, nm_w_ff1, nv_w_ff1 = _adamw("adamw_w_ff1", w_ff1[0], g_w_ff1, m_w_ff1[0], v_w_ff1[0])
    small_bufs, _ = _split_call("small_exchange_wait", small_bufs, wait=(small_plan, sm_send, sm_recv), after=nv_w_ff1)
    shards = [_chip_sum("grad_chip_sum_" + nm, own, ld, pos) for nm, (_, own), ld in zip(["w_in", "w_out"], pair_sums, small_bufs[2:])]
    small_pair_plan = _sibling_plan([s.shape[0] for s in shards])
    (*shards, g_w_ff2), _, g3_send, g3_recv = _split_call(
        "ff2_pair_wait_small_pair_start", shards + [half_ff2], wait=(_shifted(ff_pair_plan, 2), g2_send, g2_recv), start=small_pair_plan)
    g_w_ff2, d_w_ff2, nm_w_ff2, nv_w_ff2 = _adamw("adamw_w_ff2", w_ff2[0], g_w_ff2, m_w_ff2[0], v_w_ff2[0])
    (g_w_in_t, g_w_out), _ = _split_call("small_pair_wait", shards, wait=(small_pair_plan, g3_send, g3_recv), after=nv_w_ff2)
    g_w_in, d_w_in, nm_w_in, nv_w_in = (a.T for a in _adamw("adamw_w_in", w_in[0].T, g_w_in_t, m_w_in[0].T, v_w_in[0].T))
    g_w_out, d_w_out, nm_w_out, nv_w_out = _adamw("adamw_w_out", w_out[0], g_w_out, m_w_out[0], v_w_out[0])
    small_grads, small_d, small_nm, small_nv = _adamw_small(
        small_g,
        [v_ln_g, v_ln_b, w_spatial, b_spatial, sinks, ln1_g, ln1_b, ln2_g, ln2_b],
        [m_v_ln_g, m_v_ln_b, m_w_spatial, m_b_spatial, m_sinks, m_ln1_g, m_ln1_b, m_ln2_g, m_ln2_b],
        [v_v_ln_g, v_v_ln_b, v_w_spatial, v_b_spatial, v_sinks, v_ln1_g, v_ln1_b, v_ln2_g, v_ln2_b])

    def with_big(small, w_in_v, w_out_v, w_ff1_v, w_ff2_v):
        g_vg, g_vb, g_ws, g_bs, g_sk, g_1g, g_1b, g_2g, g_2b = small
        return [w_in_v[None], g_vg, g_vb, g_ws, g_bs, g_sk, w_out_v[None], g_1g, g_1b, w_ff1_v[None], w_ff2_v[None], g_2g, g_2b]

    return (
        loss,
        grad_x,
        *with_big(small_grads, g_w_in, g_w_out, g_w_ff1, g_w_ff2),
        *with_big(small_d, d_w_in, d_w_out, d_w_ff1, d_w_ff2),
        *with_big(small_nm, nm_w_in, nm_w_out, nm_w_ff1, nm_w_ff2),
        *with_big(small_nv, nv_w_in, nv_w_out, nv_w_ff1, nv_w_ff2),
    )
```

```python
import math

import jax
import jax.numpy as jnp
from jax import lax
from jax.experimental import pallas as pl
from jax.experimental.pallas import tpu as pltpu

F32 = jnp.float32
BF16 = jnp.bfloat16

D_MODEL = 1024
HEAD_DIM = 64
D_GMLP = 512
D_ATTN = 512
D_KV = 128
D_IN = 2 * D_GMLP + D_ATTN + 2 * D_KV
D_MAIN = 2 * D_GMLP + D_ATTN
N_HEADS = 8
CHUNK = 128
CHUNKS_PER_STEP = 4
ROPE_THETA = 10000.0
D_FF = 4 * D_MODEL
N_FF_BLOCKS = 4
LN_EPS = 1e-5
ALPHA = (2.0 * 1) ** 0.25
NEG_INF = -1e30
SCALE = 1.0 / math.sqrt(HEAD_DIM)

ADAM_LR = 0.001
ADAM_B1 = 0.9
ADAM_B2 = 0.999
ADAM_EPS = 1e-08
ADAM_WD = 0.01
ADAM_STEP = 10

N_CHIPS = 4
LANES = 128
V7X_VMEM_BYTES = 64 * 1024 * 1024
VMEM_LIMIT = V7X_VMEM_BYTES - 8 * 1024 * 1024
TM = 1024
TM_FFN = 256
TM_FFN_FWD = 512
FFN_PART = 256
TK = 1024
SMALL_ROWS = 1152
MESH = pl.DeviceIdType.MESH

NT = (((1,), (1,)), ((), ()))
TN = (((0,), (0,)), ((), ()))


def _dot(a, b, dims=None):
    if dims is None:
        return jnp.dot(a, b, preferred_element_type=F32)
    return lax.dot_general(a, b, dims, preferred_element_type=F32)


def _params(semantics=None):
    return pltpu.CompilerParams(dimension_semantics=semantics, vmem_limit_bytes=VMEM_LIMIT)


def _const_spec(shape, single_buffer=False):
    zeros = (0,) * len(shape)
    if single_buffer:
        return pl.BlockSpec(shape, lambda *_: zeros, pipeline_mode=pl.Buffered(1))
    return pl.BlockSpec(shape, lambda *_: zeros)


def _row_spec(rows, cols):
    return pl.BlockSpec((rows, cols), lambda i: (i, 0))


def _after(dep, body, in_specs, operands):
    if dep is None:
        return body, list(in_specs), list(operands)
    return (lambda dep_ref, *refs: body(*refs)), [pl.BlockSpec(memory_space=pl.ANY)] + list(in_specs), [dep] + list(operands)


def _gelu(x):
    k = math.sqrt(2.0 / math.pi)
    return 0.5 * x * (1.0 + jnp.tanh(k * (x + 0.044715 * (x * x * x))))


def _gelu_and_grad(x):
    k = math.sqrt(2.0 / math.pi)
    x2 = x * x
    t = jnp.tanh(k * (x + 0.044715 * (x2 * x)))
    g = 0.5 * x * (1.0 + t)
    dg = 0.5 * (1.0 + t) + 0.5 * x * (1.0 - t * t) * (k * (1.0 + 3.0 * 0.044715 * x2))
    return g, dg


def _layer_norm_stats(z):
    mu = jnp.mean(z, axis=-1, keepdims=True)
    zc = z - mu
    var = jnp.mean(zc * zc, axis=-1, keepdims=True)
    rstd = lax.rsqrt(var + LN_EPS)
    return zc * rstd, rstd


def _layer_norm_bwd(dxhat, xhat, rstd):
    m1 = jnp.mean(dxhat, axis=-1, keepdims=True)
    m2 = jnp.mean(dxhat * xhat, axis=-1, keepdims=True)
    return rstd * (dxhat - m1 - xhat * m2)


def _rotate_half(t):
    n = t.shape[1]
    lane = lax.broadcasted_iota(jnp.int32, t.shape, 1)
    first = (lane & (HEAD_DIM // 2)) == 0
    return jnp.where(first, -pltpu.roll(t, n - HEAD_DIM // 2, 1), pltpu.roll(t, HEAD_DIM // 2, 1))


def _rope(t, cos, sin):
    return t * cos + _rotate_half(t) * sin


def _rope_transposed(g, cos, sin):
    return g * cos - _rotate_half(g * sin)


def _lane_tile(a, reps):
    return jnp.tile(a, (1, reps)) if reps > 1 else a


def _in_proj(x, w_in_t, cos, sin, dep=None):
    t = x.shape[0]

    def body(x_ref, w_ref, cos_ref, sin_ref, u_ref, vg_ref, q_ref, k_ref, va_ref):
        xb = x_ref[...].astype(BF16)
        u_ref[...] = _dot(xb, w_ref[0:D_GMLP, :], NT)
        vg_ref[...] = _dot(xb, w_ref[D_GMLP : 2 * D_GMLP, :], NT)
        q = _dot(xb, w_ref[2 * D_GMLP : D_MAIN, :], NT)
        k = _dot(xb, w_ref[D_MAIN : D_MAIN + D_KV, :], NT)
        va_ref[...] = _dot(xb, w_ref[D_MAIN + D_KV : D_IN, :], NT).astype(BF16)
        c, s = cos_ref[...], sin_ref[...]
        q_ref[...] = _rope(q, _lane_tile(c, D_ATTN // LANES), _lane_tile(s, D_ATTN // LANES)).astype(BF16)
        k_ref[...] = _rope(k, c, s).astype(BF16)

    body, in_specs, operands = _after(
        dep, body, [_row_spec(TM, D_MODEL), _const_spec((D_IN, D_MODEL)), _row_spec(TM, LANES), _row_spec(TM, LANES)], [x, w_in_t, cos, sin])
    return pl.pallas_call(
        body,
        name="in_proj",
        grid=(t // TM,),
        in_specs=in_specs,
        out_specs=[_row_spec(TM, D_GMLP), _row_spec(TM, D_GMLP), _row_spec(TM, D_ATTN), _row_spec(TM, D_KV), _row_spec(TM, D_KV)],
        out_shape=[
            jax.ShapeDtypeStruct((t, D_GMLP), F32),
            jax.ShapeDtypeStruct((t, D_GMLP), F32),
            jax.ShapeDtypeStruct((t, D_ATTN), BF16),
            jax.ShapeDtypeStruct((t, D_KV), BF16),
            jax.ShapeDtypeStruct((t, D_KV), BF16),
        ],
        compiler_params=_params(("parallel",)),
    )(*operands)


def _step_rows(i):
    return (i, 0)


def _chunk_before_step(i):
    return (jnp.maximum(CHUNKS_PER_STEP * i - 1, 0), 0)


def _chunk_specs():
    step = CHUNKS_PER_STEP * CHUNK
    return [
        pl.BlockSpec((step, D_GMLP), _step_rows),
        pl.BlockSpec((step, D_GMLP), _step_rows),
        pl.BlockSpec((step, D_ATTN), _step_rows),
        pl.BlockSpec((step, D_KV), _step_rows),
        pl.BlockSpec((CHUNK, D_KV), _chunk_before_step),
        pl.BlockSpec((step, D_KV), _step_rows),
        pl.BlockSpec((CHUNK, D_KV), _chunk_before_step),
    ]


def _half_lane_masks(rows):
    lane = lax.broadcasted_iota(jnp.int32, (rows, LANES), 1)
    return lane < HEAD_DIM


def _kv_variants(kv2):
    left = _half_lane_masks(kv2.shape[0])
    f = kv2.astype(F32)
    swapped = pltpu.roll(f, HEAD_DIM, 1)
    zero = jnp.zeros_like(f)
    g0 = (jnp.where(left, f, zero).astype(BF16), jnp.where(left, zero, swapped).astype(BF16))
    g1 = (jnp.where(left, swapped, zero).astype(BF16), jnp.where(left, zero, f).astype(BF16))
    return (g0, g1)


def _band_mask(i, heads=1):
    row = lax.broadcasted_iota(jnp.int32, (heads * CHUNK, 2 * CHUNK), 0) & (CHUNK - 1)
    col = lax.broadcasted_iota(jnp.int32, (heads * CHUNK, 2 * CHUNK), 1)
    no_prev = jnp.where(i > 0, 0, 4 * CHUNK)
    in_prev = jnp.logical_and(col < CHUNK, (col - row) > no_prev)
    in_cur = jnp.logical_and(col >= CHUNK, (col - CHUNK) <= row)
    return jnp.logical_or(in_prev, in_cur)


def _causal_mask():
    row = lax.broadcasted_iota(jnp.int32, (CHUNK, CHUNK), 0)
    col = lax.broadcasted_iota(jnp.int32, (CHUNK, CHUNK), 1)
    return col <= row


def _store_spatial_weights(w_ref, wcat_ref, wcat_t_ref=None):
    causal = _causal_mask()
    for p in range(D_GMLP // LANES):
        wl = jnp.where(causal, w_ref[2 * p], 0.0)
        wr = jnp.where(causal, w_ref[2 * p + 1], 0.0)
        wcat_ref[p] = jnp.concatenate([wl, wr], axis=1).astype(BF16)
        if wcat_t_ref is not None:
            wcat_t_ref[p] = jnp.concatenate([wl.T, wr.T], axis=1).astype(BF16)


def _pair_stack(xp, left):
    return jnp.concatenate([jnp.where(left, xp, 0.0), jnp.where(left, 0.0, xp)], axis=0).astype(BF16)


def _mixer_fwd(u, vg, q, k, va, v_ln_g, v_ln_b, w_spatial, bias_full, sinks, dep=None):
    t = u.shape[0]

    def body(u_ref, vg_ref, q_ref, kc_ref, kp_ref, vc_ref, vp_ref, g_ref, b_ref, w_ref, bias_ref, sink_ref, cat_ref, wcat):
        i = pl.program_id(0)
        left = _half_lane_masks(CHUNK)

        @pl.when(i == 0)
        def _():
            _store_spatial_weights(w_ref, wcat)

        heads = range(N_HEADS)
        pair_cols = [slice(p * LANES, (p + 1) * LANES) for p in range(D_GMLP // LANES)]
        sinks_h = [sink_ref[h] for h in heads]
        for c in range(CHUNKS_PER_STEP):
            rows = slice(c * CHUNK, (c + 1) * CHUNK)
            before = slice((c - 1) * CHUNK, c * CHUNK)
            k_prev = kp_ref[...] if c == 0 else kc_ref[before, :]
            v_prev = vp_ref[...] if c == 0 else vc_ref[before, :]
            k_var = _kv_variants(jnp.concatenate([k_prev, kc_ref[rows, :]], axis=0))
            v_var = _kv_variants(jnp.concatenate([v_prev, vc_ref[rows, :]], axis=0))
            scores = [_dot(q_ref[rows, pair_cols[h // 2]], k_var[h // 4][h % 2], NT) for h in heads]

            ug = _gelu(u_ref[rows, :])
            xhat, _ = _layer_norm_stats(_gelu(vg_ref[rows, :]))
            vgl = xhat * g_ref[...] + b_ref[...]
            mixed = [_dot(wcat[p], _pair_stack(vgl[:, cols], left)) for p, cols in enumerate(pair_cols)]

            valid = _band_mask(CHUNKS_PER_STEP * i + c)
            masked = [jnp.where(valid, scores[h] * SCALE, NEG_INF) for h in heads]
            maxes = [jnp.maximum(jnp.max(masked[h], axis=1, keepdims=True), sinks_h[h]) for h in heads]
            exps = [jnp.exp(masked[h] - maxes[h]) for h in heads]
            invs = [1.0 / (jnp.sum(exps[h], axis=1, keepdims=True) + jnp.exp(sinks_h[h] - maxes[h])) for h in heads]
            probs = [(exps[h] * invs[h]).astype(BF16) for h in heads]
            for p, cols in enumerate(pair_cols):
                cat_ref[rows, cols] = (ug[:, cols] * (mixed[p] + bias_ref[:, cols])).astype(BF16)
            for p in range(D_ATTN // LANES):
                out = _dot(probs[2 * p], v_var[p // 2][0]) + _dot(probs[2 * p + 1], v_var[p // 2][1])
                cat_ref[rows, D_GMLP + p * LANES : D_GMLP + (p + 1) * LANES] = out.astype(BF16)

    in_specs = _chunk_specs() + [
        _const_spec((1, D_GMLP)),
        _const_spec((1, D_GMLP)),
        _const_spec((N_HEADS, CHUNK, CHUNK)),
        _const_spec((CHUNK, D_GMLP)),
        pl.BlockSpec(memory_space=pltpu.SMEM),
    ]
    body, in_specs, operands = _after(dep, body, in_specs, [u, vg, q, k, k, va, va, v_ln_g, v_ln_b, w_spatial, bias_full, sinks])
    return pl.pallas_call(
        body,
        name="mixer_fwd",
        grid=(t // (CHUNKS_PER_STEP * CHUNK),),
        in_specs=in_specs,
        out_specs=pl.BlockSpec((CHUNKS_PER_STEP * CHUNK, D_MODEL), lambda i: (i, 0)),
        out_shape=jax.ShapeDtypeStruct((t, D_MODEL), BF16),
        scratch_shapes=[pltpu.VMEM((D_GMLP // LANES, CHUNK, 2 * CHUNK), BF16)],
        compiler_params=_params(("arbitrary",)),
    )(*operands)


def _ffn_fwd_loss(cat, x, w_out, ln1_g, ln1_b, w1, w2, ln2_g, ln2_b, target):
    t = x.shape[0]

    def body(cat_ref, x_ref, wo_ref, g1_ref, b1_ref, w1_ref, w2_ref, g2_ref, b2_ref, tgt_ref,
             xh_ref, rstd_ref, x1b_ref, r_ref, dz2_ref, dz2b_ref, dg2_ref, db2_ref, sq_ref):
        @pl.when(pl.program_id(0) == 0)
        def _():
            dg2_ref[...] = jnp.zeros_like(dg2_ref)
            db2_ref[...] = jnp.zeros_like(db2_ref)
            sq_ref[...] = jnp.zeros_like(sq_ref)

        parts = [slice(p * FFN_PART, (p + 1) * FFN_PART) for p in range(TM_FFN_FWD // FFN_PART)]

        def norm1(rows, z):
            xhat1, rstd1 = _layer_norm_stats(z)
            xh_ref[rows, :] = xhat1
            rstd_ref[rows, :] = rstd1
            x1 = xhat1 * g1_ref[...] + b1_ref[...]
            x1b = x1.astype(BF16)
            x1b_ref[rows, :] = x1b
            return x1, x1b

        def feed_forward(rows, x1b, pre):
            ff = None
            for j in range(N_FF_BLOCKS):
                r = jnp.maximum(pre, 0.0)
                r_ref[rows, j * D_MODEL : (j + 1) * D_MODEL] = r.astype(BF16)
                part = _dot((r * r).astype(BF16), w2_ref[j])
                ff = part if ff is None else ff + part
                if j + 1 < N_FF_BLOCKS:
                    pre = _dot(x1b, w1_ref[j + 1])
            return ff

        def norm2_and_loss(rows, x1, ff):
            xhat2, rstd2 = _layer_norm_stats(ALPHA * x1 + ff)
            err = xhat2 * g2_ref[...] + b2_ref[...] - tgt_ref[rows, :]
            sq_ref[...] += jnp.sum(err * err, axis=0, keepdims=True)
            dy = err * (1.0 / D_MODEL)
            dg2_ref[...] += jnp.sum(dy * xhat2, axis=0, keepdims=True)
            db2_ref[...] += jnp.sum(dy, axis=0, keepdims=True)
            dz2 = _layer_norm_bwd(dy * g2_ref[...], xhat2, rstd2)
            dz2_ref[rows, :] = dz2
            dz2b_ref[rows, :] = dz2.astype(BF16)

        projected = _dot(cat_ref[parts[0], :], wo_ref[...])
        last = None
        for i, rows in enumerate(parts):
            z = ALPHA * x_ref[rows, :] + projected
            if i + 1 < len(parts):
                projected = _dot(cat_ref[parts[i + 1], :], wo_ref[...])
            x1, x1b = norm1(rows, z)
            pre = _dot(x1b, w1_ref[0])
            if last is not None:
                norm2_and_loss(*last)
            last = (rows, x1, feed_forward(rows, x1b, pre))
        norm2_and_loss(*last)

    vec = _const_spec((1, D_MODEL))
    tile = _row_spec(TM_FFN_FWD, D_MODEL)
    wspec = _const_spec((N_FF_BLOCKS, D_MODEL, D_MODEL), single_buffer=True)
    return pl.pallas_call(
        body,
        name="ffn_fwd_loss",
        grid=(t // TM_FFN_FWD,),
        in_specs=[tile, tile, _const_spec((D_MODEL, D_MODEL), single_buffer=True), vec, vec, wspec, wspec, vec, vec, tile],
        out_specs=[tile, _row_spec(TM_FFN_FWD, 1), tile, _row_spec(TM_FFN_FWD, D_FF), tile, tile, vec, vec, vec],
        out_shape=[
            jax.ShapeDtypeStruct((t, D_MODEL), F32),
            jax.ShapeDtypeStruct((t, 1), F32),
            jax.ShapeDtypeStruct((t, D_MODEL), BF16),
            jax.ShapeDtypeStruct((t, D_FF), BF16),
            jax.ShapeDtypeStruct((t, D_MODEL), F32),
            jax.ShapeDtypeStruct((t, D_MODEL), BF16),
            jax.ShapeDtypeStruct((1, D_MODEL), F32),
            jax.ShapeDtypeStruct((1, D_MODEL), F32),
            jax.ShapeDtypeStruct((1, D_MODEL), F32),
        ],
        compiler_params=_params(("arbitrary",)),
    )(cat, x, w_out, ln1_g, ln1_b, w1, w2, ln2_g, ln2_b, target)


def _ffn_bwd_ln1(dz2, r, x1b, xhat1, rstd1, ln1_g, w1, w2, w_out, dep=None):
    t = dz2.shape[0]

    def body(dz2_ref, r_ref, x1b_ref, xh_ref, rstd_ref, g1_ref, w1_ref, w2_ref, wo_ref, gw1_ref, dz1_ref, dz1b_ref, dcat_ref, dg1_ref, db1_ref):
        @pl.when(pl.program_id(0) == 0)
        def _():
            dg1_ref[...] = jnp.zeros_like(dg1_ref)
            db1_ref[...] = jnp.zeros_like(db1_ref)
            gw1_ref[...] = jnp.zeros_like(gw1_ref)

        dz2 = dz2_ref[...]
        dz2b = dz2.astype(BF16)
        x1_t = x1b_ref[...].astype(F32).T.astype(BF16)
        dx1 = ALPHA * dz2
        for j in range(N_FF_BLOCKS):
            cols = slice(j * D_MODEL, (j + 1) * D_MODEL)
            dpre = (_dot(dz2b, w2_ref[j], NT) * (2.0 * r_ref[:, cols].astype(F32))).astype(BF16)
            gw1_ref[cols, :] += _dot(x1_t, dpre)
            dx1 = dx1 + _dot(dpre, w1_ref[j], NT)
        xhat1 = xh_ref[...]
        dg1_ref[...] += jnp.sum(dx1 * xhat1, axis=0, keepdims=True)
        db1_ref[...] += jnp.sum(dx1, axis=0, keepdims=True)
        dz1 = _layer_norm_bwd(dx1 * g1_ref[...], xhat1, rstd_ref[...])
        dz1_ref[...] = dz1
        dz1b = dz1.astype(BF16)
        dz1b_ref[...] = dz1b
        dcat_ref[...] = _dot(dz1b, wo_ref[...], NT).astype(BF16)

    vec = _const_spec((1, D_MODEL))
    tile = _row_spec(TM_FFN, D_MODEL)
    wspec = _const_spec((N_FF_BLOCKS, D_MODEL, D_MODEL), single_buffer=True)
    body, in_specs, operands = _after(
        dep, body,
        [tile, _row_spec(TM_FFN, D_FF), tile, tile, _row_spec(TM_FFN, 1), vec, wspec, wspec, _const_spec((D_MODEL, D_MODEL), single_buffer=True)],
        [dz2, r, x1b, xhat1, rstd1, ln1_g, w1, w2, w_out])
    return pl.pallas_call(
        body,
        name="ffn_bwd_ln1",
        grid=(t // TM_FFN,),
        in_specs=in_specs,
        out_specs=[_const_spec((D_FF, D_MODEL), single_buffer=True), tile, tile, tile, vec, vec],
        out_shape=[
            jax.ShapeDtypeStruct((D_FF, D_MODEL), F32),
            jax.ShapeDtypeStruct((t, D_MODEL), F32),
            jax.ShapeDtypeStruct((t, D_MODEL), BF16),
            jax.ShapeDtypeStruct((t, D_MODEL), BF16),
            jax.ShapeDtypeStruct((1, D_MODEL), F32),
            jax.ShapeDtypeStruct((1, D_MODEL), F32),
        ],
        compiler_params=_params(("arbitrary",)),
    )(*operands)


def _mixer_bwd(u, vg, q, k, va, dcat, cos, sin, v_ln_g, v_ln_b, w_spatial, bias_full, sinks, r, dz2b, dep=None):
    t = u.shape[0]
    n_chunks = t // CHUNK
    assert CHUNKS_PER_STEP == N_FF_BLOCKS

    def body(u_ref, vg_ref, q_ref, kc_ref, kp_ref, vc_ref, vp_ref, dcat_ref, cosc_ref, sinc_ref, cosp_ref, sinp_ref,
             g_ref, b_ref, w_ref, bias_ref, sink_ref, r_ref, dz2b_ref,
             dmain_ref, dkv_ref, gw2_out, dg_ref, db_ref, dw_ref, dbs_ref, dsink_ref, dmix_acc, wcat, wcat_t, gw2_ref, gw2_sems):
        i = pl.program_id(0)
        left = _half_lane_masks(CHUNK)
        lane = lax.broadcasted_iota(jnp.int32, (CHUNK, LANES), 1)
        n_pairs = D_GMLP // LANES

        @pl.when(i == 0)
        def _():
            dg_ref[...] = jnp.zeros_like(dg_ref)
            db_ref[...] = jnp.zeros_like(db_ref)
            dw_ref[...] = jnp.zeros_like(dw_ref)
            dsink_ref[...] = jnp.zeros_like(dsink_ref)
            dmix_acc[...] = jnp.zeros_like(dmix_acc)
            gw2_ref[...] = jnp.zeros_like(gw2_ref)
            _store_spatial_weights(w_ref, wcat, wcat_t)

        n_qpairs = D_ATTN // LANES
        heads = range(N_HEADS)
        pair_cols = [slice(p * LANES, (p + 1) * LANES) for p in range(n_pairs)]
        sinks_h = [sink_ref[h] for h in heads]
        gain = g_ref[...]
        causal = _causal_mask()
        lane_row = lax.broadcasted_iota(jnp.int32, (1, LANES), 1)
        heads_per_group = N_HEADS // 2

        def group_grad_t(lhs_t, rhs_heads):
            parts = []
            for g in range(2):
                group = range(g * heads_per_group, (g + 1) * heads_per_group)
                lhs = jnp.concatenate([lhs_t[h * HEAD_DIM : (h + 1) * HEAD_DIM] for h in group], axis=1)
                parts.append(_dot(lhs, jnp.concatenate([rhs_heads[h] for h in group], axis=0)))
            return jnp.concatenate(parts, axis=0)

        blocks_out = []
        for c in range(CHUNKS_PER_STEP):
            chunk = CHUNKS_PER_STEP * i + c
            rows = slice(c * CHUNK, (c + 1) * CHUNK)
            before = slice((c - 1) * CHUNK, c * CHUNK)

            k_prev = kp_ref[...] if c == 0 else kc_ref[before, :]
            v_prev = vp_ref[...] if c == 0 else vc_ref[before, :]
            k_var = _kv_variants(jnp.concatenate([k_prev, kc_ref[rows, :]], axis=0))
            v_var = _kv_variants(jnp.concatenate([v_prev, vc_ref[rows, :]], axis=0))
            q_pairs = [q_ref[rows, cols] for cols in pair_cols]
            do_all = dcat_ref[rows, D_GMLP:D_MODEL]
            do_pairs = [do_all[:, cols] for cols in pair_cols]
            scores = [_dot(q_pairs[h // 2], k_var[h // 4][h % 2], NT) for h in heads]
            dprobs = [_dot(do_pairs[h // 2], v_var[h // 4][h % 2], NT) for h in heads]
            q_t = q_ref[rows, :].astype(F32).T.astype(BF16)
            do_t = do_all.astype(F32).T.astype(BF16)

            ff_cols = slice(c * D_MODEL, (c + 1) * D_MODEL)
            relu_block = r_ref[:, ff_cols]
            gw2_ref[ff_cols, :] += _dot(relu_block * relu_block, dz2b_ref[...], TN)
            block_out = pltpu.make_async_copy(_rows(gw2_ref, c * D_MODEL, D_MODEL), _rows(gw2_out, c * D_MODEL, D_MODEL), gw2_sems.at[c])
            pl.when(i == n_chunks // CHUNKS_PER_STEP - 1)(block_out.start)
            blocks_out.append(block_out)

            ug, dug_du = _gelu_and_grad(u_ref[rows, :])
            gv, dgv_dv = _gelu_and_grad(vg_ref[rows, :])
            xhat, rstd = _layer_norm_stats(gv)
            vgl = xhat * gain + b_ref[...]
            mixed = [_dot(wcat[p], _pair_stack(vgl[:, cols], left)) for p, cols in enumerate(pair_cols)]

            valid = _band_mask(chunk)
            masked = [jnp.where(valid, scores[h] * SCALE, NEG_INF) for h in heads]
            maxes = [jnp.maximum(jnp.max(masked[h], axis=1, keepdims=True), sinks_h[h]) for h in heads]
            exps = [jnp.exp(masked[h] - maxes[h]) for h in heads]
            exp_sinks = [jnp.exp(sinks_h[h] - maxes[h]) for h in heads]
            invs = [1.0 / (jnp.sum(exps[h], axis=1, keepdims=True) + exp_sinks[h]) for h in heads]
            probs = [exps[h] * invs[h] for h in heads]
            dsums = [jnp.sum(probs[h] * dprobs[h], axis=1, keepdims=True) for h in heads]
            ds_b = [(probs[h] * (dprobs[h] - dsums[h]) * SCALE).astype(BF16) for h in heads]
            probs_b = [probs[h].astype(BF16) for h in heads]

            dm_stacks = []
            for p, cols in enumerate(pair_cols):
                da = dcat_ref[rows, cols].astype(F32)
                dmain_ref[rows, cols] = (da * (mixed[p] + bias_ref[:, cols]) * dug_du[:, cols]).astype(BF16)
                dmixed = da * ug[:, cols]
                dmix_acc[:, cols] += dmixed
                dm_stacks.append(_pair_stack(dmixed, left))

            dq_all = jnp.concatenate(
                [_dot(ds_b[2 * p], k_var[p // 2][0]) + _dot(ds_b[2 * p + 1], k_var[p // 2][1]) for p in range(n_qpairs)], axis=1)
            dk2_t = group_grad_t(q_t, ds_b)
            dv2_t = group_grad_t(do_t, probs_b)

            for p, cols in enumerate(pair_cols):
                dw_pair = _dot(dm_stacks[p], vgl[:, cols].astype(BF16), NT)
                dw_ref[2 * p] += jnp.where(causal, dw_pair[:CHUNK], 0.0)
                dw_ref[2 * p + 1] += jnp.where(causal, dw_pair[CHUNK:], 0.0)
            dvgl = jnp.concatenate([_dot(wcat_t[p], dm_stacks[p]) for p in range(n_pairs)], axis=1)

            dsink_row = jnp.zeros((1, LANES), F32)
            for h in heads:
                d_sink = -jnp.sum(exp_sinks[h] * invs[h] * dsums[h], axis=0, keepdims=True)
                dsink_row = dsink_row + jnp.where(lane_row == h, d_sink, 0.0)
            dsink_ref[0:1, :] += dsink_row
            cos_c, sin_c = cosc_ref[rows, :], sinc_ref[rows, :]
            cos_p = cosp_ref[...] if c == 0 else cosc_ref[before, :]
            sin_p = sinp_ref[...] if c == 0 else sinc_ref[before, :]
            dmain_ref[rows, 2 * D_GMLP : D_MAIN] = _rope_transposed(dq_all, _lane_tile(cos_c, n_qpairs), _lane_tile(sin_c, n_qpairs)).astype(BF16)
            dk2 = dk2_t.T
            dv2 = dv2_t.T
            cur = pl.ds(pl.multiple_of(chunk * CHUNK, CHUNK), CHUNK)
            dkv_ref[cur, 0:D_KV] = _rope_transposed(dk2[CHUNK:], cos_c, sin_c)
            dkv_ref[cur, D_KV : 2 * D_KV] = dv2[CHUNK:]
            prev = pl.ds(pl.multiple_of(jnp.maximum(chunk - 1, 0) * CHUNK, CHUNK), CHUNK)
            dkv_ref[prev, 0:D_KV] += _rope_transposed(dk2[:CHUNK], cos_p, sin_p)
            dkv_ref[prev, D_KV : 2 * D_KV] += dv2[:CHUNK]

            dg_ref[...] += jnp.sum(dvgl * xhat, axis=0, keepdims=True)
            db_ref[...] += jnp.sum(dvgl, axis=0, keepdims=True)
            dgv = _layer_norm_bwd(dvgl * gain, xhat, rstd)
            dmain_ref[rows, D_GMLP : 2 * D_GMLP] = (dgv * dgv_dv).astype(BF16)

        @pl.when(i == n_chunks // CHUNKS_PER_STEP - 1)
        def _():
            tile = jnp.zeros((CHUNK, LANES), F32)
            for p, cols in enumerate(pair_cols):
                dm = dmix_acc[:, cols]
                sl = jnp.sum(jnp.where(left, dm, 0.0), axis=1, keepdims=True)
                sr = jnp.sum(jnp.where(left, 0.0, dm), axis=1, keepdims=True)
                tile = jnp.where(lane == 2 * p, sl, tile)
                tile = jnp.where(lane == 2 * p + 1, sr, tile)
            dbs_ref[...] = tile
            for block_out in blocks_out:
                block_out.wait()

    step = CHUNKS_PER_STEP * CHUNK
    in_specs = _chunk_specs() + [
        pl.BlockSpec((step, D_MODEL), _step_rows),
        pl.BlockSpec((step, LANES), _step_rows),
        pl.BlockSpec((step, LANES), _step_rows),
        pl.BlockSpec((CHUNK, LANES), _chunk_before_step),
        pl.BlockSpec((CHUNK, LANES), _chunk_before_step),
        _const_spec((1, D_GMLP)),
        _const_spec((1, D_GMLP)),
        _const_spec((N_HEADS, CHUNK, CHUNK)),
        _const_spec((CHUNK, D_GMLP)),
        pl.BlockSpec(memory_space=pltpu.SMEM),
        pl.BlockSpec((step, D_FF), _step_rows),
        pl.BlockSpec((step, D_MODEL), _step_rows),
    ]
    body, in_specs, operands = _after(
        dep, body, in_specs, [u, vg, q, k, k, va, va, dcat, cos, sin, cos, sin, v_ln_g, v_ln_b, w_spatial, bias_full, sinks, r, dz2b])
    return pl.pallas_call(
        body,
        name="mixer_bwd",
        grid=(n_chunks // CHUNKS_PER_STEP,),
        in_specs=in_specs,
        out_specs=[
            pl.BlockSpec((step, D_MAIN), _step_rows),
            _const_spec((t, 2 * D_KV)),
            pl.BlockSpec(memory_space=pl.ANY),
            _const_spec((1, D_GMLP)),
            _const_spec((1, D_GMLP)),
            _const_spec((N_HEADS, CHUNK, CHUNK)),
            _const_spec((CHUNK, LANES)),
            _const_spec((8, LANES)),
        ],
        out_shape=[
            jax.ShapeDtypeStruct((t, D_MAIN), BF16),
            jax.ShapeDtypeStruct((t, 2 * D_KV), F32),
            jax.ShapeDtypeStruct((D_FF, D_MODEL), F32),
            jax.ShapeDtypeStruct((1, D_GMLP), F32),
            jax.ShapeDtypeStruct((1, D_GMLP), F32),
            jax.ShapeDtypeStruct((N_HEADS, CHUNK, CHUNK), F32),
            jax.ShapeDtypeStruct((CHUNK, LANES), F32),
            jax.ShapeDtypeStruct((8, LANES), F32),
        ],
        scratch_shapes=[
            pltpu.VMEM((CHUNK, D_GMLP), F32),
            pltpu.VMEM((D_GMLP // LANES, CHUNK, 2 * CHUNK), BF16),
            pltpu.VMEM((D_GMLP // LANES, CHUNK, 2 * CHUNK), BF16),
            pltpu.VMEM((D_FF, D_MODEL), F32),
            pltpu.SemaphoreType.DMA((CHUNKS_PER_STEP,)),
        ],
        compiler_params=_params(("arbitrary",)),
    )(*operands)


def _grad_x(dh_main, dkv, dz1, w_in_t, dep=None):
    t = dz1.shape[0]

    def body(dm_ref, dkv_ref, dz1_ref, w_ref, gx_ref):
        acc = ALPHA * dz1_ref[...] + _dot(dm_ref[...], w_ref[0:D_MAIN, :])
        gx_ref[...] = acc + _dot(dkv_ref[...].astype(BF16), w_ref[D_MAIN:D_IN, :])

    body, in_specs, operands = _after(
        dep, body, [_row_spec(TM, D_MAIN), _row_spec(TM, 2 * D_KV), _row_spec(TM, D_MODEL), _const_spec((D_IN, D_MODEL))], [dh_main, dkv, dz1, w_in_t])
    return pl.pallas_call(
        body,
        name="grad_x",
        grid=(t // TM,),
        in_specs=in_specs,
        out_specs=_row_spec(TM, D_MODEL),
        out_shape=jax.ShapeDtypeStruct((t, D_MODEL), F32),
        compiler_params=_params(("parallel",)),
    )(*operands)


def _token_contraction(name, out_rows, tk, in_arrays, contributions, dep=None):
    t = in_arrays[0].shape[0]

    def body(*refs):
        out_ref = refs[-1]

        @pl.when(pl.program_id(0) == 0)
        def _():
            out_ref[...] = jnp.zeros_like(out_ref)

        for row0, a, b in contributions(*refs[:-1]):
            out_ref[row0 : row0 + a.shape[1], :] += _dot(a, b, TN)

    in_specs = [_row_spec(tk, a.shape[1]) for a in in_arrays]
    body, in_specs, operands = _after(dep, body, in_specs, in_arrays)
    return pl.pallas_call(
        body,
        name=name,
        grid=(t // tk,),
        in_specs=in_specs,
        out_specs=_const_spec((out_rows, D_MODEL), single_buffer=True),
        out_shape=jax.ShapeDtypeStruct((out_rows, D_MODEL), F32),
        compiler_params=_params(("arbitrary",)),
    )(*operands)


def _grad_w_out(cat, dz1b, dep=None):
    def contributions(cat_ref, dz1_ref):
        return [(0, cat_ref[...], dz1_ref[...])]

    return _token_contraction("grad_w_out", D_MODEL, TK, [cat, dz1b], contributions, dep)


ANY = pl.BlockSpec(memory_space=pl.ANY)


def _mesh_position():
    return lax.axis_index("x"), lax.axis_index("y"), lax.axis_index("c")


def _other_chips(x, y):
    return [(1 - x, y), (x, 1 - y), (1 - x, 1 - y)]


def _remote(src, dst, send_sem, recv_sem, device):
    return pltpu.make_async_remote_copy(src_ref=src, dst_ref=dst, send_sem=send_sem, recv_sem=recv_sem, device_id=device, device_id_type=MESH)


def _rows(ref, start, size):
    return ref.at[pl.ds(start, size), :]


def _rope_tables_and_casts(pos_row, inv_freq_row, shards, dep=None):
    t = pos_row.shape[1]
    steps = t // TM
    n = len(shards)

    def body(pos_ref, f_ref, *rest):
        f32_refs, (cos_ref, sin_ref), bf16_refs = rest[:n], rest[n : n + 2], rest[n + 2 :]
        for src, dst in zip(f32_refs, bf16_refs):
            dst[...] = src[...].astype(BF16)
        pos_rows = jnp.broadcast_to(pos_ref[...].astype(F32), (LANES, TM)).T
        ang = pos_rows * f_ref[...]
        cos_ref[...] = jnp.cos(ang)
        sin_ref[...] = jnp.sin(ang)

    shard_specs = [_row_spec(s.shape[0] // steps, s.shape[1]) for s in shards]
    body, in_specs, operands = _after(
        dep, body, [pl.BlockSpec((1, TM), lambda i: (0, i)), _const_spec((1, LANES))] + shard_specs, [pos_row, inv_freq_row, *shards])
    outs = pl.pallas_call(
        body,
        name="rope_tables_and_casts",
        grid=(steps,),
        in_specs=in_specs,
        out_specs=[_row_spec(TM, LANES), _row_spec(TM, LANES)] + shard_specs,
        out_shape=[jax.ShapeDtypeStruct((t, LANES), F32)] * 2 + [jax.ShapeDtypeStruct(s.shape, BF16) for s in shards],
        compiler_params=_params(("parallel",)),
    )(*operands)
    return outs[0], outs[1], list(outs[2:])


def _grad_w_in_t_and_small_all_reduce(dh_main, dkv, x, slab, dep=None):
    t = x.shape[0]
    steps = t // TK
    rows = slab.shape[0]
    part = rows // 8

    def body(dm_ref, dkv_ref, x_ref, slab_ref, grad_ref, sum_ref, landing, reduced, gathered, send_sems, recv_sems):
        k = pl.program_id(0)
        x_, y_, c_ = _mesh_position()
        me = 4 * x_ + 2 * y_ + c_
        flips = [(f >> 2, (f >> 1) & 1, f & 1) for f in range(1, 8)]

        def peer(flip):
            fx, fy, fc = flip
            return (1 - x_ if fx else x_, 1 - y_ if fy else y_, 1 - c_ if fc else c_)

        def part_of(ref, device):
            return ref.at[pl.ds(pl.multiple_of(device * part, 8), part), :]

        def scatter_copies():
            out = []
            for kk, flip in enumerate(flips):
                px, py, pc = peer(flip)
                them = 4 * px + 2 * py + pc
                send = _remote(part_of(slab_ref, them), landing.at[me], send_sems.at[kk], recv_sems.at[kk], (px, py, pc))
                recv = _remote(landing.at[them], landing.at[them], send_sems.at[kk], recv_sems.at[kk], (px, py, pc))
                out.append((send, recv))
            return out

        def gather_copies():
            out = []
            for kk, flip in enumerate(flips):
                px, py, pc = peer(flip)
                them = 4 * px + 2 * py + pc
                send = _remote(reduced, part_of(gathered, me), send_sems.at[7 + kk], recv_sems.at[7 + kk], (px, py, pc))
                recv = _remote(part_of(gathered, them), part_of(gathered, them), send_sems.at[7 + kk], recv_sems.at[7 + kk], (px, py, pc))
                out.append((send, recv))
            return out

        @pl.when(k == 0)
        def _():
            grad_ref[...] = jnp.zeros_like(grad_ref)
            for send, _ in scatter_copies():
                send.start()
            landing[me] = part_of(slab_ref, me)[...]

        @pl.when(k == steps // 2)
        def _():
            for _, recv in scatter_copies():
                recv.wait_recv()
            total = landing[0]
            for s in range(1, 8):
                total = total + landing[s]
            reduced[...] = total
            part_of(gathered, me)[...] = total
            for send, _ in gather_copies():
                send.start()

        xb = x_ref[...].astype(BF16)
        grad_ref[0:D_MAIN, :] += _dot(dm_ref[...], xb, TN)
        grad_ref[D_MAIN:D_IN, :] += _dot(dkv_ref[...].astype(BF16), xb, TN)

        @pl.when(k == steps - 1)
        def _():
            for send, recv in gather_copies():
                recv.wait_recv()
                send.wait_send()
            for send, _ in scatter_copies():
                send.wait_send()
            sum_ref[...] = gathered[...]

    body, in_specs, operands = _after(
        dep, body, [_row_spec(TK, D_MAIN), _row_spec(TK, 2 * D_KV), _row_spec(TK, D_MODEL), _const_spec(slab.shape)], [dh_main, dkv, x, slab])
    return pl.pallas_call(
        body,
        name="grad_w_in_and_small_all_reduce",
        grid=(steps,),
        in_specs=in_specs,
        out_specs=[_const_spec((D_IN, D_MODEL), single_buffer=True), _const_spec(slab.shape)],
        out_shape=[jax.ShapeDtypeStruct((D_IN, D_MODEL), F32), jax.ShapeDtypeStruct(slab.shape, slab.dtype)],
        scratch_shapes=[
            pltpu.VMEM((8, part, LANES), F32),
            pltpu.VMEM((part, LANES), F32),
            pltpu.VMEM(slab.shape, F32),
            pltpu.SemaphoreType.DMA((14,)),
            pltpu.SemaphoreType.DMA((14,)),
        ],
        compiler_params=_params(("arbitrary",)),
    )(*operands)


HBM = pl.BlockSpec(memory_space=pltpu.HBM)
SEM = pl.BlockSpec(memory_space=pltpu.SEMAPHORE)
DATAFLOW = pltpu.SideEffectType.DATAFLOW_SIDE_EFFECTING
TOKEN = jax.ShapeDtypeStruct((8, LANES), F32)


def _plan_copies(bufs, plan, send_sems, recv_sems):
    out = []
    for i, (src, src_row, dst, dst_row, recv_row, rows, device) in enumerate(plan):
        send = _remote(_rows(bufs[src], src_row, rows), _rows(bufs[dst], dst_row, rows), send_sems.at[i], recv_sems.at[i], device)
        landed = _rows(bufs[dst], recv_row, rows)
        recv = _remote(landed, landed, send_sems.at[i], recv_sems.at[i], device)
        out.append((send, recv))
    return out


def _split_call(name, bufs, wait=None, start=None, after=None):
    n = len(bufs)
    n_in = n + (2 if wait else 0) + (1 if after is not None else 0)
    n_start = len(start(0, 0, 0)) if start else 0

    def body(*refs):
        ins = refs[:n]
        x, y, c = _mesh_position()
        if wait:
            for send, recv in _plan_copies(ins, wait[0](x, y, c), refs[n], refs[n + 1]):
                recv.wait_recv()
                send.wait_send()
        if start:
            for send, _ in _plan_copies(ins, start(x, y, c), refs[n_in + n + 1], refs[n_in + n + 2]):
                send.start()
        token = refs[n_in + n]
        token[...] = jnp.zeros_like(token)

    operands = [pltpu.with_memory_space_constraint(b, pltpu.HBM) for b in bufs]
    in_specs = [HBM] * n
    if wait:
        operands += [wait[1], wait[2]]
        in_specs += [SEM, SEM]
    if after is not None:
        operands.append(after)
        in_specs.append(ANY)
    out_shape = [pltpu.HBM(b.shape, b.dtype) for b in bufs] + [TOKEN]
    out_specs = [HBM] * n + [pl.BlockSpec(memory_space=pltpu.VMEM)]
    if start:
        out_shape += [pltpu.SemaphoreType.DMA((n_start,)), pltpu.SemaphoreType.DMA((n_start,))]
        out_specs += [SEM, SEM]
    outs = pl.pallas_call(
        body,
        name=name,
        in_specs=in_specs,
        out_specs=out_specs,
        out_shape=out_shape,
        input_output_aliases={i: i for i in range(n)},
        compiler_params=pltpu.CompilerParams(has_side_effects=DATAFLOW),
    )(*operands)
    return (list(outs[:n]), outs[n]) + tuple(outs[n + 1 :])


def _direct_gather_plans(shard_rows):
    n = len(shard_rows)

    def direct(x, y, c):
        me = 2 * x + y
        plan = []
        for w, rows in enumerate(shard_rows):
            half = rows // 2
            for px, py in _other_chips(x, y):
                plan.append((w, c * half, n + w, me * rows + c * half, (2 * px + py) * rows + c * half, half, (px, py, c)))
            plan.append((w, 0, n + w, me * rows, me * rows, rows, (x, y, 1 - c)))
        return plan

    def passed_on(x, y, c):
        plan = []
        for w, rows in enumerate(shard_rows):
            half = rows // 2
            for px, py in _other_chips(x, y):
                row = (2 * px + py) * rows
                plan.append((n + w, row + c * half, n + w, row + c * half, row + (1 - c) * half, half, (x, y, 1 - c)))
        return plan

    return direct, passed_on


def _gather_plans(shard_rows):
    n = len(shard_rows)

    def neighbours(x, y):
        return ((1 - x, y), (x, 1 - y))

    def direct(x, y, c):
        me = 2 * x + y
        plan = []
        for w, rows in enumerate(shard_rows):
            half = rows // 2
            for px, py in neighbours(x, y):
                plan.append((w, c * half, n + w, me * rows + c * half, (2 * px + py) * rows + c * half, half, (px, py, c)))
            plan.append((w, 0, n + w, me * rows, me * rows, rows, (x, y, 1 - c)))
        return plan

    def passed_on(x, y, c):
        (xn, yn), diagonal = neighbours(x, y), 2 * (1 - x) + (1 - y)
        relayed = (1 - c) * (2 * xn[0] + xn[1]) + c * (2 * yn[0] + yn[1])
        target = (x * (1 - c) + (1 - x) * c, (1 - y) * (1 - c) + y * c, c)
        plan = []
        for w, rows in enumerate(shard_rows):
            half = rows // 2
            for px, py in (xn, yn):
                row = (2 * px + py) * rows
                plan.append((n + w, row + c * half, n + w, row + c * half, row + (1 - c) * half, half, (x, y, 1 - c)))
            plan.append((n + w, relayed * rows + c * half, n + w, relayed * rows + c * half, diagonal * rows + c * half, half, target))
        return plan

    def diagonal_passed_on(x, y, c):
        plan = []
        for w, rows in enumerate(shard_rows):
            half = rows // 2
            row = (2 * (1 - x) + (1 - y)) * rows
            plan.append((n + w, row + c * half, n + w, row + c * half, row + (1 - c) * half, half, (x, y, 1 - c)))
        return plan

    return direct, passed_on, diagonal_passed_on


def _swap_plan(block_rows):
    n = len(block_rows)

    def plan_fn(x, y, c):
        plan = []
        for w, rows in enumerate(block_rows):
            half = rows // 2
            for j in range(N_CHIPS):
                plan.append((w, j * rows + (1 - c) * half, n + w, j * half, j * half, half, (x, y, 1 - c)))
        return plan

    return plan_fn


def _exchange_plan(halves):
    n = len(halves)

    def plan_fn(x, y, c):
        plan = []
        for w, half in enumerate(halves):
            for kk, (px, py) in enumerate(_other_chips(x, y)):
                plan.append((w, (2 * px + py) * half, n + w, kk * half, kk * half, half, (px, py, c)))
        return plan

    return plan_fn


def _sibling_plan(shard_rows):
    def plan_fn(x, y, c):
        return [(w, c * (rows // 2), w, c * (rows // 2), (1 - c) * (rows // 2), rows // 2, (x, y, 1 - c)) for w, rows in enumerate(shard_rows)]

    return plan_fn


def _shifted(plan_fn, first):
    return lambda x, y, c: [(src + first, a, dst + first, b, r, n, dev) for src, a, dst, b, r, n, dev in plan_fn(x, y, c)]


def _landing(rows, cols, dtype):
    return lax.empty((rows, cols), dtype)


def _row_tile(rows, cap=512):
    best = 8
    for cand in range(8, cap + 1, 8):
        if rows % cand == 0:
            best = cand
    return best


def _pair_sum(name, grad, theirs, pos):
    half = theirs.shape[0] // N_CHIPS
    cols = theirs.shape[1]
    tile = _row_tile(half)
    steps = half // tile

    def body(pos_ref, g_ref, t_ref, p_ref, own_ref):
        total = g_ref[...] + t_ref[...]
        p_ref[...] = total.astype(BF16)

        @pl.when(pl.program_id(1) == pos_ref[1])
        def _():
            own_ref[...] = total

    return pl.pallas_call(
        body,
        name=name,
        grid_spec=pltpu.PrefetchScalarGridSpec(
            num_scalar_prefetch=1,
            grid=(steps, N_CHIPS),
            in_specs=[
                pl.BlockSpec((tile, cols), lambda i, j, pos: ((2 * j + pos[0]) * steps + i, 0)),
                pl.BlockSpec((tile, cols), lambda i, j, pos: (j * steps + i, 0)),
            ],
            out_specs=[
                pl.BlockSpec((tile, cols), lambda i, j, pos: (j * steps + i, 0)),
                pl.BlockSpec((tile, cols), lambda i, j, pos: (i, 0)),
            ],
        ),
        out_shape=[jax.ShapeDtypeStruct((N_CHIPS * half, cols), BF16), jax.ShapeDtypeStruct((half, cols), F32)],
        compiler_params=_params(("parallel", "arbitrary")),
    )(pos, grad, theirs)


def _adamw_update(w, g, m, v):
    nm = ADAM_B1 * m + (1.0 - ADAM_B1) * g
    nv = ADAM_B2 * v + (1.0 - ADAM_B2) * (g * g)
    m_hat = nm / (1.0 - ADAM_B1**ADAM_STEP)
    v_hat = nv / (1.0 - ADAM_B2**ADAM_STEP)
    return -ADAM_LR * (m_hat / (jnp.sqrt(v_hat) + ADAM_EPS) + ADAM_WD * w), nm, nv


def _chip_sum(name, own, landed, pos):
    half, cols = own.shape
    tile = _row_tile(half)
    steps = half // tile

    def body(pos_ref, own_ref, l0, l1, l2, o_ref):
        o_ref[...] = ((own_ref[...] + l0[...].astype(F32)) + l1[...].astype(F32)) + l2[...].astype(F32)

    landed_specs = [pl.BlockSpec((tile, cols), lambda i, pos, _k=k: (_k * steps + i, 0)) for k in range(N_CHIPS - 1)]
    return pl.pallas_call(
        body,
        name=name,
        grid_spec=pltpu.PrefetchScalarGridSpec(
            num_scalar_prefetch=1,
            grid=(steps,),
            in_specs=[pl.BlockSpec((tile, cols), lambda i, pos: (i, 0))] + landed_specs,
            out_specs=pl.BlockSpec((tile, cols), lambda i, pos: (pos[0] * steps + i, 0)),
        ),
        out_shape=jax.ShapeDtypeStruct((2 * half, cols), F32),
        compiler_params=_params(("parallel",)),
    )(pos, own, landed, landed, landed)


def _adamw(name, w, g, m, v):
    rows, cols = w.shape
    tile = rows if rows * cols <= 256 * 1024 else _row_tile(rows)

    def body(w_ref, g_ref, m_ref, v_ref, g_out_ref, d_ref, nm_ref, nv_ref):
        g = g_ref[...]
        g_out_ref[...] = g
        d_ref[...], nm_ref[...], nv_ref[...] = _adamw_update(w_ref[...], g, m_ref[...], v_ref[...])

    spec = _row_spec(tile, cols)
    return pl.pallas_call(
        body,
        name=name,
        grid=(rows // tile,),
        in_specs=[spec] * 4,
        out_specs=[spec] * 4,
        out_shape=[jax.ShapeDtypeStruct((rows, cols), F32)] * 4,
        compiler_params=_params(("parallel",)),
    )(w, g, m, v)


_SMALL = (
    ("v_ln_g", (D_GMLP,), 8),
    ("v_ln_b", (D_GMLP,), 8),
    ("w_spatial", (N_HEADS, CHUNK, CHUNK), 1024),
    ("b_spatial", (N_HEADS, CHUNK), 8),
    ("sinks", (N_HEADS,), 8),
    ("ln1_g", (D_MODEL,), 8),
    ("ln1_b", (D_MODEL,), 8),
    ("ln2_g", (D_MODEL,), 8),
    ("ln2_b", (D_MODEL,), 8),
    ("squared_error", (D_MODEL,), 8),
)
N_SMALL_PARAMS = len(_SMALL) - 1


def _pack_small(values):
    parts = []
    for (name, shape, rows), val in zip(_SMALL, values, strict=True):
        flat = val.reshape(-1).astype(F32)
        parts.append(jnp.pad(flat, (0, rows * LANES - flat.shape[0])).reshape(rows, LANES))
    parts.append(jnp.zeros((SMALL_ROWS - sum(rows for _, _, rows in _SMALL), LANES), F32))
    return jnp.concatenate(parts, axis=0)


def _adamw_small(g_slab, params, first, second):
    n = N_SMALL_PARAMS

    def pieces(shape):
        if len(shape) == 3:
            return [((0, h), h * shape[1], shape[1], shape[2]) for h in range(shape[0])]
        if len(shape) == 2:
            return [((0,), 0, shape[0], shape[1])]
        if shape[0] >= LANES:
            return [((slice(None), slice(r * LANES, (r + 1) * LANES)), r, 1, LANES) for r in range(shape[0] // LANES)]
        return [((slice(None), slice(0, shape[0])), 0, 1, shape[0])]

    def body(*refs):
        g_ref = refs[0]
        w_refs, m_refs, v_refs = refs[1 : 1 + n], refs[1 + n : 1 + 2 * n], refs[1 + 2 * n : 1 + 3 * n]
        outs = refs[1 + 3 * n :]
        row0 = 0
        for idx, (_, shape, rows) in enumerate(_SMALL[:n]):
            for where, first_row, n_rows, lanes in pieces(shape):
                g = g_ref[row0 + first_row : row0 + first_row + n_rows, 0:lanes]
                delta, nm, nv = _adamw_update(w_refs[idx][where], g, m_refs[idx][where], v_refs[idx][where])
                for group, val in enumerate((g, delta, nm, nv)):
                    outs[group * n + idx][where] = val
            row0 += rows

    vmem = pl.BlockSpec(memory_space=pltpu.VMEM)
    shapes = [jax.ShapeDtypeStruct(p.shape, F32) for p in params]
    outs = pl.pallas_call(
        body,
        name="adamw_small",
        in_specs=[vmem] * (1 + 3 * n),
        out_specs=[vmem] * (4 * n),
        out_shape=shapes * 4,
        compiler_params=_params(),
    )(g_slab, *params, *first, *second)
    return [list(outs[group * n : (group + 1) * n]) for group in range(4)]


def kernel(x, positions, w_in, v_ln_g, v_ln_b, w_spatial, b_spatial, sinks, w_out, ln1_g, ln1_b, w_ff1, w_ff2, ln2_g, ln2_b, loss_target, m_w_in, m_v_ln_g, m_v_ln_b, m_w_spatial, m_b_spatial, m_sinks, m_w_out, m_ln1_g, m_ln1_b, m_w_ff1, m_w_ff2, m_ln2_g, m_ln2_b, v_w_in, v_v_ln_g, v_v_ln_b, v_w_spatial, v_b_spatial, v_sinks, v_w_out, v_ln1_g, v_ln1_b, v_w_ff1, v_w_ff2, v_ln2_g, v_ln2_b):
    t = x.shape[1]
    x2 = x.reshape(t, D_MODEL)
    target = loss_target.reshape(t, D_MODEL)

    w_in_shard = w_in[0].T.astype(BF16)
    in_direct, in_pass = _direct_gather_plans([w_in_shard.shape[0]])
    in_bufs, in_started, in_send, in_recv = _split_call(
        "gather_w_in_start", [w_in_shard, _landing(N_CHIPS * w_in_shard.shape[0], D_MODEL, BF16)], start=in_direct)
    inv_freq = ROPE_THETA ** (-jnp.arange(0, HEAD_DIM, 2, dtype=F32) / HEAD_DIM)
    cos, sin, later = _rope_tables_and_casts(
        positions, jnp.tile(inv_freq, LANES // (HEAD_DIM // 2)).reshape(1, LANES), [w_out[0], w_ff1[0], w_ff2[0]], dep=in_started)
    later_rows = [s.shape[0] for s in later]
    direct_plan, pass_plan, diagonal_plan = _gather_plans(later_rows)
    bufs, started, direct_send, direct_recv = _split_call(
        "gather_start", later + [_landing(N_CHIPS * r, D_MODEL, BF16) for r in later_rows], start=direct_plan, after=cos)
    in_bufs, in_passing, in_pass_send, in_pass_recv = _split_call(
        "gather_w_in_pass", in_bufs, wait=(in_direct, in_send, in_recv), start=in_pass, after=started)
    in_bufs, _ = _split_call("gather_w_in_end", in_bufs, wait=(in_pass, in_pass_send, in_pass_recv), after=in_passing)
    w_in_t = in_bufs[1]

    u, vg, q, k, va = _in_proj(x2, w_in_t, cos, sin)
    bias_full = jnp.repeat(b_spatial[0].T, HEAD_DIM, axis=1)
    sink_vec = sinks.reshape(N_HEADS)
    bufs, passing, pass_send, pass_recv = _split_call(
        "gather_pass", bufs, wait=(direct_plan, direct_send, direct_recv), start=pass_plan, after=u)
    cat = _mixer_fwd(u, vg, q, k, va, v_ln_g, v_ln_b, w_spatial[0], bias_full, sink_vec, dep=passing)
    bufs, passing, diag_send, diag_recv = _split_call(
        "gather_pass_diagonal", bufs, wait=(pass_plan, pass_send, pass_recv), start=diagonal_plan, after=cat)
    bufs, _ = _split_call("gather_end", bufs, wait=(diagonal_plan, diag_send, diag_recv), after=passing)
    w_out_all = bufs[3]
    w1_all = bufs[4].reshape(N_FF_BLOCKS, D_MODEL, D_MODEL)
    w2_all = bufs[5].reshape(N_FF_BLOCKS, D_MODEL, D_MODEL)
    xhat1, rstd1, x1b, r, dz2, dz2b, d_ln2_g, d_ln2_b, sq_err = _ffn_fwd_loss(
        cat, x2, w_out_all, ln1_g, ln1_b, w1_all, w2_all, ln2_g, ln2_b, target)

    pos = jnp.stack([lax.axis_index("c"), 2 * lax.axis_index("x") + lax.axis_index("y")]).astype(jnp.int32)
    half_landing = lambda g: _landing(g.shape[0] // 2, D_MODEL, F32)
    ff_swap_plan = _swap_plan([D_FF // N_CHIPS])
    ff_exchange_plan = _exchange_plan([D_FF // N_CHIPS // 2])
    exchange_landing = lambda p: _landing(3 * p.shape[0] // N_CHIPS, D_MODEL, BF16)
    g_ff1_local, dz1, dz1b, dcat, d_ln1_g, d_ln1_b = _ffn_bwd_ln1(dz2, r, x1b, xhat1, rstd1, ln1_g, w1_all, w2_all, w_out_all)
    ff1_bufs, swapping1, swap1_send, swap1_recv = _split_call("ff1_swap_start", [g_ff1_local, half_landing(g_ff1_local)], start=ff_swap_plan)
    g_out_local = _grad_w_out(cat, dz1b, dep=swapping1)
    ff1_bufs, _ = _split_call("ff1_swap_wait", ff1_bufs, wait=(ff_swap_plan, swap1_send, swap1_recv), after=g_out_local)
    ff1_sum, ff1_own = _pair_sum("grad_pair_sum_w_ff1", ff1_bufs[0], ff1_bufs[1], pos)
    ff1_ex, exchanging1, ex1_send, ex1_recv = _split_call(
        "ff1_exchange_start", [ff1_sum, exchange_landing(ff1_sum)], start=ff_exchange_plan)
    dh_main, dkv, g_ff2_local, d_v_ln_g, d_v_ln_b, d_w_spatial, d_b_spatial_t, d_sinks = _mixer_bwd(
        u, vg, q, k, va, dcat, cos, sin, v_ln_g, v_ln_b, w_spatial[0], bias_full, sink_vec, r, dz2b, dep=exchanging1)
    ff2_bufs, swapping2, swap2_send, swap2_recv = _split_call("ff2_swap_start", [g_ff2_local, half_landing(g_ff2_local)], start=ff_swap_plan)
    g_in_local, small_g = _grad_w_in_t_and_small_all_reduce(dh_main, dkv, x2, _pack_small(
        [d_v_ln_g, d_v_ln_b, d_w_spatial, d_b_spatial_t[:, :N_HEADS].T, d_sinks[0, :N_HEADS], d_ln1_g, d_ln1_b, d_ln2_g, d_ln2_b, sq_err]),
        dep=swapping2)
    sq_row = sum(rows for _, _, rows in _SMALL[:N_SMALL_PARAMS])
    loss = 0.5 * jnp.sum(small_g[sq_row : sq_row + _SMALL[N_SMALL_PARAMS][2]]) / D_MODEL
    ff2_bufs, _ = _split_call("ff2_swap_wait", ff2_bufs, wait=(ff_swap_plan, swap2_send, swap2_recv), after=g_in_local)
    ff2_sum, ff2_own = _pair_sum("grad_pair_sum_w_ff2", ff2_bufs[0], ff2_bufs[1], pos)
    ff2_ex, exchanging2, ex2_send, ex2_recv = _split_call(
        "ff2_exchange_start", [ff2_sum, exchange_landing(ff2_sum)], start=ff_exchange_plan)

    small = [g_in_local, g_out_local]
    small_swap_plan = _swap_plan([g.shape[0] // N_CHIPS for g in small])
    swap_bufs, small_swapping, ss_send, ss_recv = _split_call(
        "small_swap_start", small + [half_landing(g) for g in small], start=small_swap_plan, after=exchanging2)
    grad_x_flat = _grad_x(dh_main, dkv, dz1, w_in_t, dep=small_swapping)
    grad_x = grad_x_flat.reshape(1, t, D_MODEL)
    swap_bufs, _ = _split_call("small_swap_wait", swap_bufs, wait=(small_swap_plan, ss_send, ss_recv), after=grad_x_flat)
    pair_sums = [_pair_sum("grad_pair_sum_" + nm, g, th, pos) for nm, g, th in zip(["w_in", "w_out"], swap_bufs[:2], swap_bufs[2:])]
    small_plan = _exchange_plan([p.shape[0] // N_CHIPS for p, _ in pair_sums])
    small_bufs, small_exchanging, sm_send, sm_recv = _split_call(
        "small_exchange_start", [p for p, _ in pair_sums] + [exchange_landing(p) for p, _ in pair_sums], start=small_plan)

    ff1_ex, _ = _split_call("ff1_exchange_wait", ff1_ex, wait=(ff_exchange_plan, ex1_send, ex1_recv), after=small_exchanging)
    ff_pair_plan = _sibling_plan([D_FF // N_CHIPS])
    half_ff1 = _chip_sum("grad_chip_sum_w_ff1", ff1_own, ff1_ex[1], pos)
    (half_ff1, *ff2_ex), _, g1_send, g1_recv = _split_call(
        "ff2_exchange_wait_ff1_pair_start", [half_ff1] + ff2_ex, wait=(_shifted(ff_exchange_plan, 1), ex2_send, ex2_recv), start=ff_pair_plan)
    half_ff2 = _chip_sum("grad_chip_sum_w_ff2", ff2_own, ff2_ex[1], pos)
    (half_ff2, g_w_ff1), _, g2_send, g2_recv = _split_call(
        "ff1_pair_wait_ff2_pair_start", [half_ff2, half_ff1], wait=(_shifted(ff_pair_plan, 1), g1_send, g1_recv), start=ff_pair_plan)
    g_w_ff1, d_w_ff1, nm_w_ff1, nv_w_ff1 = _adamw("adamw_w_ff1", w_ff1[0], g_w_ff1, m_w_ff1[0], v_w_ff1[0])
    small_bufs, _ = _split_call("small_exchange_wait", small_bufs, wait=(small_plan, sm_send, sm_recv), after=nv_w_ff1)
    shards = [_chip_sum("grad_chip_sum_" + nm, own, ld, pos) for nm, (_, own), ld in zip(["w_in", "w_out"], pair_sums, small_bufs[2:])]
    small_pair_plan = _sibling_plan([s.shape[0] for s in shards])
    (*shards, g_w_ff2), _, g3_send, g3_recv = _split_call(
        "ff2_pair_wait_small_pair_start", shards + [half_ff2], wait=(_shifted(ff_pair_plan, 2), g2_send, g2_recv), start=small_pair_plan)
    g_w_ff2, d_w_ff2, nm_w_ff2, nv_w_ff2 = _adamw("adamw_w_ff2", w_ff2[0], g_w_ff2, m_w_ff2[0], v_w_ff2[0])
    (g_w_in_t, g_w_out), _ = _split_call("small_pair_wait", shards, wait=(small_pair_plan, g3_send, g3_recv), after=nv_w_ff2)
    g_w_in, d_w_in, nm_w_in, nv_w_in = (a.T for a in _adamw("adamw_w_in", w_in[0].T, g_w_in_t, m_w_in[0].T, v_w_in[0].T))
    g_w_out, d_w_out, nm_w_out, nv_w_out = _adamw("adamw_w_out", w_out[0], g_w_out, m_w_out[0], v_w_out[0])
    small_grads, small_d, small_nm, small_nv = _adamw_small(
        small_g,
        [v_ln_g, v_ln_b, w_spatial, b_spatial, sinks, ln1_g, ln1_b, ln2_g, ln2_b],
        [m_v_ln_g, m_v_ln_b, m_w_spatial, m_b_spatial, m_sinks, m_ln1_g, m_ln1_b, m_ln2_g, m_ln2_b],
        [v_v_ln_g, v_v_ln_b, v_w_spatial, v_b_spatial, v_sinks, v_ln1_g, v_ln1_b, v_ln2_g, v_ln2_b])

    def with_big(small, w_in_v, w_out_v, w_ff1_v, w_ff2_v):
        g_vg, g_vb, g_ws, g_bs, g_sk, g_1g, g_1b, g_2g, g_2b = small
        return [w_in_v[None], g_vg, g_vb, g_ws, g_bs, g_sk, w_out_v[None], g_1g, g_1b, w_ff1_v[None], w_ff2_v[None], g_2g, g_2b]

    return (
        loss,
        grad_x,
        *with_big(small_grads, g_w_in, g_w_out, g_w_ff1, g_w_ff2),
        *with_big(small_d, d_w_in, d_w_out, d_w_ff1, d_w_ff2),
        *with_big(small_nm, nm_w_in, nm_w_out, nm_w_ff1, nm_w_ff2),
        *with_big(small_nv, nv_w_in, nv_w_out, nv_w_ff1, nv_w_ff2),
    )
```

```python
import math

import jax
import jax.numpy as jnp
from jax import lax
from jax.experimental import pallas as pl
from jax.experimental.pallas import tpu as pltpu

F32 = jnp.float32
BF16 = jnp.bfloat16

D_MODEL = 1024
HEAD_DIM = 64
D_GMLP = 512
D_ATTN = 512
D_KV = 128
D_IN = 2 * D_GMLP + D_ATTN + 2 * D_KV
D_MAIN = 2 * D_GMLP + D_ATTN
N_HEADS = 8
CHUNK = 128
CHUNKS_PER_STEP = 4
ROPE_THETA = 10000.0
D_FF = 4 * D_MODEL
N_FF_BLOCKS = 4
LN_EPS = 1e-5
ALPHA = (2.0 * 1) ** 0.25
NEG_INF = -1e30
SCALE = 1.0 / math.sqrt(HEAD_DIM)

ADAM_LR = 0.001
ADAM_B1 = 0.9
ADAM_B2 = 0.999
ADAM_EPS = 1e-08
ADAM_WD = 0.01
ADAM_STEP = 10

N_CHIPS = 4
LANES = 128
V7X_VMEM_BYTES = 64 * 1024 * 1024
VMEM_LIMIT = V7X_VMEM_BYTES - 8 * 1024 * 1024
TM = 1024
TM_FFN = 256
TM_FFN_FWD = 512
FFN_PART = 256
TK = 1024
SMALL_ROWS = 1152
MESH = pl.DeviceIdType.MESH

NT = (((1,), (1,)), ((), ()))
TN = (((0,), (0,)), ((), ()))


def _dot(a, b, dims=None):
    if dims is None:
        return jnp.dot(a, b, preferred_element_type=F32)
    return lax.dot_general(a, b, dims, preferred_element_type=F32)


def _params(semantics=None):
    return pltpu.CompilerParams(dimension_semantics=semantics, vmem_limit_bytes=VMEM_LIMIT)


def _const_spec(shape, single_buffer=False):
    zeros = (0,) * len(shape)
    if single_buffer:
        return pl.BlockSpec(shape, lambda *_: zeros, pipeline_mode=pl.Buffered(1))
    return pl.BlockSpec(shape, lambda *_: zeros)


def _row_spec(rows, cols):
    return pl.BlockSpec((rows, cols), lambda i: (i, 0))


def _after(dep, body, in_specs, operands):
    if dep is None:
        return body, list(in_specs), list(operands)
    return (lambda dep_ref, *refs: body(*refs)), [pl.BlockSpec(memory_space=pl.ANY)] + list(in_specs), [dep] + list(operands)


def _gelu(x):
    k = math.sqrt(2.0 / math.pi)
    return 0.5 * x * (1.0 + jnp.tanh(k * (x + 0.044715 * (x * x * x))))


def _gelu_and_grad(x):
    k = math.sqrt(2.0 / math.pi)
    x2 = x * x
    t = jnp.tanh(k * (x + 0.044715 * (x2 * x)))
    g = 0.5 * x * (1.0 + t)
    dg = 0.5 * (1.0 + t) + 0.5 * x * (1.0 - t * t) * (k * (1.0 + 3.0 * 0.044715 * x2))
    return g, dg


def _layer_norm_stats(z):
    mu = jnp.mean(z, axis=-1, keepdims=True)
    zc = z - mu
    var = jnp.mean(zc * zc, axis=-1, keepdims=True)
    rstd = lax.rsqrt(var + LN_EPS)
    return zc * rstd, rstd


def _layer_norm_bwd(dxhat, xhat, rstd):
    m1 = jnp.mean(dxhat, axis=-1, keepdims=True)
    m2 = jnp.mean(dxhat * xhat, axis=-1, keepdims=True)
    return rstd * (dxhat - m1 - xhat * m2)


def _rotate_half(t):
    n = t.shape[1]
    lane = lax.broadcasted_iota(jnp.int32, t.shape, 1)
    first = (lane & (HEAD_DIM // 2)) == 0
    return jnp.where(first, -pltpu.roll(t, n - HEAD_DIM // 2, 1), pltpu.roll(t, HEAD_DIM // 2, 1))


def _rope(t, cos, sin):
    return t * cos + _rotate_half(t) * sin


def _rope_transposed(g, cos, sin):
    return g * cos - _rotate_half(g * sin)


def _lane_tile(a, reps):
    return jnp.tile(a, (1, reps)) if reps > 1 else a


def _in_proj(x, w_in_t, cos, sin, dep=None):
    t = x.shape[0]

    def body(x_ref, w_ref, cos_ref, sin_ref, u_ref, vg_ref, q_ref, k_ref, va_ref):
        xb = x_ref[...].astype(BF16)
        u_ref[...] = _dot(xb, w_ref[0:D_GMLP, :], NT)
        vg_ref[...] = _dot(xb, w_ref[D_GMLP : 2 * D_GMLP, :], NT)
        q = _dot(xb, w_ref[2 * D_GMLP : D_MAIN, :], NT)
        k = _dot(xb, w_ref[D_MAIN : D_MAIN + D_KV, :], NT)
        va_ref[...] = _dot(xb, w_ref[D_MAIN + D_KV : D_IN, :], NT).astype(BF16)
        c, s = cos_ref[...], sin_ref[...]
        q_ref[...] = _rope(q, _lane_tile(c, D_ATTN // LANES), _lane_tile(s, D_ATTN // LANES)).astype(BF16)
        k_ref[...] = _rope(k, c, s).astype(BF16)

    body, in_specs, operands = _after(
        dep, body, [_row_spec(TM, D_MODEL), _const_spec((D_IN, D_MODEL)), _row_spec(TM, LANES), _row_spec(TM, LANES)], [x, w_in_t, cos, sin])
    return pl.pallas_call(
        body,
        name="in_proj",
        grid=(t // TM,),
        in_specs=in_specs,
        out_specs=[_row_spec(TM, D_GMLP), _row_spec(TM, D_GMLP), _row_spec(TM, D_ATTN), _row_spec(TM, D_KV), _row_spec(TM, D_KV)],
        out_shape=[
            jax.ShapeDtypeStruct((t, D_GMLP), F32),
            jax.ShapeDtypeStruct((t, D_GMLP), F32),
            jax.ShapeDtypeStruct((t, D_ATTN), BF16),
            jax.ShapeDtypeStruct((t, D_KV), BF16),
            jax.ShapeDtypeStruct((t, D_KV), BF16),
        ],
        compiler_params=_params(("parallel",)),
    )(*operands)


def _step_rows(i):
    return (i, 0)


def _chunk_before_step(i):
    return (jnp.maximum(CHUNKS_PER_STEP * i - 1, 0), 0)


def _chunk_specs():
    step = CHUNKS_PER_STEP * CHUNK
    return [
        pl.BlockSpec((step, D_GMLP), _step_rows),
        pl.BlockSpec((step, D_GMLP), _step_rows),
        pl.BlockSpec((step, D_ATTN), _step_rows),
        pl.BlockSpec((step, D_KV), _step_rows),
        pl.BlockSpec((CHUNK, D_KV), _chunk_before_step),
        pl.BlockSpec((step, D_KV), _step_rows),
        pl.BlockSpec((CHUNK, D_KV), _chunk_before_step),
    ]


def _half_lane_masks(rows):
    lane = lax.broadcasted_iota(jnp.int32, (rows, LANES), 1)
    return lane < HEAD_DIM


def _kv_variants(kv2):
    left = _half_lane_masks(kv2.shape[0])
    f = kv2.astype(F32)
    swapped = pltpu.roll(f, HEAD_DIM, 1)
    zero = jnp.zeros_like(f)
    g0 = (jnp.where(left, f, zero).astype(BF16), jnp.where(left, zero, swapped).astype(BF16))
    g1 = (jnp.where(left, swapped, zero).astype(BF16), jnp.where(left, zero, f).astype(BF16))
    return (g0, g1)


def _band_mask(i, heads=1):
    row = lax.broadcasted_iota(jnp.int32, (heads * CHUNK, 2 * CHUNK), 0) & (CHUNK - 1)
    col = lax.broadcasted_iota(jnp.int32, (heads * CHUNK, 2 * CHUNK), 1)
    no_prev = jnp.where(i > 0, 0, 4 * CHUNK)
    in_prev = jnp.logical_and(col < CHUNK, (col - row) > no_prev)
    in_cur = jnp.logical_and(col >= CHUNK, (col - CHUNK) <= row)
    return jnp.logical_or(in_prev, in_cur)


def _causal_mask():
    row = lax.broadcasted_iota(jnp.int32, (CHUNK, CHUNK), 0)
    col = lax.broadcasted_iota(jnp.int32, (CHUNK, CHUNK), 1)
    return col <= row


def _store_spatial_weights(w_ref, wcat_ref, wcat_t_ref=None):
    causal = _causal_mask()
    for p in range(D_GMLP // LANES):
        wl = jnp.where(causal, w_ref[2 * p], 0.0)
        wr = jnp.where(causal, w_ref[2 * p + 1], 0.0)
        wcat_ref[p] = jnp.concatenate([wl, wr], axis=1).astype(BF16)
        if wcat_t_ref is not None:
            wcat_t_ref[p] = jnp.concatenate([wl.T, wr.T], axis=1).astype(BF16)


def _pair_stack(xp, left):
    return jnp.concatenate([jnp.where(left, xp, 0.0), jnp.where(left, 0.0, xp)], axis=0).astype(BF16)


def _mixer_fwd(u, vg, q, k, va, v_ln_g, v_ln_b, w_spatial, bias_full, sinks, dep=None):
    t = u.shape[0]

    def body(u_ref, vg_ref, q_ref, kc_ref, kp_ref, vc_ref, vp_ref, g_ref, b_ref, w_ref, bias_ref, sink_ref, cat_ref, wcat):
        i = pl.program_id(0)
        left = _half_lane_masks(CHUNK)

        @pl.when(i == 0)
        def _():
            _store_spatial_weights(w_ref, wcat)

        heads = range(N_HEADS)
        pair_cols = [slice(p * LANES, (p + 1) * LANES) for p in range(D_GMLP // LANES)]
        sinks_h = [sink_ref[h] for h in heads]
        for c in range(CHUNKS_PER_STEP):
            rows = slice(c * CHUNK, (c + 1) * CHUNK)
            before = slice((c - 1) * CHUNK, c * CHUNK)
            k_prev = kp_ref[...] if c == 0 else kc_ref[before, :]
            v_prev = vp_ref[...] if c == 0 else vc_ref[before, :]
            k_var = _kv_variants(jnp.concatenate([k_prev, kc_ref[rows, :]], axis=0))
            v_var = _kv_variants(jnp.concatenate([v_prev, vc_ref[rows, :]], axis=0))
            scores = [_dot(q_ref[rows, pair_cols[h // 2]], k_var[h // 4][h % 2], NT) for h in heads]

            ug = _gelu(u_ref[rows, :])
            xhat, _ = _layer_norm_stats(_gelu(vg_ref[rows, :]))
            vgl = xhat * g_ref[...] + b_ref[...]
            mixed = [_dot(wcat[p], _pair_stack(vgl[:, cols], left)) for p, cols in enumerate(pair_cols)]

            valid = _band_mask(CHUNKS_PER_STEP * i + c)
            masked = [jnp.where(valid, scores[h] * SCALE, NEG_INF) for h in heads]
            maxes = [jnp.maximum(jnp.max(masked[h], axis=1, keepdims=True), sinks_h[h]) for h in heads]
            exps = [jnp.exp(masked[h] - maxes[h]) for h in heads]
            invs = [1.0 / (jnp.sum(exps[h], axis=1, keepdims=True) + jnp.exp(sinks_h[h] - maxes[h])) for h in heads]
            probs = [(exps[h] * invs[h]).astype(BF16) for h in heads]
            for p, cols in enumerate(pair_cols):
                cat_ref[rows, cols] = (ug[:, cols] * (mixed[p] + bias_ref[:, cols])).astype(BF16)
            for p in range(D_ATTN // LANES):
                out = _dot(probs[2 * p], v_var[p // 2][0]) + _dot(probs[2 * p + 1], v_var[p // 2][1])
                cat_ref[rows, D_GMLP + p * LANES : D_GMLP + (p + 1) * LANES] = out.astype(BF16)

    in_specs = _chunk_specs() + [
        _const_spec((1, D_GMLP)),
        _const_spec((1, D_GMLP)),
        _const_spec((N_HEADS, CHUNK, CHUNK)),
        _const_spec((CHUNK, D_GMLP)),
        pl.BlockSpec(memory_space=pltpu.SMEM),
    ]
    body, in_specs, operands = _after(dep, body, in_specs, [u, vg, q, k, k, va, va, v_ln_g, v_ln_b, w_spatial, bias_full, sinks])
    return pl.pallas_call(
        body,
        name="mixer_fwd",
        grid=(t // (CHUNKS_PER_STEP * CHUNK),),
        in_specs=in_specs,
        out_specs=pl.BlockSpec((CHUNKS_PER_STEP * CHUNK, D_MODEL), lambda i: (i, 0)),
        out_shape=jax.ShapeDtypeStruct((t, D_MODEL), BF16),
        scratch_shapes=[pltpu.VMEM((D_GMLP // LANES, CHUNK, 2 * CHUNK), BF16)],
        compiler_params=_params(("arbitrary",)),
    )(*operands)


def _ffn_fwd_loss(cat, x, w_out, ln1_g, ln1_b, w1, w2, ln2_g, ln2_b, target):
    t = x.shape[0]

    def body(cat_ref, x_ref, wo_ref, g1_ref, b1_ref, w1_ref, w2_ref, g2_ref, b2_ref, tgt_ref,
             xh_ref, rstd_ref, x1b_ref, r_ref, dz2_ref, dz2b_ref, dg2_ref, db2_ref, sq_ref):
        @pl.when(pl.program_id(0) == 0)
        def _():
            dg2_ref[...] = jnp.zeros_like(dg2_ref)
            db2_ref[...] = jnp.zeros_like(db2_ref)
            sq_ref[...] = jnp.zeros_like(sq_ref)

        parts = [slice(p * FFN_PART, (p + 1) * FFN_PART) for p in range(TM_FFN_FWD // FFN_PART)]

        def norm1(rows, z):
            xhat1, rstd1 = _layer_norm_stats(z)
            xh_ref[rows, :] = xhat1
            rstd_ref[rows, :] = rstd1
            x1 = xhat1 * g1_ref[...] + b1_ref[...]
            x1b = x1.astype(BF16)
            x1b_ref[rows, :] = x1b
            return x1, x1b

        def feed_forward(rows, x1b, pre):
            ff = None
            for j in range(N_FF_BLOCKS):
                r = jnp.maximum(pre, 0.0)
                r_ref[rows, j * D_MODEL : (j + 1) * D_MODEL] = r.astype(BF16)
                part = _dot((r * r).astype(BF16), w2_ref[j])
                ff = part if ff is None else ff + part
                if j + 1 < N_FF_BLOCKS:
                    pre = _dot(x1b, w1_ref[j + 1])
            return ff

        def norm2_and_loss(rows, x1, ff):
            xhat2, rstd2 = _layer_norm_stats(ALPHA * x1 + ff)
            err = xhat2 * g2_ref[...] + b2_ref[...] - tgt_ref[rows, :]
            sq_ref[...] += jnp.sum(err * err, axis=0, keepdims=True)
            dy = err * (1.0 / D_MODEL)
            dg2_ref[...] += jnp.sum(dy * xhat2, axis=0, keepdims=True)
            db2_ref[...] += jnp.sum(dy, axis=0, keepdims=True)
            dz2 = _layer_norm_bwd(dy * g2_ref[...], xhat2, rstd2)
            dz2_ref[rows, :] = dz2
            dz2b_ref[rows, :] = dz2.astype(BF16)

        projected = _dot(cat_ref[parts[0], :], wo_ref[...])
        last = None
        for i, rows in enumerate(parts):
            z = ALPHA * x_ref[rows, :] + projected
            if i + 1 < len(parts):
                projected = _dot(cat_ref[parts[i + 1], :], wo_ref[...])
            x1, x1b = norm1(rows, z)
            pre = _dot(x1b, w1_ref[0])
            if last is not None:
                norm2_and_loss(*last)
            last = (rows, x1, feed_forward(rows, x1b, pre))
        norm2_and_loss(*last)

    vec = _const_spec((1, D_MODEL))
    tile = _row_spec(TM_FFN_FWD, D_MODEL)
    wspec = _const_spec((N_FF_BLOCKS, D_MODEL, D_MODEL), single_buffer=True)
    return pl.pallas_call(
        body,
        name="ffn_fwd_loss",
        grid=(t // TM_FFN_FWD,),
        in_specs=[tile, tile, _const_spec((D_MODEL, D_MODEL), single_buffer=True), vec, vec, wspec, wspec, vec, vec, tile],
        out_specs=[tile, _row_spec(TM_FFN_FWD, 1), tile, _row_spec(TM_FFN_FWD, D_FF), tile, tile, vec, vec, vec],
        out_shape=[
            jax.ShapeDtypeStruct((t, D_MODEL), F32),
            jax.ShapeDtypeStruct((t, 1), F32),
            jax.ShapeDtypeStruct((t, D_MODEL), BF16),
            jax.ShapeDtypeStruct((t, D_FF), BF16),
            jax.ShapeDtypeStruct((t, D_MODEL), F32),
            jax.ShapeDtypeStruct((t, D_MODEL), BF16),
            jax.ShapeDtypeStruct((1, D_MODEL), F32),
            jax.ShapeDtypeStruct((1, D_MODEL), F32),
            jax.ShapeDtypeStruct((1, D_MODEL), F32),
        ],
        compiler_params=_params(("arbitrary",)),
    )(cat, x, w_out, ln1_g, ln1_b, w1, w2, ln2_g, ln2_b, target)


def _ffn_bwd_ln1(dz2, r, x1b, xhat1, rstd1, ln1_g, w1, w2, w_out, dep=None):
    t = dz2.shape[0]

    def body(dz2_ref, r_ref, x1b_ref, xh_ref, rstd_ref, g1_ref, w1_ref, w2_ref, wo_ref, gw1_out, dz1_ref, dz1b_ref, dcat_ref, dg1_ref, db1_ref,
             gw1_ref, gw1_sems):
        last_step = pl.program_id(0) == t // TM_FFN - 1
        blocks_out = []

        @pl.when(pl.program_id(0) == 0)
        def _():
            dg1_ref[...] = jnp.zeros_like(dg1_ref)
            db1_ref[...] = jnp.zeros_like(db1_ref)
            gw1_ref[...] = jnp.zeros_like(gw1_ref)

        dz2 = dz2_ref[...]
        dz2b = dz2.astype(BF16)
        x1_t = x1b_ref[...].astype(F32).T.astype(BF16)
        dx1 = ALPHA * dz2
        for j in range(N_FF_BLOCKS):
            cols = slice(j * D_MODEL, (j + 1) * D_MODEL)
            dpre = (_dot(dz2b, w2_ref[j], NT) * (2.0 * r_ref[:, cols].astype(F32))).astype(BF16)
            gw1_ref[cols, :] += _dot(x1_t, dpre)
            block_out = pltpu.make_async_copy(_rows(gw1_ref, j * D_MODEL, D_MODEL), _rows(gw1_out, j * D_MODEL, D_MODEL), gw1_sems.at[j])
            pl.when(last_step)(block_out.start)
            blocks_out.append(block_out)
            dx1 = dx1 + _dot(dpre, w1_ref[j], NT)
        xhat1 = xh_ref[...]
        dg1_ref[...] += jnp.sum(dx1 * xhat1, axis=0, keepdims=True)
        db1_ref[...] += jnp.sum(dx1, axis=0, keepdims=True)
        dz1 = _layer_norm_bwd(dx1 * g1_ref[...], xhat1, rstd_ref[...])
        dz1_ref[...] = dz1
        dz1b = dz1.astype(BF16)
        dz1b_ref[...] = dz1b
        dcat_ref[...] = _dot(dz1b, wo_ref[...], NT).astype(BF16)

        @pl.when(last_step)
        def _():
            for block_out in blocks_out:
                block_out.wait()

    vec = _const_spec((1, D_MODEL))
    tile = _row_spec(TM_FFN, D_MODEL)
    wspec = _const_spec((N_FF_BLOCKS, D_MODEL, D_MODEL), single_buffer=True)
    body, in_specs, operands = _after(
        dep, body,
        [tile, _row_spec(TM_FFN, D_FF), tile, tile, _row_spec(TM_FFN, 1), vec, wspec, wspec, _const_spec((D_MODEL, D_MODEL), single_buffer=True)],
        [dz2, r, x1b, xhat1, rstd1, ln1_g, w1, w2, w_out])
    return pl.pallas_call(
        body,
        name="ffn_bwd_ln1",
        grid=(t // TM_FFN,),
        in_specs=in_specs,
        out_specs=[pl.BlockSpec(memory_space=pl.ANY), tile, tile, tile, vec, vec],
        scratch_shapes=[pltpu.VMEM((D_FF, D_MODEL), F32), pltpu.SemaphoreType.DMA((N_FF_BLOCKS,))],
        out_shape=[
            jax.ShapeDtypeStruct((D_FF, D_MODEL), F32),
            jax.ShapeDtypeStruct((t, D_MODEL), F32),
            jax.ShapeDtypeStruct((t, D_MODEL), BF16),
            jax.ShapeDtypeStruct((t, D_MODEL), BF16),
            jax.ShapeDtypeStruct((1, D_MODEL), F32),
            jax.ShapeDtypeStruct((1, D_MODEL), F32),
        ],
        compiler_params=_params(("arbitrary",)),
    )(*operands)


def _mixer_bwd(u, vg, q, k, va, dcat, cos, sin, v_ln_g, v_ln_b, w_spatial, bias_full, sinks, r, dz2b, dep=None):
    t = u.shape[0]
    n_chunks = t // CHUNK
    assert CHUNKS_PER_STEP == N_FF_BLOCKS

    def body(u_ref, vg_ref, q_ref, kc_ref, kp_ref, vc_ref, vp_ref, dcat_ref, cosc_ref, sinc_ref, cosp_ref, sinp_ref,
             g_ref, b_ref, w_ref, bias_ref, sink_ref, r_ref, dz2b_ref,
             dmain_ref, dkv_ref, gw2_out, dg_ref, db_ref, dw_ref, dbs_ref, dsink_ref, dmix_acc, wcat, wcat_t, gw2_ref, gw2_sems):
        i = pl.program_id(0)
        left = _half_lane_masks(CHUNK)
        lane = lax.broadcasted_iota(jnp.int32, (CHUNK, LANES), 1)
        n_pairs = D_GMLP // LANES

        @pl.when(i == 0)
        def _():
            dg_ref[...] = jnp.zeros_like(dg_ref)
            db_ref[...] = jnp.zeros_like(db_ref)
            dw_ref[...] = jnp.zeros_like(dw_ref)
            dsink_ref[...] = jnp.zeros_like(dsink_ref)
            dmix_acc[...] = jnp.zeros_like(dmix_acc)
            gw2_ref[...] = jnp.zeros_like(gw2_ref)
            _store_spatial_weights(w_ref, wcat, wcat_t)

        n_qpairs = D_ATTN // LANES
        heads = range(N_HEADS)
        pair_cols = [slice(p * LANES, (p + 1) * LANES) for p in range(n_pairs)]
        sinks_h = [sink_ref[h] for h in heads]
        gain = g_ref[...]
        causal = _causal_mask()
        lane_row = lax.broadcasted_iota(jnp.int32, (1, LANES), 1)
        heads_per_group = N_HEADS // 2

        def group_grad_t(lhs_t, rhs_heads):
            parts = []
            for g in range(2):
                group = range(g * heads_per_group, (g + 1) * heads_per_group)
                lhs = jnp.concatenate([lhs_t[h * HEAD_DIM : (h + 1) * HEAD_DIM] for h in group], axis=1)
                parts.append(_dot(lhs, jnp.concatenate([rhs_heads[h] for h in group], axis=0)))
            return jnp.concatenate(parts, axis=0)

        blocks_out = []
        for c in range(CHUNKS_PER_STEP):
            chunk = CHUNKS_PER_STEP * i + c
            rows = slice(c * CHUNK, (c + 1) * CHUNK)
            before = slice((c - 1) * CHUNK, c * CHUNK)

            k_prev = kp_ref[...] if c == 0 else kc_ref[before, :]
            v_prev = vp_ref[...] if c == 0 else vc_ref[before, :]
            k_var = _kv_variants(jnp.concatenate([k_prev, kc_ref[rows, :]], axis=0))
            v_var = _kv_variants(jnp.concatenate([v_prev, vc_ref[rows, :]], axis=0))
            q_pairs = [q_ref[rows, cols] for cols in pair_cols]
            do_all = dcat_ref[rows, D_GMLP:D_MODEL]
            do_pairs = [do_all[:, cols] for cols in pair_cols]
            scores = [_dot(q_pairs[h // 2], k_var[h // 4][h % 2], NT) for h in heads]
            dprobs = [_dot(do_pairs[h // 2], v_var[h // 4][h % 2], NT) for h in heads]
            q_t = q_ref[rows, :].astype(F32).T.astype(BF16)
            do_t = do_all.astype(F32).T.astype(BF16)

            ff_cols = slice(c * D_MODEL, (c + 1) * D_MODEL)
            relu_block = r_ref[:, ff_cols]
            gw2_ref[ff_cols, :] += _dot(relu_block * relu_block, dz2b_ref[...], TN)
            block_out = pltpu.make_async_copy(_rows(gw2_ref, c * D_MODEL, D_MODEL), _rows(gw2_out, c * D_MODEL, D_MODEL), gw2_sems.at[c])
            pl.when(i == n_chunks // CHUNKS_PER_STEP - 1)(block_out.start)
            blocks_out.append(block_out)

            ug, dug_du = _gelu_and_grad(u_ref[rows, :])
            gv, dgv_dv = _gelu_and_grad(vg_ref[rows, :])
            xhat, rstd = _layer_norm_stats(gv)
            vgl = xhat * gain + b_ref[...]
            mixed = [_dot(wcat[p], _pair_stack(vgl[:, cols], left)) for p, cols in enumerate(pair_cols)]

            valid = _band_mask(chunk)
            masked = [jnp.where(valid, scores[h] * SCALE, NEG_INF) for h in heads]
            maxes = [jnp.maximum(jnp.max(masked[h], axis=1, keepdims=True), sinks_h[h]) for h in heads]
            exps = [jnp.exp(masked[h] - maxes[h]) for h in heads]
            exp_sinks = [jnp.exp(sinks_h[h] - maxes[h]) for h in heads]
            invs = [1.0 / (jnp.sum(exps[h], axis=1, keepdims=True) + exp_sinks[h]) for h in heads]
            probs = [exps[h] * invs[h] for h in heads]
            dsums = [jnp.sum(probs[h] * dprobs[h], axis=1, keepdims=True) for h in heads]
            ds_b = [(probs[h] * (dprobs[h] - dsums[h]) * SCALE).astype(BF16) for h in heads]
            probs_b = [probs[h].astype(BF16) for h in heads]

            dm_stacks = []
            for p, cols in enumerate(pair_cols):
                da = dcat_ref[rows, cols].astype(F32)
                dmain_ref[rows, cols] = (da * (mixed[p] + bias_ref[:, cols]) * dug_du[:, cols]).astype(BF16)
                dmixed = da * ug[:, cols]
                dmix_acc[:, cols] += dmixed
                dm_stacks.append(_pair_stack(dmixed, left))

            dq_all = jnp.concatenate(
                [_dot(ds_b[2 * p], k_var[p // 2][0]) + _dot(ds_b[2 * p + 1], k_var[p // 2][1]) for p in range(n_qpairs)], axis=1)
            dk2_t = group_grad_t(q_t, ds_b)
            dv2_t = group_grad_t(do_t, probs_b)

            for p, cols in enumerate(pair_cols):
                dw_pair = _dot(dm_stacks[p], vgl[:, cols].astype(BF16), NT)
                dw_ref[2 * p] += jnp.where(causal, dw_pair[:CHUNK], 0.0)
                dw_ref[2 * p + 1] += jnp.where(causal, dw_pair[CHUNK:], 0.0)
            dvgl = jnp.concatenate([_dot(wcat_t[p], dm_stacks[p]) for p in range(n_pairs)], axis=1)

            dsink_row = jnp.zeros((1, LANES), F32)
            for h in heads:
                d_sink = -jnp.sum(exp_sinks[h] * invs[h] * dsums[h], axis=0, keepdims=True)
                dsink_row = dsink_row + jnp.where(lane_row == h, d_sink, 0.0)
            dsink_ref[0:1, :] += dsink_row
            cos_c, sin_c = cosc_ref[rows, :], sinc_ref[rows, :]
            cos_p = cosp_ref[...] if c == 0 else cosc_ref[before, :]
            sin_p = sinp_ref[...] if c == 0 else sinc_ref[before, :]
            dmain_ref[rows, 2 * D_GMLP : D_MAIN] = _rope_transposed(dq_all, _lane_tile(cos_c, n_qpairs), _lane_tile(sin_c, n_qpairs)).astype(BF16)
            dk2 = dk2_t.T
            dv2 = dv2_t.T
            cur = pl.ds(pl.multiple_of(chunk * CHUNK, CHUNK), CHUNK)
            dkv_ref[cur, 0:D_KV] = _rope_transposed(dk2[CHUNK:], cos_c, sin_c)
            dkv_ref[cur, D_KV : 2 * D_KV] = dv2[CHUNK:]
            prev = pl.ds(pl.multiple_of(jnp.maximum(chunk - 1, 0) * CHUNK, CHUNK), CHUNK)
            dkv_ref[prev, 0:D_KV] += _rope_transposed(dk2[:CHUNK], cos_p, sin_p)
            dkv_ref[prev, D_KV : 2 * D_KV] += dv2[:CHUNK]

            dg_ref[...] += jnp.sum(dvgl * xhat, axis=0, keepdims=True)
            db_ref[...] += jnp.sum(dvgl, axis=0, keepdims=True)
            dgv = _layer_norm_bwd(dvgl * gain, xhat, rstd)
            dmain_ref[rows, D_GMLP : 2 * D_GMLP] = (dgv * dgv_dv).astype(BF16)

        @pl.when(i == n_chunks // CHUNKS_PER_STEP - 1)
        def _():
            tile = jnp.zeros((CHUNK, LANES), F32)
            for p, cols in enumerate(pair_cols):
                dm = dmix_acc[:, cols]
                sl = jnp.sum(jnp.where(left, dm, 0.0), axis=1, keepdims=True)
                sr = jnp.sum(jnp.where(left, 0.0, dm), axis=1, keepdims=True)
                tile = jnp.where(lane == 2 * p, sl, tile)
                tile = jnp.where(lane == 2 * p + 1, sr, tile)
            dbs_ref[...] = tile
            for block_out in blocks_out:
                block_out.wait()

    step = CHUNKS_PER_STEP * CHUNK
    in_specs = _chunk_specs() + [
        pl.BlockSpec((step, D_MODEL), _step_rows),
        pl.BlockSpec((step, LANES), _step_rows),
        pl.BlockSpec((step, LANES), _step_rows),
        pl.BlockSpec((CHUNK, LANES), _chunk_before_step),
        pl.BlockSpec((CHUNK, LANES), _chunk_before_step),
        _const_spec((1, D_GMLP)),
        _const_spec((1, D_GMLP)),
        _const_spec((N_HEADS, CHUNK, CHUNK)),
        _const_spec((CHUNK, D_GMLP)),
        pl.BlockSpec(memory_space=pltpu.SMEM),
        pl.BlockSpec((step, D_FF), _step_rows),
        pl.BlockSpec((step, D_MODEL), _step_rows),
    ]
    body, in_specs, operands = _after(
        dep, body, in_specs, [u, vg, q, k, k, va, va, dcat, cos, sin, cos, sin, v_ln_g, v_ln_b, w_spatial, bias_full, sinks, r, dz2b])
    return pl.pallas_call(
        body,
        name="mixer_bwd",
        grid=(n_chunks // CHUNKS_PER_STEP,),
        in_specs=in_specs,
        out_specs=[
            pl.BlockSpec((step, D_MAIN), _step_rows),
            _const_spec((t, 2 * D_KV)),
            pl.BlockSpec(memory_space=pl.ANY),
            _const_spec((1, D_GMLP)),
            _const_spec((1, D_GMLP)),
            _const_spec((N_HEADS, CHUNK, CHUNK)),
            _const_spec((CHUNK, LANES)),
            _const_spec((8, LANES)),
        ],
        out_shape=[
            jax.ShapeDtypeStruct((t, D_MAIN), BF16),
            jax.ShapeDtypeStruct((t, 2 * D_KV), F32),
            jax.ShapeDtypeStruct((D_FF, D_MODEL), F32),
            jax.ShapeDtypeStruct((1, D_GMLP), F32),
            jax.ShapeDtypeStruct((1, D_GMLP), F32),
            jax.ShapeDtypeStruct((N_HEADS, CHUNK, CHUNK), F32),
            jax.ShapeDtypeStruct((CHUNK, LANES), F32),
            jax.ShapeDtypeStruct((8, LANES), F32),
        ],
        scratch_shapes=[
            pltpu.VMEM((CHUNK, D_GMLP), F32),
            pltpu.VMEM((D_GMLP // LANES, CHUNK, 2 * CHUNK), BF16),
            pltpu.VMEM((D_GMLP // LANES, CHUNK, 2 * CHUNK), BF16),
            pltpu.VMEM((D_FF, D_MODEL), F32),
            pltpu.SemaphoreType.DMA((CHUNKS_PER_STEP,)),
        ],
        compiler_params=_params(("arbitrary",)),
    )(*operands)


def _grad_x(dh_main, dkv, dz1, w_in_t, dep=None):
    t = dz1.shape[0]

    def body(dm_ref, dkv_ref, dz1_ref, w_ref, gx_ref):
        acc = ALPHA * dz1_ref[...] + _dot(dm_ref[...], w_ref[0:D_MAIN, :])
        gx_ref[...] = acc + _dot(dkv_ref[...].astype(BF16), w_ref[D_MAIN:D_IN, :])

    body, in_specs, operands = _after(
        dep, body, [_row_spec(TM, D_MAIN), _row_spec(TM, 2 * D_KV), _row_spec(TM, D_MODEL), _const_spec((D_IN, D_MODEL))], [dh_main, dkv, dz1, w_in_t])
    return pl.pallas_call(
        body,
        name="grad_x",
        grid=(t // TM,),
        in_specs=in_specs,
        out_specs=_row_spec(TM, D_MODEL),
        out_shape=jax.ShapeDtypeStruct((t, D_MODEL), F32),
        compiler_params=_params(("parallel",)),
    )(*operands)


def _token_contraction(name, out_rows, tk, in_arrays, contributions, dep=None):
    t = in_arrays[0].shape[0]

    def body(*refs):
        out_ref = refs[-1]

        @pl.when(pl.program_id(0) == 0)
        def _():
            out_ref[...] = jnp.zeros_like(out_ref)

        for row0, a, b in contributions(*refs[:-1]):
            out_ref[row0 : row0 + a.shape[1], :] += _dot(a, b, TN)

    in_specs = [_row_spec(tk, a.shape[1]) for a in in_arrays]
    body, in_specs, operands = _after(dep, body, in_specs, in_arrays)
    return pl.pallas_call(
        body,
        name=name,
        grid=(t // tk,),
        in_specs=in_specs,
        out_specs=_const_spec((out_rows, D_MODEL), single_buffer=True),
        out_shape=jax.ShapeDtypeStruct((out_rows, D_MODEL), F32),
        compiler_params=_params(("arbitrary",)),
    )(*operands)


def _grad_w_out(cat, dz1b, dep=None):
    def contributions(cat_ref, dz1_ref):
        return [(0, cat_ref[...], dz1_ref[...])]

    return _token_contraction("grad_w_out", D_MODEL, TK, [cat, dz1b], contributions, dep)


ANY = pl.BlockSpec(memory_space=pl.ANY)


def _mesh_position():
    return lax.axis_index("x"), lax.axis_index("y"), lax.axis_index("c")


def _other_chips(x, y):
    return [(1 - x, y), (x, 1 - y), (1 - x, 1 - y)]


def _remote(src, dst, send_sem, recv_sem, device):
    return pltpu.make_async_remote_copy(src_ref=src, dst_ref=dst, send_sem=send_sem, recv_sem=recv_sem, device_id=device, device_id_type=MESH)


def _rows(ref, start, size):
    return ref.at[pl.ds(start, size), :]


def _rope_tables_and_casts(pos_row, inv_freq_row, shards, dep=None):
    t = pos_row.shape[1]
    steps = t // TM
    n = len(shards)

    def body(pos_ref, f_ref, *rest):
        f32_refs, (cos_ref, sin_ref), bf16_refs = rest[:n], rest[n : n + 2], rest[n + 2 :]
        for src, dst in zip(f32_refs, bf16_refs):
            dst[...] = src[...].astype(BF16)
        pos_rows = jnp.broadcast_to(pos_ref[...].astype(F32), (LANES, TM)).T
        ang = pos_rows * f_ref[...]
        cos_ref[...] = jnp.cos(ang)
        sin_ref[...] = jnp.sin(ang)

    shard_specs = [_row_spec(s.shape[0] // steps, s.shape[1]) for s in shards]
    body, in_specs, operands = _after(
        dep, body, [pl.BlockSpec((1, TM), lambda i: (0, i)), _const_spec((1, LANES))] + shard_specs, [pos_row, inv_freq_row, *shards])
    outs = pl.pallas_call(
        body,
        name="rope_tables_and_casts",
        grid=(steps,),
        in_specs=in_specs,
        out_specs=[_row_spec(TM, LANES), _row_spec(TM, LANES)] + shard_specs,
        out_shape=[jax.ShapeDtypeStruct((t, LANES), F32)] * 2 + [jax.ShapeDtypeStruct(s.shape, BF16) for s in shards],
        compiler_params=_params(("parallel",)),
    )(*operands)
    return outs[0], outs[1], list(outs[2:])


def _grad_w_in_t_and_small_all_reduce(dh_main, dkv, x, slab, dep=None):
    t = x.shape[0]
    steps = t // TK
    rows = slab.shape[0]
    part = rows // 8

    def body(dm_ref, dkv_ref, x_ref, slab_ref, grad_ref, sum_ref, landing, reduced, gathered, send_sems, recv_sems):
        k = pl.program_id(0)
        x_, y_, c_ = _mesh_position()
        me = 4 * x_ + 2 * y_ + c_
        flips = [(f >> 2, (f >> 1) & 1, f & 1) for f in range(1, 8)]

        def peer(flip):
            fx, fy, fc = flip
            return (1 - x_ if fx else x_, 1 - y_ if fy else y_, 1 - c_ if fc else c_)

        def part_of(ref, device):
            return ref.at[pl.ds(pl.multiple_of(device * part, 8), part), :]

        def scatter_copies():
            out = []
            for kk, flip in enumerate(flips):
                px, py, pc = peer(flip)
                them = 4 * px + 2 * py + pc
                send = _remote(part_of(slab_ref, them), landing.at[me], send_sems.at[kk], recv_sems.at[kk], (px, py, pc))
                recv = _remote(landing.at[them], landing.at[them], send_sems.at[kk], recv_sems.at[kk], (px, py, pc))
                out.append((send, recv))
            return out

        def gather_copies():
            out = []
            for kk, flip in enumerate(flips):
                px, py, pc = peer(flip)
                them = 4 * px + 2 * py + pc
                send = _remote(reduced, part_of(gathered, me), send_sems.at[7 + kk], recv_sems.at[7 + kk], (px, py, pc))
                recv = _remote(part_of(gathered, them), part_of(gathered, them), send_sems.at[7 + kk], recv_sems.at[7 + kk], (px, py, pc))
                out.append((send, recv))
            return out

        @pl.when(k == 0)
        def _():
            grad_ref[...] = jnp.zeros_like(grad_ref)
            for send, _ in scatter_copies():
                send.start()
            landing[me] = part_of(slab_ref, me)[...]

        @pl.when(k == steps // 2)
        def _():
            for _, recv in scatter_copies():
                recv.wait_recv()
            total = landing[0]
            for s in range(1, 8):
                total = total + landing[s]
            reduced[...] = total
            part_of(gathered, me)[...] = total
            for send, _ in gather_copies():
                send.start()

        xb = x_ref[...].astype(BF16)
        grad_ref[0:D_MAIN, :] += _dot(dm_ref[...], xb, TN)
        grad_ref[D_MAIN:D_IN, :] += _dot(dkv_ref[...].astype(BF16), xb, TN)

        @pl.when(k == steps - 1)
        def _():
            for send, recv in gather_copies():
                recv.wait_recv()
                send.wait_send()
            for send, _ in scatter_copies():
                send.wait_send()
            sum_ref[...] = gathered[...]

    body, in_specs, operands = _after(
        dep, body, [_row_spec(TK, D_MAIN), _row_spec(TK, 2 * D_KV), _row_spec(TK, D_MODEL), _const_spec(slab.shape)], [dh_main, dkv, x, slab])
    return pl.pallas_call(
        body,
        name="grad_w_in_and_small_all_reduce",
        grid=(steps,),
        in_specs=in_specs,
        out_specs=[_const_spec((D_IN, D_MODEL), single_buffer=True), _const_spec(slab.shape)],
        out_shape=[jax.ShapeDtypeStruct((D_IN, D_MODEL), F32), jax.ShapeDtypeStruct(slab.shape, slab.dtype)],
        scratch_shapes=[
            pltpu.VMEM((8, part, LANES), F32),
            pltpu.VMEM((part, LANES), F32),
            pltpu.VMEM(slab.shape, F32),
            pltpu.SemaphoreType.DMA((14,)),
            pltpu.SemaphoreType.DMA((14,)),
        ],
        compiler_params=_params(("arbitrary",)),
    )(*operands)


HBM = pl.BlockSpec(memory_space=pltpu.HBM)
SEM = pl.BlockSpec(memory_space=pltpu.SEMAPHORE)
DATAFLOW = pltpu.SideEffectType.DATAFLOW_SIDE_EFFECTING
TOKEN = jax.ShapeDtypeStruct((8, LANES), F32)


def _plan_copies(bufs, plan, send_sems, recv_sems):
    out = []
    for i, (src, src_row, dst, dst_row, recv_row, rows, device) in enumerate(plan):
        send = _remote(_rows(bufs[src], src_row, rows), _rows(bufs[dst], dst_row, rows), send_sems.at[i], recv_sems.at[i], device)
        landed = _rows(bufs[dst], recv_row, rows)
        recv = _remote(landed, landed, send_sems.at[i], recv_sems.at[i], device)
        out.append((send, recv))
    return out


def _split_call(name, bufs, wait=None, start=None, after=None):
    n = len(bufs)
    n_in = n + (2 if wait else 0) + (1 if after is not None else 0)
    n_start = len(start(0, 0, 0)) if start else 0

    def body(*refs):
        ins = refs[:n]
        x, y, c = _mesh_position()
        if wait:
            for send, recv in _plan_copies(ins, wait[0](x, y, c), refs[n], refs[n + 1]):
                recv.wait_recv()
                send.wait_send()
        if start:
            for send, _ in _plan_copies(ins, start(x, y, c), refs[n_in + n + 1], refs[n_in + n + 2]):
                send.start()
        token = refs[n_in + n]
        token[...] = jnp.zeros_like(token)

    operands = [pltpu.with_memory_space_constraint(b, pltpu.HBM) for b in bufs]
    in_specs = [HBM] * n
    if wait:
        operands += [wait[1], wait[2]]
        in_specs += [SEM, SEM]
    if after is not None:
        operands.append(after)
        in_specs.append(ANY)
    out_shape = [pltpu.HBM(b.shape, b.dtype) for b in bufs] + [TOKEN]
    out_specs = [HBM] * n + [pl.BlockSpec(memory_space=pltpu.VMEM)]
    if start:
        out_shape += [pltpu.SemaphoreType.DMA((n_start,)), pltpu.SemaphoreType.DMA((n_start,))]
        out_specs += [SEM, SEM]
    outs = pl.pallas_call(
        body,
        name=name,
        in_specs=in_specs,
        out_specs=out_specs,
        out_shape=out_shape,
        input_output_aliases={i: i for i in range(n)},
        compiler_params=pltpu.CompilerParams(has_side_effects=DATAFLOW),
    )(*operands)
    return (list(outs[:n]), outs[n]) + tuple(outs[n + 1 :])


def _direct_gather_plans(shard_rows):
    n = len(shard_rows)

    def direct(x, y, c):
        me = 2 * x + y
        plan = []
        for w, rows in enumerate(shard_rows):
            half = rows // 2
            for px, py in _other_chips(x, y):
                plan.append((w, c * half, n + w, me * rows + c * half, (2 * px + py) * rows + c * half, half, (px, py, c)))
            plan.append((w, 0, n + w, me * rows, me * rows, rows, (x, y, 1 - c)))
        return plan

    def passed_on(x, y, c):
        plan = []
        for w, rows in enumerate(shard_rows):
            half = rows // 2
            for px, py in _other_chips(x, y):
                row = (2 * px + py) * rows
                plan.append((n + w, row + c * half, n + w, row + c * half, row + (1 - c) * half, half, (x, y, 1 - c)))
        return plan

    return direct, passed_on


def _gather_plans(shard_rows):
    n = len(shard_rows)

    def neighbours(x, y):
        return ((1 - x, y), (x, 1 - y))

    def direct(x, y, c):
        me = 2 * x + y
        plan = []
        for w, rows in enumerate(shard_rows):
            half = rows // 2
            for px, py in neighbours(x, y):
                plan.append((w, c * half, n + w, me * rows + c * half, (2 * px + py) * rows + c * half, half, (px, py, c)))
            plan.append((w, 0, n + w, me * rows, me * rows, rows, (x, y, 1 - c)))
        return plan

    def passed_on(x, y, c):
        (xn, yn), diagonal = neighbours(x, y), 2 * (1 - x) + (1 - y)
        relayed = (1 - c) * (2 * xn[0] + xn[1]) + c * (2 * yn[0] + yn[1])
        target = (x * (1 - c) + (1 - x) * c, (1 - y) * (1 - c) + y * c, c)
        plan = []
        for w, rows in enumerate(shard_rows):
            half = rows // 2
            for px, py in (xn, yn):
                row = (2 * px + py) * rows
                plan.append((n + w, row + c * half, n + w, row + c * half, row + (1 - c) * half, half, (x, y, 1 - c)))
            plan.append((n + w, relayed * rows + c * half, n + w, relayed * rows + c * half, diagonal * rows + c * half, half, target))
        return plan

    def diagonal_passed_on(x, y, c):
        plan = []
        for w, rows in enumerate(shard_rows):
            half = rows // 2
            row = (2 * (1 - x) + (1 - y)) * rows
            plan.append((n + w, row + c * half, n + w, row + c * half, row + (1 - c) * half, half, (x, y, 1 - c)))
        return plan

    return direct, passed_on, diagonal_passed_on


def _swap_plan(block_rows):
    n = len(block_rows)

    def plan_fn(x, y, c):
        plan = []
        for w, rows in enumerate(block_rows):
            half = rows // 2
            for j in range(N_CHIPS):
                plan.append((w, j * rows + (1 - c) * half, n + w, j * half, j * half, half, (x, y, 1 - c)))
        return plan

    return plan_fn


def _exchange_plan(halves):
    n = len(halves)

    def plan_fn(x, y, c):
        plan = []
        for w, half in enumerate(halves):
            for kk, (px, py) in enumerate(_other_chips(x, y)):
                plan.append((w, (2 * px + py) * half, n + w, kk * half, kk * half, half, (px, py, c)))
        return plan

    return plan_fn


def _sibling_plan(shard_rows):
    def plan_fn(x, y, c):
        return [(w, c * (rows // 2), w, c * (rows // 2), (1 - c) * (rows // 2), rows // 2, (x, y, 1 - c)) for w, rows in enumerate(shard_rows)]

    return plan_fn


def _shifted(plan_fn, first):
    return lambda x, y, c: [(src + first, a, dst + first, b, r, n, dev) for src, a, dst, b, r, n, dev in plan_fn(x, y, c)]


def _landing(rows, cols, dtype):
    return lax.empty((rows, cols), dtype)


def _row_tile(rows, cap=512):
    best = 8
    for cand in range(8, cap + 1, 8):
        if rows % cand == 0:
            best = cand
    return best


def _pair_sum(name, grad, theirs, pos):
    half = theirs.shape[0] // N_CHIPS
    cols = theirs.shape[1]
    tile = _row_tile(half)
    steps = half // tile

    def body(pos_ref, g_ref, t_ref, p_ref, own_ref):
        total = g_ref[...] + t_ref[...]
        p_ref[...] = total.astype(BF16)

        @pl.when(pl.program_id(1) == pos_ref[1])
        def _():
            own_ref[...] = total

    return pl.pallas_call(
        body,
        name=name,
        grid_spec=pltpu.PrefetchScalarGridSpec(
            num_scalar_prefetch=1,
            grid=(steps, N_CHIPS),
            in_specs=[
                pl.BlockSpec((tile, cols), lambda i, j, pos: ((2 * j + pos[0]) * steps + i, 0)),
                pl.BlockSpec((tile, cols), lambda i, j, pos: (j * steps + i, 0)),
            ],
            out_specs=[
                pl.BlockSpec((tile, cols), lambda i, j, pos: (j * steps + i, 0)),
                pl.BlockSpec((tile, cols), lambda i, j, pos: (i, 0)),
            ],
        ),
        out_shape=[jax.ShapeDtypeStruct((N_CHIPS * half, cols), BF16), jax.ShapeDtypeStruct((half, cols), F32)],
        compiler_params=_params(("parallel", "arbitrary")),
    )(pos, grad, theirs)


def _adamw_update(w, g, m, v):
    nm = ADAM_B1 * m + (1.0 - ADAM_B1) * g
    nv = ADAM_B2 * v + (1.0 - ADAM_B2) * (g * g)
    m_hat = nm / (1.0 - ADAM_B1**ADAM_STEP)
    v_hat = nv / (1.0 - ADAM_B2**ADAM_STEP)
    return -ADAM_LR * (m_hat / (jnp.sqrt(v_hat) + ADAM_EPS) + ADAM_WD * w), nm, nv


def _chip_sum(name, own, landed, pos):
    half, cols = own.shape
    tile = _row_tile(half)
    steps = half // tile

    def body(pos_ref, own_ref, l0, l1, l2, o_ref):
        o_ref[...] = ((own_ref[...] + l0[...].astype(F32)) + l1[...].astype(F32)) + l2[...].astype(F32)

    landed_specs = [pl.BlockSpec((tile, cols), lambda i, pos, _k=k: (_k * steps + i, 0)) for k in range(N_CHIPS - 1)]
    return pl.pallas_call(
        body,
        name=name,
        grid_spec=pltpu.PrefetchScalarGridSpec(
            num_scalar_prefetch=1,
            grid=(steps,),
            in_specs=[pl.BlockSpec((tile, cols), lambda i, pos: (i, 0))] + landed_specs,
            out_specs=pl.BlockSpec((tile, cols), lambda i, pos: (pos[0] * steps + i, 0)),
        ),
        out_shape=jax.ShapeDtypeStruct((2 * half, cols), F32),
        compiler_params=_params(("parallel",)),
    )(pos, own, landed, landed, landed)


def _adamw(name, w, g, m, v):
    rows, cols = w.shape
    tile = rows if rows * cols <= 256 * 1024 else _row_tile(rows)

    def body(w_ref, g_ref, m_ref, v_ref, g_out_ref, d_ref, nm_ref, nv_ref):
        g = g_ref[...]
        g_out_ref[...] = g
        d_ref[...], nm_ref[...], nv_ref[...] = _adamw_update(w_ref[...], g, m_ref[...], v_ref[...])

    spec = _row_spec(tile, cols)
    return pl.pallas_call(
        body,
        name=name,
        grid=(rows // tile,),
        in_specs=[spec] * 4,
        out_specs=[spec] * 4,
        out_shape=[jax.ShapeDtypeStruct((rows, cols), F32)] * 4,
        compiler_params=_params(("parallel",)),
    )(w, g, m, v)


_SMALL = (
    ("v_ln_g", (D_GMLP,), 8),
    ("v_ln_b", (D_GMLP,), 8),
    ("w_spatial", (N_HEADS, CHUNK, CHUNK), 1024),
    ("b_spatial", (N_HEADS, CHUNK), 8),
    ("sinks", (N_HEADS,), 8),
    ("ln1_g", (D_MODEL,), 8),
    ("ln1_b", (D_MODEL,), 8),
    ("ln2_g", (D_MODEL,), 8),
    ("ln2_b", (D_MODEL,), 8),
    ("squared_error", (D_MODEL,), 8),
)
N_SMALL_PARAMS = len(_SMALL) - 1


def _pack_small(values):
    parts = []
    for (name, shape, rows), val in zip(_SMALL, values, strict=True):
        flat = val.reshape(-1).astype(F32)
        parts.append(jnp.pad(flat, (0, rows * LANES - flat.shape[0])).reshape(rows, LANES))
    parts.append(jnp.zeros((SMALL_ROWS - sum(rows for _, _, rows in _SMALL), LANES), F32))
    return jnp.concatenate(parts, axis=0)


def _adamw_small(g_slab, params, first, second):
    n = N_SMALL_PARAMS

    def pieces(shape):
        if len(shape) == 3:
            return [((0, h), h * shape[1], shape[1], shape[2]) for h in range(shape[0])]
        if len(shape) == 2:
            return [((0,), 0, shape[0], shape[1])]
        if shape[0] >= LANES:
            return [((slice(None), slice(r * LANES, (r + 1) * LANES)), r, 1, LANES) for r in range(shape[0] // LANES)]
        return [((slice(None), slice(0, shape[0])), 0, 1, shape[0])]

    def body(*refs):
        g_ref = refs[0]
        w_refs, m_refs, v_refs = refs[1 : 1 + n], refs[1 + n : 1 + 2 * n], refs[1 + 2 * n : 1 + 3 * n]
        outs = refs[1 + 3 * n :]
        row0 = 0
        for idx, (_, shape, rows) in enumerate(_SMALL[:n]):
            for where, first_row, n_rows, lanes in pieces(shape):
                g = g_ref[row0 + first_row : row0 + first_row + n_rows, 0:lanes]
                delta, nm, nv = _adamw_update(w_refs[idx][where], g, m_refs[idx][where], v_refs[idx][where])
                for group, val in enumerate((g, delta, nm, nv)):
                    outs[group * n + idx][where] = val
            row0 += rows

    vmem = pl.BlockSpec(memory_space=pltpu.VMEM)
    shapes = [jax.ShapeDtypeStruct(p.shape, F32) for p in params]
    outs = pl.pallas_call(
        body,
        name="adamw_small",
        in_specs=[vmem] * (1 + 3 * n),
        out_specs=[vmem] * (4 * n),
        out_shape=shapes * 4,
        compiler_params=_params(),
    )(g_slab, *params, *first, *second)
    return [list(outs[group * n : (group + 1) * n]) for group in range(4)]


def kernel(x, positions, w_in, v_ln_g, v_ln_b, w_spatial, b_spatial, sinks, w_out, ln1_g, ln1_b, w_ff1, w_ff2, ln2_g, ln2_b, loss_target, m_w_in, m_v_ln_g, m_v_ln_b, m_w_spatial, m_b_spatial, m_sinks, m_w_out, m_ln1_g, m_ln1_b, m_w_ff1, m_w_ff2, m_ln2_g, m_ln2_b, v_w_in, v_v_ln_g, v_v_ln_b, v_w_spatial, v_b_spatial, v_sinks, v_w_out, v_ln1_g, v_ln1_b, v_w_ff1, v_w_ff2, v_ln2_g, v_ln2_b):
    t = x.shape[1]
    x2 = x.reshape(t, D_MODEL)
    target = loss_target.reshape(t, D_MODEL)

    w_in_shard = w_in[0].T.astype(BF16)
    in_direct, in_pass = _direct_gather_plans([w_in_shard.shape[0]])
    in_bufs, in_started, in_send, in_recv = _split_call(
        "gather_w_in_start", [w_in_shard, _landing(N_CHIPS * w_in_shard.shape[0], D_MODEL, BF16)], start=in_direct)
    inv_freq = ROPE_THETA ** (-jnp.arange(0, HEAD_DIM, 2, dtype=F32) / HEAD_DIM)
    cos, sin, later = _rope_tables_and_casts(
        positions, jnp.tile(inv_freq, LANES // (HEAD_DIM // 2)).reshape(1, LANES), [w_out[0], w_ff1[0], w_ff2[0]], dep=in_started)
    later_rows = [s.shape[0] for s in later]
    direct_plan, pass_plan, diagonal_plan = _gather_plans(later_rows)
    bufs, started, direct_send, direct_recv = _split_call(
        "gather_start", later + [_landing(N_CHIPS * r, D_MODEL, BF16) for r in later_rows], start=direct_plan, after=cos)
    in_bufs, in_passing, in_pass_send, in_pass_recv = _split_call(
        "gather_w_in_pass", in_bufs, wait=(in_direct, in_send, in_recv), start=in_pass, after=started)
    in_bufs, _ = _split_call("gather_w_in_end", in_bufs, wait=(in_pass, in_pass_send, in_pass_recv), after=in_passing)
    w_in_t = in_bufs[1]

    u, vg, q, k, va = _in_proj(x2, w_in_t, cos, sin)
    bias_full = jnp.repeat(b_spatial[0].T, HEAD_DIM, axis=1)
    sink_vec = sinks.reshape(N_HEADS)
    bufs, passing, pass_send, pass_recv = _split_call(
        "gather_pass", bufs, wait=(direct_plan, direct_send, direct_recv), start=pass_plan, after=u)
    cat = _mixer_fwd(u, vg, q, k, va, v_ln_g, v_ln_b, w_spatial[0], bias_full, sink_vec, dep=passing)
    bufs, passing, diag_send, diag_recv = _split_call(
        "gather_pass_diagonal", bufs, wait=(pass_plan, pass_send, pass_recv), start=diagonal_plan, after=cat)
    bufs, _ = _split_call("gather_end", bufs, wait=(diagonal_plan, diag_send, diag_recv), after=passing)
    w_out_all = bufs[3]
    w1_all = bufs[4].reshape(N_FF_BLOCKS, D_MODEL, D_MODEL)
    w2_all = bufs[5].reshape(N_FF_BLOCKS, D_MODEL, D_MODEL)
    xhat1, rstd1, x1b, r, dz2, dz2b, d_ln2_g, d_ln2_b, sq_err = _ffn_fwd_loss(
        cat, x2, w_out_all, ln1_g, ln1_b, w1_all, w2_all, ln2_g, ln2_b, target)

    pos = jnp.stack([lax.axis_index("c"), 2 * lax.axis_index("x") + lax.axis_index("y")]).astype(jnp.int32)
    half_landing = lambda g: _landing(g.shape[0] // 2, D_MODEL, F32)
    ff_swap_plan = _swap_plan([D_FF // N_CHIPS])
    ff_exchange_plan = _exchange_plan([D_FF // N_CHIPS // 2])
    exchange_landing = lambda p: _landing(3 * p.shape[0] // N_CHIPS, D_MODEL, BF16)
    g_ff1_local, dz1, dz1b, dcat, d_ln1_g, d_ln1_b = _ffn_bwd_ln1(dz2, r, x1b, xhat1, rstd1, ln1_g, w1_all, w2_all, w_out_all)
    ff1_bufs, swapping1, swap1_send, swap1_recv = _split_call("ff1_swap_start", [g_ff1_local, half_landing(g_ff1_local)], start=ff_swap_plan)
    g_out_local = _grad_w_out(cat, dz1b, dep=swapping1)
    ff1_bufs, _ = _split_call("ff1_swap_wait", ff1_bufs, wait=(ff_swap_plan, swap1_send, swap1_recv), after=g_out_local)
    ff1_sum, ff1_own = _pair_sum("grad_pair_sum_w_ff1", ff1_bufs[0], ff1_bufs[1], pos)
    ff1_ex, exchanging1, ex1_send, ex1_recv = _split_call(
        "ff1_exchange_start", [ff1_sum, exchange_landing(ff1_sum)], start=ff_exchange_plan)
    dh_main, dkv, g_ff2_local, d_v_ln_g, d_v_ln_b, d_w_spatial, d_b_spatial_t, d_sinks = _mixer_bwd(
        u, vg, q, k, va, dcat, cos, sin, v_ln_g, v_ln_b, w_spatial[0], bias_full, sink_vec, r, dz2b, dep=exchanging1)
    ff2_bufs, swapping2, swap2_send, swap2_recv = _split_call("ff2_swap_start", [g_ff2_local, half_landing(g_ff2_local)], start=ff_swap_plan)
    g_in_local, small_g = _grad_w_in_t_and_small_all_reduce(dh_main, dkv, x2, _pack_small(
        [d_v_ln_g, d_v_ln_b, d_w_spatial, d_b_spatial_t[:, :N_HEADS].T, d_sinks[0, :N_HEADS], d_ln1_g, d_ln1_b, d_ln2_g, d_ln2_b, sq_err]),
        dep=swapping2)
    sq_row = sum(rows for _, _, rows in _SMALL[:N_SMALL_PARAMS])
    loss = 0.5 * jnp.sum(small_g[sq_row : sq_row + _SMALL[N_SMALL_PARAMS][2]]) / D_MODEL
    ff2_bufs, _ = _split_call("ff2_swap_wait", ff2_bufs, wait=(ff_swap_plan, swap2_send, swap2_recv), after=g_in_local)
    ff2_sum, ff2_own = _pair_sum("grad_pair_sum_w_ff2", ff2_bufs[0], ff2_bufs[1], pos)
    ff2_ex, exchanging2, ex2_send, ex2_recv = _split_call(
        "ff2_exchange_start", [ff2_sum, exchange_landing(ff2_sum)], start=ff_exchange_plan)

    small = [g_in_local, g_out_local]
    small_swap_plan = _swap_plan([g.shape[0] // N_CHIPS for g in small])
    swap_bufs, small_swapping, ss_send, ss_recv = _split_call(
        "small_swap_start", small + [half_landing(g) for g in small], start=small_swap_plan, after=exchanging2)
    grad_x_flat = _grad_x(dh_main, dkv, dz1, w_in_t, dep=small_swapping)
    grad_x = grad_x_flat.reshape(1, t, D_MODEL)
    swap_bufs, _ = _split_call("small_swap_wait", swap_bufs, wait=(small_swap_plan, ss_send, ss_recv), after=grad_x_flat)
    pair_sums = [_pair_sum("grad_pair_sum_" + nm, g, th, pos) for nm, g, th in zip(["w_in", "w_out"], swap_bufs[:2], swap_bufs[2:])]
    small_plan = _exchange_plan([p.shape[0] // N_CHIPS for p, _ in pair_sums])
    small_bufs, small_exchanging, sm_send, sm_recv = _split_call(
        "small_exchange_start", [p for p, _ in pair_sums] + [exchange_landing(p) for p, _ in pair_sums], start=small_plan)

    ff1_ex, _ = _split_call("ff1_exchange_wait", ff1_ex, wait=(ff_exchange_plan, ex1_send, ex1_recv), after=small_exchanging)
    ff_pair_plan = _sibling_plan([D_FF // N_CHIPS])
    half_ff1 = _chip_sum("grad_chip_sum_w_ff1", ff1_own, ff1_ex[1], pos)
    (half_ff1, *ff2_ex), _, g1_send, g1_recv = _split_call(
        "ff2_exchange_wait_ff1_pair_start", [half_ff1] + ff2_ex, wait=(_shifted(ff_exchange_plan, 1), ex2_send, ex2_recv), start=ff_pair_plan)
    half_ff2 = _chip_sum("grad_chip_sum_w_ff2", ff2_own, ff2_ex[1], pos)
    (half_ff2, g_w_ff1), _, g2_send, g2_recv = _split_call(
        "ff1_pair_wait_ff2_pair_start", [half_ff2, half_ff1], wait=(_shifted(ff_pair_plan, 1), g1_send, g1_recv), start=ff_pair_plan)
    g_w_ff1, d_w_ff1, nm_w_ff1, nv_w_ff1 = _adamw("adamw_w_ff1", w_ff1[0], g_w_ff1, m_w_ff1[0], v_w_ff1[0])
    small_bufs, _ = _split_call("small_exchange_wait", small_bufs, wait=(small_plan, sm_send, sm_recv), after=nv_w_ff1)
    shards = [_chip_sum("grad_chip_sum_" + nm, own, ld, pos) for nm, (_, own), ld in zip(["w_in", "w_out"], pair_sums, small_bufs[2:])]
    small_pair_plan = _sibling_plan([s.shape[0] for s in shards])
    (*shards, g_w_ff2), _, g3_send, g3_recv = _split_call(
        "ff2_pair_wait_small_pair_start", shards + [half_ff2], wait=(_shifted(ff_pair_plan, 2), g2_send, g2_recv), start=small_pair_plan)
    g_w_ff2, d_w_ff2, nm_w_ff2, nv_w_ff2 = _adamw("adamw_w_ff2", w_ff2[0], g_w_ff2, m_w_ff2[0], v_w_ff2[0])
    (g_w_in_t, g_w_out), _ = _split_call("small_pair_wait", shards, wait=(small_pair_plan, g3_send, g3_recv), after=nv_w_ff2)
    g_w_in, d_w_in, nm_w_in, nv_w_in = (a.T for a in _adamw("adamw_w_in", w_in[0].T, g_w_in_t, m_w_in[0].T, v_w_in[0].T))
    g_w_out, d_w_out, nm_w_out, nv_w_out = _adamw("adamw_w_out", w_out[0], g_w_out, m_w_out[0], v_w_out[0])
    small_grads, small_d, small_nm, small_nv = _adamw_small(
        small_g,
        [v_ln_g, v_ln_b, w_spatial, b_spatial, sinks, ln1_g, ln1_b, ln2_g, ln2_b],
        [m_v_ln_g, m_v_ln_b, m_w_spatial, m_b_spatial, m_sinks, m_ln1_g, m_ln1_b, m_ln2_g, m_ln2_b],
        [v_v_ln_g, v_v_ln_b, v_w_spatial, v_b_spatial, v_sinks, v_ln1_g, v_ln1_b, v_ln2_g, v_ln2_b])

    def with_big(small, w_in_v, w_out_v, w_ff1_v, w_ff2_v):
        g_vg, g_vb, g_ws, g_bs, g_sk, g_1g, g_1b, g_2g, g_2b = small
        return [w_in_v[None], g_vg, g_vb, g_ws, g_bs, g_sk, w_out_v[None], g_1g, g_1b, w_ff1_v[None], w_ff2_v[None], g_2g, g_2b]

    return (
        loss,
        grad_x,
        *with_big(small_grads, g_w_in, g_w_out, g_w_ff1, g_w_ff2),
        *with_big(small_d, d_w_in, d_w_out, d_w_ff1, d_w_ff2),
        *with_big(small_nm, nm_w_in, nm_w_out, nm_w_ff1, nm_w_ff2),
        *with_big(small_nv, nv_w_in, nv_w_out, nv_w_ff1, nv_w_ff2),
    )
```
